```python
import math
import jax
import jax.numpy as jnp
from jax import lax
import numpy as np

D_MODEL = 1024
BATCH = 16
SEQ = 2048
DEPTH = 1

CTX_LEN = 256
GRID_W = 64
D_MIX = D_MODEL
D_S5 = D_MIX // 2
S5_GROUP = 16
S5_GROUPS = D_S5 // S5_GROUP
S5_STATE = 64
D_GDN = D_MIX - D_S5
GDN_HEAD = 128
GDN_HEADS = D_GDN // GDN_HEAD
CHUNK = 64
CONV_K = 3
N_DIR = 2
P_IN = 2 * D_S5 + 4 * D_GDN + 2 * N_DIR * GDN_HEADS
DEEPNORM_ALPHA = (2.0 * DEPTH) ** 0.25
DEEPNORM_BETA = (8.0 * DEPTH) ** -0.25
LN_EPS = 1e-5
NORM_EPS = 1e-6

kernel_name = "hybrid_s5_gdn_prefix_dit_block"


def _proj_splits():
    widths = (D_S5, D_S5, 3 * D_GDN, D_GDN, N_DIR * GDN_HEADS, N_DIR * GDN_HEADS)
    return tuple(int(v) for v in np.cumsum(widths)[:-1])


def _layer_norm(x, g, b):
    xf = x.astype(jnp.float32)
    mu = jnp.mean(xf, -1, keepdims=True)
    var = jnp.mean(jnp.square(xf - mu), -1, keepdims=True)
    return ((xf - mu) * lax.rsqrt(var + LN_EPS) * g.astype(jnp.float32) + b.astype(jnp.float32)).astype(x.dtype)


def _ada(cond, w_ada, b_ada):
    m = jax.nn.silu(cond) @ w_ada + b_ada
    return jnp.split(m, 3, axis=-1)


def _maybe_flip(t, rev):
    return jnp.flip(t, 1) if rev else t


def _s5_zoh(lam_re, lam_im, log_dt, b_re, b_im):
    lam = lax.complex(lam_re.astype(jnp.float32), lam_im.astype(jnp.float32))
    dt = jnp.exp(log_dt.astype(jnp.float32))[:, None]
    abar = jnp.exp(lam * dt)
    bmat = lax.complex(b_re.astype(jnp.float32), b_im.astype(jnp.float32))
    bbar = ((abar - 1.0) / lam)[..., None] * bmat
    return abar, bbar


def _s5_combine(e1, e2):
    a1, b1 = e1
    a2, b2 = e2
    return a1 * a2, a2 * b1 + b2


def _s5_states(u, abar, bbar, h0, rev):
    bu = jnp.einsum('blgc,gpc->blgp', u, bbar)
    bu = _maybe_flip(bu, rev)
    bu = bu.at[:, 0].add(abar * h0)
    a = jnp.broadcast_to(abar, (1,) + bu.shape[1:])
    _, h = lax.associative_scan(_s5_combine, (a, bu), axis=1)
    return _maybe_flip(h, rev), h[:, -1]


def _s5_readout(h, c_re, c_im):
    cmat = lax.complex(c_re.astype(jnp.float32), c_im.astype(jnp.float32))
    y = jnp.real(jnp.einsum('blgp,gcp->blgc', h, cmat))
    return y.reshape(y.shape[0], y.shape[1], D_S5)


def _s5_glu(y, w_glu, b_glu):
    g = jax.nn.gelu(y, approximate=False)
    return g * jax.nn.sigmoid(g @ w_glu.astype(jnp.float32) + b_glu.astype(jnp.float32))


def _s5_mixer(u, uc, z, zc, p, with_ctx):
    B_ = u.shape[0]
    uf = u.astype(jnp.float32)
    ucf = uc.astype(jnp.float32)
    ul = uf.reshape(B_, -1, S5_GROUPS, S5_GROUP)
    ucg = ucf.reshape(B_, -1, S5_GROUPS, S5_GROUP)
    d_skip = p['s5_d'].astype(jnp.float32)
    y_lat = d_skip * uf
    y_ctx = d_skip * ucf if with_ctx else None
    for d in range(N_DIR):
        abar, bbar = _s5_zoh(p['s5_lambda_re'][d], p['s5_lambda_im'][d], p['s5_log_dt'][d],
                             p['s5_b_re'][d], p['s5_b_im'][d])
        h0 = jnp.zeros((B_, S5_GROUPS, S5_STATE), jnp.complex64)
        hs_ctx, h_ctx_end = _s5_states(ucg, abar, bbar, h0, d == 1)
        hs_lat, _ = _s5_states(ul, abar, bbar, h_ctx_end, d == 1)
        y_lat = y_lat + _s5_readout(hs_lat, p['s5_c_re'][d], p['s5_c_im'][d])
        if with_ctx:
            y_ctx = y_ctx + _s5_readout(hs_ctx, p['s5_c_re'][d], p['s5_c_im'][d])
    out_lat = _s5_glu(y_lat, p['w_glu'], p['b_glu']) * jax.nn.silu(z.astype(jnp.float32))
    out_ctx = None
    if with_ctx:
        out_ctx = _s5_glu(y_ctx, p['w_glu'], p['b_glu']) * jax.nn.silu(zc.astype(jnp.float32))
    return out_lat, out_ctx


def _conv_latent(t, w):
    B_, L, C = t.shape
    rows = L // GRID_W
    img = t.reshape(B_, rows, GRID_W, C)
    out = lax.conv_general_dilated(img, w[:, :, None, :].astype(t.dtype), (1, 1), 'SAME',
                                   dimension_numbers=('NHWC', 'HWIO', 'NHWC'), feature_group_count=C)
    return out.reshape(B_, L, C)


def _conv_context(t, w):
    C = t.shape[-1]
    return lax.conv_general_dilated(t, w[CONV_K // 2][:, None, :].astype(t.dtype), (1,), 'SAME',
                                    dimension_numbers=('NWC', 'WIO', 'NWC'), feature_group_count=C)


def _l2norm(t):
    return t * lax.rsqrt(jnp.sum(jnp.square(t), -1, keepdims=True) + NORM_EPS)


def _gdn_qkv_heads(qkv):
    B_, L, _ = qkv.shape
    q, k, v = jnp.split(qkv.astype(jnp.float32), 3, axis=-1)
    q = _l2norm(q.reshape(B_, L, GDN_HEADS, GDN_HEAD)) * (GDN_HEAD ** -0.5)
    k = _l2norm(k.reshape(B_, L, GDN_HEADS, GDN_HEAD))
    v = v.reshape(B_, L, GDN_HEADS, GDN_HEAD)
    return q, k, v


def _gdn_gates(beta_logit, alpha_logit, a_log, dt_bias):
    B_, L, _ = beta_logit.shape
    beta = jax.nn.sigmoid(beta_logit.astype(jnp.float32)).reshape(B_, L, N_DIR, GDN_HEADS)
    a = alpha_logit.astype(jnp.float32).reshape(B_, L, N_DIR, GDN_HEADS)
    g = -jnp.exp(a_log.astype(jnp.float32)) * jax.nn.softplus(a + dt_bias.astype(jnp.float32))
    return beta, g


def _gdn_chunked(q, k, v, beta, g, s0):
    B_, L, H, _ = q.shape
    Dv = v.shape[-1]
    N = L // CHUNK

    def chunk(t):
        return t.reshape(B_, N, CHUNK, H, -1).transpose(1, 0, 3, 2, 4)

    qc, kc, vc = chunk(q), chunk(k), chunk(v)
    bc = chunk(beta[..., None])[..., 0]
    gcum = jnp.cumsum(chunk(g[..., None])[..., 0], axis=-1)
    idx = jnp.arange(CHUNK)
    lower = idx[:, None] >= idx[None, :]
    strict = idx[:, None] > idx[None, :]
    decay = jnp.exp(jnp.where(lower, gcum[..., :, None] - gcum[..., None, :], -jnp.inf))
    kk = jnp.einsum('nbhik,nbhjk->nbhij', kc, kc)
    a_mat = jnp.where(strict, bc[..., :, None] * kk * decay, 0.0)
    gamma = jnp.exp(gcum)
    rhs = jnp.concatenate([bc[..., None] * vc, (bc * gamma)[..., None] * kc], axis=-1)
    eye = jnp.eye(CHUNK, dtype=jnp.float32)
    sol = lax.linalg.triangular_solve(a_mat + eye, rhs, left_side=True, lower=True, unit_diagonal=True)
    u0, w = sol[..., :Dv], sol[..., Dv:]
    qk = jnp.einsum('nbhik,nbhjk->nbhij', qc, kc) * decay
    k_out = kc * jnp.exp(gcum[..., -1:] - gcum)[..., None]
    gamma_last = gamma[..., -1]

    def step(s, inp):
        qi, ki, ui0, wi, qki, gi, gli = inp
        u = ui0 - jnp.einsum('bhck,bhkv->bhcv', wi, s)
        o = gi[..., None] * jnp.einsum('bhck,bhkv->bhcv', qi, s) + jnp.einsum('bhij,bhjv->bhiv', qki, u)
        s = gli[..., None, None] * s + jnp.einsum('bhck,bhcv->bhkv', ki, u)
        return s, o

    s_fin, o = lax.scan(step, s0, (qc, k_out, u0, w, qk, gamma, gamma_last))
    o = o.transpose(1, 0, 3, 2, 4).reshape(B_, L, H, Dv)
    return o, s_fin


def _gated_rmsnorm(o, z, w):
    o = o * lax.rsqrt(jnp.mean(jnp.square(o), -1, keepdims=True) + NORM_EPS) * w.astype(jnp.float32)
    return o.reshape(z.shape) * jax.nn.silu(z.astype(jnp.float32))


def _gdn_mixer(qkv, qkvc, z, zc, beta_l, beta_c, alpha_l, alpha_c, p, with_ctx):
    qkv = jax.nn.silu(_conv_latent(qkv, p['conv_w']))
    qkvc = jax.nn.silu(_conv_context(qkvc, p['conv_w']))
    ql, kl, vl = _gdn_qkv_heads(qkv)
    qc, kc, vc = _gdn_qkv_heads(qkvc)
    bl, gl = _gdn_gates(beta_l, alpha_l, p['gdn_a_log'], p['gdn_dt_bias'])
    bcx, gcx = _gdn_gates(beta_c, alpha_c, p['gdn_a_log'], p['gdn_dt_bias'])
    B_ = ql.shape[0]
    o_lat = jnp.zeros(vl.shape, jnp.float32)
    o_ctx = jnp.zeros(vc.shape, jnp.float32)
    for d in range(N_DIR):
        rev = d == 1
        f = lambda t: _maybe_flip(t, rev)
        s0 = jnp.zeros((B_, GDN_HEADS, GDN_HEAD, GDN_HEAD), jnp.float32)
        oc, s_ctx = _gdn_chunked(f(qc), f(kc), f(vc), f(bcx[:, :, d]), f(gcx[:, :, d]), s0)
        ol, _ = _gdn_chunked(f(ql), f(kl), f(vl), f(bl[:, :, d]), f(gl[:, :, d]), s_ctx)
        o_lat = o_lat + f(ol)
        o_ctx = o_ctx + f(oc)
    out_lat = _gated_rmsnorm(o_lat, z, p['gdn_norm_w'])
    out_ctx = _gated_rmsnorm(o_ctx, zc, p['gdn_norm_w']) if with_ctx else None
    return out_lat, out_ctx


def _layer(x, ctx, c, c_ctx, p, with_ctx):
    shift, scale, gate = _ada(c, p['w_ada'], p['b_ada'])
    shift_c, scale_c, gate_c = _ada(c_ctx, p['w_ada'], p['b_ada'])
    h = x * (1.0 + scale[:, None]) + shift[:, None]
    hc = ctx * (1.0 + scale_c) + shift_c
    u, z_s5, qkv, z_gdn, beta_l, alpha_l = jnp.split(h @ p['w_in'], _proj_splits(), axis=-1)
    uc, z_s5c, qkvc, z_gdnc, beta_c, alpha_c = jnp.split(hc @ p['w_in'], _proj_splits(), axis=-1)
    s5_lat, s5_ctx = _s5_mixer(u, uc, z_s5, z_s5c, p, with_ctx)
    gdn_lat, gdn_ctx = _gdn_mixer(qkv, qkvc, z_gdn, z_gdnc, beta_l, beta_c, alpha_l, alpha_c, p, with_ctx)
    w_out = p['w_out'].astype(jnp.float32)
    y = jnp.concatenate([s5_lat, gdn_lat], axis=-1) @ w_out
    x_new = _layer_norm(DEEPNORM_ALPHA * x + gate[:, None] * y.astype(x.dtype), p['ln_g'], p['ln_b'])
    if not with_ctx:
        return x_new, ctx
    yc = jnp.concatenate([s5_ctx, gdn_ctx], axis=-1) @ w_out
    ctx_new = _layer_norm(DEEPNORM_ALPHA * ctx + gate_c * yc.astype(ctx.dtype), p['ln_g'], p['ln_b'])
    return x_new, ctx_new


def _fwd_setup_inputs(seed: int = 0) -> dict:
    key = jax.random.key(seed)
    ks = jax.random.split(key, 24)
    f32 = jnp.float32

    def nrm(k, shape, s):
        return jax.random.normal(k, shape, f32) * s

    lam_shape = (DEPTH, N_DIR, S5_GROUPS, S5_STATE)
    n = jnp.arange(S5_STATE, dtype=f32)
    dt = jnp.exp(jax.random.uniform(ks[19], (DEPTH, N_DIR, GDN_HEADS), f32, math.log(1e-3), math.log(1e-1)))
    return {
        "x": nrm(ks[0], (BATCH, SEQ, D_MODEL), 1.0),
        "c": nrm(ks[1], (BATCH, D_MODEL), 1.0),
        "ctx": nrm(ks[2], (BATCH, CTX_LEN, D_MODEL), 1.0),
        "c_ctx": nrm(ks[3], (D_MODEL,), 1.0),
        "w_ada": nrm(ks[4], (DEPTH, D_MODEL, 3 * D_MODEL), 0.5 * D_MODEL ** -0.5),
        "b_ada": nrm(ks[5], (DEPTH, 3 * D_MODEL), 0.01),
        "w_in": nrm(ks[6], (DEPTH, D_MODEL, P_IN), D_MODEL ** -0.5),
        "s5_lambda_re": -0.5 + nrm(ks[7], lam_shape, 0.01),
        "s5_lambda_im": math.pi * n + nrm(ks[8], lam_shape, 0.01),
        "s5_log_dt": jax.random.uniform(ks[9], (DEPTH, N_DIR, S5_GROUPS), f32, math.log(1e-3), math.log(1e-1)),
        "s5_b_re": nrm(ks[10], (DEPTH, N_DIR, S5_GROUPS, S5_STATE, S5_GROUP), (2 * S5_GROUP) ** -0.5),
        "s5_b_im": nrm(ks[11], (DEPTH, N_DIR, S5_GROUPS, S5_STATE, S5_GROUP), (2 * S5_GROUP) ** -0.5),
        "s5_c_re": nrm(ks[12], (DEPTH, N_DIR, S5_GROUPS, S5_GROUP, S5_STATE), (2 * S5_STATE) ** -0.5),
        "s5_c_im": nrm(ks[13], (DEPTH, N_DIR, S5_GROUPS, S5_GROUP, S5_STATE), (2 * S5_STATE) ** -0.5),
        "s5_d": nrm(ks[14], (DEPTH, D_S5), 1.0),
        "w_glu": nrm(ks[15], (DEPTH, D_S5, D_S5), D_S5 ** -0.5),
        "b_glu": nrm(ks[16], (DEPTH, D_S5), 0.01),
        "conv_w": nrm(ks[17], (DEPTH, CONV_K, CONV_K, 3 * D_GDN), 1.0 / CONV_K),
        "gdn_a_log": jnp.log(jax.random.uniform(ks[18], (DEPTH, N_DIR, GDN_HEADS), f32, 1.0, 16.0)),
        "gdn_dt_bias": dt + jnp.log(-jnp.expm1(-dt)),
        "gdn_norm_w": 1.0 + nrm(ks[20], (DEPTH, GDN_HEAD), 0.01),
        "w_out": nrm(ks[21], (DEPTH, D_MIX, D_MODEL), DEEPNORM_BETA * D_MIX ** -0.5),
        "ln_g": 1.0 + nrm(ks[22], (DEPTH, D_MODEL), 0.01),
        "ln_b": nrm(ks[23], (DEPTH, D_MODEL), 0.01),
    }


def _fwd_reference(x, c, ctx, c_ctx, w_ada, b_ada, w_in, s5_lambda_re, s5_lambda_im, s5_log_dt,
              s5_b_re, s5_b_im, s5_c_re, s5_c_im, s5_d, w_glu, b_glu, conv_w, gdn_a_log,
              gdn_dt_bias, gdn_norm_w, w_out, ln_g, ln_b):
    for l in range(DEPTH):
        p = {
            'w_ada': w_ada[l], 'b_ada': b_ada[l], 'w_in': w_in[l],
            's5_lambda_re': s5_lambda_re[l], 's5_lambda_im': s5_lambda_im[l], 's5_log_dt': s5_log_dt[l],
            's5_b_re': s5_b_re[l], 's5_b_im': s5_b_im[l], 's5_c_re': s5_c_re[l], 's5_c_im': s5_c_im[l],
            's5_d': s5_d[l], 'w_glu': w_glu[l], 'b_glu': b_glu[l], 'conv_w': conv_w[l],
            'gdn_a_log': gdn_a_log[l], 'gdn_dt_bias': gdn_dt_bias[l], 'gdn_norm_w': gdn_norm_w[l],
            'w_out': w_out[l], 'ln_g': ln_g[l], 'ln_b': ln_b[l],
        }
        x, ctx = _layer(x, ctx, c, c_ctx, p, l < DEPTH - 1)
    return x


import jax as _jax
import jax.numpy as _jnp

TWIN_FORMAT = 'train_step'
FWD_PARAMS = ['x', 'c', 'ctx', 'c_ctx', 'w_ada', 'b_ada', 'w_in', 's5_lambda_re', 's5_lambda_im', 's5_log_dt', 's5_b_re', 's5_b_im', 's5_c_re', 's5_c_im', 's5_d', 'w_glu', 'b_glu', 'conv_w', 'gdn_a_log', 'gdn_dt_bias', 'gdn_norm_w', 'w_out', 'ln_g', 'ln_b']
TWIN_WEIGHTS = ['c_ctx', 'w_ada', 'b_ada', 'w_in', 's5_lambda_re', 's5_lambda_im', 's5_log_dt', 's5_b_re', 's5_b_im', 's5_c_re', 's5_c_im', 's5_d', 'w_glu', 'b_glu', 'conv_w', 'gdn_a_log', 'gdn_dt_bias', 'gdn_norm_w', 'w_out', 'ln_g', 'ln_b']
TWIN_DIFF_INPUT = 'x'
TWIN_INPUTS = ['x', 'c', 'ctx', 'c_ctx', 'w_ada', 'b_ada', 'w_in', 's5_lambda_re', 's5_lambda_im', 's5_log_dt', 's5_b_re', 's5_b_im', 's5_c_re', 's5_c_im', 's5_d', 'w_glu', 'b_glu', 'conv_w', 'gdn_a_log', 'gdn_dt_bias', 'gdn_norm_w', 'w_out', 'ln_g', 'ln_b', 'loss_target', 'm_c_ctx', 'm_w_ada', 'm_b_ada', 'm_w_in', 'm_s5_lambda_re', 'm_s5_lambda_im', 'm_s5_log_dt', 'm_s5_b_re', 'm_s5_b_im', 'm_s5_c_re', 'm_s5_c_im', 'm_s5_d', 'm_w_glu', 'm_b_glu', 'm_conv_w', 'm_gdn_a_log', 'm_gdn_dt_bias', 'm_gdn_norm_w', 'm_w_out', 'm_ln_g', 'm_ln_b', 'v_c_ctx', 'v_w_ada', 'v_b_ada', 'v_w_in', 'v_s5_lambda_re', 'v_s5_lambda_im', 'v_s5_log_dt', 'v_s5_b_re', 'v_s5_b_im', 'v_s5_c_re', 'v_s5_c_im', 'v_s5_d', 'v_w_glu', 'v_b_glu', 'v_conv_w', 'v_gdn_a_log', 'v_gdn_dt_bias', 'v_gdn_norm_w', 'v_w_out', 'v_ln_g', 'v_ln_b']
TWIN_OUTPUTS = ['loss', 'grad_x', 'grad_c_ctx', 'grad_w_ada', 'grad_b_ada', 'grad_w_in', 'grad_s5_lambda_re', 'grad_s5_lambda_im', 'grad_s5_log_dt', 'grad_s5_b_re', 'grad_s5_b_im', 'grad_s5_c_re', 'grad_s5_c_im', 'grad_s5_d', 'grad_w_glu', 'grad_b_glu', 'grad_conv_w', 'grad_gdn_a_log', 'grad_gdn_dt_bias', 'grad_gdn_norm_w', 'grad_w_out', 'grad_ln_g', 'grad_ln_b', 'delta_c_ctx', 'delta_w_ada', 'delta_b_ada', 'delta_w_in', 'delta_s5_lambda_re', 'delta_s5_lambda_im', 'delta_s5_log_dt', 'delta_s5_b_re', 'delta_s5_b_im', 'delta_s5_c_re', 'delta_s5_c_im', 'delta_s5_d', 'delta_w_glu', 'delta_b_glu', 'delta_conv_w', 'delta_gdn_a_log', 'delta_gdn_dt_bias', 'delta_gdn_norm_w', 'delta_w_out', 'delta_ln_g', 'delta_ln_b', 'new_m_c_ctx', 'new_m_w_ada', 'new_m_b_ada', 'new_m_w_in', 'new_m_s5_lambda_re', 'new_m_s5_lambda_im', 'new_m_s5_log_dt', 'new_m_s5_b_re', 'new_m_s5_b_im', 'new_m_s5_c_re', 'new_m_s5_c_im', 'new_m_s5_d', 'new_m_w_glu', 'new_m_b_glu', 'new_m_conv_w', 'new_m_gdn_a_log', 'new_m_gdn_dt_bias', 'new_m_gdn_norm_w', 'new_m_w_out', 'new_m_ln_g', 'new_m_ln_b', 'new_v_c_ctx', 'new_v_w_ada', 'new_v_b_ada', 'new_v_w_in', 'new_v_s5_lambda_re', 'new_v_s5_lambda_im', 'new_v_s5_log_dt', 'new_v_s5_b_re', 'new_v_s5_b_im', 'new_v_s5_c_re', 'new_v_s5_c_im', 'new_v_s5_d', 'new_v_w_glu', 'new_v_b_glu', 'new_v_conv_w', 'new_v_gdn_a_log', 'new_v_gdn_dt_bias', 'new_v_gdn_norm_w', 'new_v_w_out', 'new_v_ln_g', 'new_v_ln_b']
TWIN_LEAF_KINDS = {'loss': 'loss', 'grad_x': 'grad_x', 'grad_c_ctx': 'grad_w', 'grad_w_ada': 'grad_w', 'grad_b_ada': 'grad_w', 'grad_w_in': 'grad_w', 'grad_s5_lambda_re': 'grad_w', 'grad_s5_lambda_im': 'grad_w', 'grad_s5_log_dt': 'grad_w', 'grad_s5_b_re': 'grad_w', 'grad_s5_b_im': 'grad_w', 'grad_s5_c_re': 'grad_w', 'grad_s5_c_im': 'grad_w', 'grad_s5_d': 'grad_w', 'grad_w_glu': 'grad_w', 'grad_b_glu': 'grad_w', 'grad_conv_w': 'grad_w', 'grad_gdn_a_log': 'grad_w', 'grad_gdn_dt_bias': 'grad_w', 'grad_gdn_norm_w': 'grad_w', 'grad_w_out': 'grad_w', 'grad_ln_g': 'grad_w', 'grad_ln_b': 'grad_w', 'delta_c_ctx': 'delta_w', 'delta_w_ada': 'delta_w', 'delta_b_ada': 'delta_w', 'delta_w_in': 'delta_w', 'delta_s5_lambda_re': 'delta_w', 'delta_s5_lambda_im': 'delta_w', 'delta_s5_log_dt': 'delta_w', 'delta_s5_b_re': 'delta_w', 'delta_s5_b_im': 'delta_w', 'delta_s5_c_re': 'delta_w', 'delta_s5_c_im': 'delta_w', 'delta_s5_d': 'delta_w', 'delta_w_glu': 'delta_w', 'delta_b_glu': 'delta_w', 'delta_conv_w': 'delta_w', 'delta_gdn_a_log': 'delta_w', 'delta_gdn_dt_bias': 'delta_w', 'delta_gdn_norm_w': 'delta_w', 'delta_w_out': 'delta_w', 'delta_ln_g': 'delta_w', 'delta_ln_b': 'delta_w', 'new_m_c_ctx': 'new_m', 'new_m_w_ada': 'new_m', 'new_m_b_ada': 'new_m', 'new_m_w_in': 'new_m', 'new_m_s5_lambda_re': 'new_m', 'new_m_s5_lambda_im': 'new_m', 'new_m_s5_log_dt': 'new_m', 'new_m_s5_b_re': 'new_m', 'new_m_s5_b_im': 'new_m', 'new_m_s5_c_re': 'new_m', 'new_m_s5_c_im': 'new_m', 'new_m_s5_d': 'new_m', 'new_m_w_glu': 'new_m', 'new_m_b_glu': 'new_m', 'new_m_conv_w': 'new_m', 'new_m_gdn_a_log': 'new_m', 'new_m_gdn_dt_bias': 'new_m', 'new_m_gdn_norm_w': 'new_m', 'new_m_w_out': 'new_m', 'new_m_ln_g': 'new_m', 'new_m_ln_b': 'new_m', 'new_v_c_ctx': 'new_v', 'new_v_w_ada': 'new_v', 'new_v_b_ada': 'new_v', 'new_v_w_in': 'new_v', 'new_v_s5_lambda_re': 'new_v', 'new_v_s5_lambda_im': 'new_v', 'new_v_s5_log_dt': 'new_v', 'new_v_s5_b_re': 'new_v', 'new_v_s5_b_im': 'new_v', 'new_v_s5_c_re': 'new_v', 'new_v_s5_c_im': 'new_v', 'new_v_s5_d': 'new_v', 'new_v_w_glu': 'new_v', 'new_v_b_glu': 'new_v', 'new_v_conv_w': 'new_v', 'new_v_gdn_a_log': 'new_v', 'new_v_gdn_dt_bias': 'new_v', 'new_v_gdn_norm_w': 'new_v', 'new_v_w_out': 'new_v', 'new_v_ln_g': 'new_v', 'new_v_ln_b': 'new_v'}


def _forward(args):
    return _fwd_reference(*[args[k] for k in FWD_PARAMS])


def _output_shape():
    out = _jax.eval_shape(lambda: _forward(_fwd_setup_inputs(0)))
    return out.shape, out.dtype

N_MICROBATCH = 1
ADAM_LR = 0.001
ADAM_B1 = 0.9
ADAM_B2 = 0.999
ADAM_EPS = 1e-08
ADAM_WD = 0.01
ADAM_STEP = 10
PER_EXAMPLE_BATCH_AXIS = {'x': 0, 'c': 0, 'ctx': 0, 'loss_target': 0}
SHARED_INPUTS = []
_WEIGHT_DTYPES = {'c_ctx': _jnp.float32, 'w_ada': _jnp.float32, 'b_ada': _jnp.float32, 'w_in': _jnp.float32, 's5_lambda_re': _jnp.float32, 's5_lambda_im': _jnp.float32, 's5_log_dt': _jnp.float32, 's5_b_re': _jnp.float32, 's5_b_im': _jnp.float32, 's5_c_re': _jnp.float32, 's5_c_im': _jnp.float32, 's5_d': _jnp.float32, 'w_glu': _jnp.float32, 'b_glu': _jnp.float32, 'conv_w': _jnp.float32, 'gdn_a_log': _jnp.float32, 'gdn_dt_bias': _jnp.float32, 'gdn_norm_w': _jnp.float32, 'w_out': _jnp.float32, 'ln_g': _jnp.float32, 'ln_b': _jnp.float32}
MOMENT_SCALE = {'c_ctx': 6.591426e-04, 'w_ada': 1.938916e-02, 'b_ada': 3.197911e-02, 'w_in': 1.253575e-02, 's5_lambda_re': 3.556831e-04, 's5_lambda_im': 5.363002e-04, 's5_log_dt': 2.988820e-01, 's5_b_re': 2.888737e-04, 's5_b_im': 2.858450e-04, 's5_c_re': 5.632162e-04, 's5_c_im': 5.737300e-04, 's5_d': 8.280720e-03, 'w_glu': 2.155594e-03, 'b_glu': 3.025738e-03, 'conv_w': 1.319026e-02, 'gdn_a_log': 6.314587e-02, 'gdn_dt_bias': 6.162860e-02, 'gdn_norm_w': 3.947093e-02, 'w_out': 2.336488e-02, 'ln_g': 3.196786e+01, 'ln_b': 3.962198e-01}


def _to_microbatches(a, axis):
    t = _jnp.moveaxis(a, axis, 0)
    t = t.reshape((N_MICROBATCH, t.shape[0] // N_MICROBATCH) + t.shape[1:])
    return _jnp.moveaxis(t, 1, axis + 1)


def setup_inputs(seed: int = 0) -> dict:
    inp = _fwd_setup_inputs(seed)
    key = _jax.random.fold_in(_jax.random.key(seed), 7919)
    shape, _ = _output_shape()
    out = dict(inp)
    out["loss_target"] = _jax.random.normal(_jax.random.fold_in(key, 0), shape, _jnp.float32)
    for i, name in enumerate(TWIN_WEIGHTS):
        w = inp[name].astype(_jnp.float32)
        if MOMENT_SCALE is None:
            s = _jnp.sqrt(_jnp.mean(_jnp.square(w)) + 1e-30)
        else:
            s = MOMENT_SCALE[name]
        km, kv = _jax.random.split(_jax.random.fold_in(key, i + 1))
        out[name] = w
        out["m_" + name] = s * _jax.random.normal(km, w.shape, _jnp.float32)
        out["v_" + name] = (s * s) * _jax.random.uniform(kv, w.shape, _jnp.float32, 0.5, 1.5)
    if N_MICROBATCH > 1:
        for name, axis in PER_EXAMPLE_BATCH_AXIS.items():
            out[name] = _to_microbatches(out[name], axis)
    return {'x': out['x'], 'c': out['c'], 'ctx': out['ctx'], 'c_ctx': out['c_ctx'], 'w_ada': out['w_ada'], 'b_ada': out['b_ada'], 'w_in': out['w_in'], 's5_lambda_re': out['s5_lambda_re'], 's5_lambda_im': out['s5_lambda_im'], 's5_log_dt': out['s5_log_dt'], 's5_b_re': out['s5_b_re'], 's5_b_im': out['s5_b_im'], 's5_c_re': out['s5_c_re'], 's5_c_im': out['s5_c_im'], 's5_d': out['s5_d'], 'w_glu': out['w_glu'], 'b_glu': out['b_glu'], 'conv_w': out['conv_w'], 'gdn_a_log': out['gdn_a_log'], 'gdn_dt_bias': out['gdn_dt_bias'], 'gdn_norm_w': out['gdn_norm_w'], 'w_out': out['w_out'], 'ln_g': out['ln_g'], 'ln_b': out['ln_b'], 'loss_target': out['loss_target'], 'm_c_ctx': out['m_c_ctx'], 'm_w_ada': out['m_w_ada'], 'm_b_ada': out['m_b_ada'], 'm_w_in': out['m_w_in'], 'm_s5_lambda_re': out['m_s5_lambda_re'], 'm_s5_lambda_im': out['m_s5_lambda_im'], 'm_s5_log_dt': out['m_s5_log_dt'], 'm_s5_b_re': out['m_s5_b_re'], 'm_s5_b_im': out['m_s5_b_im'], 'm_s5_c_re': out['m_s5_c_re'], 'm_s5_c_im': out['m_s5_c_im'], 'm_s5_d': out['m_s5_d'], 'm_w_glu': out['m_w_glu'], 'm_b_glu': out['m_b_glu'], 'm_conv_w': out['m_conv_w'], 'm_gdn_a_log': out['m_gdn_a_log'], 'm_gdn_dt_bias': out['m_gdn_dt_bias'], 'm_gdn_norm_w': out['m_gdn_norm_w'], 'm_w_out': out['m_w_out'], 'm_ln_g': out['m_ln_g'], 'm_ln_b': out['m_ln_b'], 'v_c_ctx': out['v_c_ctx'], 'v_w_ada': out['v_w_ada'], 'v_b_ada': out['v_b_ada'], 'v_w_in': out['v_w_in'], 'v_s5_lambda_re': out['v_s5_lambda_re'], 'v_s5_lambda_im': out['v_s5_lambda_im'], 'v_s5_log_dt': out['v_s5_log_dt'], 'v_s5_b_re': out['v_s5_b_re'], 'v_s5_b_im': out['v_s5_b_im'], 'v_s5_c_re': out['v_s5_c_re'], 'v_s5_c_im': out['v_s5_c_im'], 'v_s5_d': out['v_s5_d'], 'v_w_glu': out['v_w_glu'], 'v_b_glu': out['v_b_glu'], 'v_conv_w': out['v_conv_w'], 'v_gdn_a_log': out['v_gdn_a_log'], 'v_gdn_dt_bias': out['v_gdn_dt_bias'], 'v_gdn_norm_w': out['v_gdn_norm_w'], 'v_w_out': out['v_w_out'], 'v_ln_g': out['v_ln_g'], 'v_ln_b': out['v_ln_b']}


def _loss(weights, diff, rest, loss_target):
    with _jax.named_scope("forward"):
        args = {**rest, TWIN_DIFF_INPUT: diff, **{k: w.astype(_WEIGHT_DTYPES[k]) for k, w in weights.items()}}
        y = _forward(args)
    with _jax.named_scope("loss_head"):
        err = _jnp.square(y.astype(_jnp.float32) - loss_target)
        return 0.5 * _jnp.sum(_jnp.mean(err, axis=-1)) if err.ndim else 0.5 * err


def _adamw(w, g, m, v):
    m = ADAM_B1 * m + (1.0 - ADAM_B1) * g
    v = ADAM_B2 * v + (1.0 - ADAM_B2) * _jnp.square(g)
    m_hat = m / (1.0 - ADAM_B1 ** ADAM_STEP)
    v_hat = v / (1.0 - ADAM_B2 ** ADAM_STEP)
    delta = -ADAM_LR * (m_hat / (_jnp.sqrt(v_hat) + ADAM_EPS) + ADAM_WD * w)
    return delta, m, v


def reference(x, c, ctx, c_ctx, w_ada, b_ada, w_in, s5_lambda_re, s5_lambda_im, s5_log_dt, s5_b_re, s5_b_im, s5_c_re, s5_c_im, s5_d, w_glu, b_glu, conv_w, gdn_a_log, gdn_dt_bias, gdn_norm_w, w_out, ln_g, ln_b, loss_target, m_c_ctx, m_w_ada, m_b_ada, m_w_in, m_s5_lambda_re, m_s5_lambda_im, m_s5_log_dt, m_s5_b_re, m_s5_b_im, m_s5_c_re, m_s5_c_im, m_s5_d, m_w_glu, m_b_glu, m_conv_w, m_gdn_a_log, m_gdn_dt_bias, m_gdn_norm_w, m_w_out, m_ln_g, m_ln_b, v_c_ctx, v_w_ada, v_b_ada, v_w_in, v_s5_lambda_re, v_s5_lambda_im, v_s5_log_dt, v_s5_b_re, v_s5_b_im, v_s5_c_re, v_s5_c_im, v_s5_d, v_w_glu, v_b_glu, v_conv_w, v_gdn_a_log, v_gdn_dt_bias, v_gdn_norm_w, v_w_out, v_ln_g, v_ln_b):
    given = dict(x=x, c=c, ctx=ctx, c_ctx=c_ctx, w_ada=w_ada, b_ada=b_ada, w_in=w_in, s5_lambda_re=s5_lambda_re, s5_lambda_im=s5_lambda_im, s5_log_dt=s5_log_dt, s5_b_re=s5_b_re, s5_b_im=s5_b_im, s5_c_re=s5_c_re, s5_c_im=s5_c_im, s5_d=s5_d, w_glu=w_glu, b_glu=b_glu, conv_w=conv_w, gdn_a_log=gdn_a_log, gdn_dt_bias=gdn_dt_bias, gdn_norm_w=gdn_norm_w, w_out=w_out, ln_g=ln_g, ln_b=ln_b, loss_target=loss_target, m_c_ctx=m_c_ctx, m_w_ada=m_w_ada, m_b_ada=m_b_ada, m_w_in=m_w_in, m_s5_lambda_re=m_s5_lambda_re, m_s5_lambda_im=m_s5_lambda_im, m_s5_log_dt=m_s5_log_dt, m_s5_b_re=m_s5_b_re, m_s5_b_im=m_s5_b_im, m_s5_c_re=m_s5_c_re, m_s5_c_im=m_s5_c_im, m_s5_d=m_s5_d, m_w_glu=m_w_glu, m_b_glu=m_b_glu, m_conv_w=m_conv_w, m_gdn_a_log=m_gdn_a_log, m_gdn_dt_bias=m_gdn_dt_bias, m_gdn_norm_w=m_gdn_norm_w, m_w_out=m_w_out, m_ln_g=m_ln_g, m_ln_b=m_ln_b, v_c_ctx=v_c_ctx, v_w_ada=v_w_ada, v_b_ada=v_b_ada, v_w_in=v_w_in, v_s5_lambda_re=v_s5_lambda_re, v_s5_lambda_im=v_s5_lambda_im, v_s5_log_dt=v_s5_log_dt, v_s5_b_re=v_s5_b_re, v_s5_b_im=v_s5_b_im, v_s5_c_re=v_s5_c_re, v_s5_c_im=v_s5_c_im, v_s5_d=v_s5_d, v_w_glu=v_w_glu, v_b_glu=v_b_glu, v_conv_w=v_conv_w, v_gdn_a_log=v_gdn_a_log, v_gdn_dt_bias=v_gdn_dt_bias, v_gdn_norm_w=v_gdn_norm_w, v_w_out=v_w_out, v_ln_g=v_ln_g, v_ln_b=v_ln_b)
    weights = {n: given[n] for n in TWIN_WEIGHTS}
    shared = {n: given[n] for n in SHARED_INPUTS}
    per_example = {n: given[n] for n in ['x', 'c', 'ctx']}
    grad_fn = _jax.value_and_grad(_loss, argnums=(0, 1))

    def one_microbatch(ex, loss_target):
        ex = dict(ex)
        diff = ex.pop(TWIN_DIFF_INPUT)
        return grad_fn(weights, diff, {**shared, **ex}, loss_target)

    if N_MICROBATCH == 1:
        loss, (grad_w, grad_x) = one_microbatch(per_example, given["loss_target"])
    else:
        def body(carry, xs):
            loss_sum, grad_sum = carry
            l_k, (gw_k, gx_k) = one_microbatch(xs[0], xs[1])
            with _jax.named_scope("update"):
                return (loss_sum + l_k, _jax.tree.map(_jnp.add, grad_sum, gw_k)), gx_k

        init = (_jnp.zeros((), _jnp.float32), _jax.tree.map(_jnp.zeros_like, weights))
        (loss, grad_w), grad_x = _jax.lax.scan(body, init, (per_example, given["loss_target"]))
    with _jax.named_scope("update"):
        delta_w, new_m, new_v = {}, {}, {}
        for n in TWIN_WEIGHTS:
            delta_w[n], new_m[n], new_v[n] = _adamw(weights[n], grad_w[n], given["m_" + n], given["v_" + n])
    return (loss, grad_x, *[grad_w[n] for n in TWIN_WEIGHTS], *[delta_w[n] for n in TWIN_WEIGHTS],
            *[new_m[n] for n in TWIN_WEIGHTS], *[new_v[n] for n in TWIN_WEIGHTS])
```

```python
import functools

import jax
import jax.numpy as jnp
from jax import lax
from jax.experimental import pallas as pl
from jax.experimental.pallas import tpu as pltpu

f32 = jnp.float32
bf16 = jnp.bfloat16
SDS = jax.ShapeDtypeStruct

D_MODEL = 1024
D_S5 = 512
S5_GROUP = 16
S5_GROUPS = 32
S5_STATE = 64
S5_HALF = S5_GROUPS * S5_STATE
D_GDN = 512
GDN_HEAD = 128
GDN_HEADS = 4
CHUNK = 64
GRID_W = 64
N_DIR = 2
P_IN = 3088
DEEPNORM_ALPHA = 2.0 ** 0.25
LN_EPS = 1e-5
NORM_EPS = 1e-6
ADAM_LR, ADAM_B1, ADAM_B2, ADAM_EPS, ADAM_WD, ADAM_STEP = 0.001, 0.9, 0.999, 1e-08, 0.01, 10

LANES = 128
VMEM_LIMIT = 56 * 1024 * 1024
TOK_TILE = 256
S5_TILE = 256
MESH = pl.DeviceIdType.MESH


def _cparams(n_grid):
    return pltpu.CompilerParams(dimension_semantics=("arbitrary",) * n_grid, vmem_limit_bytes=VMEM_LIMIT)


def _dot(a, b):
    return jnp.dot(a.astype(bf16), b.astype(bf16), preferred_element_type=f32)


def _dot_nt(a, b):
    return lax.dot_general(a.astype(bf16), b.astype(bf16), (((1,), (1,)), ((), ())), preferred_element_type=f32)


def _dot_tn(a, b):
    return lax.dot_general(a.astype(bf16), b.astype(bf16), (((0,), (0,)), ((), ())), preferred_element_type=f32)


def _dot_hi(a, b):
    return jnp.dot(a, b, precision=lax.Precision.HIGHEST, preferred_element_type=f32)


@jax.custom_vjp
def _mm(a, b):
    return _dot(a, b)


@jax.custom_vjp
def _mm_nt(a, b):
    return _dot_nt(a, b)


@jax.custom_vjp
def _mm_tn(a, b):
    return _dot_tn(a, b)


_mm.defvjp(lambda a, b: (_dot(a, b), (a, b)), lambda r, g: (_mm_nt(g, r[1]), _mm_tn(r[0], g)))
_mm_nt.defvjp(lambda a, b: (_dot_nt(a, b), (a, b)), lambda r, g: (_mm(g, r[1]), _mm_tn(g, r[0])))
_mm_tn.defvjp(lambda a, b: (_dot_tn(a, b), (a, b)), lambda r, g: (_mm_nt(r[1], g), _mm(r[0], g)))


def _silu(x):
    return x * jax.nn.sigmoid(x)


def _gelu(x):
    return 0.5 * x * (1.0 + lax.erf(x * (2.0 ** -0.5)))


def _resident(shape):
    nd = len(shape)
    return pl.BlockSpec(shape, lambda *_: (0,) * nd, pipeline_mode=pl.Buffered(1))


def _tok(tile, width, nt=None, rev=False):
    if rev:
        return pl.BlockSpec((None, tile, width), lambda b, n: (b, nt - 1 - n, 0))
    return pl.BlockSpec((None, tile, width), lambda b, n: (b, n, 0))


def _per_batch(rows, width):
    return pl.BlockSpec((None, rows, width), lambda b, n: (b, 0, 0))


def _first_step():
    return jnp.logical_and(pl.program_id(0) == 0, pl.program_id(1) == 0)


def ada_fwd(cc, w, b):
    def body(cc_ref, w_ref, b_ref, m_ref):
        m_ref[...] = _dot(_silu(cc_ref[...]), w_ref[...]) + b_ref[...]

    return pl.pallas_call(body, name="ada_fwd", out_shape=SDS((8, 3 * D_MODEL), f32),
                          compiler_params=pltpu.CompilerParams(vmem_limit_bytes=VMEM_LIMIT))(cc, w, b)


def ada_bwd(cc, w, dm):
    def body(cc_ref, w_ref, dm_ref, dcc_ref, dw_ref, db_ref):
        dmv = dm_ref[...]
        s, vjp = jax.vjp(_silu, cc_ref[...])
        dcc_ref[...] = vjp(_dot_nt(dmv, w_ref[...]))[0]
        dw_ref[...] = _dot_tn(s, dmv)
        db_ref[...] = jnp.sum(dmv, axis=0, keepdims=True)

    return pl.pallas_call(
        body, name="ada_bwd",
        out_shape=[SDS((8, D_MODEL), f32), SDS((D_MODEL, 3 * D_MODEL), f32), SDS((1, 3 * D_MODEL), f32)],
        compiler_params=pltpu.CompilerParams(vmem_limit_bytes=VMEM_LIMIT))(cc, w, dm)


IN_WIDTHS = (D_S5, D_S5, 3 * D_GDN, D_GDN, LANES)
N_GATE = 2 * N_DIR * GDN_HEADS


def in_proj_fwd(x, mod, ws, *, name):
    B, L, _ = x.shape
    T = min(TOK_TILE, L)

    def body(x_ref, mod_ref, wu, wz, wq, wg, wb, u_ref, z_ref, q_ref, g_ref, ba_ref):
        h = (x_ref[...] * (1.0 + mod_ref[0:1, :]) + mod_ref[1:2, :]).astype(bf16)
        u_ref[...] = _dot(h, wu[...])
        z_ref[...] = _dot(h, wz[...])
        q_ref[...] = _dot(h, wq[...])
        g_ref[...] = _dot(h, wg[...])
        ba_ref[...] = _dot(h, wb[...])[:, :N_GATE]

    outw = (D_S5, D_S5, 3 * D_GDN, D_GDN, N_GATE)
    return pl.pallas_call(
        body, name=name, grid=(B, L // T),
        in_specs=[_tok(T, D_MODEL), _per_batch(2, D_MODEL)] + [_resident((D_MODEL, w)) for w in IN_WIDTHS],
        out_specs=[_tok(T, w) for w in outw],
        out_shape=[SDS((B, L, w), f32) for w in outw],
        compiler_params=_cparams(2),
    )(x, mod, *ws)


def in_proj_bwd(x, mod, ds, ws, gx_res, *, name):
    B, L, _ = x.shape
    T = min(TOK_TILE, L)
    with_dx = gx_res is not None

    def body(*refs):
        x_ref, mod_ref = refs[0], refs[1]
        d_refs = refs[2:7]
        w_refs = refs[7:12]
        k = 12
        if with_dx:
            gx_ref = refs[k]
            k += 1
        dw_refs = refs[k:k + 5]
        dmod_ref = refs[k + 5]
        if with_dx:
            dx_ref = refs[k + 6]
        n = pl.program_id(1)

        @pl.when(_first_step())
        def _():
            for r in dw_refs:
                r[...] = jnp.zeros_like(r)

        @pl.when(n == 0)
        def _():
            dmod_ref[...] = jnp.zeros_like(dmod_ref)

        xv = x_ref[...]
        scale1 = 1.0 + mod_ref[0:1, :]
        h = (xv * scale1 + mod_ref[1:2, :]).astype(bf16)
        dh = jnp.zeros((T, D_MODEL), f32)
        for d_ref, w_ref, dw_ref in zip(d_refs, w_refs, dw_refs):
            dv = d_ref[...].astype(bf16)
            dh = dh + _dot_nt(dv, w_ref[...])
            dw_ref[...] += _dot_tn(h, dv)
        dmod_ref[0:1, :] += jnp.sum(dh * xv, axis=0, keepdims=True)
        dmod_ref[1:2, :] += jnp.sum(dh, axis=0, keepdims=True)
        if with_dx:
            dx_ref[...] = gx_ref[...] + dh * scale1

    in_specs = ([_tok(T, D_MODEL), _per_batch(2, D_MODEL)] + [_tok(T, w) for w in IN_WIDTHS]
                + [_resident((D_MODEL, w)) for w in IN_WIDTHS])
    args = [x, mod, *ds, *ws]
    out_specs = [_resident((D_MODEL, w)) for w in IN_WIDTHS] + [_per_batch(2, D_MODEL)]
    out_shape = [SDS((D_MODEL, w), f32) for w in IN_WIDTHS] + [SDS((B, 2, D_MODEL), f32)]
    if with_dx:
        in_specs.append(_tok(T, D_MODEL))
        args.append(gx_res)
        out_specs.append(_tok(T, D_MODEL))
        out_shape.append(SDS((B, L, D_MODEL), f32))
    return pl.pallas_call(body, name=name, grid=(B, L // T), in_specs=in_specs, out_specs=out_specs,
                          out_shape=out_shape, compiler_params=_cparams(2))(*args)


def _s5_zoh(lr, li, ldt, bre, bim, expand):
    dt = jnp.exp(ldt)
    zr, zi = lr * dt, li * dt
    e = jnp.exp(zr)
    ar, ai = e * jnp.cos(zi), e * jnp.sin(zi)
    den = lr * lr + li * li
    czr = ((ar - 1.0) * lr + ai * li) / den
    czi = (ai * lr - (ar - 1.0) * li) / den
    czr_e, czi_e = _dot_hi(czr, expand), _dot_hi(czi, expand)
    return ar, ai, czr_e * bre - czi_e * bim, czr_e * bim + czi_e * bre


_ZOH_OUT = [(N_DIR * S5_GROUPS, S5_STATE)] * 2 + [(N_DIR * S5_GROUPS, S5_STATE * S5_GROUP)] * 2


def s5_zoh_fwd(lr, li, ldt, bre, bim, expand):
    def body(lr_ref, li_ref, ldt_ref, bre_ref, bim_ref, e_ref, ar_ref, ai_ref, bbr_ref, bbi_ref):
        ar, ai, bbr, bbi = _s5_zoh(lr_ref[...], li_ref[...], ldt_ref[...], bre_ref[...], bim_ref[...], e_ref[...])
        ar_ref[...], ai_ref[...], bbr_ref[...], bbi_ref[...] = ar, ai, bbr, bbi

    return pl.pallas_call(body, name="s5_zoh_fwd", out_shape=[SDS(s, f32) for s in _ZOH_OUT])(
        lr, li, ldt, bre, bim, expand)


def s5_zoh_bwd(lr, li, ldt, bre, bim, expand, dar, dai, dbbr, dbbi):
    def body(lr_ref, li_ref, ldt_ref, bre_ref, bim_ref, e_ref, dar_ref, dai_ref, dbbr_ref, dbbi_ref,
             dlr_ref, dli_ref, dldt_ref, dbre_ref, dbim_ref):
        ev = e_ref[...]
        _, vjp = jax.vjp(lambda a, b, c, d, e: _s5_zoh(a, b, c, d, e, ev),
                         lr_ref[...], li_ref[...], ldt_ref[...], bre_ref[...], bim_ref[...])
        outs = vjp((dar_ref[...], dai_ref[...], dbbr_ref[...], dbbi_ref[...]))
        dlr_ref[...], dli_ref[...], dldt_ref[...], dbre_ref[...], dbim_ref[...] = outs

    shapes = [lr.shape, li.shape, ldt.shape, bre.shape, bim.shape]
    return pl.pallas_call(body, name="s5_zoh_bwd", out_shape=[SDS(s, f32) for s in shapes])(
        lr, li, ldt, bre, bim, expand, dar, dai, dbbr, dbbi)


def _scan_rows(T, rev, ar, ai, hr0, hi0, r_ref, i_ref, off):
    def step(i, carry):
        hr, hi = carry
        t = off + ((T - 1 - i) if rev else i)
        nr = ar * hr - ai * hi + r_ref[pl.ds(t, 1), :]
        ni = ar * hi + ai * hr + i_ref[pl.ds(t, 1), :]
        r_ref[pl.ds(t, 1), :] = nr
        i_ref[pl.ds(t, 1), :] = ni
        return nr, ni

    return lax.fori_loop(0, T, step, (hr0, hi0), unroll=2)


def s5_scan_fwd(u, bre, bim, ctop, cbot, arow, h0, *, d, need_y, name):
    B, L, _ = u.shape
    T = min(S5_TILE, L)
    nt = L // T
    rev = d == 1

    def body(u_ref, bre_ref, bim_ref, ct_ref, cb_ref, a_ref, h0_ref, *rest):
        if need_y:
            y_ref, hin_ref, hend_ref, hr_scr, hi_scr, h_scr = rest
        else:
            hin_ref, hend_ref, hr_scr, hi_scr, h_scr = rest
        n = pl.program_id(1)

        @pl.when(n == 0)
        def _():
            h_scr[...] = h0_ref[...]

        hin_ref[...] = h_scr[...]
        uv = u_ref[...].astype(bf16)
        hr_scr[...] = _dot(uv, bre_ref[...])
        hi_scr[...] = _dot(uv, bim_ref[...])
        hr, hi = _scan_rows(T, rev, a_ref[0:1, :], a_ref[1:2, :], h_scr[0:1, :], h_scr[1:2, :], hr_scr, hi_scr, 0)
        h_scr[0:1, :] = hr
        h_scr[1:2, :] = hi
        if need_y:
            y_ref[...] = _dot(hr_scr[...], ct_ref[...]) + _dot(hi_scr[...], cb_ref[...])

        @pl.when(n == nt - 1)
        def _():
            hend_ref[...] = h_scr[...]

    state = _per_batch(2, S5_HALF)
    hin_spec = pl.BlockSpec((None, None, 2, S5_HALF), (lambda b, n: (b, nt - 1 - n, 0, 0)) if rev else (lambda b, n: (b, n, 0, 0)))
    out_specs = [hin_spec, state]
    out_shape = [SDS((B, nt, 2, S5_HALF), f32), SDS((B, 2, S5_HALF), f32)]
    if need_y:
        out_specs.insert(0, _tok(T, D_S5, nt, rev))
        out_shape.insert(0, SDS((B, L, D_S5), f32))
    return pl.pallas_call(
        body, name=name, grid=(B, nt),
        in_specs=[_tok(T, D_S5, nt, rev), _resident((D_S5, S5_HALF)), _resident((D_S5, S5_HALF)),
                  _resident((S5_HALF, D_S5)), _resident((S5_HALF, D_S5)), _resident((2, S5_HALF)), state],
        out_specs=out_specs, out_shape=out_shape,
        scratch_shapes=[pltpu.VMEM((T, S5_HALF), f32), pltpu.VMEM((T, S5_HALF), f32), pltpu.VMEM((2, S5_HALF), f32)],
        compiler_params=_cparams(2),
    )(u, bre, bim, ctop, cbot, arow, h0)


def s5_scan_bwd(u, dy, bre, bim, ctop, cbot, arow, hin, dhend, *, d, name):
    B, L, _ = u.shape
    T = min(S5_TILE, L)
    nt = L // T
    rev = d == 1
    has_dy = dy is not None
    PAD = 8

    def body(*refs):
        u_ref = refs[0]
        k = 1
        if has_dy:
            dy_ref = refs[1]
            k = 2
        bre_ref, bim_ref, ct_ref, cb_ref, a_ref, hin_ref, dhend_ref = refs[k:k + 7]
        du_ref, dbre_ref, dbim_ref, dct_ref, dcb_ref, da_ref, dh0_ref = refs[k + 7:k + 14]
        hr_scr, hi_scr, gr_scr, gi_scr, p_scr = refs[k + 14:]
        n = pl.program_id(1)

        @pl.when(_first_step())
        def _():
            for r in (dbre_ref, dbim_ref, dct_ref, dcb_ref, da_ref):
                r[...] = jnp.zeros_like(r)

        @pl.when(n == 0)
        def _():
            p_scr[...] = dhend_ref[...]

        ar, ai = a_ref[0:1, :], a_ref[1:2, :]
        uv = u_ref[...].astype(bf16)
        hr_scr[PAD:PAD + T, :] = _dot(uv, bre_ref[...])
        hi_scr[PAD:PAD + T, :] = _dot(uv, bim_ref[...])
        prev_row = PAD + T if rev else PAD - 1
        hr_scr[prev_row:prev_row + 1, :] = hin_ref[0:1, :]
        hi_scr[prev_row:prev_row + 1, :] = hin_ref[1:2, :]
        _scan_rows(T, rev, ar, ai, hin_ref[0:1, :], hin_ref[1:2, :], hr_scr, hi_scr, PAD)
        if has_dy:
            dyv = dy_ref[...].astype(bf16)
            gr_scr[...] = _dot_nt(dyv, ct_ref[...])
            gi_scr[...] = _dot_nt(dyv, cb_ref[...])
            dct_ref[...] += _dot_tn(hr_scr[PAD:PAD + T, :], dyv)
            dcb_ref[...] += _dot_tn(hi_scr[PAD:PAD + T, :], dyv)
        else:
            gr_scr[...] = jnp.zeros_like(gr_scr)
            gi_scr[...] = jnp.zeros_like(gi_scr)

        def step(i, carry):
            pr, pi, dar, dai = carry
            t = i if rev else T - 1 - i
            gr = gr_scr[pl.ds(t, 1), :] + pr
            gi = gi_scr[pl.ds(t, 1), :] + pi
            gr_scr[pl.ds(t, 1), :] = gr
            gi_scr[pl.ds(t, 1), :] = gi
            tp = PAD + t + (1 if rev else -1)
            hpr = hr_scr[pl.ds(tp, 1), :]
            hpi = hi_scr[pl.ds(tp, 1), :]
            dar = dar + hpr * gr + hpi * gi
            dai = dai + hpr * gi - hpi * gr
            return ar * gr + ai * gi, ar * gi - ai * gr, dar, dai

        zero = jnp.zeros((1, S5_HALF), f32)
        pr, pi, dar, dai = lax.fori_loop(0, T, step, (p_scr[0:1, :], p_scr[1:2, :], zero, zero), unroll=2)
        p_scr[0:1, :] = pr
        p_scr[1:2, :] = pi
        da_ref[0:1, :] += dar
        da_ref[1:2, :] += dai
        gr_all = gr_scr[...].astype(bf16)
        gi_all = gi_scr[...].astype(bf16)
        du_ref[...] = _dot_nt(gr_all, bre_ref[...]) + _dot_nt(gi_all, bim_ref[...])
        dbre_ref[...] += _dot_tn(uv, gr_all)
        dbim_ref[...] += _dot_tn(uv, gi_all)

        @pl.when(n == nt - 1)
        def _():
            dh0_ref[...] = p_scr[...]

    brev = not rev
    state = _per_batch(2, S5_HALF)
    hin_spec = pl.BlockSpec((None, None, 2, S5_HALF), (lambda b, n: (b, nt - 1 - n, 0, 0)) if brev else (lambda b, n: (b, n, 0, 0)))
    wspecs = [_resident((D_S5, S5_HALF)), _resident((D_S5, S5_HALF)), _resident((S5_HALF, D_S5)),
              _resident((S5_HALF, D_S5))]
    in_specs = [_tok(T, D_S5, nt, brev)] + ([_tok(T, D_S5, nt, brev)] if has_dy else []) + wspecs + [
        _resident((2, S5_HALF)), hin_spec, state]
    args = [u] + ([dy] if has_dy else []) + [bre, bim, ctop, cbot, arow, hin, dhend]
    return pl.pallas_call(
        body, name=name, grid=(B, nt), in_specs=in_specs,
        out_specs=[_tok(T, D_S5, nt, brev)] + wspecs + [_resident((2, S5_HALF)), state],
        out_shape=[SDS((B, L, D_S5), f32), SDS((D_S5, S5_HALF), f32), SDS((D_S5, S5_HALF), f32),
                   SDS((S5_HALF, D_S5), f32), SDS((S5_HALF, D_S5), f32), SDS((2, S5_HALF), f32),
                   SDS((B, 2, S5_HALF), f32)],
        scratch_shapes=[pltpu.VMEM((T + 2 * PAD, S5_HALF), f32), pltpu.VMEM((T + 2 * PAD, S5_HALF), f32),
                        pltpu.VMEM((T, S5_HALF), f32), pltpu.VMEM((T, S5_HALF), f32), pltpu.VMEM((2, S5_HALF), f32)],
        compiler_params=_cparams(2),
    )(*args)


def _glu_fn(u, y0, y1, z, dsk, wg, bg):
    g = _gelu(dsk * u + y0 + y1)
    return g * jax.nn.sigmoid(_mm(g, wg) + bg) * _silu(z)


def s5_glu_fwd(u, y0, y1, z, dsk, wg, bg):
    B, L, _ = u.shape
    T = min(TOK_TILE, L)

    def body(u_ref, y0_ref, y1_ref, z_ref, dsk_ref, wg_ref, bg_ref, o_ref):
        o_ref[...] = _glu_fn(u_ref[...], y0_ref[...], y1_ref[...], z_ref[...], dsk_ref[...], wg_ref[...], bg_ref[...])

    t = _tok(T, D_S5)
    return pl.pallas_call(
        body, name="s5_glu_fwd", grid=(B, L // T),
        in_specs=[t, t, t, t, _resident((1, D_S5)), _resident((D_S5, D_S5)), _resident((1, D_S5))],
        out_specs=t, out_shape=SDS((B, L, D_S5), f32), compiler_params=_cparams(2),
    )(u, y0, y1, z, dsk, wg, bg)


def s5_glu_bwd(u, y0, y1, z, dsk, wg, bg, dout):
    B, L, _ = u.shape
    T = min(TOK_TILE, L)

    def body(u_ref, y0_ref, y1_ref, z_ref, dsk_ref, wg_ref, bg_ref, do_ref, du_ref, dy_ref, dz_ref,
             ddsk_ref, dwg_ref, dbg_ref):
        @pl.when(_first_step())
        def _():
            for r in (ddsk_ref, dwg_ref, dbg_ref):
                r[...] = jnp.zeros_like(r)

        _, vjp = jax.vjp(_glu_fn, u_ref[...], y0_ref[...], y1_ref[...], z_ref[...], dsk_ref[...], wg_ref[...],
                         bg_ref[...])
        du, dy, _, dz, ddsk, dwg, dbg = vjp(do_ref[...])
        du_ref[...], dy_ref[...], dz_ref[...] = du, dy, dz
        ddsk_ref[...] += ddsk
        dwg_ref[...] += dwg
        dbg_ref[...] += dbg

    t = _tok(T, D_S5)
    small = [_resident((1, D_S5)), _resident((D_S5, D_S5)), _resident((1, D_S5))]
    return pl.pallas_call(
        body, name="s5_glu_bwd", grid=(B, L // T),
        in_specs=[t, t, t, t] + small + [t], out_specs=[t, t, t] + small,
        out_shape=[SDS((B, L, D_S5), f32)] * 3 + [SDS((1, D_S5), f32), SDS((D_S5, D_S5), f32), SDS((1, D_S5), f32)],
        compiler_params=_cparams(2),
    )(u, y0, y1, z, dsk, wg, bg, dout)


CONV_ROWS = 16


def _conv_taps(L, is_ctx):
    t = lax.broadcasted_iota(jnp.int32, (L, 1), 0)
    taps = []
    for di in ((1,) if is_ctx else (0, 1, 2)):
        for dj in (0, 1, 2):
            s = (0 if is_ctx else GRID_W * (di - 1)) + (dj - 1)
            if is_ctx:
                ok = jnp.logical_and(t + s >= 0, t + s < L)
            else:
                col = jnp.bitwise_and(t, GRID_W - 1) + (dj - 1)
                row = t + GRID_W * (di - 1)
                ok = jnp.logical_and(jnp.logical_and(col >= 0, col < GRID_W), jnp.logical_and(row >= 0, row < L))
            taps.append((di * 3 + dj, s, ok.astype(f32)))
    return taps


def _shift(x, s):
    L = x.shape[0]
    k = (-s) % L
    return x if k == 0 else pltpu.roll(x, k, axis=0)


def _qk_post(pre, is_norm, scale):
    s = _silu(pre)
    nrm = lax.rsqrt(jnp.sum(s * s, axis=-1, keepdims=True) + NORM_EPS)
    return s * jnp.where(is_norm, nrm * scale, 1.0)


def _conv_kind():
    ct = pl.program_id(1)
    return ct < 2 * GDN_HEADS, jnp.where(ct < GDN_HEADS, GDN_HEAD ** -0.5, 1.0).astype(f32)


def _conv_pre(xv, w_ref, taps):
    pre = jnp.zeros_like(xv)
    for r, s, m in taps:
        pre = pre + w_ref[r:r + 1, :] * (m * _shift(xv, s))
    return pre


def conv_fwd(qkv, w16, *, is_ctx, name):
    B, L, C = qkv.shape
    spec = pl.BlockSpec((None, L, GDN_HEAD), lambda b, ct: (b, 0, ct))
    wspec = pl.BlockSpec((CONV_ROWS, GDN_HEAD), lambda b, ct: (0, ct))

    def body(x_ref, w_ref, o_ref):
        is_norm, scale = _conv_kind()
        o_ref[...] = _qk_post(_conv_pre(x_ref[...], w_ref, _conv_taps(L, is_ctx)), is_norm, scale)

    return pl.pallas_call(body, name=name, grid=(B, C // GDN_HEAD), in_specs=[spec, wspec], out_specs=spec,
                          out_shape=SDS((B, L, C), f32), compiler_params=_cparams(2))(qkv, w16)


def conv_bwd(qkv, w16, da0, da1, *, is_ctx, name):
    B, L, C = qkv.shape
    spec = pl.BlockSpec((None, L, GDN_HEAD), lambda b, ct: (b, 0, ct))
    wspec = pl.BlockSpec((CONV_ROWS, GDN_HEAD), lambda b, ct: (0, ct))
    dwspec = pl.BlockSpec((None, CONV_ROWS, GDN_HEAD), lambda b, ct: (b, 0, ct))

    def body(x_ref, w_ref, d0_ref, d1_ref, dx_ref, dw_ref):
        is_norm, scale = _conv_kind()
        taps = _conv_taps(L, is_ctx)
        xv = x_ref[...]
        _, vjp = jax.vjp(lambda p: _qk_post(p, is_norm, scale), _conv_pre(xv, w_ref, taps))
        dpre = vjp(d0_ref[...] + d1_ref[...])[0]
        dx = jnp.zeros_like(xv)
        dw_ref[...] = jnp.zeros_like(dw_ref)
        for r, s, m in taps:
            md = m * dpre
            dx = dx + _shift(w_ref[r:r + 1, :] * md, -s)
            dw_ref[r:r + 1, :] = jnp.sum(md * _shift(xv, s), axis=0, keepdims=True)
        dx_ref[...] = dx

    return pl.pallas_call(body, name=name, grid=(B, C // GDN_HEAD), in_specs=[spec, wspec, spec, spec],
                          out_specs=[spec, dwspec], out_shape=[SDS((B, L, C), f32), SDS((B, CONV_ROWS, C), f32)],
                          compiler_params=_cparams(2))(qkv, w16, da0, da1)


def _gates_fn(ba, alog, dtb):
    T = ba.shape[0]
    lane = lax.broadcasted_iota(jnp.int32, ba.shape, 1)
    ii = lax.broadcasted_iota(jnp.int32, (T, T), 0)
    jj = lax.broadcasted_iota(jnp.int32, (T, T), 1)
    same = jnp.right_shift(ii, 6) == jnp.right_shift(jj, 6)
    lmat = jnp.logical_and(same, ii >= jj).astype(f32)
    umat = jnp.logical_and(same, ii <= jj).astype(f32)
    g = jnp.where(lane >= 8, -jnp.exp(alog) * jax.nn.softplus(ba + dtb), 0.0)
    gc = jnp.where(lane >= 12, _dot_hi(umat, g), _dot_hi(lmat, g))
    return jnp.where(lane < 8, jax.nn.sigmoid(ba), gc)


def gates_fwd(ba, alog, dtb, *, name):
    B, L, _ = ba.shape
    T = min(TOK_TILE, L)
    t = _tok(T, N_GATE)

    def body(ba_ref, al_ref, dt_ref, o_ref):
        o_ref[...] = _gates_fn(ba_ref[...], al_ref[...], dt_ref[...])

    return pl.pallas_call(body, name=name, grid=(B, L // T),
                          in_specs=[t, _resident((1, N_GATE)), _resident((1, N_GATE))], out_specs=t,
                          out_shape=SDS((B, L, N_GATE), f32), compiler_params=_cparams(2))(ba, alog, dtb)


def gates_bwd(ba, alog, dtb, dbg, *, name):
    B, L, _ = ba.shape
    T = min(TOK_TILE, L)
    t = _tok(T, N_GATE)
    small = _resident((1, N_GATE))

    def body(ba_ref, al_ref, dt_ref, d_ref, dba_ref, dal_ref, ddt_ref):
        @pl.when(_first_step())
        def _():
            dal_ref[...] = jnp.zeros_like(dal_ref)
            ddt_ref[...] = jnp.zeros_like(ddt_ref)

        _, vjp = jax.vjp(_gates_fn, ba_ref[...], al_ref[...], dt_ref[...])
        dba, dal, ddt = vjp(d_ref[...])
        dba_ref[...] = dba
        dal_ref[...] += dal
        ddt_ref[...] += ddt

    return pl.pallas_call(body, name=name, grid=(B, L // T), in_specs=[t, small, small, t],
                          out_specs=[t, small, small],
                          out_shape=[SDS((B, L, N_GATE), f32), SDS((1, N_GATE), f32), SDS((1, N_GATE), f32)],
                          compiler_params=_cparams(2))(ba, alog, dtb, dbg)


@jax.custom_vjp
def _inv_unit_tri(a):
    n = a.shape[0]
    eye = (lax.broadcasted_iota(jnp.int32, (n, n), 0) == lax.broadcasted_iota(jnp.int32, (n, n), 1)).astype(f32)
    x = eye - a
    p = _dot_hi(a, a)
    k = 2
    while k < n:
        x = x + _dot_hi(x, p)
        k *= 2
        if k < n:
            p = _dot_hi(p, p)
    return x


def _inv_unit_tri_fwd(a):
    x = _inv_unit_tri(a)
    return x, x


def _inv_unit_tri_bwd(x, dx):
    xt = x.T
    return (-_dot_hi(_dot_hi(xt, dx), xt),)


_inv_unit_tri.defvjp(_inv_unit_tri_fwd, _inv_unit_tri_bwd)


def _gdn_chunk(q, k, v, beta, gc, gr, s, *, rev):
    n = q.shape[0]
    ii = lax.broadcasted_iota(jnp.int32, (n, n), 0)
    jj = lax.broadcasted_iota(jnp.int32, (n, n), 1)
    lower = (ii <= jj) if rev else (ii >= jj)
    strict = (ii < jj) if rev else (ii > jj)
    decay = jnp.where(lower, jnp.exp(jnp.where(lower, gc - gr, 0.0)), 0.0)
    kk = _mm_nt(k, k)
    a_mat = jnp.where(strict, beta * kk * decay, 0.0)
    gamma = jnp.exp(gc)
    last = 0 if rev else n - 1
    row = lax.broadcasted_iota(jnp.int32, (n, 1), 0)
    g_last = jnp.sum(jnp.where(row == last, gc, 0.0), axis=0, keepdims=True)
    tinv = _inv_unit_tri(a_mat)
    u0 = _dot_hi(tinv, beta * v)
    w = _dot_hi(tinv, (beta * gamma) * k)
    qk = _mm_nt(q, k) * decay
    k_out = k * jnp.exp(g_last - gc)
    u = u0 - _mm(w, s)
    o = gamma * _mm(q, s) + _mm(qk, u)
    s_new = jnp.exp(g_last) * s + _mm_tn(k_out, u)
    return o, s_new


def _gdn_specs(nc, rev):
    def cidx(n):
        return (nc - 1 - n) if rev else n
    tok = lambda width: pl.BlockSpec((None, CHUNK, width), lambda b, n: (b, cidx(n), 0))
    rowspec = pl.BlockSpec((None, None, N_GATE, CHUNK), lambda b, n: (b, cidx(n), 0, 0))
    st = pl.BlockSpec((None, GDN_HEADS, GDN_HEAD, GDN_HEAD), lambda b, n: (b, 0, 0, 0))
    ck = pl.BlockSpec((None, None, GDN_HEADS, GDN_HEAD, GDN_HEAD), lambda b, n: (b, cidx(n), 0, 0, 0))
    return tok, rowspec, st, ck


def _gdn_head_args(qkv_ref, bgv, bgr, d, h):
    col = d * GDN_HEADS + h
    q = qkv_ref[:, h * GDN_HEAD:(h + 1) * GDN_HEAD]
    k = qkv_ref[:, D_GDN + h * GDN_HEAD:D_GDN + (h + 1) * GDN_HEAD]
    v = qkv_ref[:, 2 * D_GDN + h * GDN_HEAD:2 * D_GDN + (h + 1) * GDN_HEAD]
    return q, k, v, bgv[:, col:col + 1], bgv[:, 8 + col:9 + col], bgr[8 + col:9 + col, :]


def gdn_fwd(qkv, bg, bgr, s0, *, d, need_o, name):
    B, L, _ = qkv.shape
    nc = L // CHUNK
    rev = d == 1
    tok, rowspec, st, ck = _gdn_specs(nc, rev)

    def body(qkv_ref, bg_ref, bgr_ref, s0_ref, *rest):
        if need_o:
            o_ref, ck_ref, sf_ref, s_scr = rest
        else:
            ck_ref, sf_ref, s_scr = rest
        n = pl.program_id(1)

        @pl.when(n == 0)
        def _():
            s_scr[...] = s0_ref[...]

        bgv = bg_ref[...]
        bgrv = bgr_ref[...]
        for h in range(GDN_HEADS):
            s_in = s_scr[h]
            ck_ref[h] = s_in
            o, s_new = _gdn_chunk(*_gdn_head_args(qkv_ref, bgv, bgrv, d, h), s_in, rev=rev)
            if need_o:
                o_ref[:, h * GDN_HEAD:(h + 1) * GDN_HEAD] = o
            s_scr[h] = s_new

        @pl.when(n == nc - 1)
        def _():
            sf_ref[...] = s_scr[...]

    out_specs = [ck, st]
    out_shape = [SDS((B, nc, GDN_HEADS, GDN_HEAD, GDN_HEAD), f32), SDS((B, GDN_HEADS, GDN_HEAD, GDN_HEAD), f32)]
    if need_o:
        out_specs.insert(0, tok(D_GDN))
        out_shape.insert(0, SDS((B, L, D_GDN), f32))
    return pl.pallas_call(
        body, name=name, grid=(B, nc), in_specs=[tok(3 * D_GDN), tok(N_GATE), rowspec, st],
        out_specs=out_specs, out_shape=out_shape,
        scratch_shapes=[pltpu.VMEM((GDN_HEADS, GDN_HEAD, GDN_HEAD), f32)], compiler_params=_cparams(2),
    )(qkv, bg, bgr, s0)


def gdn_bwd(qkv, bg, bgr, ck, do, dsf, *, d, name):
    B, L, _ = qkv.shape
    nc = L // CHUNK
    rev = d == 1
    has_do = do is not None
    tok, rowspec, st, ckspec = _gdn_specs(nc, not rev)

    def body(*refs):
        qkv_ref, bg_ref, bgr_ref, ck_ref = refs[:4]
        k = 4
        if has_do:
            do_ref = refs[4]
            k = 5
        dsf_ref, dqkv_ref, dbg_ref, dbgr_ref, ds0_ref, ds_scr = refs[k:]
        n = pl.program_id(1)

        @pl.when(n == 0)
        def _():
            ds_scr[...] = dsf_ref[...]

        bgv = bg_ref[...]
        bgrv = bgr_ref[...]
        lane = lax.broadcasted_iota(jnp.int32, (CHUNK, N_GATE), 1)
        sub = lax.broadcasted_iota(jnp.int32, (N_GATE, CHUNK), 0)
        dbg_acc = jnp.zeros((CHUNK, N_GATE), f32)
        dbgr_acc = jnp.zeros((N_GATE, CHUNK), f32)
        for h in range(GDN_HEADS):
            col = d * GDN_HEADS + h
            sl = slice(h * GDN_HEAD, (h + 1) * GDN_HEAD)
            _, vjp = jax.vjp(functools.partial(_gdn_chunk, rev=rev), *_gdn_head_args(qkv_ref, bgv, bgrv, d, h),
                             ck_ref[h])
            do_h = do_ref[:, sl] if has_do else jnp.zeros((CHUNK, GDN_HEAD), f32)
            dq, dk, dv, db, dgc, dgr, ds = vjp((do_h, ds_scr[h]))
            dqkv_ref[:, h * GDN_HEAD:(h + 1) * GDN_HEAD] = dq
            dqkv_ref[:, D_GDN + h * GDN_HEAD:D_GDN + (h + 1) * GDN_HEAD] = dk
            dqkv_ref[:, 2 * D_GDN + h * GDN_HEAD:2 * D_GDN + (h + 1) * GDN_HEAD] = dv
            dbg_acc = dbg_acc + jnp.where(lane == col, db, 0.0) + jnp.where(lane == 8 + col, dgc, 0.0)
            dbgr_acc = dbgr_acc + jnp.where(sub == 8 + col, dgr, 0.0)
            ds_scr[h] = ds
        dbg_ref[...] = dbg_acc
        dbgr_ref[...] = dbgr_acc

        @pl.when(n == nc - 1)
        def _():
            ds0_ref[...] = ds_scr[...]

    in_specs = [tok(3 * D_GDN), tok(N_GATE), rowspec, ckspec] + ([tok(D_GDN)] if has_do else []) + [st]
    args = [qkv, bg, bgr, ck] + ([do] if has_do else []) + [dsf]
    return pl.pallas_call(
        body, name=name, grid=(B, nc), in_specs=in_specs,
        out_specs=[tok(3 * D_GDN), tok(N_GATE), rowspec, st],
        out_shape=[SDS((B, L, 3 * D_GDN), f32), SDS((B, L, N_GATE), f32), SDS((B, nc, N_GATE, CHUNK), f32),
                   SDS((B, GDN_HEADS, GDN_HEAD, GDN_HEAD), f32)],
        scratch_shapes=[pltpu.VMEM((GDN_HEADS, GDN_HEAD, GDN_HEAD), f32)], compiler_params=_cparams(2),
    )(*args)


def _gnorm_fn(o0, o1, z, w):
    o = o0 + o1
    return o * lax.rsqrt(jnp.mean(o * o, axis=-1, keepdims=True) + NORM_EPS) * w * _silu(z)


def gnorm_fwd(o0, o1, z, w):
    B, L, _ = o0.shape
    T = min(TOK_TILE, L)
    t = _tok(T, D_GDN)

    def body(o0_ref, o1_ref, z_ref, w_ref, out_ref):
        for h in range(GDN_HEADS):
            sl = slice(h * GDN_HEAD, (h + 1) * GDN_HEAD)
            out_ref[:, sl] = _gnorm_fn(o0_ref[:, sl], o1_ref[:, sl], z_ref[:, sl], w_ref[...])

    return pl.pallas_call(body, name="gnorm_fwd", grid=(B, L // T), in_specs=[t, t, t, _resident((1, GDN_HEAD))],
                          out_specs=t, out_shape=SDS((B, L, D_GDN), f32), compiler_params=_cparams(2))(o0, o1, z, w)


def gnorm_bwd(o0, o1, z, w, dout):
    B, L, _ = o0.shape
    T = min(TOK_TILE, L)
    t = _tok(T, D_GDN)

    def body(o0_ref, o1_ref, z_ref, w_ref, d_ref, do_ref, dz_ref, dw_ref):
        @pl.when(_first_step())
        def _():
            dw_ref[...] = jnp.zeros_like(dw_ref)

        for h in range(GDN_HEADS):
            sl = slice(h * GDN_HEAD, (h + 1) * GDN_HEAD)
            _, vjp = jax.vjp(_gnorm_fn, o0_ref[:, sl], o1_ref[:, sl], z_ref[:, sl], w_ref[...])
            do, _, dz, dw = vjp(d_ref[:, sl])
            do_ref[:, sl] = do
            dz_ref[:, sl] = dz
            dw_ref[...] += dw

    return pl.pallas_call(body, name="gnorm_bwd", grid=(B, L // T),
                          in_specs=[t, t, t, _resident((1, GDN_HEAD)), t], out_specs=[t, t, _resident((1, GDN_HEAD))],
                          out_shape=[SDS((B, L, D_GDN), f32), SDS((B, L, D_GDN), f32), SDS((1, GDN_HEAD), f32)],
                          compiler_params=_cparams(2))(o0, o1, z, w, dout)


def _head_loss(y, x, gate, lng, lnb, tgt):
    r = DEEPNORM_ALPHA * x + gate * y
    mu = jnp.mean(r, axis=-1, keepdims=True)
    rc = r - mu
    var = jnp.mean(rc * rc, axis=-1, keepdims=True)
    err = rc * lax.rsqrt(var + LN_EPS) * lng + lnb - tgt
    return (0.5 / D_MODEL) * jnp.sum(jnp.sum(err * err, axis=-1, keepdims=True), axis=0, keepdims=True)


def head_fwd_bwd(s5o, gdo, x, tgt, gate, lng, lnb, ws, wg):
    B, L, _ = x.shape
    T = min(TOK_TILE, L)

    def body(s_ref, g_ref, x_ref, t_ref, gate_ref, lng_ref, lnb_ref, ws_ref, wg_ref,
             loss_ref, ds_ref, dg_ref, gx_ref, dws_ref, dwg_ref, dgate_ref, dlng_ref, dlnb_ref):
        n = pl.program_id(1)

        @pl.when(_first_step())
        def _():
            for r in (dws_ref, dwg_ref, dlng_ref, dlnb_ref):
                r[...] = jnp.zeros_like(r)

        @pl.when(n == 0)
        def _():
            loss_ref[...] = jnp.zeros_like(loss_ref)
            dgate_ref[...] = jnp.zeros_like(dgate_ref)

        sv = s_ref[...].astype(bf16)
        gv = g_ref[...].astype(bf16)
        y = _dot(sv, ws_ref[...]) + _dot(gv, wg_ref[...])
        loss, vjp = jax.vjp(lambda *a: _head_loss(*a, t_ref[...]), y, x_ref[...], gate_ref[...], lng_ref[...],
                            lnb_ref[...])
        dy, dx, dgate, dlng, dlnb = vjp(jnp.ones((1, 1), f32))
        loss_ref[...] += jnp.broadcast_to(loss, loss_ref.shape)
        dyb = dy.astype(bf16)
        ds_ref[...] = _dot_nt(dyb, ws_ref[...])
        dg_ref[...] = _dot_nt(dyb, wg_ref[...])
        gx_ref[...] = dx
        dws_ref[...] += _dot_tn(sv, dyb)
        dwg_ref[...] += _dot_tn(gv, dyb)
        dgate_ref[...] += dgate
        dlng_ref[...] += dlng
        dlnb_ref[...] += dlnb

    half, full = _tok(T, D_S5), _tok(T, D_MODEL)
    row = _resident((1, D_MODEL))
    wsp = _resident((D_S5, D_MODEL))
    return pl.pallas_call(
        body, name="head_fwd_bwd", grid=(B, L // T),
        in_specs=[half, half, full, full, _per_batch(1, D_MODEL), row, row, wsp, wsp],
        out_specs=[_per_batch(8, LANES), half, half, full, wsp, wsp, _per_batch(1, D_MODEL), row, row],
        out_shape=[SDS((B, 8, LANES), f32), SDS((B, L, D_S5), f32), SDS((B, L, D_GDN), f32), SDS((B, L, D_MODEL), f32),
                   SDS((D_S5, D_MODEL), f32), SDS((D_GDN, D_MODEL), f32), SDS((B, 1, D_MODEL), f32),
                   SDS((1, D_MODEL), f32), SDS((1, D_MODEL), f32)],
        compiler_params=_cparams(2),
    )(s5o, gdo, x, tgt, gate, lng, lnb, ws, wg)


def adamw(w, g, m, v, *, name):
    R = w.shape[0]
    T = R
    for cand in (2048, 1864, 1088, 1024, 544, 512, 272, 256, 136, 128, 64, 32, 16, 8):
        if R % cand == 0:
            T = cand
            break
    spec = pl.BlockSpec((T, LANES), lambda i: (i, 0))

    def body(w_ref, g_ref, m_ref, v_ref, d_ref, nm_ref, nv_ref):
        gv = g_ref[...]
        nm = ADAM_B1 * m_ref[...] + (1.0 - ADAM_B1) * gv
        nv = ADAM_B2 * v_ref[...] + (1.0 - ADAM_B2) * jnp.square(gv)
        m_hat = nm / (1.0 - ADAM_B1 ** ADAM_STEP)
        v_hat = nv / (1.0 - ADAM_B2 ** ADAM_STEP)
        d_ref[...] = -ADAM_LR * (m_hat / (jnp.sqrt(v_hat) + ADAM_EPS) + ADAM_WD * w_ref[...])
        nm_ref[...] = nm
        nv_ref[...] = nv

    return pl.pallas_call(body, name=name, grid=(R // T,), in_specs=[spec] * 4, out_specs=[spec] * 3,
                          out_shape=[SDS((R, LANES), f32)] * 3, compiler_params=_cparams(1))(w, g, m, v)


def add_rows(a, b, *, name, extra=()):
    R = a.shape[0]
    T = R
    for cand in (2048, 1932, 1864, 1288, 1104, 1024, 966, 552, 512, 256, 184, 128, 64, 32, 16, 8):
        if R % cand == 0 and cand % 8 == 0:
            T = cand
            break
    spec = pl.BlockSpec((T, LANES), lambda i: (i, 0))
    ops = (a, b) + tuple(extra)

    def body(*refs):
        acc = refs[0][...]
        for r in refs[1:-1]:
            acc = acc + r[...]
        refs[-1][...] = acc

    return pl.pallas_call(body, name=name, grid=(R // T,), in_specs=[spec] * len(ops), out_specs=spec,
                          out_shape=SDS((R, LANES), f32), compiler_params=_cparams(1))(*ops)


CHIP_FLIPS = ((1, 0), (0, 1), (1, 1))


def _pos():
    return lax.axis_index("x"), lax.axis_index("y"), lax.axis_index("c")


def _comm_call(body, src, out_shape, n_remote, name):
    any_spec = pl.BlockSpec(memory_space=pl.ANY)
    return pl.pallas_call(
        body, name=name, in_specs=[any_spec], out_specs=any_spec, out_shape=SDS(out_shape, f32),
        scratch_shapes=[pltpu.SemaphoreType.DMA((n_remote,)), pltpu.SemaphoreType.DMA((n_remote,)),
                        pltpu.SemaphoreType.DMA((1,))],
        compiler_params=pltpu.CompilerParams(has_side_effects=True),
    )(src)


def gather_shards(shard):
    _, H, _ = shard.shape

    def body(src, out, send_sems, recv_sems, loc_sem):
        x, y, c = _pos()
        j = 2 * x + y
        sib = (x, y, 1 - c)
        mine = pltpu.make_async_copy(src, out.at[j], loc_sem.at[0])
        mine.start()
        first, passed = [], []
        for k, (fx, fy) in enumerate(CHIP_FLIPS):
            tx, ty = x ^ fx, y ^ fy
            first.append(pltpu.make_async_remote_copy(src.at[c], out.at[j, c], send_sems.at[k], recv_sems.at[k],
                                                      device_id=(tx, ty, c), device_id_type=MESH))
            jk = 2 * tx + ty
            passed.append(pltpu.make_async_remote_copy(out.at[jk, c], out.at[jk, c], send_sems.at[3 + k],
                                                       recv_sems.at[3 + k], device_id=sib, device_id_type=MESH))
        for cp in first:
            cp.start()
        for k in range(3):
            first[k].wait_recv()
            passed[k].start()
        for k in range(3):
            passed[k].wait_recv()
        for cp in first + passed:
            cp.wait_send()
        mine.wait()

    return _comm_call(body, shard, (4, 2, H, LANES), 6, "gather_shards")


def swap_halves(p):
    A, _, H, _ = p.shape

    def body(src, out, send_sems, recv_sems, loc_sem):
        x, y, c = _pos()
        cps = [pltpu.make_async_remote_copy(src.at[a, 1 - c], out.at[a], send_sems.at[a], recv_sems.at[a],
                                            device_id=(x, y, 1 - c), device_id_type=MESH) for a in range(A)]
        for cp in cps:
            cp.start()
        for cp in cps:
            cp.wait()

    return _comm_call(body, p, (A, H, LANES), A, "swap_halves")


def scatter_to_chips(q):
    _, H, _ = q.shape

    def body(src, out, send_sems, recv_sems, loc_sem):
        x, y, c = _pos()
        cps = []
        for k, (fx, fy) in enumerate(CHIP_FLIPS):
            tx, ty = x ^ fx, y ^ fy
            cps.append(pltpu.make_async_remote_copy(src.at[2 * tx + ty], out.at[k], send_sems.at[k], recv_sems.at[k],
                                                    device_id=(tx, ty, c), device_id_type=MESH))
        for cp in cps:
            cp.start()
        for cp in cps:
            cp.wait()

    return _comm_call(body, q, (3, H, LANES), 3, "scatter_to_chips")


def join_halves(f):
    H, _ = f.shape

    def body(src, out, send_sems, recv_sems, loc_sem):
        x, y, c = _pos()
        mine = pltpu.make_async_copy(src, out.at[c], loc_sem.at[0])
        mine.start()
        cp = pltpu.make_async_remote_copy(src, out.at[c], send_sems.at[0], recv_sems.at[0],
                                          device_id=(x, y, 1 - c), device_id_type=MESH)
        cp.start()
        cp.wait()
        mine.wait()

    return _comm_call(body, f, (2, H, LANES), 1, "join_halves")


def gather_small(s):
    R, _ = s.shape

    def body(src, out, send_sems, recv_sems, loc_sem):
        x, y, c = _pos()
        j = 2 * x + y
        mine = pltpu.make_async_copy(src, out.at[j], loc_sem.at[0])
        mine.start()
        cps = []
        for k, (fx, fy) in enumerate(CHIP_FLIPS):
            cps.append(pltpu.make_async_remote_copy(src, out.at[j], send_sems.at[k], recv_sems.at[k],
                                                    device_id=(x ^ fx, y ^ fy, c), device_id_type=MESH))
        for cp in cps:
            cp.start()
        for cp in cps:
            cp.wait()
        mine.wait()

    return _comm_call(body, s, (4, R, LANES), 3, "gather_small")


def _rows(a):
    flat = a.reshape(-1)
    pad = (-flat.shape[0]) % LANES
    if pad:
        flat = jnp.concatenate([flat, jnp.zeros((pad,), flat.dtype)])
    return flat.reshape(-1, LANES)


SHARD_SHAPES = ((D_MODEL, 768), (D_MODEL, 772), (256, D_MODEL), (128, D_S5), (3, 3, 384))
SHARD_ROWS = tuple(-(-(functools.reduce(lambda p, q: p * q, s)) // LANES) for s in SHARD_SHAPES)
SHARD_TOTAL = 14912
SMALL_SHAPES = ((D_MODEL,), (1, 3 * D_MODEL), (1, 2, 32, 64), (1, 2, 32, 64), (1, 2, 32), (1, 2, 32, 64, 16),
                (1, 2, 32, 64, 16), (1, 2, 32, 16, 64), (1, 2, 32, 16, 64), (1, D_S5), (1, D_S5), (1, 2, 4), (1, 2, 4),
                (1, GDN_HEAD), (1, D_MODEL), (1, D_MODEL))
SMALL_ROWS = tuple(-(-(functools.reduce(lambda p, q: p * q, s)) // LANES) for s in SMALL_SHAPES)
SMALL_TOTAL = 2176
SMALL_QUARTER = SMALL_TOTAL // 4


def _pack(parts, total):
    rows = [_rows(p.astype(f32)) for p in parts]
    n = sum(r.shape[0] for r in rows)
    if total > n:
        rows.append(jnp.zeros((total - n, LANES), f32))
    return jnp.concatenate(rows, axis=0)


def _unpack(buf, shapes, nrows):
    out, r = [], 0
    for s, n in zip(shapes, nrows):
        size = functools.reduce(lambda p, q: p * q, s)
        out.append(buf[r:r + n].reshape(-1)[:size].reshape(s))
        r += n
    return out


def _block_diag_in(bb):
    eye = jnp.eye(S5_GROUPS, dtype=f32)
    b3 = bb.reshape(S5_GROUPS, S5_STATE, S5_GROUP)
    return jnp.einsum('gpc,gh->gchp', b3, eye).reshape(D_S5, S5_HALF)


def _block_diag_in_t(d):
    d5 = d.reshape(S5_GROUPS, S5_GROUP, S5_GROUPS, S5_STATE)
    return jnp.einsum('gcgp->gpc', d5).reshape(S5_GROUPS, S5_STATE * S5_GROUP)


def _block_diag_out(cm):
    eye = jnp.eye(S5_GROUPS, dtype=f32)
    return jnp.einsum('gcp,gh->hpgc', cm, eye).reshape(S5_HALF, D_S5)


def _block_diag_out_t(d):
    d5 = d.reshape(S5_GROUPS, S5_STATE, S5_GROUPS, S5_GROUP)
    return jnp.einsum('gpgc->gcp', d5)


def _to_chunk_rows(a):
    B, L, W = a.shape
    return a.reshape(B, L // CHUNK, CHUNK, W).transpose(0, 1, 3, 2)


def _from_chunk_rows(a):
    B, nc, W, _ = a.shape
    return a.transpose(0, 1, 3, 2).reshape(B, nc * CHUNK, W)


def local_step(x, c, ctx, c_ctx, tgt, w_ada, b_ada, w_in, lam_re, lam_im, log_dt, b_re, b_im, c_re, c_im, s5_d,
               w_glu, b_glu, conv_w, a_log, dt_bias, norm_w, w_out, ln_g, ln_b):
    B, L, _ = x.shape
    Lc = ctx.shape[1]
    zeros_state = jnp.zeros((B, GDN_HEADS, GDN_HEAD, GDN_HEAD), f32)

    cc = jnp.concatenate([c, c_ctx[None, :], jnp.zeros((8 - B - 1, D_MODEL), f32)], axis=0)
    w_ada_b = w_ada.astype(bf16)
    m = ada_fwd(cc, w_ada_b, b_ada)
    shift, scale, gate = m[:B, :D_MODEL], m[:B, D_MODEL:2 * D_MODEL], m[:B, 2 * D_MODEL:]
    mod = jnp.stack([scale, shift], axis=1)
    mod_c = jnp.broadcast_to(jnp.stack([m[B, D_MODEL:2 * D_MODEL], m[B, :D_MODEL]], axis=0)[None], (B, 2, D_MODEL))

    w_ba = jnp.concatenate([w_in[:, 3072:], jnp.zeros((D_MODEL, LANES - N_GATE), f32)], axis=1)
    ws = [w.astype(bf16) for w in (w_in[:, :512], w_in[:, 512:1024], w_in[:, 1024:2560], w_in[:, 2560:3072], w_ba)]
    u, z_s5, qkv, z_gdn, ba = in_proj_fwd(x, mod, ws, name="in_proj_fwd")
    uc, _, qkvc, _, bac = in_proj_fwd(ctx, mod_c, ws, name="in_proj_fwd_ctx")

    ng = N_DIR * S5_GROUPS
    zoh_in = (lam_re.reshape(ng, S5_STATE), lam_im.reshape(ng, S5_STATE), log_dt.reshape(ng, 1),
              b_re.reshape(ng, S5_STATE * S5_GROUP), b_im.reshape(ng, S5_STATE * S5_GROUP))
    expand = (jnp.arange(S5_STATE * S5_GROUP)[None, :] // S5_GROUP == jnp.arange(S5_STATE)[:, None]).astype(f32)
    ar, ai, bbr, bbi = s5_zoh_fwd(*zoh_in, expand)
    c_re3, c_im3 = c_re.reshape(N_DIR, S5_GROUPS, S5_GROUP, S5_STATE), c_im.reshape(N_DIR, S5_GROUPS, S5_GROUP, S5_STATE)
    s5w, ys, hins, hins_c = [], [], [], []
    for d in range(N_DIR):
        g = slice(d * S5_GROUPS, (d + 1) * S5_GROUPS)
        wd = (_block_diag_in(bbr[g]).astype(bf16), _block_diag_in(bbi[g]).astype(bf16),
              _block_diag_out(c_re3[d]).astype(bf16), _block_diag_out(-c_im3[d]).astype(bf16),
              jnp.stack([ar[g].reshape(-1), ai[g].reshape(-1)], axis=0))
        s5w.append(wd)
        hin_c, hend_c = s5_scan_fwd(uc, *wd, jnp.zeros((B, 2, S5_HALF), f32), d=d, need_y=False, name=f"s5_fwd_ctx{d}")
        y_d, hin, _ = s5_scan_fwd(u, *wd, hend_c, d=d, need_y=True, name=f"s5_fwd{d}")
        ys.append(y_d)
        hins.append(hin)
        hins_c.append(hin_c)
    glu_w = (s5_d.reshape(1, D_S5), w_glu, b_glu.reshape(1, D_S5))
    s5o = s5_glu_fwd(u, ys[0], ys[1], z_s5, *glu_w)

    conv16 = jnp.concatenate([conv_w.reshape(9, 3 * D_GDN), jnp.zeros((CONV_ROWS - 9, 3 * D_GDN), f32)], axis=0)
    act = conv_fwd(qkv, conv16, is_ctx=False, name="conv_fwd")
    act_c = conv_fwd(qkvc, conv16, is_ctx=True, name="conv_fwd_ctx")
    pad8 = jnp.zeros((1, 8), f32)
    alog16 = jnp.concatenate([pad8, a_log.reshape(1, 8)], axis=1)
    dtb16 = jnp.concatenate([pad8, dt_bias.reshape(1, 8)], axis=1)
    bg = gates_fwd(ba, alog16, dtb16, name="gates_fwd")
    bg_c = gates_fwd(bac, alog16, dtb16, name="gates_fwd_ctx")
    bgr, bgr_c = _to_chunk_rows(bg), _to_chunk_rows(bg_c)
    os_, cks, cks_c = [], [], []
    for d in range(N_DIR):
        ck_c, s_c = gdn_fwd(act_c, bg_c, bgr_c, zeros_state, d=d, need_o=False, name=f"gdn_fwd_ctx{d}")
        o_d, ck, _ = gdn_fwd(act, bg, bgr, s_c, d=d, need_o=True, name=f"gdn_fwd{d}")
        os_.append(o_d)
        cks.append(ck)
        cks_c.append(ck_c)
    nw = norm_w.reshape(1, GDN_HEAD)
    gdo = gnorm_fwd(os_[0], os_[1], z_gdn, nw)

    w_out_b = w_out.astype(bf16)
    loss8, ds5o, dgdo, gx_res, dws, dwg, dgate, dlng, dlnb = head_fwd_bwd(
        s5o, gdo, x, tgt, gate[:, None, :], ln_g.reshape(1, D_MODEL), ln_b.reshape(1, D_MODEL),
        w_out_b[:D_S5], w_out_b[D_S5:])
    loss = jnp.sum(loss8[:, 0, 0])
    d_w_out = jnp.concatenate([dws, dwg], axis=0)

    do, dz_gdn, d_norm_w = gnorm_bwd(os_[0], os_[1], z_gdn, nw, dgdo)
    dacts, dacts_c = [], []
    dbg = jnp.zeros_like(bg)
    dbg_c = jnp.zeros_like(bg_c)
    for d in range(N_DIR):
        dact, dbg_d, dbgr_d, ds0 = gdn_bwd(act, bg, bgr, cks[d], do, zeros_state, d=d, name=f"gdn_bwd{d}")
        dact_c, dbgc_d, dbgrc_d, _ = gdn_bwd(act_c, bg_c, bgr_c, cks_c[d], None, ds0, d=d, name=f"gdn_bwd_ctx{d}")
        dacts.append(dact)
        dacts_c.append(dact_c)
        dbg = dbg + dbg_d + _from_chunk_rows(dbgr_d)
        dbg_c = dbg_c + dbgc_d + _from_chunk_rows(dbgrc_d)
    dba, dal, ddt = gates_bwd(ba, alog16, dtb16, dbg, name="gates_bwd")
    dbac, dal_c, ddt_c = gates_bwd(bac, alog16, dtb16, dbg_c, name="gates_bwd_ctx")
    d_a_log = (dal + dal_c)[:, 8:].reshape(1, N_DIR, GDN_HEADS)
    d_dt_bias = (ddt + ddt_c)[:, 8:].reshape(1, N_DIR, GDN_HEADS)
    dqkv, dcw = conv_bwd(qkv, conv16, dacts[0], dacts[1], is_ctx=False, name="conv_bwd")
    dqkvc, dcw_c = conv_bwd(qkvc, conv16, dacts_c[0], dacts_c[1], is_ctx=True, name="conv_bwd_ctx")
    d_conv_w = (jnp.sum(dcw, axis=0) + jnp.sum(dcw_c, axis=0))[:9].reshape(1, 3, 3, 3 * D_GDN)

    du_skip, dy, dz_s5, d_s5_d, d_w_glu, d_b_glu = s5_glu_bwd(u, ys[0], ys[1], z_s5, *glu_w, ds5o)
    du, duc = du_skip, jnp.zeros_like(uc)
    dar, dai, dbbr, dbbi, dcre, dcim = [], [], [], [], [], []
    for d in range(N_DIR):
        du_d, dbre1, dbim1, dct1, dcb1, da1, dh0 = s5_scan_bwd(u, dy, *s5w[d], hins[d],
                                                                jnp.zeros((B, 2, S5_HALF), f32), d=d, name=f"s5_bwd{d}")
        duc_d, dbre2, dbim2, _, _, da2, _ = s5_scan_bwd(uc, None, *s5w[d], hins_c[d], dh0, d=d, name=f"s5_bwd_ctx{d}")
        du, duc = du + du_d, duc + duc_d
        da = da1 + da2
        dar.append(da[0].reshape(S5_GROUPS, S5_STATE))
        dai.append(da[1].reshape(S5_GROUPS, S5_STATE))
        dbbr.append(_block_diag_in_t(dbre1 + dbre2))
        dbbi.append(_block_diag_in_t(dbim1 + dbim2))
        dcre.append(_block_diag_out_t(dct1))
        dcim.append(-_block_diag_out_t(dcb1))
    dlr, dli, dldt, dbre, dbim = s5_zoh_bwd(*zoh_in, expand, jnp.concatenate(dar, 0), jnp.concatenate(dai, 0),
                                            jnp.concatenate(dbbr, 0), jnp.concatenate(dbbi, 0))
    lam_shape = (1, N_DIR, S5_GROUPS, S5_STATE)
    b_shape = (1, N_DIR, S5_GROUPS, S5_STATE, S5_GROUP)
    c_shape = (1, N_DIR, S5_GROUPS, S5_GROUP, S5_STATE)
    d_s5 = (dlr.reshape(lam_shape), dli.reshape(lam_shape), dldt.reshape(1, N_DIR, S5_GROUPS), dbre.reshape(b_shape),
            dbim.reshape(b_shape), jnp.stack(dcre, 0).reshape(c_shape), jnp.stack(dcim, 0).reshape(c_shape))

    padg = lambda a: jnp.concatenate([a, jnp.zeros(a.shape[:2] + (LANES - N_GATE,), f32)], axis=2)
    dws_l = in_proj_bwd(x, mod, (du, dz_s5, dqkv, dz_gdn, padg(dba)), ws, gx_res, name="in_proj_bwd")
    zc = jnp.zeros_like(uc)
    dws_c = in_proj_bwd(ctx, mod_c, (duc, zc, dqkvc, zc, padg(dbac)), ws, None, name="in_proj_bwd_ctx")
    grad_x = dws_l[6]
    dwp = [a + b for a, b in zip(dws_l[:5], dws_c[:5])]
    d_w_in = jnp.concatenate([dwp[0], dwp[1], dwp[2], dwp[3], dwp[4][:, :N_GATE]], axis=1)
    dmod, dmod_c = dws_l[5], jnp.sum(dws_c[5], axis=0)

    dm_rows = jnp.concatenate([dmod[:, 1], dmod[:, 0], dgate[:, 0]], axis=1)
    dm_ctx = jnp.concatenate([dmod_c[1], dmod_c[0], jnp.zeros((D_MODEL,), f32)])[None]
    dm = jnp.concatenate([dm_rows, dm_ctx, jnp.zeros((8 - B - 1, 3 * D_MODEL), f32)], axis=0)
    dcc, d_w_ada, d_b_ada = ada_bwd(cc, w_ada_b, dm)
    grads = (dcc[B], d_w_ada[None], d_b_ada, d_w_in[None], *d_s5, d_s5_d, d_w_glu[None], d_b_glu, d_conv_w, d_a_log,
             d_dt_bias, d_norm_w, d_w_out[None], dlng, dlnb)
    return loss, grad_x, grads


SHARDED = (1, 3, 18, 12, 14)
SHARD_AXIS = (2, 2, 1, 1, 3)
SMALL = tuple(i for i in range(21) if i not in SHARDED)


def _chip_slices(g, axis):
    return [lax.slice_in_dim(g, j * (g.shape[axis] // 4), (j + 1) * (g.shape[axis] // 4), axis=axis) for j in range(4)]


def kernel(x, c, ctx, c_ctx, w_ada, b_ada, w_in, s5_lambda_re, s5_lambda_im, s5_log_dt, s5_b_re, s5_b_im, s5_c_re, s5_c_im, s5_d, w_glu, b_glu, conv_w, gdn_a_log, gdn_dt_bias, gdn_norm_w, w_out, ln_g, ln_b, loss_target, m_c_ctx, m_w_ada, m_b_ada, m_w_in, m_s5_lambda_re, m_s5_lambda_im, m_s5_log_dt, m_s5_b_re, m_s5_b_im, m_s5_c_re, m_s5_c_im, m_s5_d, m_w_glu, m_b_glu, m_conv_w, m_gdn_a_log, m_gdn_dt_bias, m_gdn_norm_w, m_w_out, m_ln_g, m_ln_b, v_c_ctx, v_w_ada, v_b_ada, v_w_in, v_s5_lambda_re, v_s5_lambda_im, v_s5_log_dt, v_s5_b_re, v_s5_b_im, v_s5_c_re, v_s5_c_im, v_s5_d, v_w_glu, v_b_glu, v_conv_w, v_gdn_a_log, v_gdn_dt_bias, v_gdn_norm_w, v_w_out, v_ln_g, v_ln_b):
    weights = [c_ctx, w_ada, b_ada, w_in, s5_lambda_re, s5_lambda_im, s5_log_dt, s5_b_re, s5_b_im, s5_c_re, s5_c_im,
               s5_d, w_glu, b_glu, conv_w, gdn_a_log, gdn_dt_bias, gdn_norm_w, w_out, ln_g, ln_b]
    ms = [m_c_ctx, m_w_ada, m_b_ada, m_w_in, m_s5_lambda_re, m_s5_lambda_im, m_s5_log_dt, m_s5_b_re, m_s5_b_im,
          m_s5_c_re, m_s5_c_im, m_s5_d, m_w_glu, m_b_glu, m_conv_w, m_gdn_a_log, m_gdn_dt_bias, m_gdn_norm_w, m_w_out,
          m_ln_g, m_ln_b]
    vs = [v_c_ctx, v_w_ada, v_b_ada, v_w_in, v_s5_lambda_re, v_s5_lambda_im, v_s5_log_dt, v_s5_b_re, v_s5_b_im,
          v_s5_c_re, v_s5_c_im, v_s5_d, v_w_glu, v_b_glu, v_conv_w, v_gdn_a_log, v_gdn_dt_bias, v_gdn_norm_w, v_w_out,
          v_ln_g, v_ln_b]
    cpos = lax.axis_index("c")
    half = SHARD_TOTAL // 2

    w_shard = _pack([weights[i] for i in SHARDED], SHARD_TOTAL)
    gathered = gather_shards(w_shard.reshape(2, half, LANES)).reshape(4, SHARD_TOTAL, LANES)
    per_chip = [_unpack(gathered[j], SHARD_SHAPES, SHARD_ROWS) for j in range(4)]
    full = {}
    for n, (i, axis) in enumerate(zip(SHARDED, SHARD_AXIS)):
        full[i] = jnp.concatenate([per_chip[j][n].reshape(weights[i].shape) for j in range(4)], axis=axis)
    wl = [full.get(i, weights[i]) for i in range(21)]

    loss, grad_x, grads = local_step(
        x, c, ctx, wl[0], loss_target, wl[1][0], wl[2], wl[3][0], wl[4], wl[5], wl[6], wl[7], wl[8], wl[9], wl[10],
        wl[11], wl[12][0], wl[13], wl[14][0], wl[15], wl[16], wl[17], wl[18][0], wl[19], wl[20])
    grads = [g.reshape(w.shape) for g, w in zip(grads, wl)]
    loss = lax.psum(loss, ("x", "y", "c"))

    small_rows = _pack([grads[i] for i in SMALL], SMALL_TOTAL).reshape(4, SMALL_QUARTER, LANES)
    sliced = [_chip_slices(grads[i], axis) for i, axis in zip(SHARDED, SHARD_AXIS)]
    slab = SHARD_TOTAL + SMALL_QUARTER
    p = jnp.stack([jnp.concatenate([_pack([s[j] for s in sliced], SHARD_TOTAL), small_rows[j]], axis=0)
                   for j in range(4)], axis=0)
    hs = slab // 2
    p4 = p.reshape(4, 2, hs, LANES)
    got = swap_halves(p4)
    own = lax.dynamic_index_in_dim(p4, cpos, axis=1, keepdims=False)
    q = add_rows(own.reshape(4 * hs, LANES), got.reshape(4 * hs, LANES), name="sum_cores").reshape(4, hs, LANES)
    jchip = 2 * lax.axis_index("x") + lax.axis_index("y")
    others = scatter_to_chips(q)
    mine = lax.dynamic_index_in_dim(q, jchip, axis=0, keepdims=False)
    f = add_rows(mine, others[0], name="sum_chips", extra=(others[1], others[2]))
    reduced = join_halves(f).reshape(slab, LANES)
    g_shard = reduced[:SHARD_TOTAL]
    g_small = gather_small(reduced[SHARD_TOTAL:]).reshape(SMALL_TOTAL, LANES)

    m_shard = _pack([ms[i] for i in SHARDED], SHARD_TOTAL)
    v_shard = _pack([vs[i] for i in SHARDED], SHARD_TOTAL)
    d_sh, nm_sh, nv_sh = adamw(w_shard, g_shard, m_shard, v_shard, name="adamw_shards")
    w_small = _pack([weights[i] for i in SMALL], SMALL_TOTAL)
    m_small = _pack([ms[i] for i in SMALL], SMALL_TOTAL)
    v_small = _pack([vs[i] for i in SMALL], SMALL_TOTAL)
    d_sm, nm_sm, nv_sm = adamw(w_small, g_small, m_small, v_small, name="adamw_small")

    def spread(sh_buf, sm_buf):
        sh = _unpack(sh_buf, SHARD_SHAPES, SHARD_ROWS)
        sm = _unpack(sm_buf, SMALL_SHAPES, SMALL_ROWS)
        out = [None] * 21
        for n, i in enumerate(SHARDED):
            out[i] = sh[n].reshape(weights[i].shape)
        for n, i in enumerate(SMALL):
            out[i] = sm[n].reshape(weights[i].shape)
        return out

    return (loss, grad_x, *spread(g_shard, g_small), *spread(d_sh, d_sm), *spread(nm_sh, nm_sm), *spread(nv_sh, nv_sm))
```

```python
import functools

import jax
import jax.numpy as jnp
from jax import lax
from jax.experimental import pallas as pl
from jax.experimental.pallas import tpu as pltpu

f32 = jnp.float32
bf16 = jnp.bfloat16
SDS = jax.ShapeDtypeStruct

D_MODEL = 1024
D_S5 = 512
S5_GROUP = 16
S5_GROUPS = 32
S5_STATE = 64
S5_HALF = S5_GROUPS * S5_STATE
D_GDN = 512
GDN_HEAD = 128
GDN_HEADS = 4
CHUNK = 64
GRID_W = 64
N_DIR = 2
P_IN = 3088
DEEPNORM_ALPHA = 2.0 ** 0.25
LN_EPS = 1e-5
NORM_EPS = 1e-6
ADAM_LR, ADAM_B1, ADAM_B2, ADAM_EPS, ADAM_WD, ADAM_STEP = 0.001, 0.9, 0.999, 1e-08, 0.01, 10

LANES = 128
VMEM_LIMIT = 56 * 1024 * 1024
TOK_TILE = 256
S5_TILE = 256
MESH = pl.DeviceIdType.MESH


def _cparams(n_grid):
    return pltpu.CompilerParams(dimension_semantics=("arbitrary",) * n_grid, vmem_limit_bytes=VMEM_LIMIT)


def _dot(a, b):
    return jnp.dot(a.astype(bf16), b.astype(bf16), preferred_element_type=f32)


def _dot_nt(a, b):
    return lax.dot_general(a.astype(bf16), b.astype(bf16), (((1,), (1,)), ((), ())), preferred_element_type=f32)


def _dot_tn(a, b):
    return lax.dot_general(a.astype(bf16), b.astype(bf16), (((0,), (0,)), ((), ())), preferred_element_type=f32)


def _dot_hi(a, b):
    return jnp.dot(a, b, precision=lax.Precision.HIGHEST, preferred_element_type=f32)


def _dot_h3(a, b):
    return jnp.dot(a, b, precision=lax.Precision.HIGH, preferred_element_type=f32)


@jax.custom_vjp
def _mm(a, b):
    return _dot(a, b)


@jax.custom_vjp
def _mm_nt(a, b):
    return _dot_nt(a, b)


@jax.custom_vjp
def _mm_tn(a, b):
    return _dot_tn(a, b)


_mm.defvjp(lambda a, b: (_dot(a, b), (a, b)), lambda r, g: (_mm_nt(g, r[1]), _mm_tn(r[0], g)))
_mm_nt.defvjp(lambda a, b: (_dot_nt(a, b), (a, b)), lambda r, g: (_mm(g, r[1]), _mm_tn(g, r[0])))
_mm_tn.defvjp(lambda a, b: (_dot_tn(a, b), (a, b)), lambda r, g: (_mm_nt(r[1], g), _mm(r[0], g)))


def _silu(x):
    return x * jax.nn.sigmoid(x)


def _gelu(x):
    return 0.5 * x * (1.0 + lax.erf(x * (2.0 ** -0.5)))


def _resident(shape):
    nd = len(shape)
    return pl.BlockSpec(shape, lambda *_: (0,) * nd, pipeline_mode=pl.Buffered(1))


def _tok(tile, width, nt=None, rev=False):
    if rev:
        return pl.BlockSpec((None, tile, width), lambda b, n: (b, nt - 1 - n, 0))
    return pl.BlockSpec((None, tile, width), lambda b, n: (b, n, 0))


def _per_batch(rows, width):
    return pl.BlockSpec((None, rows, width), lambda b, n: (b, 0, 0))


def _first_step():
    return jnp.logical_and(pl.program_id(0) == 0, pl.program_id(1) == 0)


def ada_fwd(cc, w, b):
    def body(cc_ref, w_ref, b_ref, m_ref):
        m_ref[...] = _dot(_silu(cc_ref[...]), w_ref[...]) + b_ref[...]

    return pl.pallas_call(body, name="ada_fwd", out_shape=SDS((8, 3 * D_MODEL), f32),
                          compiler_params=pltpu.CompilerParams(vmem_limit_bytes=VMEM_LIMIT))(cc, w, b)


def ada_bwd(cc, w, dm):
    def body(cc_ref, w_ref, dm_ref, dcc_ref, dw_ref, db_ref):
        dmv = dm_ref[...]
        s, vjp = jax.vjp(_silu, cc_ref[...])
        dcc_ref[...] = vjp(_dot_nt(dmv, w_ref[...]))[0]
        dw_ref[...] = _dot_tn(s, dmv)
        db_ref[...] = jnp.sum(dmv, axis=0, keepdims=True)

    return pl.pallas_call(
        body, name="ada_bwd",
        out_shape=[SDS((8, D_MODEL), f32), SDS((D_MODEL, 3 * D_MODEL), f32), SDS((1, 3 * D_MODEL), f32)],
        compiler_params=pltpu.CompilerParams(vmem_limit_bytes=VMEM_LIMIT))(cc, w, dm)


IN_WIDTHS = (D_S5, D_S5, 3 * D_GDN, D_GDN, LANES)
N_GATE = 2 * N_DIR * GDN_HEADS


def in_proj_fwd(x, mod, ws, *, name):
    B, L, _ = x.shape
    T = min(TOK_TILE, L)

    def body(x_ref, mod_ref, wu, wz, wq, wg, wb, u_ref, z_ref, q_ref, g_ref, ba_ref):
        h = (x_ref[...] * (1.0 + mod_ref[0:1, :]) + mod_ref[1:2, :]).astype(bf16)
        u_ref[...] = _dot(h, wu[...])
        z_ref[...] = _dot(h, wz[...])
        q_ref[...] = _dot(h, wq[...])
        g_ref[...] = _dot(h, wg[...])
        ba_ref[...] = _dot(h, wb[...])[:, :N_GATE]

    outw = (D_S5, D_S5, 3 * D_GDN, D_GDN, N_GATE)
    return pl.pallas_call(
        body, name=name, grid=(B, L // T),
        in_specs=[_tok(T, D_MODEL), _per_batch(2, D_MODEL)] + [_resident((D_MODEL, w)) for w in IN_WIDTHS],
        out_specs=[_tok(T, w) for w in outw],
        out_shape=[SDS((B, L, w), f32) for w in outw],
        compiler_params=_cparams(2),
    )(x, mod, *ws)


def in_proj_bwd(x, mod, ds, ws, gx_res, *, name):
    B, L, _ = x.shape
    T = min(TOK_TILE, L)
    with_dx = gx_res is not None

    def body(*refs):
        x_ref, mod_ref = refs[0], refs[1]
        d_refs = refs[2:7]
        w_refs = refs[7:12]
        k = 12
        if with_dx:
            gx_ref = refs[k]
            k += 1
        dw_refs = refs[k:k + 5]
        dmod_ref = refs[k + 5]
        if with_dx:
            dx_ref = refs[k + 6]
        n = pl.program_id(1)

        @pl.when(_first_step())
        def _():
            for r in dw_refs:
                r[...] = jnp.zeros_like(r)

        @pl.when(n == 0)
        def _():
            dmod_ref[...] = jnp.zeros_like(dmod_ref)

        xv = x_ref[...]
        scale1 = 1.0 + mod_ref[0:1, :]
        h = (xv * scale1 + mod_ref[1:2, :]).astype(bf16)
        dh = jnp.zeros((T, D_MODEL), f32)
        for d_ref, w_ref, dw_ref in zip(d_refs, w_refs, dw_refs):
            dv = d_ref[...].astype(bf16)
            dh = dh + _dot_nt(dv, w_ref[...])
            dw_ref[...] += _dot_tn(h, dv)
        dmod_ref[0:1, :] += jnp.sum(dh * xv, axis=0, keepdims=True)
        dmod_ref[1:2, :] += jnp.sum(dh, axis=0, keepdims=True)
        if with_dx:
            dx_ref[...] = gx_ref[...] + dh * scale1

    in_specs = ([_tok(T, D_MODEL), _per_batch(2, D_MODEL)] + [_tok(T, w) for w in IN_WIDTHS]
                + [_resident((D_MODEL, w)) for w in IN_WIDTHS])
    args = [x, mod, *ds, *ws]
    out_specs = [_resident((D_MODEL, w)) for w in IN_WIDTHS] + [_per_batch(2, D_MODEL)]
    out_shape = [SDS((D_MODEL, w), f32) for w in IN_WIDTHS] + [SDS((B, 2, D_MODEL), f32)]
    if with_dx:
        in_specs.append(_tok(T, D_MODEL))
        args.append(gx_res)
        out_specs.append(_tok(T, D_MODEL))
        out_shape.append(SDS((B, L, D_MODEL), f32))
    return pl.pallas_call(body, name=name, grid=(B, L // T), in_specs=in_specs, out_specs=out_specs,
                          out_shape=out_shape, compiler_params=_cparams(2))(*args)


def _s5_zoh(lr, li, ldt, bre, bim, expand):
    dt = jnp.exp(ldt)
    zr, zi = lr * dt, li * dt
    e = jnp.exp(zr)
    ar, ai = e * jnp.cos(zi), e * jnp.sin(zi)
    den = lr * lr + li * li
    czr = ((ar - 1.0) * lr + ai * li) / den
    czi = (ai * lr - (ar - 1.0) * li) / den
    czr_e, czi_e = _dot_hi(czr, expand), _dot_hi(czi, expand)
    return ar, ai, czr_e * bre - czi_e * bim, czr_e * bim + czi_e * bre


_ZOH_OUT = [(N_DIR * S5_GROUPS, S5_STATE)] * 2 + [(N_DIR * S5_GROUPS, S5_STATE * S5_GROUP)] * 2


def s5_zoh_fwd(lr, li, ldt, bre, bim, expand):
    def body(lr_ref, li_ref, ldt_ref, bre_ref, bim_ref, e_ref, ar_ref, ai_ref, bbr_ref, bbi_ref):
        ar, ai, bbr, bbi = _s5_zoh(lr_ref[...], li_ref[...], ldt_ref[...], bre_ref[...], bim_ref[...], e_ref[...])
        ar_ref[...], ai_ref[...], bbr_ref[...], bbi_ref[...] = ar, ai, bbr, bbi

    return pl.pallas_call(body, name="s5_zoh_fwd", out_shape=[SDS(s, f32) for s in _ZOH_OUT])(
        lr, li, ldt, bre, bim, expand)


def s5_zoh_bwd(lr, li, ldt, bre, bim, expand, dar, dai, dbbr, dbbi):
    def body(lr_ref, li_ref, ldt_ref, bre_ref, bim_ref, e_ref, dar_ref, dai_ref, dbbr_ref, dbbi_ref,
             dlr_ref, dli_ref, dldt_ref, dbre_ref, dbim_ref):
        ev = e_ref[...]
        _, vjp = jax.vjp(lambda a, b, c, d, e: _s5_zoh(a, b, c, d, e, ev),
                         lr_ref[...], li_ref[...], ldt_ref[...], bre_ref[...], bim_ref[...])
        outs = vjp((dar_ref[...], dai_ref[...], dbbr_ref[...], dbbi_ref[...]))
        dlr_ref[...], dli_ref[...], dldt_ref[...], dbre_ref[...], dbim_ref[...] = outs

    shapes = [lr.shape, li.shape, ldt.shape, bre.shape, bim.shape]
    return pl.pallas_call(body, name="s5_zoh_bwd", out_shape=[SDS(s, f32) for s in shapes])(
        lr, li, ldt, bre, bim, expand, dar, dai, dbbr, dbbi)


def _scan_rows(T, rev, ar, ai, hr0, hi0, r_ref, i_ref, off):
    def step(i, carry):
        hr, hi = carry
        t = off + ((T - 1 - i) if rev else i)
        nr = ar * hr - ai * hi + r_ref[pl.ds(t, 1), :]
        ni = ar * hi + ai * hr + i_ref[pl.ds(t, 1), :]
        r_ref[pl.ds(t, 1), :] = nr
        i_ref[pl.ds(t, 1), :] = ni
        return nr, ni

    return lax.fori_loop(0, T, step, (hr0, hi0), unroll=2)


def s5_scan_fwd(u, bre, bim, ctop, cbot, arow, h0, *, d, need_y, name):
    B, L, _ = u.shape
    T = min(S5_TILE, L)
    nt = L // T
    rev = d == 1

    def body(u_ref, bre_ref, bim_ref, ct_ref, cb_ref, a_ref, h0_ref, *rest):
        if need_y:
            y_ref, hin_ref, hend_ref, hr_scr, hi_scr, h_scr = rest
        else:
            hin_ref, hend_ref, hr_scr, hi_scr, h_scr = rest
        n = pl.program_id(1)

        @pl.when(n == 0)
        def _():
            h_scr[...] = h0_ref[...]

        hin_ref[...] = h_scr[...]
        uv = u_ref[...].astype(bf16)
        hr_scr[...] = _dot(uv, bre_ref[...])
        hi_scr[...] = _dot(uv, bim_ref[...])
        hr, hi = _scan_rows(T, rev, a_ref[0:1, :], a_ref[1:2, :], h_scr[0:1, :], h_scr[1:2, :], hr_scr, hi_scr, 0)
        h_scr[0:1, :] = hr
        h_scr[1:2, :] = hi
        if need_y:
            y_ref[...] = _dot(hr_scr[...], ct_ref[...]) + _dot(hi_scr[...], cb_ref[...])

        @pl.when(n == nt - 1)
        def _():
            hend_ref[...] = h_scr[...]

    state = _per_batch(2, S5_HALF)
    hin_spec = pl.BlockSpec((None, None, 2, S5_HALF), (lambda b, n: (b, nt - 1 - n, 0, 0)) if rev else (lambda b, n: (b, n, 0, 0)))
    out_specs = [hin_spec, state]
    out_shape = [SDS((B, nt, 2, S5_HALF), f32), SDS((B, 2, S5_HALF), f32)]
    if need_y:
        out_specs.insert(0, _tok(T, D_S5, nt, rev))
        out_shape.insert(0, SDS((B, L, D_S5), f32))
    return pl.pallas_call(
        body, name=name, grid=(B, nt),
        in_specs=[_tok(T, D_S5, nt, rev), _resident((D_S5, S5_HALF)), _resident((D_S5, S5_HALF)),
                  _resident((S5_HALF, D_S5)), _resident((S5_HALF, D_S5)), _resident((2, S5_HALF)), state],
        out_specs=out_specs, out_shape=out_shape,
        scratch_shapes=[pltpu.VMEM((T, S5_HALF), f32), pltpu.VMEM((T, S5_HALF), f32), pltpu.VMEM((2, S5_HALF), f32)],
        compiler_params=_cparams(2),
    )(u, bre, bim, ctop, cbot, arow, h0)


def s5_scan_bwd(u, dy, bre, bim, ctop, cbot, arow, hin, dhend, *, d, name):
    B, L, _ = u.shape
    T = min(S5_TILE, L)
    nt = L // T
    rev = d == 1
    has_dy = dy is not None
    PAD = 8

    def body(*refs):
        u_ref = refs[0]
        k = 1
        if has_dy:
            dy_ref = refs[1]
            k = 2
        bre_ref, bim_ref, ct_ref, cb_ref, a_ref, hin_ref, dhend_ref = refs[k:k + 7]
        du_ref, dbre_ref, dbim_ref, dct_ref, dcb_ref, da_ref, dh0_ref = refs[k + 7:k + 14]
        hr_scr, hi_scr, gr_scr, gi_scr, p_scr = refs[k + 14:]
        n = pl.program_id(1)

        @pl.when(_first_step())
        def _():
            for r in (dbre_ref, dbim_ref, dct_ref, dcb_ref, da_ref):
                r[...] = jnp.zeros_like(r)

        @pl.when(n == 0)
        def _():
            p_scr[...] = dhend_ref[...]

        ar, ai = a_ref[0:1, :], a_ref[1:2, :]
        uv = u_ref[...].astype(bf16)
        hr_scr[PAD:PAD + T, :] = _dot(uv, bre_ref[...])
        hi_scr[PAD:PAD + T, :] = _dot(uv, bim_ref[...])
        prev_row = PAD + T if rev else PAD - 1
        hr_scr[prev_row:prev_row + 1, :] = hin_ref[0:1, :]
        hi_scr[prev_row:prev_row + 1, :] = hin_ref[1:2, :]
        _scan_rows(T, rev, ar, ai, hin_ref[0:1, :], hin_ref[1:2, :], hr_scr, hi_scr, PAD)
        if has_dy:
            dyv = dy_ref[...].astype(bf16)
            gr_scr[...] = _dot_nt(dyv, ct_ref[...])
            gi_scr[...] = _dot_nt(dyv, cb_ref[...])
            dct_ref[...] += _dot_tn(hr_scr[PAD:PAD + T, :], dyv)
            dcb_ref[...] += _dot_tn(hi_scr[PAD:PAD + T, :], dyv)
        else:
            gr_scr[...] = jnp.zeros_like(gr_scr)
            gi_scr[...] = jnp.zeros_like(gi_scr)

        def step(i, carry):
            pr, pi, dar, dai = carry
            t = i if rev else T - 1 - i
            gr = gr_scr[pl.ds(t, 1), :] + pr
            gi = gi_scr[pl.ds(t, 1), :] + pi
            gr_scr[pl.ds(t, 1), :] = gr
            gi_scr[pl.ds(t, 1), :] = gi
            tp = PAD + t + (1 if rev else -1)
            hpr = hr_scr[pl.ds(tp, 1), :]
            hpi = hi_scr[pl.ds(tp, 1), :]
            dar = dar + hpr * gr + hpi * gi
            dai = dai + hpr * gi - hpi * gr
            return ar * gr + ai * gi, ar * gi - ai * gr, dar, dai

        zero = jnp.zeros((1, S5_HALF), f32)
        pr, pi, dar, dai = lax.fori_loop(0, T, step, (p_scr[0:1, :], p_scr[1:2, :], zero, zero), unroll=2)
        p_scr[0:1, :] = pr
        p_scr[1:2, :] = pi
        da_ref[0:1, :] += dar
        da_ref[1:2, :] += dai
        gr_all = gr_scr[...].astype(bf16)
        gi_all = gi_scr[...].astype(bf16)
        du_ref[...] = _dot_nt(gr_all, bre_ref[...]) + _dot_nt(gi_all, bim_ref[...])
        dbre_ref[...] += _dot_tn(uv, gr_all)
        dbim_ref[...] += _dot_tn(uv, gi_all)

        @pl.when(n == nt - 1)
        def _():
            dh0_ref[...] = p_scr[...]

    brev = not rev
    state = _per_batch(2, S5_HALF)
    hin_spec = pl.BlockSpec((None, None, 2, S5_HALF), (lambda b, n: (b, nt - 1 - n, 0, 0)) if brev else (lambda b, n: (b, n, 0, 0)))
    wspecs = [_resident((D_S5, S5_HALF)), _resident((D_S5, S5_HALF)), _resident((S5_HALF, D_S5)),
              _resident((S5_HALF, D_S5))]
    in_specs = [_tok(T, D_S5, nt, brev)] + ([_tok(T, D_S5, nt, brev)] if has_dy else []) + wspecs + [
        _resident((2, S5_HALF)), hin_spec, state]
    args = [u] + ([dy] if has_dy else []) + [bre, bim, ctop, cbot, arow, hin, dhend]
    return pl.pallas_call(
        body, name=name, grid=(B, nt), in_specs=in_specs,
        out_specs=[_tok(T, D_S5, nt, brev)] + wspecs + [_resident((2, S5_HALF)), state],
        out_shape=[SDS((B, L, D_S5), f32), SDS((D_S5, S5_HALF), f32), SDS((D_S5, S5_HALF), f32),
                   SDS((S5_HALF, D_S5), f32), SDS((S5_HALF, D_S5), f32), SDS((2, S5_HALF), f32),
                   SDS((B, 2, S5_HALF), f32)],
        scratch_shapes=[pltpu.VMEM((T + 2 * PAD, S5_HALF), f32), pltpu.VMEM((T + 2 * PAD, S5_HALF), f32),
                        pltpu.VMEM((T, S5_HALF), f32), pltpu.VMEM((T, S5_HALF), f32), pltpu.VMEM((2, S5_HALF), f32)],
        compiler_params=_cparams(2),
    )(*args)


def _glu_fn(u, y0, y1, z, dsk, wg, bg):
    g = _gelu(dsk * u + y0 + y1)
    return g * jax.nn.sigmoid(_mm(g, wg) + bg) * _silu(z)


def s5_glu_fwd(u, y0, y1, z, dsk, wg, bg):
    B, L, _ = u.shape
    T = min(TOK_TILE, L)

    def body(u_ref, y0_ref, y1_ref, z_ref, dsk_ref, wg_ref, bg_ref, o_ref):
        o_ref[...] = _glu_fn(u_ref[...], y0_ref[...], y1_ref[...], z_ref[...], dsk_ref[...], wg_ref[...], bg_ref[...])

    t = _tok(T, D_S5)
    return pl.pallas_call(
        body, name="s5_glu_fwd", grid=(B, L // T),
        in_specs=[t, t, t, t, _resident((1, D_S5)), _resident((D_S5, D_S5)), _resident((1, D_S5))],
        out_specs=t, out_shape=SDS((B, L, D_S5), f32), compiler_params=_cparams(2),
    )(u, y0, y1, z, dsk, wg, bg)


def s5_glu_bwd(u, y0, y1, z, dsk, wg, bg, dout):
    B, L, _ = u.shape
    T = min(TOK_TILE, L)

    def body(u_ref, y0_ref, y1_ref, z_ref, dsk_ref, wg_ref, bg_ref, do_ref, du_ref, dy_ref, dz_ref,
             ddsk_ref, dwg_ref, dbg_ref):
        @pl.when(_first_step())
        def _():
            for r in (ddsk_ref, dwg_ref, dbg_ref):
                r[...] = jnp.zeros_like(r)

        _, vjp = jax.vjp(_glu_fn, u_ref[...], y0_ref[...], y1_ref[...], z_ref[...], dsk_ref[...], wg_ref[...],
                         bg_ref[...])
        du, dy, _, dz, ddsk, dwg, dbg = vjp(do_ref[...])
        du_ref[...], dy_ref[...], dz_ref[...] = du, dy, dz
        ddsk_ref[...] += ddsk
        dwg_ref[...] += dwg
        dbg_ref[...] += dbg

    t = _tok(T, D_S5)
    small = [_resident((1, D_S5)), _resident((D_S5, D_S5)), _resident((1, D_S5))]
    return pl.pallas_call(
        body, name="s5_glu_bwd", grid=(B, L // T),
        in_specs=[t, t, t, t] + small + [t], out_specs=[t, t, t] + small,
        out_shape=[SDS((B, L, D_S5), f32)] * 3 + [SDS((1, D_S5), f32), SDS((D_S5, D_S5), f32), SDS((1, D_S5), f32)],
        compiler_params=_cparams(2),
    )(u, y0, y1, z, dsk, wg, bg, dout)


CONV_ROWS = 16


def _conv_taps(L, is_ctx):
    t = lax.broadcasted_iota(jnp.int32, (L, 1), 0)
    taps = []
    for di in ((1,) if is_ctx else (0, 1, 2)):
        for dj in (0, 1, 2):
            s = (0 if is_ctx else GRID_W * (di - 1)) + (dj - 1)
            if is_ctx:
                ok = jnp.logical_and(t + s >= 0, t + s < L)
            else:
                col = jnp.bitwise_and(t, GRID_W - 1) + (dj - 1)
                row = t + GRID_W * (di - 1)
                ok = jnp.logical_and(jnp.logical_and(col >= 0, col < GRID_W), jnp.logical_and(row >= 0, row < L))
            taps.append((di * 3 + dj, s, ok.astype(f32)))
    return taps


def _shift(x, s):
    L = x.shape[0]
    k = (-s) % L
    return x if k == 0 else pltpu.roll(x, k, axis=0)


def _qk_post(pre, is_norm, scale):
    s = _silu(pre)
    nrm = lax.rsqrt(jnp.sum(s * s, axis=-1, keepdims=True) + NORM_EPS)
    return s * jnp.where(is_norm, nrm * scale, 1.0)


def _conv_kind():
    ct = pl.program_id(1)
    return ct < 2 * GDN_HEADS, jnp.where(ct < GDN_HEADS, GDN_HEAD ** -0.5, 1.0).astype(f32)


def _conv_pre(xv, w_ref, taps):
    pre = jnp.zeros_like(xv)
    for r, s, m in taps:
        pre = pre + w_ref[r:r + 1, :] * (m * _shift(xv, s))
    return pre


def conv_fwd(qkv, w16, *, is_ctx, name):
    B, L, C = qkv.shape
    spec = pl.BlockSpec((None, L, GDN_HEAD), lambda b, ct: (b, 0, ct))
    wspec = pl.BlockSpec((CONV_ROWS, GDN_HEAD), lambda b, ct: (0, ct))

    def body(x_ref, w_ref, o_ref):
        is_norm, scale = _conv_kind()
        o_ref[...] = _qk_post(_conv_pre(x_ref[...], w_ref, _conv_taps(L, is_ctx)), is_norm, scale)

    return pl.pallas_call(body, name=name, grid=(B, C // GDN_HEAD), in_specs=[spec, wspec], out_specs=spec,
                          out_shape=SDS((B, L, C), f32), compiler_params=_cparams(2))(qkv, w16)


def conv_bwd(qkv, w16, da0, da1, *, is_ctx, name):
    B, L, C = qkv.shape
    spec = pl.BlockSpec((None, L, GDN_HEAD), lambda b, ct: (b, 0, ct))
    wspec = pl.BlockSpec((CONV_ROWS, GDN_HEAD), lambda b, ct: (0, ct))
    dwspec = pl.BlockSpec((None, CONV_ROWS, GDN_HEAD), lambda b, ct: (b, 0, ct))

    def body(x_ref, w_ref, d0_ref, d1_ref, dx_ref, dw_ref):
        is_norm, scale = _conv_kind()
        taps = _conv_taps(L, is_ctx)
        xv = x_ref[...]
        _, vjp = jax.vjp(lambda p: _qk_post(p, is_norm, scale), _conv_pre(xv, w_ref, taps))
        dpre = vjp(d0_ref[...] + d1_ref[...])[0]
        dx = jnp.zeros_like(xv)
        dw_ref[...] = jnp.zeros_like(dw_ref)
        for r, s, m in taps:
            md = m * dpre
            dx = dx + _shift(w_ref[r:r + 1, :] * md, -s)
            dw_ref[r:r + 1, :] = jnp.sum(md * _shift(xv, s), axis=0, keepdims=True)
        dx_ref[...] = dx

    return pl.pallas_call(body, name=name, grid=(B, C // GDN_HEAD), in_specs=[spec, wspec, spec, spec],
                          out_specs=[spec, dwspec], out_shape=[SDS((B, L, C), f32), SDS((B, CONV_ROWS, C), f32)],
                          compiler_params=_cparams(2))(qkv, w16, da0, da1)


def _gates_fn(ba, alog, dtb):
    T = ba.shape[0]
    lane = lax.broadcasted_iota(jnp.int32, ba.shape, 1)
    ii = lax.broadcasted_iota(jnp.int32, (T, T), 0)
    jj = lax.broadcasted_iota(jnp.int32, (T, T), 1)
    same = jnp.right_shift(ii, 6) == jnp.right_shift(jj, 6)
    lmat = jnp.logical_and(same, ii >= jj).astype(f32)
    umat = jnp.logical_and(same, ii <= jj).astype(f32)
    g = jnp.where(lane >= 8, -jnp.exp(alog) * jax.nn.softplus(ba + dtb), 0.0)
    gc = jnp.where(lane >= 12, _dot_hi(umat, g), _dot_hi(lmat, g))
    return jnp.where(lane < 8, jax.nn.sigmoid(ba), gc)


def gates_fwd(ba, alog, dtb, *, name):
    B, L, _ = ba.shape
    T = min(TOK_TILE, L)
    t = _tok(T, N_GATE)

    def body(ba_ref, al_ref, dt_ref, o_ref):
        o_ref[...] = _gates_fn(ba_ref[...], al_ref[...], dt_ref[...])

    return pl.pallas_call(body, name=name, grid=(B, L // T),
                          in_specs=[t, _resident((1, N_GATE)), _resident((1, N_GATE))], out_specs=t,
                          out_shape=SDS((B, L, N_GATE), f32), compiler_params=_cparams(2))(ba, alog, dtb)


def gates_bwd(ba, alog, dtb, dbg, *, name):
    B, L, _ = ba.shape
    T = min(TOK_TILE, L)
    t = _tok(T, N_GATE)
    small = _resident((1, N_GATE))

    def body(ba_ref, al_ref, dt_ref, d_ref, dba_ref, dal_ref, ddt_ref):
        @pl.when(_first_step())
        def _():
            dal_ref[...] = jnp.zeros_like(dal_ref)
            ddt_ref[...] = jnp.zeros_like(ddt_ref)

        _, vjp = jax.vjp(_gates_fn, ba_ref[...], al_ref[...], dt_ref[...])
        dba, dal, ddt = vjp(d_ref[...])
        dba_ref[...] = dba
        dal_ref[...] += dal
        ddt_ref[...] += ddt

    return pl.pallas_call(body, name=name, grid=(B, L // T), in_specs=[t, small, small, t],
                          out_specs=[t, small, small],
                          out_shape=[SDS((B, L, N_GATE), f32), SDS((1, N_GATE), f32), SDS((1, N_GATE), f32)],
                          compiler_params=_cparams(2))(ba, alog, dtb, dbg)


@jax.custom_vjp
def _inv_unit_tri(mats):
    n = mats[0].shape[0]
    eye = (lax.broadcasted_iota(jnp.int32, (n, n), 0) == lax.broadcasted_iota(jnp.int32, (n, n), 1)).astype(f32)
    xs = [eye - a for a in mats]
    ps = [_dot_h3(a, a) for a in mats]
    k = 2
    while k < n:
        xs = [x + _dot_h3(x, p) for x, p in zip(xs, ps)]
        k *= 2
        if k < n:
            ps = [_dot_h3(p, p) for p in ps]
    return tuple(xs)


def _inv_unit_tri_fwd(mats):
    xs = _inv_unit_tri(mats)
    return xs, xs


def _inv_unit_tri_bwd(xs, dxs):
    xts = [x.T for x in xs]
    ts = [_dot_h3(xt, dx) for xt, dx in zip(xts, dxs)]
    return (tuple(-_dot_h3(t, xt) for t, xt in zip(ts, xts)),)


_inv_unit_tri.defvjp(_inv_unit_tri_fwd, _inv_unit_tri_bwd)


def _gdn_chunk(heads, *, rev):
    n = heads[0][0].shape[0]
    ii = lax.broadcasted_iota(jnp.int32, (n, n), 0)
    jj = lax.broadcasted_iota(jnp.int32, (n, n), 1)
    lower = (ii <= jj) if rev else (ii >= jj)
    strict = (ii < jj) if rev else (ii > jj)
    last = 0 if rev else n - 1
    row = lax.broadcasted_iota(jnp.int32, (n, 1), 0)
    H = range(len(heads))
    q, k, v, beta, gc, gr, s = (list(t) for t in zip(*heads))
    decay = [jnp.where(lower, jnp.exp(jnp.where(lower, gc[h] - gr[h], 0.0)), 0.0) for h in H]
    kk = [_mm_nt(k[h], k[h]) for h in H]
    qk = [_mm_nt(q[h], k[h]) * decay[h] for h in H]
    qs = [_mm(q[h], s[h]) for h in H]
    a_mat = tuple(jnp.where(strict, beta[h] * kk[h] * decay[h], 0.0) for h in H)
    gamma = [jnp.exp(gc[h]) for h in H]
    g_last = [jnp.sum(jnp.where(row == last, gc[h], 0.0), axis=0, keepdims=True) for h in H]
    tinv = _inv_unit_tri(a_mat)
    u0 = [_dot_h3(tinv[h], beta[h] * v[h]) for h in H]
    w = [_dot_h3(tinv[h], (beta[h] * gamma[h]) * k[h]) for h in H]
    k_out = [k[h] * jnp.exp(g_last[h] - gc[h]) for h in H]
    u = [u0[h] - _mm(w[h], s[h]) for h in H]
    o = [gamma[h] * qs[h] + _mm(qk[h], u[h]) for h in H]
    s_new = [jnp.exp(g_last[h]) * s[h] + _mm_tn(k_out[h], u[h]) for h in H]
    return tuple((o[h], s_new[h]) for h in H)


def _gdn_specs(B, nc, rev):
    def cidx(n):
        return (nc - 1 - n) if rev else n
    tok = lambda width: pl.BlockSpec((B, CHUNK, width), lambda n: (0, cidx(n), 0))
    rowspec = pl.BlockSpec((B, None, N_GATE, CHUNK), lambda n: (0, cidx(n), 0, 0))
    st = pl.BlockSpec((B, GDN_HEADS, GDN_HEAD, GDN_HEAD), lambda n: (0, 0, 0, 0))
    ck = pl.BlockSpec((B, None, GDN_HEADS, GDN_HEAD, GDN_HEAD), lambda n: (0, cidx(n), 0, 0, 0))
    return tok, rowspec, st, ck


def _gdn_head_args(qkv_ref, bg_ref, bgr_ref, b, d, h):
    col = d * GDN_HEADS + h
    q = qkv_ref[b, :, h * GDN_HEAD:(h + 1) * GDN_HEAD]
    k = qkv_ref[b, :, D_GDN + h * GDN_HEAD:D_GDN + (h + 1) * GDN_HEAD]
    v = qkv_ref[b, :, 2 * D_GDN + h * GDN_HEAD:2 * D_GDN + (h + 1) * GDN_HEAD]
    bgv = bg_ref[b]
    return q, k, v, bgv[:, col:col + 1], bgv[:, 8 + col:9 + col], bgr_ref[b][8 + col:9 + col, :]


def gdn_fwd(qkv, bg, bgr, s0, *, d, need_o, name):
    B, L, _ = qkv.shape
    nc = L // CHUNK
    rev = d == 1
    tok, rowspec, st, ck = _gdn_specs(B, nc, rev)
    bh = [(b, h) for b in range(B) for h in range(GDN_HEADS)]

    def body(qkv_ref, bg_ref, bgr_ref, s0_ref, *rest):
        if need_o:
            o_ref, ck_ref, sf_ref, s_scr = rest
        else:
            ck_ref, sf_ref, s_scr = rest
        n = pl.program_id(0)

        @pl.when(n == 0)
        def _():
            s_scr[...] = s0_ref[...]

        ck_ref[...] = s_scr[...]
        heads = tuple(_gdn_head_args(qkv_ref, bg_ref, bgr_ref, b, d, h) + (s_scr[b, h],) for b, h in bh)
        for (b, h), (o, s_new) in zip(bh, _gdn_chunk(heads, rev=rev)):
            if need_o:
                o_ref[b, :, h * GDN_HEAD:(h + 1) * GDN_HEAD] = o
            s_scr[b, h] = s_new

        @pl.when(n == nc - 1)
        def _():
            sf_ref[...] = s_scr[...]

    out_specs = [ck, st]
    out_shape = [SDS((B, nc, GDN_HEADS, GDN_HEAD, GDN_HEAD), f32), SDS((B, GDN_HEADS, GDN_HEAD, GDN_HEAD), f32)]
    if need_o:
        out_specs.insert(0, tok(D_GDN))
        out_shape.insert(0, SDS((B, L, D_GDN), f32))
    return pl.pallas_call(
        body, name=name, grid=(nc,), in_specs=[tok(3 * D_GDN), tok(N_GATE), rowspec, st],
        out_specs=out_specs, out_shape=out_shape,
        scratch_shapes=[pltpu.VMEM((B, GDN_HEADS, GDN_HEAD, GDN_HEAD), f32)], compiler_params=_cparams(1),
    )(qkv, bg, bgr, s0)


def gdn_bwd(qkv, bg, bgr, ck, do, dsf, *, d, name):
    B, L, _ = qkv.shape
    nc = L // CHUNK
    rev = d == 1
    has_do = do is not None
    tok, rowspec, st, ckspec = _gdn_specs(B, nc, not rev)
    bh = [(b, h) for b in range(B) for h in range(GDN_HEADS)]

    def body(*refs):
        qkv_ref, bg_ref, bgr_ref, ck_ref = refs[:4]
        k = 4
        if has_do:
            do_ref = refs[4]
            k = 5
        dsf_ref, dqkv_ref, dbg_ref, dbgr_ref, ds0_ref, ds_scr = refs[k:]
        n = pl.program_id(0)

        @pl.when(n == 0)
        def _():
            ds_scr[...] = dsf_ref[...]

        lane = lax.broadcasted_iota(jnp.int32, (CHUNK, N_GATE), 1)
        sub = lax.broadcasted_iota(jnp.int32, (N_GATE, CHUNK), 0)
        heads = tuple(_gdn_head_args(qkv_ref, bg_ref, bgr_ref, b, d, h) + (ck_ref[b, h],) for b, h in bh)
        _, vjp = jax.vjp(functools.partial(_gdn_chunk, rev=rev), heads)
        cts = tuple(((do_ref[b, :, h * GDN_HEAD:(h + 1) * GDN_HEAD] if has_do else jnp.zeros((CHUNK, GDN_HEAD), f32)),
                     ds_scr[b, h]) for b, h in bh)
        (dheads,) = vjp(cts)
        dbg_acc = [jnp.zeros((CHUNK, N_GATE), f32) for _ in range(B)]
        dbgr_acc = [jnp.zeros((N_GATE, CHUNK), f32) for _ in range(B)]
        for (b, h), (dq, dk, dv, db, dgc, dgr, ds) in zip(bh, dheads):
            col = d * GDN_HEADS + h
            dqkv_ref[b, :, h * GDN_HEAD:(h + 1) * GDN_HEAD] = dq
            dqkv_ref[b, :, D_GDN + h * GDN_HEAD:D_GDN + (h + 1) * GDN_HEAD] = dk
            dqkv_ref[b, :, 2 * D_GDN + h * GDN_HEAD:2 * D_GDN + (h + 1) * GDN_HEAD] = dv
            dbg_acc[b] = dbg_acc[b] + jnp.where(lane == col, db, 0.0) + jnp.where(lane == 8 + col, dgc, 0.0)
            dbgr_acc[b] = dbgr_acc[b] + jnp.where(sub == 8 + col, dgr, 0.0)
            ds_scr[b, h] = ds
        for b in range(B):
            dbg_ref[b] = dbg_acc[b]
            dbgr_ref[b] = dbgr_acc[b]

        @pl.when(n == nc - 1)
        def _():
            ds0_ref[...] = ds_scr[...]

    in_specs = [tok(3 * D_GDN), tok(N_GATE), rowspec, ckspec] + ([tok(D_GDN)] if has_do else []) + [st]
    args = [qkv, bg, bgr, ck] + ([do] if has_do else []) + [dsf]
    return pl.pallas_call(
        body, name=name, grid=(nc,), in_specs=in_specs,
        out_specs=[tok(3 * D_GDN), tok(N_GATE), rowspec, st],
        out_shape=[SDS((B, L, 3 * D_GDN), f32), SDS((B, L, N_GATE), f32), SDS((B, nc, N_GATE, CHUNK), f32),
                   SDS((B, GDN_HEADS, GDN_HEAD, GDN_HEAD), f32)],
        scratch_shapes=[pltpu.VMEM((B, GDN_HEADS, GDN_HEAD, GDN_HEAD), f32)], compiler_params=_cparams(1),
    )(*args)


def _gnorm_fn(o0, o1, z, w):
    o = o0 + o1
    return o * lax.rsqrt(jnp.mean(o * o, axis=-1, keepdims=True) + NORM_EPS) * w * _silu(z)


def gnorm_fwd(o0, o1, z, w):
    B, L, _ = o0.shape
    T = min(TOK_TILE, L)
    t = _tok(T, D_GDN)

    def body(o0_ref, o1_ref, z_ref, w_ref, out_ref):
        for h in range(GDN_HEADS):
            sl = slice(h * GDN_HEAD, (h + 1) * GDN_HEAD)
            out_ref[:, sl] = _gnorm_fn(o0_ref[:, sl], o1_ref[:, sl], z_ref[:, sl], w_ref[...])

    return pl.pallas_call(body, name="gnorm_fwd", grid=(B, L // T), in_specs=[t, t, t, _resident((1, GDN_HEAD))],
                          out_specs=t, out_shape=SDS((B, L, D_GDN), f32), compiler_params=_cparams(2))(o0, o1, z, w)


def gnorm_bwd(o0, o1, z, w, dout):
    B, L, _ = o0.shape
    T = min(TOK_TILE, L)
    t = _tok(T, D_GDN)

    def body(o0_ref, o1_ref, z_ref, w_ref, d_ref, do_ref, dz_ref, dw_ref):
        @pl.when(_first_step())
        def _():
            dw_ref[...] = jnp.zeros_like(dw_ref)

        for h in range(GDN_HEADS):
            sl = slice(h * GDN_HEAD, (h + 1) * GDN_HEAD)
            _, vjp = jax.vjp(_gnorm_fn, o0_ref[:, sl], o1_ref[:, sl], z_ref[:, sl], w_ref[...])
            do, _, dz, dw = vjp(d_ref[:, sl])
            do_ref[:, sl] = do
            dz_ref[:, sl] = dz
            dw_ref[...] += dw

    return pl.pallas_call(body, name="gnorm_bwd", grid=(B, L // T),
                          in_specs=[t, t, t, _resident((1, GDN_HEAD)), t], out_specs=[t, t, _resident((1, GDN_HEAD))],
                          out_shape=[SDS((B, L, D_GDN), f32), SDS((B, L, D_GDN), f32), SDS((1, GDN_HEAD), f32)],
                          compiler_params=_cparams(2))(o0, o1, z, w, dout)


def _head_loss(y, x, gate, lng, lnb, tgt):
    r = DEEPNORM_ALPHA * x + gate * y
    mu = jnp.mean(r, axis=-1, keepdims=True)
    rc = r - mu
    var = jnp.mean(rc * rc, axis=-1, keepdims=True)
    err = rc * lax.rsqrt(var + LN_EPS) * lng + lnb - tgt
    return (0.5 / D_MODEL) * jnp.sum(jnp.sum(err * err, axis=-1, keepdims=True), axis=0, keepdims=True)


def head_fwd_bwd(s5o, gdo, x, tgt, gate, lng, lnb, ws, wg):
    B, L, _ = x.shape
    T = min(TOK_TILE, L)

    def body(s_ref, g_ref, x_ref, t_ref, gate_ref, lng_ref, lnb_ref, ws_ref, wg_ref,
             loss_ref, ds_ref, dg_ref, gx_ref, dws_ref, dwg_ref, dgate_ref, dlng_ref, dlnb_ref):
        n = pl.program_id(1)

        @pl.when(_first_step())
        def _():
            for r in (dws_ref, dwg_ref, dlng_ref, dlnb_ref):
                r[...] = jnp.zeros_like(r)

        @pl.when(n == 0)
        def _():
            loss_ref[...] = jnp.zeros_like(loss_ref)
            dgate_ref[...] = jnp.zeros_like(dgate_ref)

        sv = s_ref[...].astype(bf16)
        gv = g_ref[...].astype(bf16)
        y = _dot(sv, ws_ref[...]) + _dot(gv, wg_ref[...])
        loss, vjp = jax.vjp(lambda *a: _head_loss(*a, t_ref[...]), y, x_ref[...], gate_ref[...], lng_ref[...],
                            lnb_ref[...])
        dy, dx, dgate, dlng, dlnb = vjp(jnp.ones((1, 1), f32))
        loss_ref[...] += jnp.broadcast_to(loss, loss_ref.shape)
        dyb = dy.astype(bf16)
        ds_ref[...] = _dot_nt(dyb, ws_ref[...])
        dg_ref[...] = _dot_nt(dyb, wg_ref[...])
        gx_ref[...] = dx
        dws_ref[...] += _dot_tn(sv, dyb)
        dwg_ref[...] += _dot_tn(gv, dyb)
        dgate_ref[...] += dgate
        dlng_ref[...] += dlng
        dlnb_ref[...] += dlnb

    half, full = _tok(T, D_S5), _tok(T, D_MODEL)
    row = _resident((1, D_MODEL))
    wsp = _resident((D_S5, D_MODEL))
    return pl.pallas_call(
        body, name="head_fwd_bwd", grid=(B, L // T),
        in_specs=[half, half, full, full, _per_batch(1, D_MODEL), row, row, wsp, wsp],
        out_specs=[_per_batch(8, LANES), half, half, full, wsp, wsp, _per_batch(1, D_MODEL), row, row],
        out_shape=[SDS((B, 8, LANES), f32), SDS((B, L, D_S5), f32), SDS((B, L, D_GDN), f32), SDS((B, L, D_MODEL), f32),
                   SDS((D_S5, D_MODEL), f32), SDS((D_GDN, D_MODEL), f32), SDS((B, 1, D_MODEL), f32),
                   SDS((1, D_MODEL), f32), SDS((1, D_MODEL), f32)],
        compiler_params=_cparams(2),
    )(s5o, gdo, x, tgt, gate, lng, lnb, ws, wg)


def adamw(w, g, m, v, *, name):
    R = w.shape[0]
    T = R
    for cand in (2048, 1864, 1088, 1024, 544, 512, 272, 256, 136, 128, 64, 32, 16, 8):
        if R % cand == 0:
            T = cand
            break
    spec = pl.BlockSpec((T, LANES), lambda i: (i, 0))

    def body(w_ref, g_ref, m_ref, v_ref, d_ref, nm_ref, nv_ref):
        gv = g_ref[...]
        nm = ADAM_B1 * m_ref[...] + (1.0 - ADAM_B1) * gv
        nv = ADAM_B2 * v_ref[...] + (1.0 - ADAM_B2) * jnp.square(gv)
        m_hat = nm / (1.0 - ADAM_B1 ** ADAM_STEP)
        v_hat = nv / (1.0 - ADAM_B2 ** ADAM_STEP)
        d_ref[...] = -ADAM_LR * (m_hat / (jnp.sqrt(v_hat) + ADAM_EPS) + ADAM_WD * w_ref[...])
        nm_ref[...] = nm
        nv_ref[...] = nv

    return pl.pallas_call(body, name=name, grid=(R // T,), in_specs=[spec] * 4, out_specs=[spec] * 3,
                          out_shape=[SDS((R, LANES), f32)] * 3, compiler_params=_cparams(1))(w, g, m, v)


def add_rows(a, b, *, name, extra=()):
    R = a.shape[0]
    T = R
    for cand in (2048, 1932, 1864, 1288, 1104, 1024, 966, 552, 512, 256, 184, 128, 64, 32, 16, 8):
        if R % cand == 0 and cand % 8 == 0:
            T = cand
            break
    spec = pl.BlockSpec((T, LANES), lambda i: (i, 0))
    ops = (a, b) + tuple(extra)

    def body(*refs):
        acc = refs[0][...]
        for r in refs[1:-1]:
            acc = acc + r[...]
        refs[-1][...] = acc

    return pl.pallas_call(body, name=name, grid=(R // T,), in_specs=[spec] * len(ops), out_specs=spec,
                          out_shape=SDS((R, LANES), f32), compiler_params=_cparams(1))(*ops)


CHIP_FLIPS = ((1, 0), (0, 1), (1, 1))


def _pos():
    return lax.axis_index("x"), lax.axis_index("y"), lax.axis_index("c")


def _comm_call(body, src, out_shape, n_remote, name):
    any_spec = pl.BlockSpec(memory_space=pl.ANY)
    return pl.pallas_call(
        body, name=name, in_specs=[any_spec], out_specs=any_spec, out_shape=SDS(out_shape, f32),
        scratch_shapes=[pltpu.SemaphoreType.DMA((n_remote,)), pltpu.SemaphoreType.DMA((n_remote,)),
                        pltpu.SemaphoreType.DMA((1,))],
        compiler_params=pltpu.CompilerParams(has_side_effects=True),
    )(src)


def gather_shards(shard):
    _, H, _ = shard.shape

    def body(src, out, send_sems, recv_sems, loc_sem):
        x, y, c = _pos()
        j = 2 * x + y
        sib = (x, y, 1 - c)
        mine = pltpu.make_async_copy(src, out.at[j], loc_sem.at[0])
        mine.start()
        first, passed = [], []
        for k, (fx, fy) in enumerate(CHIP_FLIPS):
            tx, ty = x ^ fx, y ^ fy
            first.append(pltpu.make_async_remote_copy(src.at[c], out.at[j, c], send_sems.at[k], recv_sems.at[k],
                                                      device_id=(tx, ty, c), device_id_type=MESH))
            jk = 2 * tx + ty
            passed.append(pltpu.make_async_remote_copy(out.at[jk, c], out.at[jk, c], send_sems.at[3 + k],
                                                       recv_sems.at[3 + k], device_id=sib, device_id_type=MESH))
        for cp in first:
            cp.start()
        for k in range(3):
            first[k].wait_recv()
            passed[k].start()
        for k in range(3):
            passed[k].wait_recv()
        for cp in first + passed:
            cp.wait_send()
        mine.wait()

    return _comm_call(body, shard, (4, 2, H, LANES), 6, "gather_shards")


def swap_halves(p):
    A, _, H, _ = p.shape

    def body(src, out, send_sems, recv_sems, loc_sem):
        x, y, c = _pos()
        cps = [pltpu.make_async_remote_copy(src.at[a, 1 - c], out.at[a], send_sems.at[a], recv_sems.at[a],
                                            device_id=(x, y, 1 - c), device_id_type=MESH) for a in range(A)]
        for cp in cps:
            cp.start()
        for cp in cps:
            cp.wait()

    return _comm_call(body, p, (A, H, LANES), A, "swap_halves")


def scatter_to_chips(q):
    _, H, _ = q.shape

    def body(src, out, send_sems, recv_sems, loc_sem):
        x, y, c = _pos()
        cps = []
        for k, (fx, fy) in enumerate(CHIP_FLIPS):
            tx, ty = x ^ fx, y ^ fy
            cps.append(pltpu.make_async_remote_copy(src.at[2 * tx + ty], out.at[k], send_sems.at[k], recv_sems.at[k],
                                                    device_id=(tx, ty, c), device_id_type=MESH))
        for cp in cps:
            cp.start()
        for cp in cps:
            cp.wait()

    return _comm_call(body, q, (3, H, LANES), 3, "scatter_to_chips")


def join_halves(f):
    H, _ = f.shape

    def body(src, out, send_sems, recv_sems, loc_sem):
        x, y, c = _pos()
        mine = pltpu.make_async_copy(src, out.at[c], loc_sem.at[0])
        mine.start()
        cp = pltpu.make_async_remote_copy(src, out.at[c], send_sems.at[0], recv_sems.at[0],
                                          device_id=(x, y, 1 - c), device_id_type=MESH)
        cp.start()
        cp.wait()
        mine.wait()

    return _comm_call(body, f, (2, H, LANES), 1, "join_halves")


def gather_small(s):
    R, _ = s.shape

    def body(src, out, send_sems, recv_sems, loc_sem):
        x, y, c = _pos()
        j = 2 * x + y
        mine = pltpu.make_async_copy(src, out.at[j], loc_sem.at[0])
        mine.start()
        cps = []
        for k, (fx, fy) in enumerate(CHIP_FLIPS):
            cps.append(pltpu.make_async_remote_copy(src, out.at[j], send_sems.at[k], recv_sems.at[k],
                                                    device_id=(x ^ fx, y ^ fy, c), device_id_type=MESH))
        for cp in cps:
            cp.start()
        for cp in cps:
            cp.wait()
        mine.wait()

    return _comm_call(body, s, (4, R, LANES), 3, "gather_small")


def _rows(a):
    flat = a.reshape(-1)
    pad = (-flat.shape[0]) % LANES
    if pad:
        flat = jnp.concatenate([flat, jnp.zeros((pad,), flat.dtype)])
    return flat.reshape(-1, LANES)


SHARD_SHAPES = ((D_MODEL, 768), (D_MODEL, 772), (256, D_MODEL), (128, D_S5), (3, 3, 384))
SHARD_ROWS = tuple(-(-(functools.reduce(lambda p, q: p * q, s)) // LANES) for s in SHARD_SHAPES)
SHARD_TOTAL = 14912
SMALL_SHAPES = ((D_MODEL,), (1, 3 * D_MODEL), (1, 2, 32, 64), (1, 2, 32, 64), (1, 2, 32), (1, 2, 32, 64, 16),
                (1, 2, 32, 64, 16), (1, 2, 32, 16, 64), (1, 2, 32, 16, 64), (1, D_S5), (1, D_S5), (1, 2, 4), (1, 2, 4),
                (1, GDN_HEAD), (1, D_MODEL), (1, D_MODEL))
SMALL_ROWS = tuple(-(-(functools.reduce(lambda p, q: p * q, s)) // LANES) for s in SMALL_SHAPES)
SMALL_TOTAL = 2176
SMALL_QUARTER = SMALL_TOTAL // 4


def _pack(parts, total):
    rows = [_rows(p.astype(f32)) for p in parts]
    n = sum(r.shape[0] for r in rows)
    if total > n:
        rows.append(jnp.zeros((total - n, LANES), f32))
    return jnp.concatenate(rows, axis=0)


def _unpack(buf, shapes, nrows):
    out, r = [], 0
    for s, n in zip(shapes, nrows):
        size = functools.reduce(lambda p, q: p * q, s)
        out.append(buf[r:r + n].reshape(-1)[:size].reshape(s))
        r += n
    return out


def _block_diag_in(bb):
    eye = jnp.eye(S5_GROUPS, dtype=f32)
    b3 = bb.reshape(S5_GROUPS, S5_STATE, S5_GROUP)
    return jnp.einsum('gpc,gh->gchp', b3, eye).reshape(D_S5, S5_HALF)


def _block_diag_in_t(d):
    d5 = d.reshape(S5_GROUPS, S5_GROUP, S5_GROUPS, S5_STATE)
    return jnp.einsum('gcgp->gpc', d5).reshape(S5_GROUPS, S5_STATE * S5_GROUP)


def _block_diag_out(cm):
    eye = jnp.eye(S5_GROUPS, dtype=f32)
    return jnp.einsum('gcp,gh->hpgc', cm, eye).reshape(S5_HALF, D_S5)


def _block_diag_out_t(d):
    d5 = d.reshape(S5_GROUPS, S5_STATE, S5_GROUPS, S5_GROUP)
    return jnp.einsum('gpgc->gcp', d5)


def _to_chunk_rows(a):
    B, L, W = a.shape
    return a.reshape(B, L // CHUNK, CHUNK, W).transpose(0, 1, 3, 2)


def _from_chunk_rows(a):
    B, nc, W, _ = a.shape
    return a.transpose(0, 1, 3, 2).reshape(B, nc * CHUNK, W)


def local_step(x, c, ctx, c_ctx, tgt, w_ada, b_ada, w_in, lam_re, lam_im, log_dt, b_re, b_im, c_re, c_im, s5_d,
               w_glu, b_glu, conv_w, a_log, dt_bias, norm_w, w_out, ln_g, ln_b):
    B, L, _ = x.shape
    Lc = ctx.shape[1]
    zeros_state = jnp.zeros((B, GDN_HEADS, GDN_HEAD, GDN_HEAD), f32)

    cc = jnp.concatenate([c, c_ctx[None, :], jnp.zeros((8 - B - 1, D_MODEL), f32)], axis=0)
    w_ada_b = w_ada.astype(bf16)
    m = ada_fwd(cc, w_ada_b, b_ada)
    shift, scale, gate = m[:B, :D_MODEL], m[:B, D_MODEL:2 * D_MODEL], m[:B, 2 * D_MODEL:]
    mod = jnp.stack([scale, shift], axis=1)
    mod_c = jnp.broadcast_to(jnp.stack([m[B, D_MODEL:2 * D_MODEL], m[B, :D_MODEL]], axis=0)[None], (B, 2, D_MODEL))

    w_ba = jnp.concatenate([w_in[:, 3072:], jnp.zeros((D_MODEL, LANES - N_GATE), f32)], axis=1)
    ws = [w.astype(bf16) for w in (w_in[:, :512], w_in[:, 512:1024], w_in[:, 1024:2560], w_in[:, 2560:3072], w_ba)]
    u, z_s5, qkv, z_gdn, ba = in_proj_fwd(x, mod, ws, name="in_proj_fwd")
    uc, _, qkvc, _, bac = in_proj_fwd(ctx, mod_c, ws, name="in_proj_fwd_ctx")

    ng = N_DIR * S5_GROUPS
    zoh_in = (lam_re.reshape(ng, S5_STATE), lam_im.reshape(ng, S5_STATE), log_dt.reshape(ng, 1),
              b_re.reshape(ng, S5_STATE * S5_GROUP), b_im.reshape(ng, S5_STATE * S5_GROUP))
    expand = (jnp.arange(S5_STATE * S5_GROUP)[None, :] // S5_GROUP == jnp.arange(S5_STATE)[:, None]).astype(f32)
    ar, ai, bbr, bbi = s5_zoh_fwd(*zoh_in, expand)
    c_re3, c_im3 = c_re.reshape(N_DIR, S5_GROUPS, S5_GROUP, S5_STATE), c_im.reshape(N_DIR, S5_GROUPS, S5_GROUP, S5_STATE)
    s5w, ys, hins, hins_c = [], [], [], []
    for d in range(N_DIR):
        g = slice(d * S5_GROUPS, (d + 1) * S5_GROUPS)
        wd = (_block_diag_in(bbr[g]).astype(bf16), _block_diag_in(bbi[g]).astype(bf16),
              _block_diag_out(c_re3[d]).astype(bf16), _block_diag_out(-c_im3[d]).astype(bf16),
              jnp.stack([ar[g].reshape(-1), ai[g].reshape(-1)], axis=0))
        s5w.append(wd)
        hin_c, hend_c = s5_scan_fwd(uc, *wd, jnp.zeros((B, 2, S5_HALF), f32), d=d, need_y=False, name=f"s5_fwd_ctx{d}")
        y_d, hin, _ = s5_scan_fwd(u, *wd, hend_c, d=d, need_y=True, name=f"s5_fwd{d}")
        ys.append(y_d)
        hins.append(hin)
        hins_c.append(hin_c)
    glu_w = (s5_d.reshape(1, D_S5), w_glu, b_glu.reshape(1, D_S5))
    s5o = s5_glu_fwd(u, ys[0], ys[1], z_s5, *glu_w)

    conv16 = jnp.concatenate([conv_w.reshape(9, 3 * D_GDN), jnp.zeros((CONV_ROWS - 9, 3 * D_GDN), f32)], axis=0)
    act = conv_fwd(qkv, conv16, is_ctx=False, name="conv_fwd")
    act_c = conv_fwd(qkvc, conv16, is_ctx=True, name="conv_fwd_ctx")
    pad8 = jnp.zeros((1, 8), f32)
    alog16 = jnp.concatenate([pad8, a_log.reshape(1, 8)], axis=1)
    dtb16 = jnp.concatenate([pad8, dt_bias.reshape(1, 8)], axis=1)
    bg = gates_fwd(ba, alog16, dtb16, name="gates_fwd")
    bg_c = gates_fwd(bac, alog16, dtb16, name="gates_fwd_ctx")
    bgr, bgr_c = _to_chunk_rows(bg), _to_chunk_rows(bg_c)
    os_, cks, cks_c = [], [], []
    for d in range(N_DIR):
        ck_c, s_c = gdn_fwd(act_c, bg_c, bgr_c, zeros_state, d=d, need_o=False, name=f"gdn_fwd_ctx{d}")
        o_d, ck, _ = gdn_fwd(act, bg, bgr, s_c, d=d, need_o=True, name=f"gdn_fwd{d}")
        os_.append(o_d)
        cks.append(ck)
        cks_c.append(ck_c)
    nw = norm_w.reshape(1, GDN_HEAD)
    gdo = gnorm_fwd(os_[0], os_[1], z_gdn, nw)

    w_out_b = w_out.astype(bf16)
    loss8, ds5o, dgdo, gx_res, dws, dwg, dgate, dlng, dlnb = head_fwd_bwd(
        s5o, gdo, x, tgt, gate[:, None, :], ln_g.reshape(1, D_MODEL), ln_b.reshape(1, D_MODEL),
        w_out_b[:D_S5], w_out_b[D_S5:])
    loss = jnp.sum(loss8[:, 0, 0])
    d_w_out = jnp.concatenate([dws, dwg], axis=0)

    do, dz_gdn, d_norm_w = gnorm_bwd(os_[0], os_[1], z_gdn, nw, dgdo)
    dacts, dacts_c = [], []
    dbg = jnp.zeros_like(bg)
    dbg_c = jnp.zeros_like(bg_c)
    for d in range(N_DIR):
        dact, dbg_d, dbgr_d, ds0 = gdn_bwd(act, bg, bgr, cks[d], do, zeros_state, d=d, name=f"gdn_bwd{d}")
        dact_c, dbgc_d, dbgrc_d, _ = gdn_bwd(act_c, bg_c, bgr_c, cks_c[d], None, ds0, d=d, name=f"gdn_bwd_ctx{d}")
        dacts.append(dact)
        dacts_c.append(dact_c)
        dbg = dbg + dbg_d + _from_chunk_rows(dbgr_d)
        dbg_c = dbg_c + dbgc_d + _from_chunk_rows(dbgrc_d)
    dba, dal, ddt = gates_bwd(ba, alog16, dtb16, dbg, name="gates_bwd")
    dbac, dal_c, ddt_c = gates_bwd(bac, alog16, dtb16, dbg_c, name="gates_bwd_ctx")
    d_a_log = (dal + dal_c)[:, 8:].reshape(1, N_DIR, GDN_HEADS)
    d_dt_bias = (ddt + ddt_c)[:, 8:].reshape(1, N_DIR, GDN_HEADS)
    dqkv, dcw = conv_bwd(qkv, conv16, dacts[0], dacts[1], is_ctx=False, name="conv_bwd")
    dqkvc, dcw_c = conv_bwd(qkvc, conv16, dacts_c[0], dacts_c[1], is_ctx=True, name="conv_bwd_ctx")
    d_conv_w = (jnp.sum(dcw, axis=0) + jnp.sum(dcw_c, axis=0))[:9].reshape(1, 3, 3, 3 * D_GDN)

    du_skip, dy, dz_s5, d_s5_d, d_w_glu, d_b_glu = s5_glu_bwd(u, ys[0], ys[1], z_s5, *glu_w, ds5o)
    du, duc = du_skip, jnp.zeros_like(uc)
    dar, dai, dbbr, dbbi, dcre, dcim = [], [], [], [], [], []
    for d in range(N_DIR):
        du_d, dbre1, dbim1, dct1, dcb1, da1, dh0 = s5_scan_bwd(u, dy, *s5w[d], hins[d],
                                                                jnp.zeros((B, 2, S5_HALF), f32), d=d, name=f"s5_bwd{d}")
        duc_d, dbre2, dbim2, _, _, da2, _ = s5_scan_bwd(uc, None, *s5w[d], hins_c[d], dh0, d=d, name=f"s5_bwd_ctx{d}")
        du, duc = du + du_d, duc + duc_d
        da = da1 + da2
        dar.append(da[0].reshape(S5_GROUPS, S5_STATE))
        dai.append(da[1].reshape(S5_GROUPS, S5_STATE))
        dbbr.append(_block_diag_in_t(dbre1 + dbre2))
        dbbi.append(_block_diag_in_t(dbim1 + dbim2))
        dcre.append(_block_diag_out_t(dct1))
        dcim.append(-_block_diag_out_t(dcb1))
    dlr, dli, dldt, dbre, dbim = s5_zoh_bwd(*zoh_in, expand, jnp.concatenate(dar, 0), jnp.concatenate(dai, 0),
                                            jnp.concatenate(dbbr, 0), jnp.concatenate(dbbi, 0))
    lam_shape = (1, N_DIR, S5_GROUPS, S5_STATE)
    b_shape = (1, N_DIR, S5_GROUPS, S5_STATE, S5_GROUP)
    c_shape = (1, N_DIR, S5_GROUPS, S5_GROUP, S5_STATE)
    d_s5 = (dlr.reshape(lam_shape), dli.reshape(lam_shape), dldt.reshape(1, N_DIR, S5_GROUPS), dbre.reshape(b_shape),
            dbim.reshape(b_shape), jnp.stack(dcre, 0).reshape(c_shape), jnp.stack(dcim, 0).reshape(c_shape))

    padg = lambda a: jnp.concatenate([a, jnp.zeros(a.shape[:2] + (LANES - N_GATE,), f32)], axis=2)
    dws_l = in_proj_bwd(x, mod, (du, dz_s5, dqkv, dz_gdn, padg(dba)), ws, gx_res, name="in_proj_bwd")
    zc = jnp.zeros_like(uc)
    dws_c = in_proj_bwd(ctx, mod_c, (duc, zc, dqkvc, zc, padg(dbac)), ws, None, name="in_proj_bwd_ctx")
    grad_x = dws_l[6]
    dwp = [a + b for a, b in zip(dws_l[:5], dws_c[:5])]
    d_w_in = jnp.concatenate([dwp[0], dwp[1], dwp[2], dwp[3], dwp[4][:, :N_GATE]], axis=1)
    dmod, dmod_c = dws_l[5], jnp.sum(dws_c[5], axis=0)

    dm_rows = jnp.concatenate([dmod[:, 1], dmod[:, 0], dgate[:, 0]], axis=1)
    dm_ctx = jnp.concatenate([dmod_c[1], dmod_c[0], jnp.zeros((D_MODEL,), f32)])[None]
    dm = jnp.concatenate([dm_rows, dm_ctx, jnp.zeros((8 - B - 1, 3 * D_MODEL), f32)], axis=0)
    dcc, d_w_ada, d_b_ada = ada_bwd(cc, w_ada_b, dm)
    grads = (dcc[B], d_w_ada[None], d_b_ada, d_w_in[None], *d_s5, d_s5_d, d_w_glu[None], d_b_glu, d_conv_w, d_a_log,
             d_dt_bias, d_norm_w, d_w_out[None], dlng, dlnb)
    return loss, grad_x, grads


SHARDED = (1, 3, 18, 12, 14)
SHARD_AXIS = (2, 2, 1, 1, 3)
SMALL = tuple(i for i in range(21) if i not in SHARDED)


def _chip_slices(g, axis):
    return [lax.slice_in_dim(g, j * (g.shape[axis] // 4), (j + 1) * (g.shape[axis] // 4), axis=axis) for j in range(4)]


def kernel(x, c, ctx, c_ctx, w_ada, b_ada, w_in, s5_lambda_re, s5_lambda_im, s5_log_dt, s5_b_re, s5_b_im, s5_c_re, s5_c_im, s5_d, w_glu, b_glu, conv_w, gdn_a_log, gdn_dt_bias, gdn_norm_w, w_out, ln_g, ln_b, loss_target, m_c_ctx, m_w_ada, m_b_ada, m_w_in, m_s5_lambda_re, m_s5_lambda_im, m_s5_log_dt, m_s5_b_re, m_s5_b_im, m_s5_c_re, m_s5_c_im, m_s5_d, m_w_glu, m_b_glu, m_conv_w, m_gdn_a_log, m_gdn_dt_bias, m_gdn_norm_w, m_w_out, m_ln_g, m_ln_b, v_c_ctx, v_w_ada, v_b_ada, v_w_in, v_s5_lambda_re, v_s5_lambda_im, v_s5_log_dt, v_s5_b_re, v_s5_b_im, v_s5_c_re, v_s5_c_im, v_s5_d, v_w_glu, v_b_glu, v_conv_w, v_gdn_a_log, v_gdn_dt_bias, v_gdn_norm_w, v_w_out, v_ln_g, v_ln_b):
    weights = [c_ctx, w_ada, b_ada, w_in, s5_lambda_re, s5_lambda_im, s5_log_dt, s5_b_re, s5_b_im, s5_c_re, s5_c_im,
               s5_d, w_glu, b_glu, conv_w, gdn_a_log, gdn_dt_bias, gdn_norm_w, w_out, ln_g, ln_b]
    ms = [m_c_ctx, m_w_ada, m_b_ada, m_w_in, m_s5_lambda_re, m_s5_lambda_im, m_s5_log_dt, m_s5_b_re, m_s5_b_im,
          m_s5_c_re, m_s5_c_im, m_s5_d, m_w_glu, m_b_glu, m_conv_w, m_gdn_a_log, m_gdn_dt_bias, m_gdn_norm_w, m_w_out,
          m_ln_g, m_ln_b]
    vs = [v_c_ctx, v_w_ada, v_b_ada, v_w_in, v_s5_lambda_re, v_s5_lambda_im, v_s5_log_dt, v_s5_b_re, v_s5_b_im,
          v_s5_c_re, v_s5_c_im, v_s5_d, v_w_glu, v_b_glu, v_conv_w, v_gdn_a_log, v_gdn_dt_bias, v_gdn_norm_w, v_w_out,
          v_ln_g, v_ln_b]
    cpos = lax.axis_index("c")
    half = SHARD_TOTAL // 2

    w_shard = _pack([weights[i] for i in SHARDED], SHARD_TOTAL)
    gathered = gather_shards(w_shard.reshape(2, half, LANES)).reshape(4, SHARD_TOTAL, LANES)
    per_chip = [_unpack(gathered[j], SHARD_SHAPES, SHARD_ROWS) for j in range(4)]
    full = {}
    for n, (i, axis) in enumerate(zip(SHARDED, SHARD_AXIS)):
        full[i] = jnp.concatenate([per_chip[j][n].reshape(weights[i].shape) for j in range(4)], axis=axis)
    wl = [full.get(i, weights[i]) for i in range(21)]

    loss, grad_x, grads = local_step(
        x, c, ctx, wl[0], loss_target, wl[1][0], wl[2], wl[3][0], wl[4], wl[5], wl[6], wl[7], wl[8], wl[9], wl[10],
        wl[11], wl[12][0], wl[13], wl[14][0], wl[15], wl[16], wl[17], wl[18][0], wl[19], wl[20])
    grads = [g.reshape(w.shape) for g, w in zip(grads, wl)]
    loss = lax.psum(loss, ("x", "y", "c"))

    small_rows = _pack([grads[i] for i in SMALL], SMALL_TOTAL).reshape(4, SMALL_QUARTER, LANES)
    sliced = [_chip_slices(grads[i], axis) for i, axis in zip(SHARDED, SHARD_AXIS)]
    slab = SHARD_TOTAL + SMALL_QUARTER
    p = jnp.stack([jnp.concatenate([_pack([s[j] for s in sliced], SHARD_TOTAL), small_rows[j]], axis=0)
                   for j in range(4)], axis=0)
    hs = slab // 2
    p4 = p.reshape(4, 2, hs, LANES)
    got = swap_halves(p4)
    own = lax.dynamic_index_in_dim(p4, cpos, axis=1, keepdims=False)
    q = add_rows(own.reshape(4 * hs, LANES), got.reshape(4 * hs, LANES), name="sum_cores").reshape(4, hs, LANES)
    jchip = 2 * lax.axis_index("x") + lax.axis_index("y")
    others = scatter_to_chips(q)
    mine = lax.dynamic_index_in_dim(q, jchip, axis=0, keepdims=False)
    f = add_rows(mine, others[0], name="sum_chips", extra=(others[1], others[2]))
    reduced = join_halves(f).reshape(slab, LANES)
    g_shard = reduced[:SHARD_TOTAL]
    g_small = gather_small(reduced[SHARD_TOTAL:]).reshape(SMALL_TOTAL, LANES)

    m_shard = _pack([ms[i] for i in SHARDED], SHARD_TOTAL)
    v_shard = _pack([vs[i] for i in SHARDED], SHARD_TOTAL)
    d_sh, nm_sh, nv_sh = adamw(w_shard, g_shard, m_shard, v_shard, name="adamw_shards")
    w_small = _pack([weights[i] for i in SMALL], SMALL_TOTAL)
    m_small = _pack([ms[i] for i in SMALL], SMALL_TOTAL)
    v_small = _pack([vs[i] for i in SMALL], SMALL_TOTAL)
    d_sm, nm_sm, nv_sm = adamw(w_small, g_small, m_small, v_small, name="adamw_small")

    def spread(sh_buf, sm_buf):
        sh = _unpack(sh_buf, SHARD_SHAPES, SHARD_ROWS)
        sm = _unpack(sm_buf, SMALL_SHAPES, SMALL_ROWS)
        out = [None] * 21
        for n, i in enumerate(SHARDED):
            out[i] = sh[n].reshape(weights[i].shape)
        for n, i in enumerate(SMALL):
            out[i] = sm[n].reshape(weights[i].shape)
        return out

    return (loss, grad_x, *spread(g_shard, g_small), *spread(d_sh, d_sm), *spread(nm_sh, nm_sm), *spread(nv_sh, nv_sm))
```

```python
import functools

import jax
import jax.numpy as jnp
from jax import lax
from jax.experimental import pallas as pl
from jax.experimental.pallas import tpu as pltpu

f32 = jnp.float32
bf16 = jnp.bfloat16
SDS = jax.ShapeDtypeStruct

D_MODEL = 1024
D_S5 = 512
S5_GROUP = 16
S5_GROUPS = 32
S5_STATE = 64
S5_HALF = S5_GROUPS * S5_STATE
D_GDN = 512
GDN_HEAD = 128
GDN_HEADS = 4
CHUNK = 64
GRID_W = 64
N_DIR = 2
P_IN = 3088
DEEPNORM_ALPHA = 2.0 ** 0.25
LN_EPS = 1e-5
NORM_EPS = 1e-6
ADAM_LR, ADAM_B1, ADAM_B2, ADAM_EPS, ADAM_WD, ADAM_STEP = 0.001, 0.9, 0.999, 1e-08, 0.01, 10

LANES = 128
VMEM_LIMIT = 56 * 1024 * 1024
TOK_TILE = 256
S5_TILE = 256
MESH = pl.DeviceIdType.MESH


def _cparams(n_grid):
    return pltpu.CompilerParams(dimension_semantics=("arbitrary",) * n_grid, vmem_limit_bytes=VMEM_LIMIT)


def _dot(a, b):
    return jnp.dot(a.astype(bf16), b.astype(bf16), preferred_element_type=f32)


def _dot_nt(a, b):
    return lax.dot_general(a.astype(bf16), b.astype(bf16), (((1,), (1,)), ((), ())), preferred_element_type=f32)


def _dot_tn(a, b):
    return lax.dot_general(a.astype(bf16), b.astype(bf16), (((0,), (0,)), ((), ())), preferred_element_type=f32)


def _dot_hi(a, b):
    return jnp.dot(a, b, precision=lax.Precision.HIGHEST, preferred_element_type=f32)


def _dot_h3(a, b):
    return jnp.dot(a, b, precision=lax.Precision.HIGH, preferred_element_type=f32)


@jax.custom_vjp
def _mm(a, b):
    return _dot(a, b)


@jax.custom_vjp
def _mm_nt(a, b):
    return _dot_nt(a, b)


@jax.custom_vjp
def _mm_tn(a, b):
    return _dot_tn(a, b)


_mm.defvjp(lambda a, b: (_dot(a, b), (a, b)), lambda r, g: (_mm_nt(g, r[1]), _mm_tn(r[0], g)))
_mm_nt.defvjp(lambda a, b: (_dot_nt(a, b), (a, b)), lambda r, g: (_mm(g, r[1]), _mm_tn(g, r[0])))
_mm_tn.defvjp(lambda a, b: (_dot_tn(a, b), (a, b)), lambda r, g: (_mm_nt(r[1], g), _mm(r[0], g)))


def _silu(x):
    return x * jax.nn.sigmoid(x)


def _gelu(x):
    return 0.5 * x * (1.0 + lax.erf(x * (2.0 ** -0.5)))


def _resident(shape):
    nd = len(shape)
    return pl.BlockSpec(shape, lambda *_: (0,) * nd, pipeline_mode=pl.Buffered(1))


def _tok(tile, width, nt=None, rev=False):
    if rev:
        return pl.BlockSpec((None, tile, width), lambda b, n: (b, nt - 1 - n, 0))
    return pl.BlockSpec((None, tile, width), lambda b, n: (b, n, 0))


def _per_batch(rows, width):
    return pl.BlockSpec((None, rows, width), lambda b, n: (b, 0, 0))


def _first_step():
    return jnp.logical_and(pl.program_id(0) == 0, pl.program_id(1) == 0)


ADA_SHARD = 3 * D_MODEL // 4


def ada_fwd(cc, w, b):
    def body(cc_ref, w_ref, b_ref, m_ref):
        s = _silu(cc_ref[...]).astype(bf16)
        for j in range(4):
            sl = slice(j * ADA_SHARD, (j + 1) * ADA_SHARD)
            m_ref[:, sl] = _dot(s, w_ref[j]) + b_ref[:, sl]

    return pl.pallas_call(body, name="ada_fwd", out_shape=SDS((8, 3 * D_MODEL), f32),
                          compiler_params=pltpu.CompilerParams(vmem_limit_bytes=VMEM_LIMIT))(cc, w, b)


def ada_bwd(cc, w, dm):
    def body(cc_ref, w_ref, dm_ref, dcc_ref, dw_ref, db_ref):
        s, vjp = jax.vjp(_silu, cc_ref[...])
        ds = jnp.zeros((8, D_MODEL), f32)
        for j in range(4):
            dmj = dm_ref[:, j * ADA_SHARD:(j + 1) * ADA_SHARD]
            ds = ds + _dot_nt(dmj, w_ref[j])
            dw_ref[j] = _dot_tn(s, dmj)
        dcc_ref[...] = vjp(ds)[0]
        db_ref[...] = jnp.sum(dm_ref[...], axis=0, keepdims=True)

    return pl.pallas_call(
        body, name="ada_bwd",
        out_shape=[SDS((8, D_MODEL), f32), SDS((4, D_MODEL, ADA_SHARD), f32), SDS((1, 3 * D_MODEL), f32)],
        compiler_params=pltpu.CompilerParams(vmem_limit_bytes=VMEM_LIMIT))(cc, w, dm)


N_GATE = 2 * N_DIR * GDN_HEADS
IN_WIDTHS = (D_S5, D_S5, 3 * D_GDN, D_GDN, LANES)
IN_OFFS = (0, 512, 1024, 2560, 3072)
IN_PAD = 3200


def in_proj_fwd(x, mod, w, *, name):
    B, L, _ = x.shape
    T = min(TOK_TILE, L)

    def body(x_ref, mod_ref, w_ref, *o_refs):
        h = (x_ref[...] * (1.0 + mod_ref[0:1, :]) + mod_ref[1:2, :]).astype(bf16)
        for o_ref, off, wd in zip(o_refs, IN_OFFS, IN_WIDTHS):
            r = _dot(h, w_ref[:, off:off + wd])
            o_ref[...] = r[:, :o_ref.shape[-1]]

    outw = (D_S5, D_S5, 3 * D_GDN, D_GDN, N_GATE)
    return pl.pallas_call(
        body, name=name, grid=(B, L // T),
        in_specs=[_tok(T, D_MODEL), _per_batch(2, D_MODEL), _resident((D_MODEL, IN_PAD))],
        out_specs=[_tok(T, wd) for wd in outw],
        out_shape=[SDS((B, L, wd), f32) for wd in outw],
        compiler_params=_cparams(2),
    )(x, mod, w)


def in_proj_bwd(x, mod, ds, w, gx_res, *, name):
    B, L, _ = x.shape
    T = min(TOK_TILE, L)
    with_dx = gx_res is not None

    def body(*refs):
        x_ref, mod_ref = refs[0], refs[1]
        d_refs = refs[2:7]
        w_ref = refs[7]
        k = 8
        if with_dx:
            gx_ref = refs[k]
            k += 1
        dw_ref, dmod_ref = refs[k], refs[k + 1]
        if with_dx:
            dx_ref = refs[k + 2]
        n = pl.program_id(1)

        @pl.when(_first_step())
        def _():
            dw_ref[...] = jnp.zeros_like(dw_ref)

        @pl.when(n == 0)
        def _():
            dmod_ref[...] = jnp.zeros_like(dmod_ref)

        xv = x_ref[...]
        scale1 = 1.0 + mod_ref[0:1, :]
        h = (xv * scale1 + mod_ref[1:2, :]).astype(bf16)
        dh = jnp.zeros((T, D_MODEL), f32)
        for d_ref, off, wd in zip(d_refs, IN_OFFS, IN_WIDTHS):
            dv = d_ref[...].astype(bf16)
            dh = dh + _dot_nt(dv, w_ref[:, off:off + wd])
            dw_ref[:, off:off + wd] += _dot_tn(h, dv)
        dmod_ref[0:1, :] += jnp.sum(dh * xv, axis=0, keepdims=True)
        dmod_ref[1:2, :] += jnp.sum(dh, axis=0, keepdims=True)
        if with_dx:
            dx_ref[...] = gx_ref[...] + dh * scale1

    in_specs = ([_tok(T, D_MODEL), _per_batch(2, D_MODEL)] + [_tok(T, wd) for wd in IN_WIDTHS]
                + [_resident((D_MODEL, IN_PAD))])
    args = [x, mod, *ds, w]
    out_specs = [_resident((D_MODEL, IN_PAD)), _per_batch(2, D_MODEL)]
    out_shape = [SDS((D_MODEL, IN_PAD), f32), SDS((B, 2, D_MODEL), f32)]
    if with_dx:
        in_specs.append(_tok(T, D_MODEL))
        args.append(gx_res)
        out_specs.append(_tok(T, D_MODEL))
        out_shape.append(SDS((B, L, D_MODEL), f32))
    return pl.pallas_call(body, name=name, grid=(B, L // T), in_specs=in_specs, out_specs=out_specs,
                          out_shape=out_shape, compiler_params=_cparams(2))(*args)


def _s5_zoh(lr, li, ldt, bre, bim, expand):
    dt = jnp.exp(ldt)
    zr, zi = lr * dt, li * dt
    e = jnp.exp(zr)
    ar, ai = e * jnp.cos(zi), e * jnp.sin(zi)
    den = lr * lr + li * li
    czr = ((ar - 1.0) * lr + ai * li) / den
    czi = (ai * lr - (ar - 1.0) * li) / den
    czr_e, czi_e = _dot_hi(czr, expand), _dot_hi(czi, expand)
    return ar, ai, czr_e * bre - czi_e * bim, czr_e * bim + czi_e * bre


_ZOH_OUT = [(N_DIR * S5_GROUPS, S5_STATE)] * 2 + [(N_DIR * S5_GROUPS, S5_STATE * S5_GROUP)] * 2


def s5_zoh_fwd(lr, li, ldt, bre, bim, expand):
    def body(lr_ref, li_ref, ldt_ref, bre_ref, bim_ref, e_ref, ar_ref, ai_ref, bbr_ref, bbi_ref):
        ar, ai, bbr, bbi = _s5_zoh(lr_ref[...], li_ref[...], ldt_ref[...], bre_ref[...], bim_ref[...], e_ref[...])
        ar_ref[...], ai_ref[...], bbr_ref[...], bbi_ref[...] = ar, ai, bbr, bbi

    return pl.pallas_call(body, name="s5_zoh_fwd", out_shape=[SDS(s, f32) for s in _ZOH_OUT])(
        lr, li, ldt, bre, bim, expand)


def s5_zoh_bwd(lr, li, ldt, bre, bim, expand, dar, dai, dbbr, dbbi):
    def body(lr_ref, li_ref, ldt_ref, bre_ref, bim_ref, e_ref, dar_ref, dai_ref, dbbr_ref, dbbi_ref,
             dlr_ref, dli_ref, dldt_ref, dbre_ref, dbim_ref):
        ev = e_ref[...]
        _, vjp = jax.vjp(lambda a, b, c, d, e: _s5_zoh(a, b, c, d, e, ev),
                         lr_ref[...], li_ref[...], ldt_ref[...], bre_ref[...], bim_ref[...])
        outs = vjp((dar_ref[...], dai_ref[...], dbbr_ref[...], dbbi_ref[...]))
        dlr_ref[...], dli_ref[...], dldt_ref[...], dbre_ref[...], dbim_ref[...] = outs

    shapes = [lr.shape, li.shape, ldt.shape, bre.shape, bim.shape]
    return pl.pallas_call(body, name="s5_zoh_bwd", out_shape=[SDS(s, f32) for s in shapes])(
        lr, li, ldt, bre, bim, expand, dar, dai, dbbr, dbbi)


def _scan_rows(T, rev, ar, ai, hr0, hi0, r_ref, i_ref, off):
    def step(i, carry):
        hr, hi = carry
        t = off + ((T - 1 - i) if rev else i)
        nr = ar * hr - ai * hi + r_ref[pl.ds(t, 1), :]
        ni = ar * hi + ai * hr + i_ref[pl.ds(t, 1), :]
        r_ref[pl.ds(t, 1), :] = nr
        i_ref[pl.ds(t, 1), :] = ni
        return nr, ni

    return lax.fori_loop(0, T, step, (hr0, hi0), unroll=2)


def s5_scan_fwd(u, bre, bim, ctop, cbot, arow, h0, *, d, need_y, name):
    B, L, _ = u.shape
    T = min(S5_TILE, L)
    nt = L // T
    rev = d == 1

    def body(u_ref, bre_ref, bim_ref, ct_ref, cb_ref, a_ref, h0_ref, *rest):
        if need_y:
            y_ref, hin_ref, hend_ref, hr_scr, hi_scr, h_scr = rest
        else:
            hin_ref, hend_ref, hr_scr, hi_scr, h_scr = rest
        n = pl.program_id(1)

        @pl.when(n == 0)
        def _():
            h_scr[...] = h0_ref[...]

        hin_ref[...] = h_scr[...]
        uv = u_ref[...].astype(bf16)
        hr_scr[...] = _dot(uv, bre_ref[...])
        hi_scr[...] = _dot(uv, bim_ref[...])
        hr, hi = _scan_rows(T, rev, a_ref[0:1, :], a_ref[1:2, :], h_scr[0:1, :], h_scr[1:2, :], hr_scr, hi_scr, 0)
        h_scr[0:1, :] = hr
        h_scr[1:2, :] = hi
        if need_y:
            y_ref[...] = _dot(hr_scr[...], ct_ref[...]) + _dot(hi_scr[...], cb_ref[...])

        @pl.when(n == nt - 1)
        def _():
            hend_ref[...] = h_scr[...]

    state = _per_batch(2, S5_HALF)
    hin_spec = pl.BlockSpec((None, None, 2, S5_HALF), (lambda b, n: (b, nt - 1 - n, 0, 0)) if rev else (lambda b, n: (b, n, 0, 0)))
    out_specs = [hin_spec, state]
    out_shape = [SDS((B, nt, 2, S5_HALF), f32), SDS((B, 2, S5_HALF), f32)]
    if need_y:
        out_specs.insert(0, _tok(T, D_S5, nt, rev))
        out_shape.insert(0, SDS((B, L, D_S5), f32))
    return pl.pallas_call(
        body, name=name, grid=(B, nt),
        in_specs=[_tok(T, D_S5, nt, rev), _resident((D_S5, S5_HALF)), _resident((D_S5, S5_HALF)),
                  _resident((S5_HALF, D_S5)), _resident((S5_HALF, D_S5)), _resident((2, S5_HALF)), state],
        out_specs=out_specs, out_shape=out_shape,
        scratch_shapes=[pltpu.VMEM((T, S5_HALF), f32), pltpu.VMEM((T, S5_HALF), f32), pltpu.VMEM((2, S5_HALF), f32)],
        compiler_params=_cparams(2),
    )(u, bre, bim, ctop, cbot, arow, h0)


def s5_scan_bwd(u, dy, bre, bim, ctop, cbot, arow, hin, dhend, *, d, name):
    B, L, _ = u.shape
    T = min(S5_TILE, L)
    nt = L // T
    rev = d == 1
    has_dy = dy is not None
    PAD = 8

    def body(*refs):
        u_ref = refs[0]
        k = 1
        if has_dy:
            dy_ref = refs[1]
            k = 2
        bre_ref, bim_ref, ct_ref, cb_ref, a_ref, hin_ref, dhend_ref = refs[k:k + 7]
        du_ref, dbre_ref, dbim_ref, dct_ref, dcb_ref, da_ref, dh0_ref = refs[k + 7:k + 14]
        hr_scr, hi_scr, gr_scr, gi_scr, p_scr = refs[k + 14:]
        n = pl.program_id(1)

        @pl.when(_first_step())
        def _():
            for r in (dbre_ref, dbim_ref, dct_ref, dcb_ref, da_ref):
                r[...] = jnp.zeros_like(r)

        @pl.when(n == 0)
        def _():
            p_scr[...] = dhend_ref[...]

        ar, ai = a_ref[0:1, :], a_ref[1:2, :]
        uv = u_ref[...].astype(bf16)
        hr_scr[PAD:PAD + T, :] = _dot(uv, bre_ref[...])
        hi_scr[PAD:PAD + T, :] = _dot(uv, bim_ref[...])
        prev_row = PAD + T if rev else PAD - 1
        hr_scr[prev_row:prev_row + 1, :] = hin_ref[0:1, :]
        hi_scr[prev_row:prev_row + 1, :] = hin_ref[1:2, :]
        _scan_rows(T, rev, ar, ai, hin_ref[0:1, :], hin_ref[1:2, :], hr_scr, hi_scr, PAD)
        if has_dy:
            dyv = dy_ref[...].astype(bf16)
            gr_scr[...] = _dot_nt(dyv, ct_ref[...])
            gi_scr[...] = _dot_nt(dyv, cb_ref[...])
            dct_ref[...] += _dot_tn(hr_scr[PAD:PAD + T, :], dyv)
            dcb_ref[...] += _dot_tn(hi_scr[PAD:PAD + T, :], dyv)
        else:
            gr_scr[...] = jnp.zeros_like(gr_scr)
            gi_scr[...] = jnp.zeros_like(gi_scr)

        def step(i, carry):
            pr, pi, dar, dai = carry
            t = i if rev else T - 1 - i
            gr = gr_scr[pl.ds(t, 1), :] + pr
            gi = gi_scr[pl.ds(t, 1), :] + pi
            gr_scr[pl.ds(t, 1), :] = gr
            gi_scr[pl.ds(t, 1), :] = gi
            tp = PAD + t + (1 if rev else -1)
            hpr = hr_scr[pl.ds(tp, 1), :]
            hpi = hi_scr[pl.ds(tp, 1), :]
            dar = dar + hpr * gr + hpi * gi
            dai = dai + hpr * gi - hpi * gr
            return ar * gr + ai * gi, ar * gi - ai * gr, dar, dai

        zero = jnp.zeros((1, S5_HALF), f32)
        pr, pi, dar, dai = lax.fori_loop(0, T, step, (p_scr[0:1, :], p_scr[1:2, :], zero, zero), unroll=2)
        p_scr[0:1, :] = pr
        p_scr[1:2, :] = pi
        da_ref[0:1, :] += dar
        da_ref[1:2, :] += dai
        gr_all = gr_scr[...].astype(bf16)
        gi_all = gi_scr[...].astype(bf16)
        du_ref[...] = _dot_nt(gr_all, bre_ref[...]) + _dot_nt(gi_all, bim_ref[...])
        dbre_ref[...] += _dot_tn(uv, gr_all)
        dbim_ref[...] += _dot_tn(uv, gi_all)

        @pl.when(n == nt - 1)
        def _():
            dh0_ref[...] = p_scr[...]

    brev = not rev
    state = _per_batch(2, S5_HALF)
    hin_spec = pl.BlockSpec((None, None, 2, S5_HALF), (lambda b, n: (b, nt - 1 - n, 0, 0)) if brev else (lambda b, n: (b, n, 0, 0)))
    wspecs = [_resident((D_S5, S5_HALF)), _resident((D_S5, S5_HALF)), _resident((S5_HALF, D_S5)),
              _resident((S5_HALF, D_S5))]
    in_specs = [_tok(T, D_S5, nt, brev)] + ([_tok(T, D_S5, nt, brev)] if has_dy else []) + wspecs + [
        _resident((2, S5_HALF)), hin_spec, state]
    args = [u] + ([dy] if has_dy else []) + [bre, bim, ctop, cbot, arow, hin, dhend]
    return pl.pallas_call(
        body, name=name, grid=(B, nt), in_specs=in_specs,
        out_specs=[_tok(T, D_S5, nt, brev)] + wspecs + [_resident((2, S5_HALF)), state],
        out_shape=[SDS((B, L, D_S5), f32), SDS((D_S5, S5_HALF), f32), SDS((D_S5, S5_HALF), f32),
                   SDS((S5_HALF, D_S5), f32), SDS((S5_HALF, D_S5), f32), SDS((2, S5_HALF), f32),
                   SDS((B, 2, S5_HALF), f32)],
        scratch_shapes=[pltpu.VMEM((T + 2 * PAD, S5_HALF), f32), pltpu.VMEM((T + 2 * PAD, S5_HALF), f32),
                        pltpu.VMEM((T, S5_HALF), f32), pltpu.VMEM((T, S5_HALF), f32), pltpu.VMEM((2, S5_HALF), f32)],
        compiler_params=_cparams(2),
    )(*args)


def _glu_fn(u, y0, y1, z, dsk, wg, bg):
    g = _gelu(dsk * u + y0 + y1)
    return g * jax.nn.sigmoid(_mm(g, wg) + bg) * _silu(z)


def s5_glu_fwd(u, y0, y1, z, dsk, wg, bg):
    B, L, _ = u.shape
    T = min(TOK_TILE, L)

    def body(u_ref, y0_ref, y1_ref, z_ref, dsk_ref, wg_ref, bg_ref, o_ref):
        o_ref[...] = _glu_fn(u_ref[...], y0_ref[...], y1_ref[...], z_ref[...], dsk_ref[...], wg_ref[...].astype(f32),
                             bg_ref[...])

    t = _tok(T, D_S5)
    return pl.pallas_call(
        body, name="s5_glu_fwd", grid=(B, L // T),
        in_specs=[t, t, t, t, _resident((1, D_S5)), _resident((D_S5, D_S5)), _resident((1, D_S5))],
        out_specs=t, out_shape=SDS((B, L, D_S5), f32), compiler_params=_cparams(2),
    )(u, y0, y1, z, dsk, wg, bg)


def s5_glu_bwd(u, y0, y1, z, dsk, wg, bg, dout):
    B, L, _ = u.shape
    T = min(TOK_TILE, L)

    def body(u_ref, y0_ref, y1_ref, z_ref, dsk_ref, wg_ref, bg_ref, do_ref, du_ref, dy_ref, dz_ref,
             ddsk_ref, dwg_ref, dbg_ref):
        @pl.when(_first_step())
        def _():
            for r in (ddsk_ref, dwg_ref, dbg_ref):
                r[...] = jnp.zeros_like(r)

        _, vjp = jax.vjp(_glu_fn, u_ref[...], y0_ref[...], y1_ref[...], z_ref[...], dsk_ref[...],
                         wg_ref[...].astype(f32), bg_ref[...])
        du, dy, _, dz, ddsk, dwg, dbg = vjp(do_ref[...])
        du_ref[...], dy_ref[...], dz_ref[...] = du, dy, dz
        ddsk_ref[...] += ddsk
        dwg_ref[...] += dwg
        dbg_ref[...] += dbg

    t = _tok(T, D_S5)
    small = [_resident((1, D_S5)), _resident((D_S5, D_S5)), _resident((1, D_S5))]
    return pl.pallas_call(
        body, name="s5_glu_bwd", grid=(B, L // T),
        in_specs=[t, t, t, t] + small + [t], out_specs=[t, t, t] + small,
        out_shape=[SDS((B, L, D_S5), f32)] * 3 + [SDS((1, D_S5), f32), SDS((D_S5, D_S5), f32), SDS((1, D_S5), f32)],
        compiler_params=_cparams(2),
    )(u, y0, y1, z, dsk, wg, bg, dout)


CONV_ROWS = 16


def _conv_taps(L, is_ctx):
    t = lax.broadcasted_iota(jnp.int32, (L, 1), 0)
    taps = []
    for di in ((1,) if is_ctx else (0, 1, 2)):
        for dj in (0, 1, 2):
            s = (0 if is_ctx else GRID_W * (di - 1)) + (dj - 1)
            if is_ctx:
                ok = jnp.logical_and(t + s >= 0, t + s < L)
            else:
                col = jnp.bitwise_and(t, GRID_W - 1) + (dj - 1)
                row = t + GRID_W * (di - 1)
                ok = jnp.logical_and(jnp.logical_and(col >= 0, col < GRID_W), jnp.logical_and(row >= 0, row < L))
            taps.append((di * 3 + dj, s, ok.astype(f32)))
    return taps


def _shift(x, s):
    L = x.shape[0]
    k = (-s) % L
    return x if k == 0 else pltpu.roll(x, k, axis=0)


def _qk_post(pre, is_norm, scale):
    s = _silu(pre)
    nrm = lax.rsqrt(jnp.sum(s * s, axis=-1, keepdims=True) + NORM_EPS)
    return s * jnp.where(is_norm, nrm * scale, 1.0)


def _conv_kind():
    ct = pl.program_id(1)
    return ct < 2 * GDN_HEADS, jnp.where(ct < GDN_HEADS, GDN_HEAD ** -0.5, 1.0).astype(f32)


def _conv_pre(xv, w_ref, taps):
    pre = jnp.zeros_like(xv)
    for r, s, m in taps:
        pre = pre + w_ref[r:r + 1, :] * (m * _shift(xv, s))
    return pre


def conv_fwd(qkv, w16, *, is_ctx, name):
    B, L, C = qkv.shape
    spec = pl.BlockSpec((None, L, GDN_HEAD), lambda b, ct: (b, 0, ct))
    wspec = pl.BlockSpec((CONV_ROWS, GDN_HEAD), lambda b, ct: (0, ct))

    def body(x_ref, w_ref, o_ref):
        is_norm, scale = _conv_kind()
        o_ref[...] = _qk_post(_conv_pre(x_ref[...], w_ref, _conv_taps(L, is_ctx)), is_norm, scale)

    return pl.pallas_call(body, name=name, grid=(B, C // GDN_HEAD), in_specs=[spec, wspec], out_specs=spec,
                          out_shape=SDS((B, L, C), f32), compiler_params=_cparams(2))(qkv, w16)


def conv_bwd(qkv, w16, da0, da1, *, is_ctx, name):
    B, L, C = qkv.shape
    spec = pl.BlockSpec((None, L, GDN_HEAD), lambda b, ct: (b, 0, ct))
    wspec = pl.BlockSpec((CONV_ROWS, GDN_HEAD), lambda b, ct: (0, ct))
    dwspec = pl.BlockSpec((None, CONV_ROWS, GDN_HEAD), lambda b, ct: (b, 0, ct))

    def body(x_ref, w_ref, d0_ref, d1_ref, dx_ref, dw_ref):
        is_norm, scale = _conv_kind()
        taps = _conv_taps(L, is_ctx)
        xv = x_ref[...]
        _, vjp = jax.vjp(lambda p: _qk_post(p, is_norm, scale), _conv_pre(xv, w_ref, taps))
        dpre = vjp(d0_ref[...] + d1_ref[...])[0]
        dx = jnp.zeros_like(xv)
        dw_ref[...] = jnp.zeros_like(dw_ref)
        for r, s, m in taps:
            md = m * dpre
            dx = dx + _shift(w_ref[r:r + 1, :] * md, -s)
            dw_ref[r:r + 1, :] = jnp.sum(md * _shift(xv, s), axis=0, keepdims=True)
        dx_ref[...] = dx

    return pl.pallas_call(body, name=name, grid=(B, C // GDN_HEAD), in_specs=[spec, wspec, spec, spec],
                          out_specs=[spec, dwspec], out_shape=[SDS((B, L, C), f32), SDS((B, CONV_ROWS, C), f32)],
                          compiler_params=_cparams(2))(qkv, w16, da0, da1)


def _gates_fn(ba, alog, dtb):
    T = ba.shape[0]
    lane = lax.broadcasted_iota(jnp.int32, ba.shape, 1)
    ii = lax.broadcasted_iota(jnp.int32, (T, T), 0)
    jj = lax.broadcasted_iota(jnp.int32, (T, T), 1)
    same = jnp.right_shift(ii, 6) == jnp.right_shift(jj, 6)
    lmat = jnp.logical_and(same, ii >= jj).astype(f32)
    umat = jnp.logical_and(same, ii <= jj).astype(f32)
    g = jnp.where(lane >= 8, -jnp.exp(alog) * jax.nn.softplus(ba + dtb), 0.0)
    gc = jnp.where(lane >= 12, _dot_hi(umat, g), _dot_hi(lmat, g))
    return jnp.where(lane < 8, jax.nn.sigmoid(ba), gc)


def gates_fwd(ba, alog, dtb, *, name):
    B, L, _ = ba.shape
    T = min(TOK_TILE, L)
    t = _tok(T, N_GATE)

    def body(ba_ref, al_ref, dt_ref, o_ref):
        o_ref[...] = _gates_fn(ba_ref[...], al_ref[...], dt_ref[...])

    return pl.pallas_call(body, name=name, grid=(B, L // T),
                          in_specs=[t, _resident((1, N_GATE)), _resident((1, N_GATE))], out_specs=t,
                          out_shape=SDS((B, L, N_GATE), f32), compiler_params=_cparams(2))(ba, alog, dtb)


def gates_bwd(ba, alog, dtb, dbg, *, name):
    B, L, _ = ba.shape
    T = min(TOK_TILE, L)
    t = _tok(T, N_GATE)
    small = _resident((1, N_GATE))

    def body(ba_ref, al_ref, dt_ref, d_ref, dba_ref, dal_ref, ddt_ref):
        @pl.when(_first_step())
        def _():
            dal_ref[...] = jnp.zeros_like(dal_ref)
            ddt_ref[...] = jnp.zeros_like(ddt_ref)

        _, vjp = jax.vjp(_gates_fn, ba_ref[...], al_ref[...], dt_ref[...])
        dba, dal, ddt = vjp(d_ref[...])
        dba_ref[...] = dba
        dal_ref[...] += dal
        ddt_ref[...] += ddt

    return pl.pallas_call(body, name=name, grid=(B, L // T), in_specs=[t, small, small, t],
                          out_specs=[t, small, small],
                          out_shape=[SDS((B, L, N_GATE), f32), SDS((1, N_GATE), f32), SDS((1, N_GATE), f32)],
                          compiler_params=_cparams(2))(ba, alog, dtb, dbg)


@jax.custom_vjp
def _inv_unit_tri(mats):
    n = mats[0].shape[0]
    eye = (lax.broadcasted_iota(jnp.int32, (n, n), 0) == lax.broadcasted_iota(jnp.int32, (n, n), 1)).astype(f32)
    xs = [eye - a for a in mats]
    ps = [_dot_h3(a, a) for a in mats]
    k = 2
    while k < n:
        xs = [x + _dot_h3(x, p) for x, p in zip(xs, ps)]
        k *= 2
        if k < n:
            ps = [_dot_h3(p, p) for p in ps]
    return tuple(xs)


def _inv_unit_tri_fwd(mats):
    xs = _inv_unit_tri(mats)
    return xs, xs


def _inv_unit_tri_bwd(xs, dxs):
    xts = [x.T for x in xs]
    ts = [_dot_h3(xt, dx) for xt, dx in zip(xts, dxs)]
    return (tuple(-_dot_h3(t, xt) for t, xt in zip(ts, xts)),)


_inv_unit_tri.defvjp(_inv_unit_tri_fwd, _inv_unit_tri_bwd)


def _gdn_chunk(heads, *, rev):
    n = heads[0][0].shape[0]
    ii = lax.broadcasted_iota(jnp.int32, (n, n), 0)
    jj = lax.broadcasted_iota(jnp.int32, (n, n), 1)
    lower = (ii <= jj) if rev else (ii >= jj)
    strict = (ii < jj) if rev else (ii > jj)
    last = 0 if rev else n - 1
    row = lax.broadcasted_iota(jnp.int32, (n, 1), 0)
    H = range(len(heads))
    q, k, v, beta, gc, gr, s = (list(t) for t in zip(*heads))
    decay = [jnp.where(lower, jnp.exp(jnp.where(lower, gc[h] - gr[h], 0.0)), 0.0) for h in H]
    kk = [_mm_nt(k[h], k[h]) for h in H]
    qk = [_mm_nt(q[h], k[h]) * decay[h] for h in H]
    qs = [_mm(q[h], s[h]) for h in H]
    a_mat = tuple(jnp.where(strict, beta[h] * kk[h] * decay[h], 0.0) for h in H)
    gamma = [jnp.exp(gc[h]) for h in H]
    g_last = [jnp.sum(jnp.where(row == last, gc[h], 0.0), axis=0, keepdims=True) for h in H]
    tinv = _inv_unit_tri(a_mat)
    u0 = [_dot_h3(tinv[h], beta[h] * v[h]) for h in H]
    w = [_dot_h3(tinv[h], (beta[h] * gamma[h]) * k[h]) for h in H]
    k_out = [k[h] * jnp.exp(g_last[h] - gc[h]) for h in H]
    u = [u0[h] - _mm(w[h], s[h]) for h in H]
    o = [gamma[h] * qs[h] + _mm(qk[h], u[h]) for h in H]
    s_new = [jnp.exp(g_last[h]) * s[h] + _mm_tn(k_out[h], u[h]) for h in H]
    return tuple((o[h], s_new[h]) for h in H)


def _gdn_specs(B, nc, rev):
    def cidx(n):
        return (nc - 1 - n) if rev else n
    tok = lambda width: pl.BlockSpec((B, CHUNK, width), lambda n: (0, cidx(n), 0))
    rowspec = pl.BlockSpec((B, None, N_GATE, CHUNK), lambda n: (0, cidx(n), 0, 0))
    st = pl.BlockSpec((B, GDN_HEADS, GDN_HEAD, GDN_HEAD), lambda n: (0, 0, 0, 0))
    ck = pl.BlockSpec((B, None, GDN_HEADS, GDN_HEAD, GDN_HEAD), lambda n: (0, cidx(n), 0, 0, 0))
    return tok, rowspec, st, ck


def _gdn_head_args(qkv_ref, bg_ref, bgr_ref, b, d, h):
    col = d * GDN_HEADS + h
    q = qkv_ref[b, :, h * GDN_HEAD:(h + 1) * GDN_HEAD]
    k = qkv_ref[b, :, D_GDN + h * GDN_HEAD:D_GDN + (h + 1) * GDN_HEAD]
    v = qkv_ref[b, :, 2 * D_GDN + h * GDN_HEAD:2 * D_GDN + (h + 1) * GDN_HEAD]
    bgv = bg_ref[b]
    return q, k, v, bgv[:, col:col + 1], bgv[:, 8 + col:9 + col], bgr_ref[b][8 + col:9 + col, :]


def gdn_fwd(qkv, bg, bgr, s0, *, d, need_o, name):
    B, L, _ = qkv.shape
    nc = L // CHUNK
    rev = d == 1
    tok, rowspec, st, ck = _gdn_specs(B, nc, rev)
    bh = [(b, h) for b in range(B) for h in range(GDN_HEADS)]

    def body(qkv_ref, bg_ref, bgr_ref, s0_ref, *rest):
        if need_o:
            o_ref, ck_ref, sf_ref, s_scr = rest
        else:
            ck_ref, sf_ref, s_scr = rest
        n = pl.program_id(0)

        @pl.when(n == 0)
        def _():
            s_scr[...] = s0_ref[...]

        ck_ref[...] = s_scr[...]
        heads = tuple(_gdn_head_args(qkv_ref, bg_ref, bgr_ref, b, d, h) + (s_scr[b, h],) for b, h in bh)
        for (b, h), (o, s_new) in zip(bh, _gdn_chunk(heads, rev=rev)):
            if need_o:
                o_ref[b, :, h * GDN_HEAD:(h + 1) * GDN_HEAD] = o
            s_scr[b, h] = s_new

        @pl.when(n == nc - 1)
        def _():
            sf_ref[...] = s_scr[...]

    out_specs = [ck, st]
    out_shape = [SDS((B, nc, GDN_HEADS, GDN_HEAD, GDN_HEAD), f32), SDS((B, GDN_HEADS, GDN_HEAD, GDN_HEAD), f32)]
    if need_o:
        out_specs.insert(0, tok(D_GDN))
        out_shape.insert(0, SDS((B, L, D_GDN), f32))
    return pl.pallas_call(
        body, name=name, grid=(nc,), in_specs=[tok(3 * D_GDN), tok(N_GATE), rowspec, st],
        out_specs=out_specs, out_shape=out_shape,
        scratch_shapes=[pltpu.VMEM((B, GDN_HEADS, GDN_HEAD, GDN_HEAD), f32)], compiler_params=_cparams(1),
    )(qkv, bg, bgr, s0)


def gdn_bwd(qkv, bg, bgr, ck, do, dsf, *, d, name):
    B, L, _ = qkv.shape
    nc = L // CHUNK
    rev = d == 1
    has_do = do is not None
    tok, rowspec, st, ckspec = _gdn_specs(B, nc, not rev)
    bh = [(b, h) for b in range(B) for h in range(GDN_HEADS)]

    def body(*refs):
        qkv_ref, bg_ref, bgr_ref, ck_ref = refs[:4]
        k = 4
        if has_do:
            do_ref = refs[4]
            k = 5
        dsf_ref, dqkv_ref, dbg_ref, dbgr_ref, ds0_ref, ds_scr = refs[k:]
        n = pl.program_id(0)

        @pl.when(n == 0)
        def _():
            ds_scr[...] = dsf_ref[...]

        lane = lax.broadcasted_iota(jnp.int32, (CHUNK, N_GATE), 1)
        sub = lax.broadcasted_iota(jnp.int32, (N_GATE, CHUNK), 0)
        heads = tuple(_gdn_head_args(qkv_ref, bg_ref, bgr_ref, b, d, h) + (ck_ref[b, h],) for b, h in bh)
        _, vjp = jax.vjp(functools.partial(_gdn_chunk, rev=rev), heads)
        cts = tuple(((do_ref[b, :, h * GDN_HEAD:(h + 1) * GDN_HEAD] if has_do else jnp.zeros((CHUNK, GDN_HEAD), f32)),
                     ds_scr[b, h]) for b, h in bh)
        (dheads,) = vjp(cts)
        dbg_acc = [jnp.zeros((CHUNK, N_GATE), f32) for _ in range(B)]
        dbgr_acc = [jnp.zeros((N_GATE, CHUNK), f32) for _ in range(B)]
        for (b, h), (dq, dk, dv, db, dgc, dgr, ds) in zip(bh, dheads):
            col = d * GDN_HEADS + h
            dqkv_ref[b, :, h * GDN_HEAD:(h + 1) * GDN_HEAD] = dq
            dqkv_ref[b, :, D_GDN + h * GDN_HEAD:D_GDN + (h + 1) * GDN_HEAD] = dk
            dqkv_ref[b, :, 2 * D_GDN + h * GDN_HEAD:2 * D_GDN + (h + 1) * GDN_HEAD] = dv
            dbg_acc[b] = dbg_acc[b] + jnp.where(lane == col, db, 0.0) + jnp.where(lane == 8 + col, dgc, 0.0)
            dbgr_acc[b] = dbgr_acc[b] + jnp.where(sub == 8 + col, dgr, 0.0)
            ds_scr[b, h] = ds
        for b in range(B):
            dbg_ref[b] = dbg_acc[b]
            dbgr_ref[b] = dbgr_acc[b]

        @pl.when(n == nc - 1)
        def _():
            ds0_ref[...] = ds_scr[...]

    in_specs = [tok(3 * D_GDN), tok(N_GATE), rowspec, ckspec] + ([tok(D_GDN)] if has_do else []) + [st]
    args = [qkv, bg, bgr, ck] + ([do] if has_do else []) + [dsf]
    return pl.pallas_call(
        body, name=name, grid=(nc,), in_specs=in_specs,
        out_specs=[tok(3 * D_GDN), tok(N_GATE), rowspec, st],
        out_shape=[SDS((B, L, 3 * D_GDN), f32), SDS((B, L, N_GATE), f32), SDS((B, nc, N_GATE, CHUNK), f32),
                   SDS((B, GDN_HEADS, GDN_HEAD, GDN_HEAD), f32)],
        scratch_shapes=[pltpu.VMEM((B, GDN_HEADS, GDN_HEAD, GDN_HEAD), f32)], compiler_params=_cparams(1),
    )(*args)


def _gnorm_fn(o0, o1, z, w):
    o = o0 + o1
    return o * lax.rsqrt(jnp.mean(o * o, axis=-1, keepdims=True) + NORM_EPS) * w * _silu(z)


def gnorm_fwd(o0, o1, z, w):
    B, L, _ = o0.shape
    T = min(TOK_TILE, L)
    t = _tok(T, D_GDN)

    def body(o0_ref, o1_ref, z_ref, w_ref, out_ref):
        for h in range(GDN_HEADS):
            sl = slice(h * GDN_HEAD, (h + 1) * GDN_HEAD)
            out_ref[:, sl] = _gnorm_fn(o0_ref[:, sl], o1_ref[:, sl], z_ref[:, sl], w_ref[...])

    return pl.pallas_call(body, name="gnorm_fwd", grid=(B, L // T), in_specs=[t, t, t, _resident((1, GDN_HEAD))],
                          out_specs=t, out_shape=SDS((B, L, D_GDN), f32), compiler_params=_cparams(2))(o0, o1, z, w)


def gnorm_bwd(o0, o1, z, w, dout):
    B, L, _ = o0.shape
    T = min(TOK_TILE, L)
    t = _tok(T, D_GDN)

    def body(o0_ref, o1_ref, z_ref, w_ref, d_ref, do_ref, dz_ref, dw_ref):
        @pl.when(_first_step())
        def _():
            dw_ref[...] = jnp.zeros_like(dw_ref)

        for h in range(GDN_HEADS):
            sl = slice(h * GDN_HEAD, (h + 1) * GDN_HEAD)
            _, vjp = jax.vjp(_gnorm_fn, o0_ref[:, sl], o1_ref[:, sl], z_ref[:, sl], w_ref[...])
            do, _, dz, dw = vjp(d_ref[:, sl])
            do_ref[:, sl] = do
            dz_ref[:, sl] = dz
            dw_ref[...] += dw

    return pl.pallas_call(body, name="gnorm_bwd", grid=(B, L // T),
                          in_specs=[t, t, t, _resident((1, GDN_HEAD)), t], out_specs=[t, t, _resident((1, GDN_HEAD))],
                          out_shape=[SDS((B, L, D_GDN), f32), SDS((B, L, D_GDN), f32), SDS((1, GDN_HEAD), f32)],
                          compiler_params=_cparams(2))(o0, o1, z, w, dout)


def _head_loss(y, x, gate, lng, lnb, tgt):
    r = DEEPNORM_ALPHA * x + gate * y
    mu = jnp.mean(r, axis=-1, keepdims=True)
    rc = r - mu
    var = jnp.mean(rc * rc, axis=-1, keepdims=True)
    err = rc * lax.rsqrt(var + LN_EPS) * lng + lnb - tgt
    return (0.5 / D_MODEL) * jnp.sum(jnp.sum(err * err, axis=-1, keepdims=True), axis=0, keepdims=True)


def head_fwd_bwd(s5o, gdo, x, tgt, gate, lng, lnb, ws, wg):
    B, L, _ = x.shape
    T = min(TOK_TILE, L)

    def body(s_ref, g_ref, x_ref, t_ref, gate_ref, lng_ref, lnb_ref, ws_ref, wg_ref,
             loss_ref, ds_ref, dg_ref, gx_ref, dws_ref, dwg_ref, dgate_ref, dlng_ref, dlnb_ref):
        n = pl.program_id(1)

        @pl.when(_first_step())
        def _():
            for r in (dws_ref, dwg_ref, dlng_ref, dlnb_ref):
                r[...] = jnp.zeros_like(r)

        @pl.when(n == 0)
        def _():
            loss_ref[...] = jnp.zeros_like(loss_ref)
            dgate_ref[...] = jnp.zeros_like(dgate_ref)

        sv = s_ref[...].astype(bf16)
        gv = g_ref[...].astype(bf16)
        y = _dot(sv, ws_ref[...]) + _dot(gv, wg_ref[...])
        loss, vjp = jax.vjp(lambda *a: _head_loss(*a, t_ref[...]), y, x_ref[...], gate_ref[...], lng_ref[...],
                            lnb_ref[...])
        dy, dx, dgate, dlng, dlnb = vjp(jnp.ones((1, 1), f32))
        loss_ref[...] += jnp.broadcast_to(loss, loss_ref.shape)
        dyb = dy.astype(bf16)
        ds_ref[...] = _dot_nt(dyb, ws_ref[...])
        dg_ref[...] = _dot_nt(dyb, wg_ref[...])
        gx_ref[...] = dx
        dws_ref[...] += _dot_tn(sv, dyb)
        dwg_ref[...] += _dot_tn(gv, dyb)
        dgate_ref[...] += dgate
        dlng_ref[...] += dlng
        dlnb_ref[...] += dlnb

    half, full = _tok(T, D_S5), _tok(T, D_MODEL)
    row = _resident((1, D_MODEL))
    wsp = _resident((D_S5, D_MODEL))
    return pl.pallas_call(
        body, name="head_fwd_bwd", grid=(B, L // T),
        in_specs=[half, half, full, full, _per_batch(1, D_MODEL), row, row, wsp, wsp],
        out_specs=[_per_batch(8, LANES), half, half, full, wsp, wsp, _per_batch(1, D_MODEL), row, row],
        out_shape=[SDS((B, 8, LANES), f32), SDS((B, L, D_S5), f32), SDS((B, L, D_GDN), f32), SDS((B, L, D_MODEL), f32),
                   SDS((D_S5, D_MODEL), f32), SDS((D_GDN, D_MODEL), f32), SDS((B, 1, D_MODEL), f32),
                   SDS((1, D_MODEL), f32), SDS((1, D_MODEL), f32)],
        compiler_params=_cparams(2),
    )(s5o, gdo, x, tgt, gate, lng, lnb, ws, wg)


def _adamw_math(w, g, m, v):
    nm = ADAM_B1 * m + (1.0 - ADAM_B1) * g
    nv = ADAM_B2 * v + (1.0 - ADAM_B2) * jnp.square(g)
    m_hat = nm / (1.0 - ADAM_B1 ** ADAM_STEP)
    v_hat = nv / (1.0 - ADAM_B2 ** ADAM_STEP)
    return -ADAM_LR * (m_hat / (jnp.sqrt(v_hat) + ADAM_EPS) + ADAM_WD * w), nm, nv


def _row_tile(rows, cap=256):
    for t in range(min(cap, rows), 15, -1):
        if rows % t == 0 and t % 16 == 0:
            return t
    return rows


def adamw_2d(w, g, m, v, *, name):
    R, C = w.shape
    T = _row_tile(R)
    spec = pl.BlockSpec((T, C), lambda i: (i, 0))

    def body(w_ref, g_ref, m_ref, v_ref, d_ref, nm_ref, nv_ref):
        d_ref[...], nm_ref[...], nv_ref[...] = _adamw_math(w_ref[...], g_ref[...], m_ref[...], v_ref[...])

    return pl.pallas_call(body, name=name, grid=(R // T,), in_specs=[spec] * 4, out_specs=[spec] * 3,
                          out_shape=[SDS((R, C), f32)] * 3, compiler_params=_cparams(1))(w, g, m, v)


def adamw_small(ws, gs, ms, vs):
    n = len(ws)

    def body(*refs):
        outs = refs[4 * n:]
        for i in range(n):
            d, nm, nv = _adamw_math(refs[i][...], refs[n + i][...], refs[2 * n + i][...], refs[3 * n + i][...])
            outs[i][...], outs[n + i][...], outs[2 * n + i][...] = d, nm, nv

    res = pl.pallas_call(body, name="adamw_small", out_shape=[SDS(w.shape, f32) for w in ws] * 3,
                         compiler_params=pltpu.CompilerParams(vmem_limit_bytes=VMEM_LIMIT))(*ws, *gs, *ms, *vs)
    return res[:n], res[n:2 * n], res[2 * n:]


def sum_cores(own, got, *, name):
    A, H, C = own.shape
    T = _row_tile(H)
    spec = pl.BlockSpec((None, T, C), lambda a, i: (a, i, 0))

    def body(a_ref, b_ref, q32_ref, q16_ref):
        q = a_ref[...] + b_ref[...]
        q32_ref[...] = q
        q16_ref[...] = q.astype(bf16)

    return pl.pallas_call(body, name=name, grid=(A, H // T), in_specs=[spec, spec], out_specs=[spec, spec],
                          out_shape=[SDS((A, H, C), f32), SDS((A, H, C), bf16)], compiler_params=_cparams(2))(own, got)


def sum_chips(mine, rec, *, name):
    H, C = mine.shape
    T = _row_tile(H)
    spec = pl.BlockSpec((T, C), lambda i: (i, 0))
    rspec = pl.BlockSpec((3, T, C), lambda i: (0, i, 0))

    def body(m_ref, r_ref, f_ref):
        f_ref[...] = ((m_ref[...] + r_ref[0].astype(f32)) + r_ref[1].astype(f32)) + r_ref[2].astype(f32)

    return pl.pallas_call(body, name=name, grid=(H // T,), in_specs=[spec, rspec], out_specs=spec,
                          out_shape=SDS((H, C), f32), compiler_params=_cparams(1))(mine, rec)


CHIP_FLIPS = ((1, 0), (0, 1), (1, 1))


def _pos():
    return lax.axis_index("x"), lax.axis_index("y"), lax.axis_index("c")


def _comm_call(body, srcs, out_sds, n_remote, n_local, name):
    any_spec = pl.BlockSpec(memory_space=pl.ANY)
    return pl.pallas_call(
        body, name=name, in_specs=[any_spec] * len(srcs), out_specs=[any_spec] * len(out_sds), out_shape=out_sds,
        scratch_shapes=[pltpu.SemaphoreType.DMA((n_remote,)), pltpu.SemaphoreType.DMA((n_remote,)),
                        pltpu.SemaphoreType.DMA((max(n_local, 1),))],
        compiler_params=pltpu.CompilerParams(has_side_effects=True),
    )(*srcs)


def _remote(src, dst, send_sems, recv_sems, k, target):
    return pltpu.make_async_remote_copy(src, dst, send_sems.at[k], recv_sems.at[k], device_id=target,
                                        device_id_type=MESH)


def _half_rows(c, rows):
    half = rows // 2
    return pl.ds(pl.multiple_of(c * half, 8), half)


def gather_shards(shards):
    nt = len(shards)

    def body(*refs):
        srcs, outs = refs[:nt], refs[nt:2 * nt]
        send_sems, recv_sems, loc_sems = refs[2 * nt:]
        x, y, c = _pos()
        j = 2 * x + y
        local = [pltpu.make_async_copy(srcs[t], outs[t].at[j], loc_sems.at[t]) for t in range(nt)]
        for cp in local:
            cp.start()
        first, passed = [], []
        for k, (fx, fy) in enumerate(CHIP_FLIPS):
            tx, ty = x ^ fx, y ^ fy
            jk = 2 * tx + ty
            for t in range(nt):
                rows = _half_rows(c, srcs[t].shape[0])
                first.append(_remote(srcs[t].at[rows], outs[t].at[j, rows], send_sems, recv_sems, 6 * t + k, (tx, ty, c)))
                passed.append(_remote(outs[t].at[jk, rows], outs[t].at[jk, rows], send_sems, recv_sems, 6 * t + 3 + k,
                                      (x, y, 1 - c)))
        for cp in first:
            cp.start()
        for a, b in zip(first, passed):
            a.wait_recv()
            b.start()
        for cp in passed:
            cp.wait_recv()
        for cp in first + passed:
            cp.wait_send()
        for cp in local:
            cp.wait()

    return _comm_call(body, shards, [SDS((4,) + s.shape, s.dtype) for s in shards], 6 * nt, nt, "gather_shards")


def swap_halves(ps):
    nt = len(ps)

    def body(*refs):
        srcs, outs = refs[:nt], refs[nt:2 * nt]
        send_sems, recv_sems, _ = refs[2 * nt:]
        x, y, c = _pos()
        cps = [_remote(srcs[t].at[a, _half_rows(1 - c, srcs[t].shape[1])], outs[t].at[a], send_sems, recv_sems, 4 * t + a,
                       (x, y, 1 - c)) for t in range(nt) for a in range(4)]
        for cp in cps:
            cp.start()
        for cp in cps:
            cp.wait()

    return _comm_call(body, ps, [SDS((4, p.shape[1] // 2, p.shape[2]), p.dtype) for p in ps], 4 * nt, 0, "swap_halves")


def scatter_to_chips(qs):
    nt = len(qs)

    def body(*refs):
        srcs, outs = refs[:nt], refs[nt:2 * nt]
        send_sems, recv_sems, _ = refs[2 * nt:]
        x, y, c = _pos()
        cps = []
        for k, (fx, fy) in enumerate(CHIP_FLIPS):
            tx, ty = x ^ fx, y ^ fy
            for t in range(nt):
                cps.append(_remote(srcs[t].at[2 * tx + ty], outs[t].at[k], send_sems, recv_sems, 3 * t + k, (tx, ty, c)))
        for cp in cps:
            cp.start()
        for cp in cps:
            cp.wait()

    return _comm_call(body, qs, [SDS((3,) + q.shape[1:], q.dtype) for q in qs], 3 * nt, 0, "scatter_to_chips")


def join_halves(fs):
    nt = len(fs)

    def body(*refs):
        srcs, outs = refs[:nt], refs[nt:2 * nt]
        send_sems, recv_sems, loc_sems = refs[2 * nt:]
        x, y, c = _pos()
        local = [pltpu.make_async_copy(srcs[t], outs[t].at[c], loc_sems.at[t]) for t in range(nt)]
        cps = [_remote(srcs[t], outs[t].at[c], send_sems, recv_sems, t, (x, y, 1 - c)) for t in range(nt)]
        for cp in local + cps:
            cp.start()
        for cp in cps + local:
            cp.wait()

    return _comm_call(body, fs, [SDS((2,) + f.shape, f.dtype) for f in fs], nt, nt, "join_halves")


def gather_small(s):
    def body(src, out, send_sems, recv_sems, loc_sems):
        x, y, c = _pos()
        j = 2 * x + y
        mine = pltpu.make_async_copy(src, out.at[j], loc_sems.at[0])
        mine.start()
        cps = [_remote(src, out.at[j], send_sems, recv_sems, k, (x ^ fx, y ^ fy, c)) for k, (fx, fy) in enumerate(CHIP_FLIPS)]
        for cp in cps:
            cp.start()
        for cp in cps:
            cp.wait()
        mine.wait()

    return _comm_call(body, [s], [SDS((4,) + s.shape, s.dtype)], 3, 1, "gather_small")[0]


SMALL_SHAPES = ((D_MODEL,), (1, 3 * D_MODEL), (1, 2, 32, 64), (1, 2, 32, 64), (1, 2, 32), (1, 2, 32, 64, 16),
                (1, 2, 32, 64, 16), (1, 2, 32, 16, 64), (1, 2, 32, 16, 64), (1, D_S5), (1, D_S5), (1, 2, 4), (1, 2, 4),
                (1, GDN_HEAD), (1, D_MODEL), (1, D_MODEL))


def _size(shape):
    return functools.reduce(lambda p, q: p * q, shape)


SMALL_ROWS = tuple(-(-_size(s) // LANES) for s in SMALL_SHAPES)
SMALL_TOTAL = 2176
SMALL_QUARTER = SMALL_TOTAL // 4


def _rows(a):
    flat = a.reshape(-1)
    pad = (-flat.shape[0]) % LANES
    if pad:
        flat = jnp.concatenate([flat, jnp.zeros((pad,), flat.dtype)])
    return flat.reshape(-1, LANES)


def _pack_small(parts):
    rows = [_rows(p) for p in parts]
    rows.append(jnp.zeros((SMALL_TOTAL - sum(SMALL_ROWS), LANES), f32))
    return jnp.concatenate(rows, axis=0)


def _unpack_small(buf):
    out, r = [], 0
    for s, n in zip(SMALL_SHAPES, SMALL_ROWS):
        out.append(buf[r:r + n].reshape(-1)[:_size(s)].reshape(s))
        r += n
    return out


def _as_2d(a):
    return a.reshape(1, -1) if a.ndim == 1 else a.reshape(-1, a.shape[-1])


def _block_diag_in(bb):
    eye = jnp.eye(S5_GROUPS, dtype=bb.dtype)
    b3 = bb.reshape(S5_GROUPS, S5_STATE, S5_GROUP)
    return jnp.einsum('gpc,gh->gchp', b3, eye).reshape(D_S5, S5_HALF)


def _block_diag_in_t(d):
    d5 = d.reshape(S5_GROUPS, S5_GROUP, S5_GROUPS, S5_STATE)
    return jnp.einsum('gcgp->gpc', d5).reshape(S5_GROUPS, S5_STATE * S5_GROUP)


def _block_diag_out(cm):
    eye = jnp.eye(S5_GROUPS, dtype=cm.dtype)
    return jnp.einsum('gcp,gh->hpgc', cm, eye).reshape(S5_HALF, D_S5)


def _block_diag_out_t(d):
    d5 = d.reshape(S5_GROUPS, S5_STATE, S5_GROUPS, S5_GROUP)
    return jnp.einsum('gpgc->gcp', d5)


def _to_chunk_rows(a):
    B, L, W = a.shape
    return a.reshape(B, L // CHUNK, CHUNK, W).transpose(0, 1, 3, 2)


def _from_chunk_rows(a):
    B, nc, W, _ = a.shape
    return a.transpose(0, 1, 3, 2).reshape(B, nc * CHUNK, W)


def local_step(x, c, ctx, c_ctx, tgt, w_ada, b_ada, w_in, lam_re, lam_im, log_dt, b_re, b_im, c_re, c_im, s5_d,
               w_glu, b_glu, conv16, a_log, dt_bias, norm_w, w_out, ln_g, ln_b):
    B, L, _ = x.shape
    zeros_state = jnp.zeros((B, GDN_HEADS, GDN_HEAD, GDN_HEAD), f32)

    cc = jnp.concatenate([c, c_ctx[None, :], jnp.zeros((8 - B - 1, D_MODEL), f32)], axis=0)
    m = ada_fwd(cc, w_ada, b_ada)
    shift, scale, gate = m[:B, :D_MODEL], m[:B, D_MODEL:2 * D_MODEL], m[:B, 2 * D_MODEL:]
    mod = jnp.stack([scale, shift], axis=1)
    mod_c = jnp.broadcast_to(jnp.stack([m[B, D_MODEL:2 * D_MODEL], m[B, :D_MODEL]], axis=0)[None], (B, 2, D_MODEL))

    u, z_s5, qkv, z_gdn, ba = in_proj_fwd(x, mod, w_in, name="in_proj_fwd")
    uc, _, qkvc, _, bac = in_proj_fwd(ctx, mod_c, w_in, name="in_proj_fwd_ctx")

    ng = N_DIR * S5_GROUPS
    zoh_in = (lam_re.reshape(ng, S5_STATE), lam_im.reshape(ng, S5_STATE), log_dt.reshape(ng, 1),
              b_re.reshape(ng, S5_STATE * S5_GROUP), b_im.reshape(ng, S5_STATE * S5_GROUP))
    expand = (jnp.arange(S5_STATE * S5_GROUP)[None, :] // S5_GROUP == jnp.arange(S5_STATE)[:, None]).astype(f32)
    ar, ai, bbr, bbi = s5_zoh_fwd(*zoh_in, expand)
    bbr16, bbi16 = bbr.astype(bf16), bbi.astype(bf16)
    c_re16 = c_re.reshape(N_DIR, S5_GROUPS, S5_GROUP, S5_STATE).astype(bf16)
    c_im16 = (-c_im).reshape(N_DIR, S5_GROUPS, S5_GROUP, S5_STATE).astype(bf16)
    s5w, ys, hins, hins_c = [], [], [], []
    for d in range(N_DIR):
        g = slice(d * S5_GROUPS, (d + 1) * S5_GROUPS)
        wd = (_block_diag_in(bbr16[g]), _block_diag_in(bbi16[g]), _block_diag_out(c_re16[d]), _block_diag_out(c_im16[d]),
              jnp.stack([ar[g].reshape(-1), ai[g].reshape(-1)], axis=0))
        s5w.append(wd)
        hin_c, hend_c = s5_scan_fwd(uc, *wd, jnp.zeros((B, 2, S5_HALF), f32), d=d, need_y=False, name=f"s5_fwd_ctx{d}")
        y_d, hin, _ = s5_scan_fwd(u, *wd, hend_c, d=d, need_y=True, name=f"s5_fwd{d}")
        ys.append(y_d)
        hins.append(hin)
        hins_c.append(hin_c)
    glu_w = (s5_d.reshape(1, D_S5), w_glu, b_glu.reshape(1, D_S5))
    s5o = s5_glu_fwd(u, ys[0], ys[1], z_s5, *glu_w)

    act = conv_fwd(qkv, conv16, is_ctx=False, name="conv_fwd")
    act_c = conv_fwd(qkvc, conv16, is_ctx=True, name="conv_fwd_ctx")
    pad8 = jnp.zeros((1, 8), f32)
    alog16 = jnp.concatenate([pad8, a_log.reshape(1, 8)], axis=1)
    dtb16 = jnp.concatenate([pad8, dt_bias.reshape(1, 8)], axis=1)
    bg = gates_fwd(ba, alog16, dtb16, name="gates_fwd")
    bg_c = gates_fwd(bac, alog16, dtb16, name="gates_fwd_ctx")
    bgr, bgr_c = _to_chunk_rows(bg), _to_chunk_rows(bg_c)
    os_, cks, cks_c = [], [], []
    for d in range(N_DIR):
        ck_c, s_c = gdn_fwd(act_c, bg_c, bgr_c, zeros_state, d=d, need_o=False, name=f"gdn_fwd_ctx{d}")
        o_d, ck, _ = gdn_fwd(act, bg, bgr, s_c, d=d, need_o=True, name=f"gdn_fwd{d}")
        os_.append(o_d)
        cks.append(ck)
        cks_c.append(ck_c)
    nw = norm_w.reshape(1, GDN_HEAD)
    gdo = gnorm_fwd(os_[0], os_[1], z_gdn, nw)

    loss8, ds5o, dgdo, gx_res, dws, dwg, dgate, dlng, dlnb = head_fwd_bwd(
        s5o, gdo, x, tgt, gate[:, None, :], ln_g.reshape(1, D_MODEL), ln_b.reshape(1, D_MODEL), w_out[:D_S5], w_out[D_S5:])
    loss = jnp.sum(loss8[:, 0, 0])
    d_w_out = jnp.concatenate([dws, dwg], axis=0)

    do, dz_gdn, d_norm_w = gnorm_bwd(os_[0], os_[1], z_gdn, nw, dgdo)
    dacts, dacts_c = [], []
    dbg = jnp.zeros_like(bg)
    dbg_c = jnp.zeros_like(bg_c)
    for d in range(N_DIR):
        dact, dbg_d, dbgr_d, ds0 = gdn_bwd(act, bg, bgr, cks[d], do, zeros_state, d=d, name=f"gdn_bwd{d}")
        dact_c, dbgc_d, dbgrc_d, _ = gdn_bwd(act_c, bg_c, bgr_c, cks_c[d], None, ds0, d=d, name=f"gdn_bwd_ctx{d}")
        dacts.append(dact)
        dacts_c.append(dact_c)
        dbg = dbg + dbg_d + _from_chunk_rows(dbgr_d)
        dbg_c = dbg_c + dbgc_d + _from_chunk_rows(dbgrc_d)
    dba, dal, ddt = gates_bwd(ba, alog16, dtb16, dbg, name="gates_bwd")
    dbac, dal_c, ddt_c = gates_bwd(bac, alog16, dtb16, dbg_c, name="gates_bwd_ctx")
    d_a_log = (dal + dal_c)[:, 8:].reshape(1, N_DIR, GDN_HEADS)
    d_dt_bias = (ddt + ddt_c)[:, 8:].reshape(1, N_DIR, GDN_HEADS)
    dqkv, dcw = conv_bwd(qkv, conv16, dacts[0], dacts[1], is_ctx=False, name="conv_bwd")
    dqkvc, dcw_c = conv_bwd(qkvc, conv16, dacts_c[0], dacts_c[1], is_ctx=True, name="conv_bwd_ctx")
    d_conv16 = jnp.sum(dcw, axis=0) + jnp.sum(dcw_c, axis=0)

    du_skip, dy, dz_s5, d_s5_d, d_w_glu, d_b_glu = s5_glu_bwd(u, ys[0], ys[1], z_s5, *glu_w, ds5o)
    du, duc = du_skip, jnp.zeros_like(uc)
    dar, dai, dbbr, dbbi, dcre, dcim = [], [], [], [], [], []
    for d in range(N_DIR):
        du_d, dbre1, dbim1, dct1, dcb1, da1, dh0 = s5_scan_bwd(u, dy, *s5w[d], hins[d],
                                                                jnp.zeros((B, 2, S5_HALF), f32), d=d, name=f"s5_bwd{d}")
        duc_d, dbre2, dbim2, _, _, da2, _ = s5_scan_bwd(uc, None, *s5w[d], hins_c[d], dh0, d=d, name=f"s5_bwd_ctx{d}")
        du, duc = du + du_d, duc + duc_d
        da = da1 + da2
        dar.append(da[0].reshape(S5_GROUPS, S5_STATE))
        dai.append(da[1].reshape(S5_GROUPS, S5_STATE))
        dbbr.append(_block_diag_in_t(dbre1 + dbre2))
        dbbi.append(_block_diag_in_t(dbim1 + dbim2))
        dcre.append(_block_diag_out_t(dct1))
        dcim.append(-_block_diag_out_t(dcb1))
    dlr, dli, dldt, dbre, dbim = s5_zoh_bwd(*zoh_in, expand, jnp.concatenate(dar, 0), jnp.concatenate(dai, 0),
                                            jnp.concatenate(dbbr, 0), jnp.concatenate(dbbi, 0))
    d_s5 = (dlr, dli, dldt, dbre, dbim, jnp.stack(dcre, 0), jnp.stack(dcim, 0))

    padg = lambda a: jnp.concatenate([a, jnp.zeros(a.shape[:2] + (LANES - N_GATE,), f32)], axis=2)
    dw_l, dmod, grad_x = in_proj_bwd(x, mod, (du, dz_s5, dqkv, dz_gdn, padg(dba)), w_in, gx_res, name="in_proj_bwd")
    zc = jnp.zeros_like(uc)
    dw_c, dmod_c = in_proj_bwd(ctx, mod_c, (duc, zc, dqkvc, zc, padg(dbac)), w_in, None, name="in_proj_bwd_ctx")
    d_w_in = dw_l + dw_c
    dmod_c = jnp.sum(dmod_c, axis=0)

    dm_rows = jnp.concatenate([dmod[:, 1], dmod[:, 0], dgate[:, 0]], axis=1)
    dm_ctx = jnp.concatenate([dmod_c[1], dmod_c[0], jnp.zeros((D_MODEL,), f32)])[None]
    dm = jnp.concatenate([dm_rows, dm_ctx, jnp.zeros((8 - B - 1, 3 * D_MODEL), f32)], axis=0)
    dcc, d_w_ada, d_b_ada = ada_bwd(cc, w_ada, dm)
    small = (dcc[B], d_b_ada, *d_s5, d_s5_d, d_b_glu, d_a_log, d_dt_bias, d_norm_w, dlng, dlnb)
    small = tuple(g.reshape(s) for g, s in zip(small, SMALL_SHAPES))
    return loss, grad_x, (d_w_ada, d_w_in, d_w_out, d_w_glu, d_conv16), small


SHARDED = (1, 3, 18, 12, 14)
SMALL = tuple(i for i in range(21) if i not in SHARDED)
W_IN_SHARD = 772


def _conv_rows(w):
    return jnp.concatenate([w.reshape(9, w.shape[-1]), jnp.zeros((CONV_ROWS - 9, w.shape[-1]), f32)], axis=0)


def kernel(x, c, ctx, c_ctx, w_ada, b_ada, w_in, s5_lambda_re, s5_lambda_im, s5_log_dt, s5_b_re, s5_b_im, s5_c_re, s5_c_im, s5_d, w_glu, b_glu, conv_w, gdn_a_log, gdn_dt_bias, gdn_norm_w, w_out, ln_g, ln_b, loss_target, m_c_ctx, m_w_ada, m_b_ada, m_w_in, m_s5_lambda_re, m_s5_lambda_im, m_s5_log_dt, m_s5_b_re, m_s5_b_im, m_s5_c_re, m_s5_c_im, m_s5_d, m_w_glu, m_b_glu, m_conv_w, m_gdn_a_log, m_gdn_dt_bias, m_gdn_norm_w, m_w_out, m_ln_g, m_ln_b, v_c_ctx, v_w_ada, v_b_ada, v_w_in, v_s5_lambda_re, v_s5_lambda_im, v_s5_log_dt, v_s5_b_re, v_s5_b_im, v_s5_c_re, v_s5_c_im, v_s5_d, v_w_glu, v_b_glu, v_conv_w, v_gdn_a_log, v_gdn_dt_bias, v_gdn_norm_w, v_w_out, v_ln_g, v_ln_b):
    weights = [c_ctx, w_ada, b_ada, w_in, s5_lambda_re, s5_lambda_im, s5_log_dt, s5_b_re, s5_b_im, s5_c_re, s5_c_im,
               s5_d, w_glu, b_glu, conv_w, gdn_a_log, gdn_dt_bias, gdn_norm_w, w_out, ln_g, ln_b]
    ms = [m_c_ctx, m_w_ada, m_b_ada, m_w_in, m_s5_lambda_re, m_s5_lambda_im, m_s5_log_dt, m_s5_b_re, m_s5_b_im,
          m_s5_c_re, m_s5_c_im, m_s5_d, m_w_glu, m_b_glu, m_conv_w, m_gdn_a_log, m_gdn_dt_bias, m_gdn_norm_w, m_w_out,
          m_ln_g, m_ln_b]
    vs = [v_c_ctx, v_w_ada, v_b_ada, v_w_in, v_s5_lambda_re, v_s5_lambda_im, v_s5_log_dt, v_s5_b_re, v_s5_b_im,
          v_s5_c_re, v_s5_c_im, v_s5_d, v_w_glu, v_b_glu, v_conv_w, v_gdn_a_log, v_gdn_dt_bias, v_gdn_norm_w, v_w_out,
          v_ln_g, v_ln_b]
    cpos = lax.axis_index("c")
    jchip = 2 * lax.axis_index("x") + lax.axis_index("y")

    conv_shard = _conv_rows(conv_w)
    g_ada, g_in, g_out, g_glu, g_conv = gather_shards(
        [w_ada[0].astype(bf16), w_in[0].astype(bf16), w_out[0].astype(bf16), w_glu[0].astype(bf16), conv_shard])
    w_in_pad = jnp.concatenate([g_in[0], g_in[1], g_in[2], g_in[3], jnp.zeros((D_MODEL, IN_PAD - P_IN), bf16)], axis=1)
    conv16 = g_conv.transpose(1, 0, 2).reshape(CONV_ROWS, 3 * D_GDN)

    loss, grad_x, big, small = local_step(
        x, c, ctx, c_ctx, loss_target, g_ada, b_ada, w_in_pad, s5_lambda_re, s5_lambda_im, s5_log_dt, s5_b_re, s5_b_im,
        s5_c_re, s5_c_im, s5_d, g_glu.reshape(D_S5, D_S5), b_glu, conv16, gdn_a_log, gdn_dt_bias, gdn_norm_w,
        g_out.reshape(D_MODEL, D_MODEL), ln_g, ln_b)
    loss = lax.psum(loss, ("x", "y", "c"))

    d_w_ada, d_w_in, d_w_out, d_w_glu, d_conv16 = big
    slabs = [d_w_ada,
             d_w_in[:, :P_IN].reshape(D_MODEL, 4, W_IN_SHARD).transpose(1, 0, 2),
             d_w_out.reshape(4, D_MODEL // 4, D_MODEL),
             d_w_glu.reshape(4, D_S5 // 4, D_S5),
             d_conv16.reshape(CONV_ROWS, 4, 3 * D_GDN // 4).transpose(1, 0, 2),
             _pack_small(small).reshape(4, SMALL_QUARTER, LANES)]
    got = swap_halves(slabs)
    q32, q16 = [], []
    for t, (s, g) in enumerate(zip(slabs, got)):
        own = lax.dynamic_index_in_dim(s.reshape(4, 2, s.shape[1] // 2, s.shape[2]), cpos, axis=1, keepdims=False)
        a, b = sum_cores(own, g, name=f"sum_cores{t}")
        q32.append(a)
        q16.append(b)
    rec = scatter_to_chips(q16)
    fs = [sum_chips(lax.dynamic_index_in_dim(q, jchip, axis=0, keepdims=False), r, name=f"sum_chips{t}")
          for t, (q, r) in enumerate(zip(q32, rec))]
    red = [r.reshape(2 * r.shape[1], r.shape[2]) for r in join_halves(fs)]
    g_small = _unpack_small(gather_small(red[5]).reshape(SMALL_TOTAL, LANES))

    grads, deltas, new_m, new_v = [None] * 21, [None] * 21, [None] * 21, [None] * 21
    for t, i in enumerate(SHARDED):
        conv = i == 14
        prep = _conv_rows if conv else (lambda a: a[0])
        d, nm, nv = adamw_2d(prep(weights[i]), red[t], prep(ms[i]), prep(vs[i]), name=f"adamw{t}")
        for lst, val in ((grads, red[t]), (deltas, d), (new_m, nm), (new_v, nv)):
            lst[i] = (val[:9] if conv else val).reshape(weights[i].shape)
    sm = adamw_small([_as_2d(weights[i]) for i in SMALL], [_as_2d(g) for g in g_small], [_as_2d(ms[i]) for i in SMALL],
                     [_as_2d(vs[i]) for i in SMALL])
    for n, i in enumerate(SMALL):
        grads[i] = g_small[n]
        for lst, res in ((deltas, sm[0]), (new_m, sm[1]), (new_v, sm[2])):
            lst[i] = res[n].reshape(weights[i].shape)
    return (loss, grad_x, *grads, *deltas, *new_m, *new_v)
```

```python
import functools

import jax
import jax.numpy as jnp
from jax import lax
from jax.experimental import pallas as pl
from jax.experimental.pallas import tpu as pltpu

f32 = jnp.float32
bf16 = jnp.bfloat16
SDS = jax.ShapeDtypeStruct

D_MODEL = 1024
D_S5 = 512
S5_GROUP = 16
S5_GROUPS = 32
S5_STATE = 64
S5_HALF = S5_GROUPS * S5_STATE
D_GDN = 512
GDN_HEAD = 128
GDN_HEADS = 4
CHUNK = 64
GRID_W = 64
N_DIR = 2
P_IN = 3088
DEEPNORM_ALPHA = 2.0 ** 0.25
LN_EPS = 1e-5
NORM_EPS = 1e-6
ADAM_LR, ADAM_B1, ADAM_B2, ADAM_EPS, ADAM_WD, ADAM_STEP = 0.001, 0.9, 0.999, 1e-08, 0.01, 10

LANES = 128
VMEM_LIMIT = 56 * 1024 * 1024
TOK_TILE = 256
S5_TILE = 256
MESH = pl.DeviceIdType.MESH


def _cparams(n_grid):
    return pltpu.CompilerParams(dimension_semantics=("arbitrary",) * n_grid, vmem_limit_bytes=VMEM_LIMIT)


def _dot(a, b):
    return jnp.dot(a.astype(bf16), b.astype(bf16), preferred_element_type=f32)


def _dot_nt(a, b):
    return lax.dot_general(a.astype(bf16), b.astype(bf16), (((1,), (1,)), ((), ())), preferred_element_type=f32)


def _dot_tn(a, b):
    return lax.dot_general(a.astype(bf16), b.astype(bf16), (((0,), (0,)), ((), ())), preferred_element_type=f32)


def _dot_hi(a, b):
    return jnp.dot(a, b, precision=lax.Precision.HIGHEST, preferred_element_type=f32)


def _dot_h3(a, b):
    return jnp.dot(a, b, precision=lax.Precision.HIGH, preferred_element_type=f32)


@jax.custom_vjp
def _mm(a, b):
    return _dot(a, b)


@jax.custom_vjp
def _mm_nt(a, b):
    return _dot_nt(a, b)


@jax.custom_vjp
def _mm_tn(a, b):
    return _dot_tn(a, b)


_mm.defvjp(lambda a, b: (_dot(a, b), (a, b)), lambda r, g: (_mm_nt(g, r[1]), _mm_tn(r[0], g)))
_mm_nt.defvjp(lambda a, b: (_dot_nt(a, b), (a, b)), lambda r, g: (_mm(g, r[1]), _mm_tn(g, r[0])))
_mm_tn.defvjp(lambda a, b: (_dot_tn(a, b), (a, b)), lambda r, g: (_mm_nt(r[1], g), _mm(r[0], g)))


def _silu(x):
    return x * jax.nn.sigmoid(x)


def _gelu(x):
    return 0.5 * x * (1.0 + lax.erf(x * (2.0 ** -0.5)))


def _resident(shape):
    nd = len(shape)
    return pl.BlockSpec(shape, lambda *_: (0,) * nd, pipeline_mode=pl.Buffered(1))


def _tok(tile, width, nt=None, rev=False):
    if rev:
        return pl.BlockSpec((None, tile, width), lambda b, n: (b, nt - 1 - n, 0))
    return pl.BlockSpec((None, tile, width), lambda b, n: (b, n, 0))


def _per_batch(rows, width):
    return pl.BlockSpec((None, rows, width), lambda b, n: (b, 0, 0))


def _first_step():
    return jnp.logical_and(pl.program_id(0) == 0, pl.program_id(1) == 0)


ADA_SHARD = 3 * D_MODEL // 4


def ada_fwd(cc, w, b):
    def body(cc_ref, w_ref, b_ref, m_ref):
        s = _silu(cc_ref[...]).astype(bf16)
        for j in range(4):
            sl = slice(j * ADA_SHARD, (j + 1) * ADA_SHARD)
            m_ref[:, sl] = _dot(s, w_ref[j]) + b_ref[:, sl]

    return pl.pallas_call(body, name="ada_fwd", out_shape=SDS((8, 3 * D_MODEL), f32),
                          compiler_params=pltpu.CompilerParams(vmem_limit_bytes=VMEM_LIMIT))(cc, w, b)


def ada_bwd(cc, w, dm):
    def body(cc_ref, w_ref, dm_ref, dcc_ref, dw_ref, db_ref):
        s, vjp = jax.vjp(_silu, cc_ref[...])
        ds = jnp.zeros((8, D_MODEL), f32)
        for j in range(4):
            dmj = dm_ref[:, j * ADA_SHARD:(j + 1) * ADA_SHARD]
            ds = ds + _dot_nt(dmj, w_ref[j])
            dw_ref[j] = _dot_tn(s, dmj)
        dcc_ref[...] = vjp(ds)[0]
        db_ref[...] = jnp.sum(dm_ref[...], axis=0, keepdims=True)

    return pl.pallas_call(
        body, name="ada_bwd",
        out_shape=[SDS((8, D_MODEL), f32), SDS((4, D_MODEL, ADA_SHARD), f32), SDS((1, 3 * D_MODEL), f32)],
        compiler_params=pltpu.CompilerParams(vmem_limit_bytes=VMEM_LIMIT))(cc, w, dm)


N_GATE = 2 * N_DIR * GDN_HEADS
IN_WIDTHS = (D_S5, D_S5, 3 * D_GDN, D_GDN, LANES)
IN_OFFS = (0, 512, 1024, 2560, 3072)
IN_PAD = 3200


def in_proj_fwd(x, mod, w, *, name):
    B, L, _ = x.shape
    T = min(TOK_TILE, L)

    def body(x_ref, mod_ref, w_ref, *o_refs):
        h = (x_ref[...] * (1.0 + mod_ref[0:1, :]) + mod_ref[1:2, :]).astype(bf16)
        for o_ref, off, wd in zip(o_refs, IN_OFFS, IN_WIDTHS):
            r = _dot(h, w_ref[:, off:off + wd])
            o_ref[...] = r[:, :o_ref.shape[-1]]

    outw = (D_S5, D_S5, 3 * D_GDN, D_GDN, N_GATE)
    return pl.pallas_call(
        body, name=name, grid=(B, L // T),
        in_specs=[_tok(T, D_MODEL), _per_batch(2, D_MODEL), _resident((D_MODEL, IN_PAD))],
        out_specs=[_tok(T, wd) for wd in outw],
        out_shape=[SDS((B, L, wd), f32) for wd in outw],
        compiler_params=_cparams(2),
    )(x, mod, w)


def in_proj_bwd(x, mod, ds, w, gx_res, *, name):
    B, L, _ = x.shape
    T = min(TOK_TILE, L)
    with_dx = gx_res is not None

    def body(*refs):
        x_ref, mod_ref = refs[0], refs[1]
        d_refs = refs[2:7]
        w_ref = refs[7]
        k = 8
        if with_dx:
            gx_ref = refs[k]
            k += 1
        dw_ref, dmod_ref = refs[k], refs[k + 1]
        if with_dx:
            dx_ref = refs[k + 2]
        n = pl.program_id(1)

        @pl.when(_first_step())
        def _():
            dw_ref[...] = jnp.zeros_like(dw_ref)

        @pl.when(n == 0)
        def _():
            dmod_ref[...] = jnp.zeros_like(dmod_ref)

        xv = x_ref[...]
        scale1 = 1.0 + mod_ref[0:1, :]
        h = (xv * scale1 + mod_ref[1:2, :]).astype(bf16)
        dh = jnp.zeros((T, D_MODEL), f32)
        for d_ref, off, wd in zip(d_refs, IN_OFFS, IN_WIDTHS):
            dv = d_ref[...].astype(bf16)
            dh = dh + _dot_nt(dv, w_ref[:, off:off + wd])
            dw_ref[:, off:off + wd] += _dot_tn(h, dv)
        dmod_ref[0:1, :] += jnp.sum(dh * xv, axis=0, keepdims=True)
        dmod_ref[1:2, :] += jnp.sum(dh, axis=0, keepdims=True)
        if with_dx:
            dx_ref[...] = gx_ref[...] + dh * scale1

    in_specs = ([_tok(T, D_MODEL), _per_batch(2, D_MODEL)] + [_tok(T, wd) for wd in IN_WIDTHS]
                + [_resident((D_MODEL, IN_PAD))])
    args = [x, mod, *ds, w]
    out_specs = [_resident((D_MODEL, IN_PAD)), _per_batch(2, D_MODEL)]
    out_shape = [SDS((D_MODEL, IN_PAD), f32), SDS((B, 2, D_MODEL), f32)]
    if with_dx:
        in_specs.append(_tok(T, D_MODEL))
        args.append(gx_res)
        out_specs.append(_tok(T, D_MODEL))
        out_shape.append(SDS((B, L, D_MODEL), f32))
    return pl.pallas_call(body, name=name, grid=(B, L // T), in_specs=in_specs, out_specs=out_specs,
                          out_shape=out_shape, compiler_params=_cparams(2))(*args)


def _s5_zoh(lr, li, ldt, bre, bim, expand):
    dt = jnp.exp(ldt)
    zr, zi = lr * dt, li * dt
    e = jnp.exp(zr)
    ar, ai = e * jnp.cos(zi), e * jnp.sin(zi)
    den = lr * lr + li * li
    czr = ((ar - 1.0) * lr + ai * li) / den
    czi = (ai * lr - (ar - 1.0) * li) / den
    czr_e, czi_e = _dot_hi(czr, expand), _dot_hi(czi, expand)
    return ar, ai, czr_e * bre - czi_e * bim, czr_e * bim + czi_e * bre


_ZOH_OUT = [(N_DIR * S5_GROUPS, S5_STATE)] * 2 + [(N_DIR * S5_GROUPS, S5_STATE * S5_GROUP)] * 2


def s5_zoh_fwd(lr, li, ldt, bre, bim, expand):
    def body(lr_ref, li_ref, ldt_ref, bre_ref, bim_ref, e_ref, ar_ref, ai_ref, bbr_ref, bbi_ref):
        ar, ai, bbr, bbi = _s5_zoh(lr_ref[...], li_ref[...], ldt_ref[...], bre_ref[...], bim_ref[...], e_ref[...])
        ar_ref[...], ai_ref[...], bbr_ref[...], bbi_ref[...] = ar, ai, bbr, bbi

    return pl.pallas_call(body, name="s5_zoh_fwd", out_shape=[SDS(s, f32) for s in _ZOH_OUT])(
        lr, li, ldt, bre, bim, expand)


def s5_zoh_bwd(lr, li, ldt, bre, bim, expand, dar, dai, dbbr, dbbi):
    def body(lr_ref, li_ref, ldt_ref, bre_ref, bim_ref, e_ref, dar_ref, dai_ref, dbbr_ref, dbbi_ref,
             dlr_ref, dli_ref, dldt_ref, dbre_ref, dbim_ref):
        ev = e_ref[...]
        _, vjp = jax.vjp(lambda a, b, c, d, e: _s5_zoh(a, b, c, d, e, ev),
                         lr_ref[...], li_ref[...], ldt_ref[...], bre_ref[...], bim_ref[...])
        outs = vjp((dar_ref[...], dai_ref[...], dbbr_ref[...], dbbi_ref[...]))
        dlr_ref[...], dli_ref[...], dldt_ref[...], dbre_ref[...], dbim_ref[...] = outs

    shapes = [lr.shape, li.shape, ldt.shape, bre.shape, bim.shape]
    return pl.pallas_call(body, name="s5_zoh_bwd", out_shape=[SDS(s, f32) for s in shapes])(
        lr, li, ldt, bre, bim, expand, dar, dai, dbbr, dbbi)


def _scan_rows(T, rev, ar, ai, hr0, hi0, r_ref, i_ref, off):
    def step(i, carry):
        hr, hi = carry
        t = off + ((T - 1 - i) if rev else i)
        nr = ar * hr - ai * hi + r_ref[pl.ds(t, 1), :]
        ni = ar * hi + ai * hr + i_ref[pl.ds(t, 1), :]
        r_ref[pl.ds(t, 1), :] = nr
        i_ref[pl.ds(t, 1), :] = ni
        return nr, ni

    return lax.fori_loop(0, T, step, (hr0, hi0), unroll=2)


def s5_scan_fwd(u, bre, bim, ctop, cbot, arow, h0, *, d, need_y, name):
    B, L, _ = u.shape
    T = min(S5_TILE, L)
    nt = L // T
    rev = d == 1

    def body(u_ref, bre_ref, bim_ref, ct_ref, cb_ref, a_ref, h0_ref, *rest):
        if need_y:
            y_ref, hin_ref, hend_ref, hr_scr, hi_scr, h_scr = rest
        else:
            hin_ref, hend_ref, hr_scr, hi_scr, h_scr = rest
        n = pl.program_id(1)

        @pl.when(n == 0)
        def _():
            h_scr[...] = h0_ref[...]

        hin_ref[...] = h_scr[...]
        uv = u_ref[...].astype(bf16)
        hr_scr[...] = _dot(uv, bre_ref[...])
        hi_scr[...] = _dot(uv, bim_ref[...])
        hr, hi = _scan_rows(T, rev, a_ref[0:1, :], a_ref[1:2, :], h_scr[0:1, :], h_scr[1:2, :], hr_scr, hi_scr, 0)
        h_scr[0:1, :] = hr
        h_scr[1:2, :] = hi
        if need_y:
            y_ref[...] = _dot(hr_scr[...], ct_ref[...]) + _dot(hi_scr[...], cb_ref[...])

        @pl.when(n == nt - 1)
        def _():
            hend_ref[...] = h_scr[...]

    state = _per_batch(2, S5_HALF)
    hin_spec = pl.BlockSpec((None, None, 2, S5_HALF), (lambda b, n: (b, nt - 1 - n, 0, 0)) if rev else (lambda b, n: (b, n, 0, 0)))
    out_specs = [hin_spec, state]
    out_shape = [SDS((B, nt, 2, S5_HALF), f32), SDS((B, 2, S5_HALF), f32)]
    if need_y:
        out_specs.insert(0, _tok(T, D_S5, nt, rev))
        out_shape.insert(0, SDS((B, L, D_S5), f32))
    return pl.pallas_call(
        body, name=name, grid=(B, nt),
        in_specs=[_tok(T, D_S5, nt, rev), _resident((D_S5, S5_HALF)), _resident((D_S5, S5_HALF)),
                  _resident((S5_HALF, D_S5)), _resident((S5_HALF, D_S5)), _resident((2, S5_HALF)), state],
        out_specs=out_specs, out_shape=out_shape,
        scratch_shapes=[pltpu.VMEM((T, S5_HALF), f32), pltpu.VMEM((T, S5_HALF), f32), pltpu.VMEM((2, S5_HALF), f32)],
        compiler_params=_cparams(2),
    )(u, bre, bim, ctop, cbot, arow, h0)


def s5_scan_bwd(u, dy, bre, bim, ctop, cbot, arow, hin, dhend, *, d, name):
    B, L, _ = u.shape
    T = min(S5_TILE, L)
    nt = L // T
    rev = d == 1
    has_dy = dy is not None
    PAD = 8

    def body(*refs):
        u_ref = refs[0]
        k = 1
        if has_dy:
            dy_ref = refs[1]
            k = 2
        bre_ref, bim_ref, ct_ref, cb_ref, a_ref, hin_ref, dhend_ref = refs[k:k + 7]
        du_ref, dbre_ref, dbim_ref, dct_ref, dcb_ref, da_ref, dh0_ref = refs[k + 7:k + 14]
        hr_scr, hi_scr, gr_scr, gi_scr, p_scr = refs[k + 14:]
        n = pl.program_id(1)

        @pl.when(_first_step())
        def _():
            for r in (dbre_ref, dbim_ref, dct_ref, dcb_ref, da_ref):
                r[...] = jnp.zeros_like(r)

        @pl.when(n == 0)
        def _():
            p_scr[...] = dhend_ref[...]

        ar, ai = a_ref[0:1, :], a_ref[1:2, :]
        uv = u_ref[...].astype(bf16)
        hr_scr[PAD:PAD + T, :] = _dot(uv, bre_ref[...])
        hi_scr[PAD:PAD + T, :] = _dot(uv, bim_ref[...])
        prev_row = PAD + T if rev else PAD - 1
        hr_scr[prev_row:prev_row + 1, :] = hin_ref[0:1, :]
        hi_scr[prev_row:prev_row + 1, :] = hin_ref[1:2, :]
        _scan_rows(T, rev, ar, ai, hin_ref[0:1, :], hin_ref[1:2, :], hr_scr, hi_scr, PAD)
        if has_dy:
            dyv = dy_ref[...].astype(bf16)
            gr_scr[...] = _dot_nt(dyv, ct_ref[...])
            gi_scr[...] = _dot_nt(dyv, cb_ref[...])
            dct_ref[...] += _dot_tn(hr_scr[PAD:PAD + T, :], dyv)
            dcb_ref[...] += _dot_tn(hi_scr[PAD:PAD + T, :], dyv)
        else:
            gr_scr[...] = jnp.zeros_like(gr_scr)
            gi_scr[...] = jnp.zeros_like(gi_scr)

        def step(i, carry):
            pr, pi, dar, dai = carry
            t = i if rev else T - 1 - i
            gr = gr_scr[pl.ds(t, 1), :] + pr
            gi = gi_scr[pl.ds(t, 1), :] + pi
            gr_scr[pl.ds(t, 1), :] = gr
            gi_scr[pl.ds(t, 1), :] = gi
            tp = PAD + t + (1 if rev else -1)
            hpr = hr_scr[pl.ds(tp, 1), :]
            hpi = hi_scr[pl.ds(tp, 1), :]
            dar = dar + hpr * gr + hpi * gi
            dai = dai + hpr * gi - hpi * gr
            return ar * gr + ai * gi, ar * gi - ai * gr, dar, dai

        zero = jnp.zeros((1, S5_HALF), f32)
        pr, pi, dar, dai = lax.fori_loop(0, T, step, (p_scr[0:1, :], p_scr[1:2, :], zero, zero), unroll=2)
        p_scr[0:1, :] = pr
        p_scr[1:2, :] = pi
        da_ref[0:1, :] += dar
        da_ref[1:2, :] += dai
        gr_all = gr_scr[...].astype(bf16)
        gi_all = gi_scr[...].astype(bf16)
        du_ref[...] = _dot_nt(gr_all, bre_ref[...]) + _dot_nt(gi_all, bim_ref[...])
        dbre_ref[...] += _dot_tn(uv, gr_all)
        dbim_ref[...] += _dot_tn(uv, gi_all)

        @pl.when(n == nt - 1)
        def _():
            dh0_ref[...] = p_scr[...]

    brev = not rev
    state = _per_batch(2, S5_HALF)
    hin_spec = pl.BlockSpec((None, None, 2, S5_HALF), (lambda b, n: (b, nt - 1 - n, 0, 0)) if brev else (lambda b, n: (b, n, 0, 0)))
    wspecs = [_resident((D_S5, S5_HALF)), _resident((D_S5, S5_HALF)), _resident((S5_HALF, D_S5)),
              _resident((S5_HALF, D_S5))]
    in_specs = [_tok(T, D_S5, nt, brev)] + ([_tok(T, D_S5, nt, brev)] if has_dy else []) + wspecs + [
        _resident((2, S5_HALF)), hin_spec, state]
    args = [u] + ([dy] if has_dy else []) + [bre, bim, ctop, cbot, arow, hin, dhend]
    return pl.pallas_call(
        body, name=name, grid=(B, nt), in_specs=in_specs,
        out_specs=[_tok(T, D_S5, nt, brev)] + wspecs + [_resident((2, S5_HALF)), state],
        out_shape=[SDS((B, L, D_S5), f32), SDS((D_S5, S5_HALF), f32), SDS((D_S5, S5_HALF), f32),
                   SDS((S5_HALF, D_S5), f32), SDS((S5_HALF, D_S5), f32), SDS((2, S5_HALF), f32),
                   SDS((B, 2, S5_HALF), f32)],
        scratch_shapes=[pltpu.VMEM((T + 2 * PAD, S5_HALF), f32), pltpu.VMEM((T + 2 * PAD, S5_HALF), f32),
                        pltpu.VMEM((T, S5_HALF), f32), pltpu.VMEM((T, S5_HALF), f32), pltpu.VMEM((2, S5_HALF), f32)],
        compiler_params=_cparams(2),
    )(*args)


def _glu_fn(u, y0, y1, z, dsk, wg, bg):
    g = _gelu(dsk * u + y0 + y1)
    return g * jax.nn.sigmoid(_mm(g, wg) + bg) * _silu(z)


def s5_glu_fwd(u, y0, y1, z, dsk, wg, bg):
    B, L, _ = u.shape
    T = min(TOK_TILE, L)

    def body(u_ref, y0_ref, y1_ref, z_ref, dsk_ref, wg_ref, bg_ref, o_ref):
        o_ref[...] = _glu_fn(u_ref[...], y0_ref[...], y1_ref[...], z_ref[...], dsk_ref[...], wg_ref[...].astype(f32),
                             bg_ref[...])

    t = _tok(T, D_S5)
    return pl.pallas_call(
        body, name="s5_glu_fwd", grid=(B, L // T),
        in_specs=[t, t, t, t, _resident((1, D_S5)), _resident((D_S5, D_S5)), _resident((1, D_S5))],
        out_specs=t, out_shape=SDS((B, L, D_S5), f32), compiler_params=_cparams(2),
    )(u, y0, y1, z, dsk, wg, bg)


def s5_glu_bwd(u, y0, y1, z, dsk, wg, bg, dout):
    B, L, _ = u.shape
    T = min(TOK_TILE, L)

    def body(u_ref, y0_ref, y1_ref, z_ref, dsk_ref, wg_ref, bg_ref, do_ref, du_ref, dy_ref, dz_ref,
             ddsk_ref, dwg_ref, dbg_ref):
        @pl.when(_first_step())
        def _():
            for r in (ddsk_ref, dwg_ref, dbg_ref):
                r[...] = jnp.zeros_like(r)

        _, vjp = jax.vjp(_glu_fn, u_ref[...], y0_ref[...], y1_ref[...], z_ref[...], dsk_ref[...],
                         wg_ref[...].astype(f32), bg_ref[...])
        du, dy, _, dz, ddsk, dwg, dbg = vjp(do_ref[...])
        du_ref[...], dy_ref[...], dz_ref[...] = du, dy, dz
        ddsk_ref[...] += ddsk
        dwg_ref[...] += dwg
        dbg_ref[...] += dbg

    t = _tok(T, D_S5)
    small = [_resident((1, D_S5)), _resident((D_S5, D_S5)), _resident((1, D_S5))]
    return pl.pallas_call(
        body, name="s5_glu_bwd", grid=(B, L // T),
        in_specs=[t, t, t, t] + small + [t], out_specs=[t, t, t] + small,
        out_shape=[SDS((B, L, D_S5), f32)] * 3 + [SDS((1, D_S5), f32), SDS((D_S5, D_S5), f32), SDS((1, D_S5), f32)],
        compiler_params=_cparams(2),
    )(u, y0, y1, z, dsk, wg, bg, dout)


CONV_ROWS = 16


def _conv_taps(L, is_ctx):
    t = lax.broadcasted_iota(jnp.int32, (L, 1), 0)
    taps = []
    for di in ((1,) if is_ctx else (0, 1, 2)):
        for dj in (0, 1, 2):
            s = (0 if is_ctx else GRID_W * (di - 1)) + (dj - 1)
            if is_ctx:
                ok = jnp.logical_and(t + s >= 0, t + s < L)
            else:
                col = jnp.bitwise_and(t, GRID_W - 1) + (dj - 1)
                row = t + GRID_W * (di - 1)
                ok = jnp.logical_and(jnp.logical_and(col >= 0, col < GRID_W), jnp.logical_and(row >= 0, row < L))
            taps.append((di * 3 + dj, s, ok.astype(f32)))
    return taps


def _shift(x, s):
    L = x.shape[0]
    k = (-s) % L
    return x if k == 0 else pltpu.roll(x, k, axis=0)


def _qk_post(pre, is_norm, scale):
    s = _silu(pre)
    nrm = lax.rsqrt(jnp.sum(s * s, axis=-1, keepdims=True) + NORM_EPS)
    return s * jnp.where(is_norm, nrm * scale, 1.0)


def _conv_kind():
    ct = pl.program_id(1)
    return ct < 2 * GDN_HEADS, jnp.where(ct < GDN_HEADS, GDN_HEAD ** -0.5, 1.0).astype(f32)


def _conv_pre(xv, w_ref, taps):
    pre = jnp.zeros_like(xv)
    for r, s, m in taps:
        pre = pre + w_ref[r:r + 1, :] * (m * _shift(xv, s))
    return pre


def conv_fwd(qkv, w16, *, is_ctx, name):
    B, L, C = qkv.shape
    spec = pl.BlockSpec((None, L, GDN_HEAD), lambda b, ct: (b, 0, ct))
    wspec = pl.BlockSpec((CONV_ROWS, GDN_HEAD), lambda b, ct: (0, ct))

    def body(x_ref, w_ref, o_ref):
        is_norm, scale = _conv_kind()
        o_ref[...] = _qk_post(_conv_pre(x_ref[...], w_ref, _conv_taps(L, is_ctx)), is_norm, scale)

    return pl.pallas_call(body, name=name, grid=(B, C // GDN_HEAD), in_specs=[spec, wspec], out_specs=spec,
                          out_shape=SDS((B, L, C), f32), compiler_params=_cparams(2))(qkv, w16)


def conv_bwd(qkv, w16, da0, da1, *, is_ctx, name):
    B, L, C = qkv.shape
    spec = pl.BlockSpec((None, L, GDN_HEAD), lambda b, ct: (b, 0, ct))
    wspec = pl.BlockSpec((CONV_ROWS, GDN_HEAD), lambda b, ct: (0, ct))
    dwspec = pl.BlockSpec((None, CONV_ROWS, GDN_HEAD), lambda b, ct: (b, 0, ct))

    def body(x_ref, w_ref, d0_ref, d1_ref, dx_ref, dw_ref):
        is_norm, scale = _conv_kind()
        taps = _conv_taps(L, is_ctx)
        xv = x_ref[...]
        _, vjp = jax.vjp(lambda p: _qk_post(p, is_norm, scale), _conv_pre(xv, w_ref, taps))
        dpre = vjp(d0_ref[...] + d1_ref[...])[0]
        dx = jnp.zeros_like(xv)
        dw_ref[...] = jnp.zeros_like(dw_ref)
        for r, s, m in taps:
            md = m * dpre
            dx = dx + _shift(w_ref[r:r + 1, :] * md, -s)
            dw_ref[r:r + 1, :] = jnp.sum(md * _shift(xv, s), axis=0, keepdims=True)
        dx_ref[...] = dx

    return pl.pallas_call(body, name=name, grid=(B, C // GDN_HEAD), in_specs=[spec, wspec, spec, spec],
                          out_specs=[spec, dwspec], out_shape=[SDS((B, L, C), f32), SDS((B, CONV_ROWS, C), f32)],
                          compiler_params=_cparams(2))(qkv, w16, da0, da1)


def _gates_fn(ba, alog, dtb):
    T = ba.shape[0]
    lane = lax.broadcasted_iota(jnp.int32, ba.shape, 1)
    ii = lax.broadcasted_iota(jnp.int32, (T, T), 0)
    jj = lax.broadcasted_iota(jnp.int32, (T, T), 1)
    same = jnp.right_shift(ii, 6) == jnp.right_shift(jj, 6)
    lmat = jnp.logical_and(same, ii >= jj).astype(f32)
    umat = jnp.logical_and(same, ii <= jj).astype(f32)
    g = jnp.where(lane >= 8, -jnp.exp(alog) * jax.nn.softplus(ba + dtb), 0.0)
    gc = jnp.where(lane >= 12, _dot_hi(umat, g), _dot_hi(lmat, g))
    return jnp.where(lane < 8, jax.nn.sigmoid(ba), gc)


def gates_fwd(ba, alog, dtb, *, name):
    B, L, _ = ba.shape
    T = min(TOK_TILE, L)
    t = _tok(T, N_GATE)

    def body(ba_ref, al_ref, dt_ref, o_ref):
        o_ref[...] = _gates_fn(ba_ref[...], al_ref[...], dt_ref[...])

    return pl.pallas_call(body, name=name, grid=(B, L // T),
                          in_specs=[t, _resident((1, N_GATE)), _resident((1, N_GATE))], out_specs=t,
                          out_shape=SDS((B, L, N_GATE), f32), compiler_params=_cparams(2))(ba, alog, dtb)


def gates_bwd(ba, alog, dtb, dbg, *, name):
    B, L, _ = ba.shape
    T = min(TOK_TILE, L)
    t = _tok(T, N_GATE)
    small = _resident((1, N_GATE))

    def body(ba_ref, al_ref, dt_ref, d_ref, dba_ref, dal_ref, ddt_ref):
        @pl.when(_first_step())
        def _():
            dal_ref[...] = jnp.zeros_like(dal_ref)
            ddt_ref[...] = jnp.zeros_like(ddt_ref)

        _, vjp = jax.vjp(_gates_fn, ba_ref[...], al_ref[...], dt_ref[...])
        dba, dal, ddt = vjp(d_ref[...])
        dba_ref[...] = dba
        dal_ref[...] += dal
        ddt_ref[...] += ddt

    return pl.pallas_call(body, name=name, grid=(B, L // T), in_specs=[t, small, small, t],
                          out_specs=[t, small, small],
                          out_shape=[SDS((B, L, N_GATE), f32), SDS((1, N_GATE), f32), SDS((1, N_GATE), f32)],
                          compiler_params=_cparams(2))(ba, alog, dtb, dbg)


@jax.custom_vjp
def _inv_unit_tri(mats):
    n = mats[0].shape[0]
    eye = (lax.broadcasted_iota(jnp.int32, (n, n), 0) == lax.broadcasted_iota(jnp.int32, (n, n), 1)).astype(f32)
    xs = [eye - a for a in mats]
    ps = [_dot_h3(a, a) for a in mats]
    k = 2
    while k < n:
        xs = [x + _dot_h3(x, p) for x, p in zip(xs, ps)]
        k *= 2
        if k < n:
            ps = [_dot_h3(p, p) for p in ps]
    return tuple(xs)


def _inv_unit_tri_fwd(mats):
    xs = _inv_unit_tri(mats)
    return xs, xs


def _inv_unit_tri_bwd(xs, dxs):
    xts = [x.T for x in xs]
    ts = [_dot_h3(xt, dx) for xt, dx in zip(xts, dxs)]
    return (tuple(-_dot_h3(t, xt) for t, xt in zip(ts, xts)),)


_inv_unit_tri.defvjp(_inv_unit_tri_fwd, _inv_unit_tri_bwd)


def _gdn_chunk(heads, *, rev):
    n = heads[0][0].shape[0]
    ii = lax.broadcasted_iota(jnp.int32, (n, n), 0)
    jj = lax.broadcasted_iota(jnp.int32, (n, n), 1)
    lower = (ii <= jj) if rev else (ii >= jj)
    strict = (ii < jj) if rev else (ii > jj)
    last = 0 if rev else n - 1
    row = lax.broadcasted_iota(jnp.int32, (n, 1), 0)
    H = range(len(heads))
    q, k, v, beta, gc, gr, s = (list(t) for t in zip(*heads))
    decay = [jnp.where(lower, jnp.exp(jnp.where(lower, gc[h] - gr[h], 0.0)), 0.0) for h in H]
    kk = [_mm_nt(k[h], k[h]) for h in H]
    qk = [_mm_nt(q[h], k[h]) * decay[h] for h in H]
    qs = [_mm(q[h], s[h]) for h in H]
    a_mat = tuple(jnp.where(strict, beta[h] * kk[h] * decay[h], 0.0) for h in H)
    gamma = [jnp.exp(gc[h]) for h in H]
    g_last = [jnp.sum(jnp.where(row == last, gc[h], 0.0), axis=0, keepdims=True) for h in H]
    tinv = _inv_unit_tri(a_mat)
    u0 = [_dot_h3(tinv[h], beta[h] * v[h]) for h in H]
    w = [_dot_h3(tinv[h], (beta[h] * gamma[h]) * k[h]) for h in H]
    k_out = [k[h] * jnp.exp(g_last[h] - gc[h]) for h in H]
    u = [u0[h] - _mm(w[h], s[h]) for h in H]
    o = [gamma[h] * qs[h] + _mm(qk[h], u[h]) for h in H]
    s_new = [jnp.exp(g_last[h]) * s[h] + _mm_tn(k_out[h], u[h]) for h in H]
    return tuple((o[h], s_new[h]) for h in H)


def _gdn_specs(B, nc, rev):
    def cidx(n):
        return (nc - 1 - n) if rev else n
    tok = lambda width: pl.BlockSpec((B, CHUNK, width), lambda n: (0, cidx(n), 0))
    rowspec = pl.BlockSpec((B, None, N_GATE, CHUNK), lambda n: (0, cidx(n), 0, 0))
    st = pl.BlockSpec((B, GDN_HEADS, GDN_HEAD, GDN_HEAD), lambda n: (0, 0, 0, 0))
    ck = pl.BlockSpec((B, None, GDN_HEADS, GDN_HEAD, GDN_HEAD), lambda n: (0, cidx(n), 0, 0, 0))
    return tok, rowspec, st, ck


def _gdn_head_args(qkv_ref, bg_ref, bgr_ref, b, d, h):
    col = d * GDN_HEADS + h
    q = qkv_ref[b, :, h * GDN_HEAD:(h + 1) * GDN_HEAD]
    k = qkv_ref[b, :, D_GDN + h * GDN_HEAD:D_GDN + (h + 1) * GDN_HEAD]
    v = qkv_ref[b, :, 2 * D_GDN + h * GDN_HEAD:2 * D_GDN + (h + 1) * GDN_HEAD]
    bgv = bg_ref[b]
    return q, k, v, bgv[:, col:col + 1], bgv[:, 8 + col:9 + col], bgr_ref[b][8 + col:9 + col, :]


def gdn_fwd(qkv, bg, bgr, s0, *, d, need_o, name):
    B, L, _ = qkv.shape
    nc = L // CHUNK
    rev = d == 1
    tok, rowspec, st, ck = _gdn_specs(B, nc, rev)
    bh = [(b, h) for b in range(B) for h in range(GDN_HEADS)]

    def body(qkv_ref, bg_ref, bgr_ref, s0_ref, *rest):
        if need_o:
            o_ref, ck_ref, sf_ref, s_scr = rest
        else:
            ck_ref, sf_ref, s_scr = rest
        n = pl.program_id(0)

        @pl.when(n == 0)
        def _():
            s_scr[...] = s0_ref[...]

        ck_ref[...] = s_scr[...]
        heads = tuple(_gdn_head_args(qkv_ref, bg_ref, bgr_ref, b, d, h) + (s_scr[b, h],) for b, h in bh)
        for (b, h), (o, s_new) in zip(bh, _gdn_chunk(heads, rev=rev)):
            if need_o:
                o_ref[b, :, h * GDN_HEAD:(h + 1) * GDN_HEAD] = o
            s_scr[b, h] = s_new

        @pl.when(n == nc - 1)
        def _():
            sf_ref[...] = s_scr[...]

    out_specs = [ck, st]
    out_shape = [SDS((B, nc, GDN_HEADS, GDN_HEAD, GDN_HEAD), f32), SDS((B, GDN_HEADS, GDN_HEAD, GDN_HEAD), f32)]
    if need_o:
        out_specs.insert(0, tok(D_GDN))
        out_shape.insert(0, SDS((B, L, D_GDN), f32))
    return pl.pallas_call(
        body, name=name, grid=(nc,), in_specs=[tok(3 * D_GDN), tok(N_GATE), rowspec, st],
        out_specs=out_specs, out_shape=out_shape,
        scratch_shapes=[pltpu.VMEM((B, GDN_HEADS, GDN_HEAD, GDN_HEAD), f32)], compiler_params=_cparams(1),
    )(qkv, bg, bgr, s0)


def gdn_bwd(qkv, bg, bgr, ck, do, dsf, *, d, name):
    B, L, _ = qkv.shape
    nc = L // CHUNK
    rev = d == 1
    has_do = do is not None
    tok, rowspec, st, ckspec = _gdn_specs(B, nc, not rev)
    bh = [(b, h) for b in range(B) for h in range(GDN_HEADS)]

    def body(*refs):
        qkv_ref, bg_ref, bgr_ref, ck_ref = refs[:4]
        k = 4
        if has_do:
            do_ref = refs[4]
            k = 5
        dsf_ref, dqkv_ref, dbg_ref, dbgr_ref, ds0_ref, ds_scr = refs[k:]
        n = pl.program_id(0)

        @pl.when(n == 0)
        def _():
            ds_scr[...] = dsf_ref[...]

        lane = lax.broadcasted_iota(jnp.int32, (CHUNK, N_GATE), 1)
        sub = lax.broadcasted_iota(jnp.int32, (N_GATE, CHUNK), 0)
        heads = tuple(_gdn_head_args(qkv_ref, bg_ref, bgr_ref, b, d, h) + (ck_ref[b, h],) for b, h in bh)
        _, vjp = jax.vjp(functools.partial(_gdn_chunk, rev=rev), heads)
        cts = tuple(((do_ref[b, :, h * GDN_HEAD:(h + 1) * GDN_HEAD] if has_do else jnp.zeros((CHUNK, GDN_HEAD), f32)),
                     ds_scr[b, h]) for b, h in bh)
        (dheads,) = vjp(cts)
        dbg_acc = [jnp.zeros((CHUNK, N_GATE), f32) for _ in range(B)]
        dbgr_acc = [jnp.zeros((N_GATE, CHUNK), f32) for _ in range(B)]
        for (b, h), (dq, dk, dv, db, dgc, dgr, ds) in zip(bh, dheads):
            col = d * GDN_HEADS + h
            dqkv_ref[b, :, h * GDN_HEAD:(h + 1) * GDN_HEAD] = dq
            dqkv_ref[b, :, D_GDN + h * GDN_HEAD:D_GDN + (h + 1) * GDN_HEAD] = dk
            dqkv_ref[b, :, 2 * D_GDN + h * GDN_HEAD:2 * D_GDN + (h + 1) * GDN_HEAD] = dv
            dbg_acc[b] = dbg_acc[b] + jnp.where(lane == col, db, 0.0) + jnp.where(lane == 8 + col, dgc, 0.0)
            dbgr_acc[b] = dbgr_acc[b] + jnp.where(sub == 8 + col, dgr, 0.0)
            ds_scr[b, h] = ds
        for b in range(B):
            dbg_ref[b] = dbg_acc[b]
            dbgr_ref[b] = dbgr_acc[b]

        @pl.when(n == nc - 1)
        def _():
            ds0_ref[...] = ds_scr[...]

    in_specs = [tok(3 * D_GDN), tok(N_GATE), rowspec, ckspec] + ([tok(D_GDN)] if has_do else []) + [st]
    args = [qkv, bg, bgr, ck] + ([do] if has_do else []) + [dsf]
    return pl.pallas_call(
        body, name=name, grid=(nc,), in_specs=in_specs,
        out_specs=[tok(3 * D_GDN), tok(N_GATE), rowspec, st],
        out_shape=[SDS((B, L, 3 * D_GDN), f32), SDS((B, L, N_GATE), f32), SDS((B, nc, N_GATE, CHUNK), f32),
                   SDS((B, GDN_HEADS, GDN_HEAD, GDN_HEAD), f32)],
        scratch_shapes=[pltpu.VMEM((B, GDN_HEADS, GDN_HEAD, GDN_HEAD), f32)], compiler_params=_cparams(1),
    )(*args)


def _gnorm_fn(o0, o1, z, w):
    o = o0 + o1
    return o * lax.rsqrt(jnp.mean(o * o, axis=-1, keepdims=True) + NORM_EPS) * w * _silu(z)


def gnorm_fwd(o0, o1, z, w):
    B, L, _ = o0.shape
    T = min(TOK_TILE, L)
    t = _tok(T, D_GDN)

    def body(o0_ref, o1_ref, z_ref, w_ref, out_ref):
        for h in range(GDN_HEADS):
            sl = slice(h * GDN_HEAD, (h + 1) * GDN_HEAD)
            out_ref[:, sl] = _gnorm_fn(o0_ref[:, sl], o1_ref[:, sl], z_ref[:, sl], w_ref[...])

    return pl.pallas_call(body, name="gnorm_fwd", grid=(B, L // T), in_specs=[t, t, t, _resident((1, GDN_HEAD))],
                          out_specs=t, out_shape=SDS((B, L, D_GDN), f32), compiler_params=_cparams(2))(o0, o1, z, w)


def gnorm_bwd(o0, o1, z, w, dout):
    B, L, _ = o0.shape
    T = min(TOK_TILE, L)
    t = _tok(T, D_GDN)

    def body(o0_ref, o1_ref, z_ref, w_ref, d_ref, do_ref, dz_ref, dw_ref):
        @pl.when(_first_step())
        def _():
            dw_ref[...] = jnp.zeros_like(dw_ref)

        for h in range(GDN_HEADS):
            sl = slice(h * GDN_HEAD, (h + 1) * GDN_HEAD)
            _, vjp = jax.vjp(_gnorm_fn, o0_ref[:, sl], o1_ref[:, sl], z_ref[:, sl], w_ref[...])
            do, _, dz, dw = vjp(d_ref[:, sl])
            do_ref[:, sl] = do
            dz_ref[:, sl] = dz
            dw_ref[...] += dw

    return pl.pallas_call(body, name="gnorm_bwd", grid=(B, L // T),
                          in_specs=[t, t, t, _resident((1, GDN_HEAD)), t], out_specs=[t, t, _resident((1, GDN_HEAD))],
                          out_shape=[SDS((B, L, D_GDN), f32), SDS((B, L, D_GDN), f32), SDS((1, GDN_HEAD), f32)],
                          compiler_params=_cparams(2))(o0, o1, z, w, dout)


def _head_loss(y, x, gate, lng, lnb, tgt):
    r = DEEPNORM_ALPHA * x + gate * y
    mu = jnp.mean(r, axis=-1, keepdims=True)
    rc = r - mu
    var = jnp.mean(rc * rc, axis=-1, keepdims=True)
    err = rc * lax.rsqrt(var + LN_EPS) * lng + lnb - tgt
    return (0.5 / D_MODEL) * jnp.sum(jnp.sum(err * err, axis=-1, keepdims=True), axis=0, keepdims=True)


def head_fwd_bwd(s5o, gdo, x, tgt, gate, lng, lnb, ws, wg):
    B, L, _ = x.shape
    T = min(TOK_TILE, L)

    def body(s_ref, g_ref, x_ref, t_ref, gate_ref, lng_ref, lnb_ref, ws_ref, wg_ref,
             loss_ref, ds_ref, dg_ref, gx_ref, dws_ref, dwg_ref, dgate_ref, dlng_ref, dlnb_ref):
        n = pl.program_id(1)

        @pl.when(_first_step())
        def _():
            for r in (dws_ref, dwg_ref, dlng_ref, dlnb_ref):
                r[...] = jnp.zeros_like(r)

        @pl.when(n == 0)
        def _():
            loss_ref[...] = jnp.zeros_like(loss_ref)
            dgate_ref[...] = jnp.zeros_like(dgate_ref)

        sv = s_ref[...].astype(bf16)
        gv = g_ref[...].astype(bf16)
        y = _dot(sv, ws_ref[...]) + _dot(gv, wg_ref[...])
        loss, vjp = jax.vjp(lambda *a: _head_loss(*a, t_ref[...]), y, x_ref[...], gate_ref[...], lng_ref[...],
                            lnb_ref[...])
        dy, dx, dgate, dlng, dlnb = vjp(jnp.ones((1, 1), f32))
        loss_ref[...] += jnp.broadcast_to(loss, loss_ref.shape)
        dyb = dy.astype(bf16)
        ds_ref[...] = _dot_nt(dyb, ws_ref[...])
        dg_ref[...] = _dot_nt(dyb, wg_ref[...])
        gx_ref[...] = dx
        dws_ref[...] += _dot_tn(sv, dyb)
        dwg_ref[...] += _dot_tn(gv, dyb)
        dgate_ref[...] += dgate
        dlng_ref[...] += dlng
        dlnb_ref[...] += dlnb

    half, full = _tok(T, D_S5), _tok(T, D_MODEL)
    row = _resident((1, D_MODEL))
    wsp = _resident((D_S5, D_MODEL))
    return pl.pallas_call(
        body, name="head_fwd_bwd", grid=(B, L // T),
        in_specs=[half, half, full, full, _per_batch(1, D_MODEL), row, row, wsp, wsp],
        out_specs=[_per_batch(8, LANES), half, half, full, wsp, wsp, _per_batch(1, D_MODEL), row, row],
        out_shape=[SDS((B, 8, LANES), f32), SDS((B, L, D_S5), f32), SDS((B, L, D_GDN), f32), SDS((B, L, D_MODEL), f32),
                   SDS((D_S5, D_MODEL), f32), SDS((D_GDN, D_MODEL), f32), SDS((B, 1, D_MODEL), f32),
                   SDS((1, D_MODEL), f32), SDS((1, D_MODEL), f32)],
        compiler_params=_cparams(2),
    )(s5o, gdo, x, tgt, gate, lng, lnb, ws, wg)


def _adamw_math(w, g, m, v):
    nm = ADAM_B1 * m + (1.0 - ADAM_B1) * g
    nv = ADAM_B2 * v + (1.0 - ADAM_B2) * jnp.square(g)
    m_hat = nm / (1.0 - ADAM_B1 ** ADAM_STEP)
    v_hat = nv / (1.0 - ADAM_B2 ** ADAM_STEP)
    return -ADAM_LR * (m_hat / (jnp.sqrt(v_hat) + ADAM_EPS) + ADAM_WD * w), nm, nv


def _row_tile(rows, cap=512):
    for t in range(min(cap, rows), 15, -1):
        if rows % t == 0 and t % 16 == 0:
            return t
    return rows


def adamw_2d(w, g, m, v, *, name):
    R, C = w.shape
    T = _row_tile(R)
    spec = pl.BlockSpec((T, C), lambda i: (i, 0))

    def body(w_ref, g_ref, m_ref, v_ref, d_ref, nm_ref, nv_ref):
        d_ref[...], nm_ref[...], nv_ref[...] = _adamw_math(w_ref[...], g_ref[...], m_ref[...], v_ref[...])

    return pl.pallas_call(body, name=name, grid=(R // T,), in_specs=[spec] * 4, out_specs=[spec] * 3,
                          out_shape=[SDS((R, C), f32)] * 3, compiler_params=_cparams(1))(w, g, m, v)


def adamw_small(ws, gs, ms, vs):
    n = len(ws)

    def body(*refs):
        outs = refs[4 * n:]
        for i in range(n):
            d, nm, nv = _adamw_math(refs[i][...], refs[n + i][...], refs[2 * n + i][...], refs[3 * n + i][...])
            outs[i][...], outs[n + i][...], outs[2 * n + i][...] = d, nm, nv

    res = pl.pallas_call(body, name="adamw_small", out_shape=[SDS(w.shape, f32) for w in ws] * 3,
                         compiler_params=pltpu.CompilerParams(vmem_limit_bytes=VMEM_LIMIT))(*ws, *gs, *ms, *vs)
    return res[:n], res[n:2 * n], res[2 * n:]


def sum_cores(own, got, *, name):
    A, H, C = own.shape
    T = _row_tile(H)
    spec = pl.BlockSpec((None, T, C), lambda a, i: (a, i, 0))

    def body(a_ref, b_ref, q32_ref, q16_ref):
        q = a_ref[...] + b_ref[...]
        q32_ref[...] = q
        q16_ref[...] = q.astype(bf16)

    return pl.pallas_call(body, name=name, grid=(A, H // T), in_specs=[spec, spec], out_specs=[spec, spec],
                          out_shape=[SDS((A, H, C), f32), SDS((A, H, C), bf16)], compiler_params=_cparams(2))(own, got)


def sum_chips(mine, rec, cpos, *, name):
    H, C = mine.shape
    T = _row_tile(H)

    def body(c_ref, m_ref, r_ref, f_ref):
        f_ref[...] = ((m_ref[...] + r_ref[0].astype(f32)) + r_ref[1].astype(f32)) + r_ref[2].astype(f32)

    grid_spec = pltpu.PrefetchScalarGridSpec(
        num_scalar_prefetch=1, grid=(H // T,),
        in_specs=[pl.BlockSpec((T, C), lambda i, c_ref: (i, 0)), pl.BlockSpec((3, T, C), lambda i, c_ref: (0, i, 0))],
        out_specs=pl.BlockSpec((None, T, C), lambda i, c_ref: (c_ref[0], i, 0)))
    return pl.pallas_call(body, name=name, grid_spec=grid_spec, out_shape=SDS((2, H, C), f32),
                          compiler_params=_cparams(1))(cpos.reshape(1).astype(jnp.int32), mine, rec)


CHIP_FLIPS = ((1, 0), (0, 1), (1, 1))


def _pos():
    return lax.axis_index("x"), lax.axis_index("y"), lax.axis_index("c")


def _comm_call(body, srcs, out_sds, n_remote, n_local, name):
    any_spec = pl.BlockSpec(memory_space=pl.ANY)
    return pl.pallas_call(
        body, name=name, in_specs=[any_spec] * len(srcs), out_specs=[any_spec] * len(out_sds), out_shape=out_sds,
        scratch_shapes=[pltpu.SemaphoreType.DMA((n_remote,)), pltpu.SemaphoreType.DMA((n_remote,)),
                        pltpu.SemaphoreType.DMA((max(n_local, 1),))],
        compiler_params=pltpu.CompilerParams(has_side_effects=True),
    )(*srcs)


def _remote(src, dst, send_sems, recv_sems, k, target):
    return pltpu.make_async_remote_copy(src, dst, send_sems.at[k], recv_sems.at[k], device_id=target,
                                        device_id_type=MESH)


def _half_rows(c, rows):
    half = rows // 2
    return pl.ds(pl.multiple_of(c * half, 8), half)


def gather_shards(shards):
    nt = len(shards)

    def body(*refs):
        srcs, outs = refs[:nt], refs[nt:2 * nt]
        send_sems, recv_sems, _ = refs[2 * nt:]
        x, y, c = _pos()
        j = 2 * x + y
        sib = (x, y, 1 - c)
        own = [_remote(srcs[t], outs[t].at[j], send_sems, recv_sems, 7 * t + 6, sib) for t in range(nt)]
        first, passed = [], []
        for k, (fx, fy) in enumerate(CHIP_FLIPS):
            tx, ty = x ^ fx, y ^ fy
            jk = 2 * tx + ty
            for t in range(nt):
                rows = _half_rows(c, srcs[t].shape[0])
                first.append(_remote(srcs[t].at[rows], outs[t].at[j, rows], send_sems, recv_sems, 7 * t + k, (tx, ty, c)))
                passed.append(_remote(outs[t].at[jk, rows], outs[t].at[jk, rows], send_sems, recv_sems, 7 * t + 3 + k, sib))
        for cp in first + own:
            cp.start()
        for a, b in zip(first, passed):
            a.wait_recv()
            b.start()
        for cp in passed + own:
            cp.wait_recv()
        for cp in first + passed + own:
            cp.wait_send()

    return _comm_call(body, shards, [SDS((4,) + s.shape, s.dtype) for s in shards], 7 * nt, 0, "gather_shards")


def swap_halves(ps):
    nt = len(ps)

    def body(*refs):
        srcs, outs = refs[:nt], refs[nt:2 * nt]
        send_sems, recv_sems, _ = refs[2 * nt:]
        x, y, c = _pos()
        cps = [_remote(srcs[t].at[a, _half_rows(1 - c, srcs[t].shape[1])], outs[t].at[a], send_sems, recv_sems, 4 * t + a,
                       (x, y, 1 - c)) for t in range(nt) for a in range(4)]
        for cp in cps:
            cp.start()
        for cp in cps:
            cp.wait()

    return _comm_call(body, ps, [SDS((4, p.shape[1] // 2, p.shape[2]), p.dtype) for p in ps], 4 * nt, 0, "swap_halves")


def scatter_to_chips(qs):
    nt = len(qs)

    def body(*refs):
        srcs, outs = refs[:nt], refs[nt:2 * nt]
        send_sems, recv_sems, _ = refs[2 * nt:]
        x, y, c = _pos()
        cps = []
        for k, (fx, fy) in enumerate(CHIP_FLIPS):
            tx, ty = x ^ fx, y ^ fy
            for t in range(nt):
                cps.append(_remote(srcs[t].at[2 * tx + ty], outs[t].at[k], send_sems, recv_sems, 3 * t + k, (tx, ty, c)))
        for cp in cps:
            cp.start()
        for cp in cps:
            cp.wait()

    return _comm_call(body, qs, [SDS((3,) + q.shape[1:], q.dtype) for q in qs], 3 * nt, 0, "scatter_to_chips")


def join_halves(fs):
    nt = len(fs)

    def body(*refs):
        outs = refs[nt:2 * nt]
        send_sems, recv_sems, _ = refs[2 * nt:]
        x, y, c = _pos()
        cps = [_remote(outs[t].at[c], outs[t].at[c], send_sems, recv_sems, t, (x, y, 1 - c)) for t in range(nt)]
        for cp in cps:
            cp.start()
        for cp in cps:
            cp.wait()

    any_spec = pl.BlockSpec(memory_space=pl.ANY)
    return pl.pallas_call(
        body, name="join_halves", in_specs=[any_spec] * nt, out_specs=[any_spec] * nt,
        out_shape=[SDS(f.shape, f.dtype) for f in fs], input_output_aliases={t: t for t in range(nt)},
        scratch_shapes=[pltpu.SemaphoreType.DMA((nt,)), pltpu.SemaphoreType.DMA((nt,)), pltpu.SemaphoreType.DMA((1,))],
        compiler_params=pltpu.CompilerParams(has_side_effects=True),
    )(*fs)


def gather_small(s):
    def body(src, out, send_sems, recv_sems, _):
        x, y, c = _pos()
        j = 2 * x + y
        cps = [_remote(src, out.at[j], send_sems, recv_sems, k, (x ^ fx, y ^ fy, c)) for k, (fx, fy) in enumerate(CHIP_FLIPS)]
        cps.append(_remote(src, out.at[j], send_sems, recv_sems, 3, (x, y, 1 - c)))
        for cp in cps:
            cp.start()
        for cp in cps:
            cp.wait()

    return _comm_call(body, [s], [SDS((4,) + s.shape, s.dtype)], 4, 0, "gather_small")[0]


SMALL_SHAPES = ((D_MODEL,), (1, 3 * D_MODEL), (1, 2, 32, 64), (1, 2, 32, 64), (1, 2, 32), (1, 2, 32, 64, 16),
                (1, 2, 32, 64, 16), (1, 2, 32, 16, 64), (1, 2, 32, 16, 64), (1, D_S5), (1, D_S5), (1, 2, 4), (1, 2, 4),
                (1, GDN_HEAD), (1, D_MODEL), (1, D_MODEL))


def _size(shape):
    return functools.reduce(lambda p, q: p * q, shape)


SMALL_ROWS = tuple(-(-_size(s) // LANES) for s in SMALL_SHAPES)
SMALL_TOTAL = 2176
SMALL_QUARTER = SMALL_TOTAL // 4


def _rows(a):
    flat = a.reshape(-1)
    pad = (-flat.shape[0]) % LANES
    if pad:
        flat = jnp.concatenate([flat, jnp.zeros((pad,), flat.dtype)])
    return flat.reshape(-1, LANES)


def _pack_small(parts):
    rows = [_rows(p) for p in parts]
    rows.append(jnp.zeros((SMALL_TOTAL - sum(SMALL_ROWS), LANES), f32))
    return jnp.concatenate(rows, axis=0)


def _unpack_small(buf):
    out, r = [], 0
    for s, n in zip(SMALL_SHAPES, SMALL_ROWS):
        out.append(buf[r:r + n].reshape(-1)[:_size(s)].reshape(s))
        r += n
    return out


def _as_2d(a):
    return a.reshape(1, -1) if a.ndim == 1 else a.reshape(-1, a.shape[-1])


def _block_diag_in(bb):
    eye = jnp.eye(S5_GROUPS, dtype=bb.dtype)
    b3 = bb.reshape(S5_GROUPS, S5_STATE, S5_GROUP)
    return jnp.einsum('gpc,gh->gchp', b3, eye).reshape(D_S5, S5_HALF)


def _block_diag_in_t(d):
    d5 = d.reshape(S5_GROUPS, S5_GROUP, S5_GROUPS, S5_STATE)
    return jnp.einsum('gcgp->gpc', d5).reshape(S5_GROUPS, S5_STATE * S5_GROUP)


def _block_diag_out(cm):
    eye = jnp.eye(S5_GROUPS, dtype=cm.dtype)
    return jnp.einsum('gcp,gh->hpgc', cm, eye).reshape(S5_HALF, D_S5)


def _block_diag_out_t(d):
    d5 = d.reshape(S5_GROUPS, S5_STATE, S5_GROUPS, S5_GROUP)
    return jnp.einsum('gpgc->gcp', d5)


def _to_chunk_rows(a):
    B, L, W = a.shape
    return a.reshape(B, L // CHUNK, CHUNK, W).transpose(0, 1, 3, 2)


def _from_chunk_rows(a):
    B, nc, W, _ = a.shape
    return a.transpose(0, 1, 3, 2).reshape(B, nc * CHUNK, W)


def local_step(x, c, ctx, c_ctx, tgt, w_ada, b_ada, w_in, lam_re, lam_im, log_dt, b_re, b_im, c_re, c_im, s5_d,
               w_glu, b_glu, conv16, a_log, dt_bias, norm_w, w_out, ln_g, ln_b):
    B, L, _ = x.shape
    zeros_state = jnp.zeros((B, GDN_HEADS, GDN_HEAD, GDN_HEAD), f32)

    cc = jnp.concatenate([c, c_ctx[None, :], jnp.zeros((8 - B - 1, D_MODEL), f32)], axis=0)
    m = ada_fwd(cc, w_ada, b_ada)
    shift, scale, gate = m[:B, :D_MODEL], m[:B, D_MODEL:2 * D_MODEL], m[:B, 2 * D_MODEL:]
    mod = jnp.stack([scale, shift], axis=1)
    mod_c = jnp.broadcast_to(jnp.stack([m[B, D_MODEL:2 * D_MODEL], m[B, :D_MODEL]], axis=0)[None], (B, 2, D_MODEL))

    u, z_s5, qkv, z_gdn, ba = in_proj_fwd(x, mod, w_in, name="in_proj_fwd")
    uc, _, qkvc, _, bac = in_proj_fwd(ctx, mod_c, w_in, name="in_proj_fwd_ctx")

    ng = N_DIR * S5_GROUPS
    zoh_in = (lam_re.reshape(ng, S5_STATE), lam_im.reshape(ng, S5_STATE), log_dt.reshape(ng, 1),
              b_re.reshape(ng, S5_STATE * S5_GROUP), b_im.reshape(ng, S5_STATE * S5_GROUP))
    expand = (jnp.arange(S5_STATE * S5_GROUP)[None, :] // S5_GROUP == jnp.arange(S5_STATE)[:, None]).astype(f32)
    ar, ai, bbr, bbi = s5_zoh_fwd(*zoh_in, expand)
    bbr16, bbi16 = bbr.astype(bf16), bbi.astype(bf16)
    c_re16 = c_re.reshape(N_DIR, S5_GROUPS, S5_GROUP, S5_STATE).astype(bf16)
    c_im16 = (-c_im).reshape(N_DIR, S5_GROUPS, S5_GROUP, S5_STATE).astype(bf16)
    s5w, ys, hins, hins_c = [], [], [], []
    for d in range(N_DIR):
        g = slice(d * S5_GROUPS, (d + 1) * S5_GROUPS)
        wd = (_block_diag_in(bbr16[g]), _block_diag_in(bbi16[g]), _block_diag_out(c_re16[d]), _block_diag_out(c_im16[d]),
              jnp.stack([ar[g].reshape(-1), ai[g].reshape(-1)], axis=0))
        s5w.append(wd)
        hin_c, hend_c = s5_scan_fwd(uc, *wd, jnp.zeros((B, 2, S5_HALF), f32), d=d, need_y=False, name=f"s5_fwd_ctx{d}")
        y_d, hin, _ = s5_scan_fwd(u, *wd, hend_c, d=d, need_y=True, name=f"s5_fwd{d}")
        ys.append(y_d)
        hins.append(hin)
        hins_c.append(hin_c)
    glu_w = (s5_d.reshape(1, D_S5), w_glu, b_glu.reshape(1, D_S5))
    s5o = s5_glu_fwd(u, ys[0], ys[1], z_s5, *glu_w)

    act = conv_fwd(qkv, conv16, is_ctx=False, name="conv_fwd")
    act_c = conv_fwd(qkvc, conv16, is_ctx=True, name="conv_fwd_ctx")
    pad8 = jnp.zeros((1, 8), f32)
    alog16 = jnp.concatenate([pad8, a_log.reshape(1, 8)], axis=1)
    dtb16 = jnp.concatenate([pad8, dt_bias.reshape(1, 8)], axis=1)
    bg = gates_fwd(ba, alog16, dtb16, name="gates_fwd")
    bg_c = gates_fwd(bac, alog16, dtb16, name="gates_fwd_ctx")
    bgr, bgr_c = _to_chunk_rows(bg), _to_chunk_rows(bg_c)
    os_, cks, cks_c = [], [], []
    for d in range(N_DIR):
        ck_c, s_c = gdn_fwd(act_c, bg_c, bgr_c, zeros_state, d=d, need_o=False, name=f"gdn_fwd_ctx{d}")
        o_d, ck, _ = gdn_fwd(act, bg, bgr, s_c, d=d, need_o=True, name=f"gdn_fwd{d}")
        os_.append(o_d)
        cks.append(ck)
        cks_c.append(ck_c)
    nw = norm_w.reshape(1, GDN_HEAD)
    gdo = gnorm_fwd(os_[0], os_[1], z_gdn, nw)

    loss8, ds5o, dgdo, gx_res, dws, dwg, dgate, dlng, dlnb = head_fwd_bwd(
        s5o, gdo, x, tgt, gate[:, None, :], ln_g.reshape(1, D_MODEL), ln_b.reshape(1, D_MODEL), w_out[:D_S5], w_out[D_S5:])
    loss = jnp.sum(loss8[:, 0, 0])
    d_w_out = jnp.concatenate([dws, dwg], axis=0)

    do, dz_gdn, d_norm_w = gnorm_bwd(os_[0], os_[1], z_gdn, nw, dgdo)
    dacts, dacts_c = [], []
    dbg = jnp.zeros_like(bg)
    dbg_c = jnp.zeros_like(bg_c)
    for d in range(N_DIR):
        dact, dbg_d, dbgr_d, ds0 = gdn_bwd(act, bg, bgr, cks[d], do, zeros_state, d=d, name=f"gdn_bwd{d}")
        dact_c, dbgc_d, dbgrc_d, _ = gdn_bwd(act_c, bg_c, bgr_c, cks_c[d], None, ds0, d=d, name=f"gdn_bwd_ctx{d}")
        dacts.append(dact)
        dacts_c.append(dact_c)
        dbg = dbg + dbg_d + _from_chunk_rows(dbgr_d)
        dbg_c = dbg_c + dbgc_d + _from_chunk_rows(dbgrc_d)
    dba, dal, ddt = gates_bwd(ba, alog16, dtb16, dbg, name="gates_bwd")
    dbac, dal_c, ddt_c = gates_bwd(bac, alog16, dtb16, dbg_c, name="gates_bwd_ctx")
    d_a_log = (dal + dal_c)[:, 8:].reshape(1, N_DIR, GDN_HEADS)
    d_dt_bias = (ddt + ddt_c)[:, 8:].reshape(1, N_DIR, GDN_HEADS)
    dqkv, dcw = conv_bwd(qkv, conv16, dacts[0], dacts[1], is_ctx=False, name="conv_bwd")
    dqkvc, dcw_c = conv_bwd(qkvc, conv16, dacts_c[0], dacts_c[1], is_ctx=True, name="conv_bwd_ctx")
    d_conv16 = jnp.sum(dcw, axis=0) + jnp.sum(dcw_c, axis=0)

    du_skip, dy, dz_s5, d_s5_d, d_w_glu, d_b_glu = s5_glu_bwd(u, ys[0], ys[1], z_s5, *glu_w, ds5o)
    du, duc = du_skip, jnp.zeros_like(uc)
    dar, dai, dbbr, dbbi, dcre, dcim = [], [], [], [], [], []
    for d in range(N_DIR):
        du_d, dbre1, dbim1, dct1, dcb1, da1, dh0 = s5_scan_bwd(u, dy, *s5w[d], hins[d],
                                                                jnp.zeros((B, 2, S5_HALF), f32), d=d, name=f"s5_bwd{d}")
        duc_d, dbre2, dbim2, _, _, da2, _ = s5_scan_bwd(uc, None, *s5w[d], hins_c[d], dh0, d=d, name=f"s5_bwd_ctx{d}")
        du, duc = du + du_d, duc + duc_d
        da = da1 + da2
        dar.append(da[0].reshape(S5_GROUPS, S5_STATE))
        dai.append(da[1].reshape(S5_GROUPS, S5_STATE))
        dbbr.append(_block_diag_in_t(dbre1 + dbre2))
        dbbi.append(_block_diag_in_t(dbim1 + dbim2))
        dcre.append(_block_diag_out_t(dct1))
        dcim.append(-_block_diag_out_t(dcb1))
    dlr, dli, dldt, dbre, dbim = s5_zoh_bwd(*zoh_in, expand, jnp.concatenate(dar, 0), jnp.concatenate(dai, 0),
                                            jnp.concatenate(dbbr, 0), jnp.concatenate(dbbi, 0))
    d_s5 = (dlr, dli, dldt, dbre, dbim, jnp.stack(dcre, 0), jnp.stack(dcim, 0))

    padg = lambda a: jnp.concatenate([a, jnp.zeros(a.shape[:2] + (LANES - N_GATE,), f32)], axis=2)
    dw_l, dmod, grad_x = in_proj_bwd(x, mod, (du, dz_s5, dqkv, dz_gdn, padg(dba)), w_in, gx_res, name="in_proj_bwd")
    zc = jnp.zeros_like(uc)
    dw_c, dmod_c = in_proj_bwd(ctx, mod_c, (duc, zc, dqkvc, zc, padg(dbac)), w_in, None, name="in_proj_bwd_ctx")
    d_w_in = dw_l + dw_c
    dmod_c = jnp.sum(dmod_c, axis=0)

    dm_rows = jnp.concatenate([dmod[:, 1], dmod[:, 0], dgate[:, 0]], axis=1)
    dm_ctx = jnp.concatenate([dmod_c[1], dmod_c[0], jnp.zeros((D_MODEL,), f32)])[None]
    dm = jnp.concatenate([dm_rows, dm_ctx, jnp.zeros((8 - B - 1, 3 * D_MODEL), f32)], axis=0)
    dcc, d_w_ada, d_b_ada = ada_bwd(cc, w_ada, dm)
    small = (dcc[B], d_b_ada, *d_s5, d_s5_d, d_b_glu, d_a_log, d_dt_bias, d_norm_w, dlng, dlnb)
    small = tuple(g.reshape(s) for g, s in zip(small, SMALL_SHAPES))
    return loss, grad_x, (d_w_ada, d_w_in, d_w_out, d_w_glu, d_conv16), small


SHARDED = (1, 3, 18, 12, 14)
SMALL = tuple(i for i in range(21) if i not in SHARDED)
W_IN_SHARD = 772


def _conv_rows(w):
    return jnp.concatenate([w.reshape(9, w.shape[-1]), jnp.zeros((CONV_ROWS - 9, w.shape[-1]), f32)], axis=0)


def kernel(x, c, ctx, c_ctx, w_ada, b_ada, w_in, s5_lambda_re, s5_lambda_im, s5_log_dt, s5_b_re, s5_b_im, s5_c_re, s5_c_im, s5_d, w_glu, b_glu, conv_w, gdn_a_log, gdn_dt_bias, gdn_norm_w, w_out, ln_g, ln_b, loss_target, m_c_ctx, m_w_ada, m_b_ada, m_w_in, m_s5_lambda_re, m_s5_lambda_im, m_s5_log_dt, m_s5_b_re, m_s5_b_im, m_s5_c_re, m_s5_c_im, m_s5_d, m_w_glu, m_b_glu, m_conv_w, m_gdn_a_log, m_gdn_dt_bias, m_gdn_norm_w, m_w_out, m_ln_g, m_ln_b, v_c_ctx, v_w_ada, v_b_ada, v_w_in, v_s5_lambda_re, v_s5_lambda_im, v_s5_log_dt, v_s5_b_re, v_s5_b_im, v_s5_c_re, v_s5_c_im, v_s5_d, v_w_glu, v_b_glu, v_conv_w, v_gdn_a_log, v_gdn_dt_bias, v_gdn_norm_w, v_w_out, v_ln_g, v_ln_b):
    weights = [c_ctx, w_ada, b_ada, w_in, s5_lambda_re, s5_lambda_im, s5_log_dt, s5_b_re, s5_b_im, s5_c_re, s5_c_im,
               s5_d, w_glu, b_glu, conv_w, gdn_a_log, gdn_dt_bias, gdn_norm_w, w_out, ln_g, ln_b]
    ms = [m_c_ctx, m_w_ada, m_b_ada, m_w_in, m_s5_lambda_re, m_s5_lambda_im, m_s5_log_dt, m_s5_b_re, m_s5_b_im,
          m_s5_c_re, m_s5_c_im, m_s5_d, m_w_glu, m_b_glu, m_conv_w, m_gdn_a_log, m_gdn_dt_bias, m_gdn_norm_w, m_w_out,
          m_ln_g, m_ln_b]
    vs = [v_c_ctx, v_w_ada, v_b_ada, v_w_in, v_s5_lambda_re, v_s5_lambda_im, v_s5_log_dt, v_s5_b_re, v_s5_b_im,
          v_s5_c_re, v_s5_c_im, v_s5_d, v_w_glu, v_b_glu, v_conv_w, v_gdn_a_log, v_gdn_dt_bias, v_gdn_norm_w, v_w_out,
          v_ln_g, v_ln_b]
    cpos = lax.axis_index("c")
    jchip = 2 * lax.axis_index("x") + lax.axis_index("y")

    conv_shard = _conv_rows(conv_w)
    g_ada, g_in, g_out, g_glu, g_conv = gather_shards(
        [w_ada[0].astype(bf16), w_in[0].astype(bf16), w_out[0].astype(bf16), w_glu[0].astype(bf16), conv_shard])
    w_in_pad = jnp.concatenate([g_in[0], g_in[1], g_in[2], g_in[3], jnp.zeros((D_MODEL, IN_PAD - P_IN), bf16)], axis=1)
    conv16 = g_conv.transpose(1, 0, 2).reshape(CONV_ROWS, 3 * D_GDN)

    loss, grad_x, big, small = local_step(
        x, c, ctx, c_ctx, loss_target, g_ada, b_ada, w_in_pad, s5_lambda_re, s5_lambda_im, s5_log_dt, s5_b_re, s5_b_im,
        s5_c_re, s5_c_im, s5_d, g_glu.reshape(D_S5, D_S5), b_glu, conv16, gdn_a_log, gdn_dt_bias, gdn_norm_w,
        g_out.reshape(D_MODEL, D_MODEL), ln_g, ln_b)
    loss = lax.psum(loss, ("x", "y", "c"))

    d_w_ada, d_w_in, d_w_out, d_w_glu, d_conv16 = big
    slabs = [d_w_ada,
             d_w_in[:, :P_IN].reshape(D_MODEL, 4, W_IN_SHARD).transpose(1, 0, 2),
             d_w_out.reshape(4, D_MODEL // 4, D_MODEL),
             d_w_glu.reshape(4, D_S5 // 4, D_S5),
             d_conv16.reshape(CONV_ROWS, 4, 3 * D_GDN // 4).transpose(1, 0, 2),
             _pack_small(small).reshape(4, SMALL_QUARTER, LANES)]
    got = swap_halves(slabs)
    q32, q16 = [], []
    for t, (s, g) in enumerate(zip(slabs, got)):
        own = lax.dynamic_index_in_dim(s.reshape(4, 2, s.shape[1] // 2, s.shape[2]), cpos, axis=1, keepdims=False)
        a, b = sum_cores(own, g, name=f"sum_cores{t}")
        q32.append(a)
        q16.append(b)
    rec = scatter_to_chips(q16)
    fs = [sum_chips(lax.dynamic_index_in_dim(q, jchip, axis=0, keepdims=False), r, cpos, name=f"sum_chips{t}")
          for t, (q, r) in enumerate(zip(q32, rec))]
    red = [r.reshape(2 * r.shape[1], r.shape[2]) for r in join_halves(fs)]
    g_small = _unpack_small(gather_small(red[5]).reshape(SMALL_TOTAL, LANES))

    grads, deltas, new_m, new_v = [None] * 21, [None] * 21, [None] * 21, [None] * 21
    for t, i in enumerate(SHARDED):
        conv = i == 14
        prep = _conv_rows if conv else (lambda a: a[0])
        d, nm, nv = adamw_2d(prep(weights[i]), red[t], prep(ms[i]), prep(vs[i]), name=f"adamw{t}")
        for lst, val in ((grads, red[t]), (deltas, d), (new_m, nm), (new_v, nv)):
            lst[i] = (val[:9] if conv else val).reshape(weights[i].shape)
    sm = adamw_small([_as_2d(weights[i]) for i in SMALL], [_as_2d(g) for g in g_small], [_as_2d(ms[i]) for i in SMALL],
                     [_as_2d(vs[i]) for i in SMALL])
    for n, i in enumerate(SMALL):
        grads[i] = g_small[n]
        for lst, res in ((deltas, sm[0]), (new_m, sm[1]), (new_v, sm[2])):
            lst[i] = res[n].reshape(weights[i].shape)
    return (loss, grad_x, *grads, *deltas, *new_m, *new_v)
```

```python
import functools

import jax
import jax.numpy as jnp
from jax import lax
from jax.experimental import pallas as pl
from jax.experimental.pallas import tpu as pltpu

f32 = jnp.float32
bf16 = jnp.bfloat16
SDS = jax.ShapeDtypeStruct

D_MODEL = 1024
D_S5 = 512
S5_GROUP = 16
S5_GROUPS = 32
S5_STATE = 64
S5_HALF = S5_GROUPS * S5_STATE
D_GDN = 512
GDN_HEAD = 128
GDN_HEADS = 4
CHUNK = 64
GRID_W = 64
N_DIR = 2
P_IN = 3088
DEEPNORM_ALPHA = 2.0 ** 0.25
LN_EPS = 1e-5
NORM_EPS = 1e-6
ADAM_LR, ADAM_B1, ADAM_B2, ADAM_EPS, ADAM_WD, ADAM_STEP = 0.001, 0.9, 0.999, 1e-08, 0.01, 10

LANES = 128
VMEM_LIMIT = 56 * 1024 * 1024
TOK_TILE = 256
S5_TILE = 256
MESH = pl.DeviceIdType.MESH


def _cparams(n_grid):
    return pltpu.CompilerParams(dimension_semantics=("arbitrary",) * n_grid, vmem_limit_bytes=VMEM_LIMIT)


def _dot(a, b):
    return jnp.dot(a.astype(bf16), b.astype(bf16), preferred_element_type=f32)


def _dot_nt(a, b):
    return lax.dot_general(a.astype(bf16), b.astype(bf16), (((1,), (1,)), ((), ())), preferred_element_type=f32)


def _dot_tn(a, b):
    return lax.dot_general(a.astype(bf16), b.astype(bf16), (((0,), (0,)), ((), ())), preferred_element_type=f32)


def _dot_hi(a, b):
    return jnp.dot(a, b, precision=lax.Precision.HIGHEST, preferred_element_type=f32)


def _dot_h3(a, b):
    return jnp.dot(a, b, precision=lax.Precision.HIGH, preferred_element_type=f32)


@jax.custom_vjp
def _mm(a, b):
    return _dot(a, b)


@jax.custom_vjp
def _mm_nt(a, b):
    return _dot_nt(a, b)


@jax.custom_vjp
def _mm_tn(a, b):
    return _dot_tn(a, b)


_mm.defvjp(lambda a, b: (_dot(a, b), (a, b)), lambda r, g: (_mm_nt(g, r[1]), _mm_tn(r[0], g)))
_mm_nt.defvjp(lambda a, b: (_dot_nt(a, b), (a, b)), lambda r, g: (_mm(g, r[1]), _mm_tn(g, r[0])))
_mm_tn.defvjp(lambda a, b: (_dot_tn(a, b), (a, b)), lambda r, g: (_mm_nt(r[1], g), _mm(r[0], g)))


def _silu(x):
    return x * jax.nn.sigmoid(x)


def _gelu(x):
    return 0.5 * x * (1.0 + lax.erf(x * (2.0 ** -0.5)))


def _resident(shape):
    nd = len(shape)
    return pl.BlockSpec(shape, lambda *_: (0,) * nd, pipeline_mode=pl.Buffered(1))


def _tok(tile, width, nt=None, rev=False):
    if rev:
        return pl.BlockSpec((None, tile, width), lambda b, n: (b, nt - 1 - n, 0))
    return pl.BlockSpec((None, tile, width), lambda b, n: (b, n, 0))


def _per_batch(rows, width):
    return pl.BlockSpec((None, rows, width), lambda b, n: (b, 0, 0))


def _first_step():
    return jnp.logical_and(pl.program_id(0) == 0, pl.program_id(1) == 0)


ADA_SHARD = 3 * D_MODEL // 4


def ada_fwd(cc, w, b):
    def body(cc_ref, w_ref, b_ref, m_ref):
        s = _silu(cc_ref[...]).astype(bf16)
        for j in range(4):
            sl = slice(j * ADA_SHARD, (j + 1) * ADA_SHARD)
            m_ref[:, sl] = _dot(s, w_ref[j]) + b_ref[:, sl]

    return pl.pallas_call(body, name="ada_fwd", out_shape=SDS((8, 3 * D_MODEL), f32),
                          compiler_params=pltpu.CompilerParams(vmem_limit_bytes=VMEM_LIMIT))(cc, w, b)


def ada_bwd(cc, w, dm):
    def body(cc_ref, w_ref, dm_ref, dcc_ref, dw_ref, db_ref):
        s, vjp = jax.vjp(_silu, cc_ref[...])
        ds = jnp.zeros((8, D_MODEL), f32)
        for j in range(4):
            dmj = dm_ref[:, j * ADA_SHARD:(j + 1) * ADA_SHARD]
            ds = ds + _dot_nt(dmj, w_ref[j])
            dw_ref[j] = _dot_tn(s, dmj)
        dcc_ref[...] = vjp(ds)[0]
        db_ref[...] = jnp.sum(dm_ref[...], axis=0, keepdims=True)

    return pl.pallas_call(
        body, name="ada_bwd",
        out_shape=[SDS((8, D_MODEL), f32), SDS((4, D_MODEL, ADA_SHARD), f32), SDS((1, 3 * D_MODEL), f32)],
        compiler_params=pltpu.CompilerParams(vmem_limit_bytes=VMEM_LIMIT))(cc, w, dm)


N_GATE = 2 * N_DIR * GDN_HEADS
IN_WIDTHS = (D_S5, D_S5, 3 * D_GDN, D_GDN, LANES)
IN_OFFS = (0, 512, 1024, 2560, 3072)
IN_PAD = 3200


def in_proj_fwd(x, mod, w, *, name):
    B, L, _ = x.shape
    T = min(TOK_TILE, L)

    def body(x_ref, mod_ref, w_ref, *o_refs):
        h = (x_ref[...] * (1.0 + mod_ref[0:1, :]) + mod_ref[1:2, :]).astype(bf16)
        for o_ref, off, wd in zip(o_refs, IN_OFFS, IN_WIDTHS):
            r = _dot(h, w_ref[:, off:off + wd])
            o_ref[...] = r[:, :o_ref.shape[-1]]

    outw = (D_S5, D_S5, 3 * D_GDN, D_GDN, N_GATE)
    return pl.pallas_call(
        body, name=name, grid=(B, L // T),
        in_specs=[_tok(T, D_MODEL), _per_batch(2, D_MODEL), _resident((D_MODEL, IN_PAD))],
        out_specs=[_tok(T, wd) for wd in outw],
        out_shape=[SDS((B, L, wd), f32) for wd in outw],
        compiler_params=_cparams(2),
    )(x, mod, w)


def in_proj_bwd(x, mod, ds, w, gx_res, *, name):
    B, L, _ = x.shape
    T = min(TOK_TILE, L)
    with_dx = gx_res is not None

    def body(*refs):
        x_ref, mod_ref = refs[0], refs[1]
        d_refs = refs[2:7]
        w_ref = refs[7]
        k = 8
        if with_dx:
            gx_ref = refs[k]
            k += 1
        dw_ref, dmod_ref = refs[k], refs[k + 1]
        if with_dx:
            dx_ref = refs[k + 2]
        n = pl.program_id(1)

        @pl.when(_first_step())
        def _():
            dw_ref[...] = jnp.zeros_like(dw_ref)

        @pl.when(n == 0)
        def _():
            dmod_ref[...] = jnp.zeros_like(dmod_ref)

        xv = x_ref[...]
        scale1 = 1.0 + mod_ref[0:1, :]
        h = (xv * scale1 + mod_ref[1:2, :]).astype(bf16)
        dh = jnp.zeros((T, D_MODEL), f32)
        for d_ref, off, wd in zip(d_refs, IN_OFFS, IN_WIDTHS):
            dv = d_ref[...].astype(bf16)
            dh = dh + _dot_nt(dv, w_ref[:, off:off + wd])
            dw_ref[:, off:off + wd] += _dot_tn(h, dv)
        dmod_ref[0:1, :] += jnp.sum(dh * xv, axis=0, keepdims=True)
        dmod_ref[1:2, :] += jnp.sum(dh, axis=0, keepdims=True)
        if with_dx:
            dx_ref[...] = gx_ref[...] + dh * scale1

    in_specs = ([_tok(T, D_MODEL), _per_batch(2, D_MODEL)] + [_tok(T, wd) for wd in IN_WIDTHS]
                + [_resident((D_MODEL, IN_PAD))])
    args = [x, mod, *ds, w]
    out_specs = [_resident((D_MODEL, IN_PAD)), _per_batch(2, D_MODEL)]
    out_shape = [SDS((D_MODEL, IN_PAD), f32), SDS((B, 2, D_MODEL), f32)]
    if with_dx:
        in_specs.append(_tok(T, D_MODEL))
        args.append(gx_res)
        out_specs.append(_tok(T, D_MODEL))
        out_shape.append(SDS((B, L, D_MODEL), f32))
    return pl.pallas_call(body, name=name, grid=(B, L // T), in_specs=in_specs, out_specs=out_specs,
                          out_shape=out_shape, compiler_params=_cparams(2))(*args)


def _s5_zoh(lr, li, ldt, bre, bim, expand):
    dt = jnp.exp(ldt)
    zr, zi = lr * dt, li * dt
    e = jnp.exp(zr)
    ar, ai = e * jnp.cos(zi), e * jnp.sin(zi)
    den = lr * lr + li * li
    czr = ((ar - 1.0) * lr + ai * li) / den
    czi = (ai * lr - (ar - 1.0) * li) / den
    czr_e, czi_e = _dot_hi(czr, expand), _dot_hi(czi, expand)
    return ar, ai, czr_e * bre - czi_e * bim, czr_e * bim + czi_e * bre


_ZOH_OUT = [(N_DIR * S5_GROUPS, S5_STATE)] * 2 + [(N_DIR * S5_GROUPS, S5_STATE * S5_GROUP)] * 2


def s5_zoh_fwd(lr, li, ldt, bre, bim, expand):
    def body(lr_ref, li_ref, ldt_ref, bre_ref, bim_ref, e_ref, ar_ref, ai_ref, bbr_ref, bbi_ref):
        ar, ai, bbr, bbi = _s5_zoh(lr_ref[...], li_ref[...], ldt_ref[...], bre_ref[...], bim_ref[...], e_ref[...])
        ar_ref[...], ai_ref[...], bbr_ref[...], bbi_ref[...] = ar, ai, bbr, bbi

    return pl.pallas_call(body, name="s5_zoh_fwd", out_shape=[SDS(s, f32) for s in _ZOH_OUT])(
        lr, li, ldt, bre, bim, expand)


def s5_zoh_bwd(lr, li, ldt, bre, bim, expand, dar, dai, dbbr, dbbi):
    def body(lr_ref, li_ref, ldt_ref, bre_ref, bim_ref, e_ref, dar_ref, dai_ref, dbbr_ref, dbbi_ref,
             dlr_ref, dli_ref, dldt_ref, dbre_ref, dbim_ref):
        ev = e_ref[...]
        _, vjp = jax.vjp(lambda a, b, c, d, e: _s5_zoh(a, b, c, d, e, ev),
                         lr_ref[...], li_ref[...], ldt_ref[...], bre_ref[...], bim_ref[...])
        outs = vjp((dar_ref[...], dai_ref[...], dbbr_ref[...], dbbi_ref[...]))
        dlr_ref[...], dli_ref[...], dldt_ref[...], dbre_ref[...], dbim_ref[...] = outs

    shapes = [lr.shape, li.shape, ldt.shape, bre.shape, bim.shape]
    return pl.pallas_call(body, name="s5_zoh_bwd", out_shape=[SDS(s, f32) for s in shapes])(
        lr, li, ldt, bre, bim, expand, dar, dai, dbbr, dbbi)


def _scan_rows(T, rev, ar, ai, hr0, hi0, r_ref, i_ref, off):
    def step(i, carry):
        hr, hi = carry
        t = off + ((T - 1 - i) if rev else i)
        nr = ar * hr - ai * hi + r_ref[pl.ds(t, 1), :]
        ni = ar * hi + ai * hr + i_ref[pl.ds(t, 1), :]
        r_ref[pl.ds(t, 1), :] = nr
        i_ref[pl.ds(t, 1), :] = ni
        return nr, ni

    return lax.fori_loop(0, T, step, (hr0, hi0), unroll=4)


S5_BLOCKS = 4
S5_BC = D_S5 // S5_BLOCKS
S5_BS = S5_HALF // S5_BLOCKS


def _s5_in(uv, bre_ref, bim_ref, hr_ref, hi_ref, off, T):
    for jb in range(S5_BLOCKS):
        uj = uv[:, jb * S5_BC:(jb + 1) * S5_BC]
        hr_ref[off:off + T, jb * S5_BS:(jb + 1) * S5_BS] = _dot(uj, bre_ref[jb])
        hi_ref[off:off + T, jb * S5_BS:(jb + 1) * S5_BS] = _dot(uj, bim_ref[jb])


def s5_scan_fwd(u, bre, bim, ctop, cbot, arow, h0, *, d, need_y, name):
    B, L, _ = u.shape
    T = min(S5_TILE, L)
    nt = L // T
    rev = d == 1

    def body(u_ref, bre_ref, bim_ref, ct_ref, cb_ref, a_ref, h0_ref, *rest):
        if need_y:
            y_ref, hin_ref, hend_ref, hr_scr, hi_scr, h_scr = rest
        else:
            hin_ref, hend_ref, hr_scr, hi_scr, h_scr = rest
        n = pl.program_id(1)

        @pl.when(n == 0)
        def _():
            h_scr[...] = h0_ref[...]

        hin_ref[...] = h_scr[...]
        _s5_in(u_ref[...].astype(bf16), bre_ref, bim_ref, hr_scr, hi_scr, 0, T)
        hr, hi = _scan_rows(T, rev, a_ref[0:1, :], a_ref[1:2, :], h_scr[0:1, :], h_scr[1:2, :], hr_scr, hi_scr, 0)
        h_scr[0:1, :] = hr
        h_scr[1:2, :] = hi
        if need_y:
            for jb in range(S5_BLOCKS):
                st = slice(jb * S5_BS, (jb + 1) * S5_BS)
                y_ref[:, jb * S5_BC:(jb + 1) * S5_BC] = _dot(hr_scr[:, st], ct_ref[jb]) + _dot(hi_scr[:, st], cb_ref[jb])

        @pl.when(n == nt - 1)
        def _():
            hend_ref[...] = h_scr[...]

    state = _per_batch(2, S5_HALF)
    hin_spec = pl.BlockSpec((None, None, 2, S5_HALF), (lambda b, n: (b, nt - 1 - n, 0, 0)) if rev else (lambda b, n: (b, n, 0, 0)))
    out_specs = [hin_spec, state]
    out_shape = [SDS((B, nt, 2, S5_HALF), f32), SDS((B, 2, S5_HALF), f32)]
    if need_y:
        out_specs.insert(0, _tok(T, D_S5, nt, rev))
        out_shape.insert(0, SDS((B, L, D_S5), f32))
    w_in, w_out = _resident((S5_BLOCKS, S5_BC, S5_BS)), _resident((S5_BLOCKS, S5_BS, S5_BC))
    return pl.pallas_call(
        body, name=name, grid=(B, nt),
        in_specs=[_tok(T, D_S5, nt, rev), w_in, w_in, w_out, w_out, _resident((2, S5_HALF)), state],
        out_specs=out_specs, out_shape=out_shape,
        scratch_shapes=[pltpu.VMEM((T, S5_HALF), f32), pltpu.VMEM((T, S5_HALF), f32), pltpu.VMEM((2, S5_HALF), f32)],
        compiler_params=_cparams(2),
    )(u, bre, bim, ctop, cbot, arow, h0)


def s5_scan_bwd(u, dy, bre, bim, ctop, cbot, arow, hin, dhend, *, d, name):
    B, L, _ = u.shape
    T = min(S5_TILE, L)
    nt = L // T
    rev = d == 1
    has_dy = dy is not None
    PAD = 8

    def body(*refs):
        u_ref = refs[0]
        k = 1
        if has_dy:
            dy_ref = refs[1]
            k = 2
        bre_ref, bim_ref, ct_ref, cb_ref, a_ref, hin_ref, dhend_ref = refs[k:k + 7]
        du_ref, dbre_ref, dbim_ref, dct_ref, dcb_ref, da_ref, dh0_ref = refs[k + 7:k + 14]
        hr_scr, hi_scr, gr_scr, gi_scr, p_scr = refs[k + 14:]
        n = pl.program_id(1)

        @pl.when(_first_step())
        def _():
            for r in (dbre_ref, dbim_ref, dct_ref, dcb_ref, da_ref):
                r[...] = jnp.zeros_like(r)

        @pl.when(n == 0)
        def _():
            p_scr[...] = dhend_ref[...]

        ar, ai = a_ref[0:1, :], a_ref[1:2, :]
        uv = u_ref[...].astype(bf16)
        _s5_in(uv, bre_ref, bim_ref, hr_scr, hi_scr, PAD, T)
        prev_row = PAD + T if rev else PAD - 1
        hr_scr[prev_row:prev_row + 1, :] = hin_ref[0:1, :]
        hi_scr[prev_row:prev_row + 1, :] = hin_ref[1:2, :]
        _scan_rows(T, rev, ar, ai, hin_ref[0:1, :], hin_ref[1:2, :], hr_scr, hi_scr, PAD)
        if has_dy:
            dyv = dy_ref[...].astype(bf16)
            for jb in range(S5_BLOCKS):
                st = slice(jb * S5_BS, (jb + 1) * S5_BS)
                dyj = dyv[:, jb * S5_BC:(jb + 1) * S5_BC]
                gr_scr[:, st] = _dot_nt(dyj, ct_ref[jb])
                gi_scr[:, st] = _dot_nt(dyj, cb_ref[jb])
                dct_ref[jb] += _dot_tn(hr_scr[PAD:PAD + T, st], dyj)
                dcb_ref[jb] += _dot_tn(hi_scr[PAD:PAD + T, st], dyj)
        else:
            gr_scr[...] = jnp.zeros_like(gr_scr)
            gi_scr[...] = jnp.zeros_like(gi_scr)

        def step(i, carry):
            pr, pi, dar, dai = carry
            t = i if rev else T - 1 - i
            gr = gr_scr[pl.ds(t, 1), :] + pr
            gi = gi_scr[pl.ds(t, 1), :] + pi
            gr_scr[pl.ds(t, 1), :] = gr
            gi_scr[pl.ds(t, 1), :] = gi
            tp = PAD + t + (1 if rev else -1)
            hpr = hr_scr[pl.ds(tp, 1), :]
            hpi = hi_scr[pl.ds(tp, 1), :]
            dar = dar + hpr * gr + hpi * gi
            dai = dai + hpr * gi - hpi * gr
            return ar * gr + ai * gi, ar * gi - ai * gr, dar, dai

        zero = jnp.zeros((1, S5_HALF), f32)
        pr, pi, dar, dai = lax.fori_loop(0, T, step, (p_scr[0:1, :], p_scr[1:2, :], zero, zero), unroll=4)
        p_scr[0:1, :] = pr
        p_scr[1:2, :] = pi
        da_ref[0:1, :] += dar
        da_ref[1:2, :] += dai
        for jb in range(S5_BLOCKS):
            st = slice(jb * S5_BS, (jb + 1) * S5_BS)
            ch = slice(jb * S5_BC, (jb + 1) * S5_BC)
            gr_j = gr_scr[:, st].astype(bf16)
            gi_j = gi_scr[:, st].astype(bf16)
            du_ref[:, ch] = _dot_nt(gr_j, bre_ref[jb]) + _dot_nt(gi_j, bim_ref[jb])
            dbre_ref[jb] += _dot_tn(uv[:, ch], gr_j)
            dbim_ref[jb] += _dot_tn(uv[:, ch], gi_j)

        @pl.when(n == nt - 1)
        def _():
            dh0_ref[...] = p_scr[...]

    brev = not rev
    state = _per_batch(2, S5_HALF)
    hin_spec = pl.BlockSpec((None, None, 2, S5_HALF), (lambda b, n: (b, nt - 1 - n, 0, 0)) if brev else (lambda b, n: (b, n, 0, 0)))
    w_in, w_out = _resident((S5_BLOCKS, S5_BC, S5_BS)), _resident((S5_BLOCKS, S5_BS, S5_BC))
    wspecs = [w_in, w_in, w_out, w_out]
    in_specs = [_tok(T, D_S5, nt, brev)] + ([_tok(T, D_S5, nt, brev)] if has_dy else []) + wspecs + [
        _resident((2, S5_HALF)), hin_spec, state]
    args = [u] + ([dy] if has_dy else []) + [bre, bim, ctop, cbot, arow, hin, dhend]
    return pl.pallas_call(
        body, name=name, grid=(B, nt), in_specs=in_specs,
        out_specs=[_tok(T, D_S5, nt, brev)] + wspecs + [_resident((2, S5_HALF)), state],
        out_shape=[SDS((B, L, D_S5), f32), SDS((S5_BLOCKS, S5_BC, S5_BS), f32), SDS((S5_BLOCKS, S5_BC, S5_BS), f32),
                   SDS((S5_BLOCKS, S5_BS, S5_BC), f32), SDS((S5_BLOCKS, S5_BS, S5_BC), f32), SDS((2, S5_HALF), f32),
                   SDS((B, 2, S5_HALF), f32)],
        scratch_shapes=[pltpu.VMEM((T + 2 * PAD, S5_HALF), f32), pltpu.VMEM((T + 2 * PAD, S5_HALF), f32),
                        pltpu.VMEM((T, S5_HALF), f32), pltpu.VMEM((T, S5_HALF), f32), pltpu.VMEM((2, S5_HALF), f32)],
        compiler_params=_cparams(2),
    )(*args)


def _glu_fn(u, y0, y1, z, dsk, wg, bg):
    g = _gelu(dsk * u + y0 + y1)
    return g * jax.nn.sigmoid(_mm(g, wg) + bg) * _silu(z)


def s5_glu_fwd(u, y0, y1, z, dsk, wg, bg):
    B, L, _ = u.shape
    T = min(TOK_TILE, L)

    def body(u_ref, y0_ref, y1_ref, z_ref, dsk_ref, wg_ref, bg_ref, o_ref):
        o_ref[...] = _glu_fn(u_ref[...], y0_ref[...], y1_ref[...], z_ref[...], dsk_ref[...], wg_ref[...].astype(f32),
                             bg_ref[...])

    t = _tok(T, D_S5)
    return pl.pallas_call(
        body, name="s5_glu_fwd", grid=(B, L // T),
        in_specs=[t, t, t, t, _resident((1, D_S5)), _resident((D_S5, D_S5)), _resident((1, D_S5))],
        out_specs=t, out_shape=SDS((B, L, D_S5), f32), compiler_params=_cparams(2),
    )(u, y0, y1, z, dsk, wg, bg)


def s5_glu_bwd(u, y0, y1, z, dsk, wg, bg, dout):
    B, L, _ = u.shape
    T = min(TOK_TILE, L)

    def body(u_ref, y0_ref, y1_ref, z_ref, dsk_ref, wg_ref, bg_ref, do_ref, du_ref, dy_ref, dz_ref,
             ddsk_ref, dwg_ref, dbg_ref):
        @pl.when(_first_step())
        def _():
            for r in (ddsk_ref, dwg_ref, dbg_ref):
                r[...] = jnp.zeros_like(r)

        _, vjp = jax.vjp(_glu_fn, u_ref[...], y0_ref[...], y1_ref[...], z_ref[...], dsk_ref[...],
                         wg_ref[...].astype(f32), bg_ref[...])
        du, dy, _, dz, ddsk, dwg, dbg = vjp(do_ref[...])
        du_ref[...], dy_ref[...], dz_ref[...] = du, dy, dz
        ddsk_ref[...] += ddsk
        dwg_ref[...] += dwg
        dbg_ref[...] += dbg

    t = _tok(T, D_S5)
    small = [_resident((1, D_S5)), _resident((D_S5, D_S5)), _resident((1, D_S5))]
    return pl.pallas_call(
        body, name="s5_glu_bwd", grid=(B, L // T),
        in_specs=[t, t, t, t] + small + [t], out_specs=[t, t, t] + small,
        out_shape=[SDS((B, L, D_S5), f32)] * 3 + [SDS((1, D_S5), f32), SDS((D_S5, D_S5), f32), SDS((1, D_S5), f32)],
        compiler_params=_cparams(2),
    )(u, y0, y1, z, dsk, wg, bg, dout)


CONV_ROWS = 16


def _conv_taps(L, is_ctx):
    t = lax.broadcasted_iota(jnp.int32, (L, 1), 0)
    taps = []
    for di in ((1,) if is_ctx else (0, 1, 2)):
        for dj in (0, 1, 2):
            s = (0 if is_ctx else GRID_W * (di - 1)) + (dj - 1)
            if is_ctx:
                ok = jnp.logical_and(t + s >= 0, t + s < L)
            else:
                col = jnp.bitwise_and(t, GRID_W - 1) + (dj - 1)
                row = t + GRID_W * (di - 1)
                ok = jnp.logical_and(jnp.logical_and(col >= 0, col < GRID_W), jnp.logical_and(row >= 0, row < L))
            taps.append((di * 3 + dj, s, ok.astype(f32)))
    return taps


def _shift(x, s):
    L = x.shape[0]
    k = (-s) % L
    return x if k == 0 else pltpu.roll(x, k, axis=0)


def _qk_post(pre, is_norm, scale):
    s = _silu(pre)
    nrm = lax.rsqrt(jnp.sum(s * s, axis=-1, keepdims=True) + NORM_EPS)
    return s * jnp.where(is_norm, nrm * scale, 1.0)


def _conv_kind():
    ct = pl.program_id(1)
    return ct < 2 * GDN_HEADS, jnp.where(ct < GDN_HEADS, GDN_HEAD ** -0.5, 1.0).astype(f32)


def _conv_pre(xv, w_ref, taps):
    pre = jnp.zeros_like(xv)
    for r, s, m in taps:
        pre = pre + w_ref[r:r + 1, :] * (m * _shift(xv, s))
    return pre


def conv_fwd(qkv, w16, *, is_ctx, name):
    B, L, C = qkv.shape
    spec = pl.BlockSpec((None, L, GDN_HEAD), lambda b, ct: (b, 0, ct))
    wspec = pl.BlockSpec((CONV_ROWS, GDN_HEAD), lambda b, ct: (0, ct))

    def body(x_ref, w_ref, o_ref):
        is_norm, scale = _conv_kind()
        o_ref[...] = _qk_post(_conv_pre(x_ref[...], w_ref, _conv_taps(L, is_ctx)), is_norm, scale)

    return pl.pallas_call(body, name=name, grid=(B, C // GDN_HEAD), in_specs=[spec, wspec], out_specs=spec,
                          out_shape=SDS((B, L, C), f32), compiler_params=_cparams(2))(qkv, w16)


def conv_bwd(qkv, w16, da0, da1, *, is_ctx, name):
    B, L, C = qkv.shape
    spec = pl.BlockSpec((None, L, GDN_HEAD), lambda b, ct: (b, 0, ct))
    wspec = pl.BlockSpec((CONV_ROWS, GDN_HEAD), lambda b, ct: (0, ct))
    dwspec = pl.BlockSpec((None, CONV_ROWS, GDN_HEAD), lambda b, ct: (b, 0, ct))

    def body(x_ref, w_ref, d0_ref, d1_ref, dx_ref, dw_ref):
        is_norm, scale = _conv_kind()
        taps = _conv_taps(L, is_ctx)
        xv = x_ref[...]
        _, vjp = jax.vjp(lambda p: _qk_post(p, is_norm, scale), _conv_pre(xv, w_ref, taps))
        dpre = vjp(d0_ref[...] + d1_ref[...])[0]
        dx = jnp.zeros_like(xv)
        dw_ref[...] = jnp.zeros_like(dw_ref)
        for r, s, m in taps:
            md = m * dpre
            dx = dx + _shift(w_ref[r:r + 1, :] * md, -s)
            dw_ref[r:r + 1, :] = jnp.sum(md * _shift(xv, s), axis=0, keepdims=True)
        dx_ref[...] = dx

    return pl.pallas_call(body, name=name, grid=(B, C // GDN_HEAD), in_specs=[spec, wspec, spec, spec],
                          out_specs=[spec, dwspec], out_shape=[SDS((B, L, C), f32), SDS((B, CONV_ROWS, C), f32)],
                          compiler_params=_cparams(2))(qkv, w16, da0, da1)


def _gates_fn(ba, alog, dtb):
    T = ba.shape[0]
    lane = lax.broadcasted_iota(jnp.int32, ba.shape, 1)
    ii = lax.broadcasted_iota(jnp.int32, (T, T), 0)
    jj = lax.broadcasted_iota(jnp.int32, (T, T), 1)
    same = jnp.right_shift(ii, 6) == jnp.right_shift(jj, 6)
    lmat = jnp.logical_and(same, ii >= jj).astype(f32)
    umat = jnp.logical_and(same, ii <= jj).astype(f32)
    g = jnp.where(lane >= 8, -jnp.exp(alog) * jax.nn.softplus(ba + dtb), 0.0)
    gc = jnp.where(lane >= 12, _dot_hi(umat, g), _dot_hi(lmat, g))
    return jnp.where(lane < 8, jax.nn.sigmoid(ba), gc)


def gates_fwd(ba, alog, dtb, *, name):
    B, L, _ = ba.shape
    T = min(TOK_TILE, L)
    t = _tok(T, N_GATE)

    def body(ba_ref, al_ref, dt_ref, o_ref):
        o_ref[...] = _gates_fn(ba_ref[...], al_ref[...], dt_ref[...])

    return pl.pallas_call(body, name=name, grid=(B, L // T),
                          in_specs=[t, _resident((1, N_GATE)), _resident((1, N_GATE))], out_specs=t,
                          out_shape=SDS((B, L, N_GATE), f32), compiler_params=_cparams(2))(ba, alog, dtb)


def gates_bwd(ba, alog, dtb, dbg, *, name):
    B, L, _ = ba.shape
    T = min(TOK_TILE, L)
    t = _tok(T, N_GATE)
    small = _resident((1, N_GATE))

    def body(ba_ref, al_ref, dt_ref, d_ref, dba_ref, dal_ref, ddt_ref):
        @pl.when(_first_step())
        def _():
            dal_ref[...] = jnp.zeros_like(dal_ref)
            ddt_ref[...] = jnp.zeros_like(ddt_ref)

        _, vjp = jax.vjp(_gates_fn, ba_ref[...], al_ref[...], dt_ref[...])
        dba, dal, ddt = vjp(d_ref[...])
        dba_ref[...] = dba
        dal_ref[...] += dal
        ddt_ref[...] += ddt

    return pl.pallas_call(body, name=name, grid=(B, L // T), in_specs=[t, small, small, t],
                          out_specs=[t, small, small],
                          out_shape=[SDS((B, L, N_GATE), f32), SDS((1, N_GATE), f32), SDS((1, N_GATE), f32)],
                          compiler_params=_cparams(2))(ba, alog, dtb, dbg)


@jax.custom_vjp
def _inv_unit_tri(mats):
    n = mats[0].shape[0]
    eye = (lax.broadcasted_iota(jnp.int32, (n, n), 0) == lax.broadcasted_iota(jnp.int32, (n, n), 1)).astype(f32)
    xs = [eye - a for a in mats]
    ps = [_dot_h3(a, a) for a in mats]
    k = 2
    while k < n:
        xs = [x + _dot_h3(x, p) for x, p in zip(xs, ps)]
        k *= 2
        if k < n:
            ps = [_dot_h3(p, p) for p in ps]
    return tuple(xs)


def _inv_unit_tri_fwd(mats):
    xs = _inv_unit_tri(mats)
    return xs, xs


def _inv_unit_tri_bwd(xs, dxs):
    xts = [x.T for x in xs]
    ts = [_dot_h3(xt, dx) for xt, dx in zip(xts, dxs)]
    return (tuple(-_dot_h3(t, xt) for t, xt in zip(ts, xts)),)


_inv_unit_tri.defvjp(_inv_unit_tri_fwd, _inv_unit_tri_bwd)


def _gdn_chunk(heads, *, rev):
    n = heads[0][0].shape[0]
    ii = lax.broadcasted_iota(jnp.int32, (n, n), 0)
    jj = lax.broadcasted_iota(jnp.int32, (n, n), 1)
    lower = (ii <= jj) if rev else (ii >= jj)
    strict = (ii < jj) if rev else (ii > jj)
    last = 0 if rev else n - 1
    row = lax.broadcasted_iota(jnp.int32, (n, 1), 0)
    H = range(len(heads))
    q, k, v, beta, gc, gr, s = (list(t) for t in zip(*heads))
    decay = [jnp.where(lower, jnp.exp(jnp.where(lower, gc[h] - gr[h], 0.0)), 0.0) for h in H]
    kk = [_mm_nt(k[h], k[h]) for h in H]
    qk = [_mm_nt(q[h], k[h]) * decay[h] for h in H]
    qs = [_mm(q[h], s[h]) for h in H]
    a_mat = tuple(jnp.where(strict, beta[h] * kk[h] * decay[h], 0.0) for h in H)
    gamma = [jnp.exp(gc[h]) for h in H]
    g_last = [jnp.sum(jnp.where(row == last, gc[h], 0.0), axis=0, keepdims=True) for h in H]
    tinv = _inv_unit_tri(a_mat)
    u0 = [_dot_h3(tinv[h], beta[h] * v[h]) for h in H]
    w = [_dot_h3(tinv[h], (beta[h] * gamma[h]) * k[h]) for h in H]
    k_out = [k[h] * jnp.exp(g_last[h] - gc[h]) for h in H]
    u = [u0[h] - _mm(w[h], s[h]) for h in H]
    o = [gamma[h] * qs[h] + _mm(qk[h], u[h]) for h in H]
    s_new = [jnp.exp(g_last[h]) * s[h] + _mm_tn(k_out[h], u[h]) for h in H]
    return tuple((o[h], s_new[h]) for h in H)


def _gdn_specs(B, nc, rev):
    def cidx(n):
        return (nc - 1 - n) if rev else n
    tok = lambda width: pl.BlockSpec((B, CHUNK, width), lambda n: (0, cidx(n), 0))
    rowspec = pl.BlockSpec((B, None, N_GATE, CHUNK), lambda n: (0, cidx(n), 0, 0))
    st = pl.BlockSpec((B, GDN_HEADS, GDN_HEAD, GDN_HEAD), lambda n: (0, 0, 0, 0))
    ck = pl.BlockSpec((B, None, GDN_HEADS, GDN_HEAD, GDN_HEAD), lambda n: (0, cidx(n), 0, 0, 0))
    return tok, rowspec, st, ck


def _gdn_head_args(qkv_ref, bg_ref, bgr_ref, b, d, h):
    col = d * GDN_HEADS + h
    q = qkv_ref[b, :, h * GDN_HEAD:(h + 1) * GDN_HEAD]
    k = qkv_ref[b, :, D_GDN + h * GDN_HEAD:D_GDN + (h + 1) * GDN_HEAD]
    v = qkv_ref[b, :, 2 * D_GDN + h * GDN_HEAD:2 * D_GDN + (h + 1) * GDN_HEAD]
    bgv = bg_ref[b]
    return q, k, v, bgv[:, col:col + 1], bgv[:, 8 + col:9 + col], bgr_ref[b][8 + col:9 + col, :]


def gdn_fwd(qkv, bg, bgr, s0, *, d, need_o, name):
    B, L, _ = qkv.shape
    nc = L // CHUNK
    rev = d == 1
    tok, rowspec, st, ck = _gdn_specs(B, nc, rev)
    bh = [(b, h) for b in range(B) for h in range(GDN_HEADS)]

    def body(qkv_ref, bg_ref, bgr_ref, s0_ref, *rest):
        if need_o:
            o_ref, ck_ref, sf_ref, s_scr = rest
        else:
            ck_ref, sf_ref, s_scr = rest
        n = pl.program_id(0)

        @pl.when(n == 0)
        def _():
            s_scr[...] = s0_ref[...]

        ck_ref[...] = s_scr[...]
        heads = tuple(_gdn_head_args(qkv_ref, bg_ref, bgr_ref, b, d, h) + (s_scr[b, h],) for b, h in bh)
        for (b, h), (o, s_new) in zip(bh, _gdn_chunk(heads, rev=rev)):
            if need_o:
                o_ref[b, :, h * GDN_HEAD:(h + 1) * GDN_HEAD] = o
            s_scr[b, h] = s_new

        @pl.when(n == nc - 1)
        def _():
            sf_ref[...] = s_scr[...]

    out_specs = [ck, st]
    out_shape = [SDS((B, nc, GDN_HEADS, GDN_HEAD, GDN_HEAD), f32), SDS((B, GDN_HEADS, GDN_HEAD, GDN_HEAD), f32)]
    if need_o:
        out_specs.insert(0, tok(D_GDN))
        out_shape.insert(0, SDS((B, L, D_GDN), f32))
    return pl.pallas_call(
        body, name=name, grid=(nc,), in_specs=[tok(3 * D_GDN), tok(N_GATE), rowspec, st],
        out_specs=out_specs, out_shape=out_shape,
        scratch_shapes=[pltpu.VMEM((B, GDN_HEADS, GDN_HEAD, GDN_HEAD), f32)], compiler_params=_cparams(1),
    )(qkv, bg, bgr, s0)


def gdn_bwd(qkv, bg, bgr, ck, do, dsf, *, d, name):
    B, L, _ = qkv.shape
    nc = L // CHUNK
    rev = d == 1
    has_do = do is not None
    tok, rowspec, st, ckspec = _gdn_specs(B, nc, not rev)
    bh = [(b, h) for b in range(B) for h in range(GDN_HEADS)]

    def body(*refs):
        qkv_ref, bg_ref, bgr_ref, ck_ref = refs[:4]
        k = 4
        if has_do:
            do_ref = refs[4]
            k = 5
        dsf_ref, dqkv_ref, dbg_ref, dbgr_ref, ds0_ref, ds_scr = refs[k:]
        n = pl.program_id(0)

        @pl.when(n == 0)
        def _():
            ds_scr[...] = dsf_ref[...]

        lane = lax.broadcasted_iota(jnp.int32, (CHUNK, N_GATE), 1)
        sub = lax.broadcasted_iota(jnp.int32, (N_GATE, CHUNK), 0)
        heads = tuple(_gdn_head_args(qkv_ref, bg_ref, bgr_ref, b, d, h) + (ck_ref[b, h],) for b, h in bh)
        _, vjp = jax.vjp(functools.partial(_gdn_chunk, rev=rev), heads)
        cts = tuple(((do_ref[b, :, h * GDN_HEAD:(h + 1) * GDN_HEAD] if has_do else jnp.zeros((CHUNK, GDN_HEAD), f32)),
                     ds_scr[b, h]) for b, h in bh)
        (dheads,) = vjp(cts)
        dbg_acc = [jnp.zeros((CHUNK, N_GATE), f32) for _ in range(B)]
        dbgr_acc = [jnp.zeros((N_GATE, CHUNK), f32) for _ in range(B)]
        for (b, h), (dq, dk, dv, db, dgc, dgr, ds) in zip(bh, dheads):
            col = d * GDN_HEADS + h
            dqkv_ref[b, :, h * GDN_HEAD:(h + 1) * GDN_HEAD] = dq
            dqkv_ref[b, :, D_GDN + h * GDN_HEAD:D_GDN + (h + 1) * GDN_HEAD] = dk
            dqkv_ref[b, :, 2 * D_GDN + h * GDN_HEAD:2 * D_GDN + (h + 1) * GDN_HEAD] = dv
            dbg_acc[b] = dbg_acc[b] + jnp.where(lane == col, db, 0.0) + jnp.where(lane == 8 + col, dgc, 0.0)
            dbgr_acc[b] = dbgr_acc[b] + jnp.where(sub == 8 + col, dgr, 0.0)
            ds_scr[b, h] = ds
        for b in range(B):
            dbg_ref[b] = dbg_acc[b]
            dbgr_ref[b] = dbgr_acc[b]

        @pl.when(n == nc - 1)
        def _():
            ds0_ref[...] = ds_scr[...]

    in_specs = [tok(3 * D_GDN), tok(N_GATE), rowspec, ckspec] + ([tok(D_GDN)] if has_do else []) + [st]
    args = [qkv, bg, bgr, ck] + ([do] if has_do else []) + [dsf]
    return pl.pallas_call(
        body, name=name, grid=(nc,), in_specs=in_specs,
        out_specs=[tok(3 * D_GDN), tok(N_GATE), rowspec, st],
        out_shape=[SDS((B, L, 3 * D_GDN), f32), SDS((B, L, N_GATE), f32), SDS((B, nc, N_GATE, CHUNK), f32),
                   SDS((B, GDN_HEADS, GDN_HEAD, GDN_HEAD), f32)],
        scratch_shapes=[pltpu.VMEM((B, GDN_HEADS, GDN_HEAD, GDN_HEAD), f32)], compiler_params=_cparams(1),
    )(*args)


def _gnorm_fn(o0, o1, z, w):
    o = o0 + o1
    return o * lax.rsqrt(jnp.mean(o * o, axis=-1, keepdims=True) + NORM_EPS) * w * _silu(z)


def gnorm_fwd(o0, o1, z, w):
    B, L, _ = o0.shape
    T = min(TOK_TILE, L)
    t = _tok(T, D_GDN)

    def body(o0_ref, o1_ref, z_ref, w_ref, out_ref):
        for h in range(GDN_HEADS):
            sl = slice(h * GDN_HEAD, (h + 1) * GDN_HEAD)
            out_ref[:, sl] = _gnorm_fn(o0_ref[:, sl], o1_ref[:, sl], z_ref[:, sl], w_ref[...])

    return pl.pallas_call(body, name="gnorm_fwd", grid=(B, L // T), in_specs=[t, t, t, _resident((1, GDN_HEAD))],
                          out_specs=t, out_shape=SDS((B, L, D_GDN), f32), compiler_params=_cparams(2))(o0, o1, z, w)


def gnorm_bwd(o0, o1, z, w, dout):
    B, L, _ = o0.shape
    T = min(TOK_TILE, L)
    t = _tok(T, D_GDN)

    def body(o0_ref, o1_ref, z_ref, w_ref, d_ref, do_ref, dz_ref, dw_ref):
        @pl.when(_first_step())
        def _():
            dw_ref[...] = jnp.zeros_like(dw_ref)

        for h in range(GDN_HEADS):
            sl = slice(h * GDN_HEAD, (h + 1) * GDN_HEAD)
            _, vjp = jax.vjp(_gnorm_fn, o0_ref[:, sl], o1_ref[:, sl], z_ref[:, sl], w_ref[...])
            do, _, dz, dw = vjp(d_ref[:, sl])
            do_ref[:, sl] = do
            dz_ref[:, sl] = dz
            dw_ref[...] += dw

    return pl.pallas_call(body, name="gnorm_bwd", grid=(B, L // T),
                          in_specs=[t, t, t, _resident((1, GDN_HEAD)), t], out_specs=[t, t, _resident((1, GDN_HEAD))],
                          out_shape=[SDS((B, L, D_GDN), f32), SDS((B, L, D_GDN), f32), SDS((1, GDN_HEAD), f32)],
                          compiler_params=_cparams(2))(o0, o1, z, w, dout)


def _head_loss(y, x, gate, lng, lnb, tgt):
    r = DEEPNORM_ALPHA * x + gate * y
    mu = jnp.mean(r, axis=-1, keepdims=True)
    rc = r - mu
    var = jnp.mean(rc * rc, axis=-1, keepdims=True)
    err = rc * lax.rsqrt(var + LN_EPS) * lng + lnb - tgt
    return (0.5 / D_MODEL) * jnp.sum(jnp.sum(err * err, axis=-1, keepdims=True), axis=0, keepdims=True)


def head_fwd_bwd(s5o, gdo, x, tgt, gate, lng, lnb, ws, wg):
    B, L, _ = x.shape
    T = min(TOK_TILE, L)

    def body(s_ref, g_ref, x_ref, t_ref, gate_ref, lng_ref, lnb_ref, ws_ref, wg_ref,
             loss_ref, ds_ref, dg_ref, gx_ref, dws_ref, dwg_ref, dgate_ref, dlng_ref, dlnb_ref):
        n = pl.program_id(1)

        @pl.when(_first_step())
        def _():
            for r in (dws_ref, dwg_ref, dlng_ref, dlnb_ref):
                r[...] = jnp.zeros_like(r)

        @pl.when(n == 0)
        def _():
            loss_ref[...] = jnp.zeros_like(loss_ref)
            dgate_ref[...] = jnp.zeros_like(dgate_ref)

        sv = s_ref[...].astype(bf16)
        gv = g_ref[...].astype(bf16)
        y = _dot(sv, ws_ref[...]) + _dot(gv, wg_ref[...])
        loss, vjp = jax.vjp(lambda *a: _head_loss(*a, t_ref[...]), y, x_ref[...], gate_ref[...], lng_ref[...],
                            lnb_ref[...])
        dy, dx, dgate, dlng, dlnb = vjp(jnp.ones((1, 1), f32))
        loss_ref[...] += jnp.broadcast_to(loss, loss_ref.shape)
        dyb = dy.astype(bf16)
        ds_ref[...] = _dot_nt(dyb, ws_ref[...])
        dg_ref[...] = _dot_nt(dyb, wg_ref[...])
        gx_ref[...] = dx
        dws_ref[...] += _dot_tn(sv, dyb)
        dwg_ref[...] += _dot_tn(gv, dyb)
        dgate_ref[...] += dgate
        dlng_ref[...] += dlng
        dlnb_ref[...] += dlnb

    half, full = _tok(T, D_S5), _tok(T, D_MODEL)
    row = _resident((1, D_MODEL))
    wsp = _resident((D_S5, D_MODEL))
    return pl.pallas_call(
        body, name="head_fwd_bwd", grid=(B, L // T),
        in_specs=[half, half, full, full, _per_batch(1, D_MODEL), row, row, wsp, wsp],
        out_specs=[_per_batch(8, LANES), half, half, full, wsp, wsp, _per_batch(1, D_MODEL), row, row],
        out_shape=[SDS((B, 8, LANES), f32), SDS((B, L, D_S5), f32), SDS((B, L, D_GDN), f32), SDS((B, L, D_MODEL), f32),
                   SDS((D_S5, D_MODEL), f32), SDS((D_GDN, D_MODEL), f32), SDS((B, 1, D_MODEL), f32),
                   SDS((1, D_MODEL), f32), SDS((1, D_MODEL), f32)],
        compiler_params=_cparams(2),
    )(s5o, gdo, x, tgt, gate, lng, lnb, ws, wg)


def _adamw_math(w, g, m, v):
    nm = ADAM_B1 * m + (1.0 - ADAM_B1) * g
    nv = ADAM_B2 * v + (1.0 - ADAM_B2) * jnp.square(g)
    m_hat = nm / (1.0 - ADAM_B1 ** ADAM_STEP)
    v_hat = nv / (1.0 - ADAM_B2 ** ADAM_STEP)
    return -ADAM_LR * (m_hat / (jnp.sqrt(v_hat) + ADAM_EPS) + ADAM_WD * w), nm, nv


def _row_tile(rows, cap=512):
    for t in range(min(cap, rows), 15, -1):
        if rows % t == 0 and t % 16 == 0:
            return t
    return rows


def adamw_2d(w, g, m, v, *, name):
    R, C = w.shape
    T = _row_tile(R)
    spec = pl.BlockSpec((T, C), lambda i: (i, 0))

    def body(w_ref, g_ref, m_ref, v_ref, d_ref, nm_ref, nv_ref):
        d_ref[...], nm_ref[...], nv_ref[...] = _adamw_math(w_ref[...], g_ref[...], m_ref[...], v_ref[...])

    return pl.pallas_call(body, name=name, grid=(R // T,), in_specs=[spec] * 4, out_specs=[spec] * 3,
                          out_shape=[SDS((R, C), f32)] * 3, compiler_params=_cparams(1))(w, g, m, v)


def adamw_small(ws, gs, ms, vs):
    n = len(ws)

    def body(*refs):
        outs = refs[4 * n:]
        for i in range(n):
            d, nm, nv = _adamw_math(refs[i][...], refs[n + i][...], refs[2 * n + i][...], refs[3 * n + i][...])
            outs[i][...], outs[n + i][...], outs[2 * n + i][...] = d, nm, nv

    res = pl.pallas_call(body, name="adamw_small", out_shape=[SDS(w.shape, f32) for w in ws] * 3,
                         compiler_params=pltpu.CompilerParams(vmem_limit_bytes=VMEM_LIMIT))(*ws, *gs, *ms, *vs)
    return res[:n], res[n:2 * n], res[2 * n:]


def sum_cores(own, got, *, name):
    A, H, C = own.shape
    T = _row_tile(H)
    spec = pl.BlockSpec((None, T, C), lambda a, i: (a, i, 0))

    def body(a_ref, b_ref, q32_ref, q16_ref):
        q = a_ref[...] + b_ref[...]
        q32_ref[...] = q
        q16_ref[...] = q.astype(bf16)

    return pl.pallas_call(body, name=name, grid=(A, H // T), in_specs=[spec, spec], out_specs=[spec, spec],
                          out_shape=[SDS((A, H, C), f32), SDS((A, H, C), bf16)], compiler_params=_cparams(2))(own, got)


def sum_chips(mine, rec, cpos, *, name):
    H, C = mine.shape
    T = _row_tile(H)

    def body(c_ref, m_ref, r_ref, f_ref):
        f_ref[...] = ((m_ref[...] + r_ref[0].astype(f32)) + r_ref[1].astype(f32)) + r_ref[2].astype(f32)

    grid_spec = pltpu.PrefetchScalarGridSpec(
        num_scalar_prefetch=1, grid=(H // T,),
        in_specs=[pl.BlockSpec((T, C), lambda i, c_ref: (i, 0)), pl.BlockSpec((3, T, C), lambda i, c_ref: (0, i, 0))],
        out_specs=pl.BlockSpec((None, T, C), lambda i, c_ref: (c_ref[0], i, 0)))
    return pl.pallas_call(body, name=name, grid_spec=grid_spec, out_shape=SDS((2, H, C), f32),
                          compiler_params=_cparams(1))(cpos.reshape(1).astype(jnp.int32), mine, rec)


CHIP_FLIPS = ((1, 0), (0, 1), (1, 1))


def _pos():
    return lax.axis_index("x"), lax.axis_index("y"), lax.axis_index("c")


def _comm_call(body, srcs, out_sds, n_remote, n_local, name):
    any_spec = pl.BlockSpec(memory_space=pl.ANY)
    return pl.pallas_call(
        body, name=name, in_specs=[any_spec] * len(srcs), out_specs=[any_spec] * len(out_sds), out_shape=out_sds,
        scratch_shapes=[pltpu.SemaphoreType.DMA((n_remote,)), pltpu.SemaphoreType.DMA((n_remote,)),
                        pltpu.SemaphoreType.DMA((max(n_local, 1),))],
        compiler_params=pltpu.CompilerParams(has_side_effects=True),
    )(*srcs)


def _remote(src, dst, send_sems, recv_sems, k, target):
    return pltpu.make_async_remote_copy(src, dst, send_sems.at[k], recv_sems.at[k], device_id=target,
                                        device_id_type=MESH)


def _half_rows(c, rows):
    half = rows // 2
    return pl.ds(pl.multiple_of(c * half, 8), half)


def gather_shards(shards):
    nt = len(shards)

    def body(*refs):
        srcs, outs = refs[:nt], refs[nt:2 * nt]
        send_sems, recv_sems, _ = refs[2 * nt:]
        x, y, c = _pos()
        j = 2 * x + y
        sib = (x, y, 1 - c)
        own = [_remote(srcs[t], outs[t].at[j], send_sems, recv_sems, 7 * t + 6, sib) for t in range(nt)]
        first, passed = [], []
        for k, (fx, fy) in enumerate(CHIP_FLIPS):
            tx, ty = x ^ fx, y ^ fy
            jk = 2 * tx + ty
            for t in range(nt):
                rows = _half_rows(c, srcs[t].shape[0])
                first.append(_remote(srcs[t].at[rows], outs[t].at[j, rows], send_sems, recv_sems, 7 * t + k, (tx, ty, c)))
                passed.append(_remote(outs[t].at[jk, rows], outs[t].at[jk, rows], send_sems, recv_sems, 7 * t + 3 + k, sib))
        for cp in first + own:
            cp.start()
        for a, b in zip(first, passed):
            a.wait_recv()
            b.start()
        for cp in passed + own:
            cp.wait_recv()
        for cp in first + passed + own:
            cp.wait_send()

    return _comm_call(body, shards, [SDS((4,) + s.shape, s.dtype) for s in shards], 7 * nt, 0, "gather_shards")


def swap_halves(ps):
    nt = len(ps)

    def body(*refs):
        srcs, outs = refs[:nt], refs[nt:2 * nt]
        send_sems, recv_sems, _ = refs[2 * nt:]
        x, y, c = _pos()
        cps = [_remote(srcs[t].at[a, _half_rows(1 - c, srcs[t].shape[1])], outs[t].at[a], send_sems, recv_sems, 4 * t + a,
                       (x, y, 1 - c)) for t in range(nt) for a in range(4)]
        for cp in cps:
            cp.start()
        for cp in cps:
            cp.wait()

    return _comm_call(body, ps, [SDS((4, p.shape[1] // 2, p.shape[2]), p.dtype) for p in ps], 4 * nt, 0, "swap_halves")


def scatter_to_chips(qs):
    nt = len(qs)

    def body(*refs):
        srcs, outs = refs[:nt], refs[nt:2 * nt]
        send_sems, recv_sems, _ = refs[2 * nt:]
        x, y, c = _pos()
        cps = []
        for k, (fx, fy) in enumerate(CHIP_FLIPS):
            tx, ty = x ^ fx, y ^ fy
            for t in range(nt):
                cps.append(_remote(srcs[t].at[2 * tx + ty], outs[t].at[k], send_sems, recv_sems, 3 * t + k, (tx, ty, c)))
        for cp in cps:
            cp.start()
        for cp in cps:
            cp.wait()

    return _comm_call(body, qs, [SDS((3,) + q.shape[1:], q.dtype) for q in qs], 3 * nt, 0, "scatter_to_chips")


def join_halves(fs):
    nt = len(fs)

    def body(*refs):
        outs = refs[nt:2 * nt]
        send_sems, recv_sems, _ = refs[2 * nt:]
        x, y, c = _pos()
        cps = [_remote(outs[t].at[c], outs[t].at[c], send_sems, recv_sems, t, (x, y, 1 - c)) for t in range(nt)]
        for cp in cps:
            cp.start()
        for cp in cps:
            cp.wait()

    any_spec = pl.BlockSpec(memory_space=pl.ANY)
    return pl.pallas_call(
        body, name="join_halves", in_specs=[any_spec] * nt, out_specs=[any_spec] * nt,
        out_shape=[SDS(f.shape, f.dtype) for f in fs], input_output_aliases={t: t for t in range(nt)},
        scratch_shapes=[pltpu.SemaphoreType.DMA((nt,)), pltpu.SemaphoreType.DMA((nt,)), pltpu.SemaphoreType.DMA((1,))],
        compiler_params=pltpu.CompilerParams(has_side_effects=True),
    )(*fs)


def gather_small(s):
    def body(src, out, send_sems, recv_sems, _):
        x, y, c = _pos()
        j = 2 * x + y
        cps = [_remote(src, out.at[j], send_sems, recv_sems, k, (x ^ fx, y ^ fy, c)) for k, (fx, fy) in enumerate(CHIP_FLIPS)]
        cps.append(_remote(src, out.at[j], send_sems, recv_sems, 3, (x, y, 1 - c)))
        for cp in cps:
            cp.start()
        for cp in cps:
            cp.wait()

    return _comm_call(body, [s], [SDS((4,) + s.shape, s.dtype)], 4, 0, "gather_small")[0]


SMALL_SHAPES = ((D_MODEL,), (1, 3 * D_MODEL), (1, 2, 32, 64), (1, 2, 32, 64), (1, 2, 32), (1, 2, 32, 64, 16),
                (1, 2, 32, 64, 16), (1, 2, 32, 16, 64), (1, 2, 32, 16, 64), (1, D_S5), (1, D_S5), (1, 2, 4), (1, 2, 4),
                (1, GDN_HEAD), (1, D_MODEL), (1, D_MODEL))


def _size(shape):
    return functools.reduce(lambda p, q: p * q, shape)


SMALL_ROWS = tuple(-(-_size(s) // LANES) for s in SMALL_SHAPES)
SMALL_TOTAL = 2176
SMALL_QUARTER = SMALL_TOTAL // 4


def _rows(a):
    flat = a.reshape(-1)
    pad = (-flat.shape[0]) % LANES
    if pad:
        flat = jnp.concatenate([flat, jnp.zeros((pad,), flat.dtype)])
    return flat.reshape(-1, LANES)


def _pack_small(parts):
    rows = [_rows(p) for p in parts]
    rows.append(jnp.zeros((SMALL_TOTAL - sum(SMALL_ROWS), LANES), f32))
    return jnp.concatenate(rows, axis=0)


def _unpack_small(buf):
    out, r = [], 0
    for s, n in zip(SMALL_SHAPES, SMALL_ROWS):
        out.append(buf[r:r + n].reshape(-1)[:_size(s)].reshape(s))
        r += n
    return out


def _as_2d(a):
    return a.reshape(1, -1) if a.ndim == 1 else a.reshape(-1, a.shape[-1])


S5_BG = S5_GROUPS // S5_BLOCKS


def _block_diag_in(bb):
    eye = jnp.eye(S5_BG, dtype=bb.dtype)
    b4 = bb.reshape(S5_BLOCKS, S5_BG, S5_STATE, S5_GROUP)
    return jnp.einsum('jgpc,gh->jgchp', b4, eye).reshape(S5_BLOCKS, S5_BC, S5_BS)


def _block_diag_in_t(d):
    d6 = d.reshape(S5_BLOCKS, S5_BG, S5_GROUP, S5_BG, S5_STATE)
    return jnp.einsum('jgcgp->jgpc', d6).reshape(S5_GROUPS, S5_STATE * S5_GROUP)


def _block_diag_out(cm):
    eye = jnp.eye(S5_BG, dtype=cm.dtype)
    c4 = cm.reshape(S5_BLOCKS, S5_BG, S5_GROUP, S5_STATE)
    return jnp.einsum('jgcp,gh->jhpgc', c4, eye).reshape(S5_BLOCKS, S5_BS, S5_BC)


def _block_diag_out_t(d):
    d6 = d.reshape(S5_BLOCKS, S5_BG, S5_STATE, S5_BG, S5_GROUP)
    return jnp.einsum('jgpgc->jgcp', d6).reshape(S5_GROUPS, S5_GROUP, S5_STATE)


def _to_chunk_rows(a):
    B, L, W = a.shape
    return a.reshape(B, L // CHUNK, CHUNK, W).transpose(0, 1, 3, 2)


def _from_chunk_rows(a):
    B, nc, W, _ = a.shape
    return a.transpose(0, 1, 3, 2).reshape(B, nc * CHUNK, W)


def local_step(x, c, ctx, c_ctx, tgt, w_ada, b_ada, w_in, lam_re, lam_im, log_dt, b_re, b_im, c_re, c_im, s5_d,
               w_glu, b_glu, conv16, a_log, dt_bias, norm_w, w_out, ln_g, ln_b):
    B, L, _ = x.shape
    zeros_state = jnp.zeros((B, GDN_HEADS, GDN_HEAD, GDN_HEAD), f32)

    cc = jnp.concatenate([c, c_ctx[None, :], jnp.zeros((8 - B - 1, D_MODEL), f32)], axis=0)
    m = ada_fwd(cc, w_ada, b_ada)
    shift, scale, gate = m[:B, :D_MODEL], m[:B, D_MODEL:2 * D_MODEL], m[:B, 2 * D_MODEL:]
    mod = jnp.stack([scale, shift], axis=1)
    mod_c = jnp.broadcast_to(jnp.stack([m[B, D_MODEL:2 * D_MODEL], m[B, :D_MODEL]], axis=0)[None], (B, 2, D_MODEL))

    u, z_s5, qkv, z_gdn, ba = in_proj_fwd(x, mod, w_in, name="in_proj_fwd")
    uc, _, qkvc, _, bac = in_proj_fwd(ctx, mod_c, w_in, name="in_proj_fwd_ctx")

    ng = N_DIR * S5_GROUPS
    zoh_in = (lam_re.reshape(ng, S5_STATE), lam_im.reshape(ng, S5_STATE), log_dt.reshape(ng, 1),
              b_re.reshape(ng, S5_STATE * S5_GROUP), b_im.reshape(ng, S5_STATE * S5_GROUP))
    expand = (jnp.arange(S5_STATE * S5_GROUP)[None, :] // S5_GROUP == jnp.arange(S5_STATE)[:, None]).astype(f32)
    ar, ai, bbr, bbi = s5_zoh_fwd(*zoh_in, expand)
    bbr16, bbi16 = bbr.astype(bf16), bbi.astype(bf16)
    c_re16 = c_re.reshape(N_DIR, S5_GROUPS, S5_GROUP, S5_STATE).astype(bf16)
    c_im16 = (-c_im).reshape(N_DIR, S5_GROUPS, S5_GROUP, S5_STATE).astype(bf16)
    s5w, ys, hins, hins_c = [], [], [], []
    for d in range(N_DIR):
        g = slice(d * S5_GROUPS, (d + 1) * S5_GROUPS)
        wd = (_block_diag_in(bbr16[g]), _block_diag_in(bbi16[g]), _block_diag_out(c_re16[d]), _block_diag_out(c_im16[d]),
              jnp.stack([ar[g].reshape(-1), ai[g].reshape(-1)], axis=0))
        s5w.append(wd)
        hin_c, hend_c = s5_scan_fwd(uc, *wd, jnp.zeros((B, 2, S5_HALF), f32), d=d, need_y=False, name=f"s5_fwd_ctx{d}")
        y_d, hin, _ = s5_scan_fwd(u, *wd, hend_c, d=d, need_y=True, name=f"s5_fwd{d}")
        ys.append(y_d)
        hins.append(hin)
        hins_c.append(hin_c)
    glu_w = (s5_d.reshape(1, D_S5), w_glu, b_glu.reshape(1, D_S5))
    s5o = s5_glu_fwd(u, ys[0], ys[1], z_s5, *glu_w)

    act = conv_fwd(qkv, conv16, is_ctx=False, name="conv_fwd")
    act_c = conv_fwd(qkvc, conv16, is_ctx=True, name="conv_fwd_ctx")
    pad8 = jnp.zeros((1, 8), f32)
    alog16 = jnp.concatenate([pad8, a_log.reshape(1, 8)], axis=1)
    dtb16 = jnp.concatenate([pad8, dt_bias.reshape(1, 8)], axis=1)
    bg = gates_fwd(ba, alog16, dtb16, name="gates_fwd")
    bg_c = gates_fwd(bac, alog16, dtb16, name="gates_fwd_ctx")
    bgr, bgr_c = _to_chunk_rows(bg), _to_chunk_rows(bg_c)
    os_, cks, cks_c = [], [], []
    for d in range(N_DIR):
        ck_c, s_c = gdn_fwd(act_c, bg_c, bgr_c, zeros_state, d=d, need_o=False, name=f"gdn_fwd_ctx{d}")
        o_d, ck, _ = gdn_fwd(act, bg, bgr, s_c, d=d, need_o=True, name=f"gdn_fwd{d}")
        os_.append(o_d)
        cks.append(ck)
        cks_c.append(ck_c)
    nw = norm_w.reshape(1, GDN_HEAD)
    gdo = gnorm_fwd(os_[0], os_[1], z_gdn, nw)

    loss8, ds5o, dgdo, gx_res, dws, dwg, dgate, dlng, dlnb = head_fwd_bwd(
        s5o, gdo, x, tgt, gate[:, None, :], ln_g.reshape(1, D_MODEL), ln_b.reshape(1, D_MODEL), w_out[:D_S5], w_out[D_S5:])
    loss = jnp.sum(loss8[:, 0, 0])
    d_w_out = jnp.concatenate([dws, dwg], axis=0)

    do, dz_gdn, d_norm_w = gnorm_bwd(os_[0], os_[1], z_gdn, nw, dgdo)
    dacts, dacts_c = [], []
    dbg = jnp.zeros_like(bg)
    dbg_c = jnp.zeros_like(bg_c)
    for d in range(N_DIR):
        dact, dbg_d, dbgr_d, ds0 = gdn_bwd(act, bg, bgr, cks[d], do, zeros_state, d=d, name=f"gdn_bwd{d}")
        dact_c, dbgc_d, dbgrc_d, _ = gdn_bwd(act_c, bg_c, bgr_c, cks_c[d], None, ds0, d=d, name=f"gdn_bwd_ctx{d}")
        dacts.append(dact)
        dacts_c.append(dact_c)
        dbg = dbg + dbg_d + _from_chunk_rows(dbgr_d)
        dbg_c = dbg_c + dbgc_d + _from_chunk_rows(dbgrc_d)
    dba, dal, ddt = gates_bwd(ba, alog16, dtb16, dbg, name="gates_bwd")
    dbac, dal_c, ddt_c = gates_bwd(bac, alog16, dtb16, dbg_c, name="gates_bwd_ctx")
    d_a_log = (dal + dal_c)[:, 8:].reshape(1, N_DIR, GDN_HEADS)
    d_dt_bias = (ddt + ddt_c)[:, 8:].reshape(1, N_DIR, GDN_HEADS)
    dqkv, dcw = conv_bwd(qkv, conv16, dacts[0], dacts[1], is_ctx=False, name="conv_bwd")
    dqkvc, dcw_c = conv_bwd(qkvc, conv16, dacts_c[0], dacts_c[1], is_ctx=True, name="conv_bwd_ctx")
    d_conv16 = jnp.sum(dcw, axis=0) + jnp.sum(dcw_c, axis=0)

    du_skip, dy, dz_s5, d_s5_d, d_w_glu, d_b_glu = s5_glu_bwd(u, ys[0], ys[1], z_s5, *glu_w, ds5o)
    du, duc = du_skip, jnp.zeros_like(uc)
    dar, dai, dbbr, dbbi, dcre, dcim = [], [], [], [], [], []
    for d in range(N_DIR):
        du_d, dbre1, dbim1, dct1, dcb1, da1, dh0 = s5_scan_bwd(u, dy, *s5w[d], hins[d],
                                                                jnp.zeros((B, 2, S5_HALF), f32), d=d, name=f"s5_bwd{d}")
        duc_d, dbre2, dbim2, _, _, da2, _ = s5_scan_bwd(uc, None, *s5w[d], hins_c[d], dh0, d=d, name=f"s5_bwd_ctx{d}")
        du, duc = du + du_d, duc + duc_d
        da = da1 + da2
        dar.append(da[0].reshape(S5_GROUPS, S5_STATE))
        dai.append(da[1].reshape(S5_GROUPS, S5_STATE))
        dbbr.append(_block_diag_in_t(dbre1 + dbre2))
        dbbi.append(_block_diag_in_t(dbim1 + dbim2))
        dcre.append(_block_diag_out_t(dct1))
        dcim.append(-_block_diag_out_t(dcb1))
    dlr, dli, dldt, dbre, dbim = s5_zoh_bwd(*zoh_in, expand, jnp.concatenate(dar, 0), jnp.concatenate(dai, 0),
                                            jnp.concatenate(dbbr, 0), jnp.concatenate(dbbi, 0))
    d_s5 = (dlr, dli, dldt, dbre, dbim, jnp.stack(dcre, 0), jnp.stack(dcim, 0))

    padg = lambda a: jnp.concatenate([a, jnp.zeros(a.shape[:2] + (LANES - N_GATE,), f32)], axis=2)
    dw_l, dmod, grad_x = in_proj_bwd(x, mod, (du, dz_s5, dqkv, dz_gdn, padg(dba)), w_in, gx_res, name="in_proj_bwd")
    zc = jnp.zeros_like(uc)
    dw_c, dmod_c = in_proj_bwd(ctx, mod_c, (duc, zc, dqkvc, zc, padg(dbac)), w_in, None, name="in_proj_bwd_ctx")
    d_w_in = dw_l + dw_c
    dmod_c = jnp.sum(dmod_c, axis=0)

    dm_rows = jnp.concatenate([dmod[:, 1], dmod[:, 0], dgate[:, 0]], axis=1)
    dm_ctx = jnp.concatenate([dmod_c[1], dmod_c[0], jnp.zeros((D_MODEL,), f32)])[None]
    dm = jnp.concatenate([dm_rows, dm_ctx, jnp.zeros((8 - B - 1, 3 * D_MODEL), f32)], axis=0)
    dcc, d_w_ada, d_b_ada = ada_bwd(cc, w_ada, dm)
    small = (dcc[B], d_b_ada, *d_s5, d_s5_d, d_b_glu, d_a_log, d_dt_bias, d_norm_w, dlng, dlnb)
    small = tuple(g.reshape(s) for g, s in zip(small, SMALL_SHAPES))
    return loss, grad_x, (d_w_ada, d_w_in, d_w_out, d_w_glu, d_conv16), small


SHARDED = (1, 3, 18, 12, 14)
SMALL = tuple(i for i in range(21) if i not in SHARDED)
W_IN_SHARD = 772


def _conv_rows(w):
    return jnp.concatenate([w.reshape(9, w.shape[-1]), jnp.zeros((CONV_ROWS - 9, w.shape[-1]), f32)], axis=0)


def kernel(x, c, ctx, c_ctx, w_ada, b_ada, w_in, s5_lambda_re, s5_lambda_im, s5_log_dt, s5_b_re, s5_b_im, s5_c_re, s5_c_im, s5_d, w_glu, b_glu, conv_w, gdn_a_log, gdn_dt_bias, gdn_norm_w, w_out, ln_g, ln_b, loss_target, m_c_ctx, m_w_ada, m_b_ada, m_w_in, m_s5_lambda_re, m_s5_lambda_im, m_s5_log_dt, m_s5_b_re, m_s5_b_im, m_s5_c_re, m_s5_c_im, m_s5_d, m_w_glu, m_b_glu, m_conv_w, m_gdn_a_log, m_gdn_dt_bias, m_gdn_norm_w, m_w_out, m_ln_g, m_ln_b, v_c_ctx, v_w_ada, v_b_ada, v_w_in, v_s5_lambda_re, v_s5_lambda_im, v_s5_log_dt, v_s5_b_re, v_s5_b_im, v_s5_c_re, v_s5_c_im, v_s5_d, v_w_glu, v_b_glu, v_conv_w, v_gdn_a_log, v_gdn_dt_bias, v_gdn_norm_w, v_w_out, v_ln_g, v_ln_b):
    weights = [c_ctx, w_ada, b_ada, w_in, s5_lambda_re, s5_lambda_im, s5_log_dt, s5_b_re, s5_b_im, s5_c_re, s5_c_im,
               s5_d, w_glu, b_glu, conv_w, gdn_a_log, gdn_dt_bias, gdn_norm_w, w_out, ln_g, ln_b]
    ms = [m_c_ctx, m_w_ada, m_b_ada, m_w_in, m_s5_lambda_re, m_s5_lambda_im, m_s5_log_dt, m_s5_b_re, m_s5_b_im,
          m_s5_c_re, m_s5_c_im, m_s5_d, m_w_glu, m_b_glu, m_conv_w, m_gdn_a_log, m_gdn_dt_bias, m_gdn_norm_w, m_w_out,
          m_ln_g, m_ln_b]
    vs = [v_c_ctx, v_w_ada, v_b_ada, v_w_in, v_s5_lambda_re, v_s5_lambda_im, v_s5_log_dt, v_s5_b_re, v_s5_b_im,
          v_s5_c_re, v_s5_c_im, v_s5_d, v_w_glu, v_b_glu, v_conv_w, v_gdn_a_log, v_gdn_dt_bias, v_gdn_norm_w, v_w_out,
          v_ln_g, v_ln_b]
    cpos = lax.axis_index("c")
    jchip = 2 * lax.axis_index("x") + lax.axis_index("y")

    conv_shard = _conv_rows(conv_w)
    g_ada, g_in, g_out, g_glu, g_conv = gather_shards(
        [w_ada[0].astype(bf16), w_in[0].astype(bf16), w_out[0].astype(bf16), w_glu[0].astype(bf16), conv_shard])
    w_in_pad = jnp.concatenate([g_in[0], g_in[1], g_in[2], g_in[3], jnp.zeros((D_MODEL, IN_PAD - P_IN), bf16)], axis=1)
    conv16 = g_conv.transpose(1, 0, 2).reshape(CONV_ROWS, 3 * D_GDN)

    loss, grad_x, big, small = local_step(
        x, c, ctx, c_ctx, loss_target, g_ada, b_ada, w_in_pad, s5_lambda_re, s5_lambda_im, s5_log_dt, s5_b_re, s5_b_im,
        s5_c_re, s5_c_im, s5_d, g_glu.reshape(D_S5, D_S5), b_glu, conv16, gdn_a_log, gdn_dt_bias, gdn_norm_w,
        g_out.reshape(D_MODEL, D_MODEL), ln_g, ln_b)
    loss = lax.psum(loss, ("x", "y", "c"))

    d_w_ada, d_w_in, d_w_out, d_w_glu, d_conv16 = big
    slabs = [d_w_ada,
             d_w_in[:, :P_IN].reshape(D_MODEL, 4, W_IN_SHARD).transpose(1, 0, 2),
             d_w_out.reshape(4, D_MODEL // 4, D_MODEL),
             d_w_glu.reshape(4, D_S5 // 4, D_S5),
             d_conv16.reshape(CONV_ROWS, 4, 3 * D_GDN // 4).transpose(1, 0, 2),
             _pack_small(small).reshape(4, SMALL_QUARTER, LANES)]
    got = swap_halves(slabs)
    q32, q16 = [], []
    for t, (s, g) in enumerate(zip(slabs, got)):
        own = lax.dynamic_index_in_dim(s.reshape(4, 2, s.shape[1] // 2, s.shape[2]), cpos, axis=1, keepdims=False)
        a, b = sum_cores(own, g, name=f"sum_cores{t}")
        q32.append(a)
        q16.append(b)
    rec = scatter_to_chips(q16)
    fs = [sum_chips(lax.dynamic_index_in_dim(q, jchip, axis=0, keepdims=False), r, cpos, name=f"sum_chips{t}")
          for t, (q, r) in enumerate(zip(q32, rec))]
    red = [r.reshape(2 * r.shape[1], r.shape[2]) for r in join_halves(fs)]
    g_small = _unpack_small(gather_small(red[5]).reshape(SMALL_TOTAL, LANES))

    grads, deltas, new_m, new_v = [None] * 21, [None] * 21, [None] * 21, [None] * 21
    for t, i in enumerate(SHARDED):
        conv = i == 14
        prep = _conv_rows if conv else (lambda a: a[0])
        d, nm, nv = adamw_2d(prep(weights[i]), red[t], prep(ms[i]), prep(vs[i]), name=f"adamw{t}")
        for lst, val in ((grads, red[t]), (deltas, d), (new_m, nm), (new_v, nv)):
            lst[i] = (val[:9] if conv else val).reshape(weights[i].shape)
    sm = adamw_small([_as_2d(weights[i]) for i in SMALL], [_as_2d(g) for g in g_small], [_as_2d(ms[i]) for i in SMALL],
                     [_as_2d(vs[i]) for i in SMALL])
    for n, i in enumerate(SMALL):
        grads[i] = g_small[n]
        for lst, res in ((deltas, sm[0]), (new_m, sm[1]), (new_v, sm[2])):
            lst[i] = res[n].reshape(weights[i].shape)
    return (loss, grad_x, *grads, *deltas, *new_m, *new_v)
```

```python
import functools

import jax
import jax.numpy as jnp
from jax import lax
from jax.experimental import pallas as pl
from jax.experimental.pallas import tpu as pltpu

f32 = jnp.float32
bf16 = jnp.bfloat16
SDS = jax.ShapeDtypeStruct

D_MODEL = 1024
D_S5 = 512
S5_GROUP = 16
S5_GROUPS = 32
S5_STATE = 64
S5_HALF = S5_GROUPS * S5_STATE
D_GDN = 512
GDN_HEAD = 128
GDN_HEADS = 4
CHUNK = 64
GRID_W = 64
N_DIR = 2
P_IN = 3088
DEEPNORM_ALPHA = 2.0 ** 0.25
LN_EPS = 1e-5
NORM_EPS = 1e-6
ADAM_LR, ADAM_B1, ADAM_B2, ADAM_EPS, ADAM_WD, ADAM_STEP = 0.001, 0.9, 0.999, 1e-08, 0.01, 10

LANES = 128
VMEM_LIMIT = 56 * 1024 * 1024
TOK_TILE = 256
S5_TILE = 256
MESH = pl.DeviceIdType.MESH


def _cparams(n_grid):
    return pltpu.CompilerParams(dimension_semantics=("arbitrary",) * n_grid, vmem_limit_bytes=VMEM_LIMIT)


def _dot(a, b):
    return jnp.dot(a.astype(bf16), b.astype(bf16), preferred_element_type=f32)


def _dot_nt(a, b):
    return lax.dot_general(a.astype(bf16), b.astype(bf16), (((1,), (1,)), ((), ())), preferred_element_type=f32)


def _dot_tn(a, b):
    return lax.dot_general(a.astype(bf16), b.astype(bf16), (((0,), (0,)), ((), ())), preferred_element_type=f32)


def _dot_hi(a, b):
    return jnp.dot(a, b, precision=lax.Precision.HIGHEST, preferred_element_type=f32)


def _dot_h3(a, b):
    return jnp.dot(a, b, precision=lax.Precision.HIGH, preferred_element_type=f32)


@jax.custom_vjp
def _mm(a, b):
    return _dot(a, b)


@jax.custom_vjp
def _mm_nt(a, b):
    return _dot_nt(a, b)


@jax.custom_vjp
def _mm_tn(a, b):
    return _dot_tn(a, b)


_mm.defvjp(lambda a, b: (_dot(a, b), (a, b)), lambda r, g: (_mm_nt(g, r[1]), _mm_tn(r[0], g)))
_mm_nt.defvjp(lambda a, b: (_dot_nt(a, b), (a, b)), lambda r, g: (_mm(g, r[1]), _mm_tn(g, r[0])))
_mm_tn.defvjp(lambda a, b: (_dot_tn(a, b), (a, b)), lambda r, g: (_mm_nt(r[1], g), _mm(r[0], g)))


def _silu(x):
    return x * jax.nn.sigmoid(x)


def _gelu(x):
    return 0.5 * x * (1.0 + lax.erf(x * (2.0 ** -0.5)))


def _resident(shape):
    nd = len(shape)
    return pl.BlockSpec(shape, lambda *_: (0,) * nd, pipeline_mode=pl.Buffered(1))


def _tok(tile, width, nt=None, rev=False):
    if rev:
        return pl.BlockSpec((None, tile, width), lambda b, n: (b, nt - 1 - n, 0))
    return pl.BlockSpec((None, tile, width), lambda b, n: (b, n, 0))


def _per_batch(rows, width):
    return pl.BlockSpec((None, rows, width), lambda b, n: (b, 0, 0))


def _first_step():
    return jnp.logical_and(pl.program_id(0) == 0, pl.program_id(1) == 0)


ADA_SHARD = 3 * D_MODEL // 4


def ada_fwd(cc, w, b):
    def body(cc_ref, w_ref, b_ref, m_ref):
        s = _silu(cc_ref[...]).astype(bf16)
        for j in range(4):
            sl = slice(j * ADA_SHARD, (j + 1) * ADA_SHARD)
            m_ref[:, sl] = _dot(s, w_ref[j]) + b_ref[:, sl]

    return pl.pallas_call(body, name="ada_fwd", out_shape=SDS((8, 3 * D_MODEL), f32),
                          compiler_params=pltpu.CompilerParams(vmem_limit_bytes=VMEM_LIMIT))(cc, w, b)


def ada_bwd(cc, w, dm):
    def body(cc_ref, w_ref, dm_ref, dcc_ref, dw_ref, db_ref):
        s, vjp = jax.vjp(_silu, cc_ref[...])
        ds = jnp.zeros((8, D_MODEL), f32)
        for j in range(4):
            dmj = dm_ref[:, j * ADA_SHARD:(j + 1) * ADA_SHARD]
            ds = ds + _dot_nt(dmj, w_ref[j])
            dw_ref[j] = _dot_tn(s, dmj)
        dcc_ref[...] = vjp(ds)[0]
        db_ref[...] = jnp.sum(dm_ref[...], axis=0, keepdims=True)

    return pl.pallas_call(
        body, name="ada_bwd",
        out_shape=[SDS((8, D_MODEL), f32), SDS((4, D_MODEL, ADA_SHARD), f32), SDS((1, 3 * D_MODEL), f32)],
        compiler_params=pltpu.CompilerParams(vmem_limit_bytes=VMEM_LIMIT))(cc, w, dm)


N_GATE = 2 * N_DIR * GDN_HEADS
IN_WIDTHS = (D_S5, D_S5, 3 * D_GDN, D_GDN, LANES)
IN_OFFS = (0, 512, 1024, 2560, 3072)
IN_PAD = 3200


def in_proj_fwd(x, mod, w, *, name):
    B, L, _ = x.shape
    T = min(TOK_TILE, L)

    def body(x_ref, mod_ref, w_ref, *o_refs):
        h = (x_ref[...] * (1.0 + mod_ref[0:1, :]) + mod_ref[1:2, :]).astype(bf16)
        for o_ref, off, wd in zip(o_refs, IN_OFFS, IN_WIDTHS):
            r = _dot(h, w_ref[:, off:off + wd])
            o_ref[...] = r[:, :o_ref.shape[-1]]

    outw = (D_S5, D_S5, 3 * D_GDN, D_GDN, N_GATE)
    return pl.pallas_call(
        body, name=name, grid=(B, L // T),
        in_specs=[_tok(T, D_MODEL), _per_batch(2, D_MODEL), _resident((D_MODEL, IN_PAD))],
        out_specs=[_tok(T, wd) for wd in outw],
        out_shape=[SDS((B, L, wd), f32) for wd in outw],
        compiler_params=_cparams(2),
    )(x, mod, w)


def in_proj_bwd(x, mod, ds, w, gx_res, *, name):
    B, L, _ = x.shape
    T = min(TOK_TILE, L)
    with_dx = gx_res is not None

    def body(*refs):
        x_ref, mod_ref = refs[0], refs[1]
        d_refs = refs[2:7]
        w_ref = refs[7]
        k = 8
        if with_dx:
            gx_ref = refs[k]
            k += 1
        dw_ref, dmod_ref = refs[k], refs[k + 1]
        if with_dx:
            dx_ref = refs[k + 2]
        n = pl.program_id(1)

        @pl.when(_first_step())
        def _():
            dw_ref[...] = jnp.zeros_like(dw_ref)

        @pl.when(n == 0)
        def _():
            dmod_ref[...] = jnp.zeros_like(dmod_ref)

        xv = x_ref[...]
        scale1 = 1.0 + mod_ref[0:1, :]
        h = (xv * scale1 + mod_ref[1:2, :]).astype(bf16)
        dh = jnp.zeros((T, D_MODEL), f32)
        for d_ref, off, wd in zip(d_refs, IN_OFFS, IN_WIDTHS):
            dv = d_ref[...].astype(bf16)
            dh = dh + _dot_nt(dv, w_ref[:, off:off + wd])
            dw_ref[:, off:off + wd] += _dot_tn(h, dv)
        dmod_ref[0:1, :] += jnp.sum(dh * xv, axis=0, keepdims=True)
        dmod_ref[1:2, :] += jnp.sum(dh, axis=0, keepdims=True)
        if with_dx:
            dx_ref[...] = gx_ref[...] + dh * scale1

    in_specs = ([_tok(T, D_MODEL), _per_batch(2, D_MODEL)] + [_tok(T, wd) for wd in IN_WIDTHS]
                + [_resident((D_MODEL, IN_PAD))])
    args = [x, mod, *ds, w]
    out_specs = [_resident((D_MODEL, IN_PAD)), _per_batch(2, D_MODEL)]
    out_shape = [SDS((D_MODEL, IN_PAD), f32), SDS((B, 2, D_MODEL), f32)]
    if with_dx:
        in_specs.append(_tok(T, D_MODEL))
        args.append(gx_res)
        out_specs.append(_tok(T, D_MODEL))
        out_shape.append(SDS((B, L, D_MODEL), f32))
    return pl.pallas_call(body, name=name, grid=(B, L // T), in_specs=in_specs, out_specs=out_specs,
                          out_shape=out_shape, compiler_params=_cparams(2))(*args)


def _s5_zoh(lr, li, ldt, bre, bim, expand):
    dt = jnp.exp(ldt)
    zr, zi = lr * dt, li * dt
    e = jnp.exp(zr)
    ar, ai = e * jnp.cos(zi), e * jnp.sin(zi)
    den = lr * lr + li * li
    czr = ((ar - 1.0) * lr + ai * li) / den
    czi = (ai * lr - (ar - 1.0) * li) / den
    czr_e, czi_e = _dot_hi(czr, expand), _dot_hi(czi, expand)
    return ar, ai, czr_e * bre - czi_e * bim, czr_e * bim + czi_e * bre


_ZOH_OUT = [(N_DIR * S5_GROUPS, S5_STATE)] * 2 + [(N_DIR * S5_GROUPS, S5_STATE * S5_GROUP)] * 2


def s5_zoh_fwd(lr, li, ldt, bre, bim, expand):
    def body(lr_ref, li_ref, ldt_ref, bre_ref, bim_ref, e_ref, ar_ref, ai_ref, bbr_ref, bbi_ref):
        ar, ai, bbr, bbi = _s5_zoh(lr_ref[...], li_ref[...], ldt_ref[...], bre_ref[...], bim_ref[...], e_ref[...])
        ar_ref[...], ai_ref[...], bbr_ref[...], bbi_ref[...] = ar, ai, bbr, bbi

    return pl.pallas_call(body, name="s5_zoh_fwd", out_shape=[SDS(s, f32) for s in _ZOH_OUT])(
        lr, li, ldt, bre, bim, expand)


def s5_zoh_bwd(lr, li, ldt, bre, bim, expand, dar, dai, dbbr, dbbi):
    def body(lr_ref, li_ref, ldt_ref, bre_ref, bim_ref, e_ref, dar_ref, dai_ref, dbbr_ref, dbbi_ref,
             dlr_ref, dli_ref, dldt_ref, dbre_ref, dbim_ref):
        ev = e_ref[...]
        _, vjp = jax.vjp(lambda a, b, c, d, e: _s5_zoh(a, b, c, d, e, ev),
                         lr_ref[...], li_ref[...], ldt_ref[...], bre_ref[...], bim_ref[...])
        outs = vjp((dar_ref[...], dai_ref[...], dbbr_ref[...], dbbi_ref[...]))
        dlr_ref[...], dli_ref[...], dldt_ref[...], dbre_ref[...], dbim_ref[...] = outs

    shapes = [lr.shape, li.shape, ldt.shape, bre.shape, bim.shape]
    return pl.pallas_call(body, name="s5_zoh_bwd", out_shape=[SDS(s, f32) for s in shapes])(
        lr, li, ldt, bre, bim, expand, dar, dai, dbbr, dbbi)


def _scan_rows(T, rev, ar, ai, hr0, hi0, r_ref, i_ref, off):
    def step(i, carry):
        hr, hi = carry
        t = off + ((T - 1 - i) if rev else i)
        nr = ar * hr - ai * hi + r_ref[pl.ds(t, 1), :]
        ni = ar * hi + ai * hr + i_ref[pl.ds(t, 1), :]
        r_ref[pl.ds(t, 1), :] = nr
        i_ref[pl.ds(t, 1), :] = ni
        return nr, ni

    return lax.fori_loop(0, T, step, (hr0, hi0), unroll=4)


S5_BLOCKS = 4
S5_BC = D_S5 // S5_BLOCKS
S5_BS = S5_HALF // S5_BLOCKS


def _s5_in(uv, bre_ref, bim_ref, hr_ref, hi_ref, off, T):
    for jb in range(S5_BLOCKS):
        uj = uv[:, jb * S5_BC:(jb + 1) * S5_BC]
        hr_ref[off:off + T, jb * S5_BS:(jb + 1) * S5_BS] = _dot(uj, bre_ref[jb])
        hi_ref[off:off + T, jb * S5_BS:(jb + 1) * S5_BS] = _dot(uj, bim_ref[jb])


def s5_scan_fwd(u, bre, bim, ctop, cbot, arow, h0, *, d, need_y, name):
    B, L, _ = u.shape
    T = min(S5_TILE, L)
    nt = L // T
    rev = d == 1

    def body(u_ref, bre_ref, bim_ref, ct_ref, cb_ref, a_ref, h0_ref, *rest):
        if need_y:
            y_ref, hin_ref, hend_ref, hr_scr, hi_scr, h_scr = rest
        else:
            hin_ref, hend_ref, hr_scr, hi_scr, h_scr = rest
        n = pl.program_id(1)

        @pl.when(n == 0)
        def _():
            h_scr[...] = h0_ref[...]

        hin_ref[...] = h_scr[...]
        _s5_in(u_ref[...].astype(bf16), bre_ref, bim_ref, hr_scr, hi_scr, 0, T)
        hr, hi = _scan_rows(T, rev, a_ref[0:1, :], a_ref[1:2, :], h_scr[0:1, :], h_scr[1:2, :], hr_scr, hi_scr, 0)
        h_scr[0:1, :] = hr
        h_scr[1:2, :] = hi
        if need_y:
            for jb in range(S5_BLOCKS):
                st = slice(jb * S5_BS, (jb + 1) * S5_BS)
                y_ref[:, jb * S5_BC:(jb + 1) * S5_BC] = _dot(hr_scr[:, st], ct_ref[jb]) + _dot(hi_scr[:, st], cb_ref[jb])

        @pl.when(n == nt - 1)
        def _():
            hend_ref[...] = h_scr[...]

    state = _per_batch(2, S5_HALF)
    hin_spec = pl.BlockSpec((None, None, 2, S5_HALF), (lambda b, n: (b, nt - 1 - n, 0, 0)) if rev else (lambda b, n: (b, n, 0, 0)))
    out_specs = [hin_spec, state]
    out_shape = [SDS((B, nt, 2, S5_HALF), f32), SDS((B, 2, S5_HALF), f32)]
    if need_y:
        out_specs.insert(0, _tok(T, D_S5, nt, rev))
        out_shape.insert(0, SDS((B, L, D_S5), f32))
    w_in, w_out = _resident((S5_BLOCKS, S5_BC, S5_BS)), _resident((S5_BLOCKS, S5_BS, S5_BC))
    return pl.pallas_call(
        body, name=name, grid=(B, nt),
        in_specs=[_tok(T, D_S5, nt, rev), w_in, w_in, w_out, w_out, _resident((2, S5_HALF)), state],
        out_specs=out_specs, out_shape=out_shape,
        scratch_shapes=[pltpu.VMEM((T, S5_HALF), f32), pltpu.VMEM((T, S5_HALF), f32), pltpu.VMEM((2, S5_HALF), f32)],
        compiler_params=_cparams(2),
    )(u, bre, bim, ctop, cbot, arow, h0)


def s5_scan_bwd(u, dy, bre, bim, ctop, cbot, arow, hin, dhend, *, d, name):
    B, L, _ = u.shape
    T = min(S5_TILE, L)
    nt = L // T
    rev = d == 1
    has_dy = dy is not None
    PAD = 8

    def body(*refs):
        u_ref = refs[0]
        k = 1
        if has_dy:
            dy_ref = refs[1]
            k = 2
        bre_ref, bim_ref, ct_ref, cb_ref, a_ref, hin_ref, dhend_ref = refs[k:k + 7]
        du_ref, dbre_ref, dbim_ref, dct_ref, dcb_ref, da_ref, dh0_ref = refs[k + 7:k + 14]
        hr_scr, hi_scr, gr_scr, gi_scr, p_scr = refs[k + 14:]
        n = pl.program_id(1)

        @pl.when(_first_step())
        def _():
            for r in (dbre_ref, dbim_ref, dct_ref, dcb_ref, da_ref):
                r[...] = jnp.zeros_like(r)

        @pl.when(n == 0)
        def _():
            p_scr[...] = dhend_ref[...]

        ar, ai = a_ref[0:1, :], a_ref[1:2, :]
        uv = u_ref[...].astype(bf16)
        _s5_in(uv, bre_ref, bim_ref, hr_scr, hi_scr, PAD, T)
        prev_row = PAD + T if rev else PAD - 1
        hr_scr[prev_row:prev_row + 1, :] = hin_ref[0:1, :]
        hi_scr[prev_row:prev_row + 1, :] = hin_ref[1:2, :]
        _scan_rows(T, rev, ar, ai, hin_ref[0:1, :], hin_ref[1:2, :], hr_scr, hi_scr, PAD)
        if has_dy:
            dyv = dy_ref[...].astype(bf16)
            for jb in range(S5_BLOCKS):
                st = slice(jb * S5_BS, (jb + 1) * S5_BS)
                dyj = dyv[:, jb * S5_BC:(jb + 1) * S5_BC]
                gr_scr[:, st] = _dot_nt(dyj, ct_ref[jb])
                gi_scr[:, st] = _dot_nt(dyj, cb_ref[jb])
                dct_ref[jb] += _dot_tn(hr_scr[PAD:PAD + T, st], dyj)
                dcb_ref[jb] += _dot_tn(hi_scr[PAD:PAD + T, st], dyj)
        else:
            gr_scr[...] = jnp.zeros_like(gr_scr)
            gi_scr[...] = jnp.zeros_like(gi_scr)

        def step(i, carry):
            pr, pi, dar, dai = carry
            t = i if rev else T - 1 - i
            gr = gr_scr[pl.ds(t, 1), :] + pr
            gi = gi_scr[pl.ds(t, 1), :] + pi
            gr_scr[pl.ds(t, 1), :] = gr
            gi_scr[pl.ds(t, 1), :] = gi
            tp = PAD + t + (1 if rev else -1)
            hpr = hr_scr[pl.ds(tp, 1), :]
            hpi = hi_scr[pl.ds(tp, 1), :]
            dar = dar + hpr * gr + hpi * gi
            dai = dai + hpr * gi - hpi * gr
            return ar * gr + ai * gi, ar * gi - ai * gr, dar, dai

        zero = jnp.zeros((1, S5_HALF), f32)
        pr, pi, dar, dai = lax.fori_loop(0, T, step, (p_scr[0:1, :], p_scr[1:2, :], zero, zero), unroll=4)
        p_scr[0:1, :] = pr
        p_scr[1:2, :] = pi
        da_ref[0:1, :] += dar
        da_ref[1:2, :] += dai
        for jb in range(S5_BLOCKS):
            st = slice(jb * S5_BS, (jb + 1) * S5_BS)
            ch = slice(jb * S5_BC, (jb + 1) * S5_BC)
            gr_j = gr_scr[:, st].astype(bf16)
            gi_j = gi_scr[:, st].astype(bf16)
            du_ref[:, ch] = _dot_nt(gr_j, bre_ref[jb]) + _dot_nt(gi_j, bim_ref[jb])
            dbre_ref[jb] += _dot_tn(uv[:, ch], gr_j)
            dbim_ref[jb] += _dot_tn(uv[:, ch], gi_j)

        @pl.when(n == nt - 1)
        def _():
            dh0_ref[...] = p_scr[...]

    brev = not rev
    state = _per_batch(2, S5_HALF)
    hin_spec = pl.BlockSpec((None, None, 2, S5_HALF), (lambda b, n: (b, nt - 1 - n, 0, 0)) if brev else (lambda b, n: (b, n, 0, 0)))
    w_in, w_out = _resident((S5_BLOCKS, S5_BC, S5_BS)), _resident((S5_BLOCKS, S5_BS, S5_BC))
    wspecs = [w_in, w_in, w_out, w_out]
    in_specs = [_tok(T, D_S5, nt, brev)] + ([_tok(T, D_S5, nt, brev)] if has_dy else []) + wspecs + [
        _resident((2, S5_HALF)), hin_spec, state]
    args = [u] + ([dy] if has_dy else []) + [bre, bim, ctop, cbot, arow, hin, dhend]
    return pl.pallas_call(
        body, name=name, grid=(B, nt), in_specs=in_specs,
        out_specs=[_tok(T, D_S5, nt, brev)] + wspecs + [_resident((2, S5_HALF)), state],
        out_shape=[SDS((B, L, D_S5), f32), SDS((S5_BLOCKS, S5_BC, S5_BS), f32), SDS((S5_BLOCKS, S5_BC, S5_BS), f32),
                   SDS((S5_BLOCKS, S5_BS, S5_BC), f32), SDS((S5_BLOCKS, S5_BS, S5_BC), f32), SDS((2, S5_HALF), f32),
                   SDS((B, 2, S5_HALF), f32)],
        scratch_shapes=[pltpu.VMEM((T + 2 * PAD, S5_HALF), f32), pltpu.VMEM((T + 2 * PAD, S5_HALF), f32),
                        pltpu.VMEM((T, S5_HALF), f32), pltpu.VMEM((T, S5_HALF), f32), pltpu.VMEM((2, S5_HALF), f32)],
        compiler_params=_cparams(2),
    )(*args)


def _glu_fn(u, y0, y1, z, dsk, wg, bg):
    g = _gelu(dsk * u + y0 + y1)
    return g * jax.nn.sigmoid(_mm(g, wg) + bg) * _silu(z)


def s5_glu_fwd(u, y0, y1, z, dsk, wg, bg):
    B, L, _ = u.shape
    T = min(TOK_TILE, L)

    def body(u_ref, y0_ref, y1_ref, z_ref, dsk_ref, wg_ref, bg_ref, o_ref):
        o_ref[...] = _glu_fn(u_ref[...], y0_ref[...], y1_ref[...], z_ref[...], dsk_ref[...], wg_ref[...].astype(f32),
                             bg_ref[...])

    t = _tok(T, D_S5)
    return pl.pallas_call(
        body, name="s5_glu_fwd", grid=(B, L // T),
        in_specs=[t, t, t, t, _resident((1, D_S5)), _resident((D_S5, D_S5)), _resident((1, D_S5))],
        out_specs=t, out_shape=SDS((B, L, D_S5), f32), compiler_params=_cparams(2),
    )(u, y0, y1, z, dsk, wg, bg)


def s5_glu_bwd(u, y0, y1, z, dsk, wg, bg, dout):
    B, L, _ = u.shape
    T = min(TOK_TILE, L)

    def body(u_ref, y0_ref, y1_ref, z_ref, dsk_ref, wg_ref, bg_ref, do_ref, du_ref, dy_ref, dz_ref,
             ddsk_ref, dwg_ref, dbg_ref):
        @pl.when(_first_step())
        def _():
            for r in (ddsk_ref, dwg_ref, dbg_ref):
                r[...] = jnp.zeros_like(r)

        _, vjp = jax.vjp(_glu_fn, u_ref[...], y0_ref[...], y1_ref[...], z_ref[...], dsk_ref[...],
                         wg_ref[...].astype(f32), bg_ref[...])
        du, dy, _, dz, ddsk, dwg, dbg = vjp(do_ref[...])
        du_ref[...], dy_ref[...], dz_ref[...] = du, dy, dz
        ddsk_ref[...] += ddsk
        dwg_ref[...] += dwg
        dbg_ref[...] += dbg

    t = _tok(T, D_S5)
    small = [_resident((1, D_S5)), _resident((D_S5, D_S5)), _resident((1, D_S5))]
    return pl.pallas_call(
        body, name="s5_glu_bwd", grid=(B, L // T),
        in_specs=[t, t, t, t] + small + [t], out_specs=[t, t, t] + small,
        out_shape=[SDS((B, L, D_S5), f32)] * 3 + [SDS((1, D_S5), f32), SDS((D_S5, D_S5), f32), SDS((1, D_S5), f32)],
        compiler_params=_cparams(2),
    )(u, y0, y1, z, dsk, wg, bg, dout)


CONV_ROWS = 16


def _conv_taps(L, is_ctx):
    t = lax.broadcasted_iota(jnp.int32, (L, 1), 0)
    taps = []
    for di in ((1,) if is_ctx else (0, 1, 2)):
        for dj in (0, 1, 2):
            s = (0 if is_ctx else GRID_W * (di - 1)) + (dj - 1)
            if is_ctx:
                ok = jnp.logical_and(t + s >= 0, t + s < L)
            else:
                col = jnp.bitwise_and(t, GRID_W - 1) + (dj - 1)
                row = t + GRID_W * (di - 1)
                ok = jnp.logical_and(jnp.logical_and(col >= 0, col < GRID_W), jnp.logical_and(row >= 0, row < L))
            taps.append((di * 3 + dj, s, ok.astype(f32)))
    return taps


def _shift(x, s):
    L = x.shape[0]
    k = (-s) % L
    return x if k == 0 else pltpu.roll(x, k, axis=0)


def _qk_post(pre, is_norm, scale):
    s = _silu(pre)
    nrm = lax.rsqrt(jnp.sum(s * s, axis=-1, keepdims=True) + NORM_EPS)
    return s * jnp.where(is_norm, nrm * scale, 1.0)


def _conv_kind():
    ct = pl.program_id(1)
    return ct < 2 * GDN_HEADS, jnp.where(ct < GDN_HEADS, GDN_HEAD ** -0.5, 1.0).astype(f32)


def _conv_pre(xv, w_ref, taps):
    pre = jnp.zeros_like(xv)
    for r, s, m in taps:
        pre = pre + w_ref[r:r + 1, :] * (m * _shift(xv, s))
    return pre


def conv_fwd(qkv, w16, *, is_ctx, name):
    B, L, C = qkv.shape
    spec = pl.BlockSpec((None, L, GDN_HEAD), lambda b, ct: (b, 0, ct))
    wspec = pl.BlockSpec((CONV_ROWS, GDN_HEAD), lambda b, ct: (0, ct))

    def body(x_ref, w_ref, o_ref):
        is_norm, scale = _conv_kind()
        o_ref[...] = _qk_post(_conv_pre(x_ref[...], w_ref, _conv_taps(L, is_ctx)), is_norm, scale)

    return pl.pallas_call(body, name=name, grid=(B, C // GDN_HEAD), in_specs=[spec, wspec], out_specs=spec,
                          out_shape=SDS((B, L, C), f32), compiler_params=_cparams(2))(qkv, w16)


def conv_bwd(qkv, w16, da0, da1, *, is_ctx, name):
    B, L, C = qkv.shape
    spec = pl.BlockSpec((None, L, GDN_HEAD), lambda b, ct: (b, 0, ct))
    wspec = pl.BlockSpec((CONV_ROWS, GDN_HEAD), lambda b, ct: (0, ct))
    dwspec = pl.BlockSpec((None, CONV_ROWS, GDN_HEAD), lambda b, ct: (b, 0, ct))

    def body(x_ref, w_ref, d0_ref, d1_ref, dx_ref, dw_ref):
        is_norm, scale = _conv_kind()
        taps = _conv_taps(L, is_ctx)
        xv = x_ref[...]
        _, vjp = jax.vjp(lambda p: _qk_post(p, is_norm, scale), _conv_pre(xv, w_ref, taps))
        dpre = vjp(d0_ref[...] + d1_ref[...])[0]
        dx = jnp.zeros_like(xv)
        dw_ref[...] = jnp.zeros_like(dw_ref)
        for r, s, m in taps:
            md = m * dpre
            dx = dx + _shift(w_ref[r:r + 1, :] * md, -s)
            dw_ref[r:r + 1, :] = jnp.sum(md * _shift(xv, s), axis=0, keepdims=True)
        dx_ref[...] = dx

    return pl.pallas_call(body, name=name, grid=(B, C // GDN_HEAD), in_specs=[spec, wspec, spec, spec],
                          out_specs=[spec, dwspec], out_shape=[SDS((B, L, C), f32), SDS((B, CONV_ROWS, C), f32)],
                          compiler_params=_cparams(2))(qkv, w16, da0, da1)


def _gates_fn(ba, alog, dtb):
    T = ba.shape[0]
    lane = lax.broadcasted_iota(jnp.int32, ba.shape, 1)
    ii = lax.broadcasted_iota(jnp.int32, (T, T), 0)
    jj = lax.broadcasted_iota(jnp.int32, (T, T), 1)
    same = jnp.right_shift(ii, 6) == jnp.right_shift(jj, 6)
    lmat = jnp.logical_and(same, ii >= jj).astype(f32)
    umat = jnp.logical_and(same, ii <= jj).astype(f32)
    g = jnp.where(lane >= 8, -jnp.exp(alog) * jax.nn.softplus(ba + dtb), 0.0)
    gc = jnp.where(lane >= 12, _dot_hi(umat, g), _dot_hi(lmat, g))
    return jnp.where(lane < 8, jax.nn.sigmoid(ba), gc)


def gates_fwd(ba, alog, dtb, *, name):
    B, L, _ = ba.shape
    T = min(TOK_TILE, L)
    t = _tok(T, N_GATE)

    def body(ba_ref, al_ref, dt_ref, o_ref):
        o_ref[...] = _gates_fn(ba_ref[...], al_ref[...], dt_ref[...])

    return pl.pallas_call(body, name=name, grid=(B, L // T),
                          in_specs=[t, _resident((1, N_GATE)), _resident((1, N_GATE))], out_specs=t,
                          out_shape=SDS((B, L, N_GATE), f32), compiler_params=_cparams(2))(ba, alog, dtb)


def gates_bwd(ba, alog, dtb, dbg, *, name):
    B, L, _ = ba.shape
    T = min(TOK_TILE, L)
    t = _tok(T, N_GATE)
    small = _resident((1, N_GATE))

    def body(ba_ref, al_ref, dt_ref, d_ref, dba_ref, dal_ref, ddt_ref):
        @pl.when(_first_step())
        def _():
            dal_ref[...] = jnp.zeros_like(dal_ref)
            ddt_ref[...] = jnp.zeros_like(ddt_ref)

        _, vjp = jax.vjp(_gates_fn, ba_ref[...], al_ref[...], dt_ref[...])
        dba, dal, ddt = vjp(d_ref[...])
        dba_ref[...] = dba
        dal_ref[...] += dal
        ddt_ref[...] += ddt

    return pl.pallas_call(body, name=name, grid=(B, L // T), in_specs=[t, small, small, t],
                          out_specs=[t, small, small],
                          out_shape=[SDS((B, L, N_GATE), f32), SDS((1, N_GATE), f32), SDS((1, N_GATE), f32)],
                          compiler_params=_cparams(2))(ba, alog, dtb, dbg)


@jax.custom_vjp
def _inv_unit_tri(mats):
    n = mats[0].shape[0]
    eye = (lax.broadcasted_iota(jnp.int32, (n, n), 0) == lax.broadcasted_iota(jnp.int32, (n, n), 1)).astype(f32)
    xs = [eye - a for a in mats]
    sq = [_dot(a, a) for a in mats]
    ps = sq
    k = 2
    while k < n:
        xs = [x + _dot(x, p) for x, p in zip(xs, ps)]
        k *= 2
        if k < n:
            ps = [_dot(p, p) for p in ps]
    return tuple(_dot(p, x) - a for p, x, a in zip(sq, xs, mats))


def _inv_unit_tri_fwd(mats):
    ns = _inv_unit_tri(mats)
    return ns, ns


def _inv_unit_tri_bwd(ns, dns):
    ys = [dn + _dot_tn(nn, dn) for nn, dn in zip(ns, dns)]
    return (tuple(-(y + _dot_nt(y, nn)) for y, nn in zip(ys, ns)),)


_inv_unit_tri.defvjp(_inv_unit_tri_fwd, _inv_unit_tri_bwd)


def _gdn_chunk(heads, *, rev):
    n = heads[0][0].shape[0]
    ii = lax.broadcasted_iota(jnp.int32, (n, n), 0)
    jj = lax.broadcasted_iota(jnp.int32, (n, n), 1)
    lower = (ii <= jj) if rev else (ii >= jj)
    strict = (ii < jj) if rev else (ii > jj)
    last = 0 if rev else n - 1
    row = lax.broadcasted_iota(jnp.int32, (n, 1), 0)
    H = range(len(heads))
    q, k, v, beta, gc, gr, s = (list(t) for t in zip(*heads))
    decay = [jnp.where(lower, jnp.exp(jnp.where(lower, gc[h] - gr[h], 0.0)), 0.0) for h in H]
    kk = [_mm_nt(k[h], k[h]) for h in H]
    qk = [_mm_nt(q[h], k[h]) * decay[h] for h in H]
    qs = [_mm(q[h], s[h]) for h in H]
    a_mat = tuple(jnp.where(strict, beta[h] * kk[h] * decay[h], 0.0) for h in H)
    gamma = [jnp.exp(gc[h]) for h in H]
    g_last = [jnp.sum(jnp.where(row == last, gc[h], 0.0), axis=0, keepdims=True) for h in H]
    nmat = _inv_unit_tri(a_mat)
    bv = [beta[h] * v[h] for h in H]
    bk = [(beta[h] * gamma[h]) * k[h] for h in H]
    u0 = [bv[h] + _mm(nmat[h], bv[h]) for h in H]
    w = [bk[h] + _mm(nmat[h], bk[h]) for h in H]
    k_out = [k[h] * jnp.exp(g_last[h] - gc[h]) for h in H]
    u = [u0[h] - _mm(w[h], s[h]) for h in H]
    o = [gamma[h] * qs[h] + _mm(qk[h], u[h]) for h in H]
    s_new = [jnp.exp(g_last[h]) * s[h] + _mm_tn(k_out[h], u[h]) for h in H]
    return tuple((o[h], s_new[h]) for h in H)


def _gdn_specs(B, nc, rev):
    def cidx(n):
        return (nc - 1 - n) if rev else n
    tok = lambda width: pl.BlockSpec((B, CHUNK, width), lambda n: (0, cidx(n), 0))
    rowspec = pl.BlockSpec((B, None, N_GATE, CHUNK), lambda n: (0, cidx(n), 0, 0))
    st = pl.BlockSpec((B, GDN_HEADS, GDN_HEAD, GDN_HEAD), lambda n: (0, 0, 0, 0))
    ck = pl.BlockSpec((B, None, GDN_HEADS, GDN_HEAD, GDN_HEAD), lambda n: (0, cidx(n), 0, 0, 0))
    return tok, rowspec, st, ck


def _gdn_head_args(qkv_ref, bg_ref, bgr_ref, b, d, h):
    col = d * GDN_HEADS + h
    q = qkv_ref[b, :, h * GDN_HEAD:(h + 1) * GDN_HEAD]
    k = qkv_ref[b, :, D_GDN + h * GDN_HEAD:D_GDN + (h + 1) * GDN_HEAD]
    v = qkv_ref[b, :, 2 * D_GDN + h * GDN_HEAD:2 * D_GDN + (h + 1) * GDN_HEAD]
    bgv = bg_ref[b]
    return q, k, v, bgv[:, col:col + 1], bgv[:, 8 + col:9 + col], bgr_ref[b][8 + col:9 + col, :]


def gdn_fwd(qkv, bg, bgr, s0, *, d, need_o, name):
    B, L, _ = qkv.shape
    nc = L // CHUNK
    rev = d == 1
    tok, rowspec, st, ck = _gdn_specs(B, nc, rev)
    bh = [(b, h) for b in range(B) for h in range(GDN_HEADS)]

    def body(qkv_ref, bg_ref, bgr_ref, s0_ref, *rest):
        if need_o:
            o_ref, ck_ref, sf_ref, s_scr = rest
        else:
            ck_ref, sf_ref, s_scr = rest
        n = pl.program_id(0)

        @pl.when(n == 0)
        def _():
            s_scr[...] = s0_ref[...]

        ck_ref[...] = s_scr[...]
        heads = tuple(_gdn_head_args(qkv_ref, bg_ref, bgr_ref, b, d, h) + (s_scr[b, h],) for b, h in bh)
        for (b, h), (o, s_new) in zip(bh, _gdn_chunk(heads, rev=rev)):
            if need_o:
                o_ref[b, :, h * GDN_HEAD:(h + 1) * GDN_HEAD] = o
            s_scr[b, h] = s_new

        @pl.when(n == nc - 1)
        def _():
            sf_ref[...] = s_scr[...]

    out_specs = [ck, st]
    out_shape = [SDS((B, nc, GDN_HEADS, GDN_HEAD, GDN_HEAD), f32), SDS((B, GDN_HEADS, GDN_HEAD, GDN_HEAD), f32)]
    if need_o:
        out_specs.insert(0, tok(D_GDN))
        out_shape.insert(0, SDS((B, L, D_GDN), f32))
    return pl.pallas_call(
        body, name=name, grid=(nc,), in_specs=[tok(3 * D_GDN), tok(N_GATE), rowspec, st],
        out_specs=out_specs, out_shape=out_shape,
        scratch_shapes=[pltpu.VMEM((B, GDN_HEADS, GDN_HEAD, GDN_HEAD), f32)], compiler_params=_cparams(1),
    )(qkv, bg, bgr, s0)


def gdn_bwd(qkv, bg, bgr, ck, do, dsf, *, d, name):
    B, L, _ = qkv.shape
    nc = L // CHUNK
    rev = d == 1
    has_do = do is not None
    tok, rowspec, st, ckspec = _gdn_specs(B, nc, not rev)
    bh = [(b, h) for b in range(B) for h in range(GDN_HEADS)]

    def body(*refs):
        qkv_ref, bg_ref, bgr_ref, ck_ref = refs[:4]
        k = 4
        if has_do:
            do_ref = refs[4]
            k = 5
        dsf_ref, dqkv_ref, dbg_ref, dbgr_ref, ds0_ref, ds_scr = refs[k:]
        n = pl.program_id(0)

        @pl.when(n == 0)
        def _():
            ds_scr[...] = dsf_ref[...]

        lane = lax.broadcasted_iota(jnp.int32, (CHUNK, N_GATE), 1)
        sub = lax.broadcasted_iota(jnp.int32, (N_GATE, CHUNK), 0)
        heads = tuple(_gdn_head_args(qkv_ref, bg_ref, bgr_ref, b, d, h) + (ck_ref[b, h],) for b, h in bh)
        _, vjp = jax.vjp(functools.partial(_gdn_chunk, rev=rev), heads)
        cts = tuple(((do_ref[b, :, h * GDN_HEAD:(h + 1) * GDN_HEAD] if has_do else jnp.zeros((CHUNK, GDN_HEAD), f32)),
                     ds_scr[b, h]) for b, h in bh)
        (dheads,) = vjp(cts)
        dbg_acc = [jnp.zeros((CHUNK, N_GATE), f32) for _ in range(B)]
        dbgr_acc = [jnp.zeros((N_GATE, CHUNK), f32) for _ in range(B)]
        for (b, h), (dq, dk, dv, db, dgc, dgr, ds) in zip(bh, dheads):
            col = d * GDN_HEADS + h
            dqkv_ref[b, :, h * GDN_HEAD:(h + 1) * GDN_HEAD] = dq
            dqkv_ref[b, :, D_GDN + h * GDN_HEAD:D_GDN + (h + 1) * GDN_HEAD] = dk
            dqkv_ref[b, :, 2 * D_GDN + h * GDN_HEAD:2 * D_GDN + (h + 1) * GDN_HEAD] = dv
            dbg_acc[b] = dbg_acc[b] + jnp.where(lane == col, db, 0.0) + jnp.where(lane == 8 + col, dgc, 0.0)
            dbgr_acc[b] = dbgr_acc[b] + jnp.where(sub == 8 + col, dgr, 0.0)
            ds_scr[b, h] = ds
        for b in range(B):
            dbg_ref[b] = dbg_acc[b]
            dbgr_ref[b] = dbgr_acc[b]

        @pl.when(n == nc - 1)
        def _():
            ds0_ref[...] = ds_scr[...]

    in_specs = [tok(3 * D_GDN), tok(N_GATE), rowspec, ckspec] + ([tok(D_GDN)] if has_do else []) + [st]
    args = [qkv, bg, bgr, ck] + ([do] if has_do else []) + [dsf]
    return pl.pallas_call(
        body, name=name, grid=(nc,), in_specs=in_specs,
        out_specs=[tok(3 * D_GDN), tok(N_GATE), rowspec, st],
        out_shape=[SDS((B, L, 3 * D_GDN), f32), SDS((B, L, N_GATE), f32), SDS((B, nc, N_GATE, CHUNK), f32),
                   SDS((B, GDN_HEADS, GDN_HEAD, GDN_HEAD), f32)],
        scratch_shapes=[pltpu.VMEM((B, GDN_HEADS, GDN_HEAD, GDN_HEAD), f32)], compiler_params=_cparams(1),
    )(*args)


def _gnorm_fn(o0, o1, z, w):
    o = o0 + o1
    return o * lax.rsqrt(jnp.mean(o * o, axis=-1, keepdims=True) + NORM_EPS) * w * _silu(z)


def gnorm_fwd(o0, o1, z, w):
    B, L, _ = o0.shape
    T = min(TOK_TILE, L)
    t = _tok(T, D_GDN)

    def body(o0_ref, o1_ref, z_ref, w_ref, out_ref):
        for h in range(GDN_HEADS):
            sl = slice(h * GDN_HEAD, (h + 1) * GDN_HEAD)
            out_ref[:, sl] = _gnorm_fn(o0_ref[:, sl], o1_ref[:, sl], z_ref[:, sl], w_ref[...])

    return pl.pallas_call(body, name="gnorm_fwd", grid=(B, L // T), in_specs=[t, t, t, _resident((1, GDN_HEAD))],
                          out_specs=t, out_shape=SDS((B, L, D_GDN), f32), compiler_params=_cparams(2))(o0, o1, z, w)


def gnorm_bwd(o0, o1, z, w, dout):
    B, L, _ = o0.shape
    T = min(TOK_TILE, L)
    t = _tok(T, D_GDN)

    def body(o0_ref, o1_ref, z_ref, w_ref, d_ref, do_ref, dz_ref, dw_ref):
        @pl.when(_first_step())
        def _():
            dw_ref[...] = jnp.zeros_like(dw_ref)

        for h in range(GDN_HEADS):
            sl = slice(h * GDN_HEAD, (h + 1) * GDN_HEAD)
            _, vjp = jax.vjp(_gnorm_fn, o0_ref[:, sl], o1_ref[:, sl], z_ref[:, sl], w_ref[...])
            do, _, dz, dw = vjp(d_ref[:, sl])
            do_ref[:, sl] = do
            dz_ref[:, sl] = dz
            dw_ref[...] += dw

    return pl.pallas_call(body, name="gnorm_bwd", grid=(B, L // T),
                          in_specs=[t, t, t, _resident((1, GDN_HEAD)), t], out_specs=[t, t, _resident((1, GDN_HEAD))],
                          out_shape=[SDS((B, L, D_GDN), f32), SDS((B, L, D_GDN), f32), SDS((1, GDN_HEAD), f32)],
                          compiler_params=_cparams(2))(o0, o1, z, w, dout)


def _head_loss(y, x, gate, lng, lnb, tgt):
    r = DEEPNORM_ALPHA * x + gate * y
    mu = jnp.mean(r, axis=-1, keepdims=True)
    rc = r - mu
    var = jnp.mean(rc * rc, axis=-1, keepdims=True)
    err = rc * lax.rsqrt(var + LN_EPS) * lng + lnb - tgt
    return (0.5 / D_MODEL) * jnp.sum(jnp.sum(err * err, axis=-1, keepdims=True), axis=0, keepdims=True)


def head_fwd_bwd(s5o, gdo, x, tgt, gate, lng, lnb, ws, wg):
    B, L, _ = x.shape
    T = min(TOK_TILE, L)

    def body(s_ref, g_ref, x_ref, t_ref, gate_ref, lng_ref, lnb_ref, ws_ref, wg_ref,
             loss_ref, ds_ref, dg_ref, gx_ref, dws_ref, dwg_ref, dgate_ref, dlng_ref, dlnb_ref):
        n = pl.program_id(1)

        @pl.when(_first_step())
        def _():
            for r in (dws_ref, dwg_ref, dlng_ref, dlnb_ref):
                r[...] = jnp.zeros_like(r)

        @pl.when(n == 0)
        def _():
            loss_ref[...] = jnp.zeros_like(loss_ref)
            dgate_ref[...] = jnp.zeros_like(dgate_ref)

        sv = s_ref[...].astype(bf16)
        gv = g_ref[...].astype(bf16)
        y = _dot(sv, ws_ref[...]) + _dot(gv, wg_ref[...])
        loss, vjp = jax.vjp(lambda *a: _head_loss(*a, t_ref[...]), y, x_ref[...], gate_ref[...], lng_ref[...],
                            lnb_ref[...])
        dy, dx, dgate, dlng, dlnb = vjp(jnp.ones((1, 1), f32))
        loss_ref[...] += jnp.broadcast_to(loss, loss_ref.shape)
        dyb = dy.astype(bf16)
        ds_ref[...] = _dot_nt(dyb, ws_ref[...])
        dg_ref[...] = _dot_nt(dyb, wg_ref[...])
        gx_ref[...] = dx
        dws_ref[...] += _dot_tn(sv, dyb)
        dwg_ref[...] += _dot_tn(gv, dyb)
        dgate_ref[...] += dgate
        dlng_ref[...] += dlng
        dlnb_ref[...] += dlnb

    half, full = _tok(T, D_S5), _tok(T, D_MODEL)
    row = _resident((1, D_MODEL))
    wsp = _resident((D_S5, D_MODEL))
    return pl.pallas_call(
        body, name="head_fwd_bwd", grid=(B, L // T),
        in_specs=[half, half, full, full, _per_batch(1, D_MODEL), row, row, wsp, wsp],
        out_specs=[_per_batch(8, LANES), half, half, full, wsp, wsp, _per_batch(1, D_MODEL), row, row],
        out_shape=[SDS((B, 8, LANES), f32), SDS((B, L, D_S5), f32), SDS((B, L, D_GDN), f32), SDS((B, L, D_MODEL), f32),
                   SDS((D_S5, D_MODEL), f32), SDS((D_GDN, D_MODEL), f32), SDS((B, 1, D_MODEL), f32),
                   SDS((1, D_MODEL), f32), SDS((1, D_MODEL), f32)],
        compiler_params=_cparams(2),
    )(s5o, gdo, x, tgt, gate, lng, lnb, ws, wg)


def _adamw_math(w, g, m, v):
    nm = ADAM_B1 * m + (1.0 - ADAM_B1) * g
    nv = ADAM_B2 * v + (1.0 - ADAM_B2) * jnp.square(g)
    m_hat = nm / (1.0 - ADAM_B1 ** ADAM_STEP)
    v_hat = nv / (1.0 - ADAM_B2 ** ADAM_STEP)
    return -ADAM_LR * (m_hat / (jnp.sqrt(v_hat) + ADAM_EPS) + ADAM_WD * w), nm, nv


def _row_tile(rows, cap=512):
    for t in range(min(cap, rows), 15, -1):
        if rows % t == 0 and t % 16 == 0:
            return t
    return rows


def adamw_3d(w, g, m, v, *, name):
    _, R, C = w.shape
    T = _row_tile(R)
    spec = pl.BlockSpec((None, T, C), lambda i: (0, i, 0))

    def body(w_ref, g_ref, m_ref, v_ref, d_ref, nm_ref, nv_ref):
        d_ref[...], nm_ref[...], nv_ref[...] = _adamw_math(w_ref[...], g_ref[...], m_ref[...], v_ref[...])

    return pl.pallas_call(body, name=name, grid=(R // T,), in_specs=[spec] * 4, out_specs=[spec] * 3,
                          out_shape=[SDS((1, R, C), f32)] * 3, compiler_params=_cparams(1))(w, g, m, v)


def adamw_small(ws, gs, ms, vs):
    n = len(ws)

    def body(*refs):
        outs = refs[4 * n:]
        for i in range(n):
            d, nm, nv = _adamw_math(refs[i][...], refs[n + i][...], refs[2 * n + i][...], refs[3 * n + i][...])
            outs[i][...], outs[n + i][...], outs[2 * n + i][...] = d, nm, nv

    res = pl.pallas_call(body, name="adamw_small", out_shape=[SDS(w.shape, f32) for w in ws] * 3,
                         compiler_params=pltpu.CompilerParams(vmem_limit_bytes=VMEM_LIMIT))(*ws, *gs, *ms, *vs)
    return res[:n], res[n:2 * n], res[2 * n:]


def sum_cores(own, got, *, name):
    A, H, C = own.shape
    T = _row_tile(H)
    spec = pl.BlockSpec((None, T, C), lambda a, i: (a, i, 0))

    def body(a_ref, b_ref, q32_ref, q16_ref):
        q = a_ref[...] + b_ref[...]
        q32_ref[...] = q
        q16_ref[...] = q.astype(bf16)

    return pl.pallas_call(body, name=name, grid=(A, H // T), in_specs=[spec, spec], out_specs=[spec, spec],
                          out_shape=[SDS((A, H, C), f32), SDS((A, H, C), bf16)], compiler_params=_cparams(2))(own, got)


def sum_chips(mine, rec, cpos, *, name):
    H, C = mine.shape
    T = _row_tile(H)
    nt = H // T

    def body(c_ref, m_ref, r_ref, f_ref):
        f_ref[...] = ((m_ref[...] + r_ref[0].astype(f32)) + r_ref[1].astype(f32)) + r_ref[2].astype(f32)

    grid_spec = pltpu.PrefetchScalarGridSpec(
        num_scalar_prefetch=1, grid=(nt,),
        in_specs=[pl.BlockSpec((T, C), lambda i, c_ref: (i, 0)), pl.BlockSpec((3, T, C), lambda i, c_ref: (0, i, 0))],
        out_specs=pl.BlockSpec((None, T, C), lambda i, c_ref: (0, c_ref[0] * nt + i, 0)))
    return pl.pallas_call(body, name=name, grid_spec=grid_spec, out_shape=SDS((1, 2 * H, C), f32),
                          compiler_params=_cparams(1))(cpos.reshape(1).astype(jnp.int32), mine, rec)


CHIP_FLIPS = ((1, 0), (0, 1), (1, 1))


def _pos():
    return lax.axis_index("x"), lax.axis_index("y"), lax.axis_index("c")


def _comm_call(body, srcs, out_sds, n_remote, n_local, name):
    any_spec = pl.BlockSpec(memory_space=pl.ANY)
    return pl.pallas_call(
        body, name=name, in_specs=[any_spec] * len(srcs), out_specs=[any_spec] * len(out_sds), out_shape=out_sds,
        scratch_shapes=[pltpu.SemaphoreType.DMA((n_remote,)), pltpu.SemaphoreType.DMA((n_remote,)),
                        pltpu.SemaphoreType.DMA((max(n_local, 1),))],
        compiler_params=pltpu.CompilerParams(has_side_effects=True),
    )(*srcs)


def _remote(src, dst, send_sems, recv_sems, k, target):
    return pltpu.make_async_remote_copy(src, dst, send_sems.at[k], recv_sems.at[k], device_id=target,
                                        device_id_type=MESH)


def _half_rows(c, rows):
    half = rows // 2
    return pl.ds(pl.multiple_of(c * half, 8), half)


def gather_shards(shards):
    nt = len(shards)

    def body(*refs):
        srcs, outs = refs[:nt], refs[nt:2 * nt]
        send_sems, recv_sems, _ = refs[2 * nt:]
        x, y, c = _pos()
        j = 2 * x + y
        sib = (x, y, 1 - c)
        own = [_remote(srcs[t], outs[t].at[j], send_sems, recv_sems, 7 * t + 6, sib) for t in range(nt)]
        first, passed = [], []
        for k, (fx, fy) in enumerate(CHIP_FLIPS):
            tx, ty = x ^ fx, y ^ fy
            jk = 2 * tx + ty
            for t in range(nt):
                rows = _half_rows(c, srcs[t].shape[0])
                first.append(_remote(srcs[t].at[rows], outs[t].at[j, rows], send_sems, recv_sems, 7 * t + k, (tx, ty, c)))
                passed.append(_remote(outs[t].at[jk, rows], outs[t].at[jk, rows], send_sems, recv_sems, 7 * t + 3 + k, sib))
        for cp in first + own:
            cp.start()
        for a, b in zip(first, passed):
            a.wait_recv()
            b.start()
        for cp in passed + own:
            cp.wait_recv()
        for cp in first + passed + own:
            cp.wait_send()

    return _comm_call(body, shards, [SDS((4,) + s.shape, s.dtype) for s in shards], 7 * nt, 0, "gather_shards")


def swap_halves(ps):
    nt = len(ps)

    def body(*refs):
        srcs, outs = refs[:nt], refs[nt:2 * nt]
        send_sems, recv_sems, _ = refs[2 * nt:]
        x, y, c = _pos()
        cps = [_remote(srcs[t].at[a, _half_rows(1 - c, srcs[t].shape[1])], outs[t].at[a], send_sems, recv_sems, 4 * t + a,
                       (x, y, 1 - c)) for t in range(nt) for a in range(4)]
        for cp in cps:
            cp.start()
        for cp in cps:
            cp.wait()

    return _comm_call(body, ps, [SDS((4, p.shape[1] // 2, p.shape[2]), p.dtype) for p in ps], 4 * nt, 0, "swap_halves")


def scatter_to_chips(qs):
    nt = len(qs)

    def body(*refs):
        srcs, outs = refs[:nt], refs[nt:2 * nt]
        send_sems, recv_sems, _ = refs[2 * nt:]
        x, y, c = _pos()
        cps = []
        for k, (fx, fy) in enumerate(CHIP_FLIPS):
            tx, ty = x ^ fx, y ^ fy
            for t in range(nt):
                cps.append(_remote(srcs[t].at[2 * tx + ty], outs[t].at[k], send_sems, recv_sems, 3 * t + k, (tx, ty, c)))
        for cp in cps:
            cp.start()
        for cp in cps:
            cp.wait()

    return _comm_call(body, qs, [SDS((3,) + q.shape[1:], q.dtype) for q in qs], 3 * nt, 0, "scatter_to_chips")


def join_halves(fs):
    nt = len(fs)

    def body(*refs):
        outs = refs[nt:2 * nt]
        send_sems, recv_sems, _ = refs[2 * nt:]
        x, y, c = _pos()
        cps = []
        for t in range(nt):
            mine = outs[t].at[0, _half_rows(c, outs[t].shape[1])]
            cps.append(_remote(mine, mine, send_sems, recv_sems, t, (x, y, 1 - c)))
        for cp in cps:
            cp.start()
        for cp in cps:
            cp.wait()

    any_spec = pl.BlockSpec(memory_space=pl.ANY)
    return pl.pallas_call(
        body, name="join_halves", in_specs=[any_spec] * nt, out_specs=[any_spec] * nt,
        out_shape=[SDS(f.shape, f.dtype) for f in fs], input_output_aliases={t: t for t in range(nt)},
        scratch_shapes=[pltpu.SemaphoreType.DMA((nt,)), pltpu.SemaphoreType.DMA((nt,)), pltpu.SemaphoreType.DMA((1,))],
        compiler_params=pltpu.CompilerParams(has_side_effects=True),
    )(*fs)


def gather_small(s):
    def body(src, out, send_sems, recv_sems, _):
        x, y, c = _pos()
        j = 2 * x + y
        cps = [_remote(src, out.at[j], send_sems, recv_sems, k, (x ^ fx, y ^ fy, c)) for k, (fx, fy) in enumerate(CHIP_FLIPS)]
        cps.append(_remote(src, out.at[j], send_sems, recv_sems, 3, (x, y, 1 - c)))
        for cp in cps:
            cp.start()
        for cp in cps:
            cp.wait()

    return _comm_call(body, [s], [SDS((4,) + s.shape, s.dtype)], 4, 0, "gather_small")[0]


SMALL_SHAPES = ((D_MODEL,), (1, 3 * D_MODEL), (1, 2, 32, 64), (1, 2, 32, 64), (1, 2, 32), (1, 2, 32, 64, 16),
                (1, 2, 32, 64, 16), (1, 2, 32, 16, 64), (1, 2, 32, 16, 64), (1, D_S5), (1, D_S5), (1, 2, 4), (1, 2, 4),
                (1, GDN_HEAD), (1, D_MODEL), (1, D_MODEL))


def _size(shape):
    return functools.reduce(lambda p, q: p * q, shape)


SMALL_ROWS = tuple(-(-_size(s) // LANES) for s in SMALL_SHAPES)
SMALL_TOTAL = 2176
SMALL_QUARTER = SMALL_TOTAL // 4


def _rows(a):
    flat = a.reshape(-1)
    pad = (-flat.shape[0]) % LANES
    if pad:
        flat = jnp.concatenate([flat, jnp.zeros((pad,), flat.dtype)])
    return flat.reshape(-1, LANES)


def _pack_small(parts):
    rows = [_rows(p) for p in parts]
    rows.append(jnp.zeros((SMALL_TOTAL - sum(SMALL_ROWS), LANES), f32))
    return jnp.concatenate(rows, axis=0)


def _unpack_small(buf):
    out, r = [], 0
    for s, n in zip(SMALL_SHAPES, SMALL_ROWS):
        out.append(buf[r:r + n].reshape(-1)[:_size(s)].reshape(s))
        r += n
    return out


def _as_2d(a):
    return a.reshape(1, -1) if a.ndim == 1 else a.reshape(-1, a.shape[-1])


S5_BG = S5_GROUPS // S5_BLOCKS


def _block_diag_in(bb):
    eye = jnp.eye(S5_BG, dtype=bb.dtype)
    b4 = bb.reshape(S5_BLOCKS, S5_BG, S5_STATE, S5_GROUP)
    return jnp.einsum('jgpc,gh->jgchp', b4, eye).reshape(S5_BLOCKS, S5_BC, S5_BS)


def _block_diag_in_t(d):
    d6 = d.reshape(S5_BLOCKS, S5_BG, S5_GROUP, S5_BG, S5_STATE)
    return jnp.einsum('jgcgp->jgpc', d6).reshape(S5_GROUPS, S5_STATE * S5_GROUP)


def _block_diag_out(cm):
    eye = jnp.eye(S5_BG, dtype=cm.dtype)
    c4 = cm.reshape(S5_BLOCKS, S5_BG, S5_GROUP, S5_STATE)
    return jnp.einsum('jgcp,gh->jhpgc', c4, eye).reshape(S5_BLOCKS, S5_BS, S5_BC)


def _block_diag_out_t(d):
    d6 = d.reshape(S5_BLOCKS, S5_BG, S5_STATE, S5_BG, S5_GROUP)
    return jnp.einsum('jgpgc->jgcp', d6).reshape(S5_GROUPS, S5_GROUP, S5_STATE)


def _to_chunk_rows(a):
    B, L, W = a.shape
    return a.reshape(B, L // CHUNK, CHUNK, W).transpose(0, 1, 3, 2)


def _from_chunk_rows(a):
    B, nc, W, _ = a.shape
    return a.transpose(0, 1, 3, 2).reshape(B, nc * CHUNK, W)


def local_step(x, c, ctx, c_ctx, tgt, w_ada, b_ada, w_in, lam_re, lam_im, log_dt, b_re, b_im, c_re, c_im, s5_d,
               w_glu, b_glu, conv16, a_log, dt_bias, norm_w, w_out, ln_g, ln_b):
    B, L, _ = x.shape
    zeros_state = jnp.zeros((B, GDN_HEADS, GDN_HEAD, GDN_HEAD), f32)

    cc = jnp.concatenate([c, c_ctx[None, :], jnp.zeros((8 - B - 1, D_MODEL), f32)], axis=0)
    m = ada_fwd(cc, w_ada, b_ada)
    shift, scale, gate = m[:B, :D_MODEL], m[:B, D_MODEL:2 * D_MODEL], m[:B, 2 * D_MODEL:]
    mod = jnp.stack([scale, shift], axis=1)
    mod_c = jnp.broadcast_to(jnp.stack([m[B, D_MODEL:2 * D_MODEL], m[B, :D_MODEL]], axis=0)[None], (B, 2, D_MODEL))

    u, z_s5, qkv, z_gdn, ba = in_proj_fwd(x, mod, w_in, name="in_proj_fwd")
    uc, _, qkvc, _, bac = in_proj_fwd(ctx, mod_c, w_in, name="in_proj_fwd_ctx")

    ng = N_DIR * S5_GROUPS
    zoh_in = (lam_re.reshape(ng, S5_STATE), lam_im.reshape(ng, S5_STATE), log_dt.reshape(ng, 1),
              b_re.reshape(ng, S5_STATE * S5_GROUP), b_im.reshape(ng, S5_STATE * S5_GROUP))
    expand = (jnp.arange(S5_STATE * S5_GROUP)[None, :] // S5_GROUP == jnp.arange(S5_STATE)[:, None]).astype(f32)
    ar, ai, bbr, bbi = s5_zoh_fwd(*zoh_in, expand)
    bbr16, bbi16 = bbr.astype(bf16), bbi.astype(bf16)
    c_re16 = c_re.reshape(N_DIR, S5_GROUPS, S5_GROUP, S5_STATE).astype(bf16)
    c_im16 = (-c_im).reshape(N_DIR, S5_GROUPS, S5_GROUP, S5_STATE).astype(bf16)
    s5w, ys, hins, hins_c = [], [], [], []
    for d in range(N_DIR):
        g = slice(d * S5_GROUPS, (d + 1) * S5_GROUPS)
        wd = (_block_diag_in(bbr16[g]), _block_diag_in(bbi16[g]), _block_diag_out(c_re16[d]), _block_diag_out(c_im16[d]),
              jnp.stack([ar[g].reshape(-1), ai[g].reshape(-1)], axis=0))
        s5w.append(wd)
        hin_c, hend_c = s5_scan_fwd(uc, *wd, jnp.zeros((B, 2, S5_HALF), f32), d=d, need_y=False, name=f"s5_fwd_ctx{d}")
        y_d, hin, _ = s5_scan_fwd(u, *wd, hend_c, d=d, need_y=True, name=f"s5_fwd{d}")
        ys.append(y_d)
        hins.append(hin)
        hins_c.append(hin_c)
    glu_w = (s5_d.reshape(1, D_S5), w_glu, b_glu.reshape(1, D_S5))
    s5o = s5_glu_fwd(u, ys[0], ys[1], z_s5, *glu_w)

    act = conv_fwd(qkv, conv16, is_ctx=False, name="conv_fwd")
    act_c = conv_fwd(qkvc, conv16, is_ctx=True, name="conv_fwd_ctx")
    pad8 = jnp.zeros((1, 8), f32)
    alog16 = jnp.concatenate([pad8, a_log.reshape(1, 8)], axis=1)
    dtb16 = jnp.concatenate([pad8, dt_bias.reshape(1, 8)], axis=1)
    bg = gates_fwd(ba, alog16, dtb16, name="gates_fwd")
    bg_c = gates_fwd(bac, alog16, dtb16, name="gates_fwd_ctx")
    bgr, bgr_c = _to_chunk_rows(bg), _to_chunk_rows(bg_c)
    os_, cks, cks_c = [], [], []
    for d in range(N_DIR):
        ck_c, s_c = gdn_fwd(act_c, bg_c, bgr_c, zeros_state, d=d, need_o=False, name=f"gdn_fwd_ctx{d}")
        o_d, ck, _ = gdn_fwd(act, bg, bgr, s_c, d=d, need_o=True, name=f"gdn_fwd{d}")
        os_.append(o_d)
        cks.append(ck)
        cks_c.append(ck_c)
    nw = norm_w.reshape(1, GDN_HEAD)
    gdo = gnorm_fwd(os_[0], os_[1], z_gdn, nw)

    loss8, ds5o, dgdo, gx_res, dws, dwg, dgate, dlng, dlnb = head_fwd_bwd(
        s5o, gdo, x, tgt, gate[:, None, :], ln_g.reshape(1, D_MODEL), ln_b.reshape(1, D_MODEL), w_out[:D_S5], w_out[D_S5:])
    loss = jnp.sum(loss8[:, 0, 0])
    d_w_out = jnp.concatenate([dws, dwg], axis=0)

    do, dz_gdn, d_norm_w = gnorm_bwd(os_[0], os_[1], z_gdn, nw, dgdo)
    dacts, dacts_c = [], []
    dbg = jnp.zeros_like(bg)
    dbg_c = jnp.zeros_like(bg_c)
    for d in range(N_DIR):
        dact, dbg_d, dbgr_d, ds0 = gdn_bwd(act, bg, bgr, cks[d], do, zeros_state, d=d, name=f"gdn_bwd{d}")
        dact_c, dbgc_d, dbgrc_d, _ = gdn_bwd(act_c, bg_c, bgr_c, cks_c[d], None, ds0, d=d, name=f"gdn_bwd_ctx{d}")
        dacts.append(dact)
        dacts_c.append(dact_c)
        dbg = dbg + dbg_d + _from_chunk_rows(dbgr_d)
        dbg_c = dbg_c + dbgc_d + _from_chunk_rows(dbgrc_d)
    dba, dal, ddt = gates_bwd(ba, alog16, dtb16, dbg, name="gates_bwd")
    dbac, dal_c, ddt_c = gates_bwd(bac, alog16, dtb16, dbg_c, name="gates_bwd_ctx")
    d_a_log = (dal + dal_c)[:, 8:].reshape(1, N_DIR, GDN_HEADS)
    d_dt_bias = (ddt + ddt_c)[:, 8:].reshape(1, N_DIR, GDN_HEADS)
    dqkv, dcw = conv_bwd(qkv, conv16, dacts[0], dacts[1], is_ctx=False, name="conv_bwd")
    dqkvc, dcw_c = conv_bwd(qkvc, conv16, dacts_c[0], dacts_c[1], is_ctx=True, name="conv_bwd_ctx")
    d_conv16 = jnp.sum(dcw, axis=0) + jnp.sum(dcw_c, axis=0)

    du_skip, dy, dz_s5, d_s5_d, d_w_glu, d_b_glu = s5_glu_bwd(u, ys[0], ys[1], z_s5, *glu_w, ds5o)
    du, duc = du_skip, jnp.zeros_like(uc)
    dar, dai, dbbr, dbbi, dcre, dcim = [], [], [], [], [], []
    for d in range(N_DIR):
        du_d, dbre1, dbim1, dct1, dcb1, da1, dh0 = s5_scan_bwd(u, dy, *s5w[d], hins[d],
                                                                jnp.zeros((B, 2, S5_HALF), f32), d=d, name=f"s5_bwd{d}")
        duc_d, dbre2, dbim2, _, _, da2, _ = s5_scan_bwd(uc, None, *s5w[d], hins_c[d], dh0, d=d, name=f"s5_bwd_ctx{d}")
        du, duc = du + du_d, duc + duc_d
        da = da1 + da2
        dar.append(da[0].reshape(S5_GROUPS, S5_STATE))
        dai.append(da[1].reshape(S5_GROUPS, S5_STATE))
        dbbr.append(_block_diag_in_t(dbre1 + dbre2))
        dbbi.append(_block_diag_in_t(dbim1 + dbim2))
        dcre.append(_block_diag_out_t(dct1))
        dcim.append(-_block_diag_out_t(dcb1))
    dlr, dli, dldt, dbre, dbim = s5_zoh_bwd(*zoh_in, expand, jnp.concatenate(dar, 0), jnp.concatenate(dai, 0),
                                            jnp.concatenate(dbbr, 0), jnp.concatenate(dbbi, 0))
    d_s5 = (dlr, dli, dldt, dbre, dbim, jnp.stack(dcre, 0), jnp.stack(dcim, 0))

    padg = lambda a: jnp.concatenate([a, jnp.zeros(a.shape[:2] + (LANES - N_GATE,), f32)], axis=2)
    dw_l, dmod, grad_x = in_proj_bwd(x, mod, (du, dz_s5, dqkv, dz_gdn, padg(dba)), w_in, gx_res, name="in_proj_bwd")
    zc = jnp.zeros_like(uc)
    dw_c, dmod_c = in_proj_bwd(ctx, mod_c, (duc, zc, dqkvc, zc, padg(dbac)), w_in, None, name="in_proj_bwd_ctx")
    d_w_in = dw_l + dw_c
    dmod_c = jnp.sum(dmod_c, axis=0)

    dm_rows = jnp.concatenate([dmod[:, 1], dmod[:, 0], dgate[:, 0]], axis=1)
    dm_ctx = jnp.concatenate([dmod_c[1], dmod_c[0], jnp.zeros((D_MODEL,), f32)])[None]
    dm = jnp.concatenate([dm_rows, dm_ctx, jnp.zeros((8 - B - 1, 3 * D_MODEL), f32)], axis=0)
    dcc, d_w_ada, d_b_ada = ada_bwd(cc, w_ada, dm)
    small = (dcc[B], d_b_ada, *d_s5, d_s5_d, d_b_glu, d_a_log, d_dt_bias, d_norm_w, dlng, dlnb)
    small = tuple(g.reshape(s) for g, s in zip(small, SMALL_SHAPES))
    return loss, grad_x, (d_w_ada, d_w_in, d_w_out, d_w_glu, d_conv16), small


SHARDED = (1, 3, 18, 12, 14)
SMALL = tuple(i for i in range(21) if i not in SHARDED)
W_IN_SHARD = 772


def _conv_rows(w):
    return jnp.concatenate([w.reshape(9, w.shape[-1]), jnp.zeros((CONV_ROWS - 9, w.shape[-1]), f32)], axis=0)


def kernel(x, c, ctx, c_ctx, w_ada, b_ada, w_in, s5_lambda_re, s5_lambda_im, s5_log_dt, s5_b_re, s5_b_im, s5_c_re, s5_c_im, s5_d, w_glu, b_glu, conv_w, gdn_a_log, gdn_dt_bias, gdn_norm_w, w_out, ln_g, ln_b, loss_target, m_c_ctx, m_w_ada, m_b_ada, m_w_in, m_s5_lambda_re, m_s5_lambda_im, m_s5_log_dt, m_s5_b_re, m_s5_b_im, m_s5_c_re, m_s5_c_im, m_s5_d, m_w_glu, m_b_glu, m_conv_w, m_gdn_a_log, m_gdn_dt_bias, m_gdn_norm_w, m_w_out, m_ln_g, m_ln_b, v_c_ctx, v_w_ada, v_b_ada, v_w_in, v_s5_lambda_re, v_s5_lambda_im, v_s5_log_dt, v_s5_b_re, v_s5_b_im, v_s5_c_re, v_s5_c_im, v_s5_d, v_w_glu, v_b_glu, v_conv_w, v_gdn_a_log, v_gdn_dt_bias, v_gdn_norm_w, v_w_out, v_ln_g, v_ln_b):
    weights = [c_ctx, w_ada, b_ada, w_in, s5_lambda_re, s5_lambda_im, s5_log_dt, s5_b_re, s5_b_im, s5_c_re, s5_c_im,
               s5_d, w_glu, b_glu, conv_w, gdn_a_log, gdn_dt_bias, gdn_norm_w, w_out, ln_g, ln_b]
    ms = [m_c_ctx, m_w_ada, m_b_ada, m_w_in, m_s5_lambda_re, m_s5_lambda_im, m_s5_log_dt, m_s5_b_re, m_s5_b_im,
          m_s5_c_re, m_s5_c_im, m_s5_d, m_w_glu, m_b_glu, m_conv_w, m_gdn_a_log, m_gdn_dt_bias, m_gdn_norm_w, m_w_out,
          m_ln_g, m_ln_b]
    vs = [v_c_ctx, v_w_ada, v_b_ada, v_w_in, v_s5_lambda_re, v_s5_lambda_im, v_s5_log_dt, v_s5_b_re, v_s5_b_im,
          v_s5_c_re, v_s5_c_im, v_s5_d, v_w_glu, v_b_glu, v_conv_w, v_gdn_a_log, v_gdn_dt_bias, v_gdn_norm_w, v_w_out,
          v_ln_g, v_ln_b]
    cpos = lax.axis_index("c")
    jchip = 2 * lax.axis_index("x") + lax.axis_index("y")

    conv_shard = _conv_rows(conv_w)
    g_ada, g_in, g_out, g_glu, g_conv = gather_shards(
        [w_ada[0].astype(bf16), w_in[0].astype(bf16), w_out[0].astype(bf16), w_glu[0].astype(bf16), conv_shard])
    w_in_pad = jnp.concatenate([g_in[0], g_in[1], g_in[2], g_in[3], jnp.zeros((D_MODEL, IN_PAD - P_IN), bf16)], axis=1)
    conv16 = g_conv.transpose(1, 0, 2).reshape(CONV_ROWS, 3 * D_GDN)

    loss, grad_x, big, small = local_step(
        x, c, ctx, c_ctx, loss_target, g_ada, b_ada, w_in_pad, s5_lambda_re, s5_lambda_im, s5_log_dt, s5_b_re, s5_b_im,
        s5_c_re, s5_c_im, s5_d, g_glu.reshape(D_S5, D_S5), b_glu, conv16, gdn_a_log, gdn_dt_bias, gdn_norm_w,
        g_out.reshape(D_MODEL, D_MODEL), ln_g, ln_b)
    loss = lax.psum(loss, ("x", "y", "c"))

    d_w_ada, d_w_in, d_w_out, d_w_glu, d_conv16 = big
    slabs = [d_w_ada,
             d_w_in[:, :P_IN].reshape(D_MODEL, 4, W_IN_SHARD).transpose(1, 0, 2),
             d_w_out.reshape(4, D_MODEL // 4, D_MODEL),
             d_w_glu.reshape(4, D_S5 // 4, D_S5),
             d_conv16.reshape(CONV_ROWS, 4, 3 * D_GDN // 4).transpose(1, 0, 2),
             _pack_small(small).reshape(4, SMALL_QUARTER, LANES)]
    got = swap_halves(slabs)
    q32, q16 = [], []
    for t, (s, g) in enumerate(zip(slabs, got)):
        own = lax.dynamic_index_in_dim(s.reshape(4, 2, s.shape[1] // 2, s.shape[2]), cpos, axis=1, keepdims=False)
        a, b = sum_cores(own, g, name=f"sum_cores{t}")
        q32.append(a)
        q16.append(b)
    rec = scatter_to_chips(q16)
    fs = [sum_chips(lax.dynamic_index_in_dim(q, jchip, axis=0, keepdims=False), r, cpos, name=f"sum_chips{t}")
          for t, (q, r) in enumerate(zip(q32, rec))]
    red = join_halves(fs)
    g_small = _unpack_small(gather_small(red[5][0]).reshape(SMALL_TOTAL, LANES))

    grads, deltas, new_m, new_v = [None] * 21, [None] * 21, [None] * 21, [None] * 21
    for t, i in enumerate(SHARDED):
        conv = i == 14
        prep = (lambda a: _conv_rows(a)[None]) if conv else (lambda a: a)
        d, nm, nv = adamw_3d(prep(weights[i]), red[t], prep(ms[i]), prep(vs[i]), name=f"adamw{t}")
        for lst, val in ((grads, red[t]), (deltas, d), (new_m, nm), (new_v, nv)):
            lst[i] = val[0, :9].reshape(weights[i].shape) if conv else val
    sm = adamw_small([_as_2d(weights[i]) for i in SMALL], [_as_2d(g) for g in g_small], [_as_2d(ms[i]) for i in SMALL],
                     [_as_2d(vs[i]) for i in SMALL])
    for n, i in enumerate(SMALL):
        grads[i] = g_small[n]
        for lst, res in ((deltas, sm[0]), (new_m, sm[1]), (new_v, sm[2])):
            lst[i] = res[n].reshape(weights[i].shape)
    return (loss, grad_x, *grads, *deltas, *new_m, *new_v)
```

```python
import functools

import jax
import jax.numpy as jnp
from jax import lax
from jax.experimental import pallas as pl
from jax.experimental.pallas import tpu as pltpu

f32 = jnp.float32
bf16 = jnp.bfloat16
SDS = jax.ShapeDtypeStruct

D_MODEL = 1024
D_S5 = 512
S5_GROUP = 16
S5_GROUPS = 32
S5_STATE = 64
S5_HALF = S5_GROUPS * S5_STATE
D_GDN = 512
GDN_HEAD = 128
GDN_HEADS = 4
CHUNK = 64
GRID_W = 64
N_DIR = 2
P_IN = 3088
DEEPNORM_ALPHA = 2.0 ** 0.25
LN_EPS = 1e-5
NORM_EPS = 1e-6
ADAM_LR, ADAM_B1, ADAM_B2, ADAM_EPS, ADAM_WD, ADAM_STEP = 0.001, 0.9, 0.999, 1e-08, 0.01, 10

LANES = 128
VMEM_LIMIT = 56 * 1024 * 1024
TOK_TILE = 256
S5_TILE = 256
MESH = pl.DeviceIdType.MESH


def _cparams(n_grid):
    return pltpu.CompilerParams(dimension_semantics=("arbitrary",) * n_grid, vmem_limit_bytes=VMEM_LIMIT)


def _dot(a, b):
    return jnp.dot(a.astype(bf16), b.astype(bf16), preferred_element_type=f32)


def _dot_nt(a, b):
    return lax.dot_general(a.astype(bf16), b.astype(bf16), (((1,), (1,)), ((), ())), preferred_element_type=f32)


def _dot_tn(a, b):
    return lax.dot_general(a.astype(bf16), b.astype(bf16), (((0,), (0,)), ((), ())), preferred_element_type=f32)


def _dot_hi(a, b):
    return jnp.dot(a, b, precision=lax.Precision.HIGHEST, preferred_element_type=f32)


def _dot_h3(a, b):
    return jnp.dot(a, b, precision=lax.Precision.HIGH, preferred_element_type=f32)


@jax.custom_vjp
def _mm(a, b):
    return _dot(a, b)


@jax.custom_vjp
def _mm_nt(a, b):
    return _dot_nt(a, b)


@jax.custom_vjp
def _mm_tn(a, b):
    return _dot_tn(a, b)


_mm.defvjp(lambda a, b: (_dot(a, b), (a, b)), lambda r, g: (_mm_nt(g, r[1]), _mm_tn(r[0], g)))
_mm_nt.defvjp(lambda a, b: (_dot_nt(a, b), (a, b)), lambda r, g: (_mm(g, r[1]), _mm_tn(g, r[0])))
_mm_tn.defvjp(lambda a, b: (_dot_tn(a, b), (a, b)), lambda r, g: (_mm_nt(r[1], g), _mm(r[0], g)))


def _silu(x):
    return x * jax.nn.sigmoid(x)


def _gelu(x):
    return 0.5 * x * (1.0 + lax.erf(x * (2.0 ** -0.5)))


def _resident(shape):
    nd = len(shape)
    return pl.BlockSpec(shape, lambda *_: (0,) * nd, pipeline_mode=pl.Buffered(1))


def _tok(tile, width, nt=None, rev=False):
    if rev:
        return pl.BlockSpec((None, tile, width), lambda b, n: (b, nt - 1 - n, 0))
    return pl.BlockSpec((None, tile, width), lambda b, n: (b, n, 0))


def _per_batch(rows, width):
    return pl.BlockSpec((None, rows, width), lambda b, n: (b, 0, 0))


def _first_step():
    return jnp.logical_and(pl.program_id(0) == 0, pl.program_id(1) == 0)


ADA_SHARD = 3 * D_MODEL // 4


def ada_fwd(cc, w, b):
    def body(cc_ref, w_ref, b_ref, m_ref):
        s = _silu(cc_ref[...]).astype(bf16)
        for j in range(4):
            sl = slice(j * ADA_SHARD, (j + 1) * ADA_SHARD)
            m_ref[:, sl] = _dot(s, w_ref[j]) + b_ref[:, sl]

    return pl.pallas_call(body, name="ada_fwd", out_shape=SDS((8, 3 * D_MODEL), f32),
                          compiler_params=pltpu.CompilerParams(vmem_limit_bytes=VMEM_LIMIT))(cc, w, b)


def ada_bwd(cc, w, dm):
    def body(cc_ref, w_ref, dm_ref, dcc_ref, dw_ref, db_ref):
        s, vjp = jax.vjp(_silu, cc_ref[...])
        ds = jnp.zeros((8, D_MODEL), f32)
        for j in range(4):
            dmj = dm_ref[:, j * ADA_SHARD:(j + 1) * ADA_SHARD]
            ds = ds + _dot_nt(dmj, w_ref[j])
            dw_ref[j] = _dot_tn(s, dmj)
        dcc_ref[...] = vjp(ds)[0]
        db_ref[...] = jnp.sum(dm_ref[...], axis=0, keepdims=True)

    return pl.pallas_call(
        body, name="ada_bwd",
        out_shape=[SDS((8, D_MODEL), f32), SDS((4, D_MODEL, ADA_SHARD), f32), SDS((1, 3 * D_MODEL), f32)],
        compiler_params=pltpu.CompilerParams(vmem_limit_bytes=VMEM_LIMIT))(cc, w, dm)


N_GATE = 2 * N_DIR * GDN_HEADS
IN_WIDTHS = (D_S5, D_S5, 3 * D_GDN, D_GDN, LANES)
IN_OFFS = (0, 512, 1024, 2560, 3072)
IN_PAD = 3200


def in_proj_fwd(x, mod, w, *, name):
    B, L, _ = x.shape
    T = min(TOK_TILE, L)

    def body(x_ref, mod_ref, w_ref, *o_refs):
        h = (x_ref[...] * (1.0 + mod_ref[0:1, :]) + mod_ref[1:2, :]).astype(bf16)
        for o_ref, off, wd in zip(o_refs, IN_OFFS, IN_WIDTHS):
            r = _dot(h, w_ref[:, off:off + wd])
            o_ref[...] = r[:, :o_ref.shape[-1]]

    outw = (D_S5, D_S5, 3 * D_GDN, D_GDN, N_GATE)
    return pl.pallas_call(
        body, name=name, grid=(B, L // T),
        in_specs=[_tok(T, D_MODEL), _per_batch(2, D_MODEL), _resident((D_MODEL, IN_PAD))],
        out_specs=[_tok(T, wd) for wd in outw],
        out_shape=[SDS((B, L, wd), f32) for wd in outw],
        compiler_params=_cparams(2),
    )(x, mod, w)


def in_proj_bwd(x, mod, ds, w, gx_res, *, name):
    B, L, _ = x.shape
    T = min(TOK_TILE, L)
    with_dx = gx_res is not None

    def body(*refs):
        x_ref, mod_ref = refs[0], refs[1]
        d_refs = refs[2:7]
        w_ref = refs[7]
        k = 8
        if with_dx:
            gx_ref = refs[k]
            k += 1
        dw_ref, dmod_ref = refs[k], refs[k + 1]
        if with_dx:
            dx_ref = refs[k + 2]
        n = pl.program_id(1)

        @pl.when(_first_step())
        def _():
            dw_ref[...] = jnp.zeros_like(dw_ref)

        @pl.when(n == 0)
        def _():
            dmod_ref[...] = jnp.zeros_like(dmod_ref)

        xv = x_ref[...]
        scale1 = 1.0 + mod_ref[0:1, :]
        h = (xv * scale1 + mod_ref[1:2, :]).astype(bf16)
        dh = jnp.zeros((T, D_MODEL), f32)
        for d_ref, off, wd in zip(d_refs, IN_OFFS, IN_WIDTHS):
            dv = d_ref[...].astype(bf16)
            dh = dh + _dot_nt(dv, w_ref[:, off:off + wd])
            dw_ref[:, off:off + wd] += _dot_tn(h, dv)
        dmod_ref[0:1, :] += jnp.sum(dh * xv, axis=0, keepdims=True)
        dmod_ref[1:2, :] += jnp.sum(dh, axis=0, keepdims=True)
        if with_dx:
            dx_ref[...] = gx_ref[...] + dh * scale1

    in_specs = ([_tok(T, D_MODEL), _per_batch(2, D_MODEL)] + [_tok(T, wd) for wd in IN_WIDTHS]
                + [_resident((D_MODEL, IN_PAD))])
    args = [x, mod, *ds, w]
    out_specs = [_resident((D_MODEL, IN_PAD)), _per_batch(2, D_MODEL)]
    out_shape = [SDS((D_MODEL, IN_PAD), f32), SDS((B, 2, D_MODEL), f32)]
    if with_dx:
        in_specs.append(_tok(T, D_MODEL))
        args.append(gx_res)
        out_specs.append(_tok(T, D_MODEL))
        out_shape.append(SDS((B, L, D_MODEL), f32))
    return pl.pallas_call(body, name=name, grid=(B, L // T), in_specs=in_specs, out_specs=out_specs,
                          out_shape=out_shape, compiler_params=_cparams(2))(*args)


def _s5_zoh(lr, li, ldt, bre, bim, expand):
    dt = jnp.exp(ldt)
    zr, zi = lr * dt, li * dt
    e = jnp.exp(zr)
    ar, ai = e * jnp.cos(zi), e * jnp.sin(zi)
    den = lr * lr + li * li
    czr = ((ar - 1.0) * lr + ai * li) / den
    czi = (ai * lr - (ar - 1.0) * li) / den
    czr_e, czi_e = _dot_hi(czr, expand), _dot_hi(czi, expand)
    return ar, ai, czr_e * bre - czi_e * bim, czr_e * bim + czi_e * bre


_ZOH_OUT = [(N_DIR * S5_GROUPS, S5_STATE)] * 2 + [(N_DIR * S5_GROUPS, S5_STATE * S5_GROUP)] * 2


def s5_zoh_fwd(lr, li, ldt, bre, bim, expand):
    def body(lr_ref, li_ref, ldt_ref, bre_ref, bim_ref, e_ref, ar_ref, ai_ref, bbr_ref, bbi_ref):
        ar, ai, bbr, bbi = _s5_zoh(lr_ref[...], li_ref[...], ldt_ref[...], bre_ref[...], bim_ref[...], e_ref[...])
        ar_ref[...], ai_ref[...], bbr_ref[...], bbi_ref[...] = ar, ai, bbr, bbi

    return pl.pallas_call(body, name="s5_zoh_fwd", out_shape=[SDS(s, f32) for s in _ZOH_OUT])(
        lr, li, ldt, bre, bim, expand)


def s5_zoh_bwd(lr, li, ldt, bre, bim, expand, dar, dai, dbbr, dbbi):
    def body(lr_ref, li_ref, ldt_ref, bre_ref, bim_ref, e_ref, dar_ref, dai_ref, dbbr_ref, dbbi_ref,
             dlr_ref, dli_ref, dldt_ref, dbre_ref, dbim_ref):
        ev = e_ref[...]
        _, vjp = jax.vjp(lambda a, b, c, d, e: _s5_zoh(a, b, c, d, e, ev),
                         lr_ref[...], li_ref[...], ldt_ref[...], bre_ref[...], bim_ref[...])
        outs = vjp((dar_ref[...], dai_ref[...], dbbr_ref[...], dbbi_ref[...]))
        dlr_ref[...], dli_ref[...], dldt_ref[...], dbre_ref[...], dbim_ref[...] = outs

    shapes = [lr.shape, li.shape, ldt.shape, bre.shape, bim.shape]
    return pl.pallas_call(body, name="s5_zoh_bwd", out_shape=[SDS(s, f32) for s in shapes])(
        lr, li, ldt, bre, bim, expand, dar, dai, dbbr, dbbi)


def _scan_rows(T, rev, ar, ai, h0s, refs, off):
    def step(i, carry):
        t = off + ((T - 1 - i) if rev else i)
        out = []
        for (hr, hi), (r_ref, i_ref) in zip(carry, refs):
            nr = ar * hr - ai * hi + r_ref[pl.ds(t, 1), :]
            ni = ar * hi + ai * hr + i_ref[pl.ds(t, 1), :]
            r_ref[pl.ds(t, 1), :] = nr
            i_ref[pl.ds(t, 1), :] = ni
            out.append((nr, ni))
        return tuple(out)

    return lax.fori_loop(0, T, step, tuple(h0s))


S5_BLOCKS = 4
S5_BC = D_S5 // S5_BLOCKS
S5_BS = S5_HALF // S5_BLOCKS


def _s5_in(uv, bre_ref, bim_ref, hr_ref, hi_ref, off, T):
    for jb in range(S5_BLOCKS):
        uj = uv[:, jb * S5_BC:(jb + 1) * S5_BC]
        hr_ref[off:off + T, jb * S5_BS:(jb + 1) * S5_BS] = _dot(uj, bre_ref[jb])
        hi_ref[off:off + T, jb * S5_BS:(jb + 1) * S5_BS] = _dot(uj, bim_ref[jb])


def _s5_specs(B, T, nt, rev):
    tidx = (lambda n: nt - 1 - n) if rev else (lambda n: n)
    tok = pl.BlockSpec((B, T, D_S5), lambda n: (0, tidx(n), 0))
    hin = pl.BlockSpec((B, None, 2, S5_HALF), lambda n: (0, tidx(n), 0, 0))
    state = pl.BlockSpec((B, 2, S5_HALF), lambda n: (0, 0, 0))
    return tok, hin, state


def s5_scan_fwd(u, bre, bim, ctop, cbot, arow, h0, *, d, need_y, name):
    B, L, _ = u.shape
    T = min(S5_TILE, L)
    nt = L // T
    rev = d == 1

    def body(u_ref, bre_ref, bim_ref, ct_ref, cb_ref, a_ref, h0_ref, *rest):
        if need_y:
            y_ref, hin_ref, hend_ref, hr_scr, hi_scr, h_scr = rest
        else:
            hin_ref, hend_ref, hr_scr, hi_scr, h_scr = rest
        n = pl.program_id(0)

        @pl.when(n == 0)
        def _():
            h_scr[...] = h0_ref[...]

        hin_ref[...] = h_scr[...]
        for b in range(B):
            _s5_in(u_ref[b].astype(bf16), bre_ref, bim_ref, hr_scr.at[b], hi_scr.at[b], 0, T)
        hs = _scan_rows(T, rev, a_ref[0:1, :], a_ref[1:2, :], [(h_scr[b, 0:1, :], h_scr[b, 1:2, :]) for b in range(B)],
                        [(hr_scr.at[b], hi_scr.at[b]) for b in range(B)], 0)
        for b in range(B):
            h_scr[b, 0:1, :] = hs[b][0]
            h_scr[b, 1:2, :] = hs[b][1]
            if need_y:
                for jb in range(S5_BLOCKS):
                    st = slice(jb * S5_BS, (jb + 1) * S5_BS)
                    y_ref[b, :, jb * S5_BC:(jb + 1) * S5_BC] = (_dot(hr_scr[b, :, st], ct_ref[jb])
                                                                 + _dot(hi_scr[b, :, st], cb_ref[jb]))

        @pl.when(n == nt - 1)
        def _():
            hend_ref[...] = h_scr[...]

    tok, hin_spec, state = _s5_specs(B, T, nt, rev)
    out_specs = [hin_spec, state]
    out_shape = [SDS((B, nt, 2, S5_HALF), f32), SDS((B, 2, S5_HALF), f32)]
    if need_y:
        out_specs.insert(0, tok)
        out_shape.insert(0, SDS((B, L, D_S5), f32))
    w_in, w_out = _resident((S5_BLOCKS, S5_BC, S5_BS)), _resident((S5_BLOCKS, S5_BS, S5_BC))
    return pl.pallas_call(
        body, name=name, grid=(nt,),
        in_specs=[tok, w_in, w_in, w_out, w_out, _resident((2, S5_HALF)), state],
        out_specs=out_specs, out_shape=out_shape,
        scratch_shapes=[pltpu.VMEM((B, T, S5_HALF), f32), pltpu.VMEM((B, T, S5_HALF), f32),
                        pltpu.VMEM((B, 2, S5_HALF), f32)],
        compiler_params=_cparams(1),
    )(u, bre, bim, ctop, cbot, arow, h0)


def s5_scan_bwd(u, dy, bre, bim, ctop, cbot, arow, hin, dhend, *, d, name):
    B, L, _ = u.shape
    T = min(S5_TILE, L)
    nt = L // T
    rev = d == 1
    has_dy = dy is not None
    PAD = 8

    def body(*refs):
        u_ref = refs[0]
        k = 1
        if has_dy:
            dy_ref = refs[1]
            k = 2
        bre_ref, bim_ref, ct_ref, cb_ref, a_ref, hin_ref, dhend_ref = refs[k:k + 7]
        du_ref, dbre_ref, dbim_ref, dct_ref, dcb_ref, da_ref, dh0_ref = refs[k + 7:k + 14]
        hr_scr, hi_scr, gr_scr, gi_scr, p_scr = refs[k + 14:]
        n = pl.program_id(0)

        @pl.when(n == 0)
        def _():
            for r in (dbre_ref, dbim_ref, dct_ref, dcb_ref, da_ref):
                r[...] = jnp.zeros_like(r)
            p_scr[...] = dhend_ref[...]

        ar, ai = a_ref[0:1, :], a_ref[1:2, :]
        prev_row = PAD + T if rev else PAD - 1
        uvs = []
        for b in range(B):
            uvs.append(u_ref[b].astype(bf16))
            _s5_in(uvs[b], bre_ref, bim_ref, hr_scr.at[b], hi_scr.at[b], PAD, T)
            hr_scr[b, prev_row:prev_row + 1, :] = hin_ref[b, 0:1, :]
            hi_scr[b, prev_row:prev_row + 1, :] = hin_ref[b, 1:2, :]
        _scan_rows(T, rev, ar, ai, [(hin_ref[b, 0:1, :], hin_ref[b, 1:2, :]) for b in range(B)],
                   [(hr_scr.at[b], hi_scr.at[b]) for b in range(B)], PAD)
        if has_dy:
            for b in range(B):
                dyv = dy_ref[b].astype(bf16)
                for jb in range(S5_BLOCKS):
                    st = slice(jb * S5_BS, (jb + 1) * S5_BS)
                    dyj = dyv[:, jb * S5_BC:(jb + 1) * S5_BC]
                    gr_scr[b, :, st] = _dot_nt(dyj, ct_ref[jb])
                    gi_scr[b, :, st] = _dot_nt(dyj, cb_ref[jb])
                    dct_ref[jb] += _dot_tn(hr_scr[b, PAD:PAD + T, st], dyj)
                    dcb_ref[jb] += _dot_tn(hi_scr[b, PAD:PAD + T, st], dyj)
        else:
            gr_scr[...] = jnp.zeros_like(gr_scr)
            gi_scr[...] = jnp.zeros_like(gi_scr)

        def step(i, carry):
            t = i if rev else T - 1 - i
            tp = PAD + t + (1 if rev else -1)
            out = []
            for b, (pr, pi, dar, dai) in enumerate(carry):
                gr = gr_scr[b, pl.ds(t, 1), :] + pr
                gi = gi_scr[b, pl.ds(t, 1), :] + pi
                gr_scr[b, pl.ds(t, 1), :] = gr
                gi_scr[b, pl.ds(t, 1), :] = gi
                hpr = hr_scr[b, pl.ds(tp, 1), :]
                hpi = hi_scr[b, pl.ds(tp, 1), :]
                out.append((ar * gr + ai * gi, ar * gi - ai * gr, dar + hpr * gr + hpi * gi, dai + hpr * gi - hpi * gr))
            return tuple(out)

        zero = jnp.zeros((1, S5_HALF), f32)
        res = lax.fori_loop(0, T, step, tuple((p_scr[b, 0:1, :], p_scr[b, 1:2, :], zero, zero) for b in range(B)))
        for b in range(B):
            pr, pi, dar, dai = res[b]
            p_scr[b, 0:1, :] = pr
            p_scr[b, 1:2, :] = pi
            da_ref[0:1, :] += dar
            da_ref[1:2, :] += dai
            for jb in range(S5_BLOCKS):
                st = slice(jb * S5_BS, (jb + 1) * S5_BS)
                ch = slice(jb * S5_BC, (jb + 1) * S5_BC)
                gr_j = gr_scr[b, :, st].astype(bf16)
                gi_j = gi_scr[b, :, st].astype(bf16)
                du_ref[b, :, ch] = _dot_nt(gr_j, bre_ref[jb]) + _dot_nt(gi_j, bim_ref[jb])
                dbre_ref[jb] += _dot_tn(uvs[b][:, ch], gr_j)
                dbim_ref[jb] += _dot_tn(uvs[b][:, ch], gi_j)

        @pl.when(n == nt - 1)
        def _():
            dh0_ref[...] = p_scr[...]

    tok, hin_spec, state = _s5_specs(B, T, nt, not rev)
    w_in, w_out = _resident((S5_BLOCKS, S5_BC, S5_BS)), _resident((S5_BLOCKS, S5_BS, S5_BC))
    wspecs = [w_in, w_in, w_out, w_out]
    in_specs = [tok] + ([tok] if has_dy else []) + wspecs + [_resident((2, S5_HALF)), hin_spec, state]
    args = [u] + ([dy] if has_dy else []) + [bre, bim, ctop, cbot, arow, hin, dhend]
    return pl.pallas_call(
        body, name=name, grid=(nt,), in_specs=in_specs,
        out_specs=[tok] + wspecs + [_resident((2, S5_HALF)), state],
        out_shape=[SDS((B, L, D_S5), f32), SDS((S5_BLOCKS, S5_BC, S5_BS), f32), SDS((S5_BLOCKS, S5_BC, S5_BS), f32),
                   SDS((S5_BLOCKS, S5_BS, S5_BC), f32), SDS((S5_BLOCKS, S5_BS, S5_BC), f32), SDS((2, S5_HALF), f32),
                   SDS((B, 2, S5_HALF), f32)],
        scratch_shapes=[pltpu.VMEM((B, T + 2 * PAD, S5_HALF), f32), pltpu.VMEM((B, T + 2 * PAD, S5_HALF), f32),
                        pltpu.VMEM((B, T, S5_HALF), f32), pltpu.VMEM((B, T, S5_HALF), f32),
                        pltpu.VMEM((B, 2, S5_HALF), f32)],
        compiler_params=_cparams(1),
    )(*args)


def _glu_fn(u, y0, y1, z, dsk, wg, bg):
    g = _gelu(dsk * u + y0 + y1)
    return g * jax.nn.sigmoid(_mm(g, wg) + bg) * _silu(z)


def s5_glu_fwd(u, y0, y1, z, dsk, wg, bg):
    B, L, _ = u.shape
    T = min(TOK_TILE, L)

    def body(u_ref, y0_ref, y1_ref, z_ref, dsk_ref, wg_ref, bg_ref, o_ref):
        o_ref[...] = _glu_fn(u_ref[...], y0_ref[...], y1_ref[...], z_ref[...], dsk_ref[...], wg_ref[...].astype(f32),
                             bg_ref[...])

    t = _tok(T, D_S5)
    return pl.pallas_call(
        body, name="s5_glu_fwd", grid=(B, L // T),
        in_specs=[t, t, t, t, _resident((1, D_S5)), _resident((D_S5, D_S5)), _resident((1, D_S5))],
        out_specs=t, out_shape=SDS((B, L, D_S5), f32), compiler_params=_cparams(2),
    )(u, y0, y1, z, dsk, wg, bg)


def s5_glu_bwd(u, y0, y1, z, dsk, wg, bg, dout):
    B, L, _ = u.shape
    T = min(TOK_TILE, L)

    def body(u_ref, y0_ref, y1_ref, z_ref, dsk_ref, wg_ref, bg_ref, do_ref, du_ref, dy_ref, dz_ref,
             ddsk_ref, dwg_ref, dbg_ref):
        @pl.when(_first_step())
        def _():
            for r in (ddsk_ref, dwg_ref, dbg_ref):
                r[...] = jnp.zeros_like(r)

        _, vjp = jax.vjp(_glu_fn, u_ref[...], y0_ref[...], y1_ref[...], z_ref[...], dsk_ref[...],
                         wg_ref[...].astype(f32), bg_ref[...])
        du, dy, _, dz, ddsk, dwg, dbg = vjp(do_ref[...])
        du_ref[...], dy_ref[...], dz_ref[...] = du, dy, dz
        ddsk_ref[...] += ddsk
        dwg_ref[...] += dwg
        dbg_ref[...] += dbg

    t = _tok(T, D_S5)
    small = [_resident((1, D_S5)), _resident((D_S5, D_S5)), _resident((1, D_S5))]
    return pl.pallas_call(
        body, name="s5_glu_bwd", grid=(B, L // T),
        in_specs=[t, t, t, t] + small + [t], out_specs=[t, t, t] + small,
        out_shape=[SDS((B, L, D_S5), f32)] * 3 + [SDS((1, D_S5), f32), SDS((D_S5, D_S5), f32), SDS((1, D_S5), f32)],
        compiler_params=_cparams(2),
    )(u, y0, y1, z, dsk, wg, bg, dout)


CONV_ROWS = 16


def _conv_taps(L, is_ctx):
    t = lax.broadcasted_iota(jnp.int32, (L, 1), 0)
    taps = []
    for di in ((1,) if is_ctx else (0, 1, 2)):
        for dj in (0, 1, 2):
            s = (0 if is_ctx else GRID_W * (di - 1)) + (dj - 1)
            if is_ctx:
                ok = jnp.logical_and(t + s >= 0, t + s < L)
            else:
                col = jnp.bitwise_and(t, GRID_W - 1) + (dj - 1)
                row = t + GRID_W * (di - 1)
                ok = jnp.logical_and(jnp.logical_and(col >= 0, col < GRID_W), jnp.logical_and(row >= 0, row < L))
            taps.append((di * 3 + dj, s, ok.astype(f32)))
    return taps


def _shift(x, s):
    L = x.shape[0]
    k = (-s) % L
    return x if k == 0 else pltpu.roll(x, k, axis=0)


def _qk_post(pre, is_norm, scale):
    s = _silu(pre)
    nrm = lax.rsqrt(jnp.sum(s * s, axis=-1, keepdims=True) + NORM_EPS)
    return s * jnp.where(is_norm, nrm * scale, 1.0)


def _conv_kind():
    ct = pl.program_id(1)
    return ct < 2 * GDN_HEADS, jnp.where(ct < GDN_HEADS, GDN_HEAD ** -0.5, 1.0).astype(f32)


def _conv_pre(xv, w_ref, taps):
    pre = jnp.zeros_like(xv)
    for r, s, m in taps:
        pre = pre + w_ref[r:r + 1, :] * (m * _shift(xv, s))
    return pre


def conv_fwd(qkv, w16, *, is_ctx, name):
    B, L, C = qkv.shape
    spec = pl.BlockSpec((None, L, GDN_HEAD), lambda b, ct: (b, 0, ct))
    wspec = pl.BlockSpec((CONV_ROWS, GDN_HEAD), lambda b, ct: (0, ct))

    def body(x_ref, w_ref, o_ref):
        is_norm, scale = _conv_kind()
        o_ref[...] = _qk_post(_conv_pre(x_ref[...], w_ref, _conv_taps(L, is_ctx)), is_norm, scale)

    return pl.pallas_call(body, name=name, grid=(B, C // GDN_HEAD), in_specs=[spec, wspec], out_specs=spec,
                          out_shape=SDS((B, L, C), f32), compiler_params=_cparams(2))(qkv, w16)


def conv_bwd(qkv, w16, da0, da1, *, is_ctx, name):
    B, L, C = qkv.shape
    spec = pl.BlockSpec((None, L, GDN_HEAD), lambda b, ct: (b, 0, ct))
    wspec = pl.BlockSpec((CONV_ROWS, GDN_HEAD), lambda b, ct: (0, ct))
    dwspec = pl.BlockSpec((None, CONV_ROWS, GDN_HEAD), lambda b, ct: (b, 0, ct))

    def body(x_ref, w_ref, d0_ref, d1_ref, dx_ref, dw_ref):
        is_norm, scale = _conv_kind()
        taps = _conv_taps(L, is_ctx)
        xv = x_ref[...]
        _, vjp = jax.vjp(lambda p: _qk_post(p, is_norm, scale), _conv_pre(xv, w_ref, taps))
        dpre = vjp(d0_ref[...] + d1_ref[...])[0]
        dx = jnp.zeros_like(xv)
        dw_ref[...] = jnp.zeros_like(dw_ref)
        for r, s, m in taps:
            md = m * dpre
            dx = dx + _shift(w_ref[r:r + 1, :] * md, -s)
            dw_ref[r:r + 1, :] = jnp.sum(md * _shift(xv, s), axis=0, keepdims=True)
        dx_ref[...] = dx

    return pl.pallas_call(body, name=name, grid=(B, C // GDN_HEAD), in_specs=[spec, wspec, spec, spec],
                          out_specs=[spec, dwspec], out_shape=[SDS((B, L, C), f32), SDS((B, CONV_ROWS, C), f32)],
                          compiler_params=_cparams(2))(qkv, w16, da0, da1)


def _gates_fn(ba, alog, dtb):
    T = ba.shape[0]
    lane = lax.broadcasted_iota(jnp.int32, ba.shape, 1)
    ii = lax.broadcasted_iota(jnp.int32, (T, T), 0)
    jj = lax.broadcasted_iota(jnp.int32, (T, T), 1)
    same = jnp.right_shift(ii, 6) == jnp.right_shift(jj, 6)
    lmat = jnp.logical_and(same, ii >= jj).astype(f32)
    umat = jnp.logical_and(same, ii <= jj).astype(f32)
    g = jnp.where(lane >= 8, -jnp.exp(alog) * jax.nn.softplus(ba + dtb), 0.0)
    gc = jnp.where(lane >= 12, _dot_hi(umat, g), _dot_hi(lmat, g))
    return jnp.where(lane < 8, jax.nn.sigmoid(ba), gc)


def gates_fwd(ba, alog, dtb, *, name):
    B, L, _ = ba.shape
    T = min(TOK_TILE, L)
    t = _tok(T, N_GATE)

    def body(ba_ref, al_ref, dt_ref, o_ref):
        o_ref[...] = _gates_fn(ba_ref[...], al_ref[...], dt_ref[...])

    return pl.pallas_call(body, name=name, grid=(B, L // T),
                          in_specs=[t, _resident((1, N_GATE)), _resident((1, N_GATE))], out_specs=t,
                          out_shape=SDS((B, L, N_GATE), f32), compiler_params=_cparams(2))(ba, alog, dtb)


def gates_bwd(ba, alog, dtb, dbg, *, name):
    B, L, _ = ba.shape
    T = min(TOK_TILE, L)
    t = _tok(T, N_GATE)
    small = _resident((1, N_GATE))

    def body(ba_ref, al_ref, dt_ref, d_ref, dba_ref, dal_ref, ddt_ref):
        @pl.when(_first_step())
        def _():
            dal_ref[...] = jnp.zeros_like(dal_ref)
            ddt_ref[...] = jnp.zeros_like(ddt_ref)

        _, vjp = jax.vjp(_gates_fn, ba_ref[...], al_ref[...], dt_ref[...])
        dba, dal, ddt = vjp(d_ref[...])
        dba_ref[...] = dba
        dal_ref[...] += dal
        ddt_ref[...] += ddt

    return pl.pallas_call(body, name=name, grid=(B, L // T), in_specs=[t, small, small, t],
                          out_specs=[t, small, small],
                          out_shape=[SDS((B, L, N_GATE), f32), SDS((1, N_GATE), f32), SDS((1, N_GATE), f32)],
                          compiler_params=_cparams(2))(ba, alog, dtb, dbg)


@jax.custom_vjp
def _inv_unit_tri(mats):
    n = mats[0].shape[0]
    eye = (lax.broadcasted_iota(jnp.int32, (n, n), 0) == lax.broadcasted_iota(jnp.int32, (n, n), 1)).astype(f32)
    xs = [eye - a for a in mats]
    sq = [_dot(a, a) for a in mats]
    ps = sq
    k = 2
    while k < n:
        xs = [x + _dot(x, p) for x, p in zip(xs, ps)]
        k *= 2
        if k < n:
            ps = [_dot(p, p) for p in ps]
    return tuple(_dot(p, x) - a for p, x, a in zip(sq, xs, mats))


def _inv_unit_tri_fwd(mats):
    ns = _inv_unit_tri(mats)
    return ns, ns


def _inv_unit_tri_bwd(ns, dns):
    ys = [dn + _dot_tn(nn, dn) for nn, dn in zip(ns, dns)]
    return (tuple(-(y + _dot_nt(y, nn)) for y, nn in zip(ys, ns)),)


_inv_unit_tri.defvjp(_inv_unit_tri_fwd, _inv_unit_tri_bwd)


def _gdn_chunk(heads, *, revs):
    n = heads[0][0].shape[0]
    ii = lax.broadcasted_iota(jnp.int32, (n, n), 0)
    jj = lax.broadcasted_iota(jnp.int32, (n, n), 1)
    row = lax.broadcasted_iota(jnp.int32, (n, 1), 0)
    lower = {False: ii >= jj, True: ii <= jj}
    strict = {False: ii > jj, True: ii < jj}
    last = {False: n - 1, True: 0}
    H = range(len(heads))
    q, k, v, beta, gc, gr, s = (list(t) for t in zip(*heads))
    decay = [jnp.where(lower[revs[h]], jnp.exp(jnp.where(lower[revs[h]], gc[h] - gr[h], 0.0)), 0.0) for h in H]
    kk = [_mm_nt(k[h], k[h]) for h in H]
    qk = [_mm_nt(q[h], k[h]) * decay[h] for h in H]
    qs = [_mm(q[h], s[h]) for h in H]
    a_mat = tuple(jnp.where(strict[revs[h]], beta[h] * kk[h] * decay[h], 0.0) for h in H)
    gamma = [jnp.exp(gc[h]) for h in H]
    g_last = [jnp.sum(jnp.where(row == last[revs[h]], gc[h], 0.0), axis=0, keepdims=True) for h in H]
    nmat = _inv_unit_tri(a_mat)
    bv = [beta[h] * v[h] for h in H]
    bk = [(beta[h] * gamma[h]) * k[h] for h in H]
    u0 = [bv[h] + _mm(nmat[h], bv[h]) for h in H]
    w = [bk[h] + _mm(nmat[h], bk[h]) for h in H]
    k_out = [k[h] * jnp.exp(g_last[h] - gc[h]) for h in H]
    u = [u0[h] - _mm(w[h], s[h]) for h in H]
    o = [gamma[h] * qs[h] + _mm(qk[h], u[h]) for h in H]
    s_new = [jnp.exp(g_last[h]) * s[h] + _mm_tn(k_out[h], u[h]) for h in H]
    return tuple((o[h], s_new[h]) for h in H)


def _gdn_specs(B, nc, rev):
    def cidx(n):
        return (nc - 1 - n) if rev else n
    tok = lambda width: pl.BlockSpec((B, CHUNK, width), lambda n: (0, cidx(n), 0))
    rowspec = pl.BlockSpec((B, None, N_GATE, CHUNK), lambda n: (0, cidx(n), 0, 0))
    st = pl.BlockSpec((B, GDN_HEADS, GDN_HEAD, GDN_HEAD), lambda n: (0, 0, 0, 0))
    ck = pl.BlockSpec((B, None, GDN_HEADS, GDN_HEAD, GDN_HEAD), lambda n: (0, cidx(n), 0, 0, 0))
    return tok, rowspec, st, ck


def _gdn_head_args(qkv_ref, bg_ref, bgr_ref, b, d, h):
    col = d * GDN_HEADS + h
    q = qkv_ref[b, :, h * GDN_HEAD:(h + 1) * GDN_HEAD]
    k = qkv_ref[b, :, D_GDN + h * GDN_HEAD:D_GDN + (h + 1) * GDN_HEAD]
    v = qkv_ref[b, :, 2 * D_GDN + h * GDN_HEAD:2 * D_GDN + (h + 1) * GDN_HEAD]
    bgv = bg_ref[b]
    return q, k, v, bgv[:, col:col + 1], bgv[:, 8 + col:9 + col], bgr_ref[b][8 + col:9 + col, :]


def _gdn_chains(B):
    return [(d, b, h) for d in range(N_DIR) for b in range(B) for h in range(GDN_HEADS)]


def gdn_fwd(qkv, bg, bgr, s0s, *, need_o, name):
    B, L, _ = qkv.shape
    nc = L // CHUNK
    specs = [_gdn_specs(B, nc, d == 1) for d in range(N_DIR)]
    chains = _gdn_chains(B)
    state_shape = (B, GDN_HEADS, GDN_HEAD, GDN_HEAD)

    def body(*refs):
        ins = [refs[3 * d:3 * d + 3] for d in range(N_DIR)]
        s0_refs = refs[6:8]
        k = 8
        o_refs = refs[k:k + 2] if need_o else None
        k += 2 if need_o else 0
        ck_refs, sf_refs, s_scrs = refs[k:k + 2], refs[k + 2:k + 4], refs[k + 4:k + 6]
        n = pl.program_id(0)

        @pl.when(n == 0)
        def _():
            for d in range(N_DIR):
                s_scrs[d][...] = s0_refs[d][...]

        for d in range(N_DIR):
            ck_refs[d][...] = s_scrs[d][...]
        heads = tuple(_gdn_head_args(*ins[d], b, d, h) + (s_scrs[d][b, h],) for d, b, h in chains)
        outs = _gdn_chunk(heads, revs=tuple(d == 1 for d, _, _ in chains))
        for (d, b, h), (o, s_new) in zip(chains, outs):
            if need_o:
                o_refs[d][b, :, h * GDN_HEAD:(h + 1) * GDN_HEAD] = o
            s_scrs[d][b, h] = s_new

        @pl.when(n == nc - 1)
        def _():
            for d in range(N_DIR):
                sf_refs[d][...] = s_scrs[d][...]

    in_specs, out_o, out_ck, out_sf = [], [], [], []
    for tok, rowspec, st, ck in specs:
        in_specs += [tok(3 * D_GDN), tok(N_GATE), rowspec]
        out_o.append(tok(D_GDN))
        out_ck.append(ck)
        out_sf.append(st)
    in_specs += [specs[0][2]] * 2
    out_specs = (out_o if need_o else []) + out_ck + out_sf
    out_shape = (([SDS((B, L, D_GDN), f32)] * 2 if need_o else [])
                 + [SDS((B, nc) + state_shape[1:], f32)] * 2 + [SDS(state_shape, f32)] * 2)
    res = pl.pallas_call(
        body, name=name, grid=(nc,), in_specs=in_specs, out_specs=out_specs, out_shape=out_shape,
        scratch_shapes=[pltpu.VMEM(state_shape, f32)] * 2, compiler_params=_cparams(1),
    )(qkv, bg, bgr, qkv, bg, bgr, *s0s)
    if need_o:
        return res[0:2], res[2:4], res[4:6]
    return res[0:2], res[2:4]


def gdn_bwd(qkv, bg, bgr, cks, do, dsfs, *, name):
    B, L, _ = qkv.shape
    nc = L // CHUNK
    has_do = do is not None
    specs = [_gdn_specs(B, nc, d != 1) for d in range(N_DIR)]
    chains = _gdn_chains(B)
    state_shape = (B, GDN_HEADS, GDN_HEAD, GDN_HEAD)
    per_dir = 5 if has_do else 4

    def body(*refs):
        ins = [refs[per_dir * d:per_dir * d + per_dir] for d in range(N_DIR)]
        k = per_dir * N_DIR
        dsf_refs = refs[k:k + 2]
        outs = [refs[k + 2 + 3 * d:k + 5 + 3 * d] for d in range(N_DIR)]
        ds0_refs, ds_scrs = refs[k + 8:k + 10], refs[k + 10:k + 12]
        n = pl.program_id(0)

        @pl.when(n == 0)
        def _():
            for d in range(N_DIR):
                ds_scrs[d][...] = dsf_refs[d][...]

        lane = lax.broadcasted_iota(jnp.int32, (CHUNK, N_GATE), 1)
        sub = lax.broadcasted_iota(jnp.int32, (N_GATE, CHUNK), 0)
        heads = tuple(_gdn_head_args(*ins[d][:3], b, d, h) + (ins[d][3][b, h],) for d, b, h in chains)
        _, vjp = jax.vjp(functools.partial(_gdn_chunk, revs=tuple(d == 1 for d, _, _ in chains)), heads)
        zero = jnp.zeros((CHUNK, GDN_HEAD), f32)
        cts = tuple(((ins[d][4][b, :, h * GDN_HEAD:(h + 1) * GDN_HEAD] if has_do else zero), ds_scrs[d][b, h])
                    for d, b, h in chains)
        (dheads,) = vjp(cts)
        dbg_acc = [[jnp.zeros((CHUNK, N_GATE), f32) for _ in range(B)] for _ in range(N_DIR)]
        dbgr_acc = [[jnp.zeros((N_GATE, CHUNK), f32) for _ in range(B)] for _ in range(N_DIR)]
        for (d, b, h), (dq, dk, dv, db, dgc, dgr, ds) in zip(chains, dheads):
            col = d * GDN_HEADS + h
            dqkv_ref = outs[d][0]
            dqkv_ref[b, :, h * GDN_HEAD:(h + 1) * GDN_HEAD] = dq
            dqkv_ref[b, :, D_GDN + h * GDN_HEAD:D_GDN + (h + 1) * GDN_HEAD] = dk
            dqkv_ref[b, :, 2 * D_GDN + h * GDN_HEAD:2 * D_GDN + (h + 1) * GDN_HEAD] = dv
            dbg_acc[d][b] = dbg_acc[d][b] + jnp.where(lane == col, db, 0.0) + jnp.where(lane == 8 + col, dgc, 0.0)
            dbgr_acc[d][b] = dbgr_acc[d][b] + jnp.where(sub == 8 + col, dgr, 0.0)
            ds_scrs[d][b, h] = ds
        for d in range(N_DIR):
            for b in range(B):
                outs[d][1][b] = dbg_acc[d][b]
                outs[d][2][b] = dbgr_acc[d][b]

        @pl.when(n == nc - 1)
        def _():
            for d in range(N_DIR):
                ds0_refs[d][...] = ds_scrs[d][...]

    in_specs, args, out_specs, out_shape = [], [], [], []
    for d, (tok, rowspec, st, ck) in enumerate(specs):
        in_specs += [tok(3 * D_GDN), tok(N_GATE), rowspec, ck] + ([tok(D_GDN)] if has_do else [])
        args += [qkv, bg, bgr, cks[d]] + ([do] if has_do else [])
        out_specs += [tok(3 * D_GDN), tok(N_GATE), rowspec]
        out_shape += [SDS((B, L, 3 * D_GDN), f32), SDS((B, L, N_GATE), f32), SDS((B, nc, N_GATE, CHUNK), f32)]
    st = specs[0][2]
    in_specs += [st, st]
    args += list(dsfs)
    out_specs += [st, st]
    out_shape += [SDS(state_shape, f32)] * 2
    res = pl.pallas_call(
        body, name=name, grid=(nc,), in_specs=in_specs, out_specs=out_specs, out_shape=out_shape,
        scratch_shapes=[pltpu.VMEM(state_shape, f32)] * 2, compiler_params=_cparams(1),
    )(*args)
    return (res[0], res[3]), (res[1], res[4]), (res[2], res[5]), (res[6], res[7])


def _gnorm_fn(o0, o1, z, w):
    o = o0 + o1
    return o * lax.rsqrt(jnp.mean(o * o, axis=-1, keepdims=True) + NORM_EPS) * w * _silu(z)


def gnorm_fwd(o0, o1, z, w):
    B, L, _ = o0.shape
    T = min(TOK_TILE, L)
    t = _tok(T, D_GDN)

    def body(o0_ref, o1_ref, z_ref, w_ref, out_ref):
        for h in range(GDN_HEADS):
            sl = slice(h * GDN_HEAD, (h + 1) * GDN_HEAD)
            out_ref[:, sl] = _gnorm_fn(o0_ref[:, sl], o1_ref[:, sl], z_ref[:, sl], w_ref[...])

    return pl.pallas_call(body, name="gnorm_fwd", grid=(B, L // T), in_specs=[t, t, t, _resident((1, GDN_HEAD))],
                          out_specs=t, out_shape=SDS((B, L, D_GDN), f32), compiler_params=_cparams(2))(o0, o1, z, w)


def gnorm_bwd(o0, o1, z, w, dout):
    B, L, _ = o0.shape
    T = min(TOK_TILE, L)
    t = _tok(T, D_GDN)

    def body(o0_ref, o1_ref, z_ref, w_ref, d_ref, do_ref, dz_ref, dw_ref):
        @pl.when(_first_step())
        def _():
            dw_ref[...] = jnp.zeros_like(dw_ref)

        for h in range(GDN_HEADS):
            sl = slice(h * GDN_HEAD, (h + 1) * GDN_HEAD)
            _, vjp = jax.vjp(_gnorm_fn, o0_ref[:, sl], o1_ref[:, sl], z_ref[:, sl], w_ref[...])
            do, _, dz, dw = vjp(d_ref[:, sl])
            do_ref[:, sl] = do
            dz_ref[:, sl] = dz
            dw_ref[...] += dw

    return pl.pallas_call(body, name="gnorm_bwd", grid=(B, L // T),
                          in_specs=[t, t, t, _resident((1, GDN_HEAD)), t], out_specs=[t, t, _resident((1, GDN_HEAD))],
                          out_shape=[SDS((B, L, D_GDN), f32), SDS((B, L, D_GDN), f32), SDS((1, GDN_HEAD), f32)],
                          compiler_params=_cparams(2))(o0, o1, z, w, dout)


def _head_loss(y, x, gate, lng, lnb, tgt):
    r = DEEPNORM_ALPHA * x + gate * y
    mu = jnp.mean(r, axis=-1, keepdims=True)
    rc = r - mu
    var = jnp.mean(rc * rc, axis=-1, keepdims=True)
    err = rc * lax.rsqrt(var + LN_EPS) * lng + lnb - tgt
    return (0.5 / D_MODEL) * jnp.sum(jnp.sum(err * err, axis=-1, keepdims=True), axis=0, keepdims=True)


def head_fwd_bwd(s5o, gdo, x, tgt, gate, lng, lnb, ws, wg):
    B, L, _ = x.shape
    T = min(TOK_TILE, L)

    def body(s_ref, g_ref, x_ref, t_ref, gate_ref, lng_ref, lnb_ref, ws_ref, wg_ref,
             loss_ref, ds_ref, dg_ref, gx_ref, dws_ref, dwg_ref, dgate_ref, dlng_ref, dlnb_ref):
        n = pl.program_id(1)

        @pl.when(_first_step())
        def _():
            for r in (dws_ref, dwg_ref, dlng_ref, dlnb_ref):
                r[...] = jnp.zeros_like(r)

        @pl.when(n == 0)
        def _():
            loss_ref[...] = jnp.zeros_like(loss_ref)
            dgate_ref[...] = jnp.zeros_like(dgate_ref)

        sv = s_ref[...].astype(bf16)
        gv = g_ref[...].astype(bf16)
        y = _dot(sv, ws_ref[...]) + _dot(gv, wg_ref[...])
        loss, vjp = jax.vjp(lambda *a: _head_loss(*a, t_ref[...]), y, x_ref[...], gate_ref[...], lng_ref[...],
                            lnb_ref[...])
        dy, dx, dgate, dlng, dlnb = vjp(jnp.ones((1, 1), f32))
        loss_ref[...] += jnp.broadcast_to(loss, loss_ref.shape)
        dyb = dy.astype(bf16)
        ds_ref[...] = _dot_nt(dyb, ws_ref[...])
        dg_ref[...] = _dot_nt(dyb, wg_ref[...])
        gx_ref[...] = dx
        dws_ref[...] += _dot_tn(sv, dyb)
        dwg_ref[...] += _dot_tn(gv, dyb)
        dgate_ref[...] += dgate
        dlng_ref[...] += dlng
        dlnb_ref[...] += dlnb

    half, full = _tok(T, D_S5), _tok(T, D_MODEL)
    row = _resident((1, D_MODEL))
    wsp = _resident((D_S5, D_MODEL))
    return pl.pallas_call(
        body, name="head_fwd_bwd", grid=(B, L // T),
        in_specs=[half, half, full, full, _per_batch(1, D_MODEL), row, row, wsp, wsp],
        out_specs=[_per_batch(8, LANES), half, half, full, wsp, wsp, _per_batch(1, D_MODEL), row, row],
        out_shape=[SDS((B, 8, LANES), f32), SDS((B, L, D_S5), f32), SDS((B, L, D_GDN), f32), SDS((B, L, D_MODEL), f32),
                   SDS((D_S5, D_MODEL), f32), SDS((D_GDN, D_MODEL), f32), SDS((B, 1, D_MODEL), f32),
                   SDS((1, D_MODEL), f32), SDS((1, D_MODEL), f32)],
        compiler_params=_cparams(2),
    )(s5o, gdo, x, tgt, gate, lng, lnb, ws, wg)


def _adamw_math(w, g, m, v):
    nm = ADAM_B1 * m + (1.0 - ADAM_B1) * g
    nv = ADAM_B2 * v + (1.0 - ADAM_B2) * jnp.square(g)
    m_hat = nm / (1.0 - ADAM_B1 ** ADAM_STEP)
    v_hat = nv / (1.0 - ADAM_B2 ** ADAM_STEP)
    return -ADAM_LR * (m_hat / (jnp.sqrt(v_hat) + ADAM_EPS) + ADAM_WD * w), nm, nv


def _row_tile(rows, cap=512):
    for t in range(min(cap, rows), 15, -1):
        if rows % t == 0 and t % 16 == 0:
            return t
    return rows


def adamw_3d(w, g, m, v, *, name):
    _, R, C = w.shape
    T = _row_tile(R)
    spec = pl.BlockSpec((None, T, C), lambda i: (0, i, 0))

    def body(w_ref, g_ref, m_ref, v_ref, d_ref, nm_ref, nv_ref):
        d_ref[...], nm_ref[...], nv_ref[...] = _adamw_math(w_ref[...], g_ref[...], m_ref[...], v_ref[...])

    return pl.pallas_call(body, name=name, grid=(R // T,), in_specs=[spec] * 4, out_specs=[spec] * 3,
                          out_shape=[SDS((1, R, C), f32)] * 3, compiler_params=_cparams(1))(w, g, m, v)


def adamw_small(ws, gs, ms, vs):
    n = len(ws)

    def body(*refs):
        outs = refs[4 * n:]
        for i in range(n):
            d, nm, nv = _adamw_math(refs[i][...], refs[n + i][...], refs[2 * n + i][...], refs[3 * n + i][...])
            outs[i][...], outs[n + i][...], outs[2 * n + i][...] = d, nm, nv

    res = pl.pallas_call(body, name="adamw_small", out_shape=[SDS(w.shape, f32) for w in ws] * 3,
                         compiler_params=pltpu.CompilerParams(vmem_limit_bytes=VMEM_LIMIT))(*ws, *gs, *ms, *vs)
    return res[:n], res[n:2 * n], res[2 * n:]


def sum_cores(own, got, *, name):
    A, H, C = own.shape
    T = _row_tile(H)
    spec = pl.BlockSpec((None, T, C), lambda a, i: (a, i, 0))

    def body(a_ref, b_ref, q32_ref, q16_ref):
        q = a_ref[...] + b_ref[...]
        q32_ref[...] = q
        q16_ref[...] = q.astype(bf16)

    return pl.pallas_call(body, name=name, grid=(A, H // T), in_specs=[spec, spec], out_specs=[spec, spec],
                          out_shape=[SDS((A, H, C), f32), SDS((A, H, C), bf16)], compiler_params=_cparams(2))(own, got)


def sum_chips(mine, rec, cpos, *, name):
    H, C = mine.shape
    T = _row_tile(H)
    nt = H // T

    def body(c_ref, m_ref, r_ref, f_ref):
        f_ref[...] = ((m_ref[...] + r_ref[0].astype(f32)) + r_ref[1].astype(f32)) + r_ref[2].astype(f32)

    grid_spec = pltpu.PrefetchScalarGridSpec(
        num_scalar_prefetch=1, grid=(nt,),
        in_specs=[pl.BlockSpec((T, C), lambda i, c_ref: (i, 0)), pl.BlockSpec((3, T, C), lambda i, c_ref: (0, i, 0))],
        out_specs=pl.BlockSpec((None, T, C), lambda i, c_ref: (0, c_ref[0] * nt + i, 0)))
    return pl.pallas_call(body, name=name, grid_spec=grid_spec, out_shape=SDS((1, 2 * H, C), f32),
                          compiler_params=_cparams(1))(cpos.reshape(1).astype(jnp.int32), mine, rec)


CHIP_FLIPS = ((1, 0), (0, 1), (1, 1))


def _pos():
    return lax.axis_index("x"), lax.axis_index("y"), lax.axis_index("c")


def _comm_call(body, srcs, out_sds, n_remote, n_local, name):
    any_spec = pl.BlockSpec(memory_space=pl.ANY)
    return pl.pallas_call(
        body, name=name, in_specs=[any_spec] * len(srcs), out_specs=[any_spec] * len(out_sds), out_shape=out_sds,
        scratch_shapes=[pltpu.SemaphoreType.DMA((n_remote,)), pltpu.SemaphoreType.DMA((n_remote,)),
                        pltpu.SemaphoreType.DMA((max(n_local, 1),))],
        compiler_params=pltpu.CompilerParams(has_side_effects=True),
    )(*srcs)


def _remote(src, dst, send_sems, recv_sems, k, target):
    return pltpu.make_async_remote_copy(src, dst, send_sems.at[k], recv_sems.at[k], device_id=target,
                                        device_id_type=MESH)


def _half_rows(c, rows):
    half = rows // 2
    return pl.ds(pl.multiple_of(c * half, 8), half)


def gather_shards(shards):
    nt = len(shards)

    def body(*refs):
        srcs, outs = refs[:nt], refs[nt:2 * nt]
        send_sems, recv_sems, _ = refs[2 * nt:]
        x, y, c = _pos()
        j = 2 * x + y
        sib = (x, y, 1 - c)
        own = [_remote(srcs[t], outs[t].at[j], send_sems, recv_sems, 7 * t + 6, sib) for t in range(nt)]
        first, passed = [], []
        for k, (fx, fy) in enumerate(CHIP_FLIPS):
            tx, ty = x ^ fx, y ^ fy
            jk = 2 * tx + ty
            for t in range(nt):
                rows = _half_rows(c, srcs[t].shape[0])
                first.append(_remote(srcs[t].at[rows], outs[t].at[j, rows], send_sems, recv_sems, 7 * t + k, (tx, ty, c)))
                passed.append(_remote(outs[t].at[jk, rows], outs[t].at[jk, rows], send_sems, recv_sems, 7 * t + 3 + k, sib))
        for cp in first + own:
            cp.start()
        for a, b in zip(first, passed):
            a.wait_recv()
            b.start()
        for cp in passed + own:
            cp.wait_recv()
        for cp in first + passed + own:
            cp.wait_send()

    return _comm_call(body, shards, [SDS((4,) + s.shape, s.dtype) for s in shards], 7 * nt, 0, "gather_shards")


def swap_halves(ps):
    nt = len(ps)

    def body(*refs):
        srcs, outs = refs[:nt], refs[nt:2 * nt]
        send_sems, recv_sems, _ = refs[2 * nt:]
        x, y, c = _pos()
        cps = [_remote(srcs[t].at[a, _half_rows(1 - c, srcs[t].shape[1])], outs[t].at[a], send_sems, recv_sems, 4 * t + a,
                       (x, y, 1 - c)) for t in range(nt) for a in range(4)]
        for cp in cps:
            cp.start()
        for cp in cps:
            cp.wait()

    return _comm_call(body, ps, [SDS((4, p.shape[1] // 2, p.shape[2]), p.dtype) for p in ps], 4 * nt, 0, "swap_halves")


def scatter_to_chips(qs):
    nt = len(qs)

    def body(*refs):
        srcs, outs = refs[:nt], refs[nt:2 * nt]
        send_sems, recv_sems, _ = refs[2 * nt:]
        x, y, c = _pos()
        cps = []
        for k, (fx, fy) in enumerate(CHIP_FLIPS):
            tx, ty = x ^ fx, y ^ fy
            for t in range(nt):
                cps.append(_remote(srcs[t].at[2 * tx + ty], outs[t].at[k], send_sems, recv_sems, 3 * t + k, (tx, ty, c)))
        for cp in cps:
            cp.start()
        for cp in cps:
            cp.wait()

    return _comm_call(body, qs, [SDS((3,) + q.shape[1:], q.dtype) for q in qs], 3 * nt, 0, "scatter_to_chips")


def join_halves(fs):
    nt = len(fs)

    def body(*refs):
        outs = refs[nt:2 * nt]
        send_sems, recv_sems, _ = refs[2 * nt:]
        x, y, c = _pos()
        cps = []
        for t in range(nt):
            mine = outs[t].at[0, _half_rows(c, outs[t].shape[1])]
            cps.append(_remote(mine, mine, send_sems, recv_sems, t, (x, y, 1 - c)))
        for cp in cps:
            cp.start()
        for cp in cps:
            cp.wait()

    any_spec = pl.BlockSpec(memory_space=pl.ANY)
    return pl.pallas_call(
        body, name="join_halves", in_specs=[any_spec] * nt, out_specs=[any_spec] * nt,
        out_shape=[SDS(f.shape, f.dtype) for f in fs], input_output_aliases={t: t for t in range(nt)},
        scratch_shapes=[pltpu.SemaphoreType.DMA((nt,)), pltpu.SemaphoreType.DMA((nt,)), pltpu.SemaphoreType.DMA((1,))],
        compiler_params=pltpu.CompilerParams(has_side_effects=True),
    )(*fs)


def gather_small(s):
    def body(src, out, send_sems, recv_sems, _):
        x, y, c = _pos()
        j = 2 * x + y
        cps = [_remote(src, out.at[j], send_sems, recv_sems, k, (x ^ fx, y ^ fy, c)) for k, (fx, fy) in enumerate(CHIP_FLIPS)]
        cps.append(_remote(src, out.at[j], send_sems, recv_sems, 3, (x, y, 1 - c)))
        for cp in cps:
            cp.start()
        for cp in cps:
            cp.wait()

    return _comm_call(body, [s], [SDS((4,) + s.shape, s.dtype)], 4, 0, "gather_small")[0]


SMALL_SHAPES = ((D_MODEL,), (1, 3 * D_MODEL), (1, 2, 32, 64), (1, 2, 32, 64), (1, 2, 32), (1, 2, 32, 64, 16),
                (1, 2, 32, 64, 16), (1, 2, 32, 16, 64), (1, 2, 32, 16, 64), (1, D_S5), (1, D_S5), (1, 2, 4), (1, 2, 4),
                (1, GDN_HEAD), (1, D_MODEL), (1, D_MODEL))


def _size(shape):
    return functools.reduce(lambda p, q: p * q, shape)


SMALL_ROWS = tuple(-(-_size(s) // LANES) for s in SMALL_SHAPES)
SMALL_TOTAL = 2176
SMALL_QUARTER = SMALL_TOTAL // 4


def _rows(a):
    flat = a.reshape(-1)
    pad = (-flat.shape[0]) % LANES
    if pad:
        flat = jnp.concatenate([flat, jnp.zeros((pad,), flat.dtype)])
    return flat.reshape(-1, LANES)


def _pack_small(parts):
    rows = [_rows(p) for p in parts]
    rows.append(jnp.zeros((SMALL_TOTAL - sum(SMALL_ROWS), LANES), f32))
    return jnp.concatenate(rows, axis=0)


def _unpack_small(buf):
    out, r = [], 0
    for s, n in zip(SMALL_SHAPES, SMALL_ROWS):
        out.append(buf[r:r + n].reshape(-1)[:_size(s)].reshape(s))
        r += n
    return out


def _as_2d(a):
    return a.reshape(1, -1) if a.ndim == 1 else a.reshape(-1, a.shape[-1])


S5_BG = S5_GROUPS // S5_BLOCKS


def _block_diag_in(bb):
    eye = jnp.eye(S5_BG, dtype=bb.dtype)
    b4 = bb.reshape(S5_BLOCKS, S5_BG, S5_STATE, S5_GROUP)
    return jnp.einsum('jgpc,gh->jgchp', b4, eye).reshape(S5_BLOCKS, S5_BC, S5_BS)


def _block_diag_in_t(d):
    d6 = d.reshape(S5_BLOCKS, S5_BG, S5_GROUP, S5_BG, S5_STATE)
    return jnp.einsum('jgcgp->jgpc', d6).reshape(S5_GROUPS, S5_STATE * S5_GROUP)


def _block_diag_out(cm):
    eye = jnp.eye(S5_BG, dtype=cm.dtype)
    c4 = cm.reshape(S5_BLOCKS, S5_BG, S5_GROUP, S5_STATE)
    return jnp.einsum('jgcp,gh->jhpgc', c4, eye).reshape(S5_BLOCKS, S5_BS, S5_BC)


def _block_diag_out_t(d):
    d6 = d.reshape(S5_BLOCKS, S5_BG, S5_STATE, S5_BG, S5_GROUP)
    return jnp.einsum('jgpgc->jgcp', d6).reshape(S5_GROUPS, S5_GROUP, S5_STATE)


def _to_chunk_rows(a):
    B, L, W = a.shape
    return a.reshape(B, L // CHUNK, CHUNK, W).transpose(0, 1, 3, 2)


def _from_chunk_rows(a):
    B, nc, W, _ = a.shape
    return a.transpose(0, 1, 3, 2).reshape(B, nc * CHUNK, W)


def local_step(x, c, ctx, c_ctx, tgt, w_ada, b_ada, w_in, lam_re, lam_im, log_dt, b_re, b_im, c_re, c_im, s5_d,
               w_glu, b_glu, conv16, a_log, dt_bias, norm_w, w_out, ln_g, ln_b):
    B, L, _ = x.shape
    zeros_state = jnp.zeros((B, GDN_HEADS, GDN_HEAD, GDN_HEAD), f32)

    cc = jnp.concatenate([c, c_ctx[None, :], jnp.zeros((8 - B - 1, D_MODEL), f32)], axis=0)
    m = ada_fwd(cc, w_ada, b_ada)
    shift, scale, gate = m[:B, :D_MODEL], m[:B, D_MODEL:2 * D_MODEL], m[:B, 2 * D_MODEL:]
    mod = jnp.stack([scale, shift], axis=1)
    mod_c = jnp.broadcast_to(jnp.stack([m[B, D_MODEL:2 * D_MODEL], m[B, :D_MODEL]], axis=0)[None], (B, 2, D_MODEL))

    u, z_s5, qkv, z_gdn, ba = in_proj_fwd(x, mod, w_in, name="in_proj_fwd")
    uc, _, qkvc, _, bac = in_proj_fwd(ctx, mod_c, w_in, name="in_proj_fwd_ctx")

    ng = N_DIR * S5_GROUPS
    zoh_in = (lam_re.reshape(ng, S5_STATE), lam_im.reshape(ng, S5_STATE), log_dt.reshape(ng, 1),
              b_re.reshape(ng, S5_STATE * S5_GROUP), b_im.reshape(ng, S5_STATE * S5_GROUP))
    expand = (jnp.arange(S5_STATE * S5_GROUP)[None, :] // S5_GROUP == jnp.arange(S5_STATE)[:, None]).astype(f32)
    ar, ai, bbr, bbi = s5_zoh_fwd(*zoh_in, expand)
    bbr16, bbi16 = bbr.astype(bf16), bbi.astype(bf16)
    c_re16 = c_re.reshape(N_DIR, S5_GROUPS, S5_GROUP, S5_STATE).astype(bf16)
    c_im16 = (-c_im).reshape(N_DIR, S5_GROUPS, S5_GROUP, S5_STATE).astype(bf16)
    s5w, ys, hins, hins_c = [], [], [], []
    for d in range(N_DIR):
        g = slice(d * S5_GROUPS, (d + 1) * S5_GROUPS)
        wd = (_block_diag_in(bbr16[g]), _block_diag_in(bbi16[g]), _block_diag_out(c_re16[d]), _block_diag_out(c_im16[d]),
              jnp.stack([ar[g].reshape(-1), ai[g].reshape(-1)], axis=0))
        s5w.append(wd)
        hin_c, hend_c = s5_scan_fwd(uc, *wd, jnp.zeros((B, 2, S5_HALF), f32), d=d, need_y=False, name=f"s5_fwd_ctx{d}")
        y_d, hin, _ = s5_scan_fwd(u, *wd, hend_c, d=d, need_y=True, name=f"s5_fwd{d}")
        ys.append(y_d)
        hins.append(hin)
        hins_c.append(hin_c)
    glu_w = (s5_d.reshape(1, D_S5), w_glu, b_glu.reshape(1, D_S5))
    s5o = s5_glu_fwd(u, ys[0], ys[1], z_s5, *glu_w)

    act = conv_fwd(qkv, conv16, is_ctx=False, name="conv_fwd")
    act_c = conv_fwd(qkvc, conv16, is_ctx=True, name="conv_fwd_ctx")
    pad8 = jnp.zeros((1, 8), f32)
    alog16 = jnp.concatenate([pad8, a_log.reshape(1, 8)], axis=1)
    dtb16 = jnp.concatenate([pad8, dt_bias.reshape(1, 8)], axis=1)
    bg = gates_fwd(ba, alog16, dtb16, name="gates_fwd")
    bg_c = gates_fwd(bac, alog16, dtb16, name="gates_fwd_ctx")
    bgr, bgr_c = _to_chunk_rows(bg), _to_chunk_rows(bg_c)
    cks_c, s_c = gdn_fwd(act_c, bg_c, bgr_c, (zeros_state, zeros_state), need_o=False, name="gdn_fwd_ctx")
    os_, cks, _ = gdn_fwd(act, bg, bgr, s_c, need_o=True, name="gdn_fwd")
    nw = norm_w.reshape(1, GDN_HEAD)
    gdo = gnorm_fwd(os_[0], os_[1], z_gdn, nw)

    loss8, ds5o, dgdo, gx_res, dws, dwg, dgate, dlng, dlnb = head_fwd_bwd(
        s5o, gdo, x, tgt, gate[:, None, :], ln_g.reshape(1, D_MODEL), ln_b.reshape(1, D_MODEL), w_out[:D_S5], w_out[D_S5:])
    loss = jnp.sum(loss8[:, 0, 0])
    d_w_out = jnp.concatenate([dws, dwg], axis=0)

    do, dz_gdn, d_norm_w = gnorm_bwd(os_[0], os_[1], z_gdn, nw, dgdo)
    dacts, dbgs, dbgrs, ds0s = gdn_bwd(act, bg, bgr, cks, do, (zeros_state, zeros_state), name="gdn_bwd")
    dacts_c, dbgs_c, dbgrs_c, _ = gdn_bwd(act_c, bg_c, bgr_c, cks_c, None, ds0s, name="gdn_bwd_ctx")
    dbg = dbgs[0] + dbgs[1] + _from_chunk_rows(dbgrs[0] + dbgrs[1])
    dbg_c = dbgs_c[0] + dbgs_c[1] + _from_chunk_rows(dbgrs_c[0] + dbgrs_c[1])
    dba, dal, ddt = gates_bwd(ba, alog16, dtb16, dbg, name="gates_bwd")
    dbac, dal_c, ddt_c = gates_bwd(bac, alog16, dtb16, dbg_c, name="gates_bwd_ctx")
    d_a_log = (dal + dal_c)[:, 8:].reshape(1, N_DIR, GDN_HEADS)
    d_dt_bias = (ddt + ddt_c)[:, 8:].reshape(1, N_DIR, GDN_HEADS)
    dqkv, dcw = conv_bwd(qkv, conv16, dacts[0], dacts[1], is_ctx=False, name="conv_bwd")
    dqkvc, dcw_c = conv_bwd(qkvc, conv16, dacts_c[0], dacts_c[1], is_ctx=True, name="conv_bwd_ctx")
    d_conv16 = jnp.sum(dcw, axis=0) + jnp.sum(dcw_c, axis=0)

    du_skip, dy, dz_s5, d_s5_d, d_w_glu, d_b_glu = s5_glu_bwd(u, ys[0], ys[1], z_s5, *glu_w, ds5o)
    du, duc = du_skip, jnp.zeros_like(uc)
    dar, dai, dbbr, dbbi, dcre, dcim = [], [], [], [], [], []
    for d in range(N_DIR):
        du_d, dbre1, dbim1, dct1, dcb1, da1, dh0 = s5_scan_bwd(u, dy, *s5w[d], hins[d],
                                                                jnp.zeros((B, 2, S5_HALF), f32), d=d, name=f"s5_bwd{d}")
        duc_d, dbre2, dbim2, _, _, da2, _ = s5_scan_bwd(uc, None, *s5w[d], hins_c[d], dh0, d=d, name=f"s5_bwd_ctx{d}")
        du, duc = du + du_d, duc + duc_d
        da = da1 + da2
        dar.append(da[0].reshape(S5_GROUPS, S5_STATE))
        dai.append(da[1].reshape(S5_GROUPS, S5_STATE))
        dbbr.append(_block_diag_in_t(dbre1 + dbre2))
        dbbi.append(_block_diag_in_t(dbim1 + dbim2))
        dcre.append(_block_diag_out_t(dct1))
        dcim.append(-_block_diag_out_t(dcb1))
    dlr, dli, dldt, dbre, dbim = s5_zoh_bwd(*zoh_in, expand, jnp.concatenate(dar, 0), jnp.concatenate(dai, 0),
                                            jnp.concatenate(dbbr, 0), jnp.concatenate(dbbi, 0))
    d_s5 = (dlr, dli, dldt, dbre, dbim, jnp.stack(dcre, 0), jnp.stack(dcim, 0))

    padg = lambda a: jnp.concatenate([a, jnp.zeros(a.shape[:2] + (LANES - N_GATE,), f32)], axis=2)
    dw_l, dmod, grad_x = in_proj_bwd(x, mod, (du, dz_s5, dqkv, dz_gdn, padg(dba)), w_in, gx_res, name="in_proj_bwd")
    zc = jnp.zeros_like(uc)
    dw_c, dmod_c = in_proj_bwd(ctx, mod_c, (duc, zc, dqkvc, zc, padg(dbac)), w_in, None, name="in_proj_bwd_ctx")
    d_w_in = dw_l + dw_c
    dmod_c = jnp.sum(dmod_c, axis=0)

    dm_rows = jnp.concatenate([dmod[:, 1], dmod[:, 0], dgate[:, 0]], axis=1)
    dm_ctx = jnp.concatenate([dmod_c[1], dmod_c[0], jnp.zeros((D_MODEL,), f32)])[None]
    dm = jnp.concatenate([dm_rows, dm_ctx, jnp.zeros((8 - B - 1, 3 * D_MODEL), f32)], axis=0)
    dcc, d_w_ada, d_b_ada = ada_bwd(cc, w_ada, dm)
    small = (dcc[B], d_b_ada, *d_s5, d_s5_d, d_b_glu, d_a_log, d_dt_bias, d_norm_w, dlng, dlnb)
    small = tuple(g.reshape(s) for g, s in zip(small, SMALL_SHAPES))
    return loss, grad_x, (d_w_ada, d_w_in, d_w_out, d_w_glu, d_conv16), small


SHARDED = (1, 3, 18, 12, 14)
SMALL = tuple(i for i in range(21) if i not in SHARDED)
W_IN_SHARD = 772


def _conv_rows(w):
    return jnp.concatenate([w.reshape(9, w.shape[-1]), jnp.zeros((CONV_ROWS - 9, w.shape[-1]), f32)], axis=0)


def kernel(x, c, ctx, c_ctx, w_ada, b_ada, w_in, s5_lambda_re, s5_lambda_im, s5_log_dt, s5_b_re, s5_b_im, s5_c_re, s5_c_im, s5_d, w_glu, b_glu, conv_w, gdn_a_log, gdn_dt_bias, gdn_norm_w, w_out, ln_g, ln_b, loss_target, m_c_ctx, m_w_ada, m_b_ada, m_w_in, m_s5_lambda_re, m_s5_lambda_im, m_s5_log_dt, m_s5_b_re, m_s5_b_im, m_s5_c_re, m_s5_c_im, m_s5_d, m_w_glu, m_b_glu, m_conv_w, m_gdn_a_log, m_gdn_dt_bias, m_gdn_norm_w, m_w_out, m_ln_g, m_ln_b, v_c_ctx, v_w_ada, v_b_ada, v_w_in, v_s5_lambda_re, v_s5_lambda_im, v_s5_log_dt, v_s5_b_re, v_s5_b_im, v_s5_c_re, v_s5_c_im, v_s5_d, v_w_glu, v_b_glu, v_conv_w, v_gdn_a_log, v_gdn_dt_bias, v_gdn_norm_w, v_w_out, v_ln_g, v_ln_b):
    weights = [c_ctx, w_ada, b_ada, w_in, s5_lambda_re, s5_lambda_im, s5_log_dt, s5_b_re, s5_b_im, s5_c_re, s5_c_im,
               s5_d, w_glu, b_glu, conv_w, gdn_a_log, gdn_dt_bias, gdn_norm_w, w_out, ln_g, ln_b]
    ms = [m_c_ctx, m_w_ada, m_b_ada, m_w_in, m_s5_lambda_re, m_s5_lambda_im, m_s5_log_dt, m_s5_b_re, m_s5_b_im,
          m_s5_c_re, m_s5_c_im, m_s5_d, m_w_glu, m_b_glu, m_conv_w, m_gdn_a_log, m_gdn_dt_bias, m_gdn_norm_w, m_w_out,
          m_ln_g, m_ln_b]
    vs = [v_c_ctx, v_w_ada, v_b_ada, v_w_in, v_s5_lambda_re, v_s5_lambda_im, v_s5_log_dt, v_s5_b_re, v_s5_b_im,
          v_s5_c_re, v_s5_c_im, v_s5_d, v_w_glu, v_b_glu, v_conv_w, v_gdn_a_log, v_gdn_dt_bias, v_gdn_norm_w, v_w_out,
          v_ln_g, v_ln_b]
    cpos = lax.axis_index("c")
    jchip = 2 * lax.axis_index("x") + lax.axis_index("y")

    conv_shard = _conv_rows(conv_w)
    g_ada, g_in, g_out, g_glu, g_conv = gather_shards(
        [w_ada[0].astype(bf16), w_in[0].astype(bf16), w_out[0].astype(bf16), w_glu[0].astype(bf16), conv_shard])
    w_in_pad = jnp.concatenate([g_in[0], g_in[1], g_in[2], g_in[3], jnp.zeros((D_MODEL, IN_PAD - P_IN), bf16)], axis=1)
    conv16 = g_conv.transpose(1, 0, 2).reshape(CONV_ROWS, 3 * D_GDN)

    loss, grad_x, big, small = local_step(
        x, c, ctx, c_ctx, loss_target, g_ada, b_ada, w_in_pad, s5_lambda_re, s5_lambda_im, s5_log_dt, s5_b_re, s5_b_im,
        s5_c_re, s5_c_im, s5_d, g_glu.reshape(D_S5, D_S5), b_glu, conv16, gdn_a_log, gdn_dt_bias, gdn_norm_w,
        g_out.reshape(D_MODEL, D_MODEL), ln_g, ln_b)
    loss = lax.psum(loss, ("x", "y", "c"))

    d_w_ada, d_w_in, d_w_out, d_w_glu, d_conv16 = big
    slabs = [d_w_ada,
             d_w_in[:, :P_IN].reshape(D_MODEL, 4, W_IN_SHARD).transpose(1, 0, 2),
             d_w_out.reshape(4, D_MODEL // 4, D_MODEL),
             d_w_glu.reshape(4, D_S5 // 4, D_S5),
             d_conv16.reshape(CONV_ROWS, 4, 3 * D_GDN // 4).transpose(1, 0, 2),
             _pack_small(small).reshape(4, SMALL_QUARTER, LANES)]
    got = swap_halves(slabs)
    q32, q16 = [], []
    for t, (s, g) in enumerate(zip(slabs, got)):
        own = lax.dynamic_index_in_dim(s.reshape(4, 2, s.shape[1] // 2, s.shape[2]), cpos, axis=1, keepdims=False)
        a, b = sum_cores(own, g, name=f"sum_cores{t}")
        q32.append(a)
        q16.append(b)
    rec = scatter_to_chips(q16)
    fs = [sum_chips(lax.dynamic_index_in_dim(q, jchip, axis=0, keepdims=False), r, cpos, name=f"sum_chips{t}")
          for t, (q, r) in enumerate(zip(q32, rec))]
    red = join_halves(fs)
    g_small = _unpack_small(gather_small(red[5][0]).reshape(SMALL_TOTAL, LANES))

    grads, deltas, new_m, new_v = [None] * 21, [None] * 21, [None] * 21, [None] * 21
    for t, i in enumerate(SHARDED):
        conv = i == 14
        prep = (lambda a: _conv_rows(a)[None]) if conv else (lambda a: a)
        d, nm, nv = adamw_3d(prep(weights[i]), red[t], prep(ms[i]), prep(vs[i]), name=f"adamw{t}")
        for lst, val in ((grads, red[t]), (deltas, d), (new_m, nm), (new_v, nv)):
            lst[i] = val[0, :9].reshape(weights[i].shape) if conv else val
    sm = adamw_small([_as_2d(weights[i]) for i in SMALL], [_as_2d(g) for g in g_small], [_as_2d(ms[i]) for i in SMALL],
                     [_as_2d(vs[i]) for i in SMALL])
    for n, i in enumerate(SMALL):
        grads[i] = g_small[n]
        for lst, res in ((deltas, sm[0]), (new_m, sm[1]), (new_v, sm[2])):
            lst[i] = res[n].reshape(weights[i].shape)
    return (loss, grad_x, *grads, *deltas, *new_m, *new_v)
```

```python
import functools

import jax
import jax.numpy as jnp
from jax import lax
from jax.experimental import pallas as pl
from jax.experimental.pallas import tpu as pltpu

f32 = jnp.float32
bf16 = jnp.bfloat16
SDS = jax.ShapeDtypeStruct

D_MODEL = 1024
D_S5 = 512
S5_GROUP = 16
S5_GROUPS = 32
S5_STATE = 64
S5_HALF = S5_GROUPS * S5_STATE
D_GDN = 512
GDN_HEAD = 128
GDN_HEADS = 4
CHUNK = 64
GRID_W = 64
N_DIR = 2
P_IN = 3088
DEEPNORM_ALPHA = 2.0 ** 0.25
LN_EPS = 1e-5
NORM_EPS = 1e-6
ADAM_LR, ADAM_B1, ADAM_B2, ADAM_EPS, ADAM_WD, ADAM_STEP = 0.001, 0.9, 0.999, 1e-08, 0.01, 10

LANES = 128
VMEM_LIMIT = 56 * 1024 * 1024
TOK_TILE = 256
S5_TILE = 256
MESH = pl.DeviceIdType.MESH


def _cparams(n_grid):
    return pltpu.CompilerParams(dimension_semantics=("arbitrary",) * n_grid, vmem_limit_bytes=VMEM_LIMIT)


def _dot(a, b):
    return jnp.dot(a.astype(bf16), b.astype(bf16), preferred_element_type=f32)


def _dot_nt(a, b):
    return lax.dot_general(a.astype(bf16), b.astype(bf16), (((1,), (1,)), ((), ())), preferred_element_type=f32)


def _dot_tn(a, b):
    return lax.dot_general(a.astype(bf16), b.astype(bf16), (((0,), (0,)), ((), ())), preferred_element_type=f32)


def _dot_hi(a, b):
    return jnp.dot(a, b, precision=lax.Precision.HIGHEST, preferred_element_type=f32)


def _dot_h3(a, b):
    return jnp.dot(a, b, precision=lax.Precision.HIGH, preferred_element_type=f32)


@jax.custom_vjp
def _mm(a, b):
    return _dot(a, b)


@jax.custom_vjp
def _mm_nt(a, b):
    return _dot_nt(a, b)


@jax.custom_vjp
def _mm_tn(a, b):
    return _dot_tn(a, b)


_mm.defvjp(lambda a, b: (_dot(a, b), (a, b)), lambda r, g: (_mm_nt(g, r[1]), _mm_tn(r[0], g)))
_mm_nt.defvjp(lambda a, b: (_dot_nt(a, b), (a, b)), lambda r, g: (_mm(g, r[1]), _mm_tn(g, r[0])))
_mm_tn.defvjp(lambda a, b: (_dot_tn(a, b), (a, b)), lambda r, g: (_mm_nt(r[1], g), _mm(r[0], g)))


def _silu(x):
    return x * jax.nn.sigmoid(x)


def _gelu(x):
    return 0.5 * x * (1.0 + lax.erf(x * (2.0 ** -0.5)))


def _resident(shape):
    nd = len(shape)
    return pl.BlockSpec(shape, lambda *_: (0,) * nd, pipeline_mode=pl.Buffered(1))


def _tok(tile, width, nt=None, rev=False):
    if rev:
        return pl.BlockSpec((None, tile, width), lambda b, n: (b, nt - 1 - n, 0))
    return pl.BlockSpec((None, tile, width), lambda b, n: (b, n, 0))


def _per_batch(rows, width):
    return pl.BlockSpec((None, rows, width), lambda b, n: (b, 0, 0))


def _first_step():
    return jnp.logical_and(pl.program_id(0) == 0, pl.program_id(1) == 0)


ADA_SHARD = 3 * D_MODEL // 4


def ada_fwd(cc, w, b):
    def body(cc_ref, w_ref, b_ref, m_ref):
        s = _silu(cc_ref[...]).astype(bf16)
        for j in range(4):
            sl = slice(j * ADA_SHARD, (j + 1) * ADA_SHARD)
            m_ref[:, sl] = _dot(s, w_ref[j]) + b_ref[:, sl]

    return pl.pallas_call(body, name="ada_fwd", out_shape=SDS((8, 3 * D_MODEL), f32),
                          compiler_params=pltpu.CompilerParams(vmem_limit_bytes=VMEM_LIMIT))(cc, w, b)


def ada_bwd(cc, w, dm):
    def body(cc_ref, w_ref, dm_ref, dcc_ref, dw_ref, db_ref):
        s, vjp = jax.vjp(_silu, cc_ref[...])
        ds = jnp.zeros((8, D_MODEL), f32)
        for j in range(4):
            dmj = dm_ref[:, j * ADA_SHARD:(j + 1) * ADA_SHARD]
            ds = ds + _dot_nt(dmj, w_ref[j])
            dw_ref[j] = _dot_tn(s, dmj)
        dcc_ref[...] = vjp(ds)[0]
        db_ref[...] = jnp.sum(dm_ref[...], axis=0, keepdims=True)

    return pl.pallas_call(
        body, name="ada_bwd",
        out_shape=[SDS((8, D_MODEL), f32), SDS((4, D_MODEL, ADA_SHARD), f32), SDS((1, 3 * D_MODEL), f32)],
        compiler_params=pltpu.CompilerParams(vmem_limit_bytes=VMEM_LIMIT))(cc, w, dm)


N_GATE = 2 * N_DIR * GDN_HEADS
IN_WIDTHS = (D_S5, D_S5, 3 * D_GDN, D_GDN, LANES)
IN_OFFS = (0, 512, 1024, 2560, 3072)
IN_PAD = 3200


def in_proj_fwd(x, mod, w, *, name):
    B, L, _ = x.shape
    T = min(TOK_TILE, L)

    def body(x_ref, mod_ref, w_ref, *o_refs):
        h = (x_ref[...] * (1.0 + mod_ref[0:1, :]) + mod_ref[1:2, :]).astype(bf16)
        for o_ref, off, wd in zip(o_refs, IN_OFFS, IN_WIDTHS):
            r = _dot(h, w_ref[:, off:off + wd])
            o_ref[...] = r[:, :o_ref.shape[-1]]

    outw = (D_S5, D_S5, 3 * D_GDN, D_GDN, N_GATE)
    return pl.pallas_call(
        body, name=name, grid=(B, L // T),
        in_specs=[_tok(T, D_MODEL), _per_batch(2, D_MODEL), _resident((D_MODEL, IN_PAD))],
        out_specs=[_tok(T, wd) for wd in outw],
        out_shape=[SDS((B, L, wd), f32) for wd in outw],
        compiler_params=_cparams(2),
    )(x, mod, w)


def in_proj_bwd(x, mod, ds, w, gx_res, dw_start, *, name):
    B, L, _ = x.shape
    T = min(TOK_TILE, L)
    with_dx = gx_res is not None
    with_start = dw_start is not None
    n_u = len(ds[0])

    def body(*refs):
        x_ref, mod_ref = refs[0], refs[1]
        du_refs = refs[2:2 + n_u]
        d_refs = refs[2 + n_u:6 + n_u]
        w_ref = refs[6 + n_u]
        k = 7 + n_u
        if with_dx:
            gx_ref = refs[k]
            k += 1
        if with_start:
            start_ref = refs[k]
            k += 1
        dw_ref, dmod_ref = refs[k], refs[k + 1]
        if with_dx:
            dx_ref = refs[k + 2]
        n = pl.program_id(1)

        @pl.when(_first_step())
        def _():
            dw_ref[...] = start_ref[...] if with_start else jnp.zeros_like(dw_ref)

        @pl.when(n == 0)
        def _():
            dmod_ref[...] = jnp.zeros_like(dmod_ref)

        xv = x_ref[...]
        scale1 = 1.0 + mod_ref[0:1, :]
        h = (xv * scale1 + mod_ref[1:2, :]).astype(bf16)
        du = du_refs[0][...]
        for r in du_refs[1:]:
            du = du + r[...]
        dh = jnp.zeros((T, D_MODEL), f32)
        for dv, off, wd in zip([du] + [r[...] for r in d_refs], IN_OFFS, IN_WIDTHS):
            dv = dv.astype(bf16)
            dh = dh + _dot_nt(dv, w_ref[:, off:off + wd])
            dw_ref[:, off:off + wd] += _dot_tn(h, dv)
        dmod_ref[0:1, :] += jnp.sum(dh * xv, axis=0, keepdims=True)
        dmod_ref[1:2, :] += jnp.sum(dh, axis=0, keepdims=True)
        if with_dx:
            dx_ref[...] = gx_ref[...] + dh * scale1

    in_specs = ([_tok(T, D_MODEL), _per_batch(2, D_MODEL)] + [_tok(T, D_S5)] * n_u + [_tok(T, wd) for wd in IN_WIDTHS[1:]]
                + [_resident((D_MODEL, IN_PAD))])
    args = [x, mod, *ds[0], *ds[1:], w]
    out_specs = [_resident((D_MODEL, IN_PAD)), _per_batch(2, D_MODEL)]
    out_shape = [SDS((D_MODEL, IN_PAD), f32), SDS((B, 2, D_MODEL), f32)]
    if with_dx:
        in_specs.append(_tok(T, D_MODEL))
        args.append(gx_res)
        out_specs.append(_tok(T, D_MODEL))
        out_shape.append(SDS((B, L, D_MODEL), f32))
    if with_start:
        in_specs.append(_resident((D_MODEL, IN_PAD)))
        args.append(dw_start)
    return pl.pallas_call(body, name=name, grid=(B, L // T), in_specs=in_specs, out_specs=out_specs,
                          out_shape=out_shape, compiler_params=_cparams(2))(*args)


def _s5_zoh(lr, li, ldt, bre, bim, expand):
    dt = jnp.exp(ldt)
    zr, zi = lr * dt, li * dt
    e = jnp.exp(zr)
    ar, ai = e * jnp.cos(zi), e * jnp.sin(zi)
    den = lr * lr + li * li
    czr = ((ar - 1.0) * lr + ai * li) / den
    czi = (ai * lr - (ar - 1.0) * li) / den
    czr_e, czi_e = _dot_hi(czr, expand), _dot_hi(czi, expand)
    return ar, ai, czr_e * bre - czi_e * bim, czr_e * bim + czi_e * bre


_ZOH_OUT = [(N_DIR * S5_GROUPS, S5_STATE)] * 2 + [(N_DIR * S5_GROUPS, S5_STATE * S5_GROUP)] * 2


def s5_zoh_fwd(lr, li, ldt, bre, bim, expand):
    def body(lr_ref, li_ref, ldt_ref, bre_ref, bim_ref, e_ref, ar_ref, ai_ref, bbr_ref, bbi_ref):
        ar, ai, bbr, bbi = _s5_zoh(lr_ref[...], li_ref[...], ldt_ref[...], bre_ref[...], bim_ref[...], e_ref[...])
        ar_ref[...], ai_ref[...], bbr_ref[...], bbi_ref[...] = ar, ai, bbr, bbi

    return pl.pallas_call(body, name="s5_zoh_fwd", out_shape=[SDS(s, f32) for s in _ZOH_OUT])(
        lr, li, ldt, bre, bim, expand)


def s5_zoh_bwd(lr, li, ldt, bre, bim, expand, dar, dai, dbbr, dbbi):
    def body(lr_ref, li_ref, ldt_ref, bre_ref, bim_ref, e_ref, dar_ref, dai_ref, dbbr_ref, dbbi_ref,
             dlr_ref, dli_ref, dldt_ref, dbre_ref, dbim_ref):
        ev = e_ref[...]
        _, vjp = jax.vjp(lambda a, b, c, d, e: _s5_zoh(a, b, c, d, e, ev),
                         lr_ref[...], li_ref[...], ldt_ref[...], bre_ref[...], bim_ref[...])
        outs = vjp((dar_ref[...], dai_ref[...], dbbr_ref[...], dbbi_ref[...]))
        dlr_ref[...], dli_ref[...], dldt_ref[...], dbre_ref[...], dbim_ref[...] = outs

    shapes = [lr.shape, li.shape, ldt.shape, bre.shape, bim.shape]
    return pl.pallas_call(body, name="s5_zoh_bwd", out_shape=[SDS(s, f32) for s in shapes])(
        lr, li, ldt, bre, bim, expand, dar, dai, dbbr, dbbi)


def _scan_rows(T, rev, ar, ai, h0s, refs, off):
    def step(i, carry):
        t = off + ((T - 1 - i) if rev else i)
        out = []
        for (hr, hi), (r_ref, i_ref) in zip(carry, refs):
            nr = ar * hr - ai * hi + r_ref[pl.ds(t, 1), :]
            ni = ar * hi + ai * hr + i_ref[pl.ds(t, 1), :]
            r_ref[pl.ds(t, 1), :] = nr
            i_ref[pl.ds(t, 1), :] = ni
            out.append((nr, ni))
        return tuple(out)

    return lax.fori_loop(0, T, step, tuple(h0s))


S5_BLOCKS = 4
S5_BC = D_S5 // S5_BLOCKS
S5_BS = S5_HALF // S5_BLOCKS


def _s5_in(uv, bre_ref, bim_ref, hr_ref, hi_ref, off, T):
    for jb in range(S5_BLOCKS):
        uj = uv[:, jb * S5_BC:(jb + 1) * S5_BC]
        hr_ref[off:off + T, jb * S5_BS:(jb + 1) * S5_BS] = _dot(uj, bre_ref[jb])
        hi_ref[off:off + T, jb * S5_BS:(jb + 1) * S5_BS] = _dot(uj, bim_ref[jb])


def _s5_specs(B, T, nt, rev):
    tidx = (lambda n: nt - 1 - n) if rev else (lambda n: n)
    tok = pl.BlockSpec((B, T, D_S5), lambda n: (0, tidx(n), 0))
    hin = pl.BlockSpec((B, None, 2, S5_HALF), lambda n: (0, tidx(n), 0, 0))
    state = pl.BlockSpec((B, 2, S5_HALF), lambda n: (0, 0, 0))
    return tok, hin, state


def s5_scan_fwd(u, bre, bim, ctop, cbot, arow, h0, *, d, need_y, name):
    B, L, _ = u.shape
    T = min(S5_TILE, L)
    nt = L // T
    rev = d == 1

    def body(u_ref, bre_ref, bim_ref, ct_ref, cb_ref, a_ref, h0_ref, *rest):
        if need_y:
            y_ref, hin_ref, hend_ref, hr_scr, hi_scr, h_scr = rest
        else:
            hin_ref, hend_ref, hr_scr, hi_scr, h_scr = rest
        n = pl.program_id(0)

        @pl.when(n == 0)
        def _():
            h_scr[...] = h0_ref[...]

        hin_ref[...] = h_scr[...]
        for b in range(B):
            _s5_in(u_ref[b].astype(bf16), bre_ref, bim_ref, hr_scr.at[b], hi_scr.at[b], 0, T)
        hs = _scan_rows(T, rev, a_ref[0:1, :], a_ref[1:2, :], [(h_scr[b, 0:1, :], h_scr[b, 1:2, :]) for b in range(B)],
                        [(hr_scr.at[b], hi_scr.at[b]) for b in range(B)], 0)
        for b in range(B):
            h_scr[b, 0:1, :] = hs[b][0]
            h_scr[b, 1:2, :] = hs[b][1]
            if need_y:
                for jb in range(S5_BLOCKS):
                    st = slice(jb * S5_BS, (jb + 1) * S5_BS)
                    y_ref[b, :, jb * S5_BC:(jb + 1) * S5_BC] = (_dot(hr_scr[b, :, st], ct_ref[jb])
                                                                 + _dot(hi_scr[b, :, st], cb_ref[jb]))

        @pl.when(n == nt - 1)
        def _():
            hend_ref[...] = h_scr[...]

    tok, hin_spec, state = _s5_specs(B, T, nt, rev)
    out_specs = [hin_spec, state]
    out_shape = [SDS((B, nt, 2, S5_HALF), f32), SDS((B, 2, S5_HALF), f32)]
    if need_y:
        out_specs.insert(0, tok)
        out_shape.insert(0, SDS((B, L, D_S5), f32))
    w_in, w_out = _resident((S5_BLOCKS, S5_BC, S5_BS)), _resident((S5_BLOCKS, S5_BS, S5_BC))
    return pl.pallas_call(
        body, name=name, grid=(nt,),
        in_specs=[tok, w_in, w_in, w_out, w_out, _resident((2, S5_HALF)), state],
        out_specs=out_specs, out_shape=out_shape,
        scratch_shapes=[pltpu.VMEM((B, T, S5_HALF), f32), pltpu.VMEM((B, T, S5_HALF), f32),
                        pltpu.VMEM((B, 2, S5_HALF), f32)],
        compiler_params=_cparams(1),
    )(u, bre, bim, ctop, cbot, arow, h0)


def s5_scan_bwd(u, dy, bre, bim, ctop, cbot, arow, hin, dhend, *, d, name):
    B, L, _ = u.shape
    T = min(S5_TILE, L)
    nt = L // T
    rev = d == 1
    has_dy = dy is not None
    PAD = 8

    def body(*refs):
        u_ref = refs[0]
        k = 1
        if has_dy:
            dy_ref = refs[1]
            k = 2
        bre_ref, bim_ref, ct_ref, cb_ref, a_ref, hin_ref, dhend_ref = refs[k:k + 7]
        du_ref, dbre_ref, dbim_ref, dct_ref, dcb_ref, da_ref, dh0_ref = refs[k + 7:k + 14]
        hr_scr, hi_scr, gr_scr, gi_scr, p_scr = refs[k + 14:]
        n = pl.program_id(0)

        @pl.when(n == 0)
        def _():
            for r in (dbre_ref, dbim_ref, dct_ref, dcb_ref, da_ref):
                r[...] = jnp.zeros_like(r)
            p_scr[...] = dhend_ref[...]

        ar, ai = a_ref[0:1, :], a_ref[1:2, :]
        prev_row = PAD + T if rev else PAD - 1
        uvs = []
        for b in range(B):
            uvs.append(u_ref[b].astype(bf16))
            _s5_in(uvs[b], bre_ref, bim_ref, hr_scr.at[b], hi_scr.at[b], PAD, T)
            hr_scr[b, prev_row:prev_row + 1, :] = hin_ref[b, 0:1, :]
            hi_scr[b, prev_row:prev_row + 1, :] = hin_ref[b, 1:2, :]
        _scan_rows(T, rev, ar, ai, [(hin_ref[b, 0:1, :], hin_ref[b, 1:2, :]) for b in range(B)],
                   [(hr_scr.at[b], hi_scr.at[b]) for b in range(B)], PAD)
        if has_dy:
            for b in range(B):
                dyv = dy_ref[b].astype(bf16)
                for jb in range(S5_BLOCKS):
                    st = slice(jb * S5_BS, (jb + 1) * S5_BS)
                    dyj = dyv[:, jb * S5_BC:(jb + 1) * S5_BC]
                    gr_scr[b, :, st] = _dot_nt(dyj, ct_ref[jb])
                    gi_scr[b, :, st] = _dot_nt(dyj, cb_ref[jb])
                    dct_ref[jb] += _dot_tn(hr_scr[b, PAD:PAD + T, st], dyj)
                    dcb_ref[jb] += _dot_tn(hi_scr[b, PAD:PAD + T, st], dyj)
        else:
            gr_scr[...] = jnp.zeros_like(gr_scr)
            gi_scr[...] = jnp.zeros_like(gi_scr)

        def step(i, carry):
            t = i if rev else T - 1 - i
            tp = PAD + t + (1 if rev else -1)
            out = []
            for b, (pr, pi, dar, dai) in enumerate(carry):
                gr = gr_scr[b, pl.ds(t, 1), :] + pr
                gi = gi_scr[b, pl.ds(t, 1), :] + pi
                gr_scr[b, pl.ds(t, 1), :] = gr
                gi_scr[b, pl.ds(t, 1), :] = gi
                hpr = hr_scr[b, pl.ds(tp, 1), :]
                hpi = hi_scr[b, pl.ds(tp, 1), :]
                out.append((ar * gr + ai * gi, ar * gi - ai * gr, dar + hpr * gr + hpi * gi, dai + hpr * gi - hpi * gr))
            return tuple(out)

        zero = jnp.zeros((1, S5_HALF), f32)
        res = lax.fori_loop(0, T, step, tuple((p_scr[b, 0:1, :], p_scr[b, 1:2, :], zero, zero) for b in range(B)))
        for b in range(B):
            pr, pi, dar, dai = res[b]
            p_scr[b, 0:1, :] = pr
            p_scr[b, 1:2, :] = pi
            da_ref[0:1, :] += dar
            da_ref[1:2, :] += dai
            for jb in range(S5_BLOCKS):
                st = slice(jb * S5_BS, (jb + 1) * S5_BS)
                ch = slice(jb * S5_BC, (jb + 1) * S5_BC)
                gr_j = gr_scr[b, :, st].astype(bf16)
                gi_j = gi_scr[b, :, st].astype(bf16)
                du_ref[b, :, ch] = _dot_nt(gr_j, bre_ref[jb]) + _dot_nt(gi_j, bim_ref[jb])
                dbre_ref[jb] += _dot_tn(uvs[b][:, ch], gr_j)
                dbim_ref[jb] += _dot_tn(uvs[b][:, ch], gi_j)

        @pl.when(n == nt - 1)
        def _():
            dh0_ref[...] = p_scr[...]

    tok, hin_spec, state = _s5_specs(B, T, nt, not rev)
    w_in, w_out = _resident((S5_BLOCKS, S5_BC, S5_BS)), _resident((S5_BLOCKS, S5_BS, S5_BC))
    wspecs = [w_in, w_in, w_out, w_out]
    in_specs = [tok] + ([tok] if has_dy else []) + wspecs + [_resident((2, S5_HALF)), hin_spec, state]
    args = [u] + ([dy] if has_dy else []) + [bre, bim, ctop, cbot, arow, hin, dhend]
    return pl.pallas_call(
        body, name=name, grid=(nt,), in_specs=in_specs,
        out_specs=[tok] + wspecs + [_resident((2, S5_HALF)), state],
        out_shape=[SDS((B, L, D_S5), f32), SDS((S5_BLOCKS, S5_BC, S5_BS), f32), SDS((S5_BLOCKS, S5_BC, S5_BS), f32),
                   SDS((S5_BLOCKS, S5_BS, S5_BC), f32), SDS((S5_BLOCKS, S5_BS, S5_BC), f32), SDS((2, S5_HALF), f32),
                   SDS((B, 2, S5_HALF), f32)],
        scratch_shapes=[pltpu.VMEM((B, T + 2 * PAD, S5_HALF), f32), pltpu.VMEM((B, T + 2 * PAD, S5_HALF), f32),
                        pltpu.VMEM((B, T, S5_HALF), f32), pltpu.VMEM((B, T, S5_HALF), f32),
                        pltpu.VMEM((B, 2, S5_HALF), f32)],
        compiler_params=_cparams(1),
    )(*args)


def _glu_fn(u, y0, y1, z, dsk, wg, bg):
    g = _gelu(dsk * u + y0 + y1)
    return g * jax.nn.sigmoid(_mm(g, wg) + bg) * _silu(z)


def s5_glu_fwd(u, y0, y1, z, dsk, wg, bg):
    B, L, _ = u.shape
    T = min(TOK_TILE, L)

    def body(u_ref, y0_ref, y1_ref, z_ref, dsk_ref, wg_ref, bg_ref, o_ref):
        o_ref[...] = _glu_fn(u_ref[...], y0_ref[...], y1_ref[...], z_ref[...], dsk_ref[...], wg_ref[...].astype(f32),
                             bg_ref[...])

    t = _tok(T, D_S5)
    return pl.pallas_call(
        body, name="s5_glu_fwd", grid=(B, L // T),
        in_specs=[t, t, t, t, _resident((1, D_S5)), _resident((D_S5, D_S5)), _resident((1, D_S5))],
        out_specs=t, out_shape=SDS((B, L, D_S5), f32), compiler_params=_cparams(2),
    )(u, y0, y1, z, dsk, wg, bg)


def s5_glu_bwd(u, y0, y1, z, dsk, wg, bg, dout):
    B, L, _ = u.shape
    T = min(TOK_TILE, L)

    def body(u_ref, y0_ref, y1_ref, z_ref, dsk_ref, wg_ref, bg_ref, do_ref, du_ref, dy_ref, dz_ref,
             ddsk_ref, dwg_ref, dbg_ref):
        @pl.when(_first_step())
        def _():
            for r in (ddsk_ref, dwg_ref, dbg_ref):
                r[...] = jnp.zeros_like(r)

        _, vjp = jax.vjp(_glu_fn, u_ref[...], y0_ref[...], y1_ref[...], z_ref[...], dsk_ref[...],
                         wg_ref[...].astype(f32), bg_ref[...])
        du, dy, _, dz, ddsk, dwg, dbg = vjp(do_ref[...])
        du_ref[...], dy_ref[...], dz_ref[...] = du, dy, dz
        ddsk_ref[...] += ddsk
        dwg_ref[...] += dwg
        dbg_ref[...] += dbg

    t = _tok(T, D_S5)
    small = [_resident((1, D_S5)), _resident((D_S5, D_S5)), _resident((1, D_S5))]
    return pl.pallas_call(
        body, name="s5_glu_bwd", grid=(B, L // T),
        in_specs=[t, t, t, t] + small + [t], out_specs=[t, t, t] + small,
        out_shape=[SDS((B, L, D_S5), f32)] * 3 + [SDS((1, D_S5), f32), SDS((D_S5, D_S5), f32), SDS((1, D_S5), f32)],
        compiler_params=_cparams(2),
    )(u, y0, y1, z, dsk, wg, bg, dout)


CONV_ROWS = 16


def _conv_taps(L, is_ctx):
    t = lax.broadcasted_iota(jnp.int32, (L, 1), 0)
    taps = []
    for di in ((1,) if is_ctx else (0, 1, 2)):
        for dj in (0, 1, 2):
            s = (0 if is_ctx else GRID_W * (di - 1)) + (dj - 1)
            if is_ctx:
                ok = jnp.logical_and(t + s >= 0, t + s < L)
            else:
                col = jnp.bitwise_and(t, GRID_W - 1) + (dj - 1)
                row = t + GRID_W * (di - 1)
                ok = jnp.logical_and(jnp.logical_and(col >= 0, col < GRID_W), jnp.logical_and(row >= 0, row < L))
            taps.append((di * 3 + dj, s, ok.astype(f32)))
    return taps


def _shift(x, s):
    L = x.shape[0]
    k = (-s) % L
    return x if k == 0 else pltpu.roll(x, k, axis=0)


def _qk_post(pre, is_norm, scale):
    s = _silu(pre)
    nrm = lax.rsqrt(jnp.sum(s * s, axis=-1, keepdims=True) + NORM_EPS)
    return s * jnp.where(is_norm, nrm * scale, 1.0)


def _conv_kind():
    ct = pl.program_id(1)
    return ct < 2 * GDN_HEADS, jnp.where(ct < GDN_HEADS, GDN_HEAD ** -0.5, 1.0).astype(f32)


def _conv_pre(xv, w_ref, taps):
    pre = jnp.zeros_like(xv)
    for r, s, m in taps:
        pre = pre + w_ref[r:r + 1, :] * (m * _shift(xv, s))
    return pre


def conv_fwd(qkv, w16, *, is_ctx, name):
    B, L, C = qkv.shape
    spec = pl.BlockSpec((None, L, GDN_HEAD), lambda b, ct: (b, 0, ct))
    wspec = pl.BlockSpec((CONV_ROWS, GDN_HEAD), lambda b, ct: (0, ct))

    def body(x_ref, w_ref, o_ref):
        is_norm, scale = _conv_kind()
        o_ref[...] = _qk_post(_conv_pre(x_ref[...], w_ref, _conv_taps(L, is_ctx)), is_norm, scale)

    return pl.pallas_call(body, name=name, grid=(B, C // GDN_HEAD), in_specs=[spec, wspec], out_specs=spec,
                          out_shape=SDS((B, L, C), f32), compiler_params=_cparams(2))(qkv, w16)


def conv_bwd(qkv, w16, da0, da1, *, is_ctx, name):
    B, L, C = qkv.shape
    spec = pl.BlockSpec((None, L, GDN_HEAD), lambda b, ct: (b, 0, ct))
    wspec = pl.BlockSpec((CONV_ROWS, GDN_HEAD), lambda b, ct: (0, ct))
    dwspec = pl.BlockSpec((None, CONV_ROWS, GDN_HEAD), lambda b, ct: (b, 0, ct))

    def body(x_ref, w_ref, d0_ref, d1_ref, dx_ref, dw_ref):
        is_norm, scale = _conv_kind()
        taps = _conv_taps(L, is_ctx)
        xv = x_ref[...]
        _, vjp = jax.vjp(lambda p: _qk_post(p, is_norm, scale), _conv_pre(xv, w_ref, taps))
        dpre = vjp(d0_ref[...] + d1_ref[...])[0]
        dx = jnp.zeros_like(xv)
        dw_ref[...] = jnp.zeros_like(dw_ref)
        for r, s, m in taps:
            md = m * dpre
            dx = dx + _shift(w_ref[r:r + 1, :] * md, -s)
            dw_ref[r:r + 1, :] = jnp.sum(md * _shift(xv, s), axis=0, keepdims=True)
        dx_ref[...] = dx

    return pl.pallas_call(body, name=name, grid=(B, C // GDN_HEAD), in_specs=[spec, wspec, spec, spec],
                          out_specs=[spec, dwspec], out_shape=[SDS((B, L, C), f32), SDS((B, CONV_ROWS, C), f32)],
                          compiler_params=_cparams(2))(qkv, w16, da0, da1)


def _gates_fn(ba, alog, dtb):
    T = ba.shape[0]
    lane = lax.broadcasted_iota(jnp.int32, ba.shape, 1)
    ii = lax.broadcasted_iota(jnp.int32, (T, T), 0)
    jj = lax.broadcasted_iota(jnp.int32, (T, T), 1)
    same = jnp.right_shift(ii, 6) == jnp.right_shift(jj, 6)
    lmat = jnp.logical_and(same, ii >= jj).astype(f32)
    umat = jnp.logical_and(same, ii <= jj).astype(f32)
    g = jnp.where(lane >= 8, -jnp.exp(alog) * jax.nn.softplus(ba + dtb), 0.0)
    gc = jnp.where(lane >= 12, _dot_hi(umat, g), _dot_hi(lmat, g))
    return jnp.where(lane < 8, jax.nn.sigmoid(ba), gc)


def gates_fwd(ba, alog, dtb, *, name):
    B, L, _ = ba.shape
    T = min(TOK_TILE, L)
    t = _tok(T, N_GATE)

    def body(ba_ref, al_ref, dt_ref, o_ref):
        o_ref[...] = _gates_fn(ba_ref[...], al_ref[...], dt_ref[...])

    return pl.pallas_call(body, name=name, grid=(B, L // T),
                          in_specs=[t, _resident((1, N_GATE)), _resident((1, N_GATE))], out_specs=t,
                          out_shape=SDS((B, L, N_GATE), f32), compiler_params=_cparams(2))(ba, alog, dtb)


def gates_bwd(ba, alog, dtb, dbg, *, name):
    B, L, _ = ba.shape
    T = min(TOK_TILE, L)
    t = _tok(T, N_GATE)
    small = _resident((1, N_GATE))

    def body(ba_ref, al_ref, dt_ref, d_ref, dba_ref, dal_ref, ddt_ref):
        @pl.when(_first_step())
        def _():
            dal_ref[...] = jnp.zeros_like(dal_ref)
            ddt_ref[...] = jnp.zeros_like(ddt_ref)

        _, vjp = jax.vjp(_gates_fn, ba_ref[...], al_ref[...], dt_ref[...])
        dba, dal, ddt = vjp(d_ref[...])
        dba_ref[...] = dba
        dal_ref[...] += dal
        ddt_ref[...] += ddt

    return pl.pallas_call(body, name=name, grid=(B, L // T), in_specs=[t, small, small, t],
                          out_specs=[t, small, small],
                          out_shape=[SDS((B, L, N_GATE), f32), SDS((1, N_GATE), f32), SDS((1, N_GATE), f32)],
                          compiler_params=_cparams(2))(ba, alog, dtb, dbg)


@jax.custom_vjp
def _inv_unit_tri(mats):
    n = mats[0].shape[0]
    eye = (lax.broadcasted_iota(jnp.int32, (n, n), 0) == lax.broadcasted_iota(jnp.int32, (n, n), 1)).astype(f32)
    xs = [eye - a for a in mats]
    sq = [_dot(a, a) for a in mats]
    ps = sq
    k = 2
    while k < n:
        xs = [x + _dot(x, p) for x, p in zip(xs, ps)]
        k *= 2
        if k < n:
            ps = [_dot(p, p) for p in ps]
    return tuple(_dot(p, x) - a for p, x, a in zip(sq, xs, mats))


def _inv_unit_tri_fwd(mats):
    ns = _inv_unit_tri(mats)
    return ns, ns


def _inv_unit_tri_bwd(ns, dns):
    ys = [dn + _dot_tn(nn, dn) for nn, dn in zip(ns, dns)]
    return (tuple(-(y + _dot_nt(y, nn)) for y, nn in zip(ys, ns)),)


_inv_unit_tri.defvjp(_inv_unit_tri_fwd, _inv_unit_tri_bwd)


def _gdn_chunk(heads, *, revs):
    n = heads[0][0].shape[0]
    ii = lax.broadcasted_iota(jnp.int32, (n, n), 0)
    jj = lax.broadcasted_iota(jnp.int32, (n, n), 1)
    row = lax.broadcasted_iota(jnp.int32, (n, 1), 0)
    lower = {False: ii >= jj, True: ii <= jj}
    strict = {False: ii > jj, True: ii < jj}
    last = {False: n - 1, True: 0}
    H = range(len(heads))
    q, k, v, beta, gc, gr, s = (list(t) for t in zip(*heads))
    decay = [jnp.where(lower[revs[h]], jnp.exp(jnp.where(lower[revs[h]], gc[h] - gr[h], 0.0)), 0.0) for h in H]
    kk = [_mm_nt(k[h], k[h]) for h in H]
    qk = [_mm_nt(q[h], k[h]) * decay[h] for h in H]
    qs = [_mm(q[h], s[h]) for h in H]
    a_mat = tuple(jnp.where(strict[revs[h]], beta[h] * kk[h] * decay[h], 0.0) for h in H)
    gamma = [jnp.exp(gc[h]) for h in H]
    g_last = [jnp.sum(jnp.where(row == last[revs[h]], gc[h], 0.0), axis=0, keepdims=True) for h in H]
    nmat = _inv_unit_tri(a_mat)
    bv = [beta[h] * v[h] for h in H]
    bk = [(beta[h] * gamma[h]) * k[h] for h in H]
    u0 = [bv[h] + _mm(nmat[h], bv[h]) for h in H]
    w = [bk[h] + _mm(nmat[h], bk[h]) for h in H]
    k_out = [k[h] * jnp.exp(g_last[h] - gc[h]) for h in H]
    u = [u0[h] - _mm(w[h], s[h]) for h in H]
    o = [gamma[h] * qs[h] + _mm(qk[h], u[h]) for h in H]
    s_new = [jnp.exp(g_last[h]) * s[h] + _mm_tn(k_out[h], u[h]) for h in H]
    return tuple((o[h], s_new[h]) for h in H)


def _gdn_specs(B, nc, rev):
    def cidx(n):
        return (nc - 1 - n) if rev else n
    tok = lambda width: pl.BlockSpec((B, CHUNK, width), lambda n: (0, cidx(n), 0))
    rowspec = pl.BlockSpec((B, None, N_GATE, CHUNK), lambda n: (0, cidx(n), 0, 0))
    st = pl.BlockSpec((B, GDN_HEADS, GDN_HEAD, GDN_HEAD), lambda n: (0, 0, 0, 0))
    ck = pl.BlockSpec((B, None, GDN_HEADS, GDN_HEAD, GDN_HEAD), lambda n: (0, cidx(n), 0, 0, 0))
    return tok, rowspec, st, ck


def _gdn_head_args(qkv_ref, bg_ref, bgr_ref, b, d, h):
    col = d * GDN_HEADS + h
    q = qkv_ref[b, :, h * GDN_HEAD:(h + 1) * GDN_HEAD]
    k = qkv_ref[b, :, D_GDN + h * GDN_HEAD:D_GDN + (h + 1) * GDN_HEAD]
    v = qkv_ref[b, :, 2 * D_GDN + h * GDN_HEAD:2 * D_GDN + (h + 1) * GDN_HEAD]
    bgv = bg_ref[b]
    return q, k, v, bgv[:, col:col + 1], bgv[:, 8 + col:9 + col], bgr_ref[b][8 + col:9 + col, :]


def _gdn_chains(B):
    return [(d, b, h) for d in range(N_DIR) for b in range(B) for h in range(GDN_HEADS)]


def gdn_fwd(qkv, bg, bgr, s0s, *, need_o, name):
    B, L, _ = qkv.shape
    nc = L // CHUNK
    specs = [_gdn_specs(B, nc, d == 1) for d in range(N_DIR)]
    chains = _gdn_chains(B)
    state_shape = (B, GDN_HEADS, GDN_HEAD, GDN_HEAD)

    def body(*refs):
        ins = [refs[3 * d:3 * d + 3] for d in range(N_DIR)]
        s0_refs = refs[6:8]
        k = 8
        o_refs = refs[k:k + 2] if need_o else None
        k += 2 if need_o else 0
        ck_refs, sf_refs, s_scrs = refs[k:k + 2], refs[k + 2:k + 4], refs[k + 4:k + 6]
        n = pl.program_id(0)

        @pl.when(n == 0)
        def _():
            for d in range(N_DIR):
                s_scrs[d][...] = s0_refs[d][...]

        for d in range(N_DIR):
            ck_refs[d][...] = s_scrs[d][...]
        heads = tuple(_gdn_head_args(*ins[d], b, d, h) + (s_scrs[d][b, h],) for d, b, h in chains)
        outs = _gdn_chunk(heads, revs=tuple(d == 1 for d, _, _ in chains))
        for (d, b, h), (o, s_new) in zip(chains, outs):
            if need_o:
                o_refs[d][b, :, h * GDN_HEAD:(h + 1) * GDN_HEAD] = o
            s_scrs[d][b, h] = s_new

        @pl.when(n == nc - 1)
        def _():
            for d in range(N_DIR):
                sf_refs[d][...] = s_scrs[d][...]

    in_specs, out_o, out_ck, out_sf = [], [], [], []
    for tok, rowspec, st, ck in specs:
        in_specs += [tok(3 * D_GDN), tok(N_GATE), rowspec]
        out_o.append(tok(D_GDN))
        out_ck.append(ck)
        out_sf.append(st)
    in_specs += [specs[0][2]] * 2
    out_specs = (out_o if need_o else []) + out_ck + out_sf
    out_shape = (([SDS((B, L, D_GDN), f32)] * 2 if need_o else [])
                 + [SDS((B, nc) + state_shape[1:], f32)] * 2 + [SDS(state_shape, f32)] * 2)
    res = pl.pallas_call(
        body, name=name, grid=(nc,), in_specs=in_specs, out_specs=out_specs, out_shape=out_shape,
        scratch_shapes=[pltpu.VMEM(state_shape, f32)] * 2, compiler_params=_cparams(1),
    )(qkv, bg, bgr, qkv, bg, bgr, *s0s)
    if need_o:
        return res[0:2], res[2:4], res[4:6]
    return res[0:2], res[2:4]


def gdn_bwd(qkv, bg, bgr, cks, do, dsfs, *, name):
    B, L, _ = qkv.shape
    nc = L // CHUNK
    has_do = do is not None
    specs = [_gdn_specs(B, nc, d != 1) for d in range(N_DIR)]
    chains = _gdn_chains(B)
    state_shape = (B, GDN_HEADS, GDN_HEAD, GDN_HEAD)
    per_dir = 5 if has_do else 4

    def body(*refs):
        ins = [refs[per_dir * d:per_dir * d + per_dir] for d in range(N_DIR)]
        k = per_dir * N_DIR
        dsf_refs = refs[k:k + 2]
        outs = [refs[k + 2 + 3 * d:k + 5 + 3 * d] for d in range(N_DIR)]
        ds0_refs, ds_scrs = refs[k + 8:k + 10], refs[k + 10:k + 12]
        n = pl.program_id(0)

        @pl.when(n == 0)
        def _():
            for d in range(N_DIR):
                ds_scrs[d][...] = dsf_refs[d][...]

        lane = lax.broadcasted_iota(jnp.int32, (CHUNK, N_GATE), 1)
        sub = lax.broadcasted_iota(jnp.int32, (N_GATE, CHUNK), 0)
        heads = tuple(_gdn_head_args(*ins[d][:3], b, d, h) + (ins[d][3][b, h],) for d, b, h in chains)
        _, vjp = jax.vjp(functools.partial(_gdn_chunk, revs=tuple(d == 1 for d, _, _ in chains)), heads)
        zero = jnp.zeros((CHUNK, GDN_HEAD), f32)
        cts = tuple(((ins[d][4][b, :, h * GDN_HEAD:(h + 1) * GDN_HEAD] if has_do else zero), ds_scrs[d][b, h])
                    for d, b, h in chains)
        (dheads,) = vjp(cts)
        dbg_acc = [[jnp.zeros((CHUNK, N_GATE), f32) for _ in range(B)] for _ in range(N_DIR)]
        dbgr_acc = [[jnp.zeros((N_GATE, CHUNK), f32) for _ in range(B)] for _ in range(N_DIR)]
        for (d, b, h), (dq, dk, dv, db, dgc, dgr, ds) in zip(chains, dheads):
            col = d * GDN_HEADS + h
            dqkv_ref = outs[d][0]
            dqkv_ref[b, :, h * GDN_HEAD:(h + 1) * GDN_HEAD] = dq
            dqkv_ref[b, :, D_GDN + h * GDN_HEAD:D_GDN + (h + 1) * GDN_HEAD] = dk
            dqkv_ref[b, :, 2 * D_GDN + h * GDN_HEAD:2 * D_GDN + (h + 1) * GDN_HEAD] = dv
            dbg_acc[d][b] = dbg_acc[d][b] + jnp.where(lane == col, db, 0.0) + jnp.where(lane == 8 + col, dgc, 0.0)
            dbgr_acc[d][b] = dbgr_acc[d][b] + jnp.where(sub == 8 + col, dgr, 0.0)
            ds_scrs[d][b, h] = ds
        for d in range(N_DIR):
            for b in range(B):
                outs[d][1][b] = dbg_acc[d][b]
                outs[d][2][b] = dbgr_acc[d][b]

        @pl.when(n == nc - 1)
        def _():
            for d in range(N_DIR):
                ds0_refs[d][...] = ds_scrs[d][...]

    in_specs, args, out_specs, out_shape = [], [], [], []
    for d, (tok, rowspec, st, ck) in enumerate(specs):
        in_specs += [tok(3 * D_GDN), tok(N_GATE), rowspec, ck] + ([tok(D_GDN)] if has_do else [])
        args += [qkv, bg, bgr, cks[d]] + ([do] if has_do else [])
        out_specs += [tok(3 * D_GDN), tok(N_GATE), rowspec]
        out_shape += [SDS((B, L, 3 * D_GDN), f32), SDS((B, L, N_GATE), f32), SDS((B, nc, N_GATE, CHUNK), f32)]
    st = specs[0][2]
    in_specs += [st, st]
    args += list(dsfs)
    out_specs += [st, st]
    out_shape += [SDS(state_shape, f32)] * 2
    res = pl.pallas_call(
        body, name=name, grid=(nc,), in_specs=in_specs, out_specs=out_specs, out_shape=out_shape,
        scratch_shapes=[pltpu.VMEM(state_shape, f32)] * 2, compiler_params=_cparams(1),
    )(*args)
    return (res[0], res[3]), (res[1], res[4]), (res[2], res[5]), (res[6], res[7])


def _gnorm_fn(o0, o1, z, w):
    o = o0 + o1
    return o * lax.rsqrt(jnp.mean(o * o, axis=-1, keepdims=True) + NORM_EPS) * w * _silu(z)


def gnorm_fwd(o0, o1, z, w):
    B, L, _ = o0.shape
    T = min(TOK_TILE, L)
    t = _tok(T, D_GDN)

    def body(o0_ref, o1_ref, z_ref, w_ref, out_ref):
        for h in range(GDN_HEADS):
            sl = slice(h * GDN_HEAD, (h + 1) * GDN_HEAD)
            out_ref[:, sl] = _gnorm_fn(o0_ref[:, sl], o1_ref[:, sl], z_ref[:, sl], w_ref[...])

    return pl.pallas_call(body, name="gnorm_fwd", grid=(B, L // T), in_specs=[t, t, t, _resident((1, GDN_HEAD))],
                          out_specs=t, out_shape=SDS((B, L, D_GDN), f32), compiler_params=_cparams(2))(o0, o1, z, w)


def gnorm_bwd(o0, o1, z, w, dout):
    B, L, _ = o0.shape
    T = min(TOK_TILE, L)
    t = _tok(T, D_GDN)

    def body(o0_ref, o1_ref, z_ref, w_ref, d_ref, do_ref, dz_ref, dw_ref):
        @pl.when(_first_step())
        def _():
            dw_ref[...] = jnp.zeros_like(dw_ref)

        for h in range(GDN_HEADS):
            sl = slice(h * GDN_HEAD, (h + 1) * GDN_HEAD)
            _, vjp = jax.vjp(_gnorm_fn, o0_ref[:, sl], o1_ref[:, sl], z_ref[:, sl], w_ref[...])
            do, _, dz, dw = vjp(d_ref[:, sl])
            do_ref[:, sl] = do
            dz_ref[:, sl] = dz
            dw_ref[...] += dw

    return pl.pallas_call(body, name="gnorm_bwd", grid=(B, L // T),
                          in_specs=[t, t, t, _resident((1, GDN_HEAD)), t], out_specs=[t, t, _resident((1, GDN_HEAD))],
                          out_shape=[SDS((B, L, D_GDN), f32), SDS((B, L, D_GDN), f32), SDS((1, GDN_HEAD), f32)],
                          compiler_params=_cparams(2))(o0, o1, z, w, dout)


def _head_loss(y, x, gate, lng, lnb, tgt):
    r = DEEPNORM_ALPHA * x + gate * y
    mu = jnp.mean(r, axis=-1, keepdims=True)
    rc = r - mu
    var = jnp.mean(rc * rc, axis=-1, keepdims=True)
    err = rc * lax.rsqrt(var + LN_EPS) * lng + lnb - tgt
    return (0.5 / D_MODEL) * jnp.sum(jnp.sum(err * err, axis=-1, keepdims=True), axis=0, keepdims=True)


def head_fwd_bwd(s5o, gdo, x, tgt, gate, lng, lnb, ws, wg):
    B, L, _ = x.shape
    T = min(TOK_TILE, L)

    def body(s_ref, g_ref, x_ref, t_ref, gate_ref, lng_ref, lnb_ref, ws_ref, wg_ref,
             loss_ref, ds_ref, dg_ref, gx_ref, dws_ref, dwg_ref, dgate_ref, dlng_ref, dlnb_ref):
        n = pl.program_id(1)

        @pl.when(_first_step())
        def _():
            for r in (dws_ref, dwg_ref, dlng_ref, dlnb_ref):
                r[...] = jnp.zeros_like(r)

        @pl.when(n == 0)
        def _():
            loss_ref[...] = jnp.zeros_like(loss_ref)
            dgate_ref[...] = jnp.zeros_like(dgate_ref)

        sv = s_ref[...].astype(bf16)
        gv = g_ref[...].astype(bf16)
        y = _dot(sv, ws_ref[...]) + _dot(gv, wg_ref[...])
        loss, vjp = jax.vjp(lambda *a: _head_loss(*a, t_ref[...]), y, x_ref[...], gate_ref[...], lng_ref[...],
                            lnb_ref[...])
        dy, dx, dgate, dlng, dlnb = vjp(jnp.ones((1, 1), f32))
        loss_ref[...] += jnp.broadcast_to(loss, loss_ref.shape)
        dyb = dy.astype(bf16)
        ds_ref[...] = _dot_nt(dyb, ws_ref[...])
        dg_ref[...] = _dot_nt(dyb, wg_ref[...])
        gx_ref[...] = dx
        dws_ref[...] += _dot_tn(sv, dyb)
        dwg_ref[...] += _dot_tn(gv, dyb)
        dgate_ref[...] += dgate
        dlng_ref[...] += dlng
        dlnb_ref[...] += dlnb

    half, full = _tok(T, D_S5), _tok(T, D_MODEL)
    row = _resident((1, D_MODEL))
    wsp = _resident((D_S5, D_MODEL))
    return pl.pallas_call(
        body, name="head_fwd_bwd", grid=(B, L // T),
        in_specs=[half, half, full, full, _per_batch(1, D_MODEL), row, row, wsp, wsp],
        out_specs=[_per_batch(8, LANES), half, half, full, wsp, wsp, _per_batch(1, D_MODEL), row, row],
        out_shape=[SDS((B, 8, LANES), f32), SDS((B, L, D_S5), f32), SDS((B, L, D_GDN), f32), SDS((B, L, D_MODEL), f32),
                   SDS((D_S5, D_MODEL), f32), SDS((D_GDN, D_MODEL), f32), SDS((B, 1, D_MODEL), f32),
                   SDS((1, D_MODEL), f32), SDS((1, D_MODEL), f32)],
        compiler_params=_cparams(2),
    )(s5o, gdo, x, tgt, gate, lng, lnb, ws, wg)


def _adamw_math(w, g, m, v):
    nm = ADAM_B1 * m + (1.0 - ADAM_B1) * g
    nv = ADAM_B2 * v + (1.0 - ADAM_B2) * jnp.square(g)
    m_hat = nm / (1.0 - ADAM_B1 ** ADAM_STEP)
    v_hat = nv / (1.0 - ADAM_B2 ** ADAM_STEP)
    return -ADAM_LR * (m_hat / (jnp.sqrt(v_hat) + ADAM_EPS) + ADAM_WD * w), nm, nv


def _row_tile(rows, cap=512):
    for t in range(min(cap, rows), 15, -1):
        if rows % t == 0 and t % 16 == 0:
            return t
    return rows


def adamw_3d(w, g, m, v, *, lead=False, name):
    R, C = (w.shape[0], w.shape[2]) if lead else w.shape[1:]
    if lead:
        T = next(t for t in range(min(256, R), 0, -1) if R % t == 0)
        spec = pl.BlockSpec((T, 1, C), lambda i: (i, 0, 0))
    else:
        T = _row_tile(R)
        spec = pl.BlockSpec((None, T, C), lambda i: (0, i, 0))

    def body(w_ref, g_ref, m_ref, v_ref, d_ref, nm_ref, nv_ref):
        d_ref[...], nm_ref[...], nv_ref[...] = _adamw_math(w_ref[...], g_ref[...], m_ref[...], v_ref[...])

    return pl.pallas_call(body, name=name, grid=(R // T,), in_specs=[spec] * 4, out_specs=[spec] * 3,
                          out_shape=[SDS(w.shape, f32)] * 3, compiler_params=_cparams(1))(w, g, m, v)


def adamw_small(ws, gs, ms, vs):
    n = len(ws)

    def body(*refs):
        outs = refs[4 * n:]
        for i in range(n):
            d, nm, nv = _adamw_math(refs[i][...], refs[n + i][...], refs[2 * n + i][...], refs[3 * n + i][...])
            outs[i][...], outs[n + i][...], outs[2 * n + i][...] = d, nm, nv

    res = pl.pallas_call(body, name="adamw_small", out_shape=[SDS(w.shape, f32) for w in ws] * 3,
                         compiler_params=pltpu.CompilerParams(vmem_limit_bytes=VMEM_LIMIT))(*ws, *gs, *ms, *vs)
    return res[:n], res[n:2 * n], res[2 * n:]


def sum_cores(own, got, *, name):
    A, H, C = own.shape
    T = _row_tile(H)
    spec = pl.BlockSpec((None, T, C), lambda a, i: (a, i, 0))

    def body(a_ref, b_ref, q32_ref, q16_ref):
        q = a_ref[...] + b_ref[...]
        q32_ref[...] = q
        q16_ref[...] = q.astype(bf16)

    return pl.pallas_call(body, name=name, grid=(A, H // T), in_specs=[spec, spec], out_specs=[spec, spec],
                          out_shape=[SDS((A, H, C), f32), SDS((A, H, C), bf16)], compiler_params=_cparams(2))(own, got)


def sum_chips(mine, rec, cpos, *, name):
    H, C = mine.shape
    T = _row_tile(H)
    nt = H // T

    def body(c_ref, m_ref, r_ref, f_ref):
        f_ref[...] = ((m_ref[...] + r_ref[0].astype(f32)) + r_ref[1].astype(f32)) + r_ref[2].astype(f32)

    grid_spec = pltpu.PrefetchScalarGridSpec(
        num_scalar_prefetch=1, grid=(nt,),
        in_specs=[pl.BlockSpec((T, C), lambda i, c_ref: (i, 0)), pl.BlockSpec((3, T, C), lambda i, c_ref: (0, i, 0))],
        out_specs=pl.BlockSpec((None, T, C), lambda i, c_ref: (0, c_ref[0] * nt + i, 0)))
    return pl.pallas_call(body, name=name, grid_spec=grid_spec, out_shape=SDS((1, 2 * H, C), f32),
                          compiler_params=_cparams(1))(cpos.reshape(1).astype(jnp.int32), mine, rec)


CHIP_FLIPS = ((1, 0), (0, 1), (1, 1))


def _pos():
    return lax.axis_index("x"), lax.axis_index("y"), lax.axis_index("c")


def _comm_call(body, srcs, out_sds, n_remote, n_local, name):
    any_spec = pl.BlockSpec(memory_space=pl.ANY)
    return pl.pallas_call(
        body, name=name, in_specs=[any_spec] * len(srcs), out_specs=[any_spec] * len(out_sds), out_shape=out_sds,
        scratch_shapes=[pltpu.SemaphoreType.DMA((n_remote,)), pltpu.SemaphoreType.DMA((n_remote,)),
                        pltpu.SemaphoreType.DMA((max(n_local, 1),))],
        compiler_params=pltpu.CompilerParams(has_side_effects=True),
    )(*srcs)


def _remote(src, dst, send_sems, recv_sems, k, target):
    return pltpu.make_async_remote_copy(src, dst, send_sems.at[k], recv_sems.at[k], device_id=target,
                                        device_id_type=MESH)


def _half_rows(c, rows):
    half = rows // 2
    return pl.ds(pl.multiple_of(c * half, 8), half)


def gather_shards(shards):
    nt = len(shards)

    def body(*refs):
        srcs, outs = refs[:nt], refs[nt:2 * nt]
        send_sems, recv_sems, _ = refs[2 * nt:]
        x, y, c = _pos()
        j = 2 * x + y
        sib = (x, y, 1 - c)
        own = [_remote(srcs[t], outs[t].at[j], send_sems, recv_sems, 7 * t + 6, sib) for t in range(nt)]
        first, passed = [], []
        for k, (fx, fy) in enumerate(CHIP_FLIPS):
            tx, ty = x ^ fx, y ^ fy
            jk = 2 * tx + ty
            for t in range(nt):
                rows = _half_rows(c, srcs[t].shape[0])
                first.append(_remote(srcs[t].at[rows], outs[t].at[j, rows], send_sems, recv_sems, 7 * t + k, (tx, ty, c)))
                passed.append(_remote(outs[t].at[jk, rows], outs[t].at[jk, rows], send_sems, recv_sems, 7 * t + 3 + k, sib))
        for cp in first + own:
            cp.start()
        for a, b in zip(first, passed):
            a.wait_recv()
            b.start()
        for cp in passed + own:
            cp.wait_recv()
        for cp in first + passed + own:
            cp.wait_send()

    return _comm_call(body, shards, [SDS((4,) + s.shape, s.dtype) for s in shards], 7 * nt, 0, "gather_shards")


def swap_halves(ps):
    nt = len(ps)

    def body(*refs):
        srcs, outs = refs[:nt], refs[nt:2 * nt]
        send_sems, recv_sems, _ = refs[2 * nt:]
        x, y, c = _pos()
        cps = [_remote(srcs[t].at[a, _half_rows(1 - c, srcs[t].shape[1])], outs[t].at[a], send_sems, recv_sems, 4 * t + a,
                       (x, y, 1 - c)) for t in range(nt) for a in range(4)]
        for cp in cps:
            cp.start()
        for cp in cps:
            cp.wait()

    return _comm_call(body, ps, [SDS((4, p.shape[1] // 2, p.shape[2]), p.dtype) for p in ps], 4 * nt, 0, "swap_halves")


def scatter_to_chips(qs):
    nt = len(qs)

    def body(*refs):
        srcs, outs = refs[:nt], refs[nt:2 * nt]
        send_sems, recv_sems, _ = refs[2 * nt:]
        x, y, c = _pos()
        cps = []
        for k, (fx, fy) in enumerate(CHIP_FLIPS):
            tx, ty = x ^ fx, y ^ fy
            for t in range(nt):
                cps.append(_remote(srcs[t].at[2 * tx + ty], outs[t].at[k], send_sems, recv_sems, 3 * t + k, (tx, ty, c)))
        for cp in cps:
            cp.start()
        for cp in cps:
            cp.wait()

    return _comm_call(body, qs, [SDS((3,) + q.shape[1:], q.dtype) for q in qs], 3 * nt, 0, "scatter_to_chips")


def join_halves(fs):
    nt = len(fs)

    def body(*refs):
        outs = refs[nt:2 * nt]
        send_sems, recv_sems, _ = refs[2 * nt:]
        x, y, c = _pos()
        cps = []
        for t in range(nt):
            mine = outs[t].at[0, _half_rows(c, outs[t].shape[1])]
            cps.append(_remote(mine, mine, send_sems, recv_sems, t, (x, y, 1 - c)))
        for cp in cps:
            cp.start()
        for cp in cps:
            cp.wait()

    any_spec = pl.BlockSpec(memory_space=pl.ANY)
    return pl.pallas_call(
        body, name="join_halves", in_specs=[any_spec] * nt, out_specs=[any_spec] * nt,
        out_shape=[SDS(f.shape, f.dtype) for f in fs], input_output_aliases={t: t for t in range(nt)},
        scratch_shapes=[pltpu.SemaphoreType.DMA((nt,)), pltpu.SemaphoreType.DMA((nt,)), pltpu.SemaphoreType.DMA((1,))],
        compiler_params=pltpu.CompilerParams(has_side_effects=True),
    )(*fs)


def gather_small(s):
    def body(src, out, send_sems, recv_sems, _):
        x, y, c = _pos()
        j = 2 * x + y
        cps = [_remote(src, out.at[j], send_sems, recv_sems, k, (x ^ fx, y ^ fy, c)) for k, (fx, fy) in enumerate(CHIP_FLIPS)]
        cps.append(_remote(src, out.at[j], send_sems, recv_sems, 3, (x, y, 1 - c)))
        for cp in cps:
            cp.start()
        for cp in cps:
            cp.wait()

    return _comm_call(body, [s], [SDS((4,) + s.shape, s.dtype)], 4, 0, "gather_small")[0]


SMALL_SHAPES = ((D_MODEL,), (1, 3 * D_MODEL), (1, 2, 32, 64), (1, 2, 32, 64), (1, 2, 32), (1, 2, 32, 16, 64),
                (1, 2, 32, 16, 64), (1, 2, 32, 16, 64), (1, 2, 32, 16, 64), (1, D_S5), (1, D_S5), (1, 2, 4), (1, 2, 4),
                (1, GDN_HEAD), (1, D_MODEL), (1, D_MODEL))
SMALL_SWAPPED = (5, 6)


def _size(shape):
    return functools.reduce(lambda p, q: p * q, shape)


SMALL_ROWS = tuple(-(-_size(s) // LANES) for s in SMALL_SHAPES)
SMALL_TOTAL = 2176
SMALL_QUARTER = SMALL_TOTAL // 4


def _rows(a):
    flat = a.reshape(-1)
    pad = (-flat.shape[0]) % LANES
    if pad:
        flat = jnp.concatenate([flat, jnp.zeros((pad,), flat.dtype)])
    return flat.reshape(-1, LANES)


def _pack_small(parts):
    rows = [_rows(p) for p in parts]
    rows.append(jnp.zeros((SMALL_TOTAL - sum(SMALL_ROWS), LANES), f32))
    return jnp.concatenate(rows, axis=0)


def _unpack_small(buf):
    out, r = [], 0
    for s, n in zip(SMALL_SHAPES, SMALL_ROWS):
        out.append(buf[r:r + n].reshape(-1)[:_size(s)].reshape(s))
        r += n
    return out


def _as_2d(a):
    return a.reshape(1, -1) if a.ndim == 1 else a.reshape(-1, a.shape[-1])


S5_BG = S5_GROUPS // S5_BLOCKS


def _block_diag_in(bb):
    eye = jnp.eye(S5_BG, dtype=bb.dtype)
    b4 = bb.reshape(S5_BLOCKS, S5_BG, S5_GROUP, S5_STATE)
    return jnp.einsum('jgcp,gh->jgchp', b4, eye).reshape(S5_BLOCKS, S5_BC, S5_BS)


def _block_diag_in_t(d):
    d6 = d.reshape(S5_BLOCKS, S5_BG, S5_GROUP, S5_BG, S5_STATE)
    return jnp.einsum('jgcgp->jgcp', d6).reshape(S5_GROUPS, S5_GROUP * S5_STATE)


def _block_diag_out(cm):
    eye = jnp.eye(S5_BG, dtype=cm.dtype)
    c4 = cm.reshape(S5_BLOCKS, S5_BG, S5_GROUP, S5_STATE)
    return jnp.einsum('jgcp,gh->jhpgc', c4, eye).reshape(S5_BLOCKS, S5_BS, S5_BC)


def _block_diag_out_t(d):
    d6 = d.reshape(S5_BLOCKS, S5_BG, S5_STATE, S5_BG, S5_GROUP)
    return jnp.einsum('jgpgc->jgcp', d6).reshape(S5_GROUPS, S5_GROUP, S5_STATE)


def _to_chunk_rows(a):
    B, L, W = a.shape
    return a.reshape(B, L // CHUNK, CHUNK, W).transpose(0, 1, 3, 2)


def _from_chunk_rows(a):
    B, nc, W, _ = a.shape
    return a.transpose(0, 1, 3, 2).reshape(B, nc * CHUNK, W)


def local_step(x, c, ctx, c_ctx, tgt, w_ada, b_ada, w_in, lam_re, lam_im, log_dt, b_re, b_im, c_re, c_im, s5_d,
               w_glu, b_glu, conv16, a_log, dt_bias, norm_w, w_out, ln_g, ln_b):
    B, L, _ = x.shape
    zeros_state = jnp.zeros((B, GDN_HEADS, GDN_HEAD, GDN_HEAD), f32)

    cc = jnp.concatenate([c, c_ctx[None, :], jnp.zeros((8 - B - 1, D_MODEL), f32)], axis=0)
    m = ada_fwd(cc, w_ada, b_ada)
    shift, scale, gate = m[:B, :D_MODEL], m[:B, D_MODEL:2 * D_MODEL], m[:B, 2 * D_MODEL:]
    mod = jnp.stack([scale, shift], axis=1)
    mod_c = jnp.broadcast_to(jnp.stack([m[B, D_MODEL:2 * D_MODEL], m[B, :D_MODEL]], axis=0)[None], (B, 2, D_MODEL))

    u, z_s5, qkv, z_gdn, ba = in_proj_fwd(x, mod, w_in, name="in_proj_fwd")
    uc, _, qkvc, _, bac = in_proj_fwd(ctx, mod_c, w_in, name="in_proj_fwd_ctx")

    ng = N_DIR * S5_GROUPS
    zoh_in = (lam_re.reshape(ng, S5_STATE), lam_im.reshape(ng, S5_STATE), log_dt.reshape(ng, 1),
              b_re.reshape(ng, S5_GROUP * S5_STATE), b_im.reshape(ng, S5_GROUP * S5_STATE))
    expand = (jnp.arange(S5_GROUP * S5_STATE)[None, :] % S5_STATE == jnp.arange(S5_STATE)[:, None]).astype(f32)
    ar, ai, bbr, bbi = s5_zoh_fwd(*zoh_in, expand)
    bbr16, bbi16 = bbr.astype(bf16), bbi.astype(bf16)
    c_re16 = c_re.reshape(N_DIR, S5_GROUPS, S5_GROUP, S5_STATE).astype(bf16)
    c_im16 = (-c_im).reshape(N_DIR, S5_GROUPS, S5_GROUP, S5_STATE).astype(bf16)
    s5w, ys, hins, hins_c = [], [], [], []
    for d in range(N_DIR):
        g = slice(d * S5_GROUPS, (d + 1) * S5_GROUPS)
        wd = (_block_diag_in(bbr16[g]), _block_diag_in(bbi16[g]), _block_diag_out(c_re16[d]), _block_diag_out(c_im16[d]),
              jnp.stack([ar[g].reshape(-1), ai[g].reshape(-1)], axis=0))
        s5w.append(wd)
        hin_c, hend_c = s5_scan_fwd(uc, *wd, jnp.zeros((B, 2, S5_HALF), f32), d=d, need_y=False, name=f"s5_fwd_ctx{d}")
        y_d, hin, _ = s5_scan_fwd(u, *wd, hend_c, d=d, need_y=True, name=f"s5_fwd{d}")
        ys.append(y_d)
        hins.append(hin)
        hins_c.append(hin_c)
    glu_w = (s5_d.reshape(1, D_S5), w_glu, b_glu.reshape(1, D_S5))
    s5o = s5_glu_fwd(u, ys[0], ys[1], z_s5, *glu_w)

    act = conv_fwd(qkv, conv16, is_ctx=False, name="conv_fwd")
    act_c = conv_fwd(qkvc, conv16, is_ctx=True, name="conv_fwd_ctx")
    pad8 = jnp.zeros((1, 8), f32)
    alog16 = jnp.concatenate([pad8, a_log.reshape(1, 8)], axis=1)
    dtb16 = jnp.concatenate([pad8, dt_bias.reshape(1, 8)], axis=1)
    bg = gates_fwd(ba, alog16, dtb16, name="gates_fwd")
    bg_c = gates_fwd(bac, alog16, dtb16, name="gates_fwd_ctx")
    bgr, bgr_c = _to_chunk_rows(bg), _to_chunk_rows(bg_c)
    cks_c, s_c = gdn_fwd(act_c, bg_c, bgr_c, (zeros_state, zeros_state), need_o=False, name="gdn_fwd_ctx")
    os_, cks, _ = gdn_fwd(act, bg, bgr, s_c, need_o=True, name="gdn_fwd")
    nw = norm_w.reshape(1, GDN_HEAD)
    gdo = gnorm_fwd(os_[0], os_[1], z_gdn, nw)

    loss8, ds5o, dgdo, gx_res, dws, dwg, dgate, dlng, dlnb = head_fwd_bwd(
        s5o, gdo, x, tgt, gate[:, None, :], ln_g.reshape(1, D_MODEL), ln_b.reshape(1, D_MODEL), w_out[:D_S5], w_out[D_S5:])
    loss = jnp.sum(loss8[:, 0, 0])
    d_w_out = jnp.concatenate([dws, dwg], axis=0)

    do, dz_gdn, d_norm_w = gnorm_bwd(os_[0], os_[1], z_gdn, nw, dgdo)
    dacts, dbgs, dbgrs, ds0s = gdn_bwd(act, bg, bgr, cks, do, (zeros_state, zeros_state), name="gdn_bwd")
    dacts_c, dbgs_c, dbgrs_c, _ = gdn_bwd(act_c, bg_c, bgr_c, cks_c, None, ds0s, name="gdn_bwd_ctx")
    dbg = dbgs[0] + dbgs[1] + _from_chunk_rows(dbgrs[0] + dbgrs[1])
    dbg_c = dbgs_c[0] + dbgs_c[1] + _from_chunk_rows(dbgrs_c[0] + dbgrs_c[1])
    dba, dal, ddt = gates_bwd(ba, alog16, dtb16, dbg, name="gates_bwd")
    dbac, dal_c, ddt_c = gates_bwd(bac, alog16, dtb16, dbg_c, name="gates_bwd_ctx")
    d_a_log = (dal + dal_c)[:, 8:].reshape(1, N_DIR, GDN_HEADS)
    d_dt_bias = (ddt + ddt_c)[:, 8:].reshape(1, N_DIR, GDN_HEADS)
    dqkv, dcw = conv_bwd(qkv, conv16, dacts[0], dacts[1], is_ctx=False, name="conv_bwd")
    dqkvc, dcw_c = conv_bwd(qkvc, conv16, dacts_c[0], dacts_c[1], is_ctx=True, name="conv_bwd_ctx")
    d_conv16 = jnp.sum(dcw, axis=0) + jnp.sum(dcw_c, axis=0)

    du_skip, dy, dz_s5, d_s5_d, d_w_glu, d_b_glu = s5_glu_bwd(u, ys[0], ys[1], z_s5, *glu_w, ds5o)
    dus, ducs = [du_skip], []
    dar, dai, dbbr, dbbi, dcre, dcim = [], [], [], [], [], []
    for d in range(N_DIR):
        du_d, dbre1, dbim1, dct1, dcb1, da1, dh0 = s5_scan_bwd(u, dy, *s5w[d], hins[d],
                                                                jnp.zeros((B, 2, S5_HALF), f32), d=d, name=f"s5_bwd{d}")
        duc_d, dbre2, dbim2, _, _, da2, _ = s5_scan_bwd(uc, None, *s5w[d], hins_c[d], dh0, d=d, name=f"s5_bwd_ctx{d}")
        dus.append(du_d)
        ducs.append(duc_d)
        da = da1 + da2
        dar.append(da[0].reshape(S5_GROUPS, S5_STATE))
        dai.append(da[1].reshape(S5_GROUPS, S5_STATE))
        dbbr.append(_block_diag_in_t(dbre1 + dbre2))
        dbbi.append(_block_diag_in_t(dbim1 + dbim2))
        dcre.append(_block_diag_out_t(dct1))
        dcim.append(-_block_diag_out_t(dcb1))
    dlr, dli, dldt, dbre, dbim = s5_zoh_bwd(*zoh_in, expand, jnp.concatenate(dar, 0), jnp.concatenate(dai, 0),
                                            jnp.concatenate(dbbr, 0), jnp.concatenate(dbbi, 0))
    d_s5 = (dlr, dli, dldt, dbre, dbim, jnp.stack(dcre, 0), jnp.stack(dcim, 0))

    padg = lambda a: jnp.concatenate([a, jnp.zeros(a.shape[:2] + (LANES - N_GATE,), f32)], axis=2)
    zc = jnp.zeros_like(uc)
    dw_c, dmod_c = in_proj_bwd(ctx, mod_c, (tuple(ducs), zc, dqkvc, zc, padg(dbac)), w_in, None, None,
                               name="in_proj_bwd_ctx")
    d_w_in, dmod, grad_x = in_proj_bwd(x, mod, (tuple(dus), dz_s5, dqkv, dz_gdn, padg(dba)), w_in, gx_res, dw_c,
                                       name="in_proj_bwd")
    dmod_c = jnp.sum(dmod_c, axis=0)

    dm_rows = jnp.concatenate([dmod[:, 1], dmod[:, 0], dgate[:, 0]], axis=1)
    dm_ctx = jnp.concatenate([dmod_c[1], dmod_c[0], jnp.zeros((D_MODEL,), f32)])[None]
    dm = jnp.concatenate([dm_rows, dm_ctx, jnp.zeros((8 - B - 1, 3 * D_MODEL), f32)], axis=0)
    dcc, d_w_ada, d_b_ada = ada_bwd(cc, w_ada, dm)
    small = (dcc[B], d_b_ada, *d_s5, d_s5_d, d_b_glu, d_a_log, d_dt_bias, d_norm_w, dlng, dlnb)
    small = tuple(g.reshape(s) for g, s in zip(small, SMALL_SHAPES))
    return loss, grad_x, (d_w_ada, d_w_in, d_w_out, d_w_glu, d_conv16), small


SHARDED = (1, 3, 18, 12, 14)
SMALL = tuple(i for i in range(21) if i not in SHARDED)
W_IN_SHARD = 772


def _conv_rows(w):
    return jnp.concatenate([w.reshape(9, w.shape[-1]), jnp.zeros((CONV_ROWS - 9, w.shape[-1]), f32)], axis=0)


def kernel(x, c, ctx, c_ctx, w_ada, b_ada, w_in, s5_lambda_re, s5_lambda_im, s5_log_dt, s5_b_re, s5_b_im, s5_c_re, s5_c_im, s5_d, w_glu, b_glu, conv_w, gdn_a_log, gdn_dt_bias, gdn_norm_w, w_out, ln_g, ln_b, loss_target, m_c_ctx, m_w_ada, m_b_ada, m_w_in, m_s5_lambda_re, m_s5_lambda_im, m_s5_log_dt, m_s5_b_re, m_s5_b_im, m_s5_c_re, m_s5_c_im, m_s5_d, m_w_glu, m_b_glu, m_conv_w, m_gdn_a_log, m_gdn_dt_bias, m_gdn_norm_w, m_w_out, m_ln_g, m_ln_b, v_c_ctx, v_w_ada, v_b_ada, v_w_in, v_s5_lambda_re, v_s5_lambda_im, v_s5_log_dt, v_s5_b_re, v_s5_b_im, v_s5_c_re, v_s5_c_im, v_s5_d, v_w_glu, v_b_glu, v_conv_w, v_gdn_a_log, v_gdn_dt_bias, v_gdn_norm_w, v_w_out, v_ln_g, v_ln_b):
    weights = [c_ctx, w_ada, b_ada, w_in, s5_lambda_re, s5_lambda_im, s5_log_dt, s5_b_re, s5_b_im, s5_c_re, s5_c_im,
               s5_d, w_glu, b_glu, conv_w, gdn_a_log, gdn_dt_bias, gdn_norm_w, w_out, ln_g, ln_b]
    ms = [m_c_ctx, m_w_ada, m_b_ada, m_w_in, m_s5_lambda_re, m_s5_lambda_im, m_s5_log_dt, m_s5_b_re, m_s5_b_im,
          m_s5_c_re, m_s5_c_im, m_s5_d, m_w_glu, m_b_glu, m_conv_w, m_gdn_a_log, m_gdn_dt_bias, m_gdn_norm_w, m_w_out,
          m_ln_g, m_ln_b]
    vs = [v_c_ctx, v_w_ada, v_b_ada, v_w_in, v_s5_lambda_re, v_s5_lambda_im, v_s5_log_dt, v_s5_b_re, v_s5_b_im,
          v_s5_c_re, v_s5_c_im, v_s5_d, v_w_glu, v_b_glu, v_conv_w, v_gdn_a_log, v_gdn_dt_bias, v_gdn_norm_w, v_w_out,
          v_ln_g, v_ln_b]
    cpos = lax.axis_index("c")
    jchip = 2 * lax.axis_index("x") + lax.axis_index("y")

    conv_shard = _conv_rows(conv_w)
    g_ada, g_in, g_out, g_glu, g_conv = gather_shards(
        [w_ada[0].astype(bf16), w_in[0].astype(bf16), w_out[0].astype(bf16), w_glu[0].astype(bf16), conv_shard])
    w_in_pad = jnp.concatenate([g_in[0], g_in[1], g_in[2], g_in[3], jnp.zeros((D_MODEL, IN_PAD - P_IN), bf16)], axis=1)
    conv16 = g_conv.transpose(1, 0, 2).reshape(CONV_ROWS, 3 * D_GDN)

    swap = lambda a: jnp.swapaxes(a, 3, 4)
    loss, grad_x, big, small = local_step(
        x, c, ctx, c_ctx, loss_target, g_ada, b_ada, w_in_pad, s5_lambda_re, s5_lambda_im, s5_log_dt, swap(s5_b_re),
        swap(s5_b_im),
        s5_c_re, s5_c_im, s5_d, g_glu.reshape(D_S5, D_S5), b_glu, conv16, gdn_a_log, gdn_dt_bias, gdn_norm_w,
        g_out.reshape(D_MODEL, D_MODEL), ln_g, ln_b)
    loss = lax.psum(loss, ("x", "y", "c"))

    d_w_ada, d_w_in, d_w_out, d_w_glu, d_conv16 = big
    slabs = [d_w_ada,
             d_w_in[:, :P_IN].reshape(D_MODEL, 4, W_IN_SHARD).transpose(1, 0, 2),
             d_w_out.reshape(4, D_MODEL // 4, D_MODEL),
             d_w_glu.reshape(4, D_S5 // 4, D_S5),
             d_conv16.reshape(CONV_ROWS, 4, 3 * D_GDN // 4).transpose(1, 0, 2),
             _pack_small(small).reshape(4, SMALL_QUARTER, LANES)]
    got = swap_halves(slabs)
    q32, q16 = [], []
    for t, (s, g) in enumerate(zip(slabs, got)):
        own = lax.dynamic_index_in_dim(s.reshape(4, 2, s.shape[1] // 2, s.shape[2]), cpos, axis=1, keepdims=False)
        a, b = sum_cores(own, g, name=f"sum_cores{t}")
        q32.append(a)
        q16.append(b)
    rec = scatter_to_chips(q16)
    fs = [sum_chips(lax.dynamic_index_in_dim(q, jchip, axis=0, keepdims=False), r, cpos, name=f"sum_chips{t}")
          for t, (q, r) in enumerate(zip(q32, rec))]
    red = join_halves(fs)
    g_small = _unpack_small(gather_small(red[5][0]).reshape(SMALL_TOTAL, LANES))

    grads, deltas, new_m, new_v = [None] * 21, [None] * 21, [None] * 21, [None] * 21
    for t, i in enumerate(SHARDED):
        conv, win = i == 14, i == 3
        prep = (lambda a: _conv_rows(a)[None]) if conv else ((lambda a: jnp.transpose(a, (2, 0, 1))) if win else (lambda a: a))
        g = jnp.transpose(red[t], (2, 0, 1)) if win else red[t]
        d, nm, nv = adamw_3d(prep(weights[i]), g, prep(ms[i]), prep(vs[i]), lead=win, name=f"adamw{t}")
        for lst, val in ((grads, g), (deltas, d), (new_m, nm), (new_v, nv)):
            lst[i] = (val[0, :9].reshape(weights[i].shape) if conv else (jnp.transpose(val, (1, 2, 0)) if win else val))
    small_in = lambda lst: [_as_2d(swap(lst[i]) if n in SMALL_SWAPPED else lst[i]) for n, i in enumerate(SMALL)]
    sm = adamw_small(small_in(weights), [_as_2d(g) for g in g_small], small_in(ms), small_in(vs))
    for n, i in enumerate(SMALL):
        back = (lambda a: swap(a.reshape(SMALL_SHAPES[n]))) if n in SMALL_SWAPPED else (lambda a: a.reshape(weights[i].shape))
        grads[i] = back(g_small[n])
        for lst, res in ((deltas, sm[0]), (new_m, sm[1]), (new_v, sm[2])):
            lst[i] = back(res[n])
    return (loss, grad_x, *grads, *deltas, *new_m, *new_v)
```

```python
import functools

import jax
import jax.numpy as jnp
from jax import lax
from jax.experimental import pallas as pl
from jax.experimental.pallas import tpu as pltpu

f32 = jnp.float32
bf16 = jnp.bfloat16
SDS = jax.ShapeDtypeStruct

D_MODEL = 1024
D_S5 = 512
S5_GROUP = 16
S5_GROUPS = 32
S5_STATE = 64
S5_HALF = S5_GROUPS * S5_STATE
D_GDN = 512
GDN_HEAD = 128
GDN_HEADS = 4
CHUNK = 64
GRID_W = 64
N_DIR = 2
P_IN = 3088
DEEPNORM_ALPHA = 2.0 ** 0.25
LN_EPS = 1e-5
NORM_EPS = 1e-6
ADAM_LR, ADAM_B1, ADAM_B2, ADAM_EPS, ADAM_WD, ADAM_STEP = 0.001, 0.9, 0.999, 1e-08, 0.01, 10

LANES = 128
VMEM_LIMIT = 56 * 1024 * 1024
TOK_TILE = 256
S5_TILE = 256
MESH = pl.DeviceIdType.MESH


def _cparams(n_grid):
    return pltpu.CompilerParams(dimension_semantics=("arbitrary",) * n_grid, vmem_limit_bytes=VMEM_LIMIT)


def _dot(a, b):
    return jnp.dot(a.astype(bf16), b.astype(bf16), preferred_element_type=f32)


def _dot_nt(a, b):
    return lax.dot_general(a.astype(bf16), b.astype(bf16), (((1,), (1,)), ((), ())), preferred_element_type=f32)


def _dot_tn(a, b):
    return lax.dot_general(a.astype(bf16), b.astype(bf16), (((0,), (0,)), ((), ())), preferred_element_type=f32)


def _dot_hi(a, b):
    return jnp.dot(a, b, precision=lax.Precision.HIGHEST, preferred_element_type=f32)


def _dot_h3(a, b):
    return jnp.dot(a, b, precision=lax.Precision.HIGH, preferred_element_type=f32)


@jax.custom_vjp
def _mm(a, b):
    return _dot(a, b)


@jax.custom_vjp
def _mm_nt(a, b):
    return _dot_nt(a, b)


@jax.custom_vjp
def _mm_tn(a, b):
    return _dot_tn(a, b)


_mm.defvjp(lambda a, b: (_dot(a, b), (a, b)), lambda r, g: (_mm_nt(g, r[1]), _mm_tn(r[0], g)))
_mm_nt.defvjp(lambda a, b: (_dot_nt(a, b), (a, b)), lambda r, g: (_mm(g, r[1]), _mm_tn(g, r[0])))
_mm_tn.defvjp(lambda a, b: (_dot_tn(a, b), (a, b)), lambda r, g: (_mm_nt(r[1], g), _mm(r[0], g)))


def _silu(x):
    return x * jax.nn.sigmoid(x)


def _gelu(x):
    return 0.5 * x * (1.0 + lax.erf(x * (2.0 ** -0.5)))


def _resident(shape):
    nd = len(shape)
    return pl.BlockSpec(shape, lambda *_: (0,) * nd, pipeline_mode=pl.Buffered(1))


def _tok(tile, width, nt=None, rev=False):
    if rev:
        return pl.BlockSpec((None, tile, width), lambda b, n: (b, nt - 1 - n, 0))
    return pl.BlockSpec((None, tile, width), lambda b, n: (b, n, 0))


def _per_batch(rows, width):
    return pl.BlockSpec((None, rows, width), lambda b, n: (b, 0, 0))


def _first_step():
    return jnp.logical_and(pl.program_id(0) == 0, pl.program_id(1) == 0)


ADA_SHARD = 3 * D_MODEL // 4


def ada_fwd(cc, w, b):
    def body(cc_ref, w_ref, b_ref, m_ref):
        s = _silu(cc_ref[...]).astype(bf16)
        for j in range(4):
            sl = slice(j * ADA_SHARD, (j + 1) * ADA_SHARD)
            m_ref[:, sl] = _dot(s, w_ref[j]) + b_ref[:, sl]

    return pl.pallas_call(body, name="ada_fwd", out_shape=SDS((8, 3 * D_MODEL), f32),
                          compiler_params=pltpu.CompilerParams(vmem_limit_bytes=VMEM_LIMIT))(cc, w, b)


def ada_bwd(cc, w, dm):
    def body(cc_ref, w_ref, dm_ref, dcc_ref, dw_ref, db_ref):
        s, vjp = jax.vjp(_silu, cc_ref[...])
        ds = jnp.zeros((8, D_MODEL), f32)
        for j in range(4):
            dmj = dm_ref[:, j * ADA_SHARD:(j + 1) * ADA_SHARD]
            ds = ds + _dot_nt(dmj, w_ref[j])
            dw_ref[j] = _dot_tn(s, dmj)
        dcc_ref[...] = vjp(ds)[0]
        db_ref[...] = jnp.sum(dm_ref[...], axis=0, keepdims=True)

    return pl.pallas_call(
        body, name="ada_bwd",
        out_shape=[SDS((8, D_MODEL), f32), SDS((4, D_MODEL, ADA_SHARD), f32), SDS((1, 3 * D_MODEL), f32)],
        compiler_params=pltpu.CompilerParams(vmem_limit_bytes=VMEM_LIMIT))(cc, w, dm)


N_GATE = 2 * N_DIR * GDN_HEADS
IN_WIDTHS = (D_S5, D_S5, 3 * D_GDN, D_GDN, LANES)
IN_OFFS = (0, 512, 1024, 2560, 3072)
IN_PAD = 3200


def in_proj_fwd(x, mod, w, *, name):
    B, L, _ = x.shape
    T = min(TOK_TILE, L)

    def body(x_ref, mod_ref, w_ref, *o_refs):
        h = (x_ref[...] * (1.0 + mod_ref[0:1, :]) + mod_ref[1:2, :]).astype(bf16)
        for o_ref, off, wd in zip(o_refs, IN_OFFS, IN_WIDTHS):
            r = _dot(h, w_ref[:, off:off + wd])
            o_ref[...] = r[:, :o_ref.shape[-1]]

    outw = (D_S5, D_S5, 3 * D_GDN, D_GDN, N_GATE)
    return pl.pallas_call(
        body, name=name, grid=(B, L // T),
        in_specs=[_tok(T, D_MODEL), _per_batch(2, D_MODEL), _resident((D_MODEL, IN_PAD))],
        out_specs=[_tok(T, wd) for wd in outw],
        out_shape=[SDS((B, L, wd), f32) for wd in outw],
        compiler_params=_cparams(2),
    )(x, mod, w)


def in_proj_bwd(x, mod, ds, w, gx_res, dw_start, *, name):
    B, L, _ = x.shape
    T = min(TOK_TILE, L)
    with_dx = gx_res is not None
    with_start = dw_start is not None
    n_u = len(ds[0])

    def body(*refs):
        x_ref, mod_ref = refs[0], refs[1]
        du_refs = refs[2:2 + n_u]
        d_refs = refs[2 + n_u:6 + n_u]
        w_ref = refs[6 + n_u]
        k = 7 + n_u
        if with_dx:
            gx_ref = refs[k]
            k += 1
        if with_start:
            start_ref = refs[k]
            k += 1
        dw_ref, dmod_ref = refs[k], refs[k + 1]
        if with_dx:
            dx_ref = refs[k + 2]
        n = pl.program_id(1)

        @pl.when(_first_step())
        def _():
            dw_ref[...] = start_ref[...] if with_start else jnp.zeros_like(dw_ref)

        @pl.when(n == 0)
        def _():
            dmod_ref[...] = jnp.zeros_like(dmod_ref)

        xv = x_ref[...]
        scale1 = 1.0 + mod_ref[0:1, :]
        h = (xv * scale1 + mod_ref[1:2, :]).astype(bf16)
        du = du_refs[0][...]
        for r in du_refs[1:]:
            du = du + r[...]
        dh = jnp.zeros((T, D_MODEL), f32)
        for dv, off, wd in zip([du] + [r[...] for r in d_refs], IN_OFFS, IN_WIDTHS):
            dv = dv.astype(bf16)
            dh = dh + _dot_nt(dv, w_ref[:, off:off + wd])
            dw_ref[:, off:off + wd] += _dot_tn(h, dv)
        dmod_ref[0:1, :] += jnp.sum(dh * xv, axis=0, keepdims=True)
        dmod_ref[1:2, :] += jnp.sum(dh, axis=0, keepdims=True)
        if with_dx:
            dx_ref[...] = gx_ref[...] + dh * scale1

    in_specs = ([_tok(T, D_MODEL), _per_batch(2, D_MODEL)] + [_tok(T, D_S5)] * n_u + [_tok(T, wd) for wd in IN_WIDTHS[1:]]
                + [_resident((D_MODEL, IN_PAD))])
    args = [x, mod, *ds[0], *ds[1:], w]
    out_specs = [_resident((D_MODEL, IN_PAD)), _per_batch(2, D_MODEL)]
    out_shape = [SDS((D_MODEL, IN_PAD), f32), SDS((B, 2, D_MODEL), f32)]
    if with_dx:
        in_specs.append(_tok(T, D_MODEL))
        args.append(gx_res)
        out_specs.append(_tok(T, D_MODEL))
        out_shape.append(SDS((B, L, D_MODEL), f32))
    if with_start:
        in_specs.append(_resident((D_MODEL, IN_PAD)))
        args.append(dw_start)
    return pl.pallas_call(body, name=name, grid=(B, L // T), in_specs=in_specs, out_specs=out_specs,
                          out_shape=out_shape, compiler_params=_cparams(2))(*args)


def _s5_zoh(lr, li, ldt, bre, bim, expand):
    dt = jnp.exp(ldt)
    zr, zi = lr * dt, li * dt
    e = jnp.exp(zr)
    ar, ai = e * jnp.cos(zi), e * jnp.sin(zi)
    den = lr * lr + li * li
    czr = ((ar - 1.0) * lr + ai * li) / den
    czi = (ai * lr - (ar - 1.0) * li) / den
    czr_e, czi_e = _dot_hi(czr, expand), _dot_hi(czi, expand)
    return ar, ai, czr_e * bre - czi_e * bim, czr_e * bim + czi_e * bre


_ZOH_OUT = [(N_DIR * S5_GROUPS, S5_STATE)] * 2 + [(N_DIR * S5_GROUPS, S5_STATE * S5_GROUP)] * 2


def s5_zoh_fwd(lr, li, ldt, bre, bim, expand):
    def body(lr_ref, li_ref, ldt_ref, bre_ref, bim_ref, e_ref, ar_ref, ai_ref, bbr_ref, bbi_ref):
        ar, ai, bbr, bbi = _s5_zoh(lr_ref[...], li_ref[...], ldt_ref[...], bre_ref[...], bim_ref[...], e_ref[...])
        ar_ref[...], ai_ref[...], bbr_ref[...], bbi_ref[...] = ar, ai, bbr, bbi

    return pl.pallas_call(body, name="s5_zoh_fwd", out_shape=[SDS(s, f32) for s in _ZOH_OUT])(
        lr, li, ldt, bre, bim, expand)


def s5_zoh_bwd(lr, li, ldt, bre, bim, expand, dar, dai, dbbr, dbbi):
    def body(lr_ref, li_ref, ldt_ref, bre_ref, bim_ref, e_ref, dar_ref, dai_ref, dbbr_ref, dbbi_ref,
             dlr_ref, dli_ref, dldt_ref, dbre_ref, dbim_ref):
        ev = e_ref[...]
        _, vjp = jax.vjp(lambda a, b, c, d, e: _s5_zoh(a, b, c, d, e, ev),
                         lr_ref[...], li_ref[...], ldt_ref[...], bre_ref[...], bim_ref[...])
        outs = vjp((dar_ref[...], dai_ref[...], dbbr_ref[...], dbbi_ref[...]))
        dlr_ref[...], dli_ref[...], dldt_ref[...], dbre_ref[...], dbim_ref[...] = outs

    shapes = [lr.shape, li.shape, ldt.shape, bre.shape, bim.shape]
    return pl.pallas_call(body, name="s5_zoh_bwd", out_shape=[SDS(s, f32) for s in shapes])(
        lr, li, ldt, bre, bim, expand, dar, dai, dbbr, dbbi)


def _scan_rows(T, rev, ar, ai, h0s, refs, off):
    def step(i, carry):
        t = off + ((T - 1 - i) if rev else i)
        out = []
        for (hr, hi), (r_ref, i_ref) in zip(carry, refs):
            nr = ar * hr - ai * hi + r_ref[pl.ds(t, 1), :]
            ni = ar * hi + ai * hr + i_ref[pl.ds(t, 1), :]
            r_ref[pl.ds(t, 1), :] = nr
            i_ref[pl.ds(t, 1), :] = ni
            out.append((nr, ni))
        return tuple(out)

    return lax.fori_loop(0, T, step, tuple(h0s))


S5_BLOCKS = 4
S5_BC = D_S5 // S5_BLOCKS
S5_BS = S5_HALF // S5_BLOCKS


def _s5_in(uv, bre_ref, bim_ref, hr_ref, hi_ref, off, T):
    for jb in range(S5_BLOCKS):
        uj = uv[:, jb * S5_BC:(jb + 1) * S5_BC]
        hr_ref[off:off + T, jb * S5_BS:(jb + 1) * S5_BS] = _dot(uj, bre_ref[jb])
        hi_ref[off:off + T, jb * S5_BS:(jb + 1) * S5_BS] = _dot(uj, bim_ref[jb])


def _s5_specs(B, T, nt, rev):
    tidx = (lambda n: nt - 1 - n) if rev else (lambda n: n)
    tok = pl.BlockSpec((B, T, D_S5), lambda n: (0, tidx(n), 0))
    hin = pl.BlockSpec((B, None, 2, S5_HALF), lambda n: (0, tidx(n), 0, 0))
    state = pl.BlockSpec((B, 2, S5_HALF), lambda n: (0, 0, 0))
    return tok, hin, state


def s5_scan_fwd(u, bre, bim, ctop, cbot, arow, h0, *, d, need_y, name):
    B, L, _ = u.shape
    T = min(S5_TILE, L)
    nt = L // T
    rev = d == 1

    def body(u_ref, bre_ref, bim_ref, ct_ref, cb_ref, a_ref, h0_ref, *rest):
        if need_y:
            y_ref, hin_ref, hend_ref, hr_scr, hi_scr, h_scr = rest
        else:
            hin_ref, hend_ref, hr_scr, hi_scr, h_scr = rest
        n = pl.program_id(0)

        @pl.when(n == 0)
        def _():
            h_scr[...] = h0_ref[...]

        hin_ref[...] = h_scr[...]
        for b in range(B):
            _s5_in(u_ref[b].astype(bf16), bre_ref, bim_ref, hr_scr.at[b], hi_scr.at[b], 0, T)
        hs = _scan_rows(T, rev, a_ref[0:1, :], a_ref[1:2, :], [(h_scr[b, 0:1, :], h_scr[b, 1:2, :]) for b in range(B)],
                        [(hr_scr.at[b], hi_scr.at[b]) for b in range(B)], 0)
        for b in range(B):
            h_scr[b, 0:1, :] = hs[b][0]
            h_scr[b, 1:2, :] = hs[b][1]
            if need_y:
                for jb in range(S5_BLOCKS):
                    st = slice(jb * S5_BS, (jb + 1) * S5_BS)
                    y_ref[b, :, jb * S5_BC:(jb + 1) * S5_BC] = (_dot(hr_scr[b, :, st], ct_ref[jb])
                                                                 + _dot(hi_scr[b, :, st], cb_ref[jb]))

        @pl.when(n == nt - 1)
        def _():
            hend_ref[...] = h_scr[...]

    tok, hin_spec, state = _s5_specs(B, T, nt, rev)
    out_specs = [hin_spec, state]
    out_shape = [SDS((B, nt, 2, S5_HALF), f32), SDS((B, 2, S5_HALF), f32)]
    if need_y:
        out_specs.insert(0, tok)
        out_shape.insert(0, SDS((B, L, D_S5), f32))
    w_in, w_out = _resident((S5_BLOCKS, S5_BC, S5_BS)), _resident((S5_BLOCKS, S5_BS, S5_BC))
    return pl.pallas_call(
        body, name=name, grid=(nt,),
        in_specs=[tok, w_in, w_in, w_out, w_out, _resident((2, S5_HALF)), state],
        out_specs=out_specs, out_shape=out_shape,
        scratch_shapes=[pltpu.VMEM((B, T, S5_HALF), f32), pltpu.VMEM((B, T, S5_HALF), f32),
                        pltpu.VMEM((B, 2, S5_HALF), f32)],
        compiler_params=_cparams(1),
    )(u, bre, bim, ctop, cbot, arow, h0)


def s5_scan_bwd(u, dy, bre, bim, ctop, cbot, arow, hin, dhend, *, d, name):
    B, L, _ = u.shape
    T = min(S5_TILE, L)
    nt = L // T
    rev = d == 1
    has_dy = dy is not None
    PAD = 8

    def body(*refs):
        u_ref = refs[0]
        k = 1
        if has_dy:
            dy_ref = refs[1]
            k = 2
        bre_ref, bim_ref, ct_ref, cb_ref, a_ref, hin_ref, dhend_ref = refs[k:k + 7]
        du_ref, dbre_ref, dbim_ref, dct_ref, dcb_ref, da_ref, dh0_ref = refs[k + 7:k + 14]
        hr_scr, hi_scr, gr_scr, gi_scr, p_scr = refs[k + 14:]
        n = pl.program_id(0)

        @pl.when(n == 0)
        def _():
            for r in (dbre_ref, dbim_ref, dct_ref, dcb_ref, da_ref):
                r[...] = jnp.zeros_like(r)
            p_scr[...] = dhend_ref[...]

        ar, ai = a_ref[0:1, :], a_ref[1:2, :]
        prev_row = PAD + T if rev else PAD - 1
        uvs = []
        for b in range(B):
            uvs.append(u_ref[b].astype(bf16))
            _s5_in(uvs[b], bre_ref, bim_ref, hr_scr.at[b], hi_scr.at[b], PAD, T)
            hr_scr[b, prev_row:prev_row + 1, :] = hin_ref[b, 0:1, :]
            hi_scr[b, prev_row:prev_row + 1, :] = hin_ref[b, 1:2, :]
        _scan_rows(T, rev, ar, ai, [(hin_ref[b, 0:1, :], hin_ref[b, 1:2, :]) for b in range(B)],
                   [(hr_scr.at[b], hi_scr.at[b]) for b in range(B)], PAD)
        if has_dy:
            for b in range(B):
                dyv = dy_ref[b].astype(bf16)
                for jb in range(S5_BLOCKS):
                    st = slice(jb * S5_BS, (jb + 1) * S5_BS)
                    dyj = dyv[:, jb * S5_BC:(jb + 1) * S5_BC]
                    gr_scr[b, :, st] = _dot_nt(dyj, ct_ref[jb])
                    gi_scr[b, :, st] = _dot_nt(dyj, cb_ref[jb])
                    dct_ref[jb] += _dot_tn(hr_scr[b, PAD:PAD + T, st], dyj)
                    dcb_ref[jb] += _dot_tn(hi_scr[b, PAD:PAD + T, st], dyj)
        else:
            gr_scr[...] = jnp.zeros_like(gr_scr)
            gi_scr[...] = jnp.zeros_like(gi_scr)

        def step(i, carry):
            t = i if rev else T - 1 - i
            tp = PAD + t + (1 if rev else -1)
            out = []
            for b, (pr, pi, dar, dai) in enumerate(carry):
                gr = gr_scr[b, pl.ds(t, 1), :] + pr
                gi = gi_scr[b, pl.ds(t, 1), :] + pi
                gr_scr[b, pl.ds(t, 1), :] = gr
                gi_scr[b, pl.ds(t, 1), :] = gi
                hpr = hr_scr[b, pl.ds(tp, 1), :]
                hpi = hi_scr[b, pl.ds(tp, 1), :]
                out.append((ar * gr + ai * gi, ar * gi - ai * gr, dar + hpr * gr + hpi * gi, dai + hpr * gi - hpi * gr))
            return tuple(out)

        zero = jnp.zeros((1, S5_HALF), f32)
        res = lax.fori_loop(0, T, step, tuple((p_scr[b, 0:1, :], p_scr[b, 1:2, :], zero, zero) for b in range(B)))
        for b in range(B):
            pr, pi, dar, dai = res[b]
            p_scr[b, 0:1, :] = pr
            p_scr[b, 1:2, :] = pi
            da_ref[0:1, :] += dar
            da_ref[1:2, :] += dai
            for jb in range(S5_BLOCKS):
                st = slice(jb * S5_BS, (jb + 1) * S5_BS)
                ch = slice(jb * S5_BC, (jb + 1) * S5_BC)
                gr_j = gr_scr[b, :, st].astype(bf16)
                gi_j = gi_scr[b, :, st].astype(bf16)
                du_ref[b, :, ch] = _dot_nt(gr_j, bre_ref[jb]) + _dot_nt(gi_j, bim_ref[jb])
                dbre_ref[jb] += _dot_tn(uvs[b][:, ch], gr_j)
                dbim_ref[jb] += _dot_tn(uvs[b][:, ch], gi_j)

        @pl.when(n == nt - 1)
        def _():
            dh0_ref[...] = p_scr[...]

    tok, hin_spec, state = _s5_specs(B, T, nt, not rev)
    w_in, w_out = _resident((S5_BLOCKS, S5_BC, S5_BS)), _resident((S5_BLOCKS, S5_BS, S5_BC))
    wspecs = [w_in, w_in, w_out, w_out]
    in_specs = [tok] + ([tok] if has_dy else []) + wspecs + [_resident((2, S5_HALF)), hin_spec, state]
    args = [u] + ([dy] if has_dy else []) + [bre, bim, ctop, cbot, arow, hin, dhend]
    return pl.pallas_call(
        body, name=name, grid=(nt,), in_specs=in_specs,
        out_specs=[tok] + wspecs + [_resident((2, S5_HALF)), state],
        out_shape=[SDS((B, L, D_S5), f32), SDS((S5_BLOCKS, S5_BC, S5_BS), f32), SDS((S5_BLOCKS, S5_BC, S5_BS), f32),
                   SDS((S5_BLOCKS, S5_BS, S5_BC), f32), SDS((S5_BLOCKS, S5_BS, S5_BC), f32), SDS((2, S5_HALF), f32),
                   SDS((B, 2, S5_HALF), f32)],
        scratch_shapes=[pltpu.VMEM((B, T + 2 * PAD, S5_HALF), f32), pltpu.VMEM((B, T + 2 * PAD, S5_HALF), f32),
                        pltpu.VMEM((B, T, S5_HALF), f32), pltpu.VMEM((B, T, S5_HALF), f32),
                        pltpu.VMEM((B, 2, S5_HALF), f32)],
        compiler_params=_cparams(1),
    )(*args)


def _glu_fn(u, y0, y1, z, dsk, wg, bg):
    g = _gelu(dsk * u + y0 + y1)
    return g * jax.nn.sigmoid(_mm(g, wg) + bg) * _silu(z)


CONV_ROWS = 16


def _conv_taps(L, is_ctx):
    t = lax.broadcasted_iota(jnp.int32, (L, 1), 0)
    taps = []
    for di in ((1,) if is_ctx else (0, 1, 2)):
        for dj in (0, 1, 2):
            s = (0 if is_ctx else GRID_W * (di - 1)) + (dj - 1)
            if is_ctx:
                ok = jnp.logical_and(t + s >= 0, t + s < L)
            else:
                col = jnp.bitwise_and(t, GRID_W - 1) + (dj - 1)
                row = t + GRID_W * (di - 1)
                ok = jnp.logical_and(jnp.logical_and(col >= 0, col < GRID_W), jnp.logical_and(row >= 0, row < L))
            taps.append((di * 3 + dj, s, ok.astype(f32)))
    return taps


def _shift(x, s):
    L = x.shape[0]
    k = (-s) % L
    return x if k == 0 else pltpu.roll(x, k, axis=0)


def _qk_post(pre, is_norm, scale):
    s = _silu(pre)
    nrm = lax.rsqrt(jnp.sum(s * s, axis=-1, keepdims=True) + NORM_EPS)
    return s * jnp.where(is_norm, nrm * scale, 1.0)


def _conv_kind():
    ct = pl.program_id(1)
    return ct < 2 * GDN_HEADS, jnp.where(ct < GDN_HEADS, GDN_HEAD ** -0.5, 1.0).astype(f32)


def _conv_pre(xv, w_ref, taps):
    pre = jnp.zeros_like(xv)
    for r, s, m in taps:
        pre = pre + w_ref[r:r + 1, :] * (m * _shift(xv, s))
    return pre


def conv_fwd(qkv, w16, *, is_ctx, name):
    B, L, C = qkv.shape
    spec = pl.BlockSpec((None, L, GDN_HEAD), lambda b, ct: (b, 0, ct))
    wspec = pl.BlockSpec((CONV_ROWS, GDN_HEAD), lambda b, ct: (0, ct))

    def body(x_ref, w_ref, o_ref):
        is_norm, scale = _conv_kind()
        o_ref[...] = _qk_post(_conv_pre(x_ref[...], w_ref, _conv_taps(L, is_ctx)), is_norm, scale)

    return pl.pallas_call(body, name=name, grid=(B, C // GDN_HEAD), in_specs=[spec, wspec], out_specs=spec,
                          out_shape=SDS((B, L, C), f32), compiler_params=_cparams(2))(qkv, w16)


def conv_bwd(qkv, w16, da0, da1, *, is_ctx, name):
    B, L, C = qkv.shape
    spec = pl.BlockSpec((None, L, GDN_HEAD), lambda b, ct: (b, 0, ct))
    wspec = pl.BlockSpec((CONV_ROWS, GDN_HEAD), lambda b, ct: (0, ct))
    dwspec = pl.BlockSpec((None, CONV_ROWS, GDN_HEAD), lambda b, ct: (b, 0, ct))

    def body(x_ref, w_ref, d0_ref, d1_ref, dx_ref, dw_ref):
        is_norm, scale = _conv_kind()
        taps = _conv_taps(L, is_ctx)
        xv = x_ref[...]
        _, vjp = jax.vjp(lambda p: _qk_post(p, is_norm, scale), _conv_pre(xv, w_ref, taps))
        dpre = vjp(d0_ref[...] + d1_ref[...])[0]
        dx = jnp.zeros_like(xv)
        dw_ref[...] = jnp.zeros_like(dw_ref)
        for r, s, m in taps:
            md = m * dpre
            dx = dx + _shift(w_ref[r:r + 1, :] * md, -s)
            dw_ref[r:r + 1, :] = jnp.sum(md * _shift(xv, s), axis=0, keepdims=True)
        dx_ref[...] = dx

    return pl.pallas_call(body, name=name, grid=(B, C // GDN_HEAD), in_specs=[spec, wspec, spec, spec],
                          out_specs=[spec, dwspec], out_shape=[SDS((B, L, C), f32), SDS((B, CONV_ROWS, C), f32)],
                          compiler_params=_cparams(2))(qkv, w16, da0, da1)


def _gates_fn(ba, alog, dtb):
    T = ba.shape[0]
    lane = lax.broadcasted_iota(jnp.int32, ba.shape, 1)
    ii = lax.broadcasted_iota(jnp.int32, (T, T), 0)
    jj = lax.broadcasted_iota(jnp.int32, (T, T), 1)
    same = jnp.right_shift(ii, 6) == jnp.right_shift(jj, 6)
    lmat = jnp.logical_and(same, ii >= jj).astype(f32)
    umat = jnp.logical_and(same, ii <= jj).astype(f32)
    g = jnp.where(lane >= 8, -jnp.exp(alog) * jax.nn.softplus(ba + dtb), 0.0)
    gc = jnp.where(lane >= 12, _dot_hi(umat, g), _dot_hi(lmat, g))
    return jnp.where(lane < 8, jax.nn.sigmoid(ba), gc)


def gates_fwd(ba, alog, dtb, *, name):
    B, L, _ = ba.shape
    T = min(TOK_TILE, L)
    t = _tok(T, N_GATE)

    def body(ba_ref, al_ref, dt_ref, o_ref):
        o_ref[...] = _gates_fn(ba_ref[...], al_ref[...], dt_ref[...])

    return pl.pallas_call(body, name=name, grid=(B, L // T),
                          in_specs=[t, _resident((1, N_GATE)), _resident((1, N_GATE))], out_specs=t,
                          out_shape=SDS((B, L, N_GATE), f32), compiler_params=_cparams(2))(ba, alog, dtb)


def gates_bwd(ba, alog, dtb, dbg, *, name):
    B, L, _ = ba.shape
    T = min(TOK_TILE, L)
    t = _tok(T, N_GATE)
    small = _resident((1, N_GATE))

    def body(ba_ref, al_ref, dt_ref, d_ref, dba_ref, dal_ref, ddt_ref):
        @pl.when(_first_step())
        def _():
            dal_ref[...] = jnp.zeros_like(dal_ref)
            ddt_ref[...] = jnp.zeros_like(ddt_ref)

        _, vjp = jax.vjp(_gates_fn, ba_ref[...], al_ref[...], dt_ref[...])
        dba, dal, ddt = vjp(d_ref[...])
        dba_ref[...] = dba
        dal_ref[...] += dal
        ddt_ref[...] += ddt

    return pl.pallas_call(body, name=name, grid=(B, L // T), in_specs=[t, small, small, t],
                          out_specs=[t, small, small],
                          out_shape=[SDS((B, L, N_GATE), f32), SDS((1, N_GATE), f32), SDS((1, N_GATE), f32)],
                          compiler_params=_cparams(2))(ba, alog, dtb, dbg)


@jax.custom_vjp
def _inv_unit_tri(mats):
    n = mats[0].shape[0]
    eye = (lax.broadcasted_iota(jnp.int32, (n, n), 0) == lax.broadcasted_iota(jnp.int32, (n, n), 1)).astype(f32)
    xs = [eye - a for a in mats]
    sq = [_dot(a, a) for a in mats]
    ps = sq
    k = 2
    while k < n:
        xs = [x + _dot(x, p) for x, p in zip(xs, ps)]
        k *= 2
        if k < n:
            ps = [_dot(p, p) for p in ps]
    return tuple(_dot(p, x) - a for p, x, a in zip(sq, xs, mats))


def _inv_unit_tri_fwd(mats):
    ns = _inv_unit_tri(mats)
    return ns, ns


def _inv_unit_tri_bwd(ns, dns):
    ys = [dn + _dot_tn(nn, dn) for nn, dn in zip(ns, dns)]
    return (tuple(-(y + _dot_nt(y, nn)) for y, nn in zip(ys, ns)),)


_inv_unit_tri.defvjp(_inv_unit_tri_fwd, _inv_unit_tri_bwd)


def _gdn_chunk(heads, *, revs):
    n = heads[0][0].shape[0]
    ii = lax.broadcasted_iota(jnp.int32, (n, n), 0)
    jj = lax.broadcasted_iota(jnp.int32, (n, n), 1)
    row = lax.broadcasted_iota(jnp.int32, (n, 1), 0)
    lower = {False: ii >= jj, True: ii <= jj}
    strict = {False: ii > jj, True: ii < jj}
    last = {False: n - 1, True: 0}
    H = range(len(heads))
    q, k, v, beta, gc, gr, s = (list(t) for t in zip(*heads))
    decay = [jnp.where(lower[revs[h]], jnp.exp(jnp.where(lower[revs[h]], gc[h] - gr[h], 0.0)), 0.0) for h in H]
    kk = [_mm_nt(k[h], k[h]) for h in H]
    qk = [_mm_nt(q[h], k[h]) * decay[h] for h in H]
    qs = [_mm(q[h], s[h]) for h in H]
    a_mat = tuple(jnp.where(strict[revs[h]], beta[h] * kk[h] * decay[h], 0.0) for h in H)
    gamma = [jnp.exp(gc[h]) for h in H]
    g_last = [jnp.sum(jnp.where(row == last[revs[h]], gc[h], 0.0), axis=0, keepdims=True) for h in H]
    nmat = _inv_unit_tri(a_mat)
    bv = [beta[h] * v[h] for h in H]
    bk = [(beta[h] * gamma[h]) * k[h] for h in H]
    u0 = [bv[h] + _mm(nmat[h], bv[h]) for h in H]
    w = [bk[h] + _mm(nmat[h], bk[h]) for h in H]
    k_out = [k[h] * jnp.exp(g_last[h] - gc[h]) for h in H]
    u = [u0[h] - _mm(w[h], s[h]) for h in H]
    o = [gamma[h] * qs[h] + _mm(qk[h], u[h]) for h in H]
    s_new = [jnp.exp(g_last[h]) * s[h] + _mm_tn(k_out[h], u[h]) for h in H]
    return tuple((o[h], s_new[h]) for h in H)


def _gdn_specs(B, nc, rev):
    def cidx(n):
        return (nc - 1 - n) if rev else n
    tok = lambda width: pl.BlockSpec((B, CHUNK, width), lambda n: (0, cidx(n), 0))
    rowspec = pl.BlockSpec((B, None, N_GATE, CHUNK), lambda n: (0, cidx(n), 0, 0))
    st = pl.BlockSpec((B, GDN_HEADS, GDN_HEAD, GDN_HEAD), lambda n: (0, 0, 0, 0))
    ck = pl.BlockSpec((B, None, GDN_HEADS, GDN_HEAD, GDN_HEAD), lambda n: (0, cidx(n), 0, 0, 0))
    return tok, rowspec, st, ck


def _gdn_head_args(qkv_ref, bg_ref, bgr_ref, b, d, h):
    col = d * GDN_HEADS + h
    q = qkv_ref[b, :, h * GDN_HEAD:(h + 1) * GDN_HEAD]
    k = qkv_ref[b, :, D_GDN + h * GDN_HEAD:D_GDN + (h + 1) * GDN_HEAD]
    v = qkv_ref[b, :, 2 * D_GDN + h * GDN_HEAD:2 * D_GDN + (h + 1) * GDN_HEAD]
    bgv = bg_ref[b]
    return q, k, v, bgv[:, col:col + 1], bgv[:, 8 + col:9 + col], bgr_ref[b][8 + col:9 + col, :]


def _gdn_chains(B):
    return [(d, b, h) for d in range(N_DIR) for b in range(B) for h in range(GDN_HEADS)]


def gdn_fwd(qkv, bg, bgr, s0s, *, need_o, name):
    B, L, _ = qkv.shape
    nc = L // CHUNK
    specs = [_gdn_specs(B, nc, d == 1) for d in range(N_DIR)]
    chains = _gdn_chains(B)
    state_shape = (B, GDN_HEADS, GDN_HEAD, GDN_HEAD)

    def body(*refs):
        ins = [refs[3 * d:3 * d + 3] for d in range(N_DIR)]
        s0_refs = refs[6:8]
        k = 8
        o_refs = refs[k:k + 2] if need_o else None
        k += 2 if need_o else 0
        ck_refs, sf_refs, s_scrs = refs[k:k + 2], refs[k + 2:k + 4], refs[k + 4:k + 6]
        n = pl.program_id(0)

        @pl.when(n == 0)
        def _():
            for d in range(N_DIR):
                s_scrs[d][...] = s0_refs[d][...]

        for d in range(N_DIR):
            ck_refs[d][...] = s_scrs[d][...]
        heads = tuple(_gdn_head_args(*ins[d], b, d, h) + (s_scrs[d][b, h],) for d, b, h in chains)
        outs = _gdn_chunk(heads, revs=tuple(d == 1 for d, _, _ in chains))
        for (d, b, h), (o, s_new) in zip(chains, outs):
            if need_o:
                o_refs[d][b, :, h * GDN_HEAD:(h + 1) * GDN_HEAD] = o
            s_scrs[d][b, h] = s_new

        @pl.when(n == nc - 1)
        def _():
            for d in range(N_DIR):
                sf_refs[d][...] = s_scrs[d][...]

    in_specs, out_o, out_ck, out_sf = [], [], [], []
    for tok, rowspec, st, ck in specs:
        in_specs += [tok(3 * D_GDN), tok(N_GATE), rowspec]
        out_o.append(tok(D_GDN))
        out_ck.append(ck)
        out_sf.append(st)
    in_specs += [specs[0][2]] * 2
    out_specs = (out_o if need_o else []) + out_ck + out_sf
    out_shape = (([SDS((B, L, D_GDN), f32)] * 2 if need_o else [])
                 + [SDS((B, nc) + state_shape[1:], f32)] * 2 + [SDS(state_shape, f32)] * 2)
    res = pl.pallas_call(
        body, name=name, grid=(nc,), in_specs=in_specs, out_specs=out_specs, out_shape=out_shape,
        scratch_shapes=[pltpu.VMEM(state_shape, f32)] * 2, compiler_params=_cparams(1),
    )(qkv, bg, bgr, qkv, bg, bgr, *s0s)
    if need_o:
        return res[0:2], res[2:4], res[4:6]
    return res[0:2], res[2:4]


def gdn_bwd(qkv, bg, bgr, cks, do, dsfs, *, name):
    B, L, _ = qkv.shape
    nc = L // CHUNK
    has_do = do is not None
    specs = [_gdn_specs(B, nc, d != 1) for d in range(N_DIR)]
    chains = _gdn_chains(B)
    state_shape = (B, GDN_HEADS, GDN_HEAD, GDN_HEAD)
    per_dir = 5 if has_do else 4

    def body(*refs):
        ins = [refs[per_dir * d:per_dir * d + per_dir] for d in range(N_DIR)]
        k = per_dir * N_DIR
        dsf_refs = refs[k:k + 2]
        outs = [refs[k + 2 + 3 * d:k + 5 + 3 * d] for d in range(N_DIR)]
        ds0_refs, ds_scrs = refs[k + 8:k + 10], refs[k + 10:k + 12]
        n = pl.program_id(0)

        @pl.when(n == 0)
        def _():
            for d in range(N_DIR):
                ds_scrs[d][...] = dsf_refs[d][...]

        lane = lax.broadcasted_iota(jnp.int32, (CHUNK, N_GATE), 1)
        sub = lax.broadcasted_iota(jnp.int32, (N_GATE, CHUNK), 0)
        heads = tuple(_gdn_head_args(*ins[d][:3], b, d, h) + (ins[d][3][b, h],) for d, b, h in chains)
        _, vjp = jax.vjp(functools.partial(_gdn_chunk, revs=tuple(d == 1 for d, _, _ in chains)), heads)
        zero = jnp.zeros((CHUNK, GDN_HEAD), f32)
        cts = tuple(((ins[d][4][b, :, h * GDN_HEAD:(h + 1) * GDN_HEAD] if has_do else zero), ds_scrs[d][b, h])
                    for d, b, h in chains)
        (dheads,) = vjp(cts)
        dbg_acc = [[jnp.zeros((CHUNK, N_GATE), f32) for _ in range(B)] for _ in range(N_DIR)]
        dbgr_acc = [[jnp.zeros((N_GATE, CHUNK), f32) for _ in range(B)] for _ in range(N_DIR)]
        for (d, b, h), (dq, dk, dv, db, dgc, dgr, ds) in zip(chains, dheads):
            col = d * GDN_HEADS + h
            dqkv_ref = outs[d][0]
            dqkv_ref[b, :, h * GDN_HEAD:(h + 1) * GDN_HEAD] = dq
            dqkv_ref[b, :, D_GDN + h * GDN_HEAD:D_GDN + (h + 1) * GDN_HEAD] = dk
            dqkv_ref[b, :, 2 * D_GDN + h * GDN_HEAD:2 * D_GDN + (h + 1) * GDN_HEAD] = dv
            dbg_acc[d][b] = dbg_acc[d][b] + jnp.where(lane == col, db, 0.0) + jnp.where(lane == 8 + col, dgc, 0.0)
            dbgr_acc[d][b] = dbgr_acc[d][b] + jnp.where(sub == 8 + col, dgr, 0.0)
            ds_scrs[d][b, h] = ds
        for d in range(N_DIR):
            for b in range(B):
                outs[d][1][b] = dbg_acc[d][b]
                outs[d][2][b] = dbgr_acc[d][b]

        @pl.when(n == nc - 1)
        def _():
            for d in range(N_DIR):
                ds0_refs[d][...] = ds_scrs[d][...]

    in_specs, args, out_specs, out_shape = [], [], [], []
    for d, (tok, rowspec, st, ck) in enumerate(specs):
        in_specs += [tok(3 * D_GDN), tok(N_GATE), rowspec, ck] + ([tok(D_GDN)] if has_do else [])
        args += [qkv, bg, bgr, cks[d]] + ([do] if has_do else [])
        out_specs += [tok(3 * D_GDN), tok(N_GATE), rowspec]
        out_shape += [SDS((B, L, 3 * D_GDN), f32), SDS((B, L, N_GATE), f32), SDS((B, nc, N_GATE, CHUNK), f32)]
    st = specs[0][2]
    in_specs += [st, st]
    args += list(dsfs)
    out_specs += [st, st]
    out_shape += [SDS(state_shape, f32)] * 2
    res = pl.pallas_call(
        body, name=name, grid=(nc,), in_specs=in_specs, out_specs=out_specs, out_shape=out_shape,
        scratch_shapes=[pltpu.VMEM(state_shape, f32)] * 2, compiler_params=_cparams(1),
    )(*args)
    return (res[0], res[3]), (res[1], res[4]), (res[2], res[5]), (res[6], res[7])


def _gnorm_fn(o0, o1, z, w):
    o = o0 + o1
    return o * lax.rsqrt(jnp.mean(o * o, axis=-1, keepdims=True) + NORM_EPS) * w * _silu(z)


def _head_loss(y, x, gate, lng, lnb, tgt):
    r = DEEPNORM_ALPHA * x + gate * y
    mu = jnp.mean(r, axis=-1, keepdims=True)
    rc = r - mu
    var = jnp.mean(rc * rc, axis=-1, keepdims=True)
    err = rc * lax.rsqrt(var + LN_EPS) * lng + lnb - tgt
    return (0.5 / D_MODEL) * jnp.sum(jnp.sum(err * err, axis=-1, keepdims=True), axis=0, keepdims=True)


def tail_fwd_bwd(u, y0, y1, z_s5, o0, o1, z_gdn, x, tgt, gate, lng, lnb, ws, wg, dsk, wglu, bglu, nw):
    B, L, _ = x.shape
    T = min(TOK_TILE, L)

    def body(u_ref, y0_ref, y1_ref, z_ref, o0_ref, o1_ref, zg_ref, x_ref, t_ref, gate_ref, lng_ref, lnb_ref, ws_ref,
             wg_ref, dsk_ref, wglu_ref, bglu_ref, nw_ref,
             loss_ref, du_ref, dys_ref, dz_ref, do_ref, dzg_ref, gx_ref, dws_ref, dwg_ref, dgate_ref, dlng_ref, dlnb_ref,
             ddsk_ref, dwglu_ref, dbglu_ref, dnw_ref):
        n = pl.program_id(1)

        @pl.when(_first_step())
        def _():
            for r in (dws_ref, dwg_ref, dlng_ref, dlnb_ref, ddsk_ref, dwglu_ref, dbglu_ref, dnw_ref):
                r[...] = jnp.zeros_like(r)

        @pl.when(n == 0)
        def _():
            loss_ref[...] = jnp.zeros_like(loss_ref)
            dgate_ref[...] = jnp.zeros_like(dgate_ref)

        s5o, glu_vjp = jax.vjp(_glu_fn, u_ref[...], y0_ref[...], y1_ref[...], z_ref[...], dsk_ref[...],
                               wglu_ref[...].astype(f32), bglu_ref[...])
        heads = []
        for h in range(GDN_HEADS):
            sl = slice(h * GDN_HEAD, (h + 1) * GDN_HEAD)
            heads.append(jax.vjp(_gnorm_fn, o0_ref[:, sl], o1_ref[:, sl], zg_ref[:, sl], nw_ref[...]))
        sv = s5o.astype(bf16)
        gv = jnp.concatenate([out for out, _ in heads], axis=1).astype(bf16)
        y = _dot(sv, ws_ref[...]) + _dot(gv, wg_ref[...])
        loss, vjp = jax.vjp(lambda *a: _head_loss(*a, t_ref[...]), y, x_ref[...], gate_ref[...], lng_ref[...],
                            lnb_ref[...])
        dy, dx, dgate, dlng, dlnb = vjp(jnp.ones((1, 1), f32))
        loss_ref[...] += jnp.broadcast_to(loss, loss_ref.shape)
        dyb = dy.astype(bf16)
        gx_ref[...] = dx
        dws_ref[...] += _dot_tn(sv, dyb)
        dwg_ref[...] += _dot_tn(gv, dyb)
        dgate_ref[...] += dgate
        dlng_ref[...] += dlng
        dlnb_ref[...] += dlnb
        du, dys, _, dz, ddsk, dwglu, dbglu = glu_vjp(_dot_nt(dyb, ws_ref[...]))
        du_ref[...], dys_ref[...], dz_ref[...] = du, dys, dz
        ddsk_ref[...] += ddsk
        dwglu_ref[...] += dwglu
        dbglu_ref[...] += dbglu
        dgdo = _dot_nt(dyb, wg_ref[...])
        for h, (_, hvjp) in enumerate(heads):
            sl = slice(h * GDN_HEAD, (h + 1) * GDN_HEAD)
            do, _, dzg, dnw = hvjp(dgdo[:, sl])
            do_ref[:, sl] = do
            dzg_ref[:, sl] = dzg
            dnw_ref[...] += dnw

    half, full = _tok(T, D_S5), _tok(T, D_MODEL)
    row = _resident((1, D_MODEL))
    wsp = _resident((D_S5, D_MODEL))
    r512, rglu, r128 = _resident((1, D_S5)), _resident((D_S5, D_S5)), _resident((1, GDN_HEAD))
    return pl.pallas_call(
        body, name="tail_fwd_bwd", grid=(B, L // T),
        in_specs=[half] * 7 + [full, full, _per_batch(1, D_MODEL), row, row, wsp, wsp, r512, rglu, r512, r128],
        out_specs=[_per_batch(8, LANES)] + [half] * 5 + [full, wsp, wsp, _per_batch(1, D_MODEL), row, row, r512, rglu, r512,
                                                           r128],
        out_shape=[SDS((B, 8, LANES), f32)] + [SDS((B, L, D_S5), f32)] * 5 + [
            SDS((B, L, D_MODEL), f32), SDS((D_S5, D_MODEL), f32), SDS((D_GDN, D_MODEL), f32), SDS((B, 1, D_MODEL), f32),
            SDS((1, D_MODEL), f32), SDS((1, D_MODEL), f32), SDS((1, D_S5), f32), SDS((D_S5, D_S5), f32), SDS((1, D_S5), f32),
            SDS((1, GDN_HEAD), f32)],
        compiler_params=_cparams(2),
    )(u, y0, y1, z_s5, o0, o1, z_gdn, x, tgt, gate, lng, lnb, ws, wg, dsk, wglu, bglu, nw)


def _adamw_math(w, g, m, v):
    nm = ADAM_B1 * m + (1.0 - ADAM_B1) * g
    nv = ADAM_B2 * v + (1.0 - ADAM_B2) * jnp.square(g)
    m_hat = nm / (1.0 - ADAM_B1 ** ADAM_STEP)
    v_hat = nv / (1.0 - ADAM_B2 ** ADAM_STEP)
    return -ADAM_LR * (m_hat / (jnp.sqrt(v_hat) + ADAM_EPS) + ADAM_WD * w), nm, nv


def _row_tile(rows, cap=512):
    for t in range(min(cap, rows), 15, -1):
        if rows % t == 0 and t % 16 == 0:
            return t
    return rows


def adamw_3d(w, g, m, v, *, lead=False, name):
    R, C = (w.shape[0], w.shape[2]) if lead else w.shape[1:]
    if lead:
        T = next(t for t in range(min(256, R), 0, -1) if R % t == 0)
        spec = pl.BlockSpec((T, 1, C), lambda i: (i, 0, 0))
    else:
        T = _row_tile(R)
        spec = pl.BlockSpec((None, T, C), lambda i: (0, i, 0))

    def body(w_ref, g_ref, m_ref, v_ref, d_ref, nm_ref, nv_ref):
        d_ref[...], nm_ref[...], nv_ref[...] = _adamw_math(w_ref[...], g_ref[...], m_ref[...], v_ref[...])

    return pl.pallas_call(body, name=name, grid=(R // T,), in_specs=[spec] * 4, out_specs=[spec] * 3,
                          out_shape=[SDS(w.shape, f32)] * 3, compiler_params=_cparams(1))(w, g, m, v)


def adamw_small(ws, gs, ms, vs):
    n = len(ws)

    def body(*refs):
        outs = refs[4 * n:]
        for i in range(n):
            d, nm, nv = _adamw_math(refs[i][...], refs[n + i][...], refs[2 * n + i][...], refs[3 * n + i][...])
            outs[i][...], outs[n + i][...], outs[2 * n + i][...] = d, nm, nv

    res = pl.pallas_call(body, name="adamw_small", out_shape=[SDS(w.shape, f32) for w in ws] * 3,
                         compiler_params=pltpu.CompilerParams(vmem_limit_bytes=VMEM_LIMIT))(*ws, *gs, *ms, *vs)
    return res[:n], res[n:2 * n], res[2 * n:]


def sum_cores(own, got, *, name):
    A, H, C = own.shape
    T = _row_tile(H)
    spec = pl.BlockSpec((None, T, C), lambda a, i: (a, i, 0))

    def body(a_ref, b_ref, q32_ref, q16_ref):
        q = a_ref[...] + b_ref[...]
        q32_ref[...] = q
        q16_ref[...] = q.astype(bf16)

    return pl.pallas_call(body, name=name, grid=(A, H // T), in_specs=[spec, spec], out_specs=[spec, spec],
                          out_shape=[SDS((A, H, C), f32), SDS((A, H, C), bf16)], compiler_params=_cparams(2))(own, got)


def sum_chips(mine, rec, cpos, *, name):
    H, C = mine.shape
    T = _row_tile(H)
    nt = H // T

    def body(c_ref, m_ref, r_ref, f_ref):
        f_ref[...] = ((m_ref[...] + r_ref[0].astype(f32)) + r_ref[1].astype(f32)) + r_ref[2].astype(f32)

    grid_spec = pltpu.PrefetchScalarGridSpec(
        num_scalar_prefetch=1, grid=(nt,),
        in_specs=[pl.BlockSpec((T, C), lambda i, c_ref: (i, 0)), pl.BlockSpec((3, T, C), lambda i, c_ref: (0, i, 0))],
        out_specs=pl.BlockSpec((None, T, C), lambda i, c_ref: (0, c_ref[0] * nt + i, 0)))
    return pl.pallas_call(body, name=name, grid_spec=grid_spec, out_shape=SDS((1, 2 * H, C), f32),
                          compiler_params=_cparams(1))(cpos.reshape(1).astype(jnp.int32), mine, rec)


CHIP_FLIPS = ((1, 0), (0, 1), (1, 1))


def _pos():
    return lax.axis_index("x"), lax.axis_index("y"), lax.axis_index("c")


def _comm_call(body, srcs, out_sds, n_remote, n_local, name):
    any_spec = pl.BlockSpec(memory_space=pl.ANY)
    return pl.pallas_call(
        body, name=name, in_specs=[any_spec] * len(srcs), out_specs=[any_spec] * len(out_sds), out_shape=out_sds,
        scratch_shapes=[pltpu.SemaphoreType.DMA((n_remote,)), pltpu.SemaphoreType.DMA((n_remote,)),
                        pltpu.SemaphoreType.DMA((max(n_local, 1),))],
        compiler_params=pltpu.CompilerParams(has_side_effects=True),
    )(*srcs)


def _remote(src, dst, send_sems, recv_sems, k, target):
    return pltpu.make_async_remote_copy(src, dst, send_sems.at[k], recv_sems.at[k], device_id=target,
                                        device_id_type=MESH)


def _half_rows(c, rows):
    half = rows // 2
    return pl.ds(pl.multiple_of(c * half, 8), half)


def gather_shards(shards):
    nt = len(shards)

    def body(*refs):
        srcs, outs = refs[:nt], refs[nt:2 * nt]
        send_sems, recv_sems, _ = refs[2 * nt:]
        x, y, c = _pos()
        j = 2 * x + y
        sib = (x, y, 1 - c)
        own = [_remote(srcs[t], outs[t].at[j], send_sems, recv_sems, 7 * t + 6, sib) for t in range(nt)]
        first, passed = [], []
        for k, (fx, fy) in enumerate(CHIP_FLIPS):
            tx, ty = x ^ fx, y ^ fy
            jk = 2 * tx + ty
            for t in range(nt):
                rows = _half_rows(c, srcs[t].shape[0])
                first.append(_remote(srcs[t].at[rows], outs[t].at[j, rows], send_sems, recv_sems, 7 * t + k, (tx, ty, c)))
                passed.append(_remote(outs[t].at[jk, rows], outs[t].at[jk, rows], send_sems, recv_sems, 7 * t + 3 + k, sib))
        for cp in first + own:
            cp.start()
        for a, b in zip(first, passed):
            a.wait_recv()
            b.start()
        for cp in passed + own:
            cp.wait_recv()
        for cp in first + passed + own:
            cp.wait_send()

    return _comm_call(body, shards, [SDS((4,) + s.shape, s.dtype) for s in shards], 7 * nt, 0, "gather_shards")


def swap_halves(ps):
    nt = len(ps)

    def body(*refs):
        srcs, outs = refs[:nt], refs[nt:2 * nt]
        send_sems, recv_sems, _ = refs[2 * nt:]
        x, y, c = _pos()
        cps = [_remote(srcs[t].at[a, _half_rows(1 - c, srcs[t].shape[1])], outs[t].at[a], send_sems, recv_sems, 4 * t + a,
                       (x, y, 1 - c)) for t in range(nt) for a in range(4)]
        for cp in cps:
            cp.start()
        for cp in cps:
            cp.wait()

    return _comm_call(body, ps, [SDS((4, p.shape[1] // 2, p.shape[2]), p.dtype) for p in ps], 4 * nt, 0, "swap_halves")


def scatter_to_chips(qs):
    nt = len(qs)

    def body(*refs):
        srcs, outs = refs[:nt], refs[nt:2 * nt]
        send_sems, recv_sems, _ = refs[2 * nt:]
        x, y, c = _pos()
        cps = []
        for k, (fx, fy) in enumerate(CHIP_FLIPS):
            tx, ty = x ^ fx, y ^ fy
            for t in range(nt):
                cps.append(_remote(srcs[t].at[2 * tx + ty], outs[t].at[k], send_sems, recv_sems, 3 * t + k, (tx, ty, c)))
        for cp in cps:
            cp.start()
        for cp in cps:
            cp.wait()

    return _comm_call(body, qs, [SDS((3,) + q.shape[1:], q.dtype) for q in qs], 3 * nt, 0, "scatter_to_chips")


def join_halves(fs):
    nt = len(fs)

    def body(*refs):
        outs = refs[nt:2 * nt]
        send_sems, recv_sems, _ = refs[2 * nt:]
        x, y, c = _pos()
        cps = []
        for t in range(nt):
            mine = outs[t].at[0, _half_rows(c, outs[t].shape[1])]
            cps.append(_remote(mine, mine, send_sems, recv_sems, t, (x, y, 1 - c)))
        for cp in cps:
            cp.start()
        for cp in cps:
            cp.wait()

    any_spec = pl.BlockSpec(memory_space=pl.ANY)
    return pl.pallas_call(
        body, name="join_halves", in_specs=[any_spec] * nt, out_specs=[any_spec] * nt,
        out_shape=[SDS(f.shape, f.dtype) for f in fs], input_output_aliases={t: t for t in range(nt)},
        scratch_shapes=[pltpu.SemaphoreType.DMA((nt,)), pltpu.SemaphoreType.DMA((nt,)), pltpu.SemaphoreType.DMA((1,))],
        compiler_params=pltpu.CompilerParams(has_side_effects=True),
    )(*fs)


def gather_small(s):
    def body(src, out, send_sems, recv_sems, _):
        x, y, c = _pos()
        j = 2 * x + y
        cps = [_remote(src, out.at[j], send_sems, recv_sems, k, (x ^ fx, y ^ fy, c)) for k, (fx, fy) in enumerate(CHIP_FLIPS)]
        cps.append(_remote(src, out.at[j], send_sems, recv_sems, 3, (x, y, 1 - c)))
        for cp in cps:
            cp.start()
        for cp in cps:
            cp.wait()

    return _comm_call(body, [s], [SDS((4,) + s.shape, s.dtype)], 4, 0, "gather_small")[0]


SMALL_SHAPES = ((D_MODEL,), (1, 3 * D_MODEL), (1, 2, 32, 64), (1, 2, 32, 64), (1, 2, 32), (1, 2, 32, 16, 64),
                (1, 2, 32, 16, 64), (1, 2, 32, 16, 64), (1, 2, 32, 16, 64), (1, D_S5), (1, D_S5), (1, 2, 4), (1, 2, 4),
                (1, GDN_HEAD), (1, D_MODEL), (1, D_MODEL))
SMALL_SWAPPED = (5, 6)


def _size(shape):
    return functools.reduce(lambda p, q: p * q, shape)


SMALL_ROWS = tuple(-(-_size(s) // LANES) for s in SMALL_SHAPES)
SMALL_TOTAL = 2176
SMALL_QUARTER = SMALL_TOTAL // 4


def _rows(a):
    flat = a.reshape(-1)
    pad = (-flat.shape[0]) % LANES
    if pad:
        flat = jnp.concatenate([flat, jnp.zeros((pad,), flat.dtype)])
    return flat.reshape(-1, LANES)


def _pack_small(parts):
    rows = [_rows(p) for p in parts]
    rows.append(jnp.zeros((SMALL_TOTAL - sum(SMALL_ROWS), LANES), f32))
    return jnp.concatenate(rows, axis=0)


def _unpack_small(buf):
    out, r = [], 0
    for s, n in zip(SMALL_SHAPES, SMALL_ROWS):
        out.append(buf[r:r + n].reshape(-1)[:_size(s)].reshape(s))
        r += n
    return out


def _as_2d(a):
    return a.reshape(1, -1) if a.ndim == 1 else a.reshape(-1, a.shape[-1])


S5_BG = S5_GROUPS // S5_BLOCKS


def _block_diag_in(bb):
    eye = jnp.eye(S5_BG, dtype=bb.dtype)
    b4 = bb.reshape(S5_BLOCKS, S5_BG, S5_GROUP, S5_STATE)
    return jnp.einsum('jgcp,gh->jgchp', b4, eye).reshape(S5_BLOCKS, S5_BC, S5_BS)


def _block_diag_in_t(d):
    d6 = d.reshape(S5_BLOCKS, S5_BG, S5_GROUP, S5_BG, S5_STATE)
    return jnp.einsum('jgcgp->jgcp', d6).reshape(S5_GROUPS, S5_GROUP * S5_STATE)


def _block_diag_out(cm):
    eye = jnp.eye(S5_BG, dtype=cm.dtype)
    c4 = cm.reshape(S5_BLOCKS, S5_BG, S5_GROUP, S5_STATE)
    return jnp.einsum('jgcp,gh->jhpgc', c4, eye).reshape(S5_BLOCKS, S5_BS, S5_BC)


def _block_diag_out_t(d):
    d6 = d.reshape(S5_BLOCKS, S5_BG, S5_STATE, S5_BG, S5_GROUP)
    return jnp.einsum('jgpgc->jgcp', d6).reshape(S5_GROUPS, S5_GROUP, S5_STATE)


def _to_chunk_rows(a):
    B, L, W = a.shape
    return a.reshape(B, L // CHUNK, CHUNK, W).transpose(0, 1, 3, 2)


def _from_chunk_rows(a):
    B, nc, W, _ = a.shape
    return a.transpose(0, 1, 3, 2).reshape(B, nc * CHUNK, W)


def local_step(x, c, ctx, c_ctx, tgt, w_ada, b_ada, w_in, lam_re, lam_im, log_dt, b_re, b_im, c_re, c_im, s5_d,
               w_glu, b_glu, conv16, a_log, dt_bias, norm_w, w_out, ln_g, ln_b):
    B, L, _ = x.shape
    zeros_state = jnp.zeros((B, GDN_HEADS, GDN_HEAD, GDN_HEAD), f32)

    cc = jnp.concatenate([c, c_ctx[None, :], jnp.zeros((8 - B - 1, D_MODEL), f32)], axis=0)
    m = ada_fwd(cc, w_ada, b_ada)
    shift, scale, gate = m[:B, :D_MODEL], m[:B, D_MODEL:2 * D_MODEL], m[:B, 2 * D_MODEL:]
    mod = jnp.stack([scale, shift], axis=1)
    mod_c = jnp.broadcast_to(jnp.stack([m[B, D_MODEL:2 * D_MODEL], m[B, :D_MODEL]], axis=0)[None], (B, 2, D_MODEL))

    u, z_s5, qkv, z_gdn, ba = in_proj_fwd(x, mod, w_in, name="in_proj_fwd")
    uc, _, qkvc, _, bac = in_proj_fwd(ctx, mod_c, w_in, name="in_proj_fwd_ctx")

    ng = N_DIR * S5_GROUPS
    zoh_in = (lam_re.reshape(ng, S5_STATE), lam_im.reshape(ng, S5_STATE), log_dt.reshape(ng, 1),
              b_re.reshape(ng, S5_GROUP * S5_STATE), b_im.reshape(ng, S5_GROUP * S5_STATE))
    expand = (jnp.arange(S5_GROUP * S5_STATE)[None, :] % S5_STATE == jnp.arange(S5_STATE)[:, None]).astype(f32)
    ar, ai, bbr, bbi = s5_zoh_fwd(*zoh_in, expand)
    bbr16, bbi16 = bbr.astype(bf16), bbi.astype(bf16)
    c_re16 = c_re.reshape(N_DIR, S5_GROUPS, S5_GROUP, S5_STATE).astype(bf16)
    c_im16 = (-c_im).reshape(N_DIR, S5_GROUPS, S5_GROUP, S5_STATE).astype(bf16)
    s5w, ys, hins, hins_c = [], [], [], []
    for d in range(N_DIR):
        g = slice(d * S5_GROUPS, (d + 1) * S5_GROUPS)
        wd = (_block_diag_in(bbr16[g]), _block_diag_in(bbi16[g]), _block_diag_out(c_re16[d]), _block_diag_out(c_im16[d]),
              jnp.stack([ar[g].reshape(-1), ai[g].reshape(-1)], axis=0))
        s5w.append(wd)
        hin_c, hend_c = s5_scan_fwd(uc, *wd, jnp.zeros((B, 2, S5_HALF), f32), d=d, need_y=False, name=f"s5_fwd_ctx{d}")
        y_d, hin, _ = s5_scan_fwd(u, *wd, hend_c, d=d, need_y=True, name=f"s5_fwd{d}")
        ys.append(y_d)
        hins.append(hin)
        hins_c.append(hin_c)
    glu_w = (s5_d.reshape(1, D_S5), w_glu, b_glu.reshape(1, D_S5))

    act = conv_fwd(qkv, conv16, is_ctx=False, name="conv_fwd")
    act_c = conv_fwd(qkvc, conv16, is_ctx=True, name="conv_fwd_ctx")
    pad8 = jnp.zeros((1, 8), f32)
    alog16 = jnp.concatenate([pad8, a_log.reshape(1, 8)], axis=1)
    dtb16 = jnp.concatenate([pad8, dt_bias.reshape(1, 8)], axis=1)
    bg = gates_fwd(ba, alog16, dtb16, name="gates_fwd")
    bg_c = gates_fwd(bac, alog16, dtb16, name="gates_fwd_ctx")
    bgr, bgr_c = _to_chunk_rows(bg), _to_chunk_rows(bg_c)
    cks_c, s_c = gdn_fwd(act_c, bg_c, bgr_c, (zeros_state, zeros_state), need_o=False, name="gdn_fwd_ctx")
    os_, cks, _ = gdn_fwd(act, bg, bgr, s_c, need_o=True, name="gdn_fwd")
    nw = norm_w.reshape(1, GDN_HEAD)

    (loss8, du_skip, dy, dz_s5, do, dz_gdn, gx_res, dws, dwg, dgate, dlng, dlnb, d_s5_d, d_w_glu, d_b_glu,
     d_norm_w) = tail_fwd_bwd(u, ys[0], ys[1], z_s5, os_[0], os_[1], z_gdn, x, tgt, gate[:, None, :],
                              ln_g.reshape(1, D_MODEL), ln_b.reshape(1, D_MODEL), w_out[:D_S5], w_out[D_S5:], *glu_w, nw)
    loss = jnp.sum(loss8[:, 0, 0])
    d_w_out = jnp.concatenate([dws, dwg], axis=0)

    dacts, dbgs, dbgrs, ds0s = gdn_bwd(act, bg, bgr, cks, do, (zeros_state, zeros_state), name="gdn_bwd")
    dacts_c, dbgs_c, dbgrs_c, _ = gdn_bwd(act_c, bg_c, bgr_c, cks_c, None, ds0s, name="gdn_bwd_ctx")
    dbg = dbgs[0] + dbgs[1] + _from_chunk_rows(dbgrs[0] + dbgrs[1])
    dbg_c = dbgs_c[0] + dbgs_c[1] + _from_chunk_rows(dbgrs_c[0] + dbgrs_c[1])
    dba, dal, ddt = gates_bwd(ba, alog16, dtb16, dbg, name="gates_bwd")
    dbac, dal_c, ddt_c = gates_bwd(bac, alog16, dtb16, dbg_c, name="gates_bwd_ctx")
    d_a_log = (dal + dal_c)[:, 8:].reshape(1, N_DIR, GDN_HEADS)
    d_dt_bias = (ddt + ddt_c)[:, 8:].reshape(1, N_DIR, GDN_HEADS)
    dqkv, dcw = conv_bwd(qkv, conv16, dacts[0], dacts[1], is_ctx=False, name="conv_bwd")
    dqkvc, dcw_c = conv_bwd(qkvc, conv16, dacts_c[0], dacts_c[1], is_ctx=True, name="conv_bwd_ctx")
    d_conv16 = jnp.sum(dcw, axis=0) + jnp.sum(dcw_c, axis=0)

    dus, ducs = [du_skip], []
    dar, dai, dbbr, dbbi, dcre, dcim = [], [], [], [], [], []
    for d in range(N_DIR):
        du_d, dbre1, dbim1, dct1, dcb1, da1, dh0 = s5_scan_bwd(u, dy, *s5w[d], hins[d],
                                                                jnp.zeros((B, 2, S5_HALF), f32), d=d, name=f"s5_bwd{d}")
        duc_d, dbre2, dbim2, _, _, da2, _ = s5_scan_bwd(uc, None, *s5w[d], hins_c[d], dh0, d=d, name=f"s5_bwd_ctx{d}")
        dus.append(du_d)
        ducs.append(duc_d)
        da = da1 + da2
        dar.append(da[0].reshape(S5_GROUPS, S5_STATE))
        dai.append(da[1].reshape(S5_GROUPS, S5_STATE))
        dbbr.append(_block_diag_in_t(dbre1 + dbre2))
        dbbi.append(_block_diag_in_t(dbim1 + dbim2))
        dcre.append(_block_diag_out_t(dct1))
        dcim.append(-_block_diag_out_t(dcb1))
    dlr, dli, dldt, dbre, dbim = s5_zoh_bwd(*zoh_in, expand, jnp.concatenate(dar, 0), jnp.concatenate(dai, 0),
                                            jnp.concatenate(dbbr, 0), jnp.concatenate(dbbi, 0))
    d_s5 = (dlr, dli, dldt, dbre, dbim, jnp.stack(dcre, 0), jnp.stack(dcim, 0))

    padg = lambda a: jnp.concatenate([a, jnp.zeros(a.shape[:2] + (LANES - N_GATE,), f32)], axis=2)
    zc = jnp.zeros_like(uc)
    dw_c, dmod_c = in_proj_bwd(ctx, mod_c, (tuple(ducs), zc, dqkvc, zc, padg(dbac)), w_in, None, None,
                               name="in_proj_bwd_ctx")
    d_w_in, dmod, grad_x = in_proj_bwd(x, mod, (tuple(dus), dz_s5, dqkv, dz_gdn, padg(dba)), w_in, gx_res, dw_c,
                                       name="in_proj_bwd")
    dmod_c = jnp.sum(dmod_c, axis=0)

    dm_rows = jnp.concatenate([dmod[:, 1], dmod[:, 0], dgate[:, 0]], axis=1)
    dm_ctx = jnp.concatenate([dmod_c[1], dmod_c[0], jnp.zeros((D_MODEL,), f32)])[None]
    dm = jnp.concatenate([dm_rows, dm_ctx, jnp.zeros((8 - B - 1, 3 * D_MODEL), f32)], axis=0)
    dcc, d_w_ada, d_b_ada = ada_bwd(cc, w_ada, dm)
    small = (dcc[B], d_b_ada, *d_s5, d_s5_d, d_b_glu, d_a_log, d_dt_bias, d_norm_w, dlng, dlnb)
    small = tuple(g.reshape(s) for g, s in zip(small, SMALL_SHAPES))
    return loss, grad_x, (d_w_ada, d_w_in, d_w_out, d_w_glu, d_conv16), small


SHARDED = (1, 3, 18, 12, 14)
SMALL = tuple(i for i in range(21) if i not in SHARDED)
W_IN_SHARD = 772


def _conv_rows(w):
    return jnp.concatenate([w.reshape(9, w.shape[-1]), jnp.zeros((CONV_ROWS - 9, w.shape[-1]), f32)], axis=0)


def kernel(x, c, ctx, c_ctx, w_ada, b_ada, w_in, s5_lambda_re, s5_lambda_im, s5_log_dt, s5_b_re, s5_b_im, s5_c_re, s5_c_im, s5_d, w_glu, b_glu, conv_w, gdn_a_log, gdn_dt_bias, gdn_norm_w, w_out, ln_g, ln_b, loss_target, m_c_ctx, m_w_ada, m_b_ada, m_w_in, m_s5_lambda_re, m_s5_lambda_im, m_s5_log_dt, m_s5_b_re, m_s5_b_im, m_s5_c_re, m_s5_c_im, m_s5_d, m_w_glu, m_b_glu, m_conv_w, m_gdn_a_log, m_gdn_dt_bias, m_gdn_norm_w, m_w_out, m_ln_g, m_ln_b, v_c_ctx, v_w_ada, v_b_ada, v_w_in, v_s5_lambda_re, v_s5_lambda_im, v_s5_log_dt, v_s5_b_re, v_s5_b_im, v_s5_c_re, v_s5_c_im, v_s5_d, v_w_glu, v_b_glu, v_conv_w, v_gdn_a_log, v_gdn_dt_bias, v_gdn_norm_w, v_w_out, v_ln_g, v_ln_b):
    weights = [c_ctx, w_ada, b_ada, w_in, s5_lambda_re, s5_lambda_im, s5_log_dt, s5_b_re, s5_b_im, s5_c_re, s5_c_im,
               s5_d, w_glu, b_glu, conv_w, gdn_a_log, gdn_dt_bias, gdn_norm_w, w_out, ln_g, ln_b]
    ms = [m_c_ctx, m_w_ada, m_b_ada, m_w_in, m_s5_lambda_re, m_s5_lambda_im, m_s5_log_dt, m_s5_b_re, m_s5_b_im,
          m_s5_c_re, m_s5_c_im, m_s5_d, m_w_glu, m_b_glu, m_conv_w, m_gdn_a_log, m_gdn_dt_bias, m_gdn_norm_w, m_w_out,
          m_ln_g, m_ln_b]
    vs = [v_c_ctx, v_w_ada, v_b_ada, v_w_in, v_s5_lambda_re, v_s5_lambda_im, v_s5_log_dt, v_s5_b_re, v_s5_b_im,
          v_s5_c_re, v_s5_c_im, v_s5_d, v_w_glu, v_b_glu, v_conv_w, v_gdn_a_log, v_gdn_dt_bias, v_gdn_norm_w, v_w_out,
          v_ln_g, v_ln_b]
    cpos = lax.axis_index("c")
    jchip = 2 * lax.axis_index("x") + lax.axis_index("y")

    conv_shard = _conv_rows(conv_w)
    g_ada, g_in, g_out, g_glu, g_conv = gather_shards(
        [w_ada[0].astype(bf16), w_in[0].astype(bf16), w_out[0].astype(bf16), w_glu[0].astype(bf16), conv_shard])
    w_in_pad = jnp.concatenate([g_in[0], g_in[1], g_in[2], g_in[3], jnp.zeros((D_MODEL, IN_PAD - P_IN), bf16)], axis=1)
    conv16 = g_conv.transpose(1, 0, 2).reshape(CONV_ROWS, 3 * D_GDN)

    swap = lambda a: jnp.swapaxes(a, 3, 4)
    loss, grad_x, big, small = local_step(
        x, c, ctx, c_ctx, loss_target, g_ada, b_ada, w_in_pad, s5_lambda_re, s5_lambda_im, s5_log_dt, swap(s5_b_re),
        swap(s5_b_im),
        s5_c_re, s5_c_im, s5_d, g_glu.reshape(D_S5, D_S5), b_glu, conv16, gdn_a_log, gdn_dt_bias, gdn_norm_w,
        g_out.reshape(D_MODEL, D_MODEL), ln_g, ln_b)
    loss = lax.psum(loss, ("x", "y", "c"))

    d_w_ada, d_w_in, d_w_out, d_w_glu, d_conv16 = big
    slabs = [d_w_ada,
             d_w_in[:, :P_IN].reshape(D_MODEL, 4, W_IN_SHARD).transpose(1, 0, 2),
             d_w_out.reshape(4, D_MODEL // 4, D_MODEL),
             d_w_glu.reshape(4, D_S5 // 4, D_S5),
             d_conv16.reshape(CONV_ROWS, 4, 3 * D_GDN // 4).transpose(1, 0, 2),
             _pack_small(small).reshape(4, SMALL_QUARTER, LANES)]
    got = swap_halves(slabs)
    q32, q16 = [], []
    for t, (s, g) in enumerate(zip(slabs, got)):
        own = lax.dynamic_index_in_dim(s.reshape(4, 2, s.shape[1] // 2, s.shape[2]), cpos, axis=1, keepdims=False)
        a, b = sum_cores(own, g, name=f"sum_cores{t}")
        q32.append(a)
        q16.append(b)
    rec = scatter_to_chips(q16)
    fs = [sum_chips(lax.dynamic_index_in_dim(q, jchip, axis=0, keepdims=False), r, cpos, name=f"sum_chips{t}")
          for t, (q, r) in enumerate(zip(q32, rec))]
    red = join_halves(fs)
    g_small = _unpack_small(gather_small(red[5][0]).reshape(SMALL_TOTAL, LANES))

    grads, deltas, new_m, new_v = [None] * 21, [None] * 21, [None] * 21, [None] * 21
    for t, i in enumerate(SHARDED):
        conv, win = i == 14, i == 3
        prep = (lambda a: _conv_rows(a)[None]) if conv else ((lambda a: jnp.transpose(a, (2, 0, 1))) if win else (lambda a: a))
        g = jnp.transpose(red[t], (2, 0, 1)) if win else red[t]
        d, nm, nv = adamw_3d(prep(weights[i]), g, prep(ms[i]), prep(vs[i]), lead=win, name=f"adamw{t}")
        for lst, val in ((grads, g), (deltas, d), (new_m, nm), (new_v, nv)):
            lst[i] = (val[0, :9].reshape(weights[i].shape) if conv else (jnp.transpose(val, (1, 2, 0)) if win else val))
    small_in = lambda lst: [_as_2d(swap(lst[i]) if n in SMALL_SWAPPED else lst[i]) for n, i in enumerate(SMALL)]
    sm = adamw_small(small_in(weights), [_as_2d(g) for g in g_small], small_in(ms), small_in(vs))
    for n, i in enumerate(SMALL):
        back = (lambda a: swap(a.reshape(SMALL_SHAPES[n]))) if n in SMALL_SWAPPED else (lambda a: a.reshape(weights[i].shape))
        grads[i] = back(g_small[n])
        for lst, res in ((deltas, sm[0]), (new_m, sm[1]), (new_v, sm[2])):
            lst[i] = back(res[n])
    return (loss, grad_x, *grads, *deltas, *new_m, *new_v)
```

```python
import functools

import jax
import jax.numpy as jnp
from jax import lax
from jax.experimental import pallas as pl
from jax.experimental.pallas import tpu as pltpu

f32 = jnp.float32
bf16 = jnp.bfloat16
SDS = jax.ShapeDtypeStruct

D_MODEL = 1024
D_S5 = 512
S5_GROUP = 16
S5_GROUPS = 32
S5_STATE = 64
S5_HALF = S5_GROUPS * S5_STATE
D_GDN = 512
GDN_HEAD = 128
GDN_HEADS = 4
CHUNK = 64
GRID_W = 64
N_DIR = 2
P_IN = 3088
DEEPNORM_ALPHA = 2.0 ** 0.25
LN_EPS = 1e-5
NORM_EPS = 1e-6
ADAM_LR, ADAM_B1, ADAM_B2, ADAM_EPS, ADAM_WD, ADAM_STEP = 0.001, 0.9, 0.999, 1e-08, 0.01, 10

LANES = 128
VMEM_LIMIT = 56 * 1024 * 1024
TOK_TILE = 256
S5_TILE = 256
MESH = pl.DeviceIdType.MESH


def _cparams(n_grid):
    return pltpu.CompilerParams(dimension_semantics=("arbitrary",) * n_grid, vmem_limit_bytes=VMEM_LIMIT)


def _dot(a, b):
    return jnp.dot(a.astype(bf16), b.astype(bf16), preferred_element_type=f32)


def _dot_nt(a, b):
    return lax.dot_general(a.astype(bf16), b.astype(bf16), (((1,), (1,)), ((), ())), preferred_element_type=f32)


def _dot_tn(a, b):
    return lax.dot_general(a.astype(bf16), b.astype(bf16), (((0,), (0,)), ((), ())), preferred_element_type=f32)


def _dot_hi(a, b):
    return jnp.dot(a, b, precision=lax.Precision.HIGHEST, preferred_element_type=f32)


def _dot_h3(a, b):
    return jnp.dot(a, b, precision=lax.Precision.HIGH, preferred_element_type=f32)


@jax.custom_vjp
def _mm(a, b):
    return _dot(a, b)


@jax.custom_vjp
def _mm_nt(a, b):
    return _dot_nt(a, b)


@jax.custom_vjp
def _mm_tn(a, b):
    return _dot_tn(a, b)


_mm.defvjp(lambda a, b: (_dot(a, b), (a, b)), lambda r, g: (_mm_nt(g, r[1]), _mm_tn(r[0], g)))
_mm_nt.defvjp(lambda a, b: (_dot_nt(a, b), (a, b)), lambda r, g: (_mm(g, r[1]), _mm_tn(g, r[0])))
_mm_tn.defvjp(lambda a, b: (_dot_tn(a, b), (a, b)), lambda r, g: (_mm_nt(r[1], g), _mm(r[0], g)))


def _silu(x):
    return x * jax.nn.sigmoid(x)


def _gelu(x):
    return 0.5 * x * (1.0 + lax.erf(x * (2.0 ** -0.5)))


def _resident(shape):
    nd = len(shape)
    return pl.BlockSpec(shape, lambda *_: (0,) * nd, pipeline_mode=pl.Buffered(1))


def _tok(tile, width, nt=None, rev=False):
    if rev:
        return pl.BlockSpec((None, tile, width), lambda b, n: (b, nt - 1 - n, 0))
    return pl.BlockSpec((None, tile, width), lambda b, n: (b, n, 0))


def _per_batch(rows, width):
    return pl.BlockSpec((None, rows, width), lambda b, n: (b, 0, 0))


def _first_step():
    return jnp.logical_and(pl.program_id(0) == 0, pl.program_id(1) == 0)


ADA_SHARD = 3 * D_MODEL // 4


def ada_fwd(cc, w, b):
    def body(cc_ref, w_ref, b_ref, m_ref):
        s = _silu(cc_ref[...]).astype(bf16)
        for j in range(4):
            sl = slice(j * ADA_SHARD, (j + 1) * ADA_SHARD)
            m_ref[:, sl] = _dot(s, w_ref[j]) + b_ref[:, sl]

    return pl.pallas_call(body, name="ada_fwd", out_shape=SDS((8, 3 * D_MODEL), f32),
                          compiler_params=pltpu.CompilerParams(vmem_limit_bytes=VMEM_LIMIT))(cc, w, b)


def ada_bwd(cc, w, dm):
    def body(cc_ref, w_ref, dm_ref, dcc_ref, dw_ref, db_ref):
        s, vjp = jax.vjp(_silu, cc_ref[...])
        ds = jnp.zeros((8, D_MODEL), f32)
        for j in range(4):
            dmj = dm_ref[:, j * ADA_SHARD:(j + 1) * ADA_SHARD]
            ds = ds + _dot_nt(dmj, w_ref[j])
            dw_ref[j] = _dot_tn(s, dmj)
        dcc_ref[...] = vjp(ds)[0]
        db_ref[...] = jnp.sum(dm_ref[...], axis=0, keepdims=True)

    return pl.pallas_call(
        body, name="ada_bwd",
        out_shape=[SDS((8, D_MODEL), f32), SDS((4, D_MODEL, ADA_SHARD), f32), SDS((1, 3 * D_MODEL), f32)],
        compiler_params=pltpu.CompilerParams(vmem_limit_bytes=VMEM_LIMIT))(cc, w, dm)


N_GATE = 2 * N_DIR * GDN_HEADS
IN_WIDTHS = (D_S5, D_S5, 3 * D_GDN, D_GDN, LANES)
IN_OFFS = (0, 512, 1024, 2560, 3072)
IN_PAD = 3200


def in_proj_fwd(x, mod, w, *, name):
    B, L, _ = x.shape
    T = min(TOK_TILE, L)

    def body(x_ref, mod_ref, w_ref, *o_refs):
        h = (x_ref[...] * (1.0 + mod_ref[0:1, :]) + mod_ref[1:2, :]).astype(bf16)
        for o_ref, off, wd in zip(o_refs, IN_OFFS, IN_WIDTHS):
            r = _dot(h, w_ref[:, off:off + wd])
            o_ref[...] = r[:, :o_ref.shape[-1]]

    outw = (D_S5, D_S5, 3 * D_GDN, D_GDN, N_GATE)
    return pl.pallas_call(
        body, name=name, grid=(B, L // T),
        in_specs=[_tok(T, D_MODEL), _per_batch(2, D_MODEL), _resident((D_MODEL, IN_PAD))],
        out_specs=[_tok(T, wd) for wd in outw],
        out_shape=[SDS((B, L, wd), f32) for wd in outw],
        compiler_params=_cparams(2),
    )(x, mod, w)


def in_proj_bwd(x, mod, ds, w, gx_res, dw_start, *, name):
    B, L, _ = x.shape
    T = min(TOK_TILE, L)
    with_dx = gx_res is not None
    with_start = dw_start is not None
    n_u = len(ds[0])

    def body(*refs):
        x_ref, mod_ref = refs[0], refs[1]
        du_refs = refs[2:2 + n_u]
        d_refs = refs[2 + n_u:6 + n_u]
        w_ref = refs[6 + n_u]
        k = 7 + n_u
        if with_dx:
            gx_ref = refs[k]
            k += 1
        if with_start:
            start_ref = refs[k]
            k += 1
        dw_ref, dmod_ref = refs[k], refs[k + 1]
        if with_dx:
            dx_ref = refs[k + 2]
        n = pl.program_id(1)

        @pl.when(_first_step())
        def _():
            dw_ref[...] = start_ref[...] if with_start else jnp.zeros_like(dw_ref)

        @pl.when(n == 0)
        def _():
            dmod_ref[...] = jnp.zeros_like(dmod_ref)

        xv = x_ref[...]
        scale1 = 1.0 + mod_ref[0:1, :]
        h = (xv * scale1 + mod_ref[1:2, :]).astype(bf16)
        du = du_refs[0][...]
        for r in du_refs[1:]:
            du = du + r[...]
        dh = jnp.zeros((T, D_MODEL), f32)
        for dv, off, wd in zip([du] + [r[...] for r in d_refs], IN_OFFS, IN_WIDTHS):
            dv = dv.astype(bf16)
            dh = dh + _dot_nt(dv, w_ref[:, off:off + wd])
            dw_ref[:, off:off + wd] += _dot_tn(h, dv)
        dmod_ref[0:1, :] += jnp.sum(dh * xv, axis=0, keepdims=True)
        dmod_ref[1:2, :] += jnp.sum(dh, axis=0, keepdims=True)
        if with_dx:
            dx_ref[...] = gx_ref[...] + dh * scale1

    in_specs = ([_tok(T, D_MODEL), _per_batch(2, D_MODEL)] + [_tok(T, D_S5)] * n_u + [_tok(T, wd) for wd in IN_WIDTHS[1:]]
                + [_resident((D_MODEL, IN_PAD))])
    args = [x, mod, *ds[0], *ds[1:], w]
    out_specs = [_resident((D_MODEL, IN_PAD)), _per_batch(2, D_MODEL)]
    out_shape = [SDS((D_MODEL, IN_PAD), f32), SDS((B, 2, D_MODEL), f32)]
    if with_dx:
        in_specs.append(_tok(T, D_MODEL))
        args.append(gx_res)
        out_specs.append(_tok(T, D_MODEL))
        out_shape.append(SDS((B, L, D_MODEL), f32))
    if with_start:
        in_specs.append(_resident((D_MODEL, IN_PAD)))
        args.append(dw_start)
    return pl.pallas_call(body, name=name, grid=(B, L // T), in_specs=in_specs, out_specs=out_specs,
                          out_shape=out_shape, compiler_params=_cparams(2))(*args)


def _s5_zoh(lr, li, ldt, bre, bim, expand):
    dt = jnp.exp(ldt)
    zr, zi = lr * dt, li * dt
    e = jnp.exp(zr)
    ar, ai = e * jnp.cos(zi), e * jnp.sin(zi)
    den = lr * lr + li * li
    czr = ((ar - 1.0) * lr + ai * li) / den
    czi = (ai * lr - (ar - 1.0) * li) / den
    czr_e, czi_e = _dot_hi(czr, expand), _dot_hi(czi, expand)
    return ar, ai, czr_e * bre - czi_e * bim, czr_e * bim + czi_e * bre


_ZOH_OUT = [(N_DIR * S5_GROUPS, S5_STATE)] * 2 + [(N_DIR * S5_GROUPS, S5_STATE * S5_GROUP)] * 2


def s5_zoh_fwd(lr, li, ldt, bre, bim, expand):
    def body(lr_ref, li_ref, ldt_ref, bre_ref, bim_ref, e_ref, ar_ref, ai_ref, bbr_ref, bbi_ref):
        ar, ai, bbr, bbi = _s5_zoh(lr_ref[...], li_ref[...], ldt_ref[...], bre_ref[...], bim_ref[...], e_ref[...])
        ar_ref[...], ai_ref[...], bbr_ref[...], bbi_ref[...] = ar, ai, bbr, bbi

    return pl.pallas_call(body, name="s5_zoh_fwd", out_shape=[SDS(s, f32) for s in _ZOH_OUT])(
        lr, li, ldt, bre, bim, expand)


def s5_zoh_bwd(lr, li, ldt, bre, bim, expand, dar, dai, dbbr, dbbi):
    def body(lr_ref, li_ref, ldt_ref, bre_ref, bim_ref, e_ref, dar_ref, dai_ref, dbbr_ref, dbbi_ref,
             dlr_ref, dli_ref, dldt_ref, dbre_ref, dbim_ref):
        ev = e_ref[...]
        _, vjp = jax.vjp(lambda a, b, c, d, e: _s5_zoh(a, b, c, d, e, ev),
                         lr_ref[...], li_ref[...], ldt_ref[...], bre_ref[...], bim_ref[...])
        outs = vjp((dar_ref[...], dai_ref[...], dbbr_ref[...], dbbi_ref[...]))
        dlr_ref[...], dli_ref[...], dldt_ref[...], dbre_ref[...], dbim_ref[...] = outs

    shapes = [lr.shape, li.shape, ldt.shape, bre.shape, bim.shape]
    return pl.pallas_call(body, name="s5_zoh_bwd", out_shape=[SDS(s, f32) for s in shapes])(
        lr, li, ldt, bre, bim, expand, dar, dai, dbbr, dbbi)


def _scan_rows(T, rev, ar, ai, h0s, refs, off):
    def step(i, carry):
        t = off + ((T - 1 - i) if rev else i)
        out = []
        for (hr, hi), (r_ref, i_ref) in zip(carry, refs):
            nr = ar * hr - ai * hi + r_ref[pl.ds(t, 1), :]
            ni = ar * hi + ai * hr + i_ref[pl.ds(t, 1), :]
            r_ref[pl.ds(t, 1), :] = nr
            i_ref[pl.ds(t, 1), :] = ni
            out.append((nr, ni))
        return tuple(out)

    return lax.fori_loop(0, T, step, tuple(h0s))


S5_BLOCKS = 4
S5_BC = D_S5 // S5_BLOCKS
S5_BS = S5_HALF // S5_BLOCKS


def _s5_in(uv, bre_ref, bim_ref, hr_ref, hi_ref, off, T):
    for jb in range(S5_BLOCKS):
        uj = uv[:, jb * S5_BC:(jb + 1) * S5_BC]
        hr_ref[off:off + T, jb * S5_BS:(jb + 1) * S5_BS] = _dot(uj, bre_ref[jb])
        hi_ref[off:off + T, jb * S5_BS:(jb + 1) * S5_BS] = _dot(uj, bim_ref[jb])


def _s5_specs(B, T, nt, rev):
    tidx = (lambda n: nt - 1 - n) if rev else (lambda n: n)
    tok = pl.BlockSpec((B, T, D_S5), lambda n: (0, tidx(n), 0))
    hin = pl.BlockSpec((B, None, 2, S5_HALF), lambda n: (0, tidx(n), 0, 0))
    state = pl.BlockSpec((B, 2, S5_HALF), lambda n: (0, 0, 0))
    return tok, hin, state


def s5_scan_fwd(u, bre, bim, ctop, cbot, arow, h0, *, d, need_y, name):
    B, L, _ = u.shape
    T = min(S5_TILE, L)
    nt = L // T
    rev = d == 1

    def body(u_ref, bre_ref, bim_ref, ct_ref, cb_ref, a_ref, h0_ref, *rest):
        if need_y:
            y_ref, hin_ref, hend_ref, hr_scr, hi_scr, h_scr = rest
        else:
            hin_ref, hend_ref, hr_scr, hi_scr, h_scr = rest
        n = pl.program_id(0)

        @pl.when(n == 0)
        def _():
            h_scr[...] = h0_ref[...]

        hin_ref[...] = h_scr[...]
        for b in range(B):
            _s5_in(u_ref[b].astype(bf16), bre_ref, bim_ref, hr_scr.at[b], hi_scr.at[b], 0, T)
        hs = _scan_rows(T, rev, a_ref[0:1, :], a_ref[1:2, :], [(h_scr[b, 0:1, :], h_scr[b, 1:2, :]) for b in range(B)],
                        [(hr_scr.at[b], hi_scr.at[b]) for b in range(B)], 0)
        for b in range(B):
            h_scr[b, 0:1, :] = hs[b][0]
            h_scr[b, 1:2, :] = hs[b][1]
            if need_y:
                for jb in range(S5_BLOCKS):
                    st = slice(jb * S5_BS, (jb + 1) * S5_BS)
                    y_ref[b, :, jb * S5_BC:(jb + 1) * S5_BC] = (_dot(hr_scr[b, :, st], ct_ref[jb])
                                                                 + _dot(hi_scr[b, :, st], cb_ref[jb]))

        @pl.when(n == nt - 1)
        def _():
            hend_ref[...] = h_scr[...]

    tok, hin_spec, state = _s5_specs(B, T, nt, rev)
    out_specs = [hin_spec, state]
    out_shape = [SDS((B, nt, 2, S5_HALF), f32), SDS((B, 2, S5_HALF), f32)]
    if need_y:
        out_specs.insert(0, tok)
        out_shape.insert(0, SDS((B, L, D_S5), f32))
    w_in, w_out = _resident((S5_BLOCKS, S5_BC, S5_BS)), _resident((S5_BLOCKS, S5_BS, S5_BC))
    return pl.pallas_call(
        body, name=name, grid=(nt,),
        in_specs=[tok, w_in, w_in, w_out, w_out, _resident((2, S5_HALF)), state],
        out_specs=out_specs, out_shape=out_shape,
        scratch_shapes=[pltpu.VMEM((B, T, S5_HALF), f32), pltpu.VMEM((B, T, S5_HALF), f32),
                        pltpu.VMEM((B, 2, S5_HALF), f32)],
        compiler_params=_cparams(1),
    )(u, bre, bim, ctop, cbot, arow, h0)


def s5_scan_bwd(u, dy, bre, bim, ctop, cbot, arow, hin, dhend, *, d, name):
    B, L, _ = u.shape
    T = min(S5_TILE, L)
    nt = L // T
    rev = d == 1
    has_dy = dy is not None
    PAD = 8

    def body(*refs):
        u_ref = refs[0]
        k = 1
        if has_dy:
            dy_ref = refs[1]
            k = 2
        bre_ref, bim_ref, ct_ref, cb_ref, a_ref, hin_ref, dhend_ref = refs[k:k + 7]
        du_ref, dbre_ref, dbim_ref, dct_ref, dcb_ref, da_ref, dh0_ref = refs[k + 7:k + 14]
        hr_scr, hi_scr, gr_scr, gi_scr, p_scr = refs[k + 14:]
        n = pl.program_id(0)

        @pl.when(n == 0)
        def _():
            for r in (dbre_ref, dbim_ref, dct_ref, dcb_ref, da_ref):
                r[...] = jnp.zeros_like(r)
            p_scr[...] = dhend_ref[...]

        ar, ai = a_ref[0:1, :], a_ref[1:2, :]
        prev_row = PAD + T if rev else PAD - 1
        uvs = []
        for b in range(B):
            uvs.append(u_ref[b].astype(bf16))
            _s5_in(uvs[b], bre_ref, bim_ref, hr_scr.at[b], hi_scr.at[b], PAD, T)
            hr_scr[b, prev_row:prev_row + 1, :] = hin_ref[b, 0:1, :]
            hi_scr[b, prev_row:prev_row + 1, :] = hin_ref[b, 1:2, :]
        _scan_rows(T, rev, ar, ai, [(hin_ref[b, 0:1, :], hin_ref[b, 1:2, :]) for b in range(B)],
                   [(hr_scr.at[b], hi_scr.at[b]) for b in range(B)], PAD)
        if has_dy:
            for b in range(B):
                dyv = dy_ref[b].astype(bf16)
                for jb in range(S5_BLOCKS):
                    st = slice(jb * S5_BS, (jb + 1) * S5_BS)
                    dyj = dyv[:, jb * S5_BC:(jb + 1) * S5_BC]
                    gr_scr[b, :, st] = _dot_nt(dyj, ct_ref[jb])
                    gi_scr[b, :, st] = _dot_nt(dyj, cb_ref[jb])
                    dct_ref[jb] += _dot_tn(hr_scr[b, PAD:PAD + T, st], dyj)
                    dcb_ref[jb] += _dot_tn(hi_scr[b, PAD:PAD + T, st], dyj)
        else:
            gr_scr[...] = jnp.zeros_like(gr_scr)
            gi_scr[...] = jnp.zeros_like(gi_scr)

        def step(i, carry):
            t = i if rev else T - 1 - i
            tp = PAD + t + (1 if rev else -1)
            out = []
            for b, (pr, pi, dar, dai) in enumerate(carry):
                gr = gr_scr[b, pl.ds(t, 1), :] + pr
                gi = gi_scr[b, pl.ds(t, 1), :] + pi
                gr_scr[b, pl.ds(t, 1), :] = gr
                gi_scr[b, pl.ds(t, 1), :] = gi
                hpr = hr_scr[b, pl.ds(tp, 1), :]
                hpi = hi_scr[b, pl.ds(tp, 1), :]
                out.append((ar * gr + ai * gi, ar * gi - ai * gr, dar + hpr * gr + hpi * gi, dai + hpr * gi - hpi * gr))
            return tuple(out)

        zero = jnp.zeros((1, S5_HALF), f32)
        res = lax.fori_loop(0, T, step, tuple((p_scr[b, 0:1, :], p_scr[b, 1:2, :], zero, zero) for b in range(B)))
        for b in range(B):
            pr, pi, dar, dai = res[b]
            p_scr[b, 0:1, :] = pr
            p_scr[b, 1:2, :] = pi
            da_ref[0:1, :] += dar
            da_ref[1:2, :] += dai
            for jb in range(S5_BLOCKS):
                st = slice(jb * S5_BS, (jb + 1) * S5_BS)
                ch = slice(jb * S5_BC, (jb + 1) * S5_BC)
                gr_j = gr_scr[b, :, st].astype(bf16)
                gi_j = gi_scr[b, :, st].astype(bf16)
                du_ref[b, :, ch] = _dot_nt(gr_j, bre_ref[jb]) + _dot_nt(gi_j, bim_ref[jb])
                dbre_ref[jb] += _dot_tn(uvs[b][:, ch], gr_j)
                dbim_ref[jb] += _dot_tn(uvs[b][:, ch], gi_j)

        @pl.when(n == nt - 1)
        def _():
            dh0_ref[...] = p_scr[...]

    tok, hin_spec, state = _s5_specs(B, T, nt, not rev)
    w_in, w_out = _resident((S5_BLOCKS, S5_BC, S5_BS)), _resident((S5_BLOCKS, S5_BS, S5_BC))
    wspecs = [w_in, w_in, w_out, w_out]
    in_specs = [tok] + ([tok] if has_dy else []) + wspecs + [_resident((2, S5_HALF)), hin_spec, state]
    args = [u] + ([dy] if has_dy else []) + [bre, bim, ctop, cbot, arow, hin, dhend]
    return pl.pallas_call(
        body, name=name, grid=(nt,), in_specs=in_specs,
        out_specs=[tok] + wspecs + [_resident((2, S5_HALF)), state],
        out_shape=[SDS((B, L, D_S5), f32), SDS((S5_BLOCKS, S5_BC, S5_BS), f32), SDS((S5_BLOCKS, S5_BC, S5_BS), f32),
                   SDS((S5_BLOCKS, S5_BS, S5_BC), f32), SDS((S5_BLOCKS, S5_BS, S5_BC), f32), SDS((2, S5_HALF), f32),
                   SDS((B, 2, S5_HALF), f32)],
        scratch_shapes=[pltpu.VMEM((B, T + 2 * PAD, S5_HALF), f32), pltpu.VMEM((B, T + 2 * PAD, S5_HALF), f32),
                        pltpu.VMEM((B, T, S5_HALF), f32), pltpu.VMEM((B, T, S5_HALF), f32),
                        pltpu.VMEM((B, 2, S5_HALF), f32)],
        compiler_params=_cparams(1),
    )(*args)


def _glu_fn(u, y0, y1, z, dsk, wg, bg):
    g = _gelu(dsk * u + y0 + y1)
    return g * jax.nn.sigmoid(_mm(g, wg) + bg) * _silu(z)


CONV_ROWS = 16


def _shift(x, s):
    L = x.shape[0]
    k = (-s) % L
    return x if k == 0 else pltpu.roll(x, k, axis=0)


def _conv_masks(L, is_ctx):
    t = lax.broadcasted_iota(jnp.int32, (L, 1), 0)
    if is_ctx:
        return t == L - 1, t == 0, None, None
    col = jnp.bitwise_and(t, GRID_W - 1)
    return col == GRID_W - 1, col == 0, t >= GRID_W, t < L - GRID_W


def _conv_sides(xv, masks):
    no_left, no_right, _, _ = masks
    return _shift(jnp.where(no_left, 0.0, xv), -1), _shift(jnp.where(no_right, 0.0, xv), 1)


def _conv_pre(xv, w_ref, masks, is_ctx):
    xl, xr = _conv_sides(xv, masks)
    z = [w_ref[3 * di:3 * di + 1, :] * xl + w_ref[3 * di + 1:3 * di + 2, :] * xv + w_ref[3 * di + 2:3 * di + 3, :] * xr
         for di in ((1,) if is_ctx else (0, 1, 2))]
    if is_ctx:
        return z[0]
    _, _, has_up, has_down = masks
    return z[1] + jnp.where(has_up, _shift(z[0], -GRID_W), 0.0) + jnp.where(has_down, _shift(z[2], GRID_W), 0.0)


def _conv_pre_bwd(xv, w_ref, dpre, masks, is_ctx, dw_ref):
    no_left, no_right, has_up, has_down = masks
    xl, xr = _conv_sides(xv, masks)
    if is_ctx:
        dz = {1: dpre}
    else:
        dz = {0: _shift(jnp.where(has_up, dpre, 0.0), GRID_W), 1: dpre, 2: _shift(jnp.where(has_down, dpre, 0.0), -GRID_W)}
    dxl = dxc = dxr = None
    for di, d in dz.items():
        for dj, side in enumerate((xl, xv, xr)):
            dw_ref[3 * di + dj:3 * di + dj + 1, :] = jnp.sum(d * side, axis=0, keepdims=True)
        tl, tc, tr = (w_ref[3 * di + dj:3 * di + dj + 1, :] * d for dj in range(3))
        dxl, dxc, dxr = (tl, tc, tr) if dxl is None else (dxl + tl, dxc + tc, dxr + tr)
    return dxc + jnp.where(no_left, 0.0, _shift(dxl, 1)) + jnp.where(no_right, 0.0, _shift(dxr, -1))


def _qk_post(pre, is_norm, scale):
    s = _silu(pre)
    nrm = lax.rsqrt(jnp.sum(s * s, axis=-1, keepdims=True) + NORM_EPS)
    return s * jnp.where(is_norm, nrm * scale, 1.0)


def _conv_kind():
    ct = pl.program_id(1)
    return ct < 2 * GDN_HEADS, jnp.where(ct < GDN_HEADS, GDN_HEAD ** -0.5, 1.0).astype(f32)


def conv_fwd(qkv, w16, *, is_ctx, name):
    B, L, C = qkv.shape
    spec = pl.BlockSpec((None, L, GDN_HEAD), lambda b, ct: (b, 0, ct))
    wspec = pl.BlockSpec((CONV_ROWS, GDN_HEAD), lambda b, ct: (0, ct))

    def body(x_ref, w_ref, o_ref, pre_ref):
        is_norm, scale = _conv_kind()
        pre = _conv_pre(x_ref[...], w_ref, _conv_masks(L, is_ctx), is_ctx)
        pre_ref[...] = pre
        o_ref[...] = _qk_post(pre, is_norm, scale)

    return pl.pallas_call(body, name=name, grid=(B, C // GDN_HEAD), in_specs=[spec, wspec], out_specs=[spec, spec],
                          out_shape=[SDS((B, L, C), f32)] * 2, compiler_params=_cparams(2))(qkv, w16)


def conv_bwd(qkv, pre, w16, da0, da1, *, is_ctx, name):
    B, L, C = qkv.shape
    spec = pl.BlockSpec((None, L, GDN_HEAD), lambda b, ct: (b, 0, ct))
    wspec = pl.BlockSpec((CONV_ROWS, GDN_HEAD), lambda b, ct: (0, ct))
    dwspec = pl.BlockSpec((None, CONV_ROWS, GDN_HEAD), lambda b, ct: (b, 0, ct))

    def body(x_ref, pre_ref, w_ref, d0_ref, d1_ref, dx_ref, dw_ref):
        is_norm, scale = _conv_kind()
        _, vjp = jax.vjp(lambda p: _qk_post(p, is_norm, scale), pre_ref[...])
        dpre = vjp(d0_ref[...] + d1_ref[...])[0]
        dw_ref[...] = jnp.zeros_like(dw_ref)
        dx_ref[...] = _conv_pre_bwd(x_ref[...], w_ref, dpre, _conv_masks(L, is_ctx), is_ctx, dw_ref)

    return pl.pallas_call(body, name=name, grid=(B, C // GDN_HEAD), in_specs=[spec, spec, wspec, spec, spec],
                          out_specs=[spec, dwspec], out_shape=[SDS((B, L, C), f32), SDS((B, CONV_ROWS, C), f32)],
                          compiler_params=_cparams(2))(qkv, pre, w16, da0, da1)


def _gates_fn(ba, alog, dtb):
    T = ba.shape[0]
    lane = lax.broadcasted_iota(jnp.int32, ba.shape, 1)
    ii = lax.broadcasted_iota(jnp.int32, (T, T), 0)
    jj = lax.broadcasted_iota(jnp.int32, (T, T), 1)
    same = jnp.right_shift(ii, 6) == jnp.right_shift(jj, 6)
    lmat = jnp.logical_and(same, ii >= jj).astype(f32)
    umat = jnp.logical_and(same, ii <= jj).astype(f32)
    g = jnp.where(lane >= 8, -jnp.exp(alog) * jax.nn.softplus(ba + dtb), 0.0)
    gc = jnp.where(lane >= 12, _dot_hi(umat, g), _dot_hi(lmat, g))
    return jnp.where(lane < 8, jax.nn.sigmoid(ba), gc)


def gates_fwd(ba, alog, dtb, *, name):
    B, L, _ = ba.shape
    T = min(TOK_TILE, L)
    t = _tok(T, N_GATE)

    def body(ba_ref, al_ref, dt_ref, o_ref):
        o_ref[...] = _gates_fn(ba_ref[...], al_ref[...], dt_ref[...])

    return pl.pallas_call(body, name=name, grid=(B, L // T),
                          in_specs=[t, _resident((1, N_GATE)), _resident((1, N_GATE))], out_specs=t,
                          out_shape=SDS((B, L, N_GATE), f32), compiler_params=_cparams(2))(ba, alog, dtb)


def gates_bwd(ba, alog, dtb, dbg, *, name):
    B, L, _ = ba.shape
    T = min(TOK_TILE, L)
    t = _tok(T, N_GATE)
    small = _resident((1, N_GATE))

    def body(ba_ref, al_ref, dt_ref, d_ref, dba_ref, dal_ref, ddt_ref):
        @pl.when(_first_step())
        def _():
            dal_ref[...] = jnp.zeros_like(dal_ref)
            ddt_ref[...] = jnp.zeros_like(ddt_ref)

        _, vjp = jax.vjp(_gates_fn, ba_ref[...], al_ref[...], dt_ref[...])
        dba, dal, ddt = vjp(d_ref[...])
        dba_ref[...] = dba
        dal_ref[...] += dal
        ddt_ref[...] += ddt

    return pl.pallas_call(body, name=name, grid=(B, L // T), in_specs=[t, small, small, t],
                          out_specs=[t, small, small],
                          out_shape=[SDS((B, L, N_GATE), f32), SDS((1, N_GATE), f32), SDS((1, N_GATE), f32)],
                          compiler_params=_cparams(2))(ba, alog, dtb, dbg)


@jax.custom_vjp
def _inv_unit_tri(mats):
    n = mats[0].shape[0]
    eye = (lax.broadcasted_iota(jnp.int32, (n, n), 0) == lax.broadcasted_iota(jnp.int32, (n, n), 1)).astype(f32)
    xs = [eye - a for a in mats]
    sq = [_dot(a, a) for a in mats]
    ps = sq
    k = 2
    while k < n:
        xs = [x + _dot(x, p) for x, p in zip(xs, ps)]
        k *= 2
        if k < n:
            ps = [_dot(p, p) for p in ps]
    return tuple(_dot(p, x) - a for p, x, a in zip(sq, xs, mats))


def _inv_unit_tri_fwd(mats):
    ns = _inv_unit_tri(mats)
    return ns, ns


def _inv_unit_tri_bwd(ns, dns):
    ys = [dn + _dot_tn(nn, dn) for nn, dn in zip(ns, dns)]
    return (tuple(-(y + _dot_nt(y, nn)) for y, nn in zip(ys, ns)),)


_inv_unit_tri.defvjp(_inv_unit_tri_fwd, _inv_unit_tri_bwd)


def _gdn_chunk(heads, *, revs):
    n = heads[0][0].shape[0]
    ii = lax.broadcasted_iota(jnp.int32, (n, n), 0)
    jj = lax.broadcasted_iota(jnp.int32, (n, n), 1)
    row = lax.broadcasted_iota(jnp.int32, (n, 1), 0)
    lower = {False: ii >= jj, True: ii <= jj}
    strict = {False: ii > jj, True: ii < jj}
    last = {False: n - 1, True: 0}
    H = range(len(heads))
    q, k, v, beta, gc, gr, s = (list(t) for t in zip(*heads))
    decay = [jnp.where(lower[revs[h]], jnp.exp(jnp.where(lower[revs[h]], gc[h] - gr[h], 0.0)), 0.0) for h in H]
    kk = [_mm_nt(k[h], k[h]) for h in H]
    qk = [_mm_nt(q[h], k[h]) * decay[h] for h in H]
    qs = [_mm(q[h], s[h]) for h in H]
    a_mat = tuple(jnp.where(strict[revs[h]], beta[h] * kk[h] * decay[h], 0.0) for h in H)
    gamma = [jnp.exp(gc[h]) for h in H]
    g_last = [jnp.sum(jnp.where(row == last[revs[h]], gc[h], 0.0), axis=0, keepdims=True) for h in H]
    nmat = _inv_unit_tri(a_mat)
    bv = [beta[h] * v[h] for h in H]
    bk = [(beta[h] * gamma[h]) * k[h] for h in H]
    u0 = [bv[h] + _mm(nmat[h], bv[h]) for h in H]
    w = [bk[h] + _mm(nmat[h], bk[h]) for h in H]
    k_out = [k[h] * jnp.exp(g_last[h] - gc[h]) for h in H]
    u = [u0[h] - _mm(w[h], s[h]) for h in H]
    o = [gamma[h] * qs[h] + _mm(qk[h], u[h]) for h in H]
    s_new = [jnp.exp(g_last[h]) * s[h] + _mm_tn(k_out[h], u[h]) for h in H]
    return tuple((o[h], s_new[h]) for h in H)


def _gdn_specs(B, nc, rev):
    def cidx(n):
        return (nc - 1 - n) if rev else n
    tok = lambda width: pl.BlockSpec((B, CHUNK, width), lambda n: (0, cidx(n), 0))
    rowspec = pl.BlockSpec((B, None, N_GATE, CHUNK), lambda n: (0, cidx(n), 0, 0))
    st = pl.BlockSpec((B, GDN_HEADS, GDN_HEAD, GDN_HEAD), lambda n: (0, 0, 0, 0))
    ck = pl.BlockSpec((B, None, GDN_HEADS, GDN_HEAD, GDN_HEAD), lambda n: (0, cidx(n), 0, 0, 0))
    return tok, rowspec, st, ck


def _gdn_head_args(qkv_ref, bg_ref, bgr_ref, b, d, h):
    col = d * GDN_HEADS + h
    q = qkv_ref[b, :, h * GDN_HEAD:(h + 1) * GDN_HEAD]
    k = qkv_ref[b, :, D_GDN + h * GDN_HEAD:D_GDN + (h + 1) * GDN_HEAD]
    v = qkv_ref[b, :, 2 * D_GDN + h * GDN_HEAD:2 * D_GDN + (h + 1) * GDN_HEAD]
    bgv = bg_ref[b]
    return q, k, v, bgv[:, col:col + 1], bgv[:, 8 + col:9 + col], bgr_ref[b][8 + col:9 + col, :]


def _gdn_chains(B):
    return [(d, b, h) for d in range(N_DIR) for b in range(B) for h in range(GDN_HEADS)]


def gdn_fwd(qkv, bg, bgr, s0s, *, need_o, name):
    B, L, _ = qkv.shape
    nc = L // CHUNK
    specs = [_gdn_specs(B, nc, d == 1) for d in range(N_DIR)]
    chains = _gdn_chains(B)
    state_shape = (B, GDN_HEADS, GDN_HEAD, GDN_HEAD)

    def body(*refs):
        ins = [refs[3 * d:3 * d + 3] for d in range(N_DIR)]
        s0_refs = refs[6:8]
        k = 8
        o_refs = refs[k:k + 2] if need_o else None
        k += 2 if need_o else 0
        ck_refs, sf_refs, s_scrs = refs[k:k + 2], refs[k + 2:k + 4], refs[k + 4:k + 6]
        n = pl.program_id(0)

        @pl.when(n == 0)
        def _():
            for d in range(N_DIR):
                s_scrs[d][...] = s0_refs[d][...]

        for d in range(N_DIR):
            ck_refs[d][...] = s_scrs[d][...]
        heads = tuple(_gdn_head_args(*ins[d], b, d, h) + (s_scrs[d][b, h],) for d, b, h in chains)
        outs = _gdn_chunk(heads, revs=tuple(d == 1 for d, _, _ in chains))
        for (d, b, h), (o, s_new) in zip(chains, outs):
            if need_o:
                o_refs[d][b, :, h * GDN_HEAD:(h + 1) * GDN_HEAD] = o
            s_scrs[d][b, h] = s_new

        @pl.when(n == nc - 1)
        def _():
            for d in range(N_DIR):
                sf_refs[d][...] = s_scrs[d][...]

    in_specs, out_o, out_ck, out_sf = [], [], [], []
    for tok, rowspec, st, ck in specs:
        in_specs += [tok(3 * D_GDN), tok(N_GATE), rowspec]
        out_o.append(tok(D_GDN))
        out_ck.append(ck)
        out_sf.append(st)
    in_specs += [specs[0][2]] * 2
    out_specs = (out_o if need_o else []) + out_ck + out_sf
    out_shape = (([SDS((B, L, D_GDN), f32)] * 2 if need_o else [])
                 + [SDS((B, nc) + state_shape[1:], f32)] * 2 + [SDS(state_shape, f32)] * 2)
    res = pl.pallas_call(
        body, name=name, grid=(nc,), in_specs=in_specs, out_specs=out_specs, out_shape=out_shape,
        scratch_shapes=[pltpu.VMEM(state_shape, f32)] * 2, compiler_params=_cparams(1),
    )(qkv, bg, bgr, qkv, bg, bgr, *s0s)
    if need_o:
        return res[0:2], res[2:4], res[4:6]
    return res[0:2], res[2:4]


def gdn_bwd(qkv, bg, bgr, cks, do, dsfs, *, name):
    B, L, _ = qkv.shape
    nc = L // CHUNK
    has_do = do is not None
    specs = [_gdn_specs(B, nc, d != 1) for d in range(N_DIR)]
    chains = _gdn_chains(B)
    state_shape = (B, GDN_HEADS, GDN_HEAD, GDN_HEAD)
    per_dir = 5 if has_do else 4

    def body(*refs):
        ins = [refs[per_dir * d:per_dir * d + per_dir] for d in range(N_DIR)]
        k = per_dir * N_DIR
        dsf_refs = refs[k:k + 2]
        outs = [refs[k + 2 + 3 * d:k + 5 + 3 * d] for d in range(N_DIR)]
        ds0_refs, ds_scrs = refs[k + 8:k + 10], refs[k + 10:k + 12]
        n = pl.program_id(0)

        @pl.when(n == 0)
        def _():
            for d in range(N_DIR):
                ds_scrs[d][...] = dsf_refs[d][...]

        lane = lax.broadcasted_iota(jnp.int32, (CHUNK, N_GATE), 1)
        sub = lax.broadcasted_iota(jnp.int32, (N_GATE, CHUNK), 0)
        heads = tuple(_gdn_head_args(*ins[d][:3], b, d, h) + (ins[d][3][b, h],) for d, b, h in chains)
        _, vjp = jax.vjp(functools.partial(_gdn_chunk, revs=tuple(d == 1 for d, _, _ in chains)), heads)
        zero = jnp.zeros((CHUNK, GDN_HEAD), f32)
        cts = tuple(((ins[d][4][b, :, h * GDN_HEAD:(h + 1) * GDN_HEAD] if has_do else zero), ds_scrs[d][b, h])
                    for d, b, h in chains)
        (dheads,) = vjp(cts)
        dbg_acc = [[jnp.zeros((CHUNK, N_GATE), f32) for _ in range(B)] for _ in range(N_DIR)]
        dbgr_acc = [[jnp.zeros((N_GATE, CHUNK), f32) for _ in range(B)] for _ in range(N_DIR)]
        for (d, b, h), (dq, dk, dv, db, dgc, dgr, ds) in zip(chains, dheads):
            col = d * GDN_HEADS + h
            dqkv_ref = outs[d][0]
            dqkv_ref[b, :, h * GDN_HEAD:(h + 1) * GDN_HEAD] = dq
            dqkv_ref[b, :, D_GDN + h * GDN_HEAD:D_GDN + (h + 1) * GDN_HEAD] = dk
            dqkv_ref[b, :, 2 * D_GDN + h * GDN_HEAD:2 * D_GDN + (h + 1) * GDN_HEAD] = dv
            dbg_acc[d][b] = dbg_acc[d][b] + jnp.where(lane == col, db, 0.0) + jnp.where(lane == 8 + col, dgc, 0.0)
            dbgr_acc[d][b] = dbgr_acc[d][b] + jnp.where(sub == 8 + col, dgr, 0.0)
            ds_scrs[d][b, h] = ds
        for d in range(N_DIR):
            for b in range(B):
                outs[d][1][b] = dbg_acc[d][b]
                outs[d][2][b] = dbgr_acc[d][b]

        @pl.when(n == nc - 1)
        def _():
            for d in range(N_DIR):
                ds0_refs[d][...] = ds_scrs[d][...]

    in_specs, args, out_specs, out_shape = [], [], [], []
    for d, (tok, rowspec, st, ck) in enumerate(specs):
        in_specs += [tok(3 * D_GDN), tok(N_GATE), rowspec, ck] + ([tok(D_GDN)] if has_do else [])
        args += [qkv, bg, bgr, cks[d]] + ([do] if has_do else [])
        out_specs += [tok(3 * D_GDN), tok(N_GATE), rowspec]
        out_shape += [SDS((B, L, 3 * D_GDN), f32), SDS((B, L, N_GATE), f32), SDS((B, nc, N_GATE, CHUNK), f32)]
    st = specs[0][2]
    in_specs += [st, st]
    args += list(dsfs)
    out_specs += [st, st]
    out_shape += [SDS(state_shape, f32)] * 2
    res = pl.pallas_call(
        body, name=name, grid=(nc,), in_specs=in_specs, out_specs=out_specs, out_shape=out_shape,
        scratch_shapes=[pltpu.VMEM(state_shape, f32)] * 2, compiler_params=_cparams(1),
    )(*args)
    return (res[0], res[3]), (res[1], res[4]), (res[2], res[5]), (res[6], res[7])


def _gnorm_fn(o0, o1, z, w):
    o = o0 + o1
    return o * lax.rsqrt(jnp.mean(o * o, axis=-1, keepdims=True) + NORM_EPS) * w * _silu(z)


def _head_loss(y, x, gate, lng, lnb, tgt):
    r = DEEPNORM_ALPHA * x + gate * y
    mu = jnp.mean(r, axis=-1, keepdims=True)
    rc = r - mu
    var = jnp.mean(rc * rc, axis=-1, keepdims=True)
    err = rc * lax.rsqrt(var + LN_EPS) * lng + lnb - tgt
    return (0.5 / D_MODEL) * jnp.sum(jnp.sum(err * err, axis=-1, keepdims=True), axis=0, keepdims=True)


def tail_fwd_bwd(u, y0, y1, z_s5, o0, o1, z_gdn, x, tgt, gate, lng, lnb, ws, wg, dsk, wglu, bglu, nw):
    B, L, _ = x.shape
    T = min(TOK_TILE, L)

    def body(u_ref, y0_ref, y1_ref, z_ref, o0_ref, o1_ref, zg_ref, x_ref, t_ref, gate_ref, lng_ref, lnb_ref, ws_ref,
             wg_ref, dsk_ref, wglu_ref, bglu_ref, nw_ref,
             loss_ref, du_ref, dys_ref, dz_ref, do_ref, dzg_ref, gx_ref, dws_ref, dwg_ref, dgate_ref, dlng_ref, dlnb_ref,
             ddsk_ref, dwglu_ref, dbglu_ref, dnw_ref):
        n = pl.program_id(1)

        @pl.when(_first_step())
        def _():
            for r in (dws_ref, dwg_ref, dlng_ref, dlnb_ref, ddsk_ref, dwglu_ref, dbglu_ref, dnw_ref):
                r[...] = jnp.zeros_like(r)

        @pl.when(n == 0)
        def _():
            loss_ref[...] = jnp.zeros_like(loss_ref)
            dgate_ref[...] = jnp.zeros_like(dgate_ref)

        s5o, glu_vjp = jax.vjp(_glu_fn, u_ref[...], y0_ref[...], y1_ref[...], z_ref[...], dsk_ref[...],
                               wglu_ref[...].astype(f32), bglu_ref[...])
        heads = []
        for h in range(GDN_HEADS):
            sl = slice(h * GDN_HEAD, (h + 1) * GDN_HEAD)
            heads.append(jax.vjp(_gnorm_fn, o0_ref[:, sl], o1_ref[:, sl], zg_ref[:, sl], nw_ref[...]))
        sv = s5o.astype(bf16)
        gv = jnp.concatenate([out for out, _ in heads], axis=1).astype(bf16)
        y = _dot(sv, ws_ref[...]) + _dot(gv, wg_ref[...])
        loss, vjp = jax.vjp(lambda *a: _head_loss(*a, t_ref[...]), y, x_ref[...], gate_ref[...], lng_ref[...],
                            lnb_ref[...])
        dy, dx, dgate, dlng, dlnb = vjp(jnp.ones((1, 1), f32))
        loss_ref[...] += jnp.broadcast_to(loss, loss_ref.shape)
        dyb = dy.astype(bf16)
        gx_ref[...] = dx
        dws_ref[...] += _dot_tn(sv, dyb)
        dwg_ref[...] += _dot_tn(gv, dyb)
        dgate_ref[...] += dgate
        dlng_ref[...] += dlng
        dlnb_ref[...] += dlnb
        du, dys, _, dz, ddsk, dwglu, dbglu = glu_vjp(_dot_nt(dyb, ws_ref[...]))
        du_ref[...], dys_ref[...], dz_ref[...] = du, dys, dz
        ddsk_ref[...] += ddsk
        dwglu_ref[...] += dwglu
        dbglu_ref[...] += dbglu
        dgdo = _dot_nt(dyb, wg_ref[...])
        for h, (_, hvjp) in enumerate(heads):
            sl = slice(h * GDN_HEAD, (h + 1) * GDN_HEAD)
            do, _, dzg, dnw = hvjp(dgdo[:, sl])
            do_ref[:, sl] = do
            dzg_ref[:, sl] = dzg
            dnw_ref[...] += dnw

    half, full = _tok(T, D_S5), _tok(T, D_MODEL)
    row = _resident((1, D_MODEL))
    wsp = _resident((D_S5, D_MODEL))
    r512, rglu, r128 = _resident((1, D_S5)), _resident((D_S5, D_S5)), _resident((1, GDN_HEAD))
    return pl.pallas_call(
        body, name="tail_fwd_bwd", grid=(B, L // T),
        in_specs=[half] * 7 + [full, full, _per_batch(1, D_MODEL), row, row, wsp, wsp, r512, rglu, r512, r128],
        out_specs=[_per_batch(8, LANES)] + [half] * 5 + [full, wsp, wsp, _per_batch(1, D_MODEL), row, row, r512, rglu, r512,
                                                           r128],
        out_shape=[SDS((B, 8, LANES), f32)] + [SDS((B, L, D_S5), f32)] * 5 + [
            SDS((B, L, D_MODEL), f32), SDS((D_S5, D_MODEL), f32), SDS((D_GDN, D_MODEL), f32), SDS((B, 1, D_MODEL), f32),
            SDS((1, D_MODEL), f32), SDS((1, D_MODEL), f32), SDS((1, D_S5), f32), SDS((D_S5, D_S5), f32), SDS((1, D_S5), f32),
            SDS((1, GDN_HEAD), f32)],
        compiler_params=_cparams(2),
    )(u, y0, y1, z_s5, o0, o1, z_gdn, x, tgt, gate, lng, lnb, ws, wg, dsk, wglu, bglu, nw)


def _adamw_math(w, g, m, v):
    nm = ADAM_B1 * m + (1.0 - ADAM_B1) * g
    nv = ADAM_B2 * v + (1.0 - ADAM_B2) * jnp.square(g)
    m_hat = nm / (1.0 - ADAM_B1 ** ADAM_STEP)
    v_hat = nv / (1.0 - ADAM_B2 ** ADAM_STEP)
    return -ADAM_LR * (m_hat / (jnp.sqrt(v_hat) + ADAM_EPS) + ADAM_WD * w), nm, nv


def _row_tile(rows, cap=512):
    for t in range(min(cap, rows), 15, -1):
        if rows % t == 0 and t % 16 == 0:
            return t
    return rows


def adamw_3d(w, g, m, v, *, lead=False, name):
    R, C = (w.shape[0], w.shape[2]) if lead else w.shape[1:]
    if lead:
        T = next(t for t in range(min(256, R), 0, -1) if R % t == 0)
        spec = pl.BlockSpec((T, 1, C), lambda i: (i, 0, 0))
    else:
        T = _row_tile(R)
        spec = pl.BlockSpec((None, T, C), lambda i: (0, i, 0))

    def body(w_ref, g_ref, m_ref, v_ref, d_ref, nm_ref, nv_ref):
        d_ref[...], nm_ref[...], nv_ref[...] = _adamw_math(w_ref[...], g_ref[...], m_ref[...], v_ref[...])

    return pl.pallas_call(body, name=name, grid=(R // T,), in_specs=[spec] * 4, out_specs=[spec] * 3,
                          out_shape=[SDS(w.shape, f32)] * 3, compiler_params=_cparams(1))(w, g, m, v)


def adamw_small(ws, gs, ms, vs):
    n = len(ws)

    def body(*refs):
        outs = refs[4 * n:]
        for i in range(n):
            d, nm, nv = _adamw_math(refs[i][...], refs[n + i][...], refs[2 * n + i][...], refs[3 * n + i][...])
            outs[i][...], outs[n + i][...], outs[2 * n + i][...] = d, nm, nv

    res = pl.pallas_call(body, name="adamw_small", out_shape=[SDS(w.shape, f32) for w in ws] * 3,
                         compiler_params=pltpu.CompilerParams(vmem_limit_bytes=VMEM_LIMIT))(*ws, *gs, *ms, *vs)
    return res[:n], res[n:2 * n], res[2 * n:]


def sum_cores(own, got, *, name):
    A, H, C = own.shape
    T = _row_tile(H)
    spec = pl.BlockSpec((None, T, C), lambda a, i: (a, i, 0))

    def body(a_ref, b_ref, q32_ref, q16_ref):
        q = a_ref[...] + b_ref[...]
        q32_ref[...] = q
        q16_ref[...] = q.astype(bf16)

    return pl.pallas_call(body, name=name, grid=(A, H // T), in_specs=[spec, spec], out_specs=[spec, spec],
                          out_shape=[SDS((A, H, C), f32), SDS((A, H, C), bf16)], compiler_params=_cparams(2))(own, got)


def sum_chips(mine, rec, cpos, *, name):
    H, C = mine.shape
    T = _row_tile(H)
    nt = H // T

    def body(c_ref, m_ref, r_ref, f_ref):
        f_ref[...] = ((m_ref[...] + r_ref[0].astype(f32)) + r_ref[1].astype(f32)) + r_ref[2].astype(f32)

    grid_spec = pltpu.PrefetchScalarGridSpec(
        num_scalar_prefetch=1, grid=(nt,),
        in_specs=[pl.BlockSpec((T, C), lambda i, c_ref: (i, 0)), pl.BlockSpec((3, T, C), lambda i, c_ref: (0, i, 0))],
        out_specs=pl.BlockSpec((None, T, C), lambda i, c_ref: (0, c_ref[0] * nt + i, 0)))
    return pl.pallas_call(body, name=name, grid_spec=grid_spec, out_shape=SDS((1, 2 * H, C), f32),
                          compiler_params=_cparams(1))(cpos.reshape(1).astype(jnp.int32), mine, rec)


CHIP_FLIPS = ((1, 0), (0, 1), (1, 1))


def _pos():
    return lax.axis_index("x"), lax.axis_index("y"), lax.axis_index("c")


def _comm_call(body, srcs, out_sds, n_remote, n_local, name):
    any_spec = pl.BlockSpec(memory_space=pl.ANY)
    return pl.pallas_call(
        body, name=name, in_specs=[any_spec] * len(srcs), out_specs=[any_spec] * len(out_sds), out_shape=out_sds,
        scratch_shapes=[pltpu.SemaphoreType.DMA((n_remote,)), pltpu.SemaphoreType.DMA((n_remote,)),
                        pltpu.SemaphoreType.DMA((max(n_local, 1),))],
        compiler_params=pltpu.CompilerParams(has_side_effects=True),
    )(*srcs)


def _remote(src, dst, send_sems, recv_sems, k, target):
    return pltpu.make_async_remote_copy(src, dst, send_sems.at[k], recv_sems.at[k], device_id=target,
                                        device_id_type=MESH)


def _half_rows(c, rows):
    half = rows // 2
    return pl.ds(pl.multiple_of(c * half, 8), half)


def gather_shards(shards):
    nt = len(shards)

    def body(*refs):
        srcs, outs = refs[:nt], refs[nt:2 * nt]
        send_sems, recv_sems, _ = refs[2 * nt:]
        x, y, c = _pos()
        j = 2 * x + y
        sib = (x, y, 1 - c)
        own = [_remote(srcs[t], outs[t].at[j], send_sems, recv_sems, 7 * t + 6, sib) for t in range(nt)]
        first, passed = [], []
        for k, (fx, fy) in enumerate(CHIP_FLIPS):
            tx, ty = x ^ fx, y ^ fy
            jk = 2 * tx + ty
            for t in range(nt):
                rows = _half_rows(c, srcs[t].shape[0])
                first.append(_remote(srcs[t].at[rows], outs[t].at[j, rows], send_sems, recv_sems, 7 * t + k, (tx, ty, c)))
                passed.append(_remote(outs[t].at[jk, rows], outs[t].at[jk, rows], send_sems, recv_sems, 7 * t + 3 + k, sib))
        for cp in first + own:
            cp.start()
        for a, b in zip(first, passed):
            a.wait_recv()
            b.start()
        for cp in passed + own:
            cp.wait_recv()
        for cp in first + passed + own:
            cp.wait_send()

    return _comm_call(body, shards, [SDS((4,) + s.shape, s.dtype) for s in shards], 7 * nt, 0, "gather_shards")


def swap_halves(ps):
    nt = len(ps)

    def body(*refs):
        srcs, outs = refs[:nt], refs[nt:2 * nt]
        send_sems, recv_sems, _ = refs[2 * nt:]
        x, y, c = _pos()
        cps = [_remote(srcs[t].at[a, _half_rows(1 - c, srcs[t].shape[1])], outs[t].at[a], send_sems, recv_sems, 4 * t + a,
                       (x, y, 1 - c)) for t in range(nt) for a in range(4)]
        for cp in cps:
            cp.start()
        for cp in cps:
            cp.wait()

    return _comm_call(body, ps, [SDS((4, p.shape[1] // 2, p.shape[2]), p.dtype) for p in ps], 4 * nt, 0, "swap_halves")


def scatter_to_chips(qs):
    nt = len(qs)

    def body(*refs):
        srcs, outs = refs[:nt], refs[nt:2 * nt]
        send_sems, recv_sems, _ = refs[2 * nt:]
        x, y, c = _pos()
        cps = []
        for k, (fx, fy) in enumerate(CHIP_FLIPS):
            tx, ty = x ^ fx, y ^ fy
            for t in range(nt):
                cps.append(_remote(srcs[t].at[2 * tx + ty], outs[t].at[k], send_sems, recv_sems, 3 * t + k, (tx, ty, c)))
        for cp in cps:
            cp.start()
        for cp in cps:
            cp.wait()

    return _comm_call(body, qs, [SDS((3,) + q.shape[1:], q.dtype) for q in qs], 3 * nt, 0, "scatter_to_chips")


def join_halves(fs):
    nt = len(fs)

    def body(*refs):
        outs = refs[nt:2 * nt]
        send_sems, recv_sems, _ = refs[2 * nt:]
        x, y, c = _pos()
        cps = []
        for t in range(nt):
            mine = outs[t].at[0, _half_rows(c, outs[t].shape[1])]
            cps.append(_remote(mine, mine, send_sems, recv_sems, t, (x, y, 1 - c)))
        for cp in cps:
            cp.start()
        for cp in cps:
            cp.wait()

    any_spec = pl.BlockSpec(memory_space=pl.ANY)
    return pl.pallas_call(
        body, name="join_halves", in_specs=[any_spec] * nt, out_specs=[any_spec] * nt,
        out_shape=[SDS(f.shape, f.dtype) for f in fs], input_output_aliases={t: t for t in range(nt)},
        scratch_shapes=[pltpu.SemaphoreType.DMA((nt,)), pltpu.SemaphoreType.DMA((nt,)), pltpu.SemaphoreType.DMA((1,))],
        compiler_params=pltpu.CompilerParams(has_side_effects=True),
    )(*fs)


def gather_small(s):
    def body(src, out, send_sems, recv_sems, _):
        x, y, c = _pos()
        j = 2 * x + y
        cps = [_remote(src, out.at[j], send_sems, recv_sems, k, (x ^ fx, y ^ fy, c)) for k, (fx, fy) in enumerate(CHIP_FLIPS)]
        cps.append(_remote(src, out.at[j], send_sems, recv_sems, 3, (x, y, 1 - c)))
        for cp in cps:
            cp.start()
        for cp in cps:
            cp.wait()

    return _comm_call(body, [s], [SDS((4,) + s.shape, s.dtype)], 4, 0, "gather_small")[0]


SMALL_SHAPES = ((D_MODEL,), (1, 3 * D_MODEL), (1, 2, 32, 64), (1, 2, 32, 64), (1, 2, 32), (1, 2, 32, 16, 64),
                (1, 2, 32, 16, 64), (1, 2, 32, 16, 64), (1, 2, 32, 16, 64), (1, D_S5), (1, D_S5), (1, 2, 4), (1, 2, 4),
                (1, GDN_HEAD), (1, D_MODEL), (1, D_MODEL))
SMALL_SWAPPED = (5, 6)


def _size(shape):
    return functools.reduce(lambda p, q: p * q, shape)


SMALL_ROWS = tuple(-(-_size(s) // LANES) for s in SMALL_SHAPES)
SMALL_TOTAL = 2176
SMALL_QUARTER = SMALL_TOTAL // 4


def _rows(a):
    flat = a.reshape(-1)
    pad = (-flat.shape[0]) % LANES
    if pad:
        flat = jnp.concatenate([flat, jnp.zeros((pad,), flat.dtype)])
    return flat.reshape(-1, LANES)


def _pack_small(parts):
    rows = [_rows(p) for p in parts]
    rows.append(jnp.zeros((SMALL_TOTAL - sum(SMALL_ROWS), LANES), f32))
    return jnp.concatenate(rows, axis=0)


def _unpack_small(buf):
    out, r = [], 0
    for s, n in zip(SMALL_SHAPES, SMALL_ROWS):
        out.append(buf[r:r + n].reshape(-1)[:_size(s)].reshape(s))
        r += n
    return out


def _as_2d(a):
    return a.reshape(1, -1) if a.ndim == 1 else a.reshape(-1, a.shape[-1])


S5_BG = S5_GROUPS // S5_BLOCKS


def _block_diag_in(bb):
    eye = jnp.eye(S5_BG, dtype=bb.dtype)
    b4 = bb.reshape(S5_BLOCKS, S5_BG, S5_GROUP, S5_STATE)
    return jnp.einsum('jgcp,gh->jgchp', b4, eye).reshape(S5_BLOCKS, S5_BC, S5_BS)


def _block_diag_in_t(d):
    d6 = d.reshape(S5_BLOCKS, S5_BG, S5_GROUP, S5_BG, S5_STATE)
    return jnp.einsum('jgcgp->jgcp', d6).reshape(S5_GROUPS, S5_GROUP * S5_STATE)


def _block_diag_out(cm):
    eye = jnp.eye(S5_BG, dtype=cm.dtype)
    c4 = cm.reshape(S5_BLOCKS, S5_BG, S5_GROUP, S5_STATE)
    return jnp.einsum('jgcp,gh->jhpgc', c4, eye).reshape(S5_BLOCKS, S5_BS, S5_BC)


def _block_diag_out_t(d):
    d6 = d.reshape(S5_BLOCKS, S5_BG, S5_STATE, S5_BG, S5_GROUP)
    return jnp.einsum('jgpgc->jgcp', d6).reshape(S5_GROUPS, S5_GROUP, S5_STATE)


def _to_chunk_rows(a):
    B, L, W = a.shape
    return a.reshape(B, L // CHUNK, CHUNK, W).transpose(0, 1, 3, 2)


def _from_chunk_rows(a):
    B, nc, W, _ = a.shape
    return a.transpose(0, 1, 3, 2).reshape(B, nc * CHUNK, W)


def local_step(x, c, ctx, c_ctx, tgt, w_ada, b_ada, w_in, lam_re, lam_im, log_dt, b_re, b_im, c_re, c_im, s5_d,
               w_glu, b_glu, conv16, a_log, dt_bias, norm_w, w_out, ln_g, ln_b):
    B, L, _ = x.shape
    zeros_state = jnp.zeros((B, GDN_HEADS, GDN_HEAD, GDN_HEAD), f32)

    cc = jnp.concatenate([c, c_ctx[None, :], jnp.zeros((8 - B - 1, D_MODEL), f32)], axis=0)
    m = ada_fwd(cc, w_ada, b_ada)
    shift, scale, gate = m[:B, :D_MODEL], m[:B, D_MODEL:2 * D_MODEL], m[:B, 2 * D_MODEL:]
    mod = jnp.stack([scale, shift], axis=1)
    mod_c = jnp.broadcast_to(jnp.stack([m[B, D_MODEL:2 * D_MODEL], m[B, :D_MODEL]], axis=0)[None], (B, 2, D_MODEL))

    u, z_s5, qkv, z_gdn, ba = in_proj_fwd(x, mod, w_in, name="in_proj_fwd")
    uc, _, qkvc, _, bac = in_proj_fwd(ctx, mod_c, w_in, name="in_proj_fwd_ctx")

    ng = N_DIR * S5_GROUPS
    zoh_in = (lam_re.reshape(ng, S5_STATE), lam_im.reshape(ng, S5_STATE), log_dt.reshape(ng, 1),
              b_re.reshape(ng, S5_GROUP * S5_STATE), b_im.reshape(ng, S5_GROUP * S5_STATE))
    expand = (jnp.arange(S5_GROUP * S5_STATE)[None, :] % S5_STATE == jnp.arange(S5_STATE)[:, None]).astype(f32)
    ar, ai, bbr, bbi = s5_zoh_fwd(*zoh_in, expand)
    bbr16, bbi16 = bbr.astype(bf16), bbi.astype(bf16)
    c_re16 = c_re.reshape(N_DIR, S5_GROUPS, S5_GROUP, S5_STATE).astype(bf16)
    c_im16 = (-c_im).reshape(N_DIR, S5_GROUPS, S5_GROUP, S5_STATE).astype(bf16)
    s5w, ys, hins, hins_c = [], [], [], []
    for d in range(N_DIR):
        g = slice(d * S5_GROUPS, (d + 1) * S5_GROUPS)
        wd = (_block_diag_in(bbr16[g]), _block_diag_in(bbi16[g]), _block_diag_out(c_re16[d]), _block_diag_out(c_im16[d]),
              jnp.stack([ar[g].reshape(-1), ai[g].reshape(-1)], axis=0))
        s5w.append(wd)
        hin_c, hend_c = s5_scan_fwd(uc, *wd, jnp.zeros((B, 2, S5_HALF), f32), d=d, need_y=False, name=f"s5_fwd_ctx{d}")
        y_d, hin, _ = s5_scan_fwd(u, *wd, hend_c, d=d, need_y=True, name=f"s5_fwd{d}")
        ys.append(y_d)
        hins.append(hin)
        hins_c.append(hin_c)
    glu_w = (s5_d.reshape(1, D_S5), w_glu, b_glu.reshape(1, D_S5))

    act, pre = conv_fwd(qkv, conv16, is_ctx=False, name="conv_fwd")
    act_c, pre_c = conv_fwd(qkvc, conv16, is_ctx=True, name="conv_fwd_ctx")
    pad8 = jnp.zeros((1, 8), f32)
    alog16 = jnp.concatenate([pad8, a_log.reshape(1, 8)], axis=1)
    dtb16 = jnp.concatenate([pad8, dt_bias.reshape(1, 8)], axis=1)
    bg = gates_fwd(ba, alog16, dtb16, name="gates_fwd")
    bg_c = gates_fwd(bac, alog16, dtb16, name="gates_fwd_ctx")
    bgr, bgr_c = _to_chunk_rows(bg), _to_chunk_rows(bg_c)
    cks_c, s_c = gdn_fwd(act_c, bg_c, bgr_c, (zeros_state, zeros_state), need_o=False, name="gdn_fwd_ctx")
    os_, cks, _ = gdn_fwd(act, bg, bgr, s_c, need_o=True, name="gdn_fwd")
    nw = norm_w.reshape(1, GDN_HEAD)

    (loss8, du_skip, dy, dz_s5, do, dz_gdn, gx_res, dws, dwg, dgate, dlng, dlnb, d_s5_d, d_w_glu, d_b_glu,
     d_norm_w) = tail_fwd_bwd(u, ys[0], ys[1], z_s5, os_[0], os_[1], z_gdn, x, tgt, gate[:, None, :],
                              ln_g.reshape(1, D_MODEL), ln_b.reshape(1, D_MODEL), w_out[:D_S5], w_out[D_S5:], *glu_w, nw)
    loss = jnp.sum(loss8[:, 0, 0])
    d_w_out = jnp.concatenate([dws, dwg], axis=0)

    dacts, dbgs, dbgrs, ds0s = gdn_bwd(act, bg, bgr, cks, do, (zeros_state, zeros_state), name="gdn_bwd")
    dacts_c, dbgs_c, dbgrs_c, _ = gdn_bwd(act_c, bg_c, bgr_c, cks_c, None, ds0s, name="gdn_bwd_ctx")
    dbg = dbgs[0] + dbgs[1] + _from_chunk_rows(dbgrs[0] + dbgrs[1])
    dbg_c = dbgs_c[0] + dbgs_c[1] + _from_chunk_rows(dbgrs_c[0] + dbgrs_c[1])
    dba, dal, ddt = gates_bwd(ba, alog16, dtb16, dbg, name="gates_bwd")
    dbac, dal_c, ddt_c = gates_bwd(bac, alog16, dtb16, dbg_c, name="gates_bwd_ctx")
    d_a_log = (dal + dal_c)[:, 8:].reshape(1, N_DIR, GDN_HEADS)
    d_dt_bias = (ddt + ddt_c)[:, 8:].reshape(1, N_DIR, GDN_HEADS)
    dqkv, dcw = conv_bwd(qkv, pre, conv16, dacts[0], dacts[1], is_ctx=False, name="conv_bwd")
    dqkvc, dcw_c = conv_bwd(qkvc, pre_c, conv16, dacts_c[0], dacts_c[1], is_ctx=True, name="conv_bwd_ctx")
    d_conv16 = jnp.sum(dcw, axis=0) + jnp.sum(dcw_c, axis=0)

    dus, ducs = [du_skip], []
    dar, dai, dbbr, dbbi, dcre, dcim = [], [], [], [], [], []
    for d in range(N_DIR):
        du_d, dbre1, dbim1, dct1, dcb1, da1, dh0 = s5_scan_bwd(u, dy, *s5w[d], hins[d],
                                                                jnp.zeros((B, 2, S5_HALF), f32), d=d, name=f"s5_bwd{d}")
        duc_d, dbre2, dbim2, _, _, da2, _ = s5_scan_bwd(uc, None, *s5w[d], hins_c[d], dh0, d=d, name=f"s5_bwd_ctx{d}")
        dus.append(du_d)
        ducs.append(duc_d)
        da = da1 + da2
        dar.append(da[0].reshape(S5_GROUPS, S5_STATE))
        dai.append(da[1].reshape(S5_GROUPS, S5_STATE))
        dbbr.append(_block_diag_in_t(dbre1 + dbre2))
        dbbi.append(_block_diag_in_t(dbim1 + dbim2))
        dcre.append(_block_diag_out_t(dct1))
        dcim.append(-_block_diag_out_t(dcb1))
    dlr, dli, dldt, dbre, dbim = s5_zoh_bwd(*zoh_in, expand, jnp.concatenate(dar, 0), jnp.concatenate(dai, 0),
                                            jnp.concatenate(dbbr, 0), jnp.concatenate(dbbi, 0))
    d_s5 = (dlr, dli, dldt, dbre, dbim, jnp.stack(dcre, 0), jnp.stack(dcim, 0))

    padg = lambda a: jnp.concatenate([a, jnp.zeros(a.shape[:2] + (LANES - N_GATE,), f32)], axis=2)
    zc = jnp.zeros_like(uc)
    dw_c, dmod_c = in_proj_bwd(ctx, mod_c, (tuple(ducs), zc, dqkvc, zc, padg(dbac)), w_in, None, None,
                               name="in_proj_bwd_ctx")
    d_w_in, dmod, grad_x = in_proj_bwd(x, mod, (tuple(dus), dz_s5, dqkv, dz_gdn, padg(dba)), w_in, gx_res, dw_c,
                                       name="in_proj_bwd")
    dmod_c = jnp.sum(dmod_c, axis=0)

    dm_rows = jnp.concatenate([dmod[:, 1], dmod[:, 0], dgate[:, 0]], axis=1)
    dm_ctx = jnp.concatenate([dmod_c[1], dmod_c[0], jnp.zeros((D_MODEL,), f32)])[None]
    dm = jnp.concatenate([dm_rows, dm_ctx, jnp.zeros((8 - B - 1, 3 * D_MODEL), f32)], axis=0)
    dcc, d_w_ada, d_b_ada = ada_bwd(cc, w_ada, dm)
    small = (dcc[B], d_b_ada, *d_s5, d_s5_d, d_b_glu, d_a_log, d_dt_bias, d_norm_w, dlng, dlnb)
    small = tuple(g.reshape(s) for g, s in zip(small, SMALL_SHAPES))
    return loss, grad_x, (d_w_ada, d_w_in, d_w_out, d_w_glu, d_conv16), small


SHARDED = (1, 3, 18, 12, 14)
SMALL = tuple(i for i in range(21) if i not in SHARDED)
W_IN_SHARD = 772


def _conv_rows(w):
    return jnp.concatenate([w.reshape(9, w.shape[-1]), jnp.zeros((CONV_ROWS - 9, w.shape[-1]), f32)], axis=0)


def kernel(x, c, ctx, c_ctx, w_ada, b_ada, w_in, s5_lambda_re, s5_lambda_im, s5_log_dt, s5_b_re, s5_b_im, s5_c_re, s5_c_im, s5_d, w_glu, b_glu, conv_w, gdn_a_log, gdn_dt_bias, gdn_norm_w, w_out, ln_g, ln_b, loss_target, m_c_ctx, m_w_ada, m_b_ada, m_w_in, m_s5_lambda_re, m_s5_lambda_im, m_s5_log_dt, m_s5_b_re, m_s5_b_im, m_s5_c_re, m_s5_c_im, m_s5_d, m_w_glu, m_b_glu, m_conv_w, m_gdn_a_log, m_gdn_dt_bias, m_gdn_norm_w, m_w_out, m_ln_g, m_ln_b, v_c_ctx, v_w_ada, v_b_ada, v_w_in, v_s5_lambda_re, v_s5_lambda_im, v_s5_log_dt, v_s5_b_re, v_s5_b_im, v_s5_c_re, v_s5_c_im, v_s5_d, v_w_glu, v_b_glu, v_conv_w, v_gdn_a_log, v_gdn_dt_bias, v_gdn_norm_w, v_w_out, v_ln_g, v_ln_b):
    weights = [c_ctx, w_ada, b_ada, w_in, s5_lambda_re, s5_lambda_im, s5_log_dt, s5_b_re, s5_b_im, s5_c_re, s5_c_im,
               s5_d, w_glu, b_glu, conv_w, gdn_a_log, gdn_dt_bias, gdn_norm_w, w_out, ln_g, ln_b]
    ms = [m_c_ctx, m_w_ada, m_b_ada, m_w_in, m_s5_lambda_re, m_s5_lambda_im, m_s5_log_dt, m_s5_b_re, m_s5_b_im,
          m_s5_c_re, m_s5_c_im, m_s5_d, m_w_glu, m_b_glu, m_conv_w, m_gdn_a_log, m_gdn_dt_bias, m_gdn_norm_w, m_w_out,
          m_ln_g, m_ln_b]
    vs = [v_c_ctx, v_w_ada, v_b_ada, v_w_in, v_s5_lambda_re, v_s5_lambda_im, v_s5_log_dt, v_s5_b_re, v_s5_b_im,
          v_s5_c_re, v_s5_c_im, v_s5_d, v_w_glu, v_b_glu, v_conv_w, v_gdn_a_log, v_gdn_dt_bias, v_gdn_norm_w, v_w_out,
          v_ln_g, v_ln_b]
    cpos = lax.axis_index("c")
    jchip = 2 * lax.axis_index("x") + lax.axis_index("y")

    conv_shard = _conv_rows(conv_w)
    g_ada, g_in, g_out, g_glu, g_conv = gather_shards(
        [w_ada[0].astype(bf16), w_in[0].astype(bf16), w_out[0].astype(bf16), w_glu[0].astype(bf16), conv_shard])
    w_in_pad = jnp.concatenate([g_in[0], g_in[1], g_in[2], g_in[3], jnp.zeros((D_MODEL, IN_PAD - P_IN), bf16)], axis=1)
    conv16 = g_conv.transpose(1, 0, 2).reshape(CONV_ROWS, 3 * D_GDN)

    swap = lambda a: jnp.swapaxes(a, 3, 4)
    loss, grad_x, big, small = local_step(
        x, c, ctx, c_ctx, loss_target, g_ada, b_ada, w_in_pad, s5_lambda_re, s5_lambda_im, s5_log_dt, swap(s5_b_re),
        swap(s5_b_im),
        s5_c_re, s5_c_im, s5_d, g_glu.reshape(D_S5, D_S5), b_glu, conv16, gdn_a_log, gdn_dt_bias, gdn_norm_w,
        g_out.reshape(D_MODEL, D_MODEL), ln_g, ln_b)
    loss = lax.psum(loss, ("x", "y", "c"))

    d_w_ada, d_w_in, d_w_out, d_w_glu, d_conv16 = big
    slabs = [d_w_ada,
             d_w_in[:, :P_IN].reshape(D_MODEL, 4, W_IN_SHARD).transpose(1, 0, 2),
             d_w_out.reshape(4, D_MODEL // 4, D_MODEL),
             d_w_glu.reshape(4, D_S5 // 4, D_S5),
             d_conv16.reshape(CONV_ROWS, 4, 3 * D_GDN // 4).transpose(1, 0, 2),
             _pack_small(small).reshape(4, SMALL_QUARTER, LANES)]
    got = swap_halves(slabs)
    q32, q16 = [], []
    for t, (s, g) in enumerate(zip(slabs, got)):
        own = lax.dynamic_index_in_dim(s.reshape(4, 2, s.shape[1] // 2, s.shape[2]), cpos, axis=1, keepdims=False)
        a, b = sum_cores(own, g, name=f"sum_cores{t}")
        q32.append(a)
        q16.append(b)
    rec = scatter_to_chips(q16)
    fs = [sum_chips(lax.dynamic_index_in_dim(q, jchip, axis=0, keepdims=False), r, cpos, name=f"sum_chips{t}")
          for t, (q, r) in enumerate(zip(q32, rec))]
    red = join_halves(fs)
    g_small = _unpack_small(gather_small(red[5][0]).reshape(SMALL_TOTAL, LANES))

    grads, deltas, new_m, new_v = [None] * 21, [None] * 21, [None] * 21, [None] * 21
    for t, i in enumerate(SHARDED):
        conv, win = i == 14, i == 3
        prep = (lambda a: _conv_rows(a)[None]) if conv else ((lambda a: jnp.transpose(a, (2, 0, 1))) if win else (lambda a: a))
        g = jnp.transpose(red[t], (2, 0, 1)) if win else red[t]
        d, nm, nv = adamw_3d(prep(weights[i]), g, prep(ms[i]), prep(vs[i]), lead=win, name=f"adamw{t}")
        for lst, val in ((grads, g), (deltas, d), (new_m, nm), (new_v, nv)):
            lst[i] = (val[0, :9].reshape(weights[i].shape) if conv else (jnp.transpose(val, (1, 2, 0)) if win else val))
    small_in = lambda lst: [_as_2d(swap(lst[i]) if n in SMALL_SWAPPED else lst[i]) for n, i in enumerate(SMALL)]
    sm = adamw_small(small_in(weights), [_as_2d(g) for g in g_small], small_in(ms), small_in(vs))
    for n, i in enumerate(SMALL):
        back = (lambda a: swap(a.reshape(SMALL_SHAPES[n]))) if n in SMALL_SWAPPED else (lambda a: a.reshape(weights[i].shape))
        grads[i] = back(g_small[n])
        for lst, res in ((deltas, sm[0]), (new_m, sm[1]), (new_v, sm[2])):
            lst[i] = back(res[n])
    return (loss, grad_x, *grads, *deltas, *new_m, *new_v)
```

```python
import functools

import jax
import jax.numpy as jnp
from jax import lax
from jax.experimental import pallas as pl
from jax.experimental.pallas import tpu as pltpu

f32 = jnp.float32
bf16 = jnp.bfloat16
SDS = jax.ShapeDtypeStruct

D_MODEL = 1024
D_S5 = 512
S5_GROUP = 16
S5_GROUPS = 32
S5_STATE = 64
S5_HALF = S5_GROUPS * S5_STATE
D_GDN = 512
GDN_HEAD = 128
GDN_HEADS = 4
CHUNK = 64
GRID_W = 64
N_DIR = 2
P_IN = 3088
DEEPNORM_ALPHA = 2.0 ** 0.25
LN_EPS = 1e-5
NORM_EPS = 1e-6
ADAM_LR, ADAM_B1, ADAM_B2, ADAM_EPS, ADAM_WD, ADAM_STEP = 0.001, 0.9, 0.999, 1e-08, 0.01, 10

LANES = 128
VMEM_LIMIT = 56 * 1024 * 1024
TOK_TILE = 256
S5_TILE = 256
MESH = pl.DeviceIdType.MESH


def _cparams(n_grid):
    return pltpu.CompilerParams(dimension_semantics=("arbitrary",) * n_grid, vmem_limit_bytes=VMEM_LIMIT)


def _dot(a, b):
    return jnp.dot(a.astype(bf16), b.astype(bf16), preferred_element_type=f32)


def _dot_nt(a, b):
    return lax.dot_general(a.astype(bf16), b.astype(bf16), (((1,), (1,)), ((), ())), preferred_element_type=f32)


def _dot_tn(a, b):
    return lax.dot_general(a.astype(bf16), b.astype(bf16), (((0,), (0,)), ((), ())), preferred_element_type=f32)


def _dot_hi(a, b):
    return jnp.dot(a, b, precision=lax.Precision.HIGHEST, preferred_element_type=f32)


def _dot_h3(a, b):
    return jnp.dot(a, b, precision=lax.Precision.HIGH, preferred_element_type=f32)


@jax.custom_vjp
def _mm(a, b):
    return _dot(a, b)


@jax.custom_vjp
def _mm_nt(a, b):
    return _dot_nt(a, b)


@jax.custom_vjp
def _mm_tn(a, b):
    return _dot_tn(a, b)


_mm.defvjp(lambda a, b: (_dot(a, b), (a, b)), lambda r, g: (_mm_nt(g, r[1]), _mm_tn(r[0], g)))
_mm_nt.defvjp(lambda a, b: (_dot_nt(a, b), (a, b)), lambda r, g: (_mm(g, r[1]), _mm_tn(g, r[0])))
_mm_tn.defvjp(lambda a, b: (_dot_tn(a, b), (a, b)), lambda r, g: (_mm_nt(r[1], g), _mm(r[0], g)))


def _silu(x):
    return x * jax.nn.sigmoid(x)


def _gelu(x):
    return 0.5 * x * (1.0 + lax.erf(x * (2.0 ** -0.5)))


def _resident(shape):
    nd = len(shape)
    return pl.BlockSpec(shape, lambda *_: (0,) * nd, pipeline_mode=pl.Buffered(1))


def _tok(tile, width, nt=None, rev=False):
    if rev:
        return pl.BlockSpec((None, tile, width), lambda b, n: (b, nt - 1 - n, 0))
    return pl.BlockSpec((None, tile, width), lambda b, n: (b, n, 0))


def _per_batch(rows, width):
    return pl.BlockSpec((None, rows, width), lambda b, n: (b, 0, 0))


def _first_step():
    return jnp.logical_and(pl.program_id(0) == 0, pl.program_id(1) == 0)


ADA_SHARD = 3 * D_MODEL // 4
N_DEV = 8


def ada_fwd(cc, w, b):
    def body(cc_ref, w_ref, b_ref, m_ref):
        for k in range(N_DEV):
            m_ref[k] = _dot(_silu(cc_ref[k]), w_ref[...]) + b_ref[...]

    return pl.pallas_call(body, name="ada_fwd", out_shape=SDS((N_DEV, 8, ADA_SHARD), f32),
                          compiler_params=pltpu.CompilerParams(vmem_limit_bytes=VMEM_LIMIT))(cc, w, b)


def ada_bwd(cc, w, dmj, dm):
    def body(cc_ref, w_ref, dmj_ref, dm_ref, dw_ref, db_ref, p_ref):
        dw = jnp.zeros((D_MODEL, ADA_SHARD), f32)
        p = jnp.zeros((8, D_MODEL), f32)
        db = jnp.zeros((1, 3 * D_MODEL), f32)
        for k in range(N_DEV):
            dw = dw + _dot_tn(_silu(cc_ref[k]), dmj_ref[k])
            p = p + _dot_nt(dmj_ref[k], w_ref[...])
            db = db + jnp.sum(dm_ref[k], axis=0, keepdims=True)
        dw_ref[0] = dw
        db_ref[...] = db
        p_ref[...] = p

    return pl.pallas_call(
        body, name="ada_bwd",
        out_shape=[SDS((1, D_MODEL, ADA_SHARD), f32), SDS((1, 3 * D_MODEL), f32), SDS((8, D_MODEL), f32)],
        compiler_params=pltpu.CompilerParams(vmem_limit_bytes=VMEM_LIMIT))(cc, w, dmj, dm)


def c_ctx_bwd(p_all, c_ctx):
    def body(p_ref, c_ref, d_ref):
        ds = ((p_ref[0, 2:3, :] + p_ref[2, 2:3, :]) + p_ref[4, 2:3, :]) + p_ref[6, 2:3, :]
        _, vjp = jax.vjp(_silu, c_ref[...])
        d_ref[...] = vjp(ds)[0]

    return pl.pallas_call(body, name="c_ctx_bwd", out_shape=SDS((1, D_MODEL), f32))(p_all, c_ctx)


N_GATE = 2 * N_DIR * GDN_HEADS
IN_WIDTHS = (D_S5, D_S5, 3 * D_GDN, D_GDN, LANES)
IN_OFFS = (0, 512, 1024, 2560, 3072)
IN_PAD = 3200


def in_proj_fwd(x, mod, w, *, name):
    B, L, _ = x.shape
    T = min(TOK_TILE, L)

    def body(x_ref, mod_ref, w_ref, *o_refs):
        h = (x_ref[...] * (1.0 + mod_ref[0:1, :]) + mod_ref[1:2, :]).astype(bf16)
        for o_ref, off, wd in zip(o_refs, IN_OFFS, IN_WIDTHS):
            r = _dot(h, w_ref[:, off:off + wd])
            o_ref[...] = r[:, :o_ref.shape[-1]]

    outw = (D_S5, D_S5, 3 * D_GDN, D_GDN, N_GATE)
    return pl.pallas_call(
        body, name=name, grid=(B, L // T),
        in_specs=[_tok(T, D_MODEL), _per_batch(2, D_MODEL), _resident((D_MODEL, IN_PAD))],
        out_specs=[_tok(T, wd) for wd in outw],
        out_shape=[SDS((B, L, wd), f32) for wd in outw],
        compiler_params=_cparams(2),
    )(x, mod, w)


def in_proj_bwd(x, mod, ds, w, gx_res, dw_start, *, name):
    B, L, _ = x.shape
    T = min(TOK_TILE, L)
    with_dx = gx_res is not None
    with_start = dw_start is not None
    n_u = len(ds[0])

    def body(*refs):
        x_ref, mod_ref = refs[0], refs[1]
        du_refs = refs[2:2 + n_u]
        d_refs = refs[2 + n_u:6 + n_u]
        w_ref = refs[6 + n_u]
        k = 7 + n_u
        if with_dx:
            gx_ref = refs[k]
            k += 1
        if with_start:
            start_ref = refs[k]
            k += 1
        dw_ref, dmod_ref = refs[k], refs[k + 1]
        if with_dx:
            dx_ref = refs[k + 2]
        n = pl.program_id(1)

        @pl.when(_first_step())
        def _():
            dw_ref[...] = start_ref[...] if with_start else jnp.zeros_like(dw_ref)

        @pl.when(n == 0)
        def _():
            dmod_ref[...] = jnp.zeros_like(dmod_ref)

        xv = x_ref[...]
        scale1 = 1.0 + mod_ref[0:1, :]
        h = (xv * scale1 + mod_ref[1:2, :]).astype(bf16)
        du = du_refs[0][...]
        for r in du_refs[1:]:
            du = du + r[...]
        dh = jnp.zeros((T, D_MODEL), f32)
        for dv, off, wd in zip([du] + [r[...] for r in d_refs], IN_OFFS, IN_WIDTHS):
            dv = dv.astype(bf16)
            dh = dh + _dot_nt(dv, w_ref[:, off:off + wd])
            dw_ref[:, off:off + wd] += _dot_tn(h, dv)
        dmod_ref[0:1, :] += jnp.sum(dh * xv, axis=0, keepdims=True)
        dmod_ref[1:2, :] += jnp.sum(dh, axis=0, keepdims=True)
        if with_dx:
            dx_ref[...] = gx_ref[...] + dh * scale1

    in_specs = ([_tok(T, D_MODEL), _per_batch(2, D_MODEL)] + [_tok(T, D_S5)] * n_u + [_tok(T, wd) for wd in IN_WIDTHS[1:]]
                + [_resident((D_MODEL, IN_PAD))])
    args = [x, mod, *ds[0], *ds[1:], w]
    out_specs = [_resident((D_MODEL, IN_PAD)), _per_batch(2, D_MODEL)]
    out_shape = [SDS((D_MODEL, IN_PAD), f32), SDS((B, 2, D_MODEL), f32)]
    if with_dx:
        in_specs.append(_tok(T, D_MODEL))
        args.append(gx_res)
        out_specs.append(_tok(T, D_MODEL))
        out_shape.append(SDS((B, L, D_MODEL), f32))
    if with_start:
        in_specs.append(_resident((D_MODEL, IN_PAD)))
        args.append(dw_start)
    return pl.pallas_call(body, name=name, grid=(B, L // T), in_specs=in_specs, out_specs=out_specs,
                          out_shape=out_shape, compiler_params=_cparams(2))(*args)


def _s5_zoh(lr, li, ldt, bre, bim, expand):
    dt = jnp.exp(ldt)
    zr, zi = lr * dt, li * dt
    e = jnp.exp(zr)
    ar, ai = e * jnp.cos(zi), e * jnp.sin(zi)
    den = lr * lr + li * li
    czr = ((ar - 1.0) * lr + ai * li) / den
    czi = (ai * lr - (ar - 1.0) * li) / den
    czr_e, czi_e = _dot_hi(czr, expand), _dot_hi(czi, expand)
    return ar, ai, czr_e * bre - czi_e * bim, czr_e * bim + czi_e * bre


_ZOH_OUT = [(N_DIR * S5_GROUPS, S5_STATE)] * 2 + [(N_DIR * S5_GROUPS, S5_STATE * S5_GROUP)] * 2


def s5_zoh_fwd(lr, li, ldt, bre, bim, expand):
    def body(lr_ref, li_ref, ldt_ref, bre_ref, bim_ref, e_ref, ar_ref, ai_ref, bbr_ref, bbi_ref):
        ar, ai, bbr, bbi = _s5_zoh(lr_ref[...], li_ref[...], ldt_ref[...], bre_ref[...], bim_ref[...], e_ref[...])
        ar_ref[...], ai_ref[...], bbr_ref[...], bbi_ref[...] = ar, ai, bbr, bbi

    return pl.pallas_call(body, name="s5_zoh_fwd", out_shape=[SDS(s, f32) for s in _ZOH_OUT])(
        lr, li, ldt, bre, bim, expand)


def s5_zoh_bwd(lr, li, ldt, bre, bim, expand, dar, dai, dbbr, dbbi):
    def body(lr_ref, li_ref, ldt_ref, bre_ref, bim_ref, e_ref, dar_ref, dai_ref, dbbr_ref, dbbi_ref,
             dlr_ref, dli_ref, dldt_ref, dbre_ref, dbim_ref):
        ev = e_ref[...]
        _, vjp = jax.vjp(lambda a, b, c, d, e: _s5_zoh(a, b, c, d, e, ev),
                         lr_ref[...], li_ref[...], ldt_ref[...], bre_ref[...], bim_ref[...])
        outs = vjp((dar_ref[...], dai_ref[...], dbbr_ref[...], dbbi_ref[...]))
        dlr_ref[...], dli_ref[...], dldt_ref[...], dbre_ref[...], dbim_ref[...] = outs

    shapes = [lr.shape, li.shape, ldt.shape, bre.shape, bim.shape]
    return pl.pallas_call(body, name="s5_zoh_bwd", out_shape=[SDS(s, f32) for s in shapes])(
        lr, li, ldt, bre, bim, expand, dar, dai, dbbr, dbbi)


def _scan_rows(T, rev, ar, ai, h0s, refs, off):
    def step(i, carry):
        t = off + ((T - 1 - i) if rev else i)
        out = []
        for (hr, hi), (r_ref, i_ref) in zip(carry, refs):
            nr = ar * hr - ai * hi + r_ref[pl.ds(t, 1), :]
            ni = ar * hi + ai * hr + i_ref[pl.ds(t, 1), :]
            r_ref[pl.ds(t, 1), :] = nr
            i_ref[pl.ds(t, 1), :] = ni
            out.append((nr, ni))
        return tuple(out)

    return lax.fori_loop(0, T, step, tuple(h0s))


S5_BLOCKS = 4
S5_BC = D_S5 // S5_BLOCKS
S5_BS = S5_HALF // S5_BLOCKS


def _s5_in(uv, bre_ref, bim_ref, hr_ref, hi_ref, off, T):
    for jb in range(S5_BLOCKS):
        uj = uv[:, jb * S5_BC:(jb + 1) * S5_BC]
        hr_ref[off:off + T, jb * S5_BS:(jb + 1) * S5_BS] = _dot(uj, bre_ref[jb])
        hi_ref[off:off + T, jb * S5_BS:(jb + 1) * S5_BS] = _dot(uj, bim_ref[jb])


def _s5_specs(B, T, nt, rev):
    tidx = (lambda n: nt - 1 - n) if rev else (lambda n: n)
    tok = pl.BlockSpec((B, T, D_S5), lambda n: (0, tidx(n), 0))
    hin = pl.BlockSpec((B, None, 2, S5_HALF), lambda n: (0, tidx(n), 0, 0))
    state = pl.BlockSpec((B, 2, S5_HALF), lambda n: (0, 0, 0))
    return tok, hin, state


def s5_scan_fwd(u, bre, bim, ctop, cbot, arow, h0, *, d, need_y, name):
    B, L, _ = u.shape
    T = min(S5_TILE, L)
    nt = L // T
    rev = d == 1

    def body(u_ref, bre_ref, bim_ref, ct_ref, cb_ref, a_ref, h0_ref, *rest):
        if need_y:
            y_ref, hin_ref, hend_ref, hr_scr, hi_scr, h_scr = rest
        else:
            hin_ref, hend_ref, hr_scr, hi_scr, h_scr = rest
        n = pl.program_id(0)

        @pl.when(n == 0)
        def _():
            h_scr[...] = h0_ref[...]

        hin_ref[...] = h_scr[...]
        for b in range(B):
            _s5_in(u_ref[b].astype(bf16), bre_ref, bim_ref, hr_scr.at[b], hi_scr.at[b], 0, T)
        hs = _scan_rows(T, rev, a_ref[0:1, :], a_ref[1:2, :], [(h_scr[b, 0:1, :], h_scr[b, 1:2, :]) for b in range(B)],
                        [(hr_scr.at[b], hi_scr.at[b]) for b in range(B)], 0)
        for b in range(B):
            h_scr[b, 0:1, :] = hs[b][0]
            h_scr[b, 1:2, :] = hs[b][1]
            if need_y:
                for jb in range(S5_BLOCKS):
                    st = slice(jb * S5_BS, (jb + 1) * S5_BS)
                    y_ref[b, :, jb * S5_BC:(jb + 1) * S5_BC] = (_dot(hr_scr[b, :, st], ct_ref[jb])
                                                                 + _dot(hi_scr[b, :, st], cb_ref[jb]))

        @pl.when(n == nt - 1)
        def _():
            hend_ref[...] = h_scr[...]

    tok, hin_spec, state = _s5_specs(B, T, nt, rev)
    out_specs = [hin_spec, state]
    out_shape = [SDS((B, nt, 2, S5_HALF), f32), SDS((B, 2, S5_HALF), f32)]
    if need_y:
        out_specs.insert(0, tok)
        out_shape.insert(0, SDS((B, L, D_S5), f32))
    w_in, w_out = _resident((S5_BLOCKS, S5_BC, S5_BS)), _resident((S5_BLOCKS, S5_BS, S5_BC))
    return pl.pallas_call(
        body, name=name, grid=(nt,),
        in_specs=[tok, w_in, w_in, w_out, w_out, _resident((2, S5_HALF)), state],
        out_specs=out_specs, out_shape=out_shape,
        scratch_shapes=[pltpu.VMEM((B, T, S5_HALF), f32), pltpu.VMEM((B, T, S5_HALF), f32),
                        pltpu.VMEM((B, 2, S5_HALF), f32)],
        compiler_params=_cparams(1),
    )(u, bre, bim, ctop, cbot, arow, h0)


def s5_scan_bwd(u, dy, bre, bim, ctop, cbot, arow, hin, dhend, *, d, name):
    B, L, _ = u.shape
    T = min(S5_TILE, L)
    nt = L // T
    rev = d == 1
    has_dy = dy is not None
    PAD = 8

    def body(*refs):
        u_ref = refs[0]
        k = 1
        if has_dy:
            dy_ref = refs[1]
            k = 2
        bre_ref, bim_ref, ct_ref, cb_ref, a_ref, hin_ref, dhend_ref = refs[k:k + 7]
        du_ref, dbre_ref, dbim_ref, dct_ref, dcb_ref, da_ref, dh0_ref = refs[k + 7:k + 14]
        hr_scr, hi_scr, gr_scr, gi_scr, p_scr = refs[k + 14:]
        n = pl.program_id(0)

        @pl.when(n == 0)
        def _():
            for r in (dbre_ref, dbim_ref, dct_ref, dcb_ref, da_ref):
                r[...] = jnp.zeros_like(r)
            p_scr[...] = dhend_ref[...]

        ar, ai = a_ref[0:1, :], a_ref[1:2, :]
        prev_row = PAD + T if rev else PAD - 1
        uvs = []
        for b in range(B):
            uvs.append(u_ref[b].astype(bf16))
            _s5_in(uvs[b], bre_ref, bim_ref, hr_scr.at[b], hi_scr.at[b], PAD, T)
            hr_scr[b, prev_row:prev_row + 1, :] = hin_ref[b, 0:1, :]
            hi_scr[b, prev_row:prev_row + 1, :] = hin_ref[b, 1:2, :]
        _scan_rows(T, rev, ar, ai, [(hin_ref[b, 0:1, :], hin_ref[b, 1:2, :]) for b in range(B)],
                   [(hr_scr.at[b], hi_scr.at[b]) for b in range(B)], PAD)
        if has_dy:
            for b in range(B):
                dyv = dy_ref[b].astype(bf16)
                for jb in range(S5_BLOCKS):
                    st = slice(jb * S5_BS, (jb + 1) * S5_BS)
                    dyj = dyv[:, jb * S5_BC:(jb + 1) * S5_BC]
                    gr_scr[b, :, st] = _dot_nt(dyj, ct_ref[jb])
                    gi_scr[b, :, st] = _dot_nt(dyj, cb_ref[jb])
                    dct_ref[jb] += _dot_tn(hr_scr[b, PAD:PAD + T, st], dyj)
                    dcb_ref[jb] += _dot_tn(hi_scr[b, PAD:PAD + T, st], dyj)
        else:
            gr_scr[...] = jnp.zeros_like(gr_scr)
            gi_scr[...] = jnp.zeros_like(gi_scr)

        def step(i, carry):
            t = i if rev else T - 1 - i
            tp = PAD + t + (1 if rev else -1)
            out = []
            for b, (pr, pi, dar, dai) in enumerate(carry):
                gr = gr_scr[b, pl.ds(t, 1), :] + pr
                gi = gi_scr[b, pl.ds(t, 1), :] + pi
                gr_scr[b, pl.ds(t, 1), :] = gr
                gi_scr[b, pl.ds(t, 1), :] = gi
                hpr = hr_scr[b, pl.ds(tp, 1), :]
                hpi = hi_scr[b, pl.ds(tp, 1), :]
                out.append((ar * gr + ai * gi, ar * gi - ai * gr, dar + hpr * gr + hpi * gi, dai + hpr * gi - hpi * gr))
            return tuple(out)

        zero = jnp.zeros((1, S5_HALF), f32)
        res = lax.fori_loop(0, T, step, tuple((p_scr[b, 0:1, :], p_scr[b, 1:2, :], zero, zero) for b in range(B)))
        for b in range(B):
            pr, pi, dar, dai = res[b]
            p_scr[b, 0:1, :] = pr
            p_scr[b, 1:2, :] = pi
            da_ref[0:1, :] += dar
            da_ref[1:2, :] += dai
            for jb in range(S5_BLOCKS):
                st = slice(jb * S5_BS, (jb + 1) * S5_BS)
                ch = slice(jb * S5_BC, (jb + 1) * S5_BC)
                gr_j = gr_scr[b, :, st].astype(bf16)
                gi_j = gi_scr[b, :, st].astype(bf16)
                du_ref[b, :, ch] = _dot_nt(gr_j, bre_ref[jb]) + _dot_nt(gi_j, bim_ref[jb])
                dbre_ref[jb] += _dot_tn(uvs[b][:, ch], gr_j)
                dbim_ref[jb] += _dot_tn(uvs[b][:, ch], gi_j)

        @pl.when(n == nt - 1)
        def _():
            dh0_ref[...] = p_scr[...]

    tok, hin_spec, state = _s5_specs(B, T, nt, not rev)
    w_in, w_out = _resident((S5_BLOCKS, S5_BC, S5_BS)), _resident((S5_BLOCKS, S5_BS, S5_BC))
    wspecs = [w_in, w_in, w_out, w_out]
    in_specs = [tok] + ([tok] if has_dy else []) + wspecs + [_resident((2, S5_HALF)), hin_spec, state]
    args = [u] + ([dy] if has_dy else []) + [bre, bim, ctop, cbot, arow, hin, dhend]
    return pl.pallas_call(
        body, name=name, grid=(nt,), in_specs=in_specs,
        out_specs=[tok] + wspecs + [_resident((2, S5_HALF)), state],
        out_shape=[SDS((B, L, D_S5), f32), SDS((S5_BLOCKS, S5_BC, S5_BS), f32), SDS((S5_BLOCKS, S5_BC, S5_BS), f32),
                   SDS((S5_BLOCKS, S5_BS, S5_BC), f32), SDS((S5_BLOCKS, S5_BS, S5_BC), f32), SDS((2, S5_HALF), f32),
                   SDS((B, 2, S5_HALF), f32)],
        scratch_shapes=[pltpu.VMEM((B, T + 2 * PAD, S5_HALF), f32), pltpu.VMEM((B, T + 2 * PAD, S5_HALF), f32),
                        pltpu.VMEM((B, T, S5_HALF), f32), pltpu.VMEM((B, T, S5_HALF), f32),
                        pltpu.VMEM((B, 2, S5_HALF), f32)],
        compiler_params=_cparams(1),
    )(*args)


def _glu_fn(u, y0, y1, z, dsk, wg, bg):
    g = _gelu(dsk * u + y0 + y1)
    return g * jax.nn.sigmoid(_mm(g, wg) + bg) * _silu(z)


CONV_ROWS = 16


def _shift(x, s):
    L = x.shape[0]
    k = (-s) % L
    return x if k == 0 else pltpu.roll(x, k, axis=0)


def _conv_masks(L, is_ctx):
    t = lax.broadcasted_iota(jnp.int32, (L, 1), 0)
    if is_ctx:
        return t == L - 1, t == 0, None, None
    col = jnp.bitwise_and(t, GRID_W - 1)
    return col == GRID_W - 1, col == 0, t >= GRID_W, t < L - GRID_W


def _conv_sides(xv, masks):
    no_left, no_right, _, _ = masks
    return _shift(jnp.where(no_left, 0.0, xv), -1), _shift(jnp.where(no_right, 0.0, xv), 1)


def _conv_pre(xv, w_ref, masks, is_ctx):
    xl, xr = _conv_sides(xv, masks)
    z = [w_ref[3 * di:3 * di + 1, :] * xl + w_ref[3 * di + 1:3 * di + 2, :] * xv + w_ref[3 * di + 2:3 * di + 3, :] * xr
         for di in ((1,) if is_ctx else (0, 1, 2))]
    if is_ctx:
        return z[0]
    _, _, has_up, has_down = masks
    return z[1] + jnp.where(has_up, _shift(z[0], -GRID_W), 0.0) + jnp.where(has_down, _shift(z[2], GRID_W), 0.0)


def _conv_pre_bwd(xv, w_ref, dpre, masks, is_ctx, dw_ref):
    no_left, no_right, has_up, has_down = masks
    xl, xr = _conv_sides(xv, masks)
    if is_ctx:
        dz = {1: dpre}
    else:
        dz = {0: _shift(jnp.where(has_up, dpre, 0.0), GRID_W), 1: dpre, 2: _shift(jnp.where(has_down, dpre, 0.0), -GRID_W)}
    dxl = dxc = dxr = None
    for di, d in dz.items():
        for dj, side in enumerate((xl, xv, xr)):
            dw_ref[3 * di + dj:3 * di + dj + 1, :] = jnp.sum(d * side, axis=0, keepdims=True)
        tl, tc, tr = (w_ref[3 * di + dj:3 * di + dj + 1, :] * d for dj in range(3))
        dxl, dxc, dxr = (tl, tc, tr) if dxl is None else (dxl + tl, dxc + tc, dxr + tr)
    return dxc + jnp.where(no_left, 0.0, _shift(dxl, 1)) + jnp.where(no_right, 0.0, _shift(dxr, -1))


def _qk_post(pre, is_norm, scale):
    s = _silu(pre)
    nrm = lax.rsqrt(jnp.sum(s * s, axis=-1, keepdims=True) + NORM_EPS)
    return s * jnp.where(is_norm, nrm * scale, 1.0)


def _conv_kind():
    ct = pl.program_id(1)
    return ct < 2 * GDN_HEADS, jnp.where(ct < GDN_HEADS, GDN_HEAD ** -0.5, 1.0).astype(f32)


def conv_fwd(qkv, w16, *, is_ctx, name):
    B, L, C = qkv.shape
    spec = pl.BlockSpec((None, L, GDN_HEAD), lambda b, ct: (b, 0, ct))
    wspec = pl.BlockSpec((CONV_ROWS, GDN_HEAD), lambda b, ct: (0, ct))

    def body(x_ref, w_ref, o_ref, pre_ref):
        is_norm, scale = _conv_kind()
        pre = _conv_pre(x_ref[...], w_ref, _conv_masks(L, is_ctx), is_ctx)
        pre_ref[...] = pre
        o_ref[...] = _qk_post(pre, is_norm, scale)

    return pl.pallas_call(body, name=name, grid=(B, C // GDN_HEAD), in_specs=[spec, wspec], out_specs=[spec, spec],
                          out_shape=[SDS((B, L, C), f32)] * 2, compiler_params=_cparams(2))(qkv, w16)


def conv_bwd(qkv, pre, w16, da0, da1, *, is_ctx, name):
    B, L, C = qkv.shape
    spec = pl.BlockSpec((None, L, GDN_HEAD), lambda b, ct: (b, 0, ct))
    wspec = pl.BlockSpec((CONV_ROWS, GDN_HEAD), lambda b, ct: (0, ct))
    dwspec = pl.BlockSpec((None, CONV_ROWS, GDN_HEAD), lambda b, ct: (b, 0, ct))

    def body(x_ref, pre_ref, w_ref, d0_ref, d1_ref, dx_ref, dw_ref):
        is_norm, scale = _conv_kind()
        _, vjp = jax.vjp(lambda p: _qk_post(p, is_norm, scale), pre_ref[...])
        dpre = vjp(d0_ref[...] + d1_ref[...])[0]
        dw_ref[...] = jnp.zeros_like(dw_ref)
        dx_ref[...] = _conv_pre_bwd(x_ref[...], w_ref, dpre, _conv_masks(L, is_ctx), is_ctx, dw_ref)

    return pl.pallas_call(body, name=name, grid=(B, C // GDN_HEAD), in_specs=[spec, spec, wspec, spec, spec],
                          out_specs=[spec, dwspec], out_shape=[SDS((B, L, C), f32), SDS((B, CONV_ROWS, C), f32)],
                          compiler_params=_cparams(2))(qkv, pre, w16, da0, da1)


def _gates_fn(ba, alog, dtb):
    T = ba.shape[0]
    lane = lax.broadcasted_iota(jnp.int32, ba.shape, 1)
    ii = lax.broadcasted_iota(jnp.int32, (T, T), 0)
    jj = lax.broadcasted_iota(jnp.int32, (T, T), 1)
    same = jnp.right_shift(ii, 6) == jnp.right_shift(jj, 6)
    lmat = jnp.logical_and(same, ii >= jj).astype(f32)
    umat = jnp.logical_and(same, ii <= jj).astype(f32)
    g = jnp.where(lane >= 8, -jnp.exp(alog) * jax.nn.softplus(ba + dtb), 0.0)
    gc = jnp.where(lane >= 12, _dot_hi(umat, g), _dot_hi(lmat, g))
    return jnp.where(lane < 8, jax.nn.sigmoid(ba), gc)


def gates_fwd(ba, alog, dtb, *, name):
    B, L, _ = ba.shape
    T = min(TOK_TILE, L)
    t = _tok(T, N_GATE)

    def body(ba_ref, al_ref, dt_ref, o_ref):
        o_ref[...] = _gates_fn(ba_ref[...], al_ref[...], dt_ref[...])

    return pl.pallas_call(body, name=name, grid=(B, L // T),
                          in_specs=[t, _resident((1, N_GATE)), _resident((1, N_GATE))], out_specs=t,
                          out_shape=SDS((B, L, N_GATE), f32), compiler_params=_cparams(2))(ba, alog, dtb)


def gates_bwd(ba, alog, dtb, dbg, *, name):
    B, L, _ = ba.shape
    T = min(TOK_TILE, L)
    t = _tok(T, N_GATE)
    small = _resident((1, N_GATE))

    def body(ba_ref, al_ref, dt_ref, d_ref, dba_ref, dal_ref, ddt_ref):
        @pl.when(_first_step())
        def _():
            dal_ref[...] = jnp.zeros_like(dal_ref)
            ddt_ref[...] = jnp.zeros_like(ddt_ref)

        _, vjp = jax.vjp(_gates_fn, ba_ref[...], al_ref[...], dt_ref[...])
        dba, dal, ddt = vjp(d_ref[...])
        dba_ref[...] = dba
        dal_ref[...] += dal
        ddt_ref[...] += ddt

    return pl.pallas_call(body, name=name, grid=(B, L // T), in_specs=[t, small, small, t],
                          out_specs=[t, small, small],
                          out_shape=[SDS((B, L, N_GATE), f32), SDS((1, N_GATE), f32), SDS((1, N_GATE), f32)],
                          compiler_params=_cparams(2))(ba, alog, dtb, dbg)


@jax.custom_vjp
def _inv_unit_tri(mats):
    n = mats[0].shape[0]
    eye = (lax.broadcasted_iota(jnp.int32, (n, n), 0) == lax.broadcasted_iota(jnp.int32, (n, n), 1)).astype(f32)
    xs = [eye - a for a in mats]
    sq = [_dot(a, a) for a in mats]
    ps = sq
    k = 2
    while k < n:
        xs = [x + _dot(x, p) for x, p in zip(xs, ps)]
        k *= 2
        if k < n:
            ps = [_dot(p, p) for p in ps]
    return tuple(_dot(p, x) - a for p, x, a in zip(sq, xs, mats))


def _inv_unit_tri_fwd(mats):
    ns = _inv_unit_tri(mats)
    return ns, ns


def _inv_unit_tri_bwd(ns, dns):
    ys = [dn + _dot_tn(nn, dn) for nn, dn in zip(ns, dns)]
    return (tuple(-(y + _dot_nt(y, nn)) for y, nn in zip(ys, ns)),)


_inv_unit_tri.defvjp(_inv_unit_tri_fwd, _inv_unit_tri_bwd)


def _gdn_chunk(heads, *, revs):
    n = heads[0][0].shape[0]
    ii = lax.broadcasted_iota(jnp.int32, (n, n), 0)
    jj = lax.broadcasted_iota(jnp.int32, (n, n), 1)
    row = lax.broadcasted_iota(jnp.int32, (n, 1), 0)
    lower = {False: ii >= jj, True: ii <= jj}
    strict = {False: ii > jj, True: ii < jj}
    last = {False: n - 1, True: 0}
    H = range(len(heads))
    q, k, v, beta, gc, gr, s = (list(t) for t in zip(*heads))
    decay = [jnp.where(lower[revs[h]], jnp.exp(jnp.where(lower[revs[h]], gc[h] - gr[h], 0.0)), 0.0) for h in H]
    kk = [_mm_nt(k[h], k[h]) for h in H]
    qk = [_mm_nt(q[h], k[h]) * decay[h] for h in H]
    qs = [_mm(q[h], s[h]) for h in H]
    a_mat = tuple(jnp.where(strict[revs[h]], beta[h] * kk[h] * decay[h], 0.0) for h in H)
    gamma = [jnp.exp(gc[h]) for h in H]
    g_last = [jnp.sum(jnp.where(row == last[revs[h]], gc[h], 0.0), axis=0, keepdims=True) for h in H]
    nmat = _inv_unit_tri(a_mat)
    bv = [beta[h] * v[h] for h in H]
    bk = [(beta[h] * gamma[h]) * k[h] for h in H]
    u0 = [bv[h] + _mm(nmat[h], bv[h]) for h in H]
    w = [bk[h] + _mm(nmat[h], bk[h]) for h in H]
    k_out = [k[h] * jnp.exp(g_last[h] - gc[h]) for h in H]
    u = [u0[h] - _mm(w[h], s[h]) for h in H]
    o = [gamma[h] * qs[h] + _mm(qk[h], u[h]) for h in H]
    s_new = [jnp.exp(g_last[h]) * s[h] + _mm_tn(k_out[h], u[h]) for h in H]
    return tuple((o[h], s_new[h]) for h in H)


def _gdn_specs(B, nc, rev):
    def cidx(n):
        return (nc - 1 - n) if rev else n
    tok = lambda width: pl.BlockSpec((B, CHUNK, width), lambda n: (0, cidx(n), 0))
    rowspec = pl.BlockSpec((B, None, N_GATE, CHUNK), lambda n: (0, cidx(n), 0, 0))
    st = pl.BlockSpec((B, GDN_HEADS, GDN_HEAD, GDN_HEAD), lambda n: (0, 0, 0, 0))
    ck = pl.BlockSpec((B, None, GDN_HEADS, GDN_HEAD, GDN_HEAD), lambda n: (0, cidx(n), 0, 0, 0))
    return tok, rowspec, st, ck


def _gdn_head_args(qkv_ref, bg_ref, bgr_ref, b, d, h):
    col = d * GDN_HEADS + h
    q = qkv_ref[b, :, h * GDN_HEAD:(h + 1) * GDN_HEAD]
    k = qkv_ref[b, :, D_GDN + h * GDN_HEAD:D_GDN + (h + 1) * GDN_HEAD]
    v = qkv_ref[b, :, 2 * D_GDN + h * GDN_HEAD:2 * D_GDN + (h + 1) * GDN_HEAD]
    bgv = bg_ref[b]
    return q, k, v, bgv[:, col:col + 1], bgv[:, 8 + col:9 + col], bgr_ref[b][8 + col:9 + col, :]


def _gdn_chains(B):
    return [(d, b, h) for d in range(N_DIR) for b in range(B) for h in range(GDN_HEADS)]


def gdn_fwd(qkv, bg, bgr, s0s, *, need_o, name):
    B, L, _ = qkv.shape
    nc = L // CHUNK
    specs = [_gdn_specs(B, nc, d == 1) for d in range(N_DIR)]
    chains = _gdn_chains(B)
    state_shape = (B, GDN_HEADS, GDN_HEAD, GDN_HEAD)

    def body(*refs):
        ins = [refs[3 * d:3 * d + 3] for d in range(N_DIR)]
        s0_refs = refs[6:8]
        k = 8
        o_refs = refs[k:k + 2] if need_o else None
        k += 2 if need_o else 0
        ck_refs, sf_refs, s_scrs = refs[k:k + 2], refs[k + 2:k + 4], refs[k + 4:k + 6]
        n = pl.program_id(0)

        @pl.when(n == 0)
        def _():
            for d in range(N_DIR):
                s_scrs[d][...] = s0_refs[d][...]

        for d in range(N_DIR):
            ck_refs[d][...] = s_scrs[d][...]
        heads = tuple(_gdn_head_args(*ins[d], b, d, h) + (s_scrs[d][b, h],) for d, b, h in chains)
        outs = _gdn_chunk(heads, revs=tuple(d == 1 for d, _, _ in chains))
        for (d, b, h), (o, s_new) in zip(chains, outs):
            if need_o:
                o_refs[d][b, :, h * GDN_HEAD:(h + 1) * GDN_HEAD] = o
            s_scrs[d][b, h] = s_new

        @pl.when(n == nc - 1)
        def _():
            for d in range(N_DIR):
                sf_refs[d][...] = s_scrs[d][...]

    in_specs, out_o, out_ck, out_sf = [], [], [], []
    for tok, rowspec, st, ck in specs:
        in_specs += [tok(3 * D_GDN), tok(N_GATE), rowspec]
        out_o.append(tok(D_GDN))
        out_ck.append(ck)
        out_sf.append(st)
    in_specs += [specs[0][2]] * 2
    out_specs = (out_o if need_o else []) + out_ck + out_sf
    out_shape = (([SDS((B, L, D_GDN), f32)] * 2 if need_o else [])
                 + [SDS((B, nc) + state_shape[1:], f32)] * 2 + [SDS(state_shape, f32)] * 2)
    res = pl.pallas_call(
        body, name=name, grid=(nc,), in_specs=in_specs, out_specs=out_specs, out_shape=out_shape,
        scratch_shapes=[pltpu.VMEM(state_shape, f32)] * 2, compiler_params=_cparams(1),
    )(qkv, bg, bgr, qkv, bg, bgr, *s0s)
    if need_o:
        return res[0:2], res[2:4], res[4:6]
    return res[0:2], res[2:4]


def gdn_bwd(qkv, bg, bgr, cks, do, dsfs, *, name):
    B, L, _ = qkv.shape
    nc = L // CHUNK
    has_do = do is not None
    specs = [_gdn_specs(B, nc, d != 1) for d in range(N_DIR)]
    chains = _gdn_chains(B)
    state_shape = (B, GDN_HEADS, GDN_HEAD, GDN_HEAD)
    per_dir = 5 if has_do else 4

    def body(*refs):
        ins = [refs[per_dir * d:per_dir * d + per_dir] for d in range(N_DIR)]
        k = per_dir * N_DIR
        dsf_refs = refs[k:k + 2]
        outs = [refs[k + 2 + 3 * d:k + 5 + 3 * d] for d in range(N_DIR)]
        ds0_refs, ds_scrs = refs[k + 8:k + 10], refs[k + 10:k + 12]
        n = pl.program_id(0)

        @pl.when(n == 0)
        def _():
            for d in range(N_DIR):
                ds_scrs[d][...] = dsf_refs[d][...]

        lane = lax.broadcasted_iota(jnp.int32, (CHUNK, N_GATE), 1)
        sub = lax.broadcasted_iota(jnp.int32, (N_GATE, CHUNK), 0)
        heads = tuple(_gdn_head_args(*ins[d][:3], b, d, h) + (ins[d][3][b, h],) for d, b, h in chains)
        _, vjp = jax.vjp(functools.partial(_gdn_chunk, revs=tuple(d == 1 for d, _, _ in chains)), heads)
        zero = jnp.zeros((CHUNK, GDN_HEAD), f32)
        cts = tuple(((ins[d][4][b, :, h * GDN_HEAD:(h + 1) * GDN_HEAD] if has_do else zero), ds_scrs[d][b, h])
                    for d, b, h in chains)
        (dheads,) = vjp(cts)
        dbg_acc = [[jnp.zeros((CHUNK, N_GATE), f32) for _ in range(B)] for _ in range(N_DIR)]
        dbgr_acc = [[jnp.zeros((N_GATE, CHUNK), f32) for _ in range(B)] for _ in range(N_DIR)]
        for (d, b, h), (dq, dk, dv, db, dgc, dgr, ds) in zip(chains, dheads):
            col = d * GDN_HEADS + h
            dqkv_ref = outs[d][0]
            dqkv_ref[b, :, h * GDN_HEAD:(h + 1) * GDN_HEAD] = dq
            dqkv_ref[b, :, D_GDN + h * GDN_HEAD:D_GDN + (h + 1) * GDN_HEAD] = dk
            dqkv_ref[b, :, 2 * D_GDN + h * GDN_HEAD:2 * D_GDN + (h + 1) * GDN_HEAD] = dv
            dbg_acc[d][b] = dbg_acc[d][b] + jnp.where(lane == col, db, 0.0) + jnp.where(lane == 8 + col, dgc, 0.0)
            dbgr_acc[d][b] = dbgr_acc[d][b] + jnp.where(sub == 8 + col, dgr, 0.0)
            ds_scrs[d][b, h] = ds
        for d in range(N_DIR):
            for b in range(B):
                outs[d][1][b] = dbg_acc[d][b]
                outs[d][2][b] = dbgr_acc[d][b]

        @pl.when(n == nc - 1)
        def _():
            for d in range(N_DIR):
                ds0_refs[d][...] = ds_scrs[d][...]

    in_specs, args, out_specs, out_shape = [], [], [], []
    for d, (tok, rowspec, st, ck) in enumerate(specs):
        in_specs += [tok(3 * D_GDN), tok(N_GATE), rowspec, ck] + ([tok(D_GDN)] if has_do else [])
        args += [qkv, bg, bgr, cks[d]] + ([do] if has_do else [])
        out_specs += [tok(3 * D_GDN), tok(N_GATE), rowspec]
        out_shape += [SDS((B, L, 3 * D_GDN), f32), SDS((B, L, N_GATE), f32), SDS((B, nc, N_GATE, CHUNK), f32)]
    st = specs[0][2]
    in_specs += [st, st]
    args += list(dsfs)
    out_specs += [st, st]
    out_shape += [SDS(state_shape, f32)] * 2
    res = pl.pallas_call(
        body, name=name, grid=(nc,), in_specs=in_specs, out_specs=out_specs, out_shape=out_shape,
        scratch_shapes=[pltpu.VMEM(state_shape, f32)] * 2, compiler_params=_cparams(1),
    )(*args)
    return (res[0], res[3]), (res[1], res[4]), (res[2], res[5]), (res[6], res[7])


def _gnorm_fn(o0, o1, z, w):
    o = o0 + o1
    return o * lax.rsqrt(jnp.mean(o * o, axis=-1, keepdims=True) + NORM_EPS) * w * _silu(z)


def _head_loss(y, x, gate, lng, lnb, tgt):
    r = DEEPNORM_ALPHA * x + gate * y
    mu = jnp.mean(r, axis=-1, keepdims=True)
    rc = r - mu
    var = jnp.mean(rc * rc, axis=-1, keepdims=True)
    err = rc * lax.rsqrt(var + LN_EPS) * lng + lnb - tgt
    return (0.5 / D_MODEL) * jnp.sum(jnp.sum(err * err, axis=-1, keepdims=True), axis=0, keepdims=True)


def tail_fwd_bwd(u, y0, y1, z_s5, o0, o1, z_gdn, x, tgt, gate, lng, lnb, ws, wg, dsk, wglu, bglu, nw):
    B, L, _ = x.shape
    T = min(TOK_TILE, L)

    def body(u_ref, y0_ref, y1_ref, z_ref, o0_ref, o1_ref, zg_ref, x_ref, t_ref, gate_ref, lng_ref, lnb_ref, ws_ref,
             wg_ref, dsk_ref, wglu_ref, bglu_ref, nw_ref,
             loss_ref, du_ref, dys_ref, dz_ref, do_ref, dzg_ref, gx_ref, dws_ref, dwg_ref, dgate_ref, dlng_ref, dlnb_ref,
             ddsk_ref, dwglu_ref, dbglu_ref, dnw_ref):
        n = pl.program_id(1)

        @pl.when(_first_step())
        def _():
            for r in (dws_ref, dwg_ref, dlng_ref, dlnb_ref, ddsk_ref, dwglu_ref, dbglu_ref, dnw_ref):
                r[...] = jnp.zeros_like(r)

        @pl.when(n == 0)
        def _():
            loss_ref[...] = jnp.zeros_like(loss_ref)
            dgate_ref[...] = jnp.zeros_like(dgate_ref)

        s5o, glu_vjp = jax.vjp(_glu_fn, u_ref[...], y0_ref[...], y1_ref[...], z_ref[...], dsk_ref[...],
                               wglu_ref[...].astype(f32), bglu_ref[...])
        heads = []
        for h in range(GDN_HEADS):
            sl = slice(h * GDN_HEAD, (h + 1) * GDN_HEAD)
            heads.append(jax.vjp(_gnorm_fn, o0_ref[:, sl], o1_ref[:, sl], zg_ref[:, sl], nw_ref[...]))
        sv = s5o.astype(bf16)
        gv = jnp.concatenate([out for out, _ in heads], axis=1).astype(bf16)
        y = _dot(sv, ws_ref[...]) + _dot(gv, wg_ref[...])
        loss, vjp = jax.vjp(lambda *a: _head_loss(*a, t_ref[...]), y, x_ref[...], gate_ref[...], lng_ref[...],
                            lnb_ref[...])
        dy, dx, dgate, dlng, dlnb = vjp(jnp.ones((1, 1), f32))
        loss_ref[...] += jnp.broadcast_to(loss, loss_ref.shape)
        dyb = dy.astype(bf16)
        gx_ref[...] = dx
        dws_ref[...] += _dot_tn(sv, dyb)
        dwg_ref[...] += _dot_tn(gv, dyb)
        dgate_ref[...] += dgate
        dlng_ref[...] += dlng
        dlnb_ref[...] += dlnb
        du, dys, _, dz, ddsk, dwglu, dbglu = glu_vjp(_dot_nt(dyb, ws_ref[...]))
        du_ref[...], dys_ref[...], dz_ref[...] = du, dys, dz
        ddsk_ref[...] += ddsk
        dwglu_ref[...] += dwglu
        dbglu_ref[...] += dbglu
        dgdo = _dot_nt(dyb, wg_ref[...])
        for h, (_, hvjp) in enumerate(heads):
            sl = slice(h * GDN_HEAD, (h + 1) * GDN_HEAD)
            do, _, dzg, dnw = hvjp(dgdo[:, sl])
            do_ref[:, sl] = do
            dzg_ref[:, sl] = dzg
            dnw_ref[...] += dnw

    half, full = _tok(T, D_S5), _tok(T, D_MODEL)
    row = _resident((1, D_MODEL))
    wsp = _resident((D_S5, D_MODEL))
    r512, rglu, r128 = _resident((1, D_S5)), _resident((D_S5, D_S5)), _resident((1, GDN_HEAD))
    return pl.pallas_call(
        body, name="tail_fwd_bwd", grid=(B, L // T),
        in_specs=[half] * 7 + [full, full, _per_batch(1, D_MODEL), row, row, wsp, wsp, r512, rglu, r512, r128],
        out_specs=[_per_batch(8, LANES)] + [half] * 5 + [full, wsp, wsp, _per_batch(1, D_MODEL), row, row, r512, rglu, r512,
                                                           r128],
        out_shape=[SDS((B, 8, LANES), f32)] + [SDS((B, L, D_S5), f32)] * 5 + [
            SDS((B, L, D_MODEL), f32), SDS((D_S5, D_MODEL), f32), SDS((D_GDN, D_MODEL), f32), SDS((B, 1, D_MODEL), f32),
            SDS((1, D_MODEL), f32), SDS((1, D_MODEL), f32), SDS((1, D_S5), f32), SDS((D_S5, D_S5), f32), SDS((1, D_S5), f32),
            SDS((1, GDN_HEAD), f32)],
        compiler_params=_cparams(2),
    )(u, y0, y1, z_s5, o0, o1, z_gdn, x, tgt, gate, lng, lnb, ws, wg, dsk, wglu, bglu, nw)


def _adamw_math(w, g, m, v):
    nm = ADAM_B1 * m + (1.0 - ADAM_B1) * g
    nv = ADAM_B2 * v + (1.0 - ADAM_B2) * jnp.square(g)
    m_hat = nm / (1.0 - ADAM_B1 ** ADAM_STEP)
    v_hat = nv / (1.0 - ADAM_B2 ** ADAM_STEP)
    return -ADAM_LR * (m_hat / (jnp.sqrt(v_hat) + ADAM_EPS) + ADAM_WD * w), nm, nv


def _row_tile(rows, cap=512):
    for t in range(min(cap, rows), 15, -1):
        if rows % t == 0 and t % 16 == 0:
            return t
    return rows


def adamw_3d(w, g, m, v, *, lead=False, name):
    R, C = (w.shape[0], w.shape[2]) if lead else w.shape[1:]
    if lead:
        T = next(t for t in range(min(256, R), 0, -1) if R % t == 0)
        spec = pl.BlockSpec((T, 1, C), lambda i: (i, 0, 0))
    else:
        T = _row_tile(R)
        spec = pl.BlockSpec((None, T, C), lambda i: (0, i, 0))

    def body(w_ref, g_ref, m_ref, v_ref, d_ref, nm_ref, nv_ref):
        d_ref[...], nm_ref[...], nv_ref[...] = _adamw_math(w_ref[...], g_ref[...], m_ref[...], v_ref[...])

    return pl.pallas_call(body, name=name, grid=(R // T,), in_specs=[spec] * 4, out_specs=[spec] * 3,
                          out_shape=[SDS(w.shape, f32)] * 3, compiler_params=_cparams(1))(w, g, m, v)


def adamw_small(ws, gs, ms, vs):
    n = len(ws)

    def body(*refs):
        outs = refs[4 * n:]
        for i in range(n):
            d, nm, nv = _adamw_math(refs[i][...], refs[n + i][...], refs[2 * n + i][...], refs[3 * n + i][...])
            outs[i][...], outs[n + i][...], outs[2 * n + i][...] = d, nm, nv

    res = pl.pallas_call(body, name="adamw_small", out_shape=[SDS(w.shape, f32) for w in ws] * 3,
                         compiler_params=pltpu.CompilerParams(vmem_limit_bytes=VMEM_LIMIT))(*ws, *gs, *ms, *vs)
    return res[:n], res[n:2 * n], res[2 * n:]


def sum_cores(own, got, *, name):
    A, H, C = own.shape
    T = _row_tile(H)
    spec = pl.BlockSpec((None, T, C), lambda a, i: (a, i, 0))

    def body(a_ref, b_ref, q32_ref, q16_ref):
        q = a_ref[...] + b_ref[...]
        q32_ref[...] = q
        q16_ref[...] = q.astype(bf16)

    return pl.pallas_call(body, name=name, grid=(A, H // T), in_specs=[spec, spec], out_specs=[spec, spec],
                          out_shape=[SDS((A, H, C), f32), SDS((A, H, C), bf16)], compiler_params=_cparams(2))(own, got)


def sum_chips(mine, rec, cpos, *, name):
    H, C = mine.shape
    T = _row_tile(H)
    nt = H // T

    def body(c_ref, m_ref, r_ref, f_ref):
        f_ref[...] = ((m_ref[...] + r_ref[0].astype(f32)) + r_ref[1].astype(f32)) + r_ref[2].astype(f32)

    grid_spec = pltpu.PrefetchScalarGridSpec(
        num_scalar_prefetch=1, grid=(nt,),
        in_specs=[pl.BlockSpec((T, C), lambda i, c_ref: (i, 0)), pl.BlockSpec((3, T, C), lambda i, c_ref: (0, i, 0))],
        out_specs=pl.BlockSpec((None, T, C), lambda i, c_ref: (0, c_ref[0] * nt + i, 0)))
    return pl.pallas_call(body, name=name, grid_spec=grid_spec, out_shape=SDS((1, 2 * H, C), f32),
                          compiler_params=_cparams(1))(cpos.reshape(1).astype(jnp.int32), mine, rec)


CHIP_FLIPS = ((1, 0), (0, 1), (1, 1))


def _pos():
    return lax.axis_index("x"), lax.axis_index("y"), lax.axis_index("c")


def _comm_call(body, srcs, out_sds, n_remote, n_local, name):
    any_spec = pl.BlockSpec(memory_space=pl.ANY)
    return pl.pallas_call(
        body, name=name, in_specs=[any_spec] * len(srcs), out_specs=[any_spec] * len(out_sds), out_shape=out_sds,
        scratch_shapes=[pltpu.SemaphoreType.DMA((n_remote,)), pltpu.SemaphoreType.DMA((n_remote,)),
                        pltpu.SemaphoreType.DMA((max(n_local, 1),))],
        compiler_params=pltpu.CompilerParams(has_side_effects=True),
    )(*srcs)


def _remote(src, dst, send_sems, recv_sems, k, target):
    return pltpu.make_async_remote_copy(src, dst, send_sems.at[k], recv_sems.at[k], device_id=target,
                                        device_id_type=MESH)


def _half_rows(c, rows):
    half = rows // 2
    return pl.ds(pl.multiple_of(c * half, 8), half)


def gather_shards(shards):
    nt = len(shards)

    def body(*refs):
        srcs, outs = refs[:nt], refs[nt:2 * nt]
        send_sems, recv_sems, _ = refs[2 * nt:]
        x, y, c = _pos()
        j = 2 * x + y
        sib = (x, y, 1 - c)
        own = [_remote(srcs[t], outs[t].at[j], send_sems, recv_sems, 7 * t + 6, sib) for t in range(nt)]
        first, passed = [], []
        for k, (fx, fy) in enumerate(CHIP_FLIPS):
            tx, ty = x ^ fx, y ^ fy
            jk = 2 * tx + ty
            for t in range(nt):
                rows = _half_rows(c, srcs[t].shape[0])
                first.append(_remote(srcs[t].at[rows], outs[t].at[j, rows], send_sems, recv_sems, 7 * t + k, (tx, ty, c)))
                passed.append(_remote(outs[t].at[jk, rows], outs[t].at[jk, rows], send_sems, recv_sems, 7 * t + 3 + k, sib))
        for cp in first + own:
            cp.start()
        for a, b in zip(first, passed):
            a.wait_recv()
            b.start()
        for cp in passed + own:
            cp.wait_recv()
        for cp in first + passed + own:
            cp.wait_send()

    return _comm_call(body, shards, [SDS((4,) + s.shape, s.dtype) for s in shards], 7 * nt, 0, "gather_shards")


def swap_halves(ps):
    nt = len(ps)

    def body(*refs):
        srcs, outs = refs[:nt], refs[nt:2 * nt]
        send_sems, recv_sems, _ = refs[2 * nt:]
        x, y, c = _pos()
        cps = [_remote(srcs[t].at[a, _half_rows(1 - c, srcs[t].shape[1])], outs[t].at[a], send_sems, recv_sems, 4 * t + a,
                       (x, y, 1 - c)) for t in range(nt) for a in range(4)]
        for cp in cps:
            cp.start()
        for cp in cps:
            cp.wait()

    return _comm_call(body, ps, [SDS((4, p.shape[1] // 2, p.shape[2]), p.dtype) for p in ps], 4 * nt, 0, "swap_halves")


def scatter_to_chips(qs):
    nt = len(qs)

    def body(*refs):
        srcs, outs = refs[:nt], refs[nt:2 * nt]
        send_sems, recv_sems, _ = refs[2 * nt:]
        x, y, c = _pos()
        cps = []
        for k, (fx, fy) in enumerate(CHIP_FLIPS):
            tx, ty = x ^ fx, y ^ fy
            for t in range(nt):
                cps.append(_remote(srcs[t].at[2 * tx + ty], outs[t].at[k], send_sems, recv_sems, 3 * t + k, (tx, ty, c)))
        for cp in cps:
            cp.start()
        for cp in cps:
            cp.wait()

    return _comm_call(body, qs, [SDS((3,) + q.shape[1:], q.dtype) for q in qs], 3 * nt, 0, "scatter_to_chips")


def join_halves(fs):
    nt = len(fs)

    def body(*refs):
        outs = refs[nt:2 * nt]
        send_sems, recv_sems, _ = refs[2 * nt:]
        x, y, c = _pos()
        cps = []
        for t in range(nt):
            mine = outs[t].at[0, _half_rows(c, outs[t].shape[1])]
            cps.append(_remote(mine, mine, send_sems, recv_sems, t, (x, y, 1 - c)))
        for cp in cps:
            cp.start()
        for cp in cps:
            cp.wait()

    any_spec = pl.BlockSpec(memory_space=pl.ANY)
    return pl.pallas_call(
        body, name="join_halves", in_specs=[any_spec] * nt, out_specs=[any_spec] * nt,
        out_shape=[SDS(f.shape, f.dtype) for f in fs], input_output_aliases={t: t for t in range(nt)},
        scratch_shapes=[pltpu.SemaphoreType.DMA((nt,)), pltpu.SemaphoreType.DMA((nt,)), pltpu.SemaphoreType.DMA((1,))],
        compiler_params=pltpu.CompilerParams(has_side_effects=True),
    )(*fs)


DEV_FLIPS = tuple((fx, fy, fc) for fx in (0, 1) for fy in (0, 1) for fc in (0, 1))[1:]


def gather_devices(block, *, name):
    def body(src, out, send_sems, recv_sems, loc_sems):
        x, y, c = _pos()
        me = 4 * x + 2 * y + c
        mine = pltpu.make_async_copy(src, out.at[me], loc_sems.at[0])
        mine.start()
        cps = [_remote(src, out.at[me], send_sems, recv_sems, k, (x ^ fx, y ^ fy, c ^ fc))
               for k, (fx, fy, fc) in enumerate(DEV_FLIPS)]
        for cp in cps:
            cp.start()
        for cp in cps:
            cp.wait()
        mine.wait()

    return _comm_call(body, [block], [SDS((N_DEV,) + block.shape, block.dtype)], 7, 1, name)[0]


def gather_small(s):
    def body(src, out, send_sems, recv_sems, _):
        x, y, c = _pos()
        j = 2 * x + y
        cps = [_remote(src, out.at[j], send_sems, recv_sems, k, (x ^ fx, y ^ fy, c)) for k, (fx, fy) in enumerate(CHIP_FLIPS)]
        cps.append(_remote(src, out.at[j], send_sems, recv_sems, 3, (x, y, 1 - c)))
        for cp in cps:
            cp.start()
        for cp in cps:
            cp.wait()

    return _comm_call(body, [s], [SDS((4,) + s.shape, s.dtype)], 4, 0, "gather_small")[0]


SMALL_SHAPES = ((1, 2, 32, 64), (1, 2, 32, 64), (1, 2, 32), (1, 2, 32, 16, 64),
                (1, 2, 32, 16, 64), (1, 2, 32, 16, 64), (1, 2, 32, 16, 64), (1, D_S5), (1, D_S5), (1, 2, 4), (1, 2, 4),
                (1, GDN_HEAD), (1, D_MODEL), (1, D_MODEL))
SMALL_SWAPPED = (3, 4)


def _size(shape):
    return functools.reduce(lambda p, q: p * q, shape)


SMALL_ROWS = tuple(-(-_size(s) // LANES) for s in SMALL_SHAPES)
SMALL_TOTAL = 2176
SMALL_QUARTER = SMALL_TOTAL // 4


def _rows(a):
    flat = a.reshape(-1)
    pad = (-flat.shape[0]) % LANES
    if pad:
        flat = jnp.concatenate([flat, jnp.zeros((pad,), flat.dtype)])
    return flat.reshape(-1, LANES)


def _pack_small(parts):
    rows = [_rows(p) for p in parts]
    rows.append(jnp.zeros((SMALL_TOTAL - sum(SMALL_ROWS), LANES), f32))
    return jnp.concatenate(rows, axis=0)


def _unpack_small(buf):
    out, r = [], 0
    for s, n in zip(SMALL_SHAPES, SMALL_ROWS):
        out.append(buf[r:r + n].reshape(-1)[:_size(s)].reshape(s))
        r += n
    return out


def _as_2d(a):
    return a.reshape(1, -1) if a.ndim == 1 else a.reshape(-1, a.shape[-1])


S5_BG = S5_GROUPS // S5_BLOCKS


def _block_diag_in(bb):
    eye = jnp.eye(S5_BG, dtype=bb.dtype)
    b4 = bb.reshape(S5_BLOCKS, S5_BG, S5_GROUP, S5_STATE)
    return jnp.einsum('jgcp,gh->jgchp', b4, eye).reshape(S5_BLOCKS, S5_BC, S5_BS)


def _block_diag_in_t(d):
    d6 = d.reshape(S5_BLOCKS, S5_BG, S5_GROUP, S5_BG, S5_STATE)
    return jnp.einsum('jgcgp->jgcp', d6).reshape(S5_GROUPS, S5_GROUP * S5_STATE)


def _block_diag_out(cm):
    eye = jnp.eye(S5_BG, dtype=cm.dtype)
    c4 = cm.reshape(S5_BLOCKS, S5_BG, S5_GROUP, S5_STATE)
    return jnp.einsum('jgcp,gh->jhpgc', c4, eye).reshape(S5_BLOCKS, S5_BS, S5_BC)


def _block_diag_out_t(d):
    d6 = d.reshape(S5_BLOCKS, S5_BG, S5_STATE, S5_BG, S5_GROUP)
    return jnp.einsum('jgpgc->jgcp', d6).reshape(S5_GROUPS, S5_GROUP, S5_STATE)


def _to_chunk_rows(a):
    B, L, W = a.shape
    return a.reshape(B, L // CHUNK, CHUNK, W).transpose(0, 1, 3, 2)


def _from_chunk_rows(a):
    B, nc, W, _ = a.shape
    return a.transpose(0, 1, 3, 2).reshape(B, nc * CHUNK, W)


def local_step(x, ctx, tgt, m, w_in, lam_re, lam_im, log_dt, b_re, b_im, c_re, c_im, s5_d,
               w_glu, b_glu, conv16, a_log, dt_bias, norm_w, w_out, ln_g, ln_b):
    B, L, _ = x.shape
    zeros_state = jnp.zeros((B, GDN_HEADS, GDN_HEAD, GDN_HEAD), f32)

    shift, scale, gate = m[:B, :D_MODEL], m[:B, D_MODEL:2 * D_MODEL], m[:B, 2 * D_MODEL:]
    mod = jnp.stack([scale, shift], axis=1)
    mod_c = jnp.broadcast_to(jnp.stack([m[B, D_MODEL:2 * D_MODEL], m[B, :D_MODEL]], axis=0)[None], (B, 2, D_MODEL))

    u, z_s5, qkv, z_gdn, ba = in_proj_fwd(x, mod, w_in, name="in_proj_fwd")
    uc, _, qkvc, _, bac = in_proj_fwd(ctx, mod_c, w_in, name="in_proj_fwd_ctx")

    ng = N_DIR * S5_GROUPS
    zoh_in = (lam_re.reshape(ng, S5_STATE), lam_im.reshape(ng, S5_STATE), log_dt.reshape(ng, 1),
              b_re.reshape(ng, S5_GROUP * S5_STATE), b_im.reshape(ng, S5_GROUP * S5_STATE))
    expand = (jnp.arange(S5_GROUP * S5_STATE)[None, :] % S5_STATE == jnp.arange(S5_STATE)[:, None]).astype(f32)
    ar, ai, bbr, bbi = s5_zoh_fwd(*zoh_in, expand)
    bbr16, bbi16 = bbr.astype(bf16), bbi.astype(bf16)
    c_re16 = c_re.reshape(N_DIR, S5_GROUPS, S5_GROUP, S5_STATE).astype(bf16)
    c_im16 = (-c_im).reshape(N_DIR, S5_GROUPS, S5_GROUP, S5_STATE).astype(bf16)
    s5w, ys, hins, hins_c = [], [], [], []
    for d in range(N_DIR):
        g = slice(d * S5_GROUPS, (d + 1) * S5_GROUPS)
        wd = (_block_diag_in(bbr16[g]), _block_diag_in(bbi16[g]), _block_diag_out(c_re16[d]), _block_diag_out(c_im16[d]),
              jnp.stack([ar[g].reshape(-1), ai[g].reshape(-1)], axis=0))
        s5w.append(wd)
        hin_c, hend_c = s5_scan_fwd(uc, *wd, jnp.zeros((B, 2, S5_HALF), f32), d=d, need_y=False, name=f"s5_fwd_ctx{d}")
        y_d, hin, _ = s5_scan_fwd(u, *wd, hend_c, d=d, need_y=True, name=f"s5_fwd{d}")
        ys.append(y_d)
        hins.append(hin)
        hins_c.append(hin_c)
    glu_w = (s5_d.reshape(1, D_S5), w_glu, b_glu.reshape(1, D_S5))

    act, pre = conv_fwd(qkv, conv16, is_ctx=False, name="conv_fwd")
    act_c, pre_c = conv_fwd(qkvc, conv16, is_ctx=True, name="conv_fwd_ctx")
    pad8 = jnp.zeros((1, 8), f32)
    alog16 = jnp.concatenate([pad8, a_log.reshape(1, 8)], axis=1)
    dtb16 = jnp.concatenate([pad8, dt_bias.reshape(1, 8)], axis=1)
    bg = gates_fwd(ba, alog16, dtb16, name="gates_fwd")
    bg_c = gates_fwd(bac, alog16, dtb16, name="gates_fwd_ctx")
    bgr, bgr_c = _to_chunk_rows(bg), _to_chunk_rows(bg_c)
    cks_c, s_c = gdn_fwd(act_c, bg_c, bgr_c, (zeros_state, zeros_state), need_o=False, name="gdn_fwd_ctx")
    os_, cks, _ = gdn_fwd(act, bg, bgr, s_c, need_o=True, name="gdn_fwd")
    nw = norm_w.reshape(1, GDN_HEAD)

    (loss8, du_skip, dy, dz_s5, do, dz_gdn, gx_res, dws, dwg, dgate, dlng, dlnb, d_s5_d, d_w_glu, d_b_glu,
     d_norm_w) = tail_fwd_bwd(u, ys[0], ys[1], z_s5, os_[0], os_[1], z_gdn, x, tgt, gate[:, None, :],
                              ln_g.reshape(1, D_MODEL), ln_b.reshape(1, D_MODEL), w_out[:D_S5], w_out[D_S5:], *glu_w, nw)
    loss = jnp.sum(loss8[:, 0, 0])
    d_w_out = jnp.concatenate([dws, dwg], axis=0)

    dacts, dbgs, dbgrs, ds0s = gdn_bwd(act, bg, bgr, cks, do, (zeros_state, zeros_state), name="gdn_bwd")
    dacts_c, dbgs_c, dbgrs_c, _ = gdn_bwd(act_c, bg_c, bgr_c, cks_c, None, ds0s, name="gdn_bwd_ctx")
    dbg = dbgs[0] + dbgs[1] + _from_chunk_rows(dbgrs[0] + dbgrs[1])
    dbg_c = dbgs_c[0] + dbgs_c[1] + _from_chunk_rows(dbgrs_c[0] + dbgrs_c[1])
    dba, dal, ddt = gates_bwd(ba, alog16, dtb16, dbg, name="gates_bwd")
    dbac, dal_c, ddt_c = gates_bwd(bac, alog16, dtb16, dbg_c, name="gates_bwd_ctx")
    d_a_log = (dal + dal_c)[:, 8:].reshape(1, N_DIR, GDN_HEADS)
    d_dt_bias = (ddt + ddt_c)[:, 8:].reshape(1, N_DIR, GDN_HEADS)
    dqkv, dcw = conv_bwd(qkv, pre, conv16, dacts[0], dacts[1], is_ctx=False, name="conv_bwd")
    dqkvc, dcw_c = conv_bwd(qkvc, pre_c, conv16, dacts_c[0], dacts_c[1], is_ctx=True, name="conv_bwd_ctx")
    d_conv16 = jnp.sum(dcw, axis=0) + jnp.sum(dcw_c, axis=0)

    dus, ducs = [du_skip], []
    dar, dai, dbbr, dbbi, dcre, dcim = [], [], [], [], [], []
    for d in range(N_DIR):
        du_d, dbre1, dbim1, dct1, dcb1, da1, dh0 = s5_scan_bwd(u, dy, *s5w[d], hins[d],
                                                                jnp.zeros((B, 2, S5_HALF), f32), d=d, name=f"s5_bwd{d}")
        duc_d, dbre2, dbim2, _, _, da2, _ = s5_scan_bwd(uc, None, *s5w[d], hins_c[d], dh0, d=d, name=f"s5_bwd_ctx{d}")
        dus.append(du_d)
        ducs.append(duc_d)
        da = da1 + da2
        dar.append(da[0].reshape(S5_GROUPS, S5_STATE))
        dai.append(da[1].reshape(S5_GROUPS, S5_STATE))
        dbbr.append(_block_diag_in_t(dbre1 + dbre2))
        dbbi.append(_block_diag_in_t(dbim1 + dbim2))
        dcre.append(_block_diag_out_t(dct1))
        dcim.append(-_block_diag_out_t(dcb1))
    dlr, dli, dldt, dbre, dbim = s5_zoh_bwd(*zoh_in, expand, jnp.concatenate(dar, 0), jnp.concatenate(dai, 0),
                                            jnp.concatenate(dbbr, 0), jnp.concatenate(dbbi, 0))
    d_s5 = (dlr, dli, dldt, dbre, dbim, jnp.stack(dcre, 0), jnp.stack(dcim, 0))

    padg = lambda a: jnp.concatenate([a, jnp.zeros(a.shape[:2] + (LANES - N_GATE,), f32)], axis=2)
    zc = jnp.zeros_like(uc)
    dw_c, dmod_c = in_proj_bwd(ctx, mod_c, (tuple(ducs), zc, dqkvc, zc, padg(dbac)), w_in, None, None,
                               name="in_proj_bwd_ctx")
    d_w_in, dmod, grad_x = in_proj_bwd(x, mod, (tuple(dus), dz_s5, dqkv, dz_gdn, padg(dba)), w_in, gx_res, dw_c,
                                       name="in_proj_bwd")
    dmod_c = jnp.sum(dmod_c, axis=0)

    dm_rows = jnp.concatenate([dmod[:, 1], dmod[:, 0], dgate[:, 0]], axis=1)
    dm_ctx = jnp.concatenate([dmod_c[1], dmod_c[0], jnp.zeros((D_MODEL,), f32)])[None]
    dm = jnp.concatenate([dm_rows, dm_ctx], axis=0)
    small = (*d_s5, d_s5_d, d_b_glu, d_a_log, d_dt_bias, d_norm_w, dlng, dlnb)
    small = tuple(g.reshape(s) for g, s in zip(small, SMALL_SHAPES))
    return loss, grad_x, (d_w_in, d_w_out, d_w_glu, d_conv16), small, dm


SHARDED = (1, 3, 18, 12, 14)
REDUCED = (3, 18, 12, 14)
UNSHARDED = tuple(i for i in range(21) if i not in SHARDED)
SMALL = tuple(i for i in UNSHARDED if i not in (0, 2))
W_IN_SHARD = 772


def _conv_rows(w):
    return jnp.concatenate([w.reshape(9, w.shape[-1]), jnp.zeros((CONV_ROWS - 9, w.shape[-1]), f32)], axis=0)


def kernel(x, c, ctx, c_ctx, w_ada, b_ada, w_in, s5_lambda_re, s5_lambda_im, s5_log_dt, s5_b_re, s5_b_im, s5_c_re, s5_c_im, s5_d, w_glu, b_glu, conv_w, gdn_a_log, gdn_dt_bias, gdn_norm_w, w_out, ln_g, ln_b, loss_target, m_c_ctx, m_w_ada, m_b_ada, m_w_in, m_s5_lambda_re, m_s5_lambda_im, m_s5_log_dt, m_s5_b_re, m_s5_b_im, m_s5_c_re, m_s5_c_im, m_s5_d, m_w_glu, m_b_glu, m_conv_w, m_gdn_a_log, m_gdn_dt_bias, m_gdn_norm_w, m_w_out, m_ln_g, m_ln_b, v_c_ctx, v_w_ada, v_b_ada, v_w_in, v_s5_lambda_re, v_s5_lambda_im, v_s5_log_dt, v_s5_b_re, v_s5_b_im, v_s5_c_re, v_s5_c_im, v_s5_d, v_w_glu, v_b_glu, v_conv_w, v_gdn_a_log, v_gdn_dt_bias, v_gdn_norm_w, v_w_out, v_ln_g, v_ln_b):
    weights = [c_ctx, w_ada, b_ada, w_in, s5_lambda_re, s5_lambda_im, s5_log_dt, s5_b_re, s5_b_im, s5_c_re, s5_c_im,
               s5_d, w_glu, b_glu, conv_w, gdn_a_log, gdn_dt_bias, gdn_norm_w, w_out, ln_g, ln_b]
    ms = [m_c_ctx, m_w_ada, m_b_ada, m_w_in, m_s5_lambda_re, m_s5_lambda_im, m_s5_log_dt, m_s5_b_re, m_s5_b_im,
          m_s5_c_re, m_s5_c_im, m_s5_d, m_w_glu, m_b_glu, m_conv_w, m_gdn_a_log, m_gdn_dt_bias, m_gdn_norm_w, m_w_out,
          m_ln_g, m_ln_b]
    vs = [v_c_ctx, v_w_ada, v_b_ada, v_w_in, v_s5_lambda_re, v_s5_lambda_im, v_s5_log_dt, v_s5_b_re, v_s5_b_im,
          v_s5_c_re, v_s5_c_im, v_s5_d, v_w_glu, v_b_glu, v_conv_w, v_gdn_a_log, v_gdn_dt_bias, v_gdn_norm_w, v_w_out,
          v_ln_g, v_ln_b]
    cpos = lax.axis_index("c")
    jchip = 2 * lax.axis_index("x") + lax.axis_index("y")
    me = 2 * jchip + cpos

    c_all = gather_devices(c, name="gather_c")
    cc = jnp.concatenate([c_all, jnp.broadcast_to(c_ctx[None, None, :], (N_DEV, 1, D_MODEL)),
                          jnp.zeros((N_DEV, 5, D_MODEL), f32)], axis=1)
    w_ada16 = w_ada[0].astype(bf16)
    b_cols = lax.dynamic_slice_in_dim(b_ada, jchip * ADA_SHARD, ADA_SHARD, axis=1)
    m_cols = gather_devices(ada_fwd(cc, w_ada16, b_cols).reshape(N_DEV * 8, ADA_SHARD), name="gather_m")
    m_mine = lax.dynamic_index_in_dim(m_cols.reshape(N_DEV, N_DEV, 8, ADA_SHARD), me, axis=1, keepdims=False)
    m_rows = jnp.concatenate([m_mine[2 * j, :3] for j in range(4)], axis=1)

    conv_shard = _conv_rows(conv_w)
    g_in, g_out, g_glu, g_conv = gather_shards(
        [w_in[0].astype(bf16), w_out[0].astype(bf16), w_glu[0].astype(bf16), conv_shard])
    w_in_pad = jnp.concatenate([g_in[0], g_in[1], g_in[2], g_in[3], jnp.zeros((D_MODEL, IN_PAD - P_IN), bf16)], axis=1)
    conv16 = g_conv.transpose(1, 0, 2).reshape(CONV_ROWS, 3 * D_GDN)

    swap = lambda a: jnp.swapaxes(a, 3, 4)
    loss, grad_x, big, small, dm_rows = local_step(
        x, ctx, loss_target, m_rows, w_in_pad, s5_lambda_re, s5_lambda_im, s5_log_dt, swap(s5_b_re), swap(s5_b_im),
        s5_c_re, s5_c_im, s5_d, g_glu.reshape(D_S5, D_S5), b_glu, conv16, gdn_a_log, gdn_dt_bias, gdn_norm_w,
        g_out.reshape(D_MODEL, D_MODEL), ln_g, ln_b)
    loss = lax.psum(loss, ("x", "y", "c"))

    dm_all = gather_devices(jnp.concatenate([dm_rows, jnp.zeros((5, 3 * D_MODEL), f32)], axis=0), name="gather_dm")
    dm_cols = lax.dynamic_slice_in_dim(dm_all, jchip * ADA_SHARD, ADA_SHARD, axis=2)
    g_w_ada, g_b_ada, p = ada_bwd(cc, w_ada16, dm_cols, dm_all)
    g_c_ctx = c_ctx_bwd(gather_devices(p, name="gather_p"), c_ctx[None, :])[0]

    d_w_in, d_w_out, d_w_glu, d_conv16 = big
    slabs = [d_w_in[:, :P_IN].reshape(D_MODEL, 4, W_IN_SHARD).transpose(1, 0, 2),
             d_w_out.reshape(4, D_MODEL // 4, D_MODEL),
             d_w_glu.reshape(4, D_S5 // 4, D_S5),
             d_conv16.reshape(CONV_ROWS, 4, 3 * D_GDN // 4).transpose(1, 0, 2),
             _pack_small(small).reshape(4, SMALL_QUARTER, LANES)]
    got = swap_halves(slabs)
    q32, q16 = [], []
    for t, (s, g) in enumerate(zip(slabs, got)):
        own = lax.dynamic_index_in_dim(s.reshape(4, 2, s.shape[1] // 2, s.shape[2]), cpos, axis=1, keepdims=False)
        a, b = sum_cores(own, g, name=f"sum_cores{t}")
        q32.append(a)
        q16.append(b)
    rec = scatter_to_chips(q16)
    fs = [sum_chips(lax.dynamic_index_in_dim(q, jchip, axis=0, keepdims=False), r, cpos, name=f"sum_chips{t}")
          for t, (q, r) in enumerate(zip(q32, rec))]
    red = join_halves(fs)
    g_small = _unpack_small(gather_small(red[4][0]).reshape(SMALL_TOTAL, LANES))
    g_shard = {1: g_w_ada, 3: red[0], 18: red[1], 12: red[2], 14: red[3]}

    grads, deltas, new_m, new_v = [None] * 21, [None] * 21, [None] * 21, [None] * 21
    for t, i in enumerate(SHARDED):
        conv, win = i == 14, i == 3
        prep = (lambda a: _conv_rows(a)[None]) if conv else ((lambda a: jnp.transpose(a, (2, 0, 1))) if win else (lambda a: a))
        g = jnp.transpose(g_shard[i], (2, 0, 1)) if win else g_shard[i]
        d, nm, nv = adamw_3d(prep(weights[i]), g, prep(ms[i]), prep(vs[i]), lead=win, name=f"adamw{t}")
        for lst, val in ((grads, g), (deltas, d), (new_m, nm), (new_v, nv)):
            lst[i] = (val[0, :9].reshape(weights[i].shape) if conv else (jnp.transpose(val, (1, 2, 0)) if win else val))
    g_un = {0: g_c_ctx, 2: g_b_ada, **{i: g_small[n] for n, i in enumerate(SMALL)}}
    swapped = [SMALL[n] for n in SMALL_SWAPPED]
    small_in = lambda lst: [_as_2d(swap(lst[i]) if i in swapped else lst[i]) for i in UNSHARDED]
    sm = adamw_small(small_in(weights), [_as_2d(g_un[i]) for i in UNSHARDED], small_in(ms), small_in(vs))
    for n, i in enumerate(UNSHARDED):
        back = ((lambda a: swap(a.reshape(swap(weights[i]).shape))) if i in swapped
                else (lambda a: a.reshape(weights[i].shape)))
        grads[i] = back(g_un[i])
        for lst, res in ((deltas, sm[0]), (new_m, sm[1]), (new_v, sm[2])):
            lst[i] = back(res[n])
    return (loss, grad_x, *grads, *deltas, *new_m, *new_v)
```

```python
import functools

import jax
import jax.numpy as jnp
from jax import lax
from jax.experimental import pallas as pl
from jax.experimental.pallas import tpu as pltpu

f32 = jnp.float32
bf16 = jnp.bfloat16
SDS = jax.ShapeDtypeStruct

D_MODEL = 1024
D_S5 = 512
S5_GROUP = 16
S5_GROUPS = 32
S5_STATE = 64
S5_HALF = S5_GROUPS * S5_STATE
D_GDN = 512
GDN_HEAD = 128
GDN_HEADS = 4
CHUNK = 64
GRID_W = 64
N_DIR = 2
P_IN = 3088
DEEPNORM_ALPHA = 2.0 ** 0.25
LN_EPS = 1e-5
NORM_EPS = 1e-6
ADAM_LR, ADAM_B1, ADAM_B2, ADAM_EPS, ADAM_WD, ADAM_STEP = 0.001, 0.9, 0.999, 1e-08, 0.01, 10

LANES = 128
VMEM_LIMIT = 56 * 1024 * 1024
TOK_TILE = 256
S5_TILE = 256
MESH = pl.DeviceIdType.MESH


def _cparams(n_grid):
    return pltpu.CompilerParams(dimension_semantics=("arbitrary",) * n_grid, vmem_limit_bytes=VMEM_LIMIT)


def _dot(a, b):
    return jnp.dot(a.astype(bf16), b.astype(bf16), preferred_element_type=f32)


def _dot_nt(a, b):
    return lax.dot_general(a.astype(bf16), b.astype(bf16), (((1,), (1,)), ((), ())), preferred_element_type=f32)


def _dot_tn(a, b):
    return lax.dot_general(a.astype(bf16), b.astype(bf16), (((0,), (0,)), ((), ())), preferred_element_type=f32)


def _dot_hi(a, b):
    return jnp.dot(a, b, precision=lax.Precision.HIGHEST, preferred_element_type=f32)


def _dot_h3(a, b):
    return jnp.dot(a, b, precision=lax.Precision.HIGH, preferred_element_type=f32)


@jax.custom_vjp
def _mm(a, b):
    return _dot(a, b)


@jax.custom_vjp
def _mm_nt(a, b):
    return _dot_nt(a, b)


@jax.custom_vjp
def _mm_tn(a, b):
    return _dot_tn(a, b)


_mm.defvjp(lambda a, b: (_dot(a, b), (a, b)), lambda r, g: (_mm_nt(g, r[1]), _mm_tn(r[0], g)))
_mm_nt.defvjp(lambda a, b: (_dot_nt(a, b), (a, b)), lambda r, g: (_mm(g, r[1]), _mm_tn(g, r[0])))
_mm_tn.defvjp(lambda a, b: (_dot_tn(a, b), (a, b)), lambda r, g: (_mm_nt(r[1], g), _mm(r[0], g)))


def _silu(x):
    return x * jax.nn.sigmoid(x)


def _gelu(x):
    return 0.5 * x * (1.0 + lax.erf(x * (2.0 ** -0.5)))


def _resident(shape):
    nd = len(shape)
    return pl.BlockSpec(shape, lambda *_: (0,) * nd, pipeline_mode=pl.Buffered(1))


def _tok(tile, width, nt=None, rev=False):
    if rev:
        return pl.BlockSpec((None, tile, width), lambda b, n: (b, nt - 1 - n, 0))
    return pl.BlockSpec((None, tile, width), lambda b, n: (b, n, 0))


def _per_batch(rows, width):
    return pl.BlockSpec((None, rows, width), lambda b, n: (b, 0, 0))


def _first_step():
    return jnp.logical_and(pl.program_id(0) == 0, pl.program_id(1) == 0)


ADA_SHARD = 3 * D_MODEL // 4
N_DEV = 8


def ada_fwd(cc, w, b):
    def body(cc_ref, w_ref, b_ref, m_ref):
        for k in range(N_DEV):
            m_ref[k] = _dot(_silu(cc_ref[k]), w_ref[...]) + b_ref[...]

    return pl.pallas_call(body, name="ada_fwd", out_shape=SDS((N_DEV, 8, ADA_SHARD), f32),
                          compiler_params=pltpu.CompilerParams(vmem_limit_bytes=VMEM_LIMIT))(cc, w, b)


def ada_bwd(cc, w, dmj):
    def body(cc_ref, w_ref, dmj_ref, dw_ref, pb_ref):
        dw = jnp.zeros((D_MODEL, ADA_SHARD), f32)
        p = jnp.zeros((8, D_MODEL), f32)
        db = jnp.zeros((1, ADA_SHARD), f32)
        for k in range(N_DEV):
            dw = dw + _dot_tn(_silu(cc_ref[k]), dmj_ref[k])
            p = p + _dot_nt(dmj_ref[k], w_ref[...])
            db = db + jnp.sum(dmj_ref[k], axis=0, keepdims=True)
        dw_ref[0] = dw
        pb_ref[...] = jnp.zeros_like(pb_ref)
        pb_ref[0:1, :] = p[2:3, :]
        pb_ref[1:2, 0:ADA_SHARD] = db

    return pl.pallas_call(
        body, name="ada_bwd", out_shape=[SDS((1, D_MODEL, ADA_SHARD), f32), SDS((8, D_MODEL), f32)],
        compiler_params=pltpu.CompilerParams(vmem_limit_bytes=VMEM_LIMIT))(cc, w, dmj)


def c_ctx_bwd(pb_all, c_ctx):
    def body(p_ref, c_ref, d_ref):
        ds = ((p_ref[0, 0:1, :] + p_ref[2, 0:1, :]) + p_ref[4, 0:1, :]) + p_ref[6, 0:1, :]
        _, vjp = jax.vjp(_silu, c_ref[...])
        d_ref[...] = vjp(ds)[0]

    return pl.pallas_call(body, name="c_ctx_bwd", out_shape=SDS((1, D_MODEL), f32))(pb_all, c_ctx)


N_GATE = 2 * N_DIR * GDN_HEADS
IN_WIDTHS = (D_S5, D_S5, 3 * D_GDN, D_GDN, LANES)
IN_OFFS = (0, 512, 1024, 2560, 3072)
IN_PAD = 3200


def in_proj_fwd(x, mod, w, *, name):
    B, L, _ = x.shape
    T = min(TOK_TILE, L)

    def body(x_ref, mod_ref, w_ref, *o_refs):
        h = (x_ref[...] * (1.0 + mod_ref[0:1, :]) + mod_ref[1:2, :]).astype(bf16)
        for o_ref, off, wd in zip(o_refs, IN_OFFS, IN_WIDTHS):
            r = _dot(h, w_ref[:, off:off + wd])
            o_ref[...] = r[:, :o_ref.shape[-1]]

    outw = (D_S5, D_S5, 3 * D_GDN, D_GDN, N_GATE)
    return pl.pallas_call(
        body, name=name, grid=(B, L // T),
        in_specs=[_tok(T, D_MODEL), _per_batch(2, D_MODEL), _resident((D_MODEL, IN_PAD))],
        out_specs=[_tok(T, wd) for wd in outw],
        out_shape=[SDS((B, L, wd), f32) for wd in outw],
        compiler_params=_cparams(2),
    )(x, mod, w)


def in_proj_bwd(x, mod, ds, w, gx_res, dw_start, *, name):
    B, L, _ = x.shape
    T = min(TOK_TILE, L)
    with_dx = gx_res is not None
    with_start = dw_start is not None
    n_u = len(ds[0])

    def body(*refs):
        x_ref, mod_ref = refs[0], refs[1]
        du_refs = refs[2:2 + n_u]
        d_refs = refs[2 + n_u:6 + n_u]
        w_ref = refs[6 + n_u]
        k = 7 + n_u
        if with_dx:
            gx_ref = refs[k]
            k += 1
        if with_start:
            start_ref = refs[k]
            k += 1
        dw_ref, dmod_ref = refs[k], refs[k + 1]
        if with_dx:
            dx_ref = refs[k + 2]
        n = pl.program_id(1)

        @pl.when(_first_step())
        def _():
            dw_ref[...] = start_ref[...] if with_start else jnp.zeros_like(dw_ref)

        @pl.when(n == 0)
        def _():
            dmod_ref[...] = jnp.zeros_like(dmod_ref)

        xv = x_ref[...]
        scale1 = 1.0 + mod_ref[0:1, :]
        h = (xv * scale1 + mod_ref[1:2, :]).astype(bf16)
        du = du_refs[0][...]
        for r in du_refs[1:]:
            du = du + r[...]
        dh = jnp.zeros((T, D_MODEL), f32)
        for dv, off, wd in zip([du] + [r[...] for r in d_refs], IN_OFFS, IN_WIDTHS):
            dv = dv.astype(bf16)
            dh = dh + _dot_nt(dv, w_ref[:, off:off + wd])
            dw_ref[:, off:off + wd] += _dot_tn(h, dv)
        dmod_ref[0:1, :] += jnp.sum(dh * xv, axis=0, keepdims=True)
        dmod_ref[1:2, :] += jnp.sum(dh, axis=0, keepdims=True)
        if with_dx:
            dx_ref[...] = gx_ref[...] + dh * scale1

    in_specs = ([_tok(T, D_MODEL), _per_batch(2, D_MODEL)] + [_tok(T, D_S5)] * n_u + [_tok(T, wd) for wd in IN_WIDTHS[1:]]
                + [_resident((D_MODEL, IN_PAD))])
    args = [x, mod, *ds[0], *ds[1:], w]
    out_specs = [_resident((D_MODEL, IN_PAD)), _per_batch(2, D_MODEL)]
    out_shape = [SDS((D_MODEL, IN_PAD), f32), SDS((B, 2, D_MODEL), f32)]
    if with_dx:
        in_specs.append(_tok(T, D_MODEL))
        args.append(gx_res)
        out_specs.append(_tok(T, D_MODEL))
        out_shape.append(SDS((B, L, D_MODEL), f32))
    if with_start:
        in_specs.append(_resident((D_MODEL, IN_PAD)))
        args.append(dw_start)
    return pl.pallas_call(body, name=name, grid=(B, L // T), in_specs=in_specs, out_specs=out_specs,
                          out_shape=out_shape, compiler_params=_cparams(2))(*args)


def _s5_zoh(lr, li, ldt, bre, bim, expand):
    dt = jnp.exp(ldt)
    zr, zi = lr * dt, li * dt
    e = jnp.exp(zr)
    ar, ai = e * jnp.cos(zi), e * jnp.sin(zi)
    den = lr * lr + li * li
    czr = ((ar - 1.0) * lr + ai * li) / den
    czi = (ai * lr - (ar - 1.0) * li) / den
    czr_e, czi_e = _dot_hi(czr, expand), _dot_hi(czi, expand)
    return ar, ai, czr_e * bre - czi_e * bim, czr_e * bim + czi_e * bre


_ZOH_OUT = [(N_DIR * S5_GROUPS, S5_STATE)] * 2 + [(N_DIR * S5_GROUPS, S5_STATE * S5_GROUP)] * 2


def s5_zoh_fwd(lr, li, ldt, bre, bim, expand):
    def body(lr_ref, li_ref, ldt_ref, bre_ref, bim_ref, e_ref, ar_ref, ai_ref, bbr_ref, bbi_ref):
        ar, ai, bbr, bbi = _s5_zoh(lr_ref[...], li_ref[...], ldt_ref[...], bre_ref[...], bim_ref[...], e_ref[...])
        ar_ref[...], ai_ref[...], bbr_ref[...], bbi_ref[...] = ar, ai, bbr, bbi

    return pl.pallas_call(body, name="s5_zoh_fwd", out_shape=[SDS(s, f32) for s in _ZOH_OUT])(
        lr, li, ldt, bre, bim, expand)


def s5_zoh_bwd(lr, li, ldt, bre, bim, expand, dar, dai, dbbr, dbbi):
    def body(lr_ref, li_ref, ldt_ref, bre_ref, bim_ref, e_ref, dar_ref, dai_ref, dbbr_ref, dbbi_ref,
             dlr_ref, dli_ref, dldt_ref, dbre_ref, dbim_ref):
        ev = e_ref[...]
        _, vjp = jax.vjp(lambda a, b, c, d, e: _s5_zoh(a, b, c, d, e, ev),
                         lr_ref[...], li_ref[...], ldt_ref[...], bre_ref[...], bim_ref[...])
        outs = vjp((dar_ref[...], dai_ref[...], dbbr_ref[...], dbbi_ref[...]))
        dlr_ref[...], dli_ref[...], dldt_ref[...], dbre_ref[...], dbim_ref[...] = outs

    shapes = [lr.shape, li.shape, ldt.shape, bre.shape, bim.shape]
    return pl.pallas_call(body, name="s5_zoh_bwd", out_shape=[SDS(s, f32) for s in shapes])(
        lr, li, ldt, bre, bim, expand, dar, dai, dbbr, dbbi)


def _scan_rows(T, rev, ar, ai, h0s, refs, off):
    def step(i, carry):
        t = off + ((T - 1 - i) if rev else i)
        out = []
        for (hr, hi), (r_ref, i_ref) in zip(carry, refs):
            nr = ar * hr - ai * hi + r_ref[pl.ds(t, 1), :]
            ni = ar * hi + ai * hr + i_ref[pl.ds(t, 1), :]
            r_ref[pl.ds(t, 1), :] = nr
            i_ref[pl.ds(t, 1), :] = ni
            out.append((nr, ni))
        return tuple(out)

    return lax.fori_loop(0, T, step, tuple(h0s))


S5_BLOCKS = 4
S5_BC = D_S5 // S5_BLOCKS
S5_BS = S5_HALF // S5_BLOCKS


def _s5_in(uv, bre_ref, bim_ref, hr_ref, hi_ref, off, T):
    for jb in range(S5_BLOCKS):
        uj = uv[:, jb * S5_BC:(jb + 1) * S5_BC]
        hr_ref[off:off + T, jb * S5_BS:(jb + 1) * S5_BS] = _dot(uj, bre_ref[jb])
        hi_ref[off:off + T, jb * S5_BS:(jb + 1) * S5_BS] = _dot(uj, bim_ref[jb])


def _s5_specs(B, T, nt, rev):
    tidx = (lambda n: nt - 1 - n) if rev else (lambda n: n)
    tok = pl.BlockSpec((B, T, D_S5), lambda n: (0, tidx(n), 0))
    hin = pl.BlockSpec((B, None, 2, S5_HALF), lambda n: (0, tidx(n), 0, 0))
    state = pl.BlockSpec((B, 2, S5_HALF), lambda n: (0, 0, 0))
    return tok, hin, state


def s5_scan_fwd(u, bre, bim, ctop, cbot, arow, h0, *, d, need_y, name):
    B, L, _ = u.shape
    T = min(S5_TILE, L)
    nt = L // T
    rev = d == 1

    def body(u_ref, bre_ref, bim_ref, ct_ref, cb_ref, a_ref, h0_ref, *rest):
        if need_y:
            y_ref, hin_ref, hend_ref, hr_scr, hi_scr, h_scr = rest
        else:
            hin_ref, hend_ref, hr_scr, hi_scr, h_scr = rest
        n = pl.program_id(0)

        @pl.when(n == 0)
        def _():
            h_scr[...] = h0_ref[...]

        hin_ref[...] = h_scr[...]
        for b in range(B):
            _s5_in(u_ref[b].astype(bf16), bre_ref, bim_ref, hr_scr.at[b], hi_scr.at[b], 0, T)
        hs = _scan_rows(T, rev, a_ref[0:1, :], a_ref[1:2, :], [(h_scr[b, 0:1, :], h_scr[b, 1:2, :]) for b in range(B)],
                        [(hr_scr.at[b], hi_scr.at[b]) for b in range(B)], 0)
        for b in range(B):
            h_scr[b, 0:1, :] = hs[b][0]
            h_scr[b, 1:2, :] = hs[b][1]
            if need_y:
                for jb in range(S5_BLOCKS):
                    st = slice(jb * S5_BS, (jb + 1) * S5_BS)
                    y_ref[b, :, jb * S5_BC:(jb + 1) * S5_BC] = (_dot(hr_scr[b, :, st], ct_ref[jb])
                                                                 + _dot(hi_scr[b, :, st], cb_ref[jb]))

        @pl.when(n == nt - 1)
        def _():
            hend_ref[...] = h_scr[...]

    tok, hin_spec, state = _s5_specs(B, T, nt, rev)
    out_specs = [hin_spec, state]
    out_shape = [SDS((B, nt, 2, S5_HALF), f32), SDS((B, 2, S5_HALF), f32)]
    if need_y:
        out_specs.insert(0, tok)
        out_shape.insert(0, SDS((B, L, D_S5), f32))
    w_in, w_out = _resident((S5_BLOCKS, S5_BC, S5_BS)), _resident((S5_BLOCKS, S5_BS, S5_BC))
    return pl.pallas_call(
        body, name=name, grid=(nt,),
        in_specs=[tok, w_in, w_in, w_out, w_out, _resident((2, S5_HALF)), state],
        out_specs=out_specs, out_shape=out_shape,
        scratch_shapes=[pltpu.VMEM((B, T, S5_HALF), f32), pltpu.VMEM((B, T, S5_HALF), f32),
                        pltpu.VMEM((B, 2, S5_HALF), f32)],
        compiler_params=_cparams(1),
    )(u, bre, bim, ctop, cbot, arow, h0)


def s5_scan_bwd(u, dy, bre, bim, ctop, cbot, arow, hin, dhend, *, d, name):
    B, L, _ = u.shape
    T = min(S5_TILE, L)
    nt = L // T
    rev = d == 1
    has_dy = dy is not None
    PAD = 8

    def body(*refs):
        u_ref = refs[0]
        k = 1
        if has_dy:
            dy_ref = refs[1]
            k = 2
        bre_ref, bim_ref, ct_ref, cb_ref, a_ref, hin_ref, dhend_ref = refs[k:k + 7]
        du_ref, dbre_ref, dbim_ref, dct_ref, dcb_ref, da_ref, dh0_ref = refs[k + 7:k + 14]
        hr_scr, hi_scr, gr_scr, gi_scr, p_scr = refs[k + 14:]
        n = pl.program_id(0)

        @pl.when(n == 0)
        def _():
            for r in (dbre_ref, dbim_ref, dct_ref, dcb_ref, da_ref):
                r[...] = jnp.zeros_like(r)
            p_scr[...] = dhend_ref[...]

        ar, ai = a_ref[0:1, :], a_ref[1:2, :]
        prev_row = PAD + T if rev else PAD - 1
        uvs = []
        for b in range(B):
            uvs.append(u_ref[b].astype(bf16))
            _s5_in(uvs[b], bre_ref, bim_ref, hr_scr.at[b], hi_scr.at[b], PAD, T)
            hr_scr[b, prev_row:prev_row + 1, :] = hin_ref[b, 0:1, :]
            hi_scr[b, prev_row:prev_row + 1, :] = hin_ref[b, 1:2, :]
        _scan_rows(T, rev, ar, ai, [(hin_ref[b, 0:1, :], hin_ref[b, 1:2, :]) for b in range(B)],
                   [(hr_scr.at[b], hi_scr.at[b]) for b in range(B)], PAD)
        if has_dy:
            for b in range(B):
                dyv = dy_ref[b].astype(bf16)
                for jb in range(S5_BLOCKS):
                    st = slice(jb * S5_BS, (jb + 1) * S5_BS)
                    dyj = dyv[:, jb * S5_BC:(jb + 1) * S5_BC]
                    gr_scr[b, :, st] = _dot_nt(dyj, ct_ref[jb])
                    gi_scr[b, :, st] = _dot_nt(dyj, cb_ref[jb])
                    dct_ref[jb] += _dot_tn(hr_scr[b, PAD:PAD + T, st], dyj)
                    dcb_ref[jb] += _dot_tn(hi_scr[b, PAD:PAD + T, st], dyj)
        else:
            gr_scr[...] = jnp.zeros_like(gr_scr)
            gi_scr[...] = jnp.zeros_like(gi_scr)

        def step(i, carry):
            t = i if rev else T - 1 - i
            tp = PAD + t + (1 if rev else -1)
            out = []
            for b, (pr, pi, dar, dai) in enumerate(carry):
                gr = gr_scr[b, pl.ds(t, 1), :] + pr
                gi = gi_scr[b, pl.ds(t, 1), :] + pi
                gr_scr[b, pl.ds(t, 1), :] = gr
                gi_scr[b, pl.ds(t, 1), :] = gi
                hpr = hr_scr[b, pl.ds(tp, 1), :]
                hpi = hi_scr[b, pl.ds(tp, 1), :]
                out.append((ar * gr + ai * gi, ar * gi - ai * gr, dar + hpr * gr + hpi * gi, dai + hpr * gi - hpi * gr))
            return tuple(out)

        zero = jnp.zeros((1, S5_HALF), f32)
        res = lax.fori_loop(0, T, step, tuple((p_scr[b, 0:1, :], p_scr[b, 1:2, :], zero, zero) for b in range(B)))
        for b in range(B):
            pr, pi, dar, dai = res[b]
            p_scr[b, 0:1, :] = pr
            p_scr[b, 1:2, :] = pi
            da_ref[0:1, :] += dar
            da_ref[1:2, :] += dai
            for jb in range(S5_BLOCKS):
                st = slice(jb * S5_BS, (jb + 1) * S5_BS)
                ch = slice(jb * S5_BC, (jb + 1) * S5_BC)
                gr_j = gr_scr[b, :, st].astype(bf16)
                gi_j = gi_scr[b, :, st].astype(bf16)
                du_ref[b, :, ch] = _dot_nt(gr_j, bre_ref[jb]) + _dot_nt(gi_j, bim_ref[jb])
                dbre_ref[jb] += _dot_tn(uvs[b][:, ch], gr_j)
                dbim_ref[jb] += _dot_tn(uvs[b][:, ch], gi_j)

        @pl.when(n == nt - 1)
        def _():
            dh0_ref[...] = p_scr[...]

    tok, hin_spec, state = _s5_specs(B, T, nt, not rev)
    w_in, w_out = _resident((S5_BLOCKS, S5_BC, S5_BS)), _resident((S5_BLOCKS, S5_BS, S5_BC))
    wspecs = [w_in, w_in, w_out, w_out]
    in_specs = [tok] + ([tok] if has_dy else []) + wspecs + [_resident((2, S5_HALF)), hin_spec, state]
    args = [u] + ([dy] if has_dy else []) + [bre, bim, ctop, cbot, arow, hin, dhend]
    return pl.pallas_call(
        body, name=name, grid=(nt,), in_specs=in_specs,
        out_specs=[tok] + wspecs + [_resident((2, S5_HALF)), state],
        out_shape=[SDS((B, L, D_S5), f32), SDS((S5_BLOCKS, S5_BC, S5_BS), f32), SDS((S5_BLOCKS, S5_BC, S5_BS), f32),
                   SDS((S5_BLOCKS, S5_BS, S5_BC), f32), SDS((S5_BLOCKS, S5_BS, S5_BC), f32), SDS((2, S5_HALF), f32),
                   SDS((B, 2, S5_HALF), f32)],
        scratch_shapes=[pltpu.VMEM((B, T + 2 * PAD, S5_HALF), f32), pltpu.VMEM((B, T + 2 * PAD, S5_HALF), f32),
                        pltpu.VMEM((B, T, S5_HALF), f32), pltpu.VMEM((B, T, S5_HALF), f32),
                        pltpu.VMEM((B, 2, S5_HALF), f32)],
        compiler_params=_cparams(1),
    )(*args)


def _glu_fn(u, y0, y1, z, dsk, wg, bg):
    g = _gelu(dsk * u + y0 + y1)
    return g * jax.nn.sigmoid(_mm(g, wg) + bg) * _silu(z)


CONV_ROWS = 16


def _shift(x, s):
    L = x.shape[0]
    k = (-s) % L
    return x if k == 0 else pltpu.roll(x, k, axis=0)


def _conv_masks(L, is_ctx):
    t = lax.broadcasted_iota(jnp.int32, (L, 1), 0)
    if is_ctx:
        return t == L - 1, t == 0, None, None
    col = jnp.bitwise_and(t, GRID_W - 1)
    return col == GRID_W - 1, col == 0, t >= GRID_W, t < L - GRID_W


def _conv_sides(xv, masks):
    no_left, no_right, _, _ = masks
    return _shift(jnp.where(no_left, 0.0, xv), -1), _shift(jnp.where(no_right, 0.0, xv), 1)


def _conv_pre(xv, w_ref, masks, is_ctx):
    xl, xr = _conv_sides(xv, masks)
    z = [w_ref[3 * di:3 * di + 1, :] * xl + w_ref[3 * di + 1:3 * di + 2, :] * xv + w_ref[3 * di + 2:3 * di + 3, :] * xr
         for di in ((1,) if is_ctx else (0, 1, 2))]
    if is_ctx:
        return z[0]
    _, _, has_up, has_down = masks
    return z[1] + jnp.where(has_up, _shift(z[0], -GRID_W), 0.0) + jnp.where(has_down, _shift(z[2], GRID_W), 0.0)


def _conv_pre_bwd(xv, w_ref, dpre, masks, is_ctx, dw_ref):
    no_left, no_right, has_up, has_down = masks
    xl, xr = _conv_sides(xv, masks)
    if is_ctx:
        dz = {1: dpre}
    else:
        dz = {0: _shift(jnp.where(has_up, dpre, 0.0), GRID_W), 1: dpre, 2: _shift(jnp.where(has_down, dpre, 0.0), -GRID_W)}
    dxl = dxc = dxr = None
    for di, d in dz.items():
        for dj, side in enumerate((xl, xv, xr)):
            dw_ref[3 * di + dj:3 * di + dj + 1, :] = jnp.sum(d * side, axis=0, keepdims=True)
        tl, tc, tr = (w_ref[3 * di + dj:3 * di + dj + 1, :] * d for dj in range(3))
        dxl, dxc, dxr = (tl, tc, tr) if dxl is None else (dxl + tl, dxc + tc, dxr + tr)
    return dxc + jnp.where(no_left, 0.0, _shift(dxl, 1)) + jnp.where(no_right, 0.0, _shift(dxr, -1))


def _qk_post(pre, is_norm, scale):
    s = _silu(pre)
    nrm = lax.rsqrt(jnp.sum(s * s, axis=-1, keepdims=True) + NORM_EPS)
    return s * jnp.where(is_norm, nrm * scale, 1.0)


def _conv_kind():
    ct = pl.program_id(1)
    return ct < 2 * GDN_HEADS, jnp.where(ct < GDN_HEADS, GDN_HEAD ** -0.5, 1.0).astype(f32)


def conv_fwd(qkv, w16, *, is_ctx, name):
    B, L, C = qkv.shape
    spec = pl.BlockSpec((None, L, GDN_HEAD), lambda b, ct: (b, 0, ct))
    wspec = pl.BlockSpec((CONV_ROWS, GDN_HEAD), lambda b, ct: (0, ct))

    def body(x_ref, w_ref, o_ref, pre_ref):
        is_norm, scale = _conv_kind()
        pre = _conv_pre(x_ref[...], w_ref, _conv_masks(L, is_ctx), is_ctx)
        pre_ref[...] = pre
        o_ref[...] = _qk_post(pre, is_norm, scale)

    return pl.pallas_call(body, name=name, grid=(B, C // GDN_HEAD), in_specs=[spec, wspec], out_specs=[spec, spec],
                          out_shape=[SDS((B, L, C), f32)] * 2, compiler_params=_cparams(2))(qkv, w16)


def conv_bwd(qkv, pre, w16, da0, da1, *, is_ctx, name):
    B, L, C = qkv.shape
    spec = pl.BlockSpec((None, L, GDN_HEAD), lambda b, ct: (b, 0, ct))
    wspec = pl.BlockSpec((CONV_ROWS, GDN_HEAD), lambda b, ct: (0, ct))
    dwspec = pl.BlockSpec((None, CONV_ROWS, GDN_HEAD), lambda b, ct: (b, 0, ct))

    def body(x_ref, pre_ref, w_ref, d0_ref, d1_ref, dx_ref, dw_ref):
        is_norm, scale = _conv_kind()
        _, vjp = jax.vjp(lambda p: _qk_post(p, is_norm, scale), pre_ref[...])
        dpre = vjp(d0_ref[...] + d1_ref[...])[0]
        dw_ref[...] = jnp.zeros_like(dw_ref)
        dx_ref[...] = _conv_pre_bwd(x_ref[...], w_ref, dpre, _conv_masks(L, is_ctx), is_ctx, dw_ref)

    return pl.pallas_call(body, name=name, grid=(B, C // GDN_HEAD), in_specs=[spec, spec, wspec, spec, spec],
                          out_specs=[spec, dwspec], out_shape=[SDS((B, L, C), f32), SDS((B, CONV_ROWS, C), f32)],
                          compiler_params=_cparams(2))(qkv, pre, w16, da0, da1)


def _gates_fn(ba, alog, dtb):
    T = ba.shape[0]
    lane = lax.broadcasted_iota(jnp.int32, ba.shape, 1)
    ii = lax.broadcasted_iota(jnp.int32, (T, T), 0)
    jj = lax.broadcasted_iota(jnp.int32, (T, T), 1)
    same = jnp.right_shift(ii, 6) == jnp.right_shift(jj, 6)
    lmat = jnp.logical_and(same, ii >= jj).astype(f32)
    umat = jnp.logical_and(same, ii <= jj).astype(f32)
    g = jnp.where(lane >= 8, -jnp.exp(alog) * jax.nn.softplus(ba + dtb), 0.0)
    gc = jnp.where(lane >= 12, _dot_hi(umat, g), _dot_hi(lmat, g))
    return jnp.where(lane < 8, jax.nn.sigmoid(ba), gc)


def gates_fwd(ba, alog, dtb, *, name):
    B, L, _ = ba.shape
    T = min(TOK_TILE, L)
    t = _tok(T, N_GATE)

    def body(ba_ref, al_ref, dt_ref, o_ref):
        o_ref[...] = _gates_fn(ba_ref[...], al_ref[...], dt_ref[...])

    return pl.pallas_call(body, name=name, grid=(B, L // T),
                          in_specs=[t, _resident((1, N_GATE)), _resident((1, N_GATE))], out_specs=t,
                          out_shape=SDS((B, L, N_GATE), f32), compiler_params=_cparams(2))(ba, alog, dtb)


def gates_bwd(ba, alog, dtb, dbg, *, name):
    B, L, _ = ba.shape
    T = min(TOK_TILE, L)
    t = _tok(T, N_GATE)
    small = _resident((1, N_GATE))

    def body(ba_ref, al_ref, dt_ref, d_ref, dba_ref, dal_ref, ddt_ref):
        @pl.when(_first_step())
        def _():
            dal_ref[...] = jnp.zeros_like(dal_ref)
            ddt_ref[...] = jnp.zeros_like(ddt_ref)

        _, vjp = jax.vjp(_gates_fn, ba_ref[...], al_ref[...], dt_ref[...])
        dba, dal, ddt = vjp(d_ref[...])
        dba_ref[...] = dba
        dal_ref[...] += dal
        ddt_ref[...] += ddt

    return pl.pallas_call(body, name=name, grid=(B, L // T), in_specs=[t, small, small, t],
                          out_specs=[t, small, small],
                          out_shape=[SDS((B, L, N_GATE), f32), SDS((1, N_GATE), f32), SDS((1, N_GATE), f32)],
                          compiler_params=_cparams(2))(ba, alog, dtb, dbg)


@jax.custom_vjp
def _inv_unit_tri(mats):
    n = mats[0].shape[0]
    eye = (lax.broadcasted_iota(jnp.int32, (n, n), 0) == lax.broadcasted_iota(jnp.int32, (n, n), 1)).astype(f32)
    xs = [eye - a for a in mats]
    sq = [_dot(a, a) for a in mats]
    ps = sq
    k = 2
    while k < n:
        xs = [x + _dot(x, p) for x, p in zip(xs, ps)]
        k *= 2
        if k < n:
            ps = [_dot(p, p) for p in ps]
    return tuple(_dot(p, x) - a for p, x, a in zip(sq, xs, mats))


def _inv_unit_tri_fwd(mats):
    ns = _inv_unit_tri(mats)
    return ns, ns


def _inv_unit_tri_bwd(ns, dns):
    ys = [dn + _dot_tn(nn, dn) for nn, dn in zip(ns, dns)]
    return (tuple(-(y + _dot_nt(y, nn)) for y, nn in zip(ys, ns)),)


_inv_unit_tri.defvjp(_inv_unit_tri_fwd, _inv_unit_tri_bwd)


def _gdn_chunk(heads, *, revs):
    n = heads[0][0].shape[0]
    ii = lax.broadcasted_iota(jnp.int32, (n, n), 0)
    jj = lax.broadcasted_iota(jnp.int32, (n, n), 1)
    row = lax.broadcasted_iota(jnp.int32, (n, 1), 0)
    lower = {False: ii >= jj, True: ii <= jj}
    strict = {False: ii > jj, True: ii < jj}
    last = {False: n - 1, True: 0}
    H = range(len(heads))
    q, k, v, beta, gc, gr, s = (list(t) for t in zip(*heads))
    decay = [jnp.where(lower[revs[h]], jnp.exp(jnp.where(lower[revs[h]], gc[h] - gr[h], 0.0)), 0.0) for h in H]
    kk = [_mm_nt(k[h], k[h]) for h in H]
    qk = [_mm_nt(q[h], k[h]) * decay[h] for h in H]
    qs = [_mm(q[h], s[h]) for h in H]
    a_mat = tuple(jnp.where(strict[revs[h]], beta[h] * kk[h] * decay[h], 0.0) for h in H)
    gamma = [jnp.exp(gc[h]) for h in H]
    g_last = [jnp.sum(jnp.where(row == last[revs[h]], gc[h], 0.0), axis=0, keepdims=True) for h in H]
    nmat = _inv_unit_tri(a_mat)
    bv = [beta[h] * v[h] for h in H]
    bk = [(beta[h] * gamma[h]) * k[h] for h in H]
    u0 = [bv[h] + _mm(nmat[h], bv[h]) for h in H]
    w = [bk[h] + _mm(nmat[h], bk[h]) for h in H]
    k_out = [k[h] * jnp.exp(g_last[h] - gc[h]) for h in H]
    u = [u0[h] - _mm(w[h], s[h]) for h in H]
    o = [gamma[h] * qs[h] + _mm(qk[h], u[h]) for h in H]
    s_new = [jnp.exp(g_last[h]) * s[h] + _mm_tn(k_out[h], u[h]) for h in H]
    return tuple((o[h], s_new[h]) for h in H)


def _gdn_specs(B, nc, rev):
    def cidx(n):
        return (nc - 1 - n) if rev else n
    tok = lambda width: pl.BlockSpec((B, CHUNK, width), lambda n: (0, cidx(n), 0))
    rowspec = pl.BlockSpec((B, None, N_GATE, CHUNK), lambda n: (0, cidx(n), 0, 0))
    st = pl.BlockSpec((B, GDN_HEADS, GDN_HEAD, GDN_HEAD), lambda n: (0, 0, 0, 0))
    ck = pl.BlockSpec((B, None, GDN_HEADS, GDN_HEAD, GDN_HEAD), lambda n: (0, cidx(n), 0, 0, 0))
    return tok, rowspec, st, ck


def _gdn_head_args(qkv_ref, bg_ref, bgr_ref, b, d, h):
    col = d * GDN_HEADS + h
    q = qkv_ref[b, :, h * GDN_HEAD:(h + 1) * GDN_HEAD]
    k = qkv_ref[b, :, D_GDN + h * GDN_HEAD:D_GDN + (h + 1) * GDN_HEAD]
    v = qkv_ref[b, :, 2 * D_GDN + h * GDN_HEAD:2 * D_GDN + (h + 1) * GDN_HEAD]
    bgv = bg_ref[b]
    return q, k, v, bgv[:, col:col + 1], bgv[:, 8 + col:9 + col], bgr_ref[b][8 + col:9 + col, :]


def _gdn_chains(B):
    return [(d, b, h) for d in range(N_DIR) for b in range(B) for h in range(GDN_HEADS)]


def gdn_fwd(qkv, bg, bgr, s0s, *, need_o, name):
    B, L, _ = qkv.shape
    nc = L // CHUNK
    specs = [_gdn_specs(B, nc, d == 1) for d in range(N_DIR)]
    chains = _gdn_chains(B)
    state_shape = (B, GDN_HEADS, GDN_HEAD, GDN_HEAD)

    def body(*refs):
        ins = [refs[3 * d:3 * d + 3] for d in range(N_DIR)]
        s0_refs = refs[6:8]
        k = 8
        o_refs = refs[k:k + 2] if need_o else None
        k += 2 if need_o else 0
        ck_refs, sf_refs, s_scrs = refs[k:k + 2], refs[k + 2:k + 4], refs[k + 4:k + 6]
        n = pl.program_id(0)

        @pl.when(n == 0)
        def _():
            for d in range(N_DIR):
                s_scrs[d][...] = s0_refs[d][...]

        for d in range(N_DIR):
            ck_refs[d][...] = s_scrs[d][...]
        heads = tuple(_gdn_head_args(*ins[d], b, d, h) + (s_scrs[d][b, h],) for d, b, h in chains)
        outs = _gdn_chunk(heads, revs=tuple(d == 1 for d, _, _ in chains))
        for (d, b, h), (o, s_new) in zip(chains, outs):
            if need_o:
                o_refs[d][b, :, h * GDN_HEAD:(h + 1) * GDN_HEAD] = o
            s_scrs[d][b, h] = s_new

        @pl.when(n == nc - 1)
        def _():
            for d in range(N_DIR):
                sf_refs[d][...] = s_scrs[d][...]

    in_specs, out_o, out_ck, out_sf = [], [], [], []
    for tok, rowspec, st, ck in specs:
        in_specs += [tok(3 * D_GDN), tok(N_GATE), rowspec]
        out_o.append(tok(D_GDN))
        out_ck.append(ck)
        out_sf.append(st)
    in_specs += [specs[0][2]] * 2
    out_specs = (out_o if need_o else []) + out_ck + out_sf
    out_shape = (([SDS((B, L, D_GDN), f32)] * 2 if need_o else [])
                 + [SDS((B, nc) + state_shape[1:], f32)] * 2 + [SDS(state_shape, f32)] * 2)
    res = pl.pallas_call(
        body, name=name, grid=(nc,), in_specs=in_specs, out_specs=out_specs, out_shape=out_shape,
        scratch_shapes=[pltpu.VMEM(state_shape, f32)] * 2, compiler_params=_cparams(1),
    )(qkv, bg, bgr, qkv, bg, bgr, *s0s)
    if need_o:
        return res[0:2], res[2:4], res[4:6]
    return res[0:2], res[2:4]


def gdn_bwd(qkv, bg, bgr, cks, do, dsfs, *, name):
    B, L, _ = qkv.shape
    nc = L // CHUNK
    has_do = do is not None
    specs = [_gdn_specs(B, nc, d != 1) for d in range(N_DIR)]
    chains = _gdn_chains(B)
    state_shape = (B, GDN_HEADS, GDN_HEAD, GDN_HEAD)
    per_dir = 5 if has_do else 4

    def body(*refs):
        ins = [refs[per_dir * d:per_dir * d + per_dir] for d in range(N_DIR)]
        k = per_dir * N_DIR
        dsf_refs = refs[k:k + 2]
        outs = [refs[k + 2 + 3 * d:k + 5 + 3 * d] for d in range(N_DIR)]
        ds0_refs, ds_scrs = refs[k + 8:k + 10], refs[k + 10:k + 12]
        n = pl.program_id(0)

        @pl.when(n == 0)
        def _():
            for d in range(N_DIR):
                ds_scrs[d][...] = dsf_refs[d][...]

        lane = lax.broadcasted_iota(jnp.int32, (CHUNK, N_GATE), 1)
        sub = lax.broadcasted_iota(jnp.int32, (N_GATE, CHUNK), 0)
        heads = tuple(_gdn_head_args(*ins[d][:3], b, d, h) + (ins[d][3][b, h],) for d, b, h in chains)
        _, vjp = jax.vjp(functools.partial(_gdn_chunk, revs=tuple(d == 1 for d, _, _ in chains)), heads)
        zero = jnp.zeros((CHUNK, GDN_HEAD), f32)
        cts = tuple(((ins[d][4][b, :, h * GDN_HEAD:(h + 1) * GDN_HEAD] if has_do else zero), ds_scrs[d][b, h])
                    for d, b, h in chains)
        (dheads,) = vjp(cts)
        dbg_acc = [[jnp.zeros((CHUNK, N_GATE), f32) for _ in range(B)] for _ in range(N_DIR)]
        dbgr_acc = [[jnp.zeros((N_GATE, CHUNK), f32) for _ in range(B)] for _ in range(N_DIR)]
        for (d, b, h), (dq, dk, dv, db, dgc, dgr, ds) in zip(chains, dheads):
            col = d * GDN_HEADS + h
            dqkv_ref = outs[d][0]
            dqkv_ref[b, :, h * GDN_HEAD:(h + 1) * GDN_HEAD] = dq
            dqkv_ref[b, :, D_GDN + h * GDN_HEAD:D_GDN + (h + 1) * GDN_HEAD] = dk
            dqkv_ref[b, :, 2 * D_GDN + h * GDN_HEAD:2 * D_GDN + (h + 1) * GDN_HEAD] = dv
            dbg_acc[d][b] = dbg_acc[d][b] + jnp.where(lane == col, db, 0.0) + jnp.where(lane == 8 + col, dgc, 0.0)
            dbgr_acc[d][b] = dbgr_acc[d][b] + jnp.where(sub == 8 + col, dgr, 0.0)
            ds_scrs[d][b, h] = ds
        for d in range(N_DIR):
            for b in range(B):
                outs[d][1][b] = dbg_acc[d][b]
                outs[d][2][b] = dbgr_acc[d][b]

        @pl.when(n == nc - 1)
        def _():
            for d in range(N_DIR):
                ds0_refs[d][...] = ds_scrs[d][...]

    in_specs, args, out_specs, out_shape = [], [], [], []
    for d, (tok, rowspec, st, ck) in enumerate(specs):
        in_specs += [tok(3 * D_GDN), tok(N_GATE), rowspec, ck] + ([tok(D_GDN)] if has_do else [])
        args += [qkv, bg, bgr, cks[d]] + ([do] if has_do else [])
        out_specs += [tok(3 * D_GDN), tok(N_GATE), rowspec]
        out_shape += [SDS((B, L, 3 * D_GDN), f32), SDS((B, L, N_GATE), f32), SDS((B, nc, N_GATE, CHUNK), f32)]
    st = specs[0][2]
    in_specs += [st, st]
    args += list(dsfs)
    out_specs += [st, st]
    out_shape += [SDS(state_shape, f32)] * 2
    res = pl.pallas_call(
        body, name=name, grid=(nc,), in_specs=in_specs, out_specs=out_specs, out_shape=out_shape,
        scratch_shapes=[pltpu.VMEM(state_shape, f32)] * 2, compiler_params=_cparams(1),
    )(*args)
    return (res[0], res[3]), (res[1], res[4]), (res[2], res[5]), (res[6], res[7])


def _gnorm_fn(o0, o1, z, w):
    o = o0 + o1
    return o * lax.rsqrt(jnp.mean(o * o, axis=-1, keepdims=True) + NORM_EPS) * w * _silu(z)


def _head_loss(y, x, gate, lng, lnb, tgt):
    r = DEEPNORM_ALPHA * x + gate * y
    mu = jnp.mean(r, axis=-1, keepdims=True)
    rc = r - mu
    var = jnp.mean(rc * rc, axis=-1, keepdims=True)
    err = rc * lax.rsqrt(var + LN_EPS) * lng + lnb - tgt
    return (0.5 / D_MODEL) * jnp.sum(jnp.sum(err * err, axis=-1, keepdims=True), axis=0, keepdims=True)


def tail_fwd_bwd(u, y0, y1, z_s5, o0, o1, z_gdn, x, tgt, gate, lng, lnb, ws, wg, dsk, wglu, bglu, nw):
    B, L, _ = x.shape
    T = min(TOK_TILE, L)

    def body(u_ref, y0_ref, y1_ref, z_ref, o0_ref, o1_ref, zg_ref, x_ref, t_ref, gate_ref, lng_ref, lnb_ref, ws_ref,
             wg_ref, dsk_ref, wglu_ref, bglu_ref, nw_ref,
             loss_ref, du_ref, dys_ref, dz_ref, do_ref, dzg_ref, gx_ref, dws_ref, dwg_ref, dgate_ref, dlng_ref, dlnb_ref,
             ddsk_ref, dwglu_ref, dbglu_ref, dnw_ref):
        n = pl.program_id(1)

        @pl.when(_first_step())
        def _():
            for r in (dws_ref, dwg_ref, dlng_ref, dlnb_ref, ddsk_ref, dwglu_ref, dbglu_ref, dnw_ref):
                r[...] = jnp.zeros_like(r)

        @pl.when(n == 0)
        def _():
            loss_ref[...] = jnp.zeros_like(loss_ref)
            dgate_ref[...] = jnp.zeros_like(dgate_ref)

        s5o, glu_vjp = jax.vjp(_glu_fn, u_ref[...], y0_ref[...], y1_ref[...], z_ref[...], dsk_ref[...],
                               wglu_ref[...].astype(f32), bglu_ref[...])
        heads = []
        for h in range(GDN_HEADS):
            sl = slice(h * GDN_HEAD, (h + 1) * GDN_HEAD)
            heads.append(jax.vjp(_gnorm_fn, o0_ref[:, sl], o1_ref[:, sl], zg_ref[:, sl], nw_ref[...]))
        sv = s5o.astype(bf16)
        gv = jnp.concatenate([out for out, _ in heads], axis=1).astype(bf16)
        y = _dot(sv, ws_ref[...]) + _dot(gv, wg_ref[...])
        loss, vjp = jax.vjp(lambda *a: _head_loss(*a, t_ref[...]), y, x_ref[...], gate_ref[...], lng_ref[...],
                            lnb_ref[...])
        dy, dx, dgate, dlng, dlnb = vjp(jnp.ones((1, 1), f32))
        loss_ref[...] += jnp.broadcast_to(loss, loss_ref.shape)
        dyb = dy.astype(bf16)
        gx_ref[...] = dx
        dws_ref[...] += _dot_tn(sv, dyb)
        dwg_ref[...] += _dot_tn(gv, dyb)
        dgate_ref[...] += dgate
        dlng_ref[...] += dlng
        dlnb_ref[...] += dlnb
        du, dys, _, dz, ddsk, dwglu, dbglu = glu_vjp(_dot_nt(dyb, ws_ref[...]))
        du_ref[...], dys_ref[...], dz_ref[...] = du, dys, dz
        ddsk_ref[...] += ddsk
        dwglu_ref[...] += dwglu
        dbglu_ref[...] += dbglu
        dgdo = _dot_nt(dyb, wg_ref[...])
        for h, (_, hvjp) in enumerate(heads):
            sl = slice(h * GDN_HEAD, (h + 1) * GDN_HEAD)
            do, _, dzg, dnw = hvjp(dgdo[:, sl])
            do_ref[:, sl] = do
            dzg_ref[:, sl] = dzg
            dnw_ref[...] += dnw

    half, full = _tok(T, D_S5), _tok(T, D_MODEL)
    row = _resident((1, D_MODEL))
    wsp = _resident((D_S5, D_MODEL))
    r512, rglu, r128 = _resident((1, D_S5)), _resident((D_S5, D_S5)), _resident((1, GDN_HEAD))
    return pl.pallas_call(
        body, name="tail_fwd_bwd", grid=(B, L // T),
        in_specs=[half] * 7 + [full, full, _per_batch(1, D_MODEL), row, row, wsp, wsp, r512, rglu, r512, r128],
        out_specs=[_per_batch(8, LANES)] + [half] * 5 + [full, wsp, wsp, _per_batch(1, D_MODEL), row, row, r512, rglu, r512,
                                                           r128],
        out_shape=[SDS((B, 8, LANES), f32)] + [SDS((B, L, D_S5), f32)] * 5 + [
            SDS((B, L, D_MODEL), f32), SDS((D_S5, D_MODEL), f32), SDS((D_GDN, D_MODEL), f32), SDS((B, 1, D_MODEL), f32),
            SDS((1, D_MODEL), f32), SDS((1, D_MODEL), f32), SDS((1, D_S5), f32), SDS((D_S5, D_S5), f32), SDS((1, D_S5), f32),
            SDS((1, GDN_HEAD), f32)],
        compiler_params=_cparams(2),
    )(u, y0, y1, z_s5, o0, o1, z_gdn, x, tgt, gate, lng, lnb, ws, wg, dsk, wglu, bglu, nw)


def _adamw_math(w, g, m, v):
    nm = ADAM_B1 * m + (1.0 - ADAM_B1) * g
    nv = ADAM_B2 * v + (1.0 - ADAM_B2) * jnp.square(g)
    m_hat = nm / (1.0 - ADAM_B1 ** ADAM_STEP)
    v_hat = nv / (1.0 - ADAM_B2 ** ADAM_STEP)
    return -ADAM_LR * (m_hat / (jnp.sqrt(v_hat) + ADAM_EPS) + ADAM_WD * w), nm, nv


def _row_tile(rows, cap=512):
    for t in range(min(cap, rows), 15, -1):
        if rows % t == 0 and t % 16 == 0:
            return t
    return rows


def adamw_3d(w, g, m, v, *, lead=False, name):
    R, C = (w.shape[0], w.shape[2]) if lead else w.shape[1:]
    if lead:
        T = next(t for t in range(min(256, R), 0, -1) if R % t == 0)
        spec = pl.BlockSpec((T, 1, C), lambda i: (i, 0, 0))
    else:
        T = _row_tile(R)
        spec = pl.BlockSpec((None, T, C), lambda i: (0, i, 0))

    def body(w_ref, g_ref, m_ref, v_ref, d_ref, nm_ref, nv_ref):
        d_ref[...], nm_ref[...], nv_ref[...] = _adamw_math(w_ref[...], g_ref[...], m_ref[...], v_ref[...])

    return pl.pallas_call(body, name=name, grid=(R // T,), in_specs=[spec] * 4, out_specs=[spec] * 3,
                          out_shape=[SDS(w.shape, f32)] * 3, compiler_params=_cparams(1))(w, g, m, v)


def adamw_small(ws, gs, ms, vs):
    n = len(ws)

    def body(*refs):
        outs = refs[4 * n:]
        for i in range(n):
            d, nm, nv = _adamw_math(refs[i][...], refs[n + i][...], refs[2 * n + i][...], refs[3 * n + i][...])
            outs[i][...], outs[n + i][...], outs[2 * n + i][...] = d, nm, nv

    res = pl.pallas_call(body, name="adamw_small", out_shape=[SDS(w.shape, f32) for w in ws] * 3,
                         compiler_params=pltpu.CompilerParams(vmem_limit_bytes=VMEM_LIMIT))(*ws, *gs, *ms, *vs)
    return res[:n], res[n:2 * n], res[2 * n:]


def sum_cores(own, got, *, name):
    A, H, C = own.shape
    T = _row_tile(H)
    spec = pl.BlockSpec((None, T, C), lambda a, i: (a, i, 0))

    def body(a_ref, b_ref, q32_ref, q16_ref):
        q = a_ref[...] + b_ref[...]
        q32_ref[...] = q
        q16_ref[...] = q.astype(bf16)

    return pl.pallas_call(body, name=name, grid=(A, H // T), in_specs=[spec, spec], out_specs=[spec, spec],
                          out_shape=[SDS((A, H, C), f32), SDS((A, H, C), bf16)], compiler_params=_cparams(2))(own, got)


def sum_chips(mine, rec, cpos, *, name):
    H, C = mine.shape
    T = _row_tile(H)
    nt = H // T

    def body(c_ref, m_ref, r_ref, f_ref):
        f_ref[...] = ((m_ref[...] + r_ref[0].astype(f32)) + r_ref[1].astype(f32)) + r_ref[2].astype(f32)

    grid_spec = pltpu.PrefetchScalarGridSpec(
        num_scalar_prefetch=1, grid=(nt,),
        in_specs=[pl.BlockSpec((T, C), lambda i, c_ref: (i, 0)), pl.BlockSpec((3, T, C), lambda i, c_ref: (0, i, 0))],
        out_specs=pl.BlockSpec((None, T, C), lambda i, c_ref: (0, c_ref[0] * nt + i, 0)))
    return pl.pallas_call(body, name=name, grid_spec=grid_spec, out_shape=SDS((1, 2 * H, C), f32),
                          compiler_params=_cparams(1))(cpos.reshape(1).astype(jnp.int32), mine, rec)


CHIP_FLIPS = ((1, 0), (0, 1), (1, 1))


def _pos():
    return lax.axis_index("x"), lax.axis_index("y"), lax.axis_index("c")


def _comm_call(body, srcs, out_sds, n_remote, n_local, name):
    any_spec = pl.BlockSpec(memory_space=pl.ANY)
    return pl.pallas_call(
        body, name=name, in_specs=[any_spec] * len(srcs), out_specs=[any_spec] * len(out_sds), out_shape=out_sds,
        scratch_shapes=[pltpu.SemaphoreType.DMA((n_remote,)), pltpu.SemaphoreType.DMA((n_remote,)),
                        pltpu.SemaphoreType.DMA((max(n_local, 1),))],
        compiler_params=pltpu.CompilerParams(has_side_effects=True),
    )(*srcs)


def _remote(src, dst, send_sems, recv_sems, k, target):
    return pltpu.make_async_remote_copy(src, dst, send_sems.at[k], recv_sems.at[k], device_id=target,
                                        device_id_type=MESH)


def _half_rows(c, rows):
    half = rows // 2
    return pl.ds(pl.multiple_of(c * half, 8), half)


def gather_shards(shards):
    nt = len(shards)

    def body(*refs):
        srcs, outs = refs[:nt], refs[nt:2 * nt]
        send_sems, recv_sems, _ = refs[2 * nt:]
        x, y, c = _pos()
        j = 2 * x + y
        sib = (x, y, 1 - c)
        own = [_remote(srcs[t], outs[t].at[j], send_sems, recv_sems, 7 * t + 6, sib) for t in range(nt)]
        first, passed = [], []
        for k, (fx, fy) in enumerate(CHIP_FLIPS):
            tx, ty = x ^ fx, y ^ fy
            jk = 2 * tx + ty
            for t in range(nt):
                rows = _half_rows(c, srcs[t].shape[0])
                first.append(_remote(srcs[t].at[rows], outs[t].at[j, rows], send_sems, recv_sems, 7 * t + k, (tx, ty, c)))
                passed.append(_remote(outs[t].at[jk, rows], outs[t].at[jk, rows], send_sems, recv_sems, 7 * t + 3 + k, sib))
        for cp in first + own:
            cp.start()
        for a, b in zip(first, passed):
            a.wait_recv()
            b.start()
        for cp in passed + own:
            cp.wait_recv()
        for cp in first + passed + own:
            cp.wait_send()

    return _comm_call(body, shards, [SDS((4,) + s.shape, s.dtype) for s in shards], 7 * nt, 0, "gather_shards")


def swap_halves(ps):
    nt = len(ps)

    def body(*refs):
        srcs, outs = refs[:nt], refs[nt:2 * nt]
        send_sems, recv_sems, _ = refs[2 * nt:]
        x, y, c = _pos()
        cps = [_remote(srcs[t].at[a, _half_rows(1 - c, srcs[t].shape[1])], outs[t].at[a], send_sems, recv_sems, 4 * t + a,
                       (x, y, 1 - c)) for t in range(nt) for a in range(4)]
        for cp in cps:
            cp.start()
        for cp in cps:
            cp.wait()

    return _comm_call(body, ps, [SDS((4, p.shape[1] // 2, p.shape[2]), p.dtype) for p in ps], 4 * nt, 0, "swap_halves")


def scatter_to_chips(qs):
    nt = len(qs)

    def body(*refs):
        srcs, outs = refs[:nt], refs[nt:2 * nt]
        send_sems, recv_sems, _ = refs[2 * nt:]
        x, y, c = _pos()
        cps = []
        for k, (fx, fy) in enumerate(CHIP_FLIPS):
            tx, ty = x ^ fx, y ^ fy
            for t in range(nt):
                cps.append(_remote(srcs[t].at[2 * tx + ty], outs[t].at[k], send_sems, recv_sems, 3 * t + k, (tx, ty, c)))
        for cp in cps:
            cp.start()
        for cp in cps:
            cp.wait()

    return _comm_call(body, qs, [SDS((3,) + q.shape[1:], q.dtype) for q in qs], 3 * nt, 0, "scatter_to_chips")


def join_halves(fs):
    nt = len(fs)

    def body(*refs):
        outs = refs[nt:2 * nt]
        send_sems, recv_sems, _ = refs[2 * nt:]
        x, y, c = _pos()
        cps = []
        for t in range(nt):
            mine = outs[t].at[0, _half_rows(c, outs[t].shape[1])]
            cps.append(_remote(mine, mine, send_sems, recv_sems, t, (x, y, 1 - c)))
        for cp in cps:
            cp.start()
        for cp in cps:
            cp.wait()

    any_spec = pl.BlockSpec(memory_space=pl.ANY)
    return pl.pallas_call(
        body, name="join_halves", in_specs=[any_spec] * nt, out_specs=[any_spec] * nt,
        out_shape=[SDS(f.shape, f.dtype) for f in fs], input_output_aliases={t: t for t in range(nt)},
        scratch_shapes=[pltpu.SemaphoreType.DMA((nt,)), pltpu.SemaphoreType.DMA((nt,)), pltpu.SemaphoreType.DMA((1,))],
        compiler_params=pltpu.CompilerParams(has_side_effects=True),
    )(*fs)


DEV_FLIPS = tuple((fx, fy, fc) for fx in (0, 1) for fy in (0, 1) for fc in (0, 1))[1:]


def gather_devices(block, *, name):
    def body(src, out, send_sems, recv_sems, loc_sems):
        x, y, c = _pos()
        me = 4 * x + 2 * y + c
        mine = pltpu.make_async_copy(src, out.at[me], loc_sems.at[0])
        mine.start()
        cps = [_remote(src, out.at[me], send_sems, recv_sems, k, (x ^ fx, y ^ fy, c ^ fc))
               for k, (fx, fy, fc) in enumerate(DEV_FLIPS)]
        for cp in cps:
            cp.start()
        for cp in cps:
            cp.wait()
        mine.wait()

    return _comm_call(body, [block], [SDS((N_DEV,) + block.shape, block.dtype)], 7, 1, name)[0]


def exchange_devices(blocks, *, name):
    def body(src, out, send_sems, recv_sems, loc_sems):
        x, y, c = _pos()
        me = 4 * x + 2 * y + c
        mine = pltpu.make_async_copy(src.at[me], out.at[me], loc_sems.at[0])
        mine.start()
        cps = []
        for k, (fx, fy, fc) in enumerate(DEV_FLIPS):
            tx, ty, tc = x ^ fx, y ^ fy, c ^ fc
            cps.append(_remote(src.at[4 * tx + 2 * ty + tc], out.at[me], send_sems, recv_sems, k, (tx, ty, tc)))
        for cp in cps:
            cp.start()
        for cp in cps:
            cp.wait()
        mine.wait()

    return _comm_call(body, [blocks], [SDS(blocks.shape, blocks.dtype)], 7, 1, name)[0]


def gather_small(s):
    def body(src, out, send_sems, recv_sems, _):
        x, y, c = _pos()
        j = 2 * x + y
        cps = [_remote(src, out.at[j], send_sems, recv_sems, k, (x ^ fx, y ^ fy, c)) for k, (fx, fy) in enumerate(CHIP_FLIPS)]
        cps.append(_remote(src, out.at[j], send_sems, recv_sems, 3, (x, y, 1 - c)))
        for cp in cps:
            cp.start()
        for cp in cps:
            cp.wait()

    return _comm_call(body, [s], [SDS((4,) + s.shape, s.dtype)], 4, 0, "gather_small")[0]


SMALL_SHAPES = ((1, 2, 32, 64), (1, 2, 32, 64), (1, 2, 32), (1, 2, 32, 16, 64),
                (1, 2, 32, 16, 64), (1, 2, 32, 16, 64), (1, 2, 32, 16, 64), (1, D_S5), (1, D_S5), (1, 2, 4), (1, 2, 4),
                (1, GDN_HEAD), (1, D_MODEL), (1, D_MODEL))
SMALL_SWAPPED = (3, 4)


def _size(shape):
    return functools.reduce(lambda p, q: p * q, shape)


SMALL_ROWS = tuple(-(-_size(s) // LANES) for s in SMALL_SHAPES)
SMALL_TOTAL = 2176
SMALL_QUARTER = SMALL_TOTAL // 4


def _rows(a):
    flat = a.reshape(-1)
    pad = (-flat.shape[0]) % LANES
    if pad:
        flat = jnp.concatenate([flat, jnp.zeros((pad,), flat.dtype)])
    return flat.reshape(-1, LANES)


def _pack_small(parts):
    rows = [_rows(p) for p in parts]
    rows.append(jnp.zeros((SMALL_TOTAL - sum(SMALL_ROWS), LANES), f32))
    return jnp.concatenate(rows, axis=0)


def _unpack_small(buf):
    out, r = [], 0
    for s, n in zip(SMALL_SHAPES, SMALL_ROWS):
        out.append(buf[r:r + n].reshape(-1)[:_size(s)].reshape(s))
        r += n
    return out


def _as_2d(a):
    return a.reshape(1, -1) if a.ndim == 1 else a.reshape(-1, a.shape[-1])


S5_BG = S5_GROUPS // S5_BLOCKS


def _block_diag_in(bb):
    eye = jnp.eye(S5_BG, dtype=bb.dtype)
    b4 = bb.reshape(S5_BLOCKS, S5_BG, S5_GROUP, S5_STATE)
    return jnp.einsum('jgcp,gh->jgchp', b4, eye).reshape(S5_BLOCKS, S5_BC, S5_BS)


def _block_diag_in_t(d):
    d6 = d.reshape(S5_BLOCKS, S5_BG, S5_GROUP, S5_BG, S5_STATE)
    return jnp.einsum('jgcgp->jgcp', d6).reshape(S5_GROUPS, S5_GROUP * S5_STATE)


def _block_diag_out(cm):
    eye = jnp.eye(S5_BG, dtype=cm.dtype)
    c4 = cm.reshape(S5_BLOCKS, S5_BG, S5_GROUP, S5_STATE)
    return jnp.einsum('jgcp,gh->jhpgc', c4, eye).reshape(S5_BLOCKS, S5_BS, S5_BC)


def _block_diag_out_t(d):
    d6 = d.reshape(S5_BLOCKS, S5_BG, S5_STATE, S5_BG, S5_GROUP)
    return jnp.einsum('jgpgc->jgcp', d6).reshape(S5_GROUPS, S5_GROUP, S5_STATE)


def _to_chunk_rows(a):
    B, L, W = a.shape
    return a.reshape(B, L // CHUNK, CHUNK, W).transpose(0, 1, 3, 2)


def _from_chunk_rows(a):
    B, nc, W, _ = a.shape
    return a.transpose(0, 1, 3, 2).reshape(B, nc * CHUNK, W)


def local_step(x, ctx, tgt, m, w_in, lam_re, lam_im, log_dt, b_re, b_im, c_re, c_im, s5_d,
               w_glu, b_glu, conv16, a_log, dt_bias, norm_w, w_out, ln_g, ln_b):
    B, L, _ = x.shape
    zeros_state = jnp.zeros((B, GDN_HEADS, GDN_HEAD, GDN_HEAD), f32)

    shift, scale, gate = m[:B, :D_MODEL], m[:B, D_MODEL:2 * D_MODEL], m[:B, 2 * D_MODEL:]
    mod = jnp.stack([scale, shift], axis=1)
    mod_c = jnp.broadcast_to(jnp.stack([m[B, D_MODEL:2 * D_MODEL], m[B, :D_MODEL]], axis=0)[None], (B, 2, D_MODEL))

    u, z_s5, qkv, z_gdn, ba = in_proj_fwd(x, mod, w_in, name="in_proj_fwd")
    uc, _, qkvc, _, bac = in_proj_fwd(ctx, mod_c, w_in, name="in_proj_fwd_ctx")

    ng = N_DIR * S5_GROUPS
    zoh_in = (lam_re.reshape(ng, S5_STATE), lam_im.reshape(ng, S5_STATE), log_dt.reshape(ng, 1),
              b_re.reshape(ng, S5_GROUP * S5_STATE), b_im.reshape(ng, S5_GROUP * S5_STATE))
    expand = (jnp.arange(S5_GROUP * S5_STATE)[None, :] % S5_STATE == jnp.arange(S5_STATE)[:, None]).astype(f32)
    ar, ai, bbr, bbi = s5_zoh_fwd(*zoh_in, expand)
    bbr16, bbi16 = bbr.astype(bf16), bbi.astype(bf16)
    c_re16 = c_re.reshape(N_DIR, S5_GROUPS, S5_GROUP, S5_STATE).astype(bf16)
    c_im16 = (-c_im).reshape(N_DIR, S5_GROUPS, S5_GROUP, S5_STATE).astype(bf16)
    s5w, ys, hins, hins_c = [], [], [], []
    for d in range(N_DIR):
        g = slice(d * S5_GROUPS, (d + 1) * S5_GROUPS)
        wd = (_block_diag_in(bbr16[g]), _block_diag_in(bbi16[g]), _block_diag_out(c_re16[d]), _block_diag_out(c_im16[d]),
              jnp.stack([ar[g].reshape(-1), ai[g].reshape(-1)], axis=0))
        s5w.append(wd)
        hin_c, hend_c = s5_scan_fwd(uc, *wd, jnp.zeros((B, 2, S5_HALF), f32), d=d, need_y=False, name=f"s5_fwd_ctx{d}")
        y_d, hin, _ = s5_scan_fwd(u, *wd, hend_c, d=d, need_y=True, name=f"s5_fwd{d}")
        ys.append(y_d)
        hins.append(hin)
        hins_c.append(hin_c)
    glu_w = (s5_d.reshape(1, D_S5), w_glu, b_glu.reshape(1, D_S5))

    act, pre = conv_fwd(qkv, conv16, is_ctx=False, name="conv_fwd")
    act_c, pre_c = conv_fwd(qkvc, conv16, is_ctx=True, name="conv_fwd_ctx")
    pad8 = jnp.zeros((1, 8), f32)
    alog16 = jnp.concatenate([pad8, a_log.reshape(1, 8)], axis=1)
    dtb16 = jnp.concatenate([pad8, dt_bias.reshape(1, 8)], axis=1)
    bg = gates_fwd(ba, alog16, dtb16, name="gates_fwd")
    bg_c = gates_fwd(bac, alog16, dtb16, name="gates_fwd_ctx")
    bgr, bgr_c = _to_chunk_rows(bg), _to_chunk_rows(bg_c)
    cks_c, s_c = gdn_fwd(act_c, bg_c, bgr_c, (zeros_state, zeros_state), need_o=False, name="gdn_fwd_ctx")
    os_, cks, _ = gdn_fwd(act, bg, bgr, s_c, need_o=True, name="gdn_fwd")
    nw = norm_w.reshape(1, GDN_HEAD)

    (loss8, du_skip, dy, dz_s5, do, dz_gdn, gx_res, dws, dwg, dgate, dlng, dlnb, d_s5_d, d_w_glu, d_b_glu,
     d_norm_w) = tail_fwd_bwd(u, ys[0], ys[1], z_s5, os_[0], os_[1], z_gdn, x, tgt, gate[:, None, :],
                              ln_g.reshape(1, D_MODEL), ln_b.reshape(1, D_MODEL), w_out[:D_S5], w_out[D_S5:], *glu_w, nw)
    loss = jnp.sum(loss8[:, 0, 0])
    d_w_out = jnp.concatenate([dws, dwg], axis=0)

    dacts, dbgs, dbgrs, ds0s = gdn_bwd(act, bg, bgr, cks, do, (zeros_state, zeros_state), name="gdn_bwd")
    dacts_c, dbgs_c, dbgrs_c, _ = gdn_bwd(act_c, bg_c, bgr_c, cks_c, None, ds0s, name="gdn_bwd_ctx")
    dbg = dbgs[0] + dbgs[1] + _from_chunk_rows(dbgrs[0] + dbgrs[1])
    dbg_c = dbgs_c[0] + dbgs_c[1] + _from_chunk_rows(dbgrs_c[0] + dbgrs_c[1])
    dba, dal, ddt = gates_bwd(ba, alog16, dtb16, dbg, name="gates_bwd")
    dbac, dal_c, ddt_c = gates_bwd(bac, alog16, dtb16, dbg_c, name="gates_bwd_ctx")
    d_a_log = (dal + dal_c)[:, 8:].reshape(1, N_DIR, GDN_HEADS)
    d_dt_bias = (ddt + ddt_c)[:, 8:].reshape(1, N_DIR, GDN_HEADS)
    dqkv, dcw = conv_bwd(qkv, pre, conv16, dacts[0], dacts[1], is_ctx=False, name="conv_bwd")
    dqkvc, dcw_c = conv_bwd(qkvc, pre_c, conv16, dacts_c[0], dacts_c[1], is_ctx=True, name="conv_bwd_ctx")
    d_conv16 = jnp.sum(dcw, axis=0) + jnp.sum(dcw_c, axis=0)

    dus, ducs = [du_skip], []
    dar, dai, dbbr, dbbi, dcre, dcim = [], [], [], [], [], []
    for d in range(N_DIR):
        du_d, dbre1, dbim1, dct1, dcb1, da1, dh0 = s5_scan_bwd(u, dy, *s5w[d], hins[d],
                                                                jnp.zeros((B, 2, S5_HALF), f32), d=d, name=f"s5_bwd{d}")
        duc_d, dbre2, dbim2, _, _, da2, _ = s5_scan_bwd(uc, None, *s5w[d], hins_c[d], dh0, d=d, name=f"s5_bwd_ctx{d}")
        dus.append(du_d)
        ducs.append(duc_d)
        da = da1 + da2
        dar.append(da[0].reshape(S5_GROUPS, S5_STATE))
        dai.append(da[1].reshape(S5_GROUPS, S5_STATE))
        dbbr.append(_block_diag_in_t(dbre1 + dbre2))
        dbbi.append(_block_diag_in_t(dbim1 + dbim2))
        dcre.append(_block_diag_out_t(dct1))
        dcim.append(-_block_diag_out_t(dcb1))
    dlr, dli, dldt, dbre, dbim = s5_zoh_bwd(*zoh_in, expand, jnp.concatenate(dar, 0), jnp.concatenate(dai, 0),
                                            jnp.concatenate(dbbr, 0), jnp.concatenate(dbbi, 0))
    d_s5 = (dlr, dli, dldt, dbre, dbim, jnp.stack(dcre, 0), jnp.stack(dcim, 0))

    padg = lambda a: jnp.concatenate([a, jnp.zeros(a.shape[:2] + (LANES - N_GATE,), f32)], axis=2)
    zc = jnp.zeros_like(uc)
    dw_c, dmod_c = in_proj_bwd(ctx, mod_c, (tuple(ducs), zc, dqkvc, zc, padg(dbac)), w_in, None, None,
                               name="in_proj_bwd_ctx")
    d_w_in, dmod, grad_x = in_proj_bwd(x, mod, (tuple(dus), dz_s5, dqkv, dz_gdn, padg(dba)), w_in, gx_res, dw_c,
                                       name="in_proj_bwd")
    dmod_c = jnp.sum(dmod_c, axis=0)

    dm_rows = jnp.concatenate([dmod[:, 1], dmod[:, 0], dgate[:, 0]], axis=1)
    dm_ctx = jnp.concatenate([dmod_c[1], dmod_c[0], jnp.zeros((D_MODEL,), f32)])[None]
    dm = jnp.concatenate([dm_rows, dm_ctx], axis=0)
    small = (*d_s5, d_s5_d, d_b_glu, d_a_log, d_dt_bias, d_norm_w, dlng, dlnb)
    small = tuple(g.reshape(s) for g, s in zip(small, SMALL_SHAPES))
    return loss, grad_x, (d_w_in, d_w_out, d_w_glu, d_conv16), small, dm


SHARDED = (1, 3, 18, 12, 14)
REDUCED = (3, 18, 12, 14)
UNSHARDED = tuple(i for i in range(21) if i not in SHARDED)
SMALL = tuple(i for i in UNSHARDED if i not in (0, 2))
W_IN_SHARD = 772


def _conv_rows(w):
    return jnp.concatenate([w.reshape(9, w.shape[-1]), jnp.zeros((CONV_ROWS - 9, w.shape[-1]), f32)], axis=0)


def kernel(x, c, ctx, c_ctx, w_ada, b_ada, w_in, s5_lambda_re, s5_lambda_im, s5_log_dt, s5_b_re, s5_b_im, s5_c_re, s5_c_im, s5_d, w_glu, b_glu, conv_w, gdn_a_log, gdn_dt_bias, gdn_norm_w, w_out, ln_g, ln_b, loss_target, m_c_ctx, m_w_ada, m_b_ada, m_w_in, m_s5_lambda_re, m_s5_lambda_im, m_s5_log_dt, m_s5_b_re, m_s5_b_im, m_s5_c_re, m_s5_c_im, m_s5_d, m_w_glu, m_b_glu, m_conv_w, m_gdn_a_log, m_gdn_dt_bias, m_gdn_norm_w, m_w_out, m_ln_g, m_ln_b, v_c_ctx, v_w_ada, v_b_ada, v_w_in, v_s5_lambda_re, v_s5_lambda_im, v_s5_log_dt, v_s5_b_re, v_s5_b_im, v_s5_c_re, v_s5_c_im, v_s5_d, v_w_glu, v_b_glu, v_conv_w, v_gdn_a_log, v_gdn_dt_bias, v_gdn_norm_w, v_w_out, v_ln_g, v_ln_b):
    weights = [c_ctx, w_ada, b_ada, w_in, s5_lambda_re, s5_lambda_im, s5_log_dt, s5_b_re, s5_b_im, s5_c_re, s5_c_im,
               s5_d, w_glu, b_glu, conv_w, gdn_a_log, gdn_dt_bias, gdn_norm_w, w_out, ln_g, ln_b]
    ms = [m_c_ctx, m_w_ada, m_b_ada, m_w_in, m_s5_lambda_re, m_s5_lambda_im, m_s5_log_dt, m_s5_b_re, m_s5_b_im,
          m_s5_c_re, m_s5_c_im, m_s5_d, m_w_glu, m_b_glu, m_conv_w, m_gdn_a_log, m_gdn_dt_bias, m_gdn_norm_w, m_w_out,
          m_ln_g, m_ln_b]
    vs = [v_c_ctx, v_w_ada, v_b_ada, v_w_in, v_s5_lambda_re, v_s5_lambda_im, v_s5_log_dt, v_s5_b_re, v_s5_b_im,
          v_s5_c_re, v_s5_c_im, v_s5_d, v_w_glu, v_b_glu, v_conv_w, v_gdn_a_log, v_gdn_dt_bias, v_gdn_norm_w, v_w_out,
          v_ln_g, v_ln_b]
    cpos = lax.axis_index("c")
    jchip = 2 * lax.axis_index("x") + lax.axis_index("y")

    c_all = gather_devices(c, name="gather_c")
    cc = jnp.concatenate([c_all, jnp.broadcast_to(c_ctx[None, None, :], (N_DEV, 1, D_MODEL)),
                          jnp.zeros((N_DEV, 5, D_MODEL), f32)], axis=1)
    w_ada16 = w_ada[0].astype(bf16)
    b_cols = lax.dynamic_slice_in_dim(b_ada, jchip * ADA_SHARD, ADA_SHARD, axis=1)
    m_mine = exchange_devices(ada_fwd(cc, w_ada16, b_cols), name="exchange_m")
    m_rows = jnp.concatenate([m_mine[2 * j, :3] for j in range(4)], axis=1)

    conv_shard = _conv_rows(conv_w)
    g_in, g_out, g_glu, g_conv = gather_shards(
        [w_in[0].astype(bf16), w_out[0].astype(bf16), w_glu[0].astype(bf16), conv_shard])
    w_in_pad = jnp.concatenate([g_in[0], g_in[1], g_in[2], g_in[3], jnp.zeros((D_MODEL, IN_PAD - P_IN), bf16)], axis=1)
    conv16 = g_conv.transpose(1, 0, 2).reshape(CONV_ROWS, 3 * D_GDN)

    swap = lambda a: jnp.swapaxes(a, 3, 4)
    loss, grad_x, big, small, dm_rows = local_step(
        x, ctx, loss_target, m_rows, w_in_pad, s5_lambda_re, s5_lambda_im, s5_log_dt, swap(s5_b_re), swap(s5_b_im),
        s5_c_re, s5_c_im, s5_d, g_glu.reshape(D_S5, D_S5), b_glu, conv16, gdn_a_log, gdn_dt_bias, gdn_norm_w,
        g_out.reshape(D_MODEL, D_MODEL), ln_g, ln_b)
    loss = lax.psum(loss, ("x", "y", "c"))

    dm8 = jnp.concatenate([dm_rows, jnp.zeros((5, 3 * D_MODEL), f32)], axis=0)
    dm_by_chip = dm8.reshape(8, 4, ADA_SHARD).transpose(1, 0, 2)
    dm_cols = exchange_devices(jnp.repeat(dm_by_chip, 2, axis=0), name="exchange_dm")
    g_w_ada, pb = ada_bwd(cc, w_ada16, dm_cols)
    pb_all = gather_devices(pb, name="gather_p")
    g_c_ctx = c_ctx_bwd(pb_all, c_ctx[None, :])[0]
    g_b_ada = jnp.concatenate([pb_all[2 * j, 1:2, :ADA_SHARD] for j in range(4)], axis=1)

    d_w_in, d_w_out, d_w_glu, d_conv16 = big
    slabs = [d_w_in[:, :P_IN].reshape(D_MODEL, 4, W_IN_SHARD).transpose(1, 0, 2),
             d_w_out.reshape(4, D_MODEL // 4, D_MODEL),
             d_w_glu.reshape(4, D_S5 // 4, D_S5),
             d_conv16.reshape(CONV_ROWS, 4, 3 * D_GDN // 4).transpose(1, 0, 2),
             _pack_small(small).reshape(4, SMALL_QUARTER, LANES)]
    got = swap_halves(slabs)
    q32, q16 = [], []
    for t, (s, g) in enumerate(zip(slabs, got)):
        own = lax.dynamic_index_in_dim(s.reshape(4, 2, s.shape[1] // 2, s.shape[2]), cpos, axis=1, keepdims=False)
        a, b = sum_cores(own, g, name=f"sum_cores{t}")
        q32.append(a)
        q16.append(b)
    rec = scatter_to_chips(q16)
    fs = [sum_chips(lax.dynamic_index_in_dim(q, jchip, axis=0, keepdims=False), r, cpos, name=f"sum_chips{t}")
          for t, (q, r) in enumerate(zip(q32, rec))]
    red = join_halves(fs)
    g_small = _unpack_small(gather_small(red[4][0]).reshape(SMALL_TOTAL, LANES))
    g_shard = {1: g_w_ada, 3: red[0], 18: red[1], 12: red[2], 14: red[3]}

    grads, deltas, new_m, new_v = [None] * 21, [None] * 21, [None] * 21, [None] * 21
    for t, i in enumerate(SHARDED):
        conv, win = i == 14, i == 3
        prep = (lambda a: _conv_rows(a)[None]) if conv else ((lambda a: jnp.transpose(a, (2, 0, 1))) if win else (lambda a: a))
        g = jnp.transpose(g_shard[i], (2, 0, 1)) if win else g_shard[i]
        d, nm, nv = adamw_3d(prep(weights[i]), g, prep(ms[i]), prep(vs[i]), lead=win, name=f"adamw{t}")
        for lst, val in ((grads, g), (deltas, d), (new_m, nm), (new_v, nv)):
            lst[i] = (val[0, :9].reshape(weights[i].shape) if conv else (jnp.transpose(val, (1, 2, 0)) if win else val))
    g_un = {0: g_c_ctx, 2: g_b_ada, **{i: g_small[n] for n, i in enumerate(SMALL)}}
    swapped = [SMALL[n] for n in SMALL_SWAPPED]
    small_in = lambda lst: [_as_2d(swap(lst[i]) if i in swapped else lst[i]) for i in UNSHARDED]
    sm = adamw_small(small_in(weights), [_as_2d(g_un[i]) for i in UNSHARDED], small_in(ms), small_in(vs))
    for n, i in enumerate(UNSHARDED):
        back = ((lambda a: swap(a.reshape(swap(weights[i]).shape))) if i in swapped
                else (lambda a: a.reshape(weights[i].shape)))
        grads[i] = back(g_un[i])
        for lst, res in ((deltas, sm[0]), (new_m, sm[1]), (new_v, sm[2])):
            lst[i] = back(res[n])
    return (loss, grad_x, *grads, *deltas, *new_m, *new_v)
```

```python
import functools

import jax
import jax.numpy as jnp
from jax import lax
from jax.experimental import pallas as pl
from jax.experimental.pallas import tpu as pltpu

f32 = jnp.float32
bf16 = jnp.bfloat16
SDS = jax.ShapeDtypeStruct

D_MODEL = 1024
D_S5 = 512
S5_GROUP = 16
S5_GROUPS = 32
S5_STATE = 64
S5_HALF = S5_GROUPS * S5_STATE
D_GDN = 512
GDN_HEAD = 128
GDN_HEADS = 4
CHUNK = 64
GRID_W = 64
N_DIR = 2
P_IN = 3088
DEEPNORM_ALPHA = 2.0 ** 0.25
LN_EPS = 1e-5
NORM_EPS = 1e-6
ADAM_LR, ADAM_B1, ADAM_B2, ADAM_EPS, ADAM_WD, ADAM_STEP = 0.001, 0.9, 0.999, 1e-08, 0.01, 10

LANES = 128
VMEM_LIMIT = 56 * 1024 * 1024
TOK_TILE = 256
S5_TILE = 256
MESH = pl.DeviceIdType.MESH


def _cparams(n_grid):
    return pltpu.CompilerParams(dimension_semantics=("arbitrary",) * n_grid, vmem_limit_bytes=VMEM_LIMIT)


def _dot(a, b):
    return jnp.dot(a.astype(bf16), b.astype(bf16), preferred_element_type=f32)


def _dot_nt(a, b):
    return lax.dot_general(a.astype(bf16), b.astype(bf16), (((1,), (1,)), ((), ())), preferred_element_type=f32)


def _dot_tn(a, b):
    return lax.dot_general(a.astype(bf16), b.astype(bf16), (((0,), (0,)), ((), ())), preferred_element_type=f32)


def _dot_hi(a, b):
    return jnp.dot(a, b, precision=lax.Precision.HIGHEST, preferred_element_type=f32)


def _dot_h3(a, b):
    return jnp.dot(a, b, precision=lax.Precision.HIGH, preferred_element_type=f32)


@jax.custom_vjp
def _mm(a, b):
    return _dot(a, b)


@jax.custom_vjp
def _mm_nt(a, b):
    return _dot_nt(a, b)


@jax.custom_vjp
def _mm_tn(a, b):
    return _dot_tn(a, b)


_mm.defvjp(lambda a, b: (_dot(a, b), (a, b)), lambda r, g: (_mm_nt(g, r[1]), _mm_tn(r[0], g)))
_mm_nt.defvjp(lambda a, b: (_dot_nt(a, b), (a, b)), lambda r, g: (_mm(g, r[1]), _mm_tn(g, r[0])))
_mm_tn.defvjp(lambda a, b: (_dot_tn(a, b), (a, b)), lambda r, g: (_mm_nt(r[1], g), _mm(r[0], g)))


def _silu(x):
    return x * jax.nn.sigmoid(x)


def _gelu(x):
    return 0.5 * x * (1.0 + lax.erf(x * (2.0 ** -0.5)))


def _resident(shape):
    nd = len(shape)
    return pl.BlockSpec(shape, lambda *_: (0,) * nd, pipeline_mode=pl.Buffered(1))


def _tok(tile, width, nt=None, rev=False):
    if rev:
        return pl.BlockSpec((None, tile, width), lambda b, n: (b, nt - 1 - n, 0))
    return pl.BlockSpec((None, tile, width), lambda b, n: (b, n, 0))


def _per_batch(rows, width):
    return pl.BlockSpec((None, rows, width), lambda b, n: (b, 0, 0))


def _first_step():
    return jnp.logical_and(pl.program_id(0) == 0, pl.program_id(1) == 0)


ADA_SHARD = 3 * D_MODEL // 4
N_DEV = 8


def ada_fwd(cc, w, b):
    def body(cc_ref, w_ref, b_ref, m_ref):
        for k in range(N_DEV):
            m_ref[k] = _dot(_silu(cc_ref[k]), w_ref[...]) + b_ref[...]

    return pl.pallas_call(body, name="ada_fwd", out_shape=SDS((N_DEV, 8, ADA_SHARD), f32),
                          compiler_params=pltpu.CompilerParams(vmem_limit_bytes=VMEM_LIMIT))(cc, w, b)


def ada_bwd(cc, w, dmj):
    def body(cc_ref, w_ref, dmj_ref, dw_ref, pb_ref):
        dw = jnp.zeros((D_MODEL, ADA_SHARD), f32)
        p = jnp.zeros((8, D_MODEL), f32)
        db = jnp.zeros((1, ADA_SHARD), f32)
        for k in range(N_DEV):
            dw = dw + _dot_tn(_silu(cc_ref[k]), dmj_ref[k])
            p = p + _dot_nt(dmj_ref[k], w_ref[...])
            db = db + jnp.sum(dmj_ref[k], axis=0, keepdims=True)
        dw_ref[0] = dw
        pb_ref[...] = jnp.zeros_like(pb_ref)
        pb_ref[0:1, :] = p[2:3, :]
        pb_ref[1:2, 0:ADA_SHARD] = db

    return pl.pallas_call(
        body, name="ada_bwd", out_shape=[SDS((1, D_MODEL, ADA_SHARD), f32), SDS((8, D_MODEL), f32)],
        compiler_params=pltpu.CompilerParams(vmem_limit_bytes=VMEM_LIMIT))(cc, w, dmj)


def c_ctx_bwd(pb_all, c_ctx):
    def body(p_ref, c_ref, d_ref):
        ds = ((p_ref[0, 0:1, :] + p_ref[2, 0:1, :]) + p_ref[4, 0:1, :]) + p_ref[6, 0:1, :]
        _, vjp = jax.vjp(_silu, c_ref[...])
        d_ref[...] = vjp(ds)[0]

    return pl.pallas_call(body, name="c_ctx_bwd", out_shape=SDS((1, D_MODEL), f32))(pb_all, c_ctx)


N_GATE = 2 * N_DIR * GDN_HEADS
IN_WIDTHS = (D_S5, D_S5, 3 * D_GDN, D_GDN, LANES)
IN_OFFS = (0, 512, 1024, 2560, 3072)
IN_PAD = 3200


def in_proj_fwd(x, mod, w, *, name):
    B, L, _ = x.shape
    T = min(TOK_TILE, L)

    def body(x_ref, mod_ref, w_ref, *o_refs):
        h = (x_ref[...] * (1.0 + mod_ref[0:1, :]) + mod_ref[1:2, :]).astype(bf16)
        for o_ref, off, wd in zip(o_refs, IN_OFFS, IN_WIDTHS):
            r = _dot(h, w_ref[:, off:off + wd])
            o_ref[...] = r[:, :o_ref.shape[-1]]

    outw = (D_S5, D_S5, 3 * D_GDN, D_GDN, N_GATE)
    return pl.pallas_call(
        body, name=name, grid=(B, L // T),
        in_specs=[_tok(T, D_MODEL), _per_batch(2, D_MODEL), _resident((D_MODEL, IN_PAD))],
        out_specs=[_tok(T, wd) for wd in outw],
        out_shape=[SDS((B, L, wd), f32) for wd in outw],
        compiler_params=_cparams(2),
    )(x, mod, w)


def in_proj_bwd(x, mod, ds, w, gx_res, dw_start, *, name):
    B, L, _ = x.shape
    T = min(TOK_TILE, L)
    with_dx = gx_res is not None
    with_start = dw_start is not None
    n_u = len(ds[0])

    def body(*refs):
        x_ref, mod_ref = refs[0], refs[1]
        du_refs = refs[2:2 + n_u]
        d_refs = refs[2 + n_u:6 + n_u]
        w_ref = refs[6 + n_u]
        k = 7 + n_u
        if with_dx:
            gx_ref = refs[k]
            k += 1
        if with_start:
            start_ref = refs[k]
            k += 1
        dw_ref, dmod_ref = refs[k], refs[k + 1]
        if with_dx:
            dx_ref = refs[k + 2]
        n = pl.program_id(1)

        @pl.when(_first_step())
        def _():
            dw_ref[...] = start_ref[...] if with_start else jnp.zeros_like(dw_ref)

        @pl.when(n == 0)
        def _():
            dmod_ref[...] = jnp.zeros_like(dmod_ref)

        xv = x_ref[...]
        scale1 = 1.0 + mod_ref[0:1, :]
        h = (xv * scale1 + mod_ref[1:2, :]).astype(bf16)
        du = du_refs[0][...]
        for r in du_refs[1:]:
            du = du + r[...]
        dh = jnp.zeros((T, D_MODEL), f32)
        for dv, off, wd in zip([du] + [r[...] for r in d_refs], IN_OFFS, IN_WIDTHS):
            dv = dv.astype(bf16)
            dh = dh + _dot_nt(dv, w_ref[:, off:off + wd])
            dw_ref[:, off:off + wd] += _dot_tn(h, dv)
        dmod_ref[0:1, :] += jnp.sum(dh * xv, axis=0, keepdims=True)
        dmod_ref[1:2, :] += jnp.sum(dh, axis=0, keepdims=True)
        if with_dx:
            dx_ref[...] = gx_ref[...] + dh * scale1

    in_specs = ([_tok(T, D_MODEL), _per_batch(2, D_MODEL)] + [_tok(T, D_S5)] * n_u + [_tok(T, wd) for wd in IN_WIDTHS[1:]]
                + [_resident((D_MODEL, IN_PAD))])
    args = [x, mod, *ds[0], *ds[1:], w]
    out_specs = [_resident((D_MODEL, IN_PAD)), _per_batch(2, D_MODEL)]
    out_shape = [SDS((D_MODEL, IN_PAD), f32), SDS((B, 2, D_MODEL), f32)]
    if with_dx:
        in_specs.append(_tok(T, D_MODEL))
        args.append(gx_res)
        out_specs.append(_tok(T, D_MODEL))
        out_shape.append(SDS((B, L, D_MODEL), f32))
    if with_start:
        in_specs.append(_resident((D_MODEL, IN_PAD)))
        args.append(dw_start)
    return pl.pallas_call(body, name=name, grid=(B, L // T), in_specs=in_specs, out_specs=out_specs,
                          out_shape=out_shape, compiler_params=_cparams(2))(*args)


def _s5_zoh(lr, li, ldt, bre, bim, expand):
    dt = jnp.exp(ldt)
    zr, zi = lr * dt, li * dt
    e = jnp.exp(zr)
    ar, ai = e * jnp.cos(zi), e * jnp.sin(zi)
    den = lr * lr + li * li
    czr = ((ar - 1.0) * lr + ai * li) / den
    czi = (ai * lr - (ar - 1.0) * li) / den
    czr_e, czi_e = _dot_hi(czr, expand), _dot_hi(czi, expand)
    return ar, ai, czr_e * bre - czi_e * bim, czr_e * bim + czi_e * bre


_ZOH_OUT = [(N_DIR * S5_GROUPS, S5_STATE)] * 2 + [(N_DIR * S5_GROUPS, S5_STATE * S5_GROUP)] * 2


def s5_zoh_fwd(lr, li, ldt, bre, bim, expand):
    def body(lr_ref, li_ref, ldt_ref, bre_ref, bim_ref, e_ref, ar_ref, ai_ref, bbr_ref, bbi_ref):
        ar, ai, bbr, bbi = _s5_zoh(lr_ref[...], li_ref[...], ldt_ref[...], bre_ref[...], bim_ref[...], e_ref[...])
        ar_ref[...], ai_ref[...], bbr_ref[...], bbi_ref[...] = ar, ai, bbr, bbi

    return pl.pallas_call(body, name="s5_zoh_fwd", out_shape=[SDS(s, f32) for s in _ZOH_OUT])(
        lr, li, ldt, bre, bim, expand)


def s5_zoh_bwd(lr, li, ldt, bre, bim, expand, dar, dai, dbbr, dbbi):
    def body(lr_ref, li_ref, ldt_ref, bre_ref, bim_ref, e_ref, dar_ref, dai_ref, dbbr_ref, dbbi_ref,
             dlr_ref, dli_ref, dldt_ref, dbre_ref, dbim_ref):
        ev = e_ref[...]
        _, vjp = jax.vjp(lambda a, b, c, d, e: _s5_zoh(a, b, c, d, e, ev),
                         lr_ref[...], li_ref[...], ldt_ref[...], bre_ref[...], bim_ref[...])
        outs = vjp((dar_ref[...], dai_ref[...], dbbr_ref[...], dbbi_ref[...]))
        dlr_ref[...], dli_ref[...], dldt_ref[...], dbre_ref[...], dbim_ref[...] = outs

    shapes = [lr.shape, li.shape, ldt.shape, bre.shape, bim.shape]
    return pl.pallas_call(body, name="s5_zoh_bwd", out_shape=[SDS(s, f32) for s in shapes])(
        lr, li, ldt, bre, bim, expand, dar, dai, dbbr, dbbi)


def _scan_rows(T, rev, ar, ai, h0s, refs, off):
    def step(i, carry):
        t = off + ((T - 1 - i) if rev else i)
        out = []
        for (hr, hi), (r_ref, i_ref) in zip(carry, refs):
            nr = ar * hr - ai * hi + r_ref[pl.ds(t, 1), :]
            ni = ar * hi + ai * hr + i_ref[pl.ds(t, 1), :]
            r_ref[pl.ds(t, 1), :] = nr
            i_ref[pl.ds(t, 1), :] = ni
            out.append((nr, ni))
        return tuple(out)

    return lax.fori_loop(0, T, step, tuple(h0s))


S5_BLOCKS = 4
S5_BC = D_S5 // S5_BLOCKS
S5_BS = S5_HALF // S5_BLOCKS


def _s5_in(uv, bre_ref, bim_ref, hr_ref, hi_ref, off, T):
    for jb in range(S5_BLOCKS):
        uj = uv[:, jb * S5_BC:(jb + 1) * S5_BC]
        hr_ref[off:off + T, jb * S5_BS:(jb + 1) * S5_BS] = _dot(uj, bre_ref[jb])
        hi_ref[off:off + T, jb * S5_BS:(jb + 1) * S5_BS] = _dot(uj, bim_ref[jb])


def _s5_specs(B, T, nt, rev):
    tidx = (lambda n: nt - 1 - n) if rev else (lambda n: n)
    tok = pl.BlockSpec((B, T, D_S5), lambda n: (0, tidx(n), 0))
    hin = pl.BlockSpec((B, None, 2, S5_HALF), lambda n: (0, tidx(n), 0, 0))
    state = pl.BlockSpec((B, 2, S5_HALF), lambda n: (0, 0, 0))
    return tok, hin, state


def s5_scan_fwd(u, bre, bim, ctop, cbot, arow, h0, *, d, need_y, name):
    B, L, _ = u.shape
    T = min(S5_TILE, L)
    nt = L // T
    rev = d == 1

    def body(u_ref, bre_ref, bim_ref, ct_ref, cb_ref, a_ref, h0_ref, *rest):
        if need_y:
            y_ref, hin_ref, hend_ref, hr_scr, hi_scr, h_scr = rest
        else:
            hin_ref, hend_ref, hr_scr, hi_scr, h_scr = rest
        n = pl.program_id(0)

        @pl.when(n == 0)
        def _():
            h_scr[...] = h0_ref[...]

        hin_ref[...] = h_scr[...]
        for b in range(B):
            _s5_in(u_ref[b].astype(bf16), bre_ref, bim_ref, hr_scr.at[b], hi_scr.at[b], 0, T)
        hs = _scan_rows(T, rev, a_ref[0:1, :], a_ref[1:2, :], [(h_scr[b, 0:1, :], h_scr[b, 1:2, :]) for b in range(B)],
                        [(hr_scr.at[b], hi_scr.at[b]) for b in range(B)], 0)
        for b in range(B):
            h_scr[b, 0:1, :] = hs[b][0]
            h_scr[b, 1:2, :] = hs[b][1]
            if need_y:
                for jb in range(S5_BLOCKS):
                    st = slice(jb * S5_BS, (jb + 1) * S5_BS)
                    y_ref[b, :, jb * S5_BC:(jb + 1) * S5_BC] = (_dot(hr_scr[b, :, st], ct_ref[jb])
                                                                 + _dot(hi_scr[b, :, st], cb_ref[jb]))

        @pl.when(n == nt - 1)
        def _():
            hend_ref[...] = h_scr[...]

    tok, hin_spec, state = _s5_specs(B, T, nt, rev)
    out_specs = [hin_spec, state]
    out_shape = [SDS((B, nt, 2, S5_HALF), f32), SDS((B, 2, S5_HALF), f32)]
    if need_y:
        out_specs.insert(0, tok)
        out_shape.insert(0, SDS((B, L, D_S5), f32))
    w_in, w_out = _resident((S5_BLOCKS, S5_BC, S5_BS)), _resident((S5_BLOCKS, S5_BS, S5_BC))
    return pl.pallas_call(
        body, name=name, grid=(nt,),
        in_specs=[tok, w_in, w_in, w_out, w_out, _resident((2, S5_HALF)), state],
        out_specs=out_specs, out_shape=out_shape,
        scratch_shapes=[pltpu.VMEM((B, T, S5_HALF), f32), pltpu.VMEM((B, T, S5_HALF), f32),
                        pltpu.VMEM((B, 2, S5_HALF), f32)],
        compiler_params=_cparams(1),
    )(u, bre, bim, ctop, cbot, arow, h0)


def s5_scan_bwd(u, dy, bre, bim, ctop, cbot, arow, hin, dhend, *, d, name):
    B, L, _ = u.shape
    T = min(S5_TILE, L)
    nt = L // T
    rev = d == 1
    has_dy = dy is not None
    PAD = 8

    def body(*refs):
        u_ref = refs[0]
        k = 1
        if has_dy:
            dy_ref = refs[1]
            k = 2
        bre_ref, bim_ref, ct_ref, cb_ref, a_ref, hin_ref, dhend_ref = refs[k:k + 7]
        du_ref, dbre_ref, dbim_ref, dct_ref, dcb_ref, da_ref, dh0_ref = refs[k + 7:k + 14]
        hr_scr, hi_scr, gr_scr, gi_scr, p_scr = refs[k + 14:]
        n = pl.program_id(0)

        @pl.when(n == 0)
        def _():
            for r in (dbre_ref, dbim_ref, dct_ref, dcb_ref, da_ref):
                r[...] = jnp.zeros_like(r)
            p_scr[...] = dhend_ref[...]

        ar, ai = a_ref[0:1, :], a_ref[1:2, :]
        prev_row = PAD + T if rev else PAD - 1
        uvs = []
        for b in range(B):
            uvs.append(u_ref[b].astype(bf16))
            _s5_in(uvs[b], bre_ref, bim_ref, hr_scr.at[b], hi_scr.at[b], PAD, T)
            hr_scr[b, prev_row:prev_row + 1, :] = hin_ref[b, 0:1, :]
            hi_scr[b, prev_row:prev_row + 1, :] = hin_ref[b, 1:2, :]
        _scan_rows(T, rev, ar, ai, [(hin_ref[b, 0:1, :], hin_ref[b, 1:2, :]) for b in range(B)],
                   [(hr_scr.at[b], hi_scr.at[b]) for b in range(B)], PAD)
        if has_dy:
            for b in range(B):
                dyv = dy_ref[b].astype(bf16)
                for jb in range(S5_BLOCKS):
                    st = slice(jb * S5_BS, (jb + 1) * S5_BS)
                    dyj = dyv[:, jb * S5_BC:(jb + 1) * S5_BC]
                    gr_scr[b, :, st] = _dot_nt(dyj, ct_ref[jb])
                    gi_scr[b, :, st] = _dot_nt(dyj, cb_ref[jb])
                    dct_ref[jb] += _dot_tn(hr_scr[b, PAD:PAD + T, st], dyj)
                    dcb_ref[jb] += _dot_tn(hi_scr[b, PAD:PAD + T, st], dyj)
        else:
            gr_scr[...] = jnp.zeros_like(gr_scr)
            gi_scr[...] = jnp.zeros_like(gi_scr)

        def step(i, carry):
            t = i if rev else T - 1 - i
            tp = PAD + t + (1 if rev else -1)
            out = []
            for b, (pr, pi, dar, dai) in enumerate(carry):
                gr = gr_scr[b, pl.ds(t, 1), :] + pr
                gi = gi_scr[b, pl.ds(t, 1), :] + pi
                gr_scr[b, pl.ds(t, 1), :] = gr
                gi_scr[b, pl.ds(t, 1), :] = gi
                hpr = hr_scr[b, pl.ds(tp, 1), :]
                hpi = hi_scr[b, pl.ds(tp, 1), :]
                out.append((ar * gr + ai * gi, ar * gi - ai * gr, dar + hpr * gr + hpi * gi, dai + hpr * gi - hpi * gr))
            return tuple(out)

        zero = jnp.zeros((1, S5_HALF), f32)
        res = lax.fori_loop(0, T, step, tuple((p_scr[b, 0:1, :], p_scr[b, 1:2, :], zero, zero) for b in range(B)))
        for b in range(B):
            pr, pi, dar, dai = res[b]
            p_scr[b, 0:1, :] = pr
            p_scr[b, 1:2, :] = pi
            da_ref[0:1, :] += dar
            da_ref[1:2, :] += dai
            for jb in range(S5_BLOCKS):
                st = slice(jb * S5_BS, (jb + 1) * S5_BS)
                ch = slice(jb * S5_BC, (jb + 1) * S5_BC)
                gr_j = gr_scr[b, :, st].astype(bf16)
                gi_j = gi_scr[b, :, st].astype(bf16)
                du_ref[b, :, ch] = _dot_nt(gr_j, bre_ref[jb]) + _dot_nt(gi_j, bim_ref[jb])
                dbre_ref[jb] += _dot_tn(uvs[b][:, ch], gr_j)
                dbim_ref[jb] += _dot_tn(uvs[b][:, ch], gi_j)

        @pl.when(n == nt - 1)
        def _():
            dh0_ref[...] = p_scr[...]

    tok, hin_spec, state = _s5_specs(B, T, nt, not rev)
    w_in, w_out = _resident((S5_BLOCKS, S5_BC, S5_BS)), _resident((S5_BLOCKS, S5_BS, S5_BC))
    wspecs = [w_in, w_in, w_out, w_out]
    in_specs = [tok] + ([tok] if has_dy else []) + wspecs + [_resident((2, S5_HALF)), hin_spec, state]
    args = [u] + ([dy] if has_dy else []) + [bre, bim, ctop, cbot, arow, hin, dhend]
    return pl.pallas_call(
        body, name=name, grid=(nt,), in_specs=in_specs,
        out_specs=[tok] + wspecs + [_resident((2, S5_HALF)), state],
        out_shape=[SDS((B, L, D_S5), f32), SDS((S5_BLOCKS, S5_BC, S5_BS), f32), SDS((S5_BLOCKS, S5_BC, S5_BS), f32),
                   SDS((S5_BLOCKS, S5_BS, S5_BC), f32), SDS((S5_BLOCKS, S5_BS, S5_BC), f32), SDS((2, S5_HALF), f32),
                   SDS((B, 2, S5_HALF), f32)],
        scratch_shapes=[pltpu.VMEM((B, T + 2 * PAD, S5_HALF), f32), pltpu.VMEM((B, T + 2 * PAD, S5_HALF), f32),
                        pltpu.VMEM((B, T, S5_HALF), f32), pltpu.VMEM((B, T, S5_HALF), f32),
                        pltpu.VMEM((B, 2, S5_HALF), f32)],
        compiler_params=_cparams(1),
    )(*args)


def _glu_fn(u, y0, y1, z, dsk, wg, bg):
    g = _gelu(dsk * u + y0 + y1)
    return g * jax.nn.sigmoid(_mm(g, wg) + bg) * _silu(z)


CONV_ROWS = 16


def _shift(x, s):
    L = x.shape[0]
    k = (-s) % L
    return x if k == 0 else pltpu.roll(x, k, axis=0)


def _r16(v):
    return v.astype(bf16).astype(f32)


def _conv_masks(L, is_ctx):
    t = lax.broadcasted_iota(jnp.int32, (L, 1), 0)
    if is_ctx:
        return t == L - 1, t == 0, None, None
    col = jnp.bitwise_and(t, GRID_W - 1)
    return col == GRID_W - 1, col == 0, t >= GRID_W, t < L - GRID_W


def _conv_sides(xv, masks):
    no_left, no_right, _, _ = masks
    return _shift(jnp.where(no_left, 0.0, xv), -1), _shift(jnp.where(no_right, 0.0, xv), 1)


def _conv_pre(xv, w_ref, masks, is_ctx):
    xv = _r16(xv)
    wv = _r16(w_ref[...])
    xl, xr = _conv_sides(xv, masks)
    z = [wv[3 * di:3 * di + 1, :] * xl + wv[3 * di + 1:3 * di + 2, :] * xv + wv[3 * di + 2:3 * di + 3, :] * xr
         for di in ((1,) if is_ctx else (0, 1, 2))]
    if is_ctx:
        return z[0]
    _, _, has_up, has_down = masks
    return z[1] + jnp.where(has_up, _shift(z[0], -GRID_W), 0.0) + jnp.where(has_down, _shift(z[2], GRID_W), 0.0)


def _conv_pre_bwd(xv, w_ref, dpre, masks, is_ctx, dw_ref):
    no_left, no_right, has_up, has_down = masks
    xv, dpre, wv = _r16(xv), _r16(dpre), _r16(w_ref[...])
    xl, xr = _conv_sides(xv, masks)
    if is_ctx:
        dz = {1: dpre}
    else:
        dz = {0: _shift(jnp.where(has_up, dpre, 0.0), GRID_W), 1: dpre, 2: _shift(jnp.where(has_down, dpre, 0.0), -GRID_W)}
    dxl = dxc = dxr = None
    for di, d in dz.items():
        for dj, side in enumerate((xl, xv, xr)):
            dw_ref[3 * di + dj:3 * di + dj + 1, :] = jnp.sum(d * side, axis=0, keepdims=True)
        tl, tc, tr = (wv[3 * di + dj:3 * di + dj + 1, :] * d for dj in range(3))
        dxl, dxc, dxr = (tl, tc, tr) if dxl is None else (dxl + tl, dxc + tc, dxr + tr)
    return dxc + jnp.where(no_left, 0.0, _shift(dxl, 1)) + jnp.where(no_right, 0.0, _shift(dxr, -1))


def _qk_post(pre, is_norm, scale):
    s = _silu(pre)
    nrm = lax.rsqrt(jnp.sum(s * s, axis=-1, keepdims=True) + NORM_EPS)
    return s * jnp.where(is_norm, nrm * scale, 1.0)


def _conv_kind():
    ct = pl.program_id(1)
    return ct < 2 * GDN_HEADS, jnp.where(ct < GDN_HEADS, GDN_HEAD ** -0.5, 1.0).astype(f32)


def conv_fwd(qkv, w16, *, is_ctx, name):
    B, L, C = qkv.shape
    spec = pl.BlockSpec((None, L, GDN_HEAD), lambda b, ct: (b, 0, ct))
    wspec = pl.BlockSpec((CONV_ROWS, GDN_HEAD), lambda b, ct: (0, ct))

    def body(x_ref, w_ref, o_ref, pre_ref):
        is_norm, scale = _conv_kind()
        pre = _conv_pre(x_ref[...], w_ref, _conv_masks(L, is_ctx), is_ctx)
        pre_ref[...] = pre
        o_ref[...] = _qk_post(pre, is_norm, scale)

    return pl.pallas_call(body, name=name, grid=(B, C // GDN_HEAD), in_specs=[spec, wspec], out_specs=[spec, spec],
                          out_shape=[SDS((B, L, C), f32)] * 2, compiler_params=_cparams(2))(qkv, w16)


def conv_bwd(qkv, pre, w16, da0, da1, *, is_ctx, name):
    B, L, C = qkv.shape
    spec = pl.BlockSpec((None, L, GDN_HEAD), lambda b, ct: (b, 0, ct))
    wspec = pl.BlockSpec((CONV_ROWS, GDN_HEAD), lambda b, ct: (0, ct))
    dwspec = pl.BlockSpec((None, CONV_ROWS, GDN_HEAD), lambda b, ct: (b, 0, ct))

    def body(x_ref, pre_ref, w_ref, d0_ref, d1_ref, dx_ref, dw_ref):
        is_norm, scale = _conv_kind()
        _, vjp = jax.vjp(lambda p: _qk_post(p, is_norm, scale), pre_ref[...])
        dpre = vjp(d0_ref[...] + d1_ref[...])[0]
        dw_ref[...] = jnp.zeros_like(dw_ref)
        dx_ref[...] = _conv_pre_bwd(x_ref[...], w_ref, dpre, _conv_masks(L, is_ctx), is_ctx, dw_ref)

    return pl.pallas_call(body, name=name, grid=(B, C // GDN_HEAD), in_specs=[spec, spec, wspec, spec, spec],
                          out_specs=[spec, dwspec], out_shape=[SDS((B, L, C), f32), SDS((B, CONV_ROWS, C), f32)],
                          compiler_params=_cparams(2))(qkv, pre, w16, da0, da1)


def _gates_fn(ba, alog, dtb):
    T = ba.shape[0]
    lane = lax.broadcasted_iota(jnp.int32, ba.shape, 1)
    ii = lax.broadcasted_iota(jnp.int32, (T, T), 0)
    jj = lax.broadcasted_iota(jnp.int32, (T, T), 1)
    same = jnp.right_shift(ii, 6) == jnp.right_shift(jj, 6)
    lmat = jnp.logical_and(same, ii >= jj).astype(f32)
    umat = jnp.logical_and(same, ii <= jj).astype(f32)
    g = jnp.where(lane >= 8, -jnp.exp(alog) * jax.nn.softplus(ba + dtb), 0.0)
    gc = jnp.where(lane >= 12, _dot_hi(umat, g), _dot_hi(lmat, g))
    return jnp.where(lane < 8, jax.nn.sigmoid(ba), gc)


def gates_fwd(ba, alog, dtb, *, name):
    B, L, _ = ba.shape
    T = min(TOK_TILE, L)
    t = _tok(T, N_GATE)

    def body(ba_ref, al_ref, dt_ref, o_ref):
        o_ref[...] = _gates_fn(ba_ref[...], al_ref[...], dt_ref[...])

    return pl.pallas_call(body, name=name, grid=(B, L // T),
                          in_specs=[t, _resident((1, N_GATE)), _resident((1, N_GATE))], out_specs=t,
                          out_shape=SDS((B, L, N_GATE), f32), compiler_params=_cparams(2))(ba, alog, dtb)


def gates_bwd(ba, alog, dtb, dbg, *, name):
    B, L, _ = ba.shape
    T = min(TOK_TILE, L)
    t = _tok(T, N_GATE)
    small = _resident((1, N_GATE))

    def body(ba_ref, al_ref, dt_ref, d_ref, dba_ref, dal_ref, ddt_ref):
        @pl.when(_first_step())
        def _():
            dal_ref[...] = jnp.zeros_like(dal_ref)
            ddt_ref[...] = jnp.zeros_like(ddt_ref)

        _, vjp = jax.vjp(_gates_fn, ba_ref[...], al_ref[...], dt_ref[...])
        dba, dal, ddt = vjp(d_ref[...])
        dba_ref[...] = dba
        dal_ref[...] += dal
        ddt_ref[...] += ddt

    return pl.pallas_call(body, name=name, grid=(B, L // T), in_specs=[t, small, small, t],
                          out_specs=[t, small, small],
                          out_shape=[SDS((B, L, N_GATE), f32), SDS((1, N_GATE), f32), SDS((1, N_GATE), f32)],
                          compiler_params=_cparams(2))(ba, alog, dtb, dbg)


@jax.custom_vjp
def _inv_unit_tri(mats):
    n = mats[0].shape[0]
    eye = (lax.broadcasted_iota(jnp.int32, (n, n), 0) == lax.broadcasted_iota(jnp.int32, (n, n), 1)).astype(f32)
    xs = [eye - a for a in mats]
    sq = [_dot(a, a) for a in mats]
    ps = sq
    k = 2
    while k < n:
        xs = [x + _dot(x, p) for x, p in zip(xs, ps)]
        k *= 2
        if k < n:
            ps = [_dot(p, p) for p in ps]
    return tuple(_dot(p, x) - a for p, x, a in zip(sq, xs, mats))


def _inv_unit_tri_fwd(mats):
    ns = _inv_unit_tri(mats)
    return ns, ns


def _inv_unit_tri_bwd(ns, dns):
    ys = [dn + _dot_tn(nn, dn) for nn, dn in zip(ns, dns)]
    return (tuple(-(y + _dot_nt(y, nn)) for y, nn in zip(ys, ns)),)


_inv_unit_tri.defvjp(_inv_unit_tri_fwd, _inv_unit_tri_bwd)


def _gdn_chunk(heads, *, revs):
    n = heads[0][0].shape[0]
    ii = lax.broadcasted_iota(jnp.int32, (n, n), 0)
    jj = lax.broadcasted_iota(jnp.int32, (n, n), 1)
    row = lax.broadcasted_iota(jnp.int32, (n, 1), 0)
    lower = {False: ii >= jj, True: ii <= jj}
    strict = {False: ii > jj, True: ii < jj}
    last = {False: n - 1, True: 0}
    H = range(len(heads))
    q, k, v, beta, gc, gr, s = (list(t) for t in zip(*heads))
    decay = [jnp.where(lower[revs[h]], jnp.exp(jnp.where(lower[revs[h]], gc[h] - gr[h], 0.0)), 0.0) for h in H]
    kk = [_mm_nt(k[h], k[h]) for h in H]
    qk = [_mm_nt(q[h], k[h]) * decay[h] for h in H]
    qs = [_mm(q[h], s[h]) for h in H]
    a_mat = tuple(jnp.where(strict[revs[h]], beta[h] * kk[h] * decay[h], 0.0) for h in H)
    gamma = [jnp.exp(gc[h]) for h in H]
    g_last = [jnp.sum(jnp.where(row == last[revs[h]], gc[h], 0.0), axis=0, keepdims=True) for h in H]
    nmat = _inv_unit_tri(a_mat)
    bv = [beta[h] * v[h] for h in H]
    bk = [(beta[h] * gamma[h]) * k[h] for h in H]
    u0 = [bv[h] + _mm(nmat[h], bv[h]) for h in H]
    w = [bk[h] + _mm(nmat[h], bk[h]) for h in H]
    k_out = [k[h] * jnp.exp(g_last[h] - gc[h]) for h in H]
    u = [u0[h] - _mm(w[h], s[h]) for h in H]
    o = [gamma[h] * qs[h] + _mm(qk[h], u[h]) for h in H]
    s_new = [jnp.exp(g_last[h]) * s[h] + _mm_tn(k_out[h], u[h]) for h in H]
    return tuple((o[h], s_new[h]) for h in H)


def _gdn_specs(B, nc, rev):
    def cidx(n):
        return (nc - 1 - n) if rev else n
    tok = lambda width: pl.BlockSpec((B, CHUNK, width), lambda n: (0, cidx(n), 0))
    rowspec = pl.BlockSpec((B, None, N_GATE, CHUNK), lambda n: (0, cidx(n), 0, 0))
    st = pl.BlockSpec((B, GDN_HEADS, GDN_HEAD, GDN_HEAD), lambda n: (0, 0, 0, 0))
    ck = pl.BlockSpec((B, None, GDN_HEADS, GDN_HEAD, GDN_HEAD), lambda n: (0, cidx(n), 0, 0, 0))
    return tok, rowspec, st, ck


def _gdn_head_args(qkv_ref, bg_ref, bgr_ref, b, d, h):
    col = d * GDN_HEADS + h
    q = qkv_ref[b, :, h * GDN_HEAD:(h + 1) * GDN_HEAD]
    k = qkv_ref[b, :, D_GDN + h * GDN_HEAD:D_GDN + (h + 1) * GDN_HEAD]
    v = qkv_ref[b, :, 2 * D_GDN + h * GDN_HEAD:2 * D_GDN + (h + 1) * GDN_HEAD]
    bgv = bg_ref[b]
    return q, k, v, bgv[:, col:col + 1], bgv[:, 8 + col:9 + col], bgr_ref[b][8 + col:9 + col, :]


def _gdn_chains(B):
    return [(d, b, h) for d in range(N_DIR) for b in range(B) for h in range(GDN_HEADS)]


def gdn_fwd(qkv, bg, bgr, s0s, *, need_o, name):
    B, L, _ = qkv.shape
    nc = L // CHUNK
    specs = [_gdn_specs(B, nc, d == 1) for d in range(N_DIR)]
    chains = _gdn_chains(B)
    state_shape = (B, GDN_HEADS, GDN_HEAD, GDN_HEAD)

    def body(*refs):
        ins = [refs[3 * d:3 * d + 3] for d in range(N_DIR)]
        s0_refs = refs[6:8]
        k = 8
        o_refs = refs[k:k + 2] if need_o else None
        k += 2 if need_o else 0
        ck_refs, sf_refs, s_scrs = refs[k:k + 2], refs[k + 2:k + 4], refs[k + 4:k + 6]
        n = pl.program_id(0)

        @pl.when(n == 0)
        def _():
            for d in range(N_DIR):
                s_scrs[d][...] = s0_refs[d][...]

        for d in range(N_DIR):
            ck_refs[d][...] = s_scrs[d][...]
        heads = tuple(_gdn_head_args(*ins[d], b, d, h) + (s_scrs[d][b, h],) for d, b, h in chains)
        outs = _gdn_chunk(heads, revs=tuple(d == 1 for d, _, _ in chains))
        for (d, b, h), (o, s_new) in zip(chains, outs):
            if need_o:
                o_refs[d][b, :, h * GDN_HEAD:(h + 1) * GDN_HEAD] = o
            s_scrs[d][b, h] = s_new

        @pl.when(n == nc - 1)
        def _():
            for d in range(N_DIR):
                sf_refs[d][...] = s_scrs[d][...]

    in_specs, out_o, out_ck, out_sf = [], [], [], []
    for tok, rowspec, st, ck in specs:
        in_specs += [tok(3 * D_GDN), tok(N_GATE), rowspec]
        out_o.append(tok(D_GDN))
        out_ck.append(ck)
        out_sf.append(st)
    in_specs += [specs[0][2]] * 2
    out_specs = (out_o if need_o else []) + out_ck + out_sf
    out_shape = (([SDS((B, L, D_GDN), f32)] * 2 if need_o else [])
                 + [SDS((B, nc) + state_shape[1:], f32)] * 2 + [SDS(state_shape, f32)] * 2)
    res = pl.pallas_call(
        body, name=name, grid=(nc,), in_specs=in_specs, out_specs=out_specs, out_shape=out_shape,
        scratch_shapes=[pltpu.VMEM(state_shape, f32)] * 2, compiler_params=_cparams(1),
    )(qkv, bg, bgr, qkv, bg, bgr, *s0s)
    if need_o:
        return res[0:2], res[2:4], res[4:6]
    return res[0:2], res[2:4]


def gdn_bwd(qkv, bg, bgr, cks, do, dsfs, *, name):
    B, L, _ = qkv.shape
    nc = L // CHUNK
    has_do = do is not None
    specs = [_gdn_specs(B, nc, d != 1) for d in range(N_DIR)]
    chains = _gdn_chains(B)
    state_shape = (B, GDN_HEADS, GDN_HEAD, GDN_HEAD)
    per_dir = 5 if has_do else 4

    def body(*refs):
        ins = [refs[per_dir * d:per_dir * d + per_dir] for d in range(N_DIR)]
        k = per_dir * N_DIR
        dsf_refs = refs[k:k + 2]
        outs = [refs[k + 2 + 3 * d:k + 5 + 3 * d] for d in range(N_DIR)]
        ds0_refs, ds_scrs = refs[k + 8:k + 10], refs[k + 10:k + 12]
        n = pl.program_id(0)

        @pl.when(n == 0)
        def _():
            for d in range(N_DIR):
                ds_scrs[d][...] = dsf_refs[d][...]

        lane = lax.broadcasted_iota(jnp.int32, (CHUNK, N_GATE), 1)
        sub = lax.broadcasted_iota(jnp.int32, (N_GATE, CHUNK), 0)
        heads = tuple(_gdn_head_args(*ins[d][:3], b, d, h) + (ins[d][3][b, h],) for d, b, h in chains)
        _, vjp = jax.vjp(functools.partial(_gdn_chunk, revs=tuple(d == 1 for d, _, _ in chains)), heads)
        zero = jnp.zeros((CHUNK, GDN_HEAD), f32)
        cts = tuple(((ins[d][4][b, :, h * GDN_HEAD:(h + 1) * GDN_HEAD] if has_do else zero), ds_scrs[d][b, h])
                    for d, b, h in chains)
        (dheads,) = vjp(cts)
        dbg_acc = [[jnp.zeros((CHUNK, N_GATE), f32) for _ in range(B)] for _ in range(N_DIR)]
        dbgr_acc = [[jnp.zeros((N_GATE, CHUNK), f32) for _ in range(B)] for _ in range(N_DIR)]
        for (d, b, h), (dq, dk, dv, db, dgc, dgr, ds) in zip(chains, dheads):
            col = d * GDN_HEADS + h
            dqkv_ref = outs[d][0]
            dqkv_ref[b, :, h * GDN_HEAD:(h + 1) * GDN_HEAD] = dq
            dqkv_ref[b, :, D_GDN + h * GDN_HEAD:D_GDN + (h + 1) * GDN_HEAD] = dk
            dqkv_ref[b, :, 2 * D_GDN + h * GDN_HEAD:2 * D_GDN + (h + 1) * GDN_HEAD] = dv
            dbg_acc[d][b] = dbg_acc[d][b] + jnp.where(lane == col, db, 0.0) + jnp.where(lane == 8 + col, dgc, 0.0)
            dbgr_acc[d][b] = dbgr_acc[d][b] + jnp.where(sub == 8 + col, dgr, 0.0)
            ds_scrs[d][b, h] = ds
        for d in range(N_DIR):
            for b in range(B):
                outs[d][1][b] = dbg_acc[d][b]
                outs[d][2][b] = dbgr_acc[d][b]

        @pl.when(n == nc - 1)
        def _():
            for d in range(N_DIR):
                ds0_refs[d][...] = ds_scrs[d][...]

    in_specs, args, out_specs, out_shape = [], [], [], []
    for d, (tok, rowspec, st, ck) in enumerate(specs):
        in_specs += [tok(3 * D_GDN), tok(N_GATE), rowspec, ck] + ([tok(D_GDN)] if has_do else [])
        args += [qkv, bg, bgr, cks[d]] + ([do] if has_do else [])
        out_specs += [tok(3 * D_GDN), tok(N_GATE), rowspec]
        out_shape += [SDS((B, L, 3 * D_GDN), f32), SDS((B, L, N_GATE), f32), SDS((B, nc, N_GATE, CHUNK), f32)]
    st = specs[0][2]
    in_specs += [st, st]
    args += list(dsfs)
    out_specs += [st, st]
    out_shape += [SDS(state_shape, f32)] * 2
    res = pl.pallas_call(
        body, name=name, grid=(nc,), in_specs=in_specs, out_specs=out_specs, out_shape=out_shape,
        scratch_shapes=[pltpu.VMEM(state_shape, f32)] * 2, compiler_params=_cparams(1),
    )(*args)
    return (res[0], res[3]), (res[1], res[4]), (res[2], res[5]), (res[6], res[7])


def _gnorm_fn(o0, o1, z, w):
    o = o0 + o1
    return o * lax.rsqrt(jnp.mean(o * o, axis=-1, keepdims=True) + NORM_EPS) * w * _silu(z)


def _head_loss(y, x, gate, lng, lnb, tgt):
    r = DEEPNORM_ALPHA * x + gate * y
    mu = jnp.mean(r, axis=-1, keepdims=True)
    rc = r - mu
    var = jnp.mean(rc * rc, axis=-1, keepdims=True)
    err = rc * lax.rsqrt(var + LN_EPS) * lng + lnb - tgt
    return (0.5 / D_MODEL) * jnp.sum(jnp.sum(err * err, axis=-1, keepdims=True), axis=0, keepdims=True)


def tail_fwd_bwd(u, y0, y1, z_s5, o0, o1, z_gdn, x, tgt, gate, lng, lnb, ws, wg, dsk, wglu, bglu, nw):
    B, L, _ = x.shape
    T = min(TOK_TILE, L)

    def body(u_ref, y0_ref, y1_ref, z_ref, o0_ref, o1_ref, zg_ref, x_ref, t_ref, gate_ref, lng_ref, lnb_ref, ws_ref,
             wg_ref, dsk_ref, wglu_ref, bglu_ref, nw_ref,
             loss_ref, du_ref, dys_ref, dz_ref, do_ref, dzg_ref, gx_ref, dws_ref, dwg_ref, dgate_ref, dlng_ref, dlnb_ref,
             ddsk_ref, dwglu_ref, dbglu_ref, dnw_ref):
        n = pl.program_id(1)

        @pl.when(_first_step())
        def _():
            for r in (dws_ref, dwg_ref, dlng_ref, dlnb_ref, ddsk_ref, dwglu_ref, dbglu_ref, dnw_ref):
                r[...] = jnp.zeros_like(r)

        @pl.when(n == 0)
        def _():
            loss_ref[...] = jnp.zeros_like(loss_ref)
            dgate_ref[...] = jnp.zeros_like(dgate_ref)

        s5o, glu_vjp = jax.vjp(_glu_fn, u_ref[...], y0_ref[...], y1_ref[...], z_ref[...], dsk_ref[...],
                               wglu_ref[...].astype(f32), bglu_ref[...])
        heads = []
        for h in range(GDN_HEADS):
            sl = slice(h * GDN_HEAD, (h + 1) * GDN_HEAD)
            heads.append(jax.vjp(_gnorm_fn, o0_ref[:, sl], o1_ref[:, sl], zg_ref[:, sl], nw_ref[...]))
        sv = s5o.astype(bf16)
        gv = jnp.concatenate([out for out, _ in heads], axis=1).astype(bf16)
        y = _dot(sv, ws_ref[...]) + _dot(gv, wg_ref[...])
        loss, vjp = jax.vjp(lambda *a: _head_loss(*a, t_ref[...]), y, x_ref[...], gate_ref[...], lng_ref[...],
                            lnb_ref[...])
        dy, dx, dgate, dlng, dlnb = vjp(jnp.ones((1, 1), f32))
        loss_ref[...] += jnp.broadcast_to(loss, loss_ref.shape)
        dyb = dy.astype(bf16)
        gx_ref[...] = dx
        dws_ref[...] += _dot_tn(sv, dyb)
        dwg_ref[...] += _dot_tn(gv, dyb)
        dgate_ref[...] += dgate
        dlng_ref[...] += dlng
        dlnb_ref[...] += dlnb
        du, dys, _, dz, ddsk, dwglu, dbglu = glu_vjp(_dot_nt(dyb, ws_ref[...]))
        du_ref[...], dys_ref[...], dz_ref[...] = du, dys, dz
        ddsk_ref[...] += ddsk
        dwglu_ref[...] += dwglu
        dbglu_ref[...] += dbglu
        dgdo = _dot_nt(dyb, wg_ref[...])
        for h, (_, hvjp) in enumerate(heads):
            sl = slice(h * GDN_HEAD, (h + 1) * GDN_HEAD)
            do, _, dzg, dnw = hvjp(dgdo[:, sl])
            do_ref[:, sl] = do
            dzg_ref[:, sl] = dzg
            dnw_ref[...] += dnw

    half, full = _tok(T, D_S5), _tok(T, D_MODEL)
    row = _resident((1, D_MODEL))
    wsp = _resident((D_S5, D_MODEL))
    r512, rglu, r128 = _resident((1, D_S5)), _resident((D_S5, D_S5)), _resident((1, GDN_HEAD))
    return pl.pallas_call(
        body, name="tail_fwd_bwd", grid=(B, L // T),
        in_specs=[half] * 7 + [full, full, _per_batch(1, D_MODEL), row, row, wsp, wsp, r512, rglu, r512, r128],
        out_specs=[_per_batch(8, LANES)] + [half] * 5 + [full, wsp, wsp, _per_batch(1, D_MODEL), row, row, r512, rglu, r512,
                                                           r128],
        out_shape=[SDS((B, 8, LANES), f32)] + [SDS((B, L, D_S5), f32)] * 5 + [
            SDS((B, L, D_MODEL), f32), SDS((D_S5, D_MODEL), f32), SDS((D_GDN, D_MODEL), f32), SDS((B, 1, D_MODEL), f32),
            SDS((1, D_MODEL), f32), SDS((1, D_MODEL), f32), SDS((1, D_S5), f32), SDS((D_S5, D_S5), f32), SDS((1, D_S5), f32),
            SDS((1, GDN_HEAD), f32)],
        compiler_params=_cparams(2),
    )(u, y0, y1, z_s5, o0, o1, z_gdn, x, tgt, gate, lng, lnb, ws, wg, dsk, wglu, bglu, nw)


def _adamw_math(w, g, m, v):
    nm = ADAM_B1 * m + (1.0 - ADAM_B1) * g
    nv = ADAM_B2 * v + (1.0 - ADAM_B2) * jnp.square(g)
    m_hat = nm / (1.0 - ADAM_B1 ** ADAM_STEP)
    v_hat = nv / (1.0 - ADAM_B2 ** ADAM_STEP)
    return -ADAM_LR * (m_hat / (jnp.sqrt(v_hat) + ADAM_EPS) + ADAM_WD * w), nm, nv


def _row_tile(rows, cap=512):
    for t in range(min(cap, rows), 15, -1):
        if rows % t == 0 and t % 16 == 0:
            return t
    return rows


def adamw_3d(w, g, m, v, *, lead=False, name):
    R, C = (w.shape[0], w.shape[2]) if lead else w.shape[1:]
    if lead:
        T = next(t for t in range(min(256, R), 0, -1) if R % t == 0)
        spec = pl.BlockSpec((T, 1, C), lambda i: (i, 0, 0))
    else:
        T = _row_tile(R)
        spec = pl.BlockSpec((None, T, C), lambda i: (0, i, 0))

    def body(w_ref, g_ref, m_ref, v_ref, d_ref, nm_ref, nv_ref):
        d_ref[...], nm_ref[...], nv_ref[...] = _adamw_math(w_ref[...], g_ref[...], m_ref[...], v_ref[...])

    return pl.pallas_call(body, name=name, grid=(R // T,), in_specs=[spec] * 4, out_specs=[spec] * 3,
                          out_shape=[SDS(w.shape, f32)] * 3, compiler_params=_cparams(1))(w, g, m, v)


def adamw_small(ws, gs, ms, vs):
    n = len(ws)

    def body(*refs):
        outs = refs[4 * n:]
        for i in range(n):
            d, nm, nv = _adamw_math(refs[i][...], refs[n + i][...], refs[2 * n + i][...], refs[3 * n + i][...])
            outs[i][...], outs[n + i][...], outs[2 * n + i][...] = d, nm, nv

    res = pl.pallas_call(body, name="adamw_small", out_shape=[SDS(w.shape, f32) for w in ws] * 3,
                         compiler_params=pltpu.CompilerParams(vmem_limit_bytes=VMEM_LIMIT))(*ws, *gs, *ms, *vs)
    return res[:n], res[n:2 * n], res[2 * n:]


def sum_cores(own, got, *, name):
    A, H, C = own.shape
    T = _row_tile(H)
    spec = pl.BlockSpec((None, T, C), lambda a, i: (a, i, 0))

    def body(a_ref, b_ref, q32_ref, q16_ref):
        q = a_ref[...] + b_ref[...]
        q32_ref[...] = q
        q16_ref[...] = q.astype(bf16)

    return pl.pallas_call(body, name=name, grid=(A, H // T), in_specs=[spec, spec], out_specs=[spec, spec],
                          out_shape=[SDS((A, H, C), f32), SDS((A, H, C), bf16)], compiler_params=_cparams(2))(own, got)


def sum_chips(mine, rec, cpos, *, name):
    H, C = mine.shape
    T = _row_tile(H)
    nt = H // T

    def body(c_ref, m_ref, r_ref, f_ref):
        f_ref[...] = ((m_ref[...] + r_ref[0].astype(f32)) + r_ref[1].astype(f32)) + r_ref[2].astype(f32)

    grid_spec = pltpu.PrefetchScalarGridSpec(
        num_scalar_prefetch=1, grid=(nt,),
        in_specs=[pl.BlockSpec((T, C), lambda i, c_ref: (i, 0)), pl.BlockSpec((3, T, C), lambda i, c_ref: (0, i, 0))],
        out_specs=pl.BlockSpec((None, T, C), lambda i, c_ref: (0, c_ref[0] * nt + i, 0)))
    return pl.pallas_call(body, name=name, grid_spec=grid_spec, out_shape=SDS((1, 2 * H, C), f32),
                          compiler_params=_cparams(1))(cpos.reshape(1).astype(jnp.int32), mine, rec)


CHIP_FLIPS = ((1, 0), (0, 1), (1, 1))


def _pos():
    return lax.axis_index("x"), lax.axis_index("y"), lax.axis_index("c")


def _comm_call(body, srcs, out_sds, n_remote, n_local, name):
    any_spec = pl.BlockSpec(memory_space=pl.ANY)
    return pl.pallas_call(
        body, name=name, in_specs=[any_spec] * len(srcs), out_specs=[any_spec] * len(out_sds), out_shape=out_sds,
        scratch_shapes=[pltpu.SemaphoreType.DMA((n_remote,)), pltpu.SemaphoreType.DMA((n_remote,)),
                        pltpu.SemaphoreType.DMA((max(n_local, 1),))],
        compiler_params=pltpu.CompilerParams(has_side_effects=True),
    )(*srcs)


def _remote(src, dst, send_sems, recv_sems, k, target):
    return pltpu.make_async_remote_copy(src, dst, send_sems.at[k], recv_sems.at[k], device_id=target,
                                        device_id_type=MESH)


def _half_rows(c, rows):
    half = rows // 2
    return pl.ds(pl.multiple_of(c * half, 8), half)


def gather_shards(shards):
    nt = len(shards)

    def body(*refs):
        srcs, outs = refs[:nt], refs[nt:2 * nt]
        send_sems, recv_sems, _ = refs[2 * nt:]
        x, y, c = _pos()
        j = 2 * x + y
        sib = (x, y, 1 - c)
        own = [_remote(srcs[t], outs[t].at[j], send_sems, recv_sems, 7 * t + 6, sib) for t in range(nt)]
        first, passed = [], []
        for k, (fx, fy) in enumerate(CHIP_FLIPS):
            tx, ty = x ^ fx, y ^ fy
            jk = 2 * tx + ty
            for t in range(nt):
                rows = _half_rows(c, srcs[t].shape[0])
                first.append(_remote(srcs[t].at[rows], outs[t].at[j, rows], send_sems, recv_sems, 7 * t + k, (tx, ty, c)))
                passed.append(_remote(outs[t].at[jk, rows], outs[t].at[jk, rows], send_sems, recv_sems, 7 * t + 3 + k, sib))
        for cp in first + own:
            cp.start()
        for a, b in zip(first, passed):
            a.wait_recv()
            b.start()
        for cp in passed + own:
            cp.wait_recv()
        for cp in first + passed + own:
            cp.wait_send()

    return _comm_call(body, shards, [SDS((4,) + s.shape, s.dtype) for s in shards], 7 * nt, 0, "gather_shards")


def swap_halves(ps):
    nt = len(ps)

    def body(*refs):
        srcs, outs = refs[:nt], refs[nt:2 * nt]
        send_sems, recv_sems, _ = refs[2 * nt:]
        x, y, c = _pos()
        cps = [_remote(srcs[t].at[a, _half_rows(1 - c, srcs[t].shape[1])], outs[t].at[a], send_sems, recv_sems, 4 * t + a,
                       (x, y, 1 - c)) for t in range(nt) for a in range(4)]
        for cp in cps:
            cp.start()
        for cp in cps:
            cp.wait()

    return _comm_call(body, ps, [SDS((4, p.shape[1] // 2, p.shape[2]), p.dtype) for p in ps], 4 * nt, 0, "swap_halves")


def scatter_to_chips(qs):
    nt = len(qs)

    def body(*refs):
        srcs, outs = refs[:nt], refs[nt:2 * nt]
        send_sems, recv_sems, _ = refs[2 * nt:]
        x, y, c = _pos()
        cps = []
        for k, (fx, fy) in enumerate(CHIP_FLIPS):
            tx, ty = x ^ fx, y ^ fy
            for t in range(nt):
                cps.append(_remote(srcs[t].at[2 * tx + ty], outs[t].at[k], send_sems, recv_sems, 3 * t + k, (tx, ty, c)))
        for cp in cps:
            cp.start()
        for cp in cps:
            cp.wait()

    return _comm_call(body, qs, [SDS((3,) + q.shape[1:], q.dtype) for q in qs], 3 * nt, 0, "scatter_to_chips")


def join_halves(fs):
    nt = len(fs)

    def body(*refs):
        outs = refs[nt:2 * nt]
        send_sems, recv_sems, _ = refs[2 * nt:]
        x, y, c = _pos()
        cps = []
        for t in range(nt):
            mine = outs[t].at[0, _half_rows(c, outs[t].shape[1])]
            cps.append(_remote(mine, mine, send_sems, recv_sems, t, (x, y, 1 - c)))
        for cp in cps:
            cp.start()
        for cp in cps:
            cp.wait()

    any_spec = pl.BlockSpec(memory_space=pl.ANY)
    return pl.pallas_call(
        body, name="join_halves", in_specs=[any_spec] * nt, out_specs=[any_spec] * nt,
        out_shape=[SDS(f.shape, f.dtype) for f in fs], input_output_aliases={t: t for t in range(nt)},
        scratch_shapes=[pltpu.SemaphoreType.DMA((nt,)), pltpu.SemaphoreType.DMA((nt,)), pltpu.SemaphoreType.DMA((1,))],
        compiler_params=pltpu.CompilerParams(has_side_effects=True),
    )(*fs)


DEV_FLIPS = tuple((fx, fy, fc) for fx in (0, 1) for fy in (0, 1) for fc in (0, 1))[1:]


def gather_devices(block, *, name):
    def body(src, out, send_sems, recv_sems, loc_sems):
        x, y, c = _pos()
        me = 4 * x + 2 * y + c
        mine = pltpu.make_async_copy(src, out.at[me], loc_sems.at[0])
        mine.start()
        cps = [_remote(src, out.at[me], send_sems, recv_sems, k, (x ^ fx, y ^ fy, c ^ fc))
               for k, (fx, fy, fc) in enumerate(DEV_FLIPS)]
        for cp in cps:
            cp.start()
        for cp in cps:
            cp.wait()
        mine.wait()

    return _comm_call(body, [block], [SDS((N_DEV,) + block.shape, block.dtype)], 7, 1, name)[0]


def exchange_devices(blocks, *, name):
    def body(src, out, send_sems, recv_sems, loc_sems):
        x, y, c = _pos()
        me = 4 * x + 2 * y + c
        mine = pltpu.make_async_copy(src.at[me], out.at[me], loc_sems.at[0])
        mine.start()
        cps = []
        for k, (fx, fy, fc) in enumerate(DEV_FLIPS):
            tx, ty, tc = x ^ fx, y ^ fy, c ^ fc
            cps.append(_remote(src.at[4 * tx + 2 * ty + tc], out.at[me], send_sems, recv_sems, k, (tx, ty, tc)))
        for cp in cps:
            cp.start()
        for cp in cps:
            cp.wait()
        mine.wait()

    return _comm_call(body, [blocks], [SDS(blocks.shape, blocks.dtype)], 7, 1, name)[0]


def gather_small(s):
    def body(src, out, send_sems, recv_sems, _):
        x, y, c = _pos()
        j = 2 * x + y
        cps = [_remote(src, out.at[j], send_sems, recv_sems, k, (x ^ fx, y ^ fy, c)) for k, (fx, fy) in enumerate(CHIP_FLIPS)]
        cps.append(_remote(src, out.at[j], send_sems, recv_sems, 3, (x, y, 1 - c)))
        for cp in cps:
            cp.start()
        for cp in cps:
            cp.wait()

    return _comm_call(body, [s], [SDS((4,) + s.shape, s.dtype)], 4, 0, "gather_small")[0]


SMALL_SHAPES = ((1, 2, 32, 64), (1, 2, 32, 64), (1, 2, 32), (1, 2, 32, 16, 64),
                (1, 2, 32, 16, 64), (1, 2, 32, 16, 64), (1, 2, 32, 16, 64), (1, D_S5), (1, D_S5), (1, 2, 4), (1, 2, 4),
                (1, GDN_HEAD), (1, D_MODEL), (1, D_MODEL))
SMALL_SWAPPED = (3, 4)


def _size(shape):
    return functools.reduce(lambda p, q: p * q, shape)


SMALL_ROWS = tuple(-(-_size(s) // LANES) for s in SMALL_SHAPES)
SMALL_TOTAL = 2176
SMALL_QUARTER = SMALL_TOTAL // 4


def _rows(a):
    flat = a.reshape(-1)
    pad = (-flat.shape[0]) % LANES
    if pad:
        flat = jnp.concatenate([flat, jnp.zeros((pad,), flat.dtype)])
    return flat.reshape(-1, LANES)


def _pack_small(parts):
    rows = [_rows(p) for p in parts]
    rows.append(jnp.zeros((SMALL_TOTAL - sum(SMALL_ROWS), LANES), f32))
    return jnp.concatenate(rows, axis=0)


def _unpack_small(buf):
    out, r = [], 0
    for s, n in zip(SMALL_SHAPES, SMALL_ROWS):
        out.append(buf[r:r + n].reshape(-1)[:_size(s)].reshape(s))
        r += n
    return out


def _as_2d(a):
    return a.reshape(1, -1) if a.ndim == 1 else a.reshape(-1, a.shape[-1])


S5_BG = S5_GROUPS // S5_BLOCKS


def _block_diag_in(bb):
    eye = jnp.eye(S5_BG, dtype=bb.dtype)
    b4 = bb.reshape(S5_BLOCKS, S5_BG, S5_GROUP, S5_STATE)
    return jnp.einsum('jgcp,gh->jgchp', b4, eye).reshape(S5_BLOCKS, S5_BC, S5_BS)


def _block_diag_in_t(d):
    d6 = d.reshape(S5_BLOCKS, S5_BG, S5_GROUP, S5_BG, S5_STATE)
    return jnp.einsum('jgcgp->jgcp', d6).reshape(S5_GROUPS, S5_GROUP * S5_STATE)


def _block_diag_out(cm):
    eye = jnp.eye(S5_BG, dtype=cm.dtype)
    c4 = cm.reshape(S5_BLOCKS, S5_BG, S5_GROUP, S5_STATE)
    return jnp.einsum('jgcp,gh->jhpgc', c4, eye).reshape(S5_BLOCKS, S5_BS, S5_BC)


def _block_diag_out_t(d):
    d6 = d.reshape(S5_BLOCKS, S5_BG, S5_STATE, S5_BG, S5_GROUP)
    return jnp.einsum('jgpgc->jgcp', d6).reshape(S5_GROUPS, S5_GROUP, S5_STATE)


def _to_chunk_rows(a):
    B, L, W = a.shape
    return a.reshape(B, L // CHUNK, CHUNK, W).transpose(0, 1, 3, 2)


def _from_chunk_rows(a):
    B, nc, W, _ = a.shape
    return a.transpose(0, 1, 3, 2).reshape(B, nc * CHUNK, W)


def local_step(x, ctx, tgt, m, w_in, lam_re, lam_im, log_dt, b_re, b_im, c_re, c_im, s5_d,
               w_glu, b_glu, conv16, a_log, dt_bias, norm_w, w_out, ln_g, ln_b):
    B, L, _ = x.shape
    zeros_state = jnp.zeros((B, GDN_HEADS, GDN_HEAD, GDN_HEAD), f32)

    shift, scale, gate = m[:B, :D_MODEL], m[:B, D_MODEL:2 * D_MODEL], m[:B, 2 * D_MODEL:]
    mod = jnp.stack([scale, shift], axis=1)
    mod_c = jnp.broadcast_to(jnp.stack([m[B, D_MODEL:2 * D_MODEL], m[B, :D_MODEL]], axis=0)[None], (B, 2, D_MODEL))

    u, z_s5, qkv, z_gdn, ba = in_proj_fwd(x, mod, w_in, name="in_proj_fwd")
    uc, _, qkvc, _, bac = in_proj_fwd(ctx, mod_c, w_in, name="in_proj_fwd_ctx")

    ng = N_DIR * S5_GROUPS
    zoh_in = (lam_re.reshape(ng, S5_STATE), lam_im.reshape(ng, S5_STATE), log_dt.reshape(ng, 1),
              b_re.reshape(ng, S5_GROUP * S5_STATE), b_im.reshape(ng, S5_GROUP * S5_STATE))
    expand = (jnp.arange(S5_GROUP * S5_STATE)[None, :] % S5_STATE == jnp.arange(S5_STATE)[:, None]).astype(f32)
    ar, ai, bbr, bbi = s5_zoh_fwd(*zoh_in, expand)
    bbr16, bbi16 = bbr.astype(bf16), bbi.astype(bf16)
    c_re16 = c_re.reshape(N_DIR, S5_GROUPS, S5_GROUP, S5_STATE).astype(bf16)
    c_im16 = (-c_im).reshape(N_DIR, S5_GROUPS, S5_GROUP, S5_STATE).astype(bf16)
    s5w, ys, hins, hins_c = [], [], [], []
    for d in range(N_DIR):
        g = slice(d * S5_GROUPS, (d + 1) * S5_GROUPS)
        wd = (_block_diag_in(bbr16[g]), _block_diag_in(bbi16[g]), _block_diag_out(c_re16[d]), _block_diag_out(c_im16[d]),
              jnp.stack([ar[g].reshape(-1), ai[g].reshape(-1)], axis=0))
        s5w.append(wd)
        hin_c, hend_c = s5_scan_fwd(uc, *wd, jnp.zeros((B, 2, S5_HALF), f32), d=d, need_y=False, name=f"s5_fwd_ctx{d}")
        y_d, hin, _ = s5_scan_fwd(u, *wd, hend_c, d=d, need_y=True, name=f"s5_fwd{d}")
        ys.append(y_d)
        hins.append(hin)
        hins_c.append(hin_c)
    glu_w = (s5_d.reshape(1, D_S5), w_glu, b_glu.reshape(1, D_S5))

    act, pre = conv_fwd(qkv, conv16, is_ctx=False, name="conv_fwd")
    act_c, pre_c = conv_fwd(qkvc, conv16, is_ctx=True, name="conv_fwd_ctx")
    pad8 = jnp.zeros((1, 8), f32)
    alog16 = jnp.concatenate([pad8, a_log.reshape(1, 8)], axis=1)
    dtb16 = jnp.concatenate([pad8, dt_bias.reshape(1, 8)], axis=1)
    bg = gates_fwd(ba, alog16, dtb16, name="gates_fwd")
    bg_c = gates_fwd(bac, alog16, dtb16, name="gates_fwd_ctx")
    bgr, bgr_c = _to_chunk_rows(bg), _to_chunk_rows(bg_c)
    cks_c, s_c = gdn_fwd(act_c, bg_c, bgr_c, (zeros_state, zeros_state), need_o=False, name="gdn_fwd_ctx")
    os_, cks, _ = gdn_fwd(act, bg, bgr, s_c, need_o=True, name="gdn_fwd")
    nw = norm_w.reshape(1, GDN_HEAD)

    (loss8, du_skip, dy, dz_s5, do, dz_gdn, gx_res, dws, dwg, dgate, dlng, dlnb, d_s5_d, d_w_glu, d_b_glu,
     d_norm_w) = tail_fwd_bwd(u, ys[0], ys[1], z_s5, os_[0], os_[1], z_gdn, x, tgt, gate[:, None, :],
                              ln_g.reshape(1, D_MODEL), ln_b.reshape(1, D_MODEL), w_out[:D_S5], w_out[D_S5:], *glu_w, nw)
    loss = jnp.sum(loss8[:, 0, 0])
    d_w_out = jnp.concatenate([dws, dwg], axis=0)

    dacts, dbgs, dbgrs, ds0s = gdn_bwd(act, bg, bgr, cks, do, (zeros_state, zeros_state), name="gdn_bwd")
    dacts_c, dbgs_c, dbgrs_c, _ = gdn_bwd(act_c, bg_c, bgr_c, cks_c, None, ds0s, name="gdn_bwd_ctx")
    dbg = dbgs[0] + dbgs[1] + _from_chunk_rows(dbgrs[0] + dbgrs[1])
    dbg_c = dbgs_c[0] + dbgs_c[1] + _from_chunk_rows(dbgrs_c[0] + dbgrs_c[1])
    dba, dal, ddt = gates_bwd(ba, alog16, dtb16, dbg, name="gates_bwd")
    dbac, dal_c, ddt_c = gates_bwd(bac, alog16, dtb16, dbg_c, name="gates_bwd_ctx")
    d_a_log = (dal + dal_c)[:, 8:].reshape(1, N_DIR, GDN_HEADS)
    d_dt_bias = (ddt + ddt_c)[:, 8:].reshape(1, N_DIR, GDN_HEADS)
    dqkv, dcw = conv_bwd(qkv, pre, conv16, dacts[0], dacts[1], is_ctx=False, name="conv_bwd")
    dqkvc, dcw_c = conv_bwd(qkvc, pre_c, conv16, dacts_c[0], dacts_c[1], is_ctx=True, name="conv_bwd_ctx")
    d_conv16 = jnp.sum(dcw, axis=0) + jnp.sum(dcw_c, axis=0)

    dus, ducs = [du_skip], []
    dar, dai, dbbr, dbbi, dcre, dcim = [], [], [], [], [], []
    for d in range(N_DIR):
        du_d, dbre1, dbim1, dct1, dcb1, da1, dh0 = s5_scan_bwd(u, dy, *s5w[d], hins[d],
                                                                jnp.zeros((B, 2, S5_HALF), f32), d=d, name=f"s5_bwd{d}")
        duc_d, dbre2, dbim2, _, _, da2, _ = s5_scan_bwd(uc, None, *s5w[d], hins_c[d], dh0, d=d, name=f"s5_bwd_ctx{d}")
        dus.append(du_d)
        ducs.append(duc_d)
        da = da1 + da2
        dar.append(da[0].reshape(S5_GROUPS, S5_STATE))
        dai.append(da[1].reshape(S5_GROUPS, S5_STATE))
        dbbr.append(_block_diag_in_t(dbre1 + dbre2))
        dbbi.append(_block_diag_in_t(dbim1 + dbim2))
        dcre.append(_block_diag_out_t(dct1))
        dcim.append(-_block_diag_out_t(dcb1))
    dlr, dli, dldt, dbre, dbim = s5_zoh_bwd(*zoh_in, expand, jnp.concatenate(dar, 0), jnp.concatenate(dai, 0),
                                            jnp.concatenate(dbbr, 0), jnp.concatenate(dbbi, 0))
    d_s5 = (dlr, dli, dldt, dbre, dbim, jnp.stack(dcre, 0), jnp.stack(dcim, 0))

    padg = lambda a: jnp.concatenate([a, jnp.zeros(a.shape[:2] + (LANES - N_GATE,), f32)], axis=2)
    zc = jnp.zeros_like(uc)
    dw_c, dmod_c = in_proj_bwd(ctx, mod_c, (tuple(ducs), zc, dqkvc, zc, padg(dbac)), w_in, None, None,
                               name="in_proj_bwd_ctx")
    d_w_in, dmod, grad_x = in_proj_bwd(x, mod, (tuple(dus), dz_s5, dqkv, dz_gdn, padg(dba)), w_in, gx_res, dw_c,
                                       name="in_proj_bwd")
    dmod_c = jnp.sum(dmod_c, axis=0)

    dm_rows = jnp.concatenate([dmod[:, 1], dmod[:, 0], dgate[:, 0]], axis=1)
    dm_ctx = jnp.concatenate([dmod_c[1], dmod_c[0], jnp.zeros((D_MODEL,), f32)])[None]
    dm = jnp.concatenate([dm_rows, dm_ctx], axis=0)
    small = (*d_s5, d_s5_d, d_b_glu, d_a_log, d_dt_bias, d_norm_w, dlng, dlnb)
    small = tuple(g.reshape(s) for g, s in zip(small, SMALL_SHAPES))
    return loss, grad_x, (d_w_in, d_w_out, d_w_glu, d_conv16), small, dm


SHARDED = (1, 3, 18, 12, 14)
REDUCED = (3, 18, 12, 14)
UNSHARDED = tuple(i for i in range(21) if i not in SHARDED)
SMALL = tuple(i for i in UNSHARDED if i not in (0, 2))
W_IN_SHARD = 772


def _conv_rows(w):
    return jnp.concatenate([w.reshape(9, w.shape[-1]), jnp.zeros((CONV_ROWS - 9, w.shape[-1]), f32)], axis=0)


def kernel(x, c, ctx, c_ctx, w_ada, b_ada, w_in, s5_lambda_re, s5_lambda_im, s5_log_dt, s5_b_re, s5_b_im, s5_c_re, s5_c_im, s5_d, w_glu, b_glu, conv_w, gdn_a_log, gdn_dt_bias, gdn_norm_w, w_out, ln_g, ln_b, loss_target, m_c_ctx, m_w_ada, m_b_ada, m_w_in, m_s5_lambda_re, m_s5_lambda_im, m_s5_log_dt, m_s5_b_re, m_s5_b_im, m_s5_c_re, m_s5_c_im, m_s5_d, m_w_glu, m_b_glu, m_conv_w, m_gdn_a_log, m_gdn_dt_bias, m_gdn_norm_w, m_w_out, m_ln_g, m_ln_b, v_c_ctx, v_w_ada, v_b_ada, v_w_in, v_s5_lambda_re, v_s5_lambda_im, v_s5_log_dt, v_s5_b_re, v_s5_b_im, v_s5_c_re, v_s5_c_im, v_s5_d, v_w_glu, v_b_glu, v_conv_w, v_gdn_a_log, v_gdn_dt_bias, v_gdn_norm_w, v_w_out, v_ln_g, v_ln_b):
    weights = [c_ctx, w_ada, b_ada, w_in, s5_lambda_re, s5_lambda_im, s5_log_dt, s5_b_re, s5_b_im, s5_c_re, s5_c_im,
               s5_d, w_glu, b_glu, conv_w, gdn_a_log, gdn_dt_bias, gdn_norm_w, w_out, ln_g, ln_b]
    ms = [m_c_ctx, m_w_ada, m_b_ada, m_w_in, m_s5_lambda_re, m_s5_lambda_im, m_s5_log_dt, m_s5_b_re, m_s5_b_im,
          m_s5_c_re, m_s5_c_im, m_s5_d, m_w_glu, m_b_glu, m_conv_w, m_gdn_a_log, m_gdn_dt_bias, m_gdn_norm_w, m_w_out,
          m_ln_g, m_ln_b]
    vs = [v_c_ctx, v_w_ada, v_b_ada, v_w_in, v_s5_lambda_re, v_s5_lambda_im, v_s5_log_dt, v_s5_b_re, v_s5_b_im,
          v_s5_c_re, v_s5_c_im, v_s5_d, v_w_glu, v_b_glu, v_conv_w, v_gdn_a_log, v_gdn_dt_bias, v_gdn_norm_w, v_w_out,
          v_ln_g, v_ln_b]
    cpos = lax.axis_index("c")
    jchip = 2 * lax.axis_index("x") + lax.axis_index("y")

    c_all = gather_devices(c, name="gather_c")
    cc = jnp.concatenate([c_all, jnp.broadcast_to(c_ctx[None, None, :], (N_DEV, 1, D_MODEL)),
                          jnp.zeros((N_DEV, 5, D_MODEL), f32)], axis=1)
    w_ada16 = w_ada[0].astype(bf16)
    b_cols = lax.dynamic_slice_in_dim(b_ada, jchip * ADA_SHARD, ADA_SHARD, axis=1)
    m_mine = exchange_devices(ada_fwd(cc, w_ada16, b_cols), name="exchange_m")
    m_rows = jnp.concatenate([m_mine[2 * j, :3] for j in range(4)], axis=1)

    conv_shard = _conv_rows(conv_w)
    g_in, g_out, g_glu, g_conv = gather_shards(
        [w_in[0].astype(bf16), w_out[0].astype(bf16), w_glu[0].astype(bf16), conv_shard])
    w_in_pad = jnp.concatenate([g_in[0], g_in[1], g_in[2], g_in[3], jnp.zeros((D_MODEL, IN_PAD - P_IN), bf16)], axis=1)
    conv16 = g_conv.transpose(1, 0, 2).reshape(CONV_ROWS, 3 * D_GDN)

    swap = lambda a: jnp.swapaxes(a, 3, 4)
    loss, grad_x, big, small, dm_rows = local_step(
        x, ctx, loss_target, m_rows, w_in_pad, s5_lambda_re, s5_lambda_im, s5_log_dt, swap(s5_b_re), swap(s5_b_im),
        s5_c_re, s5_c_im, s5_d, g_glu.reshape(D_S5, D_S5), b_glu, conv16, gdn_a_log, gdn_dt_bias, gdn_norm_w,
        g_out.reshape(D_MODEL, D_MODEL), ln_g, ln_b)
    loss = lax.psum(loss, ("x", "y", "c"))

    dm8 = jnp.concatenate([dm_rows, jnp.zeros((5, 3 * D_MODEL), f32)], axis=0)
    dm_by_chip = dm8.reshape(8, 4, ADA_SHARD).transpose(1, 0, 2)
    dm_cols = exchange_devices(jnp.repeat(dm_by_chip, 2, axis=0), name="exchange_dm")
    g_w_ada, pb = ada_bwd(cc, w_ada16, dm_cols)
    pb_all = gather_devices(pb, name="gather_p")
    g_c_ctx = c_ctx_bwd(pb_all, c_ctx[None, :])[0]
    g_b_ada = jnp.concatenate([pb_all[2 * j, 1:2, :ADA_SHARD] for j in range(4)], axis=1)

    d_w_in, d_w_out, d_w_glu, d_conv16 = big
    slabs = [d_w_in[:, :P_IN].reshape(D_MODEL, 4, W_IN_SHARD).transpose(1, 0, 2),
             d_w_out.reshape(4, D_MODEL // 4, D_MODEL),
             d_w_glu.reshape(4, D_S5 // 4, D_S5),
             d_conv16.reshape(CONV_ROWS, 4, 3 * D_GDN // 4).transpose(1, 0, 2),
             _pack_small(small).reshape(4, SMALL_QUARTER, LANES)]
    got = swap_halves(slabs)
    q32, q16 = [], []
    for t, (s, g) in enumerate(zip(slabs, got)):
        own = lax.dynamic_index_in_dim(s.reshape(4, 2, s.shape[1] // 2, s.shape[2]), cpos, axis=1, keepdims=False)
        a, b = sum_cores(own, g, name=f"sum_cores{t}")
        q32.append(a)
        q16.append(b)
    rec = scatter_to_chips(q16)
    fs = [sum_chips(lax.dynamic_index_in_dim(q, jchip, axis=0, keepdims=False), r, cpos, name=f"sum_chips{t}")
          for t, (q, r) in enumerate(zip(q32, rec))]
    red = join_halves(fs)
    g_small = _unpack_small(gather_small(red[4][0]).reshape(SMALL_TOTAL, LANES))
    g_shard = {1: g_w_ada, 3: red[0], 18: red[1], 12: red[2], 14: red[3]}

    grads, deltas, new_m, new_v = [None] * 21, [None] * 21, [None] * 21, [None] * 21
    for t, i in enumerate(SHARDED):
        conv, win = i == 14, i == 3
        prep = (lambda a: _conv_rows(a)[None]) if conv else ((lambda a: jnp.transpose(a, (2, 0, 1))) if win else (lambda a: a))
        g = jnp.transpose(g_shard[i], (2, 0, 1)) if win else g_shard[i]
        d, nm, nv = adamw_3d(prep(weights[i]), g, prep(ms[i]), prep(vs[i]), lead=win, name=f"adamw{t}")
        for lst, val in ((grads, g), (deltas, d), (new_m, nm), (new_v, nv)):
            lst[i] = (val[0, :9].reshape(weights[i].shape) if conv else (jnp.transpose(val, (1, 2, 0)) if win else val))
    g_un = {0: g_c_ctx, 2: g_b_ada, **{i: g_small[n] for n, i in enumerate(SMALL)}}
    swapped = [SMALL[n] for n in SMALL_SWAPPED]
    small_in = lambda lst: [_as_2d(swap(lst[i]) if i in swapped else lst[i]) for i in UNSHARDED]
    sm = adamw_small(small_in(weights), [_as_2d(g_un[i]) for i in UNSHARDED], small_in(ms), small_in(vs))
    for n, i in enumerate(UNSHARDED):
        back = ((lambda a: swap(a.reshape(swap(weights[i]).shape))) if i in swapped
                else (lambda a: a.reshape(weights[i].shape)))
        grads[i] = back(g_un[i])
        for lst, res in ((deltas, sm[0]), (new_m, sm[1]), (new_v, sm[2])):
            lst[i] = back(res[n])
    return (loss, grad_x, *grads, *deltas, *new_m, *new_v)
```

```python
import functools

import jax
import jax.numpy as jnp
from jax import lax
from jax.experimental import pallas as pl
from jax.experimental.pallas import tpu as pltpu

f32 = jnp.float32
bf16 = jnp.bfloat16
SDS = jax.ShapeDtypeStruct

D_MODEL = 1024
D_S5 = 512
S5_GROUP = 16
S5_GROUPS = 32
S5_STATE = 64
S5_HALF = S5_GROUPS * S5_STATE
D_GDN = 512
GDN_HEAD = 128
GDN_HEADS = 4
CHUNK = 64
GRID_W = 64
N_DIR = 2
P_IN = 3088
DEEPNORM_ALPHA = 2.0 ** 0.25
LN_EPS = 1e-5
NORM_EPS = 1e-6
ADAM_LR, ADAM_B1, ADAM_B2, ADAM_EPS, ADAM_WD, ADAM_STEP = 0.001, 0.9, 0.999, 1e-08, 0.01, 10

LANES = 128
VMEM_LIMIT = 56 * 1024 * 1024
TOK_TILE = 256
S5_TILE = 256
MESH = pl.DeviceIdType.MESH


def _cparams(n_grid):
    return pltpu.CompilerParams(dimension_semantics=("arbitrary",) * n_grid, vmem_limit_bytes=VMEM_LIMIT)


def _dot(a, b):
    return jnp.dot(a.astype(bf16), b.astype(bf16), preferred_element_type=f32)


def _dot_nt(a, b):
    return lax.dot_general(a.astype(bf16), b.astype(bf16), (((1,), (1,)), ((), ())), preferred_element_type=f32)


def _dot_tn(a, b):
    return lax.dot_general(a.astype(bf16), b.astype(bf16), (((0,), (0,)), ((), ())), preferred_element_type=f32)


def _dot_hi(a, b):
    return jnp.dot(a, b, precision=lax.Precision.HIGHEST, preferred_element_type=f32)


def _dot_h3(a, b):
    return jnp.dot(a, b, precision=lax.Precision.HIGH, preferred_element_type=f32)


@jax.custom_vjp
def _mm(a, b):
    return _dot(a, b)


@jax.custom_vjp
def _mm_nt(a, b):
    return _dot_nt(a, b)


@jax.custom_vjp
def _mm_tn(a, b):
    return _dot_tn(a, b)


_mm.defvjp(lambda a, b: (_dot(a, b), (a, b)), lambda r, g: (_mm_nt(g, r[1]), _mm_tn(r[0], g)))
_mm_nt.defvjp(lambda a, b: (_dot_nt(a, b), (a, b)), lambda r, g: (_mm(g, r[1]), _mm_tn(g, r[0])))
_mm_tn.defvjp(lambda a, b: (_dot_tn(a, b), (a, b)), lambda r, g: (_mm_nt(r[1], g), _mm(r[0], g)))


def _silu(x):
    return x * jax.nn.sigmoid(x)


def _gelu(x):
    return 0.5 * x * (1.0 + lax.erf(x * (2.0 ** -0.5)))


def _resident(shape):
    nd = len(shape)
    return pl.BlockSpec(shape, lambda *_: (0,) * nd, pipeline_mode=pl.Buffered(1))


def _tok(tile, width, nt=None, rev=False):
    if rev:
        return pl.BlockSpec((None, tile, width), lambda b, n: (b, nt - 1 - n, 0))
    return pl.BlockSpec((None, tile, width), lambda b, n: (b, n, 0))


def _per_batch(rows, width):
    return pl.BlockSpec((None, rows, width), lambda b, n: (b, 0, 0))


def _first_step():
    return jnp.logical_and(pl.program_id(0) == 0, pl.program_id(1) == 0)


ADA_SHARD = 3 * D_MODEL // 4
N_DEV = 8


def ada_fwd(cc, w, b):
    def body(cc_ref, w_ref, b_ref, m_ref):
        for k in range(N_DEV):
            m_ref[k] = _dot(_silu(cc_ref[k]), w_ref[...]) + b_ref[...]

    return pl.pallas_call(body, name="ada_fwd", out_shape=SDS((N_DEV, 8, ADA_SHARD), f32),
                          compiler_params=pltpu.CompilerParams(vmem_limit_bytes=VMEM_LIMIT))(cc, w, b)


def ada_bwd(cc, w, dmj):
    def body(cc_ref, w_ref, dmj_ref, dw_ref, pb_ref):
        dw = jnp.zeros((D_MODEL, ADA_SHARD), f32)
        p = jnp.zeros((8, D_MODEL), f32)
        db = jnp.zeros((1, ADA_SHARD), f32)
        for k in range(N_DEV):
            dw = dw + _dot_tn(_silu(cc_ref[k]), dmj_ref[k])
            p = p + _dot_nt(dmj_ref[k], w_ref[...])
            db = db + jnp.sum(dmj_ref[k], axis=0, keepdims=True)
        dw_ref[0] = dw
        pb_ref[...] = jnp.zeros_like(pb_ref)
        pb_ref[0:1, :] = p[2:3, :]
        pb_ref[1:2, 0:ADA_SHARD] = db

    return pl.pallas_call(
        body, name="ada_bwd", out_shape=[SDS((1, D_MODEL, ADA_SHARD), f32), SDS((8, D_MODEL), f32)],
        compiler_params=pltpu.CompilerParams(vmem_limit_bytes=VMEM_LIMIT))(cc, w, dmj)


def c_ctx_bwd(pb_all, c_ctx):
    def body(p_ref, c_ref, d_ref):
        ds = ((p_ref[0, 0:1, :] + p_ref[2, 0:1, :]) + p_ref[4, 0:1, :]) + p_ref[6, 0:1, :]
        _, vjp = jax.vjp(_silu, c_ref[...])
        d_ref[...] = vjp(ds)[0]

    return pl.pallas_call(body, name="c_ctx_bwd", out_shape=SDS((1, D_MODEL), f32))(pb_all, c_ctx)


N_GATE = 2 * N_DIR * GDN_HEADS
IN_WIDTHS = (D_S5, D_S5, 3 * D_GDN, D_GDN, LANES)
IN_OFFS = (0, 512, 1024, 2560, 3072)
IN_PAD = 3200


def in_proj_fwd(x, mod, w, *, name):
    B, L, _ = x.shape
    T = min(TOK_TILE, L)

    def body(x_ref, mod_ref, w_ref, *o_refs):
        h = (x_ref[...] * (1.0 + mod_ref[0:1, :]) + mod_ref[1:2, :]).astype(bf16)
        for o_ref, off, wd in zip(o_refs, IN_OFFS, IN_WIDTHS):
            r = _dot(h, w_ref[:, off:off + wd])
            o_ref[...] = r[:, :o_ref.shape[-1]]

    outw = (D_S5, D_S5, 3 * D_GDN, D_GDN, N_GATE)
    return pl.pallas_call(
        body, name=name, grid=(B, L // T),
        in_specs=[_tok(T, D_MODEL), _per_batch(2, D_MODEL), _resident((D_MODEL, IN_PAD))],
        out_specs=[_tok(T, wd) for wd in outw],
        out_shape=[SDS((B, L, wd), f32) for wd in outw],
        compiler_params=_cparams(2),
    )(x, mod, w)


def in_proj_bwd(x, mod, ds, w, gx_res, dw_start, *, name):
    B, L, _ = x.shape
    T = min(TOK_TILE, L)
    with_dx = gx_res is not None
    with_start = dw_start is not None
    n_u = len(ds[0])

    def body(*refs):
        x_ref, mod_ref = refs[0], refs[1]
        du_refs = refs[2:2 + n_u]
        d_refs = refs[2 + n_u:6 + n_u]
        w_ref = refs[6 + n_u]
        k = 7 + n_u
        if with_dx:
            gx_ref = refs[k]
            k += 1
        if with_start:
            start_ref = refs[k]
            k += 1
        dw_ref, dmod_ref = refs[k], refs[k + 1]
        if with_dx:
            dx_ref = refs[k + 2]
        n = pl.program_id(1)

        @pl.when(_first_step())
        def _():
            dw_ref[...] = start_ref[...] if with_start else jnp.zeros_like(dw_ref)

        @pl.when(n == 0)
        def _():
            dmod_ref[...] = jnp.zeros_like(dmod_ref)

        xv = x_ref[...]
        scale1 = 1.0 + mod_ref[0:1, :]
        h = (xv * scale1 + mod_ref[1:2, :]).astype(bf16)
        du = du_refs[0][...]
        for r in du_refs[1:]:
            du = du + r[...]
        dh = jnp.zeros((T, D_MODEL), f32)
        for dv, off, wd in zip([du] + [r[...] for r in d_refs], IN_OFFS, IN_WIDTHS):
            dv = dv.astype(bf16)
            dh = dh + _dot_nt(dv, w_ref[:, off:off + wd])
            dw_ref[:, off:off + wd] += _dot_tn(h, dv)
        dmod_ref[0:1, :] += jnp.sum(dh * xv, axis=0, keepdims=True)
        dmod_ref[1:2, :] += jnp.sum(dh, axis=0, keepdims=True)
        if with_dx:
            dx_ref[...] = gx_ref[...] + dh * scale1

    in_specs = ([_tok(T, D_MODEL), _per_batch(2, D_MODEL)] + [_tok(T, D_S5)] * n_u + [_tok(T, wd) for wd in IN_WIDTHS[1:]]
                + [_resident((D_MODEL, IN_PAD))])
    args = [x, mod, *ds[0], *ds[1:], w]
    out_specs = [_resident((D_MODEL, IN_PAD)), _per_batch(2, D_MODEL)]
    out_shape = [SDS((D_MODEL, IN_PAD), f32), SDS((B, 2, D_MODEL), f32)]
    if with_dx:
        in_specs.append(_tok(T, D_MODEL))
        args.append(gx_res)
        out_specs.append(_tok(T, D_MODEL))
        out_shape.append(SDS((B, L, D_MODEL), f32))
    if with_start:
        in_specs.append(_resident((D_MODEL, IN_PAD)))
        args.append(dw_start)
    return pl.pallas_call(body, name=name, grid=(B, L // T), in_specs=in_specs, out_specs=out_specs,
                          out_shape=out_shape, compiler_params=_cparams(2))(*args)


def _s5_zoh(lr, li, ldt, bre, bim, expand):
    dt = jnp.exp(ldt)
    zr, zi = lr * dt, li * dt
    e = jnp.exp(zr)
    ar, ai = e * jnp.cos(zi), e * jnp.sin(zi)
    den = lr * lr + li * li
    czr = ((ar - 1.0) * lr + ai * li) / den
    czi = (ai * lr - (ar - 1.0) * li) / den
    czr_e, czi_e = _dot_hi(czr, expand), _dot_hi(czi, expand)
    return ar, ai, czr_e * bre - czi_e * bim, czr_e * bim + czi_e * bre


_ZOH_OUT = [(N_DIR * S5_GROUPS, S5_STATE)] * 2 + [(N_DIR * S5_GROUPS, S5_STATE * S5_GROUP)] * 2


def s5_zoh_fwd(lr, li, ldt, bre, bim, expand):
    def body(lr_ref, li_ref, ldt_ref, bre_ref, bim_ref, e_ref, ar_ref, ai_ref, bbr_ref, bbi_ref):
        ar, ai, bbr, bbi = _s5_zoh(lr_ref[...], li_ref[...], ldt_ref[...], bre_ref[...], bim_ref[...], e_ref[...])
        ar_ref[...], ai_ref[...], bbr_ref[...], bbi_ref[...] = ar, ai, bbr, bbi

    return pl.pallas_call(body, name="s5_zoh_fwd", out_shape=[SDS(s, f32) for s in _ZOH_OUT])(
        lr, li, ldt, bre, bim, expand)


def s5_zoh_bwd(lr, li, ldt, bre, bim, expand, dar, dai, dbbr, dbbi):
    def body(lr_ref, li_ref, ldt_ref, bre_ref, bim_ref, e_ref, dar_ref, dai_ref, dbbr_ref, dbbi_ref,
             dlr_ref, dli_ref, dldt_ref, dbre_ref, dbim_ref):
        ev = e_ref[...]
        _, vjp = jax.vjp(lambda a, b, c, d, e: _s5_zoh(a, b, c, d, e, ev),
                         lr_ref[...], li_ref[...], ldt_ref[...], bre_ref[...], bim_ref[...])
        outs = vjp((dar_ref[...], dai_ref[...], dbbr_ref[...], dbbi_ref[...]))
        dlr_ref[...], dli_ref[...], dldt_ref[...], dbre_ref[...], dbim_ref[...] = outs

    shapes = [lr.shape, li.shape, ldt.shape, bre.shape, bim.shape]
    return pl.pallas_call(body, name="s5_zoh_bwd", out_shape=[SDS(s, f32) for s in shapes])(
        lr, li, ldt, bre, bim, expand, dar, dai, dbbr, dbbi)


def _scan_rows(T, rev, ar, ai, h0s, refs, off):
    def step(i, carry):
        t = off + ((T - 1 - i) if rev else i)
        out = []
        for (hr, hi), (r_ref, i_ref) in zip(carry, refs):
            nr = ar * hr - ai * hi + r_ref[pl.ds(t, 1), :]
            ni = ar * hi + ai * hr + i_ref[pl.ds(t, 1), :]
            r_ref[pl.ds(t, 1), :] = nr
            i_ref[pl.ds(t, 1), :] = ni
            out.append((nr, ni))
        return tuple(out)

    return lax.fori_loop(0, T, step, tuple(h0s))


S5_BLOCKS = 4
S5_BC = D_S5 // S5_BLOCKS
S5_BS = S5_HALF // S5_BLOCKS


def _s5_in(uv, bre_ref, bim_ref, hr_ref, hi_ref, off, T):
    for jb in range(S5_BLOCKS):
        uj = uv[:, jb * S5_BC:(jb + 1) * S5_BC]
        hr_ref[off:off + T, jb * S5_BS:(jb + 1) * S5_BS] = _dot(uj, bre_ref[jb])
        hi_ref[off:off + T, jb * S5_BS:(jb + 1) * S5_BS] = _dot(uj, bim_ref[jb])


def _s5_specs(B, T, nt, rev):
    tidx = (lambda n: nt - 1 - n) if rev else (lambda n: n)
    tok = pl.BlockSpec((B, T, D_S5), lambda n: (0, tidx(n), 0))
    hin = pl.BlockSpec((B, None, 2, S5_HALF), lambda n: (0, tidx(n), 0, 0))
    state = pl.BlockSpec((B, 2, S5_HALF), lambda n: (0, 0, 0))
    return tok, hin, state


def s5_scan_fwd(u, bre, bim, ctop, cbot, arow, h0, *, d, need_y, name):
    B, L, _ = u.shape
    T = min(S5_TILE, L)
    nt = L // T
    rev = d == 1

    def body(u_ref, bre_ref, bim_ref, ct_ref, cb_ref, a_ref, h0_ref, *rest):
        if need_y:
            y_ref, hin_ref, hend_ref, hr_scr, hi_scr, h_scr = rest
        else:
            hin_ref, hend_ref, hr_scr, hi_scr, h_scr = rest
        n = pl.program_id(0)

        @pl.when(n == 0)
        def _():
            h_scr[...] = h0_ref[...]

        hin_ref[...] = h_scr[...]
        for b in range(B):
            _s5_in(u_ref[b].astype(bf16), bre_ref, bim_ref, hr_scr.at[b], hi_scr.at[b], 0, T)
        hs = _scan_rows(T, rev, a_ref[0:1, :], a_ref[1:2, :], [(h_scr[b, 0:1, :], h_scr[b, 1:2, :]) for b in range(B)],
                        [(hr_scr.at[b], hi_scr.at[b]) for b in range(B)], 0)
        for b in range(B):
            h_scr[b, 0:1, :] = hs[b][0]
            h_scr[b, 1:2, :] = hs[b][1]
            if need_y:
                for jb in range(S5_BLOCKS):
                    st = slice(jb * S5_BS, (jb + 1) * S5_BS)
                    y_ref[b, :, jb * S5_BC:(jb + 1) * S5_BC] = (_dot(hr_scr[b, :, st], ct_ref[jb])
                                                                 + _dot(hi_scr[b, :, st], cb_ref[jb]))

        @pl.when(n == nt - 1)
        def _():
            hend_ref[...] = h_scr[...]

    tok, hin_spec, state = _s5_specs(B, T, nt, rev)
    out_specs = [hin_spec, state]
    out_shape = [SDS((B, nt, 2, S5_HALF), f32), SDS((B, 2, S5_HALF), f32)]
    if need_y:
        out_specs.insert(0, tok)
        out_shape.insert(0, SDS((B, L, D_S5), f32))
    w_in, w_out = _resident((S5_BLOCKS, S5_BC, S5_BS)), _resident((S5_BLOCKS, S5_BS, S5_BC))
    return pl.pallas_call(
        body, name=name, grid=(nt,),
        in_specs=[tok, w_in, w_in, w_out, w_out, _resident((2, S5_HALF)), state],
        out_specs=out_specs, out_shape=out_shape,
        scratch_shapes=[pltpu.VMEM((B, T, S5_HALF), f32), pltpu.VMEM((B, T, S5_HALF), f32),
                        pltpu.VMEM((B, 2, S5_HALF), f32)],
        compiler_params=_cparams(1),
    )(u, bre, bim, ctop, cbot, arow, h0)


def s5_scan_bwd(u, dy, bre, bim, ctop, cbot, arow, hin, dhend, *, d, name):
    B, L, _ = u.shape
    T = min(S5_TILE, L)
    nt = L // T
    rev = d == 1
    has_dy = dy is not None
    PAD = 8

    def body(*refs):
        u_ref = refs[0]
        k = 1
        if has_dy:
            dy_ref = refs[1]
            k = 2
        bre_ref, bim_ref, ct_ref, cb_ref, a_ref, hin_ref, dhend_ref = refs[k:k + 7]
        du_ref, dbre_ref, dbim_ref, dct_ref, dcb_ref, da_ref, dh0_ref = refs[k + 7:k + 14]
        hr_scr, hi_scr, gr_scr, gi_scr, p_scr = refs[k + 14:]
        n = pl.program_id(0)

        @pl.when(n == 0)
        def _():
            for r in (dbre_ref, dbim_ref, dct_ref, dcb_ref, da_ref):
                r[...] = jnp.zeros_like(r)
            p_scr[...] = dhend_ref[...]

        ar, ai = a_ref[0:1, :], a_ref[1:2, :]
        prev_row = PAD + T if rev else PAD - 1
        uvs = []
        for b in range(B):
            uvs.append(u_ref[b].astype(bf16))
            _s5_in(uvs[b], bre_ref, bim_ref, hr_scr.at[b], hi_scr.at[b], PAD, T)
            hr_scr[b, prev_row:prev_row + 1, :] = hin_ref[b, 0:1, :]
            hi_scr[b, prev_row:prev_row + 1, :] = hin_ref[b, 1:2, :]
        _scan_rows(T, rev, ar, ai, [(hin_ref[b, 0:1, :], hin_ref[b, 1:2, :]) for b in range(B)],
                   [(hr_scr.at[b], hi_scr.at[b]) for b in range(B)], PAD)
        if has_dy:
            for b in range(B):
                dyv = dy_ref[b].astype(bf16)
                for jb in range(S5_BLOCKS):
                    st = slice(jb * S5_BS, (jb + 1) * S5_BS)
                    dyj = dyv[:, jb * S5_BC:(jb + 1) * S5_BC]
                    gr_scr[b, :, st] = _dot_nt(dyj, ct_ref[jb])
                    gi_scr[b, :, st] = _dot_nt(dyj, cb_ref[jb])
                    dct_ref[jb] += _dot_tn(hr_scr[b, PAD:PAD + T, st], dyj)
                    dcb_ref[jb] += _dot_tn(hi_scr[b, PAD:PAD + T, st], dyj)
        else:
            gr_scr[...] = jnp.zeros_like(gr_scr)
            gi_scr[...] = jnp.zeros_like(gi_scr)

        def step(i, carry):
            t = i if rev else T - 1 - i
            tp = PAD + t + (1 if rev else -1)
            out = []
            for b, (pr, pi, dar, dai) in enumerate(carry):
                gr = gr_scr[b, pl.ds(t, 1), :] + pr
                gi = gi_scr[b, pl.ds(t, 1), :] + pi
                gr_scr[b, pl.ds(t, 1), :] = gr
                gi_scr[b, pl.ds(t, 1), :] = gi
                hpr = hr_scr[b, pl.ds(tp, 1), :]
                hpi = hi_scr[b, pl.ds(tp, 1), :]
                out.append((ar * gr + ai * gi, ar * gi - ai * gr, dar + hpr * gr + hpi * gi, dai + hpr * gi - hpi * gr))
            return tuple(out)

        zero = jnp.zeros((1, S5_HALF), f32)
        res = lax.fori_loop(0, T, step, tuple((p_scr[b, 0:1, :], p_scr[b, 1:2, :], zero, zero) for b in range(B)))
        for b in range(B):
            pr, pi, dar, dai = res[b]
            p_scr[b, 0:1, :] = pr
            p_scr[b, 1:2, :] = pi
            da_ref[0:1, :] += dar
            da_ref[1:2, :] += dai
            for jb in range(S5_BLOCKS):
                st = slice(jb * S5_BS, (jb + 1) * S5_BS)
                ch = slice(jb * S5_BC, (jb + 1) * S5_BC)
                gr_j = gr_scr[b, :, st].astype(bf16)
                gi_j = gi_scr[b, :, st].astype(bf16)
                du_ref[b, :, ch] = _dot_nt(gr_j, bre_ref[jb]) + _dot_nt(gi_j, bim_ref[jb])
                dbre_ref[jb] += _dot_tn(uvs[b][:, ch], gr_j)
                dbim_ref[jb] += _dot_tn(uvs[b][:, ch], gi_j)

        @pl.when(n == nt - 1)
        def _():
            dh0_ref[...] = p_scr[...]

    tok, hin_spec, state = _s5_specs(B, T, nt, not rev)
    w_in, w_out = _resident((S5_BLOCKS, S5_BC, S5_BS)), _resident((S5_BLOCKS, S5_BS, S5_BC))
    wspecs = [w_in, w_in, w_out, w_out]
    in_specs = [tok] + ([tok] if has_dy else []) + wspecs + [_resident((2, S5_HALF)), hin_spec, state]
    args = [u] + ([dy] if has_dy else []) + [bre, bim, ctop, cbot, arow, hin, dhend]
    return pl.pallas_call(
        body, name=name, grid=(nt,), in_specs=in_specs,
        out_specs=[tok] + wspecs + [_resident((2, S5_HALF)), state],
        out_shape=[SDS((B, L, D_S5), f32), SDS((S5_BLOCKS, S5_BC, S5_BS), f32), SDS((S5_BLOCKS, S5_BC, S5_BS), f32),
                   SDS((S5_BLOCKS, S5_BS, S5_BC), f32), SDS((S5_BLOCKS, S5_BS, S5_BC), f32), SDS((2, S5_HALF), f32),
                   SDS((B, 2, S5_HALF), f32)],
        scratch_shapes=[pltpu.VMEM((B, T + 2 * PAD, S5_HALF), f32), pltpu.VMEM((B, T + 2 * PAD, S5_HALF), f32),
                        pltpu.VMEM((B, T, S5_HALF), f32), pltpu.VMEM((B, T, S5_HALF), f32),
                        pltpu.VMEM((B, 2, S5_HALF), f32)],
        compiler_params=_cparams(1),
    )(*args)


def _glu_fn(u, y0, y1, z, dsk, wg, bg):
    g = _gelu(dsk * u + y0 + y1)
    return g * jax.nn.sigmoid(_mm(g, wg) + bg) * _silu(z)


CONV_ROWS = 16


def _shift(x, s):
    L = x.shape[0]
    k = (-s) % L
    return x if k == 0 else pltpu.roll(x, k, axis=0)


def _r16(v):
    return v.astype(bf16).astype(f32)


def _conv_masks(L, is_ctx):
    t = lax.broadcasted_iota(jnp.int32, (L, 1), 0)
    if is_ctx:
        return t == L - 1, t == 0, None, None
    col = jnp.bitwise_and(t, GRID_W - 1)
    return col == GRID_W - 1, col == 0, t >= GRID_W, t < L - GRID_W


def _conv_sides(xv, masks):
    no_left, no_right, _, _ = masks
    return _shift(jnp.where(no_left, 0.0, xv), -1), _shift(jnp.where(no_right, 0.0, xv), 1)


def _conv_pre(xv, w_ref, masks, is_ctx):
    xv = _r16(xv)
    wv = _r16(w_ref[...])
    xl, xr = _conv_sides(xv, masks)
    z = [wv[3 * di:3 * di + 1, :] * xl + wv[3 * di + 1:3 * di + 2, :] * xv + wv[3 * di + 2:3 * di + 3, :] * xr
         for di in ((1,) if is_ctx else (0, 1, 2))]
    if is_ctx:
        return z[0]
    _, _, has_up, has_down = masks
    return z[1] + jnp.where(has_up, _shift(z[0], -GRID_W), 0.0) + jnp.where(has_down, _shift(z[2], GRID_W), 0.0)


def _conv_pre_bwd(xv, w_ref, dpre, masks, is_ctx, dw_ref):
    no_left, no_right, has_up, has_down = masks
    xv, dpre, wv = _r16(xv), _r16(dpre), _r16(w_ref[...])
    xl, xr = _conv_sides(xv, masks)
    if is_ctx:
        dz = {1: dpre}
    else:
        dz = {0: _shift(jnp.where(has_up, dpre, 0.0), GRID_W), 1: dpre, 2: _shift(jnp.where(has_down, dpre, 0.0), -GRID_W)}
    dxl = dxc = dxr = None
    for di, d in dz.items():
        for dj, side in enumerate((xl, xv, xr)):
            dw_ref[3 * di + dj:3 * di + dj + 1, :] = jnp.sum(d * side, axis=0, keepdims=True)
        tl, tc, tr = (wv[3 * di + dj:3 * di + dj + 1, :] * d for dj in range(3))
        dxl, dxc, dxr = (tl, tc, tr) if dxl is None else (dxl + tl, dxc + tc, dxr + tr)
    return dxc + jnp.where(no_left, 0.0, _shift(dxl, 1)) + jnp.where(no_right, 0.0, _shift(dxr, -1))


def _qk_post(pre, is_norm, scale):
    s = _silu(pre)
    nrm = lax.rsqrt(jnp.sum(s * s, axis=-1, keepdims=True) + NORM_EPS)
    return s * jnp.where(is_norm, nrm * scale, 1.0)


def _conv_kind():
    ct = pl.program_id(1)
    return ct < 2 * GDN_HEADS, jnp.where(ct < GDN_HEADS, GDN_HEAD ** -0.5, 1.0).astype(f32)


def conv_fwd(qkv, w16, *, is_ctx, name):
    B, L, C = qkv.shape
    spec = pl.BlockSpec((None, L, GDN_HEAD), lambda b, ct: (b, 0, ct))
    wspec = pl.BlockSpec((CONV_ROWS, GDN_HEAD), lambda b, ct: (0, ct))

    def body(x_ref, w_ref, o_ref, pre_ref):
        is_norm, scale = _conv_kind()
        pre = _conv_pre(x_ref[...], w_ref, _conv_masks(L, is_ctx), is_ctx)
        pre_ref[...] = pre
        o_ref[...] = _qk_post(pre, is_norm, scale)

    return pl.pallas_call(body, name=name, grid=(B, C // GDN_HEAD), in_specs=[spec, wspec], out_specs=[spec, spec],
                          out_shape=[SDS((B, L, C), f32)] * 2, compiler_params=_cparams(2))(qkv, w16)


def conv_bwd(qkv, pre, w16, da0, da1, *, is_ctx, name):
    B, L, C = qkv.shape
    spec = pl.BlockSpec((None, L, GDN_HEAD), lambda b, ct: (b, 0, ct))
    wspec = pl.BlockSpec((CONV_ROWS, GDN_HEAD), lambda b, ct: (0, ct))
    dwspec = pl.BlockSpec((None, CONV_ROWS, GDN_HEAD), lambda b, ct: (b, 0, ct))

    def body(x_ref, pre_ref, w_ref, d0_ref, d1_ref, dx_ref, dw_ref):
        is_norm, scale = _conv_kind()
        _, vjp = jax.vjp(lambda p: _qk_post(p, is_norm, scale), pre_ref[...])
        dpre = vjp(d0_ref[...] + d1_ref[...])[0]
        dw_ref[...] = jnp.zeros_like(dw_ref)
        dx_ref[...] = _conv_pre_bwd(x_ref[...], w_ref, dpre, _conv_masks(L, is_ctx), is_ctx, dw_ref)

    return pl.pallas_call(body, name=name, grid=(B, C // GDN_HEAD), in_specs=[spec, spec, wspec, spec, spec],
                          out_specs=[spec, dwspec], out_shape=[SDS((B, L, C), f32), SDS((B, CONV_ROWS, C), f32)],
                          compiler_params=_cparams(2))(qkv, pre, w16, da0, da1)


def _gates_fn(ba, alog, dtb):
    T = ba.shape[0]
    lane = lax.broadcasted_iota(jnp.int32, ba.shape, 1)
    ii = lax.broadcasted_iota(jnp.int32, (T, T), 0)
    jj = lax.broadcasted_iota(jnp.int32, (T, T), 1)
    same = jnp.right_shift(ii, 6) == jnp.right_shift(jj, 6)
    lmat = jnp.logical_and(same, ii >= jj).astype(f32)
    umat = jnp.logical_and(same, ii <= jj).astype(f32)
    g = jnp.where(lane >= 8, -jnp.exp(alog) * jax.nn.softplus(ba + dtb), 0.0)
    gc = jnp.where(lane >= 12, _dot_hi(umat, g), _dot_hi(lmat, g))
    return jnp.where(lane < 8, jax.nn.sigmoid(ba), gc)


def gates_fwd(ba, alog, dtb, *, name):
    B, L, _ = ba.shape
    T = min(TOK_TILE, L)
    t = _tok(T, N_GATE)

    def body(ba_ref, al_ref, dt_ref, o_ref):
        o_ref[...] = _gates_fn(ba_ref[...], al_ref[...], dt_ref[...])

    return pl.pallas_call(body, name=name, grid=(B, L // T),
                          in_specs=[t, _resident((1, N_GATE)), _resident((1, N_GATE))], out_specs=t,
                          out_shape=SDS((B, L, N_GATE), f32), compiler_params=_cparams(2))(ba, alog, dtb)


def gates_bwd(ba, alog, dtb, dbg, *, name):
    B, L, _ = ba.shape
    T = min(TOK_TILE, L)
    t = _tok(T, N_GATE)
    small = _resident((1, N_GATE))

    def body(ba_ref, al_ref, dt_ref, d_ref, dba_ref, dal_ref, ddt_ref):
        @pl.when(_first_step())
        def _():
            dal_ref[...] = jnp.zeros_like(dal_ref)
            ddt_ref[...] = jnp.zeros_like(ddt_ref)

        _, vjp = jax.vjp(_gates_fn, ba_ref[...], al_ref[...], dt_ref[...])
        dba, dal, ddt = vjp(d_ref[...])
        dba_ref[...] = dba
        dal_ref[...] += dal
        ddt_ref[...] += ddt

    return pl.pallas_call(body, name=name, grid=(B, L // T), in_specs=[t, small, small, t],
                          out_specs=[t, small, small],
                          out_shape=[SDS((B, L, N_GATE), f32), SDS((1, N_GATE), f32), SDS((1, N_GATE), f32)],
                          compiler_params=_cparams(2))(ba, alog, dtb, dbg)


@jax.custom_vjp
def _inv_unit_tri(mats):
    n = mats[0].shape[0]
    eye = (lax.broadcasted_iota(jnp.int32, (n, n), 0) == lax.broadcasted_iota(jnp.int32, (n, n), 1)).astype(f32)
    xs = [eye - a for a in mats]
    sq = [_dot(a, a) for a in mats]
    ps = sq
    k = 2
    while k < n:
        xs = [x + _dot(x, p) for x, p in zip(xs, ps)]
        k *= 2
        if k < n:
            ps = [_dot(p, p) for p in ps]
    return tuple(_dot(p, x) - a for p, x, a in zip(sq, xs, mats))


def _inv_unit_tri_fwd(mats):
    ns = _inv_unit_tri(mats)
    return ns, ns


def _inv_unit_tri_bwd(ns, dns):
    ys = [dn + _dot_tn(nn, dn) for nn, dn in zip(ns, dns)]
    return (tuple(-(y + _dot_nt(y, nn)) for y, nn in zip(ys, ns)),)


_inv_unit_tri.defvjp(_inv_unit_tri_fwd, _inv_unit_tri_bwd)


@jax.custom_vjp
def _inv_unit_tri_saved(mats, saved):
    return saved


_inv_unit_tri_saved.defvjp(lambda mats, saved: (saved, saved),
                           lambda ns, dns: _inv_unit_tri_bwd(ns, dns) + (tuple(jnp.zeros_like(n) for n in ns),))


def _gdn_chunk(heads, *, revs, saved=None, with_n=False):
    n = heads[0][0].shape[0]
    ii = lax.broadcasted_iota(jnp.int32, (n, n), 0)
    jj = lax.broadcasted_iota(jnp.int32, (n, n), 1)
    row = lax.broadcasted_iota(jnp.int32, (n, 1), 0)
    lower = {False: ii >= jj, True: ii <= jj}
    strict = {False: ii > jj, True: ii < jj}
    last = {False: n - 1, True: 0}
    H = range(len(heads))
    q, k, v, beta, gc, gr, s = (list(t) for t in zip(*heads))
    decay = [jnp.where(lower[revs[h]], jnp.exp(jnp.where(lower[revs[h]], gc[h] - gr[h], 0.0)), 0.0) for h in H]
    kk = [_mm_nt(k[h], k[h]) for h in H]
    qk = [_mm_nt(q[h], k[h]) * decay[h] for h in H]
    qs = [_mm(q[h], s[h]) for h in H]
    a_mat = tuple(jnp.where(strict[revs[h]], beta[h] * kk[h] * decay[h], 0.0) for h in H)
    gamma = [jnp.exp(gc[h]) for h in H]
    g_last = [jnp.sum(jnp.where(row == last[revs[h]], gc[h], 0.0), axis=0, keepdims=True) for h in H]
    nmat = _inv_unit_tri(a_mat) if saved is None else _inv_unit_tri_saved(a_mat, saved)
    bv = [beta[h] * v[h] for h in H]
    bk = [(beta[h] * gamma[h]) * k[h] for h in H]
    u0 = [bv[h] + _mm(nmat[h], bv[h]) for h in H]
    w = [bk[h] + _mm(nmat[h], bk[h]) for h in H]
    k_out = [k[h] * jnp.exp(g_last[h] - gc[h]) for h in H]
    u = [u0[h] - _mm(w[h], s[h]) for h in H]
    o = [gamma[h] * qs[h] + _mm(qk[h], u[h]) for h in H]
    s_new = [jnp.exp(g_last[h]) * s[h] + _mm_tn(k_out[h], u[h]) for h in H]
    outs = tuple((o[h], s_new[h]) for h in H)
    return (outs, nmat) if with_n else outs


def _gdn_specs(B, nc, rev):
    def cidx(n):
        return (nc - 1 - n) if rev else n
    tok = lambda width: pl.BlockSpec((B, CHUNK, width), lambda n: (0, cidx(n), 0))
    rowspec = pl.BlockSpec((B, None, N_GATE, CHUNK), lambda n: (0, cidx(n), 0, 0))
    st = pl.BlockSpec((B, GDN_HEADS, GDN_HEAD, GDN_HEAD), lambda n: (0, 0, 0, 0))
    ck = pl.BlockSpec((B, None, GDN_HEADS, GDN_HEAD, GDN_HEAD), lambda n: (0, cidx(n), 0, 0, 0))
    nsp = pl.BlockSpec((B, None, GDN_HEADS, CHUNK, CHUNK), lambda n: (0, cidx(n), 0, 0, 0))
    return tok, rowspec, st, ck, nsp


def _gdn_head_args(qkv_ref, bg_ref, bgr_ref, b, d, h):
    col = d * GDN_HEADS + h
    q = qkv_ref[b, :, h * GDN_HEAD:(h + 1) * GDN_HEAD]
    k = qkv_ref[b, :, D_GDN + h * GDN_HEAD:D_GDN + (h + 1) * GDN_HEAD]
    v = qkv_ref[b, :, 2 * D_GDN + h * GDN_HEAD:2 * D_GDN + (h + 1) * GDN_HEAD]
    bgv = bg_ref[b]
    return q, k, v, bgv[:, col:col + 1], bgv[:, 8 + col:9 + col], bgr_ref[b][8 + col:9 + col, :]


def _gdn_chains(B):
    return [(d, b, h) for d in range(N_DIR) for b in range(B) for h in range(GDN_HEADS)]


def gdn_fwd(qkv, bg, bgr, s0s, *, need_o, name):
    B, L, _ = qkv.shape
    nc = L // CHUNK
    specs = [_gdn_specs(B, nc, d == 1) for d in range(N_DIR)]
    chains = _gdn_chains(B)
    state_shape = (B, GDN_HEADS, GDN_HEAD, GDN_HEAD)

    def body(*refs):
        ins = [refs[3 * d:3 * d + 3] for d in range(N_DIR)]
        s0_refs = refs[6:8]
        k = 8
        o_refs = refs[k:k + 2] if need_o else None
        k += 2 if need_o else 0
        ck_refs, n_refs, sf_refs, s_scrs = refs[k:k + 2], refs[k + 2:k + 4], refs[k + 4:k + 6], refs[k + 6:k + 8]
        n = pl.program_id(0)

        @pl.when(n == 0)
        def _():
            for d in range(N_DIR):
                s_scrs[d][...] = s0_refs[d][...]

        for d in range(N_DIR):
            ck_refs[d][...] = s_scrs[d][...]
        heads = tuple(_gdn_head_args(*ins[d], b, d, h) + (s_scrs[d][b, h],) for d, b, h in chains)
        outs, nmat = _gdn_chunk(heads, revs=tuple(d == 1 for d, _, _ in chains), with_n=True)
        for (d, b, h), (o, s_new), nn in zip(chains, outs, nmat):
            if need_o:
                o_refs[d][b, :, h * GDN_HEAD:(h + 1) * GDN_HEAD] = o
            s_scrs[d][b, h] = s_new
            n_refs[d][b, h] = nn

        @pl.when(n == nc - 1)
        def _():
            for d in range(N_DIR):
                sf_refs[d][...] = s_scrs[d][...]

    in_specs, out_o, out_ck, out_n, out_sf = [], [], [], [], []
    for tok, rowspec, st, ck, nsp in specs:
        in_specs += [tok(3 * D_GDN), tok(N_GATE), rowspec]
        out_o.append(tok(D_GDN))
        out_ck.append(ck)
        out_n.append(nsp)
        out_sf.append(st)
    in_specs += [specs[0][2]] * 2
    out_specs = (out_o if need_o else []) + out_ck + out_n + out_sf
    out_shape = (([SDS((B, L, D_GDN), f32)] * 2 if need_o else []) + [SDS((B, nc) + state_shape[1:], f32)] * 2
                 + [SDS((B, nc, GDN_HEADS, CHUNK, CHUNK), f32)] * 2 + [SDS(state_shape, f32)] * 2)
    res = pl.pallas_call(
        body, name=name, grid=(nc,), in_specs=in_specs, out_specs=out_specs, out_shape=out_shape,
        scratch_shapes=[pltpu.VMEM(state_shape, f32)] * 2, compiler_params=_cparams(1),
    )(qkv, bg, bgr, qkv, bg, bgr, *s0s)
    if need_o:
        return res[0:2], res[2:4], res[4:6], res[6:8]
    return res[0:2], res[2:4], res[4:6]


def gdn_bwd(qkv, bg, bgr, cks, ns, do, dsfs, *, name):
    B, L, _ = qkv.shape
    nc = L // CHUNK
    has_do = do is not None
    specs = [_gdn_specs(B, nc, d != 1) for d in range(N_DIR)]
    chains = _gdn_chains(B)
    state_shape = (B, GDN_HEADS, GDN_HEAD, GDN_HEAD)
    per_dir = 6 if has_do else 5

    def body(*refs):
        ins = [refs[per_dir * d:per_dir * d + per_dir] for d in range(N_DIR)]
        k = per_dir * N_DIR
        dsf_refs = refs[k:k + 2]
        outs = [refs[k + 2 + 3 * d:k + 5 + 3 * d] for d in range(N_DIR)]
        ds0_refs, ds_scrs = refs[k + 8:k + 10], refs[k + 10:k + 12]
        n = pl.program_id(0)

        @pl.when(n == 0)
        def _():
            for d in range(N_DIR):
                ds_scrs[d][...] = dsf_refs[d][...]

        lane = lax.broadcasted_iota(jnp.int32, (CHUNK, N_GATE), 1)
        sub = lax.broadcasted_iota(jnp.int32, (N_GATE, CHUNK), 0)
        heads = tuple(_gdn_head_args(*ins[d][:3], b, d, h) + (ins[d][3][b, h],) for d, b, h in chains)
        saved = tuple(ins[d][4][b, h] for d, b, h in chains)
        _, vjp = jax.vjp(functools.partial(_gdn_chunk, revs=tuple(d == 1 for d, _, _ in chains), saved=saved), heads)
        zero = jnp.zeros((CHUNK, GDN_HEAD), f32)
        cts = tuple(((ins[d][5][b, :, h * GDN_HEAD:(h + 1) * GDN_HEAD] if has_do else zero), ds_scrs[d][b, h])
                    for d, b, h in chains)
        (dheads,) = vjp(cts)
        dbg_acc = [[jnp.zeros((CHUNK, N_GATE), f32) for _ in range(B)] for _ in range(N_DIR)]
        dbgr_acc = [[jnp.zeros((N_GATE, CHUNK), f32) for _ in range(B)] for _ in range(N_DIR)]
        for (d, b, h), (dq, dk, dv, db, dgc, dgr, ds) in zip(chains, dheads):
            col = d * GDN_HEADS + h
            dqkv_ref = outs[d][0]
            dqkv_ref[b, :, h * GDN_HEAD:(h + 1) * GDN_HEAD] = dq
            dqkv_ref[b, :, D_GDN + h * GDN_HEAD:D_GDN + (h + 1) * GDN_HEAD] = dk
            dqkv_ref[b, :, 2 * D_GDN + h * GDN_HEAD:2 * D_GDN + (h + 1) * GDN_HEAD] = dv
            dbg_acc[d][b] = dbg_acc[d][b] + jnp.where(lane == col, db, 0.0) + jnp.where(lane == 8 + col, dgc, 0.0)
            dbgr_acc[d][b] = dbgr_acc[d][b] + jnp.where(sub == 8 + col, dgr, 0.0)
            ds_scrs[d][b, h] = ds
        for d in range(N_DIR):
            for b in range(B):
                outs[d][1][b] = dbg_acc[d][b]
                outs[d][2][b] = dbgr_acc[d][b]

        @pl.when(n == nc - 1)
        def _():
            for d in range(N_DIR):
                ds0_refs[d][...] = ds_scrs[d][...]

    in_specs, args, out_specs, out_shape = [], [], [], []
    for d, (tok, rowspec, st, ck, nsp) in enumerate(specs):
        in_specs += [tok(3 * D_GDN), tok(N_GATE), rowspec, ck, nsp] + ([tok(D_GDN)] if has_do else [])
        args += [qkv, bg, bgr, cks[d], ns[d]] + ([do] if has_do else [])
        out_specs += [tok(3 * D_GDN), tok(N_GATE), rowspec]
        out_shape += [SDS((B, L, 3 * D_GDN), f32), SDS((B, L, N_GATE), f32), SDS((B, nc, N_GATE, CHUNK), f32)]
    st = specs[0][2]
    in_specs += [st, st]
    args += list(dsfs)
    out_specs += [st, st]
    out_shape += [SDS(state_shape, f32)] * 2
    res = pl.pallas_call(
        body, name=name, grid=(nc,), in_specs=in_specs, out_specs=out_specs, out_shape=out_shape,
        scratch_shapes=[pltpu.VMEM(state_shape, f32)] * 2, compiler_params=_cparams(1),
    )(*args)
    return (res[0], res[3]), (res[1], res[4]), (res[2], res[5]), (res[6], res[7])


def _gnorm_fn(o0, o1, z, w):
    o = o0 + o1
    return o * lax.rsqrt(jnp.mean(o * o, axis=-1, keepdims=True) + NORM_EPS) * w * _silu(z)


def _head_loss(y, x, gate, lng, lnb, tgt):
    r = DEEPNORM_ALPHA * x + gate * y
    mu = jnp.mean(r, axis=-1, keepdims=True)
    rc = r - mu
    var = jnp.mean(rc * rc, axis=-1, keepdims=True)
    err = rc * lax.rsqrt(var + LN_EPS) * lng + lnb - tgt
    return (0.5 / D_MODEL) * jnp.sum(jnp.sum(err * err, axis=-1, keepdims=True), axis=0, keepdims=True)


def tail_fwd_bwd(u, y0, y1, z_s5, o0, o1, z_gdn, x, tgt, gate, lng, lnb, ws, wg, dsk, wglu, bglu, nw):
    B, L, _ = x.shape
    T = min(TOK_TILE, L)

    def body(u_ref, y0_ref, y1_ref, z_ref, o0_ref, o1_ref, zg_ref, x_ref, t_ref, gate_ref, lng_ref, lnb_ref, ws_ref,
             wg_ref, dsk_ref, wglu_ref, bglu_ref, nw_ref,
             loss_ref, du_ref, dys_ref, dz_ref, do_ref, dzg_ref, gx_ref, dws_ref, dwg_ref, dgate_ref, dlng_ref, dlnb_ref,
             ddsk_ref, dwglu_ref, dbglu_ref, dnw_ref):
        n = pl.program_id(1)

        @pl.when(_first_step())
        def _():
            for r in (dws_ref, dwg_ref, dlng_ref, dlnb_ref, ddsk_ref, dwglu_ref, dbglu_ref, dnw_ref):
                r[...] = jnp.zeros_like(r)

        @pl.when(n == 0)
        def _():
            loss_ref[...] = jnp.zeros_like(loss_ref)
            dgate_ref[...] = jnp.zeros_like(dgate_ref)

        s5o, glu_vjp = jax.vjp(_glu_fn, u_ref[...], y0_ref[...], y1_ref[...], z_ref[...], dsk_ref[...],
                               wglu_ref[...].astype(f32), bglu_ref[...])
        heads = []
        for h in range(GDN_HEADS):
            sl = slice(h * GDN_HEAD, (h + 1) * GDN_HEAD)
            heads.append(jax.vjp(_gnorm_fn, o0_ref[:, sl], o1_ref[:, sl], zg_ref[:, sl], nw_ref[...]))
        sv = s5o.astype(bf16)
        gv = jnp.concatenate([out for out, _ in heads], axis=1).astype(bf16)
        y = _dot(sv, ws_ref[...]) + _dot(gv, wg_ref[...])
        loss, vjp = jax.vjp(lambda *a: _head_loss(*a, t_ref[...]), y, x_ref[...], gate_ref[...], lng_ref[...],
                            lnb_ref[...])
        dy, dx, dgate, dlng, dlnb = vjp(jnp.ones((1, 1), f32))
        loss_ref[...] += jnp.broadcast_to(loss, loss_ref.shape)
        dyb = dy.astype(bf16)
        gx_ref[...] = dx
        dws_ref[...] += _dot_tn(sv, dyb)
        dwg_ref[...] += _dot_tn(gv, dyb)
        dgate_ref[...] += dgate
        dlng_ref[...] += dlng
        dlnb_ref[...] += dlnb
        du, dys, _, dz, ddsk, dwglu, dbglu = glu_vjp(_dot_nt(dyb, ws_ref[...]))
        du_ref[...], dys_ref[...], dz_ref[...] = du, dys, dz
        ddsk_ref[...] += ddsk
        dwglu_ref[...] += dwglu
        dbglu_ref[...] += dbglu
        dgdo = _dot_nt(dyb, wg_ref[...])
        for h, (_, hvjp) in enumerate(heads):
            sl = slice(h * GDN_HEAD, (h + 1) * GDN_HEAD)
            do, _, dzg, dnw = hvjp(dgdo[:, sl])
            do_ref[:, sl] = do
            dzg_ref[:, sl] = dzg
            dnw_ref[...] += dnw

    half, full = _tok(T, D_S5), _tok(T, D_MODEL)
    row = _resident((1, D_MODEL))
    wsp = _resident((D_S5, D_MODEL))
    r512, rglu, r128 = _resident((1, D_S5)), _resident((D_S5, D_S5)), _resident((1, GDN_HEAD))
    return pl.pallas_call(
        body, name="tail_fwd_bwd", grid=(B, L // T),
        in_specs=[half] * 7 + [full, full, _per_batch(1, D_MODEL), row, row, wsp, wsp, r512, rglu, r512, r128],
        out_specs=[_per_batch(8, LANES)] + [half] * 5 + [full, wsp, wsp, _per_batch(1, D_MODEL), row, row, r512, rglu, r512,
                                                           r128],
        out_shape=[SDS((B, 8, LANES), f32)] + [SDS((B, L, D_S5), f32)] * 5 + [
            SDS((B, L, D_MODEL), f32), SDS((D_S5, D_MODEL), f32), SDS((D_GDN, D_MODEL), f32), SDS((B, 1, D_MODEL), f32),
            SDS((1, D_MODEL), f32), SDS((1, D_MODEL), f32), SDS((1, D_S5), f32), SDS((D_S5, D_S5), f32), SDS((1, D_S5), f32),
            SDS((1, GDN_HEAD), f32)],
        compiler_params=_cparams(2),
    )(u, y0, y1, z_s5, o0, o1, z_gdn, x, tgt, gate, lng, lnb, ws, wg, dsk, wglu, bglu, nw)


def _adamw_math(w, g, m, v):
    nm = ADAM_B1 * m + (1.0 - ADAM_B1) * g
    nv = ADAM_B2 * v + (1.0 - ADAM_B2) * jnp.square(g)
    m_hat = nm / (1.0 - ADAM_B1 ** ADAM_STEP)
    v_hat = nv / (1.0 - ADAM_B2 ** ADAM_STEP)
    return -ADAM_LR * (m_hat / (jnp.sqrt(v_hat) + ADAM_EPS) + ADAM_WD * w), nm, nv


def _row_tile(rows, cap=512):
    for t in range(min(cap, rows), 15, -1):
        if rows % t == 0 and t % 16 == 0:
            return t
    return rows


def adamw_3d(w, g, m, v, *, lead=False, name):
    R, C = (w.shape[0], w.shape[2]) if lead else w.shape[1:]
    if lead:
        T = next(t for t in range(min(256, R), 0, -1) if R % t == 0)
        spec = pl.BlockSpec((T, 1, C), lambda i: (i, 0, 0))
    else:
        T = _row_tile(R)
        spec = pl.BlockSpec((None, T, C), lambda i: (0, i, 0))

    def body(w_ref, g_ref, m_ref, v_ref, d_ref, nm_ref, nv_ref):
        d_ref[...], nm_ref[...], nv_ref[...] = _adamw_math(w_ref[...], g_ref[...], m_ref[...], v_ref[...])

    return pl.pallas_call(body, name=name, grid=(R // T,), in_specs=[spec] * 4, out_specs=[spec] * 3,
                          out_shape=[SDS(w.shape, f32)] * 3, compiler_params=_cparams(1))(w, g, m, v)


def adamw_small(ws, gs, ms, vs):
    n = len(ws)

    def body(*refs):
        outs = refs[4 * n:]
        for i in range(n):
            d, nm, nv = _adamw_math(refs[i][...], refs[n + i][...], refs[2 * n + i][...], refs[3 * n + i][...])
            outs[i][...], outs[n + i][...], outs[2 * n + i][...] = d, nm, nv

    res = pl.pallas_call(body, name="adamw_small", out_shape=[SDS(w.shape, f32) for w in ws] * 3,
                         compiler_params=pltpu.CompilerParams(vmem_limit_bytes=VMEM_LIMIT))(*ws, *gs, *ms, *vs)
    return res[:n], res[n:2 * n], res[2 * n:]


def sum_cores(own, got, *, name):
    A, H, C = own.shape
    T = _row_tile(H)
    spec = pl.BlockSpec((None, T, C), lambda a, i: (a, i, 0))

    def body(a_ref, b_ref, q32_ref, q16_ref):
        q = a_ref[...] + b_ref[...]
        q32_ref[...] = q
        q16_ref[...] = q.astype(bf16)

    return pl.pallas_call(body, name=name, grid=(A, H // T), in_specs=[spec, spec], out_specs=[spec, spec],
                          out_shape=[SDS((A, H, C), f32), SDS((A, H, C), bf16)], compiler_params=_cparams(2))(own, got)


def sum_chips(mine, rec, cpos, *, name):
    H, C = mine.shape
    T = _row_tile(H)
    nt = H // T

    def body(c_ref, m_ref, r_ref, f_ref):
        f_ref[...] = ((m_ref[...] + r_ref[0].astype(f32)) + r_ref[1].astype(f32)) + r_ref[2].astype(f32)

    grid_spec = pltpu.PrefetchScalarGridSpec(
        num_scalar_prefetch=1, grid=(nt,),
        in_specs=[pl.BlockSpec((T, C), lambda i, c_ref: (i, 0)), pl.BlockSpec((3, T, C), lambda i, c_ref: (0, i, 0))],
        out_specs=pl.BlockSpec((None, T, C), lambda i, c_ref: (0, c_ref[0] * nt + i, 0)))
    return pl.pallas_call(body, name=name, grid_spec=grid_spec, out_shape=SDS((1, 2 * H, C), f32),
                          compiler_params=_cparams(1))(cpos.reshape(1).astype(jnp.int32), mine, rec)


CHIP_FLIPS = ((1, 0), (0, 1), (1, 1))


def _pos():
    return lax.axis_index("x"), lax.axis_index("y"), lax.axis_index("c")


def _comm_call(body, srcs, out_sds, n_remote, n_local, name):
    any_spec = pl.BlockSpec(memory_space=pl.ANY)
    return pl.pallas_call(
        body, name=name, in_specs=[any_spec] * len(srcs), out_specs=[any_spec] * len(out_sds), out_shape=out_sds,
        scratch_shapes=[pltpu.SemaphoreType.DMA((n_remote,)), pltpu.SemaphoreType.DMA((n_remote,)),
                        pltpu.SemaphoreType.DMA((max(n_local, 1),))],
        compiler_params=pltpu.CompilerParams(has_side_effects=True),
    )(*srcs)


def _remote(src, dst, send_sems, recv_sems, k, target):
    return pltpu.make_async_remote_copy(src, dst, send_sems.at[k], recv_sems.at[k], device_id=target,
                                        device_id_type=MESH)


def _half_rows(c, rows):
    half = rows // 2
    return pl.ds(pl.multiple_of(c * half, 8), half)


def gather_shards(shards):
    nt = len(shards)

    def body(*refs):
        srcs, outs = refs[:nt], refs[nt:2 * nt]
        send_sems, recv_sems, _ = refs[2 * nt:]
        x, y, c = _pos()
        j = 2 * x + y
        sib = (x, y, 1 - c)
        own = [_remote(srcs[t], outs[t].at[j], send_sems, recv_sems, 7 * t + 6, sib) for t in range(nt)]
        first, passed = [], []
        for k, (fx, fy) in enumerate(CHIP_FLIPS):
            tx, ty = x ^ fx, y ^ fy
            jk = 2 * tx + ty
            for t in range(nt):
                rows = _half_rows(c, srcs[t].shape[0])
                first.append(_remote(srcs[t].at[rows], outs[t].at[j, rows], send_sems, recv_sems, 7 * t + k, (tx, ty, c)))
                passed.append(_remote(outs[t].at[jk, rows], outs[t].at[jk, rows], send_sems, recv_sems, 7 * t + 3 + k, sib))
        for cp in first + own:
            cp.start()
        for a, b in zip(first, passed):
            a.wait_recv()
            b.start()
        for cp in passed + own:
            cp.wait_recv()
        for cp in first + passed + own:
            cp.wait_send()

    return _comm_call(body, shards, [SDS((4,) + s.shape, s.dtype) for s in shards], 7 * nt, 0, "gather_shards")


def swap_halves(ps):
    nt = len(ps)

    def body(*refs):
        srcs, outs = refs[:nt], refs[nt:2 * nt]
        send_sems, recv_sems, _ = refs[2 * nt:]
        x, y, c = _pos()
        cps = [_remote(srcs[t].at[a, _half_rows(1 - c, srcs[t].shape[1])], outs[t].at[a], send_sems, recv_sems, 4 * t + a,
                       (x, y, 1 - c)) for t in range(nt) for a in range(4)]
        for cp in cps:
            cp.start()
        for cp in cps:
            cp.wait()

    return _comm_call(body, ps, [SDS((4, p.shape[1] // 2, p.shape[2]), p.dtype) for p in ps], 4 * nt, 0, "swap_halves")


def scatter_to_chips(qs):
    nt = len(qs)

    def body(*refs):
        srcs, outs = refs[:nt], refs[nt:2 * nt]
        send_sems, recv_sems, _ = refs[2 * nt:]
        x, y, c = _pos()
        cps = []
        for k, (fx, fy) in enumerate(CHIP_FLIPS):
            tx, ty = x ^ fx, y ^ fy
            for t in range(nt):
                cps.append(_remote(srcs[t].at[2 * tx + ty], outs[t].at[k], send_sems, recv_sems, 3 * t + k, (tx, ty, c)))
        for cp in cps:
            cp.start()
        for cp in cps:
            cp.wait()

    return _comm_call(body, qs, [SDS((3,) + q.shape[1:], q.dtype) for q in qs], 3 * nt, 0, "scatter_to_chips")


def join_halves(fs):
    nt = len(fs)

    def body(*refs):
        outs = refs[nt:2 * nt]
        send_sems, recv_sems, _ = refs[2 * nt:]
        x, y, c = _pos()
        cps = []
        for t in range(nt):
            mine = outs[t].at[0, _half_rows(c, outs[t].shape[1])]
            cps.append(_remote(mine, mine, send_sems, recv_sems, t, (x, y, 1 - c)))
        for cp in cps:
            cp.start()
        for cp in cps:
            cp.wait()

    any_spec = pl.BlockSpec(memory_space=pl.ANY)
    return pl.pallas_call(
        body, name="join_halves", in_specs=[any_spec] * nt, out_specs=[any_spec] * nt,
        out_shape=[SDS(f.shape, f.dtype) for f in fs], input_output_aliases={t: t for t in range(nt)},
        scratch_shapes=[pltpu.SemaphoreType.DMA((nt,)), pltpu.SemaphoreType.DMA((nt,)), pltpu.SemaphoreType.DMA((1,))],
        compiler_params=pltpu.CompilerParams(has_side_effects=True),
    )(*fs)


DEV_FLIPS = tuple((fx, fy, fc) for fx in (0, 1) for fy in (0, 1) for fc in (0, 1))[1:]


def gather_devices(block, *, name):
    def body(src, out, send_sems, recv_sems, loc_sems):
        x, y, c = _pos()
        me = 4 * x + 2 * y + c
        mine = pltpu.make_async_copy(src, out.at[me], loc_sems.at[0])
        mine.start()
        cps = [_remote(src, out.at[me], send_sems, recv_sems, k, (x ^ fx, y ^ fy, c ^ fc))
               for k, (fx, fy, fc) in enumerate(DEV_FLIPS)]
        for cp in cps:
            cp.start()
        for cp in cps:
            cp.wait()
        mine.wait()

    return _comm_call(body, [block], [SDS((N_DEV,) + block.shape, block.dtype)], 7, 1, name)[0]


def exchange_devices(blocks, *, name):
    def body(src, out, send_sems, recv_sems, loc_sems):
        x, y, c = _pos()
        me = 4 * x + 2 * y + c
        mine = pltpu.make_async_copy(src.at[me], out.at[me], loc_sems.at[0])
        mine.start()
        cps = []
        for k, (fx, fy, fc) in enumerate(DEV_FLIPS):
            tx, ty, tc = x ^ fx, y ^ fy, c ^ fc
            cps.append(_remote(src.at[4 * tx + 2 * ty + tc], out.at[me], send_sems, recv_sems, k, (tx, ty, tc)))
        for cp in cps:
            cp.start()
        for cp in cps:
            cp.wait()
        mine.wait()

    return _comm_call(body, [blocks], [SDS(blocks.shape, blocks.dtype)], 7, 1, name)[0]


def gather_small(s):
    def body(src, out, send_sems, recv_sems, _):
        x, y, c = _pos()
        j = 2 * x + y
        cps = [_remote(src, out.at[j], send_sems, recv_sems, k, (x ^ fx, y ^ fy, c)) for k, (fx, fy) in enumerate(CHIP_FLIPS)]
        cps.append(_remote(src, out.at[j], send_sems, recv_sems, 3, (x, y, 1 - c)))
        for cp in cps:
            cp.start()
        for cp in cps:
            cp.wait()

    return _comm_call(body, [s], [SDS((4,) + s.shape, s.dtype)], 4, 0, "gather_small")[0]


SMALL_SHAPES = ((1, 2, 32, 64), (1, 2, 32, 64), (1, 2, 32), (1, 2, 32, 16, 64),
                (1, 2, 32, 16, 64), (1, 2, 32, 16, 64), (1, 2, 32, 16, 64), (1, D_S5), (1, D_S5), (1, 2, 4), (1, 2, 4),
                (1, GDN_HEAD), (1, D_MODEL), (1, D_MODEL))
SMALL_SWAPPED = (3, 4)


def _size(shape):
    return functools.reduce(lambda p, q: p * q, shape)


SMALL_ROWS = tuple(-(-_size(s) // LANES) for s in SMALL_SHAPES)
SMALL_TOTAL = 2176
SMALL_QUARTER = SMALL_TOTAL // 4


def _rows(a):
    flat = a.reshape(-1)
    pad = (-flat.shape[0]) % LANES
    if pad:
        flat = jnp.concatenate([flat, jnp.zeros((pad,), flat.dtype)])
    return flat.reshape(-1, LANES)


def _pack_small(parts):
    rows = [_rows(p) for p in parts]
    rows.append(jnp.zeros((SMALL_TOTAL - sum(SMALL_ROWS), LANES), f32))
    return jnp.concatenate(rows, axis=0)


def _unpack_small(buf):
    out, r = [], 0
    for s, n in zip(SMALL_SHAPES, SMALL_ROWS):
        out.append(buf[r:r + n].reshape(-1)[:_size(s)].reshape(s))
        r += n
    return out


def _as_2d(a):
    return a.reshape(1, -1) if a.ndim == 1 else a.reshape(-1, a.shape[-1])


S5_BG = S5_GROUPS // S5_BLOCKS


def _block_diag_in(bb):
    eye = jnp.eye(S5_BG, dtype=bb.dtype)
    b4 = bb.reshape(S5_BLOCKS, S5_BG, S5_GROUP, S5_STATE)
    return jnp.einsum('jgcp,gh->jgchp', b4, eye).reshape(S5_BLOCKS, S5_BC, S5_BS)


def _block_diag_in_t(d):
    d6 = d.reshape(S5_BLOCKS, S5_BG, S5_GROUP, S5_BG, S5_STATE)
    return jnp.einsum('jgcgp->jgcp', d6).reshape(S5_GROUPS, S5_GROUP * S5_STATE)


def _block_diag_out(cm):
    eye = jnp.eye(S5_BG, dtype=cm.dtype)
    c4 = cm.reshape(S5_BLOCKS, S5_BG, S5_GROUP, S5_STATE)
    return jnp.einsum('jgcp,gh->jhpgc', c4, eye).reshape(S5_BLOCKS, S5_BS, S5_BC)


def _block_diag_out_t(d):
    d6 = d.reshape(S5_BLOCKS, S5_BG, S5_STATE, S5_BG, S5_GROUP)
    return jnp.einsum('jgpgc->jgcp', d6).reshape(S5_GROUPS, S5_GROUP, S5_STATE)


def _to_chunk_rows(a):
    B, L, W = a.shape
    return a.reshape(B, L // CHUNK, CHUNK, W).transpose(0, 1, 3, 2)


def _from_chunk_rows(a):
    B, nc, W, _ = a.shape
    return a.transpose(0, 1, 3, 2).reshape(B, nc * CHUNK, W)


def local_step(x, ctx, tgt, m, w_in, lam_re, lam_im, log_dt, b_re, b_im, c_re, c_im, s5_d,
               w_glu, b_glu, conv16, a_log, dt_bias, norm_w, w_out, ln_g, ln_b):
    B, L, _ = x.shape
    zeros_state = jnp.zeros((B, GDN_HEADS, GDN_HEAD, GDN_HEAD), f32)

    shift, scale, gate = m[:B, :D_MODEL], m[:B, D_MODEL:2 * D_MODEL], m[:B, 2 * D_MODEL:]
    mod = jnp.stack([scale, shift], axis=1)
    mod_c = jnp.broadcast_to(jnp.stack([m[B, D_MODEL:2 * D_MODEL], m[B, :D_MODEL]], axis=0)[None], (B, 2, D_MODEL))

    u, z_s5, qkv, z_gdn, ba = in_proj_fwd(x, mod, w_in, name="in_proj_fwd")
    uc, _, qkvc, _, bac = in_proj_fwd(ctx, mod_c, w_in, name="in_proj_fwd_ctx")

    ng = N_DIR * S5_GROUPS
    zoh_in = (lam_re.reshape(ng, S5_STATE), lam_im.reshape(ng, S5_STATE), log_dt.reshape(ng, 1),
              b_re.reshape(ng, S5_GROUP * S5_STATE), b_im.reshape(ng, S5_GROUP * S5_STATE))
    expand = (jnp.arange(S5_GROUP * S5_STATE)[None, :] % S5_STATE == jnp.arange(S5_STATE)[:, None]).astype(f32)
    ar, ai, bbr, bbi = s5_zoh_fwd(*zoh_in, expand)
    bbr16, bbi16 = bbr.astype(bf16), bbi.astype(bf16)
    c_re16 = c_re.reshape(N_DIR, S5_GROUPS, S5_GROUP, S5_STATE).astype(bf16)
    c_im16 = (-c_im).reshape(N_DIR, S5_GROUPS, S5_GROUP, S5_STATE).astype(bf16)
    s5w, ys, hins, hins_c = [], [], [], []
    for d in range(N_DIR):
        g = slice(d * S5_GROUPS, (d + 1) * S5_GROUPS)
        wd = (_block_diag_in(bbr16[g]), _block_diag_in(bbi16[g]), _block_diag_out(c_re16[d]), _block_diag_out(c_im16[d]),
              jnp.stack([ar[g].reshape(-1), ai[g].reshape(-1)], axis=0))
        s5w.append(wd)
        hin_c, hend_c = s5_scan_fwd(uc, *wd, jnp.zeros((B, 2, S5_HALF), f32), d=d, need_y=False, name=f"s5_fwd_ctx{d}")
        y_d, hin, _ = s5_scan_fwd(u, *wd, hend_c, d=d, need_y=True, name=f"s5_fwd{d}")
        ys.append(y_d)
        hins.append(hin)
        hins_c.append(hin_c)
    glu_w = (s5_d.reshape(1, D_S5), w_glu, b_glu.reshape(1, D_S5))

    act, pre = conv_fwd(qkv, conv16, is_ctx=False, name="conv_fwd")
    act_c, pre_c = conv_fwd(qkvc, conv16, is_ctx=True, name="conv_fwd_ctx")
    pad8 = jnp.zeros((1, 8), f32)
    alog16 = jnp.concatenate([pad8, a_log.reshape(1, 8)], axis=1)
    dtb16 = jnp.concatenate([pad8, dt_bias.reshape(1, 8)], axis=1)
    bg = gates_fwd(ba, alog16, dtb16, name="gates_fwd")
    bg_c = gates_fwd(bac, alog16, dtb16, name="gates_fwd_ctx")
    bgr, bgr_c = _to_chunk_rows(bg), _to_chunk_rows(bg_c)
    cks_c, ns_c, s_c = gdn_fwd(act_c, bg_c, bgr_c, (zeros_state, zeros_state), need_o=False, name="gdn_fwd_ctx")
    os_, cks, ns, _ = gdn_fwd(act, bg, bgr, s_c, need_o=True, name="gdn_fwd")
    nw = norm_w.reshape(1, GDN_HEAD)

    (loss8, du_skip, dy, dz_s5, do, dz_gdn, gx_res, dws, dwg, dgate, dlng, dlnb, d_s5_d, d_w_glu, d_b_glu,
     d_norm_w) = tail_fwd_bwd(u, ys[0], ys[1], z_s5, os_[0], os_[1], z_gdn, x, tgt, gate[:, None, :],
                              ln_g.reshape(1, D_MODEL), ln_b.reshape(1, D_MODEL), w_out[:D_S5], w_out[D_S5:], *glu_w, nw)
    loss = jnp.sum(loss8[:, 0, 0])
    d_w_out = jnp.concatenate([dws, dwg], axis=0)

    dacts, dbgs, dbgrs, ds0s = gdn_bwd(act, bg, bgr, cks, ns, do, (zeros_state, zeros_state), name="gdn_bwd")
    dacts_c, dbgs_c, dbgrs_c, _ = gdn_bwd(act_c, bg_c, bgr_c, cks_c, ns_c, None, ds0s, name="gdn_bwd_ctx")
    dbg = dbgs[0] + dbgs[1] + _from_chunk_rows(dbgrs[0] + dbgrs[1])
    dbg_c = dbgs_c[0] + dbgs_c[1] + _from_chunk_rows(dbgrs_c[0] + dbgrs_c[1])
    dba, dal, ddt = gates_bwd(ba, alog16, dtb16, dbg, name="gates_bwd")
    dbac, dal_c, ddt_c = gates_bwd(bac, alog16, dtb16, dbg_c, name="gates_bwd_ctx")
    d_a_log = (dal + dal_c)[:, 8:].reshape(1, N_DIR, GDN_HEADS)
    d_dt_bias = (ddt + ddt_c)[:, 8:].reshape(1, N_DIR, GDN_HEADS)
    dqkv, dcw = conv_bwd(qkv, pre, conv16, dacts[0], dacts[1], is_ctx=False, name="conv_bwd")
    dqkvc, dcw_c = conv_bwd(qkvc, pre_c, conv16, dacts_c[0], dacts_c[1], is_ctx=True, name="conv_bwd_ctx")
    d_conv16 = jnp.sum(dcw, axis=0) + jnp.sum(dcw_c, axis=0)

    dus, ducs = [du_skip], []
    dar, dai, dbbr, dbbi, dcre, dcim = [], [], [], [], [], []
    for d in range(N_DIR):
        du_d, dbre1, dbim1, dct1, dcb1, da1, dh0 = s5_scan_bwd(u, dy, *s5w[d], hins[d],
                                                                jnp.zeros((B, 2, S5_HALF), f32), d=d, name=f"s5_bwd{d}")
        duc_d, dbre2, dbim2, _, _, da2, _ = s5_scan_bwd(uc, None, *s5w[d], hins_c[d], dh0, d=d, name=f"s5_bwd_ctx{d}")
        dus.append(du_d)
        ducs.append(duc_d)
        da = da1 + da2
        dar.append(da[0].reshape(S5_GROUPS, S5_STATE))
        dai.append(da[1].reshape(S5_GROUPS, S5_STATE))
        dbbr.append(_block_diag_in_t(dbre1 + dbre2))
        dbbi.append(_block_diag_in_t(dbim1 + dbim2))
        dcre.append(_block_diag_out_t(dct1))
        dcim.append(-_block_diag_out_t(dcb1))
    dlr, dli, dldt, dbre, dbim = s5_zoh_bwd(*zoh_in, expand, jnp.concatenate(dar, 0), jnp.concatenate(dai, 0),
                                            jnp.concatenate(dbbr, 0), jnp.concatenate(dbbi, 0))
    d_s5 = (dlr, dli, dldt, dbre, dbim, jnp.stack(dcre, 0), jnp.stack(dcim, 0))

    padg = lambda a: jnp.concatenate([a, jnp.zeros(a.shape[:2] + (LANES - N_GATE,), f32)], axis=2)
    zc = jnp.zeros_like(uc)
    dw_c, dmod_c = in_proj_bwd(ctx, mod_c, (tuple(ducs), zc, dqkvc, zc, padg(dbac)), w_in, None, None,
                               name="in_proj_bwd_ctx")
    d_w_in, dmod, grad_x = in_proj_bwd(x, mod, (tuple(dus), dz_s5, dqkv, dz_gdn, padg(dba)), w_in, gx_res, dw_c,
                                       name="in_proj_bwd")
    dmod_c = jnp.sum(dmod_c, axis=0)

    dm_rows = jnp.concatenate([dmod[:, 1], dmod[:, 0], dgate[:, 0]], axis=1)
    dm_ctx = jnp.concatenate([dmod_c[1], dmod_c[0], jnp.zeros((D_MODEL,), f32)])[None]
    dm = jnp.concatenate([dm_rows, dm_ctx], axis=0)
    small = (*d_s5, d_s5_d, d_b_glu, d_a_log, d_dt_bias, d_norm_w, dlng, dlnb)
    small = tuple(g.reshape(s) for g, s in zip(small, SMALL_SHAPES))
    return loss, grad_x, (d_w_in, d_w_out, d_w_glu, d_conv16), small, dm


SHARDED = (1, 3, 18, 12, 14)
REDUCED = (3, 18, 12, 14)
UNSHARDED = tuple(i for i in range(21) if i not in SHARDED)
SMALL = tuple(i for i in UNSHARDED if i not in (0, 2))
W_IN_SHARD = 772


def _conv_rows(w):
    return jnp.concatenate([w.reshape(9, w.shape[-1]), jnp.zeros((CONV_ROWS - 9, w.shape[-1]), f32)], axis=0)


def kernel(x, c, ctx, c_ctx, w_ada, b_ada, w_in, s5_lambda_re, s5_lambda_im, s5_log_dt, s5_b_re, s5_b_im, s5_c_re, s5_c_im, s5_d, w_glu, b_glu, conv_w, gdn_a_log, gdn_dt_bias, gdn_norm_w, w_out, ln_g, ln_b, loss_target, m_c_ctx, m_w_ada, m_b_ada, m_w_in, m_s5_lambda_re, m_s5_lambda_im, m_s5_log_dt, m_s5_b_re, m_s5_b_im, m_s5_c_re, m_s5_c_im, m_s5_d, m_w_glu, m_b_glu, m_conv_w, m_gdn_a_log, m_gdn_dt_bias, m_gdn_norm_w, m_w_out, m_ln_g, m_ln_b, v_c_ctx, v_w_ada, v_b_ada, v_w_in, v_s5_lambda_re, v_s5_lambda_im, v_s5_log_dt, v_s5_b_re, v_s5_b_im, v_s5_c_re, v_s5_c_im, v_s5_d, v_w_glu, v_b_glu, v_conv_w, v_gdn_a_log, v_gdn_dt_bias, v_gdn_norm_w, v_w_out, v_ln_g, v_ln_b):
    weights = [c_ctx, w_ada, b_ada, w_in, s5_lambda_re, s5_lambda_im, s5_log_dt, s5_b_re, s5_b_im, s5_c_re, s5_c_im,
               s5_d, w_glu, b_glu, conv_w, gdn_a_log, gdn_dt_bias, gdn_norm_w, w_out, ln_g, ln_b]
    ms = [m_c_ctx, m_w_ada, m_b_ada, m_w_in, m_s5_lambda_re, m_s5_lambda_im, m_s5_log_dt, m_s5_b_re, m_s5_b_im,
          m_s5_c_re, m_s5_c_im, m_s5_d, m_w_glu, m_b_glu, m_conv_w, m_gdn_a_log, m_gdn_dt_bias, m_gdn_norm_w, m_w_out,
          m_ln_g, m_ln_b]
    vs = [v_c_ctx, v_w_ada, v_b_ada, v_w_in, v_s5_lambda_re, v_s5_lambda_im, v_s5_log_dt, v_s5_b_re, v_s5_b_im,
          v_s5_c_re, v_s5_c_im, v_s5_d, v_w_glu, v_b_glu, v_conv_w, v_gdn_a_log, v_gdn_dt_bias, v_gdn_norm_w, v_w_out,
          v_ln_g, v_ln_b]
    cpos = lax.axis_index("c")
    jchip = 2 * lax.axis_index("x") + lax.axis_index("y")

    c_all = gather_devices(c, name="gather_c")
    cc = jnp.concatenate([c_all, jnp.broadcast_to(c_ctx[None, None, :], (N_DEV, 1, D_MODEL)),
                          jnp.zeros((N_DEV, 5, D_MODEL), f32)], axis=1)
    w_ada16 = w_ada[0].astype(bf16)
    b_cols = lax.dynamic_slice_in_dim(b_ada, jchip * ADA_SHARD, ADA_SHARD, axis=1)
    m_mine = exchange_devices(ada_fwd(cc, w_ada16, b_cols), name="exchange_m")
    m_rows = jnp.concatenate([m_mine[2 * j, :3] for j in range(4)], axis=1)

    conv_shard = _conv_rows(conv_w)
    g_in, g_out, g_glu, g_conv = gather_shards(
        [w_in[0].astype(bf16), w_out[0].astype(bf16), w_glu[0].astype(bf16), conv_shard])
    w_in_pad = jnp.concatenate([g_in[0], g_in[1], g_in[2], g_in[3], jnp.zeros((D_MODEL, IN_PAD - P_IN), bf16)], axis=1)
    conv16 = g_conv.transpose(1, 0, 2).reshape(CONV_ROWS, 3 * D_GDN)

    swap = lambda a: jnp.swapaxes(a, 3, 4)
    loss, grad_x, big, small, dm_rows = local_step(
        x, ctx, loss_target, m_rows, w_in_pad, s5_lambda_re, s5_lambda_im, s5_log_dt, swap(s5_b_re), swap(s5_b_im),
        s5_c_re, s5_c_im, s5_d, g_glu.reshape(D_S5, D_S5), b_glu, conv16, gdn_a_log, gdn_dt_bias, gdn_norm_w,
        g_out.reshape(D_MODEL, D_MODEL), ln_g, ln_b)
    loss = lax.psum(loss, ("x", "y", "c"))

    dm8 = jnp.concatenate([dm_rows, jnp.zeros((5, 3 * D_MODEL), f32)], axis=0)
    dm_by_chip = dm8.reshape(8, 4, ADA_SHARD).transpose(1, 0, 2)
    dm_cols = exchange_devices(jnp.repeat(dm_by_chip, 2, axis=0), name="exchange_dm")
    g_w_ada, pb = ada_bwd(cc, w_ada16, dm_cols)
    pb_all = gather_devices(pb, name="gather_p")
    g_c_ctx = c_ctx_bwd(pb_all, c_ctx[None, :])[0]
    g_b_ada = jnp.concatenate([pb_all[2 * j, 1:2, :ADA_SHARD] for j in range(4)], axis=1)

    d_w_in, d_w_out, d_w_glu, d_conv16 = big
    slabs = [d_w_in[:, :P_IN].reshape(D_MODEL, 4, W_IN_SHARD).transpose(1, 0, 2),
             d_w_out.reshape(4, D_MODEL // 4, D_MODEL),
             d_w_glu.reshape(4, D_S5 // 4, D_S5),
             d_conv16.reshape(CONV_ROWS, 4, 3 * D_GDN // 4).transpose(1, 0, 2),
             _pack_small(small).reshape(4, SMALL_QUARTER, LANES)]
    got = swap_halves(slabs)
    q32, q16 = [], []
    for t, (s, g) in enumerate(zip(slabs, got)):
        own = lax.dynamic_index_in_dim(s.reshape(4, 2, s.shape[1] // 2, s.shape[2]), cpos, axis=1, keepdims=False)
        a, b = sum_cores(own, g, name=f"sum_cores{t}")
        q32.append(a)
        q16.append(b)
    rec = scatter_to_chips(q16)
    fs = [sum_chips(lax.dynamic_index_in_dim(q, jchip, axis=0, keepdims=False), r, cpos, name=f"sum_chips{t}")
          for t, (q, r) in enumerate(zip(q32, rec))]
    red = join_halves(fs)
    g_small = _unpack_small(gather_small(red[4][0]).reshape(SMALL_TOTAL, LANES))
    g_shard = {1: g_w_ada, 3: red[0], 18: red[1], 12: red[2], 14: red[3]}

    grads, deltas, new_m, new_v = [None] * 21, [None] * 21, [None] * 21, [None] * 21
    for t, i in enumerate(SHARDED):
        conv, win = i == 14, i == 3
        prep = (lambda a: _conv_rows(a)[None]) if conv else ((lambda a: jnp.transpose(a, (2, 0, 1))) if win else (lambda a: a))
        g = jnp.transpose(g_shard[i], (2, 0, 1)) if win else g_shard[i]
        d, nm, nv = adamw_3d(prep(weights[i]), g, prep(ms[i]), prep(vs[i]), lead=win, name=f"adamw{t}")
        for lst, val in ((grads, g), (deltas, d), (new_m, nm), (new_v, nv)):
            lst[i] = (val[0, :9].reshape(weights[i].shape) if conv else (jnp.transpose(val, (1, 2, 0)) if win else val))
    g_un = {0: g_c_ctx, 2: g_b_ada, **{i: g_small[n] for n, i in enumerate(SMALL)}}
    swapped = [SMALL[n] for n in SMALL_SWAPPED]
    small_in = lambda lst: [_as_2d(swap(lst[i]) if i in swapped else lst[i]) for i in UNSHARDED]
    sm = adamw_small(small_in(weights), [_as_2d(g_un[i]) for i in UNSHARDED], small_in(ms), small_in(vs))
    for n, i in enumerate(UNSHARDED):
        back = ((lambda a: swap(a.reshape(swap(weights[i]).shape))) if i in swapped
                else (lambda a: a.reshape(weights[i].shape)))
        grads[i] = back(g_un[i])
        for lst, res in ((deltas, sm[0]), (new_m, sm[1]), (new_v, sm[2])):
            lst[i] = back(res[n])
    return (loss, grad_x, *grads, *deltas, *new_m, *new_v)
```

```python
import functools

import jax
import jax.numpy as jnp
from jax import lax
from jax.experimental import pallas as pl
from jax.experimental.pallas import tpu as pltpu

f32 = jnp.float32
bf16 = jnp.bfloat16
SDS = jax.ShapeDtypeStruct

D_MODEL = 1024
D_S5 = 512
S5_GROUP = 16
S5_GROUPS = 32
S5_STATE = 64
S5_HALF = S5_GROUPS * S5_STATE
D_GDN = 512
GDN_HEAD = 128
GDN_HEADS = 4
CHUNK = 64
GRID_W = 64
N_DIR = 2
P_IN = 3088
DEEPNORM_ALPHA = 2.0 ** 0.25
LN_EPS = 1e-5
NORM_EPS = 1e-6
ADAM_LR, ADAM_B1, ADAM_B2, ADAM_EPS, ADAM_WD, ADAM_STEP = 0.001, 0.9, 0.999, 1e-08, 0.01, 10

LANES = 128
VMEM_LIMIT = 56 * 1024 * 1024
TOK_TILE = 256
S5_TILE = 256
MESH = pl.DeviceIdType.MESH


def _cparams(n_grid):
    return pltpu.CompilerParams(dimension_semantics=("arbitrary",) * n_grid, vmem_limit_bytes=VMEM_LIMIT)


def _dot(a, b):
    return jnp.dot(a.astype(bf16), b.astype(bf16), preferred_element_type=f32)


def _dot_nt(a, b):
    return lax.dot_general(a.astype(bf16), b.astype(bf16), (((1,), (1,)), ((), ())), preferred_element_type=f32)


def _dot_tn(a, b):
    return lax.dot_general(a.astype(bf16), b.astype(bf16), (((0,), (0,)), ((), ())), preferred_element_type=f32)


def _dot_hi(a, b):
    return jnp.dot(a, b, precision=lax.Precision.HIGHEST, preferred_element_type=f32)


def _dot_h3(a, b):
    return jnp.dot(a, b, precision=lax.Precision.HIGH, preferred_element_type=f32)


@jax.custom_vjp
def _mm(a, b):
    return _dot(a, b)


@jax.custom_vjp
def _mm_nt(a, b):
    return _dot_nt(a, b)


@jax.custom_vjp
def _mm_tn(a, b):
    return _dot_tn(a, b)


_mm.defvjp(lambda a, b: (_dot(a, b), (a, b)), lambda r, g: (_mm_nt(g, r[1]), _mm_tn(r[0], g)))
_mm_nt.defvjp(lambda a, b: (_dot_nt(a, b), (a, b)), lambda r, g: (_mm(g, r[1]), _mm_tn(g, r[0])))
_mm_tn.defvjp(lambda a, b: (_dot_tn(a, b), (a, b)), lambda r, g: (_mm_nt(r[1], g), _mm(r[0], g)))


def _silu(x):
    return x * jax.nn.sigmoid(x)


def _gelu(x):
    return 0.5 * x * (1.0 + lax.erf(x * (2.0 ** -0.5)))


def _resident(shape):
    nd = len(shape)
    return pl.BlockSpec(shape, lambda *_: (0,) * nd, pipeline_mode=pl.Buffered(1))


def _tok(tile, width, nt=None, rev=False):
    if rev:
        return pl.BlockSpec((None, tile, width), lambda b, n: (b, nt - 1 - n, 0))
    return pl.BlockSpec((None, tile, width), lambda b, n: (b, n, 0))


def _per_batch(rows, width):
    return pl.BlockSpec((None, rows, width), lambda b, n: (b, 0, 0))


def _first_step():
    return jnp.logical_and(pl.program_id(0) == 0, pl.program_id(1) == 0)


ADA_SHARD = 3 * D_MODEL // 4
N_DEV = 8


def ada_fwd(cc, w, b):
    def body(cc_ref, w_ref, b_ref, m_ref):
        for k in range(N_DEV):
            m_ref[k] = _dot(_silu(cc_ref[k]), w_ref[...]) + b_ref[...]

    return pl.pallas_call(body, name="ada_fwd", out_shape=SDS((N_DEV, 8, ADA_SHARD), f32),
                          compiler_params=pltpu.CompilerParams(vmem_limit_bytes=VMEM_LIMIT))(cc, w, b)


def ada_bwd(cc, w, dmj):
    def body(cc_ref, w_ref, dmj_ref, dw_ref, pb_ref):
        dw = jnp.zeros((D_MODEL, ADA_SHARD), f32)
        p = jnp.zeros((8, D_MODEL), f32)
        db = jnp.zeros((1, ADA_SHARD), f32)
        for k in range(N_DEV):
            dw = dw + _dot_tn(_silu(cc_ref[k]), dmj_ref[k])
            p = p + _dot_nt(dmj_ref[k], w_ref[...])
            db = db + jnp.sum(dmj_ref[k], axis=0, keepdims=True)
        dw_ref[0] = dw
        pb_ref[...] = jnp.zeros_like(pb_ref)
        pb_ref[0:1, :] = p[2:3, :]
        pb_ref[1:2, 0:ADA_SHARD] = db

    return pl.pallas_call(
        body, name="ada_bwd", out_shape=[SDS((1, D_MODEL, ADA_SHARD), f32), SDS((8, D_MODEL), f32)],
        compiler_params=pltpu.CompilerParams(vmem_limit_bytes=VMEM_LIMIT))(cc, w, dmj)


def c_ctx_bwd(pb_all, c_ctx):
    def body(p_ref, c_ref, d_ref):
        ds = ((p_ref[0, 0:1, :] + p_ref[2, 0:1, :]) + p_ref[4, 0:1, :]) + p_ref[6, 0:1, :]
        _, vjp = jax.vjp(_silu, c_ref[...])
        d_ref[...] = vjp(ds)[0]

    return pl.pallas_call(body, name="c_ctx_bwd", out_shape=SDS((1, D_MODEL), f32))(pb_all, c_ctx)


N_GATE = 2 * N_DIR * GDN_HEADS
IN_WIDTHS = (D_S5, D_S5, 3 * D_GDN, D_GDN, LANES)
IN_OFFS = (0, 512, 1024, 2560, 3072)
IN_PAD = 3200


def in_proj_fwd(x, mod, w, *, name):
    B, L, _ = x.shape
    T = min(TOK_TILE, L)

    def body(x_ref, mod_ref, w_ref, *o_refs):
        h = (x_ref[...] * (1.0 + mod_ref[0:1, :]) + mod_ref[1:2, :]).astype(bf16)
        for o_ref, off, wd in zip(o_refs, IN_OFFS, IN_WIDTHS):
            r = _dot(h, w_ref[:, off:off + wd])
            o_ref[...] = r[:, :o_ref.shape[-1]]

    outw = (D_S5, D_S5, 3 * D_GDN, D_GDN, N_GATE)
    return pl.pallas_call(
        body, name=name, grid=(B, L // T),
        in_specs=[_tok(T, D_MODEL), _per_batch(2, D_MODEL), _resident((D_MODEL, IN_PAD))],
        out_specs=[_tok(T, wd) for wd in outw],
        out_shape=[SDS((B, L, wd), f32) for wd in outw],
        compiler_params=_cparams(2),
    )(x, mod, w)


def in_proj_bwd(x, mod, ds, w, gx_res, dw_start, *, name):
    B, L, _ = x.shape
    T = min(TOK_TILE, L)
    with_dx = gx_res is not None
    with_start = dw_start is not None
    n_u = len(ds[0])

    def body(*refs):
        x_ref, mod_ref = refs[0], refs[1]
        du_refs = refs[2:2 + n_u]
        d_refs = refs[2 + n_u:6 + n_u]
        w_ref = refs[6 + n_u]
        k = 7 + n_u
        if with_dx:
            gx_ref = refs[k]
            k += 1
        if with_start:
            start_ref = refs[k]
            k += 1
        dw_ref, dmod_ref = refs[k], refs[k + 1]
        if with_dx:
            dx_ref = refs[k + 2]
        n = pl.program_id(1)

        @pl.when(_first_step())
        def _():
            dw_ref[...] = start_ref[...] if with_start else jnp.zeros_like(dw_ref)

        @pl.when(n == 0)
        def _():
            dmod_ref[...] = jnp.zeros_like(dmod_ref)

        xv = x_ref[...]
        scale1 = 1.0 + mod_ref[0:1, :]
        h = (xv * scale1 + mod_ref[1:2, :]).astype(bf16)
        du = du_refs[0][...]
        for r in du_refs[1:]:
            du = du + r[...]
        dh = jnp.zeros((T, D_MODEL), f32)
        for dv, off, wd in zip([du] + [r[...] for r in d_refs], IN_OFFS, IN_WIDTHS):
            dv = dv.astype(bf16)
            dh = dh + _dot_nt(dv, w_ref[:, off:off + wd])
            dw_ref[:, off:off + wd] += _dot_tn(h, dv)
        dmod_ref[0:1, :] += jnp.sum(dh * xv, axis=0, keepdims=True)
        dmod_ref[1:2, :] += jnp.sum(dh, axis=0, keepdims=True)
        if with_dx:
            dx_ref[...] = gx_ref[...] + dh * scale1

    in_specs = ([_tok(T, D_MODEL), _per_batch(2, D_MODEL)] + [_tok(T, D_S5)] * n_u + [_tok(T, wd) for wd in IN_WIDTHS[1:]]
                + [_resident((D_MODEL, IN_PAD))])
    args = [x, mod, *ds[0], *ds[1:], w]
    out_specs = [_resident((D_MODEL, IN_PAD)), _per_batch(2, D_MODEL)]
    out_shape = [SDS((D_MODEL, IN_PAD), f32), SDS((B, 2, D_MODEL), f32)]
    if with_dx:
        in_specs.append(_tok(T, D_MODEL))
        args.append(gx_res)
        out_specs.append(_tok(T, D_MODEL))
        out_shape.append(SDS((B, L, D_MODEL), f32))
    if with_start:
        in_specs.append(_resident((D_MODEL, IN_PAD)))
        args.append(dw_start)
    return pl.pallas_call(body, name=name, grid=(B, L // T), in_specs=in_specs, out_specs=out_specs,
                          out_shape=out_shape, compiler_params=_cparams(2))(*args)


def _s5_zoh(lr, li, ldt, bre, bim, expand):
    dt = jnp.exp(ldt)
    zr, zi = lr * dt, li * dt
    e = jnp.exp(zr)
    ar, ai = e * jnp.cos(zi), e * jnp.sin(zi)
    den = lr * lr + li * li
    czr = ((ar - 1.0) * lr + ai * li) / den
    czi = (ai * lr - (ar - 1.0) * li) / den
    czr_e, czi_e = _dot_hi(czr, expand), _dot_hi(czi, expand)
    return ar, ai, czr_e * bre - czi_e * bim, czr_e * bim + czi_e * bre


_ZOH_OUT = [(N_DIR * S5_GROUPS, S5_STATE)] * 2 + [(N_DIR * S5_GROUPS, S5_STATE * S5_GROUP)] * 2


def s5_zoh_fwd(lr, li, ldt, bre, bim, expand):
    def body(lr_ref, li_ref, ldt_ref, bre_ref, bim_ref, e_ref, ar_ref, ai_ref, bbr_ref, bbi_ref):
        ar, ai, bbr, bbi = _s5_zoh(lr_ref[...], li_ref[...], ldt_ref[...], bre_ref[...], bim_ref[...], e_ref[...])
        ar_ref[...], ai_ref[...], bbr_ref[...], bbi_ref[...] = ar, ai, bbr, bbi

    return pl.pallas_call(body, name="s5_zoh_fwd", out_shape=[SDS(s, f32) for s in _ZOH_OUT])(
        lr, li, ldt, bre, bim, expand)


def s5_zoh_bwd(lr, li, ldt, bre, bim, expand, dar, dai, dbbr, dbbi):
    def body(lr_ref, li_ref, ldt_ref, bre_ref, bim_ref, e_ref, dar_ref, dai_ref, dbbr_ref, dbbi_ref,
             dlr_ref, dli_ref, dldt_ref, dbre_ref, dbim_ref):
        ev = e_ref[...]
        _, vjp = jax.vjp(lambda a, b, c, d, e: _s5_zoh(a, b, c, d, e, ev),
                         lr_ref[...], li_ref[...], ldt_ref[...], bre_ref[...], bim_ref[...])
        outs = vjp((dar_ref[...], dai_ref[...], dbbr_ref[...], dbbi_ref[...]))
        dlr_ref[...], dli_ref[...], dldt_ref[...], dbre_ref[...], dbim_ref[...] = outs

    shapes = [lr.shape, li.shape, ldt.shape, bre.shape, bim.shape]
    return pl.pallas_call(body, name="s5_zoh_bwd", out_shape=[SDS(s, f32) for s in shapes])(
        lr, li, ldt, bre, bim, expand, dar, dai, dbbr, dbbi)


def _scan_rows(T, rev, ar, ai, h0s, refs, off):
    def step(i, carry):
        t = off + ((T - 1 - i) if rev else i)
        out = []
        for (hr, hi), (r_ref, i_ref) in zip(carry, refs):
            nr = ar * hr - ai * hi + r_ref[pl.ds(t, 1), :]
            ni = ar * hi + ai * hr + i_ref[pl.ds(t, 1), :]
            r_ref[pl.ds(t, 1), :] = nr
            i_ref[pl.ds(t, 1), :] = ni
            out.append((nr, ni))
        return tuple(out)

    return lax.fori_loop(0, T, step, tuple(h0s))


S5_BLOCKS = 4
S5_BC = D_S5 // S5_BLOCKS
S5_BS = S5_HALF // S5_BLOCKS


def _s5_in(uv, bre_ref, bim_ref, hr_ref, hi_ref, off, T):
    for jb in range(S5_BLOCKS):
        uj = uv[:, jb * S5_BC:(jb + 1) * S5_BC]
        hr_ref[off:off + T, jb * S5_BS:(jb + 1) * S5_BS] = _dot(uj, bre_ref[jb])
        hi_ref[off:off + T, jb * S5_BS:(jb + 1) * S5_BS] = _dot(uj, bim_ref[jb])


def _s5_specs(B, T, nt, rev):
    tidx = (lambda n: nt - 1 - n) if rev else (lambda n: n)
    tok = pl.BlockSpec((B, T, D_S5), lambda n: (0, tidx(n), 0))
    hin = pl.BlockSpec((B, None, 2, S5_HALF), lambda n: (0, tidx(n), 0, 0))
    state = pl.BlockSpec((B, 2, S5_HALF), lambda n: (0, 0, 0))
    return tok, hin, state


def s5_scan_fwd(u, bre, bim, ctop, cbot, arow, h0, *, d, need_y, name):
    B, L, _ = u.shape
    T = min(S5_TILE, L)
    nt = L // T
    rev = d == 1

    def body(u_ref, bre_ref, bim_ref, ct_ref, cb_ref, a_ref, h0_ref, *rest):
        if need_y:
            y_ref, hs_ref, hin_ref, hend_ref, hr_scr, hi_scr, h_scr = rest
        else:
            hs_ref, hin_ref, hend_ref, hr_scr, hi_scr, h_scr = rest
        n = pl.program_id(0)

        @pl.when(n == 0)
        def _():
            h_scr[...] = h0_ref[...]

        hin_ref[...] = h_scr[...]
        for b in range(B):
            _s5_in(u_ref[b].astype(bf16), bre_ref, bim_ref, hr_scr.at[b], hi_scr.at[b], 0, T)
        hs = _scan_rows(T, rev, a_ref[0:1, :], a_ref[1:2, :], [(h_scr[b, 0:1, :], h_scr[b, 1:2, :]) for b in range(B)],
                        [(hr_scr.at[b], hi_scr.at[b]) for b in range(B)], 0)
        for b in range(B):
            h_scr[b, 0:1, :] = hs[b][0]
            h_scr[b, 1:2, :] = hs[b][1]
            hs_ref[b, :, 0:S5_HALF] = hr_scr[b].astype(bf16)
            hs_ref[b, :, S5_HALF:2 * S5_HALF] = hi_scr[b].astype(bf16)
            if need_y:
                for jb in range(S5_BLOCKS):
                    st = slice(jb * S5_BS, (jb + 1) * S5_BS)
                    y_ref[b, :, jb * S5_BC:(jb + 1) * S5_BC] = (_dot(hr_scr[b, :, st], ct_ref[jb])
                                                                 + _dot(hi_scr[b, :, st], cb_ref[jb]))

        @pl.when(n == nt - 1)
        def _():
            hend_ref[...] = h_scr[...]

    tok, hin_spec, state = _s5_specs(B, T, nt, rev)
    hs_spec = pl.BlockSpec((B, T, 2 * S5_HALF), tok.index_map)
    out_specs = [hs_spec, hin_spec, state]
    out_shape = [SDS((B, L, 2 * S5_HALF), bf16), SDS((B, nt, 2, S5_HALF), f32), SDS((B, 2, S5_HALF), f32)]
    if need_y:
        out_specs.insert(0, tok)
        out_shape.insert(0, SDS((B, L, D_S5), f32))
    w_in, w_out = _resident((S5_BLOCKS, S5_BC, S5_BS)), _resident((S5_BLOCKS, S5_BS, S5_BC))
    return pl.pallas_call(
        body, name=name, grid=(nt,),
        in_specs=[tok, w_in, w_in, w_out, w_out, _resident((2, S5_HALF)), state],
        out_specs=out_specs, out_shape=out_shape,
        scratch_shapes=[pltpu.VMEM((B, T, S5_HALF), f32), pltpu.VMEM((B, T, S5_HALF), f32),
                        pltpu.VMEM((B, 2, S5_HALF), f32)],
        compiler_params=_cparams(1),
    )(u, bre, bim, ctop, cbot, arow, h0)


def s5_scan_bwd(u, dy, hs, bre, bim, ctop, cbot, arow, hin, dhend, *, d, name):
    B, L, _ = u.shape
    T = min(S5_TILE, L)
    nt = L // T
    rev = d == 1
    has_dy = dy is not None
    PAD = 8

    def body(*refs):
        u_ref = refs[0]
        k = 1
        if has_dy:
            dy_ref = refs[1]
            k = 2
        hs_ref = refs[k]
        k += 1
        bre_ref, bim_ref, ct_ref, cb_ref, a_ref, hin_ref, dhend_ref = refs[k:k + 7]
        du_ref, dbre_ref, dbim_ref, dct_ref, dcb_ref, da_ref, dh0_ref = refs[k + 7:k + 14]
        hr_scr, hi_scr, gr_scr, gi_scr, p_scr = refs[k + 14:]
        n = pl.program_id(0)

        @pl.when(n == 0)
        def _():
            for r in (dbre_ref, dbim_ref, dct_ref, dcb_ref, da_ref):
                r[...] = jnp.zeros_like(r)
            p_scr[...] = dhend_ref[...]

        ar, ai = a_ref[0:1, :], a_ref[1:2, :]
        prev_row = PAD + T if rev else PAD - 1
        uvs = []
        for b in range(B):
            uvs.append(u_ref[b].astype(bf16))
            hr_scr[b, PAD:PAD + T, :] = hs_ref[b, :, 0:S5_HALF].astype(f32)
            hi_scr[b, PAD:PAD + T, :] = hs_ref[b, :, S5_HALF:2 * S5_HALF].astype(f32)
            hr_scr[b, prev_row:prev_row + 1, :] = hin_ref[b, 0:1, :]
            hi_scr[b, prev_row:prev_row + 1, :] = hin_ref[b, 1:2, :]
        if has_dy:
            for b in range(B):
                dyv = dy_ref[b].astype(bf16)
                for jb in range(S5_BLOCKS):
                    st = slice(jb * S5_BS, (jb + 1) * S5_BS)
                    dyj = dyv[:, jb * S5_BC:(jb + 1) * S5_BC]
                    gr_scr[b, :, st] = _dot_nt(dyj, ct_ref[jb])
                    gi_scr[b, :, st] = _dot_nt(dyj, cb_ref[jb])
                    dct_ref[jb] += _dot_tn(hr_scr[b, PAD:PAD + T, st], dyj)
                    dcb_ref[jb] += _dot_tn(hi_scr[b, PAD:PAD + T, st], dyj)
        else:
            gr_scr[...] = jnp.zeros_like(gr_scr)
            gi_scr[...] = jnp.zeros_like(gi_scr)

        def step(i, carry):
            t = i if rev else T - 1 - i
            tp = PAD + t + (1 if rev else -1)
            out = []
            for b, (pr, pi, dar, dai) in enumerate(carry):
                gr = gr_scr[b, pl.ds(t, 1), :] + pr
                gi = gi_scr[b, pl.ds(t, 1), :] + pi
                gr_scr[b, pl.ds(t, 1), :] = gr
                gi_scr[b, pl.ds(t, 1), :] = gi
                hpr = hr_scr[b, pl.ds(tp, 1), :]
                hpi = hi_scr[b, pl.ds(tp, 1), :]
                out.append((ar * gr + ai * gi, ar * gi - ai * gr, dar + hpr * gr + hpi * gi, dai + hpr * gi - hpi * gr))
            return tuple(out)

        zero = jnp.zeros((1, S5_HALF), f32)
        res = lax.fori_loop(0, T, step, tuple((p_scr[b, 0:1, :], p_scr[b, 1:2, :], zero, zero) for b in range(B)))
        for b in range(B):
            pr, pi, dar, dai = res[b]
            p_scr[b, 0:1, :] = pr
            p_scr[b, 1:2, :] = pi
            da_ref[0:1, :] += dar
            da_ref[1:2, :] += dai
            for jb in range(S5_BLOCKS):
                st = slice(jb * S5_BS, (jb + 1) * S5_BS)
                ch = slice(jb * S5_BC, (jb + 1) * S5_BC)
                gr_j = gr_scr[b, :, st].astype(bf16)
                gi_j = gi_scr[b, :, st].astype(bf16)
                du_ref[b, :, ch] = _dot_nt(gr_j, bre_ref[jb]) + _dot_nt(gi_j, bim_ref[jb])
                dbre_ref[jb] += _dot_tn(uvs[b][:, ch], gr_j)
                dbim_ref[jb] += _dot_tn(uvs[b][:, ch], gi_j)

        @pl.when(n == nt - 1)
        def _():
            dh0_ref[...] = p_scr[...]

    tok, hin_spec, state = _s5_specs(B, T, nt, not rev)
    hs_spec = pl.BlockSpec((B, T, 2 * S5_HALF), tok.index_map)
    w_in, w_out = _resident((S5_BLOCKS, S5_BC, S5_BS)), _resident((S5_BLOCKS, S5_BS, S5_BC))
    wspecs = [w_in, w_in, w_out, w_out]
    in_specs = [tok] + ([tok] if has_dy else []) + [hs_spec] + wspecs + [_resident((2, S5_HALF)), hin_spec, state]
    args = [u] + ([dy] if has_dy else []) + [hs, bre, bim, ctop, cbot, arow, hin, dhend]
    return pl.pallas_call(
        body, name=name, grid=(nt,), in_specs=in_specs,
        out_specs=[tok] + wspecs + [_resident((2, S5_HALF)), state],
        out_shape=[SDS((B, L, D_S5), f32), SDS((S5_BLOCKS, S5_BC, S5_BS), f32), SDS((S5_BLOCKS, S5_BC, S5_BS), f32),
                   SDS((S5_BLOCKS, S5_BS, S5_BC), f32), SDS((S5_BLOCKS, S5_BS, S5_BC), f32), SDS((2, S5_HALF), f32),
                   SDS((B, 2, S5_HALF), f32)],
        scratch_shapes=[pltpu.VMEM((B, T + 2 * PAD, S5_HALF), f32), pltpu.VMEM((B, T + 2 * PAD, S5_HALF), f32),
                        pltpu.VMEM((B, T, S5_HALF), f32), pltpu.VMEM((B, T, S5_HALF), f32),
                        pltpu.VMEM((B, 2, S5_HALF), f32)],
        compiler_params=_cparams(1),
    )(*args)


def _glu_fn(u, y0, y1, z, dsk, wg, bg):
    g = _gelu(dsk * u + y0 + y1)
    return g * jax.nn.sigmoid(_mm(g, wg) + bg) * _silu(z)


CONV_ROWS = 16


def _shift(x, s):
    L = x.shape[0]
    k = (-s) % L
    return x if k == 0 else pltpu.roll(x, k, axis=0)


def _r16(v):
    return v.astype(bf16).astype(f32)


def _conv_masks(L, is_ctx):
    t = lax.broadcasted_iota(jnp.int32, (L, 1), 0)
    if is_ctx:
        return t == L - 1, t == 0, None, None
    col = jnp.bitwise_and(t, GRID_W - 1)
    return col == GRID_W - 1, col == 0, t >= GRID_W, t < L - GRID_W


def _conv_sides(xv, masks):
    no_left, no_right, _, _ = masks
    return _shift(jnp.where(no_left, 0.0, xv), -1), _shift(jnp.where(no_right, 0.0, xv), 1)


def _conv_pre(xv, w_ref, masks, is_ctx):
    xv = _r16(xv)
    wv = _r16(w_ref[...])
    xl, xr = _conv_sides(xv, masks)
    z = [wv[3 * di:3 * di + 1, :] * xl + wv[3 * di + 1:3 * di + 2, :] * xv + wv[3 * di + 2:3 * di + 3, :] * xr
         for di in ((1,) if is_ctx else (0, 1, 2))]
    if is_ctx:
        return z[0]
    _, _, has_up, has_down = masks
    return z[1] + jnp.where(has_up, _shift(z[0], -GRID_W), 0.0) + jnp.where(has_down, _shift(z[2], GRID_W), 0.0)


def _conv_pre_bwd(xv, w_ref, dpre, masks, is_ctx, dw_ref):
    no_left, no_right, has_up, has_down = masks
    xv, dpre, wv = _r16(xv), _r16(dpre), _r16(w_ref[...])
    xl, xr = _conv_sides(xv, masks)
    if is_ctx:
        dz = {1: dpre}
    else:
        dz = {0: _shift(jnp.where(has_up, dpre, 0.0), GRID_W), 1: dpre, 2: _shift(jnp.where(has_down, dpre, 0.0), -GRID_W)}
    dxl = dxc = dxr = None
    for di, d in dz.items():
        for dj, side in enumerate((xl, xv, xr)):
            dw_ref[3 * di + dj:3 * di + dj + 1, :] = jnp.sum(d * side, axis=0, keepdims=True)
        tl, tc, tr = (wv[3 * di + dj:3 * di + dj + 1, :] * d for dj in range(3))
        dxl, dxc, dxr = (tl, tc, tr) if dxl is None else (dxl + tl, dxc + tc, dxr + tr)
    return dxc + jnp.where(no_left, 0.0, _shift(dxl, 1)) + jnp.where(no_right, 0.0, _shift(dxr, -1))


def _qk_post(pre, is_norm, scale):
    s = _silu(pre)
    nrm = lax.rsqrt(jnp.sum(s * s, axis=-1, keepdims=True) + NORM_EPS)
    return s * jnp.where(is_norm, nrm * scale, 1.0)


def _conv_kind():
    ct = pl.program_id(1)
    return ct < 2 * GDN_HEADS, jnp.where(ct < GDN_HEADS, GDN_HEAD ** -0.5, 1.0).astype(f32)


def conv_fwd(qkv, w16, *, is_ctx, name):
    B, L, C = qkv.shape
    spec = pl.BlockSpec((None, L, GDN_HEAD), lambda b, ct: (b, 0, ct))
    wspec = pl.BlockSpec((CONV_ROWS, GDN_HEAD), lambda b, ct: (0, ct))

    def body(x_ref, w_ref, o_ref, pre_ref):
        is_norm, scale = _conv_kind()
        pre = _conv_pre(x_ref[...], w_ref, _conv_masks(L, is_ctx), is_ctx)
        pre_ref[...] = pre
        o_ref[...] = _qk_post(pre, is_norm, scale)

    return pl.pallas_call(body, name=name, grid=(B, C // GDN_HEAD), in_specs=[spec, wspec], out_specs=[spec, spec],
                          out_shape=[SDS((B, L, C), f32)] * 2, compiler_params=_cparams(2))(qkv, w16)


def conv_bwd(qkv, pre, w16, da0, da1, *, is_ctx, name):
    B, L, C = qkv.shape
    spec = pl.BlockSpec((None, L, GDN_HEAD), lambda b, ct: (b, 0, ct))
    wspec = pl.BlockSpec((CONV_ROWS, GDN_HEAD), lambda b, ct: (0, ct))
    dwspec = pl.BlockSpec((None, CONV_ROWS, GDN_HEAD), lambda b, ct: (b, 0, ct))

    def body(x_ref, pre_ref, w_ref, d0_ref, d1_ref, dx_ref, dw_ref):
        is_norm, scale = _conv_kind()
        _, vjp = jax.vjp(lambda p: _qk_post(p, is_norm, scale), pre_ref[...])
        dpre = vjp(d0_ref[...] + d1_ref[...])[0]
        dw_ref[...] = jnp.zeros_like(dw_ref)
        dx_ref[...] = _conv_pre_bwd(x_ref[...], w_ref, dpre, _conv_masks(L, is_ctx), is_ctx, dw_ref)

    return pl.pallas_call(body, name=name, grid=(B, C // GDN_HEAD), in_specs=[spec, spec, wspec, spec, spec],
                          out_specs=[spec, dwspec], out_shape=[SDS((B, L, C), f32), SDS((B, CONV_ROWS, C), f32)],
                          compiler_params=_cparams(2))(qkv, pre, w16, da0, da1)


def _gates_fn(ba, alog, dtb):
    T = ba.shape[0]
    lane = lax.broadcasted_iota(jnp.int32, ba.shape, 1)
    ii = lax.broadcasted_iota(jnp.int32, (T, T), 0)
    jj = lax.broadcasted_iota(jnp.int32, (T, T), 1)
    same = jnp.right_shift(ii, 6) == jnp.right_shift(jj, 6)
    lmat = jnp.logical_and(same, ii >= jj).astype(f32)
    umat = jnp.logical_and(same, ii <= jj).astype(f32)
    g = jnp.where(lane >= 8, -jnp.exp(alog) * jax.nn.softplus(ba + dtb), 0.0)
    gc = jnp.where(lane >= 12, _dot_hi(umat, g), _dot_hi(lmat, g))
    return jnp.where(lane < 8, jax.nn.sigmoid(ba), gc)


def gates_fwd(ba, alog, dtb, *, name):
    B, L, _ = ba.shape
    T = min(TOK_TILE, L)
    t = _tok(T, N_GATE)

    def body(ba_ref, al_ref, dt_ref, o_ref):
        o_ref[...] = _gates_fn(ba_ref[...], al_ref[...], dt_ref[...])

    return pl.pallas_call(body, name=name, grid=(B, L // T),
                          in_specs=[t, _resident((1, N_GATE)), _resident((1, N_GATE))], out_specs=t,
                          out_shape=SDS((B, L, N_GATE), f32), compiler_params=_cparams(2))(ba, alog, dtb)


def gates_bwd(ba, alog, dtb, dbg, *, name):
    B, L, _ = ba.shape
    T = min(TOK_TILE, L)
    t = _tok(T, N_GATE)
    small = _resident((1, N_GATE))

    def body(ba_ref, al_ref, dt_ref, d_ref, dba_ref, dal_ref, ddt_ref):
        @pl.when(_first_step())
        def _():
            dal_ref[...] = jnp.zeros_like(dal_ref)
            ddt_ref[...] = jnp.zeros_like(ddt_ref)

        _, vjp = jax.vjp(_gates_fn, ba_ref[...], al_ref[...], dt_ref[...])
        dba, dal, ddt = vjp(d_ref[...])
        dba_ref[...] = dba
        dal_ref[...] += dal
        ddt_ref[...] += ddt

    return pl.pallas_call(body, name=name, grid=(B, L // T), in_specs=[t, small, small, t],
                          out_specs=[t, small, small],
                          out_shape=[SDS((B, L, N_GATE), f32), SDS((1, N_GATE), f32), SDS((1, N_GATE), f32)],
                          compiler_params=_cparams(2))(ba, alog, dtb, dbg)


@jax.custom_vjp
def _inv_unit_tri(mats):
    n = mats[0].shape[0]
    eye = (lax.broadcasted_iota(jnp.int32, (n, n), 0) == lax.broadcasted_iota(jnp.int32, (n, n), 1)).astype(f32)
    xs = [eye - a for a in mats]
    sq = [_dot(a, a) for a in mats]
    ps = sq
    k = 2
    while k < n:
        xs = [x + _dot(x, p) for x, p in zip(xs, ps)]
        k *= 2
        if k < n:
            ps = [_dot(p, p) for p in ps]
    return tuple(_dot(p, x) - a for p, x, a in zip(sq, xs, mats))


def _inv_unit_tri_fwd(mats):
    ns = _inv_unit_tri(mats)
    return ns, ns


def _inv_unit_tri_bwd(ns, dns):
    ys = [dn + _dot_tn(nn, dn) for nn, dn in zip(ns, dns)]
    return (tuple(-(y + _dot_nt(y, nn)) for y, nn in zip(ys, ns)),)


_inv_unit_tri.defvjp(_inv_unit_tri_fwd, _inv_unit_tri_bwd)


@jax.custom_vjp
def _inv_unit_tri_saved(mats, saved):
    return saved


_inv_unit_tri_saved.defvjp(lambda mats, saved: (saved, saved),
                           lambda ns, dns: _inv_unit_tri_bwd(ns, dns) + (tuple(jnp.zeros_like(n) for n in ns),))


def _gdn_chunk(heads, *, revs, saved=None, with_n=False):
    n = heads[0][0].shape[0]
    ii = lax.broadcasted_iota(jnp.int32, (n, n), 0)
    jj = lax.broadcasted_iota(jnp.int32, (n, n), 1)
    row = lax.broadcasted_iota(jnp.int32, (n, 1), 0)
    lower = {False: ii >= jj, True: ii <= jj}
    strict = {False: ii > jj, True: ii < jj}
    last = {False: n - 1, True: 0}
    H = range(len(heads))
    q, k, v, beta, gc, gr, s = (list(t) for t in zip(*heads))
    decay = [jnp.where(lower[revs[h]], jnp.exp(jnp.where(lower[revs[h]], gc[h] - gr[h], 0.0)), 0.0) for h in H]
    kk = [_mm_nt(k[h], k[h]) for h in H]
    qk = [_mm_nt(q[h], k[h]) * decay[h] for h in H]
    qs = [_mm(q[h], s[h]) for h in H]
    a_mat = tuple(jnp.where(strict[revs[h]], beta[h] * kk[h] * decay[h], 0.0) for h in H)
    gamma = [jnp.exp(gc[h]) for h in H]
    g_last = [jnp.sum(jnp.where(row == last[revs[h]], gc[h], 0.0), axis=0, keepdims=True) for h in H]
    nmat = _inv_unit_tri(a_mat) if saved is None else _inv_unit_tri_saved(a_mat, saved)
    bv = [beta[h] * v[h] for h in H]
    bk = [(beta[h] * gamma[h]) * k[h] for h in H]
    u0 = [bv[h] + _mm(nmat[h], bv[h]) for h in H]
    w = [bk[h] + _mm(nmat[h], bk[h]) for h in H]
    k_out = [k[h] * jnp.exp(g_last[h] - gc[h]) for h in H]
    u = [u0[h] - _mm(w[h], s[h]) for h in H]
    o = [gamma[h] * qs[h] + _mm(qk[h], u[h]) for h in H]
    s_new = [jnp.exp(g_last[h]) * s[h] + _mm_tn(k_out[h], u[h]) for h in H]
    outs = tuple((o[h], s_new[h]) for h in H)
    return (outs, nmat) if with_n else outs


def _gdn_specs(B, nc, rev):
    def cidx(n):
        return (nc - 1 - n) if rev else n
    tok = lambda width: pl.BlockSpec((B, CHUNK, width), lambda n: (0, cidx(n), 0))
    rowspec = pl.BlockSpec((B, None, N_GATE, CHUNK), lambda n: (0, cidx(n), 0, 0))
    st = pl.BlockSpec((B, GDN_HEADS, GDN_HEAD, GDN_HEAD), lambda n: (0, 0, 0, 0))
    ck = pl.BlockSpec((B, None, GDN_HEADS, GDN_HEAD, GDN_HEAD), lambda n: (0, cidx(n), 0, 0, 0))
    nsp = pl.BlockSpec((B, None, GDN_HEADS, CHUNK, CHUNK), lambda n: (0, cidx(n), 0, 0, 0))
    return tok, rowspec, st, ck, nsp


def _gdn_head_args(qkv_ref, bg_ref, bgr_ref, b, d, h):
    col = d * GDN_HEADS + h
    q = qkv_ref[b, :, h * GDN_HEAD:(h + 1) * GDN_HEAD]
    k = qkv_ref[b, :, D_GDN + h * GDN_HEAD:D_GDN + (h + 1) * GDN_HEAD]
    v = qkv_ref[b, :, 2 * D_GDN + h * GDN_HEAD:2 * D_GDN + (h + 1) * GDN_HEAD]
    bgv = bg_ref[b]
    return q, k, v, bgv[:, col:col + 1], bgv[:, 8 + col:9 + col], bgr_ref[b][8 + col:9 + col, :]


def _gdn_chains(B):
    return [(d, b, h) for d in range(N_DIR) for b in range(B) for h in range(GDN_HEADS)]


def gdn_fwd(qkv, bg, bgr, s0s, *, need_o, name):
    B, L, _ = qkv.shape
    nc = L // CHUNK
    specs = [_gdn_specs(B, nc, d == 1) for d in range(N_DIR)]
    chains = _gdn_chains(B)
    state_shape = (B, GDN_HEADS, GDN_HEAD, GDN_HEAD)

    def body(*refs):
        ins = [refs[3 * d:3 * d + 3] for d in range(N_DIR)]
        s0_refs = refs[6:8]
        k = 8
        o_refs = refs[k:k + 2] if need_o else None
        k += 2 if need_o else 0
        ck_refs, n_refs, sf_refs, s_scrs = refs[k:k + 2], refs[k + 2:k + 4], refs[k + 4:k + 6], refs[k + 6:k + 8]
        n = pl.program_id(0)

        @pl.when(n == 0)
        def _():
            for d in range(N_DIR):
                s_scrs[d][...] = s0_refs[d][...]

        for d in range(N_DIR):
            ck_refs[d][...] = s_scrs[d][...]
        heads = tuple(_gdn_head_args(*ins[d], b, d, h) + (s_scrs[d][b, h],) for d, b, h in chains)
        outs, nmat = _gdn_chunk(heads, revs=tuple(d == 1 for d, _, _ in chains), with_n=True)
        for (d, b, h), (o, s_new), nn in zip(chains, outs, nmat):
            if need_o:
                o_refs[d][b, :, h * GDN_HEAD:(h + 1) * GDN_HEAD] = o
            s_scrs[d][b, h] = s_new
            n_refs[d][b, h] = nn

        @pl.when(n == nc - 1)
        def _():
            for d in range(N_DIR):
                sf_refs[d][...] = s_scrs[d][...]

    in_specs, out_o, out_ck, out_n, out_sf = [], [], [], [], []
    for tok, rowspec, st, ck, nsp in specs:
        in_specs += [tok(3 * D_GDN), tok(N_GATE), rowspec]
        out_o.append(tok(D_GDN))
        out_ck.append(ck)
        out_n.append(nsp)
        out_sf.append(st)
    in_specs += [specs[0][2]] * 2
    out_specs = (out_o if need_o else []) + out_ck + out_n + out_sf
    out_shape = (([SDS((B, L, D_GDN), f32)] * 2 if need_o else []) + [SDS((B, nc) + state_shape[1:], f32)] * 2
                 + [SDS((B, nc, GDN_HEADS, CHUNK, CHUNK), f32)] * 2 + [SDS(state_shape, f32)] * 2)
    res = pl.pallas_call(
        body, name=name, grid=(nc,), in_specs=in_specs, out_specs=out_specs, out_shape=out_shape,
        scratch_shapes=[pltpu.VMEM(state_shape, f32)] * 2, compiler_params=_cparams(1),
    )(qkv, bg, bgr, qkv, bg, bgr, *s0s)
    if need_o:
        return res[0:2], res[2:4], res[4:6], res[6:8]
    return res[0:2], res[2:4], res[4:6]


def gdn_bwd(qkv, bg, bgr, cks, ns, do, dsfs, *, name):
    B, L, _ = qkv.shape
    nc = L // CHUNK
    has_do = do is not None
    specs = [_gdn_specs(B, nc, d != 1) for d in range(N_DIR)]
    chains = _gdn_chains(B)
    state_shape = (B, GDN_HEADS, GDN_HEAD, GDN_HEAD)
    per_dir = 6 if has_do else 5

    def body(*refs):
        ins = [refs[per_dir * d:per_dir * d + per_dir] for d in range(N_DIR)]
        k = per_dir * N_DIR
        dsf_refs = refs[k:k + 2]
        outs = [refs[k + 2 + 3 * d:k + 5 + 3 * d] for d in range(N_DIR)]
        ds0_refs, ds_scrs = refs[k + 8:k + 10], refs[k + 10:k + 12]
        n = pl.program_id(0)

        @pl.when(n == 0)
        def _():
            for d in range(N_DIR):
                ds_scrs[d][...] = dsf_refs[d][...]

        lane = lax.broadcasted_iota(jnp.int32, (CHUNK, N_GATE), 1)
        sub = lax.broadcasted_iota(jnp.int32, (N_GATE, CHUNK), 0)
        heads = tuple(_gdn_head_args(*ins[d][:3], b, d, h) + (ins[d][3][b, h],) for d, b, h in chains)
        saved = tuple(ins[d][4][b, h] for d, b, h in chains)
        _, vjp = jax.vjp(functools.partial(_gdn_chunk, revs=tuple(d == 1 for d, _, _ in chains), saved=saved), heads)
        zero = jnp.zeros((CHUNK, GDN_HEAD), f32)
        cts = tuple(((ins[d][5][b, :, h * GDN_HEAD:(h + 1) * GDN_HEAD] if has_do else zero), ds_scrs[d][b, h])
                    for d, b, h in chains)
        (dheads,) = vjp(cts)
        dbg_acc = [[jnp.zeros((CHUNK, N_GATE), f32) for _ in range(B)] for _ in range(N_DIR)]
        dbgr_acc = [[jnp.zeros((N_GATE, CHUNK), f32) for _ in range(B)] for _ in range(N_DIR)]
        for (d, b, h), (dq, dk, dv, db, dgc, dgr, ds) in zip(chains, dheads):
            col = d * GDN_HEADS + h
            dqkv_ref = outs[d][0]
            dqkv_ref[b, :, h * GDN_HEAD:(h + 1) * GDN_HEAD] = dq
            dqkv_ref[b, :, D_GDN + h * GDN_HEAD:D_GDN + (h + 1) * GDN_HEAD] = dk
            dqkv_ref[b, :, 2 * D_GDN + h * GDN_HEAD:2 * D_GDN + (h + 1) * GDN_HEAD] = dv
            dbg_acc[d][b] = dbg_acc[d][b] + jnp.where(lane == col, db, 0.0) + jnp.where(lane == 8 + col, dgc, 0.0)
            dbgr_acc[d][b] = dbgr_acc[d][b] + jnp.where(sub == 8 + col, dgr, 0.0)
            ds_scrs[d][b, h] = ds
        for d in range(N_DIR):
            for b in range(B):
                outs[d][1][b] = dbg_acc[d][b]
                outs[d][2][b] = dbgr_acc[d][b]

        @pl.when(n == nc - 1)
        def _():
            for d in range(N_DIR):
                ds0_refs[d][...] = ds_scrs[d][...]

    in_specs, args, out_specs, out_shape = [], [], [], []
    for d, (tok, rowspec, st, ck, nsp) in enumerate(specs):
        in_specs += [tok(3 * D_GDN), tok(N_GATE), rowspec, ck, nsp] + ([tok(D_GDN)] if has_do else [])
        args += [qkv, bg, bgr, cks[d], ns[d]] + ([do] if has_do else [])
        out_specs += [tok(3 * D_GDN), tok(N_GATE), rowspec]
        out_shape += [SDS((B, L, 3 * D_GDN), f32), SDS((B, L, N_GATE), f32), SDS((B, nc, N_GATE, CHUNK), f32)]
    st = specs[0][2]
    in_specs += [st, st]
    args += list(dsfs)
    out_specs += [st, st]
    out_shape += [SDS(state_shape, f32)] * 2
    res = pl.pallas_call(
        body, name=name, grid=(nc,), in_specs=in_specs, out_specs=out_specs, out_shape=out_shape,
        scratch_shapes=[pltpu.VMEM(state_shape, f32)] * 2, compiler_params=_cparams(1),
    )(*args)
    return (res[0], res[3]), (res[1], res[4]), (res[2], res[5]), (res[6], res[7])


def _gnorm_fn(o0, o1, z, w):
    o = o0 + o1
    return o * lax.rsqrt(jnp.mean(o * o, axis=-1, keepdims=True) + NORM_EPS) * w * _silu(z)


def _head_loss(y, x, gate, lng, lnb, tgt):
    r = DEEPNORM_ALPHA * x + gate * y
    mu = jnp.mean(r, axis=-1, keepdims=True)
    rc = r - mu
    var = jnp.mean(rc * rc, axis=-1, keepdims=True)
    err = rc * lax.rsqrt(var + LN_EPS) * lng + lnb - tgt
    return (0.5 / D_MODEL) * jnp.sum(jnp.sum(err * err, axis=-1, keepdims=True), axis=0, keepdims=True)


def tail_fwd_bwd(u, y0, y1, z_s5, o0, o1, z_gdn, x, tgt, gate, lng, lnb, ws, wg, dsk, wglu, bglu, nw):
    B, L, _ = x.shape
    T = min(TOK_TILE, L)

    def body(u_ref, y0_ref, y1_ref, z_ref, o0_ref, o1_ref, zg_ref, x_ref, t_ref, gate_ref, lng_ref, lnb_ref, ws_ref,
             wg_ref, dsk_ref, wglu_ref, bglu_ref, nw_ref,
             loss_ref, du_ref, dys_ref, dz_ref, do_ref, dzg_ref, gx_ref, dws_ref, dwg_ref, dgate_ref, dlng_ref, dlnb_ref,
             ddsk_ref, dwglu_ref, dbglu_ref, dnw_ref):
        n = pl.program_id(1)

        @pl.when(_first_step())
        def _():
            for r in (dws_ref, dwg_ref, dlng_ref, dlnb_ref, ddsk_ref, dwglu_ref, dbglu_ref, dnw_ref):
                r[...] = jnp.zeros_like(r)

        @pl.when(n == 0)
        def _():
            loss_ref[...] = jnp.zeros_like(loss_ref)
            dgate_ref[...] = jnp.zeros_like(dgate_ref)

        s5o, glu_vjp = jax.vjp(_glu_fn, u_ref[...], y0_ref[...], y1_ref[...], z_ref[...], dsk_ref[...],
                               wglu_ref[...].astype(f32), bglu_ref[...])
        heads = []
        for h in range(GDN_HEADS):
            sl = slice(h * GDN_HEAD, (h + 1) * GDN_HEAD)
            heads.append(jax.vjp(_gnorm_fn, o0_ref[:, sl], o1_ref[:, sl], zg_ref[:, sl], nw_ref[...]))
        sv = s5o.astype(bf16)
        gv = jnp.concatenate([out for out, _ in heads], axis=1).astype(bf16)
        y = _dot(sv, ws_ref[...]) + _dot(gv, wg_ref[...])
        loss, vjp = jax.vjp(lambda *a: _head_loss(*a, t_ref[...]), y, x_ref[...], gate_ref[...], lng_ref[...],
                            lnb_ref[...])
        dy, dx, dgate, dlng, dlnb = vjp(jnp.ones((1, 1), f32))
        loss_ref[...] += jnp.broadcast_to(loss, loss_ref.shape)
        dyb = dy.astype(bf16)
        gx_ref[...] = dx
        dws_ref[...] += _dot_tn(sv, dyb)
        dwg_ref[...] += _dot_tn(gv, dyb)
        dgate_ref[...] += dgate
        dlng_ref[...] += dlng
        dlnb_ref[...] += dlnb
        du, dys, _, dz, ddsk, dwglu, dbglu = glu_vjp(_dot_nt(dyb, ws_ref[...]))
        du_ref[...], dys_ref[...], dz_ref[...] = du, dys, dz
        ddsk_ref[...] += ddsk
        dwglu_ref[...] += dwglu
        dbglu_ref[...] += dbglu
        dgdo = _dot_nt(dyb, wg_ref[...])
        for h, (_, hvjp) in enumerate(heads):
            sl = slice(h * GDN_HEAD, (h + 1) * GDN_HEAD)
            do, _, dzg, dnw = hvjp(dgdo[:, sl])
            do_ref[:, sl] = do
            dzg_ref[:, sl] = dzg
            dnw_ref[...] += dnw

    half, full = _tok(T, D_S5), _tok(T, D_MODEL)
    row = _resident((1, D_MODEL))
    wsp = _resident((D_S5, D_MODEL))
    r512, rglu, r128 = _resident((1, D_S5)), _resident((D_S5, D_S5)), _resident((1, GDN_HEAD))
    return pl.pallas_call(
        body, name="tail_fwd_bwd", grid=(B, L // T),
        in_specs=[half] * 7 + [full, full, _per_batch(1, D_MODEL), row, row, wsp, wsp, r512, rglu, r512, r128],
        out_specs=[_per_batch(8, LANES)] + [half] * 5 + [full, wsp, wsp, _per_batch(1, D_MODEL), row, row, r512, rglu, r512,
                                                           r128],
        out_shape=[SDS((B, 8, LANES), f32)] + [SDS((B, L, D_S5), f32)] * 5 + [
            SDS((B, L, D_MODEL), f32), SDS((D_S5, D_MODEL), f32), SDS((D_GDN, D_MODEL), f32), SDS((B, 1, D_MODEL), f32),
            SDS((1, D_MODEL), f32), SDS((1, D_MODEL), f32), SDS((1, D_S5), f32), SDS((D_S5, D_S5), f32), SDS((1, D_S5), f32),
            SDS((1, GDN_HEAD), f32)],
        compiler_params=_cparams(2),
    )(u, y0, y1, z_s5, o0, o1, z_gdn, x, tgt, gate, lng, lnb, ws, wg, dsk, wglu, bglu, nw)


def _adamw_math(w, g, m, v):
    nm = ADAM_B1 * m + (1.0 - ADAM_B1) * g
    nv = ADAM_B2 * v + (1.0 - ADAM_B2) * jnp.square(g)
    m_hat = nm / (1.0 - ADAM_B1 ** ADAM_STEP)
    v_hat = nv / (1.0 - ADAM_B2 ** ADAM_STEP)
    return -ADAM_LR * (m_hat / (jnp.sqrt(v_hat) + ADAM_EPS) + ADAM_WD * w), nm, nv


def _row_tile(rows, cap=512):
    for t in range(min(cap, rows), 15, -1):
        if rows % t == 0 and t % 16 == 0:
            return t
    return rows


def adamw_3d(w, g, m, v, *, lead=False, name):
    R, C = (w.shape[0], w.shape[2]) if lead else w.shape[1:]
    if lead:
        T = next(t for t in range(min(256, R), 0, -1) if R % t == 0)
        spec = pl.BlockSpec((T, 1, C), lambda i: (i, 0, 0))
    else:
        T = _row_tile(R)
        spec = pl.BlockSpec((None, T, C), lambda i: (0, i, 0))

    def body(w_ref, g_ref, m_ref, v_ref, d_ref, nm_ref, nv_ref):
        d_ref[...], nm_ref[...], nv_ref[...] = _adamw_math(w_ref[...], g_ref[...], m_ref[...], v_ref[...])

    return pl.pallas_call(body, name=name, grid=(R // T,), in_specs=[spec] * 4, out_specs=[spec] * 3,
                          out_shape=[SDS(w.shape, f32)] * 3, compiler_params=_cparams(1))(w, g, m, v)


def adamw_small(ws, gs, ms, vs):
    n = len(ws)

    def body(*refs):
        outs = refs[4 * n:]
        for i in range(n):
            d, nm, nv = _adamw_math(refs[i][...], refs[n + i][...], refs[2 * n + i][...], refs[3 * n + i][...])
            outs[i][...], outs[n + i][...], outs[2 * n + i][...] = d, nm, nv

    res = pl.pallas_call(body, name="adamw_small", out_shape=[SDS(w.shape, f32) for w in ws] * 3,
                         compiler_params=pltpu.CompilerParams(vmem_limit_bytes=VMEM_LIMIT))(*ws, *gs, *ms, *vs)
    return res[:n], res[n:2 * n], res[2 * n:]


def sum_cores(own, got, *, name):
    A, H, C = own.shape
    T = _row_tile(H)
    spec = pl.BlockSpec((None, T, C), lambda a, i: (a, i, 0))

    def body(a_ref, b_ref, q32_ref, q16_ref):
        q = a_ref[...] + b_ref[...]
        q32_ref[...] = q
        q16_ref[...] = q.astype(bf16)

    return pl.pallas_call(body, name=name, grid=(A, H // T), in_specs=[spec, spec], out_specs=[spec, spec],
                          out_shape=[SDS((A, H, C), f32), SDS((A, H, C), bf16)], compiler_params=_cparams(2))(own, got)


def sum_chips(mine, rec, cpos, *, name):
    H, C = mine.shape
    T = _row_tile(H)
    nt = H // T

    def body(c_ref, m_ref, r_ref, f_ref):
        f_ref[...] = ((m_ref[...] + r_ref[0].astype(f32)) + r_ref[1].astype(f32)) + r_ref[2].astype(f32)

    grid_spec = pltpu.PrefetchScalarGridSpec(
        num_scalar_prefetch=1, grid=(nt,),
        in_specs=[pl.BlockSpec((T, C), lambda i, c_ref: (i, 0)), pl.BlockSpec((3, T, C), lambda i, c_ref: (0, i, 0))],
        out_specs=pl.BlockSpec((None, T, C), lambda i, c_ref: (0, c_ref[0] * nt + i, 0)))
    return pl.pallas_call(body, name=name, grid_spec=grid_spec, out_shape=SDS((1, 2 * H, C), f32),
                          compiler_params=_cparams(1))(cpos.reshape(1).astype(jnp.int32), mine, rec)


CHIP_FLIPS = ((1, 0), (0, 1), (1, 1))


def _pos():
    return lax.axis_index("x"), lax.axis_index("y"), lax.axis_index("c")


def _comm_call(body, srcs, out_sds, n_remote, n_local, name):
    any_spec = pl.BlockSpec(memory_space=pl.ANY)
    return pl.pallas_call(
        body, name=name, in_specs=[any_spec] * len(srcs), out_specs=[any_spec] * len(out_sds), out_shape=out_sds,
        scratch_shapes=[pltpu.SemaphoreType.DMA((n_remote,)), pltpu.SemaphoreType.DMA((n_remote,)),
                        pltpu.SemaphoreType.DMA((max(n_local, 1),))],
        compiler_params=pltpu.CompilerParams(has_side_effects=True),
    )(*srcs)


def _remote(src, dst, send_sems, recv_sems, k, target):
    return pltpu.make_async_remote_copy(src, dst, send_sems.at[k], recv_sems.at[k], device_id=target,
                                        device_id_type=MESH)


def _half_rows(c, rows):
    half = rows // 2
    return pl.ds(pl.multiple_of(c * half, 8), half)


def gather_shards(shards):
    nt = len(shards)

    def body(*refs):
        srcs, outs = refs[:nt], refs[nt:2 * nt]
        send_sems, recv_sems, _ = refs[2 * nt:]
        x, y, c = _pos()
        j = 2 * x + y
        sib = (x, y, 1 - c)
        own = [_remote(srcs[t], outs[t].at[j], send_sems, recv_sems, 7 * t + 6, sib) for t in range(nt)]
        first, passed = [], []
        for k, (fx, fy) in enumerate(CHIP_FLIPS):
            tx, ty = x ^ fx, y ^ fy
            jk = 2 * tx + ty
            for t in range(nt):
                rows = _half_rows(c, srcs[t].shape[0])
                first.append(_remote(srcs[t].at[rows], outs[t].at[j, rows], send_sems, recv_sems, 7 * t + k, (tx, ty, c)))
                passed.append(_remote(outs[t].at[jk, rows], outs[t].at[jk, rows], send_sems, recv_sems, 7 * t + 3 + k, sib))
        for cp in first + own:
            cp.start()
        for a, b in zip(first, passed):
            a.wait_recv()
            b.start()
        for cp in passed + own:
            cp.wait_recv()
        for cp in first + passed + own:
            cp.wait_send()

    return _comm_call(body, shards, [SDS((4,) + s.shape, s.dtype) for s in shards], 7 * nt, 0, "gather_shards")


def swap_halves(ps):
    nt = len(ps)

    def body(*refs):
        srcs, outs = refs[:nt], refs[nt:2 * nt]
        send_sems, recv_sems, _ = refs[2 * nt:]
        x, y, c = _pos()
        cps = [_remote(srcs[t].at[a, _half_rows(1 - c, srcs[t].shape[1])], outs[t].at[a], send_sems, recv_sems, 4 * t + a,
                       (x, y, 1 - c)) for t in range(nt) for a in range(4)]
        for cp in cps:
            cp.start()
        for cp in cps:
            cp.wait()

    return _comm_call(body, ps, [SDS((4, p.shape[1] // 2, p.shape[2]), p.dtype) for p in ps], 4 * nt, 0, "swap_halves")


def scatter_to_chips(qs):
    nt = len(qs)

    def body(*refs):
        srcs, outs = refs[:nt], refs[nt:2 * nt]
        send_sems, recv_sems, _ = refs[2 * nt:]
        x, y, c = _pos()
        cps = []
        for k, (fx, fy) in enumerate(CHIP_FLIPS):
            tx, ty = x ^ fx, y ^ fy
            for t in range(nt):
                cps.append(_remote(srcs[t].at[2 * tx + ty], outs[t].at[k], send_sems, recv_sems, 3 * t + k, (tx, ty, c)))
        for cp in cps:
            cp.start()
        for cp in cps:
            cp.wait()

    return _comm_call(body, qs, [SDS((3,) + q.shape[1:], q.dtype) for q in qs], 3 * nt, 0, "scatter_to_chips")


def join_halves(fs):
    nt = len(fs)

    def body(*refs):
        outs = refs[nt:2 * nt]
        send_sems, recv_sems, _ = refs[2 * nt:]
        x, y, c = _pos()
        cps = []
        for t in range(nt):
            mine = outs[t].at[0, _half_rows(c, outs[t].shape[1])]
            cps.append(_remote(mine, mine, send_sems, recv_sems, t, (x, y, 1 - c)))
        for cp in cps:
            cp.start()
        for cp in cps:
            cp.wait()

    any_spec = pl.BlockSpec(memory_space=pl.ANY)
    return pl.pallas_call(
        body, name="join_halves", in_specs=[any_spec] * nt, out_specs=[any_spec] * nt,
        out_shape=[SDS(f.shape, f.dtype) for f in fs], input_output_aliases={t: t for t in range(nt)},
        scratch_shapes=[pltpu.SemaphoreType.DMA((nt,)), pltpu.SemaphoreType.DMA((nt,)), pltpu.SemaphoreType.DMA((1,))],
        compiler_params=pltpu.CompilerParams(has_side_effects=True),
    )(*fs)


DEV_FLIPS = tuple((fx, fy, fc) for fx in (0, 1) for fy in (0, 1) for fc in (0, 1))[1:]


def gather_devices(block, *, name):
    def body(src, out, send_sems, recv_sems, loc_sems):
        x, y, c = _pos()
        me = 4 * x + 2 * y + c
        mine = pltpu.make_async_copy(src, out.at[me], loc_sems.at[0])
        mine.start()
        cps = [_remote(src, out.at[me], send_sems, recv_sems, k, (x ^ fx, y ^ fy, c ^ fc))
               for k, (fx, fy, fc) in enumerate(DEV_FLIPS)]
        for cp in cps:
            cp.start()
        for cp in cps:
            cp.wait()
        mine.wait()

    return _comm_call(body, [block], [SDS((N_DEV,) + block.shape, block.dtype)], 7, 1, name)[0]


def exchange_devices(blocks, *, name):
    def body(src, out, send_sems, recv_sems, loc_sems):
        x, y, c = _pos()
        me = 4 * x + 2 * y + c
        mine = pltpu.make_async_copy(src.at[me], out.at[me], loc_sems.at[0])
        mine.start()
        cps = []
        for k, (fx, fy, fc) in enumerate(DEV_FLIPS):
            tx, ty, tc = x ^ fx, y ^ fy, c ^ fc
            cps.append(_remote(src.at[4 * tx + 2 * ty + tc], out.at[me], send_sems, recv_sems, k, (tx, ty, tc)))
        for cp in cps:
            cp.start()
        for cp in cps:
            cp.wait()
        mine.wait()

    return _comm_call(body, [blocks], [SDS(blocks.shape, blocks.dtype)], 7, 1, name)[0]


def gather_small(s):
    def body(src, out, send_sems, recv_sems, _):
        x, y, c = _pos()
        j = 2 * x + y
        cps = [_remote(src, out.at[j], send_sems, recv_sems, k, (x ^ fx, y ^ fy, c)) for k, (fx, fy) in enumerate(CHIP_FLIPS)]
        cps.append(_remote(src, out.at[j], send_sems, recv_sems, 3, (x, y, 1 - c)))
        for cp in cps:
            cp.start()
        for cp in cps:
            cp.wait()

    return _comm_call(body, [s], [SDS((4,) + s.shape, s.dtype)], 4, 0, "gather_small")[0]


SMALL_SHAPES = ((1, 2, 32, 64), (1, 2, 32, 64), (1, 2, 32), (1, 2, 32, 16, 64),
                (1, 2, 32, 16, 64), (1, 2, 32, 16, 64), (1, 2, 32, 16, 64), (1, D_S5), (1, D_S5), (1, 2, 4), (1, 2, 4),
                (1, GDN_HEAD), (1, D_MODEL), (1, D_MODEL))
SMALL_SWAPPED = (3, 4)


def _size(shape):
    return functools.reduce(lambda p, q: p * q, shape)


SMALL_ROWS = tuple(-(-_size(s) // LANES) for s in SMALL_SHAPES)
SMALL_TOTAL = 2176
SMALL_QUARTER = SMALL_TOTAL // 4


def _rows(a):
    flat = a.reshape(-1)
    pad = (-flat.shape[0]) % LANES
    if pad:
        flat = jnp.concatenate([flat, jnp.zeros((pad,), flat.dtype)])
    return flat.reshape(-1, LANES)


def _pack_small(parts):
    rows = [_rows(p) for p in parts]
    rows.append(jnp.zeros((SMALL_TOTAL - sum(SMALL_ROWS), LANES), f32))
    return jnp.concatenate(rows, axis=0)


def _unpack_small(buf):
    out, r = [], 0
    for s, n in zip(SMALL_SHAPES, SMALL_ROWS):
        out.append(buf[r:r + n].reshape(-1)[:_size(s)].reshape(s))
        r += n
    return out


def _as_2d(a):
    return a.reshape(1, -1) if a.ndim == 1 else a.reshape(-1, a.shape[-1])


S5_BG = S5_GROUPS // S5_BLOCKS


def _block_diag_in(bb):
    eye = jnp.eye(S5_BG, dtype=bb.dtype)
    b4 = bb.reshape(S5_BLOCKS, S5_BG, S5_GROUP, S5_STATE)
    return jnp.einsum('jgcp,gh->jgchp', b4, eye).reshape(S5_BLOCKS, S5_BC, S5_BS)


def _block_diag_in_t(d):
    d6 = d.reshape(S5_BLOCKS, S5_BG, S5_GROUP, S5_BG, S5_STATE)
    return jnp.einsum('jgcgp->jgcp', d6).reshape(S5_GROUPS, S5_GROUP * S5_STATE)


def _block_diag_out(cm):
    eye = jnp.eye(S5_BG, dtype=cm.dtype)
    c4 = cm.reshape(S5_BLOCKS, S5_BG, S5_GROUP, S5_STATE)
    return jnp.einsum('jgcp,gh->jhpgc', c4, eye).reshape(S5_BLOCKS, S5_BS, S5_BC)


def _block_diag_out_t(d):
    d6 = d.reshape(S5_BLOCKS, S5_BG, S5_STATE, S5_BG, S5_GROUP)
    return jnp.einsum('jgpgc->jgcp', d6).reshape(S5_GROUPS, S5_GROUP, S5_STATE)


def _to_chunk_rows(a):
    B, L, W = a.shape
    return a.reshape(B, L // CHUNK, CHUNK, W).transpose(0, 1, 3, 2)


def _from_chunk_rows(a):
    B, nc, W, _ = a.shape
    return a.transpose(0, 1, 3, 2).reshape(B, nc * CHUNK, W)


def local_step(x, ctx, tgt, m, w_in, lam_re, lam_im, log_dt, b_re, b_im, c_re, c_im, s5_d,
               w_glu, b_glu, conv16, a_log, dt_bias, norm_w, w_out, ln_g, ln_b):
    B, L, _ = x.shape
    zeros_state = jnp.zeros((B, GDN_HEADS, GDN_HEAD, GDN_HEAD), f32)

    shift, scale, gate = m[:B, :D_MODEL], m[:B, D_MODEL:2 * D_MODEL], m[:B, 2 * D_MODEL:]
    mod = jnp.stack([scale, shift], axis=1)
    mod_c = jnp.broadcast_to(jnp.stack([m[B, D_MODEL:2 * D_MODEL], m[B, :D_MODEL]], axis=0)[None], (B, 2, D_MODEL))

    u, z_s5, qkv, z_gdn, ba = in_proj_fwd(x, mod, w_in, name="in_proj_fwd")
    uc, _, qkvc, _, bac = in_proj_fwd(ctx, mod_c, w_in, name="in_proj_fwd_ctx")

    ng = N_DIR * S5_GROUPS
    zoh_in = (lam_re.reshape(ng, S5_STATE), lam_im.reshape(ng, S5_STATE), log_dt.reshape(ng, 1),
              b_re.reshape(ng, S5_GROUP * S5_STATE), b_im.reshape(ng, S5_GROUP * S5_STATE))
    expand = (jnp.arange(S5_GROUP * S5_STATE)[None, :] % S5_STATE == jnp.arange(S5_STATE)[:, None]).astype(f32)
    ar, ai, bbr, bbi = s5_zoh_fwd(*zoh_in, expand)
    bbr16, bbi16 = bbr.astype(bf16), bbi.astype(bf16)
    c_re16 = c_re.reshape(N_DIR, S5_GROUPS, S5_GROUP, S5_STATE).astype(bf16)
    c_im16 = (-c_im).reshape(N_DIR, S5_GROUPS, S5_GROUP, S5_STATE).astype(bf16)
    s5w, ys, hins, hins_c, hss, hss_c = [], [], [], [], [], []
    for d in range(N_DIR):
        g = slice(d * S5_GROUPS, (d + 1) * S5_GROUPS)
        wd = (_block_diag_in(bbr16[g]), _block_diag_in(bbi16[g]), _block_diag_out(c_re16[d]), _block_diag_out(c_im16[d]),
              jnp.stack([ar[g].reshape(-1), ai[g].reshape(-1)], axis=0))
        s5w.append(wd)
        hs_c, hin_c, hend_c = s5_scan_fwd(uc, *wd, jnp.zeros((B, 2, S5_HALF), f32), d=d, need_y=False,
                                          name=f"s5_fwd_ctx{d}")
        y_d, hs_d, hin, _ = s5_scan_fwd(u, *wd, hend_c, d=d, need_y=True, name=f"s5_fwd{d}")
        hss.append(hs_d)
        hss_c.append(hs_c)
        ys.append(y_d)
        hins.append(hin)
        hins_c.append(hin_c)
    glu_w = (s5_d.reshape(1, D_S5), w_glu, b_glu.reshape(1, D_S5))

    act, pre = conv_fwd(qkv, conv16, is_ctx=False, name="conv_fwd")
    act_c, pre_c = conv_fwd(qkvc, conv16, is_ctx=True, name="conv_fwd_ctx")
    pad8 = jnp.zeros((1, 8), f32)
    alog16 = jnp.concatenate([pad8, a_log.reshape(1, 8)], axis=1)
    dtb16 = jnp.concatenate([pad8, dt_bias.reshape(1, 8)], axis=1)
    bg = gates_fwd(ba, alog16, dtb16, name="gates_fwd")
    bg_c = gates_fwd(bac, alog16, dtb16, name="gates_fwd_ctx")
    bgr, bgr_c = _to_chunk_rows(bg), _to_chunk_rows(bg_c)
    cks_c, ns_c, s_c = gdn_fwd(act_c, bg_c, bgr_c, (zeros_state, zeros_state), need_o=False, name="gdn_fwd_ctx")
    os_, cks, ns, _ = gdn_fwd(act, bg, bgr, s_c, need_o=True, name="gdn_fwd")
    nw = norm_w.reshape(1, GDN_HEAD)

    (loss8, du_skip, dy, dz_s5, do, dz_gdn, gx_res, dws, dwg, dgate, dlng, dlnb, d_s5_d, d_w_glu, d_b_glu,
     d_norm_w) = tail_fwd_bwd(u, ys[0], ys[1], z_s5, os_[0], os_[1], z_gdn, x, tgt, gate[:, None, :],
                              ln_g.reshape(1, D_MODEL), ln_b.reshape(1, D_MODEL), w_out[:D_S5], w_out[D_S5:], *glu_w, nw)
    loss = jnp.sum(loss8[:, 0, 0])
    d_w_out = jnp.concatenate([dws, dwg], axis=0)

    dacts, dbgs, dbgrs, ds0s = gdn_bwd(act, bg, bgr, cks, ns, do, (zeros_state, zeros_state), name="gdn_bwd")
    dacts_c, dbgs_c, dbgrs_c, _ = gdn_bwd(act_c, bg_c, bgr_c, cks_c, ns_c, None, ds0s, name="gdn_bwd_ctx")
    dbg = dbgs[0] + dbgs[1] + _from_chunk_rows(dbgrs[0] + dbgrs[1])
    dbg_c = dbgs_c[0] + dbgs_c[1] + _from_chunk_rows(dbgrs_c[0] + dbgrs_c[1])
    dba, dal, ddt = gates_bwd(ba, alog16, dtb16, dbg, name="gates_bwd")
    dbac, dal_c, ddt_c = gates_bwd(bac, alog16, dtb16, dbg_c, name="gates_bwd_ctx")
    d_a_log = (dal + dal_c)[:, 8:].reshape(1, N_DIR, GDN_HEADS)
    d_dt_bias = (ddt + ddt_c)[:, 8:].reshape(1, N_DIR, GDN_HEADS)
    dqkv, dcw = conv_bwd(qkv, pre, conv16, dacts[0], dacts[1], is_ctx=False, name="conv_bwd")
    dqkvc, dcw_c = conv_bwd(qkvc, pre_c, conv16, dacts_c[0], dacts_c[1], is_ctx=True, name="conv_bwd_ctx")
    d_conv16 = jnp.sum(dcw, axis=0) + jnp.sum(dcw_c, axis=0)

    dus, ducs = [du_skip], []
    dar, dai, dbbr, dbbi, dcre, dcim = [], [], [], [], [], []
    for d in range(N_DIR):
        du_d, dbre1, dbim1, dct1, dcb1, da1, dh0 = s5_scan_bwd(u, dy, hss[d], *s5w[d], hins[d],
                                                                jnp.zeros((B, 2, S5_HALF), f32), d=d, name=f"s5_bwd{d}")
        duc_d, dbre2, dbim2, _, _, da2, _ = s5_scan_bwd(uc, None, hss_c[d], *s5w[d], hins_c[d], dh0, d=d,
                                                        name=f"s5_bwd_ctx{d}")
        dus.append(du_d)
        ducs.append(duc_d)
        da = da1 + da2
        dar.append(da[0].reshape(S5_GROUPS, S5_STATE))
        dai.append(da[1].reshape(S5_GROUPS, S5_STATE))
        dbbr.append(_block_diag_in_t(dbre1 + dbre2))
        dbbi.append(_block_diag_in_t(dbim1 + dbim2))
        dcre.append(_block_diag_out_t(dct1))
        dcim.append(-_block_diag_out_t(dcb1))
    dlr, dli, dldt, dbre, dbim = s5_zoh_bwd(*zoh_in, expand, jnp.concatenate(dar, 0), jnp.concatenate(dai, 0),
                                            jnp.concatenate(dbbr, 0), jnp.concatenate(dbbi, 0))
    d_s5 = (dlr, dli, dldt, dbre, dbim, jnp.stack(dcre, 0), jnp.stack(dcim, 0))

    padg = lambda a: jnp.concatenate([a, jnp.zeros(a.shape[:2] + (LANES - N_GATE,), f32)], axis=2)
    zc = jnp.zeros_like(uc)
    dw_c, dmod_c = in_proj_bwd(ctx, mod_c, (tuple(ducs), zc, dqkvc, zc, padg(dbac)), w_in, None, None,
                               name="in_proj_bwd_ctx")
    d_w_in, dmod, grad_x = in_proj_bwd(x, mod, (tuple(dus), dz_s5, dqkv, dz_gdn, padg(dba)), w_in, gx_res, dw_c,
                                       name="in_proj_bwd")
    dmod_c = jnp.sum(dmod_c, axis=0)

    dm_rows = jnp.concatenate([dmod[:, 1], dmod[:, 0], dgate[:, 0]], axis=1)
    dm_ctx = jnp.concatenate([dmod_c[1], dmod_c[0], jnp.zeros((D_MODEL,), f32)])[None]
    dm = jnp.concatenate([dm_rows, dm_ctx], axis=0)
    small = (*d_s5, d_s5_d, d_b_glu, d_a_log, d_dt_bias, d_norm_w, dlng, dlnb)
    small = tuple(g.reshape(s) for g, s in zip(small, SMALL_SHAPES))
    return loss, grad_x, (d_w_in, d_w_out, d_w_glu, d_conv16), small, dm


SHARDED = (1, 3, 18, 12, 14)
REDUCED = (3, 18, 12, 14)
UNSHARDED = tuple(i for i in range(21) if i not in SHARDED)
SMALL = tuple(i for i in UNSHARDED if i not in (0, 2))
W_IN_SHARD = 772


def _conv_rows(w):
    return jnp.concatenate([w.reshape(9, w.shape[-1]), jnp.zeros((CONV_ROWS - 9, w.shape[-1]), f32)], axis=0)


def kernel(x, c, ctx, c_ctx, w_ada, b_ada, w_in, s5_lambda_re, s5_lambda_im, s5_log_dt, s5_b_re, s5_b_im, s5_c_re, s5_c_im, s5_d, w_glu, b_glu, conv_w, gdn_a_log, gdn_dt_bias, gdn_norm_w, w_out, ln_g, ln_b, loss_target, m_c_ctx, m_w_ada, m_b_ada, m_w_in, m_s5_lambda_re, m_s5_lambda_im, m_s5_log_dt, m_s5_b_re, m_s5_b_im, m_s5_c_re, m_s5_c_im, m_s5_d, m_w_glu, m_b_glu, m_conv_w, m_gdn_a_log, m_gdn_dt_bias, m_gdn_norm_w, m_w_out, m_ln_g, m_ln_b, v_c_ctx, v_w_ada, v_b_ada, v_w_in, v_s5_lambda_re, v_s5_lambda_im, v_s5_log_dt, v_s5_b_re, v_s5_b_im, v_s5_c_re, v_s5_c_im, v_s5_d, v_w_glu, v_b_glu, v_conv_w, v_gdn_a_log, v_gdn_dt_bias, v_gdn_norm_w, v_w_out, v_ln_g, v_ln_b):
    weights = [c_ctx, w_ada, b_ada, w_in, s5_lambda_re, s5_lambda_im, s5_log_dt, s5_b_re, s5_b_im, s5_c_re, s5_c_im,
               s5_d, w_glu, b_glu, conv_w, gdn_a_log, gdn_dt_bias, gdn_norm_w, w_out, ln_g, ln_b]
    ms = [m_c_ctx, m_w_ada, m_b_ada, m_w_in, m_s5_lambda_re, m_s5_lambda_im, m_s5_log_dt, m_s5_b_re, m_s5_b_im,
          m_s5_c_re, m_s5_c_im, m_s5_d, m_w_glu, m_b_glu, m_conv_w, m_gdn_a_log, m_gdn_dt_bias, m_gdn_norm_w, m_w_out,
          m_ln_g, m_ln_b]
    vs = [v_c_ctx, v_w_ada, v_b_ada, v_w_in, v_s5_lambda_re, v_s5_lambda_im, v_s5_log_dt, v_s5_b_re, v_s5_b_im,
          v_s5_c_re, v_s5_c_im, v_s5_d, v_w_glu, v_b_glu, v_conv_w, v_gdn_a_log, v_gdn_dt_bias, v_gdn_norm_w, v_w_out,
          v_ln_g, v_ln_b]
    cpos = lax.axis_index("c")
    jchip = 2 * lax.axis_index("x") + lax.axis_index("y")

    c_all = gather_devices(c, name="gather_c")
    cc = jnp.concatenate([c_all, jnp.broadcast_to(c_ctx[None, None, :], (N_DEV, 1, D_MODEL)),
                          jnp.zeros((N_DEV, 5, D_MODEL), f32)], axis=1)
    w_ada16 = w_ada[0].astype(bf16)
    b_cols = lax.dynamic_slice_in_dim(b_ada, jchip * ADA_SHARD, ADA_SHARD, axis=1)
    m_mine = exchange_devices(ada_fwd(cc, w_ada16, b_cols), name="exchange_m")
    m_rows = jnp.concatenate([m_mine[2 * j, :3] for j in range(4)], axis=1)

    conv_shard = _conv_rows(conv_w)
    g_in, g_out, g_glu, g_conv = gather_shards(
        [w_in[0].astype(bf16), w_out[0].astype(bf16), w_glu[0].astype(bf16), conv_shard])
    w_in_pad = jnp.concatenate([g_in[0], g_in[1], g_in[2], g_in[3], jnp.zeros((D_MODEL, IN_PAD - P_IN), bf16)], axis=1)
    conv16 = g_conv.transpose(1, 0, 2).reshape(CONV_ROWS, 3 * D_GDN)

    swap = lambda a: jnp.swapaxes(a, 3, 4)
    loss, grad_x, big, small, dm_rows = local_step(
        x, ctx, loss_target, m_rows, w_in_pad, s5_lambda_re, s5_lambda_im, s5_log_dt, swap(s5_b_re), swap(s5_b_im),
        s5_c_re, s5_c_im, s5_d, g_glu.reshape(D_S5, D_S5), b_glu, conv16, gdn_a_log, gdn_dt_bias, gdn_norm_w,
        g_out.reshape(D_MODEL, D_MODEL), ln_g, ln_b)
    loss = lax.psum(loss, ("x", "y", "c"))

    dm8 = jnp.concatenate([dm_rows, jnp.zeros((5, 3 * D_MODEL), f32)], axis=0)
    dm_by_chip = dm8.reshape(8, 4, ADA_SHARD).transpose(1, 0, 2)
    dm_cols = exchange_devices(jnp.repeat(dm_by_chip, 2, axis=0), name="exchange_dm")
    g_w_ada, pb = ada_bwd(cc, w_ada16, dm_cols)
    pb_all = gather_devices(pb, name="gather_p")
    g_c_ctx = c_ctx_bwd(pb_all, c_ctx[None, :])[0]
    g_b_ada = jnp.concatenate([pb_all[2 * j, 1:2, :ADA_SHARD] for j in range(4)], axis=1)

    d_w_in, d_w_out, d_w_glu, d_conv16 = big
    slabs = [d_w_in[:, :P_IN].reshape(D_MODEL, 4, W_IN_SHARD).transpose(1, 0, 2),
             d_w_out.reshape(4, D_MODEL // 4, D_MODEL),
             d_w_glu.reshape(4, D_S5 // 4, D_S5),
             d_conv16.reshape(CONV_ROWS, 4, 3 * D_GDN // 4).transpose(1, 0, 2),
             _pack_small(small).reshape(4, SMALL_QUARTER, LANES)]
    got = swap_halves(slabs)
    q32, q16 = [], []
    for t, (s, g) in enumerate(zip(slabs, got)):
        own = lax.dynamic_index_in_dim(s.reshape(4, 2, s.shape[1] // 2, s.shape[2]), cpos, axis=1, keepdims=False)
        a, b = sum_cores(own, g, name=f"sum_cores{t}")
        q32.append(a)
        q16.append(b)
    rec = scatter_to_chips(q16)
    fs = [sum_chips(lax.dynamic_index_in_dim(q, jchip, axis=0, keepdims=False), r, cpos, name=f"sum_chips{t}")
          for t, (q, r) in enumerate(zip(q32, rec))]
    red = join_halves(fs)
    g_small = _unpack_small(gather_small(red[4][0]).reshape(SMALL_TOTAL, LANES))
    g_shard = {1: g_w_ada, 3: red[0], 18: red[1], 12: red[2], 14: red[3]}

    grads, deltas, new_m, new_v = [None] * 21, [None] * 21, [None] * 21, [None] * 21
    for t, i in enumerate(SHARDED):
        conv, win = i == 14, i == 3
        prep = (lambda a: _conv_rows(a)[None]) if conv else ((lambda a: jnp.transpose(a, (2, 0, 1))) if win else (lambda a: a))
        g = jnp.transpose(g_shard[i], (2, 0, 1)) if win else g_shard[i]
        d, nm, nv = adamw_3d(prep(weights[i]), g, prep(ms[i]), prep(vs[i]), lead=win, name=f"adamw{t}")
        for lst, val in ((grads, g), (deltas, d), (new_m, nm), (new_v, nv)):
            lst[i] = (val[0, :9].reshape(weights[i].shape) if conv else (jnp.transpose(val, (1, 2, 0)) if win else val))
    g_un = {0: g_c_ctx, 2: g_b_ada, **{i: g_small[n] for n, i in enumerate(SMALL)}}
    swapped = [SMALL[n] for n in SMALL_SWAPPED]
    small_in = lambda lst: [_as_2d(swap(lst[i]) if i in swapped else lst[i]) for i in UNSHARDED]
    sm = adamw_small(small_in(weights), [_as_2d(g_un[i]) for i in UNSHARDED], small_in(ms), small_in(vs))
    for n, i in enumerate(UNSHARDED):
        back = ((lambda a: swap(a.reshape(swap(weights[i]).shape))) if i in swapped
                else (lambda a: a.reshape(weights[i].shape)))
        grads[i] = back(g_un[i])
        for lst, res in ((deltas, sm[0]), (new_m, sm[1]), (new_v, sm[2])):
            lst[i] = back(res[n])
    return (loss, grad_x, *grads, *deltas, *new_m, *new_v)
```

```python
import functools

import jax
import jax.numpy as jnp
from jax import lax
from jax.experimental import pallas as pl
from jax.experimental.pallas import tpu as pltpu

f32 = jnp.float32
bf16 = jnp.bfloat16
SDS = jax.ShapeDtypeStruct

D_MODEL = 1024
D_S5 = 512
S5_GROUP = 16
S5_GROUPS = 32
S5_STATE = 64
S5_HALF = S5_GROUPS * S5_STATE
D_GDN = 512
GDN_HEAD = 128
GDN_HEADS = 4
CHUNK = 64
GRID_W = 64
N_DIR = 2
P_IN = 3088
DEEPNORM_ALPHA = 2.0 ** 0.25
LN_EPS = 1e-5
NORM_EPS = 1e-6
ADAM_LR, ADAM_B1, ADAM_B2, ADAM_EPS, ADAM_WD, ADAM_STEP = 0.001, 0.9, 0.999, 1e-08, 0.01, 10

LANES = 128
VMEM_LIMIT = 56 * 1024 * 1024
TOK_TILE = 256
S5_TILE = 256
MESH = pl.DeviceIdType.MESH


def _cparams(n_grid):
    return pltpu.CompilerParams(dimension_semantics=("arbitrary",) * n_grid, vmem_limit_bytes=VMEM_LIMIT)


def _dot(a, b):
    return jnp.dot(a.astype(bf16), b.astype(bf16), preferred_element_type=f32)


def _dot_nt(a, b):
    return lax.dot_general(a.astype(bf16), b.astype(bf16), (((1,), (1,)), ((), ())), preferred_element_type=f32)


def _dot_tn(a, b):
    return lax.dot_general(a.astype(bf16), b.astype(bf16), (((0,), (0,)), ((), ())), preferred_element_type=f32)


def _dot_hi(a, b):
    return jnp.dot(a, b, precision=lax.Precision.HIGHEST, preferred_element_type=f32)


def _dot_h3(a, b):
    return jnp.dot(a, b, precision=lax.Precision.HIGH, preferred_element_type=f32)


@jax.custom_vjp
def _mm(a, b):
    return _dot(a, b)


@jax.custom_vjp
def _mm_nt(a, b):
    return _dot_nt(a, b)


@jax.custom_vjp
def _mm_tn(a, b):
    return _dot_tn(a, b)


_mm.defvjp(lambda a, b: (_dot(a, b), (a, b)), lambda r, g: (_mm_nt(g, r[1]), _mm_tn(r[0], g)))
_mm_nt.defvjp(lambda a, b: (_dot_nt(a, b), (a, b)), lambda r, g: (_mm(g, r[1]), _mm_tn(g, r[0])))
_mm_tn.defvjp(lambda a, b: (_dot_tn(a, b), (a, b)), lambda r, g: (_mm_nt(r[1], g), _mm(r[0], g)))


def _silu(x):
    return x * jax.nn.sigmoid(x)


def _gelu(x):
    return 0.5 * x * (1.0 + lax.erf(x * (2.0 ** -0.5)))


def _resident(shape):
    nd = len(shape)
    return pl.BlockSpec(shape, lambda *_: (0,) * nd, pipeline_mode=pl.Buffered(1))


def _tok(tile, width, nt=None, rev=False):
    if rev:
        return pl.BlockSpec((None, tile, width), lambda b, n: (b, nt - 1 - n, 0))
    return pl.BlockSpec((None, tile, width), lambda b, n: (b, n, 0))


def _per_batch(rows, width):
    return pl.BlockSpec((None, rows, width), lambda b, n: (b, 0, 0))


def _first_step():
    return jnp.logical_and(pl.program_id(0) == 0, pl.program_id(1) == 0)


ADA_SHARD = 3 * D_MODEL // 4
N_DEV = 8


def ada_fwd(cc, w, b):
    def body(cc_ref, w_ref, b_ref, m_ref):
        for k in range(N_DEV):
            m_ref[k] = _dot(_silu(cc_ref[k]), w_ref[...]) + b_ref[...]

    return pl.pallas_call(body, name="ada_fwd", out_shape=SDS((N_DEV, 8, ADA_SHARD), f32),
                          compiler_params=pltpu.CompilerParams(vmem_limit_bytes=VMEM_LIMIT))(cc, w, b)


def ada_bwd(cc, w, dmj):
    def body(cc_ref, w_ref, dmj_ref, dw_ref, pb_ref):
        dw = jnp.zeros((D_MODEL, ADA_SHARD), f32)
        p = jnp.zeros((8, D_MODEL), f32)
        db = jnp.zeros((1, ADA_SHARD), f32)
        for k in range(N_DEV):
            dw = dw + _dot_tn(_silu(cc_ref[k]), dmj_ref[k])
            p = p + _dot_nt(dmj_ref[k], w_ref[...])
            db = db + jnp.sum(dmj_ref[k], axis=0, keepdims=True)
        dw_ref[0] = dw
        pb_ref[...] = jnp.zeros_like(pb_ref)
        pb_ref[0:1, :] = p[2:3, :]
        pb_ref[1:2, 0:ADA_SHARD] = db

    return pl.pallas_call(
        body, name="ada_bwd", out_shape=[SDS((1, D_MODEL, ADA_SHARD), f32), SDS((8, D_MODEL), f32)],
        compiler_params=pltpu.CompilerParams(vmem_limit_bytes=VMEM_LIMIT))(cc, w, dmj)


def c_ctx_bwd(pb_all, c_ctx):
    def body(p_ref, c_ref, d_ref):
        ds = ((p_ref[0, 0:1, :] + p_ref[2, 0:1, :]) + p_ref[4, 0:1, :]) + p_ref[6, 0:1, :]
        _, vjp = jax.vjp(_silu, c_ref[...])
        d_ref[...] = vjp(ds)[0]

    return pl.pallas_call(body, name="c_ctx_bwd", out_shape=SDS((1, D_MODEL), f32))(pb_all, c_ctx)


N_GATE = 2 * N_DIR * GDN_HEADS
IN_WIDTHS = (D_S5, D_S5, 3 * D_GDN, D_GDN, LANES)
IN_OFFS = (0, 512, 1024, 2560, 3072)
IN_PAD = 3200


def in_proj_fwd(x, mod, w, *, name):
    B, L, _ = x.shape
    T = min(TOK_TILE, L)

    def body(x_ref, mod_ref, w_ref, *o_refs):
        h = (x_ref[...] * (1.0 + mod_ref[0:1, :]) + mod_ref[1:2, :]).astype(bf16)
        for o_ref, off, wd in zip(o_refs, IN_OFFS, IN_WIDTHS):
            r = _dot(h, w_ref[:, off:off + wd])
            o_ref[...] = r[:, :o_ref.shape[-1]]

    outw = (D_S5, D_S5, 3 * D_GDN, D_GDN, N_GATE)
    return pl.pallas_call(
        body, name=name, grid=(B, L // T),
        in_specs=[_tok(T, D_MODEL), _per_batch(2, D_MODEL), _resident((D_MODEL, IN_PAD))],
        out_specs=[_tok(T, wd) for wd in outw],
        out_shape=[SDS((B, L, wd), f32) for wd in outw],
        compiler_params=_cparams(2),
    )(x, mod, w)


def in_proj_bwd(x, mod, ds, w, gx_res, dw_start, *, name):
    B, L, _ = x.shape
    T = min(TOK_TILE, L)
    with_dx = gx_res is not None
    with_start = dw_start is not None
    n_u = len(ds[0])

    def body(*refs):
        x_ref, mod_ref = refs[0], refs[1]
        du_refs = refs[2:2 + n_u]
        d_refs = refs[2 + n_u:6 + n_u]
        w_ref = refs[6 + n_u]
        k = 7 + n_u
        if with_dx:
            gx_ref = refs[k]
            k += 1
        if with_start:
            start_ref = refs[k]
            k += 1
        dw_ref, dmod_ref = refs[k], refs[k + 1]
        if with_dx:
            dx_ref = refs[k + 2]
        n = pl.program_id(1)

        @pl.when(_first_step())
        def _():
            dw_ref[...] = start_ref[...] if with_start else jnp.zeros_like(dw_ref)

        @pl.when(n == 0)
        def _():
            dmod_ref[...] = jnp.zeros_like(dmod_ref)

        xv = x_ref[...]
        scale1 = 1.0 + mod_ref[0:1, :]
        h = (xv * scale1 + mod_ref[1:2, :]).astype(bf16)
        du = du_refs[0][...]
        for r in du_refs[1:]:
            du = du + r[...]
        dh = jnp.zeros((T, D_MODEL), f32)
        for dv, off, wd in zip([du] + [r[...] for r in d_refs], IN_OFFS, IN_WIDTHS):
            dv = dv.astype(bf16)
            dh = dh + _dot_nt(dv, w_ref[:, off:off + wd])
            dw_ref[:, off:off + wd] += _dot_tn(h, dv)
        dmod_ref[0:1, :] += jnp.sum(dh * xv, axis=0, keepdims=True)
        dmod_ref[1:2, :] += jnp.sum(dh, axis=0, keepdims=True)
        if with_dx:
            dx_ref[...] = gx_ref[...] + dh * scale1

    in_specs = ([_tok(T, D_MODEL), _per_batch(2, D_MODEL)] + [_tok(T, D_S5)] * n_u + [_tok(T, wd) for wd in IN_WIDTHS[1:]]
                + [_resident((D_MODEL, IN_PAD))])
    args = [x, mod, *ds[0], *ds[1:], w]
    out_specs = [_resident((D_MODEL, IN_PAD)), _per_batch(2, D_MODEL)]
    out_shape = [SDS((D_MODEL, IN_PAD), f32), SDS((B, 2, D_MODEL), f32)]
    if with_dx:
        in_specs.append(_tok(T, D_MODEL))
        args.append(gx_res)
        out_specs.append(_tok(T, D_MODEL))
        out_shape.append(SDS((B, L, D_MODEL), f32))
    if with_start:
        in_specs.append(_resident((D_MODEL, IN_PAD)))
        args.append(dw_start)
    return pl.pallas_call(body, name=name, grid=(B, L // T), in_specs=in_specs, out_specs=out_specs,
                          out_shape=out_shape, compiler_params=_cparams(2))(*args)


def _s5_zoh(lr, li, ldt, bre, bim, expand):
    dt = jnp.exp(ldt)
    zr, zi = lr * dt, li * dt
    e = jnp.exp(zr)
    ar, ai = e * jnp.cos(zi), e * jnp.sin(zi)
    den = lr * lr + li * li
    czr = ((ar - 1.0) * lr + ai * li) / den
    czi = (ai * lr - (ar - 1.0) * li) / den
    czr_e, czi_e = _dot_hi(czr, expand), _dot_hi(czi, expand)
    return ar, ai, czr_e * bre - czi_e * bim, czr_e * bim + czi_e * bre


_ZOH_OUT = [(N_DIR * S5_GROUPS, S5_STATE)] * 2 + [(N_DIR * S5_GROUPS, S5_STATE * S5_GROUP)] * 2


def s5_zoh_fwd(lr, li, ldt, bre, bim, expand):
    def body(lr_ref, li_ref, ldt_ref, bre_ref, bim_ref, e_ref, ar_ref, ai_ref, bbr_ref, bbi_ref):
        ar, ai, bbr, bbi = _s5_zoh(lr_ref[...], li_ref[...], ldt_ref[...], bre_ref[...], bim_ref[...], e_ref[...])
        ar_ref[...], ai_ref[...], bbr_ref[...], bbi_ref[...] = ar, ai, bbr, bbi

    return pl.pallas_call(body, name="s5_zoh_fwd", out_shape=[SDS(s, f32) for s in _ZOH_OUT])(
        lr, li, ldt, bre, bim, expand)


def s5_zoh_bwd(lr, li, ldt, bre, bim, expand, dar, dai, dbbr, dbbi):
    def body(lr_ref, li_ref, ldt_ref, bre_ref, bim_ref, e_ref, dar_ref, dai_ref, dbbr_ref, dbbi_ref,
             dlr_ref, dli_ref, dldt_ref, dbre_ref, dbim_ref):
        ev = e_ref[...]
        _, vjp = jax.vjp(lambda a, b, c, d, e: _s5_zoh(a, b, c, d, e, ev),
                         lr_ref[...], li_ref[...], ldt_ref[...], bre_ref[...], bim_ref[...])
        outs = vjp((dar_ref[...], dai_ref[...], dbbr_ref[...], dbbi_ref[...]))
        dlr_ref[...], dli_ref[...], dldt_ref[...], dbre_ref[...], dbim_ref[...] = outs

    shapes = [lr.shape, li.shape, ldt.shape, bre.shape, bim.shape]
    return pl.pallas_call(body, name="s5_zoh_bwd", out_shape=[SDS(s, f32) for s in shapes])(
        lr, li, ldt, bre, bim, expand, dar, dai, dbbr, dbbi)


def _scan_rows(T, rev, ar, ai, h0s, refs, off):
    def step(i, carry):
        t = off + ((T - 1 - i) if rev else i)
        out = []
        for (hr, hi), (r_ref, i_ref) in zip(carry, refs):
            nr = ar * hr - ai * hi + r_ref[pl.ds(t, 1), :]
            ni = ar * hi + ai * hr + i_ref[pl.ds(t, 1), :]
            r_ref[pl.ds(t, 1), :] = nr
            i_ref[pl.ds(t, 1), :] = ni
            out.append((nr, ni))
        return tuple(out)

    return lax.fori_loop(0, T, step, tuple(h0s))


S5_BLOCKS = 4
S5_BC = D_S5 // S5_BLOCKS
S5_BS = S5_HALF // S5_BLOCKS


def _s5_in(uv, bre_ref, bim_ref, hr_ref, hi_ref, off, T):
    for jb in range(S5_BLOCKS):
        uj = uv[:, jb * S5_BC:(jb + 1) * S5_BC]
        hr_ref[off:off + T, jb * S5_BS:(jb + 1) * S5_BS] = _dot(uj, bre_ref[jb])
        hi_ref[off:off + T, jb * S5_BS:(jb + 1) * S5_BS] = _dot(uj, bim_ref[jb])


def _s5_specs(B, T, nt, rev):
    tidx = (lambda n: nt - 1 - n) if rev else (lambda n: n)
    tok = pl.BlockSpec((B, T, D_S5), lambda n: (0, tidx(n), 0))
    hin = pl.BlockSpec((B, None, 2, S5_HALF), lambda n: (0, tidx(n), 0, 0))
    state = pl.BlockSpec((B, 2, S5_HALF), lambda n: (0, 0, 0))
    return tok, hin, state


def s5_scan_fwd(u, bre, bim, ctop, cbot, arow, h0, *, d, need_y, name):
    B, L, _ = u.shape
    T = min(S5_TILE, L)
    nt = L // T
    rev = d == 1

    def body(u_ref, bre_ref, bim_ref, ct_ref, cb_ref, a_ref, h0_ref, *rest):
        if need_y:
            y_ref, hs_ref, hin_ref, hend_ref, hr_scr, hi_scr, h_scr = rest
        else:
            hs_ref, hin_ref, hend_ref, hr_scr, hi_scr, h_scr = rest
        n = pl.program_id(0)

        @pl.when(n == 0)
        def _():
            h_scr[...] = h0_ref[...]

        hin_ref[...] = h_scr[...]
        for b in range(B):
            _s5_in(u_ref[b].astype(bf16), bre_ref, bim_ref, hr_scr.at[b], hi_scr.at[b], 0, T)
        hs = _scan_rows(T, rev, a_ref[0:1, :], a_ref[1:2, :], [(h_scr[b, 0:1, :], h_scr[b, 1:2, :]) for b in range(B)],
                        [(hr_scr.at[b], hi_scr.at[b]) for b in range(B)], 0)
        for b in range(B):
            h_scr[b, 0:1, :] = hs[b][0]
            h_scr[b, 1:2, :] = hs[b][1]
            hs_ref[b, :, 0:S5_HALF] = hr_scr[b].astype(bf16)
            hs_ref[b, :, S5_HALF:2 * S5_HALF] = hi_scr[b].astype(bf16)
            if need_y:
                for jb in range(S5_BLOCKS):
                    st = slice(jb * S5_BS, (jb + 1) * S5_BS)
                    y_ref[b, :, jb * S5_BC:(jb + 1) * S5_BC] = (_dot(hr_scr[b, :, st], ct_ref[jb])
                                                                 + _dot(hi_scr[b, :, st], cb_ref[jb]))

        @pl.when(n == nt - 1)
        def _():
            hend_ref[...] = h_scr[...]

    tok, hin_spec, state = _s5_specs(B, T, nt, rev)
    hs_spec = pl.BlockSpec((B, T, 2 * S5_HALF), tok.index_map)
    out_specs = [hs_spec, hin_spec, state]
    out_shape = [SDS((B, L, 2 * S5_HALF), bf16), SDS((B, nt, 2, S5_HALF), f32), SDS((B, 2, S5_HALF), f32)]
    if need_y:
        out_specs.insert(0, tok)
        out_shape.insert(0, SDS((B, L, D_S5), f32))
    w_in, w_out = _resident((S5_BLOCKS, S5_BC, S5_BS)), _resident((S5_BLOCKS, S5_BS, S5_BC))
    return pl.pallas_call(
        body, name=name, grid=(nt,),
        in_specs=[tok, w_in, w_in, w_out, w_out, _resident((2, S5_HALF)), state],
        out_specs=out_specs, out_shape=out_shape,
        scratch_shapes=[pltpu.VMEM((B, T, S5_HALF), f32), pltpu.VMEM((B, T, S5_HALF), f32),
                        pltpu.VMEM((B, 2, S5_HALF), f32)],
        compiler_params=_cparams(1),
    )(u, bre, bim, ctop, cbot, arow, h0)


def s5_scan_bwd(u, dy, hs, bre, bim, ctop, cbot, arow, hin, dhend, *, d, name):
    B, L, _ = u.shape
    T = min(S5_TILE, L)
    nt = L // T
    rev = d == 1
    has_dy = dy is not None
    PAD = 8

    def body(*refs):
        u_ref = refs[0]
        k = 1
        if has_dy:
            dy_ref = refs[1]
            k = 2
        hs_ref = refs[k]
        k += 1
        bre_ref, bim_ref, ct_ref, cb_ref, a_ref, hin_ref, dhend_ref = refs[k:k + 7]
        du_ref, dbre_ref, dbim_ref, dct_ref, dcb_ref, da_ref, dh0_ref = refs[k + 7:k + 14]
        hr_scr, hi_scr, gr_scr, gi_scr, p_scr = refs[k + 14:]
        n = pl.program_id(0)

        @pl.when(n == 0)
        def _():
            for r in (dbre_ref, dbim_ref, dct_ref, dcb_ref, da_ref):
                r[...] = jnp.zeros_like(r)
            p_scr[...] = dhend_ref[...]

        ar, ai = a_ref[0:1, :], a_ref[1:2, :]
        prev_row = PAD + T if rev else PAD - 1
        uvs = []
        for b in range(B):
            uvs.append(u_ref[b].astype(bf16))
            hr_scr[b, PAD:PAD + T, :] = hs_ref[b, :, 0:S5_HALF].astype(f32)
            hi_scr[b, PAD:PAD + T, :] = hs_ref[b, :, S5_HALF:2 * S5_HALF].astype(f32)
            hr_scr[b, prev_row:prev_row + 1, :] = hin_ref[b, 0:1, :]
            hi_scr[b, prev_row:prev_row + 1, :] = hin_ref[b, 1:2, :]
        if has_dy:
            for b in range(B):
                dyv = dy_ref[b].astype(bf16)
                for jb in range(S5_BLOCKS):
                    st = slice(jb * S5_BS, (jb + 1) * S5_BS)
                    dyj = dyv[:, jb * S5_BC:(jb + 1) * S5_BC]
                    gr_scr[b, :, st] = _dot_nt(dyj, ct_ref[jb])
                    gi_scr[b, :, st] = _dot_nt(dyj, cb_ref[jb])
                    dct_ref[jb] += _dot_tn(hr_scr[b, PAD:PAD + T, st], dyj)
                    dcb_ref[jb] += _dot_tn(hi_scr[b, PAD:PAD + T, st], dyj)
        else:
            gr_scr[...] = jnp.zeros_like(gr_scr)
            gi_scr[...] = jnp.zeros_like(gi_scr)

        def step(i, carry):
            t = i if rev else T - 1 - i
            tp = PAD + t + (1 if rev else -1)
            out = []
            for b, (pr, pi, dar, dai) in enumerate(carry):
                gr = gr_scr[b, pl.ds(t, 1), :] + pr
                gi = gi_scr[b, pl.ds(t, 1), :] + pi
                gr_scr[b, pl.ds(t, 1), :] = gr
                gi_scr[b, pl.ds(t, 1), :] = gi
                hpr = hr_scr[b, pl.ds(tp, 1), :]
                hpi = hi_scr[b, pl.ds(tp, 1), :]
                out.append((ar * gr + ai * gi, ar * gi - ai * gr, dar + hpr * gr + hpi * gi, dai + hpr * gi - hpi * gr))
            return tuple(out)

        zero = jnp.zeros((1, S5_HALF), f32)
        res = lax.fori_loop(0, T, step, tuple((p_scr[b, 0:1, :], p_scr[b, 1:2, :], zero, zero) for b in range(B)))
        for b in range(B):
            pr, pi, dar, dai = res[b]
            p_scr[b, 0:1, :] = pr
            p_scr[b, 1:2, :] = pi
            da_ref[0:1, :] += dar
            da_ref[1:2, :] += dai
            for jb in range(S5_BLOCKS):
                st = slice(jb * S5_BS, (jb + 1) * S5_BS)
                ch = slice(jb * S5_BC, (jb + 1) * S5_BC)
                gr_j = gr_scr[b, :, st].astype(bf16)
                gi_j = gi_scr[b, :, st].astype(bf16)
                du_ref[b, :, ch] = _dot_nt(gr_j, bre_ref[jb]) + _dot_nt(gi_j, bim_ref[jb])
                dbre_ref[jb] += _dot_tn(uvs[b][:, ch], gr_j)
                dbim_ref[jb] += _dot_tn(uvs[b][:, ch], gi_j)

        @pl.when(n == nt - 1)
        def _():
            dh0_ref[...] = p_scr[...]

    tok, hin_spec, state = _s5_specs(B, T, nt, not rev)
    hs_spec = pl.BlockSpec((B, T, 2 * S5_HALF), tok.index_map)
    w_in, w_out = _resident((S5_BLOCKS, S5_BC, S5_BS)), _resident((S5_BLOCKS, S5_BS, S5_BC))
    wspecs = [w_in, w_in, w_out, w_out]
    in_specs = [tok] + ([tok] if has_dy else []) + [hs_spec] + wspecs + [_resident((2, S5_HALF)), hin_spec, state]
    args = [u] + ([dy] if has_dy else []) + [hs, bre, bim, ctop, cbot, arow, hin, dhend]
    return pl.pallas_call(
        body, name=name, grid=(nt,), in_specs=in_specs,
        out_specs=[tok] + wspecs + [_resident((2, S5_HALF)), state],
        out_shape=[SDS((B, L, D_S5), f32), SDS((S5_BLOCKS, S5_BC, S5_BS), f32), SDS((S5_BLOCKS, S5_BC, S5_BS), f32),
                   SDS((S5_BLOCKS, S5_BS, S5_BC), f32), SDS((S5_BLOCKS, S5_BS, S5_BC), f32), SDS((2, S5_HALF), f32),
                   SDS((B, 2, S5_HALF), f32)],
        scratch_shapes=[pltpu.VMEM((B, T + 2 * PAD, S5_HALF), f32), pltpu.VMEM((B, T + 2 * PAD, S5_HALF), f32),
                        pltpu.VMEM((B, T, S5_HALF), f32), pltpu.VMEM((B, T, S5_HALF), f32),
                        pltpu.VMEM((B, 2, S5_HALF), f32)],
        compiler_params=_cparams(1),
    )(*args)


def _glu_fn(u, y0, y1, z, dsk, wg, bg):
    g = _gelu(dsk * u + y0 + y1)
    return g * jax.nn.sigmoid(_mm(g, wg) + bg) * _silu(z)


CONV_ROWS = 16


def _shift(x, s):
    L = x.shape[0]
    k = (-s) % L
    return x if k == 0 else pltpu.roll(x, k, axis=0)


def _r16(v):
    return v.astype(bf16).astype(f32)


def _conv_masks(L, is_ctx):
    t = lax.broadcasted_iota(jnp.int32, (L, 1), 0)
    if is_ctx:
        return t == L - 1, t == 0, None, None
    col = jnp.bitwise_and(t, GRID_W - 1)
    return col == GRID_W - 1, col == 0, t >= GRID_W, t < L - GRID_W


def _conv_sides(xv, masks):
    no_left, no_right, _, _ = masks
    return _shift(jnp.where(no_left, 0.0, xv), -1), _shift(jnp.where(no_right, 0.0, xv), 1)


def _conv_pre(xv, w_ref, masks, is_ctx):
    xv = _r16(xv)
    wv = _r16(w_ref[...])
    xl, xr = _conv_sides(xv, masks)
    z = [wv[3 * di:3 * di + 1, :] * xl + wv[3 * di + 1:3 * di + 2, :] * xv + wv[3 * di + 2:3 * di + 3, :] * xr
         for di in ((1,) if is_ctx else (0, 1, 2))]
    if is_ctx:
        return z[0]
    _, _, has_up, has_down = masks
    return z[1] + jnp.where(has_up, _shift(z[0], -GRID_W), 0.0) + jnp.where(has_down, _shift(z[2], GRID_W), 0.0)


def _conv_pre_bwd(xv, w_ref, dpre, masks, is_ctx, dw_ref):
    no_left, no_right, has_up, has_down = masks
    xv, dpre, wv = _r16(xv), _r16(dpre), _r16(w_ref[...])
    xl, xr = _conv_sides(xv, masks)
    if is_ctx:
        dz = {1: dpre}
    else:
        dz = {0: _shift(jnp.where(has_up, dpre, 0.0), GRID_W), 1: dpre, 2: _shift(jnp.where(has_down, dpre, 0.0), -GRID_W)}
    dxl = dxc = dxr = None
    for di, d in dz.items():
        for dj, side in enumerate((xl, xv, xr)):
            dw_ref[3 * di + dj:3 * di + dj + 1, :] = jnp.sum(d * side, axis=0, keepdims=True)
        tl, tc, tr = (wv[3 * di + dj:3 * di + dj + 1, :] * d for dj in range(3))
        dxl, dxc, dxr = (tl, tc, tr) if dxl is None else (dxl + tl, dxc + tc, dxr + tr)
    return dxc + jnp.where(no_left, 0.0, _shift(dxl, 1)) + jnp.where(no_right, 0.0, _shift(dxr, -1))


def _qk_post(pre, is_norm, scale):
    s = _silu(pre)
    nrm = lax.rsqrt(jnp.sum(s * s, axis=-1, keepdims=True) + NORM_EPS)
    return s * jnp.where(is_norm, nrm * scale, 1.0)


def _conv_kind():
    ct = pl.program_id(1)
    return ct < 2 * GDN_HEADS, jnp.where(ct < GDN_HEADS, GDN_HEAD ** -0.5, 1.0).astype(f32)


def conv_fwd(qkv, w16, *, is_ctx, name):
    B, L, C = qkv.shape
    spec = pl.BlockSpec((None, L, GDN_HEAD), lambda b, ct: (b, 0, ct))
    wspec = pl.BlockSpec((CONV_ROWS, GDN_HEAD), lambda b, ct: (0, ct))

    def body(x_ref, w_ref, o_ref, pre_ref):
        is_norm, scale = _conv_kind()
        pre = _conv_pre(x_ref[...], w_ref, _conv_masks(L, is_ctx), is_ctx)
        pre_ref[...] = pre
        o_ref[...] = _qk_post(pre, is_norm, scale)

    return pl.pallas_call(body, name=name, grid=(B, C // GDN_HEAD), in_specs=[spec, wspec], out_specs=[spec, spec],
                          out_shape=[SDS((B, L, C), f32)] * 2, compiler_params=_cparams(2))(qkv, w16)


def conv_bwd(qkv, pre, w16, da0, da1, *, is_ctx, name):
    B, L, C = qkv.shape
    spec = pl.BlockSpec((None, L, GDN_HEAD), lambda b, ct: (b, 0, ct))
    wspec = pl.BlockSpec((CONV_ROWS, GDN_HEAD), lambda b, ct: (0, ct))
    dwspec = pl.BlockSpec((None, CONV_ROWS, GDN_HEAD), lambda b, ct: (b, 0, ct))

    def body(x_ref, pre_ref, w_ref, d0_ref, d1_ref, dx_ref, dw_ref):
        is_norm, scale = _conv_kind()
        _, vjp = jax.vjp(lambda p: _qk_post(p, is_norm, scale), pre_ref[...])
        dpre = vjp(d0_ref[...] + d1_ref[...])[0]
        dw_ref[...] = jnp.zeros_like(dw_ref)
        dx_ref[...] = _conv_pre_bwd(x_ref[...], w_ref, dpre, _conv_masks(L, is_ctx), is_ctx, dw_ref)

    return pl.pallas_call(body, name=name, grid=(B, C // GDN_HEAD), in_specs=[spec, spec, wspec, spec, spec],
                          out_specs=[spec, dwspec], out_shape=[SDS((B, L, C), f32), SDS((B, CONV_ROWS, C), f32)],
                          compiler_params=_cparams(2))(qkv, pre, w16, da0, da1)


def _gates_fn(ba, alog, dtb):
    T = ba.shape[0]
    lane = lax.broadcasted_iota(jnp.int32, ba.shape, 1)
    ii = lax.broadcasted_iota(jnp.int32, (T, T), 0)
    jj = lax.broadcasted_iota(jnp.int32, (T, T), 1)
    same = jnp.right_shift(ii, 6) == jnp.right_shift(jj, 6)
    lmat = jnp.logical_and(same, ii >= jj).astype(f32)
    umat = jnp.logical_and(same, ii <= jj).astype(f32)
    g = jnp.where(lane >= 8, -jnp.exp(alog) * jax.nn.softplus(ba + dtb), 0.0)
    gc = jnp.where(lane >= 12, _dot_hi(umat, g), _dot_hi(lmat, g))
    return jnp.where(lane < 8, jax.nn.sigmoid(ba), gc)


def gates_fwd(ba, alog, dtb, *, name):
    B, L, _ = ba.shape
    T = min(TOK_TILE, L)
    t = _tok(T, N_GATE)

    def body(ba_ref, al_ref, dt_ref, o_ref):
        o_ref[...] = _gates_fn(ba_ref[...], al_ref[...], dt_ref[...])

    return pl.pallas_call(body, name=name, grid=(B, L // T),
                          in_specs=[t, _resident((1, N_GATE)), _resident((1, N_GATE))], out_specs=t,
                          out_shape=SDS((B, L, N_GATE), f32), compiler_params=_cparams(2))(ba, alog, dtb)


def gates_bwd(ba, alog, dtb, dbg, *, name):
    B, L, _ = ba.shape
    T = min(TOK_TILE, L)
    t = _tok(T, N_GATE)
    small = _resident((1, N_GATE))

    def body(ba_ref, al_ref, dt_ref, d_ref, dba_ref, dal_ref, ddt_ref):
        @pl.when(_first_step())
        def _():
            dal_ref[...] = jnp.zeros_like(dal_ref)
            ddt_ref[...] = jnp.zeros_like(ddt_ref)

        _, vjp = jax.vjp(_gates_fn, ba_ref[...], al_ref[...], dt_ref[...])
        dba, dal, ddt = vjp(d_ref[...])
        dba_ref[...] = dba
        dal_ref[...] += dal
        ddt_ref[...] += ddt

    return pl.pallas_call(body, name=name, grid=(B, L // T), in_specs=[t, small, small, t],
                          out_specs=[t, small, small],
                          out_shape=[SDS((B, L, N_GATE), f32), SDS((1, N_GATE), f32), SDS((1, N_GATE), f32)],
                          compiler_params=_cparams(2))(ba, alog, dtb, dbg)


@jax.custom_vjp
def _inv_unit_tri(mats):
    n = mats[0].shape[0]
    eye = (lax.broadcasted_iota(jnp.int32, (n, n), 0) == lax.broadcasted_iota(jnp.int32, (n, n), 1)).astype(f32)
    xs = [eye - a for a in mats]
    sq = [_dot(a, a) for a in mats]
    ps = sq
    k = 2
    while k < n:
        xs = [x + _dot(x, p) for x, p in zip(xs, ps)]
        k *= 2
        if k < n:
            ps = [_dot(p, p) for p in ps]
    return tuple(_dot(p, x) - a for p, x, a in zip(sq, xs, mats))


def _inv_unit_tri_fwd(mats):
    ns = _inv_unit_tri(mats)
    return ns, ns


def _inv_unit_tri_bwd(ns, dns):
    ys = [dn + _dot_tn(nn, dn) for nn, dn in zip(ns, dns)]
    return (tuple(-(y + _dot_nt(y, nn)) for y, nn in zip(ys, ns)),)


_inv_unit_tri.defvjp(_inv_unit_tri_fwd, _inv_unit_tri_bwd)


@jax.custom_vjp
def _inv_unit_tri_saved(mats, saved):
    return saved


_inv_unit_tri_saved.defvjp(lambda mats, saved: (saved, saved),
                           lambda ns, dns: _inv_unit_tri_bwd(ns, dns) + (tuple(jnp.zeros_like(n) for n in ns),))


def _gdn_chunk(heads, *, revs, saved=None, with_n=False):
    n = heads[0][0].shape[0]
    ii = lax.broadcasted_iota(jnp.int32, (n, n), 0)
    jj = lax.broadcasted_iota(jnp.int32, (n, n), 1)
    row = lax.broadcasted_iota(jnp.int32, (n, 1), 0)
    lower = {False: ii >= jj, True: ii <= jj}
    strict = {False: ii > jj, True: ii < jj}
    last = {False: n - 1, True: 0}
    H = range(len(heads))
    q, k, v, beta, gc, gr, s = (list(t) for t in zip(*heads))
    decay = [jnp.where(lower[revs[h]], jnp.exp(jnp.where(lower[revs[h]], gc[h] - gr[h], 0.0)), 0.0) for h in H]
    kk = [_mm_nt(k[h], k[h]) for h in H]
    qk = [_mm_nt(q[h], k[h]) * decay[h] for h in H]
    qs = [_mm(q[h], s[h]) for h in H]
    a_mat = tuple(jnp.where(strict[revs[h]], beta[h] * kk[h] * decay[h], 0.0) for h in H)
    gamma = [jnp.exp(gc[h]) for h in H]
    g_last = [jnp.sum(jnp.where(row == last[revs[h]], gc[h], 0.0), axis=0, keepdims=True) for h in H]
    nmat = _inv_unit_tri(a_mat) if saved is None else _inv_unit_tri_saved(a_mat, saved)
    bv = [beta[h] * v[h] for h in H]
    bk = [(beta[h] * gamma[h]) * k[h] for h in H]
    u0 = [bv[h] + _mm(nmat[h], bv[h]) for h in H]
    w = [bk[h] + _mm(nmat[h], bk[h]) for h in H]
    k_out = [k[h] * jnp.exp(g_last[h] - gc[h]) for h in H]
    u = [u0[h] - _mm(w[h], s[h]) for h in H]
    o = [gamma[h] * qs[h] + _mm(qk[h], u[h]) for h in H]
    s_new = [jnp.exp(g_last[h]) * s[h] + _mm_tn(k_out[h], u[h]) for h in H]
    outs = tuple((o[h], s_new[h]) for h in H)
    return (outs, nmat) if with_n else outs


def _gdn_specs(B, nc, rev):
    def cidx(n):
        return (nc - 1 - n) if rev else n
    tok = lambda width: pl.BlockSpec((B, CHUNK, width), lambda n: (0, cidx(n), 0))
    rowspec = pl.BlockSpec((B, None, N_GATE, CHUNK), lambda n: (0, cidx(n), 0, 0))
    st = pl.BlockSpec((B, GDN_HEADS, GDN_HEAD, GDN_HEAD), lambda n: (0, 0, 0, 0))
    ck = pl.BlockSpec((B, None, GDN_HEADS, GDN_HEAD, GDN_HEAD), lambda n: (0, cidx(n), 0, 0, 0))
    nsp = pl.BlockSpec((B, None, GDN_HEADS, CHUNK, CHUNK), lambda n: (0, cidx(n), 0, 0, 0))
    return tok, rowspec, st, ck, nsp


def _gdn_head_args(qkv_ref, bg_ref, bgr_ref, b, d, h):
    col = d * GDN_HEADS + h
    q = qkv_ref[b, :, h * GDN_HEAD:(h + 1) * GDN_HEAD]
    k = qkv_ref[b, :, D_GDN + h * GDN_HEAD:D_GDN + (h + 1) * GDN_HEAD]
    v = qkv_ref[b, :, 2 * D_GDN + h * GDN_HEAD:2 * D_GDN + (h + 1) * GDN_HEAD]
    bgv = bg_ref[b]
    return q, k, v, bgv[:, col:col + 1], bgv[:, 8 + col:9 + col], bgr_ref[b][8 + col:9 + col, :]


def _gdn_chains(B):
    return [(d, b, h) for d in range(N_DIR) for b in range(B) for h in range(GDN_HEADS)]


def gdn_fwd(qkv, bg, bgr, s0s, *, need_o, name):
    B, L, _ = qkv.shape
    nc = L // CHUNK
    specs = [_gdn_specs(B, nc, d == 1) for d in range(N_DIR)]
    chains = _gdn_chains(B)
    state_shape = (B, GDN_HEADS, GDN_HEAD, GDN_HEAD)

    def body(*refs):
        ins = [refs[3 * d:3 * d + 3] for d in range(N_DIR)]
        s0_refs = refs[6:8]
        k = 8
        o_refs = refs[k:k + 2] if need_o else None
        k += 2 if need_o else 0
        ck_refs, n_refs, sf_refs, s_scrs = refs[k:k + 2], refs[k + 2:k + 4], refs[k + 4:k + 6], refs[k + 6:k + 8]
        n = pl.program_id(0)

        @pl.when(n == 0)
        def _():
            for d in range(N_DIR):
                s_scrs[d][...] = s0_refs[d][...]

        for d in range(N_DIR):
            ck_refs[d][...] = s_scrs[d][...]
        heads = tuple(_gdn_head_args(*ins[d], b, d, h) + (s_scrs[d][b, h],) for d, b, h in chains)
        outs, nmat = _gdn_chunk(heads, revs=tuple(d == 1 for d, _, _ in chains), with_n=True)
        for (d, b, h), (o, s_new), nn in zip(chains, outs, nmat):
            if need_o:
                o_refs[d][b, :, h * GDN_HEAD:(h + 1) * GDN_HEAD] = o
            s_scrs[d][b, h] = s_new
            n_refs[d][b, h] = nn

        @pl.when(n == nc - 1)
        def _():
            for d in range(N_DIR):
                sf_refs[d][...] = s_scrs[d][...]

    in_specs, out_o, out_ck, out_n, out_sf = [], [], [], [], []
    for tok, rowspec, st, ck, nsp in specs:
        in_specs += [tok(3 * D_GDN), tok(N_GATE), rowspec]
        out_o.append(tok(D_GDN))
        out_ck.append(ck)
        out_n.append(nsp)
        out_sf.append(st)
    in_specs += [specs[0][2]] * 2
    out_specs = (out_o if need_o else []) + out_ck + out_n + out_sf
    out_shape = (([SDS((B, L, D_GDN), f32)] * 2 if need_o else []) + [SDS((B, nc) + state_shape[1:], f32)] * 2
                 + [SDS((B, nc, GDN_HEADS, CHUNK, CHUNK), f32)] * 2 + [SDS(state_shape, f32)] * 2)
    res = pl.pallas_call(
        body, name=name, grid=(nc,), in_specs=in_specs, out_specs=out_specs, out_shape=out_shape,
        scratch_shapes=[pltpu.VMEM(state_shape, f32)] * 2, compiler_params=_cparams(1),
    )(qkv, bg, bgr, qkv, bg, bgr, *s0s)
    if need_o:
        return res[0:2], res[2:4], res[4:6], res[6:8]
    return res[0:2], res[2:4], res[4:6]


def gdn_bwd(qkv, bg, bgr, cks, ns, do, dsfs, *, name):
    B, L, _ = qkv.shape
    nc = L // CHUNK
    has_do = do is not None
    specs = [_gdn_specs(B, nc, d != 1) for d in range(N_DIR)]
    chains = _gdn_chains(B)
    state_shape = (B, GDN_HEADS, GDN_HEAD, GDN_HEAD)
    per_dir = 6 if has_do else 5

    def body(*refs):
        ins = [refs[per_dir * d:per_dir * d + per_dir] for d in range(N_DIR)]
        k = per_dir * N_DIR
        dsf_refs = refs[k:k + 2]
        outs = [refs[k + 2 + 3 * d:k + 5 + 3 * d] for d in range(N_DIR)]
        ds0_refs, ds_scrs = refs[k + 8:k + 10], refs[k + 10:k + 12]
        n = pl.program_id(0)

        @pl.when(n == 0)
        def _():
            for d in range(N_DIR):
                ds_scrs[d][...] = dsf_refs[d][...]

        lane = lax.broadcasted_iota(jnp.int32, (CHUNK, N_GATE), 1)
        sub = lax.broadcasted_iota(jnp.int32, (N_GATE, CHUNK), 0)
        heads = tuple(_gdn_head_args(*ins[d][:3], b, d, h) + (ins[d][3][b, h],) for d, b, h in chains)
        saved = tuple(ins[d][4][b, h] for d, b, h in chains)
        _, vjp = jax.vjp(functools.partial(_gdn_chunk, revs=tuple(d == 1 for d, _, _ in chains), saved=saved), heads)
        zero = jnp.zeros((CHUNK, GDN_HEAD), f32)
        cts = tuple(((ins[d][5][b, :, h * GDN_HEAD:(h + 1) * GDN_HEAD] if has_do else zero), ds_scrs[d][b, h])
                    for d, b, h in chains)
        (dheads,) = vjp(cts)
        dbg_acc = [[jnp.zeros((CHUNK, N_GATE), f32) for _ in range(B)] for _ in range(N_DIR)]
        dbgr_acc = [[jnp.zeros((N_GATE, CHUNK), f32) for _ in range(B)] for _ in range(N_DIR)]
        for (d, b, h), (dq, dk, dv, db, dgc, dgr, ds) in zip(chains, dheads):
            col = d * GDN_HEADS + h
            dqkv_ref = outs[d][0]
            dqkv_ref[b, :, h * GDN_HEAD:(h + 1) * GDN_HEAD] = dq
            dqkv_ref[b, :, D_GDN + h * GDN_HEAD:D_GDN + (h + 1) * GDN_HEAD] = dk
            dqkv_ref[b, :, 2 * D_GDN + h * GDN_HEAD:2 * D_GDN + (h + 1) * GDN_HEAD] = dv
            dbg_acc[d][b] = dbg_acc[d][b] + jnp.where(lane == col, db, 0.0) + jnp.where(lane == 8 + col, dgc, 0.0)
            dbgr_acc[d][b] = dbgr_acc[d][b] + jnp.where(sub == 8 + col, dgr, 0.0)
            ds_scrs[d][b, h] = ds
        for d in range(N_DIR):
            for b in range(B):
                outs[d][1][b] = dbg_acc[d][b]
                outs[d][2][b] = dbgr_acc[d][b]

        @pl.when(n == nc - 1)
        def _():
            for d in range(N_DIR):
                ds0_refs[d][...] = ds_scrs[d][...]

    in_specs, args, out_specs, out_shape = [], [], [], []
    for d, (tok, rowspec, st, ck, nsp) in enumerate(specs):
        in_specs += [tok(3 * D_GDN), tok(N_GATE), rowspec, ck, nsp] + ([tok(D_GDN)] if has_do else [])
        args += [qkv, bg, bgr, cks[d], ns[d]] + ([do] if has_do else [])
        out_specs += [tok(3 * D_GDN), tok(N_GATE), rowspec]
        out_shape += [SDS((B, L, 3 * D_GDN), f32), SDS((B, L, N_GATE), f32), SDS((B, nc, N_GATE, CHUNK), f32)]
    st = specs[0][2]
    in_specs += [st, st]
    args += list(dsfs)
    out_specs += [st, st]
    out_shape += [SDS(state_shape, f32)] * 2
    res = pl.pallas_call(
        body, name=name, grid=(nc,), in_specs=in_specs, out_specs=out_specs, out_shape=out_shape,
        scratch_shapes=[pltpu.VMEM(state_shape, f32)] * 2, compiler_params=_cparams(1),
    )(*args)
    return (res[0], res[3]), (res[1], res[4]), (res[2], res[5]), (res[6], res[7])


def _gnorm_fn(o0, o1, z, w):
    o = o0 + o1
    return o * lax.rsqrt(jnp.mean(o * o, axis=-1, keepdims=True) + NORM_EPS) * w * _silu(z)


def _head_loss(y, x, gate, lng, lnb, tgt):
    r = DEEPNORM_ALPHA * x + gate * y
    mu = jnp.mean(r, axis=-1, keepdims=True)
    rc = r - mu
    var = jnp.mean(rc * rc, axis=-1, keepdims=True)
    err = rc * lax.rsqrt(var + LN_EPS) * lng + lnb - tgt
    return (0.5 / D_MODEL) * jnp.sum(jnp.sum(err * err, axis=-1, keepdims=True), axis=0, keepdims=True)


def tail_fwd_bwd(u, y0, y1, z_s5, o0, o1, z_gdn, x, tgt, gate, lng, lnb, ws, wg, dsk, wglu, bglu, nw):
    B, L, _ = x.shape
    T = min(TOK_TILE, L)

    def body(u_ref, y0_ref, y1_ref, z_ref, o0_ref, o1_ref, zg_ref, x_ref, t_ref, gate_ref, lng_ref, lnb_ref, ws_ref,
             wg_ref, dsk_ref, wglu_ref, bglu_ref, nw_ref,
             loss_ref, du_ref, dys_ref, dz_ref, do_ref, dzg_ref, gx_ref, dws_ref, dwg_ref, dgate_ref, dlng_ref, dlnb_ref,
             ddsk_ref, dwglu_ref, dbglu_ref, dnw_ref):
        n = pl.program_id(1)

        @pl.when(_first_step())
        def _():
            for r in (dws_ref, dwg_ref, dlng_ref, dlnb_ref, ddsk_ref, dwglu_ref, dbglu_ref, dnw_ref):
                r[...] = jnp.zeros_like(r)

        @pl.when(n == 0)
        def _():
            loss_ref[...] = jnp.zeros_like(loss_ref)
            dgate_ref[...] = jnp.zeros_like(dgate_ref)

        s5o, glu_vjp = jax.vjp(_glu_fn, u_ref[...], y0_ref[...], y1_ref[...], z_ref[...], dsk_ref[...],
                               wglu_ref[...].astype(f32), bglu_ref[...])
        heads = []
        for h in range(GDN_HEADS):
            sl = slice(h * GDN_HEAD, (h + 1) * GDN_HEAD)
            heads.append(jax.vjp(_gnorm_fn, o0_ref[:, sl], o1_ref[:, sl], zg_ref[:, sl], nw_ref[...]))
        sv = s5o.astype(bf16)
        gv = jnp.concatenate([out for out, _ in heads], axis=1).astype(bf16)
        y = _dot(sv, ws_ref[...]) + _dot(gv, wg_ref[...])
        loss, vjp = jax.vjp(lambda *a: _head_loss(*a, t_ref[...]), y, x_ref[...], gate_ref[...], lng_ref[...],
                            lnb_ref[...])
        dy, dx, dgate, dlng, dlnb = vjp(jnp.ones((1, 1), f32))
        loss_ref[...] += jnp.broadcast_to(loss, loss_ref.shape)
        dyb = dy.astype(bf16)
        gx_ref[...] = dx
        dws_ref[...] += _dot_tn(sv, dyb)
        dwg_ref[...] += _dot_tn(gv, dyb)
        dgate_ref[...] += dgate
        dlng_ref[...] += dlng
        dlnb_ref[...] += dlnb
        du, dys, _, dz, ddsk, dwglu, dbglu = glu_vjp(_dot_nt(dyb, ws_ref[...]))
        du_ref[...], dys_ref[...], dz_ref[...] = du, dys, dz
        ddsk_ref[...] += ddsk
        dwglu_ref[...] += dwglu
        dbglu_ref[...] += dbglu
        dgdo = _dot_nt(dyb, wg_ref[...])
        for h, (_, hvjp) in enumerate(heads):
            sl = slice(h * GDN_HEAD, (h + 1) * GDN_HEAD)
            do, _, dzg, dnw = hvjp(dgdo[:, sl])
            do_ref[:, sl] = do
            dzg_ref[:, sl] = dzg
            dnw_ref[...] += dnw

    half, full = _tok(T, D_S5), _tok(T, D_MODEL)
    row = _resident((1, D_MODEL))
    wsp = _resident((D_S5, D_MODEL))
    r512, rglu, r128 = _resident((1, D_S5)), _resident((D_S5, D_S5)), _resident((1, GDN_HEAD))
    return pl.pallas_call(
        body, name="tail_fwd_bwd", grid=(B, L // T),
        in_specs=[half] * 7 + [full, full, _per_batch(1, D_MODEL), row, row, wsp, wsp, r512, rglu, r512, r128],
        out_specs=[_per_batch(8, LANES)] + [half] * 5 + [full, wsp, wsp, _per_batch(1, D_MODEL), row, row, r512, rglu, r512,
                                                           r128],
        out_shape=[SDS((B, 8, LANES), f32)] + [SDS((B, L, D_S5), f32)] * 5 + [
            SDS((B, L, D_MODEL), f32), SDS((D_S5, D_MODEL), f32), SDS((D_GDN, D_MODEL), f32), SDS((B, 1, D_MODEL), f32),
            SDS((1, D_MODEL), f32), SDS((1, D_MODEL), f32), SDS((1, D_S5), f32), SDS((D_S5, D_S5), f32), SDS((1, D_S5), f32),
            SDS((1, GDN_HEAD), f32)],
        compiler_params=_cparams(2),
    )(u, y0, y1, z_s5, o0, o1, z_gdn, x, tgt, gate, lng, lnb, ws, wg, dsk, wglu, bglu, nw)


def _adamw_math(w, g, m, v):
    nm = ADAM_B1 * m + (1.0 - ADAM_B1) * g
    nv = ADAM_B2 * v + (1.0 - ADAM_B2) * jnp.square(g)
    m_hat = nm / (1.0 - ADAM_B1 ** ADAM_STEP)
    v_hat = nv / (1.0 - ADAM_B2 ** ADAM_STEP)
    return -ADAM_LR * (m_hat / (jnp.sqrt(v_hat) + ADAM_EPS) + ADAM_WD * w), nm, nv


def _row_tile(rows, cap=512):
    for t in range(min(cap, rows), 15, -1):
        if rows % t == 0 and t % 16 == 0:
            return t
    return rows


def adamw_3d(w, g, m, v, *, lead=False, name):
    R, C = (w.shape[0], w.shape[2]) if lead else w.shape[1:]
    if lead:
        T = next(t for t in range(min(256, R), 0, -1) if R % t == 0)
        spec = pl.BlockSpec((T, 1, C), lambda i: (i, 0, 0))
    else:
        T = _row_tile(R)
        spec = pl.BlockSpec((None, T, C), lambda i: (0, i, 0))

    def body(w_ref, g_ref, m_ref, v_ref, d_ref, nm_ref, nv_ref):
        d_ref[...], nm_ref[...], nv_ref[...] = _adamw_math(w_ref[...], g_ref[...], m_ref[...], v_ref[...])

    return pl.pallas_call(body, name=name, grid=(R // T,), in_specs=[spec] * 4, out_specs=[spec] * 3,
                          out_shape=[SDS(w.shape, f32)] * 3, compiler_params=_cparams(1))(w, g, m, v)


def adamw_small(ws, gs, ms, vs):
    n = len(ws)

    def body(*refs):
        outs = refs[4 * n:]
        for i in range(n):
            d, nm, nv = _adamw_math(refs[i][...], refs[n + i][...], refs[2 * n + i][...], refs[3 * n + i][...])
            outs[i][...], outs[n + i][...], outs[2 * n + i][...] = d, nm, nv

    res = pl.pallas_call(body, name="adamw_small", out_shape=[SDS(w.shape, f32) for w in ws] * 3,
                         compiler_params=pltpu.CompilerParams(vmem_limit_bytes=VMEM_LIMIT))(*ws, *gs, *ms, *vs)
    return res[:n], res[n:2 * n], res[2 * n:]


def sum_cores(own, got, *, name):
    A, H, C = own.shape
    T = _row_tile(H)
    spec = pl.BlockSpec((None, T, C), lambda a, i: (a, i, 0))

    def body(a_ref, b_ref, q32_ref, q16_ref):
        q = a_ref[...] + b_ref[...]
        q32_ref[...] = q
        q16_ref[...] = q.astype(bf16)

    return pl.pallas_call(body, name=name, grid=(A, H // T), in_specs=[spec, spec], out_specs=[spec, spec],
                          out_shape=[SDS((A, H, C), f32), SDS((A, H, C), bf16)], compiler_params=_cparams(2))(own, got)


def sum_chips(mine, rec, cpos, *, name):
    H, C = mine.shape
    T = _row_tile(H)
    nt = H // T

    def body(c_ref, m_ref, r_ref, f_ref):
        f_ref[...] = ((m_ref[...] + r_ref[0].astype(f32)) + r_ref[1].astype(f32)) + r_ref[2].astype(f32)

    grid_spec = pltpu.PrefetchScalarGridSpec(
        num_scalar_prefetch=1, grid=(nt,),
        in_specs=[pl.BlockSpec((T, C), lambda i, c_ref: (i, 0)), pl.BlockSpec((3, T, C), lambda i, c_ref: (0, i, 0))],
        out_specs=pl.BlockSpec((None, T, C), lambda i, c_ref: (0, c_ref[0] * nt + i, 0)))
    return pl.pallas_call(body, name=name, grid_spec=grid_spec, out_shape=SDS((1, 2 * H, C), f32),
                          compiler_params=_cparams(1))(cpos.reshape(1).astype(jnp.int32), mine, rec)


CHIP_FLIPS = ((1, 0), (0, 1), (1, 1))


def _pos():
    return lax.axis_index("x"), lax.axis_index("y"), lax.axis_index("c")


def _comm_call(body, srcs, out_sds, n_remote, n_local, name):
    any_spec = pl.BlockSpec(memory_space=pl.ANY)
    return pl.pallas_call(
        body, name=name, in_specs=[any_spec] * len(srcs), out_specs=[any_spec] * len(out_sds), out_shape=out_sds,
        scratch_shapes=[pltpu.SemaphoreType.DMA((n_remote,)), pltpu.SemaphoreType.DMA((n_remote,)),
                        pltpu.SemaphoreType.DMA((max(n_local, 1),))],
        compiler_params=pltpu.CompilerParams(has_side_effects=True),
    )(*srcs)


def _remote(src, dst, send_sems, recv_sems, k, target):
    return pltpu.make_async_remote_copy(src, dst, send_sems.at[k], recv_sems.at[k], device_id=target,
                                        device_id_type=MESH)


def _half_rows(c, rows):
    half = rows // 2
    return pl.ds(pl.multiple_of(c * half, 8), half)


def gather_shards(shards):
    nt = len(shards)

    def body(*refs):
        srcs, outs = refs[:nt], refs[nt:2 * nt]
        send_sems, recv_sems, _ = refs[2 * nt:]
        x, y, c = _pos()
        j = 2 * x + y
        sib = (x, y, 1 - c)
        own = [_remote(srcs[t], outs[t].at[j], send_sems, recv_sems, 7 * t + 6, sib) for t in range(nt)]
        first, passed = [], []
        for k, (fx, fy) in enumerate(CHIP_FLIPS):
            tx, ty = x ^ fx, y ^ fy
            jk = 2 * tx + ty
            for t in range(nt):
                rows = _half_rows(c, srcs[t].shape[0])
                first.append(_remote(srcs[t].at[rows], outs[t].at[j, rows], send_sems, recv_sems, 7 * t + k, (tx, ty, c)))
                passed.append(_remote(outs[t].at[jk, rows], outs[t].at[jk, rows], send_sems, recv_sems, 7 * t + 3 + k, sib))
        for cp in first + own:
            cp.start()
        for a, b in zip(first, passed):
            a.wait_recv()
            b.start()
        for cp in passed + own:
            cp.wait_recv()
        for cp in first + passed + own:
            cp.wait_send()

    return _comm_call(body, shards, [SDS((4,) + s.shape, s.dtype) for s in shards], 7 * nt, 0, "gather_shards")


def swap_halves(ps):
    nt = len(ps)

    def body(*refs):
        srcs, outs = refs[:nt], refs[nt:2 * nt]
        send_sems, recv_sems, _ = refs[2 * nt:]
        x, y, c = _pos()
        cps = [_remote(srcs[t].at[a, _half_rows(1 - c, srcs[t].shape[1])], outs[t].at[a], send_sems, recv_sems, 4 * t + a,
                       (x, y, 1 - c)) for t in range(nt) for a in range(4)]
        for cp in cps:
            cp.start()
        for cp in cps:
            cp.wait()

    return _comm_call(body, ps, [SDS((4, p.shape[1] // 2, p.shape[2]), p.dtype) for p in ps], 4 * nt, 0, "swap_halves")


def scatter_to_chips(qs):
    nt = len(qs)

    def body(*refs):
        srcs, outs = refs[:nt], refs[nt:2 * nt]
        send_sems, recv_sems, _ = refs[2 * nt:]
        x, y, c = _pos()
        cps = []
        for k, (fx, fy) in enumerate(CHIP_FLIPS):
            tx, ty = x ^ fx, y ^ fy
            for t in range(nt):
                cps.append(_remote(srcs[t].at[2 * tx + ty], outs[t].at[k], send_sems, recv_sems, 3 * t + k, (tx, ty, c)))
        for cp in cps:
            cp.start()
        for cp in cps:
            cp.wait()

    return _comm_call(body, qs, [SDS((3,) + q.shape[1:], q.dtype) for q in qs], 3 * nt, 0, "scatter_to_chips")


def join_halves(fs):
    nt = len(fs)

    def body(*refs):
        outs = refs[nt:2 * nt]
        send_sems, recv_sems, _ = refs[2 * nt:]
        x, y, c = _pos()
        cps = []
        for t in range(nt):
            mine = outs[t].at[0, _half_rows(c, outs[t].shape[1])]
            cps.append(_remote(mine, mine, send_sems, recv_sems, t, (x, y, 1 - c)))
        for cp in cps:
            cp.start()
        for cp in cps:
            cp.wait()

    any_spec = pl.BlockSpec(memory_space=pl.ANY)
    return pl.pallas_call(
        body, name="join_halves", in_specs=[any_spec] * nt, out_specs=[any_spec] * nt,
        out_shape=[SDS(f.shape, f.dtype) for f in fs], input_output_aliases={t: t for t in range(nt)},
        scratch_shapes=[pltpu.SemaphoreType.DMA((nt,)), pltpu.SemaphoreType.DMA((nt,)), pltpu.SemaphoreType.DMA((1,))],
        compiler_params=pltpu.CompilerParams(has_side_effects=True),
    )(*fs)


DEV_FLIPS = tuple((fx, fy, fc) for fx in (0, 1) for fy in (0, 1) for fc in (0, 1))[1:]


def gather_devices(block, *, name):
    def body(src, out, send_sems, recv_sems, loc_sems):
        x, y, c = _pos()
        me = 4 * x + 2 * y + c
        mine = pltpu.make_async_copy(src, out.at[me], loc_sems.at[0])
        mine.start()
        cps = [_remote(src, out.at[me], send_sems, recv_sems, k, (x ^ fx, y ^ fy, c ^ fc))
               for k, (fx, fy, fc) in enumerate(DEV_FLIPS)]
        for cp in cps:
            cp.start()
        for cp in cps:
            cp.wait()
        mine.wait()

    return _comm_call(body, [block], [SDS((N_DEV,) + block.shape, block.dtype)], 7, 1, name)[0]


def exchange_devices(blocks, *, name):
    def body(src, out, send_sems, recv_sems, loc_sems):
        x, y, c = _pos()
        me = 4 * x + 2 * y + c
        mine = pltpu.make_async_copy(src.at[me], out.at[me], loc_sems.at[0])
        mine.start()
        cps = []
        for k, (fx, fy, fc) in enumerate(DEV_FLIPS):
            tx, ty, tc = x ^ fx, y ^ fy, c ^ fc
            cps.append(_remote(src.at[4 * tx + 2 * ty + tc], out.at[me], send_sems, recv_sems, k, (tx, ty, tc)))
        for cp in cps:
            cp.start()
        for cp in cps:
            cp.wait()
        mine.wait()

    return _comm_call(body, [blocks], [SDS(blocks.shape, blocks.dtype)], 7, 1, name)[0]


def gather_small(s):
    def body(src, out, send_sems, recv_sems, _):
        x, y, c = _pos()
        j = 2 * x + y
        cps = [_remote(src, out.at[j], send_sems, recv_sems, k, (x ^ fx, y ^ fy, c)) for k, (fx, fy) in enumerate(CHIP_FLIPS)]
        cps.append(_remote(src, out.at[j], send_sems, recv_sems, 3, (x, y, 1 - c)))
        for cp in cps:
            cp.start()
        for cp in cps:
            cp.wait()

    return _comm_call(body, [s], [SDS((4,) + s.shape, s.dtype)], 4, 0, "gather_small")[0]


SMALL_SHAPES = ((1, 2, 32, 64), (1, 2, 32, 64), (1, 2, 32), (1, 2, 32, 16, 64),
                (1, 2, 32, 16, 64), (1, 2, 32, 16, 64), (1, 2, 32, 16, 64), (1, D_S5), (1, D_S5), (1, 2, 4), (1, 2, 4),
                (1, GDN_HEAD), (1, D_MODEL), (1, D_MODEL), (LANES,))
SMALL_SWAPPED = (3, 4)


def _size(shape):
    return functools.reduce(lambda p, q: p * q, shape)


SMALL_ROWS = tuple(-(-_size(s) // LANES) for s in SMALL_SHAPES)
SMALL_TOTAL = 2176
SMALL_QUARTER = SMALL_TOTAL // 4


def _rows(a):
    flat = a.reshape(-1)
    pad = (-flat.shape[0]) % LANES
    if pad:
        flat = jnp.concatenate([flat, jnp.zeros((pad,), flat.dtype)])
    return flat.reshape(-1, LANES)


def _pack_small(parts):
    rows = [_rows(p) for p in parts]
    rows.append(jnp.zeros((SMALL_TOTAL - sum(SMALL_ROWS), LANES), f32))
    return jnp.concatenate(rows, axis=0)


def _unpack_small(buf):
    out, r = [], 0
    for s, n in zip(SMALL_SHAPES, SMALL_ROWS):
        out.append(buf[r:r + n].reshape(-1)[:_size(s)].reshape(s))
        r += n
    return out


def _as_2d(a):
    return a.reshape(1, -1) if a.ndim == 1 else a.reshape(-1, a.shape[-1])


S5_BG = S5_GROUPS // S5_BLOCKS


def _block_diag_in(bb):
    lead = bb.shape[:-2]
    eye = jnp.eye(S5_BG, dtype=bb.dtype)
    b4 = bb.reshape(lead + (S5_BLOCKS, S5_BG, S5_GROUP, S5_STATE))
    return jnp.einsum('...jgcp,gh->...jgchp', b4, eye).reshape(lead + (S5_BLOCKS, S5_BC, S5_BS))


def _block_diag_in_t(d):
    lead = d.shape[:-3]
    d6 = d.reshape(lead + (S5_BLOCKS, S5_BG, S5_GROUP, S5_BG, S5_STATE))
    return jnp.einsum('...jgcgp->...jgcp', d6).reshape(lead + (S5_GROUPS, S5_GROUP * S5_STATE))


def _block_diag_out(cm):
    lead = cm.shape[:-3]
    eye = jnp.eye(S5_BG, dtype=cm.dtype)
    c4 = cm.reshape(lead + (S5_BLOCKS, S5_BG, S5_GROUP, S5_STATE))
    return jnp.einsum('...jgcp,gh->...jhpgc', c4, eye).reshape(lead + (S5_BLOCKS, S5_BS, S5_BC))


def _block_diag_out_t(d):
    lead = d.shape[:-3]
    d6 = d.reshape(lead + (S5_BLOCKS, S5_BG, S5_STATE, S5_BG, S5_GROUP))
    return jnp.einsum('...jgpgc->...jgcp', d6).reshape(lead + (S5_GROUPS, S5_GROUP, S5_STATE))


def _to_chunk_rows(a):
    B, L, W = a.shape
    return a.reshape(B, L // CHUNK, CHUNK, W).transpose(0, 1, 3, 2)


def _from_chunk_rows(a):
    B, nc, W, _ = a.shape
    return a.transpose(0, 1, 3, 2).reshape(B, nc * CHUNK, W)


def local_step(x, ctx, tgt, m, w_in, lam_re, lam_im, log_dt, b_re, b_im, c_re, c_im, s5_d,
               w_glu, b_glu, conv16, a_log, dt_bias, norm_w, w_out, ln_g, ln_b):
    B, L, _ = x.shape
    zeros_state = jnp.zeros((B, GDN_HEADS, GDN_HEAD, GDN_HEAD), f32)

    shift, scale, gate = m[:B, :D_MODEL], m[:B, D_MODEL:2 * D_MODEL], m[:B, 2 * D_MODEL:]
    mod = jnp.stack([scale, shift], axis=1)
    mod_c = jnp.broadcast_to(jnp.stack([m[B, D_MODEL:2 * D_MODEL], m[B, :D_MODEL]], axis=0)[None], (B, 2, D_MODEL))

    u, z_s5, qkv, z_gdn, ba = in_proj_fwd(x, mod, w_in, name="in_proj_fwd")
    uc, _, qkvc, _, bac = in_proj_fwd(ctx, mod_c, w_in, name="in_proj_fwd_ctx")

    ng = N_DIR * S5_GROUPS
    zoh_in = (lam_re.reshape(ng, S5_STATE), lam_im.reshape(ng, S5_STATE), log_dt.reshape(ng, 1),
              b_re.reshape(ng, S5_GROUP * S5_STATE), b_im.reshape(ng, S5_GROUP * S5_STATE))
    expand = (jnp.arange(S5_GROUP * S5_STATE)[None, :] % S5_STATE == jnp.arange(S5_STATE)[:, None]).astype(f32)
    ar, ai, bbr, bbi = s5_zoh_fwd(*zoh_in, expand)
    b_blocks = _block_diag_in(jnp.stack([bbr, bbi]).astype(bf16).reshape(2, N_DIR, S5_GROUPS, S5_GROUP * S5_STATE))
    c_blocks = _block_diag_out(jnp.stack([c_re, -c_im]).astype(bf16).reshape(2, N_DIR, S5_GROUPS, S5_GROUP, S5_STATE))
    a_rows = jnp.stack([ar, ai]).reshape(2, N_DIR, S5_HALF)
    s5w, ys, hins, hins_c, hss, hss_c = [], [], [], [], [], []
    for d in range(N_DIR):
        wd = (b_blocks[0, d], b_blocks[1, d], c_blocks[0, d], c_blocks[1, d], a_rows[:, d])
        s5w.append(wd)
        hs_c, hin_c, hend_c = s5_scan_fwd(uc, *wd, jnp.zeros((B, 2, S5_HALF), f32), d=d, need_y=False,
                                          name=f"s5_fwd_ctx{d}")
        y_d, hs_d, hin, _ = s5_scan_fwd(u, *wd, hend_c, d=d, need_y=True, name=f"s5_fwd{d}")
        hss.append(hs_d)
        hss_c.append(hs_c)
        ys.append(y_d)
        hins.append(hin)
        hins_c.append(hin_c)
    glu_w = (s5_d.reshape(1, D_S5), w_glu, b_glu.reshape(1, D_S5))

    act, pre = conv_fwd(qkv, conv16, is_ctx=False, name="conv_fwd")
    act_c, pre_c = conv_fwd(qkvc, conv16, is_ctx=True, name="conv_fwd_ctx")
    pad8 = jnp.zeros((1, 8), f32)
    alog16 = jnp.concatenate([pad8, a_log.reshape(1, 8)], axis=1)
    dtb16 = jnp.concatenate([pad8, dt_bias.reshape(1, 8)], axis=1)
    bg = gates_fwd(ba, alog16, dtb16, name="gates_fwd")
    bg_c = gates_fwd(bac, alog16, dtb16, name="gates_fwd_ctx")
    bgr, bgr_c = _to_chunk_rows(bg), _to_chunk_rows(bg_c)
    cks_c, ns_c, s_c = gdn_fwd(act_c, bg_c, bgr_c, (zeros_state, zeros_state), need_o=False, name="gdn_fwd_ctx")
    os_, cks, ns, _ = gdn_fwd(act, bg, bgr, s_c, need_o=True, name="gdn_fwd")
    nw = norm_w.reshape(1, GDN_HEAD)

    (loss8, du_skip, dy, dz_s5, do, dz_gdn, gx_res, dws, dwg, dgate, dlng, dlnb, d_s5_d, d_w_glu, d_b_glu,
     d_norm_w) = tail_fwd_bwd(u, ys[0], ys[1], z_s5, os_[0], os_[1], z_gdn, x, tgt, gate[:, None, :],
                              ln_g.reshape(1, D_MODEL), ln_b.reshape(1, D_MODEL), w_out[:D_S5], w_out[D_S5:], *glu_w, nw)
    loss = jnp.sum(loss8[:, 0, 0])
    d_w_out = jnp.concatenate([dws, dwg], axis=0)

    dacts, dbgs, dbgrs, ds0s = gdn_bwd(act, bg, bgr, cks, ns, do, (zeros_state, zeros_state), name="gdn_bwd")
    dacts_c, dbgs_c, dbgrs_c, _ = gdn_bwd(act_c, bg_c, bgr_c, cks_c, ns_c, None, ds0s, name="gdn_bwd_ctx")
    dbg = dbgs[0] + dbgs[1] + _from_chunk_rows(dbgrs[0] + dbgrs[1])
    dbg_c = dbgs_c[0] + dbgs_c[1] + _from_chunk_rows(dbgrs_c[0] + dbgrs_c[1])
    dba, dal, ddt = gates_bwd(ba, alog16, dtb16, dbg, name="gates_bwd")
    dbac, dal_c, ddt_c = gates_bwd(bac, alog16, dtb16, dbg_c, name="gates_bwd_ctx")
    d_a_log = (dal + dal_c)[:, 8:].reshape(1, N_DIR, GDN_HEADS)
    d_dt_bias = (ddt + ddt_c)[:, 8:].reshape(1, N_DIR, GDN_HEADS)
    dqkv, dcw = conv_bwd(qkv, pre, conv16, dacts[0], dacts[1], is_ctx=False, name="conv_bwd")
    dqkvc, dcw_c = conv_bwd(qkvc, pre_c, conv16, dacts_c[0], dacts_c[1], is_ctx=True, name="conv_bwd_ctx")
    d_conv16 = jnp.sum(dcw, axis=0) + jnp.sum(dcw_c, axis=0)

    dus, ducs = [du_skip], []
    das, dbs, dcs = [], [], []
    for d in range(N_DIR):
        du_d, dbre1, dbim1, dct1, dcb1, da1, dh0 = s5_scan_bwd(u, dy, hss[d], *s5w[d], hins[d],
                                                                jnp.zeros((B, 2, S5_HALF), f32), d=d, name=f"s5_bwd{d}")
        duc_d, dbre2, dbim2, _, _, da2, _ = s5_scan_bwd(uc, None, hss_c[d], *s5w[d], hins_c[d], dh0, d=d,
                                                        name=f"s5_bwd_ctx{d}")
        dus.append(du_d)
        ducs.append(duc_d)
        das.append(da1 + da2)
        dbs.append(jnp.stack([dbre1 + dbre2, dbim1 + dbim2]))
        dcs.append(jnp.stack([dct1, dcb1]))
    ng_shape = (N_DIR * S5_GROUPS, -1)
    da = jnp.stack(das, axis=1)
    db = _block_diag_in_t(jnp.stack(dbs, axis=1))
    dc = _block_diag_out_t(jnp.stack(dcs, axis=1))
    dlr, dli, dldt, dbre, dbim = s5_zoh_bwd(*zoh_in, expand, da[0].reshape(ng_shape), da[1].reshape(ng_shape),
                                            db[0].reshape(ng_shape), db[1].reshape(ng_shape))
    d_s5 = (dlr, dli, dldt, dbre, dbim, dc[0], -dc[1])

    padg = lambda a: jnp.concatenate([a, jnp.zeros(a.shape[:2] + (LANES - N_GATE,), f32)], axis=2)
    zc = jnp.zeros_like(uc)
    dw_c, dmod_c = in_proj_bwd(ctx, mod_c, (tuple(ducs), zc, dqkvc, zc, padg(dbac)), w_in, None, None,
                               name="in_proj_bwd_ctx")
    d_w_in, dmod, grad_x = in_proj_bwd(x, mod, (tuple(dus), dz_s5, dqkv, dz_gdn, padg(dba)), w_in, gx_res, dw_c,
                                       name="in_proj_bwd")
    dmod_c = jnp.sum(dmod_c, axis=0)

    dm_rows = jnp.concatenate([dmod[:, 1], dmod[:, 0], dgate[:, 0]], axis=1)
    dm_ctx = jnp.concatenate([dmod_c[1], dmod_c[0], jnp.zeros((D_MODEL,), f32)])[None]
    dm = jnp.concatenate([dm_rows, dm_ctx], axis=0)
    small = (*d_s5, d_s5_d, d_b_glu, d_a_log, d_dt_bias, d_norm_w, dlng, dlnb)
    small = tuple(g.reshape(s) for g, s in zip(small, SMALL_SHAPES))
    return loss, grad_x, (d_w_in, d_w_out, d_w_glu, d_conv16), small, dm


SHARDED = (1, 3, 18, 12, 14)
REDUCED = (3, 18, 12, 14)
UNSHARDED = tuple(i for i in range(21) if i not in SHARDED)
SMALL = tuple(i for i in UNSHARDED if i not in (0, 2))
W_IN_SHARD = 772


def _conv_rows(w):
    return jnp.concatenate([w.reshape(9, w.shape[-1]), jnp.zeros((CONV_ROWS - 9, w.shape[-1]), f32)], axis=0)


def kernel(x, c, ctx, c_ctx, w_ada, b_ada, w_in, s5_lambda_re, s5_lambda_im, s5_log_dt, s5_b_re, s5_b_im, s5_c_re, s5_c_im, s5_d, w_glu, b_glu, conv_w, gdn_a_log, gdn_dt_bias, gdn_norm_w, w_out, ln_g, ln_b, loss_target, m_c_ctx, m_w_ada, m_b_ada, m_w_in, m_s5_lambda_re, m_s5_lambda_im, m_s5_log_dt, m_s5_b_re, m_s5_b_im, m_s5_c_re, m_s5_c_im, m_s5_d, m_w_glu, m_b_glu, m_conv_w, m_gdn_a_log, m_gdn_dt_bias, m_gdn_norm_w, m_w_out, m_ln_g, m_ln_b, v_c_ctx, v_w_ada, v_b_ada, v_w_in, v_s5_lambda_re, v_s5_lambda_im, v_s5_log_dt, v_s5_b_re, v_s5_b_im, v_s5_c_re, v_s5_c_im, v_s5_d, v_w_glu, v_b_glu, v_conv_w, v_gdn_a_log, v_gdn_dt_bias, v_gdn_norm_w, v_w_out, v_ln_g, v_ln_b):
    weights = [c_ctx, w_ada, b_ada, w_in, s5_lambda_re, s5_lambda_im, s5_log_dt, s5_b_re, s5_b_im, s5_c_re, s5_c_im,
               s5_d, w_glu, b_glu, conv_w, gdn_a_log, gdn_dt_bias, gdn_norm_w, w_out, ln_g, ln_b]
    ms = [m_c_ctx, m_w_ada, m_b_ada, m_w_in, m_s5_lambda_re, m_s5_lambda_im, m_s5_log_dt, m_s5_b_re, m_s5_b_im,
          m_s5_c_re, m_s5_c_im, m_s5_d, m_w_glu, m_b_glu, m_conv_w, m_gdn_a_log, m_gdn_dt_bias, m_gdn_norm_w, m_w_out,
          m_ln_g, m_ln_b]
    vs = [v_c_ctx, v_w_ada, v_b_ada, v_w_in, v_s5_lambda_re, v_s5_lambda_im, v_s5_log_dt, v_s5_b_re, v_s5_b_im,
          v_s5_c_re, v_s5_c_im, v_s5_d, v_w_glu, v_b_glu, v_conv_w, v_gdn_a_log, v_gdn_dt_bias, v_gdn_norm_w, v_w_out,
          v_ln_g, v_ln_b]
    cpos = lax.axis_index("c")
    jchip = 2 * lax.axis_index("x") + lax.axis_index("y")

    c_all = gather_devices(c, name="gather_c")
    cc = jnp.concatenate([c_all, jnp.broadcast_to(c_ctx[None, None, :], (N_DEV, 1, D_MODEL)),
                          jnp.zeros((N_DEV, 5, D_MODEL), f32)], axis=1)
    w_ada16 = w_ada[0].astype(bf16)
    b_cols = lax.dynamic_slice_in_dim(b_ada, jchip * ADA_SHARD, ADA_SHARD, axis=1)
    m_mine = exchange_devices(ada_fwd(cc, w_ada16, b_cols), name="exchange_m")
    m_rows = jnp.concatenate([m_mine[2 * j, :3] for j in range(4)], axis=1)

    conv_shard = _conv_rows(conv_w)
    g_in, g_out, g_glu, g_conv = gather_shards(
        [w_in[0].astype(bf16), w_out[0].astype(bf16), w_glu[0].astype(bf16), conv_shard])
    w_in_pad = jnp.concatenate([g_in[0], g_in[1], g_in[2], g_in[3], jnp.zeros((D_MODEL, IN_PAD - P_IN), bf16)], axis=1)
    conv16 = g_conv.transpose(1, 0, 2).reshape(CONV_ROWS, 3 * D_GDN)

    swap = lambda a: jnp.swapaxes(a, 3, 4)
    loss, grad_x, big, small, dm_rows = local_step(
        x, ctx, loss_target, m_rows, w_in_pad, s5_lambda_re, s5_lambda_im, s5_log_dt, swap(s5_b_re), swap(s5_b_im),
        s5_c_re, s5_c_im, s5_d, g_glu.reshape(D_S5, D_S5), b_glu, conv16, gdn_a_log, gdn_dt_bias, gdn_norm_w,
        g_out.reshape(D_MODEL, D_MODEL), ln_g, ln_b)
    me = 2 * jchip + cpos
    loss_hi = loss.astype(bf16).astype(f32)
    loss_row = jnp.zeros((LANES,), f32).at[me].set(loss_hi).at[N_DEV + me].set(loss - loss_hi)

    dm8 = jnp.concatenate([dm_rows, jnp.zeros((5, 3 * D_MODEL), f32)], axis=0)
    dm_by_chip = dm8.reshape(8, 4, ADA_SHARD).transpose(1, 0, 2)
    dm_cols = exchange_devices(jnp.repeat(dm_by_chip, 2, axis=0), name="exchange_dm")
    g_w_ada, pb = ada_bwd(cc, w_ada16, dm_cols)
    pb_all = gather_devices(pb, name="gather_p")
    g_c_ctx = c_ctx_bwd(pb_all, c_ctx[None, :])[0]
    g_b_ada = jnp.concatenate([pb_all[2 * j, 1:2, :ADA_SHARD] for j in range(4)], axis=1)

    d_w_in, d_w_out, d_w_glu, d_conv16 = big
    slabs = [d_w_in[:, :P_IN].reshape(D_MODEL, 4, W_IN_SHARD).transpose(1, 0, 2),
             d_w_out.reshape(4, D_MODEL // 4, D_MODEL),
             d_w_glu.reshape(4, D_S5 // 4, D_S5),
             d_conv16.reshape(CONV_ROWS, 4, 3 * D_GDN // 4).transpose(1, 0, 2),
             _pack_small(small + (loss_row,)).reshape(4, SMALL_QUARTER, LANES)]
    got = swap_halves(slabs)
    q32, q16 = [], []
    for t, (s, g) in enumerate(zip(slabs, got)):
        own = lax.dynamic_index_in_dim(s.reshape(4, 2, s.shape[1] // 2, s.shape[2]), cpos, axis=1, keepdims=False)
        a, b = sum_cores(own, g, name=f"sum_cores{t}")
        q32.append(a)
        q16.append(b)
    rec = scatter_to_chips(q16)
    fs = [sum_chips(lax.dynamic_index_in_dim(q, jchip, axis=0, keepdims=False), r, cpos, name=f"sum_chips{t}")
          for t, (q, r) in enumerate(zip(q32, rec))]
    red = join_halves(fs)
    g_small = _unpack_small(gather_small(red[4][0]).reshape(SMALL_TOTAL, LANES))
    loss = jnp.sum(g_small[-1][:2 * N_DEV])
    g_small = g_small[:-1]
    g_shard = {1: g_w_ada, 3: red[0], 18: red[1], 12: red[2], 14: red[3]}

    grads, deltas, new_m, new_v = [None] * 21, [None] * 21, [None] * 21, [None] * 21
    for t, i in enumerate(SHARDED):
        conv, win = i == 14, i == 3
        prep = (lambda a: _conv_rows(a)[None]) if conv else ((lambda a: jnp.transpose(a, (2, 0, 1))) if win else (lambda a: a))
        g = jnp.transpose(g_shard[i], (2, 0, 1)) if win else g_shard[i]
        d, nm, nv = adamw_3d(prep(weights[i]), g, prep(ms[i]), prep(vs[i]), lead=win, name=f"adamw{t}")
        for lst, val in ((grads, g), (deltas, d), (new_m, nm), (new_v, nv)):
            lst[i] = (val[0, :9].reshape(weights[i].shape) if conv else (jnp.transpose(val, (1, 2, 0)) if win else val))
    g_un = {0: g_c_ctx, 2: g_b_ada, **{i: g_small[n] for n, i in enumerate(SMALL)}}
    swapped = [SMALL[n] for n in SMALL_SWAPPED]
    small_in = lambda lst: [_as_2d(swap(lst[i]) if i in swapped else lst[i]) for i in UNSHARDED]
    sm = adamw_small(small_in(weights), [_as_2d(g_un[i]) for i in UNSHARDED], small_in(ms), small_in(vs))
    for n, i in enumerate(UNSHARDED):
        back = ((lambda a: swap(a.reshape(swap(weights[i]).shape))) if i in swapped
                else (lambda a: a.reshape(weights[i].shape)))
        grads[i] = back(g_un[i])
        for lst, res in ((deltas, sm[0]), (new_m, sm[1]), (new_v, sm[2])):
            lst[i] = back(res[n])
    return (loss, grad_x, *grads, *deltas, *new_m, *new_v)
```

```python
import functools

import jax
import jax.numpy as jnp
from jax import lax
from jax.experimental import pallas as pl
from jax.experimental.pallas import tpu as pltpu

f32 = jnp.float32
bf16 = jnp.bfloat16
SDS = jax.ShapeDtypeStruct

D_MODEL = 1024
D_S5 = 512
S5_GROUP = 16
S5_GROUPS = 32
S5_STATE = 64
S5_HALF = S5_GROUPS * S5_STATE
D_GDN = 512
GDN_HEAD = 128
GDN_HEADS = 4
CHUNK = 64
GRID_W = 64
N_DIR = 2
P_IN = 3088
DEEPNORM_ALPHA = 2.0 ** 0.25
LN_EPS = 1e-5
NORM_EPS = 1e-6
ADAM_LR, ADAM_B1, ADAM_B2, ADAM_EPS, ADAM_WD, ADAM_STEP = 0.001, 0.9, 0.999, 1e-08, 0.01, 10

LANES = 128
VMEM_LIMIT = 56 * 1024 * 1024
TOK_TILE = 256
S5_TILE = 256
MESH = pl.DeviceIdType.MESH


def _cparams(n_grid):
    return pltpu.CompilerParams(dimension_semantics=("arbitrary",) * n_grid, vmem_limit_bytes=VMEM_LIMIT)


def _dot(a, b):
    return jnp.dot(a.astype(bf16), b.astype(bf16), preferred_element_type=f32)


def _dot_nt(a, b):
    return lax.dot_general(a.astype(bf16), b.astype(bf16), (((1,), (1,)), ((), ())), preferred_element_type=f32)


def _dot_tn(a, b):
    return lax.dot_general(a.astype(bf16), b.astype(bf16), (((0,), (0,)), ((), ())), preferred_element_type=f32)


def _dot_hi(a, b):
    return jnp.dot(a, b, precision=lax.Precision.HIGHEST, preferred_element_type=f32)


def _dot_h3(a, b):
    return jnp.dot(a, b, precision=lax.Precision.HIGH, preferred_element_type=f32)


@jax.custom_vjp
def _mm(a, b):
    return _dot(a, b)


@jax.custom_vjp
def _mm_nt(a, b):
    return _dot_nt(a, b)


@jax.custom_vjp
def _mm_tn(a, b):
    return _dot_tn(a, b)


_mm.defvjp(lambda a, b: (_dot(a, b), (a, b)), lambda r, g: (_mm_nt(g, r[1]), _mm_tn(r[0], g)))
_mm_nt.defvjp(lambda a, b: (_dot_nt(a, b), (a, b)), lambda r, g: (_mm(g, r[1]), _mm_tn(g, r[0])))
_mm_tn.defvjp(lambda a, b: (_dot_tn(a, b), (a, b)), lambda r, g: (_mm_nt(r[1], g), _mm(r[0], g)))


def _silu(x):
    return x * jax.nn.sigmoid(x)


def _gelu(x):
    return 0.5 * x * (1.0 + lax.erf(x * (2.0 ** -0.5)))


def _resident(shape):
    nd = len(shape)
    return pl.BlockSpec(shape, lambda *_: (0,) * nd, pipeline_mode=pl.Buffered(1))


def _tok(tile, width, nt=None, rev=False):
    if rev:
        return pl.BlockSpec((None, tile, width), lambda b, n: (b, nt - 1 - n, 0))
    return pl.BlockSpec((None, tile, width), lambda b, n: (b, n, 0))


def _per_batch(rows, width):
    return pl.BlockSpec((None, rows, width), lambda b, n: (b, 0, 0))


def _first_step():
    return jnp.logical_and(pl.program_id(0) == 0, pl.program_id(1) == 0)


ADA_SHARD = 3 * D_MODEL // 4
N_DEV = 8


def ada_fwd(cc, w, b):
    def body(cc_ref, w_ref, b_ref, m_ref):
        for k in range(N_DEV):
            m_ref[k] = _dot(_silu(cc_ref[k]), w_ref[...]) + b_ref[...]

    return pl.pallas_call(body, name="ada_fwd", out_shape=SDS((N_DEV, 8, ADA_SHARD), f32),
                          compiler_params=pltpu.CompilerParams(vmem_limit_bytes=VMEM_LIMIT))(cc, w, b)


def ada_bwd(cc, w, dmj):
    def body(cc_ref, w_ref, dmj_ref, dw_ref, pb_ref):
        dw = jnp.zeros((D_MODEL, ADA_SHARD), f32)
        p = jnp.zeros((8, D_MODEL), f32)
        db = jnp.zeros((1, ADA_SHARD), f32)
        for k in range(N_DEV):
            dw = dw + _dot_tn(_silu(cc_ref[k]), dmj_ref[k])
            p = p + _dot_nt(dmj_ref[k], w_ref[...])
            db = db + jnp.sum(dmj_ref[k], axis=0, keepdims=True)
        dw_ref[0] = dw
        pb_ref[...] = jnp.zeros_like(pb_ref)
        pb_ref[0:1, :] = p[2:3, :]
        pb_ref[1:2, 0:ADA_SHARD] = db

    return pl.pallas_call(
        body, name="ada_bwd", out_shape=[SDS((1, D_MODEL, ADA_SHARD), f32), SDS((8, D_MODEL), f32)],
        compiler_params=pltpu.CompilerParams(vmem_limit_bytes=VMEM_LIMIT))(cc, w, dmj)


def c_ctx_bwd(pb_all, c_ctx):
    def body(p_ref, c_ref, d_ref):
        ds = ((p_ref[0, 0:1, :] + p_ref[2, 0:1, :]) + p_ref[4, 0:1, :]) + p_ref[6, 0:1, :]
        _, vjp = jax.vjp(_silu, c_ref[...])
        d_ref[...] = vjp(ds)[0]

    return pl.pallas_call(body, name="c_ctx_bwd", out_shape=SDS((1, D_MODEL), f32))(pb_all, c_ctx)


N_GATE = 2 * N_DIR * GDN_HEADS
IN_WIDTHS = (D_S5, D_S5, 3 * D_GDN, D_GDN, N_GATE)
IN_OFFS = (0, 512, 1024, 2560, 3072)


def in_proj_fwd(x, mod, wt, *, name):
    B, L, _ = x.shape
    T = min(TOK_TILE, L)

    def body(x_ref, mod_ref, w_ref, *o_refs):
        h = (x_ref[...] * (1.0 + mod_ref[0:1, :]) + mod_ref[1:2, :]).astype(bf16)
        for o_ref, off, wd in zip(o_refs, IN_OFFS, IN_WIDTHS):
            o_ref[...] = _dot_nt(h, w_ref[off:off + wd, :])

    return pl.pallas_call(
        body, name=name, grid=(B, L // T),
        in_specs=[_tok(T, D_MODEL), _per_batch(2, D_MODEL), _resident((P_IN, D_MODEL))],
        out_specs=[_tok(T, wd) for wd in IN_WIDTHS],
        out_shape=[SDS((B, L, wd), f32) for wd in IN_WIDTHS],
        compiler_params=_cparams(2),
    )(x, mod, wt)


def in_proj_bwd(x, mod, ds, wt, gx_res, dw_start, *, name):
    B, L, _ = x.shape
    T = min(TOK_TILE, L)
    with_dx = gx_res is not None
    with_start = dw_start is not None
    n_u = len(ds[0])

    def body(*refs):
        x_ref, mod_ref = refs[0], refs[1]
        du_refs = refs[2:2 + n_u]
        d_refs = refs[2 + n_u:6 + n_u]
        w_ref = refs[6 + n_u]
        k = 7 + n_u
        if with_dx:
            gx_ref = refs[k]
            k += 1
        if with_start:
            start_ref = refs[k]
            k += 1
        dw_ref, dmod_ref = refs[k], refs[k + 1]
        if with_dx:
            dx_ref = refs[k + 2]
        n = pl.program_id(1)

        @pl.when(_first_step())
        def _():
            dw_ref[...] = start_ref[...] if with_start else jnp.zeros_like(dw_ref)

        @pl.when(n == 0)
        def _():
            dmod_ref[...] = jnp.zeros_like(dmod_ref)

        xv = x_ref[...]
        scale1 = 1.0 + mod_ref[0:1, :]
        h = (xv * scale1 + mod_ref[1:2, :]).astype(bf16)
        du = du_refs[0][...]
        for r in du_refs[1:]:
            du = du + r[...]
        dh = jnp.zeros((T, D_MODEL), f32)
        for dv, off, wd in zip([du] + [r[...] for r in d_refs], IN_OFFS, IN_WIDTHS):
            dv = dv.astype(bf16)
            dh = dh + _dot(dv, w_ref[off:off + wd, :])
            dw_ref[off:off + wd, :] += _dot_tn(dv, h)
        dmod_ref[0:1, :] += jnp.sum(dh * xv, axis=0, keepdims=True)
        dmod_ref[1:2, :] += jnp.sum(dh, axis=0, keepdims=True)
        if with_dx:
            dx_ref[...] = gx_ref[...] + dh * scale1

    in_specs = ([_tok(T, D_MODEL), _per_batch(2, D_MODEL)] + [_tok(T, D_S5)] * n_u + [_tok(T, wd) for wd in IN_WIDTHS[1:]]
                + [_resident((P_IN, D_MODEL))])
    args = [x, mod, *ds[0], *ds[1:], wt]
    out_specs = [_resident((P_IN, D_MODEL)), _per_batch(2, D_MODEL)]
    out_shape = [SDS((P_IN, D_MODEL), f32), SDS((B, 2, D_MODEL), f32)]
    if with_dx:
        in_specs.append(_tok(T, D_MODEL))
        args.append(gx_res)
        out_specs.append(_tok(T, D_MODEL))
        out_shape.append(SDS((B, L, D_MODEL), f32))
    if with_start:
        in_specs.append(_resident((P_IN, D_MODEL)))
        args.append(dw_start)
    return pl.pallas_call(body, name=name, grid=(B, L // T), in_specs=in_specs, out_specs=out_specs,
                          out_shape=out_shape, compiler_params=_cparams(2))(*args)


def _s5_zoh(lr, li, ldt, bre, bim, expand):
    dt = jnp.exp(ldt)
    zr, zi = lr * dt, li * dt
    e = jnp.exp(zr)
    ar, ai = e * jnp.cos(zi), e * jnp.sin(zi)
    den = lr * lr + li * li
    czr = ((ar - 1.0) * lr + ai * li) / den
    czi = (ai * lr - (ar - 1.0) * li) / den
    czr_e, czi_e = _dot_hi(czr, expand), _dot_hi(czi, expand)
    return ar, ai, czr_e * bre - czi_e * bim, czr_e * bim + czi_e * bre


_ZOH_OUT = [(N_DIR * S5_GROUPS, S5_STATE)] * 2 + [(N_DIR * S5_GROUPS, S5_STATE * S5_GROUP)] * 2


def s5_zoh_fwd(lr, li, ldt, bre, bim, expand):
    def body(lr_ref, li_ref, ldt_ref, bre_ref, bim_ref, e_ref, ar_ref, ai_ref, bbr_ref, bbi_ref):
        ar, ai, bbr, bbi = _s5_zoh(lr_ref[...], li_ref[...], ldt_ref[...], bre_ref[...], bim_ref[...], e_ref[...])
        ar_ref[...], ai_ref[...], bbr_ref[...], bbi_ref[...] = ar, ai, bbr, bbi

    return pl.pallas_call(body, name="s5_zoh_fwd", out_shape=[SDS(s, f32) for s in _ZOH_OUT])(
        lr, li, ldt, bre, bim, expand)


def s5_zoh_bwd(lr, li, ldt, bre, bim, expand, dar, dai, dbbr, dbbi):
    def body(lr_ref, li_ref, ldt_ref, bre_ref, bim_ref, e_ref, dar_ref, dai_ref, dbbr_ref, dbbi_ref,
             dlr_ref, dli_ref, dldt_ref, dbre_ref, dbim_ref):
        ev = e_ref[...]
        _, vjp = jax.vjp(lambda a, b, c, d, e: _s5_zoh(a, b, c, d, e, ev),
                         lr_ref[...], li_ref[...], ldt_ref[...], bre_ref[...], bim_ref[...])
        outs = vjp((dar_ref[...], dai_ref[...], dbbr_ref[...], dbbi_ref[...]))
        dlr_ref[...], dli_ref[...], dldt_ref[...], dbre_ref[...], dbim_ref[...] = outs

    shapes = [lr.shape, li.shape, ldt.shape, bre.shape, bim.shape]
    return pl.pallas_call(body, name="s5_zoh_bwd", out_shape=[SDS(s, f32) for s in shapes])(
        lr, li, ldt, bre, bim, expand, dar, dai, dbbr, dbbi)


def _scan_rows(T, rev, ar, ai, h0s, refs, off):
    def step(i, carry):
        t = off + ((T - 1 - i) if rev else i)
        out = []
        for (hr, hi), (r_ref, i_ref) in zip(carry, refs):
            nr = ar * hr - ai * hi + r_ref[pl.ds(t, 1), :]
            ni = ar * hi + ai * hr + i_ref[pl.ds(t, 1), :]
            r_ref[pl.ds(t, 1), :] = nr
            i_ref[pl.ds(t, 1), :] = ni
            out.append((nr, ni))
        return tuple(out)

    return lax.fori_loop(0, T, step, tuple(h0s))


S5_BLOCKS = 4
S5_BC = D_S5 // S5_BLOCKS
S5_BS = S5_HALF // S5_BLOCKS


def _s5_in(uv, bre_ref, bim_ref, hr_ref, hi_ref, off, T):
    for jb in range(S5_BLOCKS):
        uj = uv[:, jb * S5_BC:(jb + 1) * S5_BC]
        hr_ref[off:off + T, jb * S5_BS:(jb + 1) * S5_BS] = _dot(uj, bre_ref[jb])
        hi_ref[off:off + T, jb * S5_BS:(jb + 1) * S5_BS] = _dot(uj, bim_ref[jb])


def _s5_specs(B, T, nt, rev):
    tidx = (lambda n: nt - 1 - n) if rev else (lambda n: n)
    tok = pl.BlockSpec((B, T, D_S5), lambda n: (0, tidx(n), 0))
    hin = pl.BlockSpec((B, None, 2, S5_HALF), lambda n: (0, tidx(n), 0, 0))
    state = pl.BlockSpec((B, 2, S5_HALF), lambda n: (0, 0, 0))
    return tok, hin, state


def s5_scan_fwd(u, bre, bim, ctop, cbot, arow, h0, *, d, need_y, name):
    B, L, _ = u.shape
    T = min(S5_TILE, L)
    nt = L // T
    rev = d == 1

    def body(u_ref, bre_ref, bim_ref, ct_ref, cb_ref, a_ref, h0_ref, *rest):
        if need_y:
            y_ref, hs_ref, hin_ref, hend_ref, hr_scr, hi_scr, h_scr = rest
        else:
            hs_ref, hin_ref, hend_ref, hr_scr, hi_scr, h_scr = rest
        n = pl.program_id(0)

        @pl.when(n == 0)
        def _():
            h_scr[...] = h0_ref[...]

        hin_ref[...] = h_scr[...]
        for b in range(B):
            _s5_in(u_ref[b].astype(bf16), bre_ref, bim_ref, hr_scr.at[b], hi_scr.at[b], 0, T)
        hs = _scan_rows(T, rev, a_ref[0:1, :], a_ref[1:2, :], [(h_scr[b, 0:1, :], h_scr[b, 1:2, :]) for b in range(B)],
                        [(hr_scr.at[b], hi_scr.at[b]) for b in range(B)], 0)
        for b in range(B):
            h_scr[b, 0:1, :] = hs[b][0]
            h_scr[b, 1:2, :] = hs[b][1]
            hs_ref[b, :, 0:S5_HALF] = hr_scr[b].astype(bf16)
            hs_ref[b, :, S5_HALF:2 * S5_HALF] = hi_scr[b].astype(bf16)
            if need_y:
                for jb in range(S5_BLOCKS):
                    st = slice(jb * S5_BS, (jb + 1) * S5_BS)
                    y_ref[b, :, jb * S5_BC:(jb + 1) * S5_BC] = (_dot(hr_scr[b, :, st], ct_ref[jb])
                                                                 + _dot(hi_scr[b, :, st], cb_ref[jb]))

        @pl.when(n == nt - 1)
        def _():
            hend_ref[...] = h_scr[...]

    tok, hin_spec, state = _s5_specs(B, T, nt, rev)
    hs_spec = pl.BlockSpec((B, T, 2 * S5_HALF), tok.index_map)
    out_specs = [hs_spec, hin_spec, state]
    out_shape = [SDS((B, L, 2 * S5_HALF), bf16), SDS((B, nt, 2, S5_HALF), f32), SDS((B, 2, S5_HALF), f32)]
    if need_y:
        out_specs.insert(0, tok)
        out_shape.insert(0, SDS((B, L, D_S5), f32))
    w_in, w_out = _resident((S5_BLOCKS, S5_BC, S5_BS)), _resident((S5_BLOCKS, S5_BS, S5_BC))
    return pl.pallas_call(
        body, name=name, grid=(nt,),
        in_specs=[tok, w_in, w_in, w_out, w_out, _resident((2, S5_HALF)), state],
        out_specs=out_specs, out_shape=out_shape,
        scratch_shapes=[pltpu.VMEM((B, T, S5_HALF), f32), pltpu.VMEM((B, T, S5_HALF), f32),
                        pltpu.VMEM((B, 2, S5_HALF), f32)],
        compiler_params=_cparams(1),
    )(u, bre, bim, ctop, cbot, arow, h0)


def s5_scan_bwd(u, dy, hs, bre, bim, ctop, cbot, arow, hin, dhend, *, d, name):
    B, L, _ = u.shape
    T = min(S5_TILE, L)
    nt = L // T
    rev = d == 1
    has_dy = dy is not None
    PAD = 8

    def body(*refs):
        u_ref = refs[0]
        k = 1
        if has_dy:
            dy_ref = refs[1]
            k = 2
        hs_ref = refs[k]
        k += 1
        bre_ref, bim_ref, ct_ref, cb_ref, a_ref, hin_ref, dhend_ref = refs[k:k + 7]
        du_ref, dbre_ref, dbim_ref, dct_ref, dcb_ref, da_ref, dh0_ref = refs[k + 7:k + 14]
        hr_scr, hi_scr, gr_scr, gi_scr, p_scr = refs[k + 14:]
        n = pl.program_id(0)

        @pl.when(n == 0)
        def _():
            for r in (dbre_ref, dbim_ref, dct_ref, dcb_ref, da_ref):
                r[...] = jnp.zeros_like(r)
            p_scr[...] = dhend_ref[...]

        ar, ai = a_ref[0:1, :], a_ref[1:2, :]
        prev_row = PAD + T if rev else PAD - 1
        uvs = []
        for b in range(B):
            uvs.append(u_ref[b].astype(bf16))
            hr_scr[b, PAD:PAD + T, :] = hs_ref[b, :, 0:S5_HALF].astype(f32)
            hi_scr[b, PAD:PAD + T, :] = hs_ref[b, :, S5_HALF:2 * S5_HALF].astype(f32)
            hr_scr[b, prev_row:prev_row + 1, :] = hin_ref[b, 0:1, :]
            hi_scr[b, prev_row:prev_row + 1, :] = hin_ref[b, 1:2, :]
        if has_dy:
            for b in range(B):
                dyv = dy_ref[b].astype(bf16)
                for jb in range(S5_BLOCKS):
                    st = slice(jb * S5_BS, (jb + 1) * S5_BS)
                    dyj = dyv[:, jb * S5_BC:(jb + 1) * S5_BC]
                    gr_scr[b, :, st] = _dot_nt(dyj, ct_ref[jb])
                    gi_scr[b, :, st] = _dot_nt(dyj, cb_ref[jb])
                    dct_ref[jb] += _dot_tn(hr_scr[b, PAD:PAD + T, st], dyj)
                    dcb_ref[jb] += _dot_tn(hi_scr[b, PAD:PAD + T, st], dyj)
        else:
            gr_scr[...] = jnp.zeros_like(gr_scr)
            gi_scr[...] = jnp.zeros_like(gi_scr)

        def step(i, carry):
            t = i if rev else T - 1 - i
            tp = PAD + t + (1 if rev else -1)
            out = []
            for b, (pr, pi, dar, dai) in enumerate(carry):
                gr = gr_scr[b, pl.ds(t, 1), :] + pr
                gi = gi_scr[b, pl.ds(t, 1), :] + pi
                gr_scr[b, pl.ds(t, 1), :] = gr
                gi_scr[b, pl.ds(t, 1), :] = gi
                hpr = hr_scr[b, pl.ds(tp, 1), :]
                hpi = hi_scr[b, pl.ds(tp, 1), :]
                out.append((ar * gr + ai * gi, ar * gi - ai * gr, dar + hpr * gr + hpi * gi, dai + hpr * gi - hpi * gr))
            return tuple(out)

        zero = jnp.zeros((1, S5_HALF), f32)
        res = lax.fori_loop(0, T, step, tuple((p_scr[b, 0:1, :], p_scr[b, 1:2, :], zero, zero) for b in range(B)))
        for b in range(B):
            pr, pi, dar, dai = res[b]
            p_scr[b, 0:1, :] = pr
            p_scr[b, 1:2, :] = pi
            da_ref[0:1, :] += dar
            da_ref[1:2, :] += dai
            for jb in range(S5_BLOCKS):
                st = slice(jb * S5_BS, (jb + 1) * S5_BS)
                ch = slice(jb * S5_BC, (jb + 1) * S5_BC)
                gr_j = gr_scr[b, :, st].astype(bf16)
                gi_j = gi_scr[b, :, st].astype(bf16)
                du_ref[b, :, ch] = _dot_nt(gr_j, bre_ref[jb]) + _dot_nt(gi_j, bim_ref[jb])
                dbre_ref[jb] += _dot_tn(uvs[b][:, ch], gr_j)
                dbim_ref[jb] += _dot_tn(uvs[b][:, ch], gi_j)

        @pl.when(n == nt - 1)
        def _():
            dh0_ref[...] = p_scr[...]

    tok, hin_spec, state = _s5_specs(B, T, nt, not rev)
    hs_spec = pl.BlockSpec((B, T, 2 * S5_HALF), tok.index_map)
    w_in, w_out = _resident((S5_BLOCKS, S5_BC, S5_BS)), _resident((S5_BLOCKS, S5_BS, S5_BC))
    wspecs = [w_in, w_in, w_out, w_out]
    in_specs = [tok] + ([tok] if has_dy else []) + [hs_spec] + wspecs + [_resident((2, S5_HALF)), hin_spec, state]
    args = [u] + ([dy] if has_dy else []) + [hs, bre, bim, ctop, cbot, arow, hin, dhend]
    return pl.pallas_call(
        body, name=name, grid=(nt,), in_specs=in_specs,
        out_specs=[tok] + wspecs + [_resident((2, S5_HALF)), state],
        out_shape=[SDS((B, L, D_S5), f32), SDS((S5_BLOCKS, S5_BC, S5_BS), f32), SDS((S5_BLOCKS, S5_BC, S5_BS), f32),
                   SDS((S5_BLOCKS, S5_BS, S5_BC), f32), SDS((S5_BLOCKS, S5_BS, S5_BC), f32), SDS((2, S5_HALF), f32),
                   SDS((B, 2, S5_HALF), f32)],
        scratch_shapes=[pltpu.VMEM((B, T + 2 * PAD, S5_HALF), f32), pltpu.VMEM((B, T + 2 * PAD, S5_HALF), f32),
                        pltpu.VMEM((B, T, S5_HALF), f32), pltpu.VMEM((B, T, S5_HALF), f32),
                        pltpu.VMEM((B, 2, S5_HALF), f32)],
        compiler_params=_cparams(1),
    )(*args)


def _glu_fn(u, y0, y1, z, dsk, wg, bg):
    g = _gelu(dsk * u + y0 + y1)
    return g * jax.nn.sigmoid(_mm(g, wg) + bg) * _silu(z)


CONV_ROWS = 16


def _shift(x, s):
    L = x.shape[0]
    k = (-s) % L
    return x if k == 0 else pltpu.roll(x, k, axis=0)


def _r16(v):
    return v.astype(bf16).astype(f32)


def _conv_masks(L, is_ctx):
    t = lax.broadcasted_iota(jnp.int32, (L, 1), 0)
    if is_ctx:
        return t == L - 1, t == 0, None, None
    col = jnp.bitwise_and(t, GRID_W - 1)
    return col == GRID_W - 1, col == 0, t >= GRID_W, t < L - GRID_W


def _conv_sides(xv, masks):
    no_left, no_right, _, _ = masks
    return _shift(jnp.where(no_left, 0.0, xv), -1), _shift(jnp.where(no_right, 0.0, xv), 1)


def _conv_pre(xv, w_ref, masks, is_ctx):
    xv = _r16(xv)
    wv = _r16(w_ref[...])
    xl, xr = _conv_sides(xv, masks)
    z = [wv[3 * di:3 * di + 1, :] * xl + wv[3 * di + 1:3 * di + 2, :] * xv + wv[3 * di + 2:3 * di + 3, :] * xr
         for di in ((1,) if is_ctx else (0, 1, 2))]
    if is_ctx:
        return z[0]
    _, _, has_up, has_down = masks
    return z[1] + jnp.where(has_up, _shift(z[0], -GRID_W), 0.0) + jnp.where(has_down, _shift(z[2], GRID_W), 0.0)


def _conv_pre_bwd(xv, w_ref, dpre, masks, is_ctx, dw_ref):
    no_left, no_right, has_up, has_down = masks
    xv, dpre, wv = _r16(xv), _r16(dpre), _r16(w_ref[...])
    xl, xr = _conv_sides(xv, masks)
    if is_ctx:
        dz = {1: dpre}
    else:
        dz = {0: _shift(jnp.where(has_up, dpre, 0.0), GRID_W), 1: dpre, 2: _shift(jnp.where(has_down, dpre, 0.0), -GRID_W)}
    dxl = dxc = dxr = None
    for di, d in dz.items():
        for dj, side in enumerate((xl, xv, xr)):
            dw_ref[3 * di + dj:3 * di + dj + 1, :] = jnp.sum(d * side, axis=0, keepdims=True)
        tl, tc, tr = (wv[3 * di + dj:3 * di + dj + 1, :] * d for dj in range(3))
        dxl, dxc, dxr = (tl, tc, tr) if dxl is None else (dxl + tl, dxc + tc, dxr + tr)
    return dxc + jnp.where(no_left, 0.0, _shift(dxl, 1)) + jnp.where(no_right, 0.0, _shift(dxr, -1))


def _qk_post(pre, is_norm, scale):
    s = _silu(pre)
    nrm = lax.rsqrt(jnp.sum(s * s, axis=-1, keepdims=True) + NORM_EPS)
    return s * jnp.where(is_norm, nrm * scale, 1.0)


def _conv_kind():
    ct = pl.program_id(1)
    return ct < 2 * GDN_HEADS, jnp.where(ct < GDN_HEADS, GDN_HEAD ** -0.5, 1.0).astype(f32)


def conv_fwd(qkv, w16, *, is_ctx, name):
    B, L, C = qkv.shape
    spec = pl.BlockSpec((None, L, GDN_HEAD), lambda b, ct: (b, 0, ct))
    wspec = pl.BlockSpec((CONV_ROWS, GDN_HEAD), lambda b, ct: (0, ct))

    def body(x_ref, w_ref, o_ref, pre_ref):
        is_norm, scale = _conv_kind()
        pre = _conv_pre(x_ref[...], w_ref, _conv_masks(L, is_ctx), is_ctx)
        pre_ref[...] = pre
        o_ref[...] = _qk_post(pre, is_norm, scale)

    return pl.pallas_call(body, name=name, grid=(B, C // GDN_HEAD), in_specs=[spec, wspec], out_specs=[spec, spec],
                          out_shape=[SDS((B, L, C), f32)] * 2, compiler_params=_cparams(2))(qkv, w16)


def conv_bwd(qkv, pre, w16, da0, da1, *, is_ctx, name):
    B, L, C = qkv.shape
    spec = pl.BlockSpec((None, L, GDN_HEAD), lambda b, ct: (b, 0, ct))
    wspec = pl.BlockSpec((CONV_ROWS, GDN_HEAD), lambda b, ct: (0, ct))
    dwspec = pl.BlockSpec((None, CONV_ROWS, GDN_HEAD), lambda b, ct: (b, 0, ct))

    def body(x_ref, pre_ref, w_ref, d0_ref, d1_ref, dx_ref, dw_ref):
        is_norm, scale = _conv_kind()
        _, vjp = jax.vjp(lambda p: _qk_post(p, is_norm, scale), pre_ref[...])
        dpre = vjp(d0_ref[...] + d1_ref[...])[0]
        dw_ref[...] = jnp.zeros_like(dw_ref)
        dx_ref[...] = _conv_pre_bwd(x_ref[...], w_ref, dpre, _conv_masks(L, is_ctx), is_ctx, dw_ref)

    return pl.pallas_call(body, name=name, grid=(B, C // GDN_HEAD), in_specs=[spec, spec, wspec, spec, spec],
                          out_specs=[spec, dwspec], out_shape=[SDS((B, L, C), f32), SDS((B, CONV_ROWS, C), f32)],
                          compiler_params=_cparams(2))(qkv, pre, w16, da0, da1)


def _gates_fn(ba, alog, dtb):
    T = ba.shape[0]
    lane = lax.broadcasted_iota(jnp.int32, ba.shape, 1)
    ii = lax.broadcasted_iota(jnp.int32, (T, T), 0)
    jj = lax.broadcasted_iota(jnp.int32, (T, T), 1)
    same = jnp.right_shift(ii, 6) == jnp.right_shift(jj, 6)
    lmat = jnp.logical_and(same, ii >= jj).astype(f32)
    umat = jnp.logical_and(same, ii <= jj).astype(f32)
    g = jnp.where(lane >= 8, -jnp.exp(alog) * jax.nn.softplus(ba + dtb), 0.0)
    gc = jnp.where(lane >= 12, _dot_hi(umat, g), _dot_hi(lmat, g))
    return jnp.where(lane < 8, jax.nn.sigmoid(ba), gc)


def gates_fwd(ba, alog, dtb, *, name):
    B, L, _ = ba.shape
    T = min(TOK_TILE, L)
    t = _tok(T, N_GATE)

    def body(ba_ref, al_ref, dt_ref, o_ref):
        o_ref[...] = _gates_fn(ba_ref[...], al_ref[...], dt_ref[...])

    return pl.pallas_call(body, name=name, grid=(B, L // T),
                          in_specs=[t, _resident((1, N_GATE)), _resident((1, N_GATE))], out_specs=t,
                          out_shape=SDS((B, L, N_GATE), f32), compiler_params=_cparams(2))(ba, alog, dtb)


def gates_bwd(ba, alog, dtb, dbg, *, name):
    B, L, _ = ba.shape
    T = min(TOK_TILE, L)
    t = _tok(T, N_GATE)
    small = _resident((1, N_GATE))

    def body(ba_ref, al_ref, dt_ref, d_ref, dba_ref, dal_ref, ddt_ref):
        @pl.when(_first_step())
        def _():
            dal_ref[...] = jnp.zeros_like(dal_ref)
            ddt_ref[...] = jnp.zeros_like(ddt_ref)

        _, vjp = jax.vjp(_gates_fn, ba_ref[...], al_ref[...], dt_ref[...])
        dba, dal, ddt = vjp(d_ref[...])
        dba_ref[...] = dba
        dal_ref[...] += dal
        ddt_ref[...] += ddt

    return pl.pallas_call(body, name=name, grid=(B, L // T), in_specs=[t, small, small, t],
                          out_specs=[t, small, small],
                          out_shape=[SDS((B, L, N_GATE), f32), SDS((1, N_GATE), f32), SDS((1, N_GATE), f32)],
                          compiler_params=_cparams(2))(ba, alog, dtb, dbg)


@jax.custom_vjp
def _inv_unit_tri(mats):
    n = mats[0].shape[0]
    eye = (lax.broadcasted_iota(jnp.int32, (n, n), 0) == lax.broadcasted_iota(jnp.int32, (n, n), 1)).astype(f32)
    xs = [eye - a for a in mats]
    sq = [_dot(a, a) for a in mats]
    ps = sq
    k = 2
    while k < n:
        xs = [x + _dot(x, p) for x, p in zip(xs, ps)]
        k *= 2
        if k < n:
            ps = [_dot(p, p) for p in ps]
    return tuple(_dot(p, x) - a for p, x, a in zip(sq, xs, mats))


def _inv_unit_tri_fwd(mats):
    ns = _inv_unit_tri(mats)
    return ns, ns


def _inv_unit_tri_bwd(ns, dns):
    ys = [dn + _dot_tn(nn, dn) for nn, dn in zip(ns, dns)]
    return (tuple(-(y + _dot_nt(y, nn)) for y, nn in zip(ys, ns)),)


_inv_unit_tri.defvjp(_inv_unit_tri_fwd, _inv_unit_tri_bwd)


@jax.custom_vjp
def _inv_unit_tri_saved(mats, saved):
    return saved


_inv_unit_tri_saved.defvjp(lambda mats, saved: (saved, saved),
                           lambda ns, dns: _inv_unit_tri_bwd(ns, dns) + (tuple(jnp.zeros_like(n) for n in ns),))


def _gdn_chunk(heads, *, revs, saved=None, with_n=False):
    n = heads[0][0].shape[0]
    ii = lax.broadcasted_iota(jnp.int32, (n, n), 0)
    jj = lax.broadcasted_iota(jnp.int32, (n, n), 1)
    row = lax.broadcasted_iota(jnp.int32, (n, 1), 0)
    lower = {False: ii >= jj, True: ii <= jj}
    strict = {False: ii > jj, True: ii < jj}
    last = {False: n - 1, True: 0}
    H = range(len(heads))
    q, k, v, beta, gc, gr, s = (list(t) for t in zip(*heads))
    decay = [jnp.where(lower[revs[h]], jnp.exp(jnp.where(lower[revs[h]], gc[h] - gr[h], 0.0)), 0.0) for h in H]
    kk = [_mm_nt(k[h], k[h]) for h in H]
    qk = [_mm_nt(q[h], k[h]) * decay[h] for h in H]
    qs = [_mm(q[h], s[h]) for h in H]
    a_mat = tuple(jnp.where(strict[revs[h]], beta[h] * kk[h] * decay[h], 0.0) for h in H)
    gamma = [jnp.exp(gc[h]) for h in H]
    g_last = [jnp.sum(jnp.where(row == last[revs[h]], gc[h], 0.0), axis=0, keepdims=True) for h in H]
    nmat = _inv_unit_tri(a_mat) if saved is None else _inv_unit_tri_saved(a_mat, saved)
    bv = [beta[h] * v[h] for h in H]
    bk = [(beta[h] * gamma[h]) * k[h] for h in H]
    u0 = [bv[h] + _mm(nmat[h], bv[h]) for h in H]
    w = [bk[h] + _mm(nmat[h], bk[h]) for h in H]
    k_out = [k[h] * jnp.exp(g_last[h] - gc[h]) for h in H]
    u = [u0[h] - _mm(w[h], s[h]) for h in H]
    o = [gamma[h] * qs[h] + _mm(qk[h], u[h]) for h in H]
    s_new = [jnp.exp(g_last[h]) * s[h] + _mm_tn(k_out[h], u[h]) for h in H]
    outs = tuple((o[h], s_new[h]) for h in H)
    return (outs, nmat) if with_n else outs


def _gdn_specs(B, nc, rev):
    def cidx(n):
        return (nc - 1 - n) if rev else n
    tok = lambda width: pl.BlockSpec((B, CHUNK, width), lambda n: (0, cidx(n), 0))
    rowspec = pl.BlockSpec((B, None, N_GATE, CHUNK), lambda n: (0, cidx(n), 0, 0))
    st = pl.BlockSpec((B, GDN_HEADS, GDN_HEAD, GDN_HEAD), lambda n: (0, 0, 0, 0))
    ck = pl.BlockSpec((B, None, GDN_HEADS, GDN_HEAD, GDN_HEAD), lambda n: (0, cidx(n), 0, 0, 0))
    nsp = pl.BlockSpec((B, None, GDN_HEADS, CHUNK, CHUNK), lambda n: (0, cidx(n), 0, 0, 0))
    return tok, rowspec, st, ck, nsp


def _gdn_head_args(qkv_ref, bg_ref, bgr_ref, b, d, h):
    col = d * GDN_HEADS + h
    q = qkv_ref[b, :, h * GDN_HEAD:(h + 1) * GDN_HEAD]
    k = qkv_ref[b, :, D_GDN + h * GDN_HEAD:D_GDN + (h + 1) * GDN_HEAD]
    v = qkv_ref[b, :, 2 * D_GDN + h * GDN_HEAD:2 * D_GDN + (h + 1) * GDN_HEAD]
    bgv = bg_ref[b]
    return q, k, v, bgv[:, col:col + 1], bgv[:, 8 + col:9 + col], bgr_ref[b][8 + col:9 + col, :]


def _gdn_chains(B):
    return [(d, b, h) for d in range(N_DIR) for b in range(B) for h in range(GDN_HEADS)]


def gdn_fwd(qkv, bg, bgr, s0s, *, need_o, name):
    B, L, _ = qkv.shape
    nc = L // CHUNK
    specs = [_gdn_specs(B, nc, d == 1) for d in range(N_DIR)]
    chains = _gdn_chains(B)
    state_shape = (B, GDN_HEADS, GDN_HEAD, GDN_HEAD)

    def body(*refs):
        ins = [refs[3 * d:3 * d + 3] for d in range(N_DIR)]
        s0_refs = refs[6:8]
        k = 8
        o_refs = refs[k:k + 2] if need_o else None
        k += 2 if need_o else 0
        ck_refs, n_refs, sf_refs, s_scrs = refs[k:k + 2], refs[k + 2:k + 4], refs[k + 4:k + 6], refs[k + 6:k + 8]
        n = pl.program_id(0)

        @pl.when(n == 0)
        def _():
            for d in range(N_DIR):
                s_scrs[d][...] = s0_refs[d][...]

        for d in range(N_DIR):
            ck_refs[d][...] = s_scrs[d][...]
        heads = tuple(_gdn_head_args(*ins[d], b, d, h) + (s_scrs[d][b, h],) for d, b, h in chains)
        outs, nmat = _gdn_chunk(heads, revs=tuple(d == 1 for d, _, _ in chains), with_n=True)
        for (d, b, h), (o, s_new), nn in zip(chains, outs, nmat):
            if need_o:
                o_refs[d][b, :, h * GDN_HEAD:(h + 1) * GDN_HEAD] = o
            s_scrs[d][b, h] = s_new
            n_refs[d][b, h] = nn

        @pl.when(n == nc - 1)
        def _():
            for d in range(N_DIR):
                sf_refs[d][...] = s_scrs[d][...]

    in_specs, out_o, out_ck, out_n, out_sf = [], [], [], [], []
    for tok, rowspec, st, ck, nsp in specs:
        in_specs += [tok(3 * D_GDN), tok(N_GATE), rowspec]
        out_o.append(tok(D_GDN))
        out_ck.append(ck)
        out_n.append(nsp)
        out_sf.append(st)
    in_specs += [specs[0][2]] * 2
    out_specs = (out_o if need_o else []) + out_ck + out_n + out_sf
    out_shape = (([SDS((B, L, D_GDN), f32)] * 2 if need_o else []) + [SDS((B, nc) + state_shape[1:], f32)] * 2
                 + [SDS((B, nc, GDN_HEADS, CHUNK, CHUNK), f32)] * 2 + [SDS(state_shape, f32)] * 2)
    res = pl.pallas_call(
        body, name=name, grid=(nc,), in_specs=in_specs, out_specs=out_specs, out_shape=out_shape,
        scratch_shapes=[pltpu.VMEM(state_shape, f32)] * 2, compiler_params=_cparams(1),
    )(qkv, bg, bgr, qkv, bg, bgr, *s0s)
    if need_o:
        return res[0:2], res[2:4], res[4:6], res[6:8]
    return res[0:2], res[2:4], res[4:6]


def gdn_bwd(qkv, bg, bgr, cks, ns, do, dsfs, *, name):
    B, L, _ = qkv.shape
    nc = L // CHUNK
    has_do = do is not None
    specs = [_gdn_specs(B, nc, d != 1) for d in range(N_DIR)]
    chains = _gdn_chains(B)
    state_shape = (B, GDN_HEADS, GDN_HEAD, GDN_HEAD)
    per_dir = 6 if has_do else 5

    def body(*refs):
        ins = [refs[per_dir * d:per_dir * d + per_dir] for d in range(N_DIR)]
        k = per_dir * N_DIR
        dsf_refs = refs[k:k + 2]
        outs = [refs[k + 2 + 3 * d:k + 5 + 3 * d] for d in range(N_DIR)]
        ds0_refs, ds_scrs = refs[k + 8:k + 10], refs[k + 10:k + 12]
        n = pl.program_id(0)

        @pl.when(n == 0)
        def _():
            for d in range(N_DIR):
                ds_scrs[d][...] = dsf_refs[d][...]

        lane = lax.broadcasted_iota(jnp.int32, (CHUNK, N_GATE), 1)
        sub = lax.broadcasted_iota(jnp.int32, (N_GATE, CHUNK), 0)
        heads = tuple(_gdn_head_args(*ins[d][:3], b, d, h) + (ins[d][3][b, h],) for d, b, h in chains)
        saved = tuple(ins[d][4][b, h] for d, b, h in chains)
        _, vjp = jax.vjp(functools.partial(_gdn_chunk, revs=tuple(d == 1 for d, _, _ in chains), saved=saved), heads)
        zero = jnp.zeros((CHUNK, GDN_HEAD), f32)
        cts = tuple(((ins[d][5][b, :, h * GDN_HEAD:(h + 1) * GDN_HEAD] if has_do else zero), ds_scrs[d][b, h])
                    for d, b, h in chains)
        (dheads,) = vjp(cts)
        dbg_acc = [[jnp.zeros((CHUNK, N_GATE), f32) for _ in range(B)] for _ in range(N_DIR)]
        dbgr_acc = [[jnp.zeros((N_GATE, CHUNK), f32) for _ in range(B)] for _ in range(N_DIR)]
        for (d, b, h), (dq, dk, dv, db, dgc, dgr, ds) in zip(chains, dheads):
            col = d * GDN_HEADS + h
            dqkv_ref = outs[d][0]
            dqkv_ref[b, :, h * GDN_HEAD:(h + 1) * GDN_HEAD] = dq
            dqkv_ref[b, :, D_GDN + h * GDN_HEAD:D_GDN + (h + 1) * GDN_HEAD] = dk
            dqkv_ref[b, :, 2 * D_GDN + h * GDN_HEAD:2 * D_GDN + (h + 1) * GDN_HEAD] = dv
            dbg_acc[d][b] = dbg_acc[d][b] + jnp.where(lane == col, db, 0.0) + jnp.where(lane == 8 + col, dgc, 0.0)
            dbgr_acc[d][b] = dbgr_acc[d][b] + jnp.where(sub == 8 + col, dgr, 0.0)
            ds_scrs[d][b, h] = ds
        for d in range(N_DIR):
            for b in range(B):
                outs[d][1][b] = dbg_acc[d][b]
                outs[d][2][b] = dbgr_acc[d][b]

        @pl.when(n == nc - 1)
        def _():
            for d in range(N_DIR):
                ds0_refs[d][...] = ds_scrs[d][...]

    in_specs, args, out_specs, out_shape = [], [], [], []
    for d, (tok, rowspec, st, ck, nsp) in enumerate(specs):
        in_specs += [tok(3 * D_GDN), tok(N_GATE), rowspec, ck, nsp] + ([tok(D_GDN)] if has_do else [])
        args += [qkv, bg, bgr, cks[d], ns[d]] + ([do] if has_do else [])
        out_specs += [tok(3 * D_GDN), tok(N_GATE), rowspec]
        out_shape += [SDS((B, L, 3 * D_GDN), f32), SDS((B, L, N_GATE), f32), SDS((B, nc, N_GATE, CHUNK), f32)]
    st = specs[0][2]
    in_specs += [st, st]
    args += list(dsfs)
    out_specs += [st, st]
    out_shape += [SDS(state_shape, f32)] * 2
    res = pl.pallas_call(
        body, name=name, grid=(nc,), in_specs=in_specs, out_specs=out_specs, out_shape=out_shape,
        scratch_shapes=[pltpu.VMEM(state_shape, f32)] * 2, compiler_params=_cparams(1),
    )(*args)
    return (res[0], res[3]), (res[1], res[4]), (res[2], res[5]), (res[6], res[7])


def _gnorm_fn(o0, o1, z, w):
    o = o0 + o1
    return o * lax.rsqrt(jnp.mean(o * o, axis=-1, keepdims=True) + NORM_EPS) * w * _silu(z)


def _head_loss(y, x, gate, lng, lnb, tgt):
    r = DEEPNORM_ALPHA * x + gate * y
    mu = jnp.mean(r, axis=-1, keepdims=True)
    rc = r - mu
    var = jnp.mean(rc * rc, axis=-1, keepdims=True)
    err = rc * lax.rsqrt(var + LN_EPS) * lng + lnb - tgt
    return (0.5 / D_MODEL) * jnp.sum(jnp.sum(err * err, axis=-1, keepdims=True), axis=0, keepdims=True)


def tail_fwd_bwd(u, y0, y1, z_s5, o0, o1, z_gdn, x, tgt, gate, lng, lnb, ws, wg, dsk, wglu, bglu, nw):
    B, L, _ = x.shape
    T = min(TOK_TILE, L)

    def body(u_ref, y0_ref, y1_ref, z_ref, o0_ref, o1_ref, zg_ref, x_ref, t_ref, gate_ref, lng_ref, lnb_ref, ws_ref,
             wg_ref, dsk_ref, wglu_ref, bglu_ref, nw_ref,
             loss_ref, du_ref, dys_ref, dz_ref, do_ref, dzg_ref, gx_ref, dws_ref, dwg_ref, dgate_ref, dlng_ref, dlnb_ref,
             ddsk_ref, dwglu_ref, dbglu_ref, dnw_ref):
        n = pl.program_id(1)

        @pl.when(_first_step())
        def _():
            for r in (dws_ref, dwg_ref, dlng_ref, dlnb_ref, ddsk_ref, dwglu_ref, dbglu_ref, dnw_ref):
                r[...] = jnp.zeros_like(r)

        @pl.when(n == 0)
        def _():
            loss_ref[...] = jnp.zeros_like(loss_ref)
            dgate_ref[...] = jnp.zeros_like(dgate_ref)

        s5o, glu_vjp = jax.vjp(_glu_fn, u_ref[...], y0_ref[...], y1_ref[...], z_ref[...], dsk_ref[...],
                               wglu_ref[...].astype(f32), bglu_ref[...])
        heads = []
        for h in range(GDN_HEADS):
            sl = slice(h * GDN_HEAD, (h + 1) * GDN_HEAD)
            heads.append(jax.vjp(_gnorm_fn, o0_ref[:, sl], o1_ref[:, sl], zg_ref[:, sl], nw_ref[...]))
        sv = s5o.astype(bf16)
        gv = jnp.concatenate([out for out, _ in heads], axis=1).astype(bf16)
        y = _dot(sv, ws_ref[...]) + _dot(gv, wg_ref[...])
        loss, vjp = jax.vjp(lambda *a: _head_loss(*a, t_ref[...]), y, x_ref[...], gate_ref[...], lng_ref[...],
                            lnb_ref[...])
        dy, dx, dgate, dlng, dlnb = vjp(jnp.ones((1, 1), f32))
        loss_ref[...] += jnp.broadcast_to(loss, loss_ref.shape)
        dyb = dy.astype(bf16)
        gx_ref[...] = dx
        dws_ref[...] += _dot_tn(sv, dyb)
        dwg_ref[...] += _dot_tn(gv, dyb)
        dgate_ref[...] += dgate
        dlng_ref[...] += dlng
        dlnb_ref[...] += dlnb
        du, dys, _, dz, ddsk, dwglu, dbglu = glu_vjp(_dot_nt(dyb, ws_ref[...]))
        du_ref[...], dys_ref[...], dz_ref[...] = du, dys, dz
        ddsk_ref[...] += ddsk
        dwglu_ref[...] += dwglu
        dbglu_ref[...] += dbglu
        dgdo = _dot_nt(dyb, wg_ref[...])
        for h, (_, hvjp) in enumerate(heads):
            sl = slice(h * GDN_HEAD, (h + 1) * GDN_HEAD)
            do, _, dzg, dnw = hvjp(dgdo[:, sl])
            do_ref[:, sl] = do
            dzg_ref[:, sl] = dzg
            dnw_ref[...] += dnw

    half, full = _tok(T, D_S5), _tok(T, D_MODEL)
    row = _resident((1, D_MODEL))
    wsp = _resident((D_S5, D_MODEL))
    r512, rglu, r128 = _resident((1, D_S5)), _resident((D_S5, D_S5)), _resident((1, GDN_HEAD))
    return pl.pallas_call(
        body, name="tail_fwd_bwd", grid=(B, L // T),
        in_specs=[half] * 7 + [full, full, _per_batch(1, D_MODEL), row, row, wsp, wsp, r512, rglu, r512, r128],
        out_specs=[_per_batch(8, LANES)] + [half] * 5 + [full, wsp, wsp, _per_batch(1, D_MODEL), row, row, r512, rglu, r512,
                                                           r128],
        out_shape=[SDS((B, 8, LANES), f32)] + [SDS((B, L, D_S5), f32)] * 5 + [
            SDS((B, L, D_MODEL), f32), SDS((D_S5, D_MODEL), f32), SDS((D_GDN, D_MODEL), f32), SDS((B, 1, D_MODEL), f32),
            SDS((1, D_MODEL), f32), SDS((1, D_MODEL), f32), SDS((1, D_S5), f32), SDS((D_S5, D_S5), f32), SDS((1, D_S5), f32),
            SDS((1, GDN_HEAD), f32)],
        compiler_params=_cparams(2),
    )(u, y0, y1, z_s5, o0, o1, z_gdn, x, tgt, gate, lng, lnb, ws, wg, dsk, wglu, bglu, nw)


def _adamw_math(w, g, m, v):
    nm = ADAM_B1 * m + (1.0 - ADAM_B1) * g
    nv = ADAM_B2 * v + (1.0 - ADAM_B2) * jnp.square(g)
    m_hat = nm / (1.0 - ADAM_B1 ** ADAM_STEP)
    v_hat = nv / (1.0 - ADAM_B2 ** ADAM_STEP)
    return -ADAM_LR * (m_hat / (jnp.sqrt(v_hat) + ADAM_EPS) + ADAM_WD * w), nm, nv


def _row_tile(rows, cap=512):
    for t in range(min(cap, rows), 15, -1):
        if rows % t == 0 and t % 16 == 0:
            return t
    return rows


def adamw_3d(w, g, m, v, *, lead=False, name):
    R, C = (w.shape[0], w.shape[2]) if lead else w.shape[1:]
    if lead:
        T = next(t for t in range(min(256, R), 0, -1) if R % t == 0)
        spec = pl.BlockSpec((T, 1, C), lambda i: (i, 0, 0))
    else:
        T = _row_tile(R)
        spec = pl.BlockSpec((None, T, C), lambda i: (0, i, 0))

    def body(w_ref, g_ref, m_ref, v_ref, d_ref, nm_ref, nv_ref):
        d_ref[...], nm_ref[...], nv_ref[...] = _adamw_math(w_ref[...], g_ref[...], m_ref[...], v_ref[...])

    return pl.pallas_call(body, name=name, grid=(R // T,), in_specs=[spec] * 4, out_specs=[spec] * 3,
                          out_shape=[SDS(w.shape, f32)] * 3, compiler_params=_cparams(1))(w, g, m, v)


def adamw_small(ws, gs, ms, vs):
    n = len(ws)

    def body(*refs):
        outs = refs[4 * n:]
        for i in range(n):
            d, nm, nv = _adamw_math(refs[i][...], refs[n + i][...], refs[2 * n + i][...], refs[3 * n + i][...])
            outs[i][...], outs[n + i][...], outs[2 * n + i][...] = d, nm, nv

    res = pl.pallas_call(body, name="adamw_small", out_shape=[SDS(w.shape, f32) for w in ws] * 3,
                         compiler_params=pltpu.CompilerParams(vmem_limit_bytes=VMEM_LIMIT))(*ws, *gs, *ms, *vs)
    return res[:n], res[n:2 * n], res[2 * n:]


def sum_cores(own, got, *, name):
    A, H, C = own.shape
    T = _row_tile(H)
    spec = pl.BlockSpec((None, T, C), lambda a, i: (a, i, 0))

    def body(a_ref, b_ref, q32_ref, q16_ref):
        q = a_ref[...] + b_ref[...]
        q32_ref[...] = q
        q16_ref[...] = q.astype(bf16)

    return pl.pallas_call(body, name=name, grid=(A, H // T), in_specs=[spec, spec], out_specs=[spec, spec],
                          out_shape=[SDS((A, H, C), f32), SDS((A, H, C), bf16)], compiler_params=_cparams(2))(own, got)


def sum_chips(mine, rec, cpos, full, *, name):
    H, C = mine.shape
    T = _row_tile(H)
    nt = H // T
    out_idx = (lambda i, c_ref: (0, c_ref[0] * nt + i, 0)) if _by_rows(full) else (lambda i, c_ref: (0, i, c_ref[0]))

    def body(c_ref, m_ref, r_ref, f_ref):
        f_ref[...] = ((m_ref[...] + r_ref[0].astype(f32)) + r_ref[1].astype(f32)) + r_ref[2].astype(f32)

    grid_spec = pltpu.PrefetchScalarGridSpec(
        num_scalar_prefetch=1, grid=(nt,),
        in_specs=[pl.BlockSpec((T, C), lambda i, c_ref: (i, 0)), pl.BlockSpec((3, T, C), lambda i, c_ref: (0, i, 0))],
        out_specs=pl.BlockSpec((None, T, C), out_idx))
    return pl.pallas_call(body, name=name, grid_spec=grid_spec, out_shape=SDS((1,) + tuple(full), f32),
                          compiler_params=_cparams(1))(cpos.reshape(1).astype(jnp.int32), mine, rec)


CHIP_FLIPS = ((1, 0), (0, 1), (1, 1))


def _pos():
    return lax.axis_index("x"), lax.axis_index("y"), lax.axis_index("c")


def _comm_call(body, srcs, out_sds, n_remote, n_local, name):
    any_spec = pl.BlockSpec(memory_space=pl.ANY)
    return pl.pallas_call(
        body, name=name, in_specs=[any_spec] * len(srcs), out_specs=[any_spec] * len(out_sds), out_shape=out_sds,
        scratch_shapes=[pltpu.SemaphoreType.DMA((n_remote,)), pltpu.SemaphoreType.DMA((n_remote,)),
                        pltpu.SemaphoreType.DMA((max(n_local, 1),))],
        compiler_params=pltpu.CompilerParams(has_side_effects=True),
    )(*srcs)


def _remote(src, dst, send_sems, recv_sems, k, target):
    return pltpu.make_async_remote_copy(src, dst, send_sems.at[k], recv_sems.at[k], device_id=target,
                                        device_id_type=MESH)


def _by_rows(shape):
    return shape[0] % 16 == 0


def _half_shape(shape):
    return (shape[0] // 2, shape[1]) if _by_rows(shape) else (shape[0], shape[1] // 2)


def _half_of(ref, lead, c, shape):
    if _by_rows(shape):
        half = shape[0] // 2
        return ref.at[(*lead, pl.ds(pl.multiple_of(c * half, 8), half))]
    half = shape[1] // 2
    return ref.at[(*lead, slice(None), pl.ds(pl.multiple_of(c * half, LANES), half))]


def gather_shards(shards):
    nt = len(shards)

    def body(*refs):
        srcs, outs = refs[:nt], refs[nt:2 * nt]
        send_sems, recv_sems, _ = refs[2 * nt:]
        x, y, c = _pos()
        j = 2 * x + y
        sib = (x, y, 1 - c)
        own = [_remote(srcs[t], outs[t].at[j], send_sems, recv_sems, 7 * t + 6, sib) for t in range(nt)]
        first, passed = [], []
        for k, (fx, fy) in enumerate(CHIP_FLIPS):
            tx, ty = x ^ fx, y ^ fy
            jk = 2 * tx + ty
            for t in range(nt):
                sh = srcs[t].shape
                first.append(_remote(_half_of(srcs[t], (), c, sh), _half_of(outs[t], (j,), c, sh), send_sems, recv_sems,
                                     7 * t + k, (tx, ty, c)))
                landed = _half_of(outs[t], (jk,), c, sh)
                passed.append(_remote(landed, landed, send_sems, recv_sems, 7 * t + 3 + k, sib))
        for cp in first + own:
            cp.start()
        for a, b in zip(first, passed):
            a.wait_recv()
            b.start()
        for cp in passed + own:
            cp.wait_recv()
        for cp in first + passed + own:
            cp.wait_send()

    return _comm_call(body, shards, [SDS((4,) + s.shape, s.dtype) for s in shards], 7 * nt, 0, "gather_shards")


def swap_halves(ps):
    nt = len(ps)

    def body(*refs):
        srcs, outs = refs[:nt], refs[nt:2 * nt]
        send_sems, recv_sems, _ = refs[2 * nt:]
        x, y, c = _pos()
        cps = [_remote(_half_of(srcs[t], (a,), 1 - c, srcs[t].shape[1:]), outs[t].at[a], send_sems, recv_sems, 4 * t + a,
                       (x, y, 1 - c)) for t in range(nt) for a in range(4)]
        for cp in cps:
            cp.start()
        for cp in cps:
            cp.wait()

    return _comm_call(body, ps, [SDS((4,) + _half_shape(p.shape[1:]), p.dtype) for p in ps], 4 * nt, 0, "swap_halves")


def scatter_to_chips(qs):
    nt = len(qs)

    def body(*refs):
        srcs, outs = refs[:nt], refs[nt:2 * nt]
        send_sems, recv_sems, _ = refs[2 * nt:]
        x, y, c = _pos()
        cps = []
        for k, (fx, fy) in enumerate(CHIP_FLIPS):
            tx, ty = x ^ fx, y ^ fy
            for t in range(nt):
                cps.append(_remote(srcs[t].at[2 * tx + ty], outs[t].at[k], send_sems, recv_sems, 3 * t + k, (tx, ty, c)))
        for cp in cps:
            cp.start()
        for cp in cps:
            cp.wait()

    return _comm_call(body, qs, [SDS((3,) + q.shape[1:], q.dtype) for q in qs], 3 * nt, 0, "scatter_to_chips")


def join_halves(fs):
    nt = len(fs)

    def body(*refs):
        outs = refs[nt:2 * nt]
        send_sems, recv_sems, _ = refs[2 * nt:]
        x, y, c = _pos()
        cps = []
        for t in range(nt):
            mine = _half_of(outs[t], (0,), c, outs[t].shape[1:])
            cps.append(_remote(mine, mine, send_sems, recv_sems, t, (x, y, 1 - c)))
        for cp in cps:
            cp.start()
        for cp in cps:
            cp.wait()

    any_spec = pl.BlockSpec(memory_space=pl.ANY)
    return pl.pallas_call(
        body, name="join_halves", in_specs=[any_spec] * nt, out_specs=[any_spec] * nt,
        out_shape=[SDS(f.shape, f.dtype) for f in fs], input_output_aliases={t: t for t in range(nt)},
        scratch_shapes=[pltpu.SemaphoreType.DMA((nt,)), pltpu.SemaphoreType.DMA((nt,)), pltpu.SemaphoreType.DMA((1,))],
        compiler_params=pltpu.CompilerParams(has_side_effects=True),
    )(*fs)


DEV_FLIPS = tuple((fx, fy, fc) for fx in (0, 1) for fy in (0, 1) for fc in (0, 1))[1:]


def gather_devices(block, *, name):
    def body(src, out, send_sems, recv_sems, loc_sems):
        x, y, c = _pos()
        me = 4 * x + 2 * y + c
        mine = pltpu.make_async_copy(src, out.at[me], loc_sems.at[0])
        mine.start()
        cps = [_remote(src, out.at[me], send_sems, recv_sems, k, (x ^ fx, y ^ fy, c ^ fc))
               for k, (fx, fy, fc) in enumerate(DEV_FLIPS)]
        for cp in cps:
            cp.start()
        for cp in cps:
            cp.wait()
        mine.wait()

    return _comm_call(body, [block], [SDS((N_DEV,) + block.shape, block.dtype)], 7, 1, name)[0]


def exchange_devices(blocks, *, name):
    def body(src, out, send_sems, recv_sems, loc_sems):
        x, y, c = _pos()
        me = 4 * x + 2 * y + c
        mine = pltpu.make_async_copy(src.at[me], out.at[me], loc_sems.at[0])
        mine.start()
        cps = []
        for k, (fx, fy, fc) in enumerate(DEV_FLIPS):
            tx, ty, tc = x ^ fx, y ^ fy, c ^ fc
            cps.append(_remote(src.at[4 * tx + 2 * ty + tc], out.at[me], send_sems, recv_sems, k, (tx, ty, tc)))
        for cp in cps:
            cp.start()
        for cp in cps:
            cp.wait()
        mine.wait()

    return _comm_call(body, [blocks], [SDS(blocks.shape, blocks.dtype)], 7, 1, name)[0]


def gather_small(s):
    def body(src, out, send_sems, recv_sems, _):
        x, y, c = _pos()
        j = 2 * x + y
        cps = [_remote(src, out.at[j], send_sems, recv_sems, k, (x ^ fx, y ^ fy, c)) for k, (fx, fy) in enumerate(CHIP_FLIPS)]
        cps.append(_remote(src, out.at[j], send_sems, recv_sems, 3, (x, y, 1 - c)))
        for cp in cps:
            cp.start()
        for cp in cps:
            cp.wait()

    return _comm_call(body, [s], [SDS((4,) + s.shape, s.dtype)], 4, 0, "gather_small")[0]


SMALL_SHAPES = ((1, 2, 32, 64), (1, 2, 32, 64), (1, 2, 32), (1, 2, 32, 16, 64),
                (1, 2, 32, 16, 64), (1, 2, 32, 16, 64), (1, 2, 32, 16, 64), (1, D_S5), (1, D_S5), (1, 2, 4), (1, 2, 4),
                (1, GDN_HEAD), (1, D_MODEL), (1, D_MODEL), (LANES,))
SMALL_SWAPPED = (3, 4)


def _size(shape):
    return functools.reduce(lambda p, q: p * q, shape)


SMALL_ROWS = tuple(-(-_size(s) // LANES) for s in SMALL_SHAPES)
SMALL_TOTAL = 2176
SMALL_QUARTER = SMALL_TOTAL // 4


def _rows(a):
    flat = a.reshape(-1)
    pad = (-flat.shape[0]) % LANES
    if pad:
        flat = jnp.concatenate([flat, jnp.zeros((pad,), flat.dtype)])
    return flat.reshape(-1, LANES)


def _pack_small(parts):
    rows = [_rows(p) for p in parts]
    rows.append(jnp.zeros((SMALL_TOTAL - sum(SMALL_ROWS), LANES), f32))
    return jnp.concatenate(rows, axis=0)


def _unpack_small(buf):
    out, r = [], 0
    for s, n in zip(SMALL_SHAPES, SMALL_ROWS):
        out.append(buf[r:r + n].reshape(-1)[:_size(s)].reshape(s))
        r += n
    return out


def _as_2d(a):
    return a.reshape(1, -1) if a.ndim == 1 else a.reshape(-1, a.shape[-1])


S5_BG = S5_GROUPS // S5_BLOCKS


def _block_diag_in(bb):
    lead = bb.shape[:-2]
    eye = jnp.eye(S5_BG, dtype=bb.dtype)
    b4 = bb.reshape(lead + (S5_BLOCKS, S5_BG, S5_GROUP, S5_STATE))
    return jnp.einsum('...jgcp,gh->...jgchp', b4, eye).reshape(lead + (S5_BLOCKS, S5_BC, S5_BS))


def _block_diag_in_t(d):
    lead = d.shape[:-3]
    d6 = d.reshape(lead + (S5_BLOCKS, S5_BG, S5_GROUP, S5_BG, S5_STATE))
    return jnp.einsum('...jgcgp->...jgcp', d6).reshape(lead + (S5_GROUPS, S5_GROUP * S5_STATE))


def _block_diag_out(cm):
    lead = cm.shape[:-3]
    eye = jnp.eye(S5_BG, dtype=cm.dtype)
    c4 = cm.reshape(lead + (S5_BLOCKS, S5_BG, S5_GROUP, S5_STATE))
    return jnp.einsum('...jgcp,gh->...jhpgc', c4, eye).reshape(lead + (S5_BLOCKS, S5_BS, S5_BC))


def _block_diag_out_t(d):
    lead = d.shape[:-3]
    d6 = d.reshape(lead + (S5_BLOCKS, S5_BG, S5_STATE, S5_BG, S5_GROUP))
    return jnp.einsum('...jgpgc->...jgcp', d6).reshape(lead + (S5_GROUPS, S5_GROUP, S5_STATE))


def _to_chunk_rows(a):
    B, L, W = a.shape
    return a.reshape(B, L // CHUNK, CHUNK, W).transpose(0, 1, 3, 2)


def _from_chunk_rows(a):
    B, nc, W, _ = a.shape
    return a.transpose(0, 1, 3, 2).reshape(B, nc * CHUNK, W)


def local_step(x, ctx, tgt, m, w_in, lam_re, lam_im, log_dt, b_re, b_im, c_re, c_im, s5_d,
               w_glu, b_glu, conv16, a_log, dt_bias, norm_w, w_out, ln_g, ln_b):
    B, L, _ = x.shape
    zeros_state = jnp.zeros((B, GDN_HEADS, GDN_HEAD, GDN_HEAD), f32)

    shift, scale, gate = m[:B, :D_MODEL], m[:B, D_MODEL:2 * D_MODEL], m[:B, 2 * D_MODEL:]
    mod = jnp.stack([scale, shift], axis=1)
    mod_c = jnp.broadcast_to(jnp.stack([m[B, D_MODEL:2 * D_MODEL], m[B, :D_MODEL]], axis=0)[None], (B, 2, D_MODEL))

    u, z_s5, qkv, z_gdn, ba = in_proj_fwd(x, mod, w_in, name="in_proj_fwd")
    uc, _, qkvc, _, bac = in_proj_fwd(ctx, mod_c, w_in, name="in_proj_fwd_ctx")

    ng = N_DIR * S5_GROUPS
    zoh_in = (lam_re.reshape(ng, S5_STATE), lam_im.reshape(ng, S5_STATE), log_dt.reshape(ng, 1),
              b_re.reshape(ng, S5_GROUP * S5_STATE), b_im.reshape(ng, S5_GROUP * S5_STATE))
    expand = (jnp.arange(S5_GROUP * S5_STATE)[None, :] % S5_STATE == jnp.arange(S5_STATE)[:, None]).astype(f32)
    ar, ai, bbr, bbi = s5_zoh_fwd(*zoh_in, expand)
    b_blocks = _block_diag_in(jnp.stack([bbr, bbi]).astype(bf16).reshape(2, N_DIR, S5_GROUPS, S5_GROUP * S5_STATE))
    c_blocks = _block_diag_out(jnp.stack([c_re, -c_im]).astype(bf16).reshape(2, N_DIR, S5_GROUPS, S5_GROUP, S5_STATE))
    a_rows = jnp.stack([ar, ai]).reshape(2, N_DIR, S5_HALF)
    s5w, ys, hins, hins_c, hss, hss_c = [], [], [], [], [], []
    for d in range(N_DIR):
        wd = (b_blocks[0, d], b_blocks[1, d], c_blocks[0, d], c_blocks[1, d], a_rows[:, d])
        s5w.append(wd)
        hs_c, hin_c, hend_c = s5_scan_fwd(uc, *wd, jnp.zeros((B, 2, S5_HALF), f32), d=d, need_y=False,
                                          name=f"s5_fwd_ctx{d}")
        y_d, hs_d, hin, _ = s5_scan_fwd(u, *wd, hend_c, d=d, need_y=True, name=f"s5_fwd{d}")
        hss.append(hs_d)
        hss_c.append(hs_c)
        ys.append(y_d)
        hins.append(hin)
        hins_c.append(hin_c)
    glu_w = (s5_d.reshape(1, D_S5), w_glu, b_glu.reshape(1, D_S5))

    act, pre = conv_fwd(qkv, conv16, is_ctx=False, name="conv_fwd")
    act_c, pre_c = conv_fwd(qkvc, conv16, is_ctx=True, name="conv_fwd_ctx")
    pad8 = jnp.zeros((1, 8), f32)
    alog16 = jnp.concatenate([pad8, a_log.reshape(1, 8)], axis=1)
    dtb16 = jnp.concatenate([pad8, dt_bias.reshape(1, 8)], axis=1)
    bg = gates_fwd(ba, alog16, dtb16, name="gates_fwd")
    bg_c = gates_fwd(bac, alog16, dtb16, name="gates_fwd_ctx")
    bgr, bgr_c = _to_chunk_rows(bg), _to_chunk_rows(bg_c)
    cks_c, ns_c, s_c = gdn_fwd(act_c, bg_c, bgr_c, (zeros_state, zeros_state), need_o=False, name="gdn_fwd_ctx")
    os_, cks, ns, _ = gdn_fwd(act, bg, bgr, s_c, need_o=True, name="gdn_fwd")
    nw = norm_w.reshape(1, GDN_HEAD)

    (loss8, du_skip, dy, dz_s5, do, dz_gdn, gx_res, dws, dwg, dgate, dlng, dlnb, d_s5_d, d_w_glu, d_b_glu,
     d_norm_w) = tail_fwd_bwd(u, ys[0], ys[1], z_s5, os_[0], os_[1], z_gdn, x, tgt, gate[:, None, :],
                              ln_g.reshape(1, D_MODEL), ln_b.reshape(1, D_MODEL), w_out[:D_S5], w_out[D_S5:], *glu_w, nw)
    loss = jnp.sum(loss8[:, 0, 0])
    d_w_out = jnp.concatenate([dws, dwg], axis=0)

    dacts, dbgs, dbgrs, ds0s = gdn_bwd(act, bg, bgr, cks, ns, do, (zeros_state, zeros_state), name="gdn_bwd")
    dacts_c, dbgs_c, dbgrs_c, _ = gdn_bwd(act_c, bg_c, bgr_c, cks_c, ns_c, None, ds0s, name="gdn_bwd_ctx")
    dbg = dbgs[0] + dbgs[1] + _from_chunk_rows(dbgrs[0] + dbgrs[1])
    dbg_c = dbgs_c[0] + dbgs_c[1] + _from_chunk_rows(dbgrs_c[0] + dbgrs_c[1])
    dba, dal, ddt = gates_bwd(ba, alog16, dtb16, dbg, name="gates_bwd")
    dbac, dal_c, ddt_c = gates_bwd(bac, alog16, dtb16, dbg_c, name="gates_bwd_ctx")
    d_a_log = (dal + dal_c)[:, 8:].reshape(1, N_DIR, GDN_HEADS)
    d_dt_bias = (ddt + ddt_c)[:, 8:].reshape(1, N_DIR, GDN_HEADS)
    dqkv, dcw = conv_bwd(qkv, pre, conv16, dacts[0], dacts[1], is_ctx=False, name="conv_bwd")
    dqkvc, dcw_c = conv_bwd(qkvc, pre_c, conv16, dacts_c[0], dacts_c[1], is_ctx=True, name="conv_bwd_ctx")
    d_conv16 = jnp.sum(dcw, axis=0) + jnp.sum(dcw_c, axis=0)

    dus, ducs = [du_skip], []
    das, dbs, dcs = [], [], []
    for d in range(N_DIR):
        du_d, dbre1, dbim1, dct1, dcb1, da1, dh0 = s5_scan_bwd(u, dy, hss[d], *s5w[d], hins[d],
                                                                jnp.zeros((B, 2, S5_HALF), f32), d=d, name=f"s5_bwd{d}")
        duc_d, dbre2, dbim2, _, _, da2, _ = s5_scan_bwd(uc, None, hss_c[d], *s5w[d], hins_c[d], dh0, d=d,
                                                        name=f"s5_bwd_ctx{d}")
        dus.append(du_d)
        ducs.append(duc_d)
        das.append(da1 + da2)
        dbs.append(jnp.stack([dbre1 + dbre2, dbim1 + dbim2]))
        dcs.append(jnp.stack([dct1, dcb1]))
    ng_shape = (N_DIR * S5_GROUPS, -1)
    da = jnp.stack(das, axis=1)
    db = _block_diag_in_t(jnp.stack(dbs, axis=1))
    dc = _block_diag_out_t(jnp.stack(dcs, axis=1))
    dlr, dli, dldt, dbre, dbim = s5_zoh_bwd(*zoh_in, expand, da[0].reshape(ng_shape), da[1].reshape(ng_shape),
                                            db[0].reshape(ng_shape), db[1].reshape(ng_shape))
    d_s5 = (dlr, dli, dldt, dbre, dbim, dc[0], -dc[1])

    zc = jnp.zeros_like(uc)
    dw_c, dmod_c = in_proj_bwd(ctx, mod_c, (tuple(ducs), zc, dqkvc, zc, dbac), w_in, None, None,
                               name="in_proj_bwd_ctx")
    d_w_in, dmod, grad_x = in_proj_bwd(x, mod, (tuple(dus), dz_s5, dqkv, dz_gdn, dba), w_in, gx_res, dw_c,
                                       name="in_proj_bwd")
    dmod_c = jnp.sum(dmod_c, axis=0)

    dm_rows = jnp.concatenate([dmod[:, 1], dmod[:, 0], dgate[:, 0]], axis=1)
    dm_ctx = jnp.concatenate([dmod_c[1], dmod_c[0], jnp.zeros((D_MODEL,), f32)])[None]
    dm = jnp.concatenate([dm_rows, dm_ctx], axis=0)
    small = (*d_s5, d_s5_d, d_b_glu, d_a_log, d_dt_bias, d_norm_w, dlng, dlnb)
    small = tuple(g.reshape(s) for g, s in zip(small, SMALL_SHAPES))
    return loss, grad_x, (d_w_in, d_w_out, d_w_glu, d_conv16), small, dm


SHARDED = (1, 3, 18, 12, 14)
REDUCED = (3, 18, 12, 14)
UNSHARDED = tuple(i for i in range(21) if i not in SHARDED)
SMALL = tuple(i for i in UNSHARDED if i not in (0, 2))
W_IN_SHARD = 772


def _conv_rows(w):
    return jnp.concatenate([w.reshape(9, w.shape[-1]), jnp.zeros((CONV_ROWS - 9, w.shape[-1]), f32)], axis=0)


def kernel(x, c, ctx, c_ctx, w_ada, b_ada, w_in, s5_lambda_re, s5_lambda_im, s5_log_dt, s5_b_re, s5_b_im, s5_c_re, s5_c_im, s5_d, w_glu, b_glu, conv_w, gdn_a_log, gdn_dt_bias, gdn_norm_w, w_out, ln_g, ln_b, loss_target, m_c_ctx, m_w_ada, m_b_ada, m_w_in, m_s5_lambda_re, m_s5_lambda_im, m_s5_log_dt, m_s5_b_re, m_s5_b_im, m_s5_c_re, m_s5_c_im, m_s5_d, m_w_glu, m_b_glu, m_conv_w, m_gdn_a_log, m_gdn_dt_bias, m_gdn_norm_w, m_w_out, m_ln_g, m_ln_b, v_c_ctx, v_w_ada, v_b_ada, v_w_in, v_s5_lambda_re, v_s5_lambda_im, v_s5_log_dt, v_s5_b_re, v_s5_b_im, v_s5_c_re, v_s5_c_im, v_s5_d, v_w_glu, v_b_glu, v_conv_w, v_gdn_a_log, v_gdn_dt_bias, v_gdn_norm_w, v_w_out, v_ln_g, v_ln_b):
    weights = [c_ctx, w_ada, b_ada, w_in, s5_lambda_re, s5_lambda_im, s5_log_dt, s5_b_re, s5_b_im, s5_c_re, s5_c_im,
               s5_d, w_glu, b_glu, conv_w, gdn_a_log, gdn_dt_bias, gdn_norm_w, w_out, ln_g, ln_b]
    ms = [m_c_ctx, m_w_ada, m_b_ada, m_w_in, m_s5_lambda_re, m_s5_lambda_im, m_s5_log_dt, m_s5_b_re, m_s5_b_im,
          m_s5_c_re, m_s5_c_im, m_s5_d, m_w_glu, m_b_glu, m_conv_w, m_gdn_a_log, m_gdn_dt_bias, m_gdn_norm_w, m_w_out,
          m_ln_g, m_ln_b]
    vs = [v_c_ctx, v_w_ada, v_b_ada, v_w_in, v_s5_lambda_re, v_s5_lambda_im, v_s5_log_dt, v_s5_b_re, v_s5_b_im,
          v_s5_c_re, v_s5_c_im, v_s5_d, v_w_glu, v_b_glu, v_conv_w, v_gdn_a_log, v_gdn_dt_bias, v_gdn_norm_w, v_w_out,
          v_ln_g, v_ln_b]
    cpos = lax.axis_index("c")
    jchip = 2 * lax.axis_index("x") + lax.axis_index("y")

    c_all = gather_devices(c, name="gather_c")
    cc = jnp.concatenate([c_all, jnp.broadcast_to(c_ctx[None, None, :], (N_DEV, 1, D_MODEL)),
                          jnp.zeros((N_DEV, 5, D_MODEL), f32)], axis=1)
    w_ada16 = w_ada[0].astype(bf16)
    b_cols = lax.dynamic_slice_in_dim(b_ada, jchip * ADA_SHARD, ADA_SHARD, axis=1)
    m_mine = exchange_devices(ada_fwd(cc, w_ada16, b_cols), name="exchange_m")
    m_rows = jnp.concatenate([m_mine[2 * j, :3] for j in range(4)], axis=1)

    conv_shard = _conv_rows(conv_w)
    g_in, g_out, g_glu, g_conv = gather_shards(
        [jnp.transpose(w_in[0]).astype(bf16), w_out[0].astype(bf16), w_glu[0].astype(bf16), conv_shard])
    w_in_t = g_in.reshape(P_IN, D_MODEL)
    conv16 = g_conv.transpose(1, 0, 2).reshape(CONV_ROWS, 3 * D_GDN)

    swap = lambda a: jnp.swapaxes(a, 3, 4)
    loss, grad_x, big, small, dm_rows = local_step(
        x, ctx, loss_target, m_rows, w_in_t, s5_lambda_re, s5_lambda_im, s5_log_dt, swap(s5_b_re), swap(s5_b_im),
        s5_c_re, s5_c_im, s5_d, g_glu.reshape(D_S5, D_S5), b_glu, conv16, gdn_a_log, gdn_dt_bias, gdn_norm_w,
        g_out.reshape(D_MODEL, D_MODEL), ln_g, ln_b)
    me = 2 * jchip + cpos
    loss_hi = loss.astype(bf16).astype(f32)
    loss_row = jnp.zeros((LANES,), f32).at[me].set(loss_hi).at[N_DEV + me].set(loss - loss_hi)

    dm8 = jnp.concatenate([dm_rows, jnp.zeros((5, 3 * D_MODEL), f32)], axis=0)
    dm_by_chip = dm8.reshape(8, 4, ADA_SHARD).transpose(1, 0, 2)
    dm_cols = exchange_devices(jnp.repeat(dm_by_chip, 2, axis=0), name="exchange_dm")
    g_w_ada, pb = ada_bwd(cc, w_ada16, dm_cols)
    pb_all = gather_devices(pb, name="gather_p")
    g_c_ctx = c_ctx_bwd(pb_all, c_ctx[None, :])[0]
    g_b_ada = jnp.concatenate([pb_all[2 * j, 1:2, :ADA_SHARD] for j in range(4)], axis=1)

    d_w_in, d_w_out, d_w_glu, d_conv16 = big
    slabs = [d_w_in.reshape(4, W_IN_SHARD, D_MODEL),
             d_w_out.reshape(4, D_MODEL // 4, D_MODEL),
             d_w_glu.reshape(4, D_S5 // 4, D_S5),
             d_conv16.reshape(CONV_ROWS, 4, 3 * D_GDN // 4).transpose(1, 0, 2),
             _pack_small(small + (loss_row,)).reshape(4, SMALL_QUARTER, LANES)]
    got = swap_halves(slabs)
    q32, q16 = [], []
    for t, (s, g) in enumerate(zip(slabs, got)):
        if _by_rows(s.shape[1:]):
            own = lax.dynamic_index_in_dim(s.reshape(4, 2, s.shape[1] // 2, s.shape[2]), cpos, axis=1, keepdims=False)
        else:
            own = lax.dynamic_slice_in_dim(s, cpos * (s.shape[2] // 2), s.shape[2] // 2, axis=2)
        a, b = sum_cores(own, g, name=f"sum_cores{t}")
        q32.append(a)
        q16.append(b)
    rec = scatter_to_chips(q16)
    fs = [sum_chips(lax.dynamic_index_in_dim(q, jchip, axis=0, keepdims=False), r, cpos, s.shape[1:], name=f"sum_chips{t}")
          for t, (q, r, s) in enumerate(zip(q32, rec, slabs))]
    red = join_halves(fs)
    g_small = _unpack_small(gather_small(red[4][0]).reshape(SMALL_TOTAL, LANES))
    loss = jnp.sum(g_small[-1][:2 * N_DEV])
    g_small = g_small[:-1]
    g_shard = {1: g_w_ada, 3: red[0], 18: red[1], 12: red[2], 14: red[3]}

    grads, deltas, new_m, new_v = [None] * 21, [None] * 21, [None] * 21, [None] * 21
    for t, i in enumerate(SHARDED):
        conv, win = i == 14, i == 3
        prep = (lambda a: _conv_rows(a)[None]) if conv else ((lambda a: jnp.transpose(a, (2, 0, 1))) if win else (lambda a: a))
        g = jnp.transpose(g_shard[i], (1, 0, 2)) if win else g_shard[i]
        d, nm, nv = adamw_3d(prep(weights[i]), g, prep(ms[i]), prep(vs[i]), lead=win, name=f"adamw{t}")
        for lst, val in ((grads, g), (deltas, d), (new_m, nm), (new_v, nv)):
            lst[i] = (val[0, :9].reshape(weights[i].shape) if conv else (jnp.transpose(val, (1, 2, 0)) if win else val))
    g_un = {0: g_c_ctx, 2: g_b_ada, **{i: g_small[n] for n, i in enumerate(SMALL)}}
    swapped = [SMALL[n] for n in SMALL_SWAPPED]
    small_in = lambda lst: [_as_2d(swap(lst[i]) if i in swapped else lst[i]) for i in UNSHARDED]
    sm = adamw_small(small_in(weights), [_as_2d(g_un[i]) for i in UNSHARDED], small_in(ms), small_in(vs))
    for n, i in enumerate(UNSHARDED):
        back = ((lambda a: swap(a.reshape(swap(weights[i]).shape))) if i in swapped
                else (lambda a: a.reshape(weights[i].shape)))
        grads[i] = back(g_un[i])
        for lst, res in ((deltas, sm[0]), (new_m, sm[1]), (new_v, sm[2])):
            lst[i] = back(res[n])
    return (loss, grad_x, *grads, *deltas, *new_m, *new_v)
```

```python
import functools

import jax
import jax.numpy as jnp
from jax import lax
from jax.experimental import pallas as pl
from jax.experimental.pallas import tpu as pltpu

f32 = jnp.float32
bf16 = jnp.bfloat16
SDS = jax.ShapeDtypeStruct

D_MODEL = 1024
D_S5 = 512
S5_GROUP = 16
S5_GROUPS = 32
S5_STATE = 64
S5_HALF = S5_GROUPS * S5_STATE
D_GDN = 512
GDN_HEAD = 128
GDN_HEADS = 4
CHUNK = 64
GRID_W = 64
N_DIR = 2
P_IN = 3088
DEEPNORM_ALPHA = 2.0 ** 0.25
LN_EPS = 1e-5
NORM_EPS = 1e-6
ADAM_LR, ADAM_B1, ADAM_B2, ADAM_EPS, ADAM_WD, ADAM_STEP = 0.001, 0.9, 0.999, 1e-08, 0.01, 10

LANES = 128
VMEM_LIMIT = 56 * 1024 * 1024
TOK_TILE = 256
S5_TILE = 256
MESH = pl.DeviceIdType.MESH


def _cparams(n_grid):
    return pltpu.CompilerParams(dimension_semantics=("arbitrary",) * n_grid, vmem_limit_bytes=VMEM_LIMIT)


def _dot(a, b):
    return jnp.dot(a.astype(bf16), b.astype(bf16), preferred_element_type=f32)


def _dot_nt(a, b):
    return lax.dot_general(a.astype(bf16), b.astype(bf16), (((1,), (1,)), ((), ())), preferred_element_type=f32)


def _dot_tn(a, b):
    return lax.dot_general(a.astype(bf16), b.astype(bf16), (((0,), (0,)), ((), ())), preferred_element_type=f32)


def _dot_hi(a, b):
    return jnp.dot(a, b, precision=lax.Precision.HIGHEST, preferred_element_type=f32)


@jax.custom_vjp
def _mm(a, b):
    return _dot(a, b)


@jax.custom_vjp
def _mm_nt(a, b):
    return _dot_nt(a, b)


@jax.custom_vjp
def _mm_tn(a, b):
    return _dot_tn(a, b)


_mm.defvjp(lambda a, b: (_dot(a, b), (a, b)), lambda r, g: (_mm_nt(g, r[1]), _mm_tn(r[0], g)))
_mm_nt.defvjp(lambda a, b: (_dot_nt(a, b), (a, b)), lambda r, g: (_mm(g, r[1]), _mm_tn(g, r[0])))
_mm_tn.defvjp(lambda a, b: (_dot_tn(a, b), (a, b)), lambda r, g: (_mm_nt(r[1], g), _mm(r[0], g)))


def _silu(x):
    return x * jax.nn.sigmoid(x)


def _gelu(x):
    return 0.5 * x * (1.0 + lax.erf(x * (2.0 ** -0.5)))


def _resident(shape):
    nd = len(shape)
    return pl.BlockSpec(shape, lambda *_: (0,) * nd, pipeline_mode=pl.Buffered(1))


def _tok(tile, width, nt=None, rev=False):
    if rev:
        return pl.BlockSpec((None, tile, width), lambda b, n: (b, nt - 1 - n, 0))
    return pl.BlockSpec((None, tile, width), lambda b, n: (b, n, 0))


def _per_batch(rows, width):
    return pl.BlockSpec((None, rows, width), lambda b, n: (b, 0, 0))


def _first_step():
    return jnp.logical_and(pl.program_id(0) == 0, pl.program_id(1) == 0)


ADA_SHARD = 3 * D_MODEL // 4
N_DEV = 8


def ada_fwd(cc, w, b):
    def body(cc_ref, w_ref, b_ref, m_ref):
        for k in range(N_DEV):
            m_ref[k] = _dot(_silu(cc_ref[k]), w_ref[...]) + b_ref[...]

    return pl.pallas_call(body, name="ada_fwd", out_shape=SDS((N_DEV, 8, ADA_SHARD), f32),
                          compiler_params=pltpu.CompilerParams(vmem_limit_bytes=VMEM_LIMIT))(cc, w, b)


def ada_bwd(cc, w, dmj):
    def body(cc_ref, w_ref, dmj_ref, dw_ref, pb_ref):
        dw = jnp.zeros((D_MODEL, ADA_SHARD), f32)
        p = jnp.zeros((8, D_MODEL), f32)
        db = jnp.zeros((1, ADA_SHARD), f32)
        for k in range(N_DEV):
            dw = dw + _dot_tn(_silu(cc_ref[k]), dmj_ref[k])
            p = p + _dot_nt(dmj_ref[k], w_ref[...])
            db = db + jnp.sum(dmj_ref[k], axis=0, keepdims=True)
        dw_ref[0] = dw
        pb_ref[...] = jnp.zeros_like(pb_ref)
        pb_ref[0:1, :] = p[2:3, :]
        pb_ref[1:2, 0:ADA_SHARD] = db

    return pl.pallas_call(
        body, name="ada_bwd", out_shape=[SDS((1, D_MODEL, ADA_SHARD), f32), SDS((8, D_MODEL), f32)],
        compiler_params=pltpu.CompilerParams(vmem_limit_bytes=VMEM_LIMIT))(cc, w, dmj)


def c_ctx_bwd(pb_all, c_ctx):
    def body(p_ref, c_ref, d_ref):
        ds = ((p_ref[0, 0:1, :] + p_ref[2, 0:1, :]) + p_ref[4, 0:1, :]) + p_ref[6, 0:1, :]
        _, vjp = jax.vjp(_silu, c_ref[...])
        d_ref[...] = vjp(ds)[0]

    return pl.pallas_call(body, name="c_ctx_bwd", out_shape=SDS((1, D_MODEL), f32))(pb_all, c_ctx)


N_GATE = 2 * N_DIR * GDN_HEADS
IN_WIDTHS = (D_S5, D_S5, 3 * D_GDN, D_GDN, N_GATE)
IN_OFFS = (0, 512, 1024, 2560, 3072)


def in_proj_fwd(x, mod, wt, *, name):
    B, L, _ = x.shape
    T = min(TOK_TILE, L)

    def body(x_ref, mod_ref, w_ref, *o_refs):
        h = (x_ref[...] * (1.0 + mod_ref[0:1, :]) + mod_ref[1:2, :]).astype(bf16)
        for o_ref, off, wd in zip(o_refs, IN_OFFS, IN_WIDTHS):
            o_ref[...] = _dot_nt(h, w_ref[off:off + wd, :])

    return pl.pallas_call(
        body, name=name, grid=(B, L // T),
        in_specs=[_tok(T, D_MODEL), _per_batch(2, D_MODEL), _resident((P_IN, D_MODEL))],
        out_specs=[_tok(T, wd) for wd in IN_WIDTHS],
        out_shape=[SDS((B, L, wd), f32) for wd in IN_WIDTHS],
        compiler_params=_cparams(2),
    )(x, mod, wt)


def in_proj_bwd(x, mod, ds, wt, gx_res, dw_start, *, name):
    B, L, _ = x.shape
    T = min(TOK_TILE, L)
    with_dx = gx_res is not None
    with_start = dw_start is not None
    n_u = len(ds[0])

    def body(*refs):
        x_ref, mod_ref = refs[0], refs[1]
        du_refs = refs[2:2 + n_u]
        d_refs = refs[2 + n_u:6 + n_u]
        w_ref = refs[6 + n_u]
        k = 7 + n_u
        if with_dx:
            gx_ref = refs[k]
            k += 1
        if with_start:
            start_ref = refs[k]
            k += 1
        dw_ref, dmod_ref = refs[k], refs[k + 1]
        if with_dx:
            dx_ref = refs[k + 2]
        n = pl.program_id(1)

        @pl.when(_first_step())
        def _():
            dw_ref[...] = start_ref[...] if with_start else jnp.zeros_like(dw_ref)

        @pl.when(n == 0)
        def _():
            dmod_ref[...] = jnp.zeros_like(dmod_ref)

        xv = x_ref[...]
        scale1 = 1.0 + mod_ref[0:1, :]
        h = (xv * scale1 + mod_ref[1:2, :]).astype(bf16)
        du = du_refs[0][...]
        for r in du_refs[1:]:
            du = du + r[...]
        dh = jnp.zeros((T, D_MODEL), f32)
        for dv, off, wd in zip([du] + [r[...] for r in d_refs], IN_OFFS, IN_WIDTHS):
            dv = dv.astype(bf16)
            dh = dh + _dot(dv, w_ref[off:off + wd, :])
            dw_ref[off:off + wd, :] += _dot_tn(dv, h)
        dmod_ref[0:1, :] += jnp.sum(dh * xv, axis=0, keepdims=True)
        dmod_ref[1:2, :] += jnp.sum(dh, axis=0, keepdims=True)
        if with_dx:
            dx_ref[...] = gx_ref[...] + dh * scale1

    in_specs = ([_tok(T, D_MODEL), _per_batch(2, D_MODEL)] + [_tok(T, D_S5)] * n_u + [_tok(T, wd) for wd in IN_WIDTHS[1:]]
                + [_resident((P_IN, D_MODEL))])
    args = [x, mod, *ds[0], *ds[1:], wt]
    out_specs = [_resident((P_IN, D_MODEL)), _per_batch(2, D_MODEL)]
    out_shape = [SDS((P_IN, D_MODEL), f32), SDS((B, 2, D_MODEL), f32)]
    if with_dx:
        in_specs.append(_tok(T, D_MODEL))
        args.append(gx_res)
        out_specs.append(_tok(T, D_MODEL))
        out_shape.append(SDS((B, L, D_MODEL), f32))
    if with_start:
        in_specs.append(_resident((P_IN, D_MODEL)))
        args.append(dw_start)
    return pl.pallas_call(body, name=name, grid=(B, L // T), in_specs=in_specs, out_specs=out_specs,
                          out_shape=out_shape, compiler_params=_cparams(2))(*args)


def _s5_zoh(lr, li, ldt, bre, bim, expand):
    dt = jnp.exp(ldt)
    zr, zi = lr * dt, li * dt
    e = jnp.exp(zr)
    ar, ai = e * jnp.cos(zi), e * jnp.sin(zi)
    den = lr * lr + li * li
    czr = ((ar - 1.0) * lr + ai * li) / den
    czi = (ai * lr - (ar - 1.0) * li) / den
    czr_e, czi_e = _dot_hi(czr, expand), _dot_hi(czi, expand)
    return ar, ai, czr_e * bre - czi_e * bim, czr_e * bim + czi_e * bre


_ZOH_OUT = [(N_DIR * S5_GROUPS, S5_STATE)] * 2 + [(N_DIR * S5_GROUPS, S5_STATE * S5_GROUP)] * 2


def s5_zoh_fwd(lr, li, ldt, bre, bim, expand):
    def body(lr_ref, li_ref, ldt_ref, bre_ref, bim_ref, e_ref, ar_ref, ai_ref, bbr_ref, bbi_ref):
        ar, ai, bbr, bbi = _s5_zoh(lr_ref[...], li_ref[...], ldt_ref[...], bre_ref[...], bim_ref[...], e_ref[...])
        ar_ref[...], ai_ref[...], bbr_ref[...], bbi_ref[...] = ar, ai, bbr, bbi

    return pl.pallas_call(body, name="s5_zoh_fwd", out_shape=[SDS(s, f32) for s in _ZOH_OUT])(
        lr, li, ldt, bre, bim, expand)


def s5_zoh_bwd(lr, li, ldt, bre, bim, expand, dar, dai, dbbr, dbbi):
    def body(lr_ref, li_ref, ldt_ref, bre_ref, bim_ref, e_ref, dar_ref, dai_ref, dbbr_ref, dbbi_ref,
             dlr_ref, dli_ref, dldt_ref, dbre_ref, dbim_ref):
        ev = e_ref[...]
        _, vjp = jax.vjp(lambda a, b, c, d, e: _s5_zoh(a, b, c, d, e, ev),
                         lr_ref[...], li_ref[...], ldt_ref[...], bre_ref[...], bim_ref[...])
        outs = vjp((dar_ref[...], dai_ref[...], dbbr_ref[...], dbbi_ref[...]))
        dlr_ref[...], dli_ref[...], dldt_ref[...], dbre_ref[...], dbim_ref[...] = outs

    shapes = [lr.shape, li.shape, ldt.shape, bre.shape, bim.shape]
    return pl.pallas_call(body, name="s5_zoh_bwd", out_shape=[SDS(s, f32) for s in shapes])(
        lr, li, ldt, bre, bim, expand, dar, dai, dbbr, dbbi)


def _scan_rows(T, rev, ar, ai, h0s, refs, off):
    def step(i, carry):
        t = off + ((T - 1 - i) if rev else i)
        out = []
        for (hr, hi), (r_ref, i_ref) in zip(carry, refs):
            nr = ar * hr - ai * hi + r_ref[pl.ds(t, 1), :]
            ni = ar * hi + ai * hr + i_ref[pl.ds(t, 1), :]
            r_ref[pl.ds(t, 1), :] = nr
            i_ref[pl.ds(t, 1), :] = ni
            out.append((nr, ni))
        return tuple(out)

    return lax.fori_loop(0, T, step, tuple(h0s))


S5_BLOCKS = 4
S5_BC = D_S5 // S5_BLOCKS
S5_BS = S5_HALF // S5_BLOCKS


def _s5_in(uv, bre_ref, bim_ref, hr_ref, hi_ref, off, T):
    for jb in range(S5_BLOCKS):
        uj = uv[:, jb * S5_BC:(jb + 1) * S5_BC]
        hr_ref[off:off + T, jb * S5_BS:(jb + 1) * S5_BS] = _dot(uj, bre_ref[jb])
        hi_ref[off:off + T, jb * S5_BS:(jb + 1) * S5_BS] = _dot(uj, bim_ref[jb])


def _s5_specs(B, T, nt, rev):
    tidx = (lambda n: nt - 1 - n) if rev else (lambda n: n)
    tok = pl.BlockSpec((B, T, D_S5), lambda n: (0, tidx(n), 0))
    hin = pl.BlockSpec((B, None, 2, S5_HALF), lambda n: (0, tidx(n), 0, 0))
    state = pl.BlockSpec((B, 2, S5_HALF), lambda n: (0, 0, 0))
    return tok, hin, state


def s5_scan_fwd(u, bre, bim, ctop, cbot, arow, h0, *, d, need_y, name):
    B, L, _ = u.shape
    T = min(S5_TILE, L)
    nt = L // T
    rev = d == 1

    def body(u_ref, bre_ref, bim_ref, ct_ref, cb_ref, a_ref, h0_ref, *rest):
        if need_y:
            y_ref, hs_ref, hin_ref, hend_ref, hr_scr, hi_scr, h_scr = rest
        else:
            hs_ref, hin_ref, hend_ref, hr_scr, hi_scr, h_scr = rest
        n = pl.program_id(0)

        @pl.when(n == 0)
        def _():
            h_scr[...] = h0_ref[...]

        hin_ref[...] = h_scr[...]
        for b in range(B):
            _s5_in(u_ref[b].astype(bf16), bre_ref, bim_ref, hr_scr.at[b], hi_scr.at[b], 0, T)
        hs = _scan_rows(T, rev, a_ref[0:1, :], a_ref[1:2, :], [(h_scr[b, 0:1, :], h_scr[b, 1:2, :]) for b in range(B)],
                        [(hr_scr.at[b], hi_scr.at[b]) for b in range(B)], 0)
        for b in range(B):
            h_scr[b, 0:1, :] = hs[b][0]
            h_scr[b, 1:2, :] = hs[b][1]
            hs_ref[b, :, 0:S5_HALF] = hr_scr[b].astype(bf16)
            hs_ref[b, :, S5_HALF:2 * S5_HALF] = hi_scr[b].astype(bf16)
            if need_y:
                for jb in range(S5_BLOCKS):
                    st = slice(jb * S5_BS, (jb + 1) * S5_BS)
                    y_ref[b, :, jb * S5_BC:(jb + 1) * S5_BC] = (_dot(hr_scr[b, :, st], ct_ref[jb])
                                                                 + _dot(hi_scr[b, :, st], cb_ref[jb]))

        @pl.when(n == nt - 1)
        def _():
            hend_ref[...] = h_scr[...]

    tok, hin_spec, state = _s5_specs(B, T, nt, rev)
    hs_spec = pl.BlockSpec((B, T, 2 * S5_HALF), tok.index_map)
    out_specs = [hs_spec, hin_spec, state]
    out_shape = [SDS((B, L, 2 * S5_HALF), bf16), SDS((B, nt, 2, S5_HALF), f32), SDS((B, 2, S5_HALF), f32)]
    if need_y:
        out_specs.insert(0, tok)
        out_shape.insert(0, SDS((B, L, D_S5), f32))
    w_in, w_out = _resident((S5_BLOCKS, S5_BC, S5_BS)), _resident((S5_BLOCKS, S5_BS, S5_BC))
    return pl.pallas_call(
        body, name=name, grid=(nt,),
        in_specs=[tok, w_in, w_in, w_out, w_out, _resident((2, S5_HALF)), state],
        out_specs=out_specs, out_shape=out_shape,
        scratch_shapes=[pltpu.VMEM((B, T, S5_HALF), f32), pltpu.VMEM((B, T, S5_HALF), f32),
                        pltpu.VMEM((B, 2, S5_HALF), f32)],
        compiler_params=_cparams(1),
    )(u, bre, bim, ctop, cbot, arow, h0)


def s5_scan_bwd(u, dy, hs, bre, bim, ctop, cbot, arow, hin, dhend, *, d, name):
    B, L, _ = u.shape
    T = min(S5_TILE, L)
    nt = L // T
    rev = d == 1
    has_dy = dy is not None
    PAD = 8

    def body(*refs):
        u_ref = refs[0]
        k = 1
        if has_dy:
            dy_ref = refs[1]
            k = 2
        hs_ref = refs[k]
        k += 1
        bre_ref, bim_ref, ct_ref, cb_ref, a_ref, hin_ref, dhend_ref = refs[k:k + 7]
        du_ref, dbre_ref, dbim_ref, dct_ref, dcb_ref, da_ref, dh0_ref = refs[k + 7:k + 14]
        hr_scr, hi_scr, gr_scr, gi_scr, p_scr = refs[k + 14:]
        n = pl.program_id(0)

        @pl.when(n == 0)
        def _():
            for r in (dbre_ref, dbim_ref, dct_ref, dcb_ref, da_ref):
                r[...] = jnp.zeros_like(r)
            p_scr[...] = dhend_ref[...]

        ar, ai = a_ref[0:1, :], a_ref[1:2, :]
        prev_row = PAD + T if rev else PAD - 1
        uvs = []
        for b in range(B):
            uvs.append(u_ref[b].astype(bf16))
            hr_scr[b, PAD:PAD + T, :] = hs_ref[b, :, 0:S5_HALF].astype(f32)
            hi_scr[b, PAD:PAD + T, :] = hs_ref[b, :, S5_HALF:2 * S5_HALF].astype(f32)
            hr_scr[b, prev_row:prev_row + 1, :] = hin_ref[b, 0:1, :]
            hi_scr[b, prev_row:prev_row + 1, :] = hin_ref[b, 1:2, :]
        if has_dy:
            for b in range(B):
                dyv = dy_ref[b].astype(bf16)
                for jb in range(S5_BLOCKS):
                    st = slice(jb * S5_BS, (jb + 1) * S5_BS)
                    dyj = dyv[:, jb * S5_BC:(jb + 1) * S5_BC]
                    gr_scr[b, :, st] = _dot_nt(dyj, ct_ref[jb])
                    gi_scr[b, :, st] = _dot_nt(dyj, cb_ref[jb])
                    dct_ref[jb] += _dot_tn(hr_scr[b, PAD:PAD + T, st], dyj)
                    dcb_ref[jb] += _dot_tn(hi_scr[b, PAD:PAD + T, st], dyj)
        else:
            gr_scr[...] = jnp.zeros_like(gr_scr)
            gi_scr[...] = jnp.zeros_like(gi_scr)

        def step(i, carry):
            t = i if rev else T - 1 - i
            tp = PAD + t + (1 if rev else -1)
            out = []
            for b, (pr, pi, dar, dai) in enumerate(carry):
                gr = gr_scr[b, pl.ds(t, 1), :] + pr
                gi = gi_scr[b, pl.ds(t, 1), :] + pi
                gr_scr[b, pl.ds(t, 1), :] = gr
                gi_scr[b, pl.ds(t, 1), :] = gi
                hpr = hr_scr[b, pl.ds(tp, 1), :]
                hpi = hi_scr[b, pl.ds(tp, 1), :]
                out.append((ar * gr + ai * gi, ar * gi - ai * gr, dar + hpr * gr + hpi * gi, dai + hpr * gi - hpi * gr))
            return tuple(out)

        zero = jnp.zeros((1, S5_HALF), f32)
        res = lax.fori_loop(0, T, step, tuple((p_scr[b, 0:1, :], p_scr[b, 1:2, :], zero, zero) for b in range(B)))
        for b in range(B):
            pr, pi, dar, dai = res[b]
            p_scr[b, 0:1, :] = pr
            p_scr[b, 1:2, :] = pi
            da_ref[0:1, :] += dar
            da_ref[1:2, :] += dai
            for jb in range(S5_BLOCKS):
                st = slice(jb * S5_BS, (jb + 1) * S5_BS)
                ch = slice(jb * S5_BC, (jb + 1) * S5_BC)
                gr_j = gr_scr[b, :, st].astype(bf16)
                gi_j = gi_scr[b, :, st].astype(bf16)
                du_ref[b, :, ch] = _dot_nt(gr_j, bre_ref[jb]) + _dot_nt(gi_j, bim_ref[jb])
                dbre_ref[jb] += _dot_tn(uvs[b][:, ch], gr_j)
                dbim_ref[jb] += _dot_tn(uvs[b][:, ch], gi_j)

        @pl.when(n == nt - 1)
        def _():
            dh0_ref[...] = p_scr[...]

    tok, hin_spec, state = _s5_specs(B, T, nt, not rev)
    hs_spec = pl.BlockSpec((B, T, 2 * S5_HALF), tok.index_map)
    w_in, w_out = _resident((S5_BLOCKS, S5_BC, S5_BS)), _resident((S5_BLOCKS, S5_BS, S5_BC))
    wspecs = [w_in, w_in, w_out, w_out]
    in_specs = [tok] + ([tok] if has_dy else []) + [hs_spec] + wspecs + [_resident((2, S5_HALF)), hin_spec, state]
    args = [u] + ([dy] if has_dy else []) + [hs, bre, bim, ctop, cbot, arow, hin, dhend]
    return pl.pallas_call(
        body, name=name, grid=(nt,), in_specs=in_specs,
        out_specs=[tok] + wspecs + [_resident((2, S5_HALF)), state],
        out_shape=[SDS((B, L, D_S5), f32), SDS((S5_BLOCKS, S5_BC, S5_BS), f32), SDS((S5_BLOCKS, S5_BC, S5_BS), f32),
                   SDS((S5_BLOCKS, S5_BS, S5_BC), f32), SDS((S5_BLOCKS, S5_BS, S5_BC), f32), SDS((2, S5_HALF), f32),
                   SDS((B, 2, S5_HALF), f32)],
        scratch_shapes=[pltpu.VMEM((B, T + 2 * PAD, S5_HALF), f32), pltpu.VMEM((B, T + 2 * PAD, S5_HALF), f32),
                        pltpu.VMEM((B, T, S5_HALF), f32), pltpu.VMEM((B, T, S5_HALF), f32),
                        pltpu.VMEM((B, 2, S5_HALF), f32)],
        compiler_params=_cparams(1),
    )(*args)


def _glu_fn(u, y0, y1, z, dsk, wg, bg):
    g = _gelu(dsk * u + y0 + y1)
    return g * jax.nn.sigmoid(_mm(g, wg) + bg) * _silu(z)


CONV_ROWS = 16


def _shift(x, s):
    L = x.shape[0]
    k = (-s) % L
    return x if k == 0 else pltpu.roll(x, k, axis=0)


def _r16(v):
    return v.astype(bf16).astype(f32)


def _conv_masks(L, is_ctx):
    t = lax.broadcasted_iota(jnp.int32, (L, 1), 0)
    if is_ctx:
        return t == L - 1, t == 0, None, None
    col = jnp.bitwise_and(t, GRID_W - 1)
    return col == GRID_W - 1, col == 0, t >= GRID_W, t < L - GRID_W


def _conv_sides(xv, masks):
    no_left, no_right, _, _ = masks
    return _shift(jnp.where(no_left, 0.0, xv), -1), _shift(jnp.where(no_right, 0.0, xv), 1)


def _conv_pre(xv, w_ref, masks, is_ctx):
    xv = _r16(xv)
    wv = _r16(w_ref[...])
    xl, xr = _conv_sides(xv, masks)
    z = [wv[3 * di:3 * di + 1, :] * xl + wv[3 * di + 1:3 * di + 2, :] * xv + wv[3 * di + 2:3 * di + 3, :] * xr
         for di in ((1,) if is_ctx else (0, 1, 2))]
    if is_ctx:
        return z[0]
    _, _, has_up, has_down = masks
    return z[1] + jnp.where(has_up, _shift(z[0], -GRID_W), 0.0) + jnp.where(has_down, _shift(z[2], GRID_W), 0.0)


def _conv_pre_bwd(xv, w_ref, dpre, masks, is_ctx, dw_ref):
    no_left, no_right, has_up, has_down = masks
    xv, dpre, wv = _r16(xv), _r16(dpre), _r16(w_ref[...])
    xl, xr = _conv_sides(xv, masks)
    if is_ctx:
        dz = {1: dpre}
    else:
        dz = {0: _shift(jnp.where(has_up, dpre, 0.0), GRID_W), 1: dpre, 2: _shift(jnp.where(has_down, dpre, 0.0), -GRID_W)}
    dxl = dxc = dxr = None
    for di, d in dz.items():
        for dj, side in enumerate((xl, xv, xr)):
            dw_ref[3 * di + dj:3 * di + dj + 1, :] = jnp.sum(d * side, axis=0, keepdims=True)
        tl, tc, tr = (wv[3 * di + dj:3 * di + dj + 1, :] * d for dj in range(3))
        dxl, dxc, dxr = (tl, tc, tr) if dxl is None else (dxl + tl, dxc + tc, dxr + tr)
    return dxc + jnp.where(no_left, 0.0, _shift(dxl, 1)) + jnp.where(no_right, 0.0, _shift(dxr, -1))


def _qk_post(pre, is_norm, scale):
    s = _silu(pre)
    nrm = lax.rsqrt(jnp.sum(s * s, axis=-1, keepdims=True) + NORM_EPS)
    return s * jnp.where(is_norm, nrm * scale, 1.0)


def _conv_kind():
    ct = pl.program_id(1)
    return ct < 2 * GDN_HEADS, jnp.where(ct < GDN_HEADS, GDN_HEAD ** -0.5, 1.0).astype(f32)


def conv_fwd(qkv, w16, *, is_ctx, name):
    B, L, C = qkv.shape
    spec = pl.BlockSpec((None, L, GDN_HEAD), lambda b, ct: (b, 0, ct))
    wspec = pl.BlockSpec((CONV_ROWS, GDN_HEAD), lambda b, ct: (0, ct))

    def body(x_ref, w_ref, o_ref, pre_ref):
        is_norm, scale = _conv_kind()
        pre = _conv_pre(x_ref[...], w_ref, _conv_masks(L, is_ctx), is_ctx)
        pre_ref[...] = pre
        o_ref[...] = _qk_post(pre, is_norm, scale)

    return pl.pallas_call(body, name=name, grid=(B, C // GDN_HEAD), in_specs=[spec, wspec], out_specs=[spec, spec],
                          out_shape=[SDS((B, L, C), f32)] * 2, compiler_params=_cparams(2))(qkv, w16)


def conv_bwd(qkv, pre, w16, da0, da1, *, is_ctx, name):
    B, L, C = qkv.shape
    spec = pl.BlockSpec((None, L, GDN_HEAD), lambda b, ct: (b, 0, ct))
    wspec = pl.BlockSpec((CONV_ROWS, GDN_HEAD), lambda b, ct: (0, ct))
    dwspec = pl.BlockSpec((None, CONV_ROWS, GDN_HEAD), lambda b, ct: (b, 0, ct))

    def body(x_ref, pre_ref, w_ref, d0_ref, d1_ref, dx_ref, dw_ref):
        is_norm, scale = _conv_kind()
        _, vjp = jax.vjp(lambda p: _qk_post(p, is_norm, scale), pre_ref[...])
        dpre = vjp(d0_ref[...] + d1_ref[...])[0]
        dw_ref[...] = jnp.zeros_like(dw_ref)
        dx_ref[...] = _conv_pre_bwd(x_ref[...], w_ref, dpre, _conv_masks(L, is_ctx), is_ctx, dw_ref)

    return pl.pallas_call(body, name=name, grid=(B, C // GDN_HEAD), in_specs=[spec, spec, wspec, spec, spec],
                          out_specs=[spec, dwspec], out_shape=[SDS((B, L, C), f32), SDS((B, CONV_ROWS, C), f32)],
                          compiler_params=_cparams(2))(qkv, pre, w16, da0, da1)


def _gates_fn(ba, alog, dtb):
    T = ba.shape[0]
    nck = T // CHUNK
    lane = lax.broadcasted_iota(jnp.int32, ba.shape, 1)
    ii = lax.broadcasted_iota(jnp.int32, (nck, CHUNK, CHUNK), 1)
    jj = lax.broadcasted_iota(jnp.int32, (nck, CHUNK, CHUNK), 2)
    g = jnp.where(lane >= 8, -jnp.exp(alog) * jax.nn.softplus(ba + dtb), 0.0)
    g3 = g.reshape(nck, CHUNK, N_GATE)
    chunk_sum = lambda tri: lax.dot_general(tri.astype(f32), g3, (((2,), (1,)), ((0,), (0,))),
                                            precision=lax.Precision.HIGHEST, preferred_element_type=f32).reshape(T, N_GATE)
    gc = jnp.where(lane >= 12, chunk_sum(ii <= jj), chunk_sum(ii >= jj))
    return jnp.where(lane < 8, jax.nn.sigmoid(ba), gc)


def gates_fwd(ba, alog, dtb, *, name):
    B, L, _ = ba.shape
    T = min(TOK_TILE, L)
    t = _tok(T, N_GATE)

    def body(ba_ref, al_ref, dt_ref, o_ref):
        o_ref[...] = _gates_fn(ba_ref[...], al_ref[...], dt_ref[...])

    return pl.pallas_call(body, name=name, grid=(B, L // T),
                          in_specs=[t, _resident((1, N_GATE)), _resident((1, N_GATE))], out_specs=t,
                          out_shape=SDS((B, L, N_GATE), f32), compiler_params=_cparams(2))(ba, alog, dtb)


def gates_bwd(ba, alog, dtb, dbg, *, name):
    B, L, _ = ba.shape
    T = min(TOK_TILE, L)
    t = _tok(T, N_GATE)
    small = _resident((1, N_GATE))

    def body(ba_ref, al_ref, dt_ref, d_ref, dba_ref, dal_ref, ddt_ref):
        @pl.when(_first_step())
        def _():
            dal_ref[...] = jnp.zeros_like(dal_ref)
            ddt_ref[...] = jnp.zeros_like(ddt_ref)

        _, vjp = jax.vjp(_gates_fn, ba_ref[...], al_ref[...], dt_ref[...])
        dba, dal, ddt = vjp(d_ref[...])
        dba_ref[...] = dba
        dal_ref[...] += dal
        ddt_ref[...] += ddt

    return pl.pallas_call(body, name=name, grid=(B, L // T), in_specs=[t, small, small, t],
                          out_specs=[t, small, small],
                          out_shape=[SDS((B, L, N_GATE), f32), SDS((1, N_GATE), f32), SDS((1, N_GATE), f32)],
                          compiler_params=_cparams(2))(ba, alog, dtb, dbg)


@jax.custom_vjp
def _inv_unit_tri(mats):
    n = mats[0].shape[0]
    eye = (lax.broadcasted_iota(jnp.int32, (n, n), 0) == lax.broadcasted_iota(jnp.int32, (n, n), 1)).astype(f32)
    xs = [eye - a for a in mats]
    sq = [_dot(a, a) for a in mats]
    ps = sq
    k = 2
    while k < n:
        xs = [x + _dot(x, p) for x, p in zip(xs, ps)]
        k *= 2
        if k < n:
            ps = [_dot(p, p) for p in ps]
    return tuple(_dot(p, x) - a for p, x, a in zip(sq, xs, mats))


def _inv_unit_tri_fwd(mats):
    ns = _inv_unit_tri(mats)
    return ns, ns


def _inv_unit_tri_bwd(ns, dns):
    ys = [dn + _dot_tn(nn, dn) for nn, dn in zip(ns, dns)]
    return (tuple(-(y + _dot_nt(y, nn)) for y, nn in zip(ys, ns)),)


_inv_unit_tri.defvjp(_inv_unit_tri_fwd, _inv_unit_tri_bwd)


@jax.custom_vjp
def _inv_unit_tri_saved(mats, saved):
    return saved


_inv_unit_tri_saved.defvjp(lambda mats, saved: (saved, saved),
                           lambda ns, dns: _inv_unit_tri_bwd(ns, dns) + (tuple(jnp.zeros_like(n) for n in ns),))


def _gdn_chunk(heads, *, revs, saved=None, with_n=False):
    n = heads[0][0].shape[0]
    ii = lax.broadcasted_iota(jnp.int32, (n, n), 0)
    jj = lax.broadcasted_iota(jnp.int32, (n, n), 1)
    row = lax.broadcasted_iota(jnp.int32, (n, 1), 0)
    lower = {False: ii >= jj, True: ii <= jj}
    strict = {False: ii > jj, True: ii < jj}
    last = {False: n - 1, True: 0}
    H = range(len(heads))
    q, k, v, beta, gc, gr, s = (list(t) for t in zip(*heads))
    decay = [jnp.where(lower[revs[h]], jnp.exp(jnp.where(lower[revs[h]], gc[h] - gr[h], 0.0)), 0.0) for h in H]
    kk = [_mm_nt(k[h], k[h]) for h in H]
    qk = [_mm_nt(q[h], k[h]) * decay[h] for h in H]
    qs = [_mm(q[h], s[h]) for h in H]
    a_mat = tuple(jnp.where(strict[revs[h]], beta[h] * kk[h] * decay[h], 0.0) for h in H)
    gamma = [jnp.exp(gc[h]) for h in H]
    g_last = [jnp.sum(jnp.where(row == last[revs[h]], gc[h], 0.0), axis=0, keepdims=True) for h in H]
    nmat = _inv_unit_tri(a_mat) if saved is None else _inv_unit_tri_saved(a_mat, saved)
    bv = [beta[h] * v[h] for h in H]
    bk = [(beta[h] * gamma[h]) * k[h] for h in H]
    u0 = [bv[h] + _mm(nmat[h], bv[h]) for h in H]
    w = [bk[h] + _mm(nmat[h], bk[h]) for h in H]
    k_out = [k[h] * jnp.exp(g_last[h] - gc[h]) for h in H]
    u = [u0[h] - _mm(w[h], s[h]) for h in H]
    o = [gamma[h] * qs[h] + _mm(qk[h], u[h]) for h in H]
    s_new = [jnp.exp(g_last[h]) * s[h] + _mm_tn(k_out[h], u[h]) for h in H]
    outs = tuple((o[h], s_new[h]) for h in H)
    return (outs, nmat) if with_n else outs


def _gdn_specs(B, nc, rev):
    def cidx(n):
        return (nc - 1 - n) if rev else n
    tok = lambda width: pl.BlockSpec((B, CHUNK, width), lambda n: (0, cidx(n), 0))
    rowspec = pl.BlockSpec((B, None, N_GATE, CHUNK), lambda n: (0, cidx(n), 0, 0))
    st = pl.BlockSpec((B, GDN_HEADS, GDN_HEAD, GDN_HEAD), lambda n: (0, 0, 0, 0))
    ck = pl.BlockSpec((B, None, GDN_HEADS, GDN_HEAD, GDN_HEAD), lambda n: (0, cidx(n), 0, 0, 0))
    nsp = pl.BlockSpec((B, None, GDN_HEADS, CHUNK, CHUNK), lambda n: (0, cidx(n), 0, 0, 0))
    return tok, rowspec, st, ck, nsp


def _gdn_head_args(qkv_ref, bg_ref, bgr_ref, b, d, h):
    col = d * GDN_HEADS + h
    q = qkv_ref[b, :, h * GDN_HEAD:(h + 1) * GDN_HEAD]
    k = qkv_ref[b, :, D_GDN + h * GDN_HEAD:D_GDN + (h + 1) * GDN_HEAD]
    v = qkv_ref[b, :, 2 * D_GDN + h * GDN_HEAD:2 * D_GDN + (h + 1) * GDN_HEAD]
    bgv = bg_ref[b]
    return q, k, v, bgv[:, col:col + 1], bgv[:, 8 + col:9 + col], bgr_ref[b][8 + col:9 + col, :]


def _gdn_chains(B):
    return [(d, b, h) for d in range(N_DIR) for b in range(B) for h in range(GDN_HEADS)]


def gdn_fwd(qkv, bg, bgr, s0s, *, need_o, name):
    B, L, _ = qkv.shape
    nc = L // CHUNK
    specs = [_gdn_specs(B, nc, d == 1) for d in range(N_DIR)]
    chains = _gdn_chains(B)
    state_shape = (B, GDN_HEADS, GDN_HEAD, GDN_HEAD)

    def body(*refs):
        ins = [refs[3 * d:3 * d + 3] for d in range(N_DIR)]
        s0_refs = refs[6:8]
        k = 8
        o_refs = refs[k:k + 2] if need_o else None
        k += 2 if need_o else 0
        ck_refs, n_refs, sf_refs, s_scrs = refs[k:k + 2], refs[k + 2:k + 4], refs[k + 4:k + 6], refs[k + 6:k + 8]
        n = pl.program_id(0)

        @pl.when(n == 0)
        def _():
            for d in range(N_DIR):
                s_scrs[d][...] = s0_refs[d][...]

        for d in range(N_DIR):
            ck_refs[d][...] = s_scrs[d][...]
        heads = tuple(_gdn_head_args(*ins[d], b, d, h) + (s_scrs[d][b, h],) for d, b, h in chains)
        outs, nmat = _gdn_chunk(heads, revs=tuple(d == 1 for d, _, _ in chains), with_n=True)
        for (d, b, h), (o, s_new), nn in zip(chains, outs, nmat):
            if need_o:
                o_refs[d][b, :, h * GDN_HEAD:(h + 1) * GDN_HEAD] = o
            s_scrs[d][b, h] = s_new
            n_refs[d][b, h] = nn

        @pl.when(n == nc - 1)
        def _():
            for d in range(N_DIR):
                sf_refs[d][...] = s_scrs[d][...]

    in_specs, out_o, out_ck, out_n, out_sf = [], [], [], [], []
    for tok, rowspec, st, ck, nsp in specs:
        in_specs += [tok(3 * D_GDN), tok(N_GATE), rowspec]
        out_o.append(tok(D_GDN))
        out_ck.append(ck)
        out_n.append(nsp)
        out_sf.append(st)
    in_specs += [specs[0][2]] * 2
    out_specs = (out_o if need_o else []) + out_ck + out_n + out_sf
    out_shape = (([SDS((B, L, D_GDN), f32)] * 2 if need_o else []) + [SDS((B, nc) + state_shape[1:], f32)] * 2
                 + [SDS((B, nc, GDN_HEADS, CHUNK, CHUNK), f32)] * 2 + [SDS(state_shape, f32)] * 2)
    res = pl.pallas_call(
        body, name=name, grid=(nc,), in_specs=in_specs, out_specs=out_specs, out_shape=out_shape,
        scratch_shapes=[pltpu.VMEM(state_shape, f32)] * 2, compiler_params=_cparams(1),
    )(qkv, bg, bgr, qkv, bg, bgr, *s0s)
    if need_o:
        return res[0:2], res[2:4], res[4:6], res[6:8]
    return res[0:2], res[2:4], res[4:6]


def gdn_bwd(qkv, bg, bgr, cks, ns, do, dsfs, *, name):
    B, L, _ = qkv.shape
    nc = L // CHUNK
    has_do = do is not None
    specs = [_gdn_specs(B, nc, d != 1) for d in range(N_DIR)]
    chains = _gdn_chains(B)
    state_shape = (B, GDN_HEADS, GDN_HEAD, GDN_HEAD)
    per_dir = 6 if has_do else 5

    def body(*refs):
        ins = [refs[per_dir * d:per_dir * d + per_dir] for d in range(N_DIR)]
        k = per_dir * N_DIR
        dsf_refs = refs[k:k + 2]
        outs = [refs[k + 2 + 3 * d:k + 5 + 3 * d] for d in range(N_DIR)]
        ds0_refs, ds_scrs = refs[k + 8:k + 10], refs[k + 10:k + 12]
        n = pl.program_id(0)

        @pl.when(n == 0)
        def _():
            for d in range(N_DIR):
                ds_scrs[d][...] = dsf_refs[d][...]

        lane = lax.broadcasted_iota(jnp.int32, (CHUNK, N_GATE), 1)
        sub = lax.broadcasted_iota(jnp.int32, (N_GATE, CHUNK), 0)
        heads = tuple(_gdn_head_args(*ins[d][:3], b, d, h) + (ins[d][3][b, h],) for d, b, h in chains)
        saved = tuple(ins[d][4][b, h] for d, b, h in chains)
        _, vjp = jax.vjp(functools.partial(_gdn_chunk, revs=tuple(d == 1 for d, _, _ in chains), saved=saved), heads)
        zero = jnp.zeros((CHUNK, GDN_HEAD), f32)
        cts = tuple(((ins[d][5][b, :, h * GDN_HEAD:(h + 1) * GDN_HEAD] if has_do else zero), ds_scrs[d][b, h])
                    for d, b, h in chains)
        (dheads,) = vjp(cts)
        dbg_acc = [[jnp.zeros((CHUNK, N_GATE), f32) for _ in range(B)] for _ in range(N_DIR)]
        dbgr_acc = [[jnp.zeros((N_GATE, CHUNK), f32) for _ in range(B)] for _ in range(N_DIR)]
        for (d, b, h), (dq, dk, dv, db, dgc, dgr, ds) in zip(chains, dheads):
            col = d * GDN_HEADS + h
            dqkv_ref = outs[d][0]
            dqkv_ref[b, :, h * GDN_HEAD:(h + 1) * GDN_HEAD] = dq
            dqkv_ref[b, :, D_GDN + h * GDN_HEAD:D_GDN + (h + 1) * GDN_HEAD] = dk
            dqkv_ref[b, :, 2 * D_GDN + h * GDN_HEAD:2 * D_GDN + (h + 1) * GDN_HEAD] = dv
            dbg_acc[d][b] = dbg_acc[d][b] + jnp.where(lane == col, db, 0.0) + jnp.where(lane == 8 + col, dgc, 0.0)
            dbgr_acc[d][b] = dbgr_acc[d][b] + jnp.where(sub == 8 + col, dgr, 0.0)
            ds_scrs[d][b, h] = ds
        for d in range(N_DIR):
            for b in range(B):
                outs[d][1][b] = dbg_acc[d][b]
                outs[d][2][b] = dbgr_acc[d][b]

        @pl.when(n == nc - 1)
        def _():
            for d in range(N_DIR):
                ds0_refs[d][...] = ds_scrs[d][...]

    in_specs, args, out_specs, out_shape = [], [], [], []
    for d, (tok, rowspec, st, ck, nsp) in enumerate(specs):
        in_specs += [tok(3 * D_GDN), tok(N_GATE), rowspec, ck, nsp] + ([tok(D_GDN)] if has_do else [])
        args += [qkv, bg, bgr, cks[d], ns[d]] + ([do] if has_do else [])
        out_specs += [tok(3 * D_GDN), tok(N_GATE), rowspec]
        out_shape += [SDS((B, L, 3 * D_GDN), f32), SDS((B, L, N_GATE), f32), SDS((B, nc, N_GATE, CHUNK), f32)]
    st = specs[0][2]
    in_specs += [st, st]
    args += list(dsfs)
    out_specs += [st, st]
    out_shape += [SDS(state_shape, f32)] * 2
    res = pl.pallas_call(
        body, name=name, grid=(nc,), in_specs=in_specs, out_specs=out_specs, out_shape=out_shape,
        scratch_shapes=[pltpu.VMEM(state_shape, f32)] * 2, compiler_params=_cparams(1),
    )(*args)
    return (res[0], res[3]), (res[1], res[4]), (res[2], res[5]), (res[6], res[7])


def _gnorm_fn(o0, o1, z, w):
    o = o0 + o1
    return o * lax.rsqrt(jnp.mean(o * o, axis=-1, keepdims=True) + NORM_EPS) * w * _silu(z)


def _head_loss(y, x, gate, lng, lnb, tgt):
    r = DEEPNORM_ALPHA * x + gate * y
    mu = jnp.mean(r, axis=-1, keepdims=True)
    rc = r - mu
    var = jnp.mean(rc * rc, axis=-1, keepdims=True)
    err = rc * lax.rsqrt(var + LN_EPS) * lng + lnb - tgt
    return (0.5 / D_MODEL) * jnp.sum(jnp.sum(err * err, axis=-1, keepdims=True), axis=0, keepdims=True)


def tail_fwd_bwd(u, y0, y1, z_s5, o0, o1, z_gdn, x, tgt, gate, lng, lnb, ws, wg, dsk, wglu, bglu, nw):
    B, L, _ = x.shape
    T = min(TOK_TILE, L)

    def body(u_ref, y0_ref, y1_ref, z_ref, o0_ref, o1_ref, zg_ref, x_ref, t_ref, gate_ref, lng_ref, lnb_ref, ws_ref,
             wg_ref, dsk_ref, wglu_ref, bglu_ref, nw_ref,
             loss_ref, du_ref, dys_ref, dz_ref, do_ref, dzg_ref, gx_ref, dws_ref, dwg_ref, dgate_ref, dlng_ref, dlnb_ref,
             ddsk_ref, dwglu_ref, dbglu_ref, dnw_ref):
        n = pl.program_id(1)

        @pl.when(_first_step())
        def _():
            for r in (dws_ref, dwg_ref, dlng_ref, dlnb_ref, ddsk_ref, dwglu_ref, dbglu_ref, dnw_ref):
                r[...] = jnp.zeros_like(r)

        @pl.when(n == 0)
        def _():
            loss_ref[...] = jnp.zeros_like(loss_ref)
            dgate_ref[...] = jnp.zeros_like(dgate_ref)

        s5o, glu_vjp = jax.vjp(_glu_fn, u_ref[...], y0_ref[...], y1_ref[...], z_ref[...], dsk_ref[...],
                               wglu_ref[...].astype(f32), bglu_ref[...])
        heads = []
        for h in range(GDN_HEADS):
            sl = slice(h * GDN_HEAD, (h + 1) * GDN_HEAD)
            heads.append(jax.vjp(_gnorm_fn, o0_ref[:, sl], o1_ref[:, sl], zg_ref[:, sl], nw_ref[...]))
        sv = s5o.astype(bf16)
        gv = jnp.concatenate([out for out, _ in heads], axis=1).astype(bf16)
        y = _dot(sv, ws_ref[...]) + _dot(gv, wg_ref[...])
        loss, vjp = jax.vjp(lambda *a: _head_loss(*a, t_ref[...]), y, x_ref[...], gate_ref[...], lng_ref[...],
                            lnb_ref[...])
        dy, dx, dgate, dlng, dlnb = vjp(jnp.ones((1, 1), f32))
        loss_ref[...] += jnp.broadcast_to(loss, loss_ref.shape)
        dyb = dy.astype(bf16)
        gx_ref[...] = dx
        dws_ref[...] += _dot_tn(sv, dyb)
        dwg_ref[...] += _dot_tn(gv, dyb)
        dgate_ref[...] += dgate
        dlng_ref[...] += dlng
        dlnb_ref[...] += dlnb
        du, dys, _, dz, ddsk, dwglu, dbglu = glu_vjp(_dot_nt(dyb, ws_ref[...]))
        du_ref[...], dys_ref[...], dz_ref[...] = du, dys, dz
        ddsk_ref[...] += ddsk
        dwglu_ref[...] += dwglu
        dbglu_ref[...] += dbglu
        dgdo = _dot_nt(dyb, wg_ref[...])
        for h, (_, hvjp) in enumerate(heads):
            sl = slice(h * GDN_HEAD, (h + 1) * GDN_HEAD)
            do, _, dzg, dnw = hvjp(dgdo[:, sl])
            do_ref[:, sl] = do
            dzg_ref[:, sl] = dzg
            dnw_ref[...] += dnw

    half, full = _tok(T, D_S5), _tok(T, D_MODEL)
    row = _resident((1, D_MODEL))
    wsp = _resident((D_S5, D_MODEL))
    r512, rglu, r128 = _resident((1, D_S5)), _resident((D_S5, D_S5)), _resident((1, GDN_HEAD))
    return pl.pallas_call(
        body, name="tail_fwd_bwd", grid=(B, L // T),
        in_specs=[half] * 7 + [full, full, _per_batch(1, D_MODEL), row, row, wsp, wsp, r512, rglu, r512, r128],
        out_specs=[_per_batch(8, LANES)] + [half] * 5 + [full, wsp, wsp, _per_batch(1, D_MODEL), row, row, r512, rglu, r512,
                                                           r128],
        out_shape=[SDS((B, 8, LANES), f32)] + [SDS((B, L, D_S5), f32)] * 5 + [
            SDS((B, L, D_MODEL), f32), SDS((D_S5, D_MODEL), f32), SDS((D_GDN, D_MODEL), f32), SDS((B, 1, D_MODEL), f32),
            SDS((1, D_MODEL), f32), SDS((1, D_MODEL), f32), SDS((1, D_S5), f32), SDS((D_S5, D_S5), f32), SDS((1, D_S5), f32),
            SDS((1, GDN_HEAD), f32)],
        compiler_params=_cparams(2),
    )(u, y0, y1, z_s5, o0, o1, z_gdn, x, tgt, gate, lng, lnb, ws, wg, dsk, wglu, bglu, nw)


def _adamw_math(w, g, m, v):
    nm = ADAM_B1 * m + (1.0 - ADAM_B1) * g
    nv = ADAM_B2 * v + (1.0 - ADAM_B2) * jnp.square(g)
    m_hat = nm / (1.0 - ADAM_B1 ** ADAM_STEP)
    v_hat = nv / (1.0 - ADAM_B2 ** ADAM_STEP)
    return -ADAM_LR * (m_hat / (jnp.sqrt(v_hat) + ADAM_EPS) + ADAM_WD * w), nm, nv


def _row_tile(rows, cap=512):
    for t in range(min(cap, rows), 15, -1):
        if rows % t == 0 and t % 16 == 0:
            return t
    return rows


def adamw_3d(w, g, m, v, *, lead=False, name):
    R, C = (w.shape[0], w.shape[2]) if lead else w.shape[1:]
    if lead:
        T = next(t for t in range(min(256, R), 0, -1) if R % t == 0)
        spec = pl.BlockSpec((T, 1, C), lambda i: (i, 0, 0))
    else:
        T = _row_tile(R)
        spec = pl.BlockSpec((None, T, C), lambda i: (0, i, 0))

    def body(w_ref, g_ref, m_ref, v_ref, d_ref, nm_ref, nv_ref):
        d_ref[...], nm_ref[...], nv_ref[...] = _adamw_math(w_ref[...], g_ref[...], m_ref[...], v_ref[...])

    return pl.pallas_call(body, name=name, grid=(R // T,), in_specs=[spec] * 4, out_specs=[spec] * 3,
                          out_shape=[SDS(w.shape, f32)] * 3, compiler_params=_cparams(1))(w, g, m, v)


def adamw_small(ws, gs, ms, vs):
    n = len(ws)

    def body(*refs):
        outs = refs[4 * n:]
        for i in range(n):
            d, nm, nv = _adamw_math(refs[i][...], refs[n + i][...], refs[2 * n + i][...], refs[3 * n + i][...])
            outs[i][...], outs[n + i][...], outs[2 * n + i][...] = d, nm, nv

    res = pl.pallas_call(body, name="adamw_small", out_shape=[SDS(w.shape, f32) for w in ws] * 3,
                         compiler_params=pltpu.CompilerParams(vmem_limit_bytes=VMEM_LIMIT))(*ws, *gs, *ms, *vs)
    return res[:n], res[n:2 * n], res[2 * n:]


def sum_cores(own, got, *, name):
    A, H, C = own.shape
    T = _row_tile(H)
    spec = pl.BlockSpec((None, T, C), lambda a, i: (a, i, 0))

    def body(a_ref, b_ref, q32_ref, q16_ref):
        q = a_ref[...] + b_ref[...]
        q32_ref[...] = q
        q16_ref[...] = q.astype(bf16)

    return pl.pallas_call(body, name=name, grid=(A, H // T), in_specs=[spec, spec], out_specs=[spec, spec],
                          out_shape=[SDS((A, H, C), f32), SDS((A, H, C), bf16)], compiler_params=_cparams(2))(own, got)


def sum_chips(mine, rec, cpos, full, *, name):
    H, C = mine.shape
    T = _row_tile(H)
    nt = H // T
    out_idx = (lambda i, c_ref: (0, c_ref[0] * nt + i, 0)) if _by_rows(full) else (lambda i, c_ref: (0, i, c_ref[0]))

    def body(c_ref, m_ref, r_ref, f_ref):
        f_ref[...] = ((m_ref[...] + r_ref[0].astype(f32)) + r_ref[1].astype(f32)) + r_ref[2].astype(f32)

    grid_spec = pltpu.PrefetchScalarGridSpec(
        num_scalar_prefetch=1, grid=(nt,),
        in_specs=[pl.BlockSpec((T, C), lambda i, c_ref: (i, 0)), pl.BlockSpec((3, T, C), lambda i, c_ref: (0, i, 0))],
        out_specs=pl.BlockSpec((None, T, C), out_idx))
    return pl.pallas_call(body, name=name, grid_spec=grid_spec, out_shape=SDS((1,) + tuple(full), f32),
                          compiler_params=_cparams(1))(cpos.reshape(1).astype(jnp.int32), mine, rec)


CHIP_FLIPS = ((1, 0), (0, 1), (1, 1))


def _pos():
    return lax.axis_index("x"), lax.axis_index("y"), lax.axis_index("c")


def _comm_call(body, srcs, out_sds, n_remote, n_local, name):
    any_spec = pl.BlockSpec(memory_space=pl.ANY)
    return pl.pallas_call(
        body, name=name, in_specs=[any_spec] * len(srcs), out_specs=[any_spec] * len(out_sds), out_shape=out_sds,
        scratch_shapes=[pltpu.SemaphoreType.DMA((n_remote,)), pltpu.SemaphoreType.DMA((n_remote,)),
                        pltpu.SemaphoreType.DMA((max(n_local, 1),))],
        compiler_params=pltpu.CompilerParams(has_side_effects=True),
    )(*srcs)


def _remote(src, dst, send_sems, recv_sems, k, target):
    return pltpu.make_async_remote_copy(src, dst, send_sems.at[k], recv_sems.at[k], device_id=target,
                                        device_id_type=MESH)


def _by_rows(shape):
    return shape[0] % 16 == 0


def _half_shape(shape):
    return (shape[0] // 2, shape[1]) if _by_rows(shape) else (shape[0], shape[1] // 2)


def _half_of(ref, lead, c, shape):
    if _by_rows(shape):
        half = shape[0] // 2
        return ref.at[(*lead, pl.ds(pl.multiple_of(c * half, 8), half))]
    half = shape[1] // 2
    return ref.at[(*lead, slice(None), pl.ds(pl.multiple_of(c * half, LANES), half))]


def gather_shards(shards):
    nt = len(shards)

    def body(*refs):
        srcs, outs = refs[:nt], refs[nt:2 * nt]
        send_sems, recv_sems, _ = refs[2 * nt:]
        x, y, c = _pos()
        j = 2 * x + y
        sib = (x, y, 1 - c)
        own = [_remote(srcs[t], outs[t].at[j], send_sems, recv_sems, 7 * t + 6, sib) for t in range(nt)]
        first, passed = [], []
        for k, (fx, fy) in enumerate(CHIP_FLIPS):
            tx, ty = x ^ fx, y ^ fy
            jk = 2 * tx + ty
            for t in range(nt):
                sh = srcs[t].shape
                first.append(_remote(_half_of(srcs[t], (), c, sh), _half_of(outs[t], (j,), c, sh), send_sems, recv_sems,
                                     7 * t + k, (tx, ty, c)))
                landed = _half_of(outs[t], (jk,), c, sh)
                passed.append(_remote(landed, landed, send_sems, recv_sems, 7 * t + 3 + k, sib))
        for cp in first + own:
            cp.start()
        for a, b in zip(first, passed):
            a.wait_recv()
            b.start()
        for cp in passed + own:
            cp.wait_recv()
        for cp in first + passed + own:
            cp.wait_send()

    return _comm_call(body, shards, [SDS((4,) + s.shape, s.dtype) for s in shards], 7 * nt, 0, "gather_shards")


def swap_halves(ps):
    nt = len(ps)

    def body(*refs):
        srcs, outs = refs[:nt], refs[nt:2 * nt]
        send_sems, recv_sems, _ = refs[2 * nt:]
        x, y, c = _pos()
        cps = [_remote(_half_of(srcs[t], (a,), 1 - c, srcs[t].shape[1:]), outs[t].at[a], send_sems, recv_sems, 4 * t + a,
                       (x, y, 1 - c)) for t in range(nt) for a in range(4)]
        for cp in cps:
            cp.start()
        for cp in cps:
            cp.wait()

    return _comm_call(body, ps, [SDS((4,) + _half_shape(p.shape[1:]), p.dtype) for p in ps], 4 * nt, 0, "swap_halves")


def scatter_to_chips(qs):
    nt = len(qs)

    def body(*refs):
        srcs, outs = refs[:nt], refs[nt:2 * nt]
        send_sems, recv_sems, _ = refs[2 * nt:]
        x, y, c = _pos()
        cps = []
        for k, (fx, fy) in enumerate(CHIP_FLIPS):
            tx, ty = x ^ fx, y ^ fy
            for t in range(nt):
                cps.append(_remote(srcs[t].at[2 * tx + ty], outs[t].at[k], send_sems, recv_sems, 3 * t + k, (tx, ty, c)))
        for cp in cps:
            cp.start()
        for cp in cps:
            cp.wait()

    return _comm_call(body, qs, [SDS((3,) + q.shape[1:], q.dtype) for q in qs], 3 * nt, 0, "scatter_to_chips")


def join_halves(fs):
    nt = len(fs)

    def body(*refs):
        outs = refs[nt:2 * nt]
        send_sems, recv_sems, _ = refs[2 * nt:]
        x, y, c = _pos()
        cps = []
        for t in range(nt):
            mine = _half_of(outs[t], (0,), c, outs[t].shape[1:])
            cps.append(_remote(mine, mine, send_sems, recv_sems, t, (x, y, 1 - c)))
        for cp in cps:
            cp.start()
        for cp in cps:
            cp.wait()

    any_spec = pl.BlockSpec(memory_space=pl.ANY)
    return pl.pallas_call(
        body, name="join_halves", in_specs=[any_spec] * nt, out_specs=[any_spec] * nt,
        out_shape=[SDS(f.shape, f.dtype) for f in fs], input_output_aliases={t: t for t in range(nt)},
        scratch_shapes=[pltpu.SemaphoreType.DMA((nt,)), pltpu.SemaphoreType.DMA((nt,)), pltpu.SemaphoreType.DMA((1,))],
        compiler_params=pltpu.CompilerParams(has_side_effects=True),
    )(*fs)


DEV_FLIPS = tuple((fx, fy, fc) for fx in (0, 1) for fy in (0, 1) for fc in (0, 1))[1:]


def gather_devices(block, *, name):
    def body(src, out, send_sems, recv_sems, loc_sems):
        x, y, c = _pos()
        me = 4 * x + 2 * y + c
        mine = pltpu.make_async_copy(src, out.at[me], loc_sems.at[0])
        mine.start()
        cps = [_remote(src, out.at[me], send_sems, recv_sems, k, (x ^ fx, y ^ fy, c ^ fc))
               for k, (fx, fy, fc) in enumerate(DEV_FLIPS)]
        for cp in cps:
            cp.start()
        for cp in cps:
            cp.wait()
        mine.wait()

    return _comm_call(body, [block], [SDS((N_DEV,) + block.shape, block.dtype)], 7, 1, name)[0]


def exchange_devices(blocks, *, name):
    def body(src, out, send_sems, recv_sems, loc_sems):
        x, y, c = _pos()
        me = 4 * x + 2 * y + c
        mine = pltpu.make_async_copy(src.at[me], out.at[me], loc_sems.at[0])
        mine.start()
        cps = []
        for k, (fx, fy, fc) in enumerate(DEV_FLIPS):
            tx, ty, tc = x ^ fx, y ^ fy, c ^ fc
            cps.append(_remote(src.at[4 * tx + 2 * ty + tc], out.at[me], send_sems, recv_sems, k, (tx, ty, tc)))
        for cp in cps:
            cp.start()
        for cp in cps:
            cp.wait()
        mine.wait()

    return _comm_call(body, [blocks], [SDS(blocks.shape, blocks.dtype)], 7, 1, name)[0]


def gather_small(s):
    def body(src, out, send_sems, recv_sems, _):
        x, y, c = _pos()
        j = 2 * x + y
        cps = [_remote(src, out.at[j], send_sems, recv_sems, k, (x ^ fx, y ^ fy, c)) for k, (fx, fy) in enumerate(CHIP_FLIPS)]
        cps.append(_remote(src, out.at[j], send_sems, recv_sems, 3, (x, y, 1 - c)))
        for cp in cps:
            cp.start()
        for cp in cps:
            cp.wait()

    return _comm_call(body, [s], [SDS((4,) + s.shape, s.dtype)], 4, 0, "gather_small")[0]


SMALL_SHAPES = ((1, 2, 32, 64), (1, 2, 32, 64), (1, 2, 32), (1, 2, 32, 16, 64),
                (1, 2, 32, 16, 64), (1, 2, 32, 16, 64), (1, 2, 32, 16, 64), (1, D_S5), (1, D_S5), (1, 2, 4), (1, 2, 4),
                (1, GDN_HEAD), (1, D_MODEL), (1, D_MODEL), (LANES,))
SMALL_SWAPPED = (3, 4)


def _size(shape):
    return functools.reduce(lambda p, q: p * q, shape)


SMALL_ROWS = tuple(-(-_size(s) // (8 * LANES)) * 8 for s in SMALL_SHAPES)
SMALL_TOTAL = 2240
SMALL_QUARTER = SMALL_TOTAL // 4


def _rows(a):
    flat = a.reshape(-1)
    pad = (-flat.shape[0]) % (8 * LANES)
    if pad:
        flat = jnp.concatenate([flat, jnp.zeros((pad,), flat.dtype)])
    return flat.reshape(-1, LANES)


def _pack_small(parts):
    rows = [_rows(p) for p in parts]
    rows.append(jnp.zeros((SMALL_TOTAL - sum(SMALL_ROWS), LANES), f32))
    return jnp.concatenate(rows, axis=0)


def _unpack_small(buf):
    out, r = [], 0
    for s, n in zip(SMALL_SHAPES, SMALL_ROWS):
        out.append(buf[r:r + n].reshape(-1)[:_size(s)].reshape(s))
        r += n
    return out


def _as_2d(a):
    return a.reshape(1, -1) if a.ndim == 1 else a.reshape(-1, a.shape[-1])


S5_BG = S5_GROUPS // S5_BLOCKS


def _block_diag_in(bb):
    lead = bb.shape[:-2]
    eye = jnp.eye(S5_BG, dtype=bb.dtype)
    b4 = bb.reshape(lead + (S5_BLOCKS, S5_BG, S5_GROUP, S5_STATE))
    return jnp.einsum('...jgcp,gh->...jgchp', b4, eye).reshape(lead + (S5_BLOCKS, S5_BC, S5_BS))


def _block_diag_in_t(d):
    lead = d.shape[:-3]
    d6 = d.reshape(lead + (S5_BLOCKS, S5_BG, S5_GROUP, S5_BG, S5_STATE))
    return jnp.einsum('...jgcgp->...jgcp', d6).reshape(lead + (S5_GROUPS, S5_GROUP * S5_STATE))


def _block_diag_out(cm):
    lead = cm.shape[:-3]
    eye = jnp.eye(S5_BG, dtype=cm.dtype)
    c4 = cm.reshape(lead + (S5_BLOCKS, S5_BG, S5_GROUP, S5_STATE))
    return jnp.einsum('...jgcp,gh->...jhpgc', c4, eye).reshape(lead + (S5_BLOCKS, S5_BS, S5_BC))


def _block_diag_out_t(d):
    lead = d.shape[:-3]
    d6 = d.reshape(lead + (S5_BLOCKS, S5_BG, S5_STATE, S5_BG, S5_GROUP))
    return jnp.einsum('...jgpgc->...jgcp', d6).reshape(lead + (S5_GROUPS, S5_GROUP, S5_STATE))


def _to_chunk_rows(a):
    B, L, W = a.shape
    return a.reshape(B, L // CHUNK, CHUNK, W).transpose(0, 1, 3, 2)


def _from_chunk_rows(a):
    B, nc, W, _ = a.shape
    return a.transpose(0, 1, 3, 2).reshape(B, nc * CHUNK, W)


def local_step(x, ctx, tgt, m, w_in, lam_re, lam_im, log_dt, b_re, b_im, c_re, c_im, s5_d,
               w_glu, b_glu, conv16, a_log, dt_bias, norm_w, w_out, ln_g, ln_b):
    B, L, _ = x.shape
    zeros_state = jnp.zeros((B, GDN_HEADS, GDN_HEAD, GDN_HEAD), f32)

    shift, scale, gate = m[:B, :D_MODEL], m[:B, D_MODEL:2 * D_MODEL], m[:B, 2 * D_MODEL:]
    mod = jnp.stack([scale, shift], axis=1)
    mod_c = jnp.broadcast_to(jnp.stack([m[B, D_MODEL:2 * D_MODEL], m[B, :D_MODEL]], axis=0)[None], (B, 2, D_MODEL))

    u, z_s5, qkv, z_gdn, ba = in_proj_fwd(x, mod, w_in, name="in_proj_fwd")
    uc, _, qkvc, _, bac = in_proj_fwd(ctx, mod_c, w_in, name="in_proj_fwd_ctx")

    ng = N_DIR * S5_GROUPS
    zoh_in = (lam_re.reshape(ng, S5_STATE), lam_im.reshape(ng, S5_STATE), log_dt.reshape(ng, 1),
              b_re.reshape(ng, S5_GROUP * S5_STATE), b_im.reshape(ng, S5_GROUP * S5_STATE))
    expand = (jnp.arange(S5_GROUP * S5_STATE)[None, :] % S5_STATE == jnp.arange(S5_STATE)[:, None]).astype(f32)
    ar, ai, bbr, bbi = s5_zoh_fwd(*zoh_in, expand)
    b_blocks = _block_diag_in(jnp.stack([bbr, bbi]).astype(bf16).reshape(2, N_DIR, S5_GROUPS, S5_GROUP * S5_STATE))
    c_blocks = _block_diag_out(jnp.stack([c_re, -c_im]).astype(bf16).reshape(2, N_DIR, S5_GROUPS, S5_GROUP, S5_STATE))
    a_rows = jnp.stack([ar, ai]).reshape(2, N_DIR, S5_HALF)
    s5w, ys, hins, hins_c, hss, hss_c = [], [], [], [], [], []
    for d in range(N_DIR):
        wd = (b_blocks[0, d], b_blocks[1, d], c_blocks[0, d], c_blocks[1, d], a_rows[:, d])
        s5w.append(wd)
        hs_c, hin_c, hend_c = s5_scan_fwd(uc, *wd, jnp.zeros((B, 2, S5_HALF), f32), d=d, need_y=False,
                                          name=f"s5_fwd_ctx{d}")
        y_d, hs_d, hin, _ = s5_scan_fwd(u, *wd, hend_c, d=d, need_y=True, name=f"s5_fwd{d}")
        hss.append(hs_d)
        hss_c.append(hs_c)
        ys.append(y_d)
        hins.append(hin)
        hins_c.append(hin_c)
    glu_w = (s5_d.reshape(1, D_S5), w_glu, b_glu.reshape(1, D_S5))

    act, pre = conv_fwd(qkv, conv16, is_ctx=False, name="conv_fwd")
    act_c, pre_c = conv_fwd(qkvc, conv16, is_ctx=True, name="conv_fwd_ctx")
    pad8 = jnp.zeros((1, 8), f32)
    alog16 = jnp.concatenate([pad8, a_log.reshape(1, 8)], axis=1)
    dtb16 = jnp.concatenate([pad8, dt_bias.reshape(1, 8)], axis=1)
    bg = gates_fwd(ba, alog16, dtb16, name="gates_fwd")
    bg_c = gates_fwd(bac, alog16, dtb16, name="gates_fwd_ctx")
    bgr, bgr_c = _to_chunk_rows(bg), _to_chunk_rows(bg_c)
    cks_c, ns_c, s_c = gdn_fwd(act_c, bg_c, bgr_c, (zeros_state, zeros_state), need_o=False, name="gdn_fwd_ctx")
    os_, cks, ns, _ = gdn_fwd(act, bg, bgr, s_c, need_o=True, name="gdn_fwd")
    nw = norm_w.reshape(1, GDN_HEAD)

    (loss8, du_skip, dy, dz_s5, do, dz_gdn, gx_res, dws, dwg, dgate, dlng, dlnb, d_s5_d, d_w_glu, d_b_glu,
     d_norm_w) = tail_fwd_bwd(u, ys[0], ys[1], z_s5, os_[0], os_[1], z_gdn, x, tgt, gate[:, None, :],
                              ln_g.reshape(1, D_MODEL), ln_b.reshape(1, D_MODEL), w_out[:D_S5], w_out[D_S5:], *glu_w, nw)
    loss = jnp.sum(loss8[:, 0, 0])
    d_w_out = jnp.concatenate([dws, dwg], axis=0)

    dacts, dbgs, dbgrs, ds0s = gdn_bwd(act, bg, bgr, cks, ns, do, (zeros_state, zeros_state), name="gdn_bwd")
    dacts_c, dbgs_c, dbgrs_c, _ = gdn_bwd(act_c, bg_c, bgr_c, cks_c, ns_c, None, ds0s, name="gdn_bwd_ctx")
    dbg = dbgs[0] + dbgs[1] + _from_chunk_rows(dbgrs[0] + dbgrs[1])
    dbg_c = dbgs_c[0] + dbgs_c[1] + _from_chunk_rows(dbgrs_c[0] + dbgrs_c[1])
    dba, dal, ddt = gates_bwd(ba, alog16, dtb16, dbg, name="gates_bwd")
    dbac, dal_c, ddt_c = gates_bwd(bac, alog16, dtb16, dbg_c, name="gates_bwd_ctx")
    d_a_log = (dal + dal_c)[:, 8:].reshape(1, N_DIR, GDN_HEADS)
    d_dt_bias = (ddt + ddt_c)[:, 8:].reshape(1, N_DIR, GDN_HEADS)
    dqkv, dcw = conv_bwd(qkv, pre, conv16, dacts[0], dacts[1], is_ctx=False, name="conv_bwd")
    dqkvc, dcw_c = conv_bwd(qkvc, pre_c, conv16, dacts_c[0], dacts_c[1], is_ctx=True, name="conv_bwd_ctx")
    d_conv16 = jnp.sum(dcw, axis=0) + jnp.sum(dcw_c, axis=0)

    dus, ducs = [du_skip], []
    das, dbs, dcs = [], [], []
    for d in range(N_DIR):
        du_d, dbre1, dbim1, dct1, dcb1, da1, dh0 = s5_scan_bwd(u, dy, hss[d], *s5w[d], hins[d],
                                                                jnp.zeros((B, 2, S5_HALF), f32), d=d, name=f"s5_bwd{d}")
        duc_d, dbre2, dbim2, _, _, da2, _ = s5_scan_bwd(uc, None, hss_c[d], *s5w[d], hins_c[d], dh0, d=d,
                                                        name=f"s5_bwd_ctx{d}")
        dus.append(du_d)
        ducs.append(duc_d)
        das.append(da1 + da2)
        dbs.append(jnp.stack([dbre1 + dbre2, dbim1 + dbim2]))
        dcs.append(jnp.stack([dct1, dcb1]))
    ng_shape = (N_DIR * S5_GROUPS, -1)
    da = jnp.stack(das, axis=1)
    db = _block_diag_in_t(jnp.stack(dbs, axis=1))
    dc = _block_diag_out_t(jnp.stack(dcs, axis=1))
    dlr, dli, dldt, dbre, dbim = s5_zoh_bwd(*zoh_in, expand, da[0].reshape(ng_shape), da[1].reshape(ng_shape),
                                            db[0].reshape(ng_shape), db[1].reshape(ng_shape))
    d_s5 = (dlr, dli, dldt, dbre, dbim, dc[0], -dc[1])

    zc = jnp.zeros_like(uc)
    dw_c, dmod_c = in_proj_bwd(ctx, mod_c, (tuple(ducs), zc, dqkvc, zc, dbac), w_in, None, None,
                               name="in_proj_bwd_ctx")
    d_w_in, dmod, grad_x = in_proj_bwd(x, mod, (tuple(dus), dz_s5, dqkv, dz_gdn, dba), w_in, gx_res, dw_c,
                                       name="in_proj_bwd")
    dmod_c = jnp.sum(dmod_c, axis=0)

    dm_rows = jnp.concatenate([dmod[:, 1], dmod[:, 0], dgate[:, 0]], axis=1)
    dm_ctx = jnp.concatenate([dmod_c[1], dmod_c[0], jnp.zeros((D_MODEL,), f32)])[None]
    dm = jnp.concatenate([dm_rows, dm_ctx], axis=0)
    small = (*d_s5, d_s5_d, d_b_glu, d_a_log, d_dt_bias, d_norm_w, dlng, dlnb)
    small = tuple(g.reshape(s) for g, s in zip(small, SMALL_SHAPES))
    return loss, grad_x, (d_w_in, d_w_out, d_w_glu, d_conv16), small, dm


SHARDED = (1, 3, 18, 12, 14)
UNSHARDED = tuple(i for i in range(21) if i not in SHARDED)
SMALL = tuple(i for i in UNSHARDED if i not in (0, 2))
W_IN_SHARD = 772


def _conv_rows(w):
    return jnp.concatenate([w.reshape(9, w.shape[-1]), jnp.zeros((CONV_ROWS - 9, w.shape[-1]), f32)], axis=0)


def kernel(x, c, ctx, c_ctx, w_ada, b_ada, w_in, s5_lambda_re, s5_lambda_im, s5_log_dt, s5_b_re, s5_b_im, s5_c_re, s5_c_im, s5_d, w_glu, b_glu, conv_w, gdn_a_log, gdn_dt_bias, gdn_norm_w, w_out, ln_g, ln_b, loss_target, m_c_ctx, m_w_ada, m_b_ada, m_w_in, m_s5_lambda_re, m_s5_lambda_im, m_s5_log_dt, m_s5_b_re, m_s5_b_im, m_s5_c_re, m_s5_c_im, m_s5_d, m_w_glu, m_b_glu, m_conv_w, m_gdn_a_log, m_gdn_dt_bias, m_gdn_norm_w, m_w_out, m_ln_g, m_ln_b, v_c_ctx, v_w_ada, v_b_ada, v_w_in, v_s5_lambda_re, v_s5_lambda_im, v_s5_log_dt, v_s5_b_re, v_s5_b_im, v_s5_c_re, v_s5_c_im, v_s5_d, v_w_glu, v_b_glu, v_conv_w, v_gdn_a_log, v_gdn_dt_bias, v_gdn_norm_w, v_w_out, v_ln_g, v_ln_b):
    weights = [c_ctx, w_ada, b_ada, w_in, s5_lambda_re, s5_lambda_im, s5_log_dt, s5_b_re, s5_b_im, s5_c_re, s5_c_im,
               s5_d, w_glu, b_glu, conv_w, gdn_a_log, gdn_dt_bias, gdn_norm_w, w_out, ln_g, ln_b]
    ms = [m_c_ctx, m_w_ada, m_b_ada, m_w_in, m_s5_lambda_re, m_s5_lambda_im, m_s5_log_dt, m_s5_b_re, m_s5_b_im,
          m_s5_c_re, m_s5_c_im, m_s5_d, m_w_glu, m_b_glu, m_conv_w, m_gdn_a_log, m_gdn_dt_bias, m_gdn_norm_w, m_w_out,
          m_ln_g, m_ln_b]
    vs = [v_c_ctx, v_w_ada, v_b_ada, v_w_in, v_s5_lambda_re, v_s5_lambda_im, v_s5_log_dt, v_s5_b_re, v_s5_b_im,
          v_s5_c_re, v_s5_c_im, v_s5_d, v_w_glu, v_b_glu, v_conv_w, v_gdn_a_log, v_gdn_dt_bias, v_gdn_norm_w, v_w_out,
          v_ln_g, v_ln_b]
    cpos = lax.axis_index("c")
    jchip = 2 * lax.axis_index("x") + lax.axis_index("y")

    c_all = gather_devices(c, name="gather_c")
    cc = jnp.concatenate([c_all, jnp.broadcast_to(c_ctx[None, None, :], (N_DEV, 1, D_MODEL)),
                          jnp.zeros((N_DEV, 5, D_MODEL), f32)], axis=1)
    w_ada16 = w_ada[0].astype(bf16)
    b_cols = lax.dynamic_slice_in_dim(b_ada, jchip * ADA_SHARD, ADA_SHARD, axis=1)
    m_mine = exchange_devices(ada_fwd(cc, w_ada16, b_cols), name="exchange_m")
    m_rows = jnp.concatenate([m_mine[2 * j, :3] for j in range(4)], axis=1)

    conv_shard = _conv_rows(conv_w)
    g_in, g_out, g_glu, g_conv = gather_shards(
        [jnp.transpose(w_in[0]).astype(bf16), w_out[0].astype(bf16), w_glu[0].astype(bf16), conv_shard])
    w_in_t = g_in.reshape(P_IN, D_MODEL)
    conv16 = g_conv.transpose(1, 0, 2).reshape(CONV_ROWS, 3 * D_GDN)

    swap = lambda a: jnp.swapaxes(a, 3, 4)
    loss, grad_x, big, small, dm_rows = local_step(
        x, ctx, loss_target, m_rows, w_in_t, s5_lambda_re, s5_lambda_im, s5_log_dt, swap(s5_b_re), swap(s5_b_im),
        s5_c_re, s5_c_im, s5_d, g_glu.reshape(D_S5, D_S5), b_glu, conv16, gdn_a_log, gdn_dt_bias, gdn_norm_w,
        g_out.reshape(D_MODEL, D_MODEL), ln_g, ln_b)
    me = 2 * jchip + cpos
    loss_hi = loss.astype(bf16).astype(f32)
    loss_row = jnp.zeros((LANES,), f32).at[me].set(loss_hi).at[N_DEV + me].set(loss - loss_hi)

    dm8 = jnp.concatenate([dm_rows, jnp.zeros((5, 3 * D_MODEL), f32)], axis=0)
    dm_by_chip = dm8.reshape(8, 4, ADA_SHARD).transpose(1, 0, 2)
    dm_cols = exchange_devices(jnp.repeat(dm_by_chip, 2, axis=0), name="exchange_dm")
    g_w_ada, pb = ada_bwd(cc, w_ada16, dm_cols)
    pb_all = gather_devices(pb, name="gather_p")
    g_c_ctx = c_ctx_bwd(pb_all, c_ctx[None, :])[0]
    g_b_ada = jnp.concatenate([pb_all[2 * j, 1:2, :ADA_SHARD] for j in range(4)], axis=1)

    d_w_in, d_w_out, d_w_glu, d_conv16 = big
    slabs = [d_w_in.reshape(4, W_IN_SHARD, D_MODEL),
             d_w_out.reshape(4, D_MODEL // 4, D_MODEL),
             d_w_glu.reshape(4, D_S5 // 4, D_S5),
             d_conv16.reshape(CONV_ROWS, 4, 3 * D_GDN // 4).transpose(1, 0, 2),
             _pack_small(small + (loss_row,)).reshape(4, SMALL_QUARTER, LANES)]
    got = swap_halves(slabs)
    q32, q16 = [], []
    for t, (s, g) in enumerate(zip(slabs, got)):
        if _by_rows(s.shape[1:]):
            own = lax.dynamic_index_in_dim(s.reshape(4, 2, s.shape[1] // 2, s.shape[2]), cpos, axis=1, keepdims=False)
        else:
            own = lax.dynamic_slice_in_dim(s, cpos * (s.shape[2] // 2), s.shape[2] // 2, axis=2)
        a, b = sum_cores(own, g, name=f"sum_cores{t}")
        q32.append(a)
        q16.append(b)
    rec = scatter_to_chips(q16)
    fs = [sum_chips(lax.dynamic_index_in_dim(q, jchip, axis=0, keepdims=False), r, cpos, s.shape[1:], name=f"sum_chips{t}")
          for t, (q, r, s) in enumerate(zip(q32, rec, slabs))]
    red = join_halves(fs)
    g_small = _unpack_small(gather_small(red[4][0]).reshape(SMALL_TOTAL, LANES))
    loss = jnp.sum(g_small[-1][:2 * N_DEV])
    g_small = g_small[:-1]
    g_shard = {1: g_w_ada, 3: red[0], 18: red[1], 12: red[2], 14: red[3]}

    grads, deltas, new_m, new_v = [None] * 21, [None] * 21, [None] * 21, [None] * 21
    for t, i in enumerate(SHARDED):
        conv, win = i == 14, i == 3
        prep = (lambda a: _conv_rows(a)[None]) if conv else ((lambda a: jnp.transpose(a, (2, 0, 1))) if win else (lambda a: a))
        g = jnp.transpose(g_shard[i], (1, 0, 2)) if win else g_shard[i]
        d, nm, nv = adamw_3d(prep(weights[i]), g, prep(ms[i]), prep(vs[i]), lead=win, name=f"adamw{t}")
        for lst, val in ((grads, g), (deltas, d), (new_m, nm), (new_v, nv)):
            lst[i] = (val[0, :9].reshape(weights[i].shape) if conv else (jnp.transpose(val, (1, 2, 0)) if win else val))
    g_un = {0: g_c_ctx, 2: g_b_ada, **{i: g_small[n] for n, i in enumerate(SMALL)}}
    swapped = [SMALL[n] for n in SMALL_SWAPPED]
    small_in = lambda lst: [_as_2d(swap(lst[i]) if i in swapped else lst[i]) for i in UNSHARDED]
    sm = adamw_small(small_in(weights), [_as_2d(g_un[i]) for i in UNSHARDED], small_in(ms), small_in(vs))
    for n, i in enumerate(UNSHARDED):
        back = ((lambda a: swap(a.reshape(swap(weights[i]).shape))) if i in swapped
                else (lambda a: a.reshape(weights[i].shape)))
        grads[i] = back(g_un[i])
        for lst, res in ((deltas, sm[0]), (new_m, sm[1]), (new_v, sm[2])):
            lst[i] = back(res[n])
    return (loss, grad_x, *grads, *deltas, *new_m, *new_v)
```

```python
import functools

import jax
import jax.numpy as jnp
from jax import lax
from jax.experimental import pallas as pl
from jax.experimental.pallas import tpu as pltpu

f32 = jnp.float32
bf16 = jnp.bfloat16
SDS = jax.ShapeDtypeStruct

D_MODEL = 1024
D_S5 = 512
S5_GROUP = 16
S5_GROUPS = 32
S5_STATE = 64
S5_HALF = S5_GROUPS * S5_STATE
D_GDN = 512
GDN_HEAD = 128
GDN_HEADS = 4
CHUNK = 64
GRID_W = 64
N_DIR = 2
P_IN = 3088
DEEPNORM_ALPHA = 2.0 ** 0.25
LN_EPS = 1e-5
NORM_EPS = 1e-6
ADAM_LR, ADAM_B1, ADAM_B2, ADAM_EPS, ADAM_WD, ADAM_STEP = 0.001, 0.9, 0.999, 1e-08, 0.01, 10

LANES = 128
VMEM_LIMIT = 56 * 1024 * 1024
TOK_TILE = 256
S5_TILE = 256
MESH = pl.DeviceIdType.MESH


def _cparams(n_grid):
    return pltpu.CompilerParams(dimension_semantics=("arbitrary",) * n_grid, vmem_limit_bytes=VMEM_LIMIT)


def _dot(a, b):
    return jnp.dot(a.astype(bf16), b.astype(bf16), preferred_element_type=f32)


def _dot_nt(a, b):
    return lax.dot_general(a.astype(bf16), b.astype(bf16), (((1,), (1,)), ((), ())), preferred_element_type=f32)


def _dot_tn(a, b):
    return lax.dot_general(a.astype(bf16), b.astype(bf16), (((0,), (0,)), ((), ())), preferred_element_type=f32)


def _dot_hi(a, b):
    return jnp.dot(a, b, precision=lax.Precision.HIGHEST, preferred_element_type=f32)


@jax.custom_vjp
def _mm(a, b):
    return _dot(a, b)


@jax.custom_vjp
def _mm_nt(a, b):
    return _dot_nt(a, b)


@jax.custom_vjp
def _mm_tn(a, b):
    return _dot_tn(a, b)


_mm.defvjp(lambda a, b: (_dot(a, b), (a, b)), lambda r, g: (_mm_nt(g, r[1]), _mm_tn(r[0], g)))
_mm_nt.defvjp(lambda a, b: (_dot_nt(a, b), (a, b)), lambda r, g: (_mm(g, r[1]), _mm_tn(g, r[0])))
_mm_tn.defvjp(lambda a, b: (_dot_tn(a, b), (a, b)), lambda r, g: (_mm_nt(r[1], g), _mm(r[0], g)))


def _silu(x):
    return x * jax.nn.sigmoid(x)


def _gelu(x):
    return 0.5 * x * (1.0 + lax.erf(x * (2.0 ** -0.5)))


def _resident(shape):
    nd = len(shape)
    return pl.BlockSpec(shape, lambda *_: (0,) * nd, pipeline_mode=pl.Buffered(1))


def _tok(tile, width, nt=None, rev=False):
    if rev:
        return pl.BlockSpec((None, tile, width), lambda b, n: (b, nt - 1 - n, 0))
    return pl.BlockSpec((None, tile, width), lambda b, n: (b, n, 0))


def _per_batch(rows, width):
    return pl.BlockSpec((None, rows, width), lambda b, n: (b, 0, 0))


def _first_step():
    return jnp.logical_and(pl.program_id(0) == 0, pl.program_id(1) == 0)


ADA_SHARD = 3 * D_MODEL // 4
N_DEV = 8


def ada_fwd(cc, w, b):
    def body(cc_ref, w_ref, b_ref, m_ref):
        for k in range(N_DEV):
            m_ref[k] = _dot(_silu(cc_ref[k]), w_ref[...]) + b_ref[...]

    return pl.pallas_call(body, name="ada_fwd", out_shape=SDS((N_DEV, 8, ADA_SHARD), f32),
                          compiler_params=pltpu.CompilerParams(vmem_limit_bytes=VMEM_LIMIT))(cc, w, b)


def ada_bwd(cc, w, dmj):
    def body(cc_ref, w_ref, dmj_ref, dw_ref, pb_ref):
        dw = jnp.zeros((D_MODEL, ADA_SHARD), f32)
        p = jnp.zeros((8, D_MODEL), f32)
        db = jnp.zeros((1, ADA_SHARD), f32)
        for k in range(N_DEV):
            dw = dw + _dot_tn(_silu(cc_ref[k]), dmj_ref[k])
            p = p + _dot_nt(dmj_ref[k], w_ref[...])
            db = db + jnp.sum(dmj_ref[k], axis=0, keepdims=True)
        dw_ref[0] = dw
        pb_ref[...] = jnp.zeros_like(pb_ref)
        pb_ref[0:1, :] = p[2:3, :]
        pb_ref[1:2, 0:ADA_SHARD] = db

    return pl.pallas_call(
        body, name="ada_bwd", out_shape=[SDS((1, D_MODEL, ADA_SHARD), f32), SDS((8, D_MODEL), f32)],
        compiler_params=pltpu.CompilerParams(vmem_limit_bytes=VMEM_LIMIT))(cc, w, dmj)


def c_ctx_bwd(pb_all, c_ctx):
    def body(p_ref, c_ref, d_ref):
        ds = ((p_ref[0, 0:1, :] + p_ref[2, 0:1, :]) + p_ref[4, 0:1, :]) + p_ref[6, 0:1, :]
        _, vjp = jax.vjp(_silu, c_ref[...])
        d_ref[...] = vjp(ds)[0]

    return pl.pallas_call(body, name="c_ctx_bwd", out_shape=SDS((1, D_MODEL), f32))(pb_all, c_ctx)


N_GATE = 2 * N_DIR * GDN_HEADS
IN_WIDTHS = (D_S5, D_S5, 3 * D_GDN, D_GDN, N_GATE)
IN_OFFS = (0, 512, 1024, 2560, 3072)


def in_proj_fwd(x, mod, wt, *, name):
    B, L, _ = x.shape
    T = min(TOK_TILE, L)

    def body(x_ref, mod_ref, w_ref, *o_refs):
        h = (x_ref[...] * (1.0 + mod_ref[0:1, :]) + mod_ref[1:2, :]).astype(bf16)
        for o_ref, off, wd in zip(o_refs, IN_OFFS, IN_WIDTHS):
            o_ref[...] = _dot_nt(h, w_ref[off:off + wd, :])

    return pl.pallas_call(
        body, name=name, grid=(B, L // T),
        in_specs=[_tok(T, D_MODEL), _per_batch(2, D_MODEL), _resident((P_IN, D_MODEL))],
        out_specs=[_tok(T, wd) for wd in IN_WIDTHS],
        out_shape=[SDS((B, L, wd), f32) for wd in IN_WIDTHS],
        compiler_params=_cparams(2),
    )(x, mod, wt)


def in_proj_bwd(x, mod, ds, wt, gx_res, dw_start, *, name):
    B, L, _ = x.shape
    T = min(TOK_TILE, L)
    with_dx = gx_res is not None
    with_start = dw_start is not None
    n_u = len(ds[0])

    def body(*refs):
        x_ref, mod_ref = refs[0], refs[1]
        du_refs = refs[2:2 + n_u]
        d_refs = refs[2 + n_u:6 + n_u]
        w_ref = refs[6 + n_u]
        k = 7 + n_u
        if with_dx:
            gx_ref = refs[k]
            k += 1
        if with_start:
            start_ref = refs[k]
            k += 1
        dw_ref, dmod_ref = refs[k], refs[k + 1]
        if with_dx:
            dx_ref = refs[k + 2]
        n = pl.program_id(1)

        @pl.when(_first_step())
        def _():
            dw_ref[...] = start_ref[...] if with_start else jnp.zeros_like(dw_ref)

        @pl.when(n == 0)
        def _():
            dmod_ref[...] = jnp.zeros_like(dmod_ref)

        xv = x_ref[...]
        scale1 = 1.0 + mod_ref[0:1, :]
        h = (xv * scale1 + mod_ref[1:2, :]).astype(bf16)
        du = du_refs[0][...]
        for r in du_refs[1:]:
            du = du + r[...]
        dh = jnp.zeros((T, D_MODEL), f32)
        for dv, off, wd in zip([du] + [r[...] for r in d_refs], IN_OFFS, IN_WIDTHS):
            dv = dv.astype(bf16)
            dh = dh + _dot(dv, w_ref[off:off + wd, :])
            dw_ref[off:off + wd, :] += _dot_tn(dv, h)
        dmod_ref[0:1, :] += jnp.sum(dh * xv, axis=0, keepdims=True)
        dmod_ref[1:2, :] += jnp.sum(dh, axis=0, keepdims=True)
        if with_dx:
            dx_ref[...] = gx_ref[...] + dh * scale1

    in_specs = ([_tok(T, D_MODEL), _per_batch(2, D_MODEL)] + [_tok(T, D_S5)] * n_u + [_tok(T, wd) for wd in IN_WIDTHS[1:]]
                + [_resident((P_IN, D_MODEL))])
    args = [x, mod, *ds[0], *ds[1:], wt]
    out_specs = [_resident((P_IN, D_MODEL)), _per_batch(2, D_MODEL)]
    out_shape = [SDS((P_IN, D_MODEL), f32), SDS((B, 2, D_MODEL), f32)]
    if with_dx:
        in_specs.append(_tok(T, D_MODEL))
        args.append(gx_res)
        out_specs.append(_tok(T, D_MODEL))
        out_shape.append(SDS((B, L, D_MODEL), f32))
    if with_start:
        in_specs.append(_resident((P_IN, D_MODEL)))
        args.append(dw_start)
    return pl.pallas_call(body, name=name, grid=(B, L // T), in_specs=in_specs, out_specs=out_specs,
                          out_shape=out_shape, compiler_params=_cparams(2))(*args)


def _s5_zoh(lr, li, ldt, bre, bim, expand):
    dt = jnp.exp(ldt)
    zr, zi = lr * dt, li * dt
    e = jnp.exp(zr)
    ar, ai = e * jnp.cos(zi), e * jnp.sin(zi)
    den = lr * lr + li * li
    czr = ((ar - 1.0) * lr + ai * li) / den
    czi = (ai * lr - (ar - 1.0) * li) / den
    czr_e, czi_e = _dot_hi(czr, expand), _dot_hi(czi, expand)
    return ar, ai, czr_e * bre - czi_e * bim, czr_e * bim + czi_e * bre


_ZOH_OUT = [(N_DIR * S5_GROUPS, S5_STATE)] * 2 + [(N_DIR * S5_GROUPS, S5_STATE * S5_GROUP)] * 2


def s5_zoh_fwd(lr, li, ldt, bre, bim, expand):
    def body(lr_ref, li_ref, ldt_ref, bre_ref, bim_ref, e_ref, ar_ref, ai_ref, bbr_ref, bbi_ref):
        ar, ai, bbr, bbi = _s5_zoh(lr_ref[...], li_ref[...], ldt_ref[...], bre_ref[...], bim_ref[...], e_ref[...])
        ar_ref[...], ai_ref[...], bbr_ref[...], bbi_ref[...] = ar, ai, bbr, bbi

    return pl.pallas_call(body, name="s5_zoh_fwd", out_shape=[SDS(s, f32) for s in _ZOH_OUT])(
        lr, li, ldt, bre, bim, expand)


def s5_zoh_bwd(lr, li, ldt, bre, bim, expand, dar, dai, dbbr, dbbi):
    def body(lr_ref, li_ref, ldt_ref, bre_ref, bim_ref, e_ref, dar_ref, dai_ref, dbbr_ref, dbbi_ref,
             dlr_ref, dli_ref, dldt_ref, dbre_ref, dbim_ref):
        ev = e_ref[...]
        _, vjp = jax.vjp(lambda a, b, c, d, e: _s5_zoh(a, b, c, d, e, ev),
                         lr_ref[...], li_ref[...], ldt_ref[...], bre_ref[...], bim_ref[...])
        outs = vjp((dar_ref[...], dai_ref[...], dbbr_ref[...], dbbi_ref[...]))
        dlr_ref[...], dli_ref[...], dldt_ref[...], dbre_ref[...], dbim_ref[...] = outs

    shapes = [lr.shape, li.shape, ldt.shape, bre.shape, bim.shape]
    return pl.pallas_call(body, name="s5_zoh_bwd", out_shape=[SDS(s, f32) for s in shapes])(
        lr, li, ldt, bre, bim, expand, dar, dai, dbbr, dbbi)


def _scan_rows(T, rev, ar, ai, h0s, refs, off):
    def step(i, carry):
        t = off + ((T - 1 - i) if rev else i)
        out = []
        for (hr, hi), (r_ref, i_ref) in zip(carry, refs):
            nr = ar * hr - ai * hi + r_ref[pl.ds(t, 1), :]
            ni = ar * hi + ai * hr + i_ref[pl.ds(t, 1), :]
            r_ref[pl.ds(t, 1), :] = nr
            i_ref[pl.ds(t, 1), :] = ni
            out.append((nr, ni))
        return tuple(out)

    return lax.fori_loop(0, T, step, tuple(h0s))


S5_BLOCKS = 4
S5_BC = D_S5 // S5_BLOCKS
S5_BS = S5_HALF // S5_BLOCKS


def _s5_in(uv, bre_ref, bim_ref, hr_ref, hi_ref, off, T):
    for jb in range(S5_BLOCKS):
        uj = uv[:, jb * S5_BC:(jb + 1) * S5_BC]
        hr_ref[off:off + T, jb * S5_BS:(jb + 1) * S5_BS] = _dot(uj, bre_ref[jb])
        hi_ref[off:off + T, jb * S5_BS:(jb + 1) * S5_BS] = _dot(uj, bim_ref[jb])


def _s5_specs(B, T, nt, rev):
    tidx = (lambda n: nt - 1 - n) if rev else (lambda n: n)
    tok = pl.BlockSpec((B, T, D_S5), lambda n: (0, tidx(n), 0))
    hin = pl.BlockSpec((B, None, 2, S5_HALF), lambda n: (0, tidx(n), 0, 0))
    state = pl.BlockSpec((B, 2, S5_HALF), lambda n: (0, 0, 0))
    return tok, hin, state


def s5_scan_fwd(u, bre, bim, ctop, cbot, arow, h0, *, d, need_y, name):
    B, L, _ = u.shape
    T = min(S5_TILE, L)
    nt = L // T
    rev = d == 1

    def body(u_ref, bre_ref, bim_ref, ct_ref, cb_ref, a_ref, h0_ref, *rest):
        if need_y:
            y_ref, hs_ref, hin_ref, hend_ref, hr_scr, hi_scr, h_scr = rest
        else:
            hs_ref, hin_ref, hend_ref, hr_scr, hi_scr, h_scr = rest
        n = pl.program_id(0)

        @pl.when(n == 0)
        def _():
            h_scr[...] = h0_ref[...]

        hin_ref[...] = h_scr[...]
        for b in range(B):
            _s5_in(u_ref[b].astype(bf16), bre_ref, bim_ref, hr_scr.at[b], hi_scr.at[b], 0, T)
        hs = _scan_rows(T, rev, a_ref[0:1, :], a_ref[1:2, :], [(h_scr[b, 0:1, :], h_scr[b, 1:2, :]) for b in range(B)],
                        [(hr_scr.at[b], hi_scr.at[b]) for b in range(B)], 0)
        for b in range(B):
            h_scr[b, 0:1, :] = hs[b][0]
            h_scr[b, 1:2, :] = hs[b][1]
            hs_ref[b, :, 0:S5_HALF] = hr_scr[b].astype(bf16)
            hs_ref[b, :, S5_HALF:2 * S5_HALF] = hi_scr[b].astype(bf16)
            if need_y:
                for jb in range(S5_BLOCKS):
                    st = slice(jb * S5_BS, (jb + 1) * S5_BS)
                    y_ref[b, :, jb * S5_BC:(jb + 1) * S5_BC] = (_dot(hr_scr[b, :, st], ct_ref[jb])
                                                                 + _dot(hi_scr[b, :, st], cb_ref[jb]))

        @pl.when(n == nt - 1)
        def _():
            hend_ref[...] = h_scr[...]

    tok, hin_spec, state = _s5_specs(B, T, nt, rev)
    hs_spec = pl.BlockSpec((B, T, 2 * S5_HALF), tok.index_map)
    out_specs = [hs_spec, hin_spec, state]
    out_shape = [SDS((B, L, 2 * S5_HALF), bf16), SDS((B, nt, 2, S5_HALF), f32), SDS((B, 2, S5_HALF), f32)]
    if need_y:
        out_specs.insert(0, tok)
        out_shape.insert(0, SDS((B, L, D_S5), f32))
    w_in, w_out = _resident((S5_BLOCKS, S5_BC, S5_BS)), _resident((S5_BLOCKS, S5_BS, S5_BC))
    return pl.pallas_call(
        body, name=name, grid=(nt,),
        in_specs=[tok, w_in, w_in, w_out, w_out, _resident((2, S5_HALF)), state],
        out_specs=out_specs, out_shape=out_shape,
        scratch_shapes=[pltpu.VMEM((B, T, S5_HALF), f32), pltpu.VMEM((B, T, S5_HALF), f32),
                        pltpu.VMEM((B, 2, S5_HALF), f32)],
        compiler_params=_cparams(1),
    )(u, bre, bim, ctop, cbot, arow, h0)


def s5_scan_bwd(u, dy, hs, bre, bim, ctop, cbot, arow, hin, dhend, *, d, name):
    B, L, _ = u.shape
    T = min(S5_TILE, L)
    nt = L // T
    rev = d == 1
    has_dy = dy is not None
    PAD = 8

    def body(*refs):
        u_ref = refs[0]
        k = 1
        if has_dy:
            dy_ref = refs[1]
            k = 2
        hs_ref = refs[k]
        k += 1
        bre_ref, bim_ref, ct_ref, cb_ref, a_ref, hin_ref, dhend_ref = refs[k:k + 7]
        du_ref, dbre_ref, dbim_ref, dct_ref, dcb_ref, da_ref, dh0_ref = refs[k + 7:k + 14]
        hr_scr, hi_scr, gr_scr, gi_scr, p_scr = refs[k + 14:]
        n = pl.program_id(0)

        @pl.when(n == 0)
        def _():
            for r in (dbre_ref, dbim_ref, dct_ref, dcb_ref, da_ref):
                r[...] = jnp.zeros_like(r)
            p_scr[...] = dhend_ref[...]

        ar, ai = a_ref[0:1, :], a_ref[1:2, :]
        prev_row = PAD + T if rev else PAD - 1
        uvs = []
        for b in range(B):
            uvs.append(u_ref[b].astype(bf16))
            hr_scr[b, PAD:PAD + T, :] = hs_ref[b, :, 0:S5_HALF].astype(f32)
            hi_scr[b, PAD:PAD + T, :] = hs_ref[b, :, S5_HALF:2 * S5_HALF].astype(f32)
            hr_scr[b, prev_row:prev_row + 1, :] = hin_ref[b, 0:1, :]
            hi_scr[b, prev_row:prev_row + 1, :] = hin_ref[b, 1:2, :]
        if has_dy:
            for b in range(B):
                dyv = dy_ref[b].astype(bf16)
                for jb in range(S5_BLOCKS):
                    st = slice(jb * S5_BS, (jb + 1) * S5_BS)
                    dyj = dyv[:, jb * S5_BC:(jb + 1) * S5_BC]
                    gr_scr[b, :, st] = _dot_nt(dyj, ct_ref[jb])
                    gi_scr[b, :, st] = _dot_nt(dyj, cb_ref[jb])
                    dct_ref[jb] += _dot_tn(hr_scr[b, PAD:PAD + T, st], dyj)
                    dcb_ref[jb] += _dot_tn(hi_scr[b, PAD:PAD + T, st], dyj)
        else:
            gr_scr[...] = jnp.zeros_like(gr_scr)
            gi_scr[...] = jnp.zeros_like(gi_scr)

        def step(i, carry):
            t = i if rev else T - 1 - i
            tp = PAD + t + (1 if rev else -1)
            out = []
            for b, (pr, pi, dar, dai) in enumerate(carry):
                gr = gr_scr[b, pl.ds(t, 1), :] + pr
                gi = gi_scr[b, pl.ds(t, 1), :] + pi
                gr_scr[b, pl.ds(t, 1), :] = gr
                gi_scr[b, pl.ds(t, 1), :] = gi
                hpr = hr_scr[b, pl.ds(tp, 1), :]
                hpi = hi_scr[b, pl.ds(tp, 1), :]
                out.append((ar * gr + ai * gi, ar * gi - ai * gr, dar + hpr * gr + hpi * gi, dai + hpr * gi - hpi * gr))
            return tuple(out)

        zero = jnp.zeros((1, S5_HALF), f32)
        res = lax.fori_loop(0, T, step, tuple((p_scr[b, 0:1, :], p_scr[b, 1:2, :], zero, zero) for b in range(B)))
        for b in range(B):
            pr, pi, dar, dai = res[b]
            p_scr[b, 0:1, :] = pr
            p_scr[b, 1:2, :] = pi
            da_ref[0:1, :] += dar
            da_ref[1:2, :] += dai
            for jb in range(S5_BLOCKS):
                st = slice(jb * S5_BS, (jb + 1) * S5_BS)
                ch = slice(jb * S5_BC, (jb + 1) * S5_BC)
                gr_j = gr_scr[b, :, st].astype(bf16)
                gi_j = gi_scr[b, :, st].astype(bf16)
                du_ref[b, :, ch] = _dot_nt(gr_j, bre_ref[jb]) + _dot_nt(gi_j, bim_ref[jb])
                dbre_ref[jb] += _dot_tn(uvs[b][:, ch], gr_j)
                dbim_ref[jb] += _dot_tn(uvs[b][:, ch], gi_j)

        @pl.when(n == nt - 1)
        def _():
            dh0_ref[...] = p_scr[...]

    tok, hin_spec, state = _s5_specs(B, T, nt, not rev)
    hs_spec = pl.BlockSpec((B, T, 2 * S5_HALF), tok.index_map)
    w_in, w_out = _resident((S5_BLOCKS, S5_BC, S5_BS)), _resident((S5_BLOCKS, S5_BS, S5_BC))
    wspecs = [w_in, w_in, w_out, w_out]
    in_specs = [tok] + ([tok] if has_dy else []) + [hs_spec] + wspecs + [_resident((2, S5_HALF)), hin_spec, state]
    args = [u] + ([dy] if has_dy else []) + [hs, bre, bim, ctop, cbot, arow, hin, dhend]
    return pl.pallas_call(
        body, name=name, grid=(nt,), in_specs=in_specs,
        out_specs=[tok] + wspecs + [_resident((2, S5_HALF)), state],
        out_shape=[SDS((B, L, D_S5), f32), SDS((S5_BLOCKS, S5_BC, S5_BS), f32), SDS((S5_BLOCKS, S5_BC, S5_BS), f32),
                   SDS((S5_BLOCKS, S5_BS, S5_BC), f32), SDS((S5_BLOCKS, S5_BS, S5_BC), f32), SDS((2, S5_HALF), f32),
                   SDS((B, 2, S5_HALF), f32)],
        scratch_shapes=[pltpu.VMEM((B, T + 2 * PAD, S5_HALF), f32), pltpu.VMEM((B, T + 2 * PAD, S5_HALF), f32),
                        pltpu.VMEM((B, T, S5_HALF), f32), pltpu.VMEM((B, T, S5_HALF), f32),
                        pltpu.VMEM((B, 2, S5_HALF), f32)],
        compiler_params=_cparams(1),
    )(*args)


def _glu_fn(u, y0, y1, z, dsk, wg, bg):
    g = _gelu(dsk * u + y0 + y1)
    return g * jax.nn.sigmoid(_mm(g, wg) + bg) * _silu(z)


CONV_ROWS = 16


def _shift(x, s):
    L = x.shape[0]
    k = (-s) % L
    return x if k == 0 else pltpu.roll(x, k, axis=0)


def _r16(v):
    return v.astype(bf16).astype(f32)


def _conv_masks(L, is_ctx):
    t = lax.broadcasted_iota(jnp.int32, (L, 1), 0)
    if is_ctx:
        return t == L - 1, t == 0, None, None
    col = jnp.bitwise_and(t, GRID_W - 1)
    return col == GRID_W - 1, col == 0, t >= GRID_W, t < L - GRID_W


def _conv_sides(xv, masks):
    no_left, no_right, _, _ = masks
    return _shift(jnp.where(no_left, 0.0, xv), -1), _shift(jnp.where(no_right, 0.0, xv), 1)


def _conv_pre(xv, w_ref, masks, is_ctx):
    xv = _r16(xv)
    wv = _r16(w_ref[...])
    xl, xr = _conv_sides(xv, masks)
    z = [wv[3 * di:3 * di + 1, :] * xl + wv[3 * di + 1:3 * di + 2, :] * xv + wv[3 * di + 2:3 * di + 3, :] * xr
         for di in ((1,) if is_ctx else (0, 1, 2))]
    if is_ctx:
        return z[0]
    _, _, has_up, has_down = masks
    return z[1] + jnp.where(has_up, _shift(z[0], -GRID_W), 0.0) + jnp.where(has_down, _shift(z[2], GRID_W), 0.0)


def _conv_pre_bwd(xv, w_ref, dpre, masks, is_ctx, dw_ref):
    no_left, no_right, has_up, has_down = masks
    xv, dpre, wv = _r16(xv), _r16(dpre), _r16(w_ref[...])
    xl, xr = _conv_sides(xv, masks)
    if is_ctx:
        dz = {1: dpre}
    else:
        dz = {0: _shift(jnp.where(has_up, dpre, 0.0), GRID_W), 1: dpre, 2: _shift(jnp.where(has_down, dpre, 0.0), -GRID_W)}
    dxl = dxc = dxr = None
    for di, d in dz.items():
        for dj, side in enumerate((xl, xv, xr)):
            dw_ref[3 * di + dj:3 * di + dj + 1, :] = jnp.sum(d * side, axis=0, keepdims=True)
        tl, tc, tr = (wv[3 * di + dj:3 * di + dj + 1, :] * d for dj in range(3))
        dxl, dxc, dxr = (tl, tc, tr) if dxl is None else (dxl + tl, dxc + tc, dxr + tr)
    return dxc + jnp.where(no_left, 0.0, _shift(dxl, 1)) + jnp.where(no_right, 0.0, _shift(dxr, -1))


def _qk_post(pre, is_norm, scale):
    s = _silu(pre)
    nrm = lax.rsqrt(jnp.sum(s * s, axis=-1, keepdims=True) + NORM_EPS)
    return s * jnp.where(is_norm, nrm * scale, 1.0)


def _conv_tile(L):
    return D_GDN if L <= 512 else GDN_HEAD


def _conv_kind(W):
    head = pl.program_id(1) * (W // GDN_HEAD)
    return head < 2 * GDN_HEADS, jnp.where(head < GDN_HEADS, GDN_HEAD ** -0.5, 1.0).astype(f32)


def _conv_specs(L, W):
    spec = pl.BlockSpec((None, L, W), lambda b, ct: (b, 0, ct))
    wspec = pl.BlockSpec((CONV_ROWS, W), lambda b, ct: (0, ct))
    dwspec = pl.BlockSpec((None, CONV_ROWS, W), lambda b, ct: (b, 0, ct))
    return spec, wspec, dwspec


def conv_fwd(qkv, w16, *, is_ctx, name):
    B, L, C = qkv.shape
    W = _conv_tile(L)
    spec, wspec, _ = _conv_specs(L, W)

    def body(x_ref, w_ref, o_ref, pre_ref):
        is_norm, scale = _conv_kind(W)
        pre = _conv_pre(x_ref[...], w_ref, _conv_masks(L, is_ctx), is_ctx)
        pre_ref[...] = pre
        for h in range(W // GDN_HEAD):
            sl = slice(h * GDN_HEAD, (h + 1) * GDN_HEAD)
            o_ref[:, sl] = _qk_post(pre[:, sl], is_norm, scale)

    return pl.pallas_call(body, name=name, grid=(B, C // W), in_specs=[spec, wspec], out_specs=[spec, spec],
                          out_shape=[SDS((B, L, C), f32)] * 2, compiler_params=_cparams(2))(qkv, w16)


def conv_bwd(qkv, pre, w16, da0, da1, *, is_ctx, name):
    B, L, C = qkv.shape
    W = _conv_tile(L)
    spec, wspec, dwspec = _conv_specs(L, W)

    def body(x_ref, pre_ref, w_ref, d0_ref, d1_ref, dx_ref, dw_ref):
        is_norm, scale = _conv_kind(W)
        dpre = []
        for h in range(W // GDN_HEAD):
            sl = slice(h * GDN_HEAD, (h + 1) * GDN_HEAD)
            _, vjp = jax.vjp(lambda p: _qk_post(p, is_norm, scale), pre_ref[:, sl])
            dpre.append(vjp(d0_ref[:, sl] + d1_ref[:, sl])[0])
        dpre = dpre[0] if len(dpre) == 1 else jnp.concatenate(dpre, axis=1)
        dw_ref[...] = jnp.zeros_like(dw_ref)
        dx_ref[...] = _conv_pre_bwd(x_ref[...], w_ref, dpre, _conv_masks(L, is_ctx), is_ctx, dw_ref)

    return pl.pallas_call(body, name=name, grid=(B, C // W), in_specs=[spec, spec, wspec, spec, spec],
                          out_specs=[spec, dwspec], out_shape=[SDS((B, L, C), f32), SDS((B, CONV_ROWS, C), f32)],
                          compiler_params=_cparams(2))(qkv, pre, w16, da0, da1)


def _gates_fn(ba, alog, dtb):
    T = ba.shape[0]
    nck = T // CHUNK
    lane = lax.broadcasted_iota(jnp.int32, ba.shape, 1)
    ii = lax.broadcasted_iota(jnp.int32, (nck, CHUNK, CHUNK), 1)
    jj = lax.broadcasted_iota(jnp.int32, (nck, CHUNK, CHUNK), 2)
    g = jnp.where(lane >= 8, -jnp.exp(alog) * jax.nn.softplus(ba + dtb), 0.0)
    g3 = g.reshape(nck, CHUNK, N_GATE)
    chunk_sum = lambda tri: lax.dot_general(tri.astype(f32), g3, (((2,), (1,)), ((0,), (0,))),
                                            precision=lax.Precision.HIGHEST, preferred_element_type=f32).reshape(T, N_GATE)
    gc = jnp.where(lane >= 12, chunk_sum(ii <= jj), chunk_sum(ii >= jj))
    return jnp.where(lane < 8, jax.nn.sigmoid(ba), gc)


def gates_fwd(ba, alog, dtb, *, name):
    B, L, _ = ba.shape
    T = min(TOK_TILE, L)
    t = _tok(T, N_GATE)

    def body(ba_ref, al_ref, dt_ref, o_ref):
        o_ref[...] = _gates_fn(ba_ref[...], al_ref[...], dt_ref[...])

    return pl.pallas_call(body, name=name, grid=(B, L // T),
                          in_specs=[t, _resident((1, N_GATE)), _resident((1, N_GATE))], out_specs=t,
                          out_shape=SDS((B, L, N_GATE), f32), compiler_params=_cparams(2))(ba, alog, dtb)


def gates_bwd(ba, alog, dtb, dbg, *, name):
    B, L, _ = ba.shape
    T = min(TOK_TILE, L)
    t = _tok(T, N_GATE)
    small = _resident((1, N_GATE))

    def body(ba_ref, al_ref, dt_ref, d_ref, dba_ref, dal_ref, ddt_ref):
        @pl.when(_first_step())
        def _():
            dal_ref[...] = jnp.zeros_like(dal_ref)
            ddt_ref[...] = jnp.zeros_like(ddt_ref)

        _, vjp = jax.vjp(_gates_fn, ba_ref[...], al_ref[...], dt_ref[...])
        dba, dal, ddt = vjp(d_ref[...])
        dba_ref[...] = dba
        dal_ref[...] += dal
        ddt_ref[...] += ddt

    return pl.pallas_call(body, name=name, grid=(B, L // T), in_specs=[t, small, small, t],
                          out_specs=[t, small, small],
                          out_shape=[SDS((B, L, N_GATE), f32), SDS((1, N_GATE), f32), SDS((1, N_GATE), f32)],
                          compiler_params=_cparams(2))(ba, alog, dtb, dbg)


@jax.custom_vjp
def _inv_unit_tri(mats):
    n = mats[0].shape[0]
    eye = (lax.broadcasted_iota(jnp.int32, (n, n), 0) == lax.broadcasted_iota(jnp.int32, (n, n), 1)).astype(f32)
    xs = [eye - a for a in mats]
    sq = [_dot(a, a) for a in mats]
    ps = sq
    k = 2
    while k < n:
        xs = [x + _dot(x, p) for x, p in zip(xs, ps)]
        k *= 2
        if k < n:
            ps = [_dot(p, p) for p in ps]
    return tuple(_dot(p, x) - a for p, x, a in zip(sq, xs, mats))


def _inv_unit_tri_fwd(mats):
    ns = _inv_unit_tri(mats)
    return ns, ns


def _inv_unit_tri_bwd(ns, dns):
    ys = [dn + _dot_tn(nn, dn) for nn, dn in zip(ns, dns)]
    return (tuple(-(y + _dot_nt(y, nn)) for y, nn in zip(ys, ns)),)


_inv_unit_tri.defvjp(_inv_unit_tri_fwd, _inv_unit_tri_bwd)


@jax.custom_vjp
def _inv_unit_tri_saved(mats, saved):
    return saved


_inv_unit_tri_saved.defvjp(lambda mats, saved: (saved, saved),
                           lambda ns, dns: _inv_unit_tri_bwd(ns, dns) + (tuple(jnp.zeros_like(n) for n in ns),))


def _gdn_chunk(heads, *, revs, saved=None, with_n=False):
    n = heads[0][0].shape[0]
    ii = lax.broadcasted_iota(jnp.int32, (n, n), 0)
    jj = lax.broadcasted_iota(jnp.int32, (n, n), 1)
    row = lax.broadcasted_iota(jnp.int32, (n, 1), 0)
    lower = {False: ii >= jj, True: ii <= jj}
    strict = {False: ii > jj, True: ii < jj}
    last = {False: n - 1, True: 0}
    H = range(len(heads))
    q, k, v, beta, gc, gr, s = (list(t) for t in zip(*heads))
    decay = [jnp.where(lower[revs[h]], jnp.exp(jnp.where(lower[revs[h]], gc[h] - gr[h], 0.0)), 0.0) for h in H]
    kk = [_mm_nt(k[h], k[h]) for h in H]
    qk = [_mm_nt(q[h], k[h]) * decay[h] for h in H]
    qs = [_mm(q[h], s[h]) for h in H]
    a_mat = tuple(jnp.where(strict[revs[h]], beta[h] * kk[h] * decay[h], 0.0) for h in H)
    gamma = [jnp.exp(gc[h]) for h in H]
    g_last = [jnp.sum(jnp.where(row == last[revs[h]], gc[h], 0.0), axis=0, keepdims=True) for h in H]
    nmat = _inv_unit_tri(a_mat) if saved is None else _inv_unit_tri_saved(a_mat, saved)
    bv = [beta[h] * v[h] for h in H]
    bk = [(beta[h] * gamma[h]) * k[h] for h in H]
    u0 = [bv[h] + _mm(nmat[h], bv[h]) for h in H]
    w = [bk[h] + _mm(nmat[h], bk[h]) for h in H]
    k_out = [k[h] * jnp.exp(g_last[h] - gc[h]) for h in H]
    u = [u0[h] - _mm(w[h], s[h]) for h in H]
    o = [gamma[h] * qs[h] + _mm(qk[h], u[h]) for h in H]
    s_new = [jnp.exp(g_last[h]) * s[h] + _mm_tn(k_out[h], u[h]) for h in H]
    outs = tuple((o[h], s_new[h]) for h in H)
    return (outs, nmat) if with_n else outs


def _gdn_specs(B, nc, rev):
    def cidx(n):
        return (nc - 1 - n) if rev else n
    tok = lambda width: pl.BlockSpec((B, CHUNK, width), lambda n: (0, cidx(n), 0))
    rowspec = pl.BlockSpec((B, None, N_GATE, CHUNK), lambda n: (0, cidx(n), 0, 0))
    st = pl.BlockSpec((B, GDN_HEADS, GDN_HEAD, GDN_HEAD), lambda n: (0, 0, 0, 0))
    ck = pl.BlockSpec((B, None, GDN_HEADS, GDN_HEAD, GDN_HEAD), lambda n: (0, cidx(n), 0, 0, 0))
    nsp = pl.BlockSpec((B, None, GDN_HEADS, CHUNK, CHUNK), lambda n: (0, cidx(n), 0, 0, 0))
    return tok, rowspec, st, ck, nsp


def _gdn_head_args(qkv_ref, bg_ref, bgr_ref, b, d, h):
    col = d * GDN_HEADS + h
    q = qkv_ref[b, :, h * GDN_HEAD:(h + 1) * GDN_HEAD]
    k = qkv_ref[b, :, D_GDN + h * GDN_HEAD:D_GDN + (h + 1) * GDN_HEAD]
    v = qkv_ref[b, :, 2 * D_GDN + h * GDN_HEAD:2 * D_GDN + (h + 1) * GDN_HEAD]
    bgv = bg_ref[b]
    return q, k, v, bgv[:, col:col + 1], bgv[:, 8 + col:9 + col], bgr_ref[b][8 + col:9 + col, :]


def _gdn_chains(B):
    return [(d, b, h) for d in range(N_DIR) for b in range(B) for h in range(GDN_HEADS)]


def gdn_fwd(qkv, bg, bgr, s0s, *, need_o, name):
    B, L, _ = qkv.shape
    nc = L // CHUNK
    specs = [_gdn_specs(B, nc, d == 1) for d in range(N_DIR)]
    chains = _gdn_chains(B)
    state_shape = (B, GDN_HEADS, GDN_HEAD, GDN_HEAD)

    def body(*refs):
        ins = [refs[3 * d:3 * d + 3] for d in range(N_DIR)]
        s0_refs = refs[6:8]
        k = 8
        o_refs = refs[k:k + 2] if need_o else None
        k += 2 if need_o else 0
        ck_refs, n_refs, sf_refs, s_scrs = refs[k:k + 2], refs[k + 2:k + 4], refs[k + 4:k + 6], refs[k + 6:k + 8]
        n = pl.program_id(0)

        @pl.when(n == 0)
        def _():
            for d in range(N_DIR):
                s_scrs[d][...] = s0_refs[d][...]

        for d in range(N_DIR):
            ck_refs[d][...] = s_scrs[d][...]
        heads = tuple(_gdn_head_args(*ins[d], b, d, h) + (s_scrs[d][b, h],) for d, b, h in chains)
        outs, nmat = _gdn_chunk(heads, revs=tuple(d == 1 for d, _, _ in chains), with_n=True)
        for (d, b, h), (o, s_new), nn in zip(chains, outs, nmat):
            if need_o:
                o_refs[d][b, :, h * GDN_HEAD:(h + 1) * GDN_HEAD] = o
            s_scrs[d][b, h] = s_new
            n_refs[d][b, h] = nn

        @pl.when(n == nc - 1)
        def _():
            for d in range(N_DIR):
                sf_refs[d][...] = s_scrs[d][...]

    in_specs, out_o, out_ck, out_n, out_sf = [], [], [], [], []
    for tok, rowspec, st, ck, nsp in specs:
        in_specs += [tok(3 * D_GDN), tok(N_GATE), rowspec]
        out_o.append(tok(D_GDN))
        out_ck.append(ck)
        out_n.append(nsp)
        out_sf.append(st)
    in_specs += [specs[0][2]] * 2
    out_specs = (out_o if need_o else []) + out_ck + out_n + out_sf
    out_shape = (([SDS((B, L, D_GDN), f32)] * 2 if need_o else []) + [SDS((B, nc) + state_shape[1:], f32)] * 2
                 + [SDS((B, nc, GDN_HEADS, CHUNK, CHUNK), f32)] * 2 + [SDS(state_shape, f32)] * 2)
    res = pl.pallas_call(
        body, name=name, grid=(nc,), in_specs=in_specs, out_specs=out_specs, out_shape=out_shape,
        scratch_shapes=[pltpu.VMEM(state_shape, f32)] * 2, compiler_params=_cparams(1),
    )(qkv, bg, bgr, qkv, bg, bgr, *s0s)
    if need_o:
        return res[0:2], res[2:4], res[4:6], res[6:8]
    return res[0:2], res[2:4], res[4:6]


def gdn_bwd(qkv, bg, bgr, cks, ns, do, dsfs, *, name):
    B, L, _ = qkv.shape
    nc = L // CHUNK
    has_do = do is not None
    specs = [_gdn_specs(B, nc, d != 1) for d in range(N_DIR)]
    chains = _gdn_chains(B)
    state_shape = (B, GDN_HEADS, GDN_HEAD, GDN_HEAD)
    per_dir = 6 if has_do else 5

    def body(*refs):
        ins = [refs[per_dir * d:per_dir * d + per_dir] for d in range(N_DIR)]
        k = per_dir * N_DIR
        dsf_refs = refs[k:k + 2]
        outs = [refs[k + 2 + 3 * d:k + 5 + 3 * d] for d in range(N_DIR)]
        ds0_refs, ds_scrs = refs[k + 8:k + 10], refs[k + 10:k + 12]
        n = pl.program_id(0)

        @pl.when(n == 0)
        def _():
            for d in range(N_DIR):
                ds_scrs[d][...] = dsf_refs[d][...]

        lane = lax.broadcasted_iota(jnp.int32, (CHUNK, N_GATE), 1)
        sub = lax.broadcasted_iota(jnp.int32, (N_GATE, CHUNK), 0)
        heads = tuple(_gdn_head_args(*ins[d][:3], b, d, h) + (ins[d][3][b, h],) for d, b, h in chains)
        saved = tuple(ins[d][4][b, h] for d, b, h in chains)
        _, vjp = jax.vjp(functools.partial(_gdn_chunk, revs=tuple(d == 1 for d, _, _ in chains), saved=saved), heads)
        zero = jnp.zeros((CHUNK, GDN_HEAD), f32)
        cts = tuple(((ins[d][5][b, :, h * GDN_HEAD:(h + 1) * GDN_HEAD] if has_do else zero), ds_scrs[d][b, h])
                    for d, b, h in chains)
        (dheads,) = vjp(cts)
        dbg_acc = [[jnp.zeros((CHUNK, N_GATE), f32) for _ in range(B)] for _ in range(N_DIR)]
        dbgr_acc = [[jnp.zeros((N_GATE, CHUNK), f32) for _ in range(B)] for _ in range(N_DIR)]
        for (d, b, h), (dq, dk, dv, db, dgc, dgr, ds) in zip(chains, dheads):
            col = d * GDN_HEADS + h
            dqkv_ref = outs[d][0]
            dqkv_ref[b, :, h * GDN_HEAD:(h + 1) * GDN_HEAD] = dq
            dqkv_ref[b, :, D_GDN + h * GDN_HEAD:D_GDN + (h + 1) * GDN_HEAD] = dk
            dqkv_ref[b, :, 2 * D_GDN + h * GDN_HEAD:2 * D_GDN + (h + 1) * GDN_HEAD] = dv
            dbg_acc[d][b] = dbg_acc[d][b] + jnp.where(lane == col, db, 0.0) + jnp.where(lane == 8 + col, dgc, 0.0)
            dbgr_acc[d][b] = dbgr_acc[d][b] + jnp.where(sub == 8 + col, dgr, 0.0)
            ds_scrs[d][b, h] = ds
        for d in range(N_DIR):
            for b in range(B):
                outs[d][1][b] = dbg_acc[d][b]
                outs[d][2][b] = dbgr_acc[d][b]

        @pl.when(n == nc - 1)
        def _():
            for d in range(N_DIR):
                ds0_refs[d][...] = ds_scrs[d][...]

    in_specs, args, out_specs, out_shape = [], [], [], []
    for d, (tok, rowspec, st, ck, nsp) in enumerate(specs):
        in_specs += [tok(3 * D_GDN), tok(N_GATE), rowspec, ck, nsp] + ([tok(D_GDN)] if has_do else [])
        args += [qkv, bg, bgr, cks[d], ns[d]] + ([do] if has_do else [])
        out_specs += [tok(3 * D_GDN), tok(N_GATE), rowspec]
        out_shape += [SDS((B, L, 3 * D_GDN), f32), SDS((B, L, N_GATE), f32), SDS((B, nc, N_GATE, CHUNK), f32)]
    st = specs[0][2]
    in_specs += [st, st]
    args += list(dsfs)
    out_specs += [st, st]
    out_shape += [SDS(state_shape, f32)] * 2
    res = pl.pallas_call(
        body, name=name, grid=(nc,), in_specs=in_specs, out_specs=out_specs, out_shape=out_shape,
        scratch_shapes=[pltpu.VMEM(state_shape, f32)] * 2, compiler_params=_cparams(1),
    )(*args)
    return (res[0], res[3]), (res[1], res[4]), (res[2], res[5]), (res[6], res[7])


def _gnorm_fn(o0, o1, z, w):
    o = o0 + o1
    return o * lax.rsqrt(jnp.mean(o * o, axis=-1, keepdims=True) + NORM_EPS) * w * _silu(z)


def _head_loss(y, x, gate, lng, lnb, tgt):
    r = DEEPNORM_ALPHA * x + gate * y
    mu = jnp.mean(r, axis=-1, keepdims=True)
    rc = r - mu
    var = jnp.mean(rc * rc, axis=-1, keepdims=True)
    err = rc * lax.rsqrt(var + LN_EPS) * lng + lnb - tgt
    return (0.5 / D_MODEL) * jnp.sum(jnp.sum(err * err, axis=-1, keepdims=True), axis=0, keepdims=True)


def tail_fwd_bwd(u, y0, y1, z_s5, o0, o1, z_gdn, x, tgt, gate, lng, lnb, ws, wg, dsk, wglu, bglu, nw):
    B, L, _ = x.shape
    T = min(TOK_TILE, L)

    def body(u_ref, y0_ref, y1_ref, z_ref, o0_ref, o1_ref, zg_ref, x_ref, t_ref, gate_ref, lng_ref, lnb_ref, ws_ref,
             wg_ref, dsk_ref, wglu_ref, bglu_ref, nw_ref,
             loss_ref, du_ref, dys_ref, dz_ref, do_ref, dzg_ref, gx_ref, dws_ref, dwg_ref, dgate_ref, dlng_ref, dlnb_ref,
             ddsk_ref, dwglu_ref, dbglu_ref, dnw_ref):
        n = pl.program_id(1)

        @pl.when(_first_step())
        def _():
            for r in (dws_ref, dwg_ref, dlng_ref, dlnb_ref, ddsk_ref, dwglu_ref, dbglu_ref, dnw_ref):
                r[...] = jnp.zeros_like(r)

        @pl.when(n == 0)
        def _():
            loss_ref[...] = jnp.zeros_like(loss_ref)
            dgate_ref[...] = jnp.zeros_like(dgate_ref)

        s5o, glu_vjp = jax.vjp(_glu_fn, u_ref[...], y0_ref[...], y1_ref[...], z_ref[...], dsk_ref[...],
                               wglu_ref[...].astype(f32), bglu_ref[...])
        heads = []
        for h in range(GDN_HEADS):
            sl = slice(h * GDN_HEAD, (h + 1) * GDN_HEAD)
            heads.append(jax.vjp(_gnorm_fn, o0_ref[:, sl], o1_ref[:, sl], zg_ref[:, sl], nw_ref[...]))
        sv = s5o.astype(bf16)
        gv = jnp.concatenate([out for out, _ in heads], axis=1).astype(bf16)
        y = _dot(sv, ws_ref[...]) + _dot(gv, wg_ref[...])
        loss, vjp = jax.vjp(lambda *a: _head_loss(*a, t_ref[...]), y, x_ref[...], gate_ref[...], lng_ref[...],
                            lnb_ref[...])
        dy, dx, dgate, dlng, dlnb = vjp(jnp.ones((1, 1), f32))
        loss_ref[...] += jnp.broadcast_to(loss, loss_ref.shape)
        dyb = dy.astype(bf16)
        gx_ref[...] = dx
        dws_ref[...] += _dot_tn(sv, dyb)
        dwg_ref[...] += _dot_tn(gv, dyb)
        dgate_ref[...] += dgate
        dlng_ref[...] += dlng
        dlnb_ref[...] += dlnb
        du, dys, _, dz, ddsk, dwglu, dbglu = glu_vjp(_dot_nt(dyb, ws_ref[...]))
        du_ref[...], dys_ref[...], dz_ref[...] = du, dys, dz
        ddsk_ref[...] += ddsk
        dwglu_ref[...] += dwglu
        dbglu_ref[...] += dbglu
        dgdo = _dot_nt(dyb, wg_ref[...])
        for h, (_, hvjp) in enumerate(heads):
            sl = slice(h * GDN_HEAD, (h + 1) * GDN_HEAD)
            do, _, dzg, dnw = hvjp(dgdo[:, sl])
            do_ref[:, sl] = do
            dzg_ref[:, sl] = dzg
            dnw_ref[...] += dnw

    half, full = _tok(T, D_S5), _tok(T, D_MODEL)
    row = _resident((1, D_MODEL))
    wsp = _resident((D_S5, D_MODEL))
    r512, rglu, r128 = _resident((1, D_S5)), _resident((D_S5, D_S5)), _resident((1, GDN_HEAD))
    return pl.pallas_call(
        body, name="tail_fwd_bwd", grid=(B, L // T),
        in_specs=[half] * 7 + [full, full, _per_batch(1, D_MODEL), row, row, wsp, wsp, r512, rglu, r512, r128],
        out_specs=[_per_batch(8, LANES)] + [half] * 5 + [full, wsp, wsp, _per_batch(1, D_MODEL), row, row, r512, rglu, r512,
                                                           r128],
        out_shape=[SDS((B, 8, LANES), f32)] + [SDS((B, L, D_S5), f32)] * 5 + [
            SDS((B, L, D_MODEL), f32), SDS((D_S5, D_MODEL), f32), SDS((D_GDN, D_MODEL), f32), SDS((B, 1, D_MODEL), f32),
            SDS((1, D_MODEL), f32), SDS((1, D_MODEL), f32), SDS((1, D_S5), f32), SDS((D_S5, D_S5), f32), SDS((1, D_S5), f32),
            SDS((1, GDN_HEAD), f32)],
        compiler_params=_cparams(2),
    )(u, y0, y1, z_s5, o0, o1, z_gdn, x, tgt, gate, lng, lnb, ws, wg, dsk, wglu, bglu, nw)


def _adamw_math(w, g, m, v):
    nm = ADAM_B1 * m + (1.0 - ADAM_B1) * g
    nv = ADAM_B2 * v + (1.0 - ADAM_B2) * jnp.square(g)
    m_hat = nm / (1.0 - ADAM_B1 ** ADAM_STEP)
    v_hat = nv / (1.0 - ADAM_B2 ** ADAM_STEP)
    return -ADAM_LR * (m_hat / (jnp.sqrt(v_hat) + ADAM_EPS) + ADAM_WD * w), nm, nv


def _row_tile(rows, cap=512):
    for t in range(min(cap, rows), 15, -1):
        if rows % t == 0 and t % 16 == 0:
            return t
    return rows


def adamw_3d(w, g, m, v, *, lead=False, name):
    R, C = (w.shape[0], w.shape[2]) if lead else w.shape[1:]
    if lead:
        T = next(t for t in range(min(256, R), 0, -1) if R % t == 0)
        spec = pl.BlockSpec((T, 1, C), lambda i: (i, 0, 0))
    else:
        T = _row_tile(R)
        spec = pl.BlockSpec((None, T, C), lambda i: (0, i, 0))

    def body(w_ref, g_ref, m_ref, v_ref, d_ref, nm_ref, nv_ref):
        d_ref[...], nm_ref[...], nv_ref[...] = _adamw_math(w_ref[...], g_ref[...], m_ref[...], v_ref[...])

    return pl.pallas_call(body, name=name, grid=(R // T,), in_specs=[spec] * 4, out_specs=[spec] * 3,
                          out_shape=[SDS(w.shape, f32)] * 3, compiler_params=_cparams(1))(w, g, m, v)


def adamw_small(ws, gs, ms, vs):
    n = len(ws)

    def body(*refs):
        outs = refs[4 * n:]
        for i in range(n):
            d, nm, nv = _adamw_math(refs[i][...], refs[n + i][...], refs[2 * n + i][...], refs[3 * n + i][...])
            outs[i][...], outs[n + i][...], outs[2 * n + i][...] = d, nm, nv

    res = pl.pallas_call(body, name="adamw_small", out_shape=[SDS(w.shape, f32) for w in ws] * 3,
                         compiler_params=pltpu.CompilerParams(vmem_limit_bytes=VMEM_LIMIT))(*ws, *gs, *ms, *vs)
    return res[:n], res[n:2 * n], res[2 * n:]


def sum_cores(own, got, *, name):
    A, H, C = own.shape
    T = _row_tile(H)
    spec = pl.BlockSpec((None, T, C), lambda a, i: (a, i, 0))

    def body(a_ref, b_ref, q32_ref, q16_ref):
        q = a_ref[...] + b_ref[...]
        q32_ref[...] = q
        q16_ref[...] = q.astype(bf16)

    return pl.pallas_call(body, name=name, grid=(A, H // T), in_specs=[spec, spec], out_specs=[spec, spec],
                          out_shape=[SDS((A, H, C), f32), SDS((A, H, C), bf16)], compiler_params=_cparams(2))(own, got)


def sum_chips(mine, rec, cpos, full, *, slot=None, name):
    H, C = mine.shape
    T = _row_tile(H)
    nt = H // T
    by_rows = _by_rows(full)
    out_idx = lambda i, s_ref: ((s_ref[1], s_ref[0] * nt + i, 0) if by_rows else (s_ref[1], i, s_ref[0]))

    def body(s_ref, m_ref, r_ref, f_ref):
        f_ref[...] = ((m_ref[...] + r_ref[0].astype(f32)) + r_ref[1].astype(f32)) + r_ref[2].astype(f32)

    grid_spec = pltpu.PrefetchScalarGridSpec(
        num_scalar_prefetch=1, grid=(nt,),
        in_specs=[pl.BlockSpec((T, C), lambda i, s_ref: (i, 0)), pl.BlockSpec((3, T, C), lambda i, s_ref: (0, i, 0))],
        out_specs=pl.BlockSpec((None, T, C), out_idx))
    scalars = jnp.stack([cpos, jnp.zeros_like(cpos) if slot is None else slot]).astype(jnp.int32)
    return pl.pallas_call(body, name=name, grid_spec=grid_spec,
                          out_shape=SDS((1 if slot is None else 4,) + tuple(full), f32),
                          compiler_params=_cparams(1))(scalars, mine, rec)


CHIP_FLIPS = ((1, 0), (0, 1), (1, 1))


def _pos():
    return lax.axis_index("x"), lax.axis_index("y"), lax.axis_index("c")


def _comm_call(body, srcs, out_sds, n_remote, n_local, name):
    any_spec = pl.BlockSpec(memory_space=pl.ANY)
    return pl.pallas_call(
        body, name=name, in_specs=[any_spec] * len(srcs), out_specs=[any_spec] * len(out_sds), out_shape=out_sds,
        scratch_shapes=[pltpu.SemaphoreType.DMA((n_remote,)), pltpu.SemaphoreType.DMA((n_remote,)),
                        pltpu.SemaphoreType.DMA((max(n_local, 1),))],
        compiler_params=pltpu.CompilerParams(has_side_effects=True),
    )(*srcs)


def _remote(src, dst, send_sems, recv_sems, k, target):
    return pltpu.make_async_remote_copy(src, dst, send_sems.at[k], recv_sems.at[k], device_id=target,
                                        device_id_type=MESH)


def _by_rows(shape):
    return shape[0] % 16 == 0


def _half_shape(shape):
    return (shape[0] // 2, shape[1]) if _by_rows(shape) else (shape[0], shape[1] // 2)


def _half_of(ref, lead, c, shape):
    if _by_rows(shape):
        half = shape[0] // 2
        return ref.at[(*lead, pl.ds(pl.multiple_of(c * half, 8), half))]
    half = shape[1] // 2
    return ref.at[(*lead, slice(None), pl.ds(pl.multiple_of(c * half, LANES), half))]


def gather_shards(shards):
    nt = len(shards)

    def body(*refs):
        srcs, outs = refs[:nt], refs[nt:2 * nt]
        send_sems, recv_sems, _ = refs[2 * nt:]
        x, y, c = _pos()
        j = 2 * x + y
        sib = (x, y, 1 - c)
        own = [_remote(srcs[t], outs[t].at[j], send_sems, recv_sems, 7 * t + 6, sib) for t in range(nt)]
        first, passed = [], []
        for k, (fx, fy) in enumerate(CHIP_FLIPS):
            tx, ty = x ^ fx, y ^ fy
            jk = 2 * tx + ty
            for t in range(nt):
                sh = srcs[t].shape
                first.append(_remote(_half_of(srcs[t], (), c, sh), _half_of(outs[t], (j,), c, sh), send_sems, recv_sems,
                                     7 * t + k, (tx, ty, c)))
                landed = _half_of(outs[t], (jk,), c, sh)
                passed.append(_remote(landed, landed, send_sems, recv_sems, 7 * t + 3 + k, sib))
        for cp in first + own:
            cp.start()
        for a, b in zip(first, passed):
            a.wait_recv()
            b.start()
        for cp in passed + own:
            cp.wait_recv()
        for cp in first + passed + own:
            cp.wait_send()

    return _comm_call(body, shards, [SDS((4,) + s.shape, s.dtype) for s in shards], 7 * nt, 0, "gather_shards")


def swap_halves(ps):
    nt = len(ps)

    def body(*refs):
        srcs, outs = refs[:nt], refs[nt:2 * nt]
        send_sems, recv_sems, _ = refs[2 * nt:]
        x, y, c = _pos()
        cps = [_remote(_half_of(srcs[t], (a,), 1 - c, srcs[t].shape[1:]), outs[t].at[a], send_sems, recv_sems, 4 * t + a,
                       (x, y, 1 - c)) for t in range(nt) for a in range(4)]
        for cp in cps:
            cp.start()
        for cp in cps:
            cp.wait()

    return _comm_call(body, ps, [SDS((4,) + _half_shape(p.shape[1:]), p.dtype) for p in ps], 4 * nt, 0, "swap_halves")


def scatter_to_chips(qs):
    nt = len(qs)

    def body(*refs):
        srcs, outs = refs[:nt], refs[nt:2 * nt]
        send_sems, recv_sems, _ = refs[2 * nt:]
        x, y, c = _pos()
        cps = []
        for k, (fx, fy) in enumerate(CHIP_FLIPS):
            tx, ty = x ^ fx, y ^ fy
            for t in range(nt):
                cps.append(_remote(srcs[t].at[2 * tx + ty], outs[t].at[k], send_sems, recv_sems, 3 * t + k, (tx, ty, c)))
        for cp in cps:
            cp.start()
        for cp in cps:
            cp.wait()

    return _comm_call(body, qs, [SDS((3,) + q.shape[1:], q.dtype) for q in qs], 3 * nt, 0, "scatter_to_chips")


DEV_FLIPS = tuple((fx, fy, fc) for fx in (0, 1) for fy in (0, 1) for fc in (0, 1))[1:]


def join_halves(fs, small):
    nt = len(fs)

    def body(*refs):
        outs = refs[nt + 1:2 * nt + 1]
        sm = refs[2 * nt + 1]
        send_sems, recv_sems, _ = refs[2 * nt + 2:]
        x, y, c = _pos()
        cps = []
        for t in range(nt):
            mine = _half_of(outs[t], (0,), c, outs[t].shape[1:])
            cps.append(_remote(mine, mine, send_sems, recv_sems, t, (x, y, 1 - c)))
        mine = _half_of(sm, (2 * x + y,), c, sm.shape[1:])
        for k, (fx, fy, fc) in enumerate(DEV_FLIPS):
            cps.append(_remote(mine, mine, send_sems, recv_sems, nt + k, (x ^ fx, y ^ fy, c ^ fc)))
        for cp in cps:
            cp.start()
        for cp in cps:
            cp.wait()

    any_spec = pl.BlockSpec(memory_space=pl.ANY)
    n_sem = nt + len(DEV_FLIPS)
    res = pl.pallas_call(
        body, name="join_halves", in_specs=[any_spec] * (nt + 1), out_specs=[any_spec] * (nt + 1),
        out_shape=[SDS(f.shape, f.dtype) for f in fs] + [SDS(small.shape, small.dtype)],
        input_output_aliases={t: t for t in range(nt + 1)},
        scratch_shapes=[pltpu.SemaphoreType.DMA((n_sem,)), pltpu.SemaphoreType.DMA((n_sem,)), pltpu.SemaphoreType.DMA((1,))],
        compiler_params=pltpu.CompilerParams(has_side_effects=True),
    )(*fs, small)
    return res[:nt], res[nt]


def gather_devices(block, *, name):
    def body(src, out, send_sems, recv_sems, loc_sems):
        x, y, c = _pos()
        me = 4 * x + 2 * y + c
        mine = pltpu.make_async_copy(src, out.at[me], loc_sems.at[0])
        mine.start()
        cps = [_remote(src, out.at[me], send_sems, recv_sems, k, (x ^ fx, y ^ fy, c ^ fc))
               for k, (fx, fy, fc) in enumerate(DEV_FLIPS)]
        for cp in cps:
            cp.start()
        for cp in cps:
            cp.wait()
        mine.wait()

    return _comm_call(body, [block], [SDS((N_DEV,) + block.shape, block.dtype)], 7, 1, name)[0]


def exchange_devices(blocks, *, name):
    def body(src, out, send_sems, recv_sems, loc_sems):
        x, y, c = _pos()
        me = 4 * x + 2 * y + c
        mine = pltpu.make_async_copy(src.at[me], out.at[me], loc_sems.at[0])
        mine.start()
        cps = []
        for k, (fx, fy, fc) in enumerate(DEV_FLIPS):
            tx, ty, tc = x ^ fx, y ^ fy, c ^ fc
            cps.append(_remote(src.at[4 * tx + 2 * ty + tc], out.at[me], send_sems, recv_sems, k, (tx, ty, tc)))
        for cp in cps:
            cp.start()
        for cp in cps:
            cp.wait()
        mine.wait()

    return _comm_call(body, [blocks], [SDS(blocks.shape, blocks.dtype)], 7, 1, name)[0]


SMALL_SHAPES = ((1, 2, 32, 64), (1, 2, 32, 64), (1, 2, 32), (1, 2, 32, 16, 64),
                (1, 2, 32, 16, 64), (1, 2, 32, 16, 64), (1, 2, 32, 16, 64), (1, D_S5), (1, D_S5), (1, 2, 4), (1, 2, 4),
                (1, GDN_HEAD), (1, D_MODEL), (1, D_MODEL), (LANES,))
SMALL_SWAPPED = (3, 4)


def _size(shape):
    return functools.reduce(lambda p, q: p * q, shape)


SMALL_ROWS = tuple(-(-_size(s) // (8 * LANES)) * 8 for s in SMALL_SHAPES)
SMALL_TOTAL = 2240
SMALL_QUARTER = SMALL_TOTAL // 4


def _rows(a):
    flat = a.reshape(-1)
    pad = (-flat.shape[0]) % (8 * LANES)
    if pad:
        flat = jnp.concatenate([flat, jnp.zeros((pad,), flat.dtype)])
    return flat.reshape(-1, LANES)


def _pack_small(parts):
    rows = [_rows(p) for p in parts]
    rows.append(jnp.zeros((SMALL_TOTAL - sum(SMALL_ROWS), LANES), f32))
    return jnp.concatenate(rows, axis=0)


def _unpack_small(buf):
    out, r = [], 0
    for s, n in zip(SMALL_SHAPES, SMALL_ROWS):
        out.append(buf[r:r + n].reshape(-1)[:_size(s)].reshape(s))
        r += n
    return out


def _as_2d(a):
    return a.reshape(1, -1) if a.ndim == 1 else a.reshape(-1, a.shape[-1])


S5_BG = S5_GROUPS // S5_BLOCKS


def _block_diag_in(bb):
    lead = bb.shape[:-2]
    eye = jnp.eye(S5_BG, dtype=bb.dtype)
    b4 = bb.reshape(lead + (S5_BLOCKS, S5_BG, S5_GROUP, S5_STATE))
    return jnp.einsum('...jgcp,gh->...jgchp', b4, eye).reshape(lead + (S5_BLOCKS, S5_BC, S5_BS))


def _block_diag_in_t(d):
    lead = d.shape[:-3]
    d6 = d.reshape(lead + (S5_BLOCKS, S5_BG, S5_GROUP, S5_BG, S5_STATE))
    return jnp.einsum('...jgcgp->...jgcp', d6).reshape(lead + (S5_GROUPS, S5_GROUP * S5_STATE))


def _block_diag_out(cm):
    lead = cm.shape[:-3]
    eye = jnp.eye(S5_BG, dtype=cm.dtype)
    c4 = cm.reshape(lead + (S5_BLOCKS, S5_BG, S5_GROUP, S5_STATE))
    return jnp.einsum('...jgcp,gh->...jhpgc', c4, eye).reshape(lead + (S5_BLOCKS, S5_BS, S5_BC))


def _block_diag_out_t(d):
    lead = d.shape[:-3]
    d6 = d.reshape(lead + (S5_BLOCKS, S5_BG, S5_STATE, S5_BG, S5_GROUP))
    return jnp.einsum('...jgpgc->...jgcp', d6).reshape(lead + (S5_GROUPS, S5_GROUP, S5_STATE))


def _to_chunk_rows(a):
    B, L, W = a.shape
    return a.reshape(B, L // CHUNK, CHUNK, W).transpose(0, 1, 3, 2)


def _from_chunk_rows(a):
    B, nc, W, _ = a.shape
    return a.transpose(0, 1, 3, 2).reshape(B, nc * CHUNK, W)


def local_step(x, ctx, tgt, m, w_in, lam_re, lam_im, log_dt, b_re, b_im, c_re, c_im, s5_d,
               w_glu, b_glu, conv16, a_log, dt_bias, norm_w, w_out, ln_g, ln_b):
    B, L, _ = x.shape
    zeros_state = jnp.zeros((B, GDN_HEADS, GDN_HEAD, GDN_HEAD), f32)

    shift, scale, gate = m[:B, :D_MODEL], m[:B, D_MODEL:2 * D_MODEL], m[:B, 2 * D_MODEL:]
    mod = jnp.stack([scale, shift], axis=1)
    mod_c = jnp.broadcast_to(jnp.stack([m[B, D_MODEL:2 * D_MODEL], m[B, :D_MODEL]], axis=0)[None], (B, 2, D_MODEL))

    u, z_s5, qkv, z_gdn, ba = in_proj_fwd(x, mod, w_in, name="in_proj_fwd")
    uc, _, qkvc, _, bac = in_proj_fwd(ctx, mod_c, w_in, name="in_proj_fwd_ctx")

    ng = N_DIR * S5_GROUPS
    zoh_in = (lam_re.reshape(ng, S5_STATE), lam_im.reshape(ng, S5_STATE), log_dt.reshape(ng, 1),
              b_re.reshape(ng, S5_GROUP * S5_STATE), b_im.reshape(ng, S5_GROUP * S5_STATE))
    expand = (jnp.arange(S5_GROUP * S5_STATE)[None, :] % S5_STATE == jnp.arange(S5_STATE)[:, None]).astype(f32)
    ar, ai, bbr, bbi = s5_zoh_fwd(*zoh_in, expand)
    b_blocks = _block_diag_in(jnp.stack([bbr, bbi]).astype(bf16).reshape(2, N_DIR, S5_GROUPS, S5_GROUP * S5_STATE))
    c_blocks = _block_diag_out(jnp.stack([c_re, -c_im]).astype(bf16).reshape(2, N_DIR, S5_GROUPS, S5_GROUP, S5_STATE))
    a_rows = jnp.stack([ar, ai]).reshape(2, N_DIR, S5_HALF)
    s5w, ys, hins, hins_c, hss, hss_c = [], [], [], [], [], []
    for d in range(N_DIR):
        wd = (b_blocks[0, d], b_blocks[1, d], c_blocks[0, d], c_blocks[1, d], a_rows[:, d])
        s5w.append(wd)
        hs_c, hin_c, hend_c = s5_scan_fwd(uc, *wd, jnp.zeros((B, 2, S5_HALF), f32), d=d, need_y=False,
                                          name=f"s5_fwd_ctx{d}")
        y_d, hs_d, hin, _ = s5_scan_fwd(u, *wd, hend_c, d=d, need_y=True, name=f"s5_fwd{d}")
        hss.append(hs_d)
        hss_c.append(hs_c)
        ys.append(y_d)
        hins.append(hin)
        hins_c.append(hin_c)
    glu_w = (s5_d.reshape(1, D_S5), w_glu, b_glu.reshape(1, D_S5))

    act, pre = conv_fwd(qkv, conv16, is_ctx=False, name="conv_fwd")
    act_c, pre_c = conv_fwd(qkvc, conv16, is_ctx=True, name="conv_fwd_ctx")
    pad8 = jnp.zeros((1, 8), f32)
    alog16 = jnp.concatenate([pad8, a_log.reshape(1, 8)], axis=1)
    dtb16 = jnp.concatenate([pad8, dt_bias.reshape(1, 8)], axis=1)
    bg = gates_fwd(ba, alog16, dtb16, name="gates_fwd")
    bg_c = gates_fwd(bac, alog16, dtb16, name="gates_fwd_ctx")
    bgr, bgr_c = _to_chunk_rows(bg), _to_chunk_rows(bg_c)
    cks_c, ns_c, s_c = gdn_fwd(act_c, bg_c, bgr_c, (zeros_state, zeros_state), need_o=False, name="gdn_fwd_ctx")
    os_, cks, ns, _ = gdn_fwd(act, bg, bgr, s_c, need_o=True, name="gdn_fwd")
    nw = norm_w.reshape(1, GDN_HEAD)

    (loss8, du_skip, dy, dz_s5, do, dz_gdn, gx_res, dws, dwg, dgate, dlng, dlnb, d_s5_d, d_w_glu, d_b_glu,
     d_norm_w) = tail_fwd_bwd(u, ys[0], ys[1], z_s5, os_[0], os_[1], z_gdn, x, tgt, gate[:, None, :],
                              ln_g.reshape(1, D_MODEL), ln_b.reshape(1, D_MODEL), w_out[:D_S5], w_out[D_S5:], *glu_w, nw)
    loss = jnp.sum(loss8[:, 0, 0])
    d_w_out = jnp.concatenate([dws, dwg], axis=0)

    dacts, dbgs, dbgrs, ds0s = gdn_bwd(act, bg, bgr, cks, ns, do, (zeros_state, zeros_state), name="gdn_bwd")
    dacts_c, dbgs_c, dbgrs_c, _ = gdn_bwd(act_c, bg_c, bgr_c, cks_c, ns_c, None, ds0s, name="gdn_bwd_ctx")
    dbg = dbgs[0] + dbgs[1] + _from_chunk_rows(dbgrs[0] + dbgrs[1])
    dbg_c = dbgs_c[0] + dbgs_c[1] + _from_chunk_rows(dbgrs_c[0] + dbgrs_c[1])
    dba, dal, ddt = gates_bwd(ba, alog16, dtb16, dbg, name="gates_bwd")
    dbac, dal_c, ddt_c = gates_bwd(bac, alog16, dtb16, dbg_c, name="gates_bwd_ctx")
    d_a_log = (dal + dal_c)[:, 8:].reshape(1, N_DIR, GDN_HEADS)
    d_dt_bias = (ddt + ddt_c)[:, 8:].reshape(1, N_DIR, GDN_HEADS)
    dqkv, dcw = conv_bwd(qkv, pre, conv16, dacts[0], dacts[1], is_ctx=False, name="conv_bwd")
    dqkvc, dcw_c = conv_bwd(qkvc, pre_c, conv16, dacts_c[0], dacts_c[1], is_ctx=True, name="conv_bwd_ctx")
    d_conv16 = jnp.sum(dcw, axis=0) + jnp.sum(dcw_c, axis=0)

    dus, ducs = [du_skip], []
    das, dbs, dcs = [], [], []
    for d in range(N_DIR):
        du_d, dbre1, dbim1, dct1, dcb1, da1, dh0 = s5_scan_bwd(u, dy, hss[d], *s5w[d], hins[d],
                                                                jnp.zeros((B, 2, S5_HALF), f32), d=d, name=f"s5_bwd{d}")
        duc_d, dbre2, dbim2, _, _, da2, _ = s5_scan_bwd(uc, None, hss_c[d], *s5w[d], hins_c[d], dh0, d=d,
                                                        name=f"s5_bwd_ctx{d}")
        dus.append(du_d)
        ducs.append(duc_d)
        das.append(da1 + da2)
        dbs.append(jnp.stack([dbre1 + dbre2, dbim1 + dbim2]))
        dcs.append(jnp.stack([dct1, dcb1]))
    ng_shape = (N_DIR * S5_GROUPS, -1)
    da = jnp.stack(das, axis=1)
    db = _block_diag_in_t(jnp.stack(dbs, axis=1))
    dc = _block_diag_out_t(jnp.stack(dcs, axis=1))
    dlr, dli, dldt, dbre, dbim = s5_zoh_bwd(*zoh_in, expand, da[0].reshape(ng_shape), da[1].reshape(ng_shape),
                                            db[0].reshape(ng_shape), db[1].reshape(ng_shape))
    d_s5 = (dlr, dli, dldt, dbre, dbim, dc[0], -dc[1])

    zc = jnp.zeros_like(uc)
    dw_c, dmod_c = in_proj_bwd(ctx, mod_c, (tuple(ducs), zc, dqkvc, zc, dbac), w_in, None, None,
                               name="in_proj_bwd_ctx")
    d_w_in, dmod, grad_x = in_proj_bwd(x, mod, (tuple(dus), dz_s5, dqkv, dz_gdn, dba), w_in, gx_res, dw_c,
                                       name="in_proj_bwd")
    dmod_c = jnp.sum(dmod_c, axis=0)

    dm_rows = jnp.concatenate([dmod[:, 1], dmod[:, 0], dgate[:, 0]], axis=1)
    dm_ctx = jnp.concatenate([dmod_c[1], dmod_c[0], jnp.zeros((D_MODEL,), f32)])[None]
    dm = jnp.concatenate([dm_rows, dm_ctx], axis=0)
    small = (*d_s5, d_s5_d, d_b_glu, d_a_log, d_dt_bias, d_norm_w, dlng, dlnb)
    small = tuple(g.reshape(s) for g, s in zip(small, SMALL_SHAPES))
    return loss, grad_x, (d_w_in, d_w_out, d_w_glu, d_conv16), small, dm


SHARDED = (1, 3, 18, 12, 14)
UNSHARDED = tuple(i for i in range(21) if i not in SHARDED)
SMALL = tuple(i for i in UNSHARDED if i not in (0, 2))
W_IN_SHARD = 772


def _conv_rows(w):
    return jnp.concatenate([w.reshape(9, w.shape[-1]), jnp.zeros((CONV_ROWS - 9, w.shape[-1]), f32)], axis=0)


def kernel(x, c, ctx, c_ctx, w_ada, b_ada, w_in, s5_lambda_re, s5_lambda_im, s5_log_dt, s5_b_re, s5_b_im, s5_c_re, s5_c_im, s5_d, w_glu, b_glu, conv_w, gdn_a_log, gdn_dt_bias, gdn_norm_w, w_out, ln_g, ln_b, loss_target, m_c_ctx, m_w_ada, m_b_ada, m_w_in, m_s5_lambda_re, m_s5_lambda_im, m_s5_log_dt, m_s5_b_re, m_s5_b_im, m_s5_c_re, m_s5_c_im, m_s5_d, m_w_glu, m_b_glu, m_conv_w, m_gdn_a_log, m_gdn_dt_bias, m_gdn_norm_w, m_w_out, m_ln_g, m_ln_b, v_c_ctx, v_w_ada, v_b_ada, v_w_in, v_s5_lambda_re, v_s5_lambda_im, v_s5_log_dt, v_s5_b_re, v_s5_b_im, v_s5_c_re, v_s5_c_im, v_s5_d, v_w_glu, v_b_glu, v_conv_w, v_gdn_a_log, v_gdn_dt_bias, v_gdn_norm_w, v_w_out, v_ln_g, v_ln_b):
    weights = [c_ctx, w_ada, b_ada, w_in, s5_lambda_re, s5_lambda_im, s5_log_dt, s5_b_re, s5_b_im, s5_c_re, s5_c_im,
               s5_d, w_glu, b_glu, conv_w, gdn_a_log, gdn_dt_bias, gdn_norm_w, w_out, ln_g, ln_b]
    ms = [m_c_ctx, m_w_ada, m_b_ada, m_w_in, m_s5_lambda_re, m_s5_lambda_im, m_s5_log_dt, m_s5_b_re, m_s5_b_im,
          m_s5_c_re, m_s5_c_im, m_s5_d, m_w_glu, m_b_glu, m_conv_w, m_gdn_a_log, m_gdn_dt_bias, m_gdn_norm_w, m_w_out,
          m_ln_g, m_ln_b]
    vs = [v_c_ctx, v_w_ada, v_b_ada, v_w_in, v_s5_lambda_re, v_s5_lambda_im, v_s5_log_dt, v_s5_b_re, v_s5_b_im,
          v_s5_c_re, v_s5_c_im, v_s5_d, v_w_glu, v_b_glu, v_conv_w, v_gdn_a_log, v_gdn_dt_bias, v_gdn_norm_w, v_w_out,
          v_ln_g, v_ln_b]
    cpos = lax.axis_index("c")
    jchip = 2 * lax.axis_index("x") + lax.axis_index("y")

    c_all = gather_devices(c, name="gather_c")
    cc = jnp.concatenate([c_all, jnp.broadcast_to(c_ctx[None, None, :], (N_DEV, 1, D_MODEL)),
                          jnp.zeros((N_DEV, 5, D_MODEL), f32)], axis=1)
    w_ada16 = w_ada[0].astype(bf16)
    b_cols = lax.dynamic_slice_in_dim(b_ada, jchip * ADA_SHARD, ADA_SHARD, axis=1)
    m_mine = exchange_devices(ada_fwd(cc, w_ada16, b_cols), name="exchange_m")
    m_rows = jnp.concatenate([m_mine[2 * j, :3] for j in range(4)], axis=1)

    conv_shard = _conv_rows(conv_w)
    g_in, g_out, g_glu, g_conv = gather_shards(
        [jnp.transpose(w_in[0]).astype(bf16), w_out[0].astype(bf16), w_glu[0].astype(bf16), conv_shard])
    w_in_t = g_in.reshape(P_IN, D_MODEL)
    conv16 = g_conv.transpose(1, 0, 2).reshape(CONV_ROWS, 3 * D_GDN)

    swap = lambda a: jnp.swapaxes(a, 3, 4)
    loss, grad_x, big, small, dm_rows = local_step(
        x, ctx, loss_target, m_rows, w_in_t, s5_lambda_re, s5_lambda_im, s5_log_dt, swap(s5_b_re), swap(s5_b_im),
        s5_c_re, s5_c_im, s5_d, g_glu.reshape(D_S5, D_S5), b_glu, conv16, gdn_a_log, gdn_dt_bias, gdn_norm_w,
        g_out.reshape(D_MODEL, D_MODEL), ln_g, ln_b)
    me = 2 * jchip + cpos
    loss_hi = loss.astype(bf16).astype(f32)
    loss_row = jnp.zeros((LANES,), f32).at[me].set(loss_hi).at[N_DEV + me].set(loss - loss_hi)

    dm8 = jnp.concatenate([dm_rows, jnp.zeros((5, 3 * D_MODEL), f32)], axis=0)
    dm_by_chip = dm8.reshape(8, 4, ADA_SHARD).transpose(1, 0, 2)
    dm_cols = exchange_devices(jnp.repeat(dm_by_chip, 2, axis=0), name="exchange_dm")
    g_w_ada, pb = ada_bwd(cc, w_ada16, dm_cols)
    pb_all = gather_devices(pb, name="gather_p")
    g_c_ctx = c_ctx_bwd(pb_all, c_ctx[None, :])[0]
    g_b_ada = jnp.concatenate([pb_all[2 * j, 1:2, :ADA_SHARD] for j in range(4)], axis=1)

    d_w_in, d_w_out, d_w_glu, d_conv16 = big
    slabs = [d_w_in.reshape(4, W_IN_SHARD, D_MODEL),
             d_w_out.reshape(4, D_MODEL // 4, D_MODEL),
             d_w_glu.reshape(4, D_S5 // 4, D_S5),
             d_conv16.reshape(CONV_ROWS, 4, 3 * D_GDN // 4).transpose(1, 0, 2),
             _pack_small(small + (loss_row,)).reshape(4, SMALL_QUARTER, LANES)]
    got = swap_halves(slabs)
    q32, q16 = [], []
    for t, (s, g) in enumerate(zip(slabs, got)):
        if _by_rows(s.shape[1:]):
            own = lax.dynamic_index_in_dim(s.reshape(4, 2, s.shape[1] // 2, s.shape[2]), cpos, axis=1, keepdims=False)
        else:
            own = lax.dynamic_slice_in_dim(s, cpos * (s.shape[2] // 2), s.shape[2] // 2, axis=2)
        a, b = sum_cores(own, g, name=f"sum_cores{t}")
        q32.append(a)
        q16.append(b)
    rec = scatter_to_chips(q16)
    fs = [sum_chips(lax.dynamic_index_in_dim(q, jchip, axis=0, keepdims=False), r, cpos, s.shape[1:],
                    slot=jchip if t == 4 else None, name=f"sum_chips{t}")
          for t, (q, r, s) in enumerate(zip(q32, rec, slabs))]
    red, small_all = join_halves(fs[:4], fs[4])
    g_small = _unpack_small(small_all.reshape(SMALL_TOTAL, LANES))
    loss = jnp.sum(g_small[-1][:2 * N_DEV])
    g_small = g_small[:-1]
    g_shard = {1: g_w_ada, 3: red[0], 18: red[1], 12: red[2], 14: red[3]}

    grads, deltas, new_m, new_v = [None] * 21, [None] * 21, [None] * 21, [None] * 21
    for t, i in enumerate(SHARDED):
        conv, win = i == 14, i == 3
        prep = (lambda a: _conv_rows(a)[None]) if conv else ((lambda a: jnp.transpose(a, (2, 0, 1))) if win else (lambda a: a))
        g = jnp.transpose(g_shard[i], (1, 0, 2)) if win else g_shard[i]
        d, nm, nv = adamw_3d(prep(weights[i]), g, prep(ms[i]), prep(vs[i]), lead=win, name=f"adamw{t}")
        for lst, val in ((grads, g), (deltas, d), (new_m, nm), (new_v, nv)):
            lst[i] = (val[0, :9].reshape(weights[i].shape) if conv else (jnp.transpose(val, (1, 2, 0)) if win else val))
    g_un = {0: g_c_ctx, 2: g_b_ada, **{i: g_small[n] for n, i in enumerate(SMALL)}}
    swapped = [SMALL[n] for n in SMALL_SWAPPED]
    small_in = lambda lst: [_as_2d(swap(lst[i]) if i in swapped else lst[i]) for i in UNSHARDED]
    sm = adamw_small(small_in(weights), [_as_2d(g_un[i]) for i in UNSHARDED], small_in(ms), small_in(vs))
    for n, i in enumerate(UNSHARDED):
        back = ((lambda a: swap(a.reshape(swap(weights[i]).shape))) if i in swapped
                else (lambda a: a.reshape(weights[i].shape)))
        grads[i] = back(g_un[i])
        for lst, res in ((deltas, sm[0]), (new_m, sm[1]), (new_v, sm[2])):
            lst[i] = back(res[n])
    return (loss, grad_x, *grads, *deltas, *new_m, *new_v)
```

```python
import functools

import jax
import jax.numpy as jnp
from jax import lax
from jax.experimental import pallas as pl
from jax.experimental.pallas import tpu as pltpu

f32 = jnp.float32
bf16 = jnp.bfloat16
SDS = jax.ShapeDtypeStruct

D_MODEL = 1024
D_S5 = 512
S5_GROUP = 16
S5_GROUPS = 32
S5_STATE = 64
S5_HALF = S5_GROUPS * S5_STATE
D_GDN = 512
GDN_HEAD = 128
GDN_HEADS = 4
CHUNK = 64
GRID_W = 64
N_DIR = 2
P_IN = 3088
DEEPNORM_ALPHA = 2.0 ** 0.25
LN_EPS = 1e-5
NORM_EPS = 1e-6
ADAM_LR, ADAM_B1, ADAM_B2, ADAM_EPS, ADAM_WD, ADAM_STEP = 0.001, 0.9, 0.999, 1e-08, 0.01, 10

LANES = 128
VMEM_LIMIT = 56 * 1024 * 1024
TOK_TILE = 256
S5_TILE = 256
MESH = pl.DeviceIdType.MESH


def _cparams(n_grid):
    return pltpu.CompilerParams(dimension_semantics=("arbitrary",) * n_grid, vmem_limit_bytes=VMEM_LIMIT)


def _dot(a, b):
    return jnp.dot(a.astype(bf16), b.astype(bf16), preferred_element_type=f32)


def _dot_nt(a, b):
    return lax.dot_general(a.astype(bf16), b.astype(bf16), (((1,), (1,)), ((), ())), preferred_element_type=f32)


def _dot_tn(a, b):
    return lax.dot_general(a.astype(bf16), b.astype(bf16), (((0,), (0,)), ((), ())), preferred_element_type=f32)


def _dot_hi(a, b):
    return jnp.dot(a, b, precision=lax.Precision.HIGHEST, preferred_element_type=f32)


@jax.custom_vjp
def _mm(a, b):
    return _dot(a, b)


@jax.custom_vjp
def _mm_nt(a, b):
    return _dot_nt(a, b)


@jax.custom_vjp
def _mm_tn(a, b):
    return _dot_tn(a, b)


_mm.defvjp(lambda a, b: (_dot(a, b), (a, b)), lambda r, g: (_mm_nt(g, r[1]), _mm_tn(r[0], g)))
_mm_nt.defvjp(lambda a, b: (_dot_nt(a, b), (a, b)), lambda r, g: (_mm(g, r[1]), _mm_tn(g, r[0])))
_mm_tn.defvjp(lambda a, b: (_dot_tn(a, b), (a, b)), lambda r, g: (_mm_nt(r[1], g), _mm(r[0], g)))


def _silu(x):
    return x * jax.nn.sigmoid(x)


def _gelu(x):
    return 0.5 * x * (1.0 + lax.erf(x * (2.0 ** -0.5)))


def _resident(shape):
    nd = len(shape)
    return pl.BlockSpec(shape, lambda *_: (0,) * nd, pipeline_mode=pl.Buffered(1))


def _tok(tile, width, nt=None, rev=False):
    if rev:
        return pl.BlockSpec((None, tile, width), lambda b, n: (b, nt - 1 - n, 0))
    return pl.BlockSpec((None, tile, width), lambda b, n: (b, n, 0))


def _per_batch(rows, width):
    return pl.BlockSpec((None, rows, width), lambda b, n: (b, 0, 0))


def _first_step():
    return jnp.logical_and(pl.program_id(0) == 0, pl.program_id(1) == 0)


ADA_SHARD = 3 * D_MODEL // 4
N_DEV = 8


def ada_fwd(cc, w, b):
    def body(cc_ref, w_ref, b_ref, m_ref):
        for k in range(N_DEV):
            m_ref[k] = _dot(_silu(cc_ref[k]), w_ref[...]) + b_ref[...]

    return pl.pallas_call(body, name="ada_fwd", out_shape=SDS((N_DEV, 8, ADA_SHARD), f32),
                          compiler_params=pltpu.CompilerParams(vmem_limit_bytes=VMEM_LIMIT))(cc, w, b)


def ada_bwd(cc, w, dmj, me):
    def body(me_ref, cc_ref, w_ref, dmj_ref, dw_ref, pb_ref):
        dw = jnp.zeros((D_MODEL, ADA_SHARD), f32)
        p = jnp.zeros((8, D_MODEL), f32)
        db = jnp.zeros((1, ADA_SHARD), f32)
        for k in range(N_DEV):
            dw = dw + _dot_tn(_silu(cc_ref[k]), dmj_ref[k])
            p = p + _dot_nt(dmj_ref[k], w_ref[...])
            db = db + jnp.sum(dmj_ref[k], axis=0, keepdims=True)
        dw_ref[0] = dw
        pb_ref[...] = jnp.zeros_like(pb_ref)
        pb_ref[0:1, :] = p[2:3, :]
        pb_ref[1:2, 0:ADA_SHARD] = db

    whole = lambda shape: pl.BlockSpec(shape, lambda i, me_ref: (0,) * len(shape))
    grid_spec = pltpu.PrefetchScalarGridSpec(
        num_scalar_prefetch=1, grid=(1,),
        in_specs=[whole(cc.shape), whole(w.shape), whole(dmj.shape)],
        out_specs=[whole((1, D_MODEL, ADA_SHARD)), pl.BlockSpec((None, 8, D_MODEL), lambda i, me_ref: (me_ref[0], 0, 0))])
    return pl.pallas_call(
        body, name="ada_bwd", grid_spec=grid_spec,
        out_shape=[SDS((1, D_MODEL, ADA_SHARD), f32), SDS((N_DEV, 8, D_MODEL), f32)],
        compiler_params=pltpu.CompilerParams(vmem_limit_bytes=VMEM_LIMIT))(me.reshape(1).astype(jnp.int32), cc, w, dmj)


def c_ctx_bwd(pb_all, c_ctx):
    def body(p_ref, c_ref, d_ref):
        ds = ((p_ref[0, 0:1, :] + p_ref[2, 0:1, :]) + p_ref[4, 0:1, :]) + p_ref[6, 0:1, :]
        _, vjp = jax.vjp(_silu, c_ref[...])
        d_ref[...] = vjp(ds)[0]

    return pl.pallas_call(body, name="c_ctx_bwd", out_shape=SDS((1, D_MODEL), f32))(pb_all, c_ctx)


N_GATE = 2 * N_DIR * GDN_HEADS
IN_WIDTHS = (D_S5, D_S5, 3 * D_GDN, D_GDN, N_GATE)
IN_OFFS = (0, 512, 1024, 2560, 3072)


def in_proj_fwd(x, mod, wt, *, name):
    B, L, _ = x.shape
    T = min(TOK_TILE, L)

    def body(x_ref, mod_ref, w_ref, *o_refs):
        h = (x_ref[...] * (1.0 + mod_ref[0:1, :]) + mod_ref[1:2, :]).astype(bf16)
        for o_ref, off, wd in zip(o_refs, IN_OFFS, IN_WIDTHS):
            o_ref[...] = _dot_nt(h, w_ref[off:off + wd, :])

    return pl.pallas_call(
        body, name=name, grid=(B, L // T),
        in_specs=[_tok(T, D_MODEL), _per_batch(2, D_MODEL), _resident((P_IN, D_MODEL))],
        out_specs=[_tok(T, wd) for wd in IN_WIDTHS],
        out_shape=[SDS((B, L, wd), f32) for wd in IN_WIDTHS],
        compiler_params=_cparams(2),
    )(x, mod, wt)


def in_proj_bwd(x, mod, ds, wt, gx_res, dw_start, *, name):
    B, L, _ = x.shape
    T = min(TOK_TILE, L)
    with_dx = gx_res is not None
    with_start = dw_start is not None
    n_u = len(ds[0])

    def body(*refs):
        x_ref, mod_ref = refs[0], refs[1]
        du_refs = refs[2:2 + n_u]
        d_refs = refs[2 + n_u:6 + n_u]
        w_ref = refs[6 + n_u]
        k = 7 + n_u
        if with_dx:
            gx_ref = refs[k]
            k += 1
        if with_start:
            start_ref = refs[k]
            k += 1
        dw_ref, dmod_ref = refs[k], refs[k + 1]
        if with_dx:
            dx_ref = refs[k + 2]
        n = pl.program_id(1)

        @pl.when(_first_step())
        def _():
            dw_ref[...] = start_ref[...] if with_start else jnp.zeros_like(dw_ref)

        @pl.when(n == 0)
        def _():
            dmod_ref[...] = jnp.zeros_like(dmod_ref)

        xv = x_ref[...]
        scale1 = 1.0 + mod_ref[0:1, :]
        h = (xv * scale1 + mod_ref[1:2, :]).astype(bf16)
        du = du_refs[0][...]
        for r in du_refs[1:]:
            du = du + r[...]
        dh = jnp.zeros((T, D_MODEL), f32)
        for dv, off, wd in zip([du] + [r[...] for r in d_refs], IN_OFFS, IN_WIDTHS):
            dv = dv.astype(bf16)
            dh = dh + _dot(dv, w_ref[off:off + wd, :])
            dw_ref[off:off + wd, :] += _dot_tn(dv, h)
        dmod_ref[0:1, :] += jnp.sum(dh * xv, axis=0, keepdims=True)
        dmod_ref[1:2, :] += jnp.sum(dh, axis=0, keepdims=True)
        if with_dx:
            dx_ref[...] = gx_ref[...] + dh * scale1

    in_specs = ([_tok(T, D_MODEL), _per_batch(2, D_MODEL)] + [_tok(T, D_S5)] * n_u + [_tok(T, wd) for wd in IN_WIDTHS[1:]]
                + [_resident((P_IN, D_MODEL))])
    args = [x, mod, *ds[0], *ds[1:], wt]
    out_specs = [_resident((P_IN, D_MODEL)), _per_batch(2, D_MODEL)]
    out_shape = [SDS((P_IN, D_MODEL), f32), SDS((B, 2, D_MODEL), f32)]
    if with_dx:
        in_specs.append(_tok(T, D_MODEL))
        args.append(gx_res)
        out_specs.append(_tok(T, D_MODEL))
        out_shape.append(SDS((B, L, D_MODEL), f32))
    if with_start:
        in_specs.append(_resident((P_IN, D_MODEL)))
        args.append(dw_start)
    return pl.pallas_call(body, name=name, grid=(B, L // T), in_specs=in_specs, out_specs=out_specs,
                          out_shape=out_shape, compiler_params=_cparams(2))(*args)


def _s5_zoh(lr, li, ldt, bre, bim, expand):
    dt = jnp.exp(ldt)
    zr, zi = lr * dt, li * dt
    e = jnp.exp(zr)
    ar, ai = e * jnp.cos(zi), e * jnp.sin(zi)
    den = lr * lr + li * li
    czr = ((ar - 1.0) * lr + ai * li) / den
    czi = (ai * lr - (ar - 1.0) * li) / den
    czr_e, czi_e = _dot_hi(czr, expand), _dot_hi(czi, expand)
    return ar, ai, czr_e * bre - czi_e * bim, czr_e * bim + czi_e * bre


_ZOH_OUT = [(N_DIR * S5_GROUPS, S5_STATE)] * 2 + [(N_DIR * S5_GROUPS, S5_STATE * S5_GROUP)] * 2


def s5_zoh_fwd(lr, li, ldt, bre, bim, expand):
    def body(lr_ref, li_ref, ldt_ref, bre_ref, bim_ref, e_ref, ar_ref, ai_ref, bbr_ref, bbi_ref):
        ar, ai, bbr, bbi = _s5_zoh(lr_ref[...], li_ref[...], ldt_ref[...], bre_ref[...], bim_ref[...], e_ref[...])
        ar_ref[...], ai_ref[...], bbr_ref[...], bbi_ref[...] = ar, ai, bbr, bbi

    return pl.pallas_call(body, name="s5_zoh_fwd", out_shape=[SDS(s, f32) for s in _ZOH_OUT])(
        lr, li, ldt, bre, bim, expand)


def s5_zoh_bwd(lr, li, ldt, bre, bim, expand, dar, dai, dbbr, dbbi):
    def body(lr_ref, li_ref, ldt_ref, bre_ref, bim_ref, e_ref, dar_ref, dai_ref, dbbr_ref, dbbi_ref,
             dlr_ref, dli_ref, dldt_ref, dbre_ref, dbim_ref):
        ev = e_ref[...]
        _, vjp = jax.vjp(lambda a, b, c, d, e: _s5_zoh(a, b, c, d, e, ev),
                         lr_ref[...], li_ref[...], ldt_ref[...], bre_ref[...], bim_ref[...])
        outs = vjp((dar_ref[...], dai_ref[...], dbbr_ref[...], dbbi_ref[...]))
        dlr_ref[...], dli_ref[...], dldt_ref[...], dbre_ref[...], dbim_ref[...] = outs

    shapes = [lr.shape, li.shape, ldt.shape, bre.shape, bim.shape]
    return pl.pallas_call(body, name="s5_zoh_bwd", out_shape=[SDS(s, f32) for s in shapes])(
        lr, li, ldt, bre, bim, expand, dar, dai, dbbr, dbbi)


def _scan_rows(T, rev, ar, ai, h0s, refs, off):
    def step(i, carry):
        t = off + ((T - 1 - i) if rev else i)
        out = []
        for (hr, hi), (r_ref, i_ref) in zip(carry, refs):
            nr = ar * hr - ai * hi + r_ref[pl.ds(t, 1), :]
            ni = ar * hi + ai * hr + i_ref[pl.ds(t, 1), :]
            r_ref[pl.ds(t, 1), :] = nr
            i_ref[pl.ds(t, 1), :] = ni
            out.append((nr, ni))
        return tuple(out)

    return lax.fori_loop(0, T, step, tuple(h0s))


S5_BLOCKS = 4
S5_BC = D_S5 // S5_BLOCKS
S5_BS = S5_HALF // S5_BLOCKS


def _s5_in(uv, bre_ref, bim_ref, hr_ref, hi_ref, off, T):
    for jb in range(S5_BLOCKS):
        uj = uv[:, jb * S5_BC:(jb + 1) * S5_BC]
        hr_ref[off:off + T, jb * S5_BS:(jb + 1) * S5_BS] = _dot(uj, bre_ref[jb])
        hi_ref[off:off + T, jb * S5_BS:(jb + 1) * S5_BS] = _dot(uj, bim_ref[jb])


def _s5_specs(B, T, nt, rev):
    tidx = (lambda n: nt - 1 - n) if rev else (lambda n: n)
    tok = pl.BlockSpec((B, T, D_S5), lambda n: (0, tidx(n), 0))
    hin = pl.BlockSpec((B, None, 2, S5_HALF), lambda n: (0, tidx(n), 0, 0))
    state = pl.BlockSpec((B, 2, S5_HALF), lambda n: (0, 0, 0))
    return tok, hin, state


def s5_scan_fwd(u, bre, bim, ctop, cbot, arow, h0, *, d, need_y, name):
    B, L, _ = u.shape
    T = min(S5_TILE, L)
    nt = L // T
    rev = d == 1

    def body(u_ref, bre_ref, bim_ref, ct_ref, cb_ref, a_ref, h0_ref, *rest):
        if need_y:
            y_ref, hs_ref, hin_ref, hend_ref, hr_scr, hi_scr, h_scr = rest
        else:
            hs_ref, hin_ref, hend_ref, hr_scr, hi_scr, h_scr = rest
        n = pl.program_id(0)

        @pl.when(n == 0)
        def _():
            h_scr[...] = h0_ref[...]

        hin_ref[...] = h_scr[...]
        for b in range(B):
            _s5_in(u_ref[b].astype(bf16), bre_ref, bim_ref, hr_scr.at[b], hi_scr.at[b], 0, T)
        hs = _scan_rows(T, rev, a_ref[0:1, :], a_ref[1:2, :], [(h_scr[b, 0:1, :], h_scr[b, 1:2, :]) for b in range(B)],
                        [(hr_scr.at[b], hi_scr.at[b]) for b in range(B)], 0)
        for b in range(B):
            h_scr[b, 0:1, :] = hs[b][0]
            h_scr[b, 1:2, :] = hs[b][1]
            hs_ref[b, :, 0:S5_HALF] = hr_scr[b].astype(bf16)
            hs_ref[b, :, S5_HALF:2 * S5_HALF] = hi_scr[b].astype(bf16)
            if need_y:
                for jb in range(S5_BLOCKS):
                    st = slice(jb * S5_BS, (jb + 1) * S5_BS)
                    y_ref[b, :, jb * S5_BC:(jb + 1) * S5_BC] = (_dot(hr_scr[b, :, st], ct_ref[jb])
                                                                 + _dot(hi_scr[b, :, st], cb_ref[jb]))

        @pl.when(n == nt - 1)
        def _():
            hend_ref[...] = h_scr[...]

    tok, hin_spec, state = _s5_specs(B, T, nt, rev)
    hs_spec = pl.BlockSpec((B, T, 2 * S5_HALF), tok.index_map)
    out_specs = [hs_spec, hin_spec, state]
    out_shape = [SDS((B, L, 2 * S5_HALF), bf16), SDS((B, nt, 2, S5_HALF), f32), SDS((B, 2, S5_HALF), f32)]
    if need_y:
        out_specs.insert(0, tok)
        out_shape.insert(0, SDS((B, L, D_S5), f32))
    w_in, w_out = _resident((S5_BLOCKS, S5_BC, S5_BS)), _resident((S5_BLOCKS, S5_BS, S5_BC))
    return pl.pallas_call(
        body, name=name, grid=(nt,),
        in_specs=[tok, w_in, w_in, w_out, w_out, _resident((2, S5_HALF)), state],
        out_specs=out_specs, out_shape=out_shape,
        scratch_shapes=[pltpu.VMEM((B, T, S5_HALF), f32), pltpu.VMEM((B, T, S5_HALF), f32),
                        pltpu.VMEM((B, 2, S5_HALF), f32)],
        compiler_params=_cparams(1),
    )(u, bre, bim, ctop, cbot, arow, h0)


def s5_scan_bwd(u, dy, hs, bre, bim, ctop, cbot, arow, hin, dhend, *, d, name):
    B, L, _ = u.shape
    T = min(S5_TILE, L)
    nt = L // T
    rev = d == 1
    has_dy = dy is not None
    PAD = 8

    def body(*refs):
        u_ref = refs[0]
        k = 1
        if has_dy:
            dy_ref = refs[1]
            k = 2
        hs_ref = refs[k]
        k += 1
        bre_ref, bim_ref, ct_ref, cb_ref, a_ref, hin_ref, dhend_ref = refs[k:k + 7]
        du_ref, dbre_ref, dbim_ref, dct_ref, dcb_ref, da_ref, dh0_ref = refs[k + 7:k + 14]
        hr_scr, hi_scr, gr_scr, gi_scr, p_scr = refs[k + 14:]
        n = pl.program_id(0)

        @pl.when(n == 0)
        def _():
            for r in (dbre_ref, dbim_ref, dct_ref, dcb_ref, da_ref):
                r[...] = jnp.zeros_like(r)
            p_scr[...] = dhend_ref[...]

        ar, ai = a_ref[0:1, :], a_ref[1:2, :]
        prev_row = PAD + T if rev else PAD - 1
        uvs = []
        for b in range(B):
            uvs.append(u_ref[b].astype(bf16))
            hr_scr[b, PAD:PAD + T, :] = hs_ref[b, :, 0:S5_HALF].astype(f32)
            hi_scr[b, PAD:PAD + T, :] = hs_ref[b, :, S5_HALF:2 * S5_HALF].astype(f32)
            hr_scr[b, prev_row:prev_row + 1, :] = hin_ref[b, 0:1, :]
            hi_scr[b, prev_row:prev_row + 1, :] = hin_ref[b, 1:2, :]
        if has_dy:
            for b in range(B):
                dyv = dy_ref[b].astype(bf16)
                for jb in range(S5_BLOCKS):
                    st = slice(jb * S5_BS, (jb + 1) * S5_BS)
                    dyj = dyv[:, jb * S5_BC:(jb + 1) * S5_BC]
                    gr_scr[b, :, st] = _dot_nt(dyj, ct_ref[jb])
                    gi_scr[b, :, st] = _dot_nt(dyj, cb_ref[jb])
                    dct_ref[jb] += _dot_tn(hr_scr[b, PAD:PAD + T, st], dyj)
                    dcb_ref[jb] += _dot_tn(hi_scr[b, PAD:PAD + T, st], dyj)
        else:
            gr_scr[...] = jnp.zeros_like(gr_scr)
            gi_scr[...] = jnp.zeros_like(gi_scr)

        def step(i, carry):
            t = i if rev else T - 1 - i
            tp = PAD + t + (1 if rev else -1)
            out = []
            for b, (pr, pi, dar, dai) in enumerate(carry):
                gr = gr_scr[b, pl.ds(t, 1), :] + pr
                gi = gi_scr[b, pl.ds(t, 1), :] + pi
                gr_scr[b, pl.ds(t, 1), :] = gr
                gi_scr[b, pl.ds(t, 1), :] = gi
                hpr = hr_scr[b, pl.ds(tp, 1), :]
                hpi = hi_scr[b, pl.ds(tp, 1), :]
                out.append((ar * gr + ai * gi, ar * gi - ai * gr, dar + hpr * gr + hpi * gi, dai + hpr * gi - hpi * gr))
            return tuple(out)

        zero = jnp.zeros((1, S5_HALF), f32)
        res = lax.fori_loop(0, T, step, tuple((p_scr[b, 0:1, :], p_scr[b, 1:2, :], zero, zero) for b in range(B)))
        for b in range(B):
            pr, pi, dar, dai = res[b]
            p_scr[b, 0:1, :] = pr
            p_scr[b, 1:2, :] = pi
            da_ref[0:1, :] += dar
            da_ref[1:2, :] += dai
            for jb in range(S5_BLOCKS):
                st = slice(jb * S5_BS, (jb + 1) * S5_BS)
                ch = slice(jb * S5_BC, (jb + 1) * S5_BC)
                gr_j = gr_scr[b, :, st].astype(bf16)
                gi_j = gi_scr[b, :, st].astype(bf16)
                du_ref[b, :, ch] = _dot_nt(gr_j, bre_ref[jb]) + _dot_nt(gi_j, bim_ref[jb])
                dbre_ref[jb] += _dot_tn(uvs[b][:, ch], gr_j)
                dbim_ref[jb] += _dot_tn(uvs[b][:, ch], gi_j)

        @pl.when(n == nt - 1)
        def _():
            dh0_ref[...] = p_scr[...]

    tok, hin_spec, state = _s5_specs(B, T, nt, not rev)
    hs_spec = pl.BlockSpec((B, T, 2 * S5_HALF), tok.index_map)
    w_in, w_out = _resident((S5_BLOCKS, S5_BC, S5_BS)), _resident((S5_BLOCKS, S5_BS, S5_BC))
    wspecs = [w_in, w_in, w_out, w_out]
    in_specs = [tok] + ([tok] if has_dy else []) + [hs_spec] + wspecs + [_resident((2, S5_HALF)), hin_spec, state]
    args = [u] + ([dy] if has_dy else []) + [hs, bre, bim, ctop, cbot, arow, hin, dhend]
    return pl.pallas_call(
        body, name=name, grid=(nt,), in_specs=in_specs,
        out_specs=[tok] + wspecs + [_resident((2, S5_HALF)), state],
        out_shape=[SDS((B, L, D_S5), f32), SDS((S5_BLOCKS, S5_BC, S5_BS), f32), SDS((S5_BLOCKS, S5_BC, S5_BS), f32),
                   SDS((S5_BLOCKS, S5_BS, S5_BC), f32), SDS((S5_BLOCKS, S5_BS, S5_BC), f32), SDS((2, S5_HALF), f32),
                   SDS((B, 2, S5_HALF), f32)],
        scratch_shapes=[pltpu.VMEM((B, T + 2 * PAD, S5_HALF), f32), pltpu.VMEM((B, T + 2 * PAD, S5_HALF), f32),
                        pltpu.VMEM((B, T, S5_HALF), f32), pltpu.VMEM((B, T, S5_HALF), f32),
                        pltpu.VMEM((B, 2, S5_HALF), f32)],
        compiler_params=_cparams(1),
    )(*args)


def _glu_fn(u, y0, y1, z, dsk, wg, bg):
    g = _gelu(dsk * u + y0 + y1)
    return g * jax.nn.sigmoid(_mm(g, wg) + bg) * _silu(z)


CONV_ROWS = 16


def _shift(x, s):
    L = x.shape[0]
    k = (-s) % L
    return x if k == 0 else pltpu.roll(x, k, axis=0)


def _r16(v):
    return v.astype(bf16).astype(f32)


def _conv_masks(L, is_ctx):
    t = lax.broadcasted_iota(jnp.int32, (L, 1), 0)
    if is_ctx:
        return t == L - 1, t == 0, None, None
    col = jnp.bitwise_and(t, GRID_W - 1)
    return col == GRID_W - 1, col == 0, t >= GRID_W, t < L - GRID_W


def _conv_sides(xv, masks):
    no_left, no_right, _, _ = masks
    return _shift(jnp.where(no_left, 0.0, xv), -1), _shift(jnp.where(no_right, 0.0, xv), 1)


def _conv_pre(xv, w_ref, masks, is_ctx):
    xv = _r16(xv)
    wv = _r16(w_ref[...])
    xl, xr = _conv_sides(xv, masks)
    z = [wv[3 * di:3 * di + 1, :] * xl + wv[3 * di + 1:3 * di + 2, :] * xv + wv[3 * di + 2:3 * di + 3, :] * xr
         for di in ((1,) if is_ctx else (0, 1, 2))]
    if is_ctx:
        return z[0]
    _, _, has_up, has_down = masks
    return z[1] + jnp.where(has_up, _shift(z[0], -GRID_W), 0.0) + jnp.where(has_down, _shift(z[2], GRID_W), 0.0)


def _conv_pre_bwd(xv, w_ref, dpre, masks, is_ctx, dw_ref):
    no_left, no_right, has_up, has_down = masks
    xv, dpre, wv = _r16(xv), _r16(dpre), _r16(w_ref[...])
    xl, xr = _conv_sides(xv, masks)
    if is_ctx:
        dz = {1: dpre}
    else:
        dz = {0: _shift(jnp.where(has_up, dpre, 0.0), GRID_W), 1: dpre, 2: _shift(jnp.where(has_down, dpre, 0.0), -GRID_W)}
    dxl = dxc = dxr = None
    for di, d in dz.items():
        for dj, side in enumerate((xl, xv, xr)):
            dw_ref[3 * di + dj:3 * di + dj + 1, :] = jnp.sum(d * side, axis=0, keepdims=True)
        tl, tc, tr = (wv[3 * di + dj:3 * di + dj + 1, :] * d for dj in range(3))
        dxl, dxc, dxr = (tl, tc, tr) if dxl is None else (dxl + tl, dxc + tc, dxr + tr)
    return dxc + jnp.where(no_left, 0.0, _shift(dxl, 1)) + jnp.where(no_right, 0.0, _shift(dxr, -1))


def _qk_post(pre, is_norm, scale):
    s = _silu(pre)
    nrm = lax.rsqrt(jnp.sum(s * s, axis=-1, keepdims=True) + NORM_EPS)
    return s * jnp.where(is_norm, nrm * scale, 1.0)


def _conv_tile(L):
    return D_GDN if L <= 512 else GDN_HEAD


def _conv_kind(W):
    head = pl.program_id(1) * (W // GDN_HEAD)
    return head < 2 * GDN_HEADS, jnp.where(head < GDN_HEADS, GDN_HEAD ** -0.5, 1.0).astype(f32)


def _conv_specs(L, W):
    spec = pl.BlockSpec((None, L, W), lambda b, ct: (b, 0, ct))
    wspec = pl.BlockSpec((CONV_ROWS, W), lambda b, ct: (0, ct))
    dwspec = pl.BlockSpec((None, CONV_ROWS, W), lambda b, ct: (b, 0, ct))
    return spec, wspec, dwspec


def conv_fwd(qkv, w16, *, is_ctx, name):
    B, L, C = qkv.shape
    W = _conv_tile(L)
    spec, wspec, _ = _conv_specs(L, W)

    def body(x_ref, w_ref, o_ref, pre_ref):
        is_norm, scale = _conv_kind(W)
        pre = _conv_pre(x_ref[...], w_ref, _conv_masks(L, is_ctx), is_ctx)
        pre_ref[...] = pre
        for h in range(W // GDN_HEAD):
            sl = slice(h * GDN_HEAD, (h + 1) * GDN_HEAD)
            o_ref[:, sl] = _qk_post(pre[:, sl], is_norm, scale)

    return pl.pallas_call(body, name=name, grid=(B, C // W), in_specs=[spec, wspec], out_specs=[spec, spec],
                          out_shape=[SDS((B, L, C), f32)] * 2, compiler_params=_cparams(2))(qkv, w16)


def conv_bwd(qkv, pre, w16, da0, da1, *, is_ctx, name):
    B, L, C = qkv.shape
    W = _conv_tile(L)
    spec, wspec, dwspec = _conv_specs(L, W)

    def body(x_ref, pre_ref, w_ref, d0_ref, d1_ref, dx_ref, dw_ref):
        is_norm, scale = _conv_kind(W)
        dpre = []
        for h in range(W // GDN_HEAD):
            sl = slice(h * GDN_HEAD, (h + 1) * GDN_HEAD)
            _, vjp = jax.vjp(lambda p: _qk_post(p, is_norm, scale), pre_ref[:, sl])
            dpre.append(vjp(d0_ref[:, sl] + d1_ref[:, sl])[0])
        dpre = dpre[0] if len(dpre) == 1 else jnp.concatenate(dpre, axis=1)
        dw_ref[...] = jnp.zeros_like(dw_ref)
        dx_ref[...] = _conv_pre_bwd(x_ref[...], w_ref, dpre, _conv_masks(L, is_ctx), is_ctx, dw_ref)

    return pl.pallas_call(body, name=name, grid=(B, C // W), in_specs=[spec, spec, wspec, spec, spec],
                          out_specs=[spec, dwspec], out_shape=[SDS((B, L, C), f32), SDS((B, CONV_ROWS, C), f32)],
                          compiler_params=_cparams(2))(qkv, pre, w16, da0, da1)


def _gates_fn(ba, alog, dtb):
    T = ba.shape[0]
    nck = T // CHUNK
    lane = lax.broadcasted_iota(jnp.int32, ba.shape, 1)
    ii = lax.broadcasted_iota(jnp.int32, (nck, CHUNK, CHUNK), 1)
    jj = lax.broadcasted_iota(jnp.int32, (nck, CHUNK, CHUNK), 2)
    g = jnp.where(lane >= 8, -jnp.exp(alog) * jax.nn.softplus(ba + dtb), 0.0)
    g3 = g.reshape(nck, CHUNK, N_GATE)
    chunk_sum = lambda tri: lax.dot_general(tri.astype(f32), g3, (((2,), (1,)), ((0,), (0,))),
                                            precision=lax.Precision.HIGHEST, preferred_element_type=f32).reshape(T, N_GATE)
    gc = jnp.where(lane >= 12, chunk_sum(ii <= jj), chunk_sum(ii >= jj))
    return jnp.where(lane < 8, jax.nn.sigmoid(ba), gc)


def gates_fwd(ba, alog, dtb, *, name):
    B, L, _ = ba.shape
    T = min(TOK_TILE, L)
    t = _tok(T, N_GATE)

    def body(ba_ref, al_ref, dt_ref, o_ref):
        o_ref[...] = _gates_fn(ba_ref[...], al_ref[...], dt_ref[...])

    return pl.pallas_call(body, name=name, grid=(B, L // T),
                          in_specs=[t, _resident((1, N_GATE)), _resident((1, N_GATE))], out_specs=t,
                          out_shape=SDS((B, L, N_GATE), f32), compiler_params=_cparams(2))(ba, alog, dtb)


def gates_bwd(ba, alog, dtb, dbg, *, name):
    B, L, _ = ba.shape
    T = min(TOK_TILE, L)
    t = _tok(T, N_GATE)
    small = _resident((1, N_GATE))

    def body(ba_ref, al_ref, dt_ref, d_ref, dba_ref, dal_ref, ddt_ref):
        @pl.when(_first_step())
        def _():
            dal_ref[...] = jnp.zeros_like(dal_ref)
            ddt_ref[...] = jnp.zeros_like(ddt_ref)

        _, vjp = jax.vjp(_gates_fn, ba_ref[...], al_ref[...], dt_ref[...])
        dba, dal, ddt = vjp(d_ref[...])
        dba_ref[...] = dba
        dal_ref[...] += dal
        ddt_ref[...] += ddt

    return pl.pallas_call(body, name=name, grid=(B, L // T), in_specs=[t, small, small, t],
                          out_specs=[t, small, small],
                          out_shape=[SDS((B, L, N_GATE), f32), SDS((1, N_GATE), f32), SDS((1, N_GATE), f32)],
                          compiler_params=_cparams(2))(ba, alog, dtb, dbg)


@jax.custom_vjp
def _inv_unit_tri(mats):
    n = mats[0].shape[0]
    eye = (lax.broadcasted_iota(jnp.int32, (n, n), 0) == lax.broadcasted_iota(jnp.int32, (n, n), 1)).astype(f32)
    xs = [eye - a for a in mats]
    sq = [_dot(a, a) for a in mats]
    ps = sq
    k = 2
    while k < n:
        xs = [x + _dot(x, p) for x, p in zip(xs, ps)]
        k *= 2
        if k < n:
            ps = [_dot(p, p) for p in ps]
    return tuple(_dot(p, x) - a for p, x, a in zip(sq, xs, mats))


def _inv_unit_tri_fwd(mats):
    ns = _inv_unit_tri(mats)
    return ns, ns


def _inv_unit_tri_bwd(ns, dns):
    ys = [dn + _dot_tn(nn, dn) for nn, dn in zip(ns, dns)]
    return (tuple(-(y + _dot_nt(y, nn)) for y, nn in zip(ys, ns)),)


_inv_unit_tri.defvjp(_inv_unit_tri_fwd, _inv_unit_tri_bwd)


@jax.custom_vjp
def _inv_unit_tri_saved(mats, saved):
    return saved


_inv_unit_tri_saved.defvjp(lambda mats, saved: (saved, saved),
                           lambda ns, dns: _inv_unit_tri_bwd(ns, dns) + (tuple(jnp.zeros_like(n) for n in ns),))


def _gdn_chunk(heads, *, revs, saved=None, with_n=False):
    n = heads[0][0].shape[0]
    ii = lax.broadcasted_iota(jnp.int32, (n, n), 0)
    jj = lax.broadcasted_iota(jnp.int32, (n, n), 1)
    row = lax.broadcasted_iota(jnp.int32, (n, 1), 0)
    lower = {False: ii >= jj, True: ii <= jj}
    strict = {False: ii > jj, True: ii < jj}
    last = {False: n - 1, True: 0}
    H = range(len(heads))
    q, k, v, beta, gc, gr, s = (list(t) for t in zip(*heads))
    decay = [jnp.where(lower[revs[h]], jnp.exp(jnp.where(lower[revs[h]], gc[h] - gr[h], 0.0)), 0.0) for h in H]
    kk = [_mm_nt(k[h], k[h]) for h in H]
    qk = [_mm_nt(q[h], k[h]) * decay[h] for h in H]
    qs = [_mm(q[h], s[h]) for h in H]
    a_mat = tuple(jnp.where(strict[revs[h]], beta[h] * kk[h] * decay[h], 0.0) for h in H)
    gamma = [jnp.exp(gc[h]) for h in H]
    g_last = [jnp.sum(jnp.where(row == last[revs[h]], gc[h], 0.0), axis=0, keepdims=True) for h in H]
    nmat = _inv_unit_tri(a_mat) if saved is None else _inv_unit_tri_saved(a_mat, saved)
    bv = [beta[h] * v[h] for h in H]
    bk = [(beta[h] * gamma[h]) * k[h] for h in H]
    u0 = [bv[h] + _mm(nmat[h], bv[h]) for h in H]
    w = [bk[h] + _mm(nmat[h], bk[h]) for h in H]
    k_out = [k[h] * jnp.exp(g_last[h] - gc[h]) for h in H]
    u = [u0[h] - _mm(w[h], s[h]) for h in H]
    o = [gamma[h] * qs[h] + _mm(qk[h], u[h]) for h in H]
    s_new = [jnp.exp(g_last[h]) * s[h] + _mm_tn(k_out[h], u[h]) for h in H]
    outs = tuple((o[h], s_new[h]) for h in H)
    return (outs, nmat) if with_n else outs


def _gdn_specs(B, nc, rev):
    def cidx(n):
        return (nc - 1 - n) if rev else n
    tok = lambda width: pl.BlockSpec((B, CHUNK, width), lambda n: (0, cidx(n), 0))
    rowspec = pl.BlockSpec((B, None, N_GATE, CHUNK), lambda n: (0, cidx(n), 0, 0))
    st = pl.BlockSpec((B, GDN_HEADS, GDN_HEAD, GDN_HEAD), lambda n: (0, 0, 0, 0))
    ck = pl.BlockSpec((B, None, GDN_HEADS, GDN_HEAD, GDN_HEAD), lambda n: (0, cidx(n), 0, 0, 0))
    nsp = pl.BlockSpec((B, None, GDN_HEADS, CHUNK, CHUNK), lambda n: (0, cidx(n), 0, 0, 0))
    return tok, rowspec, st, ck, nsp


def _gdn_head_args(qkv_ref, bg_ref, bgr_ref, b, d, h):
    col = d * GDN_HEADS + h
    q = qkv_ref[b, :, h * GDN_HEAD:(h + 1) * GDN_HEAD]
    k = qkv_ref[b, :, D_GDN + h * GDN_HEAD:D_GDN + (h + 1) * GDN_HEAD]
    v = qkv_ref[b, :, 2 * D_GDN + h * GDN_HEAD:2 * D_GDN + (h + 1) * GDN_HEAD]
    bgv = bg_ref[b]
    return q, k, v, bgv[:, col:col + 1], bgv[:, 8 + col:9 + col], bgr_ref[b][8 + col:9 + col, :]


def _gdn_chains(B):
    return [(d, b, h) for d in range(N_DIR) for b in range(B) for h in range(GDN_HEADS)]


def gdn_fwd(qkv, bg, bgr, s0s, *, need_o, name):
    B, L, _ = qkv.shape
    nc = L // CHUNK
    specs = [_gdn_specs(B, nc, d == 1) for d in range(N_DIR)]
    chains = _gdn_chains(B)
    state_shape = (B, GDN_HEADS, GDN_HEAD, GDN_HEAD)

    def body(*refs):
        ins = [refs[3 * d:3 * d + 3] for d in range(N_DIR)]
        s0_refs = refs[6:8]
        k = 8
        o_refs = refs[k:k + 2] if need_o else None
        k += 2 if need_o else 0
        ck_refs, n_refs, sf_refs, s_scrs = refs[k:k + 2], refs[k + 2:k + 4], refs[k + 4:k + 6], refs[k + 6:k + 8]
        n = pl.program_id(0)

        @pl.when(n == 0)
        def _():
            for d in range(N_DIR):
                s_scrs[d][...] = s0_refs[d][...]

        for d in range(N_DIR):
            ck_refs[d][...] = s_scrs[d][...]
        heads = tuple(_gdn_head_args(*ins[d], b, d, h) + (s_scrs[d][b, h],) for d, b, h in chains)
        outs, nmat = _gdn_chunk(heads, revs=tuple(d == 1 for d, _, _ in chains), with_n=True)
        for (d, b, h), (o, s_new), nn in zip(chains, outs, nmat):
            if need_o:
                o_refs[d][b, :, h * GDN_HEAD:(h + 1) * GDN_HEAD] = o
            s_scrs[d][b, h] = s_new
            n_refs[d][b, h] = nn

        @pl.when(n == nc - 1)
        def _():
            for d in range(N_DIR):
                sf_refs[d][...] = s_scrs[d][...]

    in_specs, out_o, out_ck, out_n, out_sf = [], [], [], [], []
    for tok, rowspec, st, ck, nsp in specs:
        in_specs += [tok(3 * D_GDN), tok(N_GATE), rowspec]
        out_o.append(tok(D_GDN))
        out_ck.append(ck)
        out_n.append(nsp)
        out_sf.append(st)
    in_specs += [specs[0][2]] * 2
    out_specs = (out_o if need_o else []) + out_ck + out_n + out_sf
    out_shape = (([SDS((B, L, D_GDN), f32)] * 2 if need_o else []) + [SDS((B, nc) + state_shape[1:], f32)] * 2
                 + [SDS((B, nc, GDN_HEADS, CHUNK, CHUNK), f32)] * 2 + [SDS(state_shape, f32)] * 2)
    res = pl.pallas_call(
        body, name=name, grid=(nc,), in_specs=in_specs, out_specs=out_specs, out_shape=out_shape,
        scratch_shapes=[pltpu.VMEM(state_shape, f32)] * 2, compiler_params=_cparams(1),
    )(qkv, bg, bgr, qkv, bg, bgr, *s0s)
    if need_o:
        return res[0:2], res[2:4], res[4:6], res[6:8]
    return res[0:2], res[2:4], res[4:6]


def gdn_bwd(qkv, bg, bgr, cks, ns, do, dsfs, *, name):
    B, L, _ = qkv.shape
    nc = L // CHUNK
    has_do = do is not None
    specs = [_gdn_specs(B, nc, d != 1) for d in range(N_DIR)]
    chains = _gdn_chains(B)
    state_shape = (B, GDN_HEADS, GDN_HEAD, GDN_HEAD)
    per_dir = 6 if has_do else 5

    def body(*refs):
        ins = [refs[per_dir * d:per_dir * d + per_dir] for d in range(N_DIR)]
        k = per_dir * N_DIR
        dsf_refs = refs[k:k + 2]
        outs = [refs[k + 2 + 3 * d:k + 5 + 3 * d] for d in range(N_DIR)]
        ds0_refs, ds_scrs = refs[k + 8:k + 10], refs[k + 10:k + 12]
        n = pl.program_id(0)

        @pl.when(n == 0)
        def _():
            for d in range(N_DIR):
                ds_scrs[d][...] = dsf_refs[d][...]

        lane = lax.broadcasted_iota(jnp.int32, (CHUNK, N_GATE), 1)
        sub = lax.broadcasted_iota(jnp.int32, (N_GATE, CHUNK), 0)
        heads = tuple(_gdn_head_args(*ins[d][:3], b, d, h) + (ins[d][3][b, h],) for d, b, h in chains)
        saved = tuple(ins[d][4][b, h] for d, b, h in chains)
        _, vjp = jax.vjp(functools.partial(_gdn_chunk, revs=tuple(d == 1 for d, _, _ in chains), saved=saved), heads)
        zero = jnp.zeros((CHUNK, GDN_HEAD), f32)
        cts = tuple(((ins[d][5][b, :, h * GDN_HEAD:(h + 1) * GDN_HEAD] if has_do else zero), ds_scrs[d][b, h])
                    for d, b, h in chains)
        (dheads,) = vjp(cts)
        dbg_acc = [[jnp.zeros((CHUNK, N_GATE), f32) for _ in range(B)] for _ in range(N_DIR)]
        dbgr_acc = [[jnp.zeros((N_GATE, CHUNK), f32) for _ in range(B)] for _ in range(N_DIR)]
        for (d, b, h), (dq, dk, dv, db, dgc, dgr, ds) in zip(chains, dheads):
            col = d * GDN_HEADS + h
            dqkv_ref = outs[d][0]
            dqkv_ref[b, :, h * GDN_HEAD:(h + 1) * GDN_HEAD] = dq
            dqkv_ref[b, :, D_GDN + h * GDN_HEAD:D_GDN + (h + 1) * GDN_HEAD] = dk
            dqkv_ref[b, :, 2 * D_GDN + h * GDN_HEAD:2 * D_GDN + (h + 1) * GDN_HEAD] = dv
            dbg_acc[d][b] = dbg_acc[d][b] + jnp.where(lane == col, db, 0.0) + jnp.where(lane == 8 + col, dgc, 0.0)
            dbgr_acc[d][b] = dbgr_acc[d][b] + jnp.where(sub == 8 + col, dgr, 0.0)
            ds_scrs[d][b, h] = ds
        for d in range(N_DIR):
            for b in range(B):
                outs[d][1][b] = dbg_acc[d][b]
                outs[d][2][b] = dbgr_acc[d][b]

        @pl.when(n == nc - 1)
        def _():
            for d in range(N_DIR):
                ds0_refs[d][...] = ds_scrs[d][...]

    in_specs, args, out_specs, out_shape = [], [], [], []
    for d, (tok, rowspec, st, ck, nsp) in enumerate(specs):
        in_specs += [tok(3 * D_GDN), tok(N_GATE), rowspec, ck, nsp] + ([tok(D_GDN)] if has_do else [])
        args += [qkv, bg, bgr, cks[d], ns[d]] + ([do] if has_do else [])
        out_specs += [tok(3 * D_GDN), tok(N_GATE), rowspec]
        out_shape += [SDS((B, L, 3 * D_GDN), f32), SDS((B, L, N_GATE), f32), SDS((B, nc, N_GATE, CHUNK), f32)]
    st = specs[0][2]
    in_specs += [st, st]
    args += list(dsfs)
    out_specs += [st, st]
    out_shape += [SDS(state_shape, f32)] * 2
    res = pl.pallas_call(
        body, name=name, grid=(nc,), in_specs=in_specs, out_specs=out_specs, out_shape=out_shape,
        scratch_shapes=[pltpu.VMEM(state_shape, f32)] * 2, compiler_params=_cparams(1),
    )(*args)
    return (res[0], res[3]), (res[1], res[4]), (res[2], res[5]), (res[6], res[7])


def _gnorm_fn(o0, o1, z, w):
    o = o0 + o1
    return o * lax.rsqrt(jnp.mean(o * o, axis=-1, keepdims=True) + NORM_EPS) * w * _silu(z)


def _head_loss(y, x, gate, lng, lnb, tgt):
    r = DEEPNORM_ALPHA * x + gate * y
    mu = jnp.mean(r, axis=-1, keepdims=True)
    rc = r - mu
    var = jnp.mean(rc * rc, axis=-1, keepdims=True)
    err = rc * lax.rsqrt(var + LN_EPS) * lng + lnb - tgt
    return (0.5 / D_MODEL) * jnp.sum(jnp.sum(err * err, axis=-1, keepdims=True), axis=0, keepdims=True)


def tail_fwd_bwd(u, y0, y1, z_s5, o0, o1, z_gdn, x, tgt, gate, lng, lnb, ws, wg, dsk, wglu, bglu, nw):
    B, L, _ = x.shape
    T = min(TOK_TILE, L)

    def body(u_ref, y0_ref, y1_ref, z_ref, o0_ref, o1_ref, zg_ref, x_ref, t_ref, gate_ref, lng_ref, lnb_ref, ws_ref,
             wg_ref, dsk_ref, wglu_ref, bglu_ref, nw_ref,
             loss_ref, du_ref, dys_ref, dz_ref, do_ref, dzg_ref, gx_ref, dws_ref, dwg_ref, dgate_ref, dlng_ref, dlnb_ref,
             ddsk_ref, dwglu_ref, dbglu_ref, dnw_ref):
        n = pl.program_id(1)

        @pl.when(_first_step())
        def _():
            for r in (dws_ref, dwg_ref, dlng_ref, dlnb_ref, ddsk_ref, dwglu_ref, dbglu_ref, dnw_ref):
                r[...] = jnp.zeros_like(r)

        @pl.when(n == 0)
        def _():
            loss_ref[...] = jnp.zeros_like(loss_ref)
            dgate_ref[...] = jnp.zeros_like(dgate_ref)

        s5o, glu_vjp = jax.vjp(_glu_fn, u_ref[...], y0_ref[...], y1_ref[...], z_ref[...], dsk_ref[...],
                               wglu_ref[...].astype(f32), bglu_ref[...])
        heads = []
        for h in range(GDN_HEADS):
            sl = slice(h * GDN_HEAD, (h + 1) * GDN_HEAD)
            heads.append(jax.vjp(_gnorm_fn, o0_ref[:, sl], o1_ref[:, sl], zg_ref[:, sl], nw_ref[...]))
        sv = s5o.astype(bf16)
        gv = jnp.concatenate([out for out, _ in heads], axis=1).astype(bf16)
        y = _dot(sv, ws_ref[...]) + _dot(gv, wg_ref[...])
        loss, vjp = jax.vjp(lambda *a: _head_loss(*a, t_ref[...]), y, x_ref[...], gate_ref[...], lng_ref[...],
                            lnb_ref[...])
        dy, dx, dgate, dlng, dlnb = vjp(jnp.ones((1, 1), f32))
        loss_ref[...] += jnp.broadcast_to(loss, loss_ref.shape)
        dyb = dy.astype(bf16)
        gx_ref[...] = dx
        dws_ref[...] += _dot_tn(sv, dyb)
        dwg_ref[...] += _dot_tn(gv, dyb)
        dgate_ref[...] += dgate
        dlng_ref[...] += dlng
        dlnb_ref[...] += dlnb
        du, dys, _, dz, ddsk, dwglu, dbglu = glu_vjp(_dot_nt(dyb, ws_ref[...]))
        du_ref[...], dys_ref[...], dz_ref[...] = du, dys, dz
        ddsk_ref[...] += ddsk
        dwglu_ref[...] += dwglu
        dbglu_ref[...] += dbglu
        dgdo = _dot_nt(dyb, wg_ref[...])
        for h, (_, hvjp) in enumerate(heads):
            sl = slice(h * GDN_HEAD, (h + 1) * GDN_HEAD)
            do, _, dzg, dnw = hvjp(dgdo[:, sl])
            do_ref[:, sl] = do
            dzg_ref[:, sl] = dzg
            dnw_ref[...] += dnw

    half, full = _tok(T, D_S5), _tok(T, D_MODEL)
    row = _resident((1, D_MODEL))
    wsp = _resident((D_S5, D_MODEL))
    r512, rglu, r128 = _resident((1, D_S5)), _resident((D_S5, D_S5)), _resident((1, GDN_HEAD))
    return pl.pallas_call(
        body, name="tail_fwd_bwd", grid=(B, L // T),
        in_specs=[half] * 7 + [full, full, _per_batch(1, D_MODEL), row, row, wsp, wsp, r512, rglu, r512, r128],
        out_specs=[_per_batch(8, LANES)] + [half] * 5 + [full, wsp, wsp, _per_batch(1, D_MODEL), row, row, r512, rglu, r512,
                                                           r128],
        out_shape=[SDS((B, 8, LANES), f32)] + [SDS((B, L, D_S5), f32)] * 5 + [
            SDS((B, L, D_MODEL), f32), SDS((D_S5, D_MODEL), f32), SDS((D_GDN, D_MODEL), f32), SDS((B, 1, D_MODEL), f32),
            SDS((1, D_MODEL), f32), SDS((1, D_MODEL), f32), SDS((1, D_S5), f32), SDS((D_S5, D_S5), f32), SDS((1, D_S5), f32),
            SDS((1, GDN_HEAD), f32)],
        compiler_params=_cparams(2),
    )(u, y0, y1, z_s5, o0, o1, z_gdn, x, tgt, gate, lng, lnb, ws, wg, dsk, wglu, bglu, nw)


def _adamw_math(w, g, m, v):
    nm = ADAM_B1 * m + (1.0 - ADAM_B1) * g
    nv = ADAM_B2 * v + (1.0 - ADAM_B2) * jnp.square(g)
    m_hat = nm / (1.0 - ADAM_B1 ** ADAM_STEP)
    v_hat = nv / (1.0 - ADAM_B2 ** ADAM_STEP)
    return -ADAM_LR * (m_hat / (jnp.sqrt(v_hat) + ADAM_EPS) + ADAM_WD * w), nm, nv


def _row_tile(rows, cap=512):
    for t in range(min(cap, rows), 15, -1):
        if rows % t == 0 and t % 16 == 0:
            return t
    return rows


def adamw_3d(w, g, m, v, *, lead=False, name):
    R, C = (w.shape[0], w.shape[2]) if lead else w.shape[1:]
    if lead:
        T = next(t for t in range(min(256, R), 0, -1) if R % t == 0)
        spec = pl.BlockSpec((T, 1, C), lambda i: (i, 0, 0))
    else:
        T = _row_tile(R)
        spec = pl.BlockSpec((None, T, C), lambda i: (0, i, 0))

    def body(w_ref, g_ref, m_ref, v_ref, d_ref, nm_ref, nv_ref):
        d_ref[...], nm_ref[...], nv_ref[...] = _adamw_math(w_ref[...], g_ref[...], m_ref[...], v_ref[...])

    return pl.pallas_call(body, name=name, grid=(R // T,), in_specs=[spec] * 4, out_specs=[spec] * 3,
                          out_shape=[SDS(w.shape, f32)] * 3, compiler_params=_cparams(1))(w, g, m, v)


def adamw_small(ws, gs, ms, vs):
    n = len(ws)

    def body(*refs):
        outs = refs[4 * n:]
        for i in range(n):
            d, nm, nv = _adamw_math(refs[i][...], refs[n + i][...], refs[2 * n + i][...], refs[3 * n + i][...])
            outs[i][...], outs[n + i][...], outs[2 * n + i][...] = d, nm, nv

    res = pl.pallas_call(body, name="adamw_small", out_shape=[SDS(w.shape, f32) for w in ws] * 3,
                         compiler_params=pltpu.CompilerParams(vmem_limit_bytes=VMEM_LIMIT))(*ws, *gs, *ms, *vs)
    return res[:n], res[n:2 * n], res[2 * n:]


def sum_cores(own, got, *, name):
    A, H, C = own.shape
    T = _row_tile(H)
    spec = pl.BlockSpec((None, T, C), lambda a, i: (a, i, 0))

    def body(a_ref, b_ref, q32_ref, q16_ref):
        q = a_ref[...] + b_ref[...]
        q32_ref[...] = q
        q16_ref[...] = q.astype(bf16)

    return pl.pallas_call(body, name=name, grid=(A, H // T), in_specs=[spec, spec], out_specs=[spec, spec],
                          out_shape=[SDS((A, H, C), f32), SDS((A, H, C), bf16)], compiler_params=_cparams(2))(own, got)


def sum_chips(mine, rec, cpos, full, *, slot=None, name):
    H, C = mine.shape
    T = _row_tile(H)
    nt = H // T
    by_rows = _by_rows(full)
    out_idx = lambda i, s_ref: ((s_ref[1], s_ref[0] * nt + i, 0) if by_rows else (s_ref[1], i, s_ref[0]))

    def body(s_ref, m_ref, r_ref, f_ref):
        f_ref[...] = ((m_ref[...] + r_ref[0].astype(f32)) + r_ref[1].astype(f32)) + r_ref[2].astype(f32)

    grid_spec = pltpu.PrefetchScalarGridSpec(
        num_scalar_prefetch=1, grid=(nt,),
        in_specs=[pl.BlockSpec((T, C), lambda i, s_ref: (i, 0)), pl.BlockSpec((3, T, C), lambda i, s_ref: (0, i, 0))],
        out_specs=pl.BlockSpec((None, T, C), out_idx))
    scalars = jnp.stack([cpos, jnp.zeros_like(cpos) if slot is None else slot]).astype(jnp.int32)
    return pl.pallas_call(body, name=name, grid_spec=grid_spec,
                          out_shape=SDS((1 if slot is None else 4,) + tuple(full), f32),
                          compiler_params=_cparams(1))(scalars, mine, rec)


CHIP_FLIPS = ((1, 0), (0, 1), (1, 1))


def _pos():
    return lax.axis_index("x"), lax.axis_index("y"), lax.axis_index("c")


def _comm_call(body, srcs, out_sds, n_remote, n_local, name):
    any_spec = pl.BlockSpec(memory_space=pl.ANY)
    return pl.pallas_call(
        body, name=name, in_specs=[any_spec] * len(srcs), out_specs=[any_spec] * len(out_sds), out_shape=out_sds,
        scratch_shapes=[pltpu.SemaphoreType.DMA((n_remote,)), pltpu.SemaphoreType.DMA((n_remote,)),
                        pltpu.SemaphoreType.DMA((max(n_local, 1),))],
        compiler_params=pltpu.CompilerParams(has_side_effects=True),
    )(*srcs)


def _remote(src, dst, send_sems, recv_sems, k, target):
    return pltpu.make_async_remote_copy(src, dst, send_sems.at[k], recv_sems.at[k], device_id=target,
                                        device_id_type=MESH)


def _by_rows(shape):
    return shape[0] % 16 == 0


def _half_shape(shape):
    return (shape[0] // 2, shape[1]) if _by_rows(shape) else (shape[0], shape[1] // 2)


def _half_of(ref, lead, c, shape):
    if _by_rows(shape):
        half = shape[0] // 2
        return ref.at[(*lead, pl.ds(pl.multiple_of(c * half, 8), half))]
    half = shape[1] // 2
    return ref.at[(*lead, slice(None), pl.ds(pl.multiple_of(c * half, LANES), half))]


def gather_shards(shards):
    nt = len(shards)

    def body(*refs):
        srcs, outs = refs[:nt], refs[nt:2 * nt]
        send_sems, recv_sems, _ = refs[2 * nt:]
        x, y, c = _pos()
        j = 2 * x + y
        sib = (x, y, 1 - c)
        own = [_remote(srcs[t], outs[t].at[j], send_sems, recv_sems, 7 * t + 6, sib) for t in range(nt)]
        first, passed = [], []
        for k, (fx, fy) in enumerate(CHIP_FLIPS):
            tx, ty = x ^ fx, y ^ fy
            jk = 2 * tx + ty
            for t in range(nt):
                sh = srcs[t].shape
                first.append(_remote(_half_of(srcs[t], (), c, sh), _half_of(outs[t], (j,), c, sh), send_sems, recv_sems,
                                     7 * t + k, (tx, ty, c)))
                landed = _half_of(outs[t], (jk,), c, sh)
                passed.append(_remote(landed, landed, send_sems, recv_sems, 7 * t + 3 + k, sib))
        for cp in first + own:
            cp.start()
        for a, b in zip(first, passed):
            a.wait_recv()
            b.start()
        for cp in passed + own:
            cp.wait_recv()
        for cp in first + passed + own:
            cp.wait_send()

    return _comm_call(body, shards, [SDS((4,) + s.shape, s.dtype) for s in shards], 7 * nt, 0, "gather_shards")


def swap_halves(ps):
    nt = len(ps)

    def body(*refs):
        srcs, outs = refs[:nt], refs[nt:2 * nt]
        send_sems, recv_sems, _ = refs[2 * nt:]
        x, y, c = _pos()
        cps = [_remote(_half_of(srcs[t], (a,), 1 - c, srcs[t].shape[1:]), outs[t].at[a], send_sems, recv_sems, 4 * t + a,
                       (x, y, 1 - c)) for t in range(nt) for a in range(4)]
        for cp in cps:
            cp.start()
        for cp in cps:
            cp.wait()

    return _comm_call(body, ps, [SDS((4,) + _half_shape(p.shape[1:]), p.dtype) for p in ps], 4 * nt, 0, "swap_halves")


def scatter_to_chips(qs):
    nt = len(qs)

    def body(*refs):
        srcs, outs = refs[:nt], refs[nt:2 * nt]
        send_sems, recv_sems, _ = refs[2 * nt:]
        x, y, c = _pos()
        cps = []
        for k, (fx, fy) in enumerate(CHIP_FLIPS):
            tx, ty = x ^ fx, y ^ fy
            for t in range(nt):
                cps.append(_remote(srcs[t].at[2 * tx + ty], outs[t].at[k], send_sems, recv_sems, 3 * t + k, (tx, ty, c)))
        for cp in cps:
            cp.start()
        for cp in cps:
            cp.wait()

    return _comm_call(body, qs, [SDS((3,) + q.shape[1:], q.dtype) for q in qs], 3 * nt, 0, "scatter_to_chips")


DEV_FLIPS = tuple((fx, fy, fc) for fx in (0, 1) for fy in (0, 1) for fc in (0, 1))[1:]


def join_halves(fs, small, pb):
    nt = len(fs)

    def body(*refs):
        outs = refs[nt + 2:2 * nt + 2]
        sm, pbr = refs[2 * nt + 2], refs[2 * nt + 3]
        send_sems, recv_sems, _ = refs[2 * nt + 4:]
        x, y, c = _pos()
        cps = []
        for t in range(nt):
            mine = _half_of(outs[t], (0,), c, outs[t].shape[1:])
            cps.append(_remote(mine, mine, send_sems, recv_sems, t, (x, y, 1 - c)))
        mine = _half_of(sm, (2 * x + y,), c, sm.shape[1:])
        block = pbr.at[4 * x + 2 * y + c]
        for k, (fx, fy, fc) in enumerate(DEV_FLIPS):
            to = (x ^ fx, y ^ fy, c ^ fc)
            cps.append(_remote(mine, mine, send_sems, recv_sems, nt + k, to))
            cps.append(_remote(block, block, send_sems, recv_sems, nt + len(DEV_FLIPS) + k, to))
        for cp in cps:
            cp.start()
        for cp in cps:
            cp.wait()

    any_spec = pl.BlockSpec(memory_space=pl.ANY)
    n_sem = nt + 2 * len(DEV_FLIPS)
    res = pl.pallas_call(
        body, name="join_halves", in_specs=[any_spec] * (nt + 2), out_specs=[any_spec] * (nt + 2),
        out_shape=[SDS(a.shape, a.dtype) for a in (*fs, small, pb)],
        input_output_aliases={t: t for t in range(nt + 2)},
        scratch_shapes=[pltpu.SemaphoreType.DMA((n_sem,)), pltpu.SemaphoreType.DMA((n_sem,)), pltpu.SemaphoreType.DMA((1,))],
        compiler_params=pltpu.CompilerParams(has_side_effects=True),
    )(*fs, small, pb)
    return res[:nt], res[nt], res[nt + 1]


def gather_devices(block, *, name):
    def body(src, out, send_sems, recv_sems, loc_sems):
        x, y, c = _pos()
        me = 4 * x + 2 * y + c
        mine = pltpu.make_async_copy(src, out.at[me], loc_sems.at[0])
        mine.start()
        cps = [_remote(src, out.at[me], send_sems, recv_sems, k, (x ^ fx, y ^ fy, c ^ fc))
               for k, (fx, fy, fc) in enumerate(DEV_FLIPS)]
        for cp in cps:
            cp.start()
        for cp in cps:
            cp.wait()
        mine.wait()

    return _comm_call(body, [block], [SDS((N_DEV,) + block.shape, block.dtype)], 7, 1, name)[0]


def exchange_devices(blocks, *, name):
    def body(src, out, send_sems, recv_sems, loc_sems):
        x, y, c = _pos()
        me = 4 * x + 2 * y + c
        mine = pltpu.make_async_copy(src.at[me], out.at[me], loc_sems.at[0])
        mine.start()
        cps = []
        for k, (fx, fy, fc) in enumerate(DEV_FLIPS):
            tx, ty, tc = x ^ fx, y ^ fy, c ^ fc
            cps.append(_remote(src.at[4 * tx + 2 * ty + tc], out.at[me], send_sems, recv_sems, k, (tx, ty, tc)))
        for cp in cps:
            cp.start()
        for cp in cps:
            cp.wait()
        mine.wait()

    return _comm_call(body, [blocks], [SDS(blocks.shape, blocks.dtype)], 7, 1, name)[0]


SMALL_SHAPES = ((1, 2, 32, 64), (1, 2, 32, 64), (1, 2, 32), (1, 2, 32, 16, 64),
                (1, 2, 32, 16, 64), (1, 2, 32, 16, 64), (1, 2, 32, 16, 64), (1, D_S5), (1, D_S5), (1, 2, 4), (1, 2, 4),
                (1, GDN_HEAD), (1, D_MODEL), (1, D_MODEL), (LANES,))
SMALL_SWAPPED = (3, 4)


def _size(shape):
    return functools.reduce(lambda p, q: p * q, shape)


SMALL_ROWS = tuple(-(-_size(s) // (8 * LANES)) * 8 for s in SMALL_SHAPES)
SMALL_TOTAL = 2240
SMALL_QUARTER = SMALL_TOTAL // 4


def _rows(a):
    flat = a.reshape(-1)
    pad = (-flat.shape[0]) % (8 * LANES)
    if pad:
        flat = jnp.concatenate([flat, jnp.zeros((pad,), flat.dtype)])
    return flat.reshape(-1, LANES)


def _pack_small(parts):
    rows = [_rows(p) for p in parts]
    rows.append(jnp.zeros((SMALL_TOTAL - sum(SMALL_ROWS), LANES), f32))
    return jnp.concatenate(rows, axis=0)


def _unpack_small(buf):
    out, r = [], 0
    for s, n in zip(SMALL_SHAPES, SMALL_ROWS):
        out.append(buf[r:r + n].reshape(-1)[:_size(s)].reshape(s))
        r += n
    return out


def _as_2d(a):
    return a.reshape(1, -1) if a.ndim == 1 else a.reshape(-1, a.shape[-1])


S5_BG = S5_GROUPS // S5_BLOCKS


def _block_diag_in(bb):
    lead = bb.shape[:-2]
    eye = jnp.eye(S5_BG, dtype=bb.dtype)
    b4 = bb.reshape(lead + (S5_BLOCKS, S5_BG, S5_GROUP, S5_STATE))
    return jnp.einsum('...jgcp,gh->...jgchp', b4, eye).reshape(lead + (S5_BLOCKS, S5_BC, S5_BS))


def _block_diag_in_t(d):
    lead = d.shape[:-3]
    d6 = d.reshape(lead + (S5_BLOCKS, S5_BG, S5_GROUP, S5_BG, S5_STATE))
    return jnp.einsum('...jgcgp->...jgcp', d6).reshape(lead + (S5_GROUPS, S5_GROUP * S5_STATE))


def _block_diag_out(cm):
    lead = cm.shape[:-3]
    eye = jnp.eye(S5_BG, dtype=cm.dtype)
    c4 = cm.reshape(lead + (S5_BLOCKS, S5_BG, S5_GROUP, S5_STATE))
    return jnp.einsum('...jgcp,gh->...jhpgc', c4, eye).reshape(lead + (S5_BLOCKS, S5_BS, S5_BC))


def _block_diag_out_t(d):
    lead = d.shape[:-3]
    d6 = d.reshape(lead + (S5_BLOCKS, S5_BG, S5_STATE, S5_BG, S5_GROUP))
    return jnp.einsum('...jgpgc->...jgcp', d6).reshape(lead + (S5_GROUPS, S5_GROUP, S5_STATE))


def _to_chunk_rows(a):
    B, L, W = a.shape
    return a.reshape(B, L // CHUNK, CHUNK, W).transpose(0, 1, 3, 2)


def _from_chunk_rows(a):
    B, nc, W, _ = a.shape
    return a.transpose(0, 1, 3, 2).reshape(B, nc * CHUNK, W)


def local_step(x, ctx, tgt, m, w_in, lam_re, lam_im, log_dt, b_re, b_im, c_re, c_im, s5_d,
               w_glu, b_glu, conv16, a_log, dt_bias, norm_w, w_out, ln_g, ln_b):
    B, L, _ = x.shape
    zeros_state = jnp.zeros((B, GDN_HEADS, GDN_HEAD, GDN_HEAD), f32)

    shift, scale, gate = m[:B, :D_MODEL], m[:B, D_MODEL:2 * D_MODEL], m[:B, 2 * D_MODEL:]
    mod = jnp.stack([scale, shift], axis=1)
    mod_c = jnp.broadcast_to(jnp.stack([m[B, D_MODEL:2 * D_MODEL], m[B, :D_MODEL]], axis=0)[None], (B, 2, D_MODEL))

    u, z_s5, qkv, z_gdn, ba = in_proj_fwd(x, mod, w_in, name="in_proj_fwd")
    uc, _, qkvc, _, bac = in_proj_fwd(ctx, mod_c, w_in, name="in_proj_fwd_ctx")

    ng = N_DIR * S5_GROUPS
    zoh_in = (lam_re.reshape(ng, S5_STATE), lam_im.reshape(ng, S5_STATE), log_dt.reshape(ng, 1),
              b_re.reshape(ng, S5_GROUP * S5_STATE), b_im.reshape(ng, S5_GROUP * S5_STATE))
    expand = (jnp.arange(S5_GROUP * S5_STATE)[None, :] % S5_STATE == jnp.arange(S5_STATE)[:, None]).astype(f32)
    ar, ai, bbr, bbi = s5_zoh_fwd(*zoh_in, expand)
    b_blocks = _block_diag_in(jnp.stack([bbr, bbi]).astype(bf16).reshape(2, N_DIR, S5_GROUPS, S5_GROUP * S5_STATE))
    c_blocks = _block_diag_out(jnp.stack([c_re, -c_im]).astype(bf16).reshape(2, N_DIR, S5_GROUPS, S5_GROUP, S5_STATE))
    a_rows = jnp.stack([ar, ai]).reshape(2, N_DIR, S5_HALF)
    s5w, ys, hins, hins_c, hss, hss_c = [], [], [], [], [], []
    for d in range(N_DIR):
        wd = (b_blocks[0, d], b_blocks[1, d], c_blocks[0, d], c_blocks[1, d], a_rows[:, d])
        s5w.append(wd)
        hs_c, hin_c, hend_c = s5_scan_fwd(uc, *wd, jnp.zeros((B, 2, S5_HALF), f32), d=d, need_y=False,
                                          name=f"s5_fwd_ctx{d}")
        y_d, hs_d, hin, _ = s5_scan_fwd(u, *wd, hend_c, d=d, need_y=True, name=f"s5_fwd{d}")
        hss.append(hs_d)
        hss_c.append(hs_c)
        ys.append(y_d)
        hins.append(hin)
        hins_c.append(hin_c)
    glu_w = (s5_d.reshape(1, D_S5), w_glu, b_glu.reshape(1, D_S5))

    act, pre = conv_fwd(qkv, conv16, is_ctx=False, name="conv_fwd")
    act_c, pre_c = conv_fwd(qkvc, conv16, is_ctx=True, name="conv_fwd_ctx")
    pad8 = jnp.zeros((1, 8), f32)
    alog16 = jnp.concatenate([pad8, a_log.reshape(1, 8)], axis=1)
    dtb16 = jnp.concatenate([pad8, dt_bias.reshape(1, 8)], axis=1)
    bg = gates_fwd(ba, alog16, dtb16, name="gates_fwd")
    bg_c = gates_fwd(bac, alog16, dtb16, name="gates_fwd_ctx")
    bgr, bgr_c = _to_chunk_rows(bg), _to_chunk_rows(bg_c)
    cks_c, ns_c, s_c = gdn_fwd(act_c, bg_c, bgr_c, (zeros_state, zeros_state), need_o=False, name="gdn_fwd_ctx")
    os_, cks, ns, _ = gdn_fwd(act, bg, bgr, s_c, need_o=True, name="gdn_fwd")
    nw = norm_w.reshape(1, GDN_HEAD)

    (loss8, du_skip, dy, dz_s5, do, dz_gdn, gx_res, dws, dwg, dgate, dlng, dlnb, d_s5_d, d_w_glu, d_b_glu,
     d_norm_w) = tail_fwd_bwd(u, ys[0], ys[1], z_s5, os_[0], os_[1], z_gdn, x, tgt, gate[:, None, :],
                              ln_g.reshape(1, D_MODEL), ln_b.reshape(1, D_MODEL), w_out[:D_S5], w_out[D_S5:], *glu_w, nw)
    loss = jnp.sum(loss8[:, 0, 0])
    d_w_out = jnp.concatenate([dws, dwg], axis=0)

    dacts, dbgs, dbgrs, ds0s = gdn_bwd(act, bg, bgr, cks, ns, do, (zeros_state, zeros_state), name="gdn_bwd")
    dacts_c, dbgs_c, dbgrs_c, _ = gdn_bwd(act_c, bg_c, bgr_c, cks_c, ns_c, None, ds0s, name="gdn_bwd_ctx")
    dbg = dbgs[0] + dbgs[1] + _from_chunk_rows(dbgrs[0] + dbgrs[1])
    dbg_c = dbgs_c[0] + dbgs_c[1] + _from_chunk_rows(dbgrs_c[0] + dbgrs_c[1])
    dba, dal, ddt = gates_bwd(ba, alog16, dtb16, dbg, name="gates_bwd")
    dbac, dal_c, ddt_c = gates_bwd(bac, alog16, dtb16, dbg_c, name="gates_bwd_ctx")
    d_a_log = (dal + dal_c)[:, 8:].reshape(1, N_DIR, GDN_HEADS)
    d_dt_bias = (ddt + ddt_c)[:, 8:].reshape(1, N_DIR, GDN_HEADS)
    dqkv, dcw = conv_bwd(qkv, pre, conv16, dacts[0], dacts[1], is_ctx=False, name="conv_bwd")
    dqkvc, dcw_c = conv_bwd(qkvc, pre_c, conv16, dacts_c[0], dacts_c[1], is_ctx=True, name="conv_bwd_ctx")
    d_conv16 = jnp.sum(dcw, axis=0) + jnp.sum(dcw_c, axis=0)

    dus, ducs = [du_skip], []
    das, dbs, dcs = [], [], []
    for d in range(N_DIR):
        du_d, dbre1, dbim1, dct1, dcb1, da1, dh0 = s5_scan_bwd(u, dy, hss[d], *s5w[d], hins[d],
                                                                jnp.zeros((B, 2, S5_HALF), f32), d=d, name=f"s5_bwd{d}")
        duc_d, dbre2, dbim2, _, _, da2, _ = s5_scan_bwd(uc, None, hss_c[d], *s5w[d], hins_c[d], dh0, d=d,
                                                        name=f"s5_bwd_ctx{d}")
        dus.append(du_d)
        ducs.append(duc_d)
        das.append(da1 + da2)
        dbs.append(jnp.stack([dbre1 + dbre2, dbim1 + dbim2]))
        dcs.append(jnp.stack([dct1, dcb1]))
    ng_shape = (N_DIR * S5_GROUPS, -1)
    da = jnp.stack(das, axis=1)
    db = _block_diag_in_t(jnp.stack(dbs, axis=1))
    dc = _block_diag_out_t(jnp.stack(dcs, axis=1))
    dlr, dli, dldt, dbre, dbim = s5_zoh_bwd(*zoh_in, expand, da[0].reshape(ng_shape), da[1].reshape(ng_shape),
                                            db[0].reshape(ng_shape), db[1].reshape(ng_shape))
    d_s5 = (dlr, dli, dldt, dbre, dbim, dc[0], -dc[1])

    zc = jnp.zeros_like(uc)
    dw_c, dmod_c = in_proj_bwd(ctx, mod_c, (tuple(ducs), zc, dqkvc, zc, dbac), w_in, None, None,
                               name="in_proj_bwd_ctx")
    d_w_in, dmod, grad_x = in_proj_bwd(x, mod, (tuple(dus), dz_s5, dqkv, dz_gdn, dba), w_in, gx_res, dw_c,
                                       name="in_proj_bwd")
    dmod_c = jnp.sum(dmod_c, axis=0)

    dm_rows = jnp.concatenate([dmod[:, 1], dmod[:, 0], dgate[:, 0]], axis=1)
    dm_ctx = jnp.concatenate([dmod_c[1], dmod_c[0], jnp.zeros((D_MODEL,), f32)])[None]
    dm = jnp.concatenate([dm_rows, dm_ctx], axis=0)
    small = (*d_s5, d_s5_d, d_b_glu, d_a_log, d_dt_bias, d_norm_w, dlng, dlnb)
    small = tuple(g.reshape(s) for g, s in zip(small, SMALL_SHAPES))
    return loss, grad_x, (d_w_in, d_w_out, d_w_glu, d_conv16), small, dm


SHARDED = (1, 3, 18, 12, 14)
UNSHARDED = tuple(i for i in range(21) if i not in SHARDED)
SMALL = tuple(i for i in UNSHARDED if i not in (0, 2))
W_IN_SHARD = 772


def _conv_rows(w):
    return jnp.concatenate([w.reshape(9, w.shape[-1]), jnp.zeros((CONV_ROWS - 9, w.shape[-1]), f32)], axis=0)


def kernel(x, c, ctx, c_ctx, w_ada, b_ada, w_in, s5_lambda_re, s5_lambda_im, s5_log_dt, s5_b_re, s5_b_im, s5_c_re, s5_c_im, s5_d, w_glu, b_glu, conv_w, gdn_a_log, gdn_dt_bias, gdn_norm_w, w_out, ln_g, ln_b, loss_target, m_c_ctx, m_w_ada, m_b_ada, m_w_in, m_s5_lambda_re, m_s5_lambda_im, m_s5_log_dt, m_s5_b_re, m_s5_b_im, m_s5_c_re, m_s5_c_im, m_s5_d, m_w_glu, m_b_glu, m_conv_w, m_gdn_a_log, m_gdn_dt_bias, m_gdn_norm_w, m_w_out, m_ln_g, m_ln_b, v_c_ctx, v_w_ada, v_b_ada, v_w_in, v_s5_lambda_re, v_s5_lambda_im, v_s5_log_dt, v_s5_b_re, v_s5_b_im, v_s5_c_re, v_s5_c_im, v_s5_d, v_w_glu, v_b_glu, v_conv_w, v_gdn_a_log, v_gdn_dt_bias, v_gdn_norm_w, v_w_out, v_ln_g, v_ln_b):
    weights = [c_ctx, w_ada, b_ada, w_in, s5_lambda_re, s5_lambda_im, s5_log_dt, s5_b_re, s5_b_im, s5_c_re, s5_c_im,
               s5_d, w_glu, b_glu, conv_w, gdn_a_log, gdn_dt_bias, gdn_norm_w, w_out, ln_g, ln_b]
    ms = [m_c_ctx, m_w_ada, m_b_ada, m_w_in, m_s5_lambda_re, m_s5_lambda_im, m_s5_log_dt, m_s5_b_re, m_s5_b_im,
          m_s5_c_re, m_s5_c_im, m_s5_d, m_w_glu, m_b_glu, m_conv_w, m_gdn_a_log, m_gdn_dt_bias, m_gdn_norm_w, m_w_out,
          m_ln_g, m_ln_b]
    vs = [v_c_ctx, v_w_ada, v_b_ada, v_w_in, v_s5_lambda_re, v_s5_lambda_im, v_s5_log_dt, v_s5_b_re, v_s5_b_im,
          v_s5_c_re, v_s5_c_im, v_s5_d, v_w_glu, v_b_glu, v_conv_w, v_gdn_a_log, v_gdn_dt_bias, v_gdn_norm_w, v_w_out,
          v_ln_g, v_ln_b]
    cpos = lax.axis_index("c")
    jchip = 2 * lax.axis_index("x") + lax.axis_index("y")

    c_all = gather_devices(c, name="gather_c")
    cc = jnp.concatenate([c_all, jnp.broadcast_to(c_ctx[None, None, :], (N_DEV, 1, D_MODEL)),
                          jnp.zeros((N_DEV, 5, D_MODEL), f32)], axis=1)
    w_ada16 = w_ada[0].astype(bf16)
    b_cols = lax.dynamic_slice_in_dim(b_ada, jchip * ADA_SHARD, ADA_SHARD, axis=1)
    m_mine = exchange_devices(ada_fwd(cc, w_ada16, b_cols), name="exchange_m")
    m_rows = jnp.concatenate([m_mine[2 * j, :3] for j in range(4)], axis=1)

    conv_shard = _conv_rows(conv_w)
    g_in, g_out, g_glu, g_conv = gather_shards(
        [jnp.transpose(w_in[0]).astype(bf16), w_out[0].astype(bf16), w_glu[0].astype(bf16), conv_shard])
    w_in_t = g_in.reshape(P_IN, D_MODEL)
    conv16 = g_conv.transpose(1, 0, 2).reshape(CONV_ROWS, 3 * D_GDN)

    swap = lambda a: jnp.swapaxes(a, 3, 4)
    loss, grad_x, big, small, dm_rows = local_step(
        x, ctx, loss_target, m_rows, w_in_t, s5_lambda_re, s5_lambda_im, s5_log_dt, swap(s5_b_re), swap(s5_b_im),
        s5_c_re, s5_c_im, s5_d, g_glu.reshape(D_S5, D_S5), b_glu, conv16, gdn_a_log, gdn_dt_bias, gdn_norm_w,
        g_out.reshape(D_MODEL, D_MODEL), ln_g, ln_b)
    me = 2 * jchip + cpos
    loss_hi = loss.astype(bf16).astype(f32)
    loss_row = jnp.zeros((LANES,), f32).at[me].set(loss_hi).at[N_DEV + me].set(loss - loss_hi)

    dm8 = jnp.concatenate([dm_rows, jnp.zeros((5, 3 * D_MODEL), f32)], axis=0)
    dm_by_chip = dm8.reshape(8, 4, ADA_SHARD).transpose(1, 0, 2)
    dm_cols = exchange_devices(jnp.repeat(dm_by_chip, 2, axis=0), name="exchange_dm")
    g_w_ada, pb = ada_bwd(cc, w_ada16, dm_cols, me)

    d_w_in, d_w_out, d_w_glu, d_conv16 = big
    slabs = [d_w_in.reshape(4, W_IN_SHARD, D_MODEL),
             d_w_out.reshape(4, D_MODEL // 4, D_MODEL),
             d_w_glu.reshape(4, D_S5 // 4, D_S5),
             d_conv16.reshape(CONV_ROWS, 4, 3 * D_GDN // 4).transpose(1, 0, 2),
             _pack_small(small + (loss_row,)).reshape(4, SMALL_QUARTER, LANES)]
    got = swap_halves(slabs)
    q32, q16 = [], []
    for t, (s, g) in enumerate(zip(slabs, got)):
        if _by_rows(s.shape[1:]):
            own = lax.dynamic_index_in_dim(s.reshape(4, 2, s.shape[1] // 2, s.shape[2]), cpos, axis=1, keepdims=False)
        else:
            own = lax.dynamic_slice_in_dim(s, cpos * (s.shape[2] // 2), s.shape[2] // 2, axis=2)
        a, b = sum_cores(own, g, name=f"sum_cores{t}")
        q32.append(a)
        q16.append(b)
    rec = scatter_to_chips(q16)
    fs = [sum_chips(lax.dynamic_index_in_dim(q, jchip, axis=0, keepdims=False), r, cpos, s.shape[1:],
                    slot=jchip if t == 4 else None, name=f"sum_chips{t}")
          for t, (q, r, s) in enumerate(zip(q32, rec, slabs))]
    red, small_all, pb_all = join_halves(fs[:4], fs[4], pb)
    g_c_ctx = c_ctx_bwd(pb_all, c_ctx[None, :])[0]
    g_b_ada = jnp.concatenate([pb_all[2 * j, 1:2, :ADA_SHARD] for j in range(4)], axis=1)
    g_small = _unpack_small(small_all.reshape(SMALL_TOTAL, LANES))
    loss = jnp.sum(g_small[-1][:2 * N_DEV])
    g_small = g_small[:-1]
    g_shard = {1: g_w_ada, 3: red[0], 18: red[1], 12: red[2], 14: red[3]}

    grads, deltas, new_m, new_v = [None] * 21, [None] * 21, [None] * 21, [None] * 21
    for t, i in enumerate(SHARDED):
        conv, win = i == 14, i == 3
        prep = (lambda a: _conv_rows(a)[None]) if conv else ((lambda a: jnp.transpose(a, (2, 0, 1))) if win else (lambda a: a))
        g = jnp.transpose(g_shard[i], (1, 0, 2)) if win else g_shard[i]
        d, nm, nv = adamw_3d(prep(weights[i]), g, prep(ms[i]), prep(vs[i]), lead=win, name=f"adamw{t}")
        for lst, val in ((grads, g), (deltas, d), (new_m, nm), (new_v, nv)):
            lst[i] = (val[0, :9].reshape(weights[i].shape) if conv else (jnp.transpose(val, (1, 2, 0)) if win else val))
    g_un = {0: g_c_ctx, 2: g_b_ada, **{i: g_small[n] for n, i in enumerate(SMALL)}}
    swapped = [SMALL[n] for n in SMALL_SWAPPED]
    small_in = lambda lst: [_as_2d(swap(lst[i]) if i in swapped else lst[i]) for i in UNSHARDED]
    sm = adamw_small(small_in(weights), [_as_2d(g_un[i]) for i in UNSHARDED], small_in(ms), small_in(vs))
    for n, i in enumerate(UNSHARDED):
        back = ((lambda a: swap(a.reshape(swap(weights[i]).shape))) if i in swapped
                else (lambda a: a.reshape(weights[i].shape)))
        grads[i] = back(g_un[i])
        for lst, res in ((deltas, sm[0]), (new_m, sm[1]), (new_v, sm[2])):
            lst[i] = back(res[n])
    return (loss, grad_x, *grads, *deltas, *new_m, *new_v)
```

```python
import functools

import jax
import jax.numpy as jnp
from jax import lax
from jax.experimental import pallas as pl
from jax.experimental.pallas import tpu as pltpu

f32 = jnp.float32
bf16 = jnp.bfloat16
SDS = jax.ShapeDtypeStruct

D_MODEL = 1024
D_S5 = 512
S5_GROUP = 16
S5_GROUPS = 32
S5_STATE = 64
S5_HALF = S5_GROUPS * S5_STATE
D_GDN = 512
GDN_HEAD = 128
GDN_HEADS = 4
CHUNK = 64
GRID_W = 64
N_DIR = 2
P_IN = 3088
DEEPNORM_ALPHA = 2.0 ** 0.25
LN_EPS = 1e-5
NORM_EPS = 1e-6
ADAM_LR, ADAM_B1, ADAM_B2, ADAM_EPS, ADAM_WD, ADAM_STEP = 0.001, 0.9, 0.999, 1e-08, 0.01, 10

LANES = 128
VMEM_LIMIT = 56 * 1024 * 1024
TOK_TILE = 256
S5_TILE = 256
MESH = pl.DeviceIdType.MESH


def _cparams(n_grid):
    return pltpu.CompilerParams(dimension_semantics=("arbitrary",) * n_grid, vmem_limit_bytes=VMEM_LIMIT)


def _dot(a, b):
    return jnp.dot(a.astype(bf16), b.astype(bf16), preferred_element_type=f32)


def _dot_nt(a, b):
    return lax.dot_general(a.astype(bf16), b.astype(bf16), (((1,), (1,)), ((), ())), preferred_element_type=f32)


def _dot_tn(a, b):
    return lax.dot_general(a.astype(bf16), b.astype(bf16), (((0,), (0,)), ((), ())), preferred_element_type=f32)


def _dot_hi(a, b):
    return jnp.dot(a, b, precision=lax.Precision.HIGHEST, preferred_element_type=f32)


@jax.custom_vjp
def _mm(a, b):
    return _dot(a, b)


@jax.custom_vjp
def _mm_nt(a, b):
    return _dot_nt(a, b)


@jax.custom_vjp
def _mm_tn(a, b):
    return _dot_tn(a, b)


_mm.defvjp(lambda a, b: (_dot(a, b), (a, b)), lambda r, g: (_mm_nt(g, r[1]), _mm_tn(r[0], g)))
_mm_nt.defvjp(lambda a, b: (_dot_nt(a, b), (a, b)), lambda r, g: (_mm(g, r[1]), _mm_tn(g, r[0])))
_mm_tn.defvjp(lambda a, b: (_dot_tn(a, b), (a, b)), lambda r, g: (_mm_nt(r[1], g), _mm(r[0], g)))


def _silu(x):
    return x * jax.nn.sigmoid(x)


def _gelu(x):
    return 0.5 * x * (1.0 + lax.erf(x * (2.0 ** -0.5)))


def _resident(shape):
    nd = len(shape)
    return pl.BlockSpec(shape, lambda *_: (0,) * nd, pipeline_mode=pl.Buffered(1))


def _tok(tile, width, nt=None, rev=False):
    if rev:
        return pl.BlockSpec((None, tile, width), lambda b, n: (b, nt - 1 - n, 0))
    return pl.BlockSpec((None, tile, width), lambda b, n: (b, n, 0))


def _per_batch(rows, width):
    return pl.BlockSpec((None, rows, width), lambda b, n: (b, 0, 0))


def _first_step():
    return jnp.logical_and(pl.program_id(0) == 0, pl.program_id(1) == 0)


ADA_SHARD = 3 * D_MODEL // 4
N_DEV = 8


def ada_fwd(cc, w, b):
    def body(cc_ref, w_ref, b_ref, m_ref):
        for k in range(N_DEV):
            m_ref[k] = _dot(_silu(cc_ref[k]), w_ref[...]) + b_ref[...]

    return pl.pallas_call(body, name="ada_fwd", out_shape=SDS((N_DEV, 8, ADA_SHARD), f32),
                          compiler_params=pltpu.CompilerParams(vmem_limit_bytes=VMEM_LIMIT))(cc, w, b)


def ada_bwd(cc, w, dmj):
    def body(cc_ref, w_ref, dmj_ref, dw_ref, pb_ref):
        dw = jnp.zeros((D_MODEL, ADA_SHARD), f32)
        p = jnp.zeros((8, D_MODEL), f32)
        db = jnp.zeros((1, ADA_SHARD), f32)
        for k in range(N_DEV):
            dw = dw + _dot_tn(_silu(cc_ref[k]), dmj_ref[k])
            p = p + _dot_nt(dmj_ref[k], w_ref[...])
            db = db + jnp.sum(dmj_ref[k], axis=0, keepdims=True)
        dw_ref[0] = dw
        pb_ref[...] = jnp.zeros_like(pb_ref)
        pb_ref[0:1, :] = p[2:3, :]
        pb_ref[1:2, 0:ADA_SHARD] = db

    return pl.pallas_call(
        body, name="ada_bwd", out_shape=[SDS((1, D_MODEL, ADA_SHARD), f32), SDS((8, D_MODEL), f32)],
        compiler_params=pltpu.CompilerParams(vmem_limit_bytes=VMEM_LIMIT))(cc, w, dmj)


def c_ctx_bwd(pb_all, c_ctx):
    def body(p_ref, c_ref, d_ref):
        ds = ((p_ref[0, 0:1, :] + p_ref[2, 0:1, :]) + p_ref[4, 0:1, :]) + p_ref[6, 0:1, :]
        _, vjp = jax.vjp(_silu, c_ref[...])
        d_ref[...] = vjp(ds)[0]

    return pl.pallas_call(body, name="c_ctx_bwd", out_shape=SDS((1, D_MODEL), f32))(pb_all, c_ctx)


N_GATE = 2 * N_DIR * GDN_HEADS
IN_WIDTHS = (D_S5, D_S5, 3 * D_GDN, D_GDN, N_GATE)
IN_OFFS = (0, 512, 1024, 2560, 3072)


def in_proj_fwd(x, mod, wt, *, name):
    B, L, _ = x.shape
    T = min(2 * TOK_TILE, L)

    def body(x_ref, mod_ref, w_ref, *o_refs):
        h = (x_ref[...] * (1.0 + mod_ref[0:1, :]) + mod_ref[1:2, :]).astype(bf16)
        for o_ref, off, wd in zip(o_refs, IN_OFFS, IN_WIDTHS):
            o_ref[...] = _dot_nt(h, w_ref[off:off + wd, :])

    return pl.pallas_call(
        body, name=name, grid=(B, L // T),
        in_specs=[_tok(T, D_MODEL), _per_batch(2, D_MODEL), _resident((P_IN, D_MODEL))],
        out_specs=[_tok(T, wd) for wd in IN_WIDTHS],
        out_shape=[SDS((B, L, wd), f32) for wd in IN_WIDTHS],
        compiler_params=_cparams(2),
    )(x, mod, wt)


def in_proj_bwd(x, mod, ds, wt, gx_res, dw_start, *, name):
    B, L, _ = x.shape
    T = min(TOK_TILE, L)
    with_dx = gx_res is not None
    with_start = dw_start is not None
    n_u = len(ds[0])

    def body(*refs):
        x_ref, mod_ref = refs[0], refs[1]
        du_refs = refs[2:2 + n_u]
        d_refs = refs[2 + n_u:6 + n_u]
        w_ref = refs[6 + n_u]
        k = 7 + n_u
        if with_dx:
            gx_ref = refs[k]
            k += 1
        if with_start:
            start_ref = refs[k]
            k += 1
        dw_ref, dmod_ref = refs[k], refs[k + 1]
        if with_dx:
            dx_ref = refs[k + 2]
        n = pl.program_id(1)

        @pl.when(_first_step())
        def _():
            dw_ref[...] = start_ref[...] if with_start else jnp.zeros_like(dw_ref)

        @pl.when(n == 0)
        def _():
            dmod_ref[...] = jnp.zeros_like(dmod_ref)

        xv = x_ref[...]
        scale1 = 1.0 + mod_ref[0:1, :]
        h = (xv * scale1 + mod_ref[1:2, :]).astype(bf16)
        du = du_refs[0][...]
        for r in du_refs[1:]:
            du = du + r[...]
        dh = jnp.zeros((T, D_MODEL), f32)
        for dv, off, wd in zip([du] + [r[...] for r in d_refs], IN_OFFS, IN_WIDTHS):
            dv = dv.astype(bf16)
            dh = dh + _dot(dv, w_ref[off:off + wd, :])
            dw_ref[off:off + wd, :] += _dot_tn(dv, h)
        dmod_ref[0:1, :] += jnp.sum(dh * xv, axis=0, keepdims=True)
        dmod_ref[1:2, :] += jnp.sum(dh, axis=0, keepdims=True)
        if with_dx:
            dx_ref[...] = gx_ref[...] + dh * scale1

    in_specs = ([_tok(T, D_MODEL), _per_batch(2, D_MODEL)] + [_tok(T, D_S5)] * n_u + [_tok(T, wd) for wd in IN_WIDTHS[1:]]
                + [_resident((P_IN, D_MODEL))])
    args = [x, mod, *ds[0], *ds[1:], wt]
    out_specs = [_resident((P_IN, D_MODEL)), _per_batch(2, D_MODEL)]
    out_shape = [SDS((P_IN, D_MODEL), f32), SDS((B, 2, D_MODEL), f32)]
    if with_dx:
        in_specs.append(_tok(T, D_MODEL))
        args.append(gx_res)
        out_specs.append(_tok(T, D_MODEL))
        out_shape.append(SDS((B, L, D_MODEL), f32))
    if with_start:
        in_specs.append(_resident((P_IN, D_MODEL)))
        args.append(dw_start)
    return pl.pallas_call(body, name=name, grid=(B, L // T), in_specs=in_specs, out_specs=out_specs,
                          out_shape=out_shape, compiler_params=_cparams(2))(*args)


def _s5_zoh(lr, li, ldt, bre, bim, expand):
    dt = jnp.exp(ldt)
    zr, zi = lr * dt, li * dt
    e = jnp.exp(zr)
    ar, ai = e * jnp.cos(zi), e * jnp.sin(zi)
    den = lr * lr + li * li
    czr = ((ar - 1.0) * lr + ai * li) / den
    czi = (ai * lr - (ar - 1.0) * li) / den
    czr_e, czi_e = _dot_hi(czr, expand), _dot_hi(czi, expand)
    return ar, ai, czr_e * bre - czi_e * bim, czr_e * bim + czi_e * bre


_ZOH_OUT = [(N_DIR * S5_GROUPS, S5_STATE)] * 2 + [(N_DIR * S5_GROUPS, S5_STATE * S5_GROUP)] * 2


def s5_zoh_fwd(lr, li, ldt, bre, bim, expand):
    def body(lr_ref, li_ref, ldt_ref, bre_ref, bim_ref, e_ref, ar_ref, ai_ref, bbr_ref, bbi_ref):
        ar, ai, bbr, bbi = _s5_zoh(lr_ref[...], li_ref[...], ldt_ref[...], bre_ref[...], bim_ref[...], e_ref[...])
        ar_ref[...], ai_ref[...], bbr_ref[...], bbi_ref[...] = ar, ai, bbr, bbi

    return pl.pallas_call(body, name="s5_zoh_fwd", out_shape=[SDS(s, f32) for s in _ZOH_OUT])(
        lr, li, ldt, bre, bim, expand)


def s5_zoh_bwd(lr, li, ldt, bre, bim, expand, dar, dai, dbbr, dbbi):
    def body(lr_ref, li_ref, ldt_ref, bre_ref, bim_ref, e_ref, dar_ref, dai_ref, dbbr_ref, dbbi_ref,
             dlr_ref, dli_ref, dldt_ref, dbre_ref, dbim_ref):
        ev = e_ref[...]
        _, vjp = jax.vjp(lambda a, b, c, d, e: _s5_zoh(a, b, c, d, e, ev),
                         lr_ref[...], li_ref[...], ldt_ref[...], bre_ref[...], bim_ref[...])
        outs = vjp((dar_ref[...], dai_ref[...], dbbr_ref[...], dbbi_ref[...]))
        dlr_ref[...], dli_ref[...], dldt_ref[...], dbre_ref[...], dbim_ref[...] = outs

    shapes = [lr.shape, li.shape, ldt.shape, bre.shape, bim.shape]
    return pl.pallas_call(body, name="s5_zoh_bwd", out_shape=[SDS(s, f32) for s in shapes])(
        lr, li, ldt, bre, bim, expand, dar, dai, dbbr, dbbi)


def _scan_rows(T, rev, ar, ai, h0s, refs, off):
    def step(i, carry):
        t = off + ((T - 1 - i) if rev else i)
        out = []
        for (hr, hi), (r_ref, i_ref) in zip(carry, refs):
            nr = ar * hr - ai * hi + r_ref[pl.ds(t, 1), :]
            ni = ar * hi + ai * hr + i_ref[pl.ds(t, 1), :]
            r_ref[pl.ds(t, 1), :] = nr
            i_ref[pl.ds(t, 1), :] = ni
            out.append((nr, ni))
        return tuple(out)

    return lax.fori_loop(0, T, step, tuple(h0s))


S5_BLOCKS = 4
S5_BC = D_S5 // S5_BLOCKS
S5_BS = S5_HALF // S5_BLOCKS


def _s5_in(uv, bre_ref, bim_ref, hr_ref, hi_ref, off, T):
    for jb in range(S5_BLOCKS):
        uj = uv[:, jb * S5_BC:(jb + 1) * S5_BC]
        hr_ref[off:off + T, jb * S5_BS:(jb + 1) * S5_BS] = _dot(uj, bre_ref[jb])
        hi_ref[off:off + T, jb * S5_BS:(jb + 1) * S5_BS] = _dot(uj, bim_ref[jb])


def _s5_specs(B, T, nt, rev):
    tidx = (lambda n: nt - 1 - n) if rev else (lambda n: n)
    tok = pl.BlockSpec((B, T, D_S5), lambda n: (0, tidx(n), 0))
    hin = pl.BlockSpec((B, None, 2, S5_HALF), lambda n: (0, tidx(n), 0, 0))
    state = pl.BlockSpec((B, 2, S5_HALF), lambda n: (0, 0, 0))
    return tok, hin, state


def s5_scan_fwd(u, bre, bim, ctop, cbot, arow, h0, *, d, need_y, name):
    B, L, _ = u.shape
    T = min(S5_TILE, L)
    nt = L // T
    rev = d == 1

    def body(u_ref, bre_ref, bim_ref, ct_ref, cb_ref, a_ref, h0_ref, *rest):
        if need_y:
            y_ref, hs_ref, hin_ref, hend_ref, hr_scr, hi_scr, h_scr = rest
        else:
            hs_ref, hin_ref, hend_ref, hr_scr, hi_scr, h_scr = rest
        n = pl.program_id(0)

        @pl.when(n == 0)
        def _():
            h_scr[...] = h0_ref[...]

        hin_ref[...] = h_scr[...]
        for b in range(B):
            _s5_in(u_ref[b].astype(bf16), bre_ref, bim_ref, hr_scr.at[b], hi_scr.at[b], 0, T)
        hs = _scan_rows(T, rev, a_ref[0:1, :], a_ref[1:2, :], [(h_scr[b, 0:1, :], h_scr[b, 1:2, :]) for b in range(B)],
                        [(hr_scr.at[b], hi_scr.at[b]) for b in range(B)], 0)
        for b in range(B):
            h_scr[b, 0:1, :] = hs[b][0]
            h_scr[b, 1:2, :] = hs[b][1]
            hs_ref[b, :, 0:S5_HALF] = hr_scr[b].astype(bf16)
            hs_ref[b, :, S5_HALF:2 * S5_HALF] = hi_scr[b].astype(bf16)
            if need_y:
                for jb in range(S5_BLOCKS):
                    st = slice(jb * S5_BS, (jb + 1) * S5_BS)
                    y_ref[b, :, jb * S5_BC:(jb + 1) * S5_BC] = (_dot(hr_scr[b, :, st], ct_ref[jb])
                                                                 + _dot(hi_scr[b, :, st], cb_ref[jb]))

        @pl.when(n == nt - 1)
        def _():
            hend_ref[...] = h_scr[...]

    tok, hin_spec, state = _s5_specs(B, T, nt, rev)
    hs_spec = pl.BlockSpec((B, T, 2 * S5_HALF), tok.index_map)
    out_specs = [hs_spec, hin_spec, state]
    out_shape = [SDS((B, L, 2 * S5_HALF), bf16), SDS((B, nt, 2, S5_HALF), f32), SDS((B, 2, S5_HALF), f32)]
    if need_y:
        out_specs.insert(0, tok)
        out_shape.insert(0, SDS((B, L, D_S5), f32))
    w_in, w_out = _resident((S5_BLOCKS, S5_BC, S5_BS)), _resident((S5_BLOCKS, S5_BS, S5_BC))
    return pl.pallas_call(
        body, name=name, grid=(nt,),
        in_specs=[tok, w_in, w_in, w_out, w_out, _resident((2, S5_HALF)), state],
        out_specs=out_specs, out_shape=out_shape,
        scratch_shapes=[pltpu.VMEM((B, T, S5_HALF), f32), pltpu.VMEM((B, T, S5_HALF), f32),
                        pltpu.VMEM((B, 2, S5_HALF), f32)],
        compiler_params=_cparams(1),
    )(u, bre, bim, ctop, cbot, arow, h0)


def s5_scan_bwd(u, dy, hs, bre, bim, ctop, cbot, arow, hin, dhend, *, d, name):
    B, L, _ = u.shape
    T = min(S5_TILE, L)
    nt = L // T
    rev = d == 1
    has_dy = dy is not None
    PAD = 8

    def body(*refs):
        u_ref = refs[0]
        k = 1
        if has_dy:
            dy_ref = refs[1]
            k = 2
        hs_ref = refs[k]
        k += 1
        bre_ref, bim_ref, ct_ref, cb_ref, a_ref, hin_ref, dhend_ref = refs[k:k + 7]
        du_ref, dbre_ref, dbim_ref, dct_ref, dcb_ref, da_ref, dh0_ref = refs[k + 7:k + 14]
        hr_scr, hi_scr, gr_scr, gi_scr, p_scr = refs[k + 14:]
        n = pl.program_id(0)

        @pl.when(n == 0)
        def _():
            for r in (dbre_ref, dbim_ref, dct_ref, dcb_ref, da_ref):
                r[...] = jnp.zeros_like(r)
            p_scr[...] = dhend_ref[...]

        ar, ai = a_ref[0:1, :], a_ref[1:2, :]
        prev_row = PAD + T if rev else PAD - 1
        uvs = []
        for b in range(B):
            uvs.append(u_ref[b].astype(bf16))
            hr_scr[b, PAD:PAD + T, :] = hs_ref[b, :, 0:S5_HALF].astype(f32)
            hi_scr[b, PAD:PAD + T, :] = hs_ref[b, :, S5_HALF:2 * S5_HALF].astype(f32)
            hr_scr[b, prev_row:prev_row + 1, :] = hin_ref[b, 0:1, :]
            hi_scr[b, prev_row:prev_row + 1, :] = hin_ref[b, 1:2, :]
        if has_dy:
            for b in range(B):
                dyv = dy_ref[b].astype(bf16)
                for jb in range(S5_BLOCKS):
                    st = slice(jb * S5_BS, (jb + 1) * S5_BS)
                    dyj = dyv[:, jb * S5_BC:(jb + 1) * S5_BC]
                    gr_scr[b, :, st] = _dot_nt(dyj, ct_ref[jb])
                    gi_scr[b, :, st] = _dot_nt(dyj, cb_ref[jb])
                    dct_ref[jb] += _dot_tn(hr_scr[b, PAD:PAD + T, st], dyj)
                    dcb_ref[jb] += _dot_tn(hi_scr[b, PAD:PAD + T, st], dyj)
        else:
            gr_scr[...] = jnp.zeros_like(gr_scr)
            gi_scr[...] = jnp.zeros_like(gi_scr)

        def step(i, carry):
            t = i if rev else T - 1 - i
            tp = PAD + t + (1 if rev else -1)
            out = []
            for b, (pr, pi, dar, dai) in enumerate(carry):
                gr = gr_scr[b, pl.ds(t, 1), :] + pr
                gi = gi_scr[b, pl.ds(t, 1), :] + pi
                gr_scr[b, pl.ds(t, 1), :] = gr
                gi_scr[b, pl.ds(t, 1), :] = gi
                hpr = hr_scr[b, pl.ds(tp, 1), :]
                hpi = hi_scr[b, pl.ds(tp, 1), :]
                out.append((ar * gr + ai * gi, ar * gi - ai * gr, dar + hpr * gr + hpi * gi, dai + hpr * gi - hpi * gr))
            return tuple(out)

        zero = jnp.zeros((1, S5_HALF), f32)
        res = lax.fori_loop(0, T, step, tuple((p_scr[b, 0:1, :], p_scr[b, 1:2, :], zero, zero) for b in range(B)))
        for b in range(B):
            pr, pi, dar, dai = res[b]
            p_scr[b, 0:1, :] = pr
            p_scr[b, 1:2, :] = pi
            da_ref[0:1, :] += dar
            da_ref[1:2, :] += dai
            for jb in range(S5_BLOCKS):
                st = slice(jb * S5_BS, (jb + 1) * S5_BS)
                ch = slice(jb * S5_BC, (jb + 1) * S5_BC)
                gr_j = gr_scr[b, :, st].astype(bf16)
                gi_j = gi_scr[b, :, st].astype(bf16)
                du_ref[b, :, ch] = _dot_nt(gr_j, bre_ref[jb]) + _dot_nt(gi_j, bim_ref[jb])
                dbre_ref[jb] += _dot_tn(uvs[b][:, ch], gr_j)
                dbim_ref[jb] += _dot_tn(uvs[b][:, ch], gi_j)

        @pl.when(n == nt - 1)
        def _():
            dh0_ref[...] = p_scr[...]

    tok, hin_spec, state = _s5_specs(B, T, nt, not rev)
    hs_spec = pl.BlockSpec((B, T, 2 * S5_HALF), tok.index_map)
    w_in, w_out = _resident((S5_BLOCKS, S5_BC, S5_BS)), _resident((S5_BLOCKS, S5_BS, S5_BC))
    wspecs = [w_in, w_in, w_out, w_out]
    in_specs = [tok] + ([tok] if has_dy else []) + [hs_spec] + wspecs + [_resident((2, S5_HALF)), hin_spec, state]
    args = [u] + ([dy] if has_dy else []) + [hs, bre, bim, ctop, cbot, arow, hin, dhend]
    return pl.pallas_call(
        body, name=name, grid=(nt,), in_specs=in_specs,
        out_specs=[tok] + wspecs + [_resident((2, S5_HALF)), state],
        out_shape=[SDS((B, L, D_S5), f32), SDS((S5_BLOCKS, S5_BC, S5_BS), f32), SDS((S5_BLOCKS, S5_BC, S5_BS), f32),
                   SDS((S5_BLOCKS, S5_BS, S5_BC), f32), SDS((S5_BLOCKS, S5_BS, S5_BC), f32), SDS((2, S5_HALF), f32),
                   SDS((B, 2, S5_HALF), f32)],
        scratch_shapes=[pltpu.VMEM((B, T + 2 * PAD, S5_HALF), f32), pltpu.VMEM((B, T + 2 * PAD, S5_HALF), f32),
                        pltpu.VMEM((B, T, S5_HALF), f32), pltpu.VMEM((B, T, S5_HALF), f32),
                        pltpu.VMEM((B, 2, S5_HALF), f32)],
        compiler_params=_cparams(1),
    )(*args)


def _glu_fn(u, y0, y1, z, dsk, wg, bg):
    g = _gelu(dsk * u + y0 + y1)
    return g * jax.nn.sigmoid(_mm(g, wg) + bg) * _silu(z)


CONV_ROWS = 16


def _shift(x, s):
    L = x.shape[0]
    k = (-s) % L
    return x if k == 0 else pltpu.roll(x, k, axis=0)


def _r16(v):
    return v.astype(bf16).astype(f32)


def _conv_masks(L, is_ctx):
    t = lax.broadcasted_iota(jnp.int32, (L, 1), 0)
    if is_ctx:
        return t == L - 1, t == 0, None, None
    col = jnp.bitwise_and(t, GRID_W - 1)
    return col == GRID_W - 1, col == 0, t >= GRID_W, t < L - GRID_W


def _conv_sides(xv, masks):
    no_left, no_right, _, _ = masks
    return _shift(jnp.where(no_left, 0.0, xv), -1), _shift(jnp.where(no_right, 0.0, xv), 1)


def _conv_pre(xv, w_ref, masks, is_ctx):
    xv = _r16(xv)
    wv = _r16(w_ref[...])
    xl, xr = _conv_sides(xv, masks)
    z = [wv[3 * di:3 * di + 1, :] * xl + wv[3 * di + 1:3 * di + 2, :] * xv + wv[3 * di + 2:3 * di + 3, :] * xr
         for di in ((1,) if is_ctx else (0, 1, 2))]
    if is_ctx:
        return z[0]
    _, _, has_up, has_down = masks
    return z[1] + jnp.where(has_up, _shift(z[0], -GRID_W), 0.0) + jnp.where(has_down, _shift(z[2], GRID_W), 0.0)


def _conv_pre_bwd(xv, w_ref, dpre, masks, is_ctx, dw_ref):
    no_left, no_right, has_up, has_down = masks
    xv, dpre, wv = _r16(xv), _r16(dpre), _r16(w_ref[...])
    xl, xr = _conv_sides(xv, masks)
    if is_ctx:
        dz = {1: dpre}
    else:
        dz = {0: _shift(jnp.where(has_up, dpre, 0.0), GRID_W), 1: dpre, 2: _shift(jnp.where(has_down, dpre, 0.0), -GRID_W)}
    dxl = dxc = dxr = None
    for di, d in dz.items():
        for dj, side in enumerate((xl, xv, xr)):
            dw_ref[3 * di + dj:3 * di + dj + 1, :] = jnp.sum(d * side, axis=0, keepdims=True)
        tl, tc, tr = (wv[3 * di + dj:3 * di + dj + 1, :] * d for dj in range(3))
        dxl, dxc, dxr = (tl, tc, tr) if dxl is None else (dxl + tl, dxc + tc, dxr + tr)
    return dxc + jnp.where(no_left, 0.0, _shift(dxl, 1)) + jnp.where(no_right, 0.0, _shift(dxr, -1))


def _qk_post(pre, is_norm, scale):
    s = _silu(pre)
    nrm = lax.rsqrt(jnp.sum(s * s, axis=-1, keepdims=True) + NORM_EPS)
    return s * jnp.where(is_norm, nrm * scale, 1.0)


def _conv_tile(L):
    return D_GDN if L <= 512 else 2 * GDN_HEAD


def _conv_kind(W):
    head = pl.program_id(1) * (W // GDN_HEAD)
    return head < 2 * GDN_HEADS, jnp.where(head < GDN_HEADS, GDN_HEAD ** -0.5, 1.0).astype(f32)


def _conv_specs(L, W):
    spec = pl.BlockSpec((None, L, W), lambda b, ct: (b, 0, ct))
    wspec = pl.BlockSpec((CONV_ROWS, W), lambda b, ct: (0, ct))
    dwspec = pl.BlockSpec((None, CONV_ROWS, W), lambda b, ct: (b, 0, ct))
    return spec, wspec, dwspec


def conv_fwd(qkv, w16, *, is_ctx, name):
    B, L, C = qkv.shape
    W = _conv_tile(L)
    spec, wspec, _ = _conv_specs(L, W)

    def body(x_ref, w_ref, o_ref, pre_ref):
        is_norm, scale = _conv_kind(W)
        pre = _conv_pre(x_ref[...], w_ref, _conv_masks(L, is_ctx), is_ctx)
        pre_ref[...] = pre
        for h in range(W // GDN_HEAD):
            sl = slice(h * GDN_HEAD, (h + 1) * GDN_HEAD)
            o_ref[:, sl] = _qk_post(pre[:, sl], is_norm, scale)

    return pl.pallas_call(body, name=name, grid=(B, C // W), in_specs=[spec, wspec], out_specs=[spec, spec],
                          out_shape=[SDS((B, L, C), f32)] * 2, compiler_params=_cparams(2))(qkv, w16)


def conv_bwd(qkv, pre, w16, da0, da1, *, is_ctx, name):
    B, L, C = qkv.shape
    W = _conv_tile(L)
    spec, wspec, dwspec = _conv_specs(L, W)

    def body(x_ref, pre_ref, w_ref, d0_ref, d1_ref, dx_ref, dw_ref):
        is_norm, scale = _conv_kind(W)
        dpre = []
        for h in range(W // GDN_HEAD):
            sl = slice(h * GDN_HEAD, (h + 1) * GDN_HEAD)
            _, vjp = jax.vjp(lambda p: _qk_post(p, is_norm, scale), pre_ref[:, sl])
            dpre.append(vjp(d0_ref[:, sl] + d1_ref[:, sl])[0])
        dpre = dpre[0] if len(dpre) == 1 else jnp.concatenate(dpre, axis=1)
        dw_ref[...] = jnp.zeros_like(dw_ref)
        dx_ref[...] = _conv_pre_bwd(x_ref[...], w_ref, dpre, _conv_masks(L, is_ctx), is_ctx, dw_ref)

    return pl.pallas_call(body, name=name, grid=(B, C // W), in_specs=[spec, spec, wspec, spec, spec],
                          out_specs=[spec, dwspec], out_shape=[SDS((B, L, C), f32), SDS((B, CONV_ROWS, C), f32)],
                          compiler_params=_cparams(2))(qkv, pre, w16, da0, da1)


def _gates_fn(ba, alog, dtb):
    T = ba.shape[0]
    nck = T // CHUNK
    lane = lax.broadcasted_iota(jnp.int32, ba.shape, 1)
    ii = lax.broadcasted_iota(jnp.int32, (nck, CHUNK, CHUNK), 1)
    jj = lax.broadcasted_iota(jnp.int32, (nck, CHUNK, CHUNK), 2)
    g = jnp.where(lane >= 8, -jnp.exp(alog) * jax.nn.softplus(ba + dtb), 0.0)
    g3 = g.reshape(nck, CHUNK, N_GATE)
    chunk_sum = lambda tri: lax.dot_general(tri.astype(f32), g3, (((2,), (1,)), ((0,), (0,))),
                                            precision=lax.Precision.HIGHEST, preferred_element_type=f32).reshape(T, N_GATE)
    gc = jnp.where(lane >= 12, chunk_sum(ii <= jj), chunk_sum(ii >= jj))
    return jnp.where(lane < 8, jax.nn.sigmoid(ba), gc)


def gates_fwd(ba, alog, dtb, *, name):
    B, L, _ = ba.shape
    T = min(TOK_TILE, L)
    t = _tok(T, N_GATE)

    def body(ba_ref, al_ref, dt_ref, o_ref):
        o_ref[...] = _gates_fn(ba_ref[...], al_ref[...], dt_ref[...])

    return pl.pallas_call(body, name=name, grid=(B, L // T),
                          in_specs=[t, _resident((1, N_GATE)), _resident((1, N_GATE))], out_specs=t,
                          out_shape=SDS((B, L, N_GATE), f32), compiler_params=_cparams(2))(ba, alog, dtb)


def gates_bwd(ba, alog, dtb, dbg, *, name):
    B, L, _ = ba.shape
    T = min(TOK_TILE, L)
    t = _tok(T, N_GATE)
    small = _resident((1, N_GATE))

    def body(ba_ref, al_ref, dt_ref, d_ref, dba_ref, dal_ref, ddt_ref):
        @pl.when(_first_step())
        def _():
            dal_ref[...] = jnp.zeros_like(dal_ref)
            ddt_ref[...] = jnp.zeros_like(ddt_ref)

        _, vjp = jax.vjp(_gates_fn, ba_ref[...], al_ref[...], dt_ref[...])
        dba, dal, ddt = vjp(d_ref[...])
        dba_ref[...] = dba
        dal_ref[...] += dal
        ddt_ref[...] += ddt

    return pl.pallas_call(body, name=name, grid=(B, L // T), in_specs=[t, small, small, t],
                          out_specs=[t, small, small],
                          out_shape=[SDS((B, L, N_GATE), f32), SDS((1, N_GATE), f32), SDS((1, N_GATE), f32)],
                          compiler_params=_cparams(2))(ba, alog, dtb, dbg)


@jax.custom_vjp
def _inv_unit_tri(mats):
    n = mats[0].shape[0]
    eye = (lax.broadcasted_iota(jnp.int32, (n, n), 0) == lax.broadcasted_iota(jnp.int32, (n, n), 1)).astype(f32)
    xs = [eye - a for a in mats]
    sq = [_dot(a, a) for a in mats]
    ps = sq
    k = 2
    while k < n:
        xs = [x + _dot(x, p) for x, p in zip(xs, ps)]
        k *= 2
        if k < n:
            ps = [_dot(p, p) for p in ps]
    return tuple(_dot(p, x) - a for p, x, a in zip(sq, xs, mats))


def _inv_unit_tri_fwd(mats):
    ns = _inv_unit_tri(mats)
    return ns, ns


def _inv_unit_tri_bwd(ns, dns):
    ys = [dn + _dot_tn(nn, dn) for nn, dn in zip(ns, dns)]
    return (tuple(-(y + _dot_nt(y, nn)) for y, nn in zip(ys, ns)),)


_inv_unit_tri.defvjp(_inv_unit_tri_fwd, _inv_unit_tri_bwd)


@jax.custom_vjp
def _inv_unit_tri_saved(mats, saved):
    return saved


_inv_unit_tri_saved.defvjp(lambda mats, saved: (saved, saved),
                           lambda ns, dns: _inv_unit_tri_bwd(ns, dns) + (tuple(jnp.zeros_like(n) for n in ns),))


def _gdn_chunk(heads, *, revs, saved=None, with_n=False):
    n = heads[0][0].shape[0]
    ii = lax.broadcasted_iota(jnp.int32, (n, n), 0)
    jj = lax.broadcasted_iota(jnp.int32, (n, n), 1)
    row = lax.broadcasted_iota(jnp.int32, (n, 1), 0)
    lower = {False: ii >= jj, True: ii <= jj}
    strict = {False: ii > jj, True: ii < jj}
    last = {False: n - 1, True: 0}
    H = range(len(heads))
    q, k, v, beta, gc, gr, s = (list(t) for t in zip(*heads))
    decay = [jnp.where(lower[revs[h]], jnp.exp(jnp.where(lower[revs[h]], gc[h] - gr[h], 0.0)), 0.0) for h in H]
    kk = [_mm_nt(k[h], k[h]) for h in H]
    qk = [_mm_nt(q[h], k[h]) * decay[h] for h in H]
    qs = [_mm(q[h], s[h]) for h in H]
    a_mat = tuple(jnp.where(strict[revs[h]], beta[h] * kk[h] * decay[h], 0.0) for h in H)
    gamma = [jnp.exp(gc[h]) for h in H]
    g_last = [jnp.sum(jnp.where(row == last[revs[h]], gc[h], 0.0), axis=0, keepdims=True) for h in H]
    nmat = _inv_unit_tri(a_mat) if saved is None else _inv_unit_tri_saved(a_mat, saved)
    bv = [beta[h] * v[h] for h in H]
    bk = [(beta[h] * gamma[h]) * k[h] for h in H]
    u0 = [bv[h] + _mm(nmat[h], bv[h]) for h in H]
    w = [bk[h] + _mm(nmat[h], bk[h]) for h in H]
    k_out = [k[h] * jnp.exp(g_last[h] - gc[h]) for h in H]
    u = [u0[h] - _mm(w[h], s[h]) for h in H]
    o = [gamma[h] * qs[h] + _mm(qk[h], u[h]) for h in H]
    s_new = [jnp.exp(g_last[h]) * s[h] + _mm_tn(k_out[h], u[h]) for h in H]
    outs = tuple((o[h], s_new[h]) for h in H)
    return (outs, nmat) if with_n else outs


def _gdn_specs(B, nc, rev):
    def cidx(n):
        return (nc - 1 - n) if rev else n
    tok = lambda width: pl.BlockSpec((B, CHUNK, width), lambda n: (0, cidx(n), 0))
    rowspec = pl.BlockSpec((B, None, N_GATE, CHUNK), lambda n: (0, cidx(n), 0, 0))
    st = pl.BlockSpec((B, GDN_HEADS, GDN_HEAD, GDN_HEAD), lambda n: (0, 0, 0, 0))
    ck = pl.BlockSpec((B, None, GDN_HEADS, GDN_HEAD, GDN_HEAD), lambda n: (0, cidx(n), 0, 0, 0))
    nsp = pl.BlockSpec((B, None, GDN_HEADS, CHUNK, CHUNK), lambda n: (0, cidx(n), 0, 0, 0))
    return tok, rowspec, st, ck, nsp


def _gdn_head_args(qkv_ref, bg_ref, bgr_ref, b, d, h):
    col = d * GDN_HEADS + h
    q = qkv_ref[b, :, h * GDN_HEAD:(h + 1) * GDN_HEAD]
    k = qkv_ref[b, :, D_GDN + h * GDN_HEAD:D_GDN + (h + 1) * GDN_HEAD]
    v = qkv_ref[b, :, 2 * D_GDN + h * GDN_HEAD:2 * D_GDN + (h + 1) * GDN_HEAD]
    bgv = bg_ref[b]
    return q, k, v, bgv[:, col:col + 1], bgv[:, 8 + col:9 + col], bgr_ref[b][8 + col:9 + col, :]


def _gdn_chains(B):
    return [(d, b, h) for d in range(N_DIR) for b in range(B) for h in range(GDN_HEADS)]


def gdn_fwd(qkv, bg, bgr, s0s, *, need_o, name):
    B, L, _ = qkv.shape
    nc = L // CHUNK
    specs = [_gdn_specs(B, nc, d == 1) for d in range(N_DIR)]
    chains = _gdn_chains(B)
    state_shape = (B, GDN_HEADS, GDN_HEAD, GDN_HEAD)

    def body(*refs):
        ins = [refs[3 * d:3 * d + 3] for d in range(N_DIR)]
        s0_refs = refs[6:8]
        k = 8
        o_refs = refs[k:k + 2] if need_o else None
        k += 2 if need_o else 0
        ck_refs, n_refs, sf_refs, s_scrs = refs[k:k + 2], refs[k + 2:k + 4], refs[k + 4:k + 6], refs[k + 6:k + 8]
        n = pl.program_id(0)

        @pl.when(n == 0)
        def _():
            for d in range(N_DIR):
                s_scrs[d][...] = s0_refs[d][...]

        for d in range(N_DIR):
            ck_refs[d][...] = s_scrs[d][...]
        heads = tuple(_gdn_head_args(*ins[d], b, d, h) + (s_scrs[d][b, h],) for d, b, h in chains)
        outs, nmat = _gdn_chunk(heads, revs=tuple(d == 1 for d, _, _ in chains), with_n=True)
        for (d, b, h), (o, s_new), nn in zip(chains, outs, nmat):
            if need_o:
                o_refs[d][b, :, h * GDN_HEAD:(h + 1) * GDN_HEAD] = o
            s_scrs[d][b, h] = s_new
            n_refs[d][b, h] = nn

        @pl.when(n == nc - 1)
        def _():
            for d in range(N_DIR):
                sf_refs[d][...] = s_scrs[d][...]

    in_specs, out_o, out_ck, out_n, out_sf = [], [], [], [], []
    for tok, rowspec, st, ck, nsp in specs:
        in_specs += [tok(3 * D_GDN), tok(N_GATE), rowspec]
        out_o.append(tok(D_GDN))
        out_ck.append(ck)
        out_n.append(nsp)
        out_sf.append(st)
    in_specs += [specs[0][2]] * 2
    out_specs = (out_o if need_o else []) + out_ck + out_n + out_sf
    out_shape = (([SDS((B, L, D_GDN), f32)] * 2 if need_o else []) + [SDS((B, nc) + state_shape[1:], f32)] * 2
                 + [SDS((B, nc, GDN_HEADS, CHUNK, CHUNK), f32)] * 2 + [SDS(state_shape, f32)] * 2)
    res = pl.pallas_call(
        body, name=name, grid=(nc,), in_specs=in_specs, out_specs=out_specs, out_shape=out_shape,
        scratch_shapes=[pltpu.VMEM(state_shape, f32)] * 2, compiler_params=_cparams(1),
    )(qkv, bg, bgr, qkv, bg, bgr, *s0s)
    if need_o:
        return res[0:2], res[2:4], res[4:6], res[6:8]
    return res[0:2], res[2:4], res[4:6]


def gdn_bwd(qkv, bg, bgr, cks, ns, do, dsfs, *, name):
    B, L, _ = qkv.shape
    nc = L // CHUNK
    has_do = do is not None
    specs = [_gdn_specs(B, nc, d != 1) for d in range(N_DIR)]
    chains = _gdn_chains(B)
    state_shape = (B, GDN_HEADS, GDN_HEAD, GDN_HEAD)
    per_dir = 6 if has_do else 5

    def body(*refs):
        ins = [refs[per_dir * d:per_dir * d + per_dir] for d in range(N_DIR)]
        k = per_dir * N_DIR
        dsf_refs = refs[k:k + 2]
        outs = [refs[k + 2 + 3 * d:k + 5 + 3 * d] for d in range(N_DIR)]
        ds0_refs, ds_scrs = refs[k + 8:k + 10], refs[k + 10:k + 12]
        n = pl.program_id(0)

        @pl.when(n == 0)
        def _():
            for d in range(N_DIR):
                ds_scrs[d][...] = dsf_refs[d][...]

        lane = lax.broadcasted_iota(jnp.int32, (CHUNK, N_GATE), 1)
        sub = lax.broadcasted_iota(jnp.int32, (N_GATE, CHUNK), 0)
        heads = tuple(_gdn_head_args(*ins[d][:3], b, d, h) + (ins[d][3][b, h],) for d, b, h in chains)
        saved = tuple(ins[d][4][b, h] for d, b, h in chains)
        _, vjp = jax.vjp(functools.partial(_gdn_chunk, revs=tuple(d == 1 for d, _, _ in chains), saved=saved), heads)
        zero = jnp.zeros((CHUNK, GDN_HEAD), f32)
        cts = tuple(((ins[d][5][b, :, h * GDN_HEAD:(h + 1) * GDN_HEAD] if has_do else zero), ds_scrs[d][b, h])
                    for d, b, h in chains)
        (dheads,) = vjp(cts)
        dbg_acc = [[jnp.zeros((CHUNK, N_GATE), f32) for _ in range(B)] for _ in range(N_DIR)]
        dbgr_acc = [[jnp.zeros((N_GATE, CHUNK), f32) for _ in range(B)] for _ in range(N_DIR)]
        for (d, b, h), (dq, dk, dv, db, dgc, dgr, ds) in zip(chains, dheads):
            col = d * GDN_HEADS + h
            dqkv_ref = outs[d][0]
            dqkv_ref[b, :, h * GDN_HEAD:(h + 1) * GDN_HEAD] = dq
            dqkv_ref[b, :, D_GDN + h * GDN_HEAD:D_GDN + (h + 1) * GDN_HEAD] = dk
            dqkv_ref[b, :, 2 * D_GDN + h * GDN_HEAD:2 * D_GDN + (h + 1) * GDN_HEAD] = dv
            dbg_acc[d][b] = dbg_acc[d][b] + jnp.where(lane == col, db, 0.0) + jnp.where(lane == 8 + col, dgc, 0.0)
            dbgr_acc[d][b] = dbgr_acc[d][b] + jnp.where(sub == 8 + col, dgr, 0.0)
            ds_scrs[d][b, h] = ds
        for d in range(N_DIR):
            for b in range(B):
                outs[d][1][b] = dbg_acc[d][b]
                outs[d][2][b] = dbgr_acc[d][b]

        @pl.when(n == nc - 1)
        def _():
            for d in range(N_DIR):
                ds0_refs[d][...] = ds_scrs[d][...]

    in_specs, args, out_specs, out_shape = [], [], [], []
    for d, (tok, rowspec, st, ck, nsp) in enumerate(specs):
        in_specs += [tok(3 * D_GDN), tok(N_GATE), rowspec, ck, nsp] + ([tok(D_GDN)] if has_do else [])
        args += [qkv, bg, bgr, cks[d], ns[d]] + ([do] if has_do else [])
        out_specs += [tok(3 * D_GDN), tok(N_GATE), rowspec]
        out_shape += [SDS((B, L, 3 * D_GDN), f32), SDS((B, L, N_GATE), f32), SDS((B, nc, N_GATE, CHUNK), f32)]
    st = specs[0][2]
    in_specs += [st, st]
    args += list(dsfs)
    out_specs += [st, st]
    out_shape += [SDS(state_shape, f32)] * 2
    res = pl.pallas_call(
        body, name=name, grid=(nc,), in_specs=in_specs, out_specs=out_specs, out_shape=out_shape,
        scratch_shapes=[pltpu.VMEM(state_shape, f32)] * 2, compiler_params=_cparams(1),
    )(*args)
    return (res[0], res[3]), (res[1], res[4]), (res[2], res[5]), (res[6], res[7])


def _gnorm_fn(o0, o1, z, w):
    o = o0 + o1
    return o * lax.rsqrt(jnp.mean(o * o, axis=-1, keepdims=True) + NORM_EPS) * w * _silu(z)


def _head_loss(y, x, gate, lng, lnb, tgt):
    r = DEEPNORM_ALPHA * x + gate * y
    mu = jnp.mean(r, axis=-1, keepdims=True)
    rc = r - mu
    var = jnp.mean(rc * rc, axis=-1, keepdims=True)
    err = rc * lax.rsqrt(var + LN_EPS) * lng + lnb - tgt
    return (0.5 / D_MODEL) * jnp.sum(jnp.sum(err * err, axis=-1, keepdims=True), axis=0, keepdims=True)


def tail_fwd_bwd(u, y0, y1, z_s5, o0, o1, z_gdn, x, tgt, gate, lng, lnb, ws, wg, dsk, wglu, bglu, nw):
    B, L, _ = x.shape
    T = min(TOK_TILE, L)

    def body(u_ref, y0_ref, y1_ref, z_ref, o0_ref, o1_ref, zg_ref, x_ref, t_ref, gate_ref, lng_ref, lnb_ref, ws_ref,
             wg_ref, dsk_ref, wglu_ref, bglu_ref, nw_ref,
             loss_ref, du_ref, dys_ref, dz_ref, do_ref, dzg_ref, gx_ref, dws_ref, dwg_ref, dgate_ref, dlng_ref, dlnb_ref,
             ddsk_ref, dwglu_ref, dbglu_ref, dnw_ref):
        n = pl.program_id(1)

        @pl.when(_first_step())
        def _():
            for r in (dws_ref, dwg_ref, dlng_ref, dlnb_ref, ddsk_ref, dwglu_ref, dbglu_ref, dnw_ref):
                r[...] = jnp.zeros_like(r)

        @pl.when(n == 0)
        def _():
            loss_ref[...] = jnp.zeros_like(loss_ref)
            dgate_ref[...] = jnp.zeros_like(dgate_ref)

        s5o, glu_vjp = jax.vjp(_glu_fn, u_ref[...], y0_ref[...], y1_ref[...], z_ref[...], dsk_ref[...],
                               wglu_ref[...].astype(f32), bglu_ref[...])
        heads = []
        for h in range(GDN_HEADS):
            sl = slice(h * GDN_HEAD, (h + 1) * GDN_HEAD)
            heads.append(jax.vjp(_gnorm_fn, o0_ref[:, sl], o1_ref[:, sl], zg_ref[:, sl], nw_ref[...]))
        sv = s5o.astype(bf16)
        gv = jnp.concatenate([out for out, _ in heads], axis=1).astype(bf16)
        y = _dot(sv, ws_ref[...]) + _dot(gv, wg_ref[...])
        loss, vjp = jax.vjp(lambda *a: _head_loss(*a, t_ref[...]), y, x_ref[...], gate_ref[...], lng_ref[...],
                            lnb_ref[...])
        dy, dx, dgate, dlng, dlnb = vjp(jnp.ones((1, 1), f32))
        loss_ref[...] += jnp.broadcast_to(loss, loss_ref.shape)
        dyb = dy.astype(bf16)
        gx_ref[...] = dx
        dws_ref[...] += _dot_tn(sv, dyb)
        dwg_ref[...] += _dot_tn(gv, dyb)
        dgate_ref[...] += dgate
        dlng_ref[...] += dlng
        dlnb_ref[...] += dlnb
        du, dys, _, dz, ddsk, dwglu, dbglu = glu_vjp(_dot_nt(dyb, ws_ref[...]))
        du_ref[...], dys_ref[...], dz_ref[...] = du, dys, dz
        ddsk_ref[...] += ddsk
        dwglu_ref[...] += dwglu
        dbglu_ref[...] += dbglu
        dgdo = _dot_nt(dyb, wg_ref[...])
        for h, (_, hvjp) in enumerate(heads):
            sl = slice(h * GDN_HEAD, (h + 1) * GDN_HEAD)
            do, _, dzg, dnw = hvjp(dgdo[:, sl])
            do_ref[:, sl] = do
            dzg_ref[:, sl] = dzg
            dnw_ref[...] += dnw

    half, full = _tok(T, D_S5), _tok(T, D_MODEL)
    row = _resident((1, D_MODEL))
    wsp = _resident((D_S5, D_MODEL))
    r512, rglu, r128 = _resident((1, D_S5)), _resident((D_S5, D_S5)), _resident((1, GDN_HEAD))
    return pl.pallas_call(
        body, name="tail_fwd_bwd", grid=(B, L // T),
        in_specs=[half] * 7 + [full, full, _per_batch(1, D_MODEL), row, row, wsp, wsp, r512, rglu, r512, r128],
        out_specs=[_per_batch(8, LANES)] + [half] * 5 + [full, wsp, wsp, _per_batch(1, D_MODEL), row, row, r512, rglu, r512,
                                                           r128],
        out_shape=[SDS((B, 8, LANES), f32)] + [SDS((B, L, D_S5), f32)] * 5 + [
            SDS((B, L, D_MODEL), f32), SDS((D_S5, D_MODEL), f32), SDS((D_GDN, D_MODEL), f32), SDS((B, 1, D_MODEL), f32),
            SDS((1, D_MODEL), f32), SDS((1, D_MODEL), f32), SDS((1, D_S5), f32), SDS((D_S5, D_S5), f32), SDS((1, D_S5), f32),
            SDS((1, GDN_HEAD), f32)],
        compiler_params=_cparams(2),
    )(u, y0, y1, z_s5, o0, o1, z_gdn, x, tgt, gate, lng, lnb, ws, wg, dsk, wglu, bglu, nw)


def _adamw_math(w, g, m, v):
    nm = ADAM_B1 * m + (1.0 - ADAM_B1) * g
    nv = ADAM_B2 * v + (1.0 - ADAM_B2) * jnp.square(g)
    m_hat = nm / (1.0 - ADAM_B1 ** ADAM_STEP)
    v_hat = nv / (1.0 - ADAM_B2 ** ADAM_STEP)
    return -ADAM_LR * (m_hat / (jnp.sqrt(v_hat) + ADAM_EPS) + ADAM_WD * w), nm, nv


def _row_tile(rows, cap=512):
    for t in range(min(cap, rows), 15, -1):
        if rows % t == 0 and t % 16 == 0:
            return t
    return rows


def adamw_3d(w, g, m, v, *, lead=False, name):
    R, C = (w.shape[0], w.shape[2]) if lead else w.shape[1:]
    if lead:
        T = next(t for t in range(min(256, R), 0, -1) if R % t == 0)
        spec = pl.BlockSpec((T, 1, C), lambda i: (i, 0, 0))
    else:
        T = _row_tile(R)
        spec = pl.BlockSpec((None, T, C), lambda i: (0, i, 0))

    def body(w_ref, g_ref, m_ref, v_ref, d_ref, nm_ref, nv_ref):
        d_ref[...], nm_ref[...], nv_ref[...] = _adamw_math(w_ref[...], g_ref[...], m_ref[...], v_ref[...])

    return pl.pallas_call(body, name=name, grid=(R // T,), in_specs=[spec] * 4, out_specs=[spec] * 3,
                          out_shape=[SDS(w.shape, f32)] * 3, compiler_params=_cparams(1))(w, g, m, v)


def adamw_small(ws, gs, ms, vs):
    n = len(ws)

    def body(*refs):
        outs = refs[4 * n:]
        for i in range(n):
            d, nm, nv = _adamw_math(refs[i][...], refs[n + i][...], refs[2 * n + i][...], refs[3 * n + i][...])
            outs[i][...], outs[n + i][...], outs[2 * n + i][...] = d, nm, nv

    res = pl.pallas_call(body, name="adamw_small", out_shape=[SDS(w.shape, f32) for w in ws] * 3,
                         compiler_params=pltpu.CompilerParams(vmem_limit_bytes=VMEM_LIMIT))(*ws, *gs, *ms, *vs)
    return res[:n], res[n:2 * n], res[2 * n:]


def sum_cores(own, got, *, name):
    A, H, C = own.shape
    T = _row_tile(H)
    spec = pl.BlockSpec((None, T, C), lambda a, i: (a, i, 0))

    def body(a_ref, b_ref, q32_ref, q16_ref):
        q = a_ref[...] + b_ref[...]
        q32_ref[...] = q
        q16_ref[...] = q.astype(bf16)

    return pl.pallas_call(body, name=name, grid=(A, H // T), in_specs=[spec, spec], out_specs=[spec, spec],
                          out_shape=[SDS((A, H, C), f32), SDS((A, H, C), bf16)], compiler_params=_cparams(2))(own, got)


def sum_chips(mine, rec, cpos, full, *, slot=None, name):
    H, C = mine.shape
    T = _row_tile(H)
    nt = H // T
    by_rows = _by_rows(full)
    out_idx = lambda i, s_ref: ((s_ref[1], s_ref[0] * nt + i, 0) if by_rows else (s_ref[1], i, s_ref[0]))

    def body(s_ref, m_ref, r_ref, f_ref):
        f_ref[...] = ((m_ref[...] + r_ref[0].astype(f32)) + r_ref[1].astype(f32)) + r_ref[2].astype(f32)

    grid_spec = pltpu.PrefetchScalarGridSpec(
        num_scalar_prefetch=1, grid=(nt,),
        in_specs=[pl.BlockSpec((T, C), lambda i, s_ref: (i, 0)), pl.BlockSpec((3, T, C), lambda i, s_ref: (0, i, 0))],
        out_specs=pl.BlockSpec((None, T, C), out_idx))
    scalars = jnp.stack([cpos, jnp.zeros_like(cpos) if slot is None else slot]).astype(jnp.int32)
    return pl.pallas_call(body, name=name, grid_spec=grid_spec,
                          out_shape=SDS((1 if slot is None else 4,) + tuple(full), f32),
                          compiler_params=_cparams(1))(scalars, mine, rec)


CHIP_FLIPS = ((1, 0), (0, 1), (1, 1))


def _pos():
    return lax.axis_index("x"), lax.axis_index("y"), lax.axis_index("c")


def _comm_call(body, srcs, out_sds, n_remote, n_local, name):
    any_spec = pl.BlockSpec(memory_space=pl.ANY)
    return pl.pallas_call(
        body, name=name, in_specs=[any_spec] * len(srcs), out_specs=[any_spec] * len(out_sds), out_shape=out_sds,
        scratch_shapes=[pltpu.SemaphoreType.DMA((n_remote,)), pltpu.SemaphoreType.DMA((n_remote,)),
                        pltpu.SemaphoreType.DMA((max(n_local, 1),))],
        compiler_params=pltpu.CompilerParams(has_side_effects=True),
    )(*srcs)


def _remote(src, dst, send_sems, recv_sems, k, target):
    return pltpu.make_async_remote_copy(src, dst, send_sems.at[k], recv_sems.at[k], device_id=target,
                                        device_id_type=MESH)


def _by_rows(shape):
    return shape[0] % 16 == 0


def _half_shape(shape):
    return (shape[0] // 2, shape[1]) if _by_rows(shape) else (shape[0], shape[1] // 2)


def _half_of(ref, lead, c, shape):
    if _by_rows(shape):
        half = shape[0] // 2
        return ref.at[(*lead, pl.ds(pl.multiple_of(c * half, 8), half))]
    half = shape[1] // 2
    return ref.at[(*lead, slice(None), pl.ds(pl.multiple_of(c * half, LANES), half))]


def gather_shards(shards):
    nt = len(shards)

    def body(*refs):
        srcs, outs = refs[:nt], refs[nt:2 * nt]
        send_sems, recv_sems, _ = refs[2 * nt:]
        x, y, c = _pos()
        j = 2 * x + y
        sib = (x, y, 1 - c)
        own = [_remote(srcs[t], outs[t].at[j], send_sems, recv_sems, 7 * t + 6, sib) for t in range(nt)]
        first, passed = [], []
        for k, (fx, fy) in enumerate(CHIP_FLIPS):
            tx, ty = x ^ fx, y ^ fy
            jk = 2 * tx + ty
            for t in range(nt):
                sh = srcs[t].shape
                first.append(_remote(_half_of(srcs[t], (), c, sh), _half_of(outs[t], (j,), c, sh), send_sems, recv_sems,
                                     7 * t + k, (tx, ty, c)))
                landed = _half_of(outs[t], (jk,), c, sh)
                passed.append(_remote(landed, landed, send_sems, recv_sems, 7 * t + 3 + k, sib))
        for cp in first + own:
            cp.start()
        for a, b in zip(first, passed):
            a.wait_recv()
            b.start()
        for cp in passed + own:
            cp.wait_recv()
        for cp in first + passed + own:
            cp.wait_send()

    return _comm_call(body, shards, [SDS((4,) + s.shape, s.dtype) for s in shards], 7 * nt, 0, "gather_shards")


def swap_halves(ps):
    nt = len(ps)

    def body(*refs):
        srcs, outs = refs[:nt], refs[nt:2 * nt]
        send_sems, recv_sems, _ = refs[2 * nt:]
        x, y, c = _pos()
        cps = [_remote(_half_of(srcs[t], (a,), 1 - c, srcs[t].shape[1:]), outs[t].at[a], send_sems, recv_sems, 4 * t + a,
                       (x, y, 1 - c)) for t in range(nt) for a in range(4)]
        for cp in cps:
            cp.start()
        for cp in cps:
            cp.wait()

    return _comm_call(body, ps, [SDS((4,) + _half_shape(p.shape[1:]), p.dtype) for p in ps], 4 * nt, 0, "swap_halves")


def scatter_to_chips(qs):
    nt = len(qs)

    def body(*refs):
        srcs, outs = refs[:nt], refs[nt:2 * nt]
        send_sems, recv_sems, _ = refs[2 * nt:]
        x, y, c = _pos()
        cps = []
        for k, (fx, fy) in enumerate(CHIP_FLIPS):
            tx, ty = x ^ fx, y ^ fy
            for t in range(nt):
                cps.append(_remote(srcs[t].at[2 * tx + ty], outs[t].at[k], send_sems, recv_sems, 3 * t + k, (tx, ty, c)))
        for cp in cps:
            cp.start()
        for cp in cps:
            cp.wait()

    return _comm_call(body, qs, [SDS((3,) + q.shape[1:], q.dtype) for q in qs], 3 * nt, 0, "scatter_to_chips")


DEV_FLIPS = tuple((fx, fy, fc) for fx in (0, 1) for fy in (0, 1) for fc in (0, 1))[1:]


def join_halves(fs, small):
    nt = len(fs)

    def body(*refs):
        outs = refs[nt + 1:2 * nt + 1]
        sm = refs[2 * nt + 1]
        send_sems, recv_sems, _ = refs[2 * nt + 2:]
        x, y, c = _pos()
        cps = []
        for t in range(nt):
            mine = _half_of(outs[t], (0,), c, outs[t].shape[1:])
            cps.append(_remote(mine, mine, send_sems, recv_sems, t, (x, y, 1 - c)))
        mine = _half_of(sm, (2 * x + y,), c, sm.shape[1:])
        for k, (fx, fy, fc) in enumerate(DEV_FLIPS):
            cps.append(_remote(mine, mine, send_sems, recv_sems, nt + k, (x ^ fx, y ^ fy, c ^ fc)))
        for cp in cps:
            cp.start()
        for cp in cps:
            cp.wait()

    any_spec = pl.BlockSpec(memory_space=pl.ANY)
    n_sem = nt + len(DEV_FLIPS)
    res = pl.pallas_call(
        body, name="join_halves", in_specs=[any_spec] * (nt + 1), out_specs=[any_spec] * (nt + 1),
        out_shape=[SDS(f.shape, f.dtype) for f in fs] + [SDS(small.shape, small.dtype)],
        input_output_aliases={t: t for t in range(nt + 1)},
        scratch_shapes=[pltpu.SemaphoreType.DMA((n_sem,)), pltpu.SemaphoreType.DMA((n_sem,)), pltpu.SemaphoreType.DMA((1,))],
        compiler_params=pltpu.CompilerParams(has_side_effects=True),
    )(*fs, small)
    return res[:nt], res[nt]


def gather_devices(block, *, name):
    def body(src, out, send_sems, recv_sems, loc_sems):
        x, y, c = _pos()
        me = 4 * x + 2 * y + c
        mine = pltpu.make_async_copy(src, out.at[me], loc_sems.at[0])
        mine.start()
        cps = [_remote(src, out.at[me], send_sems, recv_sems, k, (x ^ fx, y ^ fy, c ^ fc))
               for k, (fx, fy, fc) in enumerate(DEV_FLIPS)]
        for cp in cps:
            cp.start()
        for cp in cps:
            cp.wait()
        mine.wait()

    return _comm_call(body, [block], [SDS((N_DEV,) + block.shape, block.dtype)], 7, 1, name)[0]


def exchange_devices(blocks, *, name):
    def body(src, out, send_sems, recv_sems, loc_sems):
        x, y, c = _pos()
        me = 4 * x + 2 * y + c
        mine = pltpu.make_async_copy(src.at[me], out.at[me], loc_sems.at[0])
        mine.start()
        cps = []
        for k, (fx, fy, fc) in enumerate(DEV_FLIPS):
            tx, ty, tc = x ^ fx, y ^ fy, c ^ fc
            cps.append(_remote(src.at[4 * tx + 2 * ty + tc], out.at[me], send_sems, recv_sems, k, (tx, ty, tc)))
        for cp in cps:
            cp.start()
        for cp in cps:
            cp.wait()
        mine.wait()

    return _comm_call(body, [blocks], [SDS(blocks.shape, blocks.dtype)], 7, 1, name)[0]


SMALL_SHAPES = ((1, 2, 32, 64), (1, 2, 32, 64), (1, 2, 32), (1, 2, 32, 16, 64),
                (1, 2, 32, 16, 64), (1, 2, 32, 16, 64), (1, 2, 32, 16, 64), (1, D_S5), (1, D_S5), (1, 2, 4), (1, 2, 4),
                (1, GDN_HEAD), (1, D_MODEL), (1, D_MODEL), (LANES,))
SMALL_SWAPPED = (3, 4)


def _size(shape):
    return functools.reduce(lambda p, q: p * q, shape)


SMALL_ROWS = tuple(-(-_size(s) // (8 * LANES)) * 8 for s in SMALL_SHAPES)
SMALL_TOTAL = 2240
SMALL_QUARTER = SMALL_TOTAL // 4


def _rows(a):
    flat = a.reshape(-1)
    pad = (-flat.shape[0]) % (8 * LANES)
    if pad:
        flat = jnp.concatenate([flat, jnp.zeros((pad,), flat.dtype)])
    return flat.reshape(-1, LANES)


def _pack_small(parts):
    rows = [_rows(p) for p in parts]
    rows.append(jnp.zeros((SMALL_TOTAL - sum(SMALL_ROWS), LANES), f32))
    return jnp.concatenate(rows, axis=0)


def _unpack_small(buf):
    out, r = [], 0
    for s, n in zip(SMALL_SHAPES, SMALL_ROWS):
        out.append(buf[r:r + n].reshape(-1)[:_size(s)].reshape(s))
        r += n
    return out


def _as_2d(a):
    return a.reshape(1, -1) if a.ndim == 1 else a.reshape(-1, a.shape[-1])


S5_BG = S5_GROUPS // S5_BLOCKS


def _block_diag_in(bb):
    lead = bb.shape[:-2]
    eye = jnp.eye(S5_BG, dtype=bb.dtype)
    b4 = bb.reshape(lead + (S5_BLOCKS, S5_BG, S5_GROUP, S5_STATE))
    return jnp.einsum('...jgcp,gh->...jgchp', b4, eye).reshape(lead + (S5_BLOCKS, S5_BC, S5_BS))


def _block_diag_in_t(d):
    lead = d.shape[:-3]
    d6 = d.reshape(lead + (S5_BLOCKS, S5_BG, S5_GROUP, S5_BG, S5_STATE))
    return jnp.einsum('...jgcgp->...jgcp', d6).reshape(lead + (S5_GROUPS, S5_GROUP * S5_STATE))


def _block_diag_out(cm):
    lead = cm.shape[:-3]
    eye = jnp.eye(S5_BG, dtype=cm.dtype)
    c4 = cm.reshape(lead + (S5_BLOCKS, S5_BG, S5_GROUP, S5_STATE))
    return jnp.einsum('...jgcp,gh->...jhpgc', c4, eye).reshape(lead + (S5_BLOCKS, S5_BS, S5_BC))


def _block_diag_out_t(d):
    lead = d.shape[:-3]
    d6 = d.reshape(lead + (S5_BLOCKS, S5_BG, S5_STATE, S5_BG, S5_GROUP))
    return jnp.einsum('...jgpgc->...jgcp', d6).reshape(lead + (S5_GROUPS, S5_GROUP, S5_STATE))


def _to_chunk_rows(a):
    B, L, W = a.shape
    return a.reshape(B, L // CHUNK, CHUNK, W).transpose(0, 1, 3, 2)


def _from_chunk_rows(a):
    B, nc, W, _ = a.shape
    return a.transpose(0, 1, 3, 2).reshape(B, nc * CHUNK, W)


def local_step(x, ctx, tgt, m, w_in, lam_re, lam_im, log_dt, b_re, b_im, c_re, c_im, s5_d,
               w_glu, b_glu, conv16, a_log, dt_bias, norm_w, w_out, ln_g, ln_b):
    B, L, _ = x.shape
    zeros_state = jnp.zeros((B, GDN_HEADS, GDN_HEAD, GDN_HEAD), f32)

    shift, scale, gate = m[:B, :D_MODEL], m[:B, D_MODEL:2 * D_MODEL], m[:B, 2 * D_MODEL:]
    mod = jnp.stack([scale, shift], axis=1)
    mod_c = jnp.broadcast_to(jnp.stack([m[B, D_MODEL:2 * D_MODEL], m[B, :D_MODEL]], axis=0)[None], (B, 2, D_MODEL))

    u, z_s5, qkv, z_gdn, ba = in_proj_fwd(x, mod, w_in, name="in_proj_fwd")
    uc, _, qkvc, _, bac = in_proj_fwd(ctx, mod_c, w_in, name="in_proj_fwd_ctx")

    ng = N_DIR * S5_GROUPS
    zoh_in = (lam_re.reshape(ng, S5_STATE), lam_im.reshape(ng, S5_STATE), log_dt.reshape(ng, 1),
              b_re.reshape(ng, S5_GROUP * S5_STATE), b_im.reshape(ng, S5_GROUP * S5_STATE))
    expand = (jnp.arange(S5_GROUP * S5_STATE)[None, :] % S5_STATE == jnp.arange(S5_STATE)[:, None]).astype(f32)
    ar, ai, bbr, bbi = s5_zoh_fwd(*zoh_in, expand)
    b_blocks = _block_diag_in(jnp.stack([bbr, bbi]).astype(bf16).reshape(2, N_DIR, S5_GROUPS, S5_GROUP * S5_STATE))
    c_blocks = _block_diag_out(jnp.stack([c_re, -c_im]).astype(bf16).reshape(2, N_DIR, S5_GROUPS, S5_GROUP, S5_STATE))
    a_rows = jnp.stack([ar, ai]).reshape(2, N_DIR, S5_HALF)
    s5w, ys, hins, hins_c, hss, hss_c = [], [], [], [], [], []
    for d in range(N_DIR):
        wd = (b_blocks[0, d], b_blocks[1, d], c_blocks[0, d], c_blocks[1, d], a_rows[:, d])
        s5w.append(wd)
        hs_c, hin_c, hend_c = s5_scan_fwd(uc, *wd, jnp.zeros((B, 2, S5_HALF), f32), d=d, need_y=False,
                                          name=f"s5_fwd_ctx{d}")
        y_d, hs_d, hin, _ = s5_scan_fwd(u, *wd, hend_c, d=d, need_y=True, name=f"s5_fwd{d}")
        hss.append(hs_d)
        hss_c.append(hs_c)
        ys.append(y_d)
        hins.append(hin)
        hins_c.append(hin_c)
    glu_w = (s5_d.reshape(1, D_S5), w_glu, b_glu.reshape(1, D_S5))

    act, pre = conv_fwd(qkv, conv16, is_ctx=False, name="conv_fwd")
    act_c, pre_c = conv_fwd(qkvc, conv16, is_ctx=True, name="conv_fwd_ctx")
    pad8 = jnp.zeros((1, 8), f32)
    alog16 = jnp.concatenate([pad8, a_log.reshape(1, 8)], axis=1)
    dtb16 = jnp.concatenate([pad8, dt_bias.reshape(1, 8)], axis=1)
    bg = gates_fwd(ba, alog16, dtb16, name="gates_fwd")
    bg_c = gates_fwd(bac, alog16, dtb16, name="gates_fwd_ctx")
    bgr, bgr_c = _to_chunk_rows(bg), _to_chunk_rows(bg_c)
    cks_c, ns_c, s_c = gdn_fwd(act_c, bg_c, bgr_c, (zeros_state, zeros_state), need_o=False, name="gdn_fwd_ctx")
    os_, cks, ns, _ = gdn_fwd(act, bg, bgr, s_c, need_o=True, name="gdn_fwd")
    nw = norm_w.reshape(1, GDN_HEAD)

    (loss8, du_skip, dy, dz_s5, do, dz_gdn, gx_res, dws, dwg, dgate, dlng, dlnb, d_s5_d, d_w_glu, d_b_glu,
     d_norm_w) = tail_fwd_bwd(u, ys[0], ys[1], z_s5, os_[0], os_[1], z_gdn, x, tgt, gate[:, None, :],
                              ln_g.reshape(1, D_MODEL), ln_b.reshape(1, D_MODEL), w_out[:D_S5], w_out[D_S5:], *glu_w, nw)
    loss = jnp.sum(loss8[:, 0, 0])
    d_w_out = jnp.concatenate([dws, dwg], axis=0)

    dacts, dbgs, dbgrs, ds0s = gdn_bwd(act, bg, bgr, cks, ns, do, (zeros_state, zeros_state), name="gdn_bwd")
    dacts_c, dbgs_c, dbgrs_c, _ = gdn_bwd(act_c, bg_c, bgr_c, cks_c, ns_c, None, ds0s, name="gdn_bwd_ctx")
    dbg = dbgs[0] + dbgs[1] + _from_chunk_rows(dbgrs[0] + dbgrs[1])
    dbg_c = dbgs_c[0] + dbgs_c[1] + _from_chunk_rows(dbgrs_c[0] + dbgrs_c[1])
    dba, dal, ddt = gates_bwd(ba, alog16, dtb16, dbg, name="gates_bwd")
    dbac, dal_c, ddt_c = gates_bwd(bac, alog16, dtb16, dbg_c, name="gates_bwd_ctx")
    d_a_log = (dal + dal_c)[:, 8:].reshape(1, N_DIR, GDN_HEADS)
    d_dt_bias = (ddt + ddt_c)[:, 8:].reshape(1, N_DIR, GDN_HEADS)
    dqkv, dcw = conv_bwd(qkv, pre, conv16, dacts[0], dacts[1], is_ctx=False, name="conv_bwd")
    dqkvc, dcw_c = conv_bwd(qkvc, pre_c, conv16, dacts_c[0], dacts_c[1], is_ctx=True, name="conv_bwd_ctx")
    d_conv16 = jnp.sum(dcw, axis=0) + jnp.sum(dcw_c, axis=0)

    dus, ducs = [du_skip], []
    das, dbs, dcs = [], [], []
    for d in range(N_DIR):
        du_d, dbre1, dbim1, dct1, dcb1, da1, dh0 = s5_scan_bwd(u, dy, hss[d], *s5w[d], hins[d],
                                                                jnp.zeros((B, 2, S5_HALF), f32), d=d, name=f"s5_bwd{d}")
        duc_d, dbre2, dbim2, _, _, da2, _ = s5_scan_bwd(uc, None, hss_c[d], *s5w[d], hins_c[d], dh0, d=d,
                                                        name=f"s5_bwd_ctx{d}")
        dus.append(du_d)
        ducs.append(duc_d)
        das.append(da1 + da2)
        dbs.append(jnp.stack([dbre1 + dbre2, dbim1 + dbim2]))
        dcs.append(jnp.stack([dct1, dcb1]))
    ng_shape = (N_DIR * S5_GROUPS, -1)
    da = jnp.stack(das, axis=1)
    db = _block_diag_in_t(jnp.stack(dbs, axis=1))
    dc = _block_diag_out_t(jnp.stack(dcs, axis=1))
    dlr, dli, dldt, dbre, dbim = s5_zoh_bwd(*zoh_in, expand, da[0].reshape(ng_shape), da[1].reshape(ng_shape),
                                            db[0].reshape(ng_shape), db[1].reshape(ng_shape))
    d_s5 = (dlr, dli, dldt, dbre, dbim, dc[0], -dc[1])

    zc = jnp.zeros_like(uc)
    dw_c, dmod_c = in_proj_bwd(ctx, mod_c, (tuple(ducs), zc, dqkvc, zc, dbac), w_in, None, None,
                               name="in_proj_bwd_ctx")
    d_w_in, dmod, grad_x = in_proj_bwd(x, mod, (tuple(dus), dz_s5, dqkv, dz_gdn, dba), w_in, gx_res, dw_c,
                                       name="in_proj_bwd")
    dmod_c = jnp.sum(dmod_c, axis=0)

    dm_rows = jnp.concatenate([dmod[:, 1], dmod[:, 0], dgate[:, 0]], axis=1)
    dm_ctx = jnp.concatenate([dmod_c[1], dmod_c[0], jnp.zeros((D_MODEL,), f32)])[None]
    dm = jnp.concatenate([dm_rows, dm_ctx], axis=0)
    small = (*d_s5, d_s5_d, d_b_glu, d_a_log, d_dt_bias, d_norm_w, dlng, dlnb)
    small = tuple(g.reshape(s) for g, s in zip(small, SMALL_SHAPES))
    return loss, grad_x, (d_w_in, d_w_out, d_w_glu, d_conv16), small, dm


SHARDED = (1, 3, 18, 12, 14)
UNSHARDED = tuple(i for i in range(21) if i not in SHARDED)
SMALL = tuple(i for i in UNSHARDED if i not in (0, 2))
W_IN_SHARD = 772


def _conv_rows(w):
    return jnp.concatenate([w.reshape(9, w.shape[-1]), jnp.zeros((CONV_ROWS - 9, w.shape[-1]), f32)], axis=0)


def kernel(x, c, ctx, c_ctx, w_ada, b_ada, w_in, s5_lambda_re, s5_lambda_im, s5_log_dt, s5_b_re, s5_b_im, s5_c_re, s5_c_im, s5_d, w_glu, b_glu, conv_w, gdn_a_log, gdn_dt_bias, gdn_norm_w, w_out, ln_g, ln_b, loss_target, m_c_ctx, m_w_ada, m_b_ada, m_w_in, m_s5_lambda_re, m_s5_lambda_im, m_s5_log_dt, m_s5_b_re, m_s5_b_im, m_s5_c_re, m_s5_c_im, m_s5_d, m_w_glu, m_b_glu, m_conv_w, m_gdn_a_log, m_gdn_dt_bias, m_gdn_norm_w, m_w_out, m_ln_g, m_ln_b, v_c_ctx, v_w_ada, v_b_ada, v_w_in, v_s5_lambda_re, v_s5_lambda_im, v_s5_log_dt, v_s5_b_re, v_s5_b_im, v_s5_c_re, v_s5_c_im, v_s5_d, v_w_glu, v_b_glu, v_conv_w, v_gdn_a_log, v_gdn_dt_bias, v_gdn_norm_w, v_w_out, v_ln_g, v_ln_b):
    weights = [c_ctx, w_ada, b_ada, w_in, s5_lambda_re, s5_lambda_im, s5_log_dt, s5_b_re, s5_b_im, s5_c_re, s5_c_im,
               s5_d, w_glu, b_glu, conv_w, gdn_a_log, gdn_dt_bias, gdn_norm_w, w_out, ln_g, ln_b]
    ms = [m_c_ctx, m_w_ada, m_b_ada, m_w_in, m_s5_lambda_re, m_s5_lambda_im, m_s5_log_dt, m_s5_b_re, m_s5_b_im,
          m_s5_c_re, m_s5_c_im, m_s5_d, m_w_glu, m_b_glu, m_conv_w, m_gdn_a_log, m_gdn_dt_bias, m_gdn_norm_w, m_w_out,
          m_ln_g, m_ln_b]
    vs = [v_c_ctx, v_w_ada, v_b_ada, v_w_in, v_s5_lambda_re, v_s5_lambda_im, v_s5_log_dt, v_s5_b_re, v_s5_b_im,
          v_s5_c_re, v_s5_c_im, v_s5_d, v_w_glu, v_b_glu, v_conv_w, v_gdn_a_log, v_gdn_dt_bias, v_gdn_norm_w, v_w_out,
          v_ln_g, v_ln_b]
    cpos = lax.axis_index("c")
    jchip = 2 * lax.axis_index("x") + lax.axis_index("y")

    c_all = gather_devices(c, name="gather_c")
    cc = jnp.concatenate([c_all, jnp.broadcast_to(c_ctx[None, None, :], (N_DEV, 1, D_MODEL)),
                          jnp.zeros((N_DEV, 5, D_MODEL), f32)], axis=1)
    w_ada16 = w_ada[0].astype(bf16)
    b_cols = lax.dynamic_slice_in_dim(b_ada, jchip * ADA_SHARD, ADA_SHARD, axis=1)
    m_mine = exchange_devices(ada_fwd(cc, w_ada16, b_cols), name="exchange_m")
    m_rows = jnp.concatenate([m_mine[2 * j, :3] for j in range(4)], axis=1)

    conv_shard = _conv_rows(conv_w)
    g_in, g_out, g_glu, g_conv = gather_shards(
        [jnp.transpose(w_in[0]).astype(bf16), w_out[0].astype(bf16), w_glu[0].astype(bf16), conv_shard])
    w_in_t = g_in.reshape(P_IN, D_MODEL)
    conv16 = g_conv.transpose(1, 0, 2).reshape(CONV_ROWS, 3 * D_GDN)

    swap = lambda a: jnp.swapaxes(a, 3, 4)
    loss, grad_x, big, small, dm_rows = local_step(
        x, ctx, loss_target, m_rows, w_in_t, s5_lambda_re, s5_lambda_im, s5_log_dt, swap(s5_b_re), swap(s5_b_im),
        s5_c_re, s5_c_im, s5_d, g_glu.reshape(D_S5, D_S5), b_glu, conv16, gdn_a_log, gdn_dt_bias, gdn_norm_w,
        g_out.reshape(D_MODEL, D_MODEL), ln_g, ln_b)
    me = 2 * jchip + cpos
    loss_hi = loss.astype(bf16).astype(f32)
    loss_row = jnp.zeros((LANES,), f32).at[me].set(loss_hi).at[N_DEV + me].set(loss - loss_hi)

    dm8 = jnp.concatenate([dm_rows, jnp.zeros((5, 3 * D_MODEL), f32)], axis=0)
    dm_by_chip = dm8.reshape(8, 4, ADA_SHARD).transpose(1, 0, 2)
    dm_cols = exchange_devices(jnp.repeat(dm_by_chip, 2, axis=0), name="exchange_dm")
    g_w_ada, pb = ada_bwd(cc, w_ada16, dm_cols)
    pb_all = gather_devices(pb, name="gather_p")
    g_c_ctx = c_ctx_bwd(pb_all, c_ctx[None, :])[0]
    g_b_ada = jnp.concatenate([pb_all[2 * j, 1:2, :ADA_SHARD] for j in range(4)], axis=1)

    d_w_in, d_w_out, d_w_glu, d_conv16 = big
    slabs = [d_w_in.reshape(4, W_IN_SHARD, D_MODEL),
             d_w_out.reshape(4, D_MODEL // 4, D_MODEL),
             d_w_glu.reshape(4, D_S5 // 4, D_S5),
             d_conv16.reshape(CONV_ROWS, 4, 3 * D_GDN // 4).transpose(1, 0, 2),
             _pack_small(small + (loss_row,)).reshape(4, SMALL_QUARTER, LANES)]
    got = swap_halves(slabs)
    q32, q16 = [], []
    for t, (s, g) in enumerate(zip(slabs, got)):
        if _by_rows(s.shape[1:]):
            own = lax.dynamic_index_in_dim(s.reshape(4, 2, s.shape[1] // 2, s.shape[2]), cpos, axis=1, keepdims=False)
        else:
            own = lax.dynamic_slice_in_dim(s, cpos * (s.shape[2] // 2), s.shape[2] // 2, axis=2)
        a, b = sum_cores(own, g, name=f"sum_cores{t}")
        q32.append(a)
        q16.append(b)
    rec = scatter_to_chips(q16)
    fs = [sum_chips(lax.dynamic_index_in_dim(q, jchip, axis=0, keepdims=False), r, cpos, s.shape[1:],
                    slot=jchip if t == 4 else None, name=f"sum_chips{t}")
          for t, (q, r, s) in enumerate(zip(q32, rec, slabs))]
    red, small_all = join_halves(fs[:4], fs[4])
    g_small = _unpack_small(small_all.reshape(SMALL_TOTAL, LANES))
    loss = jnp.sum(g_small[-1][:2 * N_DEV])
    g_small = g_small[:-1]
    g_shard = {1: g_w_ada, 3: red[0], 18: red[1], 12: red[2], 14: red[3]}

    grads, deltas, new_m, new_v = [None] * 21, [None] * 21, [None] * 21, [None] * 21
    for t, i in enumerate(SHARDED):
        conv, win = i == 14, i == 3
        prep = (lambda a: _conv_rows(a)[None]) if conv else ((lambda a: jnp.transpose(a, (2, 0, 1))) if win else (lambda a: a))
        g = jnp.transpose(g_shard[i], (1, 0, 2)) if win else g_shard[i]
        d, nm, nv = adamw_3d(prep(weights[i]), g, prep(ms[i]), prep(vs[i]), lead=win, name=f"adamw{t}")
        for lst, val in ((grads, g), (deltas, d), (new_m, nm), (new_v, nv)):
            lst[i] = (val[0, :9].reshape(weights[i].shape) if conv else (jnp.transpose(val, (1, 2, 0)) if win else val))
    g_un = {0: g_c_ctx, 2: g_b_ada, **{i: g_small[n] for n, i in enumerate(SMALL)}}
    swapped = [SMALL[n] for n in SMALL_SWAPPED]
    small_in = lambda lst: [_as_2d(swap(lst[i]) if i in swapped else lst[i]) for i in UNSHARDED]
    sm = adamw_small(small_in(weights), [_as_2d(g_un[i]) for i in UNSHARDED], small_in(ms), small_in(vs))
    for n, i in enumerate(UNSHARDED):
        back = ((lambda a: swap(a.reshape(swap(weights[i]).shape))) if i in swapped
                else (lambda a: a.reshape(weights[i].shape)))
        grads[i] = back(g_un[i])
        for lst, res in ((deltas, sm[0]), (new_m, sm[1]), (new_v, sm[2])):
            lst[i] = back(res[n])
    return (loss, grad_x, *grads, *deltas, *new_m, *new_v)
```

```python
import functools

import jax
import jax.numpy as jnp
from jax import lax
from jax.experimental import pallas as pl
from jax.experimental.pallas import tpu as pltpu

f32 = jnp.float32
bf16 = jnp.bfloat16
SDS = jax.ShapeDtypeStruct

D_MODEL = 1024
D_S5 = 512
S5_GROUP = 16
S5_GROUPS = 32
S5_STATE = 64
S5_HALF = S5_GROUPS * S5_STATE
D_GDN = 512
GDN_HEAD = 128
GDN_HEADS = 4
CHUNK = 64
GRID_W = 64
N_DIR = 2
P_IN = 3088
DEEPNORM_ALPHA = 2.0 ** 0.25
LN_EPS = 1e-5
NORM_EPS = 1e-6
ADAM_LR, ADAM_B1, ADAM_B2, ADAM_EPS, ADAM_WD, ADAM_STEP = 0.001, 0.9, 0.999, 1e-08, 0.01, 10

LANES = 128
VMEM_LIMIT = 56 * 1024 * 1024
TOK_TILE = 256
S5_TILE = 256
MESH = pl.DeviceIdType.MESH


def _cparams(n_grid):
    return pltpu.CompilerParams(dimension_semantics=("arbitrary",) * n_grid, vmem_limit_bytes=VMEM_LIMIT)


def _dot(a, b):
    return jnp.dot(a.astype(bf16), b.astype(bf16), preferred_element_type=f32)


def _dot_nt(a, b):
    return lax.dot_general(a.astype(bf16), b.astype(bf16), (((1,), (1,)), ((), ())), preferred_element_type=f32)


def _dot_tn(a, b):
    return lax.dot_general(a.astype(bf16), b.astype(bf16), (((0,), (0,)), ((), ())), preferred_element_type=f32)


def _dot_hi(a, b):
    return jnp.dot(a, b, precision=lax.Precision.HIGHEST, preferred_element_type=f32)


@jax.custom_vjp
def _mm(a, b):
    return _dot(a, b)


@jax.custom_vjp
def _mm_nt(a, b):
    return _dot_nt(a, b)


@jax.custom_vjp
def _mm_tn(a, b):
    return _dot_tn(a, b)


_mm.defvjp(lambda a, b: (_dot(a, b), (a, b)), lambda r, g: (_mm_nt(g, r[1]), _mm_tn(r[0], g)))
_mm_nt.defvjp(lambda a, b: (_dot_nt(a, b), (a, b)), lambda r, g: (_mm(g, r[1]), _mm_tn(g, r[0])))
_mm_tn.defvjp(lambda a, b: (_dot_tn(a, b), (a, b)), lambda r, g: (_mm_nt(r[1], g), _mm(r[0], g)))


def _silu(x):
    return x * jax.nn.sigmoid(x)


def _gelu(x):
    return 0.5 * x * (1.0 + lax.erf(x * (2.0 ** -0.5)))


def _resident(shape):
    nd = len(shape)
    return pl.BlockSpec(shape, lambda *_: (0,) * nd, pipeline_mode=pl.Buffered(1))


def _tok(tile, width, nt=None, rev=False):
    if rev:
        return pl.BlockSpec((None, tile, width), lambda b, n: (b, nt - 1 - n, 0))
    return pl.BlockSpec((None, tile, width), lambda b, n: (b, n, 0))


def _per_batch(rows, width):
    return pl.BlockSpec((None, rows, width), lambda b, n: (b, 0, 0))


def _first_step():
    return jnp.logical_and(pl.program_id(0) == 0, pl.program_id(1) == 0)


ADA_SHARD = 3 * D_MODEL // 4
N_DEV = 8


def ada_fwd(cc, w, b):
    def body(cc_ref, w_ref, b_ref, m_ref):
        for k in range(N_DEV):
            m_ref[k] = _dot(_silu(cc_ref[k]), w_ref[...]) + b_ref[...]

    return pl.pallas_call(body, name="ada_fwd", out_shape=SDS((N_DEV, 8, ADA_SHARD), f32),
                          compiler_params=pltpu.CompilerParams(vmem_limit_bytes=VMEM_LIMIT))(cc, w, b)


def ada_bwd(cc, w, dmj):
    def body(cc_ref, w_ref, dmj_ref, dw_ref, pb_ref):
        dw = jnp.zeros((D_MODEL, ADA_SHARD), f32)
        p = jnp.zeros((8, D_MODEL), f32)
        db = jnp.zeros((1, ADA_SHARD), f32)
        for k in range(N_DEV):
            dw = dw + _dot_tn(_silu(cc_ref[k]), dmj_ref[k])
            p = p + _dot_nt(dmj_ref[k], w_ref[...])
            db = db + jnp.sum(dmj_ref[k], axis=0, keepdims=True)
        dw_ref[0] = dw
        pb_ref[...] = jnp.zeros_like(pb_ref)
        pb_ref[0:1, :] = p[2:3, :]
        pb_ref[1:2, 0:ADA_SHARD] = db

    return pl.pallas_call(
        body, name="ada_bwd", out_shape=[SDS((1, D_MODEL, ADA_SHARD), f32), SDS((8, D_MODEL), f32)],
        compiler_params=pltpu.CompilerParams(vmem_limit_bytes=VMEM_LIMIT))(cc, w, dmj)


def c_ctx_bwd(pb_all, c_ctx):
    def body(p_ref, c_ref, d_ref):
        ds = ((p_ref[0, 0:1, :] + p_ref[2, 0:1, :]) + p_ref[4, 0:1, :]) + p_ref[6, 0:1, :]
        _, vjp = jax.vjp(_silu, c_ref[...])
        d_ref[...] = vjp(ds)[0]

    return pl.pallas_call(body, name="c_ctx_bwd", out_shape=SDS((1, D_MODEL), f32))(pb_all, c_ctx)


N_GATE = 2 * N_DIR * GDN_HEADS
IN_WIDTHS = (D_S5, D_S5, 3 * D_GDN, D_GDN, N_GATE)
IN_OFFS = (0, 512, 1024, 2560, 3072)


def in_proj_fwd(x, mod, wt, *, name):
    B, L, _ = x.shape
    T = min(2 * TOK_TILE, L)

    def body(x_ref, mod_ref, w_ref, *o_refs):
        h = (x_ref[...] * (1.0 + mod_ref[0:1, :]) + mod_ref[1:2, :]).astype(bf16)
        for o_ref, off, wd in zip(o_refs, IN_OFFS, IN_WIDTHS):
            o_ref[...] = _dot_nt(h, w_ref[off:off + wd, :])

    return pl.pallas_call(
        body, name=name, grid=(B, L // T),
        in_specs=[_tok(T, D_MODEL), _per_batch(2, D_MODEL), _resident((P_IN, D_MODEL))],
        out_specs=[_tok(T, wd) for wd in IN_WIDTHS],
        out_shape=[SDS((B, L, wd), f32) for wd in IN_WIDTHS],
        compiler_params=_cparams(2),
    )(x, mod, wt)


def in_proj_bwd(x, mod, ds, wt, gx_res, dw_start, *, name):
    B, L, _ = x.shape
    T = min(TOK_TILE, L)
    with_dx = gx_res is not None
    with_start = dw_start is not None
    n_u = len(ds[0])

    def body(*refs):
        x_ref, mod_ref = refs[0], refs[1]
        du_refs = refs[2:2 + n_u]
        d_refs = refs[2 + n_u:6 + n_u]
        w_ref = refs[6 + n_u]
        k = 7 + n_u
        if with_dx:
            gx_ref = refs[k]
            k += 1
        if with_start:
            start_ref = refs[k]
            k += 1
        dw_ref, dmod_ref = refs[k], refs[k + 1]
        if with_dx:
            dx_ref = refs[k + 2]
        n = pl.program_id(1)

        @pl.when(_first_step())
        def _():
            dw_ref[...] = start_ref[...] if with_start else jnp.zeros_like(dw_ref)

        @pl.when(n == 0)
        def _():
            dmod_ref[...] = jnp.zeros_like(dmod_ref)

        xv = x_ref[...]
        scale1 = 1.0 + mod_ref[0:1, :]
        h = (xv * scale1 + mod_ref[1:2, :]).astype(bf16)
        du = du_refs[0][...]
        for r in du_refs[1:]:
            du = du + r[...]
        dh = jnp.zeros((T, D_MODEL), f32)
        for dv, off, wd in zip([du] + [r[...] for r in d_refs], IN_OFFS, IN_WIDTHS):
            dv = dv.astype(bf16)
            dh = dh + _dot(dv, w_ref[off:off + wd, :])
            dw_ref[off:off + wd, :] += _dot_tn(dv, h)
        dmod_ref[0:1, :] += jnp.sum(dh * xv, axis=0, keepdims=True)
        dmod_ref[1:2, :] += jnp.sum(dh, axis=0, keepdims=True)
        if with_dx:
            dx_ref[...] = gx_ref[...] + dh * scale1

    in_specs = ([_tok(T, D_MODEL), _per_batch(2, D_MODEL)] + [_tok(T, D_S5)] * n_u + [_tok(T, wd) for wd in IN_WIDTHS[1:]]
                + [_resident((P_IN, D_MODEL))])
    args = [x, mod, *ds[0], *ds[1:], wt]
    out_specs = [_resident((P_IN, D_MODEL)), _per_batch(2, D_MODEL)]
    out_shape = [SDS((P_IN, D_MODEL), f32), SDS((B, 2, D_MODEL), f32)]
    if with_dx:
        in_specs.append(_tok(T, D_MODEL))
        args.append(gx_res)
        out_specs.append(_tok(T, D_MODEL))
        out_shape.append(SDS((B, L, D_MODEL), f32))
    if with_start:
        in_specs.append(_resident((P_IN, D_MODEL)))
        args.append(dw_start)
    return pl.pallas_call(body, name=name, grid=(B, L // T), in_specs=in_specs, out_specs=out_specs,
                          out_shape=out_shape, compiler_params=_cparams(2))(*args)


def _s5_zoh(lr, li, ldt, bre, bim, expand):
    dt = jnp.exp(ldt)
    zr, zi = lr * dt, li * dt
    e = jnp.exp(zr)
    ar, ai = e * jnp.cos(zi), e * jnp.sin(zi)
    den = lr * lr + li * li
    czr = ((ar - 1.0) * lr + ai * li) / den
    czi = (ai * lr - (ar - 1.0) * li) / den
    czr_e, czi_e = _dot_hi(czr, expand), _dot_hi(czi, expand)
    return ar, ai, czr_e * bre - czi_e * bim, czr_e * bim + czi_e * bre


_ZOH_OUT = [(N_DIR * S5_GROUPS, S5_STATE)] * 2 + [(N_DIR * S5_GROUPS, S5_STATE * S5_GROUP)] * 2


def s5_zoh_fwd(lr, li, ldt, bre, bim, expand):
    def body(lr_ref, li_ref, ldt_ref, bre_ref, bim_ref, e_ref, ar_ref, ai_ref, bbr_ref, bbi_ref):
        ar, ai, bbr, bbi = _s5_zoh(lr_ref[...], li_ref[...], ldt_ref[...], bre_ref[...], bim_ref[...], e_ref[...])
        ar_ref[...], ai_ref[...], bbr_ref[...], bbi_ref[...] = ar, ai, bbr, bbi

    return pl.pallas_call(body, name="s5_zoh_fwd", out_shape=[SDS(s, f32) for s in _ZOH_OUT])(
        lr, li, ldt, bre, bim, expand)


def s5_zoh_bwd(lr, li, ldt, bre, bim, expand, dar, dai, dbbr, dbbi):
    def body(lr_ref, li_ref, ldt_ref, bre_ref, bim_ref, e_ref, dar_ref, dai_ref, dbbr_ref, dbbi_ref,
             dlr_ref, dli_ref, dldt_ref, dbre_ref, dbim_ref):
        ev = e_ref[...]
        _, vjp = jax.vjp(lambda a, b, c, d, e: _s5_zoh(a, b, c, d, e, ev),
                         lr_ref[...], li_ref[...], ldt_ref[...], bre_ref[...], bim_ref[...])
        outs = vjp((dar_ref[...], dai_ref[...], dbbr_ref[...], dbbi_ref[...]))
        dlr_ref[...], dli_ref[...], dldt_ref[...], dbre_ref[...], dbim_ref[...] = outs

    shapes = [lr.shape, li.shape, ldt.shape, bre.shape, bim.shape]
    return pl.pallas_call(body, name="s5_zoh_bwd", out_shape=[SDS(s, f32) for s in shapes])(
        lr, li, ldt, bre, bim, expand, dar, dai, dbbr, dbbi)


def _scan_rows(T, rev, ar, ai, h0s, refs, off):
    def step(i, carry):
        t = off + ((T - 1 - i) if rev else i)
        out = []
        for (hr, hi), (r_ref, i_ref) in zip(carry, refs):
            nr = ar * hr - ai * hi + r_ref[pl.ds(t, 1), :]
            ni = ar * hi + ai * hr + i_ref[pl.ds(t, 1), :]
            r_ref[pl.ds(t, 1), :] = nr
            i_ref[pl.ds(t, 1), :] = ni
            out.append((nr, ni))
        return tuple(out)

    return lax.fori_loop(0, T, step, tuple(h0s))


S5_BLOCKS = 4
S5_BC = D_S5 // S5_BLOCKS
S5_BS = S5_HALF // S5_BLOCKS


def _s5_in(uv, bre_ref, bim_ref, hr_ref, hi_ref, off, T):
    for jb in range(S5_BLOCKS):
        uj = uv[:, jb * S5_BC:(jb + 1) * S5_BC]
        hr_ref[off:off + T, jb * S5_BS:(jb + 1) * S5_BS] = _dot(uj, bre_ref[jb])
        hi_ref[off:off + T, jb * S5_BS:(jb + 1) * S5_BS] = _dot(uj, bim_ref[jb])


def _s5_specs(B, T, nt, rev):
    tidx = (lambda n: nt - 1 - n) if rev else (lambda n: n)
    tok = pl.BlockSpec((B, T, D_S5), lambda n: (0, tidx(n), 0))
    hin = pl.BlockSpec((B, None, 2, S5_HALF), lambda n: (0, tidx(n), 0, 0))
    state = pl.BlockSpec((B, 2, S5_HALF), lambda n: (0, 0, 0))
    return tok, hin, state


def s5_scan_fwd(u, bre, bim, ctop, cbot, arow, h0, *, d, need_y, name):
    B, L, _ = u.shape
    T = min(S5_TILE, L)
    nt = L // T
    rev = d == 1

    def body(u_ref, bre_ref, bim_ref, ct_ref, cb_ref, a_ref, h0_ref, *rest):
        if need_y:
            y_ref, hs_ref, hin_ref, hend_ref, hr_scr, hi_scr, h_scr = rest
        else:
            hs_ref, hin_ref, hend_ref, hr_scr, hi_scr, h_scr = rest
        n = pl.program_id(0)

        @pl.when(n == 0)
        def _():
            h_scr[...] = h0_ref[...]

        hin_ref[...] = h_scr[...]
        for b in range(B):
            _s5_in(u_ref[b].astype(bf16), bre_ref, bim_ref, hr_scr.at[b], hi_scr.at[b], 0, T)
        hs = _scan_rows(T, rev, a_ref[0:1, :], a_ref[1:2, :], [(h_scr[b, 0:1, :], h_scr[b, 1:2, :]) for b in range(B)],
                        [(hr_scr.at[b], hi_scr.at[b]) for b in range(B)], 0)
        for b in range(B):
            h_scr[b, 0:1, :] = hs[b][0]
            h_scr[b, 1:2, :] = hs[b][1]
            hs_ref[b, :, 0:S5_HALF] = hr_scr[b].astype(bf16)
            hs_ref[b, :, S5_HALF:2 * S5_HALF] = hi_scr[b].astype(bf16)
            if need_y:
                for jb in range(S5_BLOCKS):
                    st = slice(jb * S5_BS, (jb + 1) * S5_BS)
                    y_ref[b, :, jb * S5_BC:(jb + 1) * S5_BC] = (_dot(hr_scr[b, :, st], ct_ref[jb])
                                                                 + _dot(hi_scr[b, :, st], cb_ref[jb]))

        @pl.when(n == nt - 1)
        def _():
            hend_ref[...] = h_scr[...]

    tok, hin_spec, state = _s5_specs(B, T, nt, rev)
    hs_spec = pl.BlockSpec((B, T, 2 * S5_HALF), tok.index_map)
    out_specs = [hs_spec, hin_spec, state]
    out_shape = [SDS((B, L, 2 * S5_HALF), bf16), SDS((B, nt, 2, S5_HALF), f32), SDS((B, 2, S5_HALF), f32)]
    if need_y:
        out_specs.insert(0, tok)
        out_shape.insert(0, SDS((B, L, D_S5), f32))
    w_in, w_out = _resident((S5_BLOCKS, S5_BC, S5_BS)), _resident((S5_BLOCKS, S5_BS, S5_BC))
    return pl.pallas_call(
        body, name=name, grid=(nt,),
        in_specs=[tok, w_in, w_in, w_out, w_out, _resident((2, S5_HALF)), state],
        out_specs=out_specs, out_shape=out_shape,
        scratch_shapes=[pltpu.VMEM((B, T, S5_HALF), f32), pltpu.VMEM((B, T, S5_HALF), f32),
                        pltpu.VMEM((B, 2, S5_HALF), f32)],
        compiler_params=_cparams(1),
    )(u, bre, bim, ctop, cbot, arow, h0)


def s5_scan_bwd(u, dy, hs, bre, bim, ctop, cbot, arow, hin, dhend, *, d, name):
    B, L, _ = u.shape
    T = min(S5_TILE, L)
    nt = L // T
    rev = d == 1
    has_dy = dy is not None
    PAD = 8

    def body(*refs):
        u_ref = refs[0]
        k = 1
        if has_dy:
            dy_ref = refs[1]
            k = 2
        hs_ref = refs[k]
        k += 1
        bre_ref, bim_ref, ct_ref, cb_ref, a_ref, hin_ref, dhend_ref = refs[k:k + 7]
        du_ref, dbre_ref, dbim_ref, dct_ref, dcb_ref, da_ref, dh0_ref = refs[k + 7:k + 14]
        hr_scr, hi_scr, gr_scr, gi_scr, p_scr = refs[k + 14:]
        n = pl.program_id(0)

        @pl.when(n == 0)
        def _():
            for r in (dbre_ref, dbim_ref, dct_ref, dcb_ref, da_ref):
                r[...] = jnp.zeros_like(r)
            p_scr[...] = dhend_ref[...]

        ar, ai = a_ref[0:1, :], a_ref[1:2, :]
        prev_row = PAD + T if rev else PAD - 1
        uvs = []
        for b in range(B):
            uvs.append(u_ref[b].astype(bf16))
            hr_scr[b, PAD:PAD + T, :] = hs_ref[b, :, 0:S5_HALF].astype(f32)
            hi_scr[b, PAD:PAD + T, :] = hs_ref[b, :, S5_HALF:2 * S5_HALF].astype(f32)
            hr_scr[b, prev_row:prev_row + 1, :] = hin_ref[b, 0:1, :]
            hi_scr[b, prev_row:prev_row + 1, :] = hin_ref[b, 1:2, :]
        if has_dy:
            for b in range(B):
                dyv = dy_ref[b].astype(bf16)
                for jb in range(S5_BLOCKS):
                    st = slice(jb * S5_BS, (jb + 1) * S5_BS)
                    dyj = dyv[:, jb * S5_BC:(jb + 1) * S5_BC]
                    gr_scr[b, :, st] = _dot_nt(dyj, ct_ref[jb])
                    gi_scr[b, :, st] = _dot_nt(dyj, cb_ref[jb])
                    dct_ref[jb] += _dot_tn(hr_scr[b, PAD:PAD + T, st], dyj)
                    dcb_ref[jb] += _dot_tn(hi_scr[b, PAD:PAD + T, st], dyj)
        else:
            gr_scr[...] = jnp.zeros_like(gr_scr)
            gi_scr[...] = jnp.zeros_like(gi_scr)

        def step(i, carry):
            t = i if rev else T - 1 - i
            tp = PAD + t + (1 if rev else -1)
            out = []
            for b, (pr, pi, dar, dai) in enumerate(carry):
                gr = gr_scr[b, pl.ds(t, 1), :] + pr
                gi = gi_scr[b, pl.ds(t, 1), :] + pi
                gr_scr[b, pl.ds(t, 1), :] = gr
                gi_scr[b, pl.ds(t, 1), :] = gi
                hpr = hr_scr[b, pl.ds(tp, 1), :]
                hpi = hi_scr[b, pl.ds(tp, 1), :]
                out.append((ar * gr + ai * gi, ar * gi - ai * gr, dar + hpr * gr + hpi * gi, dai + hpr * gi - hpi * gr))
            return tuple(out)

        zero = jnp.zeros((1, S5_HALF), f32)
        res = lax.fori_loop(0, T, step, tuple((p_scr[b, 0:1, :], p_scr[b, 1:2, :], zero, zero) for b in range(B)))
        for b in range(B):
            pr, pi, dar, dai = res[b]
            p_scr[b, 0:1, :] = pr
            p_scr[b, 1:2, :] = pi
            da_ref[0:1, :] += dar
            da_ref[1:2, :] += dai
            for jb in range(S5_BLOCKS):
                st = slice(jb * S5_BS, (jb + 1) * S5_BS)
                ch = slice(jb * S5_BC, (jb + 1) * S5_BC)
                gr_j = gr_scr[b, :, st].astype(bf16)
                gi_j = gi_scr[b, :, st].astype(bf16)
                du_ref[b, :, ch] = _dot_nt(gr_j, bre_ref[jb]) + _dot_nt(gi_j, bim_ref[jb])
                dbre_ref[jb] += _dot_tn(uvs[b][:, ch], gr_j)
                dbim_ref[jb] += _dot_tn(uvs[b][:, ch], gi_j)

        @pl.when(n == nt - 1)
        def _():
            dh0_ref[...] = p_scr[...]

    tok, hin_spec, state = _s5_specs(B, T, nt, not rev)
    hs_spec = pl.BlockSpec((B, T, 2 * S5_HALF), tok.index_map)
    w_in, w_out = _resident((S5_BLOCKS, S5_BC, S5_BS)), _resident((S5_BLOCKS, S5_BS, S5_BC))
    wspecs = [w_in, w_in, w_out, w_out]
    in_specs = [tok] + ([tok] if has_dy else []) + [hs_spec] + wspecs + [_resident((2, S5_HALF)), hin_spec, state]
    args = [u] + ([dy] if has_dy else []) + [hs, bre, bim, ctop, cbot, arow, hin, dhend]
    return pl.pallas_call(
        body, name=name, grid=(nt,), in_specs=in_specs,
        out_specs=[tok] + wspecs + [_resident((2, S5_HALF)), state],
        out_shape=[SDS((B, L, D_S5), f32), SDS((S5_BLOCKS, S5_BC, S5_BS), f32), SDS((S5_BLOCKS, S5_BC, S5_BS), f32),
                   SDS((S5_BLOCKS, S5_BS, S5_BC), f32), SDS((S5_BLOCKS, S5_BS, S5_BC), f32), SDS((2, S5_HALF), f32),
                   SDS((B, 2, S5_HALF), f32)],
        scratch_shapes=[pltpu.VMEM((B, T + 2 * PAD, S5_HALF), f32), pltpu.VMEM((B, T + 2 * PAD, S5_HALF), f32),
                        pltpu.VMEM((B, T, S5_HALF), f32), pltpu.VMEM((B, T, S5_HALF), f32),
                        pltpu.VMEM((B, 2, S5_HALF), f32)],
        compiler_params=_cparams(1),
    )(*args)


def _glu_fn(u, y0, y1, z, dsk, wg, bg):
    g = _gelu(dsk * u + y0 + y1)
    return g * jax.nn.sigmoid(_mm(g, wg) + bg) * _silu(z)


CONV_ROWS = 16


def _shift(x, s):
    L = x.shape[0]
    k = (-s) % L
    return x if k == 0 else pltpu.roll(x, k, axis=0)


def _r16(v):
    return v.astype(bf16).astype(f32)


def _conv_masks(L, is_ctx):
    t = lax.broadcasted_iota(jnp.int32, (L, 1), 0)
    if is_ctx:
        return t == L - 1, t == 0, None, None
    col = jnp.bitwise_and(t, GRID_W - 1)
    return col == GRID_W - 1, col == 0, t >= GRID_W, t < L - GRID_W


def _conv_sides(xv, masks):
    no_left, no_right, _, _ = masks
    return _shift(jnp.where(no_left, 0.0, xv), -1), _shift(jnp.where(no_right, 0.0, xv), 1)


def _conv_pre(xv, w_ref, masks, is_ctx):
    xv = _r16(xv)
    wv = _r16(w_ref[...])
    xl, xr = _conv_sides(xv, masks)
    z = [wv[3 * di:3 * di + 1, :] * xl + wv[3 * di + 1:3 * di + 2, :] * xv + wv[3 * di + 2:3 * di + 3, :] * xr
         for di in ((1,) if is_ctx else (0, 1, 2))]
    if is_ctx:
        return z[0]
    _, _, has_up, has_down = masks
    return z[1] + jnp.where(has_up, _shift(z[0], -GRID_W), 0.0) + jnp.where(has_down, _shift(z[2], GRID_W), 0.0)


def _conv_pre_bwd(xv, w_ref, dpre, masks, is_ctx, dw_ref):
    no_left, no_right, has_up, has_down = masks
    xv, dpre, wv = _r16(xv), _r16(dpre), _r16(w_ref[...])
    xl, xr = _conv_sides(xv, masks)
    if is_ctx:
        dz = {1: dpre}
    else:
        dz = {0: _shift(jnp.where(has_up, dpre, 0.0), GRID_W), 1: dpre, 2: _shift(jnp.where(has_down, dpre, 0.0), -GRID_W)}
    dxl = dxc = dxr = None
    for di, d in dz.items():
        for dj, side in enumerate((xl, xv, xr)):
            dw_ref[3 * di + dj:3 * di + dj + 1, :] = jnp.sum(d * side, axis=0, keepdims=True)
        tl, tc, tr = (wv[3 * di + dj:3 * di + dj + 1, :] * d for dj in range(3))
        dxl, dxc, dxr = (tl, tc, tr) if dxl is None else (dxl + tl, dxc + tc, dxr + tr)
    return dxc + jnp.where(no_left, 0.0, _shift(dxl, 1)) + jnp.where(no_right, 0.0, _shift(dxr, -1))


def _qk_post(pre, is_norm, scale):
    s = _silu(pre)
    nrm = lax.rsqrt(jnp.sum(s * s, axis=-1, keepdims=True) + NORM_EPS)
    return s * jnp.where(is_norm, nrm * scale, 1.0)


def _conv_tile(L):
    return D_GDN if L <= 512 else 2 * GDN_HEAD


def _conv_kind(W):
    head = pl.program_id(1) * (W // GDN_HEAD)
    return head < 2 * GDN_HEADS, jnp.where(head < GDN_HEADS, GDN_HEAD ** -0.5, 1.0).astype(f32)


def _conv_specs(L, W):
    spec = pl.BlockSpec((None, L, W), lambda b, ct: (b, 0, ct))
    wspec = pl.BlockSpec((CONV_ROWS, W), lambda b, ct: (0, ct))
    dwspec = pl.BlockSpec((None, CONV_ROWS, W), lambda b, ct: (b, 0, ct))
    return spec, wspec, dwspec


def conv_fwd(qkv, w16, *, is_ctx, name):
    B, L, C = qkv.shape
    W = _conv_tile(L)
    spec, wspec, _ = _conv_specs(L, W)

    def body(x_ref, w_ref, o_ref, pre_ref):
        is_norm, scale = _conv_kind(W)
        pre = _conv_pre(x_ref[...], w_ref, _conv_masks(L, is_ctx), is_ctx)
        pre_ref[...] = pre
        for h in range(W // GDN_HEAD):
            sl = slice(h * GDN_HEAD, (h + 1) * GDN_HEAD)
            o_ref[:, sl] = _qk_post(pre[:, sl], is_norm, scale)

    return pl.pallas_call(body, name=name, grid=(B, C // W), in_specs=[spec, wspec], out_specs=[spec, spec],
                          out_shape=[SDS((B, L, C), f32)] * 2, compiler_params=_cparams(2))(qkv, w16)


def conv_bwd(qkv, pre, w16, da0, da1, *, is_ctx, name):
    B, L, C = qkv.shape
    W = _conv_tile(L)
    spec, wspec, dwspec = _conv_specs(L, W)

    def body(x_ref, pre_ref, w_ref, d0_ref, d1_ref, dx_ref, dw_ref):
        is_norm, scale = _conv_kind(W)
        dpre = []
        for h in range(W // GDN_HEAD):
            sl = slice(h * GDN_HEAD, (h + 1) * GDN_HEAD)
            _, vjp = jax.vjp(lambda p: _qk_post(p, is_norm, scale), pre_ref[:, sl])
            dpre.append(vjp(d0_ref[:, sl] + d1_ref[:, sl])[0])
        dpre = dpre[0] if len(dpre) == 1 else jnp.concatenate(dpre, axis=1)
        dw_ref[...] = jnp.zeros_like(dw_ref)
        dx_ref[...] = _conv_pre_bwd(x_ref[...], w_ref, dpre, _conv_masks(L, is_ctx), is_ctx, dw_ref)

    return pl.pallas_call(body, name=name, grid=(B, C // W), in_specs=[spec, spec, wspec, spec, spec],
                          out_specs=[spec, dwspec], out_shape=[SDS((B, L, C), f32), SDS((B, CONV_ROWS, C), f32)],
                          compiler_params=_cparams(2))(qkv, pre, w16, da0, da1)


def _gates_fn(ba, alog, dtb):
    T = ba.shape[0]
    nck = T // CHUNK
    lane = lax.broadcasted_iota(jnp.int32, ba.shape, 1)
    ii = lax.broadcasted_iota(jnp.int32, (nck, CHUNK, CHUNK), 1)
    jj = lax.broadcasted_iota(jnp.int32, (nck, CHUNK, CHUNK), 2)
    g = jnp.where(lane >= 8, -jnp.exp(alog) * jax.nn.softplus(ba + dtb), 0.0)
    g3 = g.reshape(nck, CHUNK, N_GATE)
    chunk_sum = lambda tri: lax.dot_general(tri.astype(f32), g3, (((2,), (1,)), ((0,), (0,))),
                                            precision=lax.Precision.HIGHEST, preferred_element_type=f32).reshape(T, N_GATE)
    gc = jnp.where(lane >= 12, chunk_sum(ii <= jj), chunk_sum(ii >= jj))
    return jnp.where(lane < 8, jax.nn.sigmoid(ba), gc)


def gates_fwd(ba, alog, dtb, *, name):
    B, L, _ = ba.shape
    T = min(TOK_TILE, L)
    t = _tok(T, N_GATE)

    def body(ba_ref, al_ref, dt_ref, o_ref):
        o_ref[...] = _gates_fn(ba_ref[...], al_ref[...], dt_ref[...])

    return pl.pallas_call(body, name=name, grid=(B, L // T),
                          in_specs=[t, _resident((1, N_GATE)), _resident((1, N_GATE))], out_specs=t,
                          out_shape=SDS((B, L, N_GATE), f32), compiler_params=_cparams(2))(ba, alog, dtb)


def gates_bwd(ba, alog, dtb, dbg, *, name):
    B, L, _ = ba.shape
    T = min(TOK_TILE, L)
    t = _tok(T, N_GATE)
    small = _resident((1, N_GATE))

    def body(ba_ref, al_ref, dt_ref, d_ref, dba_ref, dal_ref, ddt_ref):
        @pl.when(_first_step())
        def _():
            dal_ref[...] = jnp.zeros_like(dal_ref)
            ddt_ref[...] = jnp.zeros_like(ddt_ref)

        _, vjp = jax.vjp(_gates_fn, ba_ref[...], al_ref[...], dt_ref[...])
        dba, dal, ddt = vjp(d_ref[...])
        dba_ref[...] = dba
        dal_ref[...] += dal
        ddt_ref[...] += ddt

    return pl.pallas_call(body, name=name, grid=(B, L // T), in_specs=[t, small, small, t],
                          out_specs=[t, small, small],
                          out_shape=[SDS((B, L, N_GATE), f32), SDS((1, N_GATE), f32), SDS((1, N_GATE), f32)],
                          compiler_params=_cparams(2))(ba, alog, dtb, dbg)


@jax.custom_vjp
def _inv_unit_tri(mats):
    n = mats[0].shape[0]
    eye = (lax.broadcasted_iota(jnp.int32, (n, n), 0) == lax.broadcasted_iota(jnp.int32, (n, n), 1)).astype(f32)
    xs = [eye - a for a in mats]
    sq = [_dot(a, a) for a in mats]
    ps = sq
    k = 2
    while k < n:
        xs = [x + _dot(x, p) for x, p in zip(xs, ps)]
        k *= 2
        if k < n:
            ps = [_dot(p, p) for p in ps]
    return tuple(_dot(p, x) - a for p, x, a in zip(sq, xs, mats))


def _inv_unit_tri_fwd(mats):
    ns = _inv_unit_tri(mats)
    return ns, ns


def _inv_unit_tri_bwd(ns, dns):
    ys = [dn + _dot_tn(nn, dn) for nn, dn in zip(ns, dns)]
    return (tuple(-(y + _dot_nt(y, nn)) for y, nn in zip(ys, ns)),)


_inv_unit_tri.defvjp(_inv_unit_tri_fwd, _inv_unit_tri_bwd)


@jax.custom_vjp
def _inv_unit_tri_saved(mats, saved):
    return saved


_inv_unit_tri_saved.defvjp(lambda mats, saved: (saved, saved),
                           lambda ns, dns: _inv_unit_tri_bwd(ns, dns) + (tuple(jnp.zeros_like(n) for n in ns),))


def _gdn_chunk(heads, *, revs, saved=None, with_n=False):
    n = heads[0][0].shape[0]
    ii = lax.broadcasted_iota(jnp.int32, (n, n), 0)
    jj = lax.broadcasted_iota(jnp.int32, (n, n), 1)
    row = lax.broadcasted_iota(jnp.int32, (n, 1), 0)
    lower = {False: ii >= jj, True: ii <= jj}
    strict = {False: ii > jj, True: ii < jj}
    last = {False: n - 1, True: 0}
    H = range(len(heads))
    q, k, v, beta, gc, gr, s = (list(t) for t in zip(*heads))
    decay = [jnp.where(lower[revs[h]], jnp.exp(jnp.where(lower[revs[h]], gc[h] - gr[h], 0.0)), 0.0) for h in H]
    kk = [_mm_nt(k[h], k[h]) for h in H]
    qk = [_mm_nt(q[h], k[h]) * decay[h] for h in H]
    qs = [_mm(q[h], s[h]) for h in H]
    a_mat = tuple(jnp.where(strict[revs[h]], beta[h] * kk[h] * decay[h], 0.0) for h in H)
    gamma = [jnp.exp(gc[h]) for h in H]
    g_last = [jnp.sum(jnp.where(row == last[revs[h]], gc[h], 0.0), axis=0, keepdims=True) for h in H]
    nmat = _inv_unit_tri(a_mat) if saved is None else _inv_unit_tri_saved(a_mat, saved)
    bv = [beta[h] * v[h] for h in H]
    bk = [(beta[h] * gamma[h]) * k[h] for h in H]
    u0 = [bv[h] + _mm(nmat[h], bv[h]) for h in H]
    w = [bk[h] + _mm(nmat[h], bk[h]) for h in H]
    k_out = [k[h] * jnp.exp(g_last[h] - gc[h]) for h in H]
    u = [u0[h] - _mm(w[h], s[h]) for h in H]
    o = [gamma[h] * qs[h] + _mm(qk[h], u[h]) for h in H]
    s_new = [jnp.exp(g_last[h]) * s[h] + _mm_tn(k_out[h], u[h]) for h in H]
    outs = tuple((o[h], s_new[h]) for h in H)
    return (outs, nmat) if with_n else outs


def _gdn_specs(B, nc, rev):
    def cidx(n):
        return (nc - 1 - n) if rev else n
    tok = lambda width: pl.BlockSpec((B, CHUNK, width), lambda n: (0, cidx(n), 0))
    rowspec = pl.BlockSpec((B, None, N_GATE, CHUNK), lambda n: (0, cidx(n), 0, 0))
    st = pl.BlockSpec((B, GDN_HEADS, GDN_HEAD, GDN_HEAD), lambda n: (0, 0, 0, 0))
    ck = pl.BlockSpec((B, None, GDN_HEADS, GDN_HEAD, GDN_HEAD), lambda n: (0, cidx(n), 0, 0, 0))
    nsp = pl.BlockSpec((B, None, GDN_HEADS, CHUNK, CHUNK), lambda n: (0, cidx(n), 0, 0, 0))
    return tok, rowspec, st, ck, nsp


def _gdn_head_args(qkv_ref, bg_ref, bgr_ref, b, d, h):
    col = d * GDN_HEADS + h
    q = qkv_ref[b, :, h * GDN_HEAD:(h + 1) * GDN_HEAD]
    k = qkv_ref[b, :, D_GDN + h * GDN_HEAD:D_GDN + (h + 1) * GDN_HEAD]
    v = qkv_ref[b, :, 2 * D_GDN + h * GDN_HEAD:2 * D_GDN + (h + 1) * GDN_HEAD]
    bgv = bg_ref[b]
    return q, k, v, bgv[:, col:col + 1], bgv[:, 8 + col:9 + col], bgr_ref[b][8 + col:9 + col, :]


def _gdn_chains(B):
    return [(d, b, h) for d in range(N_DIR) for b in range(B) for h in range(GDN_HEADS)]


def gdn_fwd(qkv, bg, bgr, s0s, *, need_o, name):
    B, L, _ = qkv.shape
    nc = L // CHUNK
    specs = [_gdn_specs(B, nc, d == 1) for d in range(N_DIR)]
    chains = _gdn_chains(B)
    state_shape = (B, GDN_HEADS, GDN_HEAD, GDN_HEAD)

    def body(*refs):
        ins = [refs[3 * d:3 * d + 3] for d in range(N_DIR)]
        s0_refs = refs[6:8]
        k = 8
        o_refs = refs[k:k + 2] if need_o else None
        k += 2 if need_o else 0
        ck_refs, n_refs, sf_refs, s_scrs = refs[k:k + 2], refs[k + 2:k + 4], refs[k + 4:k + 6], refs[k + 6:k + 8]
        n = pl.program_id(0)

        @pl.when(n == 0)
        def _():
            for d in range(N_DIR):
                s_scrs[d][...] = s0_refs[d][...]

        for d in range(N_DIR):
            ck_refs[d][...] = s_scrs[d][...]
        heads = tuple(_gdn_head_args(*ins[d], b, d, h) + (s_scrs[d][b, h],) for d, b, h in chains)
        outs, nmat = _gdn_chunk(heads, revs=tuple(d == 1 for d, _, _ in chains), with_n=True)
        for (d, b, h), (o, s_new), nn in zip(chains, outs, nmat):
            if need_o:
                o_refs[d][b, :, h * GDN_HEAD:(h + 1) * GDN_HEAD] = o
            s_scrs[d][b, h] = s_new
            n_refs[d][b, h] = nn

        @pl.when(n == nc - 1)
        def _():
            for d in range(N_DIR):
                sf_refs[d][...] = s_scrs[d][...]

    in_specs, out_o, out_ck, out_n, out_sf = [], [], [], [], []
    for tok, rowspec, st, ck, nsp in specs:
        in_specs += [tok(3 * D_GDN), tok(N_GATE), rowspec]
        out_o.append(tok(D_GDN))
        out_ck.append(ck)
        out_n.append(nsp)
        out_sf.append(st)
    in_specs += [specs[0][2]] * 2
    out_specs = (out_o if need_o else []) + out_ck + out_n + out_sf
    out_shape = (([SDS((B, L, D_GDN), f32)] * 2 if need_o else []) + [SDS((B, nc) + state_shape[1:], f32)] * 2
                 + [SDS((B, nc, GDN_HEADS, CHUNK, CHUNK), f32)] * 2 + [SDS(state_shape, f32)] * 2)
    res = pl.pallas_call(
        body, name=name, grid=(nc,), in_specs=in_specs, out_specs=out_specs, out_shape=out_shape,
        scratch_shapes=[pltpu.VMEM(state_shape, f32)] * 2, compiler_params=_cparams(1),
    )(qkv, bg, bgr, qkv, bg, bgr, *s0s)
    if need_o:
        return res[0:2], res[2:4], res[4:6], res[6:8]
    return res[0:2], res[2:4], res[4:6]


def gdn_bwd(qkv, bg, bgr, cks, ns, do, dsfs, *, name):
    B, L, _ = qkv.shape
    nc = L // CHUNK
    has_do = do is not None
    specs = [_gdn_specs(B, nc, d != 1) for d in range(N_DIR)]
    chains = _gdn_chains(B)
    state_shape = (B, GDN_HEADS, GDN_HEAD, GDN_HEAD)
    per_dir = 6 if has_do else 5

    def body(*refs):
        ins = [refs[per_dir * d:per_dir * d + per_dir] for d in range(N_DIR)]
        k = per_dir * N_DIR
        dsf_refs = refs[k:k + 2]
        outs = [refs[k + 2 + 3 * d:k + 5 + 3 * d] for d in range(N_DIR)]
        ds0_refs, ds_scrs = refs[k + 8:k + 10], refs[k + 10:k + 12]
        n = pl.program_id(0)

        @pl.when(n == 0)
        def _():
            for d in range(N_DIR):
                ds_scrs[d][...] = dsf_refs[d][...]

        lane = lax.broadcasted_iota(jnp.int32, (CHUNK, N_GATE), 1)
        sub = lax.broadcasted_iota(jnp.int32, (N_GATE, CHUNK), 0)
        heads = tuple(_gdn_head_args(*ins[d][:3], b, d, h) + (ins[d][3][b, h],) for d, b, h in chains)
        saved = tuple(ins[d][4][b, h] for d, b, h in chains)
        _, vjp = jax.vjp(functools.partial(_gdn_chunk, revs=tuple(d == 1 for d, _, _ in chains), saved=saved), heads)
        zero = jnp.zeros((CHUNK, GDN_HEAD), f32)
        cts = tuple(((ins[d][5][b, :, h * GDN_HEAD:(h + 1) * GDN_HEAD] if has_do else zero), ds_scrs[d][b, h])
                    for d, b, h in chains)
        (dheads,) = vjp(cts)
        dbg_acc = [[jnp.zeros((CHUNK, N_GATE), f32) for _ in range(B)] for _ in range(N_DIR)]
        dbgr_acc = [[jnp.zeros((N_GATE, CHUNK), f32) for _ in range(B)] for _ in range(N_DIR)]
        for (d, b, h), (dq, dk, dv, db, dgc, dgr, ds) in zip(chains, dheads):
            col = d * GDN_HEADS + h
            dqkv_ref = outs[d][0]
            dqkv_ref[b, :, h * GDN_HEAD:(h + 1) * GDN_HEAD] = dq
            dqkv_ref[b, :, D_GDN + h * GDN_HEAD:D_GDN + (h + 1) * GDN_HEAD] = dk
            dqkv_ref[b, :, 2 * D_GDN + h * GDN_HEAD:2 * D_GDN + (h + 1) * GDN_HEAD] = dv
            dbg_acc[d][b] = dbg_acc[d][b] + jnp.where(lane == col, db, 0.0) + jnp.where(lane == 8 + col, dgc, 0.0)
            dbgr_acc[d][b] = dbgr_acc[d][b] + jnp.where(sub == 8 + col, dgr, 0.0)
            ds_scrs[d][b, h] = ds
        for d in range(N_DIR):
            for b in range(B):
                outs[d][1][b] = dbg_acc[d][b]
                outs[d][2][b] = dbgr_acc[d][b]

        @pl.when(n == nc - 1)
        def _():
            for d in range(N_DIR):
                ds0_refs[d][...] = ds_scrs[d][...]

    in_specs, args, out_specs, out_shape = [], [], [], []
    for d, (tok, rowspec, st, ck, nsp) in enumerate(specs):
        in_specs += [tok(3 * D_GDN), tok(N_GATE), rowspec, ck, nsp] + ([tok(D_GDN)] if has_do else [])
        args += [qkv, bg, bgr, cks[d], ns[d]] + ([do] if has_do else [])
        out_specs += [tok(3 * D_GDN), tok(N_GATE), rowspec]
        out_shape += [SDS((B, L, 3 * D_GDN), f32), SDS((B, L, N_GATE), f32), SDS((B, nc, N_GATE, CHUNK), f32)]
    st = specs[0][2]
    in_specs += [st, st]
    args += list(dsfs)
    out_specs += [st, st]
    out_shape += [SDS(state_shape, f32)] * 2
    res = pl.pallas_call(
        body, name=name, grid=(nc,), in_specs=in_specs, out_specs=out_specs, out_shape=out_shape,
        scratch_shapes=[pltpu.VMEM(state_shape, f32)] * 2, compiler_params=_cparams(1),
    )(*args)
    return (res[0], res[3]), (res[1], res[4]), (res[2], res[5]), (res[6], res[7])


def _gnorm_fn(o0, o1, z, w):
    o = o0 + o1
    return o * lax.rsqrt(jnp.mean(o * o, axis=-1, keepdims=True) + NORM_EPS) * w * _silu(z)


def _head_loss(y, x, gate, lng, lnb, tgt):
    r = DEEPNORM_ALPHA * x + gate * y
    mu = jnp.mean(r, axis=-1, keepdims=True)
    rc = r - mu
    var = jnp.mean(rc * rc, axis=-1, keepdims=True)
    err = rc * lax.rsqrt(var + LN_EPS) * lng + lnb - tgt
    return (0.5 / D_MODEL) * jnp.sum(jnp.sum(err * err, axis=-1, keepdims=True), axis=0, keepdims=True)


def tail_fwd_bwd(u, y0, y1, z_s5, o0, o1, z_gdn, x, tgt, gate, lng, lnb, ws, wg, dsk, wglu, bglu, nw):
    B, L, _ = x.shape
    T = min(TOK_TILE, L)

    def body(u_ref, y0_ref, y1_ref, z_ref, o0_ref, o1_ref, zg_ref, x_ref, t_ref, gate_ref, lng_ref, lnb_ref, ws_ref,
             wg_ref, dsk_ref, wglu_ref, bglu_ref, nw_ref,
             loss_ref, du_ref, dys_ref, dz_ref, do_ref, dzg_ref, gx_ref, dws_ref, dwg_ref, dgate_ref, dlng_ref, dlnb_ref,
             ddsk_ref, dwglu_ref, dbglu_ref, dnw_ref):
        n = pl.program_id(1)

        @pl.when(_first_step())
        def _():
            for r in (dws_ref, dwg_ref, dlng_ref, dlnb_ref, ddsk_ref, dwglu_ref, dbglu_ref, dnw_ref):
                r[...] = jnp.zeros_like(r)

        @pl.when(n == 0)
        def _():
            loss_ref[...] = jnp.zeros_like(loss_ref)
            dgate_ref[...] = jnp.zeros_like(dgate_ref)

        s5o, glu_vjp = jax.vjp(_glu_fn, u_ref[...], y0_ref[...], y1_ref[...], z_ref[...], dsk_ref[...],
                               wglu_ref[...].astype(f32), bglu_ref[...])
        heads = []
        for h in range(GDN_HEADS):
            sl = slice(h * GDN_HEAD, (h + 1) * GDN_HEAD)
            heads.append(jax.vjp(_gnorm_fn, o0_ref[:, sl], o1_ref[:, sl], zg_ref[:, sl], nw_ref[...]))
        sv = s5o.astype(bf16)
        gv = jnp.concatenate([out for out, _ in heads], axis=1).astype(bf16)
        y = _dot(sv, ws_ref[...]) + _dot(gv, wg_ref[...])
        loss, vjp = jax.vjp(lambda *a: _head_loss(*a, t_ref[...]), y, x_ref[...], gate_ref[...], lng_ref[...],
                            lnb_ref[...])
        dy, dx, dgate, dlng, dlnb = vjp(jnp.ones((1, 1), f32))
        loss_ref[...] += jnp.broadcast_to(loss, loss_ref.shape)
        dyb = dy.astype(bf16)
        gx_ref[...] = dx
        dws_ref[...] += _dot_tn(sv, dyb)
        dwg_ref[...] += _dot_tn(gv, dyb)
        dgate_ref[...] += dgate
        dlng_ref[...] += dlng
        dlnb_ref[...] += dlnb
        du, dys, _, dz, ddsk, dwglu, dbglu = glu_vjp(_dot_nt(dyb, ws_ref[...]))
        du_ref[...], dys_ref[...], dz_ref[...] = du, dys, dz
        ddsk_ref[...] += ddsk
        dwglu_ref[...] += dwglu
        dbglu_ref[...] += dbglu
        dgdo = _dot_nt(dyb, wg_ref[...])
        for h, (_, hvjp) in enumerate(heads):
            sl = slice(h * GDN_HEAD, (h + 1) * GDN_HEAD)
            do, _, dzg, dnw = hvjp(dgdo[:, sl])
            do_ref[:, sl] = do
            dzg_ref[:, sl] = dzg
            dnw_ref[...] += dnw

    half, full = _tok(T, D_S5), _tok(T, D_MODEL)
    row = _resident((1, D_MODEL))
    wsp = _resident((D_S5, D_MODEL))
    r512, rglu, r128 = _resident((1, D_S5)), _resident((D_S5, D_S5)), _resident((1, GDN_HEAD))
    return pl.pallas_call(
        body, name="tail_fwd_bwd", grid=(B, L // T),
        in_specs=[half] * 7 + [full, full, _per_batch(1, D_MODEL), row, row, wsp, wsp, r512, rglu, r512, r128],
        out_specs=[_per_batch(8, LANES)] + [half] * 5 + [full, wsp, wsp, _per_batch(1, D_MODEL), row, row, r512, rglu, r512,
                                                           r128],
        out_shape=[SDS((B, 8, LANES), f32)] + [SDS((B, L, D_S5), f32)] * 5 + [
            SDS((B, L, D_MODEL), f32), SDS((D_S5, D_MODEL), f32), SDS((D_GDN, D_MODEL), f32), SDS((B, 1, D_MODEL), f32),
            SDS((1, D_MODEL), f32), SDS((1, D_MODEL), f32), SDS((1, D_S5), f32), SDS((D_S5, D_S5), f32), SDS((1, D_S5), f32),
            SDS((1, GDN_HEAD), f32)],
        compiler_params=_cparams(2),
    )(u, y0, y1, z_s5, o0, o1, z_gdn, x, tgt, gate, lng, lnb, ws, wg, dsk, wglu, bglu, nw)


def _adamw_math(w, g, m, v):
    nm = ADAM_B1 * m + (1.0 - ADAM_B1) * g
    nv = ADAM_B2 * v + (1.0 - ADAM_B2) * jnp.square(g)
    m_hat = nm / (1.0 - ADAM_B1 ** ADAM_STEP)
    v_hat = nv / (1.0 - ADAM_B2 ** ADAM_STEP)
    return -ADAM_LR * (m_hat / (jnp.sqrt(v_hat) + ADAM_EPS) + ADAM_WD * w), nm, nv


def _row_tile(rows, cap=512):
    for t in range(min(cap, rows), 15, -1):
        if rows % t == 0 and t % 16 == 0:
            return t
    return rows


def adamw_3d(w, g, m, v, *, lead=False, name):
    R, C = (w.shape[0], w.shape[2]) if lead else w.shape[1:]
    if lead:
        T = next(t for t in range(min(256, R), 0, -1) if R % t == 0)
        spec = pl.BlockSpec((T, 1, C), lambda i: (i, 0, 0))
    else:
        T = _row_tile(R)
        spec = pl.BlockSpec((None, T, C), lambda i: (0, i, 0))

    def body(w_ref, g_ref, m_ref, v_ref, d_ref, nm_ref, nv_ref):
        d_ref[...], nm_ref[...], nv_ref[...] = _adamw_math(w_ref[...], g_ref[...], m_ref[...], v_ref[...])

    return pl.pallas_call(body, name=name, grid=(R // T,), in_specs=[spec] * 4, out_specs=[spec] * 3,
                          out_shape=[SDS(w.shape, f32)] * 3, compiler_params=_cparams(1))(w, g, m, v)


def adamw_small(ws, gs, ms, vs):
    n = len(ws)

    def body(*refs):
        outs = refs[4 * n:]
        for i in range(n):
            d, nm, nv = _adamw_math(refs[i][...], refs[n + i][...], refs[2 * n + i][...], refs[3 * n + i][...])
            outs[i][...], outs[n + i][...], outs[2 * n + i][...] = d, nm, nv

    res = pl.pallas_call(body, name="adamw_small", out_shape=[SDS(w.shape, f32) for w in ws] * 3,
                         compiler_params=pltpu.CompilerParams(vmem_limit_bytes=VMEM_LIMIT))(*ws, *gs, *ms, *vs)
    return res[:n], res[n:2 * n], res[2 * n:]


def sum_cores(own, got, *, name):
    A, H, C = own.shape
    T = _row_tile(H)
    spec = pl.BlockSpec((None, T, C), lambda a, i: (a, i, 0))

    def body(a_ref, b_ref, q32_ref, q16_ref):
        q = a_ref[...] + b_ref[...]
        q32_ref[...] = q
        q16_ref[...] = q.astype(bf16)

    return pl.pallas_call(body, name=name, grid=(A, H // T), in_specs=[spec, spec], out_specs=[spec, spec],
                          out_shape=[SDS((A, H, C), f32), SDS((A, H, C), bf16)], compiler_params=_cparams(2))(own, got)


def sum_cores_small(owns, gots):
    n = len(owns)

    def body(*refs):
        for i in range(n):
            q = refs[i][...] + refs[n + i][...]
            refs[2 * n + i][...] = q
            refs[3 * n + i][...] = q.astype(bf16)

    res = pl.pallas_call(body, name="sum_cores_small",
                         out_shape=[SDS(o.shape, f32) for o in owns] + [SDS(o.shape, bf16) for o in owns],
                         compiler_params=pltpu.CompilerParams(vmem_limit_bytes=VMEM_LIMIT))(*owns, *gots)
    return res[:n], res[n:]


def sum_chips(mine, rec, cpos, full, *, slot=None, name):
    H, C = mine.shape
    T = _row_tile(H)
    nt = H // T
    by_rows = _by_rows(full)
    out_idx = lambda i, s_ref: ((s_ref[1], s_ref[0] * nt + i, 0) if by_rows else (s_ref[1], i, s_ref[0]))

    def body(s_ref, m_ref, r_ref, f_ref):
        f_ref[...] = ((m_ref[...] + r_ref[0].astype(f32)) + r_ref[1].astype(f32)) + r_ref[2].astype(f32)

    grid_spec = pltpu.PrefetchScalarGridSpec(
        num_scalar_prefetch=1, grid=(nt,),
        in_specs=[pl.BlockSpec((T, C), lambda i, s_ref: (i, 0)), pl.BlockSpec((3, T, C), lambda i, s_ref: (0, i, 0))],
        out_specs=pl.BlockSpec((None, T, C), out_idx))
    scalars = jnp.stack([cpos, jnp.zeros_like(cpos) if slot is None else slot]).astype(jnp.int32)
    return pl.pallas_call(body, name=name, grid_spec=grid_spec,
                          out_shape=SDS((1 if slot is None else 4,) + tuple(full), f32),
                          compiler_params=_cparams(1))(scalars, mine, rec)


CHIP_FLIPS = ((1, 0), (0, 1), (1, 1))


def _pos():
    return lax.axis_index("x"), lax.axis_index("y"), lax.axis_index("c")


def _comm_call(body, srcs, out_sds, n_remote, n_local, name):
    any_spec = pl.BlockSpec(memory_space=pl.ANY)
    return pl.pallas_call(
        body, name=name, in_specs=[any_spec] * len(srcs), out_specs=[any_spec] * len(out_sds), out_shape=out_sds,
        scratch_shapes=[pltpu.SemaphoreType.DMA((n_remote,)), pltpu.SemaphoreType.DMA((n_remote,)),
                        pltpu.SemaphoreType.DMA((max(n_local, 1),))],
        compiler_params=pltpu.CompilerParams(has_side_effects=True),
    )(*srcs)


def _remote(src, dst, send_sems, recv_sems, k, target):
    return pltpu.make_async_remote_copy(src, dst, send_sems.at[k], recv_sems.at[k], device_id=target,
                                        device_id_type=MESH)


def _by_rows(shape):
    return shape[0] % 16 == 0


def _half_shape(shape):
    return (shape[0] // 2, shape[1]) if _by_rows(shape) else (shape[0], shape[1] // 2)


def _half_of(ref, lead, c, shape):
    if _by_rows(shape):
        half = shape[0] // 2
        return ref.at[(*lead, pl.ds(pl.multiple_of(c * half, 8), half))]
    half = shape[1] // 2
    return ref.at[(*lead, slice(None), pl.ds(pl.multiple_of(c * half, LANES), half))]


def gather_shards(shards):
    nt = len(shards)

    def body(*refs):
        srcs, outs = refs[:nt], refs[nt:2 * nt]
        send_sems, recv_sems, _ = refs[2 * nt:]
        x, y, c = _pos()
        j = 2 * x + y
        sib = (x, y, 1 - c)
        own = [_remote(srcs[t], outs[t].at[j], send_sems, recv_sems, 7 * t + 6, sib) for t in range(nt)]
        first, passed = [], []
        for k, (fx, fy) in enumerate(CHIP_FLIPS):
            tx, ty = x ^ fx, y ^ fy
            jk = 2 * tx + ty
            for t in range(nt):
                sh = srcs[t].shape
                first.append(_remote(_half_of(srcs[t], (), c, sh), _half_of(outs[t], (j,), c, sh), send_sems, recv_sems,
                                     7 * t + k, (tx, ty, c)))
                landed = _half_of(outs[t], (jk,), c, sh)
                passed.append(_remote(landed, landed, send_sems, recv_sems, 7 * t + 3 + k, sib))
        for cp in first + own:
            cp.start()
        for a, b in zip(first, passed):
            a.wait_recv()
            b.start()
        for cp in passed + own:
            cp.wait_recv()
        for cp in first + passed + own:
            cp.wait_send()

    return _comm_call(body, shards, [SDS((4,) + s.shape, s.dtype) for s in shards], 7 * nt, 0, "gather_shards")


def swap_halves(ps):
    nt = len(ps)

    def body(*refs):
        srcs, outs = refs[:nt], refs[nt:2 * nt]
        send_sems, recv_sems, _ = refs[2 * nt:]
        x, y, c = _pos()
        cps = [_remote(_half_of(srcs[t], (a,), 1 - c, srcs[t].shape[1:]), outs[t].at[a], send_sems, recv_sems, 4 * t + a,
                       (x, y, 1 - c)) for t in range(nt) for a in range(4)]
        for cp in cps:
            cp.start()
        for cp in cps:
            cp.wait()

    return _comm_call(body, ps, [SDS((4,) + _half_shape(p.shape[1:]), p.dtype) for p in ps], 4 * nt, 0, "swap_halves")


def scatter_to_chips(qs):
    nt = len(qs)

    def body(*refs):
        srcs, outs = refs[:nt], refs[nt:2 * nt]
        send_sems, recv_sems, _ = refs[2 * nt:]
        x, y, c = _pos()
        cps = []
        for k, (fx, fy) in enumerate(CHIP_FLIPS):
            tx, ty = x ^ fx, y ^ fy
            for t in range(nt):
                cps.append(_remote(srcs[t].at[2 * tx + ty], outs[t].at[k], send_sems, recv_sems, 3 * t + k, (tx, ty, c)))
        for cp in cps:
            cp.start()
        for cp in cps:
            cp.wait()

    return _comm_call(body, qs, [SDS((3,) + q.shape[1:], q.dtype) for q in qs], 3 * nt, 0, "scatter_to_chips")


DEV_FLIPS = tuple((fx, fy, fc) for fx in (0, 1) for fy in (0, 1) for fc in (0, 1))[1:]


def join_halves(fs, small):
    nt = len(fs)

    def body(*refs):
        outs = refs[nt + 1:2 * nt + 1]
        sm = refs[2 * nt + 1]
        send_sems, recv_sems, _ = refs[2 * nt + 2:]
        x, y, c = _pos()
        cps = []
        for t in range(nt):
            mine = _half_of(outs[t], (0,), c, outs[t].shape[1:])
            cps.append(_remote(mine, mine, send_sems, recv_sems, t, (x, y, 1 - c)))
        mine = _half_of(sm, (2 * x + y,), c, sm.shape[1:])
        for k, (fx, fy, fc) in enumerate(DEV_FLIPS):
            cps.append(_remote(mine, mine, send_sems, recv_sems, nt + k, (x ^ fx, y ^ fy, c ^ fc)))
        for cp in cps:
            cp.start()
        for cp in cps:
            cp.wait()

    any_spec = pl.BlockSpec(memory_space=pl.ANY)
    n_sem = nt + len(DEV_FLIPS)
    res = pl.pallas_call(
        body, name="join_halves", in_specs=[any_spec] * (nt + 1), out_specs=[any_spec] * (nt + 1),
        out_shape=[SDS(f.shape, f.dtype) for f in fs] + [SDS(small.shape, small.dtype)],
        input_output_aliases={t: t for t in range(nt + 1)},
        scratch_shapes=[pltpu.SemaphoreType.DMA((n_sem,)), pltpu.SemaphoreType.DMA((n_sem,)), pltpu.SemaphoreType.DMA((1,))],
        compiler_params=pltpu.CompilerParams(has_side_effects=True),
    )(*fs, small)
    return res[:nt], res[nt]


def gather_devices(block, *, name):
    def body(src, out, send_sems, recv_sems, loc_sems):
        x, y, c = _pos()
        me = 4 * x + 2 * y + c
        mine = pltpu.make_async_copy(src, out.at[me], loc_sems.at[0])
        mine.start()
        cps = [_remote(src, out.at[me], send_sems, recv_sems, k, (x ^ fx, y ^ fy, c ^ fc))
               for k, (fx, fy, fc) in enumerate(DEV_FLIPS)]
        for cp in cps:
            cp.start()
        for cp in cps:
            cp.wait()
        mine.wait()

    return _comm_call(body, [block], [SDS((N_DEV,) + block.shape, block.dtype)], 7, 1, name)[0]


def exchange_devices(blocks, *, name):
    def body(src, out, send_sems, recv_sems, loc_sems):
        x, y, c = _pos()
        me = 4 * x + 2 * y + c
        mine = pltpu.make_async_copy(src.at[me], out.at[me], loc_sems.at[0])
        mine.start()
        cps = []
        for k, (fx, fy, fc) in enumerate(DEV_FLIPS):
            tx, ty, tc = x ^ fx, y ^ fy, c ^ fc
            cps.append(_remote(src.at[4 * tx + 2 * ty + tc], out.at[me], send_sems, recv_sems, k, (tx, ty, tc)))
        for cp in cps:
            cp.start()
        for cp in cps:
            cp.wait()
        mine.wait()

    return _comm_call(body, [blocks], [SDS(blocks.shape, blocks.dtype)], 7, 1, name)[0]


SMALL_SHAPES = ((1, 2, 32, 64), (1, 2, 32, 64), (1, 2, 32), (1, 2, 32, 16, 64),
                (1, 2, 32, 16, 64), (1, 2, 32, 16, 64), (1, 2, 32, 16, 64), (1, D_S5), (1, D_S5), (1, 2, 4), (1, 2, 4),
                (1, GDN_HEAD), (1, D_MODEL), (1, D_MODEL), (LANES,))
SMALL_SWAPPED = (3, 4)


def _size(shape):
    return functools.reduce(lambda p, q: p * q, shape)


SMALL_ROWS = tuple(-(-_size(s) // (8 * LANES)) * 8 for s in SMALL_SHAPES)
SMALL_TOTAL = 2240
SMALL_QUARTER = SMALL_TOTAL // 4


def _rows(a):
    flat = a.reshape(-1)
    pad = (-flat.shape[0]) % (8 * LANES)
    if pad:
        flat = jnp.concatenate([flat, jnp.zeros((pad,), flat.dtype)])
    return flat.reshape(-1, LANES)


def _pack_small(parts):
    rows = [_rows(p) for p in parts]
    rows.append(jnp.zeros((SMALL_TOTAL - sum(SMALL_ROWS), LANES), f32))
    return jnp.concatenate(rows, axis=0)


def _unpack_small(buf):
    out, r = [], 0
    for s, n in zip(SMALL_SHAPES, SMALL_ROWS):
        out.append(buf[r:r + n].reshape(-1)[:_size(s)].reshape(s))
        r += n
    return out


def _as_2d(a):
    return a.reshape(1, -1) if a.ndim == 1 else a.reshape(-1, a.shape[-1])


S5_BG = S5_GROUPS // S5_BLOCKS


def _block_diag_in(bb):
    lead = bb.shape[:-2]
    eye = jnp.eye(S5_BG, dtype=bb.dtype)
    b4 = bb.reshape(lead + (S5_BLOCKS, S5_BG, S5_GROUP, S5_STATE))
    return jnp.einsum('...jgcp,gh->...jgchp', b4, eye).reshape(lead + (S5_BLOCKS, S5_BC, S5_BS))


def _block_diag_in_t(d):
    lead = d.shape[:-3]
    d6 = d.reshape(lead + (S5_BLOCKS, S5_BG, S5_GROUP, S5_BG, S5_STATE))
    return jnp.einsum('...jgcgp->...jgcp', d6).reshape(lead + (S5_GROUPS, S5_GROUP * S5_STATE))


def _block_diag_out(cm):
    lead = cm.shape[:-3]
    eye = jnp.eye(S5_BG, dtype=cm.dtype)
    c4 = cm.reshape(lead + (S5_BLOCKS, S5_BG, S5_GROUP, S5_STATE))
    return jnp.einsum('...jgcp,gh->...jhpgc', c4, eye).reshape(lead + (S5_BLOCKS, S5_BS, S5_BC))


def _block_diag_out_t(d):
    lead = d.shape[:-3]
    d6 = d.reshape(lead + (S5_BLOCKS, S5_BG, S5_STATE, S5_BG, S5_GROUP))
    return jnp.einsum('...jgpgc->...jgcp', d6).reshape(lead + (S5_GROUPS, S5_GROUP, S5_STATE))


def _to_chunk_rows(a):
    B, L, W = a.shape
    return a.reshape(B, L // CHUNK, CHUNK, W).transpose(0, 1, 3, 2)


def _from_chunk_rows(a):
    B, nc, W, _ = a.shape
    return a.transpose(0, 1, 3, 2).reshape(B, nc * CHUNK, W)


def local_step(x, ctx, tgt, m, w_in, lam_re, lam_im, log_dt, b_re, b_im, c_re, c_im, s5_d,
               w_glu, b_glu, conv16, a_log, dt_bias, norm_w, w_out, ln_g, ln_b):
    B, L, _ = x.shape
    zeros_state = jnp.zeros((B, GDN_HEADS, GDN_HEAD, GDN_HEAD), f32)

    shift, scale, gate = m[:B, :D_MODEL], m[:B, D_MODEL:2 * D_MODEL], m[:B, 2 * D_MODEL:]
    mod = jnp.stack([scale, shift], axis=1)
    mod_c = jnp.broadcast_to(jnp.stack([m[B, D_MODEL:2 * D_MODEL], m[B, :D_MODEL]], axis=0)[None], (B, 2, D_MODEL))

    u, z_s5, qkv, z_gdn, ba = in_proj_fwd(x, mod, w_in, name="in_proj_fwd")
    uc, _, qkvc, _, bac = in_proj_fwd(ctx, mod_c, w_in, name="in_proj_fwd_ctx")

    ng = N_DIR * S5_GROUPS
    zoh_in = (lam_re.reshape(ng, S5_STATE), lam_im.reshape(ng, S5_STATE), log_dt.reshape(ng, 1),
              b_re.reshape(ng, S5_GROUP * S5_STATE), b_im.reshape(ng, S5_GROUP * S5_STATE))
    expand = (jnp.arange(S5_GROUP * S5_STATE)[None, :] % S5_STATE == jnp.arange(S5_STATE)[:, None]).astype(f32)
    ar, ai, bbr, bbi = s5_zoh_fwd(*zoh_in, expand)
    b_blocks = _block_diag_in(jnp.stack([bbr, bbi]).astype(bf16).reshape(2, N_DIR, S5_GROUPS, S5_GROUP * S5_STATE))
    c_blocks = _block_diag_out(jnp.stack([c_re, -c_im]).astype(bf16).reshape(2, N_DIR, S5_GROUPS, S5_GROUP, S5_STATE))
    a_rows = jnp.stack([ar, ai]).reshape(2, N_DIR, S5_HALF)
    s5w, ys, hins, hins_c, hss, hss_c = [], [], [], [], [], []
    for d in range(N_DIR):
        wd = (b_blocks[0, d], b_blocks[1, d], c_blocks[0, d], c_blocks[1, d], a_rows[:, d])
        s5w.append(wd)
        hs_c, hin_c, hend_c = s5_scan_fwd(uc, *wd, jnp.zeros((B, 2, S5_HALF), f32), d=d, need_y=False,
                                          name=f"s5_fwd_ctx{d}")
        y_d, hs_d, hin, _ = s5_scan_fwd(u, *wd, hend_c, d=d, need_y=True, name=f"s5_fwd{d}")
        hss.append(hs_d)
        hss_c.append(hs_c)
        ys.append(y_d)
        hins.append(hin)
        hins_c.append(hin_c)
    glu_w = (s5_d.reshape(1, D_S5), w_glu, b_glu.reshape(1, D_S5))

    act, pre = conv_fwd(qkv, conv16, is_ctx=False, name="conv_fwd")
    act_c, pre_c = conv_fwd(qkvc, conv16, is_ctx=True, name="conv_fwd_ctx")
    pad8 = jnp.zeros((1, 8), f32)
    alog16 = jnp.concatenate([pad8, a_log.reshape(1, 8)], axis=1)
    dtb16 = jnp.concatenate([pad8, dt_bias.reshape(1, 8)], axis=1)
    bg = gates_fwd(ba, alog16, dtb16, name="gates_fwd")
    bg_c = gates_fwd(bac, alog16, dtb16, name="gates_fwd_ctx")
    bgr, bgr_c = _to_chunk_rows(bg), _to_chunk_rows(bg_c)
    cks_c, ns_c, s_c = gdn_fwd(act_c, bg_c, bgr_c, (zeros_state, zeros_state), need_o=False, name="gdn_fwd_ctx")
    os_, cks, ns, _ = gdn_fwd(act, bg, bgr, s_c, need_o=True, name="gdn_fwd")
    nw = norm_w.reshape(1, GDN_HEAD)

    (loss8, du_skip, dy, dz_s5, do, dz_gdn, gx_res, dws, dwg, dgate, dlng, dlnb, d_s5_d, d_w_glu, d_b_glu,
     d_norm_w) = tail_fwd_bwd(u, ys[0], ys[1], z_s5, os_[0], os_[1], z_gdn, x, tgt, gate[:, None, :],
                              ln_g.reshape(1, D_MODEL), ln_b.reshape(1, D_MODEL), w_out[:D_S5], w_out[D_S5:], *glu_w, nw)
    loss = jnp.sum(loss8[:, 0, 0])
    d_w_out = jnp.concatenate([dws, dwg], axis=0)

    dacts, dbgs, dbgrs, ds0s = gdn_bwd(act, bg, bgr, cks, ns, do, (zeros_state, zeros_state), name="gdn_bwd")
    dacts_c, dbgs_c, dbgrs_c, _ = gdn_bwd(act_c, bg_c, bgr_c, cks_c, ns_c, None, ds0s, name="gdn_bwd_ctx")
    dbg = dbgs[0] + dbgs[1] + _from_chunk_rows(dbgrs[0] + dbgrs[1])
    dbg_c = dbgs_c[0] + dbgs_c[1] + _from_chunk_rows(dbgrs_c[0] + dbgrs_c[1])
    dba, dal, ddt = gates_bwd(ba, alog16, dtb16, dbg, name="gates_bwd")
    dbac, dal_c, ddt_c = gates_bwd(bac, alog16, dtb16, dbg_c, name="gates_bwd_ctx")
    d_a_log = (dal + dal_c)[:, 8:].reshape(1, N_DIR, GDN_HEADS)
    d_dt_bias = (ddt + ddt_c)[:, 8:].reshape(1, N_DIR, GDN_HEADS)
    dqkv, dcw = conv_bwd(qkv, pre, conv16, dacts[0], dacts[1], is_ctx=False, name="conv_bwd")
    dqkvc, dcw_c = conv_bwd(qkvc, pre_c, conv16, dacts_c[0], dacts_c[1], is_ctx=True, name="conv_bwd_ctx")
    d_conv16 = jnp.sum(dcw, axis=0) + jnp.sum(dcw_c, axis=0)

    dus, ducs = [du_skip], []
    das, dbs, dcs = [], [], []
    for d in range(N_DIR):
        du_d, dbre1, dbim1, dct1, dcb1, da1, dh0 = s5_scan_bwd(u, dy, hss[d], *s5w[d], hins[d],
                                                                jnp.zeros((B, 2, S5_HALF), f32), d=d, name=f"s5_bwd{d}")
        duc_d, dbre2, dbim2, _, _, da2, _ = s5_scan_bwd(uc, None, hss_c[d], *s5w[d], hins_c[d], dh0, d=d,
                                                        name=f"s5_bwd_ctx{d}")
        dus.append(du_d)
        ducs.append(duc_d)
        das.append(da1 + da2)
        dbs.append(jnp.stack([dbre1 + dbre2, dbim1 + dbim2]))
        dcs.append(jnp.stack([dct1, dcb1]))
    ng_shape = (N_DIR * S5_GROUPS, -1)
    da = jnp.stack(das, axis=1)
    db = _block_diag_in_t(jnp.stack(dbs, axis=1))
    dc = _block_diag_out_t(jnp.stack(dcs, axis=1))
    dlr, dli, dldt, dbre, dbim = s5_zoh_bwd(*zoh_in, expand, da[0].reshape(ng_shape), da[1].reshape(ng_shape),
                                            db[0].reshape(ng_shape), db[1].reshape(ng_shape))
    d_s5 = (dlr, dli, dldt, dbre, dbim, dc[0], -dc[1])

    zc = jnp.zeros_like(uc)
    dw_c, dmod_c = in_proj_bwd(ctx, mod_c, (tuple(ducs), zc, dqkvc, zc, dbac), w_in, None, None,
                               name="in_proj_bwd_ctx")
    d_w_in, dmod, grad_x = in_proj_bwd(x, mod, (tuple(dus), dz_s5, dqkv, dz_gdn, dba), w_in, gx_res, dw_c,
                                       name="in_proj_bwd")
    dmod_c = jnp.sum(dmod_c, axis=0)

    dm_rows = jnp.concatenate([dmod[:, 1], dmod[:, 0], dgate[:, 0]], axis=1)
    dm_ctx = jnp.concatenate([dmod_c[1], dmod_c[0], jnp.zeros((D_MODEL,), f32)])[None]
    dm = jnp.concatenate([dm_rows, dm_ctx], axis=0)
    small = (*d_s5, d_s5_d, d_b_glu, d_a_log, d_dt_bias, d_norm_w, dlng, dlnb)
    small = tuple(g.reshape(s) for g, s in zip(small, SMALL_SHAPES))
    return loss, grad_x, (d_w_in, d_w_out, d_w_glu, d_conv16), small, dm


SHARDED = (1, 3, 18, 12, 14)
UNSHARDED = tuple(i for i in range(21) if i not in SHARDED)
SMALL = tuple(i for i in UNSHARDED if i not in (0, 2))
W_IN_SHARD = 772


def _conv_rows(w):
    return jnp.concatenate([w.reshape(9, w.shape[-1]), jnp.zeros((CONV_ROWS - 9, w.shape[-1]), f32)], axis=0)


def kernel(x, c, ctx, c_ctx, w_ada, b_ada, w_in, s5_lambda_re, s5_lambda_im, s5_log_dt, s5_b_re, s5_b_im, s5_c_re, s5_c_im, s5_d, w_glu, b_glu, conv_w, gdn_a_log, gdn_dt_bias, gdn_norm_w, w_out, ln_g, ln_b, loss_target, m_c_ctx, m_w_ada, m_b_ada, m_w_in, m_s5_lambda_re, m_s5_lambda_im, m_s5_log_dt, m_s5_b_re, m_s5_b_im, m_s5_c_re, m_s5_c_im, m_s5_d, m_w_glu, m_b_glu, m_conv_w, m_gdn_a_log, m_gdn_dt_bias, m_gdn_norm_w, m_w_out, m_ln_g, m_ln_b, v_c_ctx, v_w_ada, v_b_ada, v_w_in, v_s5_lambda_re, v_s5_lambda_im, v_s5_log_dt, v_s5_b_re, v_s5_b_im, v_s5_c_re, v_s5_c_im, v_s5_d, v_w_glu, v_b_glu, v_conv_w, v_gdn_a_log, v_gdn_dt_bias, v_gdn_norm_w, v_w_out, v_ln_g, v_ln_b):
    weights = [c_ctx, w_ada, b_ada, w_in, s5_lambda_re, s5_lambda_im, s5_log_dt, s5_b_re, s5_b_im, s5_c_re, s5_c_im,
               s5_d, w_glu, b_glu, conv_w, gdn_a_log, gdn_dt_bias, gdn_norm_w, w_out, ln_g, ln_b]
    ms = [m_c_ctx, m_w_ada, m_b_ada, m_w_in, m_s5_lambda_re, m_s5_lambda_im, m_s5_log_dt, m_s5_b_re, m_s5_b_im,
          m_s5_c_re, m_s5_c_im, m_s5_d, m_w_glu, m_b_glu, m_conv_w, m_gdn_a_log, m_gdn_dt_bias, m_gdn_norm_w, m_w_out,
          m_ln_g, m_ln_b]
    vs = [v_c_ctx, v_w_ada, v_b_ada, v_w_in, v_s5_lambda_re, v_s5_lambda_im, v_s5_log_dt, v_s5_b_re, v_s5_b_im,
          v_s5_c_re, v_s5_c_im, v_s5_d, v_w_glu, v_b_glu, v_conv_w, v_gdn_a_log, v_gdn_dt_bias, v_gdn_norm_w, v_w_out,
          v_ln_g, v_ln_b]
    cpos = lax.axis_index("c")
    jchip = 2 * lax.axis_index("x") + lax.axis_index("y")

    c_all = gather_devices(c, name="gather_c")
    cc = jnp.concatenate([c_all, jnp.broadcast_to(c_ctx[None, None, :], (N_DEV, 1, D_MODEL)),
                          jnp.zeros((N_DEV, 5, D_MODEL), f32)], axis=1)
    w_ada16 = w_ada[0].astype(bf16)
    b_cols = lax.dynamic_slice_in_dim(b_ada, jchip * ADA_SHARD, ADA_SHARD, axis=1)
    m_mine = exchange_devices(ada_fwd(cc, w_ada16, b_cols), name="exchange_m")
    m_rows = jnp.concatenate([m_mine[2 * j, :3] for j in range(4)], axis=1)

    conv_shard = _conv_rows(conv_w)
    g_in, g_out, g_glu, g_conv = gather_shards(
        [jnp.transpose(w_in[0]).astype(bf16), w_out[0].astype(bf16), w_glu[0].astype(bf16), conv_shard])
    w_in_t = g_in.reshape(P_IN, D_MODEL)
    conv16 = g_conv.transpose(1, 0, 2).reshape(CONV_ROWS, 3 * D_GDN)

    swap = lambda a: jnp.swapaxes(a, 3, 4)
    loss, grad_x, big, small, dm_rows = local_step(
        x, ctx, loss_target, m_rows, w_in_t, s5_lambda_re, s5_lambda_im, s5_log_dt, swap(s5_b_re), swap(s5_b_im),
        s5_c_re, s5_c_im, s5_d, g_glu.reshape(D_S5, D_S5), b_glu, conv16, gdn_a_log, gdn_dt_bias, gdn_norm_w,
        g_out.reshape(D_MODEL, D_MODEL), ln_g, ln_b)
    me = 2 * jchip + cpos
    loss_hi = loss.astype(bf16).astype(f32)
    loss_row = jnp.zeros((LANES,), f32).at[me].set(loss_hi).at[N_DEV + me].set(loss - loss_hi)

    dm8 = jnp.concatenate([dm_rows, jnp.zeros((5, 3 * D_MODEL), f32)], axis=0)
    dm_by_chip = dm8.reshape(8, 4, ADA_SHARD).transpose(1, 0, 2)
    dm_cols = exchange_devices(jnp.repeat(dm_by_chip, 2, axis=0), name="exchange_dm")
    g_w_ada, pb = ada_bwd(cc, w_ada16, dm_cols)
    pb_all = gather_devices(pb, name="gather_p")
    g_c_ctx = c_ctx_bwd(pb_all, c_ctx[None, :])[0]
    g_b_ada = jnp.concatenate([pb_all[2 * j, 1:2, :ADA_SHARD] for j in range(4)], axis=1)

    d_w_in, d_w_out, d_w_glu, d_conv16 = big
    slabs = [d_w_in.reshape(4, W_IN_SHARD, D_MODEL),
             d_w_out.reshape(4, D_MODEL // 4, D_MODEL),
             d_w_glu.reshape(4, D_S5 // 4, D_S5),
             d_conv16.reshape(CONV_ROWS, 4, 3 * D_GDN // 4).transpose(1, 0, 2),
             _pack_small(small + (loss_row,)).reshape(4, SMALL_QUARTER, LANES)]
    got = swap_halves(slabs)
    owns = []
    for s in slabs:
        if _by_rows(s.shape[1:]):
            owns.append(lax.dynamic_index_in_dim(s.reshape(4, 2, s.shape[1] // 2, s.shape[2]), cpos, axis=1, keepdims=False))
        else:
            owns.append(lax.dynamic_slice_in_dim(s, cpos * (s.shape[2] // 2), s.shape[2] // 2, axis=2))
    big32, big16 = sum_cores(owns[0], got[0], name="sum_cores0")
    rest32, rest16 = sum_cores_small(owns[1:], got[1:])
    q32, q16 = [big32, *rest32], [big16, *rest16]
    rec = scatter_to_chips(q16)
    fs = [sum_chips(lax.dynamic_index_in_dim(q, jchip, axis=0, keepdims=False), r, cpos, s.shape[1:],
                    slot=jchip if t == 4 else None, name=f"sum_chips{t}")
          for t, (q, r, s) in enumerate(zip(q32, rec, slabs))]
    red, small_all = join_halves(fs[:4], fs[4])
    g_small = _unpack_small(small_all.reshape(SMALL_TOTAL, LANES))
    loss = jnp.sum(g_small[-1][:2 * N_DEV])
    g_small = g_small[:-1]
    g_shard = {1: g_w_ada, 3: red[0], 18: red[1], 12: red[2], 14: red[3]}

    grads, deltas, new_m, new_v = [None] * 21, [None] * 21, [None] * 21, [None] * 21
    for t, i in enumerate(SHARDED):
        conv, win = i == 14, i == 3
        prep = (lambda a: _conv_rows(a)[None]) if conv else ((lambda a: jnp.transpose(a, (2, 0, 1))) if win else (lambda a: a))
        g = jnp.transpose(g_shard[i], (1, 0, 2)) if win else g_shard[i]
        d, nm, nv = adamw_3d(prep(weights[i]), g, prep(ms[i]), prep(vs[i]), lead=win, name=f"adamw{t}")
        for lst, val in ((grads, g), (deltas, d), (new_m, nm), (new_v, nv)):
            lst[i] = (val[0, :9].reshape(weights[i].shape) if conv else (jnp.transpose(val, (1, 2, 0)) if win else val))
    g_un = {0: g_c_ctx, 2: g_b_ada, **{i: g_small[n] for n, i in enumerate(SMALL)}}
    swapped = [SMALL[n] for n in SMALL_SWAPPED]
    small_in = lambda lst: [_as_2d(swap(lst[i]) if i in swapped else lst[i]) for i in UNSHARDED]
    sm = adamw_small(small_in(weights), [_as_2d(g_un[i]) for i in UNSHARDED], small_in(ms), small_in(vs))
    for n, i in enumerate(UNSHARDED):
        back = ((lambda a: swap(a.reshape(swap(weights[i]).shape))) if i in swapped
                else (lambda a: a.reshape(weights[i].shape)))
        grads[i] = back(g_un[i])
        for lst, res in ((deltas, sm[0]), (new_m, sm[1]), (new_v, sm[2])):
            lst[i] = back(res[n])
    return (loss, grad_x, *grads, *deltas, *new_m, *new_v)
```

```python
import functools

import jax
import jax.numpy as jnp
from jax import lax
from jax.experimental import pallas as pl
from jax.experimental.pallas import tpu as pltpu

f32 = jnp.float32
bf16 = jnp.bfloat16
SDS = jax.ShapeDtypeStruct

D_MODEL = 1024
D_S5 = 512
S5_GROUP = 16
S5_GROUPS = 32
S5_STATE = 64
S5_HALF = S5_GROUPS * S5_STATE
D_GDN = 512
GDN_HEAD = 128
GDN_HEADS = 4
CHUNK = 64
GRID_W = 64
N_DIR = 2
P_IN = 3088
DEEPNORM_ALPHA = 2.0 ** 0.25
LN_EPS = 1e-5
NORM_EPS = 1e-6
ADAM_LR, ADAM_B1, ADAM_B2, ADAM_EPS, ADAM_WD, ADAM_STEP = 0.001, 0.9, 0.999, 1e-08, 0.01, 10

LANES = 128
VMEM_LIMIT = 56 * 1024 * 1024
TOK_TILE = 256
S5_TILE = 256
MESH = pl.DeviceIdType.MESH


def _cparams(n_grid):
    return pltpu.CompilerParams(dimension_semantics=("arbitrary",) * n_grid, vmem_limit_bytes=VMEM_LIMIT)


def _dot(a, b):
    return jnp.dot(a.astype(bf16), b.astype(bf16), preferred_element_type=f32)


def _dot_nt(a, b):
    return lax.dot_general(a.astype(bf16), b.astype(bf16), (((1,), (1,)), ((), ())), preferred_element_type=f32)


def _dot_tn(a, b):
    return lax.dot_general(a.astype(bf16), b.astype(bf16), (((0,), (0,)), ((), ())), preferred_element_type=f32)


def _dot_hi(a, b):
    return jnp.dot(a, b, precision=lax.Precision.HIGHEST, preferred_element_type=f32)


@jax.custom_vjp
def _mm(a, b):
    return _dot(a, b)


@jax.custom_vjp
def _mm_nt(a, b):
    return _dot_nt(a, b)


@jax.custom_vjp
def _mm_tn(a, b):
    return _dot_tn(a, b)


_mm.defvjp(lambda a, b: (_dot(a, b), (a, b)), lambda r, g: (_mm_nt(g, r[1]), _mm_tn(r[0], g)))
_mm_nt.defvjp(lambda a, b: (_dot_nt(a, b), (a, b)), lambda r, g: (_mm(g, r[1]), _mm_tn(g, r[0])))
_mm_tn.defvjp(lambda a, b: (_dot_tn(a, b), (a, b)), lambda r, g: (_mm_nt(r[1], g), _mm(r[0], g)))


def _silu(x):
    return x * jax.nn.sigmoid(x)


def _gelu(x):
    return 0.5 * x * (1.0 + lax.erf(x * (2.0 ** -0.5)))


def _resident(shape):
    nd = len(shape)
    return pl.BlockSpec(shape, lambda *_: (0,) * nd, pipeline_mode=pl.Buffered(1))


def _tok(tile, width, nt=None, rev=False):
    if rev:
        return pl.BlockSpec((None, tile, width), lambda b, n: (b, nt - 1 - n, 0))
    return pl.BlockSpec((None, tile, width), lambda b, n: (b, n, 0))


def _per_batch(rows, width):
    return pl.BlockSpec((None, rows, width), lambda b, n: (b, 0, 0))


def _first_step():
    return jnp.logical_and(pl.program_id(0) == 0, pl.program_id(1) == 0)


ADA_SHARD = 3 * D_MODEL // 4
N_DEV = 8


def ada_fwd(cc, w, b):
    def body(cc_ref, w_ref, b_ref, m_ref):
        for k in range(N_DEV):
            m_ref[k] = _dot(_silu(cc_ref[k]), w_ref[...]) + b_ref[...]

    return pl.pallas_call(body, name="ada_fwd", out_shape=SDS((N_DEV, 8, ADA_SHARD), f32),
                          compiler_params=pltpu.CompilerParams(vmem_limit_bytes=VMEM_LIMIT))(cc, w, b)


def ada_bwd(cc, w, dmj):
    def body(cc_ref, w_ref, dmj_ref, dw_ref, pb_ref):
        dw = jnp.zeros((D_MODEL, ADA_SHARD), f32)
        p = jnp.zeros((8, D_MODEL), f32)
        db = jnp.zeros((1, ADA_SHARD), f32)
        for k in range(N_DEV):
            dw = dw + _dot_tn(_silu(cc_ref[k]), dmj_ref[k])
            p = p + _dot_nt(dmj_ref[k], w_ref[...])
            db = db + jnp.sum(dmj_ref[k], axis=0, keepdims=True)
        dw_ref[0] = dw
        pb_ref[...] = jnp.zeros_like(pb_ref)
        pb_ref[0:1, :] = p[2:3, :]
        pb_ref[1:2, 0:ADA_SHARD] = db

    return pl.pallas_call(
        body, name="ada_bwd", out_shape=[SDS((1, D_MODEL, ADA_SHARD), f32), SDS((8, D_MODEL), f32)],
        compiler_params=pltpu.CompilerParams(vmem_limit_bytes=VMEM_LIMIT))(cc, w, dmj)


def c_ctx_bwd(pb_all, c_ctx):
    def body(p_ref, c_ref, d_ref):
        ds = ((p_ref[0, 0:1, :] + p_ref[2, 0:1, :]) + p_ref[4, 0:1, :]) + p_ref[6, 0:1, :]
        _, vjp = jax.vjp(_silu, c_ref[...])
        d_ref[...] = vjp(ds)[0]

    return pl.pallas_call(body, name="c_ctx_bwd", out_shape=SDS((1, D_MODEL), f32))(pb_all, c_ctx)


N_GATE = 2 * N_DIR * GDN_HEADS
IN_WIDTHS = (D_S5, D_S5, 3 * D_GDN, D_GDN, N_GATE)
IN_OFFS = (0, 512, 1024, 2560, 3072)


def in_proj_fwd(x, mod, wt, *, name):
    B, L, _ = x.shape
    T = min(2 * TOK_TILE, L)

    def body(x_ref, mod_ref, w_ref, *o_refs):
        h = (x_ref[...] * (1.0 + mod_ref[0:1, :]) + mod_ref[1:2, :]).astype(bf16)
        for o_ref, off, wd in zip(o_refs, IN_OFFS, IN_WIDTHS):
            o_ref[...] = _dot_nt(h, w_ref[off:off + wd, :])

    return pl.pallas_call(
        body, name=name, grid=(B, L // T),
        in_specs=[_tok(T, D_MODEL), _per_batch(2, D_MODEL), _resident((P_IN, D_MODEL))],
        out_specs=[_tok(T, wd) for wd in IN_WIDTHS],
        out_shape=[SDS((B, L, wd), f32) for wd in IN_WIDTHS],
        compiler_params=_cparams(2),
    )(x, mod, wt)


def in_proj_bwd(x, mod, ds, wt, gx_res, dw_start, *, name):
    B, L, _ = x.shape
    T = min(TOK_TILE, L)
    with_dx = gx_res is not None
    with_start = dw_start is not None
    n_u = len(ds[0])

    def body(*refs):
        x_ref, mod_ref = refs[0], refs[1]
        du_refs = refs[2:2 + n_u]
        d_refs = refs[2 + n_u:6 + n_u]
        w_ref = refs[6 + n_u]
        k = 7 + n_u
        if with_dx:
            gx_ref = refs[k]
            k += 1
        if with_start:
            start_ref = refs[k]
            k += 1
        dw_ref, dmod_ref = refs[k], refs[k + 1]
        if with_dx:
            dx_ref = refs[k + 2]
        n = pl.program_id(1)

        @pl.when(_first_step())
        def _():
            dw_ref[...] = start_ref[...] if with_start else jnp.zeros_like(dw_ref)

        @pl.when(n == 0)
        def _():
            dmod_ref[...] = jnp.zeros_like(dmod_ref)

        xv = x_ref[...]
        scale1 = 1.0 + mod_ref[0:1, :]
        h = (xv * scale1 + mod_ref[1:2, :]).astype(bf16)
        du = du_refs[0][...]
        for r in du_refs[1:]:
            du = du + r[...]
        dh = jnp.zeros((T, D_MODEL), f32)
        for dv, off, wd in zip([du] + [r[...] for r in d_refs], IN_OFFS, IN_WIDTHS):
            dv = dv.astype(bf16)
            dh = dh + _dot(dv, w_ref[off:off + wd, :])
            dw_ref[off:off + wd, :] += _dot_tn(dv, h)
        dmod_ref[0:1, :] += jnp.sum(dh * xv, axis=0, keepdims=True)
        dmod_ref[1:2, :] += jnp.sum(dh, axis=0, keepdims=True)
        if with_dx:
            dx_ref[...] = gx_ref[...] + dh * scale1

    in_specs = ([_tok(T, D_MODEL), _per_batch(2, D_MODEL)] + [_tok(T, D_S5)] * n_u + [_tok(T, wd) for wd in IN_WIDTHS[1:]]
                + [_resident((P_IN, D_MODEL))])
    args = [x, mod, *ds[0], *ds[1:], wt]
    out_specs = [_resident((P_IN, D_MODEL)), _per_batch(2, D_MODEL)]
    out_shape = [SDS((P_IN, D_MODEL), f32), SDS((B, 2, D_MODEL), f32)]
    if with_dx:
        in_specs.append(_tok(T, D_MODEL))
        args.append(gx_res)
        out_specs.append(_tok(T, D_MODEL))
        out_shape.append(SDS((B, L, D_MODEL), f32))
    if with_start:
        in_specs.append(_resident((P_IN, D_MODEL)))
        args.append(dw_start)
    return pl.pallas_call(body, name=name, grid=(B, L // T), in_specs=in_specs, out_specs=out_specs,
                          out_shape=out_shape, compiler_params=_cparams(2))(*args)


def _s5_zoh(lr, li, ldt, bre, bim, expand):
    dt = jnp.exp(ldt)
    zr, zi = lr * dt, li * dt
    e = jnp.exp(zr)
    ar, ai = e * jnp.cos(zi), e * jnp.sin(zi)
    den = lr * lr + li * li
    czr = ((ar - 1.0) * lr + ai * li) / den
    czi = (ai * lr - (ar - 1.0) * li) / den
    czr_e, czi_e = _dot_hi(czr, expand), _dot_hi(czi, expand)
    return ar, ai, czr_e * bre - czi_e * bim, czr_e * bim + czi_e * bre


_ZOH_OUT = [(N_DIR * S5_GROUPS, S5_STATE)] * 2 + [(N_DIR * S5_GROUPS, S5_STATE * S5_GROUP)] * 2


def s5_zoh_fwd(lr, li, ldt, bre, bim, expand):
    def body(lr_ref, li_ref, ldt_ref, bre_ref, bim_ref, e_ref, ar_ref, ai_ref, bbr_ref, bbi_ref):
        ar, ai, bbr, bbi = _s5_zoh(lr_ref[...], li_ref[...], ldt_ref[...], bre_ref[...], bim_ref[...], e_ref[...])
        ar_ref[...], ai_ref[...], bbr_ref[...], bbi_ref[...] = ar, ai, bbr, bbi

    return pl.pallas_call(body, name="s5_zoh_fwd", out_shape=[SDS(s, f32) for s in _ZOH_OUT])(
        lr, li, ldt, bre, bim, expand)


def s5_zoh_bwd(lr, li, ldt, bre, bim, expand, dar, dai, dbbr, dbbi):
    def body(lr_ref, li_ref, ldt_ref, bre_ref, bim_ref, e_ref, dar_ref, dai_ref, dbbr_ref, dbbi_ref,
             dlr_ref, dli_ref, dldt_ref, dbre_ref, dbim_ref):
        ev = e_ref[...]
        _, vjp = jax.vjp(lambda a, b, c, d, e: _s5_zoh(a, b, c, d, e, ev),
                         lr_ref[...], li_ref[...], ldt_ref[...], bre_ref[...], bim_ref[...])
        outs = vjp((dar_ref[...], dai_ref[...], dbbr_ref[...], dbbi_ref[...]))
        dlr_ref[...], dli_ref[...], dldt_ref[...], dbre_ref[...], dbim_ref[...] = outs

    shapes = [lr.shape, li.shape, ldt.shape, bre.shape, bim.shape]
    return pl.pallas_call(body, name="s5_zoh_bwd", out_shape=[SDS(s, f32) for s in shapes])(
        lr, li, ldt, bre, bim, expand, dar, dai, dbbr, dbbi)


def _scan_rows(T, rev, ar, ai, h0s, refs, off):
    def step(i, carry):
        t = off + ((T - 1 - i) if rev else i)
        out = []
        for (hr, hi), (r_ref, i_ref) in zip(carry, refs):
            nr = ar * hr - ai * hi + r_ref[pl.ds(t, 1), :]
            ni = ar * hi + ai * hr + i_ref[pl.ds(t, 1), :]
            r_ref[pl.ds(t, 1), :] = nr
            i_ref[pl.ds(t, 1), :] = ni
            out.append((nr, ni))
        return tuple(out)

    return lax.fori_loop(0, T, step, tuple(h0s))


S5_BLOCKS = 4
S5_BC = D_S5 // S5_BLOCKS
S5_BS = S5_HALF // S5_BLOCKS


def _s5_in(uv, bre_ref, bim_ref, hr_ref, hi_ref, off, T):
    for jb in range(S5_BLOCKS):
        uj = uv[:, jb * S5_BC:(jb + 1) * S5_BC]
        hr_ref[off:off + T, jb * S5_BS:(jb + 1) * S5_BS] = _dot(uj, bre_ref[jb])
        hi_ref[off:off + T, jb * S5_BS:(jb + 1) * S5_BS] = _dot(uj, bim_ref[jb])


def _s5_specs(B, T, nt, rev):
    tidx = (lambda n: nt - 1 - n) if rev else (lambda n: n)
    tok = pl.BlockSpec((B, T, D_S5), lambda n: (0, tidx(n), 0))
    hin = pl.BlockSpec((B, None, 2, S5_HALF), lambda n: (0, tidx(n), 0, 0))
    state = pl.BlockSpec((B, 2, S5_HALF), lambda n: (0, 0, 0))
    return tok, hin, state


def s5_scan_fwd(u, bre, bim, ctop, cbot, arow, h0, *, d, need_y, name):
    B, L, _ = u.shape
    T = min(S5_TILE, L)
    nt = L // T
    rev = d == 1

    def body(u_ref, bre_ref, bim_ref, ct_ref, cb_ref, a_ref, h0_ref, *rest):
        if need_y:
            y_ref, hs_ref, hin_ref, hend_ref, hr_scr, hi_scr, h_scr = rest
        else:
            hs_ref, hin_ref, hend_ref, hr_scr, hi_scr, h_scr = rest
        n = pl.program_id(0)

        @pl.when(n == 0)
        def _():
            h_scr[...] = h0_ref[...]

        hin_ref[...] = h_scr[...]
        for b in range(B):
            _s5_in(u_ref[b].astype(bf16), bre_ref, bim_ref, hr_scr.at[b], hi_scr.at[b], 0, T)
        hs = _scan_rows(T, rev, a_ref[0:1, :], a_ref[1:2, :], [(h_scr[b, 0:1, :], h_scr[b, 1:2, :]) for b in range(B)],
                        [(hr_scr.at[b], hi_scr.at[b]) for b in range(B)], 0)
        for b in range(B):
            h_scr[b, 0:1, :] = hs[b][0]
            h_scr[b, 1:2, :] = hs[b][1]
            hs_ref[b, :, 0:S5_HALF] = hr_scr[b].astype(bf16)
            hs_ref[b, :, S5_HALF:2 * S5_HALF] = hi_scr[b].astype(bf16)
            if need_y:
                for jb in range(S5_BLOCKS):
                    st = slice(jb * S5_BS, (jb + 1) * S5_BS)
                    y_ref[b, :, jb * S5_BC:(jb + 1) * S5_BC] = (_dot(hr_scr[b, :, st], ct_ref[jb])
                                                                 + _dot(hi_scr[b, :, st], cb_ref[jb]))

        @pl.when(n == nt - 1)
        def _():
            hend_ref[...] = h_scr[...]

    tok, hin_spec, state = _s5_specs(B, T, nt, rev)
    hs_spec = pl.BlockSpec((B, T, 2 * S5_HALF), tok.index_map)
    out_specs = [hs_spec, hin_spec, state]
    out_shape = [SDS((B, L, 2 * S5_HALF), bf16), SDS((B, nt, 2, S5_HALF), f32), SDS((B, 2, S5_HALF), f32)]
    if need_y:
        out_specs.insert(0, tok)
        out_shape.insert(0, SDS((B, L, D_S5), f32))
    w_in, w_out = _resident((S5_BLOCKS, S5_BC, S5_BS)), _resident((S5_BLOCKS, S5_BS, S5_BC))
    return pl.pallas_call(
        body, name=name, grid=(nt,),
        in_specs=[tok, w_in, w_in, w_out, w_out, _resident((2, S5_HALF)), state],
        out_specs=out_specs, out_shape=out_shape,
        scratch_shapes=[pltpu.VMEM((B, T, S5_HALF), f32), pltpu.VMEM((B, T, S5_HALF), f32),
                        pltpu.VMEM((B, 2, S5_HALF), f32)],
        compiler_params=_cparams(1),
    )(u, bre, bim, ctop, cbot, arow, h0)


def s5_scan_bwd(u, dy, hs, bre, bim, ctop, cbot, arow, hin, dhend, *, d, name):
    B, L, _ = u.shape
    T = min(S5_TILE, L)
    nt = L // T
    rev = d == 1
    has_dy = dy is not None
    PAD = 8

    def body(*refs):
        u_ref = refs[0]
        k = 1
        if has_dy:
            dy_ref = refs[1]
            k = 2
        hs_ref = refs[k]
        k += 1
        bre_ref, bim_ref, ct_ref, cb_ref, a_ref, hin_ref, dhend_ref = refs[k:k + 7]
        du_ref, dbre_ref, dbim_ref, dct_ref, dcb_ref, da_ref, dh0_ref = refs[k + 7:k + 14]
        hr_scr, hi_scr, gr_scr, gi_scr, p_scr = refs[k + 14:]
        n = pl.program_id(0)

        @pl.when(n == 0)
        def _():
            for r in (dbre_ref, dbim_ref, dct_ref, dcb_ref, da_ref):
                r[...] = jnp.zeros_like(r)
            p_scr[...] = dhend_ref[...]

        ar, ai = a_ref[0:1, :], a_ref[1:2, :]
        prev_row = PAD + T if rev else PAD - 1
        uvs = []
        for b in range(B):
            uvs.append(u_ref[b].astype(bf16))
            hr_scr[b, PAD:PAD + T, :] = hs_ref[b, :, 0:S5_HALF].astype(f32)
            hi_scr[b, PAD:PAD + T, :] = hs_ref[b, :, S5_HALF:2 * S5_HALF].astype(f32)
            hr_scr[b, prev_row:prev_row + 1, :] = hin_ref[b, 0:1, :]
            hi_scr[b, prev_row:prev_row + 1, :] = hin_ref[b, 1:2, :]
        if has_dy:
            for b in range(B):
                dyv = dy_ref[b].astype(bf16)
                for jb in range(S5_BLOCKS):
                    st = slice(jb * S5_BS, (jb + 1) * S5_BS)
                    dyj = dyv[:, jb * S5_BC:(jb + 1) * S5_BC]
                    gr_scr[b, :, st] = _dot_nt(dyj, ct_ref[jb])
                    gi_scr[b, :, st] = _dot_nt(dyj, cb_ref[jb])
                    dct_ref[jb] += _dot_tn(hr_scr[b, PAD:PAD + T, st], dyj)
                    dcb_ref[jb] += _dot_tn(hi_scr[b, PAD:PAD + T, st], dyj)
        else:
            gr_scr[...] = jnp.zeros_like(gr_scr)
            gi_scr[...] = jnp.zeros_like(gi_scr)

        def step(i, carry):
            t = i if rev else T - 1 - i
            tp = PAD + t + (1 if rev else -1)
            out = []
            for b, (pr, pi, dar, dai) in enumerate(carry):
                gr = gr_scr[b, pl.ds(t, 1), :] + pr
                gi = gi_scr[b, pl.ds(t, 1), :] + pi
                gr_scr[b, pl.ds(t, 1), :] = gr
                gi_scr[b, pl.ds(t, 1), :] = gi
                hpr = hr_scr[b, pl.ds(tp, 1), :]
                hpi = hi_scr[b, pl.ds(tp, 1), :]
                out.append((ar * gr + ai * gi, ar * gi - ai * gr, dar + hpr * gr + hpi * gi, dai + hpr * gi - hpi * gr))
            return tuple(out)

        zero = jnp.zeros((1, S5_HALF), f32)
        res = lax.fori_loop(0, T, step, tuple((p_scr[b, 0:1, :], p_scr[b, 1:2, :], zero, zero) for b in range(B)))
        for b in range(B):
            pr, pi, dar, dai = res[b]
            p_scr[b, 0:1, :] = pr
            p_scr[b, 1:2, :] = pi
            da_ref[0:1, :] += dar
            da_ref[1:2, :] += dai
            for jb in range(S5_BLOCKS):
                st = slice(jb * S5_BS, (jb + 1) * S5_BS)
                ch = slice(jb * S5_BC, (jb + 1) * S5_BC)
                gr_j = gr_scr[b, :, st].astype(bf16)
                gi_j = gi_scr[b, :, st].astype(bf16)
                du_ref[b, :, ch] = _dot_nt(gr_j, bre_ref[jb]) + _dot_nt(gi_j, bim_ref[jb])
                dbre_ref[jb] += _dot_tn(uvs[b][:, ch], gr_j)
                dbim_ref[jb] += _dot_tn(uvs[b][:, ch], gi_j)

        @pl.when(n == nt - 1)
        def _():
            dh0_ref[...] = p_scr[...]

    tok, hin_spec, state = _s5_specs(B, T, nt, not rev)
    hs_spec = pl.BlockSpec((B, T, 2 * S5_HALF), tok.index_map)
    w_in, w_out = _resident((S5_BLOCKS, S5_BC, S5_BS)), _resident((S5_BLOCKS, S5_BS, S5_BC))
    wspecs = [w_in, w_in, w_out, w_out]
    in_specs = [tok] + ([tok] if has_dy else []) + [hs_spec] + wspecs + [_resident((2, S5_HALF)), hin_spec, state]
    args = [u] + ([dy] if has_dy else []) + [hs, bre, bim, ctop, cbot, arow, hin, dhend]
    return pl.pallas_call(
        body, name=name, grid=(nt,), in_specs=in_specs,
        out_specs=[tok] + wspecs + [_resident((2, S5_HALF)), state],
        out_shape=[SDS((B, L, D_S5), f32), SDS((S5_BLOCKS, S5_BC, S5_BS), f32), SDS((S5_BLOCKS, S5_BC, S5_BS), f32),
                   SDS((S5_BLOCKS, S5_BS, S5_BC), f32), SDS((S5_BLOCKS, S5_BS, S5_BC), f32), SDS((2, S5_HALF), f32),
                   SDS((B, 2, S5_HALF), f32)],
        scratch_shapes=[pltpu.VMEM((B, T + 2 * PAD, S5_HALF), f32), pltpu.VMEM((B, T + 2 * PAD, S5_HALF), f32),
                        pltpu.VMEM((B, T, S5_HALF), f32), pltpu.VMEM((B, T, S5_HALF), f32),
                        pltpu.VMEM((B, 2, S5_HALF), f32)],
        compiler_params=_cparams(1),
    )(*args)


def _glu_fn(u, y0, y1, z, dsk, wg, bg):
    g = _gelu(dsk * u + y0 + y1)
    return g * jax.nn.sigmoid(_mm(g, wg) + bg) * _silu(z)


CONV_ROWS = 16


def _shift(x, s):
    L = x.shape[0]
    k = (-s) % L
    return x if k == 0 else pltpu.roll(x, k, axis=0)


def _r16(v):
    return v.astype(bf16).astype(f32)


def _conv_masks(L, is_ctx):
    t = lax.broadcasted_iota(jnp.int32, (L, 1), 0)
    if is_ctx:
        return t == L - 1, t == 0, None, None
    col = jnp.bitwise_and(t, GRID_W - 1)
    return col == GRID_W - 1, col == 0, t >= GRID_W, t < L - GRID_W


def _conv_sides(xv, masks):
    no_left, no_right, _, _ = masks
    return _shift(jnp.where(no_left, 0.0, xv), -1), _shift(jnp.where(no_right, 0.0, xv), 1)


def _conv_pre(xv, w_ref, masks, is_ctx):
    xv = _r16(xv)
    wv = _r16(w_ref[...])
    xl, xr = _conv_sides(xv, masks)
    z = [wv[3 * di:3 * di + 1, :] * xl + wv[3 * di + 1:3 * di + 2, :] * xv + wv[3 * di + 2:3 * di + 3, :] * xr
         for di in ((1,) if is_ctx else (0, 1, 2))]
    if is_ctx:
        return z[0]
    _, _, has_up, has_down = masks
    return z[1] + jnp.where(has_up, _shift(z[0], -GRID_W), 0.0) + jnp.where(has_down, _shift(z[2], GRID_W), 0.0)


def _conv_pre_bwd(xv, w_ref, dpre, masks, is_ctx, dw_ref):
    no_left, no_right, has_up, has_down = masks
    xv, dpre, wv = _r16(xv), _r16(dpre), _r16(w_ref[...])
    xl, xr = _conv_sides(xv, masks)
    if is_ctx:
        dz = {1: dpre}
    else:
        dz = {0: _shift(jnp.where(has_up, dpre, 0.0), GRID_W), 1: dpre, 2: _shift(jnp.where(has_down, dpre, 0.0), -GRID_W)}
    dxl = dxc = dxr = None
    for di, d in dz.items():
        for dj, side in enumerate((xl, xv, xr)):
            dw_ref[3 * di + dj:3 * di + dj + 1, :] = jnp.sum(d * side, axis=0, keepdims=True)
        tl, tc, tr = (wv[3 * di + dj:3 * di + dj + 1, :] * d for dj in range(3))
        dxl, dxc, dxr = (tl, tc, tr) if dxl is None else (dxl + tl, dxc + tc, dxr + tr)
    return dxc + jnp.where(no_left, 0.0, _shift(dxl, 1)) + jnp.where(no_right, 0.0, _shift(dxr, -1))


def _qk_post(pre, is_norm, scale):
    s = _silu(pre)
    nrm = lax.rsqrt(jnp.sum(s * s, axis=-1, keepdims=True) + NORM_EPS)
    return s * jnp.where(is_norm, nrm * scale, 1.0)


def _conv_tile(L):
    return D_GDN if L <= 512 else 2 * GDN_HEAD


def _conv_kind(W):
    head = pl.program_id(1) * (W // GDN_HEAD)
    return head < 2 * GDN_HEADS, jnp.where(head < GDN_HEADS, GDN_HEAD ** -0.5, 1.0).astype(f32)


def _conv_specs(L, W):
    spec = pl.BlockSpec((None, L, W), lambda b, ct: (b, 0, ct))
    wspec = pl.BlockSpec((CONV_ROWS, W), lambda b, ct: (0, ct))
    dwspec = pl.BlockSpec((None, CONV_ROWS, W), lambda b, ct: (b, 0, ct))
    return spec, wspec, dwspec


def conv_fwd(qkv, w16, *, is_ctx, name):
    B, L, C = qkv.shape
    W = _conv_tile(L)
    spec, wspec, _ = _conv_specs(L, W)

    def body(x_ref, w_ref, o_ref, pre_ref):
        is_norm, scale = _conv_kind(W)
        pre = _conv_pre(x_ref[...], w_ref, _conv_masks(L, is_ctx), is_ctx)
        pre_ref[...] = pre
        for h in range(W // GDN_HEAD):
            sl = slice(h * GDN_HEAD, (h + 1) * GDN_HEAD)
            o_ref[:, sl] = _qk_post(pre[:, sl], is_norm, scale)

    return pl.pallas_call(body, name=name, grid=(B, C // W), in_specs=[spec, wspec], out_specs=[spec, spec],
                          out_shape=[SDS((B, L, C), f32)] * 2, compiler_params=_cparams(2))(qkv, w16)


def conv_bwd(qkv, pre, w16, da0, da1, *, is_ctx, name):
    B, L, C = qkv.shape
    W = _conv_tile(L)
    spec, wspec, dwspec = _conv_specs(L, W)

    def body(x_ref, pre_ref, w_ref, d0_ref, d1_ref, dx_ref, dw_ref):
        is_norm, scale = _conv_kind(W)
        dpre = []
        for h in range(W // GDN_HEAD):
            sl = slice(h * GDN_HEAD, (h + 1) * GDN_HEAD)
            _, vjp = jax.vjp(lambda p: _qk_post(p, is_norm, scale), pre_ref[:, sl])
            dpre.append(vjp(d0_ref[:, sl] + d1_ref[:, sl])[0])
        dpre = dpre[0] if len(dpre) == 1 else jnp.concatenate(dpre, axis=1)
        dw_ref[...] = jnp.zeros_like(dw_ref)
        dx_ref[...] = _conv_pre_bwd(x_ref[...], w_ref, dpre, _conv_masks(L, is_ctx), is_ctx, dw_ref)

    return pl.pallas_call(body, name=name, grid=(B, C // W), in_specs=[spec, spec, wspec, spec, spec],
                          out_specs=[spec, dwspec], out_shape=[SDS((B, L, C), f32), SDS((B, CONV_ROWS, C), f32)],
                          compiler_params=_cparams(2))(qkv, pre, w16, da0, da1)


def _gates_fn(ba, alog, dtb):
    T = ba.shape[0]
    nck = T // CHUNK
    lane = lax.broadcasted_iota(jnp.int32, ba.shape, 1)
    ii = lax.broadcasted_iota(jnp.int32, (nck, CHUNK, CHUNK), 1)
    jj = lax.broadcasted_iota(jnp.int32, (nck, CHUNK, CHUNK), 2)
    g = jnp.where(lane >= 8, -jnp.exp(alog) * jax.nn.softplus(ba + dtb), 0.0)
    g3 = g.reshape(nck, CHUNK, N_GATE)
    chunk_sum = lambda tri: lax.dot_general(tri.astype(f32), g3, (((2,), (1,)), ((0,), (0,))),
                                            precision=lax.Precision.HIGHEST, preferred_element_type=f32).reshape(T, N_GATE)
    gc = jnp.where(lane >= 12, chunk_sum(ii <= jj), chunk_sum(ii >= jj))
    return jnp.where(lane < 8, jax.nn.sigmoid(ba), gc)


def gates_fwd(ba, alog, dtb, *, name):
    B, L, _ = ba.shape
    T = min(TOK_TILE, L)
    t = _tok(T, N_GATE)

    def body(ba_ref, al_ref, dt_ref, o_ref):
        o_ref[...] = _gates_fn(ba_ref[...], al_ref[...], dt_ref[...])

    return pl.pallas_call(body, name=name, grid=(B, L // T),
                          in_specs=[t, _resident((1, N_GATE)), _resident((1, N_GATE))], out_specs=t,
                          out_shape=SDS((B, L, N_GATE), f32), compiler_params=_cparams(2))(ba, alog, dtb)


def gates_bwd(ba, alog, dtb, dbg, *, name):
    B, L, _ = ba.shape
    T = min(TOK_TILE, L)
    t = _tok(T, N_GATE)
    small = _resident((1, N_GATE))

    def body(ba_ref, al_ref, dt_ref, d_ref, dba_ref, dal_ref, ddt_ref):
        @pl.when(_first_step())
        def _():
            dal_ref[...] = jnp.zeros_like(dal_ref)
            ddt_ref[...] = jnp.zeros_like(ddt_ref)

        _, vjp = jax.vjp(_gates_fn, ba_ref[...], al_ref[...], dt_ref[...])
        dba, dal, ddt = vjp(d_ref[...])
        dba_ref[...] = dba
        dal_ref[...] += dal
        ddt_ref[...] += ddt

    return pl.pallas_call(body, name=name, grid=(B, L // T), in_specs=[t, small, small, t],
                          out_specs=[t, small, small],
                          out_shape=[SDS((B, L, N_GATE), f32), SDS((1, N_GATE), f32), SDS((1, N_GATE), f32)],
                          compiler_params=_cparams(2))(ba, alog, dtb, dbg)


@jax.custom_vjp
def _inv_unit_tri(mats):
    n = mats[0].shape[0]
    eye = (lax.broadcasted_iota(jnp.int32, (n, n), 0) == lax.broadcasted_iota(jnp.int32, (n, n), 1)).astype(f32)
    xs = [eye - a for a in mats]
    sq = [_dot(a, a) for a in mats]
    ps = sq
    k = 2
    while k < n:
        xs = [x + _dot(x, p) for x, p in zip(xs, ps)]
        k *= 2
        if k < n:
            ps = [_dot(p, p) for p in ps]
    return tuple(_dot(p, x) - a for p, x, a in zip(sq, xs, mats))


def _inv_unit_tri_fwd(mats):
    ns = _inv_unit_tri(mats)
    return ns, ns


def _inv_unit_tri_bwd(ns, dns):
    ys = [dn + _dot_tn(nn, dn) for nn, dn in zip(ns, dns)]
    return (tuple(-(y + _dot_nt(y, nn)) for y, nn in zip(ys, ns)),)


_inv_unit_tri.defvjp(_inv_unit_tri_fwd, _inv_unit_tri_bwd)


@jax.custom_vjp
def _inv_unit_tri_saved(mats, saved):
    return saved


_inv_unit_tri_saved.defvjp(lambda mats, saved: (saved, saved),
                           lambda ns, dns: _inv_unit_tri_bwd(ns, dns) + (tuple(jnp.zeros_like(n) for n in ns),))


def _gdn_chunk(heads, *, revs, saved=None, with_n=False):
    n = heads[0][0].shape[0]
    ii = lax.broadcasted_iota(jnp.int32, (n, n), 0)
    jj = lax.broadcasted_iota(jnp.int32, (n, n), 1)
    row = lax.broadcasted_iota(jnp.int32, (n, 1), 0)
    lower = {False: ii >= jj, True: ii <= jj}
    strict = {False: ii > jj, True: ii < jj}
    last = {False: n - 1, True: 0}
    H = range(len(heads))
    q, k, v, beta, gc, gr, s = (list(t) for t in zip(*heads))
    decay = [jnp.where(lower[revs[h]], jnp.exp(jnp.where(lower[revs[h]], gc[h] - gr[h], 0.0)), 0.0) for h in H]
    kk = [_mm_nt(k[h], k[h]) for h in H]
    qk = [_mm_nt(q[h], k[h]) * decay[h] for h in H]
    qs = [_mm(q[h], s[h]) for h in H]
    a_mat = tuple(jnp.where(strict[revs[h]], beta[h] * kk[h] * decay[h], 0.0) for h in H)
    gamma = [jnp.exp(gc[h]) for h in H]
    g_last = [jnp.sum(jnp.where(row == last[revs[h]], gc[h], 0.0), axis=0, keepdims=True) for h in H]
    nmat = _inv_unit_tri(a_mat) if saved is None else _inv_unit_tri_saved(a_mat, saved)
    bv = [beta[h] * v[h] for h in H]
    bk = [(beta[h] * gamma[h]) * k[h] for h in H]
    u0 = [bv[h] + _mm(nmat[h], bv[h]) for h in H]
    w = [bk[h] + _mm(nmat[h], bk[h]) for h in H]
    k_out = [k[h] * jnp.exp(g_last[h] - gc[h]) for h in H]
    u = [u0[h] - _mm(w[h], s[h]) for h in H]
    o = [gamma[h] * qs[h] + _mm(qk[h], u[h]) for h in H]
    s_new = [jnp.exp(g_last[h]) * s[h] + _mm_tn(k_out[h], u[h]) for h in H]
    outs = tuple((o[h], s_new[h]) for h in H)
    return (outs, nmat) if with_n else outs


def _gdn_specs(B, nc, rev):
    def cidx(n):
        return (nc - 1 - n) if rev else n
    tok = lambda width: pl.BlockSpec((B, CHUNK, width), lambda n: (0, cidx(n), 0))
    rowspec = pl.BlockSpec((B, None, N_GATE, CHUNK), lambda n: (0, cidx(n), 0, 0))
    st = pl.BlockSpec((B, GDN_HEADS, GDN_HEAD, GDN_HEAD), lambda n: (0, 0, 0, 0))
    ck = pl.BlockSpec((B, None, GDN_HEADS, GDN_HEAD, GDN_HEAD), lambda n: (0, cidx(n), 0, 0, 0))
    nsp = pl.BlockSpec((B, None, GDN_HEADS, CHUNK, CHUNK), lambda n: (0, cidx(n), 0, 0, 0))
    return tok, rowspec, st, ck, nsp


def _gdn_head_args(qkv_ref, bg_ref, bgr_ref, b, d, h):
    col = d * GDN_HEADS + h
    q = qkv_ref[b, :, h * GDN_HEAD:(h + 1) * GDN_HEAD]
    k = qkv_ref[b, :, D_GDN + h * GDN_HEAD:D_GDN + (h + 1) * GDN_HEAD]
    v = qkv_ref[b, :, 2 * D_GDN + h * GDN_HEAD:2 * D_GDN + (h + 1) * GDN_HEAD]
    bgv = bg_ref[b]
    return q, k, v, bgv[:, col:col + 1], bgv[:, 8 + col:9 + col], bgr_ref[b][8 + col:9 + col, :]


def _gdn_chains(B):
    return [(d, b, h) for d in range(N_DIR) for b in range(B) for h in range(GDN_HEADS)]


def gdn_fwd(qkv, bg, bgr, s0s, *, need_o, name):
    B, L, _ = qkv.shape
    nc = L // CHUNK
    specs = [_gdn_specs(B, nc, d == 1) for d in range(N_DIR)]
    chains = _gdn_chains(B)
    state_shape = (B, GDN_HEADS, GDN_HEAD, GDN_HEAD)

    def body(*refs):
        ins = [refs[3 * d:3 * d + 3] for d in range(N_DIR)]
        s0_refs = refs[6:8]
        k = 8
        o_refs = refs[k:k + 2] if need_o else None
        k += 2 if need_o else 0
        ck_refs, n_refs, sf_refs, s_scrs = refs[k:k + 2], refs[k + 2:k + 4], refs[k + 4:k + 6], refs[k + 6:k + 8]
        n = pl.program_id(0)

        @pl.when(n == 0)
        def _():
            for d in range(N_DIR):
                s_scrs[d][...] = s0_refs[d][...]

        for d in range(N_DIR):
            ck_refs[d][...] = s_scrs[d][...]
        heads = tuple(_gdn_head_args(*ins[d], b, d, h) + (s_scrs[d][b, h],) for d, b, h in chains)
        outs, nmat = _gdn_chunk(heads, revs=tuple(d == 1 for d, _, _ in chains), with_n=True)
        for (d, b, h), (o, s_new), nn in zip(chains, outs, nmat):
            if need_o:
                o_refs[d][b, :, h * GDN_HEAD:(h + 1) * GDN_HEAD] = o
            s_scrs[d][b, h] = s_new
            n_refs[d][b, h] = nn

        @pl.when(n == nc - 1)
        def _():
            for d in range(N_DIR):
                sf_refs[d][...] = s_scrs[d][...]

    in_specs, out_o, out_ck, out_n, out_sf = [], [], [], [], []
    for tok, rowspec, st, ck, nsp in specs:
        in_specs += [tok(3 * D_GDN), tok(N_GATE), rowspec]
        out_o.append(tok(D_GDN))
        out_ck.append(ck)
        out_n.append(nsp)
        out_sf.append(st)
    in_specs += [specs[0][2]] * 2
    out_specs = (out_o if need_o else []) + out_ck + out_n + out_sf
    out_shape = (([SDS((B, L, D_GDN), f32)] * 2 if need_o else []) + [SDS((B, nc) + state_shape[1:], f32)] * 2
                 + [SDS((B, nc, GDN_HEADS, CHUNK, CHUNK), f32)] * 2 + [SDS(state_shape, f32)] * 2)
    res = pl.pallas_call(
        body, name=name, grid=(nc,), in_specs=in_specs, out_specs=out_specs, out_shape=out_shape,
        scratch_shapes=[pltpu.VMEM(state_shape, f32)] * 2, compiler_params=_cparams(1),
    )(qkv, bg, bgr, qkv, bg, bgr, *s0s)
    if need_o:
        return res[0:2], res[2:4], res[4:6], res[6:8]
    return res[0:2], res[2:4], res[4:6]


def gdn_bwd(qkv, bg, bgr, cks, ns, do, dsfs, *, name):
    B, L, _ = qkv.shape
    nc = L // CHUNK
    has_do = do is not None
    specs = [_gdn_specs(B, nc, d != 1) for d in range(N_DIR)]
    chains = _gdn_chains(B)
    state_shape = (B, GDN_HEADS, GDN_HEAD, GDN_HEAD)
    per_dir = 6 if has_do else 5

    def body(*refs):
        ins = [refs[per_dir * d:per_dir * d + per_dir] for d in range(N_DIR)]
        k = per_dir * N_DIR
        dsf_refs = refs[k:k + 2]
        outs = [refs[k + 2 + 3 * d:k + 5 + 3 * d] for d in range(N_DIR)]
        ds0_refs, ds_scrs = refs[k + 8:k + 10], refs[k + 10:k + 12]
        n = pl.program_id(0)

        @pl.when(n == 0)
        def _():
            for d in range(N_DIR):
                ds_scrs[d][...] = dsf_refs[d][...]

        lane = lax.broadcasted_iota(jnp.int32, (CHUNK, N_GATE), 1)
        sub = lax.broadcasted_iota(jnp.int32, (N_GATE, CHUNK), 0)
        heads = tuple(_gdn_head_args(*ins[d][:3], b, d, h) + (ins[d][3][b, h],) for d, b, h in chains)
        saved = tuple(ins[d][4][b, h] for d, b, h in chains)
        _, vjp = jax.vjp(functools.partial(_gdn_chunk, revs=tuple(d == 1 for d, _, _ in chains), saved=saved), heads)
        zero = jnp.zeros((CHUNK, GDN_HEAD), f32)
        cts = tuple(((ins[d][5][b, :, h * GDN_HEAD:(h + 1) * GDN_HEAD] if has_do else zero), ds_scrs[d][b, h])
                    for d, b, h in chains)
        (dheads,) = vjp(cts)
        dbg_acc = [[jnp.zeros((CHUNK, N_GATE), f32) for _ in range(B)] for _ in range(N_DIR)]
        dbgr_acc = [[jnp.zeros((N_GATE, CHUNK), f32) for _ in range(B)] for _ in range(N_DIR)]
        for (d, b, h), (dq, dk, dv, db, dgc, dgr, ds) in zip(chains, dheads):
            col = d * GDN_HEADS + h
            dqkv_ref = outs[d][0]
            dqkv_ref[b, :, h * GDN_HEAD:(h + 1) * GDN_HEAD] = dq
            dqkv_ref[b, :, D_GDN + h * GDN_HEAD:D_GDN + (h + 1) * GDN_HEAD] = dk
            dqkv_ref[b, :, 2 * D_GDN + h * GDN_HEAD:2 * D_GDN + (h + 1) * GDN_HEAD] = dv
            dbg_acc[d][b] = dbg_acc[d][b] + jnp.where(lane == col, db, 0.0) + jnp.where(lane == 8 + col, dgc, 0.0)
            dbgr_acc[d][b] = dbgr_acc[d][b] + jnp.where(sub == 8 + col, dgr, 0.0)
            ds_scrs[d][b, h] = ds
        for d in range(N_DIR):
            for b in range(B):
                outs[d][1][b] = dbg_acc[d][b]
                outs[d][2][b] = dbgr_acc[d][b]

        @pl.when(n == nc - 1)
        def _():
            for d in range(N_DIR):
                ds0_refs[d][...] = ds_scrs[d][...]

    in_specs, args, out_specs, out_shape = [], [], [], []
    for d, (tok, rowspec, st, ck, nsp) in enumerate(specs):
        in_specs += [tok(3 * D_GDN), tok(N_GATE), rowspec, ck, nsp] + ([tok(D_GDN)] if has_do else [])
        args += [qkv, bg, bgr, cks[d], ns[d]] + ([do] if has_do else [])
        out_specs += [tok(3 * D_GDN), tok(N_GATE), rowspec]
        out_shape += [SDS((B, L, 3 * D_GDN), f32), SDS((B, L, N_GATE), f32), SDS((B, nc, N_GATE, CHUNK), f32)]
    st = specs[0][2]
    in_specs += [st, st]
    args += list(dsfs)
    out_specs += [st, st]
    out_shape += [SDS(state_shape, f32)] * 2
    res = pl.pallas_call(
        body, name=name, grid=(nc,), in_specs=in_specs, out_specs=out_specs, out_shape=out_shape,
        scratch_shapes=[pltpu.VMEM(state_shape, f32)] * 2, compiler_params=_cparams(1),
    )(*args)
    return (res[0], res[3]), (res[1], res[4]), (res[2], res[5]), (res[6], res[7])


def _gnorm_fn(o0, o1, z, w):
    o = o0 + o1
    return o * lax.rsqrt(jnp.mean(o * o, axis=-1, keepdims=True) + NORM_EPS) * w * _silu(z)


def _head_loss(y, x, gate, lng, lnb, tgt):
    r = DEEPNORM_ALPHA * x + gate * y
    mu = jnp.mean(r, axis=-1, keepdims=True)
    rc = r - mu
    var = jnp.mean(rc * rc, axis=-1, keepdims=True)
    err = rc * lax.rsqrt(var + LN_EPS) * lng + lnb - tgt
    return (0.5 / D_MODEL) * jnp.sum(jnp.sum(err * err, axis=-1, keepdims=True), axis=0, keepdims=True)


def tail_fwd_bwd(u, y0, y1, z_s5, o0, o1, z_gdn, x, tgt, gate, lng, lnb, ws, wg, dsk, wglu, bglu, nw):
    B, L, _ = x.shape
    T = min(TOK_TILE, L)

    def body(u_ref, y0_ref, y1_ref, z_ref, o0_ref, o1_ref, zg_ref, x_ref, t_ref, gate_ref, lng_ref, lnb_ref, ws_ref,
             wg_ref, dsk_ref, wglu_ref, bglu_ref, nw_ref,
             loss_ref, du_ref, dys_ref, dz_ref, do_ref, dzg_ref, gx_ref, dws_ref, dwg_ref, dgate_ref, dlng_ref, dlnb_ref,
             ddsk_ref, dwglu_ref, dbglu_ref, dnw_ref):
        n = pl.program_id(1)

        @pl.when(_first_step())
        def _():
            for r in (dws_ref, dwg_ref, dlng_ref, dlnb_ref, ddsk_ref, dwglu_ref, dbglu_ref, dnw_ref):
                r[...] = jnp.zeros_like(r)

        @pl.when(n == 0)
        def _():
            loss_ref[...] = jnp.zeros_like(loss_ref)
            dgate_ref[...] = jnp.zeros_like(dgate_ref)

        s5o, glu_vjp = jax.vjp(_glu_fn, u_ref[...], y0_ref[...], y1_ref[...], z_ref[...], dsk_ref[...],
                               wglu_ref[...].astype(f32), bglu_ref[...])
        heads = []
        for h in range(GDN_HEADS):
            sl = slice(h * GDN_HEAD, (h + 1) * GDN_HEAD)
            heads.append(jax.vjp(_gnorm_fn, o0_ref[:, sl], o1_ref[:, sl], zg_ref[:, sl], nw_ref[...]))
        sv = s5o.astype(bf16)
        gv = jnp.concatenate([out for out, _ in heads], axis=1).astype(bf16)
        y = _dot(sv, ws_ref[...]) + _dot(gv, wg_ref[...])
        loss, vjp = jax.vjp(lambda *a: _head_loss(*a, t_ref[...]), y, x_ref[...], gate_ref[...], lng_ref[...],
                            lnb_ref[...])
        dy, dx, dgate, dlng, dlnb = vjp(jnp.ones((1, 1), f32))
        loss_ref[...] += jnp.broadcast_to(loss, loss_ref.shape)
        dyb = dy.astype(bf16)
        gx_ref[...] = dx
        dws_ref[...] += _dot_tn(sv, dyb)
        dwg_ref[...] += _dot_tn(gv, dyb)
        dgate_ref[...] += dgate
        dlng_ref[...] += dlng
        dlnb_ref[...] += dlnb
        du, dys, _, dz, ddsk, dwglu, dbglu = glu_vjp(_dot_nt(dyb, ws_ref[...]))
        du_ref[...], dys_ref[...], dz_ref[...] = du, dys, dz
        ddsk_ref[...] += ddsk
        dwglu_ref[...] += dwglu
        dbglu_ref[...] += dbglu
        dgdo = _dot_nt(dyb, wg_ref[...])
        for h, (_, hvjp) in enumerate(heads):
            sl = slice(h * GDN_HEAD, (h + 1) * GDN_HEAD)
            do, _, dzg, dnw = hvjp(dgdo[:, sl])
            do_ref[:, sl] = do
            dzg_ref[:, sl] = dzg
            dnw_ref[...] += dnw

    half, full = _tok(T, D_S5), _tok(T, D_MODEL)
    row = _resident((1, D_MODEL))
    wsp = _resident((D_S5, D_MODEL))
    r512, rglu, r128 = _resident((1, D_S5)), _resident((D_S5, D_S5)), _resident((1, GDN_HEAD))
    return pl.pallas_call(
        body, name="tail_fwd_bwd", grid=(B, L // T),
        in_specs=[half] * 7 + [full, full, _per_batch(1, D_MODEL), row, row, wsp, wsp, r512, rglu, r512, r128],
        out_specs=[_per_batch(8, LANES)] + [half] * 5 + [full, wsp, wsp, _per_batch(1, D_MODEL), row, row, r512, rglu, r512,
                                                           r128],
        out_shape=[SDS((B, 8, LANES), f32)] + [SDS((B, L, D_S5), f32)] * 5 + [
            SDS((B, L, D_MODEL), f32), SDS((D_S5, D_MODEL), f32), SDS((D_GDN, D_MODEL), f32), SDS((B, 1, D_MODEL), f32),
            SDS((1, D_MODEL), f32), SDS((1, D_MODEL), f32), SDS((1, D_S5), f32), SDS((D_S5, D_S5), f32), SDS((1, D_S5), f32),
            SDS((1, GDN_HEAD), f32)],
        compiler_params=_cparams(2),
    )(u, y0, y1, z_s5, o0, o1, z_gdn, x, tgt, gate, lng, lnb, ws, wg, dsk, wglu, bglu, nw)


def _adamw_math(w, g, m, v):
    nm = ADAM_B1 * m + (1.0 - ADAM_B1) * g
    nv = ADAM_B2 * v + (1.0 - ADAM_B2) * jnp.square(g)
    m_hat = nm / (1.0 - ADAM_B1 ** ADAM_STEP)
    v_hat = nv / (1.0 - ADAM_B2 ** ADAM_STEP)
    return -ADAM_LR * (m_hat / (jnp.sqrt(v_hat) + ADAM_EPS) + ADAM_WD * w), nm, nv


def _row_tile(rows, cap=512):
    for t in range(min(cap, rows), 15, -1):
        if rows % t == 0 and t % 16 == 0:
            return t
    return rows


def adamw_3d(w, g, m, v, *, lead=False, name):
    R, C = (w.shape[0], w.shape[2]) if lead else w.shape[1:]
    if lead:
        T = next(t for t in range(min(256, R), 0, -1) if R % t == 0)
        spec = pl.BlockSpec((T, 1, C), lambda i: (i, 0, 0))
    else:
        T = _row_tile(R)
        spec = pl.BlockSpec((None, T, C), lambda i: (0, i, 0))

    def body(w_ref, g_ref, m_ref, v_ref, d_ref, nm_ref, nv_ref):
        d_ref[...], nm_ref[...], nv_ref[...] = _adamw_math(w_ref[...], g_ref[...], m_ref[...], v_ref[...])

    return pl.pallas_call(body, name=name, grid=(R // T,), in_specs=[spec] * 4, out_specs=[spec] * 3,
                          out_shape=[SDS(w.shape, f32)] * 3, compiler_params=_cparams(1))(w, g, m, v)


def adamw_small(ws, gs, ms, vs):
    n = len(ws)

    def body(*refs):
        outs = refs[4 * n:]
        for i in range(n):
            d, nm, nv = _adamw_math(refs[i][...], refs[n + i][...], refs[2 * n + i][...], refs[3 * n + i][...])
            outs[i][...], outs[n + i][...], outs[2 * n + i][...] = d, nm, nv

    res = pl.pallas_call(body, name="adamw_small", out_shape=[SDS(w.shape, f32) for w in ws] * 3,
                         compiler_params=pltpu.CompilerParams(vmem_limit_bytes=VMEM_LIMIT))(*ws, *gs, *ms, *vs)
    return res[:n], res[n:2 * n], res[2 * n:]


def sum_cores(own, got, *, name):
    A, H, C = own.shape
    T = _row_tile(H)
    spec = pl.BlockSpec((None, T, C), lambda a, i: (a, i, 0))

    def body(a_ref, b_ref, q32_ref, q16_ref):
        q = a_ref[...] + b_ref[...]
        q32_ref[...] = q
        q16_ref[...] = q.astype(bf16)

    return pl.pallas_call(body, name=name, grid=(A, H // T), in_specs=[spec, spec], out_specs=[spec, spec],
                          out_shape=[SDS((A, H, C), f32), SDS((A, H, C), bf16)], compiler_params=_cparams(2))(own, got)


def sum_cores_small(owns, gots):
    n = len(owns)

    def body(*refs):
        for i in range(n):
            q = refs[i][...] + refs[n + i][...]
            refs[2 * n + i][...] = q
            refs[3 * n + i][...] = q.astype(bf16)

    res = pl.pallas_call(body, name="sum_cores_small",
                         out_shape=[SDS(o.shape, f32) for o in owns] + [SDS(o.shape, bf16) for o in owns],
                         compiler_params=pltpu.CompilerParams(vmem_limit_bytes=VMEM_LIMIT))(*owns, *gots)
    return res[:n], res[n:]


def sum_chips_small(mines, recs, cpos, slots):
    n = len(mines)
    lead = [1 if s is None else 4 for s in slots]

    def body(s_ref, *refs):
        for i in range(n):
            m_ref, r_ref, f_ref = refs[i], refs[n + i], refs[2 * n + i]
            H = m_ref.shape[0]
            val = ((m_ref[...] + r_ref[0].astype(f32)) + r_ref[1].astype(f32)) + r_ref[2].astype(f32)
            f_ref[s_ref[1 + i], pl.ds(pl.multiple_of(s_ref[0] * H, 8), H), :] = val

    whole = lambda shape: pl.BlockSpec(shape, lambda g, s_ref: (0,) * len(shape))
    out_shapes = [(ld, 2 * m.shape[0], m.shape[1]) for ld, m in zip(lead, mines)]
    grid_spec = pltpu.PrefetchScalarGridSpec(
        num_scalar_prefetch=1, grid=(1,),
        in_specs=[whole(m.shape) for m in mines] + [whole(r.shape) for r in recs],
        out_specs=[whole(sh) for sh in out_shapes])
    scalars = jnp.stack([cpos] + [jnp.zeros_like(cpos) if s is None else s for s in slots]).astype(jnp.int32)
    return pl.pallas_call(body, name="sum_chips_small", grid_spec=grid_spec,
                          out_shape=[SDS(sh, f32) for sh in out_shapes],
                          compiler_params=pltpu.CompilerParams(vmem_limit_bytes=VMEM_LIMIT))(scalars, *mines, *recs)


def sum_chips(mine, rec, cpos, full, *, slot=None, name):
    H, C = mine.shape
    T = _row_tile(H)
    nt = H // T
    by_rows = _by_rows(full)
    out_idx = lambda i, s_ref: ((s_ref[1], s_ref[0] * nt + i, 0) if by_rows else (s_ref[1], i, s_ref[0]))

    def body(s_ref, m_ref, r_ref, f_ref):
        f_ref[...] = ((m_ref[...] + r_ref[0].astype(f32)) + r_ref[1].astype(f32)) + r_ref[2].astype(f32)

    grid_spec = pltpu.PrefetchScalarGridSpec(
        num_scalar_prefetch=1, grid=(nt,),
        in_specs=[pl.BlockSpec((T, C), lambda i, s_ref: (i, 0)), pl.BlockSpec((3, T, C), lambda i, s_ref: (0, i, 0))],
        out_specs=pl.BlockSpec((None, T, C), out_idx))
    scalars = jnp.stack([cpos, jnp.zeros_like(cpos) if slot is None else slot]).astype(jnp.int32)
    return pl.pallas_call(body, name=name, grid_spec=grid_spec,
                          out_shape=SDS((1 if slot is None else 4,) + tuple(full), f32),
                          compiler_params=_cparams(1))(scalars, mine, rec)


CHIP_FLIPS = ((1, 0), (0, 1), (1, 1))


def _pos():
    return lax.axis_index("x"), lax.axis_index("y"), lax.axis_index("c")


def _comm_call(body, srcs, out_sds, n_remote, n_local, name):
    any_spec = pl.BlockSpec(memory_space=pl.ANY)
    return pl.pallas_call(
        body, name=name, in_specs=[any_spec] * len(srcs), out_specs=[any_spec] * len(out_sds), out_shape=out_sds,
        scratch_shapes=[pltpu.SemaphoreType.DMA((n_remote,)), pltpu.SemaphoreType.DMA((n_remote,)),
                        pltpu.SemaphoreType.DMA((max(n_local, 1),))],
        compiler_params=pltpu.CompilerParams(has_side_effects=True),
    )(*srcs)


def _remote(src, dst, send_sems, recv_sems, k, target):
    return pltpu.make_async_remote_copy(src, dst, send_sems.at[k], recv_sems.at[k], device_id=target,
                                        device_id_type=MESH)


def _by_rows(shape):
    return shape[0] % 16 == 0


def _half_shape(shape):
    return (shape[0] // 2, shape[1]) if _by_rows(shape) else (shape[0], shape[1] // 2)


def _half_of(ref, lead, c, shape):
    if _by_rows(shape):
        half = shape[0] // 2
        return ref.at[(*lead, pl.ds(pl.multiple_of(c * half, 8), half))]
    half = shape[1] // 2
    return ref.at[(*lead, slice(None), pl.ds(pl.multiple_of(c * half, LANES), half))]


def gather_shards(shards):
    nt = len(shards)

    def body(*refs):
        srcs, outs = refs[:nt], refs[nt:2 * nt]
        send_sems, recv_sems, _ = refs[2 * nt:]
        x, y, c = _pos()
        j = 2 * x + y
        sib = (x, y, 1 - c)
        own = [_remote(srcs[t], outs[t].at[j], send_sems, recv_sems, 7 * t + 6, sib) for t in range(nt)]
        first, passed = [], []
        for k, (fx, fy) in enumerate(CHIP_FLIPS):
            tx, ty = x ^ fx, y ^ fy
            jk = 2 * tx + ty
            for t in range(nt):
                sh = srcs[t].shape
                first.append(_remote(_half_of(srcs[t], (), c, sh), _half_of(outs[t], (j,), c, sh), send_sems, recv_sems,
                                     7 * t + k, (tx, ty, c)))
                landed = _half_of(outs[t], (jk,), c, sh)
                passed.append(_remote(landed, landed, send_sems, recv_sems, 7 * t + 3 + k, sib))
        for cp in first + own:
            cp.start()
        for a, b in zip(first, passed):
            a.wait_recv()
            b.start()
        for cp in passed + own:
            cp.wait_recv()
        for cp in first + passed + own:
            cp.wait_send()

    return _comm_call(body, shards, [SDS((4,) + s.shape, s.dtype) for s in shards], 7 * nt, 0, "gather_shards")


def swap_halves(ps):
    nt = len(ps)

    def body(*refs):
        srcs, outs = refs[:nt], refs[nt:2 * nt]
        send_sems, recv_sems, _ = refs[2 * nt:]
        x, y, c = _pos()
        cps = [_remote(_half_of(srcs[t], (a,), 1 - c, srcs[t].shape[1:]), outs[t].at[a], send_sems, recv_sems, 4 * t + a,
                       (x, y, 1 - c)) for t in range(nt) for a in range(4)]
        for cp in cps:
            cp.start()
        for cp in cps:
            cp.wait()

    return _comm_call(body, ps, [SDS((4,) + _half_shape(p.shape[1:]), p.dtype) for p in ps], 4 * nt, 0, "swap_halves")


def scatter_to_chips(qs):
    nt = len(qs)

    def body(*refs):
        srcs, outs = refs[:nt], refs[nt:2 * nt]
        send_sems, recv_sems, _ = refs[2 * nt:]
        x, y, c = _pos()
        cps = []
        for k, (fx, fy) in enumerate(CHIP_FLIPS):
            tx, ty = x ^ fx, y ^ fy
            for t in range(nt):
                cps.append(_remote(srcs[t].at[2 * tx + ty], outs[t].at[k], send_sems, recv_sems, 3 * t + k, (tx, ty, c)))
        for cp in cps:
            cp.start()
        for cp in cps:
            cp.wait()

    return _comm_call(body, qs, [SDS((3,) + q.shape[1:], q.dtype) for q in qs], 3 * nt, 0, "scatter_to_chips")


DEV_FLIPS = tuple((fx, fy, fc) for fx in (0, 1) for fy in (0, 1) for fc in (0, 1))[1:]


def join_halves(fs, small):
    nt = len(fs)

    def body(*refs):
        outs = refs[nt + 1:2 * nt + 1]
        sm = refs[2 * nt + 1]
        send_sems, recv_sems, _ = refs[2 * nt + 2:]
        x, y, c = _pos()
        cps = []
        for t in range(nt):
            mine = _half_of(outs[t], (0,), c, outs[t].shape[1:])
            cps.append(_remote(mine, mine, send_sems, recv_sems, t, (x, y, 1 - c)))
        mine = _half_of(sm, (2 * x + y,), c, sm.shape[1:])
        for k, (fx, fy, fc) in enumerate(DEV_FLIPS):
            cps.append(_remote(mine, mine, send_sems, recv_sems, nt + k, (x ^ fx, y ^ fy, c ^ fc)))
        for cp in cps:
            cp.start()
        for cp in cps:
            cp.wait()

    any_spec = pl.BlockSpec(memory_space=pl.ANY)
    n_sem = nt + len(DEV_FLIPS)
    res = pl.pallas_call(
        body, name="join_halves", in_specs=[any_spec] * (nt + 1), out_specs=[any_spec] * (nt + 1),
        out_shape=[SDS(f.shape, f.dtype) for f in fs] + [SDS(small.shape, small.dtype)],
        input_output_aliases={t: t for t in range(nt + 1)},
        scratch_shapes=[pltpu.SemaphoreType.DMA((n_sem,)), pltpu.SemaphoreType.DMA((n_sem,)), pltpu.SemaphoreType.DMA((1,))],
        compiler_params=pltpu.CompilerParams(has_side_effects=True),
    )(*fs, small)
    return res[:nt], res[nt]


def gather_devices(block, *, name):
    def body(src, out, send_sems, recv_sems, loc_sems):
        x, y, c = _pos()
        me = 4 * x + 2 * y + c
        mine = pltpu.make_async_copy(src, out.at[me], loc_sems.at[0])
        mine.start()
        cps = [_remote(src, out.at[me], send_sems, recv_sems, k, (x ^ fx, y ^ fy, c ^ fc))
               for k, (fx, fy, fc) in enumerate(DEV_FLIPS)]
        for cp in cps:
            cp.start()
        for cp in cps:
            cp.wait()
        mine.wait()

    return _comm_call(body, [block], [SDS((N_DEV,) + block.shape, block.dtype)], 7, 1, name)[0]


def exchange_devices(blocks, *, name):
    def body(src, out, send_sems, recv_sems, loc_sems):
        x, y, c = _pos()
        me = 4 * x + 2 * y + c
        mine = pltpu.make_async_copy(src.at[me], out.at[me], loc_sems.at[0])
        mine.start()
        cps = []
        for k, (fx, fy, fc) in enumerate(DEV_FLIPS):
            tx, ty, tc = x ^ fx, y ^ fy, c ^ fc
            cps.append(_remote(src.at[4 * tx + 2 * ty + tc], out.at[me], send_sems, recv_sems, k, (tx, ty, tc)))
        for cp in cps:
            cp.start()
        for cp in cps:
            cp.wait()
        mine.wait()

    return _comm_call(body, [blocks], [SDS(blocks.shape, blocks.dtype)], 7, 1, name)[0]


SMALL_SHAPES = ((1, 2, 32, 64), (1, 2, 32, 64), (1, 2, 32), (1, 2, 32, 16, 64),
                (1, 2, 32, 16, 64), (1, 2, 32, 16, 64), (1, 2, 32, 16, 64), (1, D_S5), (1, D_S5), (1, 2, 4), (1, 2, 4),
                (1, GDN_HEAD), (1, D_MODEL), (1, D_MODEL), (LANES,))
SMALL_SWAPPED = (3, 4)


def _size(shape):
    return functools.reduce(lambda p, q: p * q, shape)


SMALL_ROWS = tuple(-(-_size(s) // (8 * LANES)) * 8 for s in SMALL_SHAPES)
SMALL_TOTAL = 2240
SMALL_QUARTER = SMALL_TOTAL // 4


def _rows(a):
    flat = a.reshape(-1)
    pad = (-flat.shape[0]) % (8 * LANES)
    if pad:
        flat = jnp.concatenate([flat, jnp.zeros((pad,), flat.dtype)])
    return flat.reshape(-1, LANES)


def _pack_small(parts):
    rows = [_rows(p) for p in parts]
    rows.append(jnp.zeros((SMALL_TOTAL - sum(SMALL_ROWS), LANES), f32))
    return jnp.concatenate(rows, axis=0)


def _unpack_small(buf):
    out, r = [], 0
    for s, n in zip(SMALL_SHAPES, SMALL_ROWS):
        out.append(buf[r:r + n].reshape(-1)[:_size(s)].reshape(s))
        r += n
    return out


def _as_2d(a):
    return a.reshape(1, -1) if a.ndim == 1 else a.reshape(-1, a.shape[-1])


S5_BG = S5_GROUPS // S5_BLOCKS


def _block_diag_in(bb):
    lead = bb.shape[:-2]
    eye = jnp.eye(S5_BG, dtype=bb.dtype)
    b4 = bb.reshape(lead + (S5_BLOCKS, S5_BG, S5_GROUP, S5_STATE))
    return jnp.einsum('...jgcp,gh->...jgchp', b4, eye).reshape(lead + (S5_BLOCKS, S5_BC, S5_BS))


def _block_diag_in_t(d):
    lead = d.shape[:-3]
    d6 = d.reshape(lead + (S5_BLOCKS, S5_BG, S5_GROUP, S5_BG, S5_STATE))
    return jnp.einsum('...jgcgp->...jgcp', d6).reshape(lead + (S5_GROUPS, S5_GROUP * S5_STATE))


def _block_diag_out(cm):
    lead = cm.shape[:-3]
    eye = jnp.eye(S5_BG, dtype=cm.dtype)
    c4 = cm.reshape(lead + (S5_BLOCKS, S5_BG, S5_GROUP, S5_STATE))
    return jnp.einsum('...jgcp,gh->...jhpgc', c4, eye).reshape(lead + (S5_BLOCKS, S5_BS, S5_BC))


def _block_diag_out_t(d):
    lead = d.shape[:-3]
    d6 = d.reshape(lead + (S5_BLOCKS, S5_BG, S5_STATE, S5_BG, S5_GROUP))
    return jnp.einsum('...jgpgc->...jgcp', d6).reshape(lead + (S5_GROUPS, S5_GROUP, S5_STATE))


def _to_chunk_rows(a):
    B, L, W = a.shape
    return a.reshape(B, L // CHUNK, CHUNK, W).transpose(0, 1, 3, 2)


def _from_chunk_rows(a):
    B, nc, W, _ = a.shape
    return a.transpose(0, 1, 3, 2).reshape(B, nc * CHUNK, W)


def local_step(x, ctx, tgt, m, w_in, lam_re, lam_im, log_dt, b_re, b_im, c_re, c_im, s5_d,
               w_glu, b_glu, conv16, a_log, dt_bias, norm_w, w_out, ln_g, ln_b):
    B, L, _ = x.shape
    zeros_state = jnp.zeros((B, GDN_HEADS, GDN_HEAD, GDN_HEAD), f32)

    shift, scale, gate = m[:B, :D_MODEL], m[:B, D_MODEL:2 * D_MODEL], m[:B, 2 * D_MODEL:]
    mod = jnp.stack([scale, shift], axis=1)
    mod_c = jnp.broadcast_to(jnp.stack([m[B, D_MODEL:2 * D_MODEL], m[B, :D_MODEL]], axis=0)[None], (B, 2, D_MODEL))

    u, z_s5, qkv, z_gdn, ba = in_proj_fwd(x, mod, w_in, name="in_proj_fwd")
    uc, _, qkvc, _, bac = in_proj_fwd(ctx, mod_c, w_in, name="in_proj_fwd_ctx")

    ng = N_DIR * S5_GROUPS
    zoh_in = (lam_re.reshape(ng, S5_STATE), lam_im.reshape(ng, S5_STATE), log_dt.reshape(ng, 1),
              b_re.reshape(ng, S5_GROUP * S5_STATE), b_im.reshape(ng, S5_GROUP * S5_STATE))
    expand = (jnp.arange(S5_GROUP * S5_STATE)[None, :] % S5_STATE == jnp.arange(S5_STATE)[:, None]).astype(f32)
    ar, ai, bbr, bbi = s5_zoh_fwd(*zoh_in, expand)
    b_blocks = _block_diag_in(jnp.stack([bbr, bbi]).astype(bf16).reshape(2, N_DIR, S5_GROUPS, S5_GROUP * S5_STATE))
    c_blocks = _block_diag_out(jnp.stack([c_re, -c_im]).astype(bf16).reshape(2, N_DIR, S5_GROUPS, S5_GROUP, S5_STATE))
    a_rows = jnp.stack([ar, ai]).reshape(2, N_DIR, S5_HALF)
    s5w, ys, hins, hins_c, hss, hss_c = [], [], [], [], [], []
    for d in range(N_DIR):
        wd = (b_blocks[0, d], b_blocks[1, d], c_blocks[0, d], c_blocks[1, d], a_rows[:, d])
        s5w.append(wd)
        hs_c, hin_c, hend_c = s5_scan_fwd(uc, *wd, jnp.zeros((B, 2, S5_HALF), f32), d=d, need_y=False,
                                          name=f"s5_fwd_ctx{d}")
        y_d, hs_d, hin, _ = s5_scan_fwd(u, *wd, hend_c, d=d, need_y=True, name=f"s5_fwd{d}")
        hss.append(hs_d)
        hss_c.append(hs_c)
        ys.append(y_d)
        hins.append(hin)
        hins_c.append(hin_c)
    glu_w = (s5_d.reshape(1, D_S5), w_glu, b_glu.reshape(1, D_S5))

    act, pre = conv_fwd(qkv, conv16, is_ctx=False, name="conv_fwd")
    act_c, pre_c = conv_fwd(qkvc, conv16, is_ctx=True, name="conv_fwd_ctx")
    pad8 = jnp.zeros((1, 8), f32)
    alog16 = jnp.concatenate([pad8, a_log.reshape(1, 8)], axis=1)
    dtb16 = jnp.concatenate([pad8, dt_bias.reshape(1, 8)], axis=1)
    bg = gates_fwd(ba, alog16, dtb16, name="gates_fwd")
    bg_c = gates_fwd(bac, alog16, dtb16, name="gates_fwd_ctx")
    bgr, bgr_c = _to_chunk_rows(bg), _to_chunk_rows(bg_c)
    cks_c, ns_c, s_c = gdn_fwd(act_c, bg_c, bgr_c, (zeros_state, zeros_state), need_o=False, name="gdn_fwd_ctx")
    os_, cks, ns, _ = gdn_fwd(act, bg, bgr, s_c, need_o=True, name="gdn_fwd")
    nw = norm_w.reshape(1, GDN_HEAD)

    (loss8, du_skip, dy, dz_s5, do, dz_gdn, gx_res, dws, dwg, dgate, dlng, dlnb, d_s5_d, d_w_glu, d_b_glu,
     d_norm_w) = tail_fwd_bwd(u, ys[0], ys[1], z_s5, os_[0], os_[1], z_gdn, x, tgt, gate[:, None, :],
                              ln_g.reshape(1, D_MODEL), ln_b.reshape(1, D_MODEL), w_out[:D_S5], w_out[D_S5:], *glu_w, nw)
    loss = jnp.sum(loss8[:, 0, 0])
    d_w_out = jnp.concatenate([dws, dwg], axis=0)

    dacts, dbgs, dbgrs, ds0s = gdn_bwd(act, bg, bgr, cks, ns, do, (zeros_state, zeros_state), name="gdn_bwd")
    dacts_c, dbgs_c, dbgrs_c, _ = gdn_bwd(act_c, bg_c, bgr_c, cks_c, ns_c, None, ds0s, name="gdn_bwd_ctx")
    dbg = dbgs[0] + dbgs[1] + _from_chunk_rows(dbgrs[0] + dbgrs[1])
    dbg_c = dbgs_c[0] + dbgs_c[1] + _from_chunk_rows(dbgrs_c[0] + dbgrs_c[1])
    dba, dal, ddt = gates_bwd(ba, alog16, dtb16, dbg, name="gates_bwd")
    dbac, dal_c, ddt_c = gates_bwd(bac, alog16, dtb16, dbg_c, name="gates_bwd_ctx")
    d_a_log = (dal + dal_c)[:, 8:].reshape(1, N_DIR, GDN_HEADS)
    d_dt_bias = (ddt + ddt_c)[:, 8:].reshape(1, N_DIR, GDN_HEADS)
    dqkv, dcw = conv_bwd(qkv, pre, conv16, dacts[0], dacts[1], is_ctx=False, name="conv_bwd")
    dqkvc, dcw_c = conv_bwd(qkvc, pre_c, conv16, dacts_c[0], dacts_c[1], is_ctx=True, name="conv_bwd_ctx")
    d_conv16 = jnp.sum(dcw, axis=0) + jnp.sum(dcw_c, axis=0)

    dus, ducs = [du_skip], []
    das, dbs, dcs = [], [], []
    for d in range(N_DIR):
        du_d, dbre1, dbim1, dct1, dcb1, da1, dh0 = s5_scan_bwd(u, dy, hss[d], *s5w[d], hins[d],
                                                                jnp.zeros((B, 2, S5_HALF), f32), d=d, name=f"s5_bwd{d}")
        duc_d, dbre2, dbim2, _, _, da2, _ = s5_scan_bwd(uc, None, hss_c[d], *s5w[d], hins_c[d], dh0, d=d,
                                                        name=f"s5_bwd_ctx{d}")
        dus.append(du_d)
        ducs.append(duc_d)
        das.append(da1 + da2)
        dbs.append(jnp.stack([dbre1 + dbre2, dbim1 + dbim2]))
        dcs.append(jnp.stack([dct1, dcb1]))
    ng_shape = (N_DIR * S5_GROUPS, -1)
    da = jnp.stack(das, axis=1)
    db = _block_diag_in_t(jnp.stack(dbs, axis=1))
    dc = _block_diag_out_t(jnp.stack(dcs, axis=1))
    dlr, dli, dldt, dbre, dbim = s5_zoh_bwd(*zoh_in, expand, da[0].reshape(ng_shape), da[1].reshape(ng_shape),
                                            db[0].reshape(ng_shape), db[1].reshape(ng_shape))
    d_s5 = (dlr, dli, dldt, dbre, dbim, dc[0], -dc[1])

    zc = jnp.zeros_like(uc)
    dw_c, dmod_c = in_proj_bwd(ctx, mod_c, (tuple(ducs), zc, dqkvc, zc, dbac), w_in, None, None,
                               name="in_proj_bwd_ctx")
    d_w_in, dmod, grad_x = in_proj_bwd(x, mod, (tuple(dus), dz_s5, dqkv, dz_gdn, dba), w_in, gx_res, dw_c,
                                       name="in_proj_bwd")
    dmod_c = jnp.sum(dmod_c, axis=0)

    dm_rows = jnp.concatenate([dmod[:, 1], dmod[:, 0], dgate[:, 0]], axis=1)
    dm_ctx = jnp.concatenate([dmod_c[1], dmod_c[0], jnp.zeros((D_MODEL,), f32)])[None]
    dm = jnp.concatenate([dm_rows, dm_ctx], axis=0)
    small = (*d_s5, d_s5_d, d_b_glu, d_a_log, d_dt_bias, d_norm_w, dlng, dlnb)
    small = tuple(g.reshape(s) for g, s in zip(small, SMALL_SHAPES))
    return loss, grad_x, (d_w_in, d_w_out, d_w_glu, d_conv16), small, dm


SHARDED = (1, 3, 18, 12, 14)
UNSHARDED = tuple(i for i in range(21) if i not in SHARDED)
SMALL = tuple(i for i in UNSHARDED if i not in (0, 2))
W_IN_SHARD = 772


def _conv_rows(w):
    return jnp.concatenate([w.reshape(9, w.shape[-1]), jnp.zeros((CONV_ROWS - 9, w.shape[-1]), f32)], axis=0)


def kernel(x, c, ctx, c_ctx, w_ada, b_ada, w_in, s5_lambda_re, s5_lambda_im, s5_log_dt, s5_b_re, s5_b_im, s5_c_re, s5_c_im, s5_d, w_glu, b_glu, conv_w, gdn_a_log, gdn_dt_bias, gdn_norm_w, w_out, ln_g, ln_b, loss_target, m_c_ctx, m_w_ada, m_b_ada, m_w_in, m_s5_lambda_re, m_s5_lambda_im, m_s5_log_dt, m_s5_b_re, m_s5_b_im, m_s5_c_re, m_s5_c_im, m_s5_d, m_w_glu, m_b_glu, m_conv_w, m_gdn_a_log, m_gdn_dt_bias, m_gdn_norm_w, m_w_out, m_ln_g, m_ln_b, v_c_ctx, v_w_ada, v_b_ada, v_w_in, v_s5_lambda_re, v_s5_lambda_im, v_s5_log_dt, v_s5_b_re, v_s5_b_im, v_s5_c_re, v_s5_c_im, v_s5_d, v_w_glu, v_b_glu, v_conv_w, v_gdn_a_log, v_gdn_dt_bias, v_gdn_norm_w, v_w_out, v_ln_g, v_ln_b):
    weights = [c_ctx, w_ada, b_ada, w_in, s5_lambda_re, s5_lambda_im, s5_log_dt, s5_b_re, s5_b_im, s5_c_re, s5_c_im,
               s5_d, w_glu, b_glu, conv_w, gdn_a_log, gdn_dt_bias, gdn_norm_w, w_out, ln_g, ln_b]
    ms = [m_c_ctx, m_w_ada, m_b_ada, m_w_in, m_s5_lambda_re, m_s5_lambda_im, m_s5_log_dt, m_s5_b_re, m_s5_b_im,
          m_s5_c_re, m_s5_c_im, m_s5_d, m_w_glu, m_b_glu, m_conv_w, m_gdn_a_log, m_gdn_dt_bias, m_gdn_norm_w, m_w_out,
          m_ln_g, m_ln_b]
    vs = [v_c_ctx, v_w_ada, v_b_ada, v_w_in, v_s5_lambda_re, v_s5_lambda_im, v_s5_log_dt, v_s5_b_re, v_s5_b_im,
          v_s5_c_re, v_s5_c_im, v_s5_d, v_w_glu, v_b_glu, v_conv_w, v_gdn_a_log, v_gdn_dt_bias, v_gdn_norm_w, v_w_out,
          v_ln_g, v_ln_b]
    cpos = lax.axis_index("c")
    jchip = 2 * lax.axis_index("x") + lax.axis_index("y")

    c_all = gather_devices(c, name="gather_c")
    cc = jnp.concatenate([c_all, jnp.broadcast_to(c_ctx[None, None, :], (N_DEV, 1, D_MODEL)),
                          jnp.zeros((N_DEV, 5, D_MODEL), f32)], axis=1)
    w_ada16 = w_ada[0].astype(bf16)
    b_cols = lax.dynamic_slice_in_dim(b_ada, jchip * ADA_SHARD, ADA_SHARD, axis=1)
    m_mine = exchange_devices(ada_fwd(cc, w_ada16, b_cols), name="exchange_m")
    m_rows = jnp.concatenate([m_mine[2 * j, :3] for j in range(4)], axis=1)

    conv_shard = _conv_rows(conv_w)
    g_in, g_out, g_glu, g_conv = gather_shards(
        [jnp.transpose(w_in[0]).astype(bf16), w_out[0].astype(bf16), w_glu[0].astype(bf16), conv_shard])
    w_in_t = g_in.reshape(P_IN, D_MODEL)
    conv16 = g_conv.transpose(1, 0, 2).reshape(CONV_ROWS, 3 * D_GDN)

    swap = lambda a: jnp.swapaxes(a, 3, 4)
    loss, grad_x, big, small, dm_rows = local_step(
        x, ctx, loss_target, m_rows, w_in_t, s5_lambda_re, s5_lambda_im, s5_log_dt, swap(s5_b_re), swap(s5_b_im),
        s5_c_re, s5_c_im, s5_d, g_glu.reshape(D_S5, D_S5), b_glu, conv16, gdn_a_log, gdn_dt_bias, gdn_norm_w,
        g_out.reshape(D_MODEL, D_MODEL), ln_g, ln_b)
    me = 2 * jchip + cpos
    loss_hi = loss.astype(bf16).astype(f32)
    loss_row = jnp.zeros((LANES,), f32).at[me].set(loss_hi).at[N_DEV + me].set(loss - loss_hi)

    dm8 = jnp.concatenate([dm_rows, jnp.zeros((5, 3 * D_MODEL), f32)], axis=0)
    dm_by_chip = dm8.reshape(8, 4, ADA_SHARD).transpose(1, 0, 2)
    dm_cols = exchange_devices(jnp.repeat(dm_by_chip, 2, axis=0), name="exchange_dm")
    g_w_ada, pb = ada_bwd(cc, w_ada16, dm_cols)
    pb_all = gather_devices(pb, name="gather_p")
    g_c_ctx = c_ctx_bwd(pb_all, c_ctx[None, :])[0]
    g_b_ada = jnp.concatenate([pb_all[2 * j, 1:2, :ADA_SHARD] for j in range(4)], axis=1)

    d_w_in, d_w_out, d_w_glu, d_conv16 = big
    slabs = [d_w_in.reshape(4, W_IN_SHARD, D_MODEL),
             d_w_out.reshape(4, D_MODEL // 4, D_MODEL),
             d_w_glu.reshape(4, D_S5 // 4, D_S5),
             d_conv16.reshape(CONV_ROWS, 4, 3 * D_GDN // 4).transpose(1, 0, 2),
             _pack_small(small + (loss_row,)).reshape(4, SMALL_QUARTER, LANES)]
    got = swap_halves(slabs)
    owns = []
    for s in slabs:
        if _by_rows(s.shape[1:]):
            owns.append(lax.dynamic_index_in_dim(s.reshape(4, 2, s.shape[1] // 2, s.shape[2]), cpos, axis=1, keepdims=False))
        else:
            owns.append(lax.dynamic_slice_in_dim(s, cpos * (s.shape[2] // 2), s.shape[2] // 2, axis=2))
    big32, big16 = sum_cores(owns[0], got[0], name="sum_cores0")
    rest32, rest16 = sum_cores_small(owns[1:], got[1:])
    q32, q16 = [big32, *rest32], [big16, *rest16]
    rec = scatter_to_chips(q16)
    mines = [lax.dynamic_index_in_dim(q, jchip, axis=0, keepdims=False) for q in q32]
    fs = [sum_chips(mines[0], rec[0], cpos, slabs[0].shape[1:], name="sum_chips0")]
    fs += sum_chips_small(mines[1:], rec[1:], cpos, [None, None, None, jchip])
    red, small_all = join_halves(fs[:4], fs[4])
    g_small = _unpack_small(small_all.reshape(SMALL_TOTAL, LANES))
    loss = jnp.sum(g_small[-1][:2 * N_DEV])
    g_small = g_small[:-1]
    g_shard = {1: g_w_ada, 3: red[0], 18: red[1], 12: red[2], 14: red[3]}

    grads, deltas, new_m, new_v = [None] * 21, [None] * 21, [None] * 21, [None] * 21
    for t, i in enumerate(SHARDED):
        conv, win = i == 14, i == 3
        prep = (lambda a: _conv_rows(a)[None]) if conv else ((lambda a: jnp.transpose(a, (2, 0, 1))) if win else (lambda a: a))
        g = jnp.transpose(g_shard[i], (1, 0, 2)) if win else g_shard[i]
        d, nm, nv = adamw_3d(prep(weights[i]), g, prep(ms[i]), prep(vs[i]), lead=win, name=f"adamw{t}")
        for lst, val in ((grads, g), (deltas, d), (new_m, nm), (new_v, nv)):
            lst[i] = (val[0, :9].reshape(weights[i].shape) if conv else (jnp.transpose(val, (1, 2, 0)) if win else val))
    g_un = {0: g_c_ctx, 2: g_b_ada, **{i: g_small[n] for n, i in enumerate(SMALL)}}
    swapped = [SMALL[n] for n in SMALL_SWAPPED]
    small_in = lambda lst: [_as_2d(swap(lst[i]) if i in swapped else lst[i]) for i in UNSHARDED]
    sm = adamw_small(small_in(weights), [_as_2d(g_un[i]) for i in UNSHARDED], small_in(ms), small_in(vs))
    for n, i in enumerate(UNSHARDED):
        back = ((lambda a: swap(a.reshape(swap(weights[i]).shape))) if i in swapped
                else (lambda a: a.reshape(weights[i].shape)))
        grads[i] = back(g_un[i])
        for lst, res in ((deltas, sm[0]), (new_m, sm[1]), (new_v, sm[2])):
            lst[i] = back(res[n])
    return (loss, grad_x, *grads, *deltas, *new_m, *new_v)
```

```python
import functools

import jax
import jax.numpy as jnp
from jax import lax
from jax.experimental import pallas as pl
from jax.experimental.pallas import tpu as pltpu

f32 = jnp.float32
bf16 = jnp.bfloat16
SDS = jax.ShapeDtypeStruct

D_MODEL = 1024
D_S5 = 512
S5_GROUP = 16
S5_GROUPS = 32
S5_STATE = 64
S5_HALF = S5_GROUPS * S5_STATE
D_GDN = 512
GDN_HEAD = 128
GDN_HEADS = 4
CHUNK = 64
GRID_W = 64
N_DIR = 2
P_IN = 3088
DEEPNORM_ALPHA = 2.0 ** 0.25
LN_EPS = 1e-5
NORM_EPS = 1e-6
ADAM_LR, ADAM_B1, ADAM_B2, ADAM_EPS, ADAM_WD, ADAM_STEP = 0.001, 0.9, 0.999, 1e-08, 0.01, 10

LANES = 128
VMEM_LIMIT = 56 * 1024 * 1024
TOK_TILE = 256
S5_TILE = 256
MESH = pl.DeviceIdType.MESH


def _cparams(n_grid):
    return pltpu.CompilerParams(dimension_semantics=("arbitrary",) * n_grid, vmem_limit_bytes=VMEM_LIMIT)


def _dot(a, b):
    return jnp.dot(a.astype(bf16), b.astype(bf16), preferred_element_type=f32)


def _dot_nt(a, b):
    return lax.dot_general(a.astype(bf16), b.astype(bf16), (((1,), (1,)), ((), ())), preferred_element_type=f32)


def _dot_tn(a, b):
    return lax.dot_general(a.astype(bf16), b.astype(bf16), (((0,), (0,)), ((), ())), preferred_element_type=f32)


def _dot_hi(a, b):
    return jnp.dot(a, b, precision=lax.Precision.HIGHEST, preferred_element_type=f32)


@jax.custom_vjp
def _mm(a, b):
    return _dot(a, b)


@jax.custom_vjp
def _mm_nt(a, b):
    return _dot_nt(a, b)


@jax.custom_vjp
def _mm_tn(a, b):
    return _dot_tn(a, b)


_mm.defvjp(lambda a, b: (_dot(a, b), (a, b)), lambda r, g: (_mm_nt(g, r[1]), _mm_tn(r[0], g)))
_mm_nt.defvjp(lambda a, b: (_dot_nt(a, b), (a, b)), lambda r, g: (_mm(g, r[1]), _mm_tn(g, r[0])))
_mm_tn.defvjp(lambda a, b: (_dot_tn(a, b), (a, b)), lambda r, g: (_mm_nt(r[1], g), _mm(r[0], g)))


def _silu(x):
    return x * jax.nn.sigmoid(x)


def _gelu(x):
    return 0.5 * x * (1.0 + lax.erf(x * (2.0 ** -0.5)))


def _resident(shape):
    nd = len(shape)
    return pl.BlockSpec(shape, lambda *_: (0,) * nd, pipeline_mode=pl.Buffered(1))


def _tok(tile, width, nt=None, rev=False):
    if rev:
        return pl.BlockSpec((None, tile, width), lambda b, n: (b, nt - 1 - n, 0))
    return pl.BlockSpec((None, tile, width), lambda b, n: (b, n, 0))


def _per_batch(rows, width):
    return pl.BlockSpec((None, rows, width), lambda b, n: (b, 0, 0))


def _first_step():
    return jnp.logical_and(pl.program_id(0) == 0, pl.program_id(1) == 0)


ADA_SHARD = 3 * D_MODEL // 4
N_DEV = 8


def ada_fwd(cc, w, b):
    def body(cc_ref, w_ref, b_ref, m_ref):
        for k in range(N_DEV):
            m_ref[k] = _dot(_silu(cc_ref[k]), w_ref[...]) + b_ref[...]

    return pl.pallas_call(body, name="ada_fwd", out_shape=SDS((N_DEV, 8, ADA_SHARD), f32),
                          compiler_params=pltpu.CompilerParams(vmem_limit_bytes=VMEM_LIMIT))(cc, w, b)


def ada_bwd(cc, w, dmj):
    def body(cc_ref, w_ref, dmj_ref, dw_ref, pb_ref):
        dw = jnp.zeros((D_MODEL, ADA_SHARD), f32)
        p = jnp.zeros((8, D_MODEL), f32)
        db = jnp.zeros((1, ADA_SHARD), f32)
        for k in range(N_DEV):
            dw = dw + _dot_tn(_silu(cc_ref[k]), dmj_ref[k])
            p = p + _dot_nt(dmj_ref[k], w_ref[...])
            db = db + jnp.sum(dmj_ref[k], axis=0, keepdims=True)
        dw_ref[0] = dw
        pb_ref[...] = jnp.zeros_like(pb_ref)
        pb_ref[0:1, :] = p[2:3, :]
        pb_ref[1:2, 0:ADA_SHARD] = db

    return pl.pallas_call(
        body, name="ada_bwd", out_shape=[SDS((1, D_MODEL, ADA_SHARD), f32), SDS((8, D_MODEL), f32)],
        compiler_params=pltpu.CompilerParams(vmem_limit_bytes=VMEM_LIMIT))(cc, w, dmj)


def c_ctx_bwd(pb_all, c_ctx):
    def body(p_ref, c_ref, d_ref):
        ds = ((p_ref[0, 0:1, :] + p_ref[2, 0:1, :]) + p_ref[4, 0:1, :]) + p_ref[6, 0:1, :]
        _, vjp = jax.vjp(_silu, c_ref[...])
        d_ref[...] = vjp(ds)[0]

    return pl.pallas_call(body, name="c_ctx_bwd", out_shape=SDS((1, D_MODEL), f32))(pb_all, c_ctx)


N_GATE = 2 * N_DIR * GDN_HEADS
IN_WIDTHS = (D_S5, D_S5, 3 * D_GDN, D_GDN, N_GATE)
IN_OFFS = (0, 512, 1024, 2560, 3072)


def in_proj_fwd(x, mod, wt, *, name):
    B, L, _ = x.shape
    T = min(2 * TOK_TILE, L)

    def body(x_ref, mod_ref, w_ref, *o_refs):
        h = (x_ref[...] * (1.0 + mod_ref[0:1, :]) + mod_ref[1:2, :]).astype(bf16)
        for o_ref, off, wd in zip(o_refs, IN_OFFS, IN_WIDTHS):
            o_ref[...] = _dot_nt(h, w_ref[off:off + wd, :])

    return pl.pallas_call(
        body, name=name, grid=(B, L // T),
        in_specs=[_tok(T, D_MODEL), _per_batch(2, D_MODEL), _resident((P_IN, D_MODEL))],
        out_specs=[_tok(T, wd) for wd in IN_WIDTHS],
        out_shape=[SDS((B, L, wd), f32) for wd in IN_WIDTHS],
        compiler_params=_cparams(2),
    )(x, mod, wt)


def in_proj_bwd(x, mod, ds, wt, gx_res, dw_start, *, name):
    B, L, _ = x.shape
    T = min(TOK_TILE, L)
    with_dx = gx_res is not None
    with_start = dw_start is not None
    n_u = len(ds[0])

    def body(*refs):
        x_ref, mod_ref = refs[0], refs[1]
        du_refs = refs[2:2 + n_u]
        d_refs = refs[2 + n_u:6 + n_u]
        w_ref = refs[6 + n_u]
        k = 7 + n_u
        if with_dx:
            gx_ref = refs[k]
            k += 1
        if with_start:
            start_ref = refs[k]
            k += 1
        dw_ref, dmod_ref = refs[k], refs[k + 1]
        if with_dx:
            dx_ref = refs[k + 2]
        n = pl.program_id(1)

        @pl.when(_first_step())
        def _():
            dw_ref[...] = start_ref[...] if with_start else jnp.zeros_like(dw_ref)

        @pl.when(n == 0)
        def _():
            dmod_ref[...] = jnp.zeros_like(dmod_ref)

        xv = x_ref[...]
        scale1 = 1.0 + mod_ref[0:1, :]
        h = (xv * scale1 + mod_ref[1:2, :]).astype(bf16)
        du = du_refs[0][...]
        for r in du_refs[1:]:
            du = du + r[...]
        dh = jnp.zeros((T, D_MODEL), f32)
        for dv, off, wd in zip([du] + [r[...] for r in d_refs], IN_OFFS, IN_WIDTHS):
            dv = dv.astype(bf16)
            dh = dh + _dot(dv, w_ref[off:off + wd, :])
            dw_ref[off:off + wd, :] += _dot_tn(dv, h)
        dmod_ref[0:1, :] += jnp.sum(dh * xv, axis=0, keepdims=True)
        dmod_ref[1:2, :] += jnp.sum(dh, axis=0, keepdims=True)
        if with_dx:
            dx_ref[...] = gx_ref[...] + dh * scale1

    in_specs = ([_tok(T, D_MODEL), _per_batch(2, D_MODEL)] + [_tok(T, D_S5)] * n_u + [_tok(T, wd) for wd in IN_WIDTHS[1:]]
                + [_resident((P_IN, D_MODEL))])
    args = [x, mod, *ds[0], *ds[1:], wt]
    out_specs = [_resident((P_IN, D_MODEL)), _per_batch(2, D_MODEL)]
    out_shape = [SDS((P_IN, D_MODEL), f32), SDS((B, 2, D_MODEL), f32)]
    if with_dx:
        in_specs.append(_tok(T, D_MODEL))
        args.append(gx_res)
        out_specs.append(_tok(T, D_MODEL))
        out_shape.append(SDS((B, L, D_MODEL), f32))
    if with_start:
        in_specs.append(_resident((P_IN, D_MODEL)))
        args.append(dw_start)
    return pl.pallas_call(body, name=name, grid=(B, L // T), in_specs=in_specs, out_specs=out_specs,
                          out_shape=out_shape, compiler_params=_cparams(2))(*args)


def _s5_zoh(lr, li, ldt, bre, bim, expand):
    dt = jnp.exp(ldt)
    zr, zi = lr * dt, li * dt
    e = jnp.exp(zr)
    ar, ai = e * jnp.cos(zi), e * jnp.sin(zi)
    den = lr * lr + li * li
    czr = ((ar - 1.0) * lr + ai * li) / den
    czi = (ai * lr - (ar - 1.0) * li) / den
    czr_e, czi_e = _dot_hi(czr, expand), _dot_hi(czi, expand)
    return ar, ai, czr_e * bre - czi_e * bim, czr_e * bim + czi_e * bre


_ZOH_OUT = [(N_DIR * S5_GROUPS, S5_STATE)] * 2 + [(N_DIR * S5_GROUPS, S5_STATE * S5_GROUP)] * 2


def s5_zoh_fwd(lr, li, ldt, bre, bim, expand):
    def body(lr_ref, li_ref, ldt_ref, bre_ref, bim_ref, e_ref, ar_ref, ai_ref, bbr_ref, bbi_ref):
        ar, ai, bbr, bbi = _s5_zoh(lr_ref[...], li_ref[...], ldt_ref[...], bre_ref[...], bim_ref[...], e_ref[...])
        ar_ref[...], ai_ref[...], bbr_ref[...], bbi_ref[...] = ar, ai, bbr, bbi

    return pl.pallas_call(body, name="s5_zoh_fwd", out_shape=[SDS(s, f32) for s in _ZOH_OUT])(
        lr, li, ldt, bre, bim, expand)


def s5_zoh_bwd(lr, li, ldt, bre, bim, expand, dar, dai, dbbr, dbbi):
    def body(lr_ref, li_ref, ldt_ref, bre_ref, bim_ref, e_ref, dar_ref, dai_ref, dbbr_ref, dbbi_ref,
             dlr_ref, dli_ref, dldt_ref, dbre_ref, dbim_ref):
        ev = e_ref[...]
        _, vjp = jax.vjp(lambda a, b, c, d, e: _s5_zoh(a, b, c, d, e, ev),
                         lr_ref[...], li_ref[...], ldt_ref[...], bre_ref[...], bim_ref[...])
        outs = vjp((dar_ref[...], dai_ref[...], dbbr_ref[...], dbbi_ref[...]))
        dlr_ref[...], dli_ref[...], dldt_ref[...], dbre_ref[...], dbim_ref[...] = outs

    shapes = [lr.shape, li.shape, ldt.shape, bre.shape, bim.shape]
    return pl.pallas_call(body, name="s5_zoh_bwd", out_shape=[SDS(s, f32) for s in shapes])(
        lr, li, ldt, bre, bim, expand, dar, dai, dbbr, dbbi)


def _scan_rows(T, rev, ar, ai, h0s, refs, off):
    def step(i, carry):
        t = off + ((T - 1 - i) if rev else i)
        out = []
        for (hr, hi), (r_ref, i_ref) in zip(carry, refs):
            nr = ar * hr - ai * hi + r_ref[pl.ds(t, 1), :]
            ni = ar * hi + ai * hr + i_ref[pl.ds(t, 1), :]
            r_ref[pl.ds(t, 1), :] = nr
            i_ref[pl.ds(t, 1), :] = ni
            out.append((nr, ni))
        return tuple(out)

    return lax.fori_loop(0, T, step, tuple(h0s))


S5_BLOCKS = 4
S5_BC = D_S5 // S5_BLOCKS
S5_BS = S5_HALF // S5_BLOCKS


def _s5_in(uv, bre_ref, bim_ref, hr_ref, hi_ref, off, T):
    for jb in range(S5_BLOCKS):
        uj = uv[:, jb * S5_BC:(jb + 1) * S5_BC]
        hr_ref[off:off + T, jb * S5_BS:(jb + 1) * S5_BS] = _dot(uj, bre_ref[jb])
        hi_ref[off:off + T, jb * S5_BS:(jb + 1) * S5_BS] = _dot(uj, bim_ref[jb])


def _s5_specs(B, T, nt, rev):
    tidx = (lambda n: nt - 1 - n) if rev else (lambda n: n)
    tok = pl.BlockSpec((B, T, D_S5), lambda n: (0, tidx(n), 0))
    hin = pl.BlockSpec((B, None, 2, S5_HALF), lambda n: (0, tidx(n), 0, 0))
    state = pl.BlockSpec((B, 2, S5_HALF), lambda n: (0, 0, 0))
    return tok, hin, state


def s5_scan_fwd(u, bre, bim, ctop, cbot, arow, h0, *, d, need_y, name):
    B, L, _ = u.shape
    T = min(S5_TILE, L)
    nt = L // T
    rev = d == 1

    def body(u_ref, bre_ref, bim_ref, ct_ref, cb_ref, a_ref, h0_ref, *rest):
        if need_y:
            y_ref, hs_ref, hin_ref, hend_ref, hr_scr, hi_scr, h_scr = rest
        else:
            hs_ref, hin_ref, hend_ref, hr_scr, hi_scr, h_scr = rest
        n = pl.program_id(0)

        @pl.when(n == 0)
        def _():
            h_scr[...] = h0_ref[...]

        hin_ref[...] = h_scr[...]
        for b in range(B):
            _s5_in(u_ref[b].astype(bf16), bre_ref, bim_ref, hr_scr.at[b], hi_scr.at[b], 0, T)
        hs = _scan_rows(T, rev, a_ref[0:1, :], a_ref[1:2, :], [(h_scr[b, 0:1, :], h_scr[b, 1:2, :]) for b in range(B)],
                        [(hr_scr.at[b], hi_scr.at[b]) for b in range(B)], 0)
        for b in range(B):
            h_scr[b, 0:1, :] = hs[b][0]
            h_scr[b, 1:2, :] = hs[b][1]
            hs_ref[b, :, 0:S5_HALF] = hr_scr[b].astype(bf16)
            hs_ref[b, :, S5_HALF:2 * S5_HALF] = hi_scr[b].astype(bf16)
            if need_y:
                for jb in range(S5_BLOCKS):
                    st = slice(jb * S5_BS, (jb + 1) * S5_BS)
                    y_ref[b, :, jb * S5_BC:(jb + 1) * S5_BC] = (_dot(hr_scr[b, :, st], ct_ref[jb])
                                                                 + _dot(hi_scr[b, :, st], cb_ref[jb]))

        @pl.when(n == nt - 1)
        def _():
            hend_ref[...] = h_scr[...]

    tok, hin_spec, state = _s5_specs(B, T, nt, rev)
    hs_spec = pl.BlockSpec((B, T, 2 * S5_HALF), tok.index_map)
    out_specs = [hs_spec, hin_spec, state]
    out_shape = [SDS((B, L, 2 * S5_HALF), bf16), SDS((B, nt, 2, S5_HALF), f32), SDS((B, 2, S5_HALF), f32)]
    if need_y:
        out_specs.insert(0, tok)
        out_shape.insert(0, SDS((B, L, D_S5), f32))
    w_in, w_out = _resident((S5_BLOCKS, S5_BC, S5_BS)), _resident((S5_BLOCKS, S5_BS, S5_BC))
    return pl.pallas_call(
        body, name=name, grid=(nt,),
        in_specs=[tok, w_in, w_in, w_out, w_out, _resident((2, S5_HALF)), state],
        out_specs=out_specs, out_shape=out_shape,
        scratch_shapes=[pltpu.VMEM((B, T, S5_HALF), f32), pltpu.VMEM((B, T, S5_HALF), f32),
                        pltpu.VMEM((B, 2, S5_HALF), f32)],
        compiler_params=_cparams(1),
    )(u, bre, bim, ctop, cbot, arow, h0)


def s5_scan_bwd(u, dy, hs, bre, bim, ctop, cbot, arow, hin, dhend, *, d, name):
    B, L, _ = u.shape
    T = min(S5_TILE, L)
    nt = L // T
    rev = d == 1
    has_dy = dy is not None
    PAD = 8

    def body(*refs):
        u_ref = refs[0]
        k = 1
        if has_dy:
            dy_ref = refs[1]
            k = 2
        hs_ref = refs[k]
        k += 1
        bre_ref, bim_ref, ct_ref, cb_ref, a_ref, hin_ref, dhend_ref = refs[k:k + 7]
        du_ref, dbre_ref, dbim_ref, dct_ref, dcb_ref, da_ref, dh0_ref = refs[k + 7:k + 14]
        hr_scr, hi_scr, gr_scr, gi_scr, p_scr = refs[k + 14:]
        n = pl.program_id(0)

        @pl.when(n == 0)
        def _():
            for r in (dbre_ref, dbim_ref, dct_ref, dcb_ref, da_ref):
                r[...] = jnp.zeros_like(r)
            p_scr[...] = dhend_ref[...]

        ar, ai = a_ref[0:1, :], a_ref[1:2, :]
        prev_row = PAD + T if rev else PAD - 1
        uvs = []
        for b in range(B):
            uvs.append(u_ref[b].astype(bf16))
            hr_scr[b, PAD:PAD + T, :] = hs_ref[b, :, 0:S5_HALF].astype(f32)
            hi_scr[b, PAD:PAD + T, :] = hs_ref[b, :, S5_HALF:2 * S5_HALF].astype(f32)
            hr_scr[b, prev_row:prev_row + 1, :] = hin_ref[b, 0:1, :]
            hi_scr[b, prev_row:prev_row + 1, :] = hin_ref[b, 1:2, :]
        if has_dy:
            for b in range(B):
                dyv = dy_ref[b].astype(bf16)
                for jb in range(S5_BLOCKS):
                    st = slice(jb * S5_BS, (jb + 1) * S5_BS)
                    dyj = dyv[:, jb * S5_BC:(jb + 1) * S5_BC]
                    gr_scr[b, :, st] = _dot_nt(dyj, ct_ref[jb])
                    gi_scr[b, :, st] = _dot_nt(dyj, cb_ref[jb])
                    dct_ref[jb] += _dot_tn(hr_scr[b, PAD:PAD + T, st], dyj)
                    dcb_ref[jb] += _dot_tn(hi_scr[b, PAD:PAD + T, st], dyj)
        else:
            gr_scr[...] = jnp.zeros_like(gr_scr)
            gi_scr[...] = jnp.zeros_like(gi_scr)

        def step(i, carry):
            t = i if rev else T - 1 - i
            tp = PAD + t + (1 if rev else -1)
            out = []
            for b, (pr, pi, dar, dai) in enumerate(carry):
                gr = gr_scr[b, pl.ds(t, 1), :] + pr
                gi = gi_scr[b, pl.ds(t, 1), :] + pi
                gr_scr[b, pl.ds(t, 1), :] = gr
                gi_scr[b, pl.ds(t, 1), :] = gi
                hpr = hr_scr[b, pl.ds(tp, 1), :]
                hpi = hi_scr[b, pl.ds(tp, 1), :]
                out.append((ar * gr + ai * gi, ar * gi - ai * gr, dar + hpr * gr + hpi * gi, dai + hpr * gi - hpi * gr))
            return tuple(out)

        zero = jnp.zeros((1, S5_HALF), f32)
        res = lax.fori_loop(0, T, step, tuple((p_scr[b, 0:1, :], p_scr[b, 1:2, :], zero, zero) for b in range(B)))
        for b in range(B):
            pr, pi, dar, dai = res[b]
            p_scr[b, 0:1, :] = pr
            p_scr[b, 1:2, :] = pi
            da_ref[0:1, :] += dar
            da_ref[1:2, :] += dai
            for jb in range(S5_BLOCKS):
                st = slice(jb * S5_BS, (jb + 1) * S5_BS)
                ch = slice(jb * S5_BC, (jb + 1) * S5_BC)
                gr_j = gr_scr[b, :, st].astype(bf16)
                gi_j = gi_scr[b, :, st].astype(bf16)
                du_ref[b, :, ch] = _dot_nt(gr_j, bre_ref[jb]) + _dot_nt(gi_j, bim_ref[jb])
                dbre_ref[jb] += _dot_tn(uvs[b][:, ch], gr_j)
                dbim_ref[jb] += _dot_tn(uvs[b][:, ch], gi_j)

        @pl.when(n == nt - 1)
        def _():
            dh0_ref[...] = p_scr[...]

    tok, hin_spec, state = _s5_specs(B, T, nt, not rev)
    hs_spec = pl.BlockSpec((B, T, 2 * S5_HALF), tok.index_map)
    w_in, w_out = _resident((S5_BLOCKS, S5_BC, S5_BS)), _resident((S5_BLOCKS, S5_BS, S5_BC))
    wspecs = [w_in, w_in, w_out, w_out]
    in_specs = [tok] + ([tok] if has_dy else []) + [hs_spec] + wspecs + [_resident((2, S5_HALF)), hin_spec, state]
    args = [u] + ([dy] if has_dy else []) + [hs, bre, bim, ctop, cbot, arow, hin, dhend]
    return pl.pallas_call(
        body, name=name, grid=(nt,), in_specs=in_specs,
        out_specs=[tok] + wspecs + [_resident((2, S5_HALF)), state],
        out_shape=[SDS((B, L, D_S5), f32), SDS((S5_BLOCKS, S5_BC, S5_BS), f32), SDS((S5_BLOCKS, S5_BC, S5_BS), f32),
                   SDS((S5_BLOCKS, S5_BS, S5_BC), f32), SDS((S5_BLOCKS, S5_BS, S5_BC), f32), SDS((2, S5_HALF), f32),
                   SDS((B, 2, S5_HALF), f32)],
        scratch_shapes=[pltpu.VMEM((B, T + 2 * PAD, S5_HALF), f32), pltpu.VMEM((B, T + 2 * PAD, S5_HALF), f32),
                        pltpu.VMEM((B, T, S5_HALF), f32), pltpu.VMEM((B, T, S5_HALF), f32),
                        pltpu.VMEM((B, 2, S5_HALF), f32)],
        compiler_params=_cparams(1),
    )(*args)


def _glu_fn(u, y0, y1, z, dsk, wg, bg):
    g = _gelu(dsk * u + y0 + y1)
    return g * jax.nn.sigmoid(_mm(g, wg) + bg) * _silu(z)


CONV_ROWS = 16


def _shift(x, s):
    L = x.shape[0]
    k = (-s) % L
    return x if k == 0 else pltpu.roll(x, k, axis=0)


def _r16(v):
    return v.astype(bf16).astype(f32)


def _conv_masks(L, is_ctx):
    t = lax.broadcasted_iota(jnp.int32, (L, 1), 0)
    if is_ctx:
        return t == L - 1, t == 0, None, None
    col = jnp.bitwise_and(t, GRID_W - 1)
    return col == GRID_W - 1, col == 0, t >= GRID_W, t < L - GRID_W


def _conv_sides(xv, masks):
    no_left, no_right, _, _ = masks
    return _shift(jnp.where(no_left, 0.0, xv), -1), _shift(jnp.where(no_right, 0.0, xv), 1)


def _conv_pre(xv, w_ref, masks, is_ctx):
    xv = _r16(xv)
    wv = _r16(w_ref[...])
    xl, xr = _conv_sides(xv, masks)
    z = [wv[3 * di:3 * di + 1, :] * xl + wv[3 * di + 1:3 * di + 2, :] * xv + wv[3 * di + 2:3 * di + 3, :] * xr
         for di in ((1,) if is_ctx else (0, 1, 2))]
    if is_ctx:
        return z[0]
    _, _, has_up, has_down = masks
    return z[1] + jnp.where(has_up, _shift(z[0], -GRID_W), 0.0) + jnp.where(has_down, _shift(z[2], GRID_W), 0.0)


def _conv_pre_bwd(xv, w_ref, dpre, masks, is_ctx, dw_ref):
    no_left, no_right, has_up, has_down = masks
    xv, dpre, wv = _r16(xv), _r16(dpre), _r16(w_ref[...])
    xl, xr = _conv_sides(xv, masks)
    if is_ctx:
        dz = {1: dpre}
    else:
        dz = {0: _shift(jnp.where(has_up, dpre, 0.0), GRID_W), 1: dpre, 2: _shift(jnp.where(has_down, dpre, 0.0), -GRID_W)}
    dxl = dxc = dxr = None
    for di, d in dz.items():
        for dj, side in enumerate((xl, xv, xr)):
            dw_ref[3 * di + dj:3 * di + dj + 1, :] = jnp.sum(d * side, axis=0, keepdims=True)
        tl, tc, tr = (wv[3 * di + dj:3 * di + dj + 1, :] * d for dj in range(3))
        dxl, dxc, dxr = (tl, tc, tr) if dxl is None else (dxl + tl, dxc + tc, dxr + tr)
    return dxc + jnp.where(no_left, 0.0, _shift(dxl, 1)) + jnp.where(no_right, 0.0, _shift(dxr, -1))


def _qk_post(pre, is_norm, scale):
    s = _silu(pre)
    nrm = lax.rsqrt(jnp.sum(s * s, axis=-1, keepdims=True) + NORM_EPS)
    return s * jnp.where(is_norm, nrm * scale, 1.0)


def _conv_tile(L):
    return D_GDN if L <= 512 else 2 * GDN_HEAD


def _conv_kind(W):
    head = pl.program_id(1) * (W // GDN_HEAD)
    return head < 2 * GDN_HEADS, jnp.where(head < GDN_HEADS, GDN_HEAD ** -0.5, 1.0).astype(f32)


def _conv_specs(L, W):
    spec = pl.BlockSpec((None, L, W), lambda b, ct: (b, 0, ct))
    wspec = pl.BlockSpec((CONV_ROWS, W), lambda b, ct: (0, ct))
    dwspec = pl.BlockSpec((None, CONV_ROWS, W), lambda b, ct: (b, 0, ct))
    return spec, wspec, dwspec


def conv_fwd(qkv, w16, *, is_ctx, name):
    B, L, C = qkv.shape
    W = _conv_tile(L)
    spec, wspec, _ = _conv_specs(L, W)

    def body(x_ref, w_ref, o_ref, pre_ref):
        is_norm, scale = _conv_kind(W)
        pre = _conv_pre(x_ref[...], w_ref, _conv_masks(L, is_ctx), is_ctx)
        pre_ref[...] = pre
        for h in range(W // GDN_HEAD):
            sl = slice(h * GDN_HEAD, (h + 1) * GDN_HEAD)
            o_ref[:, sl] = _qk_post(pre[:, sl], is_norm, scale)

    return pl.pallas_call(body, name=name, grid=(B, C // W), in_specs=[spec, wspec], out_specs=[spec, spec],
                          out_shape=[SDS((B, L, C), f32)] * 2, compiler_params=_cparams(2))(qkv, w16)


def conv_bwd(qkv, pre, w16, da0, da1, *, is_ctx, name):
    B, L, C = qkv.shape
    W = _conv_tile(L)
    spec, wspec, dwspec = _conv_specs(L, W)

    def body(x_ref, pre_ref, w_ref, d0_ref, d1_ref, dx_ref, dw_ref):
        is_norm, scale = _conv_kind(W)
        dpre = []
        for h in range(W // GDN_HEAD):
            sl = slice(h * GDN_HEAD, (h + 1) * GDN_HEAD)
            _, vjp = jax.vjp(lambda p: _qk_post(p, is_norm, scale), pre_ref[:, sl])
            dpre.append(vjp(d0_ref[:, sl] + d1_ref[:, sl])[0])
        dpre = dpre[0] if len(dpre) == 1 else jnp.concatenate(dpre, axis=1)
        dw_ref[...] = jnp.zeros_like(dw_ref)
        dx_ref[...] = _conv_pre_bwd(x_ref[...], w_ref, dpre, _conv_masks(L, is_ctx), is_ctx, dw_ref)

    return pl.pallas_call(body, name=name, grid=(B, C // W), in_specs=[spec, spec, wspec, spec, spec],
                          out_specs=[spec, dwspec], out_shape=[SDS((B, L, C), f32), SDS((B, CONV_ROWS, C), f32)],
                          compiler_params=_cparams(2))(qkv, pre, w16, da0, da1)


def _gates_fn(ba, alog, dtb):
    T = ba.shape[0]
    nck = T // CHUNK
    lane = lax.broadcasted_iota(jnp.int32, ba.shape, 1)
    ii = lax.broadcasted_iota(jnp.int32, (nck, CHUNK, CHUNK), 1)
    jj = lax.broadcasted_iota(jnp.int32, (nck, CHUNK, CHUNK), 2)
    g = jnp.where(lane >= 8, -jnp.exp(alog) * jax.nn.softplus(ba + dtb), 0.0)
    g3 = g.reshape(nck, CHUNK, N_GATE)
    chunk_sum = lambda tri: lax.dot_general(tri.astype(f32), g3, (((2,), (1,)), ((0,), (0,))),
                                            precision=lax.Precision.HIGHEST, preferred_element_type=f32).reshape(T, N_GATE)
    gc = jnp.where(lane >= 12, chunk_sum(ii <= jj), chunk_sum(ii >= jj))
    return jnp.where(lane < 8, jax.nn.sigmoid(ba), gc)


def gates_fwd(ba, alog, dtb, *, name):
    B, L, _ = ba.shape
    T = min(TOK_TILE, L)
    t = _tok(T, N_GATE)

    def body(ba_ref, al_ref, dt_ref, o_ref):
        o_ref[...] = _gates_fn(ba_ref[...], al_ref[...], dt_ref[...])

    return pl.pallas_call(body, name=name, grid=(B, L // T),
                          in_specs=[t, _resident((1, N_GATE)), _resident((1, N_GATE))], out_specs=t,
                          out_shape=SDS((B, L, N_GATE), f32), compiler_params=_cparams(2))(ba, alog, dtb)


def gates_bwd(ba, alog, dtb, dbg, *, name):
    B, L, _ = ba.shape
    T = min(TOK_TILE, L)
    t = _tok(T, N_GATE)
    small = _resident((1, N_GATE))

    def body(ba_ref, al_ref, dt_ref, d_ref, dba_ref, dal_ref, ddt_ref):
        @pl.when(_first_step())
        def _():
            dal_ref[...] = jnp.zeros_like(dal_ref)
            ddt_ref[...] = jnp.zeros_like(ddt_ref)

        _, vjp = jax.vjp(_gates_fn, ba_ref[...], al_ref[...], dt_ref[...])
        dba, dal, ddt = vjp(d_ref[...])
        dba_ref[...] = dba
        dal_ref[...] += dal
        ddt_ref[...] += ddt

    return pl.pallas_call(body, name=name, grid=(B, L // T), in_specs=[t, small, small, t],
                          out_specs=[t, small, small],
                          out_shape=[SDS((B, L, N_GATE), f32), SDS((1, N_GATE), f32), SDS((1, N_GATE), f32)],
                          compiler_params=_cparams(2))(ba, alog, dtb, dbg)


@jax.custom_vjp
def _inv_unit_tri(mats):
    n = mats[0].shape[0]
    eye = (lax.broadcasted_iota(jnp.int32, (n, n), 0) == lax.broadcasted_iota(jnp.int32, (n, n), 1)).astype(f32)
    xs = [eye - a for a in mats]
    sq = [_dot(a, a) for a in mats]
    ps = sq
    k = 2
    while k < n:
        xs = [x + _dot(x, p) for x, p in zip(xs, ps)]
        k *= 2
        if k < n:
            ps = [_dot(p, p) for p in ps]
    return tuple(_dot(p, x) - a for p, x, a in zip(sq, xs, mats))


def _inv_unit_tri_fwd(mats):
    ns = _inv_unit_tri(mats)
    return ns, ns


def _inv_unit_tri_bwd(ns, dns):
    ys = [dn + _dot_tn(nn, dn) for nn, dn in zip(ns, dns)]
    return (tuple(-(y + _dot_nt(y, nn)) for y, nn in zip(ys, ns)),)


_inv_unit_tri.defvjp(_inv_unit_tri_fwd, _inv_unit_tri_bwd)


@jax.custom_vjp
def _inv_unit_tri_saved(mats, saved):
    return saved


_inv_unit_tri_saved.defvjp(lambda mats, saved: (saved, saved),
                           lambda ns, dns: _inv_unit_tri_bwd(ns, dns) + (tuple(jnp.zeros_like(n) for n in ns),))


def _gdn_chunk(heads, *, revs, saved=None, with_n=False):
    n = heads[0][0].shape[0]
    ii = lax.broadcasted_iota(jnp.int32, (n, n), 0)
    jj = lax.broadcasted_iota(jnp.int32, (n, n), 1)
    row = lax.broadcasted_iota(jnp.int32, (n, 1), 0)
    lower = {False: ii >= jj, True: ii <= jj}
    strict = {False: ii > jj, True: ii < jj}
    last = {False: n - 1, True: 0}
    H = range(len(heads))
    q, k, v, beta, gc, gr, s = (list(t) for t in zip(*heads))
    decay = [jnp.where(lower[revs[h]], jnp.exp(jnp.where(lower[revs[h]], gc[h] - gr[h], 0.0)), 0.0) for h in H]
    kk = [_mm_nt(k[h], k[h]) for h in H]
    qk = [_mm_nt(q[h], k[h]) * decay[h] for h in H]
    qs = [_mm(q[h], s[h]) for h in H]
    a_mat = tuple(jnp.where(strict[revs[h]], beta[h] * kk[h] * decay[h], 0.0) for h in H)
    gamma = [jnp.exp(gc[h]) for h in H]
    g_last = [jnp.sum(jnp.where(row == last[revs[h]], gc[h], 0.0), axis=0, keepdims=True) for h in H]
    nmat = _inv_unit_tri(a_mat) if saved is None else _inv_unit_tri_saved(a_mat, saved)
    bv = [beta[h] * v[h] for h in H]
    bk = [(beta[h] * gamma[h]) * k[h] for h in H]
    u0 = [bv[h] + _mm(nmat[h], bv[h]) for h in H]
    w = [bk[h] + _mm(nmat[h], bk[h]) for h in H]
    k_out = [k[h] * jnp.exp(g_last[h] - gc[h]) for h in H]
    u = [u0[h] - _mm(w[h], s[h]) for h in H]
    o = [gamma[h] * qs[h] + _mm(qk[h], u[h]) for h in H]
    s_new = [jnp.exp(g_last[h]) * s[h] + _mm_tn(k_out[h], u[h]) for h in H]
    outs = tuple((o[h], s_new[h]) for h in H)
    return (outs, nmat) if with_n else outs


def _gdn_specs(B, nc, rev):
    def cidx(n):
        return (nc - 1 - n) if rev else n
    tok = lambda width: pl.BlockSpec((B, CHUNK, width), lambda n: (0, cidx(n), 0))
    rowspec = pl.BlockSpec((B, None, N_GATE, CHUNK), lambda n: (0, cidx(n), 0, 0))
    st = pl.BlockSpec((B, GDN_HEADS, GDN_HEAD, GDN_HEAD), lambda n: (0, 0, 0, 0))
    ck = pl.BlockSpec((B, None, GDN_HEADS, GDN_HEAD, GDN_HEAD), lambda n: (0, cidx(n), 0, 0, 0))
    nsp = pl.BlockSpec((B, None, GDN_HEADS, CHUNK, CHUNK), lambda n: (0, cidx(n), 0, 0, 0))
    return tok, rowspec, st, ck, nsp


def _gdn_head_args(qkv_ref, bg_ref, bgr_ref, b, d, h):
    col = d * GDN_HEADS + h
    q = qkv_ref[b, :, h * GDN_HEAD:(h + 1) * GDN_HEAD]
    k = qkv_ref[b, :, D_GDN + h * GDN_HEAD:D_GDN + (h + 1) * GDN_HEAD]
    v = qkv_ref[b, :, 2 * D_GDN + h * GDN_HEAD:2 * D_GDN + (h + 1) * GDN_HEAD]
    bgv = bg_ref[b]
    return q, k, v, bgv[:, col:col + 1], bgv[:, 8 + col:9 + col], bgr_ref[b][8 + col:9 + col, :]


def _gdn_chains(B):
    return [(d, b, h) for d in range(N_DIR) for b in range(B) for h in range(GDN_HEADS)]


def gdn_fwd(qkv, bg, bgr, s0s, *, need_o, name):
    B, L, _ = qkv.shape
    nc = L // CHUNK
    specs = [_gdn_specs(B, nc, d == 1) for d in range(N_DIR)]
    chains = _gdn_chains(B)
    state_shape = (B, GDN_HEADS, GDN_HEAD, GDN_HEAD)

    def body(*refs):
        ins = [refs[3 * d:3 * d + 3] for d in range(N_DIR)]
        s0_refs = refs[6:8]
        k = 8
        o_refs = refs[k:k + 2] if need_o else None
        k += 2 if need_o else 0
        ck_refs, n_refs, sf_refs, s_scrs = refs[k:k + 2], refs[k + 2:k + 4], refs[k + 4:k + 6], refs[k + 6:k + 8]
        n = pl.program_id(0)

        @pl.when(n == 0)
        def _():
            for d in range(N_DIR):
                s_scrs[d][...] = s0_refs[d][...]

        for d in range(N_DIR):
            ck_refs[d][...] = s_scrs[d][...]
        heads = tuple(_gdn_head_args(*ins[d], b, d, h) + (s_scrs[d][b, h],) for d, b, h in chains)
        outs, nmat = _gdn_chunk(heads, revs=tuple(d == 1 for d, _, _ in chains), with_n=True)
        for (d, b, h), (o, s_new), nn in zip(chains, outs, nmat):
            if need_o:
                o_refs[d][b, :, h * GDN_HEAD:(h + 1) * GDN_HEAD] = o
            s_scrs[d][b, h] = s_new
            n_refs[d][b, h] = nn

        @pl.when(n == nc - 1)
        def _():
            for d in range(N_DIR):
                sf_refs[d][...] = s_scrs[d][...]

    in_specs, out_o, out_ck, out_n, out_sf = [], [], [], [], []
    for tok, rowspec, st, ck, nsp in specs:
        in_specs += [tok(3 * D_GDN), tok(N_GATE), rowspec]
        out_o.append(tok(D_GDN))
        out_ck.append(ck)
        out_n.append(nsp)
        out_sf.append(st)
    in_specs += [specs[0][2]] * 2
    out_specs = (out_o if need_o else []) + out_ck + out_n + out_sf
    out_shape = (([SDS((B, L, D_GDN), f32)] * 2 if need_o else []) + [SDS((B, nc) + state_shape[1:], f32)] * 2
                 + [SDS((B, nc, GDN_HEADS, CHUNK, CHUNK), f32)] * 2 + [SDS(state_shape, f32)] * 2)
    res = pl.pallas_call(
        body, name=name, grid=(nc,), in_specs=in_specs, out_specs=out_specs, out_shape=out_shape,
        scratch_shapes=[pltpu.VMEM(state_shape, f32)] * 2, compiler_params=_cparams(1),
    )(qkv, bg, bgr, qkv, bg, bgr, *s0s)
    if need_o:
        return res[0:2], res[2:4], res[4:6], res[6:8]
    return res[0:2], res[2:4], res[4:6]


def gdn_bwd(qkv, bg, bgr, cks, ns, do, dsfs, *, name):
    B, L, _ = qkv.shape
    nc = L // CHUNK
    has_do = do is not None
    specs = [_gdn_specs(B, nc, d != 1) for d in range(N_DIR)]
    chains = _gdn_chains(B)
    state_shape = (B, GDN_HEADS, GDN_HEAD, GDN_HEAD)
    per_dir = 6 if has_do else 5

    def body(*refs):
        ins = [refs[per_dir * d:per_dir * d + per_dir] for d in range(N_DIR)]
        k = per_dir * N_DIR
        dsf_refs = refs[k:k + 2]
        outs = [refs[k + 2 + 3 * d:k + 5 + 3 * d] for d in range(N_DIR)]
        ds0_refs, ds_scrs = refs[k + 8:k + 10], refs[k + 10:k + 12]
        n = pl.program_id(0)

        @pl.when(n == 0)
        def _():
            for d in range(N_DIR):
                ds_scrs[d][...] = dsf_refs[d][...]

        lane = lax.broadcasted_iota(jnp.int32, (CHUNK, N_GATE), 1)
        sub = lax.broadcasted_iota(jnp.int32, (N_GATE, CHUNK), 0)
        heads = tuple(_gdn_head_args(*ins[d][:3], b, d, h) + (ins[d][3][b, h],) for d, b, h in chains)
        saved = tuple(ins[d][4][b, h] for d, b, h in chains)
        _, vjp = jax.vjp(functools.partial(_gdn_chunk, revs=tuple(d == 1 for d, _, _ in chains), saved=saved), heads)
        zero = jnp.zeros((CHUNK, GDN_HEAD), f32)
        cts = tuple(((ins[d][5][b, :, h * GDN_HEAD:(h + 1) * GDN_HEAD] if has_do else zero), ds_scrs[d][b, h])
                    for d, b, h in chains)
        (dheads,) = vjp(cts)
        dbg_acc = [[jnp.zeros((CHUNK, N_GATE), f32) for _ in range(B)] for _ in range(N_DIR)]
        dbgr_acc = [[jnp.zeros((N_GATE, CHUNK), f32) for _ in range(B)] for _ in range(N_DIR)]
        for (d, b, h), (dq, dk, dv, db, dgc, dgr, ds) in zip(chains, dheads):
            col = d * GDN_HEADS + h
            dqkv_ref = outs[d][0]
            dqkv_ref[b, :, h * GDN_HEAD:(h + 1) * GDN_HEAD] = dq
            dqkv_ref[b, :, D_GDN + h * GDN_HEAD:D_GDN + (h + 1) * GDN_HEAD] = dk
            dqkv_ref[b, :, 2 * D_GDN + h * GDN_HEAD:2 * D_GDN + (h + 1) * GDN_HEAD] = dv
            dbg_acc[d][b] = dbg_acc[d][b] + jnp.where(lane == col, db, 0.0) + jnp.where(lane == 8 + col, dgc, 0.0)
            dbgr_acc[d][b] = dbgr_acc[d][b] + jnp.where(sub == 8 + col, dgr, 0.0)
            ds_scrs[d][b, h] = ds
        for d in range(N_DIR):
            for b in range(B):
                outs[d][1][b] = dbg_acc[d][b]
                outs[d][2][b] = dbgr_acc[d][b]

        @pl.when(n == nc - 1)
        def _():
            for d in range(N_DIR):
                ds0_refs[d][...] = ds_scrs[d][...]

    in_specs, args, out_specs, out_shape = [], [], [], []
    for d, (tok, rowspec, st, ck, nsp) in enumerate(specs):
        in_specs += [tok(3 * D_GDN), tok(N_GATE), rowspec, ck, nsp] + ([tok(D_GDN)] if has_do else [])
        args += [qkv, bg, bgr, cks[d], ns[d]] + ([do] if has_do else [])
        out_specs += [tok(3 * D_GDN), tok(N_GATE), rowspec]
        out_shape += [SDS((B, L, 3 * D_GDN), f32), SDS((B, L, N_GATE), f32), SDS((B, nc, N_GATE, CHUNK), f32)]
    st = specs[0][2]
    in_specs += [st, st]
    args += list(dsfs)
    out_specs += [st, st]
    out_shape += [SDS(state_shape, f32)] * 2
    res = pl.pallas_call(
        body, name=name, grid=(nc,), in_specs=in_specs, out_specs=out_specs, out_shape=out_shape,
        scratch_shapes=[pltpu.VMEM(state_shape, f32)] * 2, compiler_params=_cparams(1),
    )(*args)
    return (res[0], res[3]), (res[1], res[4]), (res[2], res[5]), (res[6], res[7])


def _gnorm_fn(o0, o1, z, w):
    o = o0 + o1
    return o * lax.rsqrt(jnp.mean(o * o, axis=-1, keepdims=True) + NORM_EPS) * w * _silu(z)


def _head_loss(y, x, gate, lng, lnb, tgt):
    r = DEEPNORM_ALPHA * x + gate * y
    mu = jnp.mean(r, axis=-1, keepdims=True)
    rc = r - mu
    var = jnp.mean(rc * rc, axis=-1, keepdims=True)
    err = rc * lax.rsqrt(var + LN_EPS) * lng + lnb - tgt
    return (0.5 / D_MODEL) * jnp.sum(jnp.sum(err * err, axis=-1, keepdims=True), axis=0, keepdims=True)


def tail_fwd_bwd(u, y0, y1, z_s5, o0, o1, z_gdn, x, tgt, gate, lng, lnb, ws, wg, dsk, wglu, bglu, nw):
    B, L, _ = x.shape
    T = min(TOK_TILE, L)

    def body(u_ref, y0_ref, y1_ref, z_ref, o0_ref, o1_ref, zg_ref, x_ref, t_ref, gate_ref, lng_ref, lnb_ref, ws_ref,
             wg_ref, dsk_ref, wglu_ref, bglu_ref, nw_ref,
             loss_ref, du_ref, dys_ref, dz_ref, do_ref, dzg_ref, gx_ref, dws_ref, dwg_ref, dgate_ref, dlng_ref, dlnb_ref,
             ddsk_ref, dwglu_ref, dbglu_ref, dnw_ref):
        n = pl.program_id(1)

        @pl.when(_first_step())
        def _():
            for r in (dws_ref, dwg_ref, dlng_ref, dlnb_ref, ddsk_ref, dwglu_ref, dbglu_ref, dnw_ref):
                r[...] = jnp.zeros_like(r)

        @pl.when(n == 0)
        def _():
            loss_ref[...] = jnp.zeros_like(loss_ref)
            dgate_ref[...] = jnp.zeros_like(dgate_ref)

        s5o, glu_vjp = jax.vjp(_glu_fn, u_ref[...], y0_ref[...], y1_ref[...], z_ref[...], dsk_ref[...],
                               wglu_ref[...].astype(f32), bglu_ref[...])
        heads = []
        for h in range(GDN_HEADS):
            sl = slice(h * GDN_HEAD, (h + 1) * GDN_HEAD)
            heads.append(jax.vjp(_gnorm_fn, o0_ref[:, sl], o1_ref[:, sl], zg_ref[:, sl], nw_ref[...]))
        sv = s5o.astype(bf16)
        gv = jnp.concatenate([out for out, _ in heads], axis=1).astype(bf16)
        y = _dot(sv, ws_ref[...]) + _dot(gv, wg_ref[...])
        loss, vjp = jax.vjp(lambda *a: _head_loss(*a, t_ref[...]), y, x_ref[...], gate_ref[...], lng_ref[...],
                            lnb_ref[...])
        dy, dx, dgate, dlng, dlnb = vjp(jnp.ones((1, 1), f32))
        loss_ref[...] += jnp.broadcast_to(loss, loss_ref.shape)
        dyb = dy.astype(bf16)
        gx_ref[...] = dx
        dws_ref[...] += _dot_tn(sv, dyb)
        dwg_ref[...] += _dot_tn(gv, dyb)
        dgate_ref[...] += dgate
        dlng_ref[...] += dlng
        dlnb_ref[...] += dlnb
        du, dys, _, dz, ddsk, dwglu, dbglu = glu_vjp(_dot_nt(dyb, ws_ref[...]))
        du_ref[...], dys_ref[...], dz_ref[...] = du, dys, dz
        ddsk_ref[...] += ddsk
        dwglu_ref[...] += dwglu
        dbglu_ref[...] += dbglu
        dgdo = _dot_nt(dyb, wg_ref[...])
        for h, (_, hvjp) in enumerate(heads):
            sl = slice(h * GDN_HEAD, (h + 1) * GDN_HEAD)
            do, _, dzg, dnw = hvjp(dgdo[:, sl])
            do_ref[:, sl] = do
            dzg_ref[:, sl] = dzg
            dnw_ref[...] += dnw

    half, full = _tok(T, D_S5), _tok(T, D_MODEL)
    row = _resident((1, D_MODEL))
    wsp = _resident((D_S5, D_MODEL))
    r512, rglu, r128 = _resident((1, D_S5)), _resident((D_S5, D_S5)), _resident((1, GDN_HEAD))
    return pl.pallas_call(
        body, name="tail_fwd_bwd", grid=(B, L // T),
        in_specs=[half] * 7 + [full, full, _per_batch(1, D_MODEL), row, row, wsp, wsp, r512, rglu, r512, r128],
        out_specs=[_per_batch(8, LANES)] + [half] * 5 + [full, wsp, wsp, _per_batch(1, D_MODEL), row, row, r512, rglu, r512,
                                                           r128],
        out_shape=[SDS((B, 8, LANES), f32)] + [SDS((B, L, D_S5), f32)] * 5 + [
            SDS((B, L, D_MODEL), f32), SDS((D_S5, D_MODEL), f32), SDS((D_GDN, D_MODEL), f32), SDS((B, 1, D_MODEL), f32),
            SDS((1, D_MODEL), f32), SDS((1, D_MODEL), f32), SDS((1, D_S5), f32), SDS((D_S5, D_S5), f32), SDS((1, D_S5), f32),
            SDS((1, GDN_HEAD), f32)],
        compiler_params=_cparams(2),
    )(u, y0, y1, z_s5, o0, o1, z_gdn, x, tgt, gate, lng, lnb, ws, wg, dsk, wglu, bglu, nw)


def _adamw_math(w, g, m, v):
    nm = ADAM_B1 * m + (1.0 - ADAM_B1) * g
    nv = ADAM_B2 * v + (1.0 - ADAM_B2) * jnp.square(g)
    m_hat = nm / (1.0 - ADAM_B1 ** ADAM_STEP)
    v_hat = nv / (1.0 - ADAM_B2 ** ADAM_STEP)
    return -ADAM_LR * (m_hat / (jnp.sqrt(v_hat) + ADAM_EPS) + ADAM_WD * w), nm, nv


def _row_tile(rows, cap=512):
    for t in range(min(cap, rows), 15, -1):
        if rows % t == 0 and t % 16 == 0:
            return t
    return rows


def adamw_3d(w, g, m, v, *, lead=False, name):
    R, C = (w.shape[0], w.shape[2]) if lead else w.shape[1:]
    if lead:
        T = next(t for t in range(min(256, R), 0, -1) if R % t == 0)
        spec = pl.BlockSpec((T, 1, C), lambda i: (i, 0, 0))
    else:
        T = _row_tile(R)
        spec = pl.BlockSpec((None, T, C), lambda i: (0, i, 0))

    def body(w_ref, g_ref, m_ref, v_ref, d_ref, nm_ref, nv_ref):
        d_ref[...], nm_ref[...], nv_ref[...] = _adamw_math(w_ref[...], g_ref[...], m_ref[...], v_ref[...])

    return pl.pallas_call(body, name=name, grid=(R // T,), in_specs=[spec] * 4, out_specs=[spec] * 3,
                          out_shape=[SDS(w.shape, f32)] * 3, compiler_params=_cparams(1))(w, g, m, v)


def adamw_small(ws, gs, ms, vs):
    n = len(ws)

    def body(*refs):
        outs = refs[4 * n:]
        for i in range(n):
            d, nm, nv = _adamw_math(refs[i][...], refs[n + i][...], refs[2 * n + i][...], refs[3 * n + i][...])
            outs[i][...], outs[n + i][...], outs[2 * n + i][...] = d, nm, nv

    res = pl.pallas_call(body, name="adamw_small", out_shape=[SDS(w.shape, f32) for w in ws] * 3,
                         compiler_params=pltpu.CompilerParams(vmem_limit_bytes=VMEM_LIMIT))(*ws, *gs, *ms, *vs)
    return res[:n], res[n:2 * n], res[2 * n:]


def sum_cores(own, got, *, name):
    A, H, C = own.shape
    T = _row_tile(H)
    spec = pl.BlockSpec((None, T, C), lambda a, i: (a, i, 0))

    def body(a_ref, b_ref, q32_ref, q16_ref):
        q = a_ref[...] + b_ref[...]
        q32_ref[...] = q
        q16_ref[...] = q.astype(bf16)

    return pl.pallas_call(body, name=name, grid=(A, H // T), in_specs=[spec, spec], out_specs=[spec, spec],
                          out_shape=[SDS((A, H, C), f32), SDS((A, H, C), bf16)], compiler_params=_cparams(2))(own, got)


def sum_cores_small(owns, gots):
    n = len(owns)

    def body(*refs):
        for i in range(n):
            q = refs[i][...] + refs[n + i][...]
            refs[2 * n + i][...] = q
            refs[3 * n + i][...] = q.astype(bf16)

    res = pl.pallas_call(body, name="sum_cores_small",
                         out_shape=[SDS(o.shape, f32) for o in owns] + [SDS(o.shape, bf16) for o in owns],
                         compiler_params=pltpu.CompilerParams(vmem_limit_bytes=VMEM_LIMIT))(*owns, *gots)
    return res[:n], res[n:]


def sum_chips_small(mines, recs, cpos, slots):
    n = len(mines)
    lead = [1 if s is None else 4 for s in slots]

    def body(s_ref, *refs):
        for i in range(n):
            m_ref, r_ref, f_ref = refs[i], refs[n + i], refs[2 * n + i]
            H = m_ref.shape[0]
            val = ((m_ref[...] + r_ref[0].astype(f32)) + r_ref[1].astype(f32)) + r_ref[2].astype(f32)
            f_ref[s_ref[1 + i], pl.ds(pl.multiple_of(s_ref[0] * H, 8), H), :] = val

    whole = lambda shape: pl.BlockSpec(shape, lambda g, s_ref: (0,) * len(shape))
    out_shapes = [(ld, 2 * m.shape[0], m.shape[1]) for ld, m in zip(lead, mines)]
    grid_spec = pltpu.PrefetchScalarGridSpec(
        num_scalar_prefetch=1, grid=(1,),
        in_specs=[whole(m.shape) for m in mines] + [whole(r.shape) for r in recs],
        out_specs=[whole(sh) for sh in out_shapes])
    scalars = jnp.stack([cpos] + [jnp.zeros_like(cpos) if s is None else s for s in slots]).astype(jnp.int32)
    return pl.pallas_call(body, name="sum_chips_small", grid_spec=grid_spec,
                          out_shape=[SDS(sh, f32) for sh in out_shapes],
                          compiler_params=pltpu.CompilerParams(vmem_limit_bytes=VMEM_LIMIT))(scalars, *mines, *recs)


def sum_chips(mine, rec, cpos, full, *, slot=None, name):
    H, C = mine.shape
    T = _row_tile(H)
    nt = H // T
    by_rows = _by_rows(full)
    out_idx = lambda i, s_ref: ((s_ref[1], s_ref[0] * nt + i, 0) if by_rows else (s_ref[1], i, s_ref[0]))

    def body(s_ref, m_ref, r_ref, f_ref):
        f_ref[...] = ((m_ref[...] + r_ref[0].astype(f32)) + r_ref[1].astype(f32)) + r_ref[2].astype(f32)

    grid_spec = pltpu.PrefetchScalarGridSpec(
        num_scalar_prefetch=1, grid=(nt,),
        in_specs=[pl.BlockSpec((T, C), lambda i, s_ref: (i, 0)), pl.BlockSpec((3, T, C), lambda i, s_ref: (0, i, 0))],
        out_specs=pl.BlockSpec((None, T, C), out_idx))
    scalars = jnp.stack([cpos, jnp.zeros_like(cpos) if slot is None else slot]).astype(jnp.int32)
    return pl.pallas_call(body, name=name, grid_spec=grid_spec,
                          out_shape=SDS((1 if slot is None else 4,) + tuple(full), f32),
                          compiler_params=_cparams(1))(scalars, mine, rec)


CHIP_FLIPS = ((1, 0), (0, 1), (1, 1))


def _pos():
    return lax.axis_index("x"), lax.axis_index("y"), lax.axis_index("c")


def _comm_call(body, srcs, out_sds, n_remote, n_local, name):
    any_spec = pl.BlockSpec(memory_space=pl.ANY)
    return pl.pallas_call(
        body, name=name, in_specs=[any_spec] * len(srcs), out_specs=[any_spec] * len(out_sds), out_shape=out_sds,
        scratch_shapes=[pltpu.SemaphoreType.DMA((n_remote,)), pltpu.SemaphoreType.DMA((n_remote,)),
                        pltpu.SemaphoreType.DMA((max(n_local, 1),))],
        compiler_params=pltpu.CompilerParams(has_side_effects=True),
    )(*srcs)


def _remote(src, dst, send_sems, recv_sems, k, target):
    return pltpu.make_async_remote_copy(src, dst, send_sems.at[k], recv_sems.at[k], device_id=target,
                                        device_id_type=MESH)


def _by_rows(shape):
    return shape[0] % 16 == 0


def _half_shape(shape):
    return (shape[0] // 2, shape[1]) if _by_rows(shape) else (shape[0], shape[1] // 2)


def _half_of(ref, lead, c, shape):
    if _by_rows(shape):
        half = shape[0] // 2
        return ref.at[(*lead, pl.ds(pl.multiple_of(c * half, 8), half))]
    half = shape[1] // 2
    return ref.at[(*lead, slice(None), pl.ds(pl.multiple_of(c * half, LANES), half))]


def gather_shards(shards):
    nt = len(shards)

    def body(*refs):
        srcs, outs = refs[:nt], refs[nt:2 * nt]
        send_sems, recv_sems, _ = refs[2 * nt:]
        x, y, c = _pos()
        j = 2 * x + y
        sib = (x, y, 1 - c)
        own = [_remote(srcs[t], outs[t].at[j], send_sems, recv_sems, 7 * t + 6, sib) for t in range(nt)]
        first, passed = [], []
        for k, (fx, fy) in enumerate(CHIP_FLIPS):
            tx, ty = x ^ fx, y ^ fy
            jk = 2 * tx + ty
            for t in range(nt):
                sh = srcs[t].shape
                first.append(_remote(_half_of(srcs[t], (), c, sh), _half_of(outs[t], (j,), c, sh), send_sems, recv_sems,
                                     7 * t + k, (tx, ty, c)))
                landed = _half_of(outs[t], (jk,), c, sh)
                passed.append(_remote(landed, landed, send_sems, recv_sems, 7 * t + 3 + k, sib))
        for cp in first + own:
            cp.start()
        for a, b in zip(first, passed):
            a.wait_recv()
            b.start()
        for cp in passed + own:
            cp.wait_recv()
        for cp in first + passed + own:
            cp.wait_send()

    return _comm_call(body, shards, [SDS((4,) + s.shape, s.dtype) for s in shards], 7 * nt, 0, "gather_shards")


def swap_halves(ps):
    nt = len(ps)

    def body(*refs):
        srcs, outs = refs[:nt], refs[nt:2 * nt]
        send_sems, recv_sems, _ = refs[2 * nt:]
        x, y, c = _pos()
        cps = [_remote(_half_of(srcs[t], (a,), 1 - c, srcs[t].shape[1:]), outs[t].at[a], send_sems, recv_sems, 4 * t + a,
                       (x, y, 1 - c)) for t in range(nt) for a in range(4)]
        for cp in cps:
            cp.start()
        for cp in cps:
            cp.wait()

    return _comm_call(body, ps, [SDS((4,) + _half_shape(p.shape[1:]), p.dtype) for p in ps], 4 * nt, 0, "swap_halves")


def scatter_to_chips(qs):
    nt = len(qs)

    def body(*refs):
        srcs, outs = refs[:nt], refs[nt:2 * nt]
        send_sems, recv_sems, _ = refs[2 * nt:]
        x, y, c = _pos()
        cps = []
        for k, (fx, fy) in enumerate(CHIP_FLIPS):
            tx, ty = x ^ fx, y ^ fy
            for t in range(nt):
                cps.append(_remote(srcs[t].at[2 * tx + ty], outs[t].at[k], send_sems, recv_sems, 3 * t + k, (tx, ty, c)))
        for cp in cps:
            cp.start()
        for cp in cps:
            cp.wait()

    return _comm_call(body, qs, [SDS((3,) + q.shape[1:], q.dtype) for q in qs], 3 * nt, 0, "scatter_to_chips")


DEV_FLIPS = tuple((fx, fy, fc) for fx in (0, 1) for fy in (0, 1) for fc in (0, 1))[1:]


def join_halves(fs, small):
    nt = len(fs)

    def body(*refs):
        outs = refs[nt + 1:2 * nt + 1]
        sm = refs[2 * nt + 1]
        send_sems, recv_sems, _ = refs[2 * nt + 2:]
        x, y, c = _pos()
        cps = []
        for t in range(nt):
            mine = _half_of(outs[t], (0,), c, outs[t].shape[1:])
            cps.append(_remote(mine, mine, send_sems, recv_sems, t, (x, y, 1 - c)))
        mine = _half_of(sm, (2 * x + y,), c, sm.shape[1:])
        for k, (fx, fy, fc) in enumerate(DEV_FLIPS):
            cps.append(_remote(mine, mine, send_sems, recv_sems, nt + k, (x ^ fx, y ^ fy, c ^ fc)))
        for cp in cps:
            cp.start()
        for cp in cps:
            cp.wait()

    any_spec = pl.BlockSpec(memory_space=pl.ANY)
    n_sem = nt + len(DEV_FLIPS)
    res = pl.pallas_call(
        body, name="join_halves", in_specs=[any_spec] * (nt + 1), out_specs=[any_spec] * (nt + 1),
        out_shape=[SDS(f.shape, f.dtype) for f in fs] + [SDS(small.shape, small.dtype)],
        input_output_aliases={t: t for t in range(nt + 1)},
        scratch_shapes=[pltpu.SemaphoreType.DMA((n_sem,)), pltpu.SemaphoreType.DMA((n_sem,)), pltpu.SemaphoreType.DMA((1,))],
        compiler_params=pltpu.CompilerParams(has_side_effects=True),
    )(*fs, small)
    return res[:nt], res[nt]


def gather_devices(block, *, name):
    def body(src, out, send_sems, recv_sems, loc_sems):
        x, y, c = _pos()
        me = 4 * x + 2 * y + c
        mine = pltpu.make_async_copy(src, out.at[me], loc_sems.at[0])
        mine.start()
        cps = [_remote(src, out.at[me], send_sems, recv_sems, k, (x ^ fx, y ^ fy, c ^ fc))
               for k, (fx, fy, fc) in enumerate(DEV_FLIPS)]
        for cp in cps:
            cp.start()
        for cp in cps:
            cp.wait()
        mine.wait()

    return _comm_call(body, [block], [SDS((N_DEV,) + block.shape, block.dtype)], 7, 1, name)[0]


def exchange_devices(blocks, *, name):
    def body(src, out, send_sems, recv_sems, loc_sems):
        x, y, c = _pos()
        me = 4 * x + 2 * y + c
        mine = pltpu.make_async_copy(src.at[me], out.at[me], loc_sems.at[0])
        mine.start()
        cps = []
        for k, (fx, fy, fc) in enumerate(DEV_FLIPS):
            tx, ty, tc = x ^ fx, y ^ fy, c ^ fc
            cps.append(_remote(src.at[4 * tx + 2 * ty + tc], out.at[me], send_sems, recv_sems, k, (tx, ty, tc)))
        for cp in cps:
            cp.start()
        for cp in cps:
            cp.wait()
        mine.wait()

    return _comm_call(body, [blocks], [SDS(blocks.shape, blocks.dtype)], 7, 1, name)[0]


SMALL_SHAPES = ((1, 2, 32, 64), (1, 2, 32, 64), (1, 2, 32), (1, 2, 32, 16, 64),
                (1, 2, 32, 16, 64), (1, 2, 32, 16, 64), (1, 2, 32, 16, 64), (1, D_S5), (1, D_S5), (1, 2, 4), (1, 2, 4),
                (1, GDN_HEAD), (1, D_MODEL), (1, D_MODEL), (LANES,))
SMALL_SWAPPED = (3, 4)


def _size(shape):
    return functools.reduce(lambda p, q: p * q, shape)


SMALL_ROWS = tuple(-(-_size(s) // (8 * LANES)) * 8 for s in SMALL_SHAPES)
SMALL_TOTAL = 2240
SMALL_QUARTER = SMALL_TOTAL // 4


def _rows(a):
    flat = a.reshape(-1)
    pad = (-flat.shape[0]) % (8 * LANES)
    if pad:
        flat = jnp.concatenate([flat, jnp.zeros((pad,), flat.dtype)])
    return flat.reshape(-1, LANES)


def _pack_small(parts):
    rows = [_rows(p) for p in parts]
    rows.append(jnp.zeros((SMALL_TOTAL - sum(SMALL_ROWS), LANES), f32))
    return jnp.concatenate(rows, axis=0)


def _unpack_small(buf):
    out, r = [], 0
    for s, n in zip(SMALL_SHAPES, SMALL_ROWS):
        out.append(buf[r:r + n].reshape(-1)[:_size(s)].reshape(s))
        r += n
    return out


def _as_2d(a):
    return a.reshape(1, -1) if a.ndim == 1 else a.reshape(-1, a.shape[-1])


S5_BG = S5_GROUPS // S5_BLOCKS


def _block_diag_in(bb):
    lead = bb.shape[:-2]
    eye = jnp.eye(S5_BG, dtype=bb.dtype)
    b4 = bb.reshape(lead + (S5_BLOCKS, S5_BG, S5_GROUP, S5_STATE))
    return jnp.einsum('...jgcp,gh->...jgchp', b4, eye).reshape(lead + (S5_BLOCKS, S5_BC, S5_BS))


def _block_diag_in_t(d):
    lead = d.shape[:-3]
    d6 = d.reshape(lead + (S5_BLOCKS, S5_BG, S5_GROUP, S5_BG, S5_STATE))
    return jnp.einsum('...jgcgp->...jgcp', d6).reshape(lead + (S5_GROUPS, S5_GROUP * S5_STATE))


def _block_diag_out(cm):
    lead = cm.shape[:-3]
    eye = jnp.eye(S5_BG, dtype=cm.dtype)
    c4 = cm.reshape(lead + (S5_BLOCKS, S5_BG, S5_GROUP, S5_STATE))
    return jnp.einsum('...jgcp,gh->...jhpgc', c4, eye).reshape(lead + (S5_BLOCKS, S5_BS, S5_BC))


def _block_diag_out_t(d):
    lead = d.shape[:-3]
    d6 = d.reshape(lead + (S5_BLOCKS, S5_BG, S5_STATE, S5_BG, S5_GROUP))
    return jnp.einsum('...jgpgc->...jgcp', d6).reshape(lead + (S5_GROUPS, S5_GROUP, S5_STATE))


def _to_chunk_rows(a):
    B, L, W = a.shape
    return a.reshape(B, L // CHUNK, CHUNK, W).transpose(0, 1, 3, 2)


def _from_chunk_rows(a):
    B, nc, W, _ = a.shape
    return a.transpose(0, 1, 3, 2).reshape(B, nc * CHUNK, W)


def local_step(x, ctx, tgt, m, w_in, lam_re, lam_im, log_dt, b_re, b_im, c_re, c_im, s5_d,
               w_glu, b_glu, conv16, a_log, dt_bias, norm_w, w_out, ln_g, ln_b):
    B, L, _ = x.shape
    zeros_state = jnp.zeros((B, GDN_HEADS, GDN_HEAD, GDN_HEAD), f32)

    shift, scale, gate = m[:B, :D_MODEL], m[:B, D_MODEL:2 * D_MODEL], m[:B, 2 * D_MODEL:]
    mod = jnp.stack([scale, shift], axis=1)
    mod_c = jnp.broadcast_to(jnp.stack([m[B, D_MODEL:2 * D_MODEL], m[B, :D_MODEL]], axis=0)[None], (B, 2, D_MODEL))

    u, z_s5, qkv, z_gdn, ba = in_proj_fwd(x, mod, w_in, name="in_proj_fwd")
    uc, _, qkvc, _, bac = in_proj_fwd(ctx, mod_c, w_in, name="in_proj_fwd_ctx")

    ng = N_DIR * S5_GROUPS
    zoh_in = (lam_re.reshape(ng, S5_STATE), lam_im.reshape(ng, S5_STATE), log_dt.reshape(ng, 1),
              b_re.reshape(ng, S5_GROUP * S5_STATE), b_im.reshape(ng, S5_GROUP * S5_STATE))
    expand = (jnp.arange(S5_GROUP * S5_STATE)[None, :] % S5_STATE == jnp.arange(S5_STATE)[:, None]).astype(f32)
    ar, ai, bbr, bbi = s5_zoh_fwd(*zoh_in, expand)
    b_blocks = _block_diag_in(jnp.stack([bbr, bbi]).astype(bf16).reshape(2, N_DIR, S5_GROUPS, S5_GROUP * S5_STATE))
    c_blocks = _block_diag_out(jnp.stack([c_re, -c_im]).astype(bf16).reshape(2, N_DIR, S5_GROUPS, S5_GROUP, S5_STATE))
    a_rows = jnp.stack([ar, ai]).reshape(2, N_DIR, S5_HALF)
    s5w, ys, hins, hins_c, hss, hss_c = [], [], [], [], [], []
    for d in range(N_DIR):
        wd = (b_blocks[0, d], b_blocks[1, d], c_blocks[0, d], c_blocks[1, d], a_rows[:, d])
        s5w.append(wd)
        hs_c, hin_c, hend_c = s5_scan_fwd(uc, *wd, jnp.zeros((B, 2, S5_HALF), f32), d=d, need_y=False,
                                          name=f"s5_fwd_ctx{d}")
        y_d, hs_d, hin, _ = s5_scan_fwd(u, *wd, hend_c, d=d, need_y=True, name=f"s5_fwd{d}")
        hss.append(hs_d)
        hss_c.append(hs_c)
        ys.append(y_d)
        hins.append(hin)
        hins_c.append(hin_c)
    glu_w = (s5_d.reshape(1, D_S5), w_glu, b_glu.reshape(1, D_S5))

    act, pre = conv_fwd(qkv, conv16, is_ctx=False, name="conv_fwd")
    act_c, pre_c = conv_fwd(qkvc, conv16, is_ctx=True, name="conv_fwd_ctx")
    pad8 = jnp.zeros((1, 8), f32)
    alog16 = jnp.concatenate([pad8, a_log.reshape(1, 8)], axis=1)
    dtb16 = jnp.concatenate([pad8, dt_bias.reshape(1, 8)], axis=1)
    bg = gates_fwd(ba, alog16, dtb16, name="gates_fwd")
    bg_c = gates_fwd(bac, alog16, dtb16, name="gates_fwd_ctx")
    bgr, bgr_c = _to_chunk_rows(bg), _to_chunk_rows(bg_c)
    cks_c, ns_c, s_c = gdn_fwd(act_c, bg_c, bgr_c, (zeros_state, zeros_state), need_o=False, name="gdn_fwd_ctx")
    os_, cks, ns, _ = gdn_fwd(act, bg, bgr, s_c, need_o=True, name="gdn_fwd")
    nw = norm_w.reshape(1, GDN_HEAD)

    (loss8, du_skip, dy, dz_s5, do, dz_gdn, gx_res, dws, dwg, dgate, dlng, dlnb, d_s5_d, d_w_glu, d_b_glu,
     d_norm_w) = tail_fwd_bwd(u, ys[0], ys[1], z_s5, os_[0], os_[1], z_gdn, x, tgt, gate[:, None, :],
                              ln_g.reshape(1, D_MODEL), ln_b.reshape(1, D_MODEL), w_out[:D_S5], w_out[D_S5:], *glu_w, nw)
    loss = jnp.sum(loss8[:, 0, 0])
    d_w_out = jnp.concatenate([dws, dwg], axis=0)

    dacts, dbgs, dbgrs, ds0s = gdn_bwd(act, bg, bgr, cks, ns, do, (zeros_state, zeros_state), name="gdn_bwd")
    dacts_c, dbgs_c, dbgrs_c, _ = gdn_bwd(act_c, bg_c, bgr_c, cks_c, ns_c, None, ds0s, name="gdn_bwd_ctx")
    dbg = dbgs[0] + dbgs[1] + _from_chunk_rows(dbgrs[0] + dbgrs[1])
    dbg_c = dbgs_c[0] + dbgs_c[1] + _from_chunk_rows(dbgrs_c[0] + dbgrs_c[1])
    dba, dal, ddt = gates_bwd(ba, alog16, dtb16, dbg, name="gates_bwd")
    dbac, dal_c, ddt_c = gates_bwd(bac, alog16, dtb16, dbg_c, name="gates_bwd_ctx")
    d_a_log = (dal + dal_c)[:, 8:].reshape(1, N_DIR, GDN_HEADS)
    d_dt_bias = (ddt + ddt_c)[:, 8:].reshape(1, N_DIR, GDN_HEADS)
    dqkv, dcw = conv_bwd(qkv, pre, conv16, dacts[0], dacts[1], is_ctx=False, name="conv_bwd")
    dqkvc, dcw_c = conv_bwd(qkvc, pre_c, conv16, dacts_c[0], dacts_c[1], is_ctx=True, name="conv_bwd_ctx")
    d_conv16 = jnp.sum(dcw, axis=0) + jnp.sum(dcw_c, axis=0)

    dus, ducs = [du_skip], []
    das, dbs, dcs = [], [], []
    for d in range(N_DIR):
        du_d, dbre1, dbim1, dct1, dcb1, da1, dh0 = s5_scan_bwd(u, dy, hss[d], *s5w[d], hins[d],
                                                                jnp.zeros((B, 2, S5_HALF), f32), d=d, name=f"s5_bwd{d}")
        duc_d, dbre2, dbim2, _, _, da2, _ = s5_scan_bwd(uc, None, hss_c[d], *s5w[d], hins_c[d], dh0, d=d,
                                                        name=f"s5_bwd_ctx{d}")
        dus.append(du_d)
        ducs.append(duc_d)
        das.append(da1 + da2)
        dbs.append(jnp.stack([dbre1 + dbre2, dbim1 + dbim2]))
        dcs.append(jnp.stack([dct1, dcb1]))
    ng_shape = (N_DIR * S5_GROUPS, -1)
    da = jnp.stack(das, axis=1)
    db = _block_diag_in_t(jnp.stack(dbs, axis=1))
    dc = _block_diag_out_t(jnp.stack(dcs, axis=1))
    dlr, dli, dldt, dbre, dbim = s5_zoh_bwd(*zoh_in, expand, da[0].reshape(ng_shape), da[1].reshape(ng_shape),
                                            db[0].reshape(ng_shape), db[1].reshape(ng_shape))
    d_s5 = (dlr, dli, dldt, dbre, dbim, dc[0], -dc[1])

    zc = jnp.zeros_like(uc)
    dw_c, dmod_c = in_proj_bwd(ctx, mod_c, (tuple(ducs), zc, dqkvc, zc, dbac), w_in, None, None,
                               name="in_proj_bwd_ctx")
    d_w_in, dmod, grad_x = in_proj_bwd(x, mod, (tuple(dus), dz_s5, dqkv, dz_gdn, dba), w_in, gx_res, dw_c,
                                       name="in_proj_bwd")
    dmod_c = jnp.sum(dmod_c, axis=0)

    dm_rows = jnp.concatenate([dmod[:, 1], dmod[:, 0], dgate[:, 0]], axis=1)
    dm_ctx = jnp.concatenate([dmod_c[1], dmod_c[0], jnp.zeros((D_MODEL,), f32)])[None]
    dm = jnp.concatenate([dm_rows, dm_ctx], axis=0)
    small = (*d_s5, d_s5_d, d_b_glu, d_a_log, d_dt_bias, d_norm_w, dlng, dlnb)
    small = tuple(g.reshape(s) for g, s in zip(small, SMALL_SHAPES))
    return loss, grad_x, (d_w_in, d_w_out, d_w_glu, d_conv16), small, dm


SHARDED = (1, 3, 18, 12, 14)
UNSHARDED = tuple(i for i in range(21) if i not in SHARDED)
SMALL = tuple(i for i in UNSHARDED if i not in (0, 2))
W_IN_SHARD = 772


def _conv_rows(w):
    return jnp.concatenate([w.reshape(9, w.shape[-1]), jnp.zeros((CONV_ROWS - 9, w.shape[-1]), f32)], axis=0)


def kernel(x, c, ctx, c_ctx, w_ada, b_ada, w_in, s5_lambda_re, s5_lambda_im, s5_log_dt, s5_b_re, s5_b_im, s5_c_re, s5_c_im, s5_d, w_glu, b_glu, conv_w, gdn_a_log, gdn_dt_bias, gdn_norm_w, w_out, ln_g, ln_b, loss_target, m_c_ctx, m_w_ada, m_b_ada, m_w_in, m_s5_lambda_re, m_s5_lambda_im, m_s5_log_dt, m_s5_b_re, m_s5_b_im, m_s5_c_re, m_s5_c_im, m_s5_d, m_w_glu, m_b_glu, m_conv_w, m_gdn_a_log, m_gdn_dt_bias, m_gdn_norm_w, m_w_out, m_ln_g, m_ln_b, v_c_ctx, v_w_ada, v_b_ada, v_w_in, v_s5_lambda_re, v_s5_lambda_im, v_s5_log_dt, v_s5_b_re, v_s5_b_im, v_s5_c_re, v_s5_c_im, v_s5_d, v_w_glu, v_b_glu, v_conv_w, v_gdn_a_log, v_gdn_dt_bias, v_gdn_norm_w, v_w_out, v_ln_g, v_ln_b):
    weights = [c_ctx, w_ada, b_ada, w_in, s5_lambda_re, s5_lambda_im, s5_log_dt, s5_b_re, s5_b_im, s5_c_re, s5_c_im,
               s5_d, w_glu, b_glu, conv_w, gdn_a_log, gdn_dt_bias, gdn_norm_w, w_out, ln_g, ln_b]
    ms = [m_c_ctx, m_w_ada, m_b_ada, m_w_in, m_s5_lambda_re, m_s5_lambda_im, m_s5_log_dt, m_s5_b_re, m_s5_b_im,
          m_s5_c_re, m_s5_c_im, m_s5_d, m_w_glu, m_b_glu, m_conv_w, m_gdn_a_log, m_gdn_dt_bias, m_gdn_norm_w, m_w_out,
          m_ln_g, m_ln_b]
    vs = [v_c_ctx, v_w_ada, v_b_ada, v_w_in, v_s5_lambda_re, v_s5_lambda_im, v_s5_log_dt, v_s5_b_re, v_s5_b_im,
          v_s5_c_re, v_s5_c_im, v_s5_d, v_w_glu, v_b_glu, v_conv_w, v_gdn_a_log, v_gdn_dt_bias, v_gdn_norm_w, v_w_out,
          v_ln_g, v_ln_b]
    cpos = lax.axis_index("c")
    jchip = 2 * lax.axis_index("x") + lax.axis_index("y")

    c_all = gather_devices(c, name="gather_c")
    cc = jnp.concatenate([c_all, jnp.broadcast_to(c_ctx[None, None, :], (N_DEV, 1, D_MODEL)),
                          jnp.zeros((N_DEV, 5, D_MODEL), f32)], axis=1)
    w_ada16 = w_ada[0].astype(bf16)
    b_cols = lax.dynamic_slice_in_dim(b_ada, jchip * ADA_SHARD, ADA_SHARD, axis=1)
    m_mine = exchange_devices(ada_fwd(cc, w_ada16, b_cols), name="exchange_m")
    m_rows = jnp.concatenate([m_mine[2 * j, :3] for j in range(4)], axis=1)

    conv_shard = _conv_rows(conv_w)
    g_in, g_out, g_glu, g_conv = gather_shards(
        [jnp.transpose(w_in[0]).astype(bf16), w_out[0].astype(bf16), w_glu[0].astype(bf16), conv_shard])
    w_in_t = g_in.reshape(P_IN, D_MODEL)
    conv16 = g_conv.transpose(1, 0, 2).reshape(CONV_ROWS, 3 * D_GDN)

    swap = lambda a: jnp.swapaxes(a, 3, 4)
    loss, grad_x, big, small, dm_rows = local_step(
        x, ctx, loss_target, m_rows, w_in_t, s5_lambda_re, s5_lambda_im, s5_log_dt, swap(s5_b_re), swap(s5_b_im),
        s5_c_re, s5_c_im, s5_d, g_glu.reshape(D_S5, D_S5), b_glu, conv16, gdn_a_log, gdn_dt_bias, gdn_norm_w,
        g_out.reshape(D_MODEL, D_MODEL), ln_g, ln_b)
    me = 2 * jchip + cpos
    loss_hi = loss.astype(bf16).astype(f32)
    loss_row = jnp.zeros((LANES,), f32).at[me].set(loss_hi).at[N_DEV + me].set(loss - loss_hi)

    dm8 = jnp.concatenate([dm_rows, jnp.zeros((5, 3 * D_MODEL), f32)], axis=0)
    dm_by_chip = dm8.reshape(8, 4, ADA_SHARD).transpose(1, 0, 2)
    dm_cols = exchange_devices(jnp.repeat(dm_by_chip, 2, axis=0), name="exchange_dm")
    g_w_ada, pb = ada_bwd(cc, w_ada16, dm_cols)
    pb_all = gather_devices(pb, name="gather_p")
    g_c_ctx = c_ctx_bwd(pb_all, c_ctx[None, :])[0]
    g_b_ada = jnp.concatenate([pb_all[2 * j, 1:2, :ADA_SHARD] for j in range(4)], axis=1)

    d_w_in, d_w_out, d_w_glu, d_conv16 = big
    slabs = [d_w_in.reshape(4, W_IN_SHARD, D_MODEL),
             d_w_out.reshape(4, D_MODEL // 4, D_MODEL),
             d_w_glu.reshape(4, D_S5 // 4, D_S5),
             d_conv16.reshape(CONV_ROWS, 4, 3 * D_GDN // 4).transpose(1, 0, 2),
             _pack_small(small + (loss_row,)).reshape(4, SMALL_QUARTER, LANES)]
    got = swap_halves(slabs)
    owns = []
    for s in slabs:
        if _by_rows(s.shape[1:]):
            owns.append(lax.dynamic_index_in_dim(s.reshape(4, 2, s.shape[1] // 2, s.shape[2]), cpos, axis=1, keepdims=False))
        else:
            owns.append(lax.dynamic_slice_in_dim(s, cpos * (s.shape[2] // 2), s.shape[2] // 2, axis=2))
    big32, big16 = sum_cores(owns[0], got[0], name="sum_cores0")
    rest32, rest16 = sum_cores_small(owns[1:], got[1:])
    q32, q16 = [big32, *rest32], [big16, *rest16]
    rec = scatter_to_chips(q16)
    mines = [lax.dynamic_index_in_dim(q, jchip, axis=0, keepdims=False) for q in q32]
    fs = [sum_chips(mines[0], rec[0], cpos, slabs[0].shape[1:], name="sum_chips0")]
    fs += sum_chips_small(mines[1:], rec[1:], cpos, [None, None, None, jchip])
    red, small_all = join_halves(fs[:4], fs[4])
    g_small = _unpack_small(small_all.reshape(SMALL_TOTAL, LANES))
    loss = jnp.sum(g_small[-1][:2 * N_DEV])
    g_small = g_small[:-1]
    g_shard = {1: g_w_ada, 3: red[0], 18: red[1], 12: red[2], 14: red[3]}

    grads, deltas, new_m, new_v = [None] * 21, [None] * 21, [None] * 21, [None] * 21
    whole = (12, 18)
    for t, i in enumerate(SHARDED):
        if i in whole:
            continue
        conv, win = i == 14, i == 3
        prep = (lambda a: _conv_rows(a)[None]) if conv else ((lambda a: jnp.transpose(a, (2, 0, 1))) if win else (lambda a: a))
        g = jnp.transpose(g_shard[i], (1, 0, 2)) if win else g_shard[i]
        d, nm, nv = adamw_3d(prep(weights[i]), g, prep(ms[i]), prep(vs[i]), lead=win, name=f"adamw{t}")
        for lst, val in ((grads, g), (deltas, d), (new_m, nm), (new_v, nv)):
            lst[i] = (val[0, :9].reshape(weights[i].shape) if conv else (jnp.transpose(val, (1, 2, 0)) if win else val))
    g_un = {0: g_c_ctx, 2: g_b_ada, **{i: g_small[n] for n, i in enumerate(SMALL)}, **{i: g_shard[i] for i in whole}}
    swapped = [SMALL[n] for n in SMALL_SWAPPED]
    together = UNSHARDED + whole
    small_in = lambda lst: [_as_2d(swap(lst[i]) if i in swapped else lst[i]) for i in together]
    sm = adamw_small(small_in(weights), [_as_2d(g_un[i]) for i in together], small_in(ms), small_in(vs))
    for n, i in enumerate(together):
        back = ((lambda a: swap(a.reshape(swap(weights[i]).shape))) if i in swapped
                else (lambda a: a.reshape(weights[i].shape)))
        grads[i] = back(g_un[i])
        for lst, res in ((deltas, sm[0]), (new_m, sm[1]), (new_v, sm[2])):
            lst[i] = back(res[n])
    return (loss, grad_x, *grads, *deltas, *new_m, *new_v)
```
